```python
import jax, jax.numpy as jnp
from jax import lax
import numpy as np

D_MODEL = 1024
BATCH = 8
SEQ = 4096
DEPTH = 1

D_MIX = D_MODEL
D_ATT = D_MIX // 2
D_POOL = D_MIX - D_ATT
N_ATT_HEADS = 8
HEAD_DIM = D_ATT // N_ATT_HEADS
POOL_WINDOWS = (2, 4, 8, 16)
N_POOL_GROUPS = len(POOL_WINDOWS)
POOL_GROUP_DIM = D_POOL // N_POOL_GROUPS
Q_BLOCK = 128
LN_EPS = 1e-5
FORGET_BIAS_INIT = 3.0
DEEPNORM_ALPHA = (2.0 * DEPTH) ** 0.25
DEEPNORM_BETA = (8.0 * DEPTH) ** -0.25
D_IN = 3 * D_ATT + N_ATT_HEADS + D_POOL + D_ATT + D_POOL
SPLIT_POINTS = (D_ATT, 2 * D_ATT, 3 * D_ATT, 3 * D_ATT + N_ATT_HEADS,
                3 * D_ATT + N_ATT_HEADS + D_POOL, 3 * D_ATT + N_ATT_HEADS + D_POOL + D_ATT)

kernel_name = "hymba_fox_poolformer_deepnorm_adaln"


def _layer_norm(h, g, b):
    h32 = h.astype(jnp.float32)
    mu = jnp.mean(h32, axis=-1, keepdims=True)
    var = jnp.mean(jnp.square(h32 - mu), axis=-1, keepdims=True)
    y = (h32 - mu) * lax.rsqrt(var + LN_EPS)
    return (y * g.astype(jnp.float32) + b.astype(jnp.float32)).astype(h.dtype)


def _forgetting_attention(q, k, v, f_logit):
    B, S, _ = q.shape
    q = q.reshape(B, S, N_ATT_HEADS, HEAD_DIM)
    k = k.reshape(B, S, N_ATT_HEADS, HEAD_DIM)
    v = v.reshape(B, S, N_ATT_HEADS, HEAD_DIM)
    log_f = jax.nn.log_sigmoid(f_logit.astype(jnp.float32))
    cum = jnp.cumsum(log_f, axis=1)
    nb = S // Q_BLOCK
    q_blocks = q.reshape(B, nb, Q_BLOCK, N_ATT_HEADS, HEAD_DIM).transpose(1, 0, 2, 3, 4)
    cum_blocks = cum.reshape(B, nb, Q_BLOCK, N_ATT_HEADS).transpose(1, 0, 2, 3)
    cum_k = cum.transpose(0, 2, 1)[:, :, None, :]
    k_pos = jnp.arange(S)
    scale = HEAD_DIM ** -0.5

    def one_block(args):
        q_blk, cum_q, idx = args
        s = jnp.einsum('bqhd,bkhd->bhqk', q_blk, k).astype(jnp.float32) * scale
        s = s + cum_q.transpose(0, 2, 1)[..., None] - cum_k
        q_pos = idx * Q_BLOCK + jnp.arange(Q_BLOCK)
        causal = k_pos[None, :] <= q_pos[:, None]
        s = jnp.where(causal[None, None], s, -jnp.inf)
        p = jax.nn.softmax(s, axis=-1).astype(v.dtype)
        return jnp.einsum('bhqk,bkhd->bqhd', p, v)

    out = lax.map(one_block, (q_blocks, cum_blocks, jnp.arange(nb)))
    return out.transpose(1, 0, 2, 3, 4).reshape(B, S, D_ATT)


def _multiscale_pool(p, w_pool_mix, b_pool_mix, pool_scale):
    B, S, _ = p.shape
    p32 = p.astype(jnp.float32)
    cs = jnp.cumsum(p32, axis=1)
    count = jnp.arange(1, S + 1, dtype=jnp.float32)
    outs = []
    for g, w in enumerate(POOL_WINDOWS):
        sl = slice(g * POOL_GROUP_DIM, (g + 1) * POOL_GROUP_DIM)
        cs_g = cs[..., sl]
        lagged = jnp.pad(cs_g, ((0, 0), (w, 0), (0, 0)))[:, :S]
        mean = (cs_g - lagged) / jnp.minimum(count, float(w))[None, :, None]
        outs.append(mean - p32[..., sl])
    pooled = jnp.stack(outs, axis=2).astype(p.dtype)
    mixed = jnp.einsum('bsgc,gce->bsge', pooled, w_pool_mix) + b_pool_mix
    return mixed.reshape(B, S, D_POOL) * pool_scale


def _hybrid_layer(x, c, w_ada, b_ada, w_in, b_in, w_pool_mix, b_pool_mix, pool_scale,
                  w_out, b_out, ln_g, ln_b):
    ada = jnp.einsum('bd,de->be', jax.nn.silu(c), w_ada) + b_ada
    shift, scale, gate = jnp.split(ada, 3, axis=-1)
    u = x * (1 + scale[:, None, :]) + shift[:, None, :]
    proj = jnp.einsum('bsd,de->bse', u, w_in) + b_in
    q, k, v, f_logit, p, g_att, g_pool = jnp.split(proj, SPLIT_POINTS, axis=-1)
    att = _forgetting_attention(q, k, v, f_logit)
    pool = _multiscale_pool(p, w_pool_mix, b_pool_mix, pool_scale)
    y = jnp.concatenate([att * jax.nn.silu(g_att), pool * jax.nn.silu(g_pool)], axis=-1)
    y = jnp.einsum('bse,ed->bsd', y, w_out) + b_out
    h = DEEPNORM_ALPHA * x + gate[:, None, :] * y
    return _layer_norm(h, ln_g, ln_b)


def _fwd_setup_inputs(seed: int = 0) -> dict:
    key = jax.random.key(seed)
    ks = jax.random.split(key, 20)
    D = D_MODEL
    s_d = D ** -0.5
    x = jax.random.normal(ks[0], (BATCH, SEQ, D), jnp.float32)
    c = jax.random.normal(ks[1], (BATCH, D), jnp.float32)
    w_ada = jax.random.normal(ks[2], (DEPTH, D, 3 * D), jnp.float32) * s_d
    b_ada = 0.01 * jax.random.normal(ks[3], (DEPTH, 3 * D), jnp.float32)
    w_q = jax.random.normal(ks[4], (DEPTH, D, D_ATT), jnp.float32) * s_d
    w_k = jax.random.normal(ks[5], (DEPTH, D, D_ATT), jnp.float32) * s_d
    w_v = jax.random.normal(ks[6], (DEPTH, D, D_ATT), jnp.float32) * (s_d * DEEPNORM_BETA)
    w_f = jax.random.normal(ks[7], (DEPTH, D, N_ATT_HEADS), jnp.float32) * (0.1 * s_d)
    w_p = jax.random.normal(ks[8], (DEPTH, D, D_POOL), jnp.float32) * s_d
    w_g = jax.random.normal(ks[9], (DEPTH, D, D_ATT + D_POOL), jnp.float32) * s_d
    w_in = jnp.concatenate([w_q, w_k, w_v, w_f, w_p, w_g], axis=-1)
    b_in = 0.01 * jax.random.normal(ks[10], (DEPTH, D_IN), jnp.float32)
    b_in = b_in.at[:, 3 * D_ATT:3 * D_ATT + N_ATT_HEADS].add(FORGET_BIAS_INIT)
    w_pool_mix = jax.random.normal(ks[11], (DEPTH, N_POOL_GROUPS, POOL_GROUP_DIM, POOL_GROUP_DIM),
                                   jnp.float32) * POOL_GROUP_DIM ** -0.5
    b_pool_mix = 0.01 * jax.random.normal(ks[12], (DEPTH, N_POOL_GROUPS, POOL_GROUP_DIM), jnp.float32)
    pool_scale = 1.0 + 0.02 * jax.random.normal(ks[13], (DEPTH, D_POOL), jnp.float32)
    w_out = jax.random.normal(ks[14], (DEPTH, D_MIX, D), jnp.float32) * (D_MIX ** -0.5 * DEEPNORM_BETA)
    b_out = 0.01 * jax.random.normal(ks[15], (DEPTH, D), jnp.float32)
    ln_g = 1.0 + 0.02 * jax.random.normal(ks[16], (DEPTH, D), jnp.float32)
    ln_b = 0.01 * jax.random.normal(ks[17], (DEPTH, D), jnp.float32)
    return {"x": x, "c": c, "w_ada": w_ada, "b_ada": b_ada, "w_in": w_in, "b_in": b_in,
            "w_pool_mix": w_pool_mix, "b_pool_mix": b_pool_mix, "pool_scale": pool_scale,
            "w_out": w_out, "b_out": b_out, "ln_g": ln_g, "ln_b": ln_b}


def _fwd_reference(x, c, w_ada, b_ada, w_in, b_in, w_pool_mix, b_pool_mix, pool_scale,
              w_out, b_out, ln_g, ln_b):
    for layer in range(DEPTH):
        x = _hybrid_layer(x, c, w_ada[layer], b_ada[layer], w_in[layer], b_in[layer],
                          w_pool_mix[layer], b_pool_mix[layer], pool_scale[layer],
                          w_out[layer], b_out[layer], ln_g[layer], ln_b[layer])
    return x


import jax as _jax
import jax.numpy as _jnp

TWIN_FORMAT = 'train_step'
FWD_PARAMS = ['x', 'c', 'w_ada', 'b_ada', 'w_in', 'b_in', 'w_pool_mix', 'b_pool_mix', 'pool_scale', 'w_out', 'b_out', 'ln_g', 'ln_b']
TWIN_WEIGHTS = ['w_ada', 'b_ada', 'w_in', 'b_in', 'w_pool_mix', 'b_pool_mix', 'pool_scale', 'w_out', 'b_out', 'ln_g', 'ln_b']
TWIN_DIFF_INPUT = 'x'
TWIN_INPUTS = ['x', 'c', 'w_ada', 'b_ada', 'w_in', 'b_in', 'w_pool_mix', 'b_pool_mix', 'pool_scale', 'w_out', 'b_out', 'ln_g', 'ln_b', 'loss_target', 'm_w_ada', 'm_b_ada', 'm_w_in', 'm_b_in', 'm_w_pool_mix', 'm_b_pool_mix', 'm_pool_scale', 'm_w_out', 'm_b_out', 'm_ln_g', 'm_ln_b', 'v_w_ada', 'v_b_ada', 'v_w_in', 'v_b_in', 'v_w_pool_mix', 'v_b_pool_mix', 'v_pool_scale', 'v_w_out', 'v_b_out', 'v_ln_g', 'v_ln_b']
TWIN_OUTPUTS = ['loss', 'grad_x', 'grad_w_ada', 'grad_b_ada', 'grad_w_in', 'grad_b_in', 'grad_w_pool_mix', 'grad_b_pool_mix', 'grad_pool_scale', 'grad_w_out', 'grad_b_out', 'grad_ln_g', 'grad_ln_b', 'delta_w_ada', 'delta_b_ada', 'delta_w_in', 'delta_b_in', 'delta_w_pool_mix', 'delta_b_pool_mix', 'delta_pool_scale', 'delta_w_out', 'delta_b_out', 'delta_ln_g', 'delta_ln_b', 'new_m_w_ada', 'new_m_b_ada', 'new_m_w_in', 'new_m_b_in', 'new_m_w_pool_mix', 'new_m_b_pool_mix', 'new_m_pool_scale', 'new_m_w_out', 'new_m_b_out', 'new_m_ln_g', 'new_m_ln_b', 'new_v_w_ada', 'new_v_b_ada', 'new_v_w_in', 'new_v_b_in', 'new_v_w_pool_mix', 'new_v_b_pool_mix', 'new_v_pool_scale', 'new_v_w_out', 'new_v_b_out', 'new_v_ln_g', 'new_v_ln_b']
TWIN_LEAF_KINDS = {'loss': 'loss', 'grad_x': 'grad_x', 'grad_w_ada': 'grad_w', 'grad_b_ada': 'grad_w', 'grad_w_in': 'grad_w', 'grad_b_in': 'grad_w', 'grad_w_pool_mix': 'grad_w', 'grad_b_pool_mix': 'grad_w', 'grad_pool_scale': 'grad_w', 'grad_w_out': 'grad_w', 'grad_b_out': 'grad_w', 'grad_ln_g': 'grad_w', 'grad_ln_b': 'grad_w', 'delta_w_ada': 'delta_w', 'delta_b_ada': 'delta_w', 'delta_w_in': 'delta_w', 'delta_b_in': 'delta_w', 'delta_w_pool_mix': 'delta_w', 'delta_b_pool_mix': 'delta_w', 'delta_pool_scale': 'delta_w', 'delta_w_out': 'delta_w', 'delta_b_out': 'delta_w', 'delta_ln_g': 'delta_w', 'delta_ln_b': 'delta_w', 'new_m_w_ada': 'new_m', 'new_m_b_ada': 'new_m', 'new_m_w_in': 'new_m', 'new_m_b_in': 'new_m', 'new_m_w_pool_mix': 'new_m', 'new_m_b_pool_mix': 'new_m', 'new_m_pool_scale': 'new_m', 'new_m_w_out': 'new_m', 'new_m_b_out': 'new_m', 'new_m_ln_g': 'new_m', 'new_m_ln_b': 'new_m', 'new_v_w_ada': 'new_v', 'new_v_b_ada': 'new_v', 'new_v_w_in': 'new_v', 'new_v_b_in': 'new_v', 'new_v_w_pool_mix': 'new_v', 'new_v_b_pool_mix': 'new_v', 'new_v_pool_scale': 'new_v', 'new_v_w_out': 'new_v', 'new_v_b_out': 'new_v', 'new_v_ln_g': 'new_v', 'new_v_ln_b': 'new_v'}


def _forward(args):
    return _fwd_reference(*[args[k] for k in FWD_PARAMS])


def _output_shape():
    def fwd():
        inp = _fwd_setup_inputs(0)
        return _fwd_reference(*[inp[k] for k in FWD_PARAMS])
    out = _jax.eval_shape(fwd)
    return out.shape, out.dtype

N_MICROBATCH = 1
ADAM_LR = 0.001
ADAM_B1 = 0.9
ADAM_B2 = 0.999
ADAM_EPS = 1e-08
ADAM_WD = 0.01
ADAM_STEP = 10
PER_EXAMPLE_BATCH_AXIS = {'x': 0, 'c': 0, 'loss_target': 0}
SHARED_INPUTS = []
_WEIGHT_DTYPES = {'w_ada': _jnp.float32, 'b_ada': _jnp.float32, 'w_in': _jnp.float32, 'b_in': _jnp.float32, 'w_pool_mix': _jnp.float32, 'b_pool_mix': _jnp.float32, 'pool_scale': _jnp.float32, 'w_out': _jnp.float32, 'b_out': _jnp.float32, 'ln_g': _jnp.float32, 'ln_b': _jnp.float32}
MOMENT_SCALE = {'w_ada': 2.752780e-02, 'b_ada': 4.544989e-02, 'w_in': 3.114356e-02, 'b_in': 2.447524e-02, 'w_pool_mix': 4.488489e-02, 'b_pool_mix': 4.471733e-02, 'pool_scale': 4.687131e-02, 'w_out': 5.719853e-02, 'b_out': 9.699027e-02, 'ln_g': 3.202949e+01, 'ln_b': 4.746680e-01}


def _to_microbatches(a, axis):
    t = _jnp.moveaxis(a, axis, 0)
    t = t.reshape((N_MICROBATCH, t.shape[0] // N_MICROBATCH) + t.shape[1:])
    return _jnp.moveaxis(t, 1, axis + 1)


def setup_inputs(seed: int = 0) -> dict:
    inp = _fwd_setup_inputs(seed)
    key = _jax.random.fold_in(_jax.random.key(seed), 7919)
    shape, _ = _output_shape()
    out = dict(inp)
    out["loss_target"] = _jax.random.normal(_jax.random.fold_in(key, 0), shape, _jnp.float32)
    for i, name in enumerate(TWIN_WEIGHTS):
        w = inp[name].astype(_jnp.float32)
        if MOMENT_SCALE is None:
            s = _jnp.sqrt(_jnp.mean(_jnp.square(w)) + 1e-30)
        else:
            s = MOMENT_SCALE[name]
        km, kv = _jax.random.split(_jax.random.fold_in(key, i + 1))
        out[name] = w
        out["m_" + name] = s * _jax.random.normal(km, w.shape, _jnp.float32)
        out["v_" + name] = (s * s) * _jax.random.uniform(kv, w.shape, _jnp.float32, 0.5, 1.5)
    if N_MICROBATCH > 1:
        for name, axis in PER_EXAMPLE_BATCH_AXIS.items():
            out[name] = _to_microbatches(out[name], axis)
    return {'x': out['x'], 'c': out['c'], 'w_ada': out['w_ada'], 'b_ada': out['b_ada'], 'w_in': out['w_in'], 'b_in': out['b_in'], 'w_pool_mix': out['w_pool_mix'], 'b_pool_mix': out['b_pool_mix'], 'pool_scale': out['pool_scale'], 'w_out': out['w_out'], 'b_out': out['b_out'], 'ln_g': out['ln_g'], 'ln_b': out['ln_b'], 'loss_target': out['loss_target'], 'm_w_ada': out['m_w_ada'], 'm_b_ada': out['m_b_ada'], 'm_w_in': out['m_w_in'], 'm_b_in': out['m_b_in'], 'm_w_pool_mix': out['m_w_pool_mix'], 'm_b_pool_mix': out['m_b_pool_mix'], 'm_pool_scale': out['m_pool_scale'], 'm_w_out': out['m_w_out'], 'm_b_out': out['m_b_out'], 'm_ln_g': out['m_ln_g'], 'm_ln_b': out['m_ln_b'], 'v_w_ada': out['v_w_ada'], 'v_b_ada': out['v_b_ada'], 'v_w_in': out['v_w_in'], 'v_b_in': out['v_b_in'], 'v_w_pool_mix': out['v_w_pool_mix'], 'v_b_pool_mix': out['v_b_pool_mix'], 'v_pool_scale': out['v_pool_scale'], 'v_w_out': out['v_w_out'], 'v_b_out': out['v_b_out'], 'v_ln_g': out['v_ln_g'], 'v_ln_b': out['v_ln_b']}


def _loss(weights, diff, rest, loss_target):
    with _jax.named_scope("forward"):
        args = {**rest, TWIN_DIFF_INPUT: diff, **{k: w.astype(_WEIGHT_DTYPES[k]) for k, w in weights.items()}}
        y = _forward(args)
    with _jax.named_scope("loss_head"):
        err = _jnp.square(y.astype(_jnp.float32) - loss_target)
        return 0.5 * _jnp.sum(_jnp.mean(err, axis=-1)) if err.ndim else 0.5 * err


def _adamw(w, g, m, v):
    m = ADAM_B1 * m + (1.0 - ADAM_B1) * g
    v = ADAM_B2 * v + (1.0 - ADAM_B2) * _jnp.square(g)
    m_hat = m / (1.0 - ADAM_B1 ** ADAM_STEP)
    v_hat = v / (1.0 - ADAM_B2 ** ADAM_STEP)
    delta = -ADAM_LR * (m_hat / (_jnp.sqrt(v_hat) + ADAM_EPS) + ADAM_WD * w)
    return delta, m, v


def reference(x, c, w_ada, b_ada, w_in, b_in, w_pool_mix, b_pool_mix, pool_scale, w_out, b_out, ln_g, ln_b, loss_target, m_w_ada, m_b_ada, m_w_in, m_b_in, m_w_pool_mix, m_b_pool_mix, m_pool_scale, m_w_out, m_b_out, m_ln_g, m_ln_b, v_w_ada, v_b_ada, v_w_in, v_b_in, v_w_pool_mix, v_b_pool_mix, v_pool_scale, v_w_out, v_b_out, v_ln_g, v_ln_b):
    given = dict(x=x, c=c, w_ada=w_ada, b_ada=b_ada, w_in=w_in, b_in=b_in, w_pool_mix=w_pool_mix, b_pool_mix=b_pool_mix, pool_scale=pool_scale, w_out=w_out, b_out=b_out, ln_g=ln_g, ln_b=ln_b, loss_target=loss_target, m_w_ada=m_w_ada, m_b_ada=m_b_ada, m_w_in=m_w_in, m_b_in=m_b_in, m_w_pool_mix=m_w_pool_mix, m_b_pool_mix=m_b_pool_mix, m_pool_scale=m_pool_scale, m_w_out=m_w_out, m_b_out=m_b_out, m_ln_g=m_ln_g, m_ln_b=m_ln_b, v_w_ada=v_w_ada, v_b_ada=v_b_ada, v_w_in=v_w_in, v_b_in=v_b_in, v_w_pool_mix=v_w_pool_mix, v_b_pool_mix=v_b_pool_mix, v_pool_scale=v_pool_scale, v_w_out=v_w_out, v_b_out=v_b_out, v_ln_g=v_ln_g, v_ln_b=v_ln_b)
    weights = {n: given[n] for n in TWIN_WEIGHTS}
    shared = {n: given[n] for n in SHARED_INPUTS}
    per_example = {n: given[n] for n in ['x', 'c']}
    grad_fn = _jax.value_and_grad(_loss, argnums=(0, 1))

    def one_microbatch(ex, loss_target):
        ex = dict(ex)
        diff = ex.pop(TWIN_DIFF_INPUT)
        return grad_fn(weights, diff, {**shared, **ex}, loss_target)

    if N_MICROBATCH == 1:
        loss, (grad_w, grad_x) = one_microbatch(per_example, given["loss_target"])
    else:
        def body(carry, xs):
            loss_sum, grad_sum = carry
            l_k, (gw_k, gx_k) = one_microbatch(xs[0], xs[1])
            with _jax.named_scope("update"):
                return (loss_sum + l_k, _jax.tree.map(_jnp.add, grad_sum, gw_k)), gx_k

        init = (_jnp.zeros((), _jnp.float32), _jax.tree.map(_jnp.zeros_like, weights))
        (loss, grad_w), grad_x = _jax.lax.scan(body, init, (per_example, given["loss_target"]))
    with _jax.named_scope("update"):
        delta_w, new_m, new_v = {}, {}, {}
        for n in TWIN_WEIGHTS:
            delta_w[n], new_m[n], new_v[n] = _adamw(weights[n], grad_w[n], given["m_" + n], given["v_" + n])
    return (loss, grad_x, *[grad_w[n] for n in TWIN_WEIGHTS], *[delta_w[n] for n in TWIN_WEIGHTS],
            *[new_m[n] for n in TWIN_WEIGHTS], *[new_v[n] for n in TWIN_WEIGHTS])
```

```python
import jax
import jax.numpy as jnp
from jax import lax
from jax.experimental import pallas as pl
from jax.experimental.pallas import tpu as pltpu

F32 = jnp.float32
BF16 = jnp.bfloat16

N_DEV = 8
D = 1024
N_HEADS = 8
HEAD_DIM = 64
D_ATT = 512
D_POOL = 512
POOL_WINDOWS = (2, 4, 8, 16)
GROUP_DIM = 128
HALO = 16
LANE = 128
D_IN = 3080
D_ADA = 3072
N_MAIN = 4608
OFF_P, OFF_GA, OFF_GP = 3072, 3584, 4096
COL_CHUNK = 512
Q_SCALE = 0.125
LN_EPS = 1e-5
ALPHA = 2.0 ** 0.25
L_CQ, L_CK, L_LSE = 64, 67, 70

ADAM_LR, ADAM_B1, ADAM_B2, ADAM_EPS, ADAM_WD, ADAM_STEP = 0.001, 0.9, 0.999, 1e-08, 0.01, 10
VMEM_LIMIT = 56 * 1024 * 1024

MESH = pl.DeviceIdType.MESH
ANY = pl.BlockSpec(memory_space=pl.ANY)


def _params(sem=None, vmem=VMEM_LIMIT):
    return pltpu.CompilerParams(dimension_semantics=sem, vmem_limit_bytes=vmem)


def _split3(a):
    hi = a.astype(BF16)
    r = a - hi.astype(F32)
    mid = r.astype(BF16)
    lo = (r - mid.astype(F32)).astype(BF16)
    return hi, mid, lo


def _dot(a, b):
    return jnp.dot(a, b, preferred_element_type=F32)


def _dot_nt(a, b):
    return lax.dot_general(a, b, (((1,), (1,)), ((), ())), preferred_element_type=F32)


def _dot_tn(a, b):
    return lax.dot_general(a, b, (((0,), (0,)), ((), ())), preferred_element_type=F32)


def _dot3(m01, a):
    hi, mid, lo = _split3(a)
    return _dot(m01, hi) + _dot(m01, mid) + _dot(m01, lo)


def _sigmoid(z):
    return 1.0 / (1.0 + jnp.exp(-z))


def _lanes(shape):
    return lax.broadcasted_iota(jnp.int32, shape, len(shape) - 1)


def _place3(lane, base, parts, other):
    out = other
    for j in range(3):
        out = jnp.where(lane == base + j, parts[j], out)
    return out


def _mesh_pos():
    return lax.axis_index("x"), lax.axis_index("y"), lax.axis_index("c")


def _dev_index(px, py, pc):
    return 4 * px + 2 * py + pc


def _all_gather(blocks, name):
    n = len(blocks)

    def body(*refs):
        ins, outs = refs[:n], refs[n:2 * n]
        send_sems, recv_sems, local_sems = refs[2 * n:]
        x, y, c = _mesh_pos()
        me, sibling = (x, y, c), (x, y, 1 - c)
        chips = [(1 - x, y), (x, 1 - y), (1 - x, 1 - y)]

        def copy(a, k, block, to, src=None):
            slot = outs[a].at[_dev_index(*block)]
            return pltpu.make_async_remote_copy(
                src_ref=slot if src is None else src, dst_ref=slot,
                send_sem=send_sems.at[a, k], recv_sem=recv_sems.at[a, k],
                device_id=to, device_id_type=MESH)

        mine = [pltpu.make_async_copy(ins[a], outs[a].at[_dev_index(*me)], local_sems.at[a]) for a in range(n)]
        for cp in mine:
            cp.start()
        first = []
        for a in range(n):
            first.append(copy(a, 0, me, sibling, src=ins[a]))
            first += [copy(a, 1 + j, me, (*chip, c), src=ins[a]) for j, chip in enumerate(chips)]
        for cp in first:
            cp.start()
        passed = []
        for j, chip in enumerate(chips):
            for a in range(n):
                copy(a, 1 + j, (*chip, c), me).wait_recv()
                fwd = copy(a, 4 + j, (*chip, c), sibling)
                fwd.start()
                passed.append(fwd)
        for a in range(n):
            copy(a, 0, sibling, me).wait_recv()
            for j, chip in enumerate(chips):
                copy(a, 4 + j, (*chip, 1 - c), me).wait_recv()
        for cp in first + passed:
            cp.wait_send()
        for cp in mine:
            cp.wait()

    return pl.pallas_call(
        body, name=name,
        out_shape=[jax.ShapeDtypeStruct((N_DEV,) + b.shape, b.dtype) for b in blocks],
        in_specs=[ANY] * n, out_specs=[ANY] * n,
        scratch_shapes=[pltpu.SemaphoreType.DMA((n, 7)), pltpu.SemaphoreType.DMA((n, 7)),
                        pltpu.SemaphoreType.DMA((n,))],
    )(*blocks)


def _exchange(scatter, gather, name):
    ns, n = len(scatter), len(scatter) + len(gather)
    arrays = list(scatter) + list(gather)

    def body(*refs):
        ins, outs = refs[:n], refs[n:2 * n]
        send_sems, recv_sems, local_sems = refs[2 * n:]
        x, y, c = _mesh_pos()
        me = _dev_index(x, y, c)
        peers = []
        for p in range(1, N_DEV):
            px, py, pc = (p >> 2) & 1, (p >> 1) & 1, p & 1
            peers.append((1 - x if px else x, 1 - y if py else y, 1 - c if pc else c))

        def src_for(a, slot):
            return ins[a].at[slot] if a < ns else ins[a]

        def copy(a, k, to):
            return pltpu.make_async_remote_copy(
                src_ref=src_for(a, _dev_index(*to)), dst_ref=outs[a].at[me],
                send_sem=send_sems.at[a, k], recv_sem=recv_sems.at[a, k],
                device_id=to, device_id_type=MESH)

        def arrival(a, k, frm):
            slot = _dev_index(*frm)
            return pltpu.make_async_remote_copy(
                src_ref=src_for(a, slot), dst_ref=outs[a].at[slot],
                send_sem=send_sems.at[a, k], recv_sem=recv_sems.at[a, k],
                device_id=frm, device_id_type=MESH)

        mine = [pltpu.make_async_copy(src_for(a, me), outs[a].at[me], local_sems.at[a]) for a in range(n)]
        for cp in mine:
            cp.start()
        sends = [copy(a, k, to) for k, to in enumerate(peers) for a in range(n)]
        for cp in sends:
            cp.start()
        for k, frm in enumerate(peers):
            for a in range(n):
                arrival(a, k, frm).wait_recv()
        for cp in sends:
            cp.wait_send()
        for cp in mine:
            cp.wait()

    out_shape = [jax.ShapeDtypeStruct(a.shape, a.dtype) for a in scatter]
    out_shape += [jax.ShapeDtypeStruct((N_DEV,) + a.shape, a.dtype) for a in gather]
    return pl.pallas_call(
        body, name=name, out_shape=out_shape,
        in_specs=[ANY] * n, out_specs=[ANY] * n,
        scratch_shapes=[pltpu.SemaphoreType.DMA((n, 7)), pltpu.SemaphoreType.DMA((n, 7)),
                        pltpu.SemaphoreType.DMA((n,))],
    )(*arrays)


def _ada_forward(c_all, w_ada):
    def body(c_ref, w_ref, sc_ref, part_ref):
        cc = c_ref[...]
        sc = cc * _sigmoid(cc)
        sc_ref[...] = sc
        part_ref[...] = _dot3_rhs(sc, w_ref[...])

    return pl.pallas_call(
        body, name="ada_forward",
        out_shape=[jax.ShapeDtypeStruct(c_all.shape, F32),
                   jax.ShapeDtypeStruct((N_DEV, w_ada.shape[1]), F32)],
        compiler_params=_params(),
    )(c_all, w_ada)


def _dot3_rhs(a, b):
    a0, a1, a2 = _split3(a)
    b0, b1, b2 = _split3(b)
    return (_dot(a0, b0) + (_dot(a0, b1) + _dot(a1, b0))
            + (_dot(a0, b2) + _dot(a1, b1) + _dot(a2, b0)))


def _inproj_forward(x, mod, w_main, w_f, b_main, b_f, tile):
    seq = x.shape[0]
    nt = seq // tile

    def body(x_ref, mod_ref, w_ref, wf_ref, b_ref, bf_ref,
             qp_ref, kp_ref, vp_ref, f_ref, p_ref, ga_ref, gp_ref, ut_ref, carry_ref):
        i = pl.program_id(0)

        @pl.when(i == 0)
        def _():
            carry_ref[...] = jnp.zeros_like(carry_ref)

        u = x_ref[...] * mod_ref[0:1, :] + mod_ref[1:2, :]
        ub = u.astype(BF16)
        ut_ref[...] = u.T.astype(BF16)

        f = _dot(ub, wf_ref[...]) + bf_ref[...]
        f_ref[...] = f
        lane = _lanes((tile, LANE))
        log_f = jnp.where(lane < N_HEADS, jnp.minimum(f, 0.0) - jnp.log(1.0 + jnp.exp(-jnp.abs(f))), 0.0)
        row = lax.broadcasted_iota(jnp.int32, (tile, tile), 0)
        col = lax.broadcasted_iota(jnp.int32, (tile, tile), 1)
        tri = (row >= col).astype(BF16)
        cum = _dot3(tri, log_f) + carry_ref[0:1, :]
        carry_ref[0:1, :] = cum[tile - 1:tile, :]
        cq = [part.astype(F32) for part in _split3(cum)]
        ck = [part.astype(F32) for part in _split3(-cum)]

        def proj(chunk):
            cols = pl.ds(chunk * COL_CHUNK, COL_CHUNK)
            return _dot(ub, w_ref[:, cols]) + b_ref[:, cols]

        heads_per_chunk = COL_CHUNK // LANE
        for chunk in range(2):
            r = proj(chunk)
            for j in range(heads_per_chunk):
                h = chunk * heads_per_chunk + j
                extra = jnp.where((lane >= L_CK) & (lane < L_CK + 3), 1.0, 0.0)
                extra = _place3(lane, L_CQ, [part[:, h:h + 1] for part in cq], extra)
                qp_ref[h] = jnp.where(lane < HEAD_DIM, r[:, j * LANE:(j + 1) * LANE] * Q_SCALE, extra).astype(BF16)
        for chunk in range(2):
            r = proj(2 + chunk)
            for j in range(heads_per_chunk):
                h = chunk * heads_per_chunk + j
                ones = ((lane >= L_CQ) & (lane < L_CQ + 3)) | ((lane >= L_LSE) & (lane < L_LSE + 3))
                extra = _place3(lane, L_CK, [part[:, h:h + 1] for part in ck], jnp.where(ones, 1.0, 0.0))
                kp_ref[h] = jnp.where(lane < HEAD_DIM, r[:, j * LANE:(j + 1) * LANE], extra).astype(BF16)
        for chunk in range(2):
            r = proj(4 + chunk)
            for j in range(heads_per_chunk):
                h = chunk * heads_per_chunk + j
                extra = jnp.where((lane >= HEAD_DIM) & (lane < HEAD_DIM + 3), -1.0, 0.0)
                vp_ref[h] = jnp.where(lane < HEAD_DIM, r[:, j * LANE:(j + 1) * LANE], extra).astype(BF16)
        p_ref[...] = proj(6)
        ga_ref[...] = proj(7)
        gp_ref[...] = proj(8)

    head_block = pl.BlockSpec((N_HEADS, tile, LANE), lambda i: (0, i, 0))
    tok = lambda width: pl.BlockSpec((tile, width), lambda i: (i, 0))
    whole = lambda a: pl.BlockSpec(a.shape, lambda i: (0,) * a.ndim)
    padded = jax.ShapeDtypeStruct((N_HEADS, seq, LANE), BF16)
    half = jax.ShapeDtypeStruct((seq, D_ATT), F32)
    return pl.pallas_call(
        body, name="inproj_forward", grid=(nt,),
        in_specs=[tok(D), whole(mod), whole(w_main), whole(w_f), whole(b_main), whole(b_f)],
        out_specs=[head_block, head_block, head_block, tok(LANE), tok(D_POOL), tok(D_ATT), tok(D_POOL),
                   pl.BlockSpec((D, tile), lambda i: (0, i))],
        out_shape=[padded, padded, padded, jax.ShapeDtypeStruct((seq, LANE), F32), half, half, half,
                   jax.ShapeDtypeStruct((D, seq), BF16)],
        scratch_shapes=[pltpu.VMEM((8, LANE), F32)],
        compiler_params=_params(("arbitrary",)),
    )(x, mod, w_main, w_f, b_main, b_f)


def _attention_forward(qp, kp, vp, tile):
    seq = qp.shape[1]
    nb = seq // tile

    def body(q_ref, k_ref, v_ref, att_ref, lse_ref):
        i = pl.program_id(1)
        lane = _lanes((tile, LANE))
        row = lax.broadcasted_iota(jnp.int32, (tile, tile), 0)
        col = lax.broadcasted_iota(jnp.int32, (tile, tile), 1)
        outs, lses = [], []
        for hh in range(2):
            q = q_ref[hh]

            def step(kb, carry, masked):
                m, l, acc = carry
                rows = pl.ds(pl.multiple_of(kb * tile, tile), tile)
                s = _dot_nt(q, k_ref[hh, rows, :])
                if masked:
                    s = jnp.where(col <= row, s, -1e30)
                m_new = jnp.maximum(m, jnp.max(s, axis=1, keepdims=True))
                scale = jnp.exp(m - m_new)
                p = jnp.exp(s - m_new)
                l = scale * l + jnp.sum(p, axis=1, keepdims=True)
                acc = scale * acc + _dot(p.astype(BF16), v_ref[hh, rows, :])
                return m_new, l, acc

            init = (jnp.full((tile, 1), -1e30, F32), jnp.zeros((tile, 1), F32), jnp.zeros((tile, LANE), F32))
            carry = lax.fori_loop(0, i, lambda kb, cr: step(kb, cr, False), init)
            m, l, acc = step(i, carry, True)
            outs.append(acc / l)
            lses.append(m + jnp.log(l))
        o1 = pltpu.roll(outs[1], HEAD_DIM, 1)
        att_ref[...] = jnp.where(lane < HEAD_DIM, outs[0], o1)
        lse_ref[0] = jnp.where(lane == 0, lses[0], jnp.where(lane == 1, lses[1], 0.0))

    pair = pl.BlockSpec((2, tile, LANE), lambda hp, i: (hp, i, 0))
    full = pl.BlockSpec((2, seq, LANE), lambda hp, i: (hp, 0, 0))
    return pl.pallas_call(
        body, name="attention_forward", grid=(N_HEADS // 2, nb),
        in_specs=[pair, full, full],
        out_specs=[pl.BlockSpec((tile, LANE), lambda hp, i: (i, hp)),
                   pl.BlockSpec((1, tile, LANE), lambda hp, i: (hp, i, 0))],
        out_shape=[jax.ShapeDtypeStruct((seq, D_ATT), F32),
                   jax.ShapeDtypeStruct((N_HEADS // 2, seq, LANE), F32)],
        compiler_params=_params(("arbitrary", "arbitrary")),
    )(qp, kp, vp)


def _window_matrices(tile, window, transposed):
    r = lax.broadcasted_iota(jnp.int32, (tile, tile), 0)
    c = lax.broadcasted_iota(jnp.int32, (tile, tile), 1)
    rh = lax.broadcasted_iota(jnp.int32, (tile, HALO), 0)
    ch = lax.broadcasted_iota(jnp.int32, (tile, HALO), 1)
    if not transposed:
        cur = (c <= r) & (r - c < window)
        halo = (rh + HALO - ch) < window
    else:
        cur = (r <= c) & (c - r < window)
        halo = (tile + ch - rh) < window
    return cur.astype(BF16), halo.astype(BF16)


def _silu_parts(g):
    sig = _sigmoid(g)
    return g * sig, sig * (1.0 + g * (1.0 - sig))


def _middle(x, tgt, att, g_att, g_pool, p, vecs, pool_vecs, w_out, w_pool, tile):
    seq = x.shape[0]
    nt = seq // tile
    halo_blocks = tile // HALO

    def body(x_ref, tgt_ref, att_ref, ga_ref, gp_ref, p_ref, ph_ref, vec_ref, pvec_ref, wo_ref, wp_ref,
             dxa_ref, datt_ref, dga_ref, dgp_ref, dpooled_ref, dwo_ref, dwp_ref, dvec_ref, dpvec_ref):
        i = pl.program_id(0)

        @pl.when(i == 0)
        def _():
            dwo_ref[...] = jnp.zeros_like(dwo_ref)
            dwp_ref[...] = jnp.zeros_like(dwp_ref)
            dvec_ref[...] = jnp.zeros_like(dvec_ref)
            dpvec_ref[...] = jnp.zeros_like(dpvec_ref)

        gate, b_out, ln_g, ln_b = (vec_ref[k:k + 1, :] for k in range(4))
        b_pool, pool_scale = pvec_ref[0:1, :], pvec_ref[1:2, :]
        x = x_ref[...]
        p = p_ref[...]
        p_halo = ph_ref[...] * jnp.where(i > 0, 1.0, 0.0)
        pos = i * tile + lax.broadcasted_iota(jnp.int32, (tile, 1), 0) + 1

        pooled, mixed = [], []
        for g, window in enumerate(POOL_WINDOWS):
            cols = slice(g * GROUP_DIM, (g + 1) * GROUP_DIM)
            m_cur, m_halo = _window_matrices(tile, window, False)
            wsum = _dot3(m_cur, p[:, cols]) + _dot3(m_halo, p_halo[:, cols])
            count = jnp.minimum(pos, window).astype(F32)
            pooled.append(wsum / count - p[:, cols])
            mixed.append(_dot(pooled[g].astype(BF16), wp_ref[g]) + b_pool[:, cols])
        mixed = jnp.concatenate(mixed, axis=1)
        pool = mixed * pool_scale

        att = att_ref[...]
        g_att, g_pool = ga_ref[...], gp_ref[...]
        silu_a, dsilu_a = _silu_parts(g_att)
        silu_p, dsilu_p = _silu_parts(g_pool)
        y_in = jnp.concatenate([att * silu_a, pool * silu_p], axis=1)
        y = _dot(y_in.astype(BF16), wo_ref[...]) + b_out
        h = ALPHA * x + gate * y
        mu = jnp.mean(h, axis=1, keepdims=True)
        hc = h - mu
        var = jnp.mean(hc * hc, axis=1, keepdims=True)
        rstd = lax.rsqrt(var + LN_EPS)
        yhat = hc * rstd
        diff = yhat * ln_g + ln_b - tgt_ref[...]
        loss_rows = jnp.sum(diff * diff, axis=1, keepdims=True)
        d_out = diff * (1.0 / D)

        d_yhat = d_out * ln_g
        dh = rstd * (d_yhat - jnp.mean(d_yhat, axis=1, keepdims=True)
                     - yhat * jnp.mean(d_yhat * yhat, axis=1, keepdims=True))
        dxa_ref[...] = ALPHA * dh
        dy = dh * gate
        dyb = dy.astype(BF16)
        lane = _lanes((1, D))
        loss_row = jnp.where(lane == 0, (0.5 / D) * jnp.sum(loss_rows, axis=0, keepdims=True), 0.0)
        dvec_ref[0:1, :] += jnp.sum(dh * y, axis=0, keepdims=True)
        dvec_ref[1:2, :] += jnp.sum(dy, axis=0, keepdims=True)
        dvec_ref[2:3, :] += jnp.sum(d_out * yhat, axis=0, keepdims=True)
        dvec_ref[3:4, :] += jnp.sum(d_out, axis=0, keepdims=True)
        dvec_ref[4:5, :] += loss_row

        dwo_ref[...] += _dot(y_in.T.astype(BF16), dyb)
        d_yin = _dot_nt(dyb, wo_ref[...])
        d_a, d_pl = d_yin[:, :D_ATT], d_yin[:, D_ATT:]
        datt_ref[...] = d_a * silu_a
        dga_ref[...] = d_a * att * dsilu_a
        dgp_ref[...] = d_pl * pool * dsilu_p
        d_pool = d_pl * silu_p
        d_mixed = d_pool * pool_scale
        dpvec_ref[0:1, :] += jnp.sum(d_mixed, axis=0, keepdims=True)
        dpvec_ref[1:2, :] += jnp.sum(d_pool * mixed, axis=0, keepdims=True)
        d_pooled = []
        for g in range(len(POOL_WINDOWS)):
            cols = slice(g * GROUP_DIM, (g + 1) * GROUP_DIM)
            dmb = d_mixed[:, cols].astype(BF16)
            dwp_ref[g] += _dot(pooled[g].T.astype(BF16), dmb)
            d_pooled.append(_dot_nt(dmb, wp_ref[g]))
        dpooled_ref[...] = jnp.concatenate(d_pooled, axis=1)

    tok = lambda width: pl.BlockSpec((tile, width), lambda i: (i, 0))
    whole = lambda a: pl.BlockSpec(a.shape, lambda i: (0,) * a.ndim)
    halo = pl.BlockSpec((HALO, D_POOL), lambda i: (jnp.maximum(i * halo_blocks - 1, 0), 0))
    half = jax.ShapeDtypeStruct((seq, D_ATT), F32)
    outs = [jax.ShapeDtypeStruct((seq, D), F32), half, half, half, half,
            jax.ShapeDtypeStruct(w_out.shape, F32), jax.ShapeDtypeStruct(w_pool.shape, F32),
            jax.ShapeDtypeStruct(vecs.shape, F32), jax.ShapeDtypeStruct(pool_vecs.shape, F32)]
    return pl.pallas_call(
        body, name="middle", grid=(nt,),
        in_specs=[tok(D), tok(D), tok(D_ATT), tok(D_ATT), tok(D_POOL), tok(D_POOL), halo,
                  whole(vecs), whole(pool_vecs), whole(w_out), whole(w_pool)],
        out_specs=[tok(D), tok(D_ATT), tok(D_ATT), tok(D_POOL), tok(D_POOL),
                   whole(w_out), whole(w_pool), whole(vecs), whole(pool_vecs)],
        out_shape=outs,
        compiler_params=_params(("arbitrary",)),
    )(x, tgt, att, g_att, g_pool, p, p, vecs, pool_vecs, w_out, w_pool)


def _attention_backward(qp, kp, vp, d_att, att, lse, tile):
    seq = qp.shape[1]
    nb = seq // tile

    def body(q_ref, k_ref, v_ref, do_ref, o_ref, lse_ref, dq_ref, dk_ref, dv_ref, q2_ref, do2_ref):
        odd = pl.program_id(0) % 2
        lane = _lanes((tile, LANE))
        row = lax.broadcasted_iota(jnp.int32, (tile, tile), 0)
        col = lax.broadcasted_iota(jnp.int32, (tile, tile), 1)

        def prep(qb, _):
            rows = pl.ds(pl.multiple_of(qb * tile, tile), tile)
            do_pair, o_pair, lse_pair = do_ref[rows, :], o_ref[rows, :], lse_ref[0, rows, :]
            do = jnp.where(odd == 1, pltpu.roll(do_pair, HEAD_DIM, 1), do_pair)
            o = jnp.where(odd == 1, pltpu.roll(o_pair, HEAD_DIM, 1), o_pair)
            do = jnp.where(lane < HEAD_DIM, do, 0.0)
            delta = jnp.sum(do * o, axis=1, keepdims=True)
            lse_col = jnp.where(odd == 1, lse_pair[:, 1:2], lse_pair[:, 0:1])
            q2 = _place3(lane, L_LSE, [part.astype(F32) for part in _split3(-lse_col)], q_ref[0, rows, :].astype(F32))
            q2_ref[rows, :] = q2.astype(BF16)
            do2 = _place3(lane, HEAD_DIM, [part.astype(F32) for part in _split3(delta)], do)
            do2_ref[rows, :] = do2.astype(BF16)
            dq_ref[0, rows, :] = jnp.zeros((tile, LANE), F32)
            return 0

        lax.fori_loop(0, nb, prep, 0)

        def kv_block(kb, _):
            krows = pl.ds(pl.multiple_of(kb * tile, tile), tile)
            k = k_ref[0, krows, :]
            v = v_ref[0, krows, :]

            def q_block(qb, carry, masked):
                dk, dv = carry
                qrows = pl.ds(pl.multiple_of(qb * tile, tile), tile)
                q2 = q2_ref[qrows, :]
                do2 = do2_ref[qrows, :]
                s_t = _dot_nt(k, q2)
                if masked:
                    s_t = jnp.where(row <= col, s_t, -1e30)
                p_t = jnp.exp(s_t)
                ds_t = (p_t * _dot_nt(v, do2)).astype(BF16)
                dv = dv + _dot(p_t.astype(BF16), do2)
                dk = dk + _dot(ds_t, q2)
                dq_ref[0, qrows, :] += _dot_tn(ds_t, k)
                return dk, dv

            zero = jnp.zeros((tile, LANE), F32)
            carry = q_block(kb, (zero, zero), True)
            dk, dv = lax.fori_loop(kb + 1, nb, lambda qb, cr: q_block(qb, cr, False), carry)
            dk_ref[0, krows, :] = dk
            dv_ref[0, krows, :] = dv
            return 0

        lax.fori_loop(0, nb, kv_block, 0)

    head = pl.BlockSpec((1, seq, LANE), lambda h: (h, 0, 0))
    pair_cols = pl.BlockSpec((seq, LANE), lambda h: (0, h // 2))
    grad = jax.ShapeDtypeStruct((N_HEADS, seq, LANE), F32)
    return pl.pallas_call(
        body, name="attention_backward", grid=(N_HEADS,),
        in_specs=[head, head, head, pair_cols, pair_cols, pl.BlockSpec((1, seq, LANE), lambda h: (h // 2, 0, 0))],
        out_specs=[head, head, head],
        out_shape=[grad, grad, grad],
        scratch_shapes=[pltpu.VMEM((seq, LANE), BF16), pltpu.VMEM((seq, LANE), BF16)],
        compiler_params=_params(("arbitrary",)),
    )(qp, kp, vp, d_att, att, lse)


def _inproj_backward(dqp, dkp, dvp, f, d_pooled, d_ga, d_gp, x, dxa, ut, mod, w_main, w_f, tile):
    seq = x.shape[0]
    nt = seq // tile
    halo_blocks = tile // HALO

    def body(dq_ref, dk_ref, dv_ref, f_ref, dpo_ref, dph_ref, dga_ref, dgp_ref, x_ref, dxa_ref, ut_ref,
             mod_ref, w_ref, wf_ref,
             dx_ref, dproj_ref, dwf_ref, db_ref, dbf_ref, dmod_ref, carry_ref):
        step = pl.program_id(0)
        i = nt - 1 - step

        @pl.when(step == 0)
        def _():
            carry_ref[...] = jnp.zeros_like(carry_ref)
            dwf_ref[...] = jnp.zeros_like(dwf_ref)
            db_ref[...] = jnp.zeros_like(db_ref)
            dbf_ref[...] = jnp.zeros_like(dbf_ref)
            dmod_ref[...] = jnp.zeros_like(dmod_ref)

        lane = _lanes((tile, LANE))
        head_lanes = lane < HEAD_DIM

        def emit(chunk, val):
            cols = pl.ds(chunk * COL_CHUNK, COL_CHUNK)
            db_ref[0:1, cols] += jnp.sum(val, axis=0, keepdims=True)
            vb = val.astype(BF16)
            dproj_ref[:, cols] = vb
            return _dot_nt(vb, w_ref[:, cols])

        heads_per_chunk = COL_CHUNK // LANE
        d_u = jnp.zeros((tile, D), F32)
        d_cum = jnp.zeros((tile, LANE), F32)
        for chunk in range(2):
            parts = []
            for j in range(heads_per_chunk):
                h = chunk * heads_per_chunk + j
                dq = dq_ref[h]
                dk = dk_ref[h]
                d_cum = jnp.where(lane == h, dq[:, L_CQ:L_CQ + 1] - dk[:, L_CK:L_CK + 1], d_cum)
                parts.append(jnp.where(head_lanes, dq * Q_SCALE, 0.0))
            d_u += emit(chunk, jnp.concatenate(parts, axis=1))
        for chunk in range(2):
            parts = [jnp.where(head_lanes, dk_ref[chunk * heads_per_chunk + j], 0.0) for j in range(heads_per_chunk)]
            d_u += emit(2 + chunk, jnp.concatenate(parts, axis=1))
        for chunk in range(2):
            parts = [jnp.where(head_lanes, dv_ref[chunk * heads_per_chunk + j], 0.0) for j in range(heads_per_chunk)]
            d_u += emit(4 + chunk, jnp.concatenate(parts, axis=1))

        d_pooled = dpo_ref[...]
        d_halo = dph_ref[...] * jnp.where(i < nt - 1, 1.0, 0.0)
        pos = i * tile + lax.broadcasted_iota(jnp.int32, (tile, 1), 0) + 1
        d_p = []
        for g, window in enumerate(POOL_WINDOWS):
            cols = slice(g * GROUP_DIM, (g + 1) * GROUP_DIM)
            m_cur, m_halo = _window_matrices(tile, window, True)
            scaled = d_pooled[:, cols] / jnp.minimum(pos, window).astype(F32)
            d_p.append(_dot3(m_cur, scaled) + _dot3(m_halo, d_halo[:, cols] * (1.0 / window)) - d_pooled[:, cols])
        d_u += emit(6, jnp.concatenate(d_p, axis=1))
        d_u += emit(7, dga_ref[...])
        d_u += emit(8, dgp_ref[...])

        row = lax.broadcasted_iota(jnp.int32, (tile, tile), 0)
        col = lax.broadcasted_iota(jnp.int32, (tile, tile), 1)
        d_logf = _dot3((row <= col).astype(BF16), d_cum) + carry_ref[0:1, :]
        carry_ref[0:1, :] = d_logf[0:1, :]
        d_f = jnp.where(lane < N_HEADS, d_logf * _sigmoid(-f_ref[...]), 0.0)
        dbf_ref[0:1, :] += jnp.sum(d_f, axis=0, keepdims=True)
        d_fb = d_f.astype(BF16)
        d_u += _dot_nt(d_fb, wf_ref[...])
        dwf_ref[...] += _dot(ut_ref[...], d_fb)

        x = x_ref[...]
        dx_ref[...] = dxa_ref[...] + d_u * mod_ref[0:1, :]
        dmod_ref[0:1, :] += jnp.sum(d_u * x, axis=0, keepdims=True)
        dmod_ref[1:2, :] += jnp.sum(d_u, axis=0, keepdims=True)

    rev = lambda step: nt - 1 - step
    tok = lambda width: pl.BlockSpec((tile, width), lambda s: (rev(s), 0))
    head_block = pl.BlockSpec((N_HEADS, tile, LANE), lambda s: (0, rev(s), 0))
    whole = lambda a: pl.BlockSpec(a.shape, lambda s: (0,) * a.ndim)
    halo = pl.BlockSpec((HALO, D_POOL), lambda s: (jnp.minimum((rev(s) + 1) * halo_blocks, seq // HALO - 1), 0))
    small = lambda width: jax.ShapeDtypeStruct((8, width), F32)
    return pl.pallas_call(
        body, name="inproj_backward", grid=(nt,),
        in_specs=[head_block, head_block, head_block, tok(LANE), tok(D_POOL), halo, tok(D_ATT), tok(D_POOL),
                  tok(D), tok(D), pl.BlockSpec((D, tile), lambda s: (0, rev(s))),
                  whole(mod), whole(w_main), whole(w_f)],
        out_specs=[tok(D), tok(N_MAIN), pl.BlockSpec((D, LANE), lambda s: (0, 0)),
                   pl.BlockSpec((8, N_MAIN), lambda s: (0, 0)), pl.BlockSpec((8, LANE), lambda s: (0, 0)),
                   pl.BlockSpec((8, D), lambda s: (0, 0))],
        out_shape=[jax.ShapeDtypeStruct((seq, D), F32), jax.ShapeDtypeStruct((seq, N_MAIN), BF16),
                   jax.ShapeDtypeStruct((D, LANE), F32), small(N_MAIN), small(LANE), small(D)],
        scratch_shapes=[pltpu.VMEM((8, LANE), F32)],
        compiler_params=_params(("arbitrary",)),
    )(dqp, dkp, dvp, f, d_pooled, d_pooled, d_ga, d_gp, x, dxa, ut, mod, w_main, w_f)


def _weight_grad(ut, dproj, k_tile):
    seq = ut.shape[1]
    nk = seq // k_tile

    def body(ut_ref, dp_ref, out_ref):
        @pl.when(pl.program_id(1) == 0)
        def _():
            out_ref[...] = jnp.zeros_like(out_ref)

        out_ref[...] += _dot(ut_ref[...], dp_ref[...])

    return pl.pallas_call(
        body, name="weight_grad", grid=(N_MAIN // COL_CHUNK, nk),
        in_specs=[pl.BlockSpec((D, k_tile), lambda n, k: (0, k)),
                  pl.BlockSpec((k_tile, COL_CHUNK), lambda n, k: (k, n))],
        out_specs=pl.BlockSpec((D, COL_CHUNK), lambda n, k: (0, n)),
        out_shape=jax.ShapeDtypeStruct((D, N_MAIN), F32),
        compiler_params=_params(("arbitrary", "arbitrary")),
    )(ut, dproj)


def _adamw(w, g, m, v):
    m = ADAM_B1 * m + (1.0 - ADAM_B1) * g
    v = ADAM_B2 * v + (1.0 - ADAM_B2) * (g * g)
    m_hat = m / (1.0 - ADAM_B1 ** ADAM_STEP)
    v_hat = v / (1.0 - ADAM_B2 ** ADAM_STEP)
    delta = -ADAM_LR * (m_hat / (jnp.sqrt(v_hat) + ADAM_EPS) + ADAM_WD * w)
    return delta, m, v


def _sum_adamw(parts, w, m, v, row_tile, name):
    rows, cols = w.shape
    nr = rows // row_tile

    def body(parts_ref, w_ref, m_ref, v_ref, g_ref, d_ref, nm_ref, nv_ref):
        g = parts_ref[0]
        for k in range(1, N_DEV):
            g = g + parts_ref[k]
        g_ref[...] = g
        d_ref[...], nm_ref[...], nv_ref[...] = _adamw(w_ref[...], g, m_ref[...], v_ref[...])

    blk = pl.BlockSpec((row_tile, cols), lambda r: (r, 0))
    shape = jax.ShapeDtypeStruct(w.shape, F32)
    return pl.pallas_call(
        body, name=name, grid=(nr,),
        in_specs=[pl.BlockSpec((N_DEV, row_tile, cols), lambda r: (0, r, 0)), blk, blk, blk],
        out_specs=[blk, blk, blk, blk],
        out_shape=[shape, shape, shape, shape],
        compiler_params=_params(("arbitrary",)),
    )(parts, w, m, v)


def _ada_adamw(sc_t, d_ada, w, m, v):
    def body(sc_ref, d_ref, w_ref, m_ref, v_ref, g_ref, dl_ref, nm_ref, nv_ref):
        g = sc_ref[:, 0:1] * d_ref[0:1, :]
        for b in range(1, N_DEV):
            g = g + sc_ref[:, b:b + 1] * d_ref[b:b + 1, :]
        g_ref[...] = g
        dl_ref[...], nm_ref[...], nv_ref[...] = _adamw(w_ref[...], g, m_ref[...], v_ref[...])

    shape = jax.ShapeDtypeStruct(w.shape, F32)
    return pl.pallas_call(
        body, name="ada_adamw", out_shape=[shape, shape, shape, shape], compiler_params=_params(),
    )(sc_t, d_ada, w, m, v)


def _pad_heads(a):
    lead = a.shape[:-1]
    a = a.reshape(lead + (N_HEADS, HEAD_DIM))
    a = jnp.pad(a, [(0, 0)] * len(lead) + [(0, 0), (0, LANE - HEAD_DIM)])
    return a.reshape(lead + (N_HEADS * LANE,))


def _unpad_heads(a):
    lead = a.shape[:-1]
    return a.reshape(lead + (N_HEADS, LANE))[..., :HEAD_DIM].reshape(lead + (N_HEADS * HEAD_DIM,))


def _to_main(a):
    q, k, v = a[..., 0:512], a[..., 512:1024], a[..., 1024:1536]
    f = a[..., 1536:1544]
    rest = a[..., 1544:]
    main = jnp.concatenate([_pad_heads(q), _pad_heads(k), _pad_heads(v), rest], axis=-1)
    f = jnp.pad(f, [(0, 0)] * (a.ndim - 1) + [(0, LANE - N_HEADS)])
    return main, f


def _from_main(main, f):
    q, k, v = (_unpad_heads(main[..., j * 1024:(j + 1) * 1024]) for j in range(3))
    return jnp.concatenate([q, k, v, f[..., :N_HEADS], main[..., 3072:]], axis=-1)


_SMALL = (("b_in", 3200), ("w_pool", 65536), ("b_pool", 512), ("pool_scale", 512),
          ("b_out", 1024), ("ln_g", 1024), ("ln_b", 1024), ("b_ada", 3072))
_SMALL_ROWS = 600
_SMALL_OFF_ADA = sum(size for name, size in _SMALL if name != "b_ada")


def _pack_small(parts):
    flat = []
    for name, size in _SMALL:
        a = parts[name].reshape(-1)
        flat.append(jnp.pad(a, (0, size - a.shape[0])))
    flat = jnp.concatenate(flat)
    flat = jnp.pad(flat, (0, _SMALL_ROWS * LANE - flat.shape[0]))
    return flat.reshape(_SMALL_ROWS, LANE)


def _unpack_small(packed, shapes):
    flat = packed.reshape(-1)
    out, off = {}, 0
    for name, size in _SMALL:
        n = 1
        for s in shapes[name]:
            n *= s
        out[name] = flat[off:off + n].reshape(shapes[name])
        off += size
    return out


def kernel(x, c, w_ada, b_ada, w_in, b_in, w_pool_mix, b_pool_mix, pool_scale, w_out, b_out, ln_g, ln_b, loss_target, m_w_ada, m_b_ada, m_w_in, m_b_in, m_w_pool_mix, m_b_pool_mix, m_pool_scale, m_w_out, m_b_out, m_ln_g, m_ln_b, v_w_ada, v_b_ada, v_w_in, v_b_in, v_w_pool_mix, v_b_pool_mix, v_pool_scale, v_w_out, v_b_out, v_ln_g, v_ln_b):
    seq = x.shape[1]
    tile = min(256, seq)
    me = _dev_index(*_mesh_pos())
    x2, tgt = x[0], loss_target[0]

    c_all, w_in_g, w_out_g = _all_gather(
        [jnp.pad(c, ((0, 7), (0, 0))), w_in[0].astype(BF16), w_out[0].astype(BF16)], "gather_weights")
    sc_all, ada_part = _ada_forward(c_all[:, 0, :], w_ada[0])
    (ada_mine,) = _exchange([ada_part.reshape(N_DEV, 1, -1)], [], "exchange_ada")
    ada = ada_mine.reshape(1, D_ADA) + b_ada
    shift, scale, gate = ada[:, 0:D], ada[:, D:2 * D], ada[:, 2 * D:]
    mod = jnp.concatenate([1.0 + scale, shift, jnp.zeros((6, D), F32)], axis=0)

    w_in_full = w_in_g.transpose(1, 0, 2).reshape(D, D_IN)
    w_main, w_f = _to_main(w_in_full)
    b_main, b_f = _to_main(b_in)
    w_out_full = w_out_g.reshape(D, D)

    qp, kp, vp, f, p, g_att, g_pool, ut = _inproj_forward(x2, mod, w_main, w_f, b_main, b_f, tile)
    att, lse = _attention_forward(qp, kp, vp, tile)

    vecs = jnp.concatenate([gate, b_out, ln_g, ln_b, jnp.zeros((4, D), F32)], axis=0)
    pool_vecs = jnp.concatenate([b_pool_mix.reshape(1, D_POOL), pool_scale, jnp.zeros((6, D_POOL), F32)], axis=0)
    dxa, d_att, d_ga, d_gp, d_pooled, dw_out, dw_pool, dvec, dpvec = _middle(
        x2, tgt, att, g_att, g_pool, p, vecs, pool_vecs, w_out_full, w_pool_mix[0].astype(BF16), tile)

    dqp, dkp, dvp = _attention_backward(qp, kp, vp, d_att, att, lse, tile)
    dx, dproj, dw_f, db_main, db_f, dmod = _inproj_backward(
        dqp, dkp, dvp, f, d_pooled, d_ga, d_gp, x2, dxa, ut, mod, w_main, w_f, tile)
    dw_main = _weight_grad(ut, dproj, min(1024, seq))

    gw_in = _from_main(dw_main, dw_f).reshape(D, N_DEV, D_IN // N_DEV).transpose(1, 0, 2)
    gw_out = dw_out.reshape(N_DEV, D // N_DEV, D)
    d_ada = jnp.concatenate([dmod[1:2], dmod[0:1], dvec[0:1]], axis=1)
    small = _pack_small({
        "b_in": _from_main(db_main[0:1], db_f[0:1]), "w_pool": dw_pool, "b_pool": dpvec[0:1],
        "pool_scale": dpvec[1:2], "b_out": dvec[1:2], "ln_g": dvec[2:3], "ln_b": dvec[3:4], "b_ada": d_ada})
    gw_in_parts, gw_out_parts, small_parts = _exchange([gw_in, gw_out], [small], "exchange_grads")

    loss = lax.psum(dvec[4, 0], ("x", "y", "c"))

    g_w_in, d_w_in, nm_w_in, nv_w_in = _sum_adamw(gw_in_parts, w_in[0], m_w_in[0], v_w_in[0], 128, "adamw_w_in")
    g_w_out, d_w_out, nm_w_out, nv_w_out = _sum_adamw(gw_out_parts, w_out[0], m_w_out[0], v_w_out[0], 128, "adamw_w_out")
    weights = {"b_in": b_in, "w_pool": w_pool_mix, "b_pool": b_pool_mix, "pool_scale": pool_scale,
               "b_out": b_out, "ln_g": ln_g, "ln_b": ln_b, "b_ada": b_ada}
    first = {"b_in": m_b_in, "w_pool": m_w_pool_mix, "b_pool": m_b_pool_mix, "pool_scale": m_pool_scale,
             "b_out": m_b_out, "ln_g": m_ln_g, "ln_b": m_ln_b, "b_ada": m_b_ada}
    second = {"b_in": v_b_in, "w_pool": v_w_pool_mix, "b_pool": v_b_pool_mix, "pool_scale": v_pool_scale,
              "b_out": v_b_out, "ln_g": v_ln_g, "ln_b": v_ln_b, "b_ada": v_b_ada}
    packed = _sum_adamw(small_parts, _pack_small(weights), _pack_small(first), _pack_small(second),
                        _SMALL_ROWS, "adamw_small")
    shapes = {k: a.shape for k, a in weights.items()}
    g_s, d_s, nm_s, nv_s = (_unpack_small(a, shapes) for a in packed)

    d_ada_all = small_parts.reshape(N_DEV, -1)[:, _SMALL_OFF_ADA:_SMALL_OFF_ADA + D_ADA]
    d_ada_local = lax.dynamic_slice_in_dim(d_ada_all, me * (D_ADA // N_DEV), D_ADA // N_DEV, axis=1)
    g_w_ada, d_w_ada, nm_w_ada, nv_w_ada = _ada_adamw(sc_all.T, d_ada_local, w_ada[0], m_w_ada[0], v_w_ada[0])

    def ordered(w_ada_, w_in_, w_out_, s):
        return (w_ada_[None], s["b_ada"], w_in_[None], s["b_in"], s["w_pool"], s["b_pool"], s["pool_scale"],
                w_out_[None], s["b_out"], s["ln_g"], s["ln_b"])

    return (loss, dx[None],
            *ordered(g_w_ada, g_w_in, g_w_out, g_s),
            *ordered(d_w_ada, d_w_in, d_w_out, d_s),
            *ordered(nm_w_ada, nm_w_in, nm_w_out, nm_s),
            *ordered(nv_w_ada, nv_w_in, nv_w_out, nv_s))
```

```python
import jax
import jax.numpy as jnp
from jax import lax
from jax.experimental import pallas as pl
from jax.experimental.pallas import tpu as pltpu

F32 = jnp.float32
BF16 = jnp.bfloat16

N_DEV = 8
D = 1024
N_HEADS = 8
HEAD_DIM = 64
D_ATT = 512
D_POOL = 512
POOL_WINDOWS = (2, 4, 8, 16)
GROUP_DIM = 128
HALO = 16
LANE = 128
D_IN = 3080
D_ADA = 3072
N_MAIN = 4608
OFF_P, OFF_GA, OFF_GP = 3072, 3584, 4096
COL_CHUNK = 512
Q_SCALE = 0.125
LN_EPS = 1e-5
ALPHA = 2.0 ** 0.25
L_CQ, L_CK, L_LSE = 64, 67, 70

ADAM_LR, ADAM_B1, ADAM_B2, ADAM_EPS, ADAM_WD, ADAM_STEP = 0.001, 0.9, 0.999, 1e-08, 0.01, 10
VMEM_LIMIT = 56 * 1024 * 1024

MESH = pl.DeviceIdType.MESH
ANY = pl.BlockSpec(memory_space=pl.ANY)


def _params(sem=None, vmem=VMEM_LIMIT):
    return pltpu.CompilerParams(dimension_semantics=sem, vmem_limit_bytes=vmem)


def _split3(a):
    hi = a.astype(BF16)
    r = a - hi.astype(F32)
    mid = r.astype(BF16)
    lo = (r - mid.astype(F32)).astype(BF16)
    return hi, mid, lo


def _dot(a, b):
    return jnp.dot(a, b, preferred_element_type=F32)


def _dot_nt(a, b):
    return lax.dot_general(a, b, (((1,), (1,)), ((), ())), preferred_element_type=F32)


def _dot_tn(a, b):
    return lax.dot_general(a, b, (((0,), (0,)), ((), ())), preferred_element_type=F32)


def _dot3(m01, a):
    hi, mid, lo = _split3(a)
    return _dot(m01, hi) + _dot(m01, mid) + _dot(m01, lo)


def _sigmoid(z):
    return 1.0 / (1.0 + jnp.exp(-z))


def _lanes(shape):
    return lax.broadcasted_iota(jnp.int32, shape, len(shape) - 1)


def _place3(lane, base, parts, other):
    out = other
    for j in range(3):
        out = jnp.where(lane == base + j, parts[j], out)
    return out


def _mesh_pos():
    return lax.axis_index("x"), lax.axis_index("y"), lax.axis_index("c")


def _dev_index(px, py, pc):
    return 4 * px + 2 * py + pc


def _all_gather(blocks, name):
    n = len(blocks)

    def body(*refs):
        ins, outs = refs[:n], refs[n:2 * n]
        send_sems, recv_sems, local_sems = refs[2 * n:]
        x, y, c = _mesh_pos()
        me, sibling = (x, y, c), (x, y, 1 - c)
        chips = [(1 - x, y), (x, 1 - y), (1 - x, 1 - y)]

        def copy(a, k, block, to, src=None):
            slot = outs[a].at[_dev_index(*block)]
            return pltpu.make_async_remote_copy(
                src_ref=slot if src is None else src, dst_ref=slot,
                send_sem=send_sems.at[a, k], recv_sem=recv_sems.at[a, k],
                device_id=to, device_id_type=MESH)

        mine = [pltpu.make_async_copy(ins[a], outs[a].at[_dev_index(*me)], local_sems.at[a]) for a in range(n)]
        for cp in mine:
            cp.start()
        first = []
        for a in range(n):
            first.append(copy(a, 0, me, sibling, src=ins[a]))
            first += [copy(a, 1 + j, me, (*chip, c), src=ins[a]) for j, chip in enumerate(chips)]
        for cp in first:
            cp.start()
        passed = []
        for j, chip in enumerate(chips):
            for a in range(n):
                copy(a, 1 + j, (*chip, c), me).wait_recv()
                fwd = copy(a, 4 + j, (*chip, c), sibling)
                fwd.start()
                passed.append(fwd)
        for a in range(n):
            copy(a, 0, sibling, me).wait_recv()
            for j, chip in enumerate(chips):
                copy(a, 4 + j, (*chip, 1 - c), me).wait_recv()
        for cp in first + passed:
            cp.wait_send()
        for cp in mine:
            cp.wait()

    return pl.pallas_call(
        body, name=name,
        out_shape=[jax.ShapeDtypeStruct((N_DEV,) + b.shape, b.dtype) for b in blocks],
        in_specs=[ANY] * n, out_specs=[ANY] * n,
        scratch_shapes=[pltpu.SemaphoreType.DMA((n, 7)), pltpu.SemaphoreType.DMA((n, 7)),
                        pltpu.SemaphoreType.DMA((n,))],
    )(*blocks)


def _exchange(scatter, gather, name):
    ns, n = len(scatter), len(scatter) + len(gather)
    arrays = list(scatter) + list(gather)

    def body(*refs):
        ins, outs = refs[:n], refs[n:2 * n]
        send_sems, recv_sems, local_sems = refs[2 * n:]
        x, y, c = _mesh_pos()
        me = _dev_index(x, y, c)
        peers = []
        for p in range(1, N_DEV):
            px, py, pc = (p >> 2) & 1, (p >> 1) & 1, p & 1
            peers.append((1 - x if px else x, 1 - y if py else y, 1 - c if pc else c))

        def src_for(a, slot):
            return ins[a].at[slot] if a < ns else ins[a]

        def copy(a, k, to):
            return pltpu.make_async_remote_copy(
                src_ref=src_for(a, _dev_index(*to)), dst_ref=outs[a].at[me],
                send_sem=send_sems.at[a, k], recv_sem=recv_sems.at[a, k],
                device_id=to, device_id_type=MESH)

        def arrival(a, k, frm):
            slot = _dev_index(*frm)
            return pltpu.make_async_remote_copy(
                src_ref=src_for(a, slot), dst_ref=outs[a].at[slot],
                send_sem=send_sems.at[a, k], recv_sem=recv_sems.at[a, k],
                device_id=frm, device_id_type=MESH)

        mine = [pltpu.make_async_copy(src_for(a, me), outs[a].at[me], local_sems.at[a]) for a in range(n)]
        for cp in mine:
            cp.start()
        sends = [copy(a, k, to) for k, to in enumerate(peers) for a in range(n)]
        for cp in sends:
            cp.start()
        for k, frm in enumerate(peers):
            for a in range(n):
                arrival(a, k, frm).wait_recv()
        for cp in sends:
            cp.wait_send()
        for cp in mine:
            cp.wait()

    out_shape = [jax.ShapeDtypeStruct(a.shape, a.dtype) for a in scatter]
    out_shape += [jax.ShapeDtypeStruct((N_DEV,) + a.shape, a.dtype) for a in gather]
    return pl.pallas_call(
        body, name=name, out_shape=out_shape,
        in_specs=[ANY] * n, out_specs=[ANY] * n,
        scratch_shapes=[pltpu.SemaphoreType.DMA((n, 7)), pltpu.SemaphoreType.DMA((n, 7)),
                        pltpu.SemaphoreType.DMA((n,))],
    )(*arrays)


def _ada_forward(c_all, w_ada):
    def body(c_ref, w_ref, sc_ref, part_ref):
        cc = c_ref[...]
        sc = cc * _sigmoid(cc)
        sc_ref[...] = sc
        part_ref[...] = _dot3_rhs(sc, w_ref[...])

    return pl.pallas_call(
        body, name="ada_forward",
        out_shape=[jax.ShapeDtypeStruct(c_all.shape, F32),
                   jax.ShapeDtypeStruct((N_DEV, w_ada.shape[1]), F32)],
        compiler_params=_params(),
    )(c_all, w_ada)


def _dot3_rhs(a, b):
    a0, a1, a2 = _split3(a)
    b0, b1, b2 = _split3(b)
    return (_dot(a0, b0) + (_dot(a0, b1) + _dot(a1, b0))
            + (_dot(a0, b2) + _dot(a1, b1) + _dot(a2, b0)))


def _inproj_forward(x, mod, w_main, w_f, b_main, b_f, tile):
    seq = x.shape[0]
    nt = seq // tile

    def body(x_ref, mod_ref, w_ref, wf_ref, b_ref, bf_ref,
             qp_ref, kp_ref, vp_ref, f_ref, p_ref, ga_ref, gp_ref, ut_ref, carry_ref):
        i = pl.program_id(0)

        @pl.when(i == 0)
        def _():
            carry_ref[...] = jnp.zeros_like(carry_ref)

        u = x_ref[...] * mod_ref[0:1, :] + mod_ref[1:2, :]
        ub = u.astype(BF16)
        ut_ref[...] = u.T.astype(BF16)

        f = _dot(ub, wf_ref[...]) + bf_ref[...]
        f_ref[...] = f
        lane = _lanes((tile, LANE))
        log_f = jnp.where(lane < N_HEADS, jnp.minimum(f, 0.0) - jnp.log(1.0 + jnp.exp(-jnp.abs(f))), 0.0)
        row = lax.broadcasted_iota(jnp.int32, (tile, tile), 0)
        col = lax.broadcasted_iota(jnp.int32, (tile, tile), 1)
        tri = (row >= col).astype(BF16)
        cum = _dot3(tri, log_f) + carry_ref[0:1, :]
        carry_ref[0:1, :] = cum[tile - 1:tile, :]
        cq = [part.astype(F32) for part in _split3(cum)]
        ck = [part.astype(F32) for part in _split3(-cum)]

        def proj(chunk):
            cols = pl.ds(chunk * COL_CHUNK, COL_CHUNK)
            return _dot(ub, w_ref[:, cols]) + b_ref[:, cols]

        heads_per_chunk = COL_CHUNK // LANE
        for chunk in range(2):
            r = proj(chunk)
            for j in range(heads_per_chunk):
                h = chunk * heads_per_chunk + j
                extra = jnp.where((lane >= L_CK) & (lane < L_CK + 3), 1.0, 0.0)
                extra = _place3(lane, L_CQ, [part[:, h:h + 1] for part in cq], extra)
                qp_ref[h] = jnp.where(lane < HEAD_DIM, r[:, j * LANE:(j + 1) * LANE] * Q_SCALE, extra).astype(BF16)
        for chunk in range(2):
            r = proj(2 + chunk)
            for j in range(heads_per_chunk):
                h = chunk * heads_per_chunk + j
                ones = ((lane >= L_CQ) & (lane < L_CQ + 3)) | ((lane >= L_LSE) & (lane < L_LSE + 3))
                extra = _place3(lane, L_CK, [part[:, h:h + 1] for part in ck], jnp.where(ones, 1.0, 0.0))
                kp_ref[h] = jnp.where(lane < HEAD_DIM, r[:, j * LANE:(j + 1) * LANE], extra).astype(BF16)
        for chunk in range(2):
            r = proj(4 + chunk)
            for j in range(heads_per_chunk):
                h = chunk * heads_per_chunk + j
                extra = jnp.where((lane >= HEAD_DIM) & (lane < HEAD_DIM + 3), -1.0, 0.0)
                vp_ref[h] = jnp.where(lane < HEAD_DIM, r[:, j * LANE:(j + 1) * LANE], extra).astype(BF16)
        p_ref[...] = proj(6)
        ga_ref[...] = proj(7)
        gp_ref[...] = proj(8)

    head_block = pl.BlockSpec((N_HEADS, tile, LANE), lambda i: (0, i, 0))
    tok = lambda width: pl.BlockSpec((tile, width), lambda i: (i, 0))
    whole = lambda a: pl.BlockSpec(a.shape, lambda i: (0,) * a.ndim)
    padded = jax.ShapeDtypeStruct((N_HEADS, seq, LANE), BF16)
    half = jax.ShapeDtypeStruct((seq, D_ATT), F32)
    return pl.pallas_call(
        body, name="inproj_forward", grid=(nt,),
        in_specs=[tok(D), whole(mod), whole(w_main), whole(w_f), whole(b_main), whole(b_f)],
        out_specs=[head_block, head_block, head_block, tok(LANE), tok(D_POOL), tok(D_ATT), tok(D_POOL),
                   pl.BlockSpec((D, tile), lambda i: (0, i))],
        out_shape=[padded, padded, padded, jax.ShapeDtypeStruct((seq, LANE), F32), half, half, half,
                   jax.ShapeDtypeStruct((D, seq), BF16)],
        scratch_shapes=[pltpu.VMEM((8, LANE), F32)],
        compiler_params=_params(("arbitrary",)),
    )(x, mod, w_main, w_f, b_main, b_f)


def _attention_forward(qp, kp, vp, tile):
    seq = qp.shape[1]
    nb = seq // tile

    def body(q_ref, k_ref, v_ref, att_ref, q2_ref):
        i = pl.program_id(1)
        lane = _lanes((tile, LANE))
        row = lax.broadcasted_iota(jnp.int32, (tile, tile), 0)
        col = lax.broadcasted_iota(jnp.int32, (tile, tile), 1)
        q = [q_ref[0], q_ref[1]]

        def step(kb, carry, masked):
            rows = pl.ds(pl.multiple_of(kb * tile, tile), tile)
            out = []
            for hh in range(2):
                m, acc = carry[hh]
                s = _dot_nt(q[hh], k_ref[hh, rows, :])
                if masked:
                    s = jnp.where(col <= row, s, -1e30)
                m_new = jnp.maximum(m, jnp.max(s, axis=1, keepdims=True))
                p = jnp.exp(s - m_new).astype(BF16)
                acc = jnp.exp(m - m_new) * acc + _dot(p, v_ref[hh, rows, :])
                out.append((m_new, acc))
            return tuple(out)

        init = (jnp.full((tile, 1), -1e30, F32), jnp.zeros((tile, LANE), F32))
        carry = lax.fori_loop(0, i, lambda kb, cr: step(kb, cr, False), (init, init))
        carry = step(i, carry, True)
        outs = []
        for hh in range(2):
            m, acc = carry[hh]
            l = -acc[:, HEAD_DIM:HEAD_DIM + 1]
            outs.append(acc / l)
            neg_lse = [part.astype(F32) for part in _split3(-(m + jnp.log(l)))]
            q2_ref[hh] = _place3(lane, L_LSE, neg_lse, q[hh].astype(F32)).astype(BF16)
        att_ref[...] = jnp.where(lane < HEAD_DIM, outs[0], pltpu.roll(outs[1], HEAD_DIM, 1))

    pair = pl.BlockSpec((2, tile, LANE), lambda hp, i: (hp, i, 0))
    full = pl.BlockSpec((2, seq, LANE), lambda hp, i: (hp, 0, 0))
    return pl.pallas_call(
        body, name="attention_forward", grid=(N_HEADS // 2, nb),
        in_specs=[pair, full, full],
        out_specs=[pl.BlockSpec((tile, LANE), lambda hp, i: (i, hp)), pair],
        out_shape=[jax.ShapeDtypeStruct((seq, D_ATT), F32),
                   jax.ShapeDtypeStruct((N_HEADS, seq, LANE), BF16)],
        compiler_params=_params(("arbitrary", "arbitrary")),
    )(qp, kp, vp)


def _window_matrices(tile, window, transposed):
    r = lax.broadcasted_iota(jnp.int32, (tile, tile), 0)
    c = lax.broadcasted_iota(jnp.int32, (tile, tile), 1)
    rh = lax.broadcasted_iota(jnp.int32, (tile, HALO), 0)
    ch = lax.broadcasted_iota(jnp.int32, (tile, HALO), 1)
    if not transposed:
        cur = (c <= r) & (r - c < window)
        halo = (rh + HALO - ch) < window
    else:
        cur = (r <= c) & (c - r < window)
        halo = (tile + ch - rh) < window
    return cur.astype(BF16), halo.astype(BF16)


def _silu_parts(g):
    sig = _sigmoid(g)
    return g * sig, sig * (1.0 + g * (1.0 - sig))


def _head_placement():
    r = jnp.arange(D_ATT + 3 * LANE)[:, None]
    n = jnp.arange(N_HEADS * LANE)[None, :]
    h, l = n // LANE, n % LANE
    src = jnp.where(l < HEAD_DIM, HEAD_DIM * h + l, D_ATT + LANE * (l - HEAD_DIM) + h)
    return ((r == src) & (l < HEAD_DIM + 3)).astype(BF16)


def _middle(x, tgt, att, g_att, g_pool, p, vecs, pool_vecs, w_out, w_pool, place, tile):
    seq = x.shape[0]
    nt = seq // tile
    halo_blocks = tile // HALO

    def body(x_ref, tgt_ref, att_ref, ga_ref, gp_ref, p_ref, ph_ref, vec_ref, pvec_ref, wo_ref, wp_ref, place_ref,
             dxa_ref, do2_ref, dga_ref, dgp_ref, dpooled_ref, dwo_ref, dwp_ref, dvec_ref, dpvec_ref):
        i = pl.program_id(0)

        @pl.when(i == 0)
        def _():
            dwo_ref[...] = jnp.zeros_like(dwo_ref)
            dwp_ref[...] = jnp.zeros_like(dwp_ref)
            dvec_ref[...] = jnp.zeros_like(dvec_ref)
            dpvec_ref[...] = jnp.zeros_like(dpvec_ref)

        gate, b_out, ln_g, ln_b = (vec_ref[k:k + 1, :] for k in range(4))
        b_pool, pool_scale = pvec_ref[0:1, :], pvec_ref[1:2, :]
        x = x_ref[...]
        p = p_ref[...]
        p_halo = ph_ref[...] * jnp.where(i > 0, 1.0, 0.0)
        pos = i * tile + lax.broadcasted_iota(jnp.int32, (tile, 1), 0) + 1

        pooled, mixed = [], []
        for g, window in enumerate(POOL_WINDOWS):
            cols = slice(g * GROUP_DIM, (g + 1) * GROUP_DIM)
            m_cur, m_halo = _window_matrices(tile, window, False)
            wsum = _dot3(m_cur, p[:, cols]) + _dot3(m_halo, p_halo[:, cols])
            count = jnp.minimum(pos, window).astype(F32)
            pooled.append(wsum / count - p[:, cols])
            mixed.append(_dot(pooled[g].astype(BF16), wp_ref[g]) + b_pool[:, cols])
        mixed = jnp.concatenate(mixed, axis=1)
        pool = mixed * pool_scale

        att = att_ref[...]
        g_att, g_pool = ga_ref[...], gp_ref[...]
        silu_a, dsilu_a = _silu_parts(g_att)
        silu_p, dsilu_p = _silu_parts(g_pool)
        y_in = jnp.concatenate([att * silu_a, pool * silu_p], axis=1)
        y = _dot(y_in.astype(BF16), wo_ref[...]) + b_out
        h = ALPHA * x + gate * y
        mu = jnp.mean(h, axis=1, keepdims=True)
        hc = h - mu
        var = jnp.mean(hc * hc, axis=1, keepdims=True)
        rstd = lax.rsqrt(var + LN_EPS)
        yhat = hc * rstd
        diff = yhat * ln_g + ln_b - tgt_ref[...]
        loss_rows = jnp.sum(diff * diff, axis=1, keepdims=True)
        d_out = diff * (1.0 / D)

        d_yhat = d_out * ln_g
        dh = rstd * (d_yhat - jnp.mean(d_yhat, axis=1, keepdims=True)
                     - yhat * jnp.mean(d_yhat * yhat, axis=1, keepdims=True))
        dxa_ref[...] = ALPHA * dh
        dy = dh * gate
        dyb = dy.astype(BF16)
        lane = _lanes((1, D))
        loss_row = jnp.where(lane == 0, (0.5 / D) * jnp.sum(loss_rows, axis=0, keepdims=True), 0.0)
        dvec_ref[0:1, :] += jnp.sum(dh * y, axis=0, keepdims=True)
        dvec_ref[1:2, :] += jnp.sum(dy, axis=0, keepdims=True)
        dvec_ref[2:3, :] += jnp.sum(d_out * yhat, axis=0, keepdims=True)
        dvec_ref[3:4, :] += jnp.sum(d_out, axis=0, keepdims=True)
        dvec_ref[4:5, :] += loss_row

        dwo_ref[...] += _dot(y_in.T.astype(BF16), dyb)
        d_yin = _dot_nt(dyb, wo_ref[...])
        d_a, d_pl = d_yin[:, :D_ATT], d_yin[:, D_ATT:]
        d_att = d_a * silu_a
        r = lax.broadcasted_iota(jnp.int32, (D_ATT, LANE), 0)
        c = lax.broadcasted_iota(jnp.int32, (D_ATT, LANE), 1)
        head_of = ((r >= HEAD_DIM * c) & (r < HEAD_DIM * c + HEAD_DIM)).astype(BF16)
        delta = sum(_dot(part, head_of) for part in _split3(d_att * att))
        do2 = _dot(jnp.concatenate([d_att.astype(BF16), *_split3(delta)], axis=1), place_ref[...])
        for h in range(N_HEADS):
            do2_ref[h] = do2[:, h * LANE:(h + 1) * LANE].astype(BF16)
        dga_ref[...] = d_a * att * dsilu_a
        dgp_ref[...] = d_pl * pool * dsilu_p
        d_pool = d_pl * silu_p
        d_mixed = d_pool * pool_scale
        dpvec_ref[0:1, :] += jnp.sum(d_mixed, axis=0, keepdims=True)
        dpvec_ref[1:2, :] += jnp.sum(d_pool * mixed, axis=0, keepdims=True)
        d_pooled = []
        for g in range(len(POOL_WINDOWS)):
            cols = slice(g * GROUP_DIM, (g + 1) * GROUP_DIM)
            dmb = d_mixed[:, cols].astype(BF16)
            dwp_ref[g] += _dot(pooled[g].T.astype(BF16), dmb)
            d_pooled.append(_dot_nt(dmb, wp_ref[g]))
        dpooled_ref[...] = jnp.concatenate(d_pooled, axis=1)

    tok = lambda width: pl.BlockSpec((tile, width), lambda i: (i, 0))
    whole = lambda a: pl.BlockSpec(a.shape, lambda i: (0,) * a.ndim)
    halo = pl.BlockSpec((HALO, D_POOL), lambda i: (jnp.maximum(i * halo_blocks - 1, 0), 0))
    half = jax.ShapeDtypeStruct((seq, D_ATT), F32)
    outs = [jax.ShapeDtypeStruct((seq, D), F32), jax.ShapeDtypeStruct((N_HEADS, seq, LANE), BF16), half, half, half,
            jax.ShapeDtypeStruct(w_out.shape, F32), jax.ShapeDtypeStruct(w_pool.shape, F32),
            jax.ShapeDtypeStruct(vecs.shape, F32), jax.ShapeDtypeStruct(pool_vecs.shape, F32)]
    return pl.pallas_call(
        body, name="middle", grid=(nt,),
        in_specs=[tok(D), tok(D), tok(D_ATT), tok(D_ATT), tok(D_POOL), tok(D_POOL), halo,
                  whole(vecs), whole(pool_vecs), whole(w_out), whole(w_pool), whole(place)],
        out_specs=[tok(D), pl.BlockSpec((N_HEADS, tile, LANE), lambda i: (0, i, 0)),
                   tok(D_ATT), tok(D_POOL), tok(D_POOL),
                   whole(w_out), whole(w_pool), whole(vecs), whole(pool_vecs)],
        out_shape=outs,
        compiler_params=_params(("arbitrary",)),
    )(x, tgt, att, g_att, g_pool, p, p, vecs, pool_vecs, w_out, w_pool, place)


def _attention_backward(q2, kp, vp, do2, tile):
    seq = q2.shape[1]
    nb = seq // tile

    def body(q_ref, k_ref, v_ref, do_ref, dq_ref, dk_ref, dv_ref, dcum_ref):
        hp = pl.program_id(0)
        lane = _lanes((tile, LANE))
        row = lax.broadcasted_iota(jnp.int32, (tile, tile), 0)
        col = lax.broadcasted_iota(jnp.int32, (tile, tile), 1)
        dq_ref[...] = jnp.zeros_like(dq_ref)

        @pl.when(hp == 0)
        def _():
            dcum_ref[...] = jnp.zeros_like(dcum_ref)

        def kv_block(kb, _):
            krows = pl.ds(pl.multiple_of(kb * tile, tile), tile)
            k = [k_ref[hh, krows, :] for hh in range(2)]
            v = [v_ref[hh, krows, :] for hh in range(2)]

            def q_block(qb, carry, masked):
                qrows = pl.ds(pl.multiple_of(qb * tile, tile), tile)
                out = []
                for hh in range(2):
                    dk, dv = carry[hh]
                    q = q_ref[hh, qrows, :]
                    do = do_ref[hh, qrows, :]
                    s_t = _dot_nt(k[hh], q)
                    if masked:
                        s_t = jnp.where(row <= col, s_t, -1e30)
                    p_t = jnp.exp(s_t)
                    ds_t = (p_t * _dot_nt(v[hh], do)).astype(BF16)
                    dv = dv + _dot(p_t.astype(BF16), do)
                    dk = dk + _dot(ds_t, q)
                    dq_ref[hh, qrows, :] += _dot_tn(ds_t, k[hh])
                    out.append((dk, dv))
                return tuple(out)

            zero = jnp.zeros((tile, LANE), F32)
            carry = q_block(kb, ((zero, zero), (zero, zero)), True)
            carry = lax.fori_loop(kb + 1, nb, lambda qb, cr: q_block(qb, cr, False), carry)
            for hh in range(2):
                dk = carry[hh][0]
                dk_ref[hh, krows, :] = dk.astype(BF16)
                dv_ref[hh, krows, :] = carry[hh][1].astype(BF16)
                dcum_ref[krows, :] += jnp.where(lane == 2 * hp + hh, -dk[:, L_CK:L_CK + 1], 0.0)
            return 0

        lax.fori_loop(0, nb, kv_block, 0)

        def row_sums(qb, _):
            qrows = pl.ds(pl.multiple_of(qb * tile, tile), tile)
            for hh in range(2):
                dcum_ref[qrows, :] += jnp.where(lane == 2 * hp + hh, dq_ref[hh, qrows, L_CQ:L_CQ + 1], 0.0)
            return 0

        lax.fori_loop(0, nb, row_sums, 0)

    pair = pl.BlockSpec((2, seq, LANE), lambda hp: (hp, 0, 0))
    return pl.pallas_call(
        body, name="attention_backward", grid=(N_HEADS // 2,),
        in_specs=[pair, pair, pair, pair],
        out_specs=[pair, pair, pair, pl.BlockSpec((seq, LANE), lambda hp: (0, 0))],
        out_shape=[jax.ShapeDtypeStruct((N_HEADS, seq, LANE), F32),
                   jax.ShapeDtypeStruct((N_HEADS, seq, LANE), BF16),
                   jax.ShapeDtypeStruct((N_HEADS, seq, LANE), BF16),
                   jax.ShapeDtypeStruct((seq, LANE), F32)],
        compiler_params=_params(("arbitrary",)),
    )(q2, kp, vp, do2)


def _inproj_backward(dqp, dkp, dvp, d_cum, f, d_pooled, d_ga, d_gp, x, dxa, ut, mod, w_main, w_f, tile):
    seq = x.shape[0]
    nt = seq // tile
    halo_blocks = tile // HALO

    def body(dq_ref, dk_ref, dv_ref, dcum_ref, f_ref, dpo_ref, dph_ref, dga_ref, dgp_ref, x_ref, dxa_ref, ut_ref,
             mod_ref, w_ref, wf_ref,
             dx_ref, dproj_ref, dwf_ref, db_ref, dbf_ref, dmod_ref, carry_ref):
        step = pl.program_id(0)
        i = nt - 1 - step

        @pl.when(step == 0)
        def _():
            carry_ref[...] = jnp.zeros_like(carry_ref)
            dwf_ref[...] = jnp.zeros_like(dwf_ref)
            db_ref[...] = jnp.zeros_like(db_ref)
            dbf_ref[...] = jnp.zeros_like(dbf_ref)
            dmod_ref[...] = jnp.zeros_like(dmod_ref)

        lane = _lanes((tile, LANE))
        head_lanes = lane < HEAD_DIM

        def emit(chunk, val):
            cols = pl.ds(chunk * COL_CHUNK, COL_CHUNK)
            db_ref[0:1, cols] += jnp.sum(val, axis=0, keepdims=True)
            vb = val.astype(BF16)
            dproj_ref[:, cols] = vb
            return _dot_nt(vb, w_ref[:, cols])

        heads_per_chunk = COL_CHUNK // LANE
        d_u = jnp.zeros((tile, D), F32)
        for chunk in range(2):
            parts = [jnp.where(head_lanes, dq_ref[chunk * heads_per_chunk + j] * Q_SCALE, 0.0)
                     for j in range(heads_per_chunk)]
            d_u += emit(chunk, jnp.concatenate(parts, axis=1))
        for chunk in range(2):
            parts = [jnp.where(head_lanes, dk_ref[chunk * heads_per_chunk + j].astype(F32), 0.0)
                     for j in range(heads_per_chunk)]
            d_u += emit(2 + chunk, jnp.concatenate(parts, axis=1))
        for chunk in range(2):
            parts = [jnp.where(head_lanes, dv_ref[chunk * heads_per_chunk + j].astype(F32), 0.0)
                     for j in range(heads_per_chunk)]
            d_u += emit(4 + chunk, jnp.concatenate(parts, axis=1))

        d_pooled = dpo_ref[...]
        d_halo = dph_ref[...] * jnp.where(i < nt - 1, 1.0, 0.0)
        pos = i * tile + lax.broadcasted_iota(jnp.int32, (tile, 1), 0) + 1
        d_p = []
        for g, window in enumerate(POOL_WINDOWS):
            cols = slice(g * GROUP_DIM, (g + 1) * GROUP_DIM)
            m_cur, m_halo = _window_matrices(tile, window, True)
            scaled = d_pooled[:, cols] / jnp.minimum(pos, window).astype(F32)
            d_p.append(_dot3(m_cur, scaled) + _dot3(m_halo, d_halo[:, cols] * (1.0 / window)) - d_pooled[:, cols])
        d_u += emit(6, jnp.concatenate(d_p, axis=1))
        d_u += emit(7, dga_ref[...])
        d_u += emit(8, dgp_ref[...])

        row = lax.broadcasted_iota(jnp.int32, (tile, tile), 0)
        col = lax.broadcasted_iota(jnp.int32, (tile, tile), 1)
        d_logf = _dot3((row <= col).astype(BF16), dcum_ref[...]) + carry_ref[0:1, :]
        carry_ref[0:1, :] = d_logf[0:1, :]
        d_f = jnp.where(lane < N_HEADS, d_logf * _sigmoid(-f_ref[...]), 0.0)
        dbf_ref[0:1, :] += jnp.sum(d_f, axis=0, keepdims=True)
        d_fb = d_f.astype(BF16)
        d_u += _dot_nt(d_fb, wf_ref[...])
        dwf_ref[...] += _dot(ut_ref[...], d_fb)

        x = x_ref[...]
        dx_ref[...] = dxa_ref[...] + d_u * mod_ref[0:1, :]
        dmod_ref[0:1, :] += jnp.sum(d_u * x, axis=0, keepdims=True)
        dmod_ref[1:2, :] += jnp.sum(d_u, axis=0, keepdims=True)

    rev = lambda step: nt - 1 - step
    tok = lambda width: pl.BlockSpec((tile, width), lambda s: (rev(s), 0))
    head_block = pl.BlockSpec((N_HEADS, tile, LANE), lambda s: (0, rev(s), 0))
    whole = lambda a: pl.BlockSpec(a.shape, lambda s: (0,) * a.ndim)
    halo = pl.BlockSpec((HALO, D_POOL), lambda s: (jnp.minimum((rev(s) + 1) * halo_blocks, seq // HALO - 1), 0))
    small = lambda width: jax.ShapeDtypeStruct((8, width), F32)
    return pl.pallas_call(
        body, name="inproj_backward", grid=(nt,),
        in_specs=[head_block, head_block, head_block, tok(LANE), tok(LANE), tok(D_POOL), halo, tok(D_ATT), tok(D_POOL),
                  tok(D), tok(D), pl.BlockSpec((D, tile), lambda s: (0, rev(s))),
                  whole(mod), whole(w_main), whole(w_f)],
        out_specs=[tok(D), tok(N_MAIN), pl.BlockSpec((D, LANE), lambda s: (0, 0)),
                   pl.BlockSpec((8, N_MAIN), lambda s: (0, 0)), pl.BlockSpec((8, LANE), lambda s: (0, 0)),
                   pl.BlockSpec((8, D), lambda s: (0, 0))],
        out_shape=[jax.ShapeDtypeStruct((seq, D), F32), jax.ShapeDtypeStruct((seq, N_MAIN), BF16),
                   jax.ShapeDtypeStruct((D, LANE), F32), small(N_MAIN), small(LANE), small(D)],
        scratch_shapes=[pltpu.VMEM((8, LANE), F32)],
        compiler_params=_params(("arbitrary",)),
    )(dqp, dkp, dvp, d_cum, f, d_pooled, d_pooled, d_ga, d_gp, x, dxa, ut, mod, w_main, w_f)


def _weight_grad(ut, dproj, k_tile):
    seq = ut.shape[1]
    nk = seq // k_tile

    def body(ut_ref, dp_ref, out_ref):
        @pl.when(pl.program_id(1) == 0)
        def _():
            out_ref[...] = jnp.zeros_like(out_ref)

        out_ref[...] += _dot(ut_ref[...], dp_ref[...])

    return pl.pallas_call(
        body, name="weight_grad", grid=(N_MAIN // COL_CHUNK, nk),
        in_specs=[pl.BlockSpec((D, k_tile), lambda n, k: (0, k)),
                  pl.BlockSpec((k_tile, COL_CHUNK), lambda n, k: (k, n))],
        out_specs=pl.BlockSpec((D, COL_CHUNK), lambda n, k: (0, n)),
        out_shape=jax.ShapeDtypeStruct((D, N_MAIN), F32),
        compiler_params=_params(("arbitrary", "arbitrary")),
    )(ut, dproj)


def _adamw(w, g, m, v):
    m = ADAM_B1 * m + (1.0 - ADAM_B1) * g
    v = ADAM_B2 * v + (1.0 - ADAM_B2) * (g * g)
    m_hat = m / (1.0 - ADAM_B1 ** ADAM_STEP)
    v_hat = v / (1.0 - ADAM_B2 ** ADAM_STEP)
    delta = -ADAM_LR * (m_hat / (jnp.sqrt(v_hat) + ADAM_EPS) + ADAM_WD * w)
    return delta, m, v


def _sum_adamw(parts, w, m, v, row_tile, name):
    rows, cols = w.shape
    nr = rows // row_tile

    def body(parts_ref, w_ref, m_ref, v_ref, g_ref, d_ref, nm_ref, nv_ref):
        g = parts_ref[0]
        for k in range(1, N_DEV):
            g = g + parts_ref[k]
        g_ref[...] = g
        d_ref[...], nm_ref[...], nv_ref[...] = _adamw(w_ref[...], g, m_ref[...], v_ref[...])

    blk = pl.BlockSpec((row_tile, cols), lambda r: (r, 0))
    shape = jax.ShapeDtypeStruct(w.shape, F32)
    return pl.pallas_call(
        body, name=name, grid=(nr,),
        in_specs=[pl.BlockSpec((N_DEV, row_tile, cols), lambda r: (0, r, 0)), blk, blk, blk],
        out_specs=[blk, blk, blk, blk],
        out_shape=[shape, shape, shape, shape],
        compiler_params=_params(("arbitrary",)),
    )(parts, w, m, v)


def _ada_adamw(sc_t, d_ada, w, m, v):
    def body(sc_ref, d_ref, w_ref, m_ref, v_ref, g_ref, dl_ref, nm_ref, nv_ref):
        g = sc_ref[:, 0:1] * d_ref[0:1, :]
        for b in range(1, N_DEV):
            g = g + sc_ref[:, b:b + 1] * d_ref[b:b + 1, :]
        g_ref[...] = g
        dl_ref[...], nm_ref[...], nv_ref[...] = _adamw(w_ref[...], g, m_ref[...], v_ref[...])

    shape = jax.ShapeDtypeStruct(w.shape, F32)
    return pl.pallas_call(
        body, name="ada_adamw", out_shape=[shape, shape, shape, shape], compiler_params=_params(),
    )(sc_t, d_ada, w, m, v)


def _pad_heads(a):
    lead = a.shape[:-1]
    a = a.reshape(lead + (N_HEADS, HEAD_DIM))
    a = jnp.pad(a, [(0, 0)] * len(lead) + [(0, 0), (0, LANE - HEAD_DIM)])
    return a.reshape(lead + (N_HEADS * LANE,))


def _unpad_heads(a):
    lead = a.shape[:-1]
    return a.reshape(lead + (N_HEADS, LANE))[..., :HEAD_DIM].reshape(lead + (N_HEADS * HEAD_DIM,))


def _to_main(a):
    q, k, v = a[..., 0:512], a[..., 512:1024], a[..., 1024:1536]
    f = a[..., 1536:1544]
    rest = a[..., 1544:]
    main = jnp.concatenate([_pad_heads(q), _pad_heads(k), _pad_heads(v), rest], axis=-1)
    f = jnp.pad(f, [(0, 0)] * (a.ndim - 1) + [(0, LANE - N_HEADS)])
    return main, f


def _from_main(main, f):
    q, k, v = (_unpad_heads(main[..., j * 1024:(j + 1) * 1024]) for j in range(3))
    return jnp.concatenate([q, k, v, f[..., :N_HEADS], main[..., 3072:]], axis=-1)


_SMALL = (("b_in", 3200), ("w_pool", 65536), ("b_pool", 512), ("pool_scale", 512),
          ("b_out", 1024), ("ln_g", 1024), ("ln_b", 1024), ("b_ada", 3072))
_SMALL_ROWS = 600
_SMALL_OFF_ADA = sum(size for name, size in _SMALL if name != "b_ada")


def _pack_small(parts):
    flat = []
    for name, size in _SMALL:
        a = parts[name].reshape(-1)
        flat.append(jnp.pad(a, (0, size - a.shape[0])))
    flat = jnp.concatenate(flat)
    flat = jnp.pad(flat, (0, _SMALL_ROWS * LANE - flat.shape[0]))
    return flat.reshape(_SMALL_ROWS, LANE)


def _unpack_small(packed, shapes):
    flat = packed.reshape(-1)
    out, off = {}, 0
    for name, size in _SMALL:
        n = 1
        for s in shapes[name]:
            n *= s
        out[name] = flat[off:off + n].reshape(shapes[name])
        off += size
    return out


def kernel(x, c, w_ada, b_ada, w_in, b_in, w_pool_mix, b_pool_mix, pool_scale, w_out, b_out, ln_g, ln_b, loss_target, m_w_ada, m_b_ada, m_w_in, m_b_in, m_w_pool_mix, m_b_pool_mix, m_pool_scale, m_w_out, m_b_out, m_ln_g, m_ln_b, v_w_ada, v_b_ada, v_w_in, v_b_in, v_w_pool_mix, v_b_pool_mix, v_pool_scale, v_w_out, v_b_out, v_ln_g, v_ln_b):
    seq = x.shape[1]
    tile = min(256, seq)
    attn_tile = min(512, max(128, seq // 4))
    me = _dev_index(*_mesh_pos())
    x2, tgt = x[0], loss_target[0]

    c_all, w_in_g, w_out_g = _all_gather(
        [jnp.pad(c, ((0, 7), (0, 0))), w_in[0].astype(BF16), w_out[0].astype(BF16)], "gather_weights")
    sc_all, ada_part = _ada_forward(c_all[:, 0, :], w_ada[0])
    (ada_mine,) = _exchange([ada_part.reshape(N_DEV, 1, -1)], [], "exchange_ada")
    ada = ada_mine.reshape(1, D_ADA) + b_ada
    shift, scale, gate = ada[:, 0:D], ada[:, D:2 * D], ada[:, 2 * D:]
    mod = jnp.concatenate([1.0 + scale, shift, jnp.zeros((6, D), F32)], axis=0)

    w_in_full = w_in_g.transpose(1, 0, 2).reshape(D, D_IN)
    w_main, w_f = _to_main(w_in_full)
    b_main, b_f = _to_main(b_in)
    w_out_full = w_out_g.reshape(D, D)

    qp, kp, vp, f, p, g_att, g_pool, ut = _inproj_forward(x2, mod, w_main, w_f, b_main, b_f, tile)
    att, q2 = _attention_forward(qp, kp, vp, attn_tile)

    vecs = jnp.concatenate([gate, b_out, ln_g, ln_b, jnp.zeros((4, D), F32)], axis=0)
    pool_vecs = jnp.concatenate([b_pool_mix.reshape(1, D_POOL), pool_scale, jnp.zeros((6, D_POOL), F32)], axis=0)
    dxa, do2, d_ga, d_gp, d_pooled, dw_out, dw_pool, dvec, dpvec = _middle(
        x2, tgt, att, g_att, g_pool, p, vecs, pool_vecs, w_out_full, w_pool_mix[0].astype(BF16),
        _head_placement(), tile)

    dqp, dkp, dvp, d_cum = _attention_backward(q2, kp, vp, do2, attn_tile)
    dx, dproj, dw_f, db_main, db_f, dmod = _inproj_backward(
        dqp, dkp, dvp, d_cum, f, d_pooled, d_ga, d_gp, x2, dxa, ut, mod, w_main, w_f, tile)
    dw_main = _weight_grad(ut, dproj, min(1024, seq))

    gw_in = _from_main(dw_main, dw_f).reshape(D, N_DEV, D_IN // N_DEV).transpose(1, 0, 2)
    gw_out = dw_out.reshape(N_DEV, D // N_DEV, D)
    d_ada = jnp.concatenate([dmod[1:2], dmod[0:1], dvec[0:1]], axis=1)
    small = _pack_small({
        "b_in": _from_main(db_main[0:1], db_f[0:1]), "w_pool": dw_pool, "b_pool": dpvec[0:1],
        "pool_scale": dpvec[1:2], "b_out": dvec[1:2], "ln_g": dvec[2:3], "ln_b": dvec[3:4], "b_ada": d_ada})
    gw_in_parts, gw_out_parts, small_parts = _exchange([gw_in, gw_out], [small], "exchange_grads")

    loss = lax.psum(dvec[4, 0], ("x", "y", "c"))

    g_w_in, d_w_in, nm_w_in, nv_w_in = _sum_adamw(gw_in_parts, w_in[0], m_w_in[0], v_w_in[0], 128, "adamw_w_in")
    g_w_out, d_w_out, nm_w_out, nv_w_out = _sum_adamw(gw_out_parts, w_out[0], m_w_out[0], v_w_out[0], 128, "adamw_w_out")
    weights = {"b_in": b_in, "w_pool": w_pool_mix, "b_pool": b_pool_mix, "pool_scale": pool_scale,
               "b_out": b_out, "ln_g": ln_g, "ln_b": ln_b, "b_ada": b_ada}
    first = {"b_in": m_b_in, "w_pool": m_w_pool_mix, "b_pool": m_b_pool_mix, "pool_scale": m_pool_scale,
             "b_out": m_b_out, "ln_g": m_ln_g, "ln_b": m_ln_b, "b_ada": m_b_ada}
    second = {"b_in": v_b_in, "w_pool": v_w_pool_mix, "b_pool": v_b_pool_mix, "pool_scale": v_pool_scale,
              "b_out": v_b_out, "ln_g": v_ln_g, "ln_b": v_ln_b, "b_ada": v_b_ada}
    packed = _sum_adamw(small_parts, _pack_small(weights), _pack_small(first), _pack_small(second),
                        _SMALL_ROWS, "adamw_small")
    shapes = {k: a.shape for k, a in weights.items()}
    g_s, d_s, nm_s, nv_s = (_unpack_small(a, shapes) for a in packed)

    d_ada_all = small_parts.reshape(N_DEV, -1)[:, _SMALL_OFF_ADA:_SMALL_OFF_ADA + D_ADA]
    d_ada_local = lax.dynamic_slice_in_dim(d_ada_all, me * (D_ADA // N_DEV), D_ADA // N_DEV, axis=1)
    g_w_ada, d_w_ada, nm_w_ada, nv_w_ada = _ada_adamw(sc_all.T, d_ada_local, w_ada[0], m_w_ada[0], v_w_ada[0])

    def ordered(w_ada_, w_in_, w_out_, s):
        return (w_ada_[None], s["b_ada"], w_in_[None], s["b_in"], s["w_pool"], s["b_pool"], s["pool_scale"],
                w_out_[None], s["b_out"], s["ln_g"], s["ln_b"])

    return (loss, dx[None],
            *ordered(g_w_ada, g_w_in, g_w_out, g_s),
            *ordered(d_w_ada, d_w_in, d_w_out, d_s),
            *ordered(nm_w_ada, nm_w_in, nm_w_out, nm_s),
            *ordered(nv_w_ada, nv_w_in, nv_w_out, nv_s))
```

```python
import jax
import jax.numpy as jnp
from jax import lax
from jax.experimental import pallas as pl
from jax.experimental.pallas import tpu as pltpu

F32 = jnp.float32
BF16 = jnp.bfloat16

N_DEV = 8
D = 1024
N_HEADS = 8
HEAD_DIM = 64
D_ATT = 512
D_POOL = 512
POOL_WINDOWS = (2, 4, 8, 16)
GROUP_DIM = 128
HALO = 16
LANE = 128
D_IN = 3080
D_ADA = 3072
N_MAIN = 4608
OFF_P, OFF_GA, OFF_GP = 3072, 3584, 4096
COL_CHUNK = 512
Q_SCALE = 0.125
LN_EPS = 1e-5
ALPHA = 2.0 ** 0.25
L_CQ, L_CK, L_LSE = 64, 67, 70

ADAM_LR, ADAM_B1, ADAM_B2, ADAM_EPS, ADAM_WD, ADAM_STEP = 0.001, 0.9, 0.999, 1e-08, 0.01, 10
VMEM_LIMIT = 56 * 1024 * 1024

MESH = pl.DeviceIdType.MESH
ANY = pl.BlockSpec(memory_space=pl.ANY)


def _params(sem=None, vmem=VMEM_LIMIT):
    return pltpu.CompilerParams(dimension_semantics=sem, vmem_limit_bytes=vmem)


def _split3(a):
    hi = a.astype(BF16)
    r = a - hi.astype(F32)
    mid = r.astype(BF16)
    lo = (r - mid.astype(F32)).astype(BF16)
    return hi, mid, lo


def _dot(a, b):
    return jnp.dot(a, b, preferred_element_type=F32)


def _dot_nt(a, b):
    return lax.dot_general(a, b, (((1,), (1,)), ((), ())), preferred_element_type=F32)


def _dot_tn(a, b):
    return lax.dot_general(a, b, (((0,), (0,)), ((), ())), preferred_element_type=F32)


def _dot3(m01, a):
    hi, mid, lo = _split3(a)
    return _dot(m01, hi) + _dot(m01, mid) + _dot(m01, lo)


def _sigmoid(z):
    return 1.0 / (1.0 + jnp.exp(-z))


def _lanes(shape):
    return lax.broadcasted_iota(jnp.int32, shape, len(shape) - 1)


def _place3(lane, base, parts, other):
    out = other
    for j in range(3):
        out = jnp.where(lane == base + j, parts[j], out)
    return out


def _mesh_pos():
    return lax.axis_index("x"), lax.axis_index("y"), lax.axis_index("c")


def _dev_index(px, py, pc):
    return 4 * px + 2 * py + pc


def _all_gather(blocks, name):
    n = len(blocks)

    def body(*refs):
        ins, outs = refs[:n], refs[n:2 * n]
        send_sems, recv_sems, local_sems = refs[2 * n:]
        x, y, c = _mesh_pos()
        me, sibling = (x, y, c), (x, y, 1 - c)
        chips = [(1 - x, y), (x, 1 - y), (1 - x, 1 - y)]

        def copy(a, k, block, to, src=None):
            slot = outs[a].at[_dev_index(*block)]
            return pltpu.make_async_remote_copy(
                src_ref=slot if src is None else src, dst_ref=slot,
                send_sem=send_sems.at[a, k], recv_sem=recv_sems.at[a, k],
                device_id=to, device_id_type=MESH)

        mine = [pltpu.make_async_copy(ins[a], outs[a].at[_dev_index(*me)], local_sems.at[a]) for a in range(n)]
        for cp in mine:
            cp.start()
        first = []
        for a in range(n):
            first.append(copy(a, 0, me, sibling, src=ins[a]))
            first += [copy(a, 1 + j, me, (*chip, c), src=ins[a]) for j, chip in enumerate(chips)]
        for cp in first:
            cp.start()
        passed = []
        for j, chip in enumerate(chips):
            for a in range(n):
                copy(a, 1 + j, (*chip, c), me).wait_recv()
                fwd = copy(a, 4 + j, (*chip, c), sibling)
                fwd.start()
                passed.append(fwd)
        for a in range(n):
            copy(a, 0, sibling, me).wait_recv()
            for j, chip in enumerate(chips):
                copy(a, 4 + j, (*chip, 1 - c), me).wait_recv()
        for cp in first + passed:
            cp.wait_send()
        for cp in mine:
            cp.wait()

    return pl.pallas_call(
        body, name=name,
        out_shape=[jax.ShapeDtypeStruct((N_DEV,) + b.shape, b.dtype) for b in blocks],
        in_specs=[ANY] * n, out_specs=[ANY] * n,
        scratch_shapes=[pltpu.SemaphoreType.DMA((n, 7)), pltpu.SemaphoreType.DMA((n, 7)),
                        pltpu.SemaphoreType.DMA((n,))],
    )(*blocks)


def _exchange(scatter, gather, name):
    ns, n = len(scatter), len(scatter) + len(gather)
    arrays = list(scatter) + list(gather)

    def body(*refs):
        ins, outs = refs[:n], refs[n:2 * n]
        send_sems, recv_sems, local_sems = refs[2 * n:]
        x, y, c = _mesh_pos()
        me = _dev_index(x, y, c)
        peers = []
        for p in range(1, N_DEV):
            px, py, pc = (p >> 2) & 1, (p >> 1) & 1, p & 1
            peers.append((1 - x if px else x, 1 - y if py else y, 1 - c if pc else c))

        def src_for(a, slot):
            return ins[a].at[slot] if a < ns else ins[a]

        def copy(a, k, to):
            return pltpu.make_async_remote_copy(
                src_ref=src_for(a, _dev_index(*to)), dst_ref=outs[a].at[me],
                send_sem=send_sems.at[a, k], recv_sem=recv_sems.at[a, k],
                device_id=to, device_id_type=MESH)

        def arrival(a, k, frm):
            slot = _dev_index(*frm)
            return pltpu.make_async_remote_copy(
                src_ref=src_for(a, slot), dst_ref=outs[a].at[slot],
                send_sem=send_sems.at[a, k], recv_sem=recv_sems.at[a, k],
                device_id=frm, device_id_type=MESH)

        mine = [pltpu.make_async_copy(src_for(a, me), outs[a].at[me], local_sems.at[a]) for a in range(n)]
        for cp in mine:
            cp.start()
        sends = [copy(a, k, to) for k, to in enumerate(peers) for a in range(n)]
        for cp in sends:
            cp.start()
        for k, frm in enumerate(peers):
            for a in range(n):
                arrival(a, k, frm).wait_recv()
        for cp in sends:
            cp.wait_send()
        for cp in mine:
            cp.wait()

    out_shape = [jax.ShapeDtypeStruct(a.shape, a.dtype) for a in scatter]
    out_shape += [jax.ShapeDtypeStruct((N_DEV,) + a.shape, a.dtype) for a in gather]
    return pl.pallas_call(
        body, name=name, out_shape=out_shape,
        in_specs=[ANY] * n, out_specs=[ANY] * n,
        scratch_shapes=[pltpu.SemaphoreType.DMA((n, 7)), pltpu.SemaphoreType.DMA((n, 7)),
                        pltpu.SemaphoreType.DMA((n,))],
    )(*arrays)


def _reduce_grads(gw_in, gw_out, small, d_ada):
    shards = (gw_in, gw_out)

    def body(in0_ref, in1_ref, small_ref, dada_ref,
             g0_ref, g1_ref, total_ref, dall_ref,
             r1_0, r1_1, s2_0, s2_1, r2_0, r2_1, sm_sib, sm_chip, sm_recv, send_sems, recv_sems):
        ins, gs = (in0_ref, in1_ref), (g0_ref, g1_ref)
        r1, s2, r2 = (r1_0, r1_1), (s2_0, s2_1), (r2_0, r2_1)
        x, y, c = _mesh_pos()
        me = _dev_index(x, y, c)
        sibling = (x, y, 1 - c)
        chips = [(x, y), (1 - x, y), (x, 1 - y), (1 - x, 1 - y)]

        def remote(src, dst, k, to):
            return pltpu.make_async_remote_copy(src_ref=src, dst_ref=dst, send_sem=send_sems.at[k],
                                                recv_sem=recv_sems.at[k], device_id=to, device_id_type=MESH)

        sends = []

        def send(cp):
            cp.start()
            sends.append(cp)

        for a in range(2):
            for q, chip in enumerate(chips):
                send(remote(ins[a].at[_dev_index(*chip, 1 - c)], r1[a].at[q], 4 * a + q, sibling))
        send(remote(small_ref, sm_sib, 14, sibling))
        peers = []
        for p in range(1, N_DEV):
            px, py, pc = (p >> 2) & 1, (p >> 1) & 1, p & 1
            peers.append((1 - x if px else x, 1 - y if py else y, 1 - c if pc else c))
        for k, to in enumerate(peers):
            send(remote(dada_ref, dall_ref.at[me], 18 + k, to))
        dall_ref[me] = dada_ref[...]

        for a in range(2):
            own = None
            for q in (1, 2, 3, 0):
                chip = chips[q]
                remote(ins[a].at[0], r1[a].at[q], 4 * a + q, sibling).wait_recv()
                pair = ins[a][_dev_index(*chip, c)].astype(F32) + r1[a][q].astype(F32)
                if q == 0:
                    own = pair
                else:
                    s2[a][q - 1] = pair.astype(BF16)
                    send(remote(s2[a].at[q - 1], r2[a].at[q - 1], 8 + 3 * a + q - 1, (*chip, c)))
            for j in range(3):
                remote(s2[a].at[j], r2[a].at[j], 8 + 3 * a + j, sibling).wait_recv()
                own = own + r2[a][j].astype(F32)
            gs[a][...] = own

        remote(small_ref, sm_sib, 14, sibling).wait_recv()
        sm_chip[...] = small_ref[...] + sm_sib[...]
        for j in range(3):
            send(remote(sm_chip, sm_recv.at[j], 15 + j, (*chips[j + 1], c)))
        for j in range(3):
            remote(sm_chip, sm_recv.at[j], 15 + j, sibling).wait_recv()
        total = None
        for ax in range(2):
            for ay in range(2):
                dx, dy = x != ax, y != ay
                term = jnp.where(dx, jnp.where(dy, sm_recv[2], sm_recv[0]), jnp.where(dy, sm_recv[1], sm_chip[...]))
                total = term if total is None else total + term
        total_ref[...] = total

        for k, frm in enumerate(peers):
            remote(dada_ref, dall_ref.at[_dev_index(*frm)], 18 + k, frm).wait_recv()
        for cp in sends:
            cp.wait_send()

    def like(a, lead, dtype):
        return pltpu.VMEM((lead,) + a.shape[1:], dtype)

    return pl.pallas_call(
        body, name="reduce_grads",
        out_shape=[jax.ShapeDtypeStruct(gw_in.shape[1:], F32), jax.ShapeDtypeStruct(gw_out.shape[1:], F32),
                   jax.ShapeDtypeStruct(small.shape, F32), jax.ShapeDtypeStruct((N_DEV,) + d_ada.shape, F32)],
        scratch_shapes=[like(gw_in, 4, BF16), like(gw_out, 4, BF16), like(gw_in, 3, BF16), like(gw_out, 3, BF16),
                        like(gw_in, 3, BF16), like(gw_out, 3, BF16),
                        pltpu.VMEM(small.shape, F32), pltpu.VMEM(small.shape, F32), pltpu.VMEM((3,) + small.shape, F32),
                        pltpu.SemaphoreType.DMA((25,)), pltpu.SemaphoreType.DMA((25,))],
        compiler_params=_params(),
    )(*shards, small, d_ada)


def _ada_forward(c_all, w_ada):
    def body(c_ref, w_ref, sc_ref, part_ref):
        cc = c_ref[...]
        sc = cc * _sigmoid(cc)
        sc_ref[...] = sc
        part_ref[...] = _dot3_rhs(sc, w_ref[...])

    return pl.pallas_call(
        body, name="ada_forward",
        out_shape=[jax.ShapeDtypeStruct(c_all.shape, F32),
                   jax.ShapeDtypeStruct((N_DEV, w_ada.shape[1]), F32)],
        compiler_params=_params(),
    )(c_all, w_ada)


def _dot3_rhs(a, b):
    a0, a1, a2 = _split3(a)
    b0, b1, b2 = _split3(b)
    return (_dot(a0, b0) + (_dot(a0, b1) + _dot(a1, b0))
            + (_dot(a0, b2) + _dot(a1, b1) + _dot(a2, b0)))


def _inproj_forward(x, mod, w_main, w_f, b_main, b_f, tile):
    seq = x.shape[0]
    nt = seq // tile

    def body(x_ref, mod_ref, w_ref, wf_ref, b_ref, bf_ref,
             qp_ref, kp_ref, vp_ref, f_ref, p_ref, ga_ref, gp_ref, u_ref, carry_ref):
        i = pl.program_id(0)

        @pl.when(i == 0)
        def _():
            carry_ref[...] = jnp.zeros_like(carry_ref)

        u = x_ref[...] * mod_ref[0:1, :] + mod_ref[1:2, :]
        ub = u.astype(BF16)
        u_ref[...] = ub

        f = _dot_nt(ub, wf_ref[...]) + bf_ref[...]
        f_ref[...] = f
        lane = _lanes((tile, LANE))
        log_f = jnp.where(lane < N_HEADS, jnp.minimum(f, 0.0) - jnp.log(1.0 + jnp.exp(-jnp.abs(f))), 0.0)
        row = lax.broadcasted_iota(jnp.int32, (tile, tile), 0)
        col = lax.broadcasted_iota(jnp.int32, (tile, tile), 1)
        tri = (row >= col).astype(BF16)
        cum = _dot3(tri, log_f) + carry_ref[0:1, :]
        carry_ref[0:1, :] = cum[tile - 1:tile, :]
        cq = [part.astype(F32) for part in _split3(cum)]
        ck = [part.astype(F32) for part in _split3(-cum)]

        def proj(chunk):
            cols = pl.ds(chunk * COL_CHUNK, COL_CHUNK)
            return _dot_nt(ub, w_ref[cols, :]) + b_ref[:, cols]

        heads_per_chunk = COL_CHUNK // LANE
        for chunk in range(2):
            r = proj(chunk)
            for j in range(heads_per_chunk):
                h = chunk * heads_per_chunk + j
                extra = jnp.where((lane >= L_CK) & (lane < L_CK + 3), 1.0, 0.0)
                extra = _place3(lane, L_CQ, [part[:, h:h + 1] for part in cq], extra)
                qp_ref[h] = jnp.where(lane < HEAD_DIM, r[:, j * LANE:(j + 1) * LANE] * Q_SCALE, extra).astype(BF16)
        for chunk in range(2):
            r = proj(2 + chunk)
            for j in range(heads_per_chunk):
                h = chunk * heads_per_chunk + j
                ones = ((lane >= L_CQ) & (lane < L_CQ + 3)) | ((lane >= L_LSE) & (lane < L_LSE + 3))
                extra = _place3(lane, L_CK, [part[:, h:h + 1] for part in ck], jnp.where(ones, 1.0, 0.0))
                kp_ref[h] = jnp.where(lane < HEAD_DIM, r[:, j * LANE:(j + 1) * LANE], extra).astype(BF16)
        for chunk in range(2):
            r = proj(4 + chunk)
            for j in range(heads_per_chunk):
                h = chunk * heads_per_chunk + j
                extra = jnp.where((lane >= HEAD_DIM) & (lane < HEAD_DIM + 3), -1.0, 0.0)
                vp_ref[h] = jnp.where(lane < HEAD_DIM, r[:, j * LANE:(j + 1) * LANE], extra).astype(BF16)
        p_ref[...] = proj(6)
        ga_ref[...] = proj(7)
        gp_ref[...] = proj(8)

    head_block = pl.BlockSpec((N_HEADS, tile, LANE), lambda i: (0, i, 0))
    tok = lambda width: pl.BlockSpec((tile, width), lambda i: (i, 0))
    whole = lambda a: pl.BlockSpec(a.shape, lambda i: (0,) * a.ndim)
    padded = jax.ShapeDtypeStruct((N_HEADS, seq, LANE), BF16)
    half = jax.ShapeDtypeStruct((seq, D_ATT), F32)
    return pl.pallas_call(
        body, name="inproj_forward", grid=(nt,),
        in_specs=[tok(D), whole(mod), whole(w_main), whole(w_f), whole(b_main), whole(b_f)],
        out_specs=[head_block, head_block, head_block, tok(LANE), tok(D_POOL), tok(D_ATT), tok(D_POOL),
                   tok(D)],
        out_shape=[padded, padded, padded, jax.ShapeDtypeStruct((seq, LANE), F32), half, half, half,
                   jax.ShapeDtypeStruct((seq, D), BF16)],
        scratch_shapes=[pltpu.VMEM((8, LANE), F32)],
        compiler_params=_params(("arbitrary",)),
    )(x, mod, w_main, w_f, b_main, b_f)


def _attention_forward(qp, kp, vp, tile):
    seq = qp.shape[1]
    nb = seq // tile

    def body(q_ref, k_ref, v_ref, att_ref, q2_ref):
        i = pl.program_id(1)
        lane = _lanes((tile, LANE))
        row = lax.broadcasted_iota(jnp.int32, (tile, tile), 0)
        col = lax.broadcasted_iota(jnp.int32, (tile, tile), 1)
        q = [q_ref[0], q_ref[1]]

        def step(kb, carry, masked):
            rows = pl.ds(pl.multiple_of(kb * tile, tile), tile)
            out = []
            for hh in range(2):
                m, acc = carry[hh]
                s = _dot_nt(q[hh], k_ref[hh, rows, :])
                if masked:
                    s = jnp.where(col <= row, s, -1e30)
                m_new = jnp.maximum(m, jnp.max(s, axis=1, keepdims=True))
                p = jnp.exp(s - m_new).astype(BF16)
                acc = jnp.exp(m - m_new) * acc + _dot(p, v_ref[hh, rows, :])
                out.append((m_new, acc))
            return tuple(out)

        init = (jnp.full((tile, 1), -1e30, F32), jnp.zeros((tile, LANE), F32))
        carry = lax.fori_loop(0, i, lambda kb, cr: step(kb, cr, False), (init, init))
        carry = step(i, carry, True)
        outs = []
        for hh in range(2):
            m, acc = carry[hh]
            l = -acc[:, HEAD_DIM:HEAD_DIM + 1]
            outs.append(acc / l)
            neg_lse = [part.astype(F32) for part in _split3(-(m + jnp.log(l)))]
            q2_ref[hh] = _place3(lane, L_LSE, neg_lse, q[hh].astype(F32)).astype(BF16)
        att_ref[...] = jnp.where(lane < HEAD_DIM, outs[0], pltpu.roll(outs[1], HEAD_DIM, 1))

    pair = pl.BlockSpec((2, tile, LANE), lambda hp, i: (hp, i, 0))
    full = pl.BlockSpec((2, seq, LANE), lambda hp, i: (hp, 0, 0))
    return pl.pallas_call(
        body, name="attention_forward", grid=(N_HEADS // 2, nb),
        in_specs=[pair, full, full],
        out_specs=[pl.BlockSpec((tile, LANE), lambda hp, i: (i, hp)), pair],
        out_shape=[jax.ShapeDtypeStruct((seq, D_ATT), F32),
                   jax.ShapeDtypeStruct((N_HEADS, seq, LANE), BF16)],
        compiler_params=_params(("arbitrary", "arbitrary")),
    )(qp, kp, vp)


def _window_matrices(tile, window, transposed):
    r = lax.broadcasted_iota(jnp.int32, (tile, tile), 0)
    c = lax.broadcasted_iota(jnp.int32, (tile, tile), 1)
    rh = lax.broadcasted_iota(jnp.int32, (tile, HALO), 0)
    ch = lax.broadcasted_iota(jnp.int32, (tile, HALO), 1)
    if not transposed:
        cur = (c <= r) & (r - c < window)
        halo = (rh + HALO - ch) < window
    else:
        cur = (r <= c) & (c - r < window)
        halo = (tile + ch - rh) < window
    return cur.astype(BF16), halo.astype(BF16)


def _silu_parts(g):
    sig = _sigmoid(g)
    return g * sig, sig * (1.0 + g * (1.0 - sig))


def _head_placement():
    r = jnp.arange(D_ATT + 3 * LANE)[:, None]
    n = jnp.arange(N_HEADS * LANE)[None, :]
    h, l = n // LANE, n % LANE
    src = jnp.where(l < HEAD_DIM, HEAD_DIM * h + l, D_ATT + LANE * (l - HEAD_DIM) + h)
    return ((r == src) & (l < HEAD_DIM + 3)).astype(BF16)


def _middle(x, tgt, att, g_att, g_pool, p, vecs, pool_vecs, w_out, w_pool, place, tile):
    seq = x.shape[0]
    nt = seq // tile
    halo_blocks = tile // HALO

    def body(x_ref, tgt_ref, att_ref, ga_ref, gp_ref, p_ref, ph_ref, vec_ref, pvec_ref, wo_ref, wp_ref, place_ref,
             dxa_ref, do2_ref, dga_ref, dgp_ref, dpooled_ref, dwo_ref, dwp_ref, dvec_ref, dpvec_ref):
        i = pl.program_id(0)

        @pl.when(i == 0)
        def _():
            dwo_ref[...] = jnp.zeros_like(dwo_ref)
            dwp_ref[...] = jnp.zeros_like(dwp_ref)
            dvec_ref[...] = jnp.zeros_like(dvec_ref)
            dpvec_ref[...] = jnp.zeros_like(dpvec_ref)

        gate, b_out, ln_g, ln_b = (vec_ref[k:k + 1, :] for k in range(4))
        b_pool, pool_scale = pvec_ref[0:1, :], pvec_ref[1:2, :]
        x = x_ref[...]
        p = p_ref[...]
        p_halo = ph_ref[...] * jnp.where(i > 0, 1.0, 0.0)
        pos = i * tile + lax.broadcasted_iota(jnp.int32, (tile, 1), 0) + 1

        pooled, mixed = [], []
        for g, window in enumerate(POOL_WINDOWS):
            cols = slice(g * GROUP_DIM, (g + 1) * GROUP_DIM)
            m_cur, m_halo = _window_matrices(tile, window, False)
            wsum = _dot3(m_cur, p[:, cols]) + _dot3(m_halo, p_halo[:, cols])
            count = jnp.minimum(pos, window).astype(F32)
            pooled.append(wsum / count - p[:, cols])
            mixed.append(_dot(pooled[g].astype(BF16), wp_ref[g]) + b_pool[:, cols])
        mixed = jnp.concatenate(mixed, axis=1)
        pool = mixed * pool_scale

        att = att_ref[...]
        g_att, g_pool = ga_ref[...], gp_ref[...]
        silu_a, dsilu_a = _silu_parts(g_att)
        silu_p, dsilu_p = _silu_parts(g_pool)
        y_in = jnp.concatenate([att * silu_a, pool * silu_p], axis=1)
        y = _dot(y_in.astype(BF16), wo_ref[...]) + b_out
        h = ALPHA * x + gate * y
        mu = jnp.mean(h, axis=1, keepdims=True)
        hc = h - mu
        var = jnp.mean(hc * hc, axis=1, keepdims=True)
        rstd = lax.rsqrt(var + LN_EPS)
        yhat = hc * rstd
        diff = yhat * ln_g + ln_b - tgt_ref[...]
        loss_rows = jnp.sum(diff * diff, axis=1, keepdims=True)
        d_out = diff * (1.0 / D)

        d_yhat = d_out * ln_g
        dh = rstd * (d_yhat - jnp.mean(d_yhat, axis=1, keepdims=True)
                     - yhat * jnp.mean(d_yhat * yhat, axis=1, keepdims=True))
        dxa_ref[...] = ALPHA * dh
        dy = dh * gate
        dyb = dy.astype(BF16)
        lane = _lanes((1, D))
        loss_row = jnp.where(lane == 0, (0.5 / D) * jnp.sum(loss_rows, axis=0, keepdims=True), 0.0)
        dvec_ref[0:1, :] += jnp.sum(dh * y, axis=0, keepdims=True)
        dvec_ref[1:2, :] += jnp.sum(dy, axis=0, keepdims=True)
        dvec_ref[2:3, :] += jnp.sum(d_out * yhat, axis=0, keepdims=True)
        dvec_ref[3:4, :] += jnp.sum(d_out, axis=0, keepdims=True)
        dvec_ref[4:5, :] += loss_row

        dwo_ref[...] += _dot(y_in.T.astype(BF16), dyb)
        d_yin = _dot_nt(dyb, wo_ref[...])
        d_a, d_pl = d_yin[:, :D_ATT], d_yin[:, D_ATT:]
        d_att = d_a * silu_a
        r = lax.broadcasted_iota(jnp.int32, (D_ATT, LANE), 0)
        c = lax.broadcasted_iota(jnp.int32, (D_ATT, LANE), 1)
        head_of = ((r >= HEAD_DIM * c) & (r < HEAD_DIM * c + HEAD_DIM)).astype(BF16)
        delta = sum(_dot(part, head_of) for part in _split3(d_att * att))
        do2 = _dot(jnp.concatenate([d_att.astype(BF16), *_split3(delta)], axis=1), place_ref[...])
        for h in range(N_HEADS):
            do2_ref[h] = do2[:, h * LANE:(h + 1) * LANE].astype(BF16)
        dga_ref[...] = d_a * att * dsilu_a
        dgp_ref[...] = d_pl * pool * dsilu_p
        d_pool = d_pl * silu_p
        d_mixed = d_pool * pool_scale
        dpvec_ref[0:1, :] += jnp.sum(d_mixed, axis=0, keepdims=True)
        dpvec_ref[1:2, :] += jnp.sum(d_pool * mixed, axis=0, keepdims=True)
        d_pooled = []
        for g in range(len(POOL_WINDOWS)):
            cols = slice(g * GROUP_DIM, (g + 1) * GROUP_DIM)
            dmb = d_mixed[:, cols].astype(BF16)
            dwp_ref[g] += _dot(pooled[g].T.astype(BF16), dmb)
            d_pooled.append(_dot_nt(dmb, wp_ref[g]))
        dpooled_ref[...] = jnp.concatenate(d_pooled, axis=1)

    tok = lambda width: pl.BlockSpec((tile, width), lambda i: (i, 0))
    whole = lambda a: pl.BlockSpec(a.shape, lambda i: (0,) * a.ndim)
    halo = pl.BlockSpec((HALO, D_POOL), lambda i: (jnp.maximum(i * halo_blocks - 1, 0), 0))
    half = jax.ShapeDtypeStruct((seq, D_ATT), F32)
    outs = [jax.ShapeDtypeStruct((seq, D), F32), jax.ShapeDtypeStruct((N_HEADS, seq, LANE), BF16), half, half, half,
            jax.ShapeDtypeStruct(w_out.shape, F32), jax.ShapeDtypeStruct(w_pool.shape, F32),
            jax.ShapeDtypeStruct(vecs.shape, F32), jax.ShapeDtypeStruct(pool_vecs.shape, F32)]
    return pl.pallas_call(
        body, name="middle", grid=(nt,),
        in_specs=[tok(D), tok(D), tok(D_ATT), tok(D_ATT), tok(D_POOL), tok(D_POOL), halo,
                  whole(vecs), whole(pool_vecs), whole(w_out), whole(w_pool), whole(place)],
        out_specs=[tok(D), pl.BlockSpec((N_HEADS, tile, LANE), lambda i: (0, i, 0)),
                   tok(D_ATT), tok(D_POOL), tok(D_POOL),
                   whole(w_out), whole(w_pool), whole(vecs), whole(pool_vecs)],
        out_shape=outs,
        compiler_params=_params(("arbitrary",)),
    )(x, tgt, att, g_att, g_pool, p, p, vecs, pool_vecs, w_out, w_pool, place)


def _attention_backward(q2, kp, vp, do2, tile):
    seq = q2.shape[1]
    nb = seq // tile

    def body(q_ref, k_ref, v_ref, do_ref, dq_ref, dk_ref, dv_ref, dcum_ref):
        hp = pl.program_id(0)
        lane = _lanes((tile, LANE))
        row = lax.broadcasted_iota(jnp.int32, (tile, tile), 0)
        col = lax.broadcasted_iota(jnp.int32, (tile, tile), 1)
        dq_ref[...] = jnp.zeros_like(dq_ref)

        @pl.when(hp == 0)
        def _():
            dcum_ref[...] = jnp.zeros_like(dcum_ref)

        def kv_block(kb, _):
            krows = pl.ds(pl.multiple_of(kb * tile, tile), tile)
            k = [k_ref[hh, krows, :] for hh in range(2)]
            v = [v_ref[hh, krows, :] for hh in range(2)]

            def q_block(qb, carry, masked):
                qrows = pl.ds(pl.multiple_of(qb * tile, tile), tile)
                out = []
                for hh in range(2):
                    dk, dv = carry[hh]
                    q = q_ref[hh, qrows, :]
                    do = do_ref[hh, qrows, :]
                    s_t = _dot_nt(k[hh], q)
                    if masked:
                        s_t = jnp.where(row <= col, s_t, -1e30)
                    p_t = jnp.exp(s_t)
                    ds_t = (p_t * _dot_nt(v[hh], do)).astype(BF16)
                    dv = dv + _dot(p_t.astype(BF16), do)
                    dk = dk + _dot(ds_t, q)
                    dq_ref[hh, qrows, :] += _dot_tn(ds_t, k[hh])
                    out.append((dk, dv))
                return tuple(out)

            zero = jnp.zeros((tile, LANE), F32)
            carry = q_block(kb, ((zero, zero), (zero, zero)), True)
            carry = lax.fori_loop(kb + 1, nb, lambda qb, cr: q_block(qb, cr, False), carry)
            for hh in range(2):
                dk = carry[hh][0]
                dk_ref[hh, krows, :] = dk.astype(BF16)
                dv_ref[hh, krows, :] = carry[hh][1].astype(BF16)
                dcum_ref[krows, :] += jnp.where(lane == 2 * hp + hh, -dk[:, L_CK:L_CK + 1], 0.0)
            return 0

        lax.fori_loop(0, nb, kv_block, 0)

        def row_sums(qb, _):
            qrows = pl.ds(pl.multiple_of(qb * tile, tile), tile)
            for hh in range(2):
                dcum_ref[qrows, :] += jnp.where(lane == 2 * hp + hh, dq_ref[hh, qrows, L_CQ:L_CQ + 1], 0.0)
            return 0

        lax.fori_loop(0, nb, row_sums, 0)

    pair = pl.BlockSpec((2, seq, LANE), lambda hp: (hp, 0, 0))
    return pl.pallas_call(
        body, name="attention_backward", grid=(N_HEADS // 2,),
        in_specs=[pair, pair, pair, pair],
        out_specs=[pair, pair, pair, pl.BlockSpec((seq, LANE), lambda hp: (0, 0))],
        out_shape=[jax.ShapeDtypeStruct((N_HEADS, seq, LANE), F32),
                   jax.ShapeDtypeStruct((N_HEADS, seq, LANE), BF16),
                   jax.ShapeDtypeStruct((N_HEADS, seq, LANE), BF16),
                   jax.ShapeDtypeStruct((seq, LANE), F32)],
        compiler_params=_params(("arbitrary",)),
    )(q2, kp, vp, do2)


def _inproj_backward(dqp, dkp, dvp, d_cum, f, d_pooled, d_ga, d_gp, x, dxa, u, mod, w_main, w_f, tile):
    seq = x.shape[0]
    nt = seq // tile
    halo_blocks = tile // HALO

    def body(dq_ref, dk_ref, dv_ref, dcum_ref, f_ref, dpo_ref, dph_ref, dga_ref, dgp_ref, x_ref, dxa_ref, u_ref,
             mod_ref, w_ref, wf_ref,
             dx_ref, dproj_ref, dwf_ref, db_ref, dbf_ref, dmod_ref, carry_ref):
        step = pl.program_id(0)
        i = nt - 1 - step

        @pl.when(step == 0)
        def _():
            carry_ref[...] = jnp.zeros_like(carry_ref)
            dwf_ref[...] = jnp.zeros_like(dwf_ref)
            db_ref[...] = jnp.zeros_like(db_ref)
            dbf_ref[...] = jnp.zeros_like(dbf_ref)
            dmod_ref[...] = jnp.zeros_like(dmod_ref)

        lane = _lanes((tile, LANE))
        head_lanes = lane < HEAD_DIM

        def emit(chunk, val):
            cols = pl.ds(chunk * COL_CHUNK, COL_CHUNK)
            db_ref[0:1, cols] += jnp.sum(val, axis=0, keepdims=True)
            vb = val.astype(BF16)
            dproj_ref[:, cols] = vb
            return _dot(vb, w_ref[cols, :])

        heads_per_chunk = COL_CHUNK // LANE
        d_u = jnp.zeros((tile, D), F32)
        for chunk in range(2):
            parts = [jnp.where(head_lanes, dq_ref[chunk * heads_per_chunk + j] * Q_SCALE, 0.0)
                     for j in range(heads_per_chunk)]
            d_u += emit(chunk, jnp.concatenate(parts, axis=1))
        for chunk in range(2):
            parts = [jnp.where(head_lanes, dk_ref[chunk * heads_per_chunk + j].astype(F32), 0.0)
                     for j in range(heads_per_chunk)]
            d_u += emit(2 + chunk, jnp.concatenate(parts, axis=1))
        for chunk in range(2):
            parts = [jnp.where(head_lanes, dv_ref[chunk * heads_per_chunk + j].astype(F32), 0.0)
                     for j in range(heads_per_chunk)]
            d_u += emit(4 + chunk, jnp.concatenate(parts, axis=1))

        d_pooled = dpo_ref[...]
        d_halo = dph_ref[...] * jnp.where(i < nt - 1, 1.0, 0.0)
        pos = i * tile + lax.broadcasted_iota(jnp.int32, (tile, 1), 0) + 1
        d_p = []
        for g, window in enumerate(POOL_WINDOWS):
            cols = slice(g * GROUP_DIM, (g + 1) * GROUP_DIM)
            m_cur, m_halo = _window_matrices(tile, window, True)
            scaled = d_pooled[:, cols] / jnp.minimum(pos, window).astype(F32)
            d_p.append(_dot3(m_cur, scaled) + _dot3(m_halo, d_halo[:, cols] * (1.0 / window)) - d_pooled[:, cols])
        d_u += emit(6, jnp.concatenate(d_p, axis=1))
        d_u += emit(7, dga_ref[...])
        d_u += emit(8, dgp_ref[...])

        row = lax.broadcasted_iota(jnp.int32, (tile, tile), 0)
        col = lax.broadcasted_iota(jnp.int32, (tile, tile), 1)
        d_logf = _dot3((row <= col).astype(BF16), dcum_ref[...]) + carry_ref[0:1, :]
        carry_ref[0:1, :] = d_logf[0:1, :]
        d_f = jnp.where(lane < N_HEADS, d_logf * _sigmoid(-f_ref[...]), 0.0)
        dbf_ref[0:1, :] += jnp.sum(d_f, axis=0, keepdims=True)
        d_fb = d_f.astype(BF16)
        d_u += _dot(d_fb, wf_ref[...])
        dwf_ref[...] += _dot_tn(d_fb, u_ref[...])

        x = x_ref[...]
        dx_ref[...] = dxa_ref[...] + d_u * mod_ref[0:1, :]
        dmod_ref[0:1, :] += jnp.sum(d_u * x, axis=0, keepdims=True)
        dmod_ref[1:2, :] += jnp.sum(d_u, axis=0, keepdims=True)

    rev = lambda step: nt - 1 - step
    tok = lambda width: pl.BlockSpec((tile, width), lambda s: (rev(s), 0))
    head_block = pl.BlockSpec((N_HEADS, tile, LANE), lambda s: (0, rev(s), 0))
    whole = lambda a: pl.BlockSpec(a.shape, lambda s: (0,) * a.ndim)
    halo = pl.BlockSpec((HALO, D_POOL), lambda s: (jnp.minimum((rev(s) + 1) * halo_blocks, seq // HALO - 1), 0))
    small = lambda width: jax.ShapeDtypeStruct((8, width), F32)
    return pl.pallas_call(
        body, name="inproj_backward", grid=(nt,),
        in_specs=[head_block, head_block, head_block, tok(LANE), tok(LANE), tok(D_POOL), halo, tok(D_ATT), tok(D_POOL),
                  tok(D), tok(D), tok(D),
                  whole(mod), whole(w_main), whole(w_f)],
        out_specs=[tok(D), tok(N_MAIN), pl.BlockSpec((LANE, D), lambda s: (0, 0)),
                   pl.BlockSpec((8, N_MAIN), lambda s: (0, 0)), pl.BlockSpec((8, LANE), lambda s: (0, 0)),
                   pl.BlockSpec((8, D), lambda s: (0, 0))],
        out_shape=[jax.ShapeDtypeStruct((seq, D), F32), jax.ShapeDtypeStruct((seq, N_MAIN), BF16),
                   jax.ShapeDtypeStruct((LANE, D), F32), small(N_MAIN), small(LANE), small(D)],
        scratch_shapes=[pltpu.VMEM((8, LANE), F32)],
        compiler_params=_params(("arbitrary",)),
    )(dqp, dkp, dvp, d_cum, f, d_pooled, d_pooled, d_ga, d_gp, x, dxa, u, mod, w_main, w_f)


def _weight_grad(dproj, u, k_tile):
    seq = u.shape[0]
    nk = seq // k_tile

    def body(dp_ref, u_ref, out_ref):
        @pl.when(pl.program_id(1) == 0)
        def _():
            out_ref[...] = jnp.zeros_like(out_ref)

        out_ref[...] += _dot_tn(dp_ref[...], u_ref[...])

    return pl.pallas_call(
        body, name="weight_grad", grid=(N_MAIN // COL_CHUNK, nk),
        in_specs=[pl.BlockSpec((k_tile, COL_CHUNK), lambda n, k: (k, n)),
                  pl.BlockSpec((k_tile, D), lambda n, k: (k, 0))],
        out_specs=pl.BlockSpec((COL_CHUNK, D), lambda n, k: (n, 0)),
        out_shape=jax.ShapeDtypeStruct((N_MAIN, D), F32),
        compiler_params=_params(("arbitrary", "arbitrary")),
    )(dproj, u)


def _adamw(w, g, m, v):
    m = ADAM_B1 * m + (1.0 - ADAM_B1) * g
    v = ADAM_B2 * v + (1.0 - ADAM_B2) * (g * g)
    m_hat = m / (1.0 - ADAM_B1 ** ADAM_STEP)
    v_hat = v / (1.0 - ADAM_B2 ** ADAM_STEP)
    delta = -ADAM_LR * (m_hat / (jnp.sqrt(v_hat) + ADAM_EPS) + ADAM_WD * w)
    return delta, m, v


def _adamw_call(g, w, m, v, lead_tile, name):
    nr = w.shape[0] // lead_tile

    def body(gi_ref, w_ref, m_ref, v_ref, g_ref, d_ref, nm_ref, nv_ref):
        g = gi_ref[...]
        g_ref[...] = g
        d_ref[...], nm_ref[...], nv_ref[...] = _adamw(w_ref[...], g, m_ref[...], v_ref[...])

    blk = pl.BlockSpec((lead_tile,) + w.shape[1:], lambda r: (r,) + (0,) * (w.ndim - 1))
    shape = jax.ShapeDtypeStruct(w.shape, F32)
    return pl.pallas_call(
        body, name=name, grid=(nr,),
        in_specs=[blk, blk, blk, blk], out_specs=[blk, blk, blk, blk],
        out_shape=[shape, shape, shape, shape],
        compiler_params=_params(("arbitrary",)),
    )(g, w, m, v)


def _sum_adamw(parts, w, m, v, row_tile, name):
    rows, cols = w.shape
    nr = rows // row_tile

    def body(parts_ref, w_ref, m_ref, v_ref, g_ref, d_ref, nm_ref, nv_ref):
        g = parts_ref[0]
        for k in range(1, N_DEV):
            g = g + parts_ref[k]
        g_ref[...] = g
        d_ref[...], nm_ref[...], nv_ref[...] = _adamw(w_ref[...], g, m_ref[...], v_ref[...])

    blk = pl.BlockSpec((row_tile, cols), lambda r: (r, 0))
    shape = jax.ShapeDtypeStruct(w.shape, F32)
    return pl.pallas_call(
        body, name=name, grid=(nr,),
        in_specs=[pl.BlockSpec((N_DEV, row_tile, cols), lambda r: (0, r, 0)), blk, blk, blk],
        out_specs=[blk, blk, blk, blk],
        out_shape=[shape, shape, shape, shape],
        compiler_params=_params(("arbitrary",)),
    )(parts, w, m, v)


def _ada_adamw(sc_t, d_ada, w, m, v):
    def body(sc_ref, d_ref, w_ref, m_ref, v_ref, g_ref, dl_ref, nm_ref, nv_ref):
        g = sc_ref[:, 0:1] * d_ref[0:1, :]
        for b in range(1, N_DEV):
            g = g + sc_ref[:, b:b + 1] * d_ref[b:b + 1, :]
        g_ref[...] = g
        dl_ref[...], nm_ref[...], nv_ref[...] = _adamw(w_ref[...], g, m_ref[...], v_ref[...])

    shape = jax.ShapeDtypeStruct(w.shape, F32)
    return pl.pallas_call(
        body, name="ada_adamw", out_shape=[shape, shape, shape, shape], compiler_params=_params(),
    )(sc_t, d_ada, w, m, v)


def _pad_heads(a):
    lead = a.shape[:-1]
    a = a.reshape(lead + (N_HEADS, HEAD_DIM))
    a = jnp.pad(a, [(0, 0)] * len(lead) + [(0, 0), (0, LANE - HEAD_DIM)])
    return a.reshape(lead + (N_HEADS * LANE,))


def _unpad_heads(a):
    lead = a.shape[:-1]
    return a.reshape(lead + (N_HEADS, LANE))[..., :HEAD_DIM].reshape(lead + (N_HEADS * HEAD_DIM,))


def _to_main(a):
    q, k, v = a[..., 0:512], a[..., 512:1024], a[..., 1024:1536]
    f = a[..., 1536:1544]
    rest = a[..., 1544:]
    main = jnp.concatenate([_pad_heads(q), _pad_heads(k), _pad_heads(v), rest], axis=-1)
    f = jnp.pad(f, [(0, 0)] * (a.ndim - 1) + [(0, LANE - N_HEADS)])
    return main, f


def _from_main(main, f):
    q, k, v = (_unpad_heads(main[..., j * 1024:(j + 1) * 1024]) for j in range(3))
    return jnp.concatenate([q, k, v, f[..., :N_HEADS], main[..., 3072:]], axis=-1)


def _rows_to_main(a):
    def pad_heads(b):
        return jnp.pad(b.reshape(N_HEADS, HEAD_DIM, -1), ((0, 0), (0, LANE - HEAD_DIM), (0, 0))).reshape(N_HEADS * LANE, -1)

    main = jnp.concatenate([pad_heads(a[0:512]), pad_heads(a[512:1024]), pad_heads(a[1024:1536]), a[1544:]], axis=0)
    return main, jnp.pad(a[1536:1544], ((0, LANE - N_HEADS), (0, 0)))


def _rows_from_main(main, f):
    def unpad_heads(b):
        return b.reshape(N_HEADS, LANE, -1)[:, :HEAD_DIM].reshape(N_HEADS * HEAD_DIM, -1)

    q, k, v = (unpad_heads(main[j * 1024:(j + 1) * 1024]) for j in range(3))
    return jnp.concatenate([q, k, v, f[:N_HEADS], main[3072:]], axis=0)


_SMALL = (("b_in", 3200), ("w_pool", 65536), ("b_pool", 512), ("pool_scale", 512),
          ("b_out", 1024), ("ln_g", 1024), ("ln_b", 1024), ("loss", 128))
_SMALL_ROWS = 576


def _pack_small(parts):
    flat = []
    for name, size in _SMALL:
        a = parts[name].reshape(-1)
        flat.append(jnp.pad(a, (0, size - a.shape[0])))
    flat = jnp.concatenate(flat)
    flat = jnp.pad(flat, (0, _SMALL_ROWS * LANE - flat.shape[0]))
    return flat.reshape(_SMALL_ROWS, LANE)


def _unpack_small(packed, shapes):
    flat = packed.reshape(-1)
    out, off = {}, 0
    for name, size in _SMALL:
        n = 1
        for s in shapes[name]:
            n *= s
        out[name] = flat[off:off + n].reshape(shapes[name])
        off += size
    return out


def kernel(x, c, w_ada, b_ada, w_in, b_in, w_pool_mix, b_pool_mix, pool_scale, w_out, b_out, ln_g, ln_b, loss_target, m_w_ada, m_b_ada, m_w_in, m_b_in, m_w_pool_mix, m_b_pool_mix, m_pool_scale, m_w_out, m_b_out, m_ln_g, m_ln_b, v_w_ada, v_b_ada, v_w_in, v_b_in, v_w_pool_mix, v_b_pool_mix, v_pool_scale, v_w_out, v_b_out, v_ln_g, v_ln_b):
    seq = x.shape[1]
    tile = min(256, seq)
    attn_tile = min(512, max(128, seq // 4))
    me = _dev_index(*_mesh_pos())
    x2, tgt = x[0], loss_target[0]

    rows_in = D_IN // N_DEV
    c_all, w_in_g, w_out_g = _all_gather(
        [jnp.pad(c, ((0, 7), (0, 0))), w_in[0].T.astype(BF16), w_out[0].astype(BF16)], "gather_weights")
    sc_all, ada_part = _ada_forward(c_all[:, 0, :], w_ada[0])
    (ada_mine,) = _exchange([ada_part.reshape(N_DEV, 1, -1)], [], "exchange_ada")
    ada = ada_mine.reshape(1, D_ADA) + b_ada
    shift, scale, gate = ada[:, 0:D], ada[:, D:2 * D], ada[:, 2 * D:]
    mod = jnp.concatenate([1.0 + scale, shift, jnp.zeros((6, D), F32)], axis=0)

    w_main, w_f = _rows_to_main(w_in_g.reshape(D_IN, D))
    b_main, b_f = _to_main(b_in)
    w_out_full = w_out_g.reshape(D, D)

    qp, kp, vp, f, p, g_att, g_pool, u = _inproj_forward(x2, mod, w_main, w_f, b_main, b_f, tile)
    att, q2 = _attention_forward(qp, kp, vp, attn_tile)

    vecs = jnp.concatenate([gate, b_out, ln_g, ln_b, jnp.zeros((4, D), F32)], axis=0)
    pool_vecs = jnp.concatenate([b_pool_mix.reshape(1, D_POOL), pool_scale, jnp.zeros((6, D_POOL), F32)], axis=0)
    dxa, do2, d_ga, d_gp, d_pooled, dw_out, dw_pool, dvec, dpvec = _middle(
        x2, tgt, att, g_att, g_pool, p, vecs, pool_vecs, w_out_full, w_pool_mix[0].astype(BF16),
        _head_placement(), tile)

    dqp, dkp, dvp, d_cum = _attention_backward(q2, kp, vp, do2, attn_tile)
    dx, dproj, dw_f, db_main, db_f, dmod = _inproj_backward(
        dqp, dkp, dvp, d_cum, f, d_pooled, d_ga, d_gp, x2, dxa, u, mod, w_main, w_f, tile)
    dw_main = _weight_grad(dproj, u, min(1024, seq))

    gw_in = _rows_from_main(dw_main, dw_f).reshape(N_DEV, rows_in, D).astype(BF16)
    gw_out = dw_out.reshape(N_DEV, D // N_DEV, D).astype(BF16)
    d_ada = jnp.concatenate([dmod[1:2], dmod[0:1], dvec[0:1]], axis=1)
    small = _pack_small({
        "b_in": _from_main(db_main[0:1], db_f[0:1]), "w_pool": dw_pool, "b_pool": dpvec[0:1],
        "pool_scale": dpvec[1:2], "b_out": dvec[1:2], "ln_g": dvec[2:3], "ln_b": dvec[3:4], "loss": dvec[4:5, 0:LANE]})
    g_in_rows, g_out, small_sum, d_ada_all = _reduce_grads(gw_in, gw_out, small, d_ada.reshape(D_ADA // LANE, LANE))

    def rows3(a):
        return a[0].T.reshape(rows_in, D // LANE, LANE)

    outs_in = _adamw_call(g_in_rows.reshape(rows_in, D // LANE, LANE), rows3(w_in), rows3(m_w_in), rows3(v_w_in),
                          rows_in // 5, "adamw_w_in")
    g_w_in, d_w_in, nm_w_in, nv_w_in = (a.reshape(rows_in, D).T for a in outs_in)
    g_w_out, d_w_out, nm_w_out, nv_w_out = _adamw_call(g_out, w_out[0], m_w_out[0], v_w_out[0], D // N_DEV, "adamw_w_out")

    zero = jnp.zeros((1,), F32)
    weights = {"b_in": b_in, "w_pool": w_pool_mix, "b_pool": b_pool_mix, "pool_scale": pool_scale,
               "b_out": b_out, "ln_g": ln_g, "ln_b": ln_b, "loss": zero}
    first = {"b_in": m_b_in, "w_pool": m_w_pool_mix, "b_pool": m_b_pool_mix, "pool_scale": m_pool_scale,
             "b_out": m_b_out, "ln_g": m_ln_g, "ln_b": m_ln_b, "loss": zero}
    second = {"b_in": v_b_in, "w_pool": v_w_pool_mix, "b_pool": v_b_pool_mix, "pool_scale": v_pool_scale,
              "b_out": v_b_out, "ln_g": v_ln_g, "ln_b": v_ln_b, "loss": zero}
    packed = _adamw_call(small_sum, _pack_small(weights), _pack_small(first), _pack_small(second),
                         _SMALL_ROWS, "adamw_small")
    shapes = {k: a.shape for k, a in weights.items()}
    g_s, d_s, nm_s, nv_s = (_unpack_small(a, shapes) for a in packed)
    loss = g_s["loss"][0]

    ada_rows = D_ADA // LANE
    b_ada_outs = _sum_adamw(d_ada_all, b_ada.reshape(ada_rows, LANE), m_b_ada.reshape(ada_rows, LANE),
                            v_b_ada.reshape(ada_rows, LANE), ada_rows, "adamw_b_ada")
    g_b_ada, d_b_ada, nm_b_ada, nv_b_ada = (a.reshape(1, D_ADA) for a in b_ada_outs)
    d_ada_local = lax.dynamic_slice_in_dim(d_ada_all.reshape(N_DEV, D_ADA), me * (D_ADA // N_DEV), D_ADA // N_DEV, axis=1)
    g_w_ada, d_w_ada, nm_w_ada, nv_w_ada = _ada_adamw(sc_all.T, d_ada_local, w_ada[0], m_w_ada[0], v_w_ada[0])

    def ordered(w_ada_, b_ada_, w_in_, w_out_, s):
        return (w_ada_[None], b_ada_, w_in_[None], s["b_in"], s["w_pool"], s["b_pool"], s["pool_scale"],
                w_out_[None], s["b_out"], s["ln_g"], s["ln_b"])

    return (loss, dx[None],
            *ordered(g_w_ada, g_b_ada, g_w_in, g_w_out, g_s),
            *ordered(d_w_ada, d_b_ada, d_w_in, d_w_out, d_s),
            *ordered(nm_w_ada, nm_b_ada, nm_w_in, nm_w_out, nm_s),
            *ordered(nv_w_ada, nv_b_ada, nv_w_in, nv_w_out, nv_s))
```

```python
import jax
import jax.numpy as jnp
from jax import lax
from jax.experimental import pallas as pl
from jax.experimental.pallas import tpu as pltpu

F32 = jnp.float32
BF16 = jnp.bfloat16

N_DEV = 8
D = 1024
N_HEADS = 8
HEAD_DIM = 64
D_ATT = 512
D_POOL = 512
POOL_WINDOWS = (2, 4, 8, 16)
GROUP_DIM = 128
HALO = 16
LANE = 128
D_IN = 3080
D_ADA = 3072
N_MAIN = 4608
OFF_P, OFF_GA, OFF_GP = 3072, 3584, 4096
COL_CHUNK = 512
Q_SCALE = 0.125
LN_EPS = 1e-5
ALPHA = 2.0 ** 0.25
L_CQ, L_CK, L_LSE = 64, 67, 70

ADAM_LR, ADAM_B1, ADAM_B2, ADAM_EPS, ADAM_WD, ADAM_STEP = 0.001, 0.9, 0.999, 1e-08, 0.01, 10
VMEM_LIMIT = 56 * 1024 * 1024

MESH = pl.DeviceIdType.MESH
ANY = pl.BlockSpec(memory_space=pl.ANY)


def _params(sem=None, vmem=VMEM_LIMIT):
    return pltpu.CompilerParams(dimension_semantics=sem, vmem_limit_bytes=vmem)


def _split3(a):
    hi = a.astype(BF16)
    r = a - hi.astype(F32)
    mid = r.astype(BF16)
    lo = (r - mid.astype(F32)).astype(BF16)
    return hi, mid, lo


def _dot(a, b):
    return jnp.dot(a, b, preferred_element_type=F32)


def _dot_nt(a, b):
    return lax.dot_general(a, b, (((1,), (1,)), ((), ())), preferred_element_type=F32)


def _dot_tn(a, b):
    return lax.dot_general(a, b, (((0,), (0,)), ((), ())), preferred_element_type=F32)


def _dot3(m01, a):
    hi, mid, lo = _split3(a)
    return _dot(m01, hi) + _dot(m01, mid) + _dot(m01, lo)


def _sigmoid(z):
    return 1.0 / (1.0 + jnp.exp(-z))


def _lanes(shape):
    return lax.broadcasted_iota(jnp.int32, shape, len(shape) - 1)


def _place3(lane, base, parts, other):
    out = other
    for j in range(3):
        out = jnp.where(lane == base + j, parts[j], out)
    return out


def _mesh_pos():
    return lax.axis_index("x"), lax.axis_index("y"), lax.axis_index("c")


def _dev_index(px, py, pc):
    return 4 * px + 2 * py + pc


def _all_gather(blocks, name):
    n = len(blocks)

    def body(*refs):
        ins, outs = refs[:n], refs[n:2 * n]
        send_sems, recv_sems, local_sems = refs[2 * n:]
        x, y, c = _mesh_pos()
        me, sibling = (x, y, c), (x, y, 1 - c)
        chips = [(1 - x, y), (x, 1 - y), (1 - x, 1 - y)]

        def copy(a, k, block, to, src=None):
            slot = outs[a].at[_dev_index(*block)]
            return pltpu.make_async_remote_copy(
                src_ref=slot if src is None else src, dst_ref=slot,
                send_sem=send_sems.at[a, k], recv_sem=recv_sems.at[a, k],
                device_id=to, device_id_type=MESH)

        mine = [pltpu.make_async_copy(ins[a], outs[a].at[_dev_index(*me)], local_sems.at[a]) for a in range(n)]
        for cp in mine:
            cp.start()
        first = []
        for a in range(n):
            first.append(copy(a, 0, me, sibling, src=ins[a]))
            first += [copy(a, 1 + j, me, (*chip, c), src=ins[a]) for j, chip in enumerate(chips)]
        for cp in first:
            cp.start()
        passed = []
        for j, chip in enumerate(chips):
            for a in range(n):
                copy(a, 1 + j, (*chip, c), me).wait_recv()
                fwd = copy(a, 4 + j, (*chip, c), sibling)
                fwd.start()
                passed.append(fwd)
        for a in range(n):
            copy(a, 0, sibling, me).wait_recv()
            for j, chip in enumerate(chips):
                copy(a, 4 + j, (*chip, 1 - c), me).wait_recv()
        for cp in first + passed:
            cp.wait_send()
        for cp in mine:
            cp.wait()

    return pl.pallas_call(
        body, name=name,
        out_shape=[jax.ShapeDtypeStruct((N_DEV,) + b.shape, b.dtype) for b in blocks],
        in_specs=[ANY] * n, out_specs=[ANY] * n,
        scratch_shapes=[pltpu.SemaphoreType.DMA((n, 7)), pltpu.SemaphoreType.DMA((n, 7)),
                        pltpu.SemaphoreType.DMA((n,))],
    )(*blocks)


def _exchange(scatter, gather, name):
    ns, n = len(scatter), len(scatter) + len(gather)
    arrays = list(scatter) + list(gather)

    def body(*refs):
        ins, outs = refs[:n], refs[n:2 * n]
        send_sems, recv_sems, local_sems = refs[2 * n:]
        x, y, c = _mesh_pos()
        me = _dev_index(x, y, c)
        peers = []
        for p in range(1, N_DEV):
            px, py, pc = (p >> 2) & 1, (p >> 1) & 1, p & 1
            peers.append((1 - x if px else x, 1 - y if py else y, 1 - c if pc else c))

        def src_for(a, slot):
            return ins[a].at[slot] if a < ns else ins[a]

        def copy(a, k, to):
            return pltpu.make_async_remote_copy(
                src_ref=src_for(a, _dev_index(*to)), dst_ref=outs[a].at[me],
                send_sem=send_sems.at[a, k], recv_sem=recv_sems.at[a, k],
                device_id=to, device_id_type=MESH)

        def arrival(a, k, frm):
            slot = _dev_index(*frm)
            return pltpu.make_async_remote_copy(
                src_ref=src_for(a, slot), dst_ref=outs[a].at[slot],
                send_sem=send_sems.at[a, k], recv_sem=recv_sems.at[a, k],
                device_id=frm, device_id_type=MESH)

        mine = [pltpu.make_async_copy(src_for(a, me), outs[a].at[me], local_sems.at[a]) for a in range(n)]
        for cp in mine:
            cp.start()
        sends = [copy(a, k, to) for k, to in enumerate(peers) for a in range(n)]
        for cp in sends:
            cp.start()
        for k, frm in enumerate(peers):
            for a in range(n):
                arrival(a, k, frm).wait_recv()
        for cp in sends:
            cp.wait_send()
        for cp in mine:
            cp.wait()

    out_shape = [jax.ShapeDtypeStruct(a.shape, a.dtype) for a in scatter]
    out_shape += [jax.ShapeDtypeStruct((N_DEV,) + a.shape, a.dtype) for a in gather]
    return pl.pallas_call(
        body, name=name, out_shape=out_shape,
        in_specs=[ANY] * n, out_specs=[ANY] * n,
        scratch_shapes=[pltpu.SemaphoreType.DMA((n, 7)), pltpu.SemaphoreType.DMA((n, 7)),
                        pltpu.SemaphoreType.DMA((n,))],
    )(*arrays)


def _reduce_grads(gw_in, gw_out, small, d_ada):
    shards = (gw_in, gw_out)

    def body(in0_ref, in1_ref, small_ref, dada_ref,
             g0_ref, g1_ref, total_ref, dall_ref,
             r1_0, r1_1, s2_0, s2_1, r2_0, r2_1, sm_sib, sm_chip, sm_recv, send_sems, recv_sems):
        ins, gs = (in0_ref, in1_ref), (g0_ref, g1_ref)
        r1, s2, r2 = (r1_0, r1_1), (s2_0, s2_1), (r2_0, r2_1)
        x, y, c = _mesh_pos()
        me = _dev_index(x, y, c)
        sibling = (x, y, 1 - c)
        chips = [(x, y), (1 - x, y), (x, 1 - y), (1 - x, 1 - y)]

        def remote(src, dst, k, to):
            return pltpu.make_async_remote_copy(src_ref=src, dst_ref=dst, send_sem=send_sems.at[k],
                                                recv_sem=recv_sems.at[k], device_id=to, device_id_type=MESH)

        sends = []

        def send(cp):
            cp.start()
            sends.append(cp)

        for a in range(2):
            for q, chip in enumerate(chips):
                send(remote(ins[a].at[_dev_index(*chip, 1 - c)], r1[a].at[q], 4 * a + q, sibling))
        send(remote(small_ref, sm_sib, 14, sibling))
        peers = []
        for p in range(1, N_DEV):
            px, py, pc = (p >> 2) & 1, (p >> 1) & 1, p & 1
            peers.append((1 - x if px else x, 1 - y if py else y, 1 - c if pc else c))
        for k, to in enumerate(peers):
            send(remote(dada_ref, dall_ref.at[me], 18 + k, to))
        dall_ref[me] = dada_ref[...]

        for a in range(2):
            own = None
            for q in (1, 2, 3, 0):
                chip = chips[q]
                remote(ins[a].at[0], r1[a].at[q], 4 * a + q, sibling).wait_recv()
                pair = ins[a][_dev_index(*chip, c)].astype(F32) + r1[a][q].astype(F32)
                if q == 0:
                    own = pair
                else:
                    s2[a][q - 1] = pair.astype(BF16)
                    send(remote(s2[a].at[q - 1], r2[a].at[q - 1], 8 + 3 * a + q - 1, (*chip, c)))
            for j in range(3):
                remote(s2[a].at[j], r2[a].at[j], 8 + 3 * a + j, sibling).wait_recv()
                own = own + r2[a][j].astype(F32)
            gs[a][...] = own

        remote(small_ref, sm_sib, 14, sibling).wait_recv()
        sm_chip[...] = small_ref[...] + sm_sib[...]
        for j in range(3):
            send(remote(sm_chip, sm_recv.at[j], 15 + j, (*chips[j + 1], c)))
        for j in range(3):
            remote(sm_chip, sm_recv.at[j], 15 + j, sibling).wait_recv()
        total = None
        for ax in range(2):
            for ay in range(2):
                dx, dy = x != ax, y != ay
                term = jnp.where(dx, jnp.where(dy, sm_recv[2], sm_recv[0]), jnp.where(dy, sm_recv[1], sm_chip[...]))
                total = term if total is None else total + term
        total_ref[...] = total

        for k, frm in enumerate(peers):
            remote(dada_ref, dall_ref.at[_dev_index(*frm)], 18 + k, frm).wait_recv()
        for cp in sends:
            cp.wait_send()

    def like(a, lead, dtype):
        return pltpu.VMEM((lead,) + a.shape[1:], dtype)

    return pl.pallas_call(
        body, name="reduce_grads",
        out_shape=[jax.ShapeDtypeStruct(gw_in.shape[1:], F32), jax.ShapeDtypeStruct(gw_out.shape[1:], F32),
                   jax.ShapeDtypeStruct(small.shape, F32), jax.ShapeDtypeStruct((N_DEV,) + d_ada.shape, F32)],
        scratch_shapes=[like(gw_in, 4, BF16), like(gw_out, 4, BF16), like(gw_in, 3, BF16), like(gw_out, 3, BF16),
                        like(gw_in, 3, BF16), like(gw_out, 3, BF16),
                        pltpu.VMEM(small.shape, F32), pltpu.VMEM(small.shape, F32), pltpu.VMEM((3,) + small.shape, F32),
                        pltpu.SemaphoreType.DMA((25,)), pltpu.SemaphoreType.DMA((25,))],
        compiler_params=_params(),
    )(*shards, small, d_ada)


def _ada_forward(c_all, w_ada):
    def body(c_ref, w_ref, sc_ref, part_ref):
        cc = c_ref[...]
        sc = cc * _sigmoid(cc)
        sc_ref[...] = sc
        part_ref[...] = _dot3_rhs(sc, w_ref[...])

    return pl.pallas_call(
        body, name="ada_forward",
        out_shape=[jax.ShapeDtypeStruct(c_all.shape, F32),
                   jax.ShapeDtypeStruct((N_DEV, w_ada.shape[1]), F32)],
        compiler_params=_params(),
    )(c_all, w_ada)


def _dot3_rhs(a, b):
    a0, a1, a2 = _split3(a)
    b0, b1, b2 = _split3(b)
    return (_dot(a0, b0) + (_dot(a0, b1) + _dot(a1, b0))
            + (_dot(a0, b2) + _dot(a1, b1) + _dot(a2, b0)))


def _inproj_forward(x, mod, w_main, w_f, b_main, b_f, tile):
    seq = x.shape[0]
    nt = seq // tile

    def body(x_ref, mod_ref, w_ref, wf_ref, b_ref, bf_ref,
             qp_ref, kp_ref, vp_ref, f_ref, p_ref, ga_ref, gp_ref, u_ref, carry_ref):
        i = pl.program_id(0)

        @pl.when(i == 0)
        def _():
            carry_ref[...] = jnp.zeros_like(carry_ref)

        u = x_ref[...] * mod_ref[0:1, :] + mod_ref[1:2, :]
        ub = u.astype(BF16)
        u_ref[...] = ub

        f = _dot_nt(ub, wf_ref[...]) + bf_ref[...]
        f_ref[...] = f
        lane = _lanes((tile, LANE))
        log_f = jnp.where(lane < N_HEADS, jnp.minimum(f, 0.0) - jnp.log(1.0 + jnp.exp(-jnp.abs(f))), 0.0)
        row = lax.broadcasted_iota(jnp.int32, (tile, tile), 0)
        col = lax.broadcasted_iota(jnp.int32, (tile, tile), 1)
        tri = (row >= col).astype(BF16)
        cum = _dot3(tri, log_f) + carry_ref[0:1, :]
        carry_ref[0:1, :] = cum[tile - 1:tile, :]
        cq = [part.astype(F32) for part in _split3(cum)]
        ck = [part.astype(F32) for part in _split3(-cum)]

        def proj(chunk):
            cols = pl.ds(chunk * COL_CHUNK, COL_CHUNK)
            return _dot_nt(ub, w_ref[cols, :]) + b_ref[:, cols]

        heads_per_chunk = COL_CHUNK // LANE
        for chunk in range(2):
            r = proj(chunk)
            for j in range(heads_per_chunk):
                h = chunk * heads_per_chunk + j
                extra = jnp.where((lane >= L_CK) & (lane < L_CK + 3), 1.0, 0.0)
                extra = _place3(lane, L_CQ, [part[:, h:h + 1] for part in cq], extra)
                qp_ref[h] = jnp.where(lane < HEAD_DIM, r[:, j * LANE:(j + 1) * LANE] * Q_SCALE, extra).astype(BF16)
        for chunk in range(2):
            r = proj(2 + chunk)
            for j in range(heads_per_chunk):
                h = chunk * heads_per_chunk + j
                ones = ((lane >= L_CQ) & (lane < L_CQ + 3)) | ((lane >= L_LSE) & (lane < L_LSE + 3))
                extra = _place3(lane, L_CK, [part[:, h:h + 1] for part in ck], jnp.where(ones, 1.0, 0.0))
                kp_ref[h] = jnp.where(lane < HEAD_DIM, r[:, j * LANE:(j + 1) * LANE], extra).astype(BF16)
        for chunk in range(2):
            r = proj(4 + chunk)
            for j in range(heads_per_chunk):
                h = chunk * heads_per_chunk + j
                extra = jnp.where((lane >= HEAD_DIM) & (lane < HEAD_DIM + 3), -1.0, 0.0)
                vp_ref[h] = jnp.where(lane < HEAD_DIM, r[:, j * LANE:(j + 1) * LANE], extra).astype(BF16)
        p_ref[...] = proj(6)
        ga_ref[...] = proj(7)
        gp_ref[...] = proj(8)

    head_block = pl.BlockSpec((N_HEADS, tile, LANE), lambda i: (0, i, 0))
    tok = lambda width: pl.BlockSpec((tile, width), lambda i: (i, 0))
    whole = lambda a: pl.BlockSpec(a.shape, lambda i: (0,) * a.ndim)
    padded = jax.ShapeDtypeStruct((N_HEADS, seq, LANE), BF16)
    half = jax.ShapeDtypeStruct((seq, D_ATT), F32)
    return pl.pallas_call(
        body, name="inproj_forward", grid=(nt,),
        in_specs=[tok(D), whole(mod), whole(w_main), whole(w_f), whole(b_main), whole(b_f)],
        out_specs=[head_block, head_block, head_block, tok(LANE), tok(D_POOL), tok(D_ATT), tok(D_POOL),
                   tok(D)],
        out_shape=[padded, padded, padded, jax.ShapeDtypeStruct((seq, LANE), F32), half, half, half,
                   jax.ShapeDtypeStruct((seq, D), BF16)],
        scratch_shapes=[pltpu.VMEM((8, LANE), F32)],
        compiler_params=_params(("arbitrary",)),
    )(x, mod, w_main, w_f, b_main, b_f)


def _attention_forward(qp, kp, vp, tile):
    seq = qp.shape[1]
    nb = seq // tile

    def body(q_ref, k_ref, v_ref, att_ref, q2t_ref, s_a, s_b):
        i = pl.program_id(1)
        sub = lax.broadcasted_iota(jnp.int32, (LANE, tile), 0)
        row = lax.broadcasted_iota(jnp.int32, (tile, tile), 0)
        col = lax.broadcasted_iota(jnp.int32, (tile, tile), 1)
        q = [q_ref[0], q_ref[1]]

        def scores(buf, kb):
            rows = pl.ds(pl.multiple_of(kb * tile, tile), tile)
            for hh in range(2):
                buf[hh] = _dot_nt(k_ref[hh, rows, :], q[hh])

        def absorb(buf, kb, carry, masked):
            rows = pl.ds(pl.multiple_of(kb * tile, tile), tile)
            out = []
            for hh in range(2):
                m, acc = carry[hh]
                s = buf[hh]
                if masked:
                    s = jnp.where(row <= col, s, -1e30)
                m_new = jnp.maximum(m, jnp.max(s, axis=0, keepdims=True))
                p = jnp.exp(s - m_new).astype(BF16)
                acc = jnp.exp(m - m_new) * acc + _dot_tn(v_ref[hh, rows, :], p)
                out.append((m_new, acc))
            return tuple(out)

        def two_blocks(j, carry):
            scores(s_b, 2 * j + 1)
            carry = absorb(s_a, 2 * j, carry, False)
            scores(s_a, 2 * j + 2)
            return absorb(s_b, 2 * j + 1, carry, False)

        def last_block(carry):
            return absorb(s_a, i, carry, True)

        def last_two_blocks(carry):
            scores(s_b, i)
            carry = absorb(s_a, i - 1, carry, False)
            return absorb(s_b, i, carry, True)

        scores(s_a, 0)
        init = (jnp.full((1, tile), -1e30, F32), jnp.zeros((LANE, tile), F32))
        carry = lax.fori_loop(0, i // 2, two_blocks, (init, init))
        carry = lax.cond(i % 2 == 0, last_block, last_two_blocks, carry)
        outs = []
        for hh in range(2):
            m, acc = carry[hh]
            l = -acc[HEAD_DIM:HEAD_DIM + 1, :]
            outs.append((acc / l)[:HEAD_DIM, :])
            neg_lse = [part.astype(F32) for part in _split3(-(m + jnp.log(l)))]
            q2t_ref[hh] = _place3(sub, L_LSE, neg_lse, q[hh].astype(F32).T).astype(BF16)
        att_ref[...] = jnp.concatenate(outs, axis=0).T

    pair = pl.BlockSpec((2, tile, LANE), lambda hp, i: (hp, i, 0))
    full = pl.BlockSpec((2, seq, LANE), lambda hp, i: (hp, 0, 0))
    return pl.pallas_call(
        body, name="attention_forward", grid=(N_HEADS // 2, nb),
        in_specs=[pair, full, full],
        out_specs=[pl.BlockSpec((tile, LANE), lambda hp, i: (i, hp)),
                   pl.BlockSpec((2, LANE, tile), lambda hp, i: (hp, 0, i))],
        out_shape=[jax.ShapeDtypeStruct((seq, D_ATT), F32),
                   jax.ShapeDtypeStruct((N_HEADS, LANE, seq), BF16)],
        scratch_shapes=[pltpu.VMEM((2, tile, tile), F32), pltpu.VMEM((2, tile, tile), F32)],
        compiler_params=_params(("arbitrary", "arbitrary")),
    )(qp, kp, vp)


def _window_matrices(tile, window, transposed):
    r = lax.broadcasted_iota(jnp.int32, (tile, tile), 0)
    c = lax.broadcasted_iota(jnp.int32, (tile, tile), 1)
    rh = lax.broadcasted_iota(jnp.int32, (tile, HALO), 0)
    ch = lax.broadcasted_iota(jnp.int32, (tile, HALO), 1)
    if not transposed:
        cur = (c <= r) & (r - c < window)
        halo = (rh + HALO - ch) < window
    else:
        cur = (r <= c) & (c - r < window)
        halo = (tile + ch - rh) < window
    return cur.astype(BF16), halo.astype(BF16)


def _silu_parts(g):
    sig = _sigmoid(g)
    return g * sig, sig * (1.0 + g * (1.0 - sig))


def _head_placement():
    n = jnp.arange(N_HEADS * LANE)[:, None]
    r = jnp.arange(D_ATT + 3 * LANE)[None, :]
    h, l = n // LANE, n % LANE
    src = jnp.where(l < HEAD_DIM, HEAD_DIM * h + l, D_ATT + LANE * (l - HEAD_DIM) + h)
    return ((r == src) & (l < HEAD_DIM + 3)).astype(BF16)


def _middle(x, tgt, att, g_att, g_pool, p, vecs, pool_vecs, w_out, w_pool, place, tile):
    seq = x.shape[0]
    nt = seq // tile
    halo_blocks = tile // HALO

    def body(x_ref, tgt_ref, att_ref, ga_ref, gp_ref, p_ref, ph_ref, vec_ref, pvec_ref, wo_ref, wp_ref, place_ref,
             dxa_ref, do2_ref, dga_ref, dgp_ref, dpooled_ref, dwo_ref, dwp_ref, dvec_ref, dpvec_ref):
        i = pl.program_id(0)

        @pl.when(i == 0)
        def _():
            dwo_ref[...] = jnp.zeros_like(dwo_ref)
            dwp_ref[...] = jnp.zeros_like(dwp_ref)
            dvec_ref[...] = jnp.zeros_like(dvec_ref)
            dpvec_ref[...] = jnp.zeros_like(dpvec_ref)

        gate, b_out, ln_g, ln_b = (vec_ref[k:k + 1, :] for k in range(4))
        b_pool, pool_scale = pvec_ref[0:1, :], pvec_ref[1:2, :]
        x = x_ref[...]
        p = p_ref[...]
        p_halo = ph_ref[...] * jnp.where(i > 0, 1.0, 0.0)
        pos = i * tile + lax.broadcasted_iota(jnp.int32, (tile, 1), 0) + 1

        pooled, mixed = [], []
        for g, window in enumerate(POOL_WINDOWS):
            cols = slice(g * GROUP_DIM, (g + 1) * GROUP_DIM)
            m_cur, m_halo = _window_matrices(tile, window, False)
            wsum = _dot3(m_cur, p[:, cols]) + _dot3(m_halo, p_halo[:, cols])
            count = jnp.minimum(pos, window).astype(F32)
            pooled.append(wsum / count - p[:, cols])
            mixed.append(_dot(pooled[g].astype(BF16), wp_ref[g]) + b_pool[:, cols])
        mixed = jnp.concatenate(mixed, axis=1)
        pool = mixed * pool_scale

        att = att_ref[...]
        g_att, g_pool = ga_ref[...], gp_ref[...]
        silu_a, dsilu_a = _silu_parts(g_att)
        silu_p, dsilu_p = _silu_parts(g_pool)
        y_in = jnp.concatenate([att * silu_a, pool * silu_p], axis=1)
        y = _dot(y_in.astype(BF16), wo_ref[...]) + b_out
        h = ALPHA * x + gate * y
        mu = jnp.mean(h, axis=1, keepdims=True)
        hc = h - mu
        var = jnp.mean(hc * hc, axis=1, keepdims=True)
        rstd = lax.rsqrt(var + LN_EPS)
        yhat = hc * rstd
        diff = yhat * ln_g + ln_b - tgt_ref[...]
        loss_rows = jnp.sum(diff * diff, axis=1, keepdims=True)
        d_out = diff * (1.0 / D)

        d_yhat = d_out * ln_g
        dh = rstd * (d_yhat - jnp.mean(d_yhat, axis=1, keepdims=True)
                     - yhat * jnp.mean(d_yhat * yhat, axis=1, keepdims=True))
        dxa_ref[...] = ALPHA * dh
        dy = dh * gate
        dyb = dy.astype(BF16)
        lane = _lanes((1, D))
        loss_row = jnp.where(lane == 0, (0.5 / D) * jnp.sum(loss_rows, axis=0, keepdims=True), 0.0)
        dvec_ref[0:1, :] += jnp.sum(dh * y, axis=0, keepdims=True)
        dvec_ref[1:2, :] += jnp.sum(dy, axis=0, keepdims=True)
        dvec_ref[2:3, :] += jnp.sum(d_out * yhat, axis=0, keepdims=True)
        dvec_ref[3:4, :] += jnp.sum(d_out, axis=0, keepdims=True)
        dvec_ref[4:5, :] += loss_row

        dwo_ref[...] += _dot(y_in.T.astype(BF16), dyb)
        d_yin = _dot_nt(dyb, wo_ref[...])
        d_a, d_pl = d_yin[:, :D_ATT], d_yin[:, D_ATT:]
        d_att = d_a * silu_a
        r = lax.broadcasted_iota(jnp.int32, (D_ATT, LANE), 0)
        c = lax.broadcasted_iota(jnp.int32, (D_ATT, LANE), 1)
        head_of = ((r >= HEAD_DIM * c) & (r < HEAD_DIM * c + HEAD_DIM)).astype(BF16)
        delta = sum(_dot(part, head_of) for part in _split3(d_att * att))
        do2t = _dot_nt(place_ref[...], jnp.concatenate([d_att.astype(BF16), *_split3(delta)], axis=1))
        for h in range(N_HEADS):
            do2_ref[h] = do2t[h * LANE:(h + 1) * LANE, :].astype(BF16)
        dga_ref[...] = d_a * att * dsilu_a
        dgp_ref[...] = d_pl * pool * dsilu_p
        d_pool = d_pl * silu_p
        d_mixed = d_pool * pool_scale
        dpvec_ref[0:1, :] += jnp.sum(d_mixed, axis=0, keepdims=True)
        dpvec_ref[1:2, :] += jnp.sum(d_pool * mixed, axis=0, keepdims=True)
        d_pooled = []
        for g in range(len(POOL_WINDOWS)):
            cols = slice(g * GROUP_DIM, (g + 1) * GROUP_DIM)
            dmb = d_mixed[:, cols].astype(BF16)
            dwp_ref[g] += _dot(pooled[g].T.astype(BF16), dmb)
            d_pooled.append(_dot_nt(dmb, wp_ref[g]))
        dpooled_ref[...] = jnp.concatenate(d_pooled, axis=1)

    tok = lambda width: pl.BlockSpec((tile, width), lambda i: (i, 0))
    whole = lambda a: pl.BlockSpec(a.shape, lambda i: (0,) * a.ndim)
    halo = pl.BlockSpec((HALO, D_POOL), lambda i: (jnp.maximum(i * halo_blocks - 1, 0), 0))
    half = jax.ShapeDtypeStruct((seq, D_ATT), F32)
    outs = [jax.ShapeDtypeStruct((seq, D), F32), jax.ShapeDtypeStruct((N_HEADS, LANE, seq), BF16), half, half, half,
            jax.ShapeDtypeStruct(w_out.shape, F32), jax.ShapeDtypeStruct(w_pool.shape, F32),
            jax.ShapeDtypeStruct(vecs.shape, F32), jax.ShapeDtypeStruct(pool_vecs.shape, F32)]
    return pl.pallas_call(
        body, name="middle", grid=(nt,),
        in_specs=[tok(D), tok(D), tok(D_ATT), tok(D_ATT), tok(D_POOL), tok(D_POOL), halo,
                  whole(vecs), whole(pool_vecs), whole(w_out), whole(w_pool), whole(place)],
        out_specs=[tok(D), pl.BlockSpec((N_HEADS, LANE, tile), lambda i: (0, 0, i)),
                   tok(D_ATT), tok(D_POOL), tok(D_POOL),
                   whole(w_out), whole(w_pool), whole(vecs), whole(pool_vecs)],
        out_shape=outs,
        compiler_params=_params(("arbitrary",)),
    )(x, tgt, att, g_att, g_pool, p, p, vecs, pool_vecs, w_out, w_pool, place)


def _attention_backward(q2t, kp, vp, do2t, tile):
    seq = kp.shape[1]
    nb = seq // tile

    def body(qt_ref, k_ref, v_ref, dot_ref, dq_ref, dk_ref, dv_ref, dcum_ref, dq_acc):
        row = lax.broadcasted_iota(jnp.int32, (tile, tile), 0)
        col = lax.broadcasted_iota(jnp.int32, (tile, tile), 1)
        dq_acc[...] = jnp.zeros_like(dq_acc)

        def kv_block(kb, _):
            krows = pl.ds(pl.multiple_of(kb * tile, tile), tile)
            k = [k_ref[hh, krows, :] for hh in range(2)]
            v = [v_ref[hh, krows, :] for hh in range(2)]
            k_t = [k[hh].T for hh in range(2)]

            def q_block(qb, carry, masked):
                qcols = pl.ds(pl.multiple_of(qb * tile, tile), tile)
                out = []
                for hh in range(2):
                    dk, dv = carry[hh]
                    q_t = qt_ref[hh, :, qcols]
                    do_t = dot_ref[hh, :, qcols]
                    s_t = _dot(k[hh], q_t)
                    if masked:
                        s_t = jnp.where(row <= col, s_t, -1e30)
                    p_t = jnp.exp(s_t)
                    ds_t = (p_t * _dot(v[hh], do_t)).astype(BF16)
                    dv = dv + _dot_nt(do_t, p_t.astype(BF16))
                    dk = dk + _dot_nt(q_t, ds_t)
                    dq_acc[hh, :, qcols] += _dot(k_t[hh], ds_t)
                    out.append((dk, dv))
                return tuple(out)

            zero = jnp.zeros((LANE, tile), F32)
            carry = q_block(kb, ((zero, zero), (zero, zero)), True)
            carry = lax.fori_loop(kb + 1, nb, lambda qb, cr: q_block(qb, cr, False), carry)
            for hh in range(2):
                dk = carry[hh][0]
                dk_ref[hh, :, krows] = dk.astype(BF16)
                dv_ref[hh, :, krows] = carry[hh][1].astype(BF16)
                dcum_ref[hh, :, krows] = -dk[L_CK:L_CK + 1, :]
            return 0

        lax.fori_loop(0, nb, kv_block, 0)
        for hh in range(2):
            dq = dq_acc[hh]
            dcum_ref[hh] += dq[L_CQ:L_CQ + 1, :]
            dq_ref[hh] = (dq * Q_SCALE).astype(BF16)

    pair = pl.BlockSpec((2, seq, LANE), lambda hp: (hp, 0, 0))
    pair_t = pl.BlockSpec((2, LANE, seq), lambda hp: (hp, 0, 0))
    grad = jax.ShapeDtypeStruct((N_HEADS, LANE, seq), BF16)
    return pl.pallas_call(
        body, name="attention_backward", grid=(N_HEADS // 2,),
        in_specs=[pair_t, pair, pair, pair_t],
        out_specs=[pair_t, pair_t, pair_t, pl.BlockSpec((2, 1, seq), lambda hp: (hp, 0, 0))],
        out_shape=[grad, grad, grad, jax.ShapeDtypeStruct((N_HEADS, 1, seq), F32)],
        scratch_shapes=[pltpu.VMEM((2, LANE, seq), F32)],
        compiler_params=_params(("arbitrary",)),
    )(q2t, kp, vp, do2t)


def _inproj_backward(dqp, dkp, dvp, d_cum, f, d_pooled, d_ga, d_gp, x, dxa, u, mod, w_main, w_f, tile):
    seq = x.shape[0]
    nt = seq // tile
    halo_blocks = tile // HALO

    def body(dq_ref, dk_ref, dv_ref, dcum_ref, f_ref, dpo_ref, dph_ref, dga_ref, dgp_ref, x_ref, dxa_ref, u_ref,
             mod_ref, w_ref, wf_ref,
             dx_ref, dproj_ref, dwf_ref, db_ref, dbf_ref, dmod_ref, carry_ref):
        step = pl.program_id(0)
        i = nt - 1 - step

        @pl.when(step == 0)
        def _():
            carry_ref[...] = jnp.zeros_like(carry_ref)
            dwf_ref[...] = jnp.zeros_like(dwf_ref)
            db_ref[...] = jnp.zeros_like(db_ref)
            dbf_ref[...] = jnp.zeros_like(dbf_ref)
            dmod_ref[...] = jnp.zeros_like(dmod_ref)

        ones = jnp.ones((8, tile), BF16)

        def emit(chunk, val):
            cols = pl.ds(chunk * COL_CHUNK, COL_CHUNK)
            db_ref[0:1, cols] += jnp.sum(val, axis=0, keepdims=True)
            vb = val.astype(BF16)
            dproj_ref[:, pl.ds((chunk - 6) * COL_CHUNK, COL_CHUNK)] = vb
            return _dot(vb, w_ref[cols, :])

        heads_per_chunk = COL_CHUNK // LANE
        d_u = jnp.zeros((tile, D), F32)
        for j, ref in enumerate((dq_ref, dk_ref, dv_ref)):
            for half in range(2):
                chunk = 2 * j + half
                cols = pl.ds(chunk * COL_CHUNK, COL_CHUNK)
                val_t = ref[pl.ds(half * heads_per_chunk, heads_per_chunk)].reshape(COL_CHUNK, tile)
                db_ref[:, cols] += _dot_nt(ones, val_t)
                d_u += _dot_tn(val_t, w_ref[cols, :])

        d_pooled = dpo_ref[...]
        d_halo = dph_ref[...] * jnp.where(i < nt - 1, 1.0, 0.0)
        pos = i * tile + lax.broadcasted_iota(jnp.int32, (tile, 1), 0) + 1
        d_p = []
        for g, window in enumerate(POOL_WINDOWS):
            cols = slice(g * GROUP_DIM, (g + 1) * GROUP_DIM)
            m_cur, m_halo = _window_matrices(tile, window, True)
            scaled = d_pooled[:, cols] / jnp.minimum(pos, window).astype(F32)
            d_p.append(_dot3(m_cur, scaled) + _dot3(m_halo, d_halo[:, cols] * (1.0 / window)) - d_pooled[:, cols])
        d_u += emit(6, jnp.concatenate(d_p, axis=1))
        d_u += emit(7, dga_ref[...])
        d_u += emit(8, dgp_ref[...])

        row = lax.broadcasted_iota(jnp.int32, (tile, tile), 0)
        col = lax.broadcasted_iota(jnp.int32, (tile, tile), 1)
        later = (row >= col).astype(BF16)
        d_logf = sum(_dot(part, later) for part in _split3(dcum_ref[:, 0, :])) + carry_ref[:, 0:1]
        carry_ref[:, 0:1] = d_logf[:, 0:1]
        d_f = d_logf * _sigmoid(-f_ref[...].T[0:N_HEADS, :])
        d_f = jnp.concatenate([d_f, jnp.zeros((LANE - N_HEADS, tile), F32)], axis=0)
        dbf_ref[...] += sum(_dot_nt(ones, part) for part in _split3(d_f))
        d_fb = d_f.astype(BF16)
        d_u += _dot_tn(d_fb, wf_ref[...])
        dwf_ref[...] += _dot(d_fb, u_ref[...])

        x = x_ref[...]
        dx_ref[...] = dxa_ref[...] + d_u * mod_ref[0:1, :]
        dmod_ref[0:1, :] += jnp.sum(d_u * x, axis=0, keepdims=True)
        dmod_ref[1:2, :] += jnp.sum(d_u, axis=0, keepdims=True)

    rev = lambda step: nt - 1 - step
    tok = lambda width: pl.BlockSpec((tile, width), lambda s: (rev(s), 0))
    head_block = pl.BlockSpec((N_HEADS, LANE, tile), lambda s: (0, 0, rev(s)))
    whole = lambda a: pl.BlockSpec(a.shape, lambda s: (0,) * a.ndim)
    halo = pl.BlockSpec((HALO, D_POOL), lambda s: (jnp.minimum((rev(s) + 1) * halo_blocks, seq // HALO - 1), 0))
    small = lambda width: jax.ShapeDtypeStruct((8, width), F32)
    n_rest = N_MAIN - OFF_P
    return pl.pallas_call(
        body, name="inproj_backward", grid=(nt,),
        in_specs=[head_block, head_block, head_block, pl.BlockSpec((N_HEADS, 1, tile), lambda s: (0, 0, rev(s))),
                  tok(LANE), tok(D_POOL), halo, tok(D_ATT), tok(D_POOL),
                  tok(D), tok(D), tok(D),
                  whole(mod), whole(w_main), whole(w_f)],
        out_specs=[tok(D), tok(n_rest), pl.BlockSpec((LANE, D), lambda s: (0, 0)),
                   pl.BlockSpec((8, N_MAIN), lambda s: (0, 0)), pl.BlockSpec((8, LANE), lambda s: (0, 0)),
                   pl.BlockSpec((8, D), lambda s: (0, 0))],
        out_shape=[jax.ShapeDtypeStruct((seq, D), F32), jax.ShapeDtypeStruct((seq, n_rest), BF16),
                   jax.ShapeDtypeStruct((LANE, D), F32), small(N_MAIN), small(LANE), small(D)],
        scratch_shapes=[pltpu.VMEM((8, LANE), F32)],
        compiler_params=_params(("arbitrary",)),
    )(dqp, dkp, dvp, d_cum, f, d_pooled, d_pooled, d_ga, d_gp, x, dxa, u, mod, w_main, w_f)


def _weight_grad(dproj, u, k_tile):
    seq, n_cols = dproj.shape
    nk = seq // k_tile

    def body(dp_ref, u_ref, out_ref):
        @pl.when(pl.program_id(1) == 0)
        def _():
            out_ref[...] = jnp.zeros_like(out_ref)

        out_ref[...] += _dot_tn(dp_ref[...], u_ref[...])

    return pl.pallas_call(
        body, name="weight_grad", grid=(n_cols // COL_CHUNK, nk),
        in_specs=[pl.BlockSpec((k_tile, COL_CHUNK), lambda n, k: (k, n)),
                  pl.BlockSpec((k_tile, D), lambda n, k: (k, 0))],
        out_specs=pl.BlockSpec((COL_CHUNK, D), lambda n, k: (n, 0)),
        out_shape=jax.ShapeDtypeStruct((n_cols, D), F32),
        compiler_params=_params(("arbitrary", "arbitrary")),
    )(dproj, u)


def _weight_grad_heads(grad_t, u, k_tile, name):
    seq = u.shape[0]
    nk = seq // k_tile
    heads = COL_CHUNK // LANE

    def body(g_ref, u_ref, out_ref):
        @pl.when(pl.program_id(1) == 0)
        def _():
            out_ref[...] = jnp.zeros_like(out_ref)

        out_ref[...] += _dot(g_ref[...].reshape(COL_CHUNK, k_tile), u_ref[...])

    return pl.pallas_call(
        body, name=name, grid=(N_HEADS // heads, nk),
        in_specs=[pl.BlockSpec((heads, LANE, k_tile), lambda n, k: (n, 0, k)),
                  pl.BlockSpec((k_tile, D), lambda n, k: (k, 0))],
        out_specs=pl.BlockSpec((COL_CHUNK, D), lambda n, k: (n, 0)),
        out_shape=jax.ShapeDtypeStruct((N_HEADS * LANE, D), F32),
        compiler_params=_params(("arbitrary", "arbitrary")),
    )(grad_t, u)


def _adamw(w, g, m, v):
    m = ADAM_B1 * m + (1.0 - ADAM_B1) * g
    v = ADAM_B2 * v + (1.0 - ADAM_B2) * (g * g)
    m_hat = m / (1.0 - ADAM_B1 ** ADAM_STEP)
    v_hat = v / (1.0 - ADAM_B2 ** ADAM_STEP)
    delta = -ADAM_LR * (m_hat / (jnp.sqrt(v_hat) + ADAM_EPS) + ADAM_WD * w)
    return delta, m, v


def _adamw_call(g, w, m, v, lead_tile, name):
    nr = w.shape[0] // lead_tile

    def body(gi_ref, w_ref, m_ref, v_ref, g_ref, d_ref, nm_ref, nv_ref):
        g = gi_ref[...]
        g_ref[...] = g
        d_ref[...], nm_ref[...], nv_ref[...] = _adamw(w_ref[...], g, m_ref[...], v_ref[...])

    blk = pl.BlockSpec((lead_tile,) + w.shape[1:], lambda r: (r,) + (0,) * (w.ndim - 1))
    shape = jax.ShapeDtypeStruct(w.shape, F32)
    return pl.pallas_call(
        body, name=name, grid=(nr,),
        in_specs=[blk, blk, blk, blk], out_specs=[blk, blk, blk, blk],
        out_shape=[shape, shape, shape, shape],
        compiler_params=_params(("arbitrary",)),
    )(g, w, m, v)


def _sum_adamw(parts, w, m, v, row_tile, name):
    rows, cols = w.shape
    nr = rows // row_tile

    def body(parts_ref, w_ref, m_ref, v_ref, g_ref, d_ref, nm_ref, nv_ref):
        g = parts_ref[0]
        for k in range(1, N_DEV):
            g = g + parts_ref[k]
        g_ref[...] = g
        d_ref[...], nm_ref[...], nv_ref[...] = _adamw(w_ref[...], g, m_ref[...], v_ref[...])

    blk = pl.BlockSpec((row_tile, cols), lambda r: (r, 0))
    shape = jax.ShapeDtypeStruct(w.shape, F32)
    return pl.pallas_call(
        body, name=name, grid=(nr,),
        in_specs=[pl.BlockSpec((N_DEV, row_tile, cols), lambda r: (0, r, 0)), blk, blk, blk],
        out_specs=[blk, blk, blk, blk],
        out_shape=[shape, shape, shape, shape],
        compiler_params=_params(("arbitrary",)),
    )(parts, w, m, v)


def _ada_adamw(sc_t, d_ada, w, m, v):
    def body(sc_ref, d_ref, w_ref, m_ref, v_ref, g_ref, dl_ref, nm_ref, nv_ref):
        g = sc_ref[:, 0:1] * d_ref[0:1, :]
        for b in range(1, N_DEV):
            g = g + sc_ref[:, b:b + 1] * d_ref[b:b + 1, :]
        g_ref[...] = g
        dl_ref[...], nm_ref[...], nv_ref[...] = _adamw(w_ref[...], g, m_ref[...], v_ref[...])

    shape = jax.ShapeDtypeStruct(w.shape, F32)
    return pl.pallas_call(
        body, name="ada_adamw", out_shape=[shape, shape, shape, shape], compiler_params=_params(),
    )(sc_t, d_ada, w, m, v)


def _pad_heads(a):
    lead = a.shape[:-1]
    a = a.reshape(lead + (N_HEADS, HEAD_DIM))
    a = jnp.pad(a, [(0, 0)] * len(lead) + [(0, 0), (0, LANE - HEAD_DIM)])
    return a.reshape(lead + (N_HEADS * LANE,))


def _unpad_heads(a):
    lead = a.shape[:-1]
    return a.reshape(lead + (N_HEADS, LANE))[..., :HEAD_DIM].reshape(lead + (N_HEADS * HEAD_DIM,))


def _to_main(a):
    q, k, v = a[..., 0:512], a[..., 512:1024], a[..., 1024:1536]
    f = a[..., 1536:1544]
    rest = a[..., 1544:]
    main = jnp.concatenate([_pad_heads(q), _pad_heads(k), _pad_heads(v), rest], axis=-1)
    f = jnp.pad(f, [(0, 0)] * (a.ndim - 1) + [(0, LANE - N_HEADS)])
    return main, f


def _from_main(main, f):
    q, k, v = (_unpad_heads(main[..., j * 1024:(j + 1) * 1024]) for j in range(3))
    return jnp.concatenate([q, k, v, f[..., :N_HEADS], main[..., 3072:]], axis=-1)


def _rows_to_main(a):
    def pad_heads(b):
        return jnp.pad(b.reshape(N_HEADS, HEAD_DIM, -1), ((0, 0), (0, LANE - HEAD_DIM), (0, 0))).reshape(N_HEADS * LANE, -1)

    main = jnp.concatenate([pad_heads(a[0:512]), pad_heads(a[512:1024]), pad_heads(a[1024:1536]), a[1544:]], axis=0)
    return main, jnp.pad(a[1536:1544], ((0, LANE - N_HEADS), (0, 0)))


def _rows_from_main(q, k, v, f, rest):
    def unpad_heads(b):
        return b.reshape(N_HEADS, LANE, -1)[:, :HEAD_DIM].reshape(N_HEADS * HEAD_DIM, -1)

    return jnp.concatenate([unpad_heads(q), unpad_heads(k), unpad_heads(v), f[:N_HEADS], rest], axis=0)


_SMALL = (("b_in", 3200), ("w_pool", 65536), ("b_pool", 512), ("pool_scale", 512),
          ("b_out", 1024), ("ln_g", 1024), ("ln_b", 1024), ("loss", 128))
_SMALL_ROWS = 576


def _pack_small(parts):
    flat = []
    for name, size in _SMALL:
        a = parts[name].reshape(-1)
        flat.append(jnp.pad(a, (0, size - a.shape[0])))
    flat = jnp.concatenate(flat)
    flat = jnp.pad(flat, (0, _SMALL_ROWS * LANE - flat.shape[0]))
    return flat.reshape(_SMALL_ROWS, LANE)


def _unpack_small(packed, shapes):
    flat = packed.reshape(-1)
    out, off = {}, 0
    for name, size in _SMALL:
        n = 1
        for s in shapes[name]:
            n *= s
        out[name] = flat[off:off + n].reshape(shapes[name])
        off += size
    return out


def kernel(x, c, w_ada, b_ada, w_in, b_in, w_pool_mix, b_pool_mix, pool_scale, w_out, b_out, ln_g, ln_b, loss_target, m_w_ada, m_b_ada, m_w_in, m_b_in, m_w_pool_mix, m_b_pool_mix, m_pool_scale, m_w_out, m_b_out, m_ln_g, m_ln_b, v_w_ada, v_b_ada, v_w_in, v_b_in, v_w_pool_mix, v_b_pool_mix, v_pool_scale, v_w_out, v_b_out, v_ln_g, v_ln_b):
    seq = x.shape[1]
    tile = min(256, seq)
    attn_tile = min(512, max(128, seq // 4))
    me = _dev_index(*_mesh_pos())
    x2, tgt = x[0], loss_target[0]

    rows_in = D_IN // N_DEV
    c_all, w_in_g, w_out_g = _all_gather(
        [jnp.pad(c, ((0, 7), (0, 0))), w_in[0].T.astype(BF16), w_out[0].astype(BF16)], "gather_weights")
    sc_all, ada_part = _ada_forward(c_all[:, 0, :], w_ada[0])
    (ada_mine,) = _exchange([ada_part.reshape(N_DEV, 1, -1)], [], "exchange_ada")
    ada = ada_mine.reshape(1, D_ADA) + b_ada
    shift, scale, gate = ada[:, 0:D], ada[:, D:2 * D], ada[:, 2 * D:]
    mod = jnp.concatenate([1.0 + scale, shift, jnp.zeros((6, D), F32)], axis=0)

    w_main, w_f = _rows_to_main(w_in_g.reshape(D_IN, D))
    b_main, b_f = _to_main(b_in)
    w_out_full = w_out_g.reshape(D, D)

    qp, kp, vp, f, p, g_att, g_pool, u = _inproj_forward(x2, mod, w_main, w_f, b_main, b_f, tile)
    att, q2t = _attention_forward(qp, kp, vp, attn_tile)

    vecs = jnp.concatenate([gate, b_out, ln_g, ln_b, jnp.zeros((4, D), F32)], axis=0)
    pool_vecs = jnp.concatenate([b_pool_mix.reshape(1, D_POOL), pool_scale, jnp.zeros((6, D_POOL), F32)], axis=0)
    dxa, do2, d_ga, d_gp, d_pooled, dw_out, dw_pool, dvec, dpvec = _middle(
        x2, tgt, att, g_att, g_pool, p, vecs, pool_vecs, w_out_full, w_pool_mix[0].astype(BF16),
        _head_placement(), tile)

    dqp, dkp, dvp, d_cum = _attention_backward(q2t, kp, vp, do2, attn_tile)
    dx, dproj, dw_f, db_main, db_f, dmod = _inproj_backward(
        dqp, dkp, dvp, d_cum, f, d_pooled, d_ga, d_gp, x2, dxa, u, mod, w_main, w_f, tile)
    k_tile = min(1024, seq)
    dw_q, dw_k, dw_v = (_weight_grad_heads(g, u, k_tile, "weight_grad_" + n)
                        for g, n in ((dqp, "q"), (dkp, "k"), (dvp, "v")))
    dw_rest = _weight_grad(dproj, u, k_tile)

    gw_in = _rows_from_main(dw_q, dw_k, dw_v, dw_f, dw_rest).reshape(N_DEV, rows_in, D).astype(BF16)
    gw_out = dw_out.reshape(N_DEV, D // N_DEV, D).astype(BF16)
    d_ada = jnp.concatenate([dmod[1:2], dmod[0:1], dvec[0:1]], axis=1)
    small = _pack_small({
        "b_in": _from_main(db_main[0:1], db_f[0:1]), "w_pool": dw_pool, "b_pool": dpvec[0:1],
        "pool_scale": dpvec[1:2], "b_out": dvec[1:2], "ln_g": dvec[2:3], "ln_b": dvec[3:4], "loss": dvec[4:5, 0:LANE]})
    g_in_rows, g_out, small_sum, d_ada_all = _reduce_grads(gw_in, gw_out, small, d_ada.reshape(D_ADA // LANE, LANE))

    def rows3(a):
        return a[0].T.reshape(rows_in, D // LANE, LANE)

    outs_in = _adamw_call(g_in_rows.reshape(rows_in, D // LANE, LANE), rows3(w_in), rows3(m_w_in), rows3(v_w_in),
                          rows_in // 5, "adamw_w_in")
    g_w_in, d_w_in, nm_w_in, nv_w_in = (a.reshape(rows_in, D).T for a in outs_in)
    g_w_out, d_w_out, nm_w_out, nv_w_out = _adamw_call(g_out, w_out[0], m_w_out[0], v_w_out[0], D // N_DEV, "adamw_w_out")

    zero = jnp.zeros((1,), F32)
    weights = {"b_in": b_in, "w_pool": w_pool_mix, "b_pool": b_pool_mix, "pool_scale": pool_scale,
               "b_out": b_out, "ln_g": ln_g, "ln_b": ln_b, "loss": zero}
    first = {"b_in": m_b_in, "w_pool": m_w_pool_mix, "b_pool": m_b_pool_mix, "pool_scale": m_pool_scale,
             "b_out": m_b_out, "ln_g": m_ln_g, "ln_b": m_ln_b, "loss": zero}
    second = {"b_in": v_b_in, "w_pool": v_w_pool_mix, "b_pool": v_b_pool_mix, "pool_scale": v_pool_scale,
              "b_out": v_b_out, "ln_g": v_ln_g, "ln_b": v_ln_b, "loss": zero}
    packed = _adamw_call(small_sum, _pack_small(weights), _pack_small(first), _pack_small(second),
                         _SMALL_ROWS, "adamw_small")
    shapes = {k: a.shape for k, a in weights.items()}
    g_s, d_s, nm_s, nv_s = (_unpack_small(a, shapes) for a in packed)
    loss = g_s["loss"][0]

    ada_rows = D_ADA // LANE
    b_ada_outs = _sum_adamw(d_ada_all, b_ada.reshape(ada_rows, LANE), m_b_ada.reshape(ada_rows, LANE),
                            v_b_ada.reshape(ada_rows, LANE), ada_rows, "adamw_b_ada")
    g_b_ada, d_b_ada, nm_b_ada, nv_b_ada = (a.reshape(1, D_ADA) for a in b_ada_outs)
    d_ada_local = lax.dynamic_slice_in_dim(d_ada_all.reshape(N_DEV, D_ADA), me * (D_ADA // N_DEV), D_ADA // N_DEV, axis=1)
    g_w_ada, d_w_ada, nm_w_ada, nv_w_ada = _ada_adamw(sc_all.T, d_ada_local, w_ada[0], m_w_ada[0], v_w_ada[0])

    def ordered(w_ada_, b_ada_, w_in_, w_out_, s):
        return (w_ada_[None], b_ada_, w_in_[None], s["b_in"], s["w_pool"], s["b_pool"], s["pool_scale"],
                w_out_[None], s["b_out"], s["ln_g"], s["ln_b"])

    return (loss, dx[None],
            *ordered(g_w_ada, g_b_ada, g_w_in, g_w_out, g_s),
            *ordered(d_w_ada, d_b_ada, d_w_in, d_w_out, d_s),
            *ordered(nm_w_ada, nm_b_ada, nm_w_in, nm_w_out, nm_s),
            *ordered(nv_w_ada, nv_b_ada, nv_w_in, nv_w_out, nv_s))
```

```python
import jax
import jax.numpy as jnp
from jax import lax
from jax.experimental import pallas as pl
from jax.experimental.pallas import tpu as pltpu

F32 = jnp.float32
BF16 = jnp.bfloat16

N_DEV = 8
D = 1024
N_HEADS = 8
HEAD_DIM = 64
D_ATT = 512
D_POOL = 512
POOL_WINDOWS = (2, 4, 8, 16)
GROUP_DIM = 128
HALO = 16
LANE = 128
D_IN = 3080
D_ADA = 3072
N_MAIN = 3072
OFF_P = 1536
COL_CHUNK = 512
Q_SCALE = 0.125
LN_EPS = 1e-5
ALPHA = 2.0 ** 0.25
L_CQ, L_CK, L_LSE = 64, 67, 70

ADAM_LR, ADAM_B1, ADAM_B2, ADAM_EPS, ADAM_WD, ADAM_STEP = 0.001, 0.9, 0.999, 1e-08, 0.01, 10
VMEM_LIMIT = 56 * 1024 * 1024

MESH = pl.DeviceIdType.MESH
ANY = pl.BlockSpec(memory_space=pl.ANY)


def _params(sem=None, vmem=VMEM_LIMIT):
    return pltpu.CompilerParams(dimension_semantics=sem, vmem_limit_bytes=vmem)


def _split3(a):
    hi = a.astype(BF16)
    r = a - hi.astype(F32)
    mid = r.astype(BF16)
    lo = (r - mid.astype(F32)).astype(BF16)
    return hi, mid, lo


def _dot(a, b):
    return jnp.dot(a, b, preferred_element_type=F32)


def _dot_nt(a, b):
    return lax.dot_general(a, b, (((1,), (1,)), ((), ())), preferred_element_type=F32)


def _dot_tn(a, b):
    return lax.dot_general(a, b, (((0,), (0,)), ((), ())), preferred_element_type=F32)


def _dot3(m01, a):
    hi, mid, lo = _split3(a)
    return _dot(m01, hi) + _dot(m01, mid) + _dot(m01, lo)


def _sigmoid(z):
    return 1.0 / (1.0 + jnp.exp(-z))


def _lanes(shape):
    return lax.broadcasted_iota(jnp.int32, shape, len(shape) - 1)


def _place3(lane, base, parts, other):
    out = other
    for j in range(3):
        out = jnp.where(lane == base + j, parts[j], out)
    return out


def _mesh_pos():
    return lax.axis_index("x"), lax.axis_index("y"), lax.axis_index("c")


def _dev_index(px, py, pc):
    return 4 * px + 2 * py + pc


def _all_gather(blocks, name):
    n = len(blocks)

    def body(*refs):
        ins, outs = refs[:n], refs[n:2 * n]
        send_sems, recv_sems, local_sems = refs[2 * n:]
        x, y, c = _mesh_pos()
        me, sibling = (x, y, c), (x, y, 1 - c)
        chips = [(1 - x, y), (x, 1 - y), (1 - x, 1 - y)]

        def copy(a, k, block, to, src=None):
            slot = outs[a].at[_dev_index(*block)]
            return pltpu.make_async_remote_copy(
                src_ref=slot if src is None else src, dst_ref=slot,
                send_sem=send_sems.at[a, k], recv_sem=recv_sems.at[a, k],
                device_id=to, device_id_type=MESH)

        mine = [pltpu.make_async_copy(ins[a], outs[a].at[_dev_index(*me)], local_sems.at[a]) for a in range(n)]
        for cp in mine:
            cp.start()
        first = []
        for a in range(n):
            first.append(copy(a, 0, me, sibling, src=ins[a]))
            first += [copy(a, 1 + j, me, (*chip, c), src=ins[a]) for j, chip in enumerate(chips)]
        for cp in first:
            cp.start()
        passed = []
        for j, chip in enumerate(chips):
            for a in range(n):
                copy(a, 1 + j, (*chip, c), me).wait_recv()
                fwd = copy(a, 4 + j, (*chip, c), sibling)
                fwd.start()
                passed.append(fwd)
        for a in range(n):
            copy(a, 0, sibling, me).wait_recv()
            for j, chip in enumerate(chips):
                copy(a, 4 + j, (*chip, 1 - c), me).wait_recv()
        for cp in first + passed:
            cp.wait_send()
        for cp in mine:
            cp.wait()

    return pl.pallas_call(
        body, name=name,
        out_shape=[jax.ShapeDtypeStruct((N_DEV,) + b.shape, b.dtype) for b in blocks],
        in_specs=[ANY] * n, out_specs=[ANY] * n,
        scratch_shapes=[pltpu.SemaphoreType.DMA((n, 7)), pltpu.SemaphoreType.DMA((n, 7)),
                        pltpu.SemaphoreType.DMA((n,))],
    )(*blocks)


def _exchange(scatter, gather, name):
    ns, n = len(scatter), len(scatter) + len(gather)
    arrays = list(scatter) + list(gather)

    def body(*refs):
        ins, outs = refs[:n], refs[n:2 * n]
        send_sems, recv_sems, local_sems = refs[2 * n:]
        x, y, c = _mesh_pos()
        me = _dev_index(x, y, c)
        peers = []
        for p in range(1, N_DEV):
            px, py, pc = (p >> 2) & 1, (p >> 1) & 1, p & 1
            peers.append((1 - x if px else x, 1 - y if py else y, 1 - c if pc else c))

        def src_for(a, slot):
            return ins[a].at[slot] if a < ns else ins[a]

        def copy(a, k, to):
            return pltpu.make_async_remote_copy(
                src_ref=src_for(a, _dev_index(*to)), dst_ref=outs[a].at[me],
                send_sem=send_sems.at[a, k], recv_sem=recv_sems.at[a, k],
                device_id=to, device_id_type=MESH)

        def arrival(a, k, frm):
            slot = _dev_index(*frm)
            return pltpu.make_async_remote_copy(
                src_ref=src_for(a, slot), dst_ref=outs[a].at[slot],
                send_sem=send_sems.at[a, k], recv_sem=recv_sems.at[a, k],
                device_id=frm, device_id_type=MESH)

        mine = [pltpu.make_async_copy(src_for(a, me), outs[a].at[me], local_sems.at[a]) for a in range(n)]
        for cp in mine:
            cp.start()
        sends = [copy(a, k, to) for k, to in enumerate(peers) for a in range(n)]
        for cp in sends:
            cp.start()
        for k, frm in enumerate(peers):
            for a in range(n):
                arrival(a, k, frm).wait_recv()
        for cp in sends:
            cp.wait_send()
        for cp in mine:
            cp.wait()

    out_shape = [jax.ShapeDtypeStruct(a.shape, a.dtype) for a in scatter]
    out_shape += [jax.ShapeDtypeStruct((N_DEV,) + a.shape, a.dtype) for a in gather]
    return pl.pallas_call(
        body, name=name, out_shape=out_shape,
        in_specs=[ANY] * n, out_specs=[ANY] * n,
        scratch_shapes=[pltpu.SemaphoreType.DMA((n, 7)), pltpu.SemaphoreType.DMA((n, 7)),
                        pltpu.SemaphoreType.DMA((n,))],
    )(*arrays)


def _reduce_grads(gw_in, gw_out, small, d_ada):
    shards = (gw_in, gw_out)

    def body(in0_ref, in1_ref, small_ref, dada_ref,
             g0_ref, g1_ref, total_ref, dall_ref,
             r1_0, r1_1, s2_0, s2_1, r2_0, r2_1, sm_sib, sm_chip, sm_recv, send_sems, recv_sems):
        ins, gs = (in0_ref, in1_ref), (g0_ref, g1_ref)
        r1, s2, r2 = (r1_0, r1_1), (s2_0, s2_1), (r2_0, r2_1)
        x, y, c = _mesh_pos()
        me = _dev_index(x, y, c)
        sibling = (x, y, 1 - c)
        chips = [(x, y), (1 - x, y), (x, 1 - y), (1 - x, 1 - y)]

        def remote(src, dst, k, to):
            return pltpu.make_async_remote_copy(src_ref=src, dst_ref=dst, send_sem=send_sems.at[k],
                                                recv_sem=recv_sems.at[k], device_id=to, device_id_type=MESH)

        sends = []

        def send(cp):
            cp.start()
            sends.append(cp)

        for a in range(2):
            for q, chip in enumerate(chips):
                send(remote(ins[a].at[_dev_index(*chip, 1 - c)], r1[a].at[q], 4 * a + q, sibling))
        send(remote(small_ref, sm_sib, 14, sibling))
        peers = []
        for p in range(1, N_DEV):
            px, py, pc = (p >> 2) & 1, (p >> 1) & 1, p & 1
            peers.append((1 - x if px else x, 1 - y if py else y, 1 - c if pc else c))
        for k, to in enumerate(peers):
            send(remote(dada_ref, dall_ref.at[me], 18 + k, to))
        dall_ref[me] = dada_ref[...]

        for a in range(2):
            own = None
            for q in (1, 2, 3, 0):
                chip = chips[q]
                remote(ins[a].at[0], r1[a].at[q], 4 * a + q, sibling).wait_recv()
                pair = ins[a][_dev_index(*chip, c)].astype(F32) + r1[a][q].astype(F32)
                if q == 0:
                    own = pair
                else:
                    s2[a][q - 1] = pair.astype(BF16)
                    send(remote(s2[a].at[q - 1], r2[a].at[q - 1], 8 + 3 * a + q - 1, (*chip, c)))
            for j in range(3):
                remote(s2[a].at[j], r2[a].at[j], 8 + 3 * a + j, sibling).wait_recv()
                own = own + r2[a][j].astype(F32)
            gs[a][...] = own

        remote(small_ref, sm_sib, 14, sibling).wait_recv()
        sm_chip[...] = small_ref[...] + sm_sib[...]
        for j in range(3):
            send(remote(sm_chip, sm_recv.at[j], 15 + j, (*chips[j + 1], c)))
        for j in range(3):
            remote(sm_chip, sm_recv.at[j], 15 + j, sibling).wait_recv()
        total = None
        for ax in range(2):
            for ay in range(2):
                dx, dy = x != ax, y != ay
                term = jnp.where(dx, jnp.where(dy, sm_recv[2], sm_recv[0]), jnp.where(dy, sm_recv[1], sm_chip[...]))
                total = term if total is None else total + term
        total_ref[...] = total

        for k, frm in enumerate(peers):
            remote(dada_ref, dall_ref.at[_dev_index(*frm)], 18 + k, frm).wait_recv()
        for cp in sends:
            cp.wait_send()

    def like(a, lead, dtype):
        return pltpu.VMEM((lead,) + a.shape[1:], dtype)

    return pl.pallas_call(
        body, name="reduce_grads",
        out_shape=[jax.ShapeDtypeStruct(gw_in.shape[1:], F32), jax.ShapeDtypeStruct(gw_out.shape[1:], F32),
                   jax.ShapeDtypeStruct(small.shape, F32), jax.ShapeDtypeStruct((N_DEV,) + d_ada.shape, F32)],
        scratch_shapes=[like(gw_in, 4, BF16), like(gw_out, 4, BF16), like(gw_in, 3, BF16), like(gw_out, 3, BF16),
                        like(gw_in, 3, BF16), like(gw_out, 3, BF16),
                        pltpu.VMEM(small.shape, F32), pltpu.VMEM(small.shape, F32), pltpu.VMEM((3,) + small.shape, F32),
                        pltpu.SemaphoreType.DMA((25,)), pltpu.SemaphoreType.DMA((25,))],
        compiler_params=_params(),
    )(*shards, small, d_ada)


def _ada_forward(c_all, w_ada):
    def body(c_ref, w_ref, sc_ref, part_ref):
        cc = c_ref[...]
        sc = cc * _sigmoid(cc)
        sc_ref[...] = sc
        part_ref[...] = _dot3_rhs(sc, w_ref[...])

    return pl.pallas_call(
        body, name="ada_forward",
        out_shape=[jax.ShapeDtypeStruct(c_all.shape, F32),
                   jax.ShapeDtypeStruct((N_DEV, w_ada.shape[1]), F32)],
        compiler_params=_params(),
    )(c_all, w_ada)


def _dot3_rhs(a, b):
    a0, a1, a2 = _split3(a)
    b0, b1, b2 = _split3(b)
    return (_dot(a0, b0) + (_dot(a0, b1) + _dot(a1, b0))
            + (_dot(a0, b2) + _dot(a1, b1) + _dot(a2, b0)))


def _inproj_forward(x, mod, w_main, w_f, b_main, b_f, tile):
    seq = x.shape[0]
    nt = seq // tile

    def body(x_ref, mod_ref, w_ref, wf_ref, b_ref, bf_ref,
             qp_ref, kp_ref, vp_ref, f_ref, p_ref, ga_ref, gp_ref, u_ref, carry_ref):
        i = pl.program_id(0)

        @pl.when(i == 0)
        def _():
            carry_ref[...] = jnp.zeros_like(carry_ref)

        u = x_ref[...] * mod_ref[0:1, :] + mod_ref[1:2, :]
        ub = u.astype(BF16)
        u_ref[...] = ub

        f = _dot_nt(ub, wf_ref[...]) + bf_ref[...]
        f_ref[...] = f
        lane = _lanes((tile, LANE))
        log_f = jnp.where(lane < N_HEADS, jnp.minimum(f, 0.0) - jnp.log(1.0 + jnp.exp(-jnp.abs(f))), 0.0)
        row = lax.broadcasted_iota(jnp.int32, (tile, tile), 0)
        col = lax.broadcasted_iota(jnp.int32, (tile, tile), 1)
        tri = (row >= col).astype(BF16)
        cum = _dot3(tri, log_f) + carry_ref[0:1, :]
        carry_ref[0:1, :] = cum[tile - 1:tile, :]
        cq = [part.astype(F32) for part in _split3(cum)]
        ck = [part.astype(F32) for part in _split3(-cum)]

        def proj(chunk):
            cols = pl.ds(chunk * COL_CHUNK, COL_CHUNK)
            return _dot_nt(ub, w_ref[cols, :]) + b_ref[:, cols]

        def head_tiles(r):
            for pair in range(N_HEADS // 2):
                both = r[:, pair * LANE:(pair + 1) * LANE]
                yield 2 * pair, both
                yield 2 * pair + 1, pltpu.roll(both, HEAD_DIM, 1)

        for h, val in head_tiles(proj(0)):
            extra = jnp.where((lane >= L_CK) & (lane < L_CK + 3), 1.0, 0.0)
            extra = _place3(lane, L_CQ, [part[:, h:h + 1] for part in cq], extra)
            qp_ref[h] = jnp.where(lane < HEAD_DIM, val * Q_SCALE, extra).astype(BF16)
        for h, val in head_tiles(proj(1)):
            ones = ((lane >= L_CQ) & (lane < L_CQ + 3)) | ((lane >= L_LSE) & (lane < L_LSE + 3))
            extra = _place3(lane, L_CK, [part[:, h:h + 1] for part in ck], jnp.where(ones, 1.0, 0.0))
            kp_ref[h] = jnp.where(lane < HEAD_DIM, val, extra).astype(BF16)
        for h, val in head_tiles(proj(2)):
            extra = jnp.where((lane >= HEAD_DIM) & (lane < HEAD_DIM + 3), -1.0, 0.0)
            vp_ref[h] = jnp.where(lane < HEAD_DIM, val, extra).astype(BF16)
        p_ref[...] = proj(3)
        ga_ref[...] = proj(4)
        gp_ref[...] = proj(5)

    head_block = pl.BlockSpec((N_HEADS, tile, LANE), lambda i: (0, i, 0))
    tok = lambda width: pl.BlockSpec((tile, width), lambda i: (i, 0))
    whole = lambda a: pl.BlockSpec(a.shape, lambda i: (0,) * a.ndim)
    padded = jax.ShapeDtypeStruct((N_HEADS, seq, LANE), BF16)
    half = jax.ShapeDtypeStruct((seq, D_ATT), F32)
    return pl.pallas_call(
        body, name="inproj_forward", grid=(nt,),
        in_specs=[tok(D), whole(mod), whole(w_main), whole(w_f), whole(b_main), whole(b_f)],
        out_specs=[head_block, head_block, head_block, tok(LANE), tok(D_POOL), tok(D_ATT), tok(D_POOL),
                   tok(D)],
        out_shape=[padded, padded, padded, jax.ShapeDtypeStruct((seq, LANE), F32), half, half, half,
                   jax.ShapeDtypeStruct((seq, D), BF16)],
        scratch_shapes=[pltpu.VMEM((8, LANE), F32)],
        compiler_params=_params(("arbitrary",)),
    )(x, mod, w_main, w_f, b_main, b_f)


def _attention_forward(qp, kp, vp, tile):
    seq = qp.shape[1]
    nb = seq // tile

    def body(q_ref, k_ref, v_ref, att_ref, q2t_ref, s_a, s_b):
        i = pl.program_id(1)
        sub = lax.broadcasted_iota(jnp.int32, (LANE, tile), 0)
        row = lax.broadcasted_iota(jnp.int32, (tile, tile), 0)
        col = lax.broadcasted_iota(jnp.int32, (tile, tile), 1)
        q = [q_ref[0], q_ref[1]]

        def scores(buf, kb):
            rows = pl.ds(pl.multiple_of(kb * tile, tile), tile)
            for hh in range(2):
                buf[hh] = _dot_nt(k_ref[hh, rows, :], q[hh])

        def absorb(buf, kb, carry, masked):
            rows = pl.ds(pl.multiple_of(kb * tile, tile), tile)
            out = []
            for hh in range(2):
                m, acc = carry[hh]
                s = buf[hh]
                if masked:
                    s = jnp.where(row <= col, s, -1e30)
                m_new = jnp.maximum(m, jnp.max(s, axis=0, keepdims=True))
                p = jnp.exp(s - m_new).astype(BF16)
                acc = jnp.exp(m - m_new) * acc + _dot_tn(v_ref[hh, rows, :], p)
                out.append((m_new, acc))
            return tuple(out)

        def two_blocks(j, carry):
            scores(s_b, 2 * j + 1)
            carry = absorb(s_a, 2 * j, carry, False)
            scores(s_a, 2 * j + 2)
            return absorb(s_b, 2 * j + 1, carry, False)

        def last_block(carry):
            return absorb(s_a, i, carry, True)

        def last_two_blocks(carry):
            scores(s_b, i)
            carry = absorb(s_a, i - 1, carry, False)
            return absorb(s_b, i, carry, True)

        scores(s_a, 0)
        init = (jnp.full((1, tile), -1e30, F32), jnp.zeros((LANE, tile), F32))
        carry = lax.fori_loop(0, i // 2, two_blocks, (init, init))
        carry = lax.cond(i % 2 == 0, last_block, last_two_blocks, carry)
        outs = []
        for hh in range(2):
            m, acc = carry[hh]
            l = -acc[HEAD_DIM:HEAD_DIM + 1, :]
            outs.append((acc / l)[:HEAD_DIM, :])
            neg_lse = [part.astype(F32) for part in _split3(-(m + jnp.log(l)))]
            q2t_ref[hh] = _place3(sub, L_LSE, neg_lse, q[hh].astype(F32).T).astype(BF16)
        att_ref[...] = jnp.concatenate(outs, axis=0).T

    pair = pl.BlockSpec((2, tile, LANE), lambda hp, i: (hp, i, 0))
    full = pl.BlockSpec((2, seq, LANE), lambda hp, i: (hp, 0, 0))
    return pl.pallas_call(
        body, name="attention_forward", grid=(N_HEADS // 2, nb),
        in_specs=[pair, full, full],
        out_specs=[pl.BlockSpec((tile, LANE), lambda hp, i: (i, hp)),
                   pl.BlockSpec((2, LANE, tile), lambda hp, i: (hp, 0, i))],
        out_shape=[jax.ShapeDtypeStruct((seq, D_ATT), F32),
                   jax.ShapeDtypeStruct((N_HEADS, LANE, seq), BF16)],
        scratch_shapes=[pltpu.VMEM((2, tile, tile), F32), pltpu.VMEM((2, tile, tile), F32)],
        compiler_params=_params(("arbitrary", "arbitrary")),
    )(qp, kp, vp)


def _window_matrices(tile, window, transposed):
    r = lax.broadcasted_iota(jnp.int32, (tile, tile), 0)
    c = lax.broadcasted_iota(jnp.int32, (tile, tile), 1)
    rh = lax.broadcasted_iota(jnp.int32, (tile, HALO), 0)
    ch = lax.broadcasted_iota(jnp.int32, (tile, HALO), 1)
    if not transposed:
        cur = (c <= r) & (r - c < window)
        halo = (rh + HALO - ch) < window
    else:
        cur = (r <= c) & (c - r < window)
        halo = (tile + ch - rh) < window
    return cur.astype(BF16), halo.astype(BF16)


def _silu_parts(g):
    sig = _sigmoid(g)
    return g * sig, sig * (1.0 + g * (1.0 - sig))


def _head_placement():
    n = jnp.arange(N_HEADS * LANE)[:, None]
    r = jnp.arange(D_ATT + 3 * LANE)[None, :]
    h, l = n // LANE, n % LANE
    src = jnp.where(l < HEAD_DIM, HEAD_DIM * h + l, D_ATT + LANE * (l - HEAD_DIM) + h)
    return ((r == src) & (l < HEAD_DIM + 3)).astype(BF16)


def _middle(x, tgt, att, g_att, g_pool, p, vecs, pool_vecs, w_out, w_pool, place, tile):
    seq = x.shape[0]
    nt = seq // tile
    halo_blocks = tile // HALO

    def body(x_ref, tgt_ref, att_ref, ga_ref, gp_ref, p_ref, ph_ref, vec_ref, pvec_ref, wo_ref, wp_ref, place_ref,
             dxa_ref, do2_ref, dga_ref, dgp_ref, dpooled_ref, dwo_ref, dwp_ref, dvec_ref, dpvec_ref):
        i = pl.program_id(0)

        @pl.when(i == 0)
        def _():
            dwo_ref[...] = jnp.zeros_like(dwo_ref)
            dwp_ref[...] = jnp.zeros_like(dwp_ref)
            dvec_ref[...] = jnp.zeros_like(dvec_ref)
            dpvec_ref[...] = jnp.zeros_like(dpvec_ref)

        gate, b_out, ln_g, ln_b = (vec_ref[k:k + 1, :] for k in range(4))
        b_pool, pool_scale = pvec_ref[0:1, :], pvec_ref[1:2, :]
        x = x_ref[...]
        p = p_ref[...]
        p_halo = ph_ref[...] * jnp.where(i > 0, 1.0, 0.0)
        pos = i * tile + lax.broadcasted_iota(jnp.int32, (tile, 1), 0) + 1

        pooled, mixed = [], []
        for g, window in enumerate(POOL_WINDOWS):
            cols = slice(g * GROUP_DIM, (g + 1) * GROUP_DIM)
            m_cur, m_halo = _window_matrices(tile, window, False)
            wsum = _dot3(m_cur, p[:, cols]) + _dot3(m_halo, p_halo[:, cols])
            count = jnp.minimum(pos, window).astype(F32)
            pooled.append(wsum / count - p[:, cols])
            mixed.append(_dot(pooled[g].astype(BF16), wp_ref[g]) + b_pool[:, cols])
        mixed = jnp.concatenate(mixed, axis=1)
        pool = mixed * pool_scale

        att = att_ref[...]
        g_att, g_pool = ga_ref[...], gp_ref[...]
        silu_a, dsilu_a = _silu_parts(g_att)
        silu_p, dsilu_p = _silu_parts(g_pool)
        y_in = jnp.concatenate([att * silu_a, pool * silu_p], axis=1)
        y = _dot(y_in.astype(BF16), wo_ref[...]) + b_out
        h = ALPHA * x + gate * y
        mu = jnp.mean(h, axis=1, keepdims=True)
        hc = h - mu
        var = jnp.mean(hc * hc, axis=1, keepdims=True)
        rstd = lax.rsqrt(var + LN_EPS)
        yhat = hc * rstd
        diff = yhat * ln_g + ln_b - tgt_ref[...]
        loss_rows = jnp.sum(diff * diff, axis=1, keepdims=True)
        d_out = diff * (1.0 / D)

        d_yhat = d_out * ln_g
        dh = rstd * (d_yhat - jnp.mean(d_yhat, axis=1, keepdims=True)
                     - yhat * jnp.mean(d_yhat * yhat, axis=1, keepdims=True))
        dxa_ref[...] = ALPHA * dh
        dy = dh * gate
        dyb = dy.astype(BF16)
        lane = _lanes((1, D))
        loss_row = jnp.where(lane == 0, (0.5 / D) * jnp.sum(loss_rows, axis=0, keepdims=True), 0.0)
        dvec_ref[0:1, :] += jnp.sum(dh * y, axis=0, keepdims=True)
        dvec_ref[1:2, :] += jnp.sum(dy, axis=0, keepdims=True)
        dvec_ref[2:3, :] += jnp.sum(d_out * yhat, axis=0, keepdims=True)
        dvec_ref[3:4, :] += jnp.sum(d_out, axis=0, keepdims=True)
        dvec_ref[4:5, :] += loss_row

        dwo_ref[...] += _dot(y_in.T.astype(BF16), dyb)
        d_yin = _dot_nt(dyb, wo_ref[...])
        d_a, d_pl = d_yin[:, :D_ATT], d_yin[:, D_ATT:]
        d_att = d_a * silu_a
        r = lax.broadcasted_iota(jnp.int32, (D_ATT, LANE), 0)
        c = lax.broadcasted_iota(jnp.int32, (D_ATT, LANE), 1)
        head_of = ((r >= HEAD_DIM * c) & (r < HEAD_DIM * c + HEAD_DIM)).astype(BF16)
        delta = sum(_dot(part, head_of) for part in _split3(d_att * att))
        do2t = _dot_nt(place_ref[...], jnp.concatenate([d_att.astype(BF16), *_split3(delta)], axis=1))
        for h in range(N_HEADS):
            do2_ref[h] = do2t[h * LANE:(h + 1) * LANE, :].astype(BF16)
        dga_ref[...] = d_a * att * dsilu_a
        dgp_ref[...] = d_pl * pool * dsilu_p
        d_pool = d_pl * silu_p
        d_mixed = d_pool * pool_scale
        dpvec_ref[0:1, :] += jnp.sum(d_mixed, axis=0, keepdims=True)
        dpvec_ref[1:2, :] += jnp.sum(d_pool * mixed, axis=0, keepdims=True)
        d_pooled = []
        for g in range(len(POOL_WINDOWS)):
            cols = slice(g * GROUP_DIM, (g + 1) * GROUP_DIM)
            dmb = d_mixed[:, cols].astype(BF16)
            dwp_ref[g] += _dot(pooled[g].T.astype(BF16), dmb)
            d_pooled.append(_dot_nt(dmb, wp_ref[g]))
        dpooled_ref[...] = jnp.concatenate(d_pooled, axis=1)

    tok = lambda width: pl.BlockSpec((tile, width), lambda i: (i, 0))
    whole = lambda a: pl.BlockSpec(a.shape, lambda i: (0,) * a.ndim)
    halo = pl.BlockSpec((HALO, D_POOL), lambda i: (jnp.maximum(i * halo_blocks - 1, 0), 0))
    half = jax.ShapeDtypeStruct((seq, D_ATT), F32)
    outs = [jax.ShapeDtypeStruct((seq, D), F32), jax.ShapeDtypeStruct((N_HEADS, LANE, seq), BF16), half, half, half,
            jax.ShapeDtypeStruct(w_out.shape, F32), jax.ShapeDtypeStruct(w_pool.shape, F32),
            jax.ShapeDtypeStruct(vecs.shape, F32), jax.ShapeDtypeStruct(pool_vecs.shape, F32)]
    return pl.pallas_call(
        body, name="middle", grid=(nt,),
        in_specs=[tok(D), tok(D), tok(D_ATT), tok(D_ATT), tok(D_POOL), tok(D_POOL), halo,
                  whole(vecs), whole(pool_vecs), whole(w_out), whole(w_pool), whole(place)],
        out_specs=[tok(D), pl.BlockSpec((N_HEADS, LANE, tile), lambda i: (0, 0, i)),
                   tok(D_ATT), tok(D_POOL), tok(D_POOL),
                   whole(w_out), whole(w_pool), whole(vecs), whole(pool_vecs)],
        out_shape=outs,
        compiler_params=_params(("arbitrary",)),
    )(x, tgt, att, g_att, g_pool, p, p, vecs, pool_vecs, w_out, w_pool, place)


def _attention_backward(q2t, kp, vp, do2t, tile):
    seq = kp.shape[1]
    nb = seq // tile

    def body(qt_ref, k_ref, v_ref, dot_ref, dq_ref, dk_ref, dv_ref, dcum_ref, dq_acc, s_a, dp_a, s_b, dp_b):
        row = lax.broadcasted_iota(jnp.int32, (tile, tile), 0)
        col = lax.broadcasted_iota(jnp.int32, (tile, tile), 1)
        dq_acc[...] = jnp.zeros_like(dq_acc)

        def kv_block(kb, _):
            krows = pl.ds(pl.multiple_of(kb * tile, tile), tile)
            k = [k_ref[hh, krows, :] for hh in range(2)]
            v = [v_ref[hh, krows, :] for hh in range(2)]
            k_t = [k[hh].T for hh in range(2)]

            def scores(s_buf, dp_buf, qb):
                qcols = pl.ds(pl.multiple_of(qb * tile, tile), tile)
                for hh in range(2):
                    s_buf[hh] = _dot(k[hh], qt_ref[hh, :, qcols])
                    dp_buf[hh] = _dot(v[hh], dot_ref[hh, :, qcols])

            def absorb(s_buf, dp_buf, qb, carry, masked):
                qcols = pl.ds(pl.multiple_of(qb * tile, tile), tile)
                out = []
                for hh in range(2):
                    dk, dv = carry[hh]
                    s_t = s_buf[hh]
                    if masked:
                        s_t = jnp.where(row <= col, s_t, -1e30)
                    p_t = jnp.exp(s_t)
                    ds_t = (p_t * dp_buf[hh]).astype(BF16)
                    dv = dv + _dot_nt(dot_ref[hh, :, qcols], p_t.astype(BF16))
                    dk = dk + _dot_nt(qt_ref[hh, :, qcols], ds_t)
                    dq_acc[hh, :, qcols] += _dot(k_t[hh], ds_t)
                    out.append((dk, dv))
                return tuple(out)

            def two_blocks(j, carry):
                qb = kb + 1 + 2 * j
                scores(s_a, dp_a, qb + 1)
                carry = absorb(s_b, dp_b, qb, carry, False)
                scores(s_b, dp_b, qb + 2)
                return absorb(s_a, dp_a, qb + 1, carry, False)

            def one_left(carry):
                return absorb(s_b, dp_b, nb - 1, carry, False)

            def two_left(carry):
                scores(s_a, dp_a, nb - 1)
                carry = absorb(s_b, dp_b, nb - 2, carry, False)
                return absorb(s_a, dp_a, nb - 1, carry, False)

            scores(s_a, dp_a, kb)
            scores(s_b, dp_b, jnp.minimum(kb + 1, nb - 1))
            zero = jnp.zeros((LANE, tile), F32)
            carry = absorb(s_a, dp_a, kb, ((zero, zero), (zero, zero)), True)
            rest = nb - 1 - kb
            trips = jnp.maximum(rest - 1, 0) // 2
            carry = lax.fori_loop(0, trips, two_blocks, carry)
            left = rest - 2 * trips
            carry = lax.cond(left == 0, lambda cr: cr,
                             lambda cr: lax.cond(left == 1, one_left, two_left, cr), carry)
            for hh in range(2):
                dk = carry[hh][0]
                dk_ref[hh, :, krows] = dk.astype(BF16)
                dv_ref[hh, :, krows] = carry[hh][1].astype(BF16)
                dcum_ref[hh, :, krows] = -dk[L_CK:L_CK + 1, :]
            return 0

        lax.fori_loop(0, nb, kv_block, 0)
        for hh in range(2):
            dq = dq_acc[hh]
            dcum_ref[hh] += dq[L_CQ:L_CQ + 1, :]
            dq_ref[hh] = (dq * Q_SCALE).astype(BF16)

    pair = pl.BlockSpec((2, seq, LANE), lambda hp: (hp, 0, 0))
    pair_t = pl.BlockSpec((2, LANE, seq), lambda hp: (hp, 0, 0))
    grad = jax.ShapeDtypeStruct((N_HEADS, LANE, seq), BF16)
    return pl.pallas_call(
        body, name="attention_backward", grid=(N_HEADS // 2,),
        in_specs=[pair_t, pair, pair, pair_t],
        out_specs=[pair_t, pair_t, pair_t, pl.BlockSpec((2, 1, seq), lambda hp: (hp, 0, 0))],
        out_shape=[grad, grad, grad, jax.ShapeDtypeStruct((N_HEADS, 1, seq), F32)],
        scratch_shapes=[pltpu.VMEM((2, LANE, seq), F32)] + [pltpu.VMEM((2, tile, tile), F32)] * 4,
        compiler_params=_params(("arbitrary",)),
    )(q2t, kp, vp, do2t)


def _inproj_backward(dqp, dkp, dvp, d_cum, f, d_pooled, d_ga, d_gp, x, dxa, u, mod, w_main, w_f, tile):
    seq = x.shape[0]
    nt = seq // tile
    halo_blocks = tile // HALO

    def body(dq_ref, dk_ref, dv_ref, dcum_ref, f_ref, dpo_ref, dph_ref, dga_ref, dgp_ref, x_ref, dxa_ref, u_ref,
             mod_ref, w_ref, wf_ref,
             dx_ref, dproj_ref, dwf_ref, db_ref, dbf_ref, dmod_ref, carry_ref):
        step = pl.program_id(0)
        i = nt - 1 - step

        @pl.when(step == 0)
        def _():
            carry_ref[...] = jnp.zeros_like(carry_ref)
            dwf_ref[...] = jnp.zeros_like(dwf_ref)
            db_ref[...] = jnp.zeros_like(db_ref)
            dbf_ref[...] = jnp.zeros_like(dbf_ref)
            dmod_ref[...] = jnp.zeros_like(dmod_ref)

        ones = jnp.ones((8, tile), BF16)

        def emit(chunk, val):
            cols = pl.ds(chunk * COL_CHUNK, COL_CHUNK)
            db_ref[0:1, cols] += jnp.sum(val, axis=0, keepdims=True)
            vb = val.astype(BF16)
            dproj_ref[:, pl.ds((chunk - 3) * COL_CHUNK, COL_CHUNK)] = vb
            return _dot(vb, w_ref[cols, :])

        d_u = jnp.zeros((tile, D), F32)
        for chunk, ref in enumerate((dq_ref, dk_ref, dv_ref)):
            cols = pl.ds(chunk * COL_CHUNK, COL_CHUNK)
            val_t = ref[:, 0:HEAD_DIM, :].reshape(COL_CHUNK, tile)
            db_ref[:, cols] += _dot_nt(ones, val_t)
            d_u += _dot_tn(val_t, w_ref[cols, :])

        d_pooled = dpo_ref[...]
        d_halo = dph_ref[...] * jnp.where(i < nt - 1, 1.0, 0.0)
        pos = i * tile + lax.broadcasted_iota(jnp.int32, (tile, 1), 0) + 1
        d_p = []
        for g, window in enumerate(POOL_WINDOWS):
            cols = slice(g * GROUP_DIM, (g + 1) * GROUP_DIM)
            m_cur, m_halo = _window_matrices(tile, window, True)
            scaled = d_pooled[:, cols] / jnp.minimum(pos, window).astype(F32)
            d_p.append(_dot3(m_cur, scaled) + _dot3(m_halo, d_halo[:, cols] * (1.0 / window)) - d_pooled[:, cols])
        d_u += emit(3, jnp.concatenate(d_p, axis=1))
        d_u += emit(4, dga_ref[...])
        d_u += emit(5, dgp_ref[...])

        row = lax.broadcasted_iota(jnp.int32, (tile, tile), 0)
        col = lax.broadcasted_iota(jnp.int32, (tile, tile), 1)
        later = (row >= col).astype(BF16)
        d_logf = sum(_dot(part, later) for part in _split3(dcum_ref[:, 0, :])) + carry_ref[:, 0:1]
        carry_ref[:, 0:1] = d_logf[:, 0:1]
        d_f = d_logf * _sigmoid(-f_ref[...].T[0:N_HEADS, :])
        d_f = jnp.concatenate([d_f, jnp.zeros((LANE - N_HEADS, tile), F32)], axis=0)
        dbf_ref[...] += sum(_dot_nt(ones, part) for part in _split3(d_f))
        d_fb = d_f.astype(BF16)
        d_u += _dot_tn(d_fb, wf_ref[...])
        dwf_ref[...] += _dot(d_fb, u_ref[...])

        x = x_ref[...]
        dx_ref[...] = dxa_ref[...] + d_u * mod_ref[0:1, :]
        dmod_ref[0:1, :] += jnp.sum(d_u * x, axis=0, keepdims=True)
        dmod_ref[1:2, :] += jnp.sum(d_u, axis=0, keepdims=True)

    rev = lambda step: nt - 1 - step
    tok = lambda width: pl.BlockSpec((tile, width), lambda s: (rev(s), 0))
    head_block = pl.BlockSpec((N_HEADS, LANE, tile), lambda s: (0, 0, rev(s)))
    whole = lambda a: pl.BlockSpec(a.shape, lambda s: (0,) * a.ndim)
    halo = pl.BlockSpec((HALO, D_POOL), lambda s: (jnp.minimum((rev(s) + 1) * halo_blocks, seq // HALO - 1), 0))
    small = lambda width: jax.ShapeDtypeStruct((8, width), F32)
    n_rest = N_MAIN - OFF_P
    return pl.pallas_call(
        body, name="inproj_backward", grid=(nt,),
        in_specs=[head_block, head_block, head_block, pl.BlockSpec((N_HEADS, 1, tile), lambda s: (0, 0, rev(s))),
                  tok(LANE), tok(D_POOL), halo, tok(D_ATT), tok(D_POOL),
                  tok(D), tok(D), tok(D),
                  whole(mod), whole(w_main), whole(w_f)],
        out_specs=[tok(D), tok(n_rest), pl.BlockSpec((LANE, D), lambda s: (0, 0)),
                   pl.BlockSpec((8, N_MAIN), lambda s: (0, 0)), pl.BlockSpec((8, LANE), lambda s: (0, 0)),
                   pl.BlockSpec((8, D), lambda s: (0, 0))],
        out_shape=[jax.ShapeDtypeStruct((seq, D), F32), jax.ShapeDtypeStruct((seq, n_rest), BF16),
                   jax.ShapeDtypeStruct((LANE, D), F32), small(N_MAIN), small(LANE), small(D)],
        scratch_shapes=[pltpu.VMEM((8, LANE), F32)],
        compiler_params=_params(("arbitrary",)),
    )(dqp, dkp, dvp, d_cum, f, d_pooled, d_pooled, d_ga, d_gp, x, dxa, u, mod, w_main, w_f)


def _weight_grad(dproj, u, k_tile):
    seq, n_cols = dproj.shape
    nk = seq // k_tile

    def body(dp_ref, u_ref, out_ref):
        @pl.when(pl.program_id(1) == 0)
        def _():
            out_ref[...] = jnp.zeros_like(out_ref)

        out_ref[...] += _dot_tn(dp_ref[...], u_ref[...])

    return pl.pallas_call(
        body, name="weight_grad", grid=(n_cols // COL_CHUNK, nk),
        in_specs=[pl.BlockSpec((k_tile, COL_CHUNK), lambda n, k: (k, n)),
                  pl.BlockSpec((k_tile, D), lambda n, k: (k, 0))],
        out_specs=pl.BlockSpec((COL_CHUNK, D), lambda n, k: (n, 0)),
        out_shape=jax.ShapeDtypeStruct((n_cols, D), F32),
        compiler_params=_params(("arbitrary", "arbitrary")),
    )(dproj, u)


def _weight_grad_heads(grad_t, u, k_tile, name):
    seq = u.shape[0]
    nk = seq // k_tile

    def body(g_ref, u_ref, out_ref):
        @pl.when(pl.program_id(0) == 0)
        def _():
            out_ref[...] = jnp.zeros_like(out_ref)

        out_ref[...] += _dot(g_ref[...].reshape(N_HEADS * HEAD_DIM, k_tile), u_ref[...])

    return pl.pallas_call(
        body, name=name, grid=(nk,),
        in_specs=[pl.BlockSpec((N_HEADS, HEAD_DIM, k_tile), lambda k: (0, 0, k)),
                  pl.BlockSpec((k_tile, D), lambda k: (k, 0))],
        out_specs=pl.BlockSpec((N_HEADS * HEAD_DIM, D), lambda k: (0, 0)),
        out_shape=jax.ShapeDtypeStruct((N_HEADS * HEAD_DIM, D), F32),
        compiler_params=_params(("arbitrary",)),
    )(grad_t, u)


def _adamw(w, g, m, v):
    m = ADAM_B1 * m + (1.0 - ADAM_B1) * g
    v = ADAM_B2 * v + (1.0 - ADAM_B2) * (g * g)
    m_hat = m / (1.0 - ADAM_B1 ** ADAM_STEP)
    v_hat = v / (1.0 - ADAM_B2 ** ADAM_STEP)
    delta = -ADAM_LR * (m_hat / (jnp.sqrt(v_hat) + ADAM_EPS) + ADAM_WD * w)
    return delta, m, v


def _adamw_call(g, w, m, v, lead_tile, name):
    nr = w.shape[0] // lead_tile

    def body(gi_ref, w_ref, m_ref, v_ref, g_ref, d_ref, nm_ref, nv_ref):
        g = gi_ref[...]
        g_ref[...] = g
        d_ref[...], nm_ref[...], nv_ref[...] = _adamw(w_ref[...], g, m_ref[...], v_ref[...])

    blk = pl.BlockSpec((lead_tile,) + w.shape[1:], lambda r: (r,) + (0,) * (w.ndim - 1))
    shape = jax.ShapeDtypeStruct(w.shape, F32)
    return pl.pallas_call(
        body, name=name, grid=(nr,),
        in_specs=[blk, blk, blk, blk], out_specs=[blk, blk, blk, blk],
        out_shape=[shape, shape, shape, shape],
        compiler_params=_params(("arbitrary",)),
    )(g, w, m, v)


def _sum_adamw(parts, w, m, v, row_tile, name):
    rows, cols = w.shape
    nr = rows // row_tile

    def body(parts_ref, w_ref, m_ref, v_ref, g_ref, d_ref, nm_ref, nv_ref):
        g = parts_ref[0]
        for k in range(1, N_DEV):
            g = g + parts_ref[k]
        g_ref[...] = g
        d_ref[...], nm_ref[...], nv_ref[...] = _adamw(w_ref[...], g, m_ref[...], v_ref[...])

    blk = pl.BlockSpec((row_tile, cols), lambda r: (r, 0))
    shape = jax.ShapeDtypeStruct(w.shape, F32)
    return pl.pallas_call(
        body, name=name, grid=(nr,),
        in_specs=[pl.BlockSpec((N_DEV, row_tile, cols), lambda r: (0, r, 0)), blk, blk, blk],
        out_specs=[blk, blk, blk, blk],
        out_shape=[shape, shape, shape, shape],
        compiler_params=_params(("arbitrary",)),
    )(parts, w, m, v)


def _ada_adamw(sc_t, d_ada, w, m, v):
    def body(sc_ref, d_ref, w_ref, m_ref, v_ref, g_ref, dl_ref, nm_ref, nv_ref):
        g = sc_ref[:, 0:1] * d_ref[0:1, :]
        for b in range(1, N_DEV):
            g = g + sc_ref[:, b:b + 1] * d_ref[b:b + 1, :]
        g_ref[...] = g
        dl_ref[...], nm_ref[...], nv_ref[...] = _adamw(w_ref[...], g, m_ref[...], v_ref[...])

    shape = jax.ShapeDtypeStruct(w.shape, F32)
    return pl.pallas_call(
        body, name="ada_adamw", out_shape=[shape, shape, shape, shape], compiler_params=_params(),
    )(sc_t, d_ada, w, m, v)


F_LO, F_HI = 3 * D_ATT, 3 * D_ATT + N_HEADS


def _split_forget(a, axis):
    idx = lambda lo, hi: tuple(slice(lo, hi) if d == axis else slice(None) for d in range(a.ndim))
    pad = [(0, LANE - N_HEADS) if d == axis else (0, 0) for d in range(a.ndim)]
    return jnp.concatenate([a[idx(0, F_LO)], a[idx(F_HI, D_IN)]], axis=axis), jnp.pad(a[idx(F_LO, F_HI)], pad)


def _join_forget(main, f, axis):
    idx = lambda lo, hi: tuple(slice(lo, hi) if d == axis else slice(None) for d in range(main.ndim))
    return jnp.concatenate([main[idx(0, F_LO)], f[idx(0, N_HEADS)], main[idx(F_LO, N_MAIN)]], axis=axis)


_SMALL = (("b_in", 3200), ("w_pool", 65536), ("b_pool", 512), ("pool_scale", 512),
          ("b_out", 1024), ("ln_g", 1024), ("ln_b", 1024), ("loss", 128))
_SMALL_ROWS = 576


def _pack_small(parts):
    flat = []
    for name, size in _SMALL:
        a = parts[name].reshape(-1)
        flat.append(jnp.pad(a, (0, size - a.shape[0])))
    flat = jnp.concatenate(flat)
    flat = jnp.pad(flat, (0, _SMALL_ROWS * LANE - flat.shape[0]))
    return flat.reshape(_SMALL_ROWS, LANE)


def _unpack_small(packed, shapes):
    flat = packed.reshape(-1)
    out, off = {}, 0
    for name, size in _SMALL:
        n = 1
        for s in shapes[name]:
            n *= s
        out[name] = flat[off:off + n].reshape(shapes[name])
        off += size
    return out


def kernel(x, c, w_ada, b_ada, w_in, b_in, w_pool_mix, b_pool_mix, pool_scale, w_out, b_out, ln_g, ln_b, loss_target, m_w_ada, m_b_ada, m_w_in, m_b_in, m_w_pool_mix, m_b_pool_mix, m_pool_scale, m_w_out, m_b_out, m_ln_g, m_ln_b, v_w_ada, v_b_ada, v_w_in, v_b_in, v_w_pool_mix, v_b_pool_mix, v_pool_scale, v_w_out, v_b_out, v_ln_g, v_ln_b):
    seq = x.shape[1]
    tile = min(256, seq)
    attn_tile = min(512, max(128, seq // 4))
    me = _dev_index(*_mesh_pos())
    x2, tgt = x[0], loss_target[0]

    rows_in = D_IN // N_DEV
    c_all, w_in_g, w_out_g = _all_gather(
        [jnp.pad(c, ((0, 7), (0, 0))), w_in[0].T.astype(BF16), w_out[0].astype(BF16)], "gather_weights")
    sc_all, ada_part = _ada_forward(c_all[:, 0, :], w_ada[0])
    (ada_mine,) = _exchange([ada_part.reshape(N_DEV, 1, -1)], [], "exchange_ada")
    ada = ada_mine.reshape(1, D_ADA) + b_ada
    shift, scale, gate = ada[:, 0:D], ada[:, D:2 * D], ada[:, 2 * D:]
    mod = jnp.concatenate([1.0 + scale, shift, jnp.zeros((6, D), F32)], axis=0)

    w_main, w_f = _split_forget(w_in_g.reshape(D_IN, D), 0)
    b_main, b_f = _split_forget(b_in, 1)
    w_out_full = w_out_g.reshape(D, D)

    qp, kp, vp, f, p, g_att, g_pool, u = _inproj_forward(x2, mod, w_main, w_f, b_main, b_f, tile)
    att, q2t = _attention_forward(qp, kp, vp, attn_tile)

    vecs = jnp.concatenate([gate, b_out, ln_g, ln_b, jnp.zeros((4, D), F32)], axis=0)
    pool_vecs = jnp.concatenate([b_pool_mix.reshape(1, D_POOL), pool_scale, jnp.zeros((6, D_POOL), F32)], axis=0)
    dxa, do2, d_ga, d_gp, d_pooled, dw_out, dw_pool, dvec, dpvec = _middle(
        x2, tgt, att, g_att, g_pool, p, vecs, pool_vecs, w_out_full, w_pool_mix[0].astype(BF16),
        _head_placement(), tile)

    dqp, dkp, dvp, d_cum = _attention_backward(q2t, kp, vp, do2, attn_tile)
    dx, dproj, dw_f, db_main, db_f, dmod = _inproj_backward(
        dqp, dkp, dvp, d_cum, f, d_pooled, d_ga, d_gp, x2, dxa, u, mod, w_main, w_f, tile)
    k_tile = min(1024, seq)
    dw_q, dw_k, dw_v = (_weight_grad_heads(g, u, k_tile, "weight_grad_" + n)
                        for g, n in ((dqp, "q"), (dkp, "k"), (dvp, "v")))
    dw_rest = _weight_grad(dproj, u, k_tile)

    dw_main = jnp.concatenate([dw_q, dw_k, dw_v, dw_rest], axis=0)
    gw_in = _join_forget(dw_main, dw_f, 0).reshape(N_DEV, rows_in, D).astype(BF16)
    gw_out = dw_out.reshape(N_DEV, D // N_DEV, D).astype(BF16)
    d_ada = jnp.concatenate([dmod[1:2], dmod[0:1], dvec[0:1]], axis=1)
    small = _pack_small({
        "b_in": _join_forget(db_main[0:1], db_f[0:1], 1), "w_pool": dw_pool, "b_pool": dpvec[0:1],
        "pool_scale": dpvec[1:2], "b_out": dvec[1:2], "ln_g": dvec[2:3], "ln_b": dvec[3:4], "loss": dvec[4:5, 0:LANE]})
    g_in_rows, g_out, small_sum, d_ada_all = _reduce_grads(gw_in, gw_out, small, d_ada.reshape(D_ADA // LANE, LANE))

    def rows3(a):
        return a[0].T.reshape(rows_in, D // LANE, LANE)

    outs_in = _adamw_call(g_in_rows.reshape(rows_in, D // LANE, LANE), rows3(w_in), rows3(m_w_in), rows3(v_w_in),
                          rows_in // 5, "adamw_w_in")
    g_w_in, d_w_in, nm_w_in, nv_w_in = (a.reshape(rows_in, D).T for a in outs_in)
    g_w_out, d_w_out, nm_w_out, nv_w_out = _adamw_call(g_out, w_out[0], m_w_out[0], v_w_out[0], D // N_DEV, "adamw_w_out")

    zero = jnp.zeros((1,), F32)
    weights = {"b_in": b_in, "w_pool": w_pool_mix, "b_pool": b_pool_mix, "pool_scale": pool_scale,
               "b_out": b_out, "ln_g": ln_g, "ln_b": ln_b, "loss": zero}
    first = {"b_in": m_b_in, "w_pool": m_w_pool_mix, "b_pool": m_b_pool_mix, "pool_scale": m_pool_scale,
             "b_out": m_b_out, "ln_g": m_ln_g, "ln_b": m_ln_b, "loss": zero}
    second = {"b_in": v_b_in, "w_pool": v_w_pool_mix, "b_pool": v_b_pool_mix, "pool_scale": v_pool_scale,
              "b_out": v_b_out, "ln_g": v_ln_g, "ln_b": v_ln_b, "loss": zero}
    packed = _adamw_call(small_sum, _pack_small(weights), _pack_small(first), _pack_small(second),
                         _SMALL_ROWS, "adamw_small")
    shapes = {k: a.shape for k, a in weights.items()}
    g_s, d_s, nm_s, nv_s = (_unpack_small(a, shapes) for a in packed)
    loss = g_s["loss"][0]

    ada_rows = D_ADA // LANE
    b_ada_outs = _sum_adamw(d_ada_all, b_ada.reshape(ada_rows, LANE), m_b_ada.reshape(ada_rows, LANE),
                            v_b_ada.reshape(ada_rows, LANE), ada_rows, "adamw_b_ada")
    g_b_ada, d_b_ada, nm_b_ada, nv_b_ada = (a.reshape(1, D_ADA) for a in b_ada_outs)
    d_ada_local = lax.dynamic_slice_in_dim(d_ada_all.reshape(N_DEV, D_ADA), me * (D_ADA // N_DEV), D_ADA // N_DEV, axis=1)
    g_w_ada, d_w_ada, nm_w_ada, nv_w_ada = _ada_adamw(sc_all.T, d_ada_local, w_ada[0], m_w_ada[0], v_w_ada[0])

    def ordered(w_ada_, b_ada_, w_in_, w_out_, s):
        return (w_ada_[None], b_ada_, w_in_[None], s["b_in"], s["w_pool"], s["b_pool"], s["pool_scale"],
                w_out_[None], s["b_out"], s["ln_g"], s["ln_b"])

    return (loss, dx[None],
            *ordered(g_w_ada, g_b_ada, g_w_in, g_w_out, g_s),
            *ordered(d_w_ada, d_b_ada, d_w_in, d_w_out, d_s),
            *ordered(nm_w_ada, nm_b_ada, nm_w_in, nm_w_out, nm_s),
            *ordered(nv_w_ada, nv_b_ada, nv_w_in, nv_w_out, nv_s))
```

```python
import jax
import jax.numpy as jnp
from jax import lax
from jax.experimental import pallas as pl
from jax.experimental.pallas import tpu as pltpu

F32 = jnp.float32
BF16 = jnp.bfloat16

N_DEV = 8
D = 1024
N_HEADS = 8
HEAD_DIM = 64
D_ATT = 512
D_POOL = 512
POOL_WINDOWS = (2, 4, 8, 16)
GROUP_DIM = 128
HALO = 16
LANE = 128
D_IN = 3080
D_ADA = 3072
N_MAIN = 3072
OFF_P = 1536
COL_CHUNK = 512
Q_SCALE = 0.125
LN_EPS = 1e-5
ALPHA = 2.0 ** 0.25
L_CQ, L_CK, L_LSE = 64, 67, 70

ADAM_LR, ADAM_B1, ADAM_B2, ADAM_EPS, ADAM_WD, ADAM_STEP = 0.001, 0.9, 0.999, 1e-08, 0.01, 10
VMEM_LIMIT = 56 * 1024 * 1024

MESH = pl.DeviceIdType.MESH
ANY = pl.BlockSpec(memory_space=pl.ANY)


def _params(sem=None, vmem=VMEM_LIMIT):
    return pltpu.CompilerParams(dimension_semantics=sem, vmem_limit_bytes=vmem)


def _split3(a):
    hi = a.astype(BF16)
    r = a - hi.astype(F32)
    mid = r.astype(BF16)
    lo = (r - mid.astype(F32)).astype(BF16)
    return hi, mid, lo


def _dot(a, b):
    return jnp.dot(a, b, preferred_element_type=F32)


def _dot_nt(a, b):
    return lax.dot_general(a, b, (((1,), (1,)), ((), ())), preferred_element_type=F32)


def _dot_tn(a, b):
    return lax.dot_general(a, b, (((0,), (0,)), ((), ())), preferred_element_type=F32)


def _dot3(m01, a):
    hi, mid, lo = _split3(a)
    return _dot(m01, hi) + _dot(m01, mid) + _dot(m01, lo)


def _sigmoid(z):
    return 1.0 / (1.0 + jnp.exp(-z))


def _lanes(shape):
    return lax.broadcasted_iota(jnp.int32, shape, len(shape) - 1)


def _place3(lane, base, parts, other):
    out = other
    for j in range(3):
        out = jnp.where(lane == base + j, parts[j], out)
    return out


def _mesh_pos():
    return lax.axis_index("x"), lax.axis_index("y"), lax.axis_index("c")


def _dev_index(px, py, pc):
    return 4 * px + 2 * py + pc


def _all_gather(blocks, name):
    n = len(blocks)

    def body(*refs):
        ins, outs = refs[:n], refs[n:2 * n]
        send_sems, recv_sems, local_sems = refs[2 * n:]
        x, y, c = _mesh_pos()
        me, sibling = (x, y, c), (x, y, 1 - c)
        chips = [(1 - x, y), (x, 1 - y), (1 - x, 1 - y)]

        def copy(a, k, block, to, src=None):
            slot = outs[a].at[_dev_index(*block)]
            return pltpu.make_async_remote_copy(
                src_ref=slot if src is None else src, dst_ref=slot,
                send_sem=send_sems.at[a, k], recv_sem=recv_sems.at[a, k],
                device_id=to, device_id_type=MESH)

        mine = [pltpu.make_async_copy(ins[a], outs[a].at[_dev_index(*me)], local_sems.at[a]) for a in range(n)]
        for cp in mine:
            cp.start()
        first = []
        for a in range(n):
            first.append(copy(a, 0, me, sibling, src=ins[a]))
            first += [copy(a, 1 + j, me, (*chip, c), src=ins[a]) for j, chip in enumerate(chips)]
        for cp in first:
            cp.start()
        passed = []
        for j, chip in enumerate(chips):
            for a in range(n):
                copy(a, 1 + j, (*chip, c), me).wait_recv()
                fwd = copy(a, 4 + j, (*chip, c), sibling)
                fwd.start()
                passed.append(fwd)
        for a in range(n):
            copy(a, 0, sibling, me).wait_recv()
            for j, chip in enumerate(chips):
                copy(a, 4 + j, (*chip, 1 - c), me).wait_recv()
        for cp in first + passed:
            cp.wait_send()
        for cp in mine:
            cp.wait()

    return pl.pallas_call(
        body, name=name,
        out_shape=[jax.ShapeDtypeStruct((N_DEV,) + b.shape, b.dtype) for b in blocks],
        in_specs=[ANY] * n, out_specs=[ANY] * n,
        scratch_shapes=[pltpu.SemaphoreType.DMA((n, 7)), pltpu.SemaphoreType.DMA((n, 7)),
                        pltpu.SemaphoreType.DMA((n,))],
    )(*blocks)


def _exchange(scatter, gather, name):
    ns, n = len(scatter), len(scatter) + len(gather)
    arrays = list(scatter) + list(gather)

    def body(*refs):
        ins, outs = refs[:n], refs[n:2 * n]
        send_sems, recv_sems, local_sems = refs[2 * n:]
        x, y, c = _mesh_pos()
        me = _dev_index(x, y, c)
        peers = []
        for p in range(1, N_DEV):
            px, py, pc = (p >> 2) & 1, (p >> 1) & 1, p & 1
            peers.append((1 - x if px else x, 1 - y if py else y, 1 - c if pc else c))

        def src_for(a, slot):
            return ins[a].at[slot] if a < ns else ins[a]

        def copy(a, k, to):
            return pltpu.make_async_remote_copy(
                src_ref=src_for(a, _dev_index(*to)), dst_ref=outs[a].at[me],
                send_sem=send_sems.at[a, k], recv_sem=recv_sems.at[a, k],
                device_id=to, device_id_type=MESH)

        def arrival(a, k, frm):
            slot = _dev_index(*frm)
            return pltpu.make_async_remote_copy(
                src_ref=src_for(a, slot), dst_ref=outs[a].at[slot],
                send_sem=send_sems.at[a, k], recv_sem=recv_sems.at[a, k],
                device_id=frm, device_id_type=MESH)

        mine = [pltpu.make_async_copy(src_for(a, me), outs[a].at[me], local_sems.at[a]) for a in range(n)]
        for cp in mine:
            cp.start()
        sends = [copy(a, k, to) for k, to in enumerate(peers) for a in range(n)]
        for cp in sends:
            cp.start()
        for k, frm in enumerate(peers):
            for a in range(n):
                arrival(a, k, frm).wait_recv()
        for cp in sends:
            cp.wait_send()
        for cp in mine:
            cp.wait()

    out_shape = [jax.ShapeDtypeStruct(a.shape, a.dtype) for a in scatter]
    out_shape += [jax.ShapeDtypeStruct((N_DEV,) + a.shape, a.dtype) for a in gather]
    return pl.pallas_call(
        body, name=name, out_shape=out_shape,
        in_specs=[ANY] * n, out_specs=[ANY] * n,
        scratch_shapes=[pltpu.SemaphoreType.DMA((n, 7)), pltpu.SemaphoreType.DMA((n, 7)),
                        pltpu.SemaphoreType.DMA((n,))],
    )(*arrays)


def _reduce_grads(gw_in, gw_out, small, d_ada):
    shards = (gw_in, gw_out)

    def body(in0_ref, in1_ref, small_ref, dada_ref,
             g0_ref, g1_ref, total_ref, dall_ref,
             r1_0, r1_1, s2_0, s2_1, r2_0, r2_1, sm_sib, sm_chip, sm_recv, send_sems, recv_sems):
        ins, gs = (in0_ref, in1_ref), (g0_ref, g1_ref)
        r1, s2, r2 = (r1_0, r1_1), (s2_0, s2_1), (r2_0, r2_1)
        x, y, c = _mesh_pos()
        me = _dev_index(x, y, c)
        sibling = (x, y, 1 - c)
        chips = [(x, y), (1 - x, y), (x, 1 - y), (1 - x, 1 - y)]

        def remote(src, dst, k, to):
            return pltpu.make_async_remote_copy(src_ref=src, dst_ref=dst, send_sem=send_sems.at[k],
                                                recv_sem=recv_sems.at[k], device_id=to, device_id_type=MESH)

        sends = []

        def send(cp):
            cp.start()
            sends.append(cp)

        for a in range(2):
            for q, chip in enumerate(chips):
                send(remote(ins[a].at[_dev_index(*chip, 1 - c)], r1[a].at[q], 4 * a + q, sibling))
        send(remote(small_ref, sm_sib, 14, sibling))
        peers = []
        for p in range(1, N_DEV):
            px, py, pc = (p >> 2) & 1, (p >> 1) & 1, p & 1
            peers.append((1 - x if px else x, 1 - y if py else y, 1 - c if pc else c))
        for k, to in enumerate(peers):
            send(remote(dada_ref, dall_ref.at[me], 18 + k, to))
        dall_ref[me] = dada_ref[...]

        for a in range(2):
            own = None
            for q in (1, 2, 3, 0):
                chip = chips[q]
                remote(ins[a].at[0], r1[a].at[q], 4 * a + q, sibling).wait_recv()
                pair = ins[a][_dev_index(*chip, c)].astype(F32) + r1[a][q].astype(F32)
                if q == 0:
                    own = pair
                else:
                    s2[a][q - 1] = pair.astype(BF16)
                    send(remote(s2[a].at[q - 1], r2[a].at[q - 1], 8 + 3 * a + q - 1, (*chip, c)))
            for j in range(3):
                remote(s2[a].at[j], r2[a].at[j], 8 + 3 * a + j, sibling).wait_recv()
                own = own + r2[a][j].astype(F32)
            gs[a][...] = own

        remote(small_ref, sm_sib, 14, sibling).wait_recv()
        sm_chip[...] = small_ref[...] + sm_sib[...]
        for j in range(3):
            send(remote(sm_chip, sm_recv.at[j], 15 + j, (*chips[j + 1], c)))
        for j in range(3):
            remote(sm_chip, sm_recv.at[j], 15 + j, sibling).wait_recv()
        total = None
        for ax in range(2):
            for ay in range(2):
                dx, dy = x != ax, y != ay
                term = jnp.where(dx, jnp.where(dy, sm_recv[2], sm_recv[0]), jnp.where(dy, sm_recv[1], sm_chip[...]))
                total = term if total is None else total + term
        total_ref[...] = total

        for k, frm in enumerate(peers):
            remote(dada_ref, dall_ref.at[_dev_index(*frm)], 18 + k, frm).wait_recv()
        for cp in sends:
            cp.wait_send()

    def like(a, lead, dtype):
        return pltpu.VMEM((lead,) + a.shape[1:], dtype)

    return pl.pallas_call(
        body, name="reduce_grads",
        out_shape=[jax.ShapeDtypeStruct(gw_in.shape[1:], F32), jax.ShapeDtypeStruct(gw_out.shape[1:], F32),
                   jax.ShapeDtypeStruct(small.shape, F32), jax.ShapeDtypeStruct((N_DEV,) + d_ada.shape, F32)],
        scratch_shapes=[like(gw_in, 4, BF16), like(gw_out, 4, BF16), like(gw_in, 3, BF16), like(gw_out, 3, BF16),
                        like(gw_in, 3, BF16), like(gw_out, 3, BF16),
                        pltpu.VMEM(small.shape, F32), pltpu.VMEM(small.shape, F32), pltpu.VMEM((3,) + small.shape, F32),
                        pltpu.SemaphoreType.DMA((25,)), pltpu.SemaphoreType.DMA((25,))],
        compiler_params=_params(),
    )(*shards, small, d_ada)


def _ada_forward(c_all, w_ada):
    def body(c_ref, w_ref, sc_ref, part_ref):
        cc = c_ref[...]
        sc = cc * _sigmoid(cc)
        sc_ref[...] = sc
        part_ref[...] = _dot3_rhs(sc, w_ref[...])

    return pl.pallas_call(
        body, name="ada_forward",
        out_shape=[jax.ShapeDtypeStruct(c_all.shape, F32),
                   jax.ShapeDtypeStruct((N_DEV, w_ada.shape[1]), F32)],
        compiler_params=_params(),
    )(c_all, w_ada)


def _dot3_rhs(a, b):
    a0, a1, a2 = _split3(a)
    b0, b1, b2 = _split3(b)
    return (_dot(a0, b0) + (_dot(a0, b1) + _dot(a1, b0))
            + (_dot(a0, b2) + _dot(a1, b1) + _dot(a2, b0)))


def _inproj_forward(x, mod, w_main, w_f, b_main, b_f, tile):
    seq = x.shape[0]
    nt = seq // tile

    def body(x_ref, mod_ref, w_ref, wf_ref, b_ref, bf_ref,
             qp_ref, kp_ref, vp_ref, f_ref, p_ref, ga_ref, gp_ref, u_ref, carry_ref):
        i = pl.program_id(0)

        @pl.when(i == 0)
        def _():
            carry_ref[...] = jnp.zeros_like(carry_ref)

        u = x_ref[...] * mod_ref[0:1, :] + mod_ref[1:2, :]
        ub = u.astype(BF16)
        u_ref[...] = ub

        f = _dot_nt(ub, wf_ref[...]) + bf_ref[...]
        f_ref[...] = f
        lane = _lanes((tile, LANE))
        log_f = jnp.where(lane < N_HEADS, jnp.minimum(f, 0.0) - jnp.log(1.0 + jnp.exp(-jnp.abs(f))), 0.0)
        row = lax.broadcasted_iota(jnp.int32, (tile, tile), 0)
        col = lax.broadcasted_iota(jnp.int32, (tile, tile), 1)
        tri = (row >= col).astype(BF16)
        cum = _dot3(tri, log_f) + carry_ref[0:1, :]
        carry_ref[0:1, :] = cum[tile - 1:tile, :]
        cq = [part.astype(F32) for part in _split3(cum)]
        ck = [part.astype(F32) for part in _split3(-cum)]

        def proj(chunk):
            cols = pl.ds(chunk * COL_CHUNK, COL_CHUNK)
            return _dot_nt(ub, w_ref[cols, :]) + b_ref[:, cols]

        def head_tiles(r):
            for pair in range(N_HEADS // 2):
                both = r[:, pair * LANE:(pair + 1) * LANE]
                yield 2 * pair, both
                yield 2 * pair + 1, pltpu.roll(both, HEAD_DIM, 1)

        for h, val in head_tiles(proj(0)):
            extra = jnp.where((lane >= L_CK) & (lane < L_CK + 3), 1.0, 0.0)
            extra = _place3(lane, L_CQ, [part[:, h:h + 1] for part in cq], extra)
            qp_ref[h] = jnp.where(lane < HEAD_DIM, val * Q_SCALE, extra).astype(BF16)
        for h, val in head_tiles(proj(1)):
            ones = ((lane >= L_CQ) & (lane < L_CQ + 3)) | ((lane >= L_LSE) & (lane < L_LSE + 3))
            extra = _place3(lane, L_CK, [part[:, h:h + 1] for part in ck], jnp.where(ones, 1.0, 0.0))
            kp_ref[h] = jnp.where(lane < HEAD_DIM, val, extra).astype(BF16)
        for h, val in head_tiles(proj(2)):
            extra = jnp.where((lane >= HEAD_DIM) & (lane < HEAD_DIM + 3), -1.0, 0.0)
            vp_ref[h] = jnp.where(lane < HEAD_DIM, val, extra).astype(BF16)
        p_ref[...] = proj(3)
        ga_ref[...] = proj(4)
        gp_ref[...] = proj(5)

    head_block = pl.BlockSpec((N_HEADS, tile, LANE), lambda i: (0, i, 0))
    tok = lambda width: pl.BlockSpec((tile, width), lambda i: (i, 0))
    whole = lambda a: pl.BlockSpec(a.shape, lambda i: (0,) * a.ndim)
    padded = jax.ShapeDtypeStruct((N_HEADS, seq, LANE), BF16)
    half = jax.ShapeDtypeStruct((seq, D_ATT), F32)
    return pl.pallas_call(
        body, name="inproj_forward", grid=(nt,),
        in_specs=[tok(D), whole(mod), whole(w_main), whole(w_f), whole(b_main), whole(b_f)],
        out_specs=[head_block, head_block, head_block, tok(LANE), tok(D_POOL), tok(D_ATT), tok(D_POOL),
                   tok(D)],
        out_shape=[padded, padded, padded, jax.ShapeDtypeStruct((seq, LANE), F32), half, half, half,
                   jax.ShapeDtypeStruct((seq, D), BF16)],
        scratch_shapes=[pltpu.VMEM((8, LANE), F32)],
        compiler_params=_params(("arbitrary",)),
    )(x, mod, w_main, w_f, b_main, b_f)


def _attention_forward(qp, kp, vp, tile):
    seq = qp.shape[1]
    nb = seq // tile

    def body(q_ref, k_ref, v_ref, att_ref, q2t_ref, s_a, s_b):
        i = pl.program_id(1)
        sub = lax.broadcasted_iota(jnp.int32, (LANE, tile), 0)
        row = lax.broadcasted_iota(jnp.int32, (tile, tile), 0)
        col = lax.broadcasted_iota(jnp.int32, (tile, tile), 1)
        q = [q_ref[0], q_ref[1]]

        def scores(buf, kb):
            rows = pl.ds(pl.multiple_of(kb * tile, tile), tile)
            for hh in range(2):
                buf[hh] = _dot_nt(k_ref[hh, rows, :], q[hh])

        def absorb(buf, kb, carry, masked):
            rows = pl.ds(pl.multiple_of(kb * tile, tile), tile)
            out = []
            for hh in range(2):
                m, acc = carry[hh]
                s = buf[hh]
                if masked:
                    s = jnp.where(row <= col, s, -1e30)
                m_new = jnp.maximum(m, jnp.max(s, axis=0, keepdims=True))
                p = jnp.exp(s - m_new).astype(BF16)
                acc = jnp.exp(m - m_new) * acc + _dot_tn(v_ref[hh, rows, :], p)
                out.append((m_new, acc))
            return tuple(out)

        def two_blocks(j, carry):
            scores(s_b, 2 * j + 1)
            carry = absorb(s_a, 2 * j, carry, False)
            scores(s_a, 2 * j + 2)
            return absorb(s_b, 2 * j + 1, carry, False)

        def last_block(carry):
            return absorb(s_a, i, carry, True)

        def last_two_blocks(carry):
            scores(s_b, i)
            carry = absorb(s_a, i - 1, carry, False)
            return absorb(s_b, i, carry, True)

        scores(s_a, 0)
        init = (jnp.full((1, tile), -1e30, F32), jnp.zeros((LANE, tile), F32))
        carry = lax.fori_loop(0, i // 2, two_blocks, (init, init))
        carry = lax.cond(i % 2 == 0, last_block, last_two_blocks, carry)
        outs = []
        for hh in range(2):
            m, acc = carry[hh]
            l = -acc[HEAD_DIM:HEAD_DIM + 1, :]
            outs.append((acc / l)[:HEAD_DIM, :])
            neg_lse = [part.astype(F32) for part in _split3(-(m + jnp.log(l)))]
            q2t_ref[hh] = _place3(sub, L_LSE, neg_lse, q[hh].astype(F32).T).astype(BF16)
        att_ref[...] = jnp.concatenate(outs, axis=0).T

    pair = pl.BlockSpec((2, tile, LANE), lambda hp, i: (hp, i, 0))
    full = pl.BlockSpec((2, seq, LANE), lambda hp, i: (hp, 0, 0))
    return pl.pallas_call(
        body, name="attention_forward", grid=(N_HEADS // 2, nb),
        in_specs=[pair, full, full],
        out_specs=[pl.BlockSpec((tile, LANE), lambda hp, i: (i, hp)),
                   pl.BlockSpec((2, LANE, tile), lambda hp, i: (hp, 0, i))],
        out_shape=[jax.ShapeDtypeStruct((seq, D_ATT), F32),
                   jax.ShapeDtypeStruct((N_HEADS, LANE, seq), BF16)],
        scratch_shapes=[pltpu.VMEM((2, tile, tile), F32), pltpu.VMEM((2, tile, tile), F32)],
        compiler_params=_params(("arbitrary", "arbitrary")),
    )(qp, kp, vp)


def _window_sum(x, halo, window, transposed):
    tile = x.shape[0]

    def split_cat(a):
        hi = a.astype(BF16)
        return jnp.concatenate([hi, (a - hi.astype(F32)).astype(BF16)], axis=1)

    def fold(r):
        return r[:, :LANE] + r[:, LANE:]

    r = lax.broadcasted_iota(jnp.int32, (tile, tile), 0)
    c = lax.broadcasted_iota(jnp.int32, (tile, tile), 1)
    rh = lax.broadcasted_iota(jnp.int32, (HALO, HALO), 0)
    ch = lax.broadcasted_iota(jnp.int32, (HALO, HALO), 1)
    if not transposed:
        band = (c <= r) & (r - c < window)
        edge = (rh + HALO - ch) < window
    else:
        band = (r <= c) & (c - r < window)
        edge = (HALO + ch - rh) < window
    out = fold(_dot(band.astype(BF16), split_cat(x)))
    reach = fold(_dot(edge.astype(BF16), split_cat(halo)))
    if not transposed:
        return jnp.concatenate([out[:HALO] + reach, out[HALO:]], axis=0)
    return jnp.concatenate([out[:tile - HALO], out[tile - HALO:] + reach], axis=0)


def _silu_parts(g):
    sig = _sigmoid(g)
    return g * sig, sig * (1.0 + g * (1.0 - sig))


def _middle(x, tgt, att, g_att, g_pool, p, vecs, pool_vecs, w_out, w_pool, tile):
    seq = x.shape[0]
    nt = seq // tile
    halo_blocks = tile // HALO

    def body(x_ref, tgt_ref, att_ref, ga_ref, gp_ref, p_ref, ph_ref, vec_ref, pvec_ref, wo_ref, wp_ref,
             dxa_ref, do2_ref, dga_ref, dgp_ref, dpooled_ref, dwo_ref, dwp_ref, dvec_ref, dpvec_ref):
        i = pl.program_id(0)

        @pl.when(i == 0)
        def _():
            dwo_ref[...] = jnp.zeros_like(dwo_ref)
            dwp_ref[...] = jnp.zeros_like(dwp_ref)
            dvec_ref[...] = jnp.zeros_like(dvec_ref)
            dpvec_ref[...] = jnp.zeros_like(dpvec_ref)

        gate, b_out, ln_g, ln_b = (vec_ref[k:k + 1, :] for k in range(4))
        b_pool, pool_scale = pvec_ref[0:1, :], pvec_ref[1:2, :]
        x = x_ref[...]
        p = p_ref[...]
        p_halo = ph_ref[...] * jnp.where(i > 0, 1.0, 0.0)
        pos = i * tile + lax.broadcasted_iota(jnp.int32, (tile, 1), 0) + 1

        pooled, mixed = [], []
        for g, window in enumerate(POOL_WINDOWS):
            cols = slice(g * GROUP_DIM, (g + 1) * GROUP_DIM)
            wsum = _window_sum(p[:, cols], p_halo[:, cols], window, False)
            count = jnp.minimum(pos, window).astype(F32)
            pooled.append(wsum / count - p[:, cols])
            mixed.append(_dot(pooled[g].astype(BF16), wp_ref[g]) + b_pool[:, cols])
        mixed = jnp.concatenate(mixed, axis=1)
        pool = mixed * pool_scale

        att = att_ref[...]
        g_att, g_pool = ga_ref[...], gp_ref[...]
        silu_a, dsilu_a = _silu_parts(g_att)
        silu_p, dsilu_p = _silu_parts(g_pool)
        y_in = jnp.concatenate([att * silu_a, pool * silu_p], axis=1)
        y = _dot(y_in.astype(BF16), wo_ref[...]) + b_out
        h = ALPHA * x + gate * y
        mu = jnp.mean(h, axis=1, keepdims=True)
        hc = h - mu
        var = jnp.mean(hc * hc, axis=1, keepdims=True)
        rstd = lax.rsqrt(var + LN_EPS)
        yhat = hc * rstd
        diff = yhat * ln_g + ln_b - tgt_ref[...]
        loss_rows = jnp.sum(diff * diff, axis=1, keepdims=True)
        d_out = diff * (1.0 / D)

        d_yhat = d_out * ln_g
        dh = rstd * (d_yhat - jnp.mean(d_yhat, axis=1, keepdims=True)
                     - yhat * jnp.mean(d_yhat * yhat, axis=1, keepdims=True))
        dxa_ref[...] = ALPHA * dh
        dy = dh * gate
        dyb = dy.astype(BF16)
        lane = _lanes((1, D))
        loss_row = jnp.where(lane == 0, (0.5 / D) * jnp.sum(loss_rows, axis=0, keepdims=True), 0.0)
        dvec_ref[0:1, :] += jnp.sum(dh * y, axis=0, keepdims=True)
        dvec_ref[1:2, :] += jnp.sum(dy, axis=0, keepdims=True)
        dvec_ref[2:3, :] += jnp.sum(d_out * yhat, axis=0, keepdims=True)
        dvec_ref[3:4, :] += jnp.sum(d_out, axis=0, keepdims=True)
        dvec_ref[4:5, :] += loss_row

        dwo_ref[...] += _dot(y_in.T.astype(BF16), dyb)
        d_yin = _dot_nt(dyb, wo_ref[...])
        d_a, d_pl = d_yin[:, :D_ATT], d_yin[:, D_ATT:]
        d_att = d_a * silu_a
        d_att_t = d_att.T
        prod_t = (d_att * att).T
        sub = lax.broadcasted_iota(jnp.int32, (HEAD_DIM, tile), 0)
        for h in range(N_HEADS):
            rows = slice(h * HEAD_DIM, (h + 1) * HEAD_DIM)
            delta = jnp.sum(prod_t[rows], axis=0, keepdims=True)
            extra = _place3(sub, 0, [part.astype(F32) for part in _split3(delta)], 0.0)
            do2_ref[h] = jnp.concatenate([d_att_t[rows], extra], axis=0).astype(BF16)
        dga_ref[...] = d_a * att * dsilu_a
        dgp_ref[...] = d_pl * pool * dsilu_p
        d_pool = d_pl * silu_p
        d_mixed = d_pool * pool_scale
        dpvec_ref[0:1, :] += jnp.sum(d_mixed, axis=0, keepdims=True)
        dpvec_ref[1:2, :] += jnp.sum(d_pool * mixed, axis=0, keepdims=True)
        d_pooled = []
        for g in range(len(POOL_WINDOWS)):
            cols = slice(g * GROUP_DIM, (g + 1) * GROUP_DIM)
            dmb = d_mixed[:, cols].astype(BF16)
            dwp_ref[g] += _dot(pooled[g].T.astype(BF16), dmb)
            d_pooled.append(_dot_nt(dmb, wp_ref[g]))
        dpooled_ref[...] = jnp.concatenate(d_pooled, axis=1)

    tok = lambda width: pl.BlockSpec((tile, width), lambda i: (i, 0))
    whole = lambda a: pl.BlockSpec(a.shape, lambda i: (0,) * a.ndim)
    halo = pl.BlockSpec((HALO, D_POOL), lambda i: (jnp.maximum(i * halo_blocks - 1, 0), 0))
    half = jax.ShapeDtypeStruct((seq, D_ATT), F32)
    outs = [jax.ShapeDtypeStruct((seq, D), F32), jax.ShapeDtypeStruct((N_HEADS, LANE, seq), BF16), half, half, half,
            jax.ShapeDtypeStruct(w_out.shape, F32), jax.ShapeDtypeStruct(w_pool.shape, F32),
            jax.ShapeDtypeStruct(vecs.shape, F32), jax.ShapeDtypeStruct(pool_vecs.shape, F32)]
    return pl.pallas_call(
        body, name="middle", grid=(nt,),
        in_specs=[tok(D), tok(D), tok(D_ATT), tok(D_ATT), tok(D_POOL), tok(D_POOL), halo,
                  whole(vecs), whole(pool_vecs), whole(w_out), whole(w_pool)],
        out_specs=[tok(D), pl.BlockSpec((N_HEADS, LANE, tile), lambda i: (0, 0, i)),
                   tok(D_ATT), tok(D_POOL), tok(D_POOL),
                   whole(w_out), whole(w_pool), whole(vecs), whole(pool_vecs)],
        out_shape=outs,
        compiler_params=_params(("arbitrary",)),
    )(x, tgt, att, g_att, g_pool, p, p, vecs, pool_vecs, w_out, w_pool)


def _attention_backward(q2t, kp, vp, do2t, tile):
    seq = kp.shape[1]
    nb = seq // tile

    def body(qt_ref, k_ref, v_ref, dot_ref, dq_ref, dk_ref, dv_ref, dcum_ref, dq_acc):
        row = lax.broadcasted_iota(jnp.int32, (tile, tile), 0)
        col = lax.broadcasted_iota(jnp.int32, (tile, tile), 1)
        dq_acc[...] = jnp.zeros_like(dq_acc)

        def kv_block(kb, _):
            krows = pl.ds(pl.multiple_of(kb * tile, tile), tile)
            k = [k_ref[hh, krows, :] for hh in range(2)]
            v = [v_ref[hh, krows, :] for hh in range(2)]
            k_t = [k[hh].T for hh in range(2)]

            def q_block(qb, carry, masked):
                qcols = pl.ds(pl.multiple_of(qb * tile, tile), tile)
                out = []
                for hh in range(2):
                    dk, dv = carry[hh]
                    q_t = qt_ref[hh, :, qcols]
                    do_t = dot_ref[hh, :, qcols]
                    s_t = _dot(k[hh], q_t)
                    if masked:
                        s_t = jnp.where(row <= col, s_t, -1e30)
                    p_t = jnp.exp(s_t)
                    ds_t = (p_t * _dot(v[hh], do_t)).astype(BF16)
                    dv = dv + _dot_nt(do_t, p_t.astype(BF16))
                    dk = dk + _dot_nt(q_t, ds_t)
                    dq_acc[hh, :, qcols] += _dot(k_t[hh], ds_t)
                    out.append((dk, dv))
                return tuple(out)

            zero = jnp.zeros((LANE, tile), F32)
            carry = q_block(kb, ((zero, zero), (zero, zero)), True)
            carry = lax.fori_loop(kb + 1, nb, lambda qb, cr: q_block(qb, cr, False), carry)
            for hh in range(2):
                dk = carry[hh][0]
                dk_ref[hh, :, krows] = dk.astype(BF16)
                dv_ref[hh, :, krows] = carry[hh][1].astype(BF16)
                dcum_ref[hh, :, krows] = -dk[L_CK:L_CK + 1, :]
            return 0

        lax.fori_loop(0, nb, kv_block, 0)
        for hh in range(2):
            dq = dq_acc[hh]
            dcum_ref[hh] += dq[L_CQ:L_CQ + 1, :]
            dq_ref[hh] = (dq * Q_SCALE).astype(BF16)

    pair = pl.BlockSpec((2, seq, LANE), lambda hp: (hp, 0, 0))
    pair_t = pl.BlockSpec((2, LANE, seq), lambda hp: (hp, 0, 0))
    grad = jax.ShapeDtypeStruct((N_HEADS, LANE, seq), BF16)
    return pl.pallas_call(
        body, name="attention_backward", grid=(N_HEADS // 2,),
        in_specs=[pair_t, pair, pair, pair_t],
        out_specs=[pair_t, pair_t, pair_t, pl.BlockSpec((2, 1, seq), lambda hp: (hp, 0, 0))],
        out_shape=[grad, grad, grad, jax.ShapeDtypeStruct((N_HEADS, 1, seq), F32)],
        scratch_shapes=[pltpu.VMEM((2, LANE, seq), F32)],
        compiler_params=_params(("arbitrary",)),
    )(q2t, kp, vp, do2t)


def _inproj_backward(dqp, dkp, dvp, d_cum, f, d_pooled, d_ga, d_gp, x, dxa, u, mod, w_main, w_f, tile):
    seq = x.shape[0]
    nt = seq // tile
    halo_blocks = tile // HALO

    def body(dq_ref, dk_ref, dv_ref, dcum_ref, f_ref, dpo_ref, dph_ref, dga_ref, dgp_ref, x_ref, dxa_ref, u_ref,
             mod_ref, w_ref, wf_ref,
             dx_ref, dproj_ref, dwf_ref, db_ref, dbf_ref, dmod_ref, carry_ref):
        step = pl.program_id(0)
        i = nt - 1 - step

        @pl.when(step == 0)
        def _():
            carry_ref[...] = jnp.zeros_like(carry_ref)
            dwf_ref[...] = jnp.zeros_like(dwf_ref)
            db_ref[...] = jnp.zeros_like(db_ref)
            dbf_ref[...] = jnp.zeros_like(dbf_ref)
            dmod_ref[...] = jnp.zeros_like(dmod_ref)

        ones = jnp.ones((8, tile), BF16)

        def emit(chunk, val):
            cols = pl.ds(chunk * COL_CHUNK, COL_CHUNK)
            db_ref[0:1, cols] += jnp.sum(val, axis=0, keepdims=True)
            vb = val.astype(BF16)
            dproj_ref[:, pl.ds((chunk - 3) * COL_CHUNK, COL_CHUNK)] = vb
            return _dot(vb, w_ref[cols, :])

        d_u = jnp.zeros((tile, D), F32)
        for chunk, ref in enumerate((dq_ref, dk_ref, dv_ref)):
            cols = pl.ds(chunk * COL_CHUNK, COL_CHUNK)
            val_t = ref[:, 0:HEAD_DIM, :].reshape(COL_CHUNK, tile)
            db_ref[:, cols] += _dot_nt(ones, val_t)
            d_u += _dot_tn(val_t, w_ref[cols, :])

        d_pooled = dpo_ref[...]
        d_halo = dph_ref[...] * jnp.where(i < nt - 1, 1.0, 0.0)
        pos = i * tile + lax.broadcasted_iota(jnp.int32, (tile, 1), 0) + 1
        d_p = []
        for g, window in enumerate(POOL_WINDOWS):
            cols = slice(g * GROUP_DIM, (g + 1) * GROUP_DIM)
            scaled = d_pooled[:, cols] / jnp.minimum(pos, window).astype(F32)
            d_p.append(_window_sum(scaled, d_halo[:, cols] * (1.0 / window), window, True) - d_pooled[:, cols])
        d_u += emit(3, jnp.concatenate(d_p, axis=1))
        d_u += emit(4, dga_ref[...])
        d_u += emit(5, dgp_ref[...])

        row = lax.broadcasted_iota(jnp.int32, (tile, tile), 0)
        col = lax.broadcasted_iota(jnp.int32, (tile, tile), 1)
        later = (row >= col).astype(BF16)
        d_logf = sum(_dot(part, later) for part in _split3(dcum_ref[:, 0, :])) + carry_ref[:, 0:1]
        carry_ref[:, 0:1] = d_logf[:, 0:1]
        d_f = d_logf * _sigmoid(-f_ref[...].T[0:N_HEADS, :])
        d_f = jnp.concatenate([d_f, jnp.zeros((LANE - N_HEADS, tile), F32)], axis=0)
        dbf_ref[...] += sum(_dot_nt(ones, part) for part in _split3(d_f))
        d_fb = d_f.astype(BF16)
        d_u += _dot_tn(d_fb, wf_ref[...])
        dwf_ref[...] += _dot(d_fb, u_ref[...])

        x = x_ref[...]
        dx_ref[...] = dxa_ref[...] + d_u * mod_ref[0:1, :]
        dmod_ref[0:1, :] += jnp.sum(d_u * x, axis=0, keepdims=True)
        dmod_ref[1:2, :] += jnp.sum(d_u, axis=0, keepdims=True)

    rev = lambda step: nt - 1 - step
    tok = lambda width: pl.BlockSpec((tile, width), lambda s: (rev(s), 0))
    head_block = pl.BlockSpec((N_HEADS, LANE, tile), lambda s: (0, 0, rev(s)))
    whole = lambda a: pl.BlockSpec(a.shape, lambda s: (0,) * a.ndim)
    halo = pl.BlockSpec((HALO, D_POOL), lambda s: (jnp.minimum((rev(s) + 1) * halo_blocks, seq // HALO - 1), 0))
    small = lambda width: jax.ShapeDtypeStruct((8, width), F32)
    n_rest = N_MAIN - OFF_P
    return pl.pallas_call(
        body, name="inproj_backward", grid=(nt,),
        in_specs=[head_block, head_block, head_block, pl.BlockSpec((N_HEADS, 1, tile), lambda s: (0, 0, rev(s))),
                  tok(LANE), tok(D_POOL), halo, tok(D_ATT), tok(D_POOL),
                  tok(D), tok(D), tok(D),
                  whole(mod), whole(w_main), whole(w_f)],
        out_specs=[tok(D), tok(n_rest), pl.BlockSpec((LANE, D), lambda s: (0, 0)),
                   pl.BlockSpec((8, N_MAIN), lambda s: (0, 0)), pl.BlockSpec((8, LANE), lambda s: (0, 0)),
                   pl.BlockSpec((8, D), lambda s: (0, 0))],
        out_shape=[jax.ShapeDtypeStruct((seq, D), F32), jax.ShapeDtypeStruct((seq, n_rest), BF16),
                   jax.ShapeDtypeStruct((LANE, D), F32), small(N_MAIN), small(LANE), small(D)],
        scratch_shapes=[pltpu.VMEM((8, LANE), F32)],
        compiler_params=_params(("arbitrary",)),
    )(dqp, dkp, dvp, d_cum, f, d_pooled, d_pooled, d_ga, d_gp, x, dxa, u, mod, w_main, w_f)


def _weight_grad(dproj, u, k_tile):
    seq, n_cols = dproj.shape
    nk = seq // k_tile

    def body(dp_ref, u_ref, out_ref):
        @pl.when(pl.program_id(1) == 0)
        def _():
            out_ref[...] = jnp.zeros_like(out_ref)

        out_ref[...] += _dot_tn(dp_ref[...], u_ref[...])

    return pl.pallas_call(
        body, name="weight_grad", grid=(n_cols // COL_CHUNK, nk),
        in_specs=[pl.BlockSpec((k_tile, COL_CHUNK), lambda n, k: (k, n)),
                  pl.BlockSpec((k_tile, D), lambda n, k: (k, 0))],
        out_specs=pl.BlockSpec((COL_CHUNK, D), lambda n, k: (n, 0)),
        out_shape=jax.ShapeDtypeStruct((n_cols, D), F32),
        compiler_params=_params(("arbitrary", "arbitrary")),
    )(dproj, u)


def _weight_grad_heads(grad_t, u, k_tile, name):
    seq = u.shape[0]
    nk = seq // k_tile

    def body(g_ref, u_ref, out_ref):
        @pl.when(pl.program_id(0) == 0)
        def _():
            out_ref[...] = jnp.zeros_like(out_ref)

        out_ref[...] += _dot(g_ref[...].reshape(N_HEADS * HEAD_DIM, k_tile), u_ref[...])

    return pl.pallas_call(
        body, name=name, grid=(nk,),
        in_specs=[pl.BlockSpec((N_HEADS, HEAD_DIM, k_tile), lambda k: (0, 0, k)),
                  pl.BlockSpec((k_tile, D), lambda k: (k, 0))],
        out_specs=pl.BlockSpec((N_HEADS * HEAD_DIM, D), lambda k: (0, 0)),
        out_shape=jax.ShapeDtypeStruct((N_HEADS * HEAD_DIM, D), F32),
        compiler_params=_params(("arbitrary",)),
    )(grad_t, u)


def _adamw(w, g, m, v):
    m = ADAM_B1 * m + (1.0 - ADAM_B1) * g
    v = ADAM_B2 * v + (1.0 - ADAM_B2) * (g * g)
    m_hat = m / (1.0 - ADAM_B1 ** ADAM_STEP)
    v_hat = v / (1.0 - ADAM_B2 ** ADAM_STEP)
    delta = -ADAM_LR * (m_hat / (jnp.sqrt(v_hat) + ADAM_EPS) + ADAM_WD * w)
    return delta, m, v


def _adamw_call(g, w, m, v, lead_tile, name):
    nr = w.shape[0] // lead_tile

    def body(gi_ref, w_ref, m_ref, v_ref, g_ref, d_ref, nm_ref, nv_ref):
        g = gi_ref[...]
        g_ref[...] = g
        d_ref[...], nm_ref[...], nv_ref[...] = _adamw(w_ref[...], g, m_ref[...], v_ref[...])

    blk = pl.BlockSpec((lead_tile,) + w.shape[1:], lambda r: (r,) + (0,) * (w.ndim - 1))
    shape = jax.ShapeDtypeStruct(w.shape, F32)
    return pl.pallas_call(
        body, name=name, grid=(nr,),
        in_specs=[blk, blk, blk, blk], out_specs=[blk, blk, blk, blk],
        out_shape=[shape, shape, shape, shape],
        compiler_params=_params(("arbitrary",)),
    )(g, w, m, v)


def _sum_adamw(parts, w, m, v, row_tile, name):
    rows, cols = w.shape
    nr = rows // row_tile

    def body(parts_ref, w_ref, m_ref, v_ref, g_ref, d_ref, nm_ref, nv_ref):
        g = parts_ref[0]
        for k in range(1, N_DEV):
            g = g + parts_ref[k]
        g_ref[...] = g
        d_ref[...], nm_ref[...], nv_ref[...] = _adamw(w_ref[...], g, m_ref[...], v_ref[...])

    blk = pl.BlockSpec((row_tile, cols), lambda r: (r, 0))
    shape = jax.ShapeDtypeStruct(w.shape, F32)
    return pl.pallas_call(
        body, name=name, grid=(nr,),
        in_specs=[pl.BlockSpec((N_DEV, row_tile, cols), lambda r: (0, r, 0)), blk, blk, blk],
        out_specs=[blk, blk, blk, blk],
        out_shape=[shape, shape, shape, shape],
        compiler_params=_params(("arbitrary",)),
    )(parts, w, m, v)


def _ada_adamw(sc_t, d_ada, w, m, v):
    def body(sc_ref, d_ref, w_ref, m_ref, v_ref, g_ref, dl_ref, nm_ref, nv_ref):
        g = sc_ref[:, 0:1] * d_ref[0:1, :]
        for b in range(1, N_DEV):
            g = g + sc_ref[:, b:b + 1] * d_ref[b:b + 1, :]
        g_ref[...] = g
        dl_ref[...], nm_ref[...], nv_ref[...] = _adamw(w_ref[...], g, m_ref[...], v_ref[...])

    shape = jax.ShapeDtypeStruct(w.shape, F32)
    return pl.pallas_call(
        body, name="ada_adamw", out_shape=[shape, shape, shape, shape], compiler_params=_params(),
    )(sc_t, d_ada, w, m, v)


F_LO, F_HI = 3 * D_ATT, 3 * D_ATT + N_HEADS


def _split_forget(a, axis):
    idx = lambda lo, hi: tuple(slice(lo, hi) if d == axis else slice(None) for d in range(a.ndim))
    pad = [(0, LANE - N_HEADS) if d == axis else (0, 0) for d in range(a.ndim)]
    return jnp.concatenate([a[idx(0, F_LO)], a[idx(F_HI, D_IN)]], axis=axis), jnp.pad(a[idx(F_LO, F_HI)], pad)


def _join_forget(main, f, axis):
    idx = lambda lo, hi: tuple(slice(lo, hi) if d == axis else slice(None) for d in range(main.ndim))
    return jnp.concatenate([main[idx(0, F_LO)], f[idx(0, N_HEADS)], main[idx(F_LO, N_MAIN)]], axis=axis)


_SMALL = (("b_in", 3200), ("w_pool", 65536), ("b_pool", 512), ("pool_scale", 512),
          ("b_out", 1024), ("ln_g", 1024), ("ln_b", 1024), ("loss", 128))
_SMALL_ROWS = 576


def _pack_small(parts):
    flat = []
    for name, size in _SMALL:
        a = parts[name].reshape(-1)
        flat.append(jnp.pad(a, (0, size - a.shape[0])))
    flat = jnp.concatenate(flat)
    flat = jnp.pad(flat, (0, _SMALL_ROWS * LANE - flat.shape[0]))
    return flat.reshape(_SMALL_ROWS, LANE)


def _unpack_small(packed, shapes):
    flat = packed.reshape(-1)
    out, off = {}, 0
    for name, size in _SMALL:
        n = 1
        for s in shapes[name]:
            n *= s
        out[name] = flat[off:off + n].reshape(shapes[name])
        off += size
    return out


def kernel(x, c, w_ada, b_ada, w_in, b_in, w_pool_mix, b_pool_mix, pool_scale, w_out, b_out, ln_g, ln_b, loss_target, m_w_ada, m_b_ada, m_w_in, m_b_in, m_w_pool_mix, m_b_pool_mix, m_pool_scale, m_w_out, m_b_out, m_ln_g, m_ln_b, v_w_ada, v_b_ada, v_w_in, v_b_in, v_w_pool_mix, v_b_pool_mix, v_pool_scale, v_w_out, v_b_out, v_ln_g, v_ln_b):
    seq = x.shape[1]
    tile = min(256, seq)
    attn_tile = min(512, max(128, seq // 4))
    me = _dev_index(*_mesh_pos())
    x2, tgt = x[0], loss_target[0]

    rows_in = D_IN // N_DEV
    c_all, w_in_g, w_out_g = _all_gather(
        [jnp.pad(c, ((0, 7), (0, 0))), w_in[0].T.astype(BF16), w_out[0].astype(BF16)], "gather_weights")
    sc_all, ada_part = _ada_forward(c_all[:, 0, :], w_ada[0])
    (ada_mine,) = _exchange([ada_part.reshape(N_DEV, 1, -1)], [], "exchange_ada")
    ada = ada_mine.reshape(1, D_ADA) + b_ada
    shift, scale, gate = ada[:, 0:D], ada[:, D:2 * D], ada[:, 2 * D:]
    mod = jnp.concatenate([1.0 + scale, shift, jnp.zeros((6, D), F32)], axis=0)

    w_main, w_f = _split_forget(w_in_g.reshape(D_IN, D), 0)
    b_main, b_f = _split_forget(b_in, 1)
    w_out_full = w_out_g.reshape(D, D)

    qp, kp, vp, f, p, g_att, g_pool, u = _inproj_forward(x2, mod, w_main, w_f, b_main, b_f, tile)
    att, q2t = _attention_forward(qp, kp, vp, attn_tile)

    vecs = jnp.concatenate([gate, b_out, ln_g, ln_b, jnp.zeros((4, D), F32)], axis=0)
    pool_vecs = jnp.concatenate([b_pool_mix.reshape(1, D_POOL), pool_scale, jnp.zeros((6, D_POOL), F32)], axis=0)
    dxa, do2, d_ga, d_gp, d_pooled, dw_out, dw_pool, dvec, dpvec = _middle(
        x2, tgt, att, g_att, g_pool, p, vecs, pool_vecs, w_out_full, w_pool_mix[0].astype(BF16), tile)

    dqp, dkp, dvp, d_cum = _attention_backward(q2t, kp, vp, do2, attn_tile)
    dx, dproj, dw_f, db_main, db_f, dmod = _inproj_backward(
        dqp, dkp, dvp, d_cum, f, d_pooled, d_ga, d_gp, x2, dxa, u, mod, w_main, w_f, tile)
    k_tile = min(1024, seq)
    dw_q, dw_k, dw_v = (_weight_grad_heads(g, u, k_tile, "weight_grad_" + n)
                        for g, n in ((dqp, "q"), (dkp, "k"), (dvp, "v")))
    dw_rest = _weight_grad(dproj, u, k_tile)

    dw_main = jnp.concatenate([dw_q, dw_k, dw_v, dw_rest], axis=0)
    gw_in = _join_forget(dw_main, dw_f, 0).reshape(N_DEV, rows_in, D).astype(BF16)
    gw_out = dw_out.reshape(N_DEV, D // N_DEV, D).astype(BF16)
    d_ada = jnp.concatenate([dmod[1:2], dmod[0:1], dvec[0:1]], axis=1)
    small = _pack_small({
        "b_in": _join_forget(db_main[0:1], db_f[0:1], 1), "w_pool": dw_pool, "b_pool": dpvec[0:1],
        "pool_scale": dpvec[1:2], "b_out": dvec[1:2], "ln_g": dvec[2:3], "ln_b": dvec[3:4], "loss": dvec[4:5, 0:LANE]})
    g_in_rows, g_out, small_sum, d_ada_all = _reduce_grads(gw_in, gw_out, small, d_ada.reshape(D_ADA // LANE, LANE))

    def rows3(a):
        return a[0].T.reshape(rows_in, D // LANE, LANE)

    outs_in = _adamw_call(g_in_rows.reshape(rows_in, D // LANE, LANE), rows3(w_in), rows3(m_w_in), rows3(v_w_in),
                          rows_in // 5, "adamw_w_in")
    g_w_in, d_w_in, nm_w_in, nv_w_in = (a.reshape(rows_in, D).T for a in outs_in)
    g_w_out, d_w_out, nm_w_out, nv_w_out = _adamw_call(g_out, w_out[0], m_w_out[0], v_w_out[0], D // N_DEV, "adamw_w_out")

    zero = jnp.zeros((1,), F32)
    weights = {"b_in": b_in, "w_pool": w_pool_mix, "b_pool": b_pool_mix, "pool_scale": pool_scale,
               "b_out": b_out, "ln_g": ln_g, "ln_b": ln_b, "loss": zero}
    first = {"b_in": m_b_in, "w_pool": m_w_pool_mix, "b_pool": m_b_pool_mix, "pool_scale": m_pool_scale,
             "b_out": m_b_out, "ln_g": m_ln_g, "ln_b": m_ln_b, "loss": zero}
    second = {"b_in": v_b_in, "w_pool": v_w_pool_mix, "b_pool": v_b_pool_mix, "pool_scale": v_pool_scale,
              "b_out": v_b_out, "ln_g": v_ln_g, "ln_b": v_ln_b, "loss": zero}
    packed = _adamw_call(small_sum, _pack_small(weights), _pack_small(first), _pack_small(second),
                         _SMALL_ROWS, "adamw_small")
    shapes = {k: a.shape for k, a in weights.items()}
    g_s, d_s, nm_s, nv_s = (_unpack_small(a, shapes) for a in packed)
    loss = g_s["loss"][0]

    ada_rows = D_ADA // LANE
    b_ada_outs = _sum_adamw(d_ada_all, b_ada.reshape(ada_rows, LANE), m_b_ada.reshape(ada_rows, LANE),
                            v_b_ada.reshape(ada_rows, LANE), ada_rows, "adamw_b_ada")
    g_b_ada, d_b_ada, nm_b_ada, nv_b_ada = (a.reshape(1, D_ADA) for a in b_ada_outs)
    d_ada_local = lax.dynamic_slice_in_dim(d_ada_all.reshape(N_DEV, D_ADA), me * (D_ADA // N_DEV), D_ADA // N_DEV, axis=1)
    g_w_ada, d_w_ada, nm_w_ada, nv_w_ada = _ada_adamw(sc_all.T, d_ada_local, w_ada[0], m_w_ada[0], v_w_ada[0])

    def ordered(w_ada_, b_ada_, w_in_, w_out_, s):
        return (w_ada_[None], b_ada_, w_in_[None], s["b_in"], s["w_pool"], s["b_pool"], s["pool_scale"],
                w_out_[None], s["b_out"], s["ln_g"], s["ln_b"])

    return (loss, dx[None],
            *ordered(g_w_ada, g_b_ada, g_w_in, g_w_out, g_s),
            *ordered(d_w_ada, d_b_ada, d_w_in, d_w_out, d_s),
            *ordered(nm_w_ada, nm_b_ada, nm_w_in, nm_w_out, nm_s),
            *ordered(nv_w_ada, nv_b_ada, nv_w_in, nv_w_out, nv_s))
```

```python
import jax
import jax.numpy as jnp
from jax import lax
from jax.experimental import pallas as pl
from jax.experimental.pallas import tpu as pltpu

F32 = jnp.float32
BF16 = jnp.bfloat16

N_DEV = 8
D = 1024
N_HEADS = 8
HEAD_DIM = 64
D_ATT = 512
D_POOL = 512
POOL_WINDOWS = (2, 4, 8, 16)
GROUP_DIM = 128
HALO = 16
LANE = 128
D_IN = 3080
D_ADA = 3072
N_MAIN = 3072
OFF_P = 1536
COL_CHUNK = 512
Q_SCALE = 0.125
LN_EPS = 1e-5
ALPHA = 2.0 ** 0.25
L_CQ, L_CK, L_LSE = 64, 67, 70

ADAM_LR, ADAM_B1, ADAM_B2, ADAM_EPS, ADAM_WD, ADAM_STEP = 0.001, 0.9, 0.999, 1e-08, 0.01, 10
VMEM_LIMIT = 56 * 1024 * 1024

MESH = pl.DeviceIdType.MESH
ANY = pl.BlockSpec(memory_space=pl.ANY)


def _params(sem=None, vmem=VMEM_LIMIT):
    return pltpu.CompilerParams(dimension_semantics=sem, vmem_limit_bytes=vmem)


def _split3(a):
    hi = a.astype(BF16)
    r = a - hi.astype(F32)
    mid = r.astype(BF16)
    lo = (r - mid.astype(F32)).astype(BF16)
    return hi, mid, lo


def _dot(a, b):
    return jnp.dot(a, b, preferred_element_type=F32)


def _dot_nt(a, b):
    return lax.dot_general(a, b, (((1,), (1,)), ((), ())), preferred_element_type=F32)


def _dot_tn(a, b):
    return lax.dot_general(a, b, (((0,), (0,)), ((), ())), preferred_element_type=F32)


def _dot3(m01, a):
    hi, mid, lo = _split3(a)
    return _dot(m01, hi) + _dot(m01, mid) + _dot(m01, lo)


def _sigmoid(z):
    return 1.0 / (1.0 + jnp.exp(-z))


def _lanes(shape):
    return lax.broadcasted_iota(jnp.int32, shape, len(shape) - 1)


def _place3(lane, base, parts, other):
    out = other
    for j in range(3):
        out = jnp.where(lane == base + j, parts[j], out)
    return out


def _mesh_pos():
    return lax.axis_index("x"), lax.axis_index("y"), lax.axis_index("c")


def _dev_index(px, py, pc):
    return 4 * px + 2 * py + pc


def _all_gather(blocks, name):
    n = len(blocks)

    def body(*refs):
        ins, outs = refs[:n], refs[n:2 * n]
        send_sems, recv_sems, local_sems = refs[2 * n:]
        x, y, c = _mesh_pos()
        me, sibling = (x, y, c), (x, y, 1 - c)
        chips = [(1 - x, y), (x, 1 - y), (1 - x, 1 - y)]

        def copy(a, k, block, to, src=None):
            slot = outs[a].at[_dev_index(*block)]
            return pltpu.make_async_remote_copy(
                src_ref=slot if src is None else src, dst_ref=slot,
                send_sem=send_sems.at[a, k], recv_sem=recv_sems.at[a, k],
                device_id=to, device_id_type=MESH)

        mine = [pltpu.make_async_copy(ins[a], outs[a].at[_dev_index(*me)], local_sems.at[a]) for a in range(n)]
        for cp in mine:
            cp.start()
        first = []
        for a in range(n):
            first.append(copy(a, 0, me, sibling, src=ins[a]))
            first += [copy(a, 1 + j, me, (*chip, c), src=ins[a]) for j, chip in enumerate(chips)]
        for cp in first:
            cp.start()
        passed = []
        for j, chip in enumerate(chips):
            for a in range(n):
                copy(a, 1 + j, (*chip, c), me).wait_recv()
                fwd = copy(a, 4 + j, (*chip, c), sibling)
                fwd.start()
                passed.append(fwd)
        for a in range(n):
            copy(a, 0, sibling, me).wait_recv()
            for j, chip in enumerate(chips):
                copy(a, 4 + j, (*chip, 1 - c), me).wait_recv()
        for cp in first + passed:
            cp.wait_send()
        for cp in mine:
            cp.wait()

    return pl.pallas_call(
        body, name=name,
        out_shape=[jax.ShapeDtypeStruct((N_DEV,) + b.shape, b.dtype) for b in blocks],
        in_specs=[ANY] * n, out_specs=[ANY] * n,
        scratch_shapes=[pltpu.SemaphoreType.DMA((n, 7)), pltpu.SemaphoreType.DMA((n, 7)),
                        pltpu.SemaphoreType.DMA((n,))],
    )(*blocks)


def _exchange(scatter, gather, name):
    ns, n = len(scatter), len(scatter) + len(gather)
    arrays = list(scatter) + list(gather)

    def body(*refs):
        ins, outs = refs[:n], refs[n:2 * n]
        send_sems, recv_sems, local_sems = refs[2 * n:]
        x, y, c = _mesh_pos()
        me = _dev_index(x, y, c)
        peers = []
        for p in range(1, N_DEV):
            px, py, pc = (p >> 2) & 1, (p >> 1) & 1, p & 1
            peers.append((1 - x if px else x, 1 - y if py else y, 1 - c if pc else c))

        def src_for(a, slot):
            return ins[a].at[slot] if a < ns else ins[a]

        def copy(a, k, to):
            return pltpu.make_async_remote_copy(
                src_ref=src_for(a, _dev_index(*to)), dst_ref=outs[a].at[me],
                send_sem=send_sems.at[a, k], recv_sem=recv_sems.at[a, k],
                device_id=to, device_id_type=MESH)

        def arrival(a, k, frm):
            slot = _dev_index(*frm)
            return pltpu.make_async_remote_copy(
                src_ref=src_for(a, slot), dst_ref=outs[a].at[slot],
                send_sem=send_sems.at[a, k], recv_sem=recv_sems.at[a, k],
                device_id=frm, device_id_type=MESH)

        mine = [pltpu.make_async_copy(src_for(a, me), outs[a].at[me], local_sems.at[a]) for a in range(n)]
        for cp in mine:
            cp.start()
        sends = [copy(a, k, to) for k, to in enumerate(peers) for a in range(n)]
        for cp in sends:
            cp.start()
        for k, frm in enumerate(peers):
            for a in range(n):
                arrival(a, k, frm).wait_recv()
        for cp in sends:
            cp.wait_send()
        for cp in mine:
            cp.wait()

    out_shape = [jax.ShapeDtypeStruct(a.shape, a.dtype) for a in scatter]
    out_shape += [jax.ShapeDtypeStruct((N_DEV,) + a.shape, a.dtype) for a in gather]
    return pl.pallas_call(
        body, name=name, out_shape=out_shape,
        in_specs=[ANY] * n, out_specs=[ANY] * n,
        scratch_shapes=[pltpu.SemaphoreType.DMA((n, 7)), pltpu.SemaphoreType.DMA((n, 7)),
                        pltpu.SemaphoreType.DMA((n,))],
    )(*arrays)


def _gather_stages(src_ref, out_ref, send_sems, recv_sems, local_sem):
    x, y, c = _mesh_pos()
    me, sibling = (x, y, c), (x, y, 1 - c)
    chips = [(1 - x, y), (x, 1 - y), (1 - x, 1 - y)]

    def copy(k, block, to, src=None):
        slot = out_ref.at[_dev_index(*block)]
        return pltpu.make_async_remote_copy(
            src_ref=slot if src is None else src, dst_ref=slot, send_sem=send_sems.at[k], recv_sem=recv_sems.at[k],
            device_id=to, device_id_type=MESH)

    mine = pltpu.make_async_copy(src_ref, out_ref.at[_dev_index(*me)], local_sem)
    first = [copy(0, me, sibling, src=src_ref)] + [copy(1 + j, me, (*chip, c), src=src_ref) for j, chip in enumerate(chips)]
    passed = [copy(4 + j, (*chip, c), sibling) for j, chip in enumerate(chips)]

    def start():
        mine.start()
        for cp in first:
            cp.start()

    def forward():
        for j, chip in enumerate(chips):
            copy(1 + j, (*chip, c), me).wait_recv()
            passed[j].start()

    def finish():
        copy(0, sibling, me).wait_recv()
        for j, chip in enumerate(chips):
            copy(4 + j, (*chip, 1 - c), me).wait_recv()
        for cp in first + passed:
            cp.wait_send()
        mine.wait()

    return start, forward, finish


N_REDUCE_SEMS = 7
N_SMALL_SEMS = 4 + 7


def _reduce_stages(ins, gs, r1, s2, r2, small, send_sems, recv_sems, rows=None):
    n = len(ins)
    x, y, c = _mesh_pos()
    me = _dev_index(x, y, c)
    sibling = (x, y, 1 - c)
    chips = [(x, y), (1 - x, y), (x, 1 - y), (1 - x, 1 - y)]
    peers = []
    for p in range(1, N_DEV):
        px, py, pc = (p >> 2) & 1, (p >> 1) & 1, p & 1
        peers.append((1 - x if px else x, 1 - y if py else y, 1 - c if pc else c))
    base_small = N_REDUCE_SEMS * n

    def remote(src, dst, k, to):
        return pltpu.make_async_remote_copy(src_ref=src, dst_ref=dst, send_sem=send_sems.at[k],
                                            recv_sem=recv_sems.at[k], device_id=to, device_id_type=MESH)

    def level1(a, q):
        return remote(ins[a].at[_dev_index(*chips[q], 1 - c)], r1[a].at[q], N_REDUCE_SEMS * a + q, sibling)

    def level2(a, j):
        return remote(s2[a].at[j], r2[a].at[j], N_REDUCE_SEMS * a + 4 + j, (*chips[j + 1], c))

    if small is not None:
        small_ref, total_ref, sm_sib, sm_chip, sm_recv = small
        to_sibling = remote(small_ref, sm_sib, base_small, sibling)
        to_chips = [remote(sm_chip, sm_recv.at[j], base_small + 1 + j, (*chips[j + 1], c)) for j in range(3)]
    if rows is not None:
        rows_ref, land_ref, all_ref = rows
        row_sends = [remote(rows_ref, land_ref.at[me], base_small + 4 + k, to) for k, to in enumerate(peers)]

    def start():
        for a in range(n):
            for q in range(4):
                level1(a, q).start()
        if small is not None:
            to_sibling.start()
        if rows is not None:
            for cp in row_sends:
                cp.start()
            land_ref[me] = rows_ref[...]

    def middle():
        for a in range(n):
            for q in (1, 2, 3, 0):
                level1(a, q).wait_recv()
                pair = ins[a][_dev_index(*chips[q], c)].astype(F32) + r1[a][q].astype(F32)
                if q == 0:
                    gs[a][...] = pair
                else:
                    s2[a][q - 1] = pair.astype(BF16)
                    level2(a, q - 1).start()
        if small is not None:
            to_sibling.wait_recv()
            sm_chip[...] = small_ref[...] + sm_sib[...]
            for cp in to_chips:
                cp.start()

    def finish():
        for a in range(n):
            own = gs[a][...]
            for j in range(3):
                level2(a, j).wait_recv()
                own = own + r2[a][j].astype(F32)
            gs[a][...] = own
            for q in range(4):
                level1(a, q).wait_send()
            for j in range(3):
                level2(a, j).wait_send()
        if small is not None:
            for cp in to_chips:
                cp.wait_recv()
            total = None
            for ax in range(2):
                for ay in range(2):
                    dx, dy = x != ax, y != ay
                    term = jnp.where(dx, jnp.where(dy, sm_recv[2], sm_recv[0]), jnp.where(dy, sm_recv[1], sm_chip[...]))
                    total = term if total is None else total + term
            total_ref[...] = total
            for cp in [to_sibling] + to_chips:
                cp.wait_send()
        if rows is not None:
            for k, frm in enumerate(peers):
                remote(rows_ref, land_ref.at[_dev_index(*frm)], base_small + 4 + k, frm).wait_recv()
            all_ref[...] = land_ref[...]
            for cp in row_sends:
                cp.wait_send()

    return start, middle, finish


def _reduce_scratch(shard, small, rows=None):
    out = [pltpu.VMEM((lead,) + shard.shape[1:], BF16) for lead in (4, 3, 3)]
    out += [pltpu.VMEM(small.shape, F32), pltpu.VMEM(small.shape, F32), pltpu.VMEM((3,) + small.shape, F32)]
    if rows is not None:
        out.append(pltpu.VMEM((N_DEV,) + rows.shape, F32))
    return out


def _reduce_grads(gw_in, small, rows):
    def body(in_ref, small_ref, rows_ref, g_ref, total_ref, rows_all_ref,
             r1, s2, r2, sm_sib, sm_chip, sm_recv, rows_land, send_sems, recv_sems):
        start, middle, finish = _reduce_stages(
            [in_ref], [g_ref], [r1], [s2], [r2], (small_ref, total_ref, sm_sib, sm_chip, sm_recv),
            send_sems, recv_sems, rows=(rows_ref, rows_land, rows_all_ref))
        start()
        middle()
        finish()

    return pl.pallas_call(
        body, name="reduce_grads",
        out_shape=[jax.ShapeDtypeStruct(gw_in.shape[1:], F32), jax.ShapeDtypeStruct(small.shape, F32),
                   jax.ShapeDtypeStruct((N_DEV,) + rows.shape, F32)],
        scratch_shapes=_reduce_scratch(gw_in, small, rows)
        + [pltpu.SemaphoreType.DMA((N_REDUCE_SEMS + N_SMALL_SEMS,))] * 2,
        compiler_params=_params(),
    )(gw_in, small, rows)


def _ada_forward(c_all, w_ada):
    def body(c_ref, w_ref, sc_ref, part_ref):
        cc = c_ref[...]
        sc = cc * _sigmoid(cc)
        sc_ref[...] = sc
        part_ref[...] = _dot3_rhs(sc, w_ref[...])

    return pl.pallas_call(
        body, name="ada_forward",
        out_shape=[jax.ShapeDtypeStruct(c_all.shape, F32),
                   jax.ShapeDtypeStruct((N_DEV, w_ada.shape[1]), F32)],
        compiler_params=_params(),
    )(c_all, w_ada)


def _dot3_rhs(a, b):
    a0, a1, a2 = _split3(a)
    b0, b1, b2 = _split3(b)
    return (_dot(a0, b0) + (_dot(a0, b1) + _dot(a1, b0))
            + (_dot(a0, b2) + _dot(a1, b1) + _dot(a2, b0)))


def _inproj_forward(x, mod, w_main, w_f, b_main, b_f, tile):
    seq = x.shape[0]
    nt = seq // tile

    def body(x_ref, mod_ref, w_ref, wf_ref, b_ref, bf_ref,
             qp_ref, kp_ref, vp_ref, f_ref, p_ref, ga_ref, gp_ref, u_ref, carry_ref):
        i = pl.program_id(0)

        @pl.when(i == 0)
        def _():
            carry_ref[...] = jnp.zeros_like(carry_ref)

        u = x_ref[...] * mod_ref[0:1, :] + mod_ref[1:2, :]
        ub = u.astype(BF16)
        u_ref[...] = ub

        f = _dot_nt(ub, wf_ref[...]) + bf_ref[...]
        f_ref[...] = f
        lane = _lanes((tile, LANE))
        log_f = jnp.where(lane < N_HEADS, jnp.minimum(f, 0.0) - jnp.log(1.0 + jnp.exp(-jnp.abs(f))), 0.0)
        row = lax.broadcasted_iota(jnp.int32, (tile, tile), 0)
        col = lax.broadcasted_iota(jnp.int32, (tile, tile), 1)
        tri = (row >= col).astype(BF16)
        cum = _dot3(tri, log_f) + carry_ref[0:1, :]
        carry_ref[0:1, :] = cum[tile - 1:tile, :]
        cq = [part.astype(F32) for part in _split3(cum)]
        ck = [part.astype(F32) for part in _split3(-cum)]

        def proj(chunk):
            cols = pl.ds(chunk * COL_CHUNK, COL_CHUNK)
            return _dot_nt(ub, w_ref[cols, :]) + b_ref[:, cols]

        def head_tiles(r):
            for pair in range(N_HEADS // 2):
                both = r[:, pair * LANE:(pair + 1) * LANE]
                yield 2 * pair, both
                yield 2 * pair + 1, pltpu.roll(both, HEAD_DIM, 1)

        for h, val in head_tiles(proj(0)):
            extra = jnp.where((lane >= L_CK) & (lane < L_CK + 3), 1.0, 0.0)
            extra = _place3(lane, L_CQ, [part[:, h:h + 1] for part in cq], extra)
            qp_ref[h] = jnp.where(lane < HEAD_DIM, val * Q_SCALE, extra).astype(BF16)
        for h, val in head_tiles(proj(1)):
            ones = ((lane >= L_CQ) & (lane < L_CQ + 3)) | ((lane >= L_LSE) & (lane < L_LSE + 3))
            extra = _place3(lane, L_CK, [part[:, h:h + 1] for part in ck], jnp.where(ones, 1.0, 0.0))
            kp_ref[h] = jnp.where(lane < HEAD_DIM, val, extra).astype(BF16)
        for h, val in head_tiles(proj(2)):
            extra = jnp.where((lane >= HEAD_DIM) & (lane < HEAD_DIM + 3), -1.0, 0.0)
            vp_ref[h] = jnp.where(lane < HEAD_DIM, val, extra).astype(BF16)
        p_ref[...] = proj(3)
        ga_ref[...] = proj(4)
        gp_ref[...] = proj(5)

    head_block = pl.BlockSpec((N_HEADS, tile, LANE), lambda i: (0, i, 0))
    tok = lambda width: pl.BlockSpec((tile, width), lambda i: (i, 0))
    whole = lambda a: pl.BlockSpec(a.shape, lambda i: (0,) * a.ndim)
    padded = jax.ShapeDtypeStruct((N_HEADS, seq, LANE), BF16)
    half = jax.ShapeDtypeStruct((seq, D_ATT), F32)
    return pl.pallas_call(
        body, name="inproj_forward", grid=(nt,),
        in_specs=[tok(D), whole(mod), whole(w_main), whole(w_f), whole(b_main), whole(b_f)],
        out_specs=[head_block, head_block, head_block, tok(LANE), tok(D_POOL), tok(D_ATT), tok(D_POOL),
                   tok(D)],
        out_shape=[padded, padded, padded, jax.ShapeDtypeStruct((seq, LANE), F32), half, half, half,
                   jax.ShapeDtypeStruct((seq, D), BF16)],
        scratch_shapes=[pltpu.VMEM((8, LANE), F32)],
        compiler_params=_params(("arbitrary",)),
    )(x, mod, w_main, w_f, b_main, b_f)


def _attention_forward(qp, kp, vp, w_out, tile):
    seq = qp.shape[1]
    nb = seq // tile
    steps = (N_HEADS // 2) * nb

    def body(q_ref, k_ref, v_ref, wo_ref, att_ref, q2t_ref, wo_all_ref, s_a, s_b, send_sems, recv_sems, local_sem):
        step = pl.program_id(0) * nb + pl.program_id(1)
        start, forward, finish = _gather_stages(wo_ref, wo_all_ref, send_sems, recv_sems, local_sem.at[0])
        pl.when(step == 0)(start)
        pl.when(step == steps // 2)(forward)

        i = pl.program_id(1)
        sub = lax.broadcasted_iota(jnp.int32, (LANE, tile), 0)
        row = lax.broadcasted_iota(jnp.int32, (tile, tile), 0)
        col = lax.broadcasted_iota(jnp.int32, (tile, tile), 1)
        q = [q_ref[0], q_ref[1]]

        def scores(buf, kb):
            rows = pl.ds(pl.multiple_of(kb * tile, tile), tile)
            for hh in range(2):
                buf[hh] = _dot_nt(k_ref[hh, rows, :], q[hh])

        def absorb(buf, kb, carry, masked):
            rows = pl.ds(pl.multiple_of(kb * tile, tile), tile)
            out = []
            for hh in range(2):
                m, acc = carry[hh]
                s = buf[hh]
                if masked:
                    s = jnp.where(row <= col, s, -1e30)
                m_new = jnp.maximum(m, jnp.max(s, axis=0, keepdims=True))
                p = jnp.exp(s - m_new).astype(BF16)
                acc = jnp.exp(m - m_new) * acc + _dot_tn(v_ref[hh, rows, :], p)
                out.append((m_new, acc))
            return tuple(out)

        def two_blocks(j, carry):
            scores(s_b, 2 * j + 1)
            carry = absorb(s_a, 2 * j, carry, False)
            scores(s_a, 2 * j + 2)
            return absorb(s_b, 2 * j + 1, carry, False)

        def last_block(carry):
            return absorb(s_a, i, carry, True)

        def last_two_blocks(carry):
            scores(s_b, i)
            carry = absorb(s_a, i - 1, carry, False)
            return absorb(s_b, i, carry, True)

        scores(s_a, 0)
        init = (jnp.full((1, tile), -1e30, F32), jnp.zeros((LANE, tile), F32))
        carry = lax.fori_loop(0, i // 2, two_blocks, (init, init))
        carry = lax.cond(i % 2 == 0, last_block, last_two_blocks, carry)
        outs = []
        for hh in range(2):
            m, acc = carry[hh]
            l = -acc[HEAD_DIM:HEAD_DIM + 1, :]
            outs.append((acc / l)[:HEAD_DIM, :])
            neg_lse = [part.astype(F32) for part in _split3(-(m + jnp.log(l)))]
            q2t_ref[hh] = _place3(sub, L_LSE, neg_lse, q[hh].astype(F32).T).astype(BF16)
        att_ref[...] = jnp.concatenate(outs, axis=0).T
        pl.when(step == steps - 1)(finish)

    pair = pl.BlockSpec((2, tile, LANE), lambda hp, i: (hp, i, 0))
    full = pl.BlockSpec((2, seq, LANE), lambda hp, i: (hp, 0, 0))
    return pl.pallas_call(
        body, name="attention_forward", grid=(N_HEADS // 2, nb),
        in_specs=[pair, full, full, ANY],
        out_specs=[pl.BlockSpec((tile, LANE), lambda hp, i: (i, hp)),
                   pl.BlockSpec((2, LANE, tile), lambda hp, i: (hp, 0, i)), ANY],
        out_shape=[jax.ShapeDtypeStruct((seq, D_ATT), F32),
                   jax.ShapeDtypeStruct((N_HEADS, LANE, seq), BF16),
                   jax.ShapeDtypeStruct((N_DEV,) + w_out.shape, w_out.dtype)],
        scratch_shapes=[pltpu.VMEM((2, tile, tile), F32), pltpu.VMEM((2, tile, tile), F32),
                        pltpu.SemaphoreType.DMA((7,)), pltpu.SemaphoreType.DMA((7,)), pltpu.SemaphoreType.DMA((1,))],
        compiler_params=_params(("arbitrary", "arbitrary")),
    )(qp, kp, vp, w_out)


def _window_sum(x, halo, window, transposed):
    tile = x.shape[0]

    def split_cat(a):
        hi = a.astype(BF16)
        return jnp.concatenate([hi, (a - hi.astype(F32)).astype(BF16)], axis=1)

    def fold(r):
        return r[:, :LANE] + r[:, LANE:]

    r = lax.broadcasted_iota(jnp.int32, (tile, tile), 0)
    c = lax.broadcasted_iota(jnp.int32, (tile, tile), 1)
    rh = lax.broadcasted_iota(jnp.int32, (HALO, HALO), 0)
    ch = lax.broadcasted_iota(jnp.int32, (HALO, HALO), 1)
    if not transposed:
        band = (c <= r) & (r - c < window)
        edge = (rh + HALO - ch) < window
    else:
        band = (r <= c) & (c - r < window)
        edge = (HALO + ch - rh) < window
    out = fold(_dot(band.astype(BF16), split_cat(x)))
    reach = fold(_dot(edge.astype(BF16), split_cat(halo)))
    if not transposed:
        return jnp.concatenate([out[:HALO] + reach, out[HALO:]], axis=0)
    return jnp.concatenate([out[:tile - HALO], out[tile - HALO:] + reach], axis=0)


def _silu_parts(g):
    sig = _sigmoid(g)
    return g * sig, sig * (1.0 + g * (1.0 - sig))


def _middle(x, tgt, att, g_att, g_pool, p, vecs, pool_vecs, w_out, w_pool, tile):
    seq = x.shape[0]
    nt = seq // tile
    halo_blocks = tile // HALO

    def body(x_ref, tgt_ref, att_ref, ga_ref, gp_ref, p_ref, ph_ref, vec_ref, pvec_ref, wo_ref, wp_ref,
             dxa_ref, do2_ref, dga_ref, dgp_ref, dpooled_ref, dwo_ref, dwp_ref, dvec_ref, dpvec_ref):
        i = pl.program_id(0)

        @pl.when(i == 0)
        def _():
            dwo_ref[...] = jnp.zeros_like(dwo_ref)
            dwp_ref[...] = jnp.zeros_like(dwp_ref)
            dvec_ref[...] = jnp.zeros_like(dvec_ref)
            dpvec_ref[...] = jnp.zeros_like(dpvec_ref)

        gate, b_out, ln_g, ln_b = (vec_ref[k:k + 1, :] for k in range(4))
        b_pool, pool_scale = pvec_ref[0:1, :], pvec_ref[1:2, :]
        x = x_ref[...]
        p = p_ref[...]
        p_halo = ph_ref[...] * jnp.where(i > 0, 1.0, 0.0)
        pos = i * tile + lax.broadcasted_iota(jnp.int32, (tile, 1), 0) + 1

        pooled, mixed = [], []
        for g, window in enumerate(POOL_WINDOWS):
            cols = slice(g * GROUP_DIM, (g + 1) * GROUP_DIM)
            wsum = _window_sum(p[:, cols], p_halo[:, cols], window, False)
            count = jnp.minimum(pos, window).astype(F32)
            pooled.append(wsum / count - p[:, cols])
            mixed.append(_dot(pooled[g].astype(BF16), wp_ref[g]) + b_pool[:, cols])
        mixed = jnp.concatenate(mixed, axis=1)
        pool = mixed * pool_scale

        att = att_ref[...]
        g_att, g_pool = ga_ref[...], gp_ref[...]
        silu_a, dsilu_a = _silu_parts(g_att)
        silu_p, dsilu_p = _silu_parts(g_pool)
        y_in = jnp.concatenate([att * silu_a, pool * silu_p], axis=1)
        y = _dot(y_in.astype(BF16), wo_ref[...]) + b_out
        h = ALPHA * x + gate * y
        mu = jnp.mean(h, axis=1, keepdims=True)
        hc = h - mu
        var = jnp.mean(hc * hc, axis=1, keepdims=True)
        rstd = lax.rsqrt(var + LN_EPS)
        yhat = hc * rstd
        diff = yhat * ln_g + ln_b - tgt_ref[...]
        loss_rows = jnp.sum(diff * diff, axis=1, keepdims=True)
        d_out = diff * (1.0 / D)

        d_yhat = d_out * ln_g
        dh = rstd * (d_yhat - jnp.mean(d_yhat, axis=1, keepdims=True)
                     - yhat * jnp.mean(d_yhat * yhat, axis=1, keepdims=True))
        dxa_ref[...] = ALPHA * dh
        dy = dh * gate
        dyb = dy.astype(BF16)
        lane = _lanes((1, D))
        loss_row = jnp.where(lane == 0, (0.5 / D) * jnp.sum(loss_rows, axis=0, keepdims=True), 0.0)
        dvec_ref[0:1, :] += jnp.sum(dh * y, axis=0, keepdims=True)
        dvec_ref[1:2, :] += jnp.sum(dy, axis=0, keepdims=True)
        dvec_ref[2:3, :] += jnp.sum(d_out * yhat, axis=0, keepdims=True)
        dvec_ref[3:4, :] += jnp.sum(d_out, axis=0, keepdims=True)
        dvec_ref[4:5, :] += loss_row

        dwo_ref[...] += _dot(y_in.T.astype(BF16), dyb)
        d_yin = _dot_nt(dyb, wo_ref[...])
        d_a, d_pl = d_yin[:, :D_ATT], d_yin[:, D_ATT:]
        d_att = d_a * silu_a
        d_att_t = d_att.T
        prod_t = (d_att * att).T
        sub = lax.broadcasted_iota(jnp.int32, (HEAD_DIM, tile), 0)
        for h in range(N_HEADS):
            rows = slice(h * HEAD_DIM, (h + 1) * HEAD_DIM)
            delta = jnp.sum(prod_t[rows], axis=0, keepdims=True)
            extra = _place3(sub, 0, [part.astype(F32) for part in _split3(delta)], 0.0)
            do2_ref[h] = jnp.concatenate([d_att_t[rows], extra], axis=0).astype(BF16)
        dga_ref[...] = d_a * att * dsilu_a
        dgp_ref[...] = d_pl * pool * dsilu_p
        d_pool = d_pl * silu_p
        d_mixed = d_pool * pool_scale
        dpvec_ref[0:1, :] += jnp.sum(d_mixed, axis=0, keepdims=True)
        dpvec_ref[1:2, :] += jnp.sum(d_pool * mixed, axis=0, keepdims=True)
        d_pooled = []
        for g in range(len(POOL_WINDOWS)):
            cols = slice(g * GROUP_DIM, (g + 1) * GROUP_DIM)
            dmb = d_mixed[:, cols].astype(BF16)
            dwp_ref[g] += _dot(pooled[g].T.astype(BF16), dmb)
            d_pooled.append(_dot_nt(dmb, wp_ref[g]))
        dpooled_ref[...] = jnp.concatenate(d_pooled, axis=1)

    tok = lambda width: pl.BlockSpec((tile, width), lambda i: (i, 0))
    whole = lambda a: pl.BlockSpec(a.shape, lambda i: (0,) * a.ndim)
    halo = pl.BlockSpec((HALO, D_POOL), lambda i: (jnp.maximum(i * halo_blocks - 1, 0), 0))
    half = jax.ShapeDtypeStruct((seq, D_ATT), F32)
    outs = [jax.ShapeDtypeStruct((seq, D), F32), jax.ShapeDtypeStruct((N_HEADS, LANE, seq), BF16), half, half, half,
            jax.ShapeDtypeStruct(w_out.shape, F32), jax.ShapeDtypeStruct(w_pool.shape, F32),
            jax.ShapeDtypeStruct(vecs.shape, F32), jax.ShapeDtypeStruct(pool_vecs.shape, F32)]
    return pl.pallas_call(
        body, name="middle", grid=(nt,),
        in_specs=[tok(D), tok(D), tok(D_ATT), tok(D_ATT), tok(D_POOL), tok(D_POOL), halo,
                  whole(vecs), whole(pool_vecs), whole(w_out), whole(w_pool)],
        out_specs=[tok(D), pl.BlockSpec((N_HEADS, LANE, tile), lambda i: (0, 0, i)),
                   tok(D_ATT), tok(D_POOL), tok(D_POOL),
                   whole(w_out), whole(w_pool), whole(vecs), whole(pool_vecs)],
        out_shape=outs,
        compiler_params=_params(("arbitrary",)),
    )(x, tgt, att, g_att, g_pool, p, p, vecs, pool_vecs, w_out, w_pool)


def _attention_backward(q2t, kp, vp, do2t, gw_out, small, tile):
    seq = kp.shape[1]
    nb = seq // tile
    last = N_HEADS // 2 - 1

    def body(qt_ref, k_ref, v_ref, dot_ref, gwo_hbm, small_hbm,
             dq_ref, dk_ref, dv_ref, dcum_ref, g_out_ref, total_ref,
             dq_acc, gwo_ref, r1, s2, r2, sm_sib, sm_chip, sm_recv, small_ref, send_sems, recv_sems):
        hp = pl.program_id(0)
        start, middle, finish = _reduce_stages(
            [gwo_ref], [g_out_ref], [r1], [s2], [r2], (small_ref, total_ref, sm_sib, sm_chip, sm_recv),
            send_sems, recv_sems)

        @pl.when(hp == 0)
        def _():
            pltpu.sync_copy(gwo_hbm, gwo_ref)
            pltpu.sync_copy(small_hbm, small_ref)
            start()

        pl.when(hp == 1)(middle)

        row = lax.broadcasted_iota(jnp.int32, (tile, tile), 0)
        col = lax.broadcasted_iota(jnp.int32, (tile, tile), 1)
        dq_acc[...] = jnp.zeros_like(dq_acc)

        def kv_block(kb, _):
            krows = pl.ds(pl.multiple_of(kb * tile, tile), tile)
            k = [k_ref[hh, krows, :] for hh in range(2)]
            v = [v_ref[hh, krows, :] for hh in range(2)]
            k_t = [k[hh].T for hh in range(2)]

            def q_block(qb, carry, masked):
                qcols = pl.ds(pl.multiple_of(qb * tile, tile), tile)
                out = []
                for hh in range(2):
                    dk, dv = carry[hh]
                    q_t = qt_ref[hh, :, qcols]
                    do_t = dot_ref[hh, :, qcols]
                    s_t = _dot(k[hh], q_t)
                    if masked:
                        s_t = jnp.where(row <= col, s_t, -1e30)
                    p_t = jnp.exp(s_t)
                    ds_t = (p_t * _dot(v[hh], do_t)).astype(BF16)
                    dv = dv + _dot_nt(do_t, p_t.astype(BF16))
                    dk = dk + _dot_nt(q_t, ds_t)
                    dq_acc[hh, :, qcols] += _dot(k_t[hh], ds_t)
                    out.append((dk, dv))
                return tuple(out)

            zero = jnp.zeros((LANE, tile), F32)
            carry = q_block(kb, ((zero, zero), (zero, zero)), True)
            carry = lax.fori_loop(kb + 1, nb, lambda qb, cr: q_block(qb, cr, False), carry)
            for hh in range(2):
                dk = carry[hh][0]
                dk_ref[hh, :, krows] = dk.astype(BF16)
                dv_ref[hh, :, krows] = carry[hh][1].astype(BF16)
                dcum_ref[hh, :, krows] = -dk[L_CK:L_CK + 1, :]
            return 0

        lax.fori_loop(0, nb, kv_block, 0)
        for hh in range(2):
            dq = dq_acc[hh]
            dcum_ref[hh] += dq[L_CQ:L_CQ + 1, :]
            dq_ref[hh] = (dq * Q_SCALE).astype(BF16)
        pl.when(hp == last)(finish)

    pair = pl.BlockSpec((2, seq, LANE), lambda hp: (hp, 0, 0))
    pair_t = pl.BlockSpec((2, LANE, seq), lambda hp: (hp, 0, 0))
    whole = lambda shape: pl.BlockSpec(shape, lambda hp: (0,) * len(shape))
    grad = jax.ShapeDtypeStruct((N_HEADS, LANE, seq), BF16)
    return pl.pallas_call(
        body, name="attention_backward", grid=(N_HEADS // 2,),
        in_specs=[pair_t, pair, pair, pair_t, ANY, ANY],
        out_specs=[pair_t, pair_t, pair_t, pl.BlockSpec((2, 1, seq), lambda hp: (hp, 0, 0)),
                   whole(gw_out.shape[1:]), whole(small.shape)],
        out_shape=[grad, grad, grad, jax.ShapeDtypeStruct((N_HEADS, 1, seq), F32),
                   jax.ShapeDtypeStruct(gw_out.shape[1:], F32), jax.ShapeDtypeStruct(small.shape, F32)],
        scratch_shapes=[pltpu.VMEM((2, LANE, seq), F32), pltpu.VMEM(gw_out.shape, BF16)]
        + _reduce_scratch(gw_out, small)
        + [pltpu.VMEM(small.shape, F32),
           pltpu.SemaphoreType.DMA((N_REDUCE_SEMS + N_SMALL_SEMS,)), pltpu.SemaphoreType.DMA((N_REDUCE_SEMS + N_SMALL_SEMS,))],
        compiler_params=_params(("arbitrary",)),
    )(q2t, kp, vp, do2t, gw_out, small)


def _inproj_backward(dqp, dkp, dvp, d_cum, f, d_pooled, d_ga, d_gp, x, dxa, u, mod, w_main, w_f, tile):
    seq = x.shape[0]
    nt = seq // tile
    halo_blocks = tile // HALO

    def body(dq_ref, dk_ref, dv_ref, dcum_ref, f_ref, dpo_ref, dph_ref, dga_ref, dgp_ref, x_ref, dxa_ref, u_ref,
             mod_ref, w_ref, wf_ref,
             dx_ref, dproj_ref, dwf_ref, db_ref, dbf_ref, dmod_ref, carry_ref):
        step = pl.program_id(0)
        i = nt - 1 - step

        @pl.when(step == 0)
        def _():
            carry_ref[...] = jnp.zeros_like(carry_ref)
            dwf_ref[...] = jnp.zeros_like(dwf_ref)
            db_ref[...] = jnp.zeros_like(db_ref)
            dbf_ref[...] = jnp.zeros_like(dbf_ref)
            dmod_ref[...] = jnp.zeros_like(dmod_ref)

        ones = jnp.ones((8, tile), BF16)

        def emit(chunk, val):
            cols = pl.ds(chunk * COL_CHUNK, COL_CHUNK)
            db_ref[0:1, cols] += jnp.sum(val, axis=0, keepdims=True)
            vb = val.astype(BF16)
            dproj_ref[:, pl.ds((chunk - 3) * COL_CHUNK, COL_CHUNK)] = vb
            return _dot(vb, w_ref[cols, :])

        d_u = jnp.zeros((tile, D), F32)
        for chunk, ref in enumerate((dq_ref, dk_ref, dv_ref)):
            cols = pl.ds(chunk * COL_CHUNK, COL_CHUNK)
            val_t = ref[:, 0:HEAD_DIM, :].reshape(COL_CHUNK, tile)
            db_ref[:, cols] += _dot_nt(ones, val_t)
            d_u += _dot_tn(val_t, w_ref[cols, :])

        d_pooled = dpo_ref[...]
        d_halo = dph_ref[...] * jnp.where(i < nt - 1, 1.0, 0.0)
        pos = i * tile + lax.broadcasted_iota(jnp.int32, (tile, 1), 0) + 1
        d_p = []
        for g, window in enumerate(POOL_WINDOWS):
            cols = slice(g * GROUP_DIM, (g + 1) * GROUP_DIM)
            scaled = d_pooled[:, cols] / jnp.minimum(pos, window).astype(F32)
            d_p.append(_window_sum(scaled, d_halo[:, cols] * (1.0 / window), window, True) - d_pooled[:, cols])
        d_u += emit(3, jnp.concatenate(d_p, axis=1))
        d_u += emit(4, dga_ref[...])
        d_u += emit(5, dgp_ref[...])

        row = lax.broadcasted_iota(jnp.int32, (tile, tile), 0)
        col = lax.broadcasted_iota(jnp.int32, (tile, tile), 1)
        later = (row >= col).astype(BF16)
        d_logf = sum(_dot(part, later) for part in _split3(dcum_ref[:, 0, :])) + carry_ref[:, 0:1]
        carry_ref[:, 0:1] = d_logf[:, 0:1]
        d_f = d_logf * _sigmoid(-f_ref[...].T[0:N_HEADS, :])
        d_f = jnp.concatenate([d_f, jnp.zeros((LANE - N_HEADS, tile), F32)], axis=0)
        dbf_ref[...] += sum(_dot_nt(ones, part) for part in _split3(d_f))
        d_fb = d_f.astype(BF16)
        d_u += _dot_tn(d_fb, wf_ref[...])
        dwf_ref[...] += _dot(d_fb, u_ref[...])

        x = x_ref[...]
        dx_ref[...] = dxa_ref[...] + d_u * mod_ref[0:1, :]
        dmod_ref[0:1, :] += jnp.sum(d_u * x, axis=0, keepdims=True)
        dmod_ref[1:2, :] += jnp.sum(d_u, axis=0, keepdims=True)

    rev = lambda step: nt - 1 - step
    tok = lambda width: pl.BlockSpec((tile, width), lambda s: (rev(s), 0))
    head_block = pl.BlockSpec((N_HEADS, LANE, tile), lambda s: (0, 0, rev(s)))
    whole = lambda a: pl.BlockSpec(a.shape, lambda s: (0,) * a.ndim)
    halo = pl.BlockSpec((HALO, D_POOL), lambda s: (jnp.minimum((rev(s) + 1) * halo_blocks, seq // HALO - 1), 0))
    small = lambda width: jax.ShapeDtypeStruct((8, width), F32)
    n_rest = N_MAIN - OFF_P
    return pl.pallas_call(
        body, name="inproj_backward", grid=(nt,),
        in_specs=[head_block, head_block, head_block, pl.BlockSpec((N_HEADS, 1, tile), lambda s: (0, 0, rev(s))),
                  tok(LANE), tok(D_POOL), halo, tok(D_ATT), tok(D_POOL),
                  tok(D), tok(D), tok(D),
                  whole(mod), whole(w_main), whole(w_f)],
        out_specs=[tok(D), tok(n_rest), pl.BlockSpec((LANE, D), lambda s: (0, 0)),
                   pl.BlockSpec((8, N_MAIN), lambda s: (0, 0)), pl.BlockSpec((8, LANE), lambda s: (0, 0)),
                   pl.BlockSpec((8, D), lambda s: (0, 0))],
        out_shape=[jax.ShapeDtypeStruct((seq, D), F32), jax.ShapeDtypeStruct((seq, n_rest), BF16),
                   jax.ShapeDtypeStruct((LANE, D), F32), small(N_MAIN), small(LANE), small(D)],
        scratch_shapes=[pltpu.VMEM((8, LANE), F32)],
        compiler_params=_params(("arbitrary",)),
    )(dqp, dkp, dvp, d_cum, f, d_pooled, d_pooled, d_ga, d_gp, x, dxa, u, mod, w_main, w_f)


def _weight_grad(dproj, u, k_tile):
    seq, n_cols = dproj.shape
    nk = seq // k_tile

    def body(dp_ref, u_ref, out_ref):
        @pl.when(pl.program_id(1) == 0)
        def _():
            out_ref[...] = jnp.zeros_like(out_ref)

        out_ref[...] += _dot_tn(dp_ref[...], u_ref[...])

    return pl.pallas_call(
        body, name="weight_grad", grid=(n_cols // COL_CHUNK, nk),
        in_specs=[pl.BlockSpec((k_tile, COL_CHUNK), lambda n, k: (k, n)),
                  pl.BlockSpec((k_tile, D), lambda n, k: (k, 0))],
        out_specs=pl.BlockSpec((COL_CHUNK, D), lambda n, k: (n, 0)),
        out_shape=jax.ShapeDtypeStruct((n_cols, D), F32),
        compiler_params=_params(("arbitrary", "arbitrary")),
    )(dproj, u)


def _weight_grad_heads(grad_t, u, k_tile, name):
    seq = u.shape[0]
    nk = seq // k_tile

    def body(g_ref, u_ref, out_ref):
        @pl.when(pl.program_id(0) == 0)
        def _():
            out_ref[...] = jnp.zeros_like(out_ref)

        out_ref[...] += _dot(g_ref[...].reshape(N_HEADS * HEAD_DIM, k_tile), u_ref[...])

    return pl.pallas_call(
        body, name=name, grid=(nk,),
        in_specs=[pl.BlockSpec((N_HEADS, HEAD_DIM, k_tile), lambda k: (0, 0, k)),
                  pl.BlockSpec((k_tile, D), lambda k: (k, 0))],
        out_specs=pl.BlockSpec((N_HEADS * HEAD_DIM, D), lambda k: (0, 0)),
        out_shape=jax.ShapeDtypeStruct((N_HEADS * HEAD_DIM, D), F32),
        compiler_params=_params(("arbitrary",)),
    )(grad_t, u)


def _adamw(w, g, m, v):
    m = ADAM_B1 * m + (1.0 - ADAM_B1) * g
    v = ADAM_B2 * v + (1.0 - ADAM_B2) * (g * g)
    m_hat = m / (1.0 - ADAM_B1 ** ADAM_STEP)
    v_hat = v / (1.0 - ADAM_B2 ** ADAM_STEP)
    delta = -ADAM_LR * (m_hat / (jnp.sqrt(v_hat) + ADAM_EPS) + ADAM_WD * w)
    return delta, m, v


def _adamw_call(g, w, m, v, lead_tile, name):
    nr = w.shape[0] // lead_tile

    def body(gi_ref, w_ref, m_ref, v_ref, g_ref, d_ref, nm_ref, nv_ref):
        g = gi_ref[...]
        g_ref[...] = g
        d_ref[...], nm_ref[...], nv_ref[...] = _adamw(w_ref[...], g, m_ref[...], v_ref[...])

    blk = pl.BlockSpec((lead_tile,) + w.shape[1:], lambda r: (r,) + (0,) * (w.ndim - 1))
    shape = jax.ShapeDtypeStruct(w.shape, F32)
    return pl.pallas_call(
        body, name=name, grid=(nr,),
        in_specs=[blk, blk, blk, blk], out_specs=[blk, blk, blk, blk],
        out_shape=[shape, shape, shape, shape],
        compiler_params=_params(("arbitrary",)),
    )(g, w, m, v)


def _sum_adamw(parts, w, m, v, row_tile, name):
    rows, cols = w.shape
    nr = rows // row_tile

    def body(parts_ref, w_ref, m_ref, v_ref, g_ref, d_ref, nm_ref, nv_ref):
        g = parts_ref[0]
        for k in range(1, N_DEV):
            g = g + parts_ref[k]
        g_ref[...] = g
        d_ref[...], nm_ref[...], nv_ref[...] = _adamw(w_ref[...], g, m_ref[...], v_ref[...])

    blk = pl.BlockSpec((row_tile, cols), lambda r: (r, 0))
    shape = jax.ShapeDtypeStruct(w.shape, F32)
    return pl.pallas_call(
        body, name=name, grid=(nr,),
        in_specs=[pl.BlockSpec((N_DEV, row_tile, cols), lambda r: (0, r, 0)), blk, blk, blk],
        out_specs=[blk, blk, blk, blk],
        out_shape=[shape, shape, shape, shape],
        compiler_params=_params(("arbitrary",)),
    )(parts, w, m, v)


def _ada_adamw(sc_t, d_ada, w, m, v):
    def body(sc_ref, d_ref, w_ref, m_ref, v_ref, g_ref, dl_ref, nm_ref, nv_ref):
        g = sc_ref[:, 0:1] * d_ref[0:1, :]
        for b in range(1, N_DEV):
            g = g + sc_ref[:, b:b + 1] * d_ref[b:b + 1, :]
        g_ref[...] = g
        dl_ref[...], nm_ref[...], nv_ref[...] = _adamw(w_ref[...], g, m_ref[...], v_ref[...])

    shape = jax.ShapeDtypeStruct(w.shape, F32)
    return pl.pallas_call(
        body, name="ada_adamw", out_shape=[shape, shape, shape, shape], compiler_params=_params(),
    )(sc_t, d_ada, w, m, v)


F_LO, F_HI = 3 * D_ATT, 3 * D_ATT + N_HEADS


def _split_forget(a, axis):
    idx = lambda lo, hi: tuple(slice(lo, hi) if d == axis else slice(None) for d in range(a.ndim))
    pad = [(0, LANE - N_HEADS) if d == axis else (0, 0) for d in range(a.ndim)]
    return jnp.concatenate([a[idx(0, F_LO)], a[idx(F_HI, D_IN)]], axis=axis), jnp.pad(a[idx(F_LO, F_HI)], pad)


def _join_forget(main, f, axis):
    idx = lambda lo, hi: tuple(slice(lo, hi) if d == axis else slice(None) for d in range(main.ndim))
    return jnp.concatenate([main[idx(0, F_LO)], f[idx(0, N_HEADS)], main[idx(F_LO, N_MAIN)]], axis=axis)


_EARLY = ((("w_pool", 65536), ("b_pool", 512), ("pool_scale", 512), ("b_out", 1024), ("ln_g", 1024), ("ln_b", 1024),
           ("loss", 128)), 552)
_LATE = ((("b_in", 3200),), 32)


def _pack_small(parts, layout):
    spec, rows = layout
    flat = []
    for name, size in spec:
        a = parts[name].reshape(-1)
        flat.append(jnp.pad(a, (0, size - a.shape[0])))
    flat = jnp.concatenate(flat)
    flat = jnp.pad(flat, (0, rows * LANE - flat.shape[0]))
    return flat.reshape(rows, LANE)


def _unpack_small(packed, shapes, layout):
    flat = packed.reshape(-1)
    out, off = {}, 0
    for name, size in layout[0]:
        n = 1
        for s in shapes[name]:
            n *= s
        out[name] = flat[off:off + n].reshape(shapes[name])
        off += size
    return out


def kernel(x, c, w_ada, b_ada, w_in, b_in, w_pool_mix, b_pool_mix, pool_scale, w_out, b_out, ln_g, ln_b, loss_target, m_w_ada, m_b_ada, m_w_in, m_b_in, m_w_pool_mix, m_b_pool_mix, m_pool_scale, m_w_out, m_b_out, m_ln_g, m_ln_b, v_w_ada, v_b_ada, v_w_in, v_b_in, v_w_pool_mix, v_b_pool_mix, v_pool_scale, v_w_out, v_b_out, v_ln_g, v_ln_b):
    seq = x.shape[1]
    tile = min(256, seq)
    attn_tile = min(512, max(128, seq // 4))
    me = _dev_index(*_mesh_pos())
    x2, tgt = x[0], loss_target[0]

    rows_in = D_IN // N_DEV
    c_all, w_in_g = _all_gather([jnp.pad(c, ((0, 7), (0, 0))), w_in[0].T.astype(BF16)], "gather_weights")
    sc_all, ada_part = _ada_forward(c_all[:, 0, :], w_ada[0])
    (ada_mine,) = _exchange([ada_part.reshape(N_DEV, 1, -1)], [], "exchange_ada")
    ada = ada_mine.reshape(1, D_ADA) + b_ada
    shift, scale, gate = ada[:, 0:D], ada[:, D:2 * D], ada[:, 2 * D:]
    mod = jnp.concatenate([1.0 + scale, shift, jnp.zeros((6, D), F32)], axis=0)

    w_main, w_f = _split_forget(w_in_g.reshape(D_IN, D), 0)
    b_main, b_f = _split_forget(b_in, 1)

    qp, kp, vp, f, p, g_att, g_pool, u = _inproj_forward(x2, mod, w_main, w_f, b_main, b_f, tile)
    att, q2t, w_out_g = _attention_forward(qp, kp, vp, w_out[0].astype(BF16), attn_tile)

    vecs = jnp.concatenate([gate, b_out, ln_g, ln_b, jnp.zeros((4, D), F32)], axis=0)
    pool_vecs = jnp.concatenate([b_pool_mix.reshape(1, D_POOL), pool_scale, jnp.zeros((6, D_POOL), F32)], axis=0)
    dxa, do2, d_ga, d_gp, d_pooled, dw_out, dw_pool, dvec, dpvec = _middle(
        x2, tgt, att, g_att, g_pool, p, vecs, pool_vecs, w_out_g.reshape(D, D), w_pool_mix[0].astype(BF16), tile)

    gw_out = dw_out.reshape(N_DEV, D // N_DEV, D).astype(BF16)
    early = _pack_small({"w_pool": dw_pool, "b_pool": dpvec[0:1], "pool_scale": dpvec[1:2], "b_out": dvec[1:2],
                         "ln_g": dvec[2:3], "ln_b": dvec[3:4], "loss": dvec[4:5, 0:LANE]}, _EARLY)
    dqp, dkp, dvp, d_cum, g_out, early_sum = _attention_backward(q2t, kp, vp, do2, gw_out, early, attn_tile)
    dx, dproj, dw_f, db_main, db_f, dmod = _inproj_backward(
        dqp, dkp, dvp, d_cum, f, d_pooled, d_ga, d_gp, x2, dxa, u, mod, w_main, w_f, tile)
    k_tile = min(1024, seq)
    dw_q, dw_k, dw_v = (_weight_grad_heads(g, u, k_tile, "weight_grad_" + n)
                        for g, n in ((dqp, "q"), (dkp, "k"), (dvp, "v")))
    dw_rest = _weight_grad(dproj, u, k_tile)

    dw_main = jnp.concatenate([dw_q, dw_k, dw_v, dw_rest], axis=0)
    gw_in = _join_forget(dw_main, dw_f, 0).reshape(N_DEV, rows_in, D).astype(BF16)
    d_ada = jnp.concatenate([dmod[1:2], dmod[0:1], dvec[0:1]], axis=1)
    late = _pack_small({"b_in": _join_forget(db_main[0:1], db_f[0:1], 1)}, _LATE)
    g_in_rows, late_sum, d_ada_all = _reduce_grads(gw_in, late, d_ada.reshape(D_ADA // LANE, LANE))

    def rows3(a):
        return a[0].T.reshape(rows_in, D // LANE, LANE)

    outs_in = _adamw_call(g_in_rows.reshape(rows_in, D // LANE, LANE), rows3(w_in), rows3(m_w_in), rows3(v_w_in),
                          rows_in // 5, "adamw_w_in")
    g_w_in, d_w_in, nm_w_in, nv_w_in = (a.reshape(rows_in, D).T for a in outs_in)
    g_w_out, d_w_out, nm_w_out, nv_w_out = _adamw_call(g_out, w_out[0], m_w_out[0], v_w_out[0], D // N_DEV, "adamw_w_out")

    def small_adamw(grad_sum, weights, first, second, layout, name):
        packed = _adamw_call(grad_sum, _pack_small(weights, layout), _pack_small(first, layout),
                             _pack_small(second, layout), layout[1], name)
        shapes = {k: a.shape for k, a in weights.items()}
        return [_unpack_small(a, shapes, layout) for a in packed]

    zero = jnp.zeros((1,), F32)
    smalls = small_adamw(
        early_sum,
        {"w_pool": w_pool_mix, "b_pool": b_pool_mix, "pool_scale": pool_scale, "b_out": b_out, "ln_g": ln_g,
         "ln_b": ln_b, "loss": zero},
        {"w_pool": m_w_pool_mix, "b_pool": m_b_pool_mix, "pool_scale": m_pool_scale, "b_out": m_b_out,
         "ln_g": m_ln_g, "ln_b": m_ln_b, "loss": zero},
        {"w_pool": v_w_pool_mix, "b_pool": v_b_pool_mix, "pool_scale": v_pool_scale, "b_out": v_b_out,
         "ln_g": v_ln_g, "ln_b": v_ln_b, "loss": zero}, _EARLY, "adamw_small")
    b_ins = small_adamw(late_sum, {"b_in": b_in}, {"b_in": m_b_in}, {"b_in": v_b_in}, _LATE, "adamw_b_in")
    g_s, d_s, nm_s, nv_s = ({**a, **b} for a, b in zip(smalls, b_ins))
    loss = g_s["loss"][0]

    ada_rows = D_ADA // LANE
    b_ada_outs = _sum_adamw(d_ada_all, b_ada.reshape(ada_rows, LANE), m_b_ada.reshape(ada_rows, LANE),
                            v_b_ada.reshape(ada_rows, LANE), ada_rows, "adamw_b_ada")
    g_b_ada, d_b_ada, nm_b_ada, nv_b_ada = (a.reshape(1, D_ADA) for a in b_ada_outs)
    d_ada_local = lax.dynamic_slice_in_dim(d_ada_all.reshape(N_DEV, D_ADA), me * (D_ADA // N_DEV), D_ADA // N_DEV, axis=1)
    g_w_ada, d_w_ada, nm_w_ada, nv_w_ada = _ada_adamw(sc_all.T, d_ada_local, w_ada[0], m_w_ada[0], v_w_ada[0])

    def ordered(w_ada_, b_ada_, w_in_, w_out_, s):
        return (w_ada_[None], b_ada_, w_in_[None], s["b_in"], s["w_pool"], s["b_pool"], s["pool_scale"],
                w_out_[None], s["b_out"], s["ln_g"], s["ln_b"])

    return (loss, dx[None],
            *ordered(g_w_ada, g_b_ada, g_w_in, g_w_out, g_s),
            *ordered(d_w_ada, d_b_ada, d_w_in, d_w_out, d_s),
            *ordered(nm_w_ada, nm_b_ada, nm_w_in, nm_w_out, nm_s),
            *ordered(nv_w_ada, nv_b_ada, nv_w_in, nv_w_out, nv_s))
```

```python
import jax
import jax.numpy as jnp
from jax import lax
from jax.experimental import pallas as pl
from jax.experimental.pallas import tpu as pltpu

F32 = jnp.float32
BF16 = jnp.bfloat16

N_DEV = 8
D = 1024
N_HEADS = 8
HEAD_DIM = 64
D_ATT = 512
D_POOL = 512
POOL_WINDOWS = (2, 4, 8, 16)
GROUP_DIM = 128
HALO = 16
LANE = 128
D_IN = 3080
D_ADA = 3072
N_MAIN = 3072
OFF_P = 1536
COL_CHUNK = 512
Q_SCALE = 0.125
LN_EPS = 1e-5
ALPHA = 2.0 ** 0.25
L_CQ, L_CK, L_LSE = 64, 67, 70

ADAM_LR, ADAM_B1, ADAM_B2, ADAM_EPS, ADAM_WD, ADAM_STEP = 0.001, 0.9, 0.999, 1e-08, 0.01, 10
VMEM_LIMIT = 56 * 1024 * 1024

MESH = pl.DeviceIdType.MESH
ANY = pl.BlockSpec(memory_space=pl.ANY)


def _params(sem=None, vmem=VMEM_LIMIT):
    return pltpu.CompilerParams(dimension_semantics=sem, vmem_limit_bytes=vmem)


def _split3(a):
    hi = a.astype(BF16)
    r = a - hi.astype(F32)
    mid = r.astype(BF16)
    lo = (r - mid.astype(F32)).astype(BF16)
    return hi, mid, lo


def _dot(a, b):
    return jnp.dot(a, b, preferred_element_type=F32)


def _dot_nt(a, b):
    return lax.dot_general(a, b, (((1,), (1,)), ((), ())), preferred_element_type=F32)


def _dot_tn(a, b):
    return lax.dot_general(a, b, (((0,), (0,)), ((), ())), preferred_element_type=F32)


def _dot3(m01, a):
    hi, mid, lo = _split3(a)
    return _dot(m01, hi) + _dot(m01, mid) + _dot(m01, lo)


def _sigmoid(z):
    return 1.0 / (1.0 + jnp.exp(-z))


def _lanes(shape):
    return lax.broadcasted_iota(jnp.int32, shape, len(shape) - 1)


def _place3(lane, base, parts, other):
    out = other
    for j in range(3):
        out = jnp.where(lane == base + j, parts[j], out)
    return out


def _mesh_pos():
    return lax.axis_index("x"), lax.axis_index("y"), lax.axis_index("c")


def _dev_index(px, py, pc):
    return 4 * px + 2 * py + pc


def _all_gather(blocks, name):
    n = len(blocks)

    def body(*refs):
        ins, outs = refs[:n], refs[n:2 * n]
        send_sems, recv_sems, local_sems = refs[2 * n:]
        x, y, c = _mesh_pos()
        me, sibling = (x, y, c), (x, y, 1 - c)
        chips = [(1 - x, y), (x, 1 - y), (1 - x, 1 - y)]

        def copy(a, k, block, to, src=None):
            slot = outs[a].at[_dev_index(*block)]
            return pltpu.make_async_remote_copy(
                src_ref=slot if src is None else src, dst_ref=slot,
                send_sem=send_sems.at[a, k], recv_sem=recv_sems.at[a, k],
                device_id=to, device_id_type=MESH)

        mine = [pltpu.make_async_copy(ins[a], outs[a].at[_dev_index(*me)], local_sems.at[a]) for a in range(n)]
        for cp in mine:
            cp.start()
        first = []
        for a in range(n):
            first.append(copy(a, 0, me, sibling, src=ins[a]))
            first += [copy(a, 1 + j, me, (*chip, c), src=ins[a]) for j, chip in enumerate(chips)]
        for cp in first:
            cp.start()
        passed = []
        for j, chip in enumerate(chips):
            for a in range(n):
                copy(a, 1 + j, (*chip, c), me).wait_recv()
                fwd = copy(a, 4 + j, (*chip, c), sibling)
                fwd.start()
                passed.append(fwd)
        for a in range(n):
            copy(a, 0, sibling, me).wait_recv()
            for j, chip in enumerate(chips):
                copy(a, 4 + j, (*chip, 1 - c), me).wait_recv()
        for cp in first + passed:
            cp.wait_send()
        for cp in mine:
            cp.wait()

    return pl.pallas_call(
        body, name=name,
        out_shape=[jax.ShapeDtypeStruct((N_DEV,) + b.shape, b.dtype) for b in blocks],
        in_specs=[ANY] * n, out_specs=[ANY] * n,
        scratch_shapes=[pltpu.SemaphoreType.DMA((n, 7)), pltpu.SemaphoreType.DMA((n, 7)),
                        pltpu.SemaphoreType.DMA((n,))],
    )(*blocks)


def _exchange(scatter, gather, name):
    ns, n = len(scatter), len(scatter) + len(gather)
    arrays = list(scatter) + list(gather)

    def body(*refs):
        ins, outs = refs[:n], refs[n:2 * n]
        send_sems, recv_sems, local_sems = refs[2 * n:]
        x, y, c = _mesh_pos()
        me = _dev_index(x, y, c)
        peers = []
        for p in range(1, N_DEV):
            px, py, pc = (p >> 2) & 1, (p >> 1) & 1, p & 1
            peers.append((1 - x if px else x, 1 - y if py else y, 1 - c if pc else c))

        def src_for(a, slot):
            return ins[a].at[slot] if a < ns else ins[a]

        def copy(a, k, to):
            return pltpu.make_async_remote_copy(
                src_ref=src_for(a, _dev_index(*to)), dst_ref=outs[a].at[me],
                send_sem=send_sems.at[a, k], recv_sem=recv_sems.at[a, k],
                device_id=to, device_id_type=MESH)

        def arrival(a, k, frm):
            slot = _dev_index(*frm)
            return pltpu.make_async_remote_copy(
                src_ref=src_for(a, slot), dst_ref=outs[a].at[slot],
                send_sem=send_sems.at[a, k], recv_sem=recv_sems.at[a, k],
                device_id=frm, device_id_type=MESH)

        mine = [pltpu.make_async_copy(src_for(a, me), outs[a].at[me], local_sems.at[a]) for a in range(n)]
        for cp in mine:
            cp.start()
        sends = [copy(a, k, to) for k, to in enumerate(peers) for a in range(n)]
        for cp in sends:
            cp.start()
        for k, frm in enumerate(peers):
            for a in range(n):
                arrival(a, k, frm).wait_recv()
        for cp in sends:
            cp.wait_send()
        for cp in mine:
            cp.wait()

    out_shape = [jax.ShapeDtypeStruct(a.shape, a.dtype) for a in scatter]
    out_shape += [jax.ShapeDtypeStruct((N_DEV,) + a.shape, a.dtype) for a in gather]
    return pl.pallas_call(
        body, name=name, out_shape=out_shape,
        in_specs=[ANY] * n, out_specs=[ANY] * n,
        scratch_shapes=[pltpu.SemaphoreType.DMA((n, 7)), pltpu.SemaphoreType.DMA((n, 7)),
                        pltpu.SemaphoreType.DMA((n,))],
    )(*arrays)


def _gather_stages(src_ref, out_ref, send_sems, recv_sems, local_sem):
    x, y, c = _mesh_pos()
    me, sibling = (x, y, c), (x, y, 1 - c)
    chips = [(1 - x, y), (x, 1 - y), (1 - x, 1 - y)]

    def copy(k, block, to, src=None):
        slot = out_ref.at[_dev_index(*block)]
        return pltpu.make_async_remote_copy(
            src_ref=slot if src is None else src, dst_ref=slot, send_sem=send_sems.at[k], recv_sem=recv_sems.at[k],
            device_id=to, device_id_type=MESH)

    mine = pltpu.make_async_copy(src_ref, out_ref.at[_dev_index(*me)], local_sem)
    first = [copy(0, me, sibling, src=src_ref)] + [copy(1 + j, me, (*chip, c), src=src_ref) for j, chip in enumerate(chips)]
    passed = [copy(4 + j, (*chip, c), sibling) for j, chip in enumerate(chips)]

    def start():
        mine.start()
        for cp in first:
            cp.start()

    def forward():
        for j, chip in enumerate(chips):
            copy(1 + j, (*chip, c), me).wait_recv()
            passed[j].start()

    def finish():
        copy(0, sibling, me).wait_recv()
        for j, chip in enumerate(chips):
            copy(4 + j, (*chip, 1 - c), me).wait_recv()
        for cp in first + passed:
            cp.wait_send()
        mine.wait()

    return start, forward, finish


N_REDUCE_SEMS = 7
N_SMALL_SEMS = 4 + 7


def _reduce_stages(ins, gs, r1, s2, r2, small, send_sems, recv_sems, rows=None):
    n = len(ins)
    x, y, c = _mesh_pos()
    me = _dev_index(x, y, c)
    sibling = (x, y, 1 - c)
    chips = [(x, y), (1 - x, y), (x, 1 - y), (1 - x, 1 - y)]
    peers = []
    for p in range(1, N_DEV):
        px, py, pc = (p >> 2) & 1, (p >> 1) & 1, p & 1
        peers.append((1 - x if px else x, 1 - y if py else y, 1 - c if pc else c))
    base_small = N_REDUCE_SEMS * n

    def remote(src, dst, k, to):
        return pltpu.make_async_remote_copy(src_ref=src, dst_ref=dst, send_sem=send_sems.at[k],
                                            recv_sem=recv_sems.at[k], device_id=to, device_id_type=MESH)

    def level1(a, q):
        return remote(ins[a].at[_dev_index(*chips[q], 1 - c)], r1[a].at[q], N_REDUCE_SEMS * a + q, sibling)

    def level2(a, j):
        return remote(s2[a].at[j], r2[a].at[j], N_REDUCE_SEMS * a + 4 + j, (*chips[j + 1], c))

    if small is not None:
        small_ref, total_ref, sm_sib, sm_chip, sm_recv = small
        to_sibling = remote(small_ref, sm_sib, base_small, sibling)
        to_chips = [remote(sm_chip, sm_recv.at[j], base_small + 1 + j, (*chips[j + 1], c)) for j in range(3)]
    if rows is not None:
        rows_ref, land_ref, all_ref = rows
        row_sends = [remote(rows_ref, land_ref.at[me], base_small + 4 + k, to) for k, to in enumerate(peers)]

    def start():
        for a in range(n):
            for q in range(4):
                level1(a, q).start()
        if small is not None:
            to_sibling.start()
        if rows is not None:
            for cp in row_sends:
                cp.start()
            land_ref[me] = rows_ref[...]

    def middle():
        for a in range(n):
            for q in (1, 2, 3, 0):
                level1(a, q).wait_recv()
                pair = ins[a][_dev_index(*chips[q], c)].astype(F32) + r1[a][q].astype(F32)
                if q == 0:
                    gs[a][...] = pair
                else:
                    s2[a][q - 1] = pair.astype(BF16)
                    level2(a, q - 1).start()
        if small is not None:
            to_sibling.wait_recv()
            sm_chip[...] = small_ref[...] + sm_sib[...]
            for cp in to_chips:
                cp.start()

    def finish():
        for a in range(n):
            own = gs[a][...]
            for j in range(3):
                level2(a, j).wait_recv()
                own = own + r2[a][j].astype(F32)
            gs[a][...] = own
            for q in range(4):
                level1(a, q).wait_send()
            for j in range(3):
                level2(a, j).wait_send()
        if small is not None:
            for cp in to_chips:
                cp.wait_recv()
            total = None
            for ax in range(2):
                for ay in range(2):
                    dx, dy = x != ax, y != ay
                    term = jnp.where(dx, jnp.where(dy, sm_recv[2], sm_recv[0]), jnp.where(dy, sm_recv[1], sm_chip[...]))
                    total = term if total is None else total + term
            total_ref[...] = total
            for cp in [to_sibling] + to_chips:
                cp.wait_send()
        if rows is not None:
            for k, frm in enumerate(peers):
                remote(rows_ref, land_ref.at[_dev_index(*frm)], base_small + 4 + k, frm).wait_recv()
            all_ref[...] = land_ref[...]
            for cp in row_sends:
                cp.wait_send()

    return start, middle, finish


def _reduce_scratch(shard, small, rows=None):
    out = [pltpu.VMEM((lead,) + shard.shape[1:], BF16) for lead in (4, 3, 3)]
    out += [pltpu.VMEM(small.shape, F32), pltpu.VMEM(small.shape, F32), pltpu.VMEM((3,) + small.shape, F32)]
    if rows is not None:
        out.append(pltpu.VMEM((N_DEV,) + rows.shape, F32))
    return out


def _reduce_grads(gw_in, small, rows):
    def body(in_ref, small_ref, rows_ref, g_ref, total_ref, rows_all_ref,
             r1, s2, r2, sm_sib, sm_chip, sm_recv, rows_land, send_sems, recv_sems):
        start, middle, finish = _reduce_stages(
            [in_ref], [g_ref], [r1], [s2], [r2], (small_ref, total_ref, sm_sib, sm_chip, sm_recv),
            send_sems, recv_sems, rows=(rows_ref, rows_land, rows_all_ref))
        start()
        middle()
        finish()

    return pl.pallas_call(
        body, name="reduce_grads",
        out_shape=[jax.ShapeDtypeStruct(gw_in.shape[1:], F32), jax.ShapeDtypeStruct(small.shape, F32),
                   jax.ShapeDtypeStruct((N_DEV,) + rows.shape, F32)],
        scratch_shapes=_reduce_scratch(gw_in, small, rows)
        + [pltpu.SemaphoreType.DMA((N_REDUCE_SEMS + N_SMALL_SEMS,))] * 2,
        compiler_params=_params(),
    )(gw_in, small, rows)


def _ada_forward(c_all, w_ada):
    def body(c_ref, w_ref, sc_ref, part_ref):
        cc = c_ref[...]
        sc = cc * _sigmoid(cc)
        sc_ref[...] = sc
        part_ref[...] = _dot3_rhs(sc, w_ref[...])

    return pl.pallas_call(
        body, name="ada_forward",
        out_shape=[jax.ShapeDtypeStruct(c_all.shape, F32),
                   jax.ShapeDtypeStruct((N_DEV, w_ada.shape[1]), F32)],
        compiler_params=_params(),
    )(c_all, w_ada)


def _dot3_rhs(a, b):
    a0, a1, a2 = _split3(a)
    b0, b1, b2 = _split3(b)
    return (_dot(a0, b0) + (_dot(a0, b1) + _dot(a1, b0))
            + (_dot(a0, b2) + _dot(a1, b1) + _dot(a2, b0)))


def _inproj_forward(x, mod, w_main, w_f, b_main, b_f, tile):
    seq = x.shape[0]
    nt = seq // tile

    def body(x_ref, mod_ref, w_ref, wf_ref, b_ref, bf_ref,
             qp_ref, kp_ref, vp_ref, f_ref, p_ref, ga_ref, gp_ref, u_ref, carry_ref):
        i = pl.program_id(0)

        @pl.when(i == 0)
        def _():
            carry_ref[...] = jnp.zeros_like(carry_ref)

        u = x_ref[...] * mod_ref[0:1, :] + mod_ref[1:2, :]
        ub = u.astype(BF16)
        u_ref[...] = ub

        f = _dot_nt(ub, wf_ref[...]) + bf_ref[...]
        f_ref[...] = f
        lane = _lanes((tile, LANE))
        log_f = jnp.where(lane < N_HEADS, jnp.minimum(f, 0.0) - jnp.log(1.0 + jnp.exp(-jnp.abs(f))), 0.0)
        row = lax.broadcasted_iota(jnp.int32, (tile, tile), 0)
        col = lax.broadcasted_iota(jnp.int32, (tile, tile), 1)
        tri = (row >= col).astype(BF16)
        cum = _dot3(tri, log_f) + carry_ref[0:1, :]
        carry_ref[0:1, :] = cum[tile - 1:tile, :]
        cq = [part.astype(F32) for part in _split3(cum)]
        ck = [part.astype(F32) for part in _split3(-cum)]

        def proj(chunk):
            cols = pl.ds(chunk * COL_CHUNK, COL_CHUNK)
            return _dot_nt(ub, w_ref[cols, :]) + b_ref[:, cols]

        def head_tiles(r):
            for pair in range(N_HEADS // 2):
                both = r[:, pair * LANE:(pair + 1) * LANE]
                yield 2 * pair, both
                yield 2 * pair + 1, pltpu.roll(both, HEAD_DIM, 1)

        for h, val in head_tiles(proj(0)):
            extra = jnp.where((lane >= L_CK) & (lane < L_CK + 3), 1.0, 0.0)
            extra = _place3(lane, L_CQ, [part[:, h:h + 1] for part in cq], extra)
            qp_ref[h] = jnp.where(lane < HEAD_DIM, val * Q_SCALE, extra).astype(BF16)
        for h, val in head_tiles(proj(1)):
            ones = ((lane >= L_CQ) & (lane < L_CQ + 3)) | ((lane >= L_LSE) & (lane < L_LSE + 3))
            extra = _place3(lane, L_CK, [part[:, h:h + 1] for part in ck], jnp.where(ones, 1.0, 0.0))
            kp_ref[h] = jnp.where(lane < HEAD_DIM, val, extra).astype(BF16)
        for h, val in head_tiles(proj(2)):
            extra = jnp.where((lane >= HEAD_DIM) & (lane < HEAD_DIM + 3), -1.0, 0.0)
            vp_ref[h] = jnp.where(lane < HEAD_DIM, val, extra).astype(BF16)
        p_ref[...] = proj(3)
        ga_ref[...] = proj(4)
        gp_ref[...] = proj(5)

    head_block = pl.BlockSpec((N_HEADS, tile, LANE), lambda i: (0, i, 0))
    tok = lambda width: pl.BlockSpec((tile, width), lambda i: (i, 0))
    whole = lambda a: pl.BlockSpec(a.shape, lambda i: (0,) * a.ndim)
    padded = jax.ShapeDtypeStruct((N_HEADS, seq, LANE), BF16)
    half = jax.ShapeDtypeStruct((seq, D_ATT), F32)
    return pl.pallas_call(
        body, name="inproj_forward", grid=(nt,),
        in_specs=[tok(D), whole(mod), whole(w_main), whole(w_f), whole(b_main), whole(b_f)],
        out_specs=[head_block, head_block, head_block, tok(LANE), tok(D_POOL), tok(D_ATT), tok(D_POOL),
                   tok(D)],
        out_shape=[padded, padded, padded, jax.ShapeDtypeStruct((seq, LANE), F32), half, half, half,
                   jax.ShapeDtypeStruct((seq, D), BF16)],
        scratch_shapes=[pltpu.VMEM((8, LANE), F32)],
        compiler_params=_params(("arbitrary",)),
    )(x, mod, w_main, w_f, b_main, b_f)


def _attention_forward(qp, kp, vp, w_out, tile):
    seq = qp.shape[1]
    nb = seq // tile
    steps = (N_HEADS // 2) * nb

    def body(q_ref, k_ref, v_ref, wo_ref, att_ref, q2t_ref, wo_all_ref, s_a, s_b, m_ref, acc_ref,
             send_sems, recv_sems, local_sem):
        step = pl.program_id(0) * nb + pl.program_id(1)
        start, forward, finish = _gather_stages(wo_ref, wo_all_ref, send_sems, recv_sems, local_sem.at[0])
        pl.when(step == 0)(start)
        pl.when(step == steps // 2)(forward)

        i = pl.program_id(1)
        sub = lax.broadcasted_iota(jnp.int32, (LANE, tile), 0)
        row = lax.broadcasted_iota(jnp.int32, (tile, tile), 0)
        col = lax.broadcasted_iota(jnp.int32, (tile, tile), 1)
        q = [q_ref[0], q_ref[1]]

        def scores(buf, kb):
            rows = pl.ds(pl.multiple_of(kb * tile, tile), tile)
            for hh in range(2):
                buf[hh] = _dot_nt(k_ref[hh, rows, :], q[hh])

        def absorb(buf, kb, masked):
            rows = pl.ds(pl.multiple_of(kb * tile, tile), tile)
            for hh in range(2):
                m = m_ref[hh, 0:1, :]
                s = buf[hh]
                if masked:
                    s = jnp.where(row <= col, s, -1e30)
                m_new = jnp.maximum(m, jnp.max(s, axis=0, keepdims=True))
                p = jnp.exp(s - m_new).astype(BF16)
                acc_ref[hh] = jnp.exp(m - m_new) * acc_ref[hh] + _dot_tn(v_ref[hh, rows, :], p)
                m_ref[hh, 0:1, :] = m_new

        def two_blocks(j, _):
            scores(s_b, 2 * j + 1)
            absorb(s_a, 2 * j, False)
            scores(s_a, 2 * j + 2)
            absorb(s_b, 2 * j + 1, False)
            return 0

        def last_block():
            absorb(s_a, i, True)

        def last_two_blocks():
            scores(s_b, i)
            absorb(s_a, i - 1, False)
            absorb(s_b, i, True)

        scores(s_a, 0)
        m_ref[...] = jnp.full(m_ref.shape, -1e30, F32)
        acc_ref[...] = jnp.zeros_like(acc_ref)
        lax.fori_loop(0, i // 2, two_blocks, 0)
        lax.cond(i % 2 == 0, last_block, last_two_blocks)
        outs = []
        for hh in range(2):
            m, acc = m_ref[hh, 0:1, :], acc_ref[hh]
            l = -acc[HEAD_DIM:HEAD_DIM + 1, :]
            outs.append((acc / l)[:HEAD_DIM, :])
            neg_lse = [part.astype(F32) for part in _split3(-(m + jnp.log(l)))]
            q2t_ref[hh] = _place3(sub, L_LSE, neg_lse, q[hh].astype(F32).T).astype(BF16)
        att_ref[...] = jnp.concatenate(outs, axis=0).T
        pl.when(step == steps - 1)(finish)

    pair = pl.BlockSpec((2, tile, LANE), lambda hp, i: (hp, i, 0))
    full = pl.BlockSpec((2, seq, LANE), lambda hp, i: (hp, 0, 0))
    return pl.pallas_call(
        body, name="attention_forward", grid=(N_HEADS // 2, nb),
        in_specs=[pair, full, full, ANY],
        out_specs=[pl.BlockSpec((tile, LANE), lambda hp, i: (i, hp)),
                   pl.BlockSpec((2, LANE, tile), lambda hp, i: (hp, 0, i)), ANY],
        out_shape=[jax.ShapeDtypeStruct((seq, D_ATT), F32),
                   jax.ShapeDtypeStruct((N_HEADS, LANE, seq), BF16),
                   jax.ShapeDtypeStruct((N_DEV,) + w_out.shape, w_out.dtype)],
        scratch_shapes=[pltpu.VMEM((2, tile, tile), F32), pltpu.VMEM((2, tile, tile), F32),
                        pltpu.VMEM((2, 8, tile), F32), pltpu.VMEM((2, LANE, tile), F32),
                        pltpu.SemaphoreType.DMA((7,)), pltpu.SemaphoreType.DMA((7,)), pltpu.SemaphoreType.DMA((1,))],
        compiler_params=_params(("arbitrary", "arbitrary")),
    )(qp, kp, vp, w_out)


def _window_sum(x, halo, window, transposed):
    tile = x.shape[0]

    def split_cat(a):
        hi = a.astype(BF16)
        return jnp.concatenate([hi, (a - hi.astype(F32)).astype(BF16)], axis=1)

    def fold(r):
        return r[:, :LANE] + r[:, LANE:]

    r = lax.broadcasted_iota(jnp.int32, (tile, tile), 0)
    c = lax.broadcasted_iota(jnp.int32, (tile, tile), 1)
    rh = lax.broadcasted_iota(jnp.int32, (HALO, HALO), 0)
    ch = lax.broadcasted_iota(jnp.int32, (HALO, HALO), 1)
    if not transposed:
        band = (c <= r) & (r - c < window)
        edge = (rh + HALO - ch) < window
    else:
        band = (r <= c) & (c - r < window)
        edge = (HALO + ch - rh) < window
    out = fold(_dot(band.astype(BF16), split_cat(x)))
    reach = fold(_dot(edge.astype(BF16), split_cat(halo)))
    if not transposed:
        return jnp.concatenate([out[:HALO] + reach, out[HALO:]], axis=0)
    return jnp.concatenate([out[:tile - HALO], out[tile - HALO:] + reach], axis=0)


def _silu_parts(g):
    sig = _sigmoid(g)
    return g * sig, sig * (1.0 + g * (1.0 - sig))


def _middle(x, tgt, att, g_att, g_pool, p, vecs, pool_vecs, w_out, w_pool, tile):
    seq = x.shape[0]
    nt = seq // tile
    halo_blocks = tile // HALO

    def body(x_ref, tgt_ref, att_ref, ga_ref, gp_ref, p_ref, ph_ref, vec_ref, pvec_ref, wo_ref, wp_ref,
             dxa_ref, do2_ref, dga_ref, dgp_ref, dpooled_ref, dwo_ref, dwp_ref, dvec_ref, dpvec_ref):
        i = pl.program_id(0)

        @pl.when(i == 0)
        def _():
            dwo_ref[...] = jnp.zeros_like(dwo_ref)
            dwp_ref[...] = jnp.zeros_like(dwp_ref)
            dvec_ref[...] = jnp.zeros_like(dvec_ref)
            dpvec_ref[...] = jnp.zeros_like(dpvec_ref)

        gate, b_out, ln_g, ln_b = (vec_ref[k:k + 1, :] for k in range(4))
        b_pool, pool_scale = pvec_ref[0:1, :], pvec_ref[1:2, :]
        x = x_ref[...]
        p = p_ref[...]
        p_halo = ph_ref[...] * jnp.where(i > 0, 1.0, 0.0)
        pos = i * tile + lax.broadcasted_iota(jnp.int32, (tile, 1), 0) + 1

        pooled, mixed = [], []
        for g, window in enumerate(POOL_WINDOWS):
            cols = slice(g * GROUP_DIM, (g + 1) * GROUP_DIM)
            wsum = _window_sum(p[:, cols], p_halo[:, cols], window, False)
            count = jnp.minimum(pos, window).astype(F32)
            pooled.append(wsum / count - p[:, cols])
            mixed.append(_dot(pooled[g].astype(BF16), wp_ref[g]) + b_pool[:, cols])
        mixed = jnp.concatenate(mixed, axis=1)
        pool = mixed * pool_scale

        att = att_ref[...]
        g_att, g_pool = ga_ref[...], gp_ref[...]
        silu_a, dsilu_a = _silu_parts(g_att)
        silu_p, dsilu_p = _silu_parts(g_pool)
        y_in = jnp.concatenate([att * silu_a, pool * silu_p], axis=1)
        y = _dot(y_in.astype(BF16), wo_ref[...]) + b_out
        h = ALPHA * x + gate * y
        mu = jnp.mean(h, axis=1, keepdims=True)
        hc = h - mu
        var = jnp.mean(hc * hc, axis=1, keepdims=True)
        rstd = lax.rsqrt(var + LN_EPS)
        yhat = hc * rstd
        diff = yhat * ln_g + ln_b - tgt_ref[...]
        loss_rows = jnp.sum(diff * diff, axis=1, keepdims=True)
        d_out = diff * (1.0 / D)

        d_yhat = d_out * ln_g
        dh = rstd * (d_yhat - jnp.mean(d_yhat, axis=1, keepdims=True)
                     - yhat * jnp.mean(d_yhat * yhat, axis=1, keepdims=True))
        dxa_ref[...] = ALPHA * dh
        dy = dh * gate
        dyb = dy.astype(BF16)
        lane = _lanes((1, D))
        loss_row = jnp.where(lane == 0, (0.5 / D) * jnp.sum(loss_rows, axis=0, keepdims=True), 0.0)
        dvec_ref[0:1, :] += jnp.sum(dh * y, axis=0, keepdims=True)
        dvec_ref[1:2, :] += jnp.sum(dy, axis=0, keepdims=True)
        dvec_ref[2:3, :] += jnp.sum(d_out * yhat, axis=0, keepdims=True)
        dvec_ref[3:4, :] += jnp.sum(d_out, axis=0, keepdims=True)
        dvec_ref[4:5, :] += loss_row

        dwo_ref[...] += _dot(y_in.T.astype(BF16), dyb)
        d_yin = _dot_nt(dyb, wo_ref[...])
        d_a, d_pl = d_yin[:, :D_ATT], d_yin[:, D_ATT:]
        d_att = d_a * silu_a
        d_att_t = d_att.T
        prod_t = (d_att * att).T
        sub = lax.broadcasted_iota(jnp.int32, (HEAD_DIM, tile), 0)
        for h in range(N_HEADS):
            rows = slice(h * HEAD_DIM, (h + 1) * HEAD_DIM)
            delta = jnp.sum(prod_t[rows], axis=0, keepdims=True)
            extra = _place3(sub, 0, [part.astype(F32) for part in _split3(delta)], 0.0)
            do2_ref[h] = jnp.concatenate([d_att_t[rows], extra], axis=0).astype(BF16)
        dga_ref[...] = d_a * att * dsilu_a
        dgp_ref[...] = d_pl * pool * dsilu_p
        d_pool = d_pl * silu_p
        d_mixed = d_pool * pool_scale
        dpvec_ref[0:1, :] += jnp.sum(d_mixed, axis=0, keepdims=True)
        dpvec_ref[1:2, :] += jnp.sum(d_pool * mixed, axis=0, keepdims=True)
        d_pooled = []
        for g in range(len(POOL_WINDOWS)):
            cols = slice(g * GROUP_DIM, (g + 1) * GROUP_DIM)
            dmb = d_mixed[:, cols].astype(BF16)
            dwp_ref[g] += _dot(pooled[g].T.astype(BF16), dmb)
            d_pooled.append(_dot_nt(dmb, wp_ref[g]))
        dpooled_ref[...] = jnp.concatenate(d_pooled, axis=1)

    tok = lambda width: pl.BlockSpec((tile, width), lambda i: (i, 0))
    whole = lambda a: pl.BlockSpec(a.shape, lambda i: (0,) * a.ndim)
    halo = pl.BlockSpec((HALO, D_POOL), lambda i: (jnp.maximum(i * halo_blocks - 1, 0), 0))
    half = jax.ShapeDtypeStruct((seq, D_ATT), F32)
    outs = [jax.ShapeDtypeStruct((seq, D), F32), jax.ShapeDtypeStruct((N_HEADS, LANE, seq), BF16), half, half, half,
            jax.ShapeDtypeStruct(w_out.shape, F32), jax.ShapeDtypeStruct(w_pool.shape, F32),
            jax.ShapeDtypeStruct(vecs.shape, F32), jax.ShapeDtypeStruct(pool_vecs.shape, F32)]
    return pl.pallas_call(
        body, name="middle", grid=(nt,),
        in_specs=[tok(D), tok(D), tok(D_ATT), tok(D_ATT), tok(D_POOL), tok(D_POOL), halo,
                  whole(vecs), whole(pool_vecs), whole(w_out), whole(w_pool)],
        out_specs=[tok(D), pl.BlockSpec((N_HEADS, LANE, tile), lambda i: (0, 0, i)),
                   tok(D_ATT), tok(D_POOL), tok(D_POOL),
                   whole(w_out), whole(w_pool), whole(vecs), whole(pool_vecs)],
        out_shape=outs,
        compiler_params=_params(("arbitrary",)),
    )(x, tgt, att, g_att, g_pool, p, p, vecs, pool_vecs, w_out, w_pool)


def _attention_backward(q2t, kp, vp, do2t, gw_out, small, tile):
    seq = kp.shape[1]
    nb = seq // tile
    last = N_HEADS // 2 - 1

    def body(qt_ref, k_ref, v_ref, dot_ref, gwo_hbm, small_hbm,
             dq_ref, dk_ref, dv_ref, dcum_ref, g_out_ref, total_ref,
             dq_acc, dk_acc, dv_acc, gwo_ref, r1, s2, r2, sm_sib, sm_chip, sm_recv, small_ref, send_sems, recv_sems):
        hp = pl.program_id(0)
        start, middle, finish = _reduce_stages(
            [gwo_ref], [g_out_ref], [r1], [s2], [r2], (small_ref, total_ref, sm_sib, sm_chip, sm_recv),
            send_sems, recv_sems)

        @pl.when(hp == 0)
        def _():
            pltpu.sync_copy(gwo_hbm, gwo_ref)
            pltpu.sync_copy(small_hbm, small_ref)
            start()

        pl.when(hp == 1)(middle)

        row = lax.broadcasted_iota(jnp.int32, (tile, tile), 0)
        col = lax.broadcasted_iota(jnp.int32, (tile, tile), 1)
        dq_acc[...] = jnp.zeros_like(dq_acc)

        def kv_block(kb, _):
            krows = pl.ds(pl.multiple_of(kb * tile, tile), tile)
            k = [k_ref[hh, krows, :] for hh in range(2)]
            v = [v_ref[hh, krows, :] for hh in range(2)]
            k_t = [k[hh].T for hh in range(2)]

            def q_block(qb, masked):
                qcols = pl.ds(pl.multiple_of(qb * tile, tile), tile)
                for hh in range(2):
                    q_t = qt_ref[hh, :, qcols]
                    do_t = dot_ref[hh, :, qcols]
                    s_t = _dot(k[hh], q_t)
                    if masked:
                        s_t = jnp.where(row <= col, s_t, -1e30)
                    p_t = jnp.exp(s_t)
                    ds_t = (p_t * _dot(v[hh], do_t)).astype(BF16)
                    dv_new = _dot_nt(do_t, p_t.astype(BF16))
                    dk_new = _dot_nt(q_t, ds_t)
                    if masked:
                        dv_acc[hh], dk_acc[hh] = dv_new, dk_new
                    else:
                        dv_acc[hh] += dv_new
                        dk_acc[hh] += dk_new
                    dq_acc[hh, :, qcols] += _dot(k_t[hh], ds_t)

            q_block(kb, True)

            def later_block(qb, _):
                q_block(qb, False)
                return 0

            lax.fori_loop(kb + 1, nb, later_block, 0)
            for hh in range(2):
                dk = dk_acc[hh]
                dk_ref[hh, :, krows] = dk.astype(BF16)
                dv_ref[hh, :, krows] = dv_acc[hh].astype(BF16)
                dcum_ref[hh, :, krows] = -dk[L_CK:L_CK + 1, :]
            return 0

        lax.fori_loop(0, nb, kv_block, 0)
        for hh in range(2):
            dq = dq_acc[hh]
            dcum_ref[hh] += dq[L_CQ:L_CQ + 1, :]
            dq_ref[hh] = (dq * Q_SCALE).astype(BF16)
        pl.when(hp == last)(finish)

    pair = pl.BlockSpec((2, seq, LANE), lambda hp: (hp, 0, 0))
    pair_t = pl.BlockSpec((2, LANE, seq), lambda hp: (hp, 0, 0))
    whole = lambda shape: pl.BlockSpec(shape, lambda hp: (0,) * len(shape))
    grad = jax.ShapeDtypeStruct((N_HEADS, LANE, seq), BF16)
    return pl.pallas_call(
        body, name="attention_backward", grid=(N_HEADS // 2,),
        in_specs=[pair_t, pair, pair, pair_t, ANY, ANY],
        out_specs=[pair_t, pair_t, pair_t, pl.BlockSpec((2, 1, seq), lambda hp: (hp, 0, 0)),
                   whole(gw_out.shape[1:]), whole(small.shape)],
        out_shape=[grad, grad, grad, jax.ShapeDtypeStruct((N_HEADS, 1, seq), F32),
                   jax.ShapeDtypeStruct(gw_out.shape[1:], F32), jax.ShapeDtypeStruct(small.shape, F32)],
        scratch_shapes=[pltpu.VMEM((2, LANE, seq), F32), pltpu.VMEM((2, LANE, tile), F32),
                        pltpu.VMEM((2, LANE, tile), F32), pltpu.VMEM(gw_out.shape, BF16)]
        + _reduce_scratch(gw_out, small)
        + [pltpu.VMEM(small.shape, F32),
           pltpu.SemaphoreType.DMA((N_REDUCE_SEMS + N_SMALL_SEMS,)), pltpu.SemaphoreType.DMA((N_REDUCE_SEMS + N_SMALL_SEMS,))],
        compiler_params=_params(("arbitrary",)),
    )(q2t, kp, vp, do2t, gw_out, small)


def _inproj_backward(dqp, dkp, dvp, d_cum, f, d_pooled, d_ga, d_gp, x, dxa, u, mod, w_main, w_f, tile):
    seq = x.shape[0]
    nt = seq // tile
    halo_blocks = tile // HALO

    def body(dq_ref, dk_ref, dv_ref, dcum_ref, f_ref, dpo_ref, dph_ref, dga_ref, dgp_ref, x_ref, dxa_ref, u_ref,
             mod_ref, w_ref, wf_ref,
             dx_ref, dproj_ref, dwf_ref, db_ref, dbf_ref, dmod_ref, carry_ref):
        step = pl.program_id(0)
        i = nt - 1 - step

        @pl.when(step == 0)
        def _():
            carry_ref[...] = jnp.zeros_like(carry_ref)
            dwf_ref[...] = jnp.zeros_like(dwf_ref)
            db_ref[...] = jnp.zeros_like(db_ref)
            dbf_ref[...] = jnp.zeros_like(dbf_ref)
            dmod_ref[...] = jnp.zeros_like(dmod_ref)

        ones = jnp.ones((8, tile), BF16)

        def emit(chunk, val):
            cols = pl.ds(chunk * COL_CHUNK, COL_CHUNK)
            db_ref[0:1, cols] += jnp.sum(val, axis=0, keepdims=True)
            vb = val.astype(BF16)
            dproj_ref[:, pl.ds((chunk - 3) * COL_CHUNK, COL_CHUNK)] = vb
            return _dot(vb, w_ref[cols, :])

        d_u = jnp.zeros((tile, D), F32)
        for chunk, ref in enumerate((dq_ref, dk_ref, dv_ref)):
            cols = pl.ds(chunk * COL_CHUNK, COL_CHUNK)
            val_t = ref[:, 0:HEAD_DIM, :].reshape(COL_CHUNK, tile)
            db_ref[:, cols] += _dot_nt(ones, val_t)
            d_u += _dot_tn(val_t, w_ref[cols, :])

        d_pooled = dpo_ref[...]
        d_halo = dph_ref[...] * jnp.where(i < nt - 1, 1.0, 0.0)
        pos = i * tile + lax.broadcasted_iota(jnp.int32, (tile, 1), 0) + 1
        d_p = []
        for g, window in enumerate(POOL_WINDOWS):
            cols = slice(g * GROUP_DIM, (g + 1) * GROUP_DIM)
            scaled = d_pooled[:, cols] / jnp.minimum(pos, window).astype(F32)
            d_p.append(_window_sum(scaled, d_halo[:, cols] * (1.0 / window), window, True) - d_pooled[:, cols])
        d_u += emit(3, jnp.concatenate(d_p, axis=1))
        d_u += emit(4, dga_ref[...])
        d_u += emit(5, dgp_ref[...])

        row = lax.broadcasted_iota(jnp.int32, (tile, tile), 0)
        col = lax.broadcasted_iota(jnp.int32, (tile, tile), 1)
        later = (row >= col).astype(BF16)
        d_logf = sum(_dot(part, later) for part in _split3(dcum_ref[:, 0, :])) + carry_ref[:, 0:1]
        carry_ref[:, 0:1] = d_logf[:, 0:1]
        d_f = d_logf * _sigmoid(-f_ref[...].T[0:N_HEADS, :])
        d_f = jnp.concatenate([d_f, jnp.zeros((LANE - N_HEADS, tile), F32)], axis=0)
        dbf_ref[...] += sum(_dot_nt(ones, part) for part in _split3(d_f))
        d_fb = d_f.astype(BF16)
        d_u += _dot_tn(d_fb, wf_ref[...])
        dwf_ref[...] += _dot(d_fb, u_ref[...])

        x = x_ref[...]
        dx_ref[...] = dxa_ref[...] + d_u * mod_ref[0:1, :]
        dmod_ref[0:1, :] += jnp.sum(d_u * x, axis=0, keepdims=True)
        dmod_ref[1:2, :] += jnp.sum(d_u, axis=0, keepdims=True)

    rev = lambda step: nt - 1 - step
    tok = lambda width: pl.BlockSpec((tile, width), lambda s: (rev(s), 0))
    head_block = pl.BlockSpec((N_HEADS, LANE, tile), lambda s: (0, 0, rev(s)))
    whole = lambda a: pl.BlockSpec(a.shape, lambda s: (0,) * a.ndim)
    halo = pl.BlockSpec((HALO, D_POOL), lambda s: (jnp.minimum((rev(s) + 1) * halo_blocks, seq // HALO - 1), 0))
    small = lambda width: jax.ShapeDtypeStruct((8, width), F32)
    n_rest = N_MAIN - OFF_P
    return pl.pallas_call(
        body, name="inproj_backward", grid=(nt,),
        in_specs=[head_block, head_block, head_block, pl.BlockSpec((N_HEADS, 1, tile), lambda s: (0, 0, rev(s))),
                  tok(LANE), tok(D_POOL), halo, tok(D_ATT), tok(D_POOL),
                  tok(D), tok(D), tok(D),
                  whole(mod), whole(w_main), whole(w_f)],
        out_specs=[tok(D), tok(n_rest), pl.BlockSpec((LANE, D), lambda s: (0, 0)),
                   pl.BlockSpec((8, N_MAIN), lambda s: (0, 0)), pl.BlockSpec((8, LANE), lambda s: (0, 0)),
                   pl.BlockSpec((8, D), lambda s: (0, 0))],
        out_shape=[jax.ShapeDtypeStruct((seq, D), F32), jax.ShapeDtypeStruct((seq, n_rest), BF16),
                   jax.ShapeDtypeStruct((LANE, D), F32), small(N_MAIN), small(LANE), small(D)],
        scratch_shapes=[pltpu.VMEM((8, LANE), F32)],
        compiler_params=_params(("arbitrary",)),
    )(dqp, dkp, dvp, d_cum, f, d_pooled, d_pooled, d_ga, d_gp, x, dxa, u, mod, w_main, w_f)


def _weight_grad(dproj, u, k_tile):
    seq, n_cols = dproj.shape
    nk = seq // k_tile

    def body(dp_ref, u_ref, out_ref):
        @pl.when(pl.program_id(1) == 0)
        def _():
            out_ref[...] = jnp.zeros_like(out_ref)

        out_ref[...] += _dot_tn(dp_ref[...], u_ref[...])

    return pl.pallas_call(
        body, name="weight_grad", grid=(n_cols // COL_CHUNK, nk),
        in_specs=[pl.BlockSpec((k_tile, COL_CHUNK), lambda n, k: (k, n)),
                  pl.BlockSpec((k_tile, D), lambda n, k: (k, 0))],
        out_specs=pl.BlockSpec((COL_CHUNK, D), lambda n, k: (n, 0)),
        out_shape=jax.ShapeDtypeStruct((n_cols, D), F32),
        compiler_params=_params(("arbitrary", "arbitrary")),
    )(dproj, u)


def _weight_grad_heads(grad_t, u, k_tile, name):
    seq = u.shape[0]
    nk = seq // k_tile

    def body(g_ref, u_ref, out_ref):
        @pl.when(pl.program_id(0) == 0)
        def _():
            out_ref[...] = jnp.zeros_like(out_ref)

        out_ref[...] += _dot(g_ref[...].reshape(N_HEADS * HEAD_DIM, k_tile), u_ref[...])

    return pl.pallas_call(
        body, name=name, grid=(nk,),
        in_specs=[pl.BlockSpec((N_HEADS, HEAD_DIM, k_tile), lambda k: (0, 0, k)),
                  pl.BlockSpec((k_tile, D), lambda k: (k, 0))],
        out_specs=pl.BlockSpec((N_HEADS * HEAD_DIM, D), lambda k: (0, 0)),
        out_shape=jax.ShapeDtypeStruct((N_HEADS * HEAD_DIM, D), F32),
        compiler_params=_params(("arbitrary",)),
    )(grad_t, u)


def _adamw(w, g, m, v):
    m = ADAM_B1 * m + (1.0 - ADAM_B1) * g
    v = ADAM_B2 * v + (1.0 - ADAM_B2) * (g * g)
    m_hat = m / (1.0 - ADAM_B1 ** ADAM_STEP)
    v_hat = v / (1.0 - ADAM_B2 ** ADAM_STEP)
    delta = -ADAM_LR * (m_hat / (jnp.sqrt(v_hat) + ADAM_EPS) + ADAM_WD * w)
    return delta, m, v


def _adamw_call(g, w, m, v, lead_tile, name):
    nr = w.shape[0] // lead_tile

    def body(gi_ref, w_ref, m_ref, v_ref, g_ref, d_ref, nm_ref, nv_ref):
        g = gi_ref[...]
        g_ref[...] = g
        d_ref[...], nm_ref[...], nv_ref[...] = _adamw(w_ref[...], g, m_ref[...], v_ref[...])

    blk = pl.BlockSpec((lead_tile,) + w.shape[1:], lambda r: (r,) + (0,) * (w.ndim - 1))
    shape = jax.ShapeDtypeStruct(w.shape, F32)
    return pl.pallas_call(
        body, name=name, grid=(nr,),
        in_specs=[blk, blk, blk, blk], out_specs=[blk, blk, blk, blk],
        out_shape=[shape, shape, shape, shape],
        compiler_params=_params(("arbitrary",)),
    )(g, w, m, v)


def _sum_adamw(parts, w, m, v, row_tile, name):
    rows, cols = w.shape
    nr = rows // row_tile

    def body(parts_ref, w_ref, m_ref, v_ref, g_ref, d_ref, nm_ref, nv_ref):
        g = parts_ref[0]
        for k in range(1, N_DEV):
            g = g + parts_ref[k]
        g_ref[...] = g
        d_ref[...], nm_ref[...], nv_ref[...] = _adamw(w_ref[...], g, m_ref[...], v_ref[...])

    blk = pl.BlockSpec((row_tile, cols), lambda r: (r, 0))
    shape = jax.ShapeDtypeStruct(w.shape, F32)
    return pl.pallas_call(
        body, name=name, grid=(nr,),
        in_specs=[pl.BlockSpec((N_DEV, row_tile, cols), lambda r: (0, r, 0)), blk, blk, blk],
        out_specs=[blk, blk, blk, blk],
        out_shape=[shape, shape, shape, shape],
        compiler_params=_params(("arbitrary",)),
    )(parts, w, m, v)


def _ada_adamw(sc_t, d_ada, w, m, v):
    def body(sc_ref, d_ref, w_ref, m_ref, v_ref, g_ref, dl_ref, nm_ref, nv_ref):
        g = sc_ref[:, 0:1] * d_ref[0:1, :]
        for b in range(1, N_DEV):
            g = g + sc_ref[:, b:b + 1] * d_ref[b:b + 1, :]
        g_ref[...] = g
        dl_ref[...], nm_ref[...], nv_ref[...] = _adamw(w_ref[...], g, m_ref[...], v_ref[...])

    shape = jax.ShapeDtypeStruct(w.shape, F32)
    return pl.pallas_call(
        body, name="ada_adamw", out_shape=[shape, shape, shape, shape], compiler_params=_params(),
    )(sc_t, d_ada, w, m, v)


F_LO, F_HI = 3 * D_ATT, 3 * D_ATT + N_HEADS


def _split_forget(a, axis):
    idx = lambda lo, hi: tuple(slice(lo, hi) if d == axis else slice(None) for d in range(a.ndim))
    pad = [(0, LANE - N_HEADS) if d == axis else (0, 0) for d in range(a.ndim)]
    return jnp.concatenate([a[idx(0, F_LO)], a[idx(F_HI, D_IN)]], axis=axis), jnp.pad(a[idx(F_LO, F_HI)], pad)


def _join_forget(main, f, axis):
    idx = lambda lo, hi: tuple(slice(lo, hi) if d == axis else slice(None) for d in range(main.ndim))
    return jnp.concatenate([main[idx(0, F_LO)], f[idx(0, N_HEADS)], main[idx(F_LO, N_MAIN)]], axis=axis)


_EARLY = ((("w_pool", 65536), ("b_pool", 512), ("pool_scale", 512), ("b_out", 1024), ("ln_g", 1024), ("ln_b", 1024),
           ("loss", 128)), 552)
_LATE = ((("b_in", 3200),), 32)


def _pack_small(parts, layout):
    spec, rows = layout
    flat = []
    for name, size in spec:
        a = parts[name].reshape(-1)
        flat.append(jnp.pad(a, (0, size - a.shape[0])))
    flat = jnp.concatenate(flat)
    flat = jnp.pad(flat, (0, rows * LANE - flat.shape[0]))
    return flat.reshape(rows, LANE)


def _unpack_small(packed, shapes, layout):
    flat = packed.reshape(-1)
    out, off = {}, 0
    for name, size in layout[0]:
        n = 1
        for s in shapes[name]:
            n *= s
        out[name] = flat[off:off + n].reshape(shapes[name])
        off += size
    return out


def kernel(x, c, w_ada, b_ada, w_in, b_in, w_pool_mix, b_pool_mix, pool_scale, w_out, b_out, ln_g, ln_b, loss_target, m_w_ada, m_b_ada, m_w_in, m_b_in, m_w_pool_mix, m_b_pool_mix, m_pool_scale, m_w_out, m_b_out, m_ln_g, m_ln_b, v_w_ada, v_b_ada, v_w_in, v_b_in, v_w_pool_mix, v_b_pool_mix, v_pool_scale, v_w_out, v_b_out, v_ln_g, v_ln_b):
    seq = x.shape[1]
    tile = min(256, seq)
    attn_tile = min(512, max(128, seq // 4))
    me = _dev_index(*_mesh_pos())
    x2, tgt = x[0], loss_target[0]

    rows_in = D_IN // N_DEV
    c_all, w_in_g = _all_gather([jnp.pad(c, ((0, 7), (0, 0))), w_in[0].T.astype(BF16)], "gather_weights")
    sc_all, ada_part = _ada_forward(c_all[:, 0, :], w_ada[0])
    (ada_mine,) = _exchange([ada_part.reshape(N_DEV, 1, -1)], [], "exchange_ada")
    ada = ada_mine.reshape(1, D_ADA) + b_ada
    shift, scale, gate = ada[:, 0:D], ada[:, D:2 * D], ada[:, 2 * D:]
    mod = jnp.concatenate([1.0 + scale, shift, jnp.zeros((6, D), F32)], axis=0)

    w_main, w_f = _split_forget(w_in_g.reshape(D_IN, D), 0)
    b_main, b_f = _split_forget(b_in, 1)

    qp, kp, vp, f, p, g_att, g_pool, u = _inproj_forward(x2, mod, w_main, w_f, b_main, b_f, tile)
    att, q2t, w_out_g = _attention_forward(qp, kp, vp, w_out[0].astype(BF16), attn_tile)

    vecs = jnp.concatenate([gate, b_out, ln_g, ln_b, jnp.zeros((4, D), F32)], axis=0)
    pool_vecs = jnp.concatenate([b_pool_mix.reshape(1, D_POOL), pool_scale, jnp.zeros((6, D_POOL), F32)], axis=0)
    dxa, do2, d_ga, d_gp, d_pooled, dw_out, dw_pool, dvec, dpvec = _middle(
        x2, tgt, att, g_att, g_pool, p, vecs, pool_vecs, w_out_g.reshape(D, D), w_pool_mix[0].astype(BF16), tile)

    gw_out = dw_out.reshape(N_DEV, D // N_DEV, D).astype(BF16)
    early = _pack_small({"w_pool": dw_pool, "b_pool": dpvec[0:1], "pool_scale": dpvec[1:2], "b_out": dvec[1:2],
                         "ln_g": dvec[2:3], "ln_b": dvec[3:4], "loss": dvec[4:5, 0:LANE]}, _EARLY)
    dqp, dkp, dvp, d_cum, g_out, early_sum = _attention_backward(q2t, kp, vp, do2, gw_out, early, attn_tile)
    dx, dproj, dw_f, db_main, db_f, dmod = _inproj_backward(
        dqp, dkp, dvp, d_cum, f, d_pooled, d_ga, d_gp, x2, dxa, u, mod, w_main, w_f, tile)
    k_tile = min(1024, seq)
    dw_q, dw_k, dw_v = (_weight_grad_heads(g, u, k_tile, "weight_grad_" + n)
                        for g, n in ((dqp, "q"), (dkp, "k"), (dvp, "v")))
    dw_rest = _weight_grad(dproj, u, k_tile)

    dw_main = jnp.concatenate([dw_q, dw_k, dw_v, dw_rest], axis=0)
    gw_in = _join_forget(dw_main, dw_f, 0).reshape(N_DEV, rows_in, D).astype(BF16)
    d_ada = jnp.concatenate([dmod[1:2], dmod[0:1], dvec[0:1]], axis=1)
    late = _pack_small({"b_in": _join_forget(db_main[0:1], db_f[0:1], 1)}, _LATE)
    g_in_rows, late_sum, d_ada_all = _reduce_grads(gw_in, late, d_ada.reshape(D_ADA // LANE, LANE))

    def rows3(a):
        return a[0].T.reshape(rows_in, D // LANE, LANE)

    outs_in = _adamw_call(g_in_rows.reshape(rows_in, D // LANE, LANE), rows3(w_in), rows3(m_w_in), rows3(v_w_in),
                          rows_in // 5, "adamw_w_in")
    g_w_in, d_w_in, nm_w_in, nv_w_in = (a.reshape(rows_in, D).T for a in outs_in)
    g_w_out, d_w_out, nm_w_out, nv_w_out = _adamw_call(g_out, w_out[0], m_w_out[0], v_w_out[0], D // N_DEV, "adamw_w_out")

    def small_adamw(grad_sum, weights, first, second, layout, name):
        packed = _adamw_call(grad_sum, _pack_small(weights, layout), _pack_small(first, layout),
                             _pack_small(second, layout), layout[1], name)
        shapes = {k: a.shape for k, a in weights.items()}
        return [_unpack_small(a, shapes, layout) for a in packed]

    zero = jnp.zeros((1,), F32)
    smalls = small_adamw(
        early_sum,
        {"w_pool": w_pool_mix, "b_pool": b_pool_mix, "pool_scale": pool_scale, "b_out": b_out, "ln_g": ln_g,
         "ln_b": ln_b, "loss": zero},
        {"w_pool": m_w_pool_mix, "b_pool": m_b_pool_mix, "pool_scale": m_pool_scale, "b_out": m_b_out,
         "ln_g": m_ln_g, "ln_b": m_ln_b, "loss": zero},
        {"w_pool": v_w_pool_mix, "b_pool": v_b_pool_mix, "pool_scale": v_pool_scale, "b_out": v_b_out,
         "ln_g": v_ln_g, "ln_b": v_ln_b, "loss": zero}, _EARLY, "adamw_small")
    b_ins = small_adamw(late_sum, {"b_in": b_in}, {"b_in": m_b_in}, {"b_in": v_b_in}, _LATE, "adamw_b_in")
    g_s, d_s, nm_s, nv_s = ({**a, **b} for a, b in zip(smalls, b_ins))
    loss = g_s["loss"][0]

    ada_rows = D_ADA // LANE
    b_ada_outs = _sum_adamw(d_ada_all, b_ada.reshape(ada_rows, LANE), m_b_ada.reshape(ada_rows, LANE),
                            v_b_ada.reshape(ada_rows, LANE), ada_rows, "adamw_b_ada")
    g_b_ada, d_b_ada, nm_b_ada, nv_b_ada = (a.reshape(1, D_ADA) for a in b_ada_outs)
    d_ada_local = lax.dynamic_slice_in_dim(d_ada_all.reshape(N_DEV, D_ADA), me * (D_ADA // N_DEV), D_ADA // N_DEV, axis=1)
    g_w_ada, d_w_ada, nm_w_ada, nv_w_ada = _ada_adamw(sc_all.T, d_ada_local, w_ada[0], m_w_ada[0], v_w_ada[0])

    def ordered(w_ada_, b_ada_, w_in_, w_out_, s):
        return (w_ada_[None], b_ada_, w_in_[None], s["b_in"], s["w_pool"], s["b_pool"], s["pool_scale"],
                w_out_[None], s["b_out"], s["ln_g"], s["ln_b"])

    return (loss, dx[None],
            *ordered(g_w_ada, g_b_ada, g_w_in, g_w_out, g_s),
            *ordered(d_w_ada, d_b_ada, d_w_in, d_w_out, d_s),
            *ordered(nm_w_ada, nm_b_ada, nm_w_in, nm_w_out, nm_s),
            *ordered(nv_w_ada, nv_b_ada, nv_w_in, nv_w_out, nv_s))
```

```python
import jax
import jax.numpy as jnp
from jax import lax
from jax.experimental import pallas as pl
from jax.experimental.pallas import tpu as pltpu

F32 = jnp.float32
BF16 = jnp.bfloat16

N_DEV = 8
D = 1024
N_HEADS = 8
HEAD_DIM = 64
D_ATT = 512
D_POOL = 512
POOL_WINDOWS = (2, 4, 8, 16)
GROUP_DIM = 128
HALO = 16
LANE = 128
D_IN = 3080
D_ADA = 3072
N_MAIN = 3072
OFF_P = 1536
COL_CHUNK = 512
Q_SCALE = 0.125
LN_EPS = 1e-5
ALPHA = 2.0 ** 0.25
L_CQ, L_CK, L_LSE = 64, 67, 70

ADAM_LR, ADAM_B1, ADAM_B2, ADAM_EPS, ADAM_WD, ADAM_STEP = 0.001, 0.9, 0.999, 1e-08, 0.01, 10
VMEM_LIMIT = 56 * 1024 * 1024

MESH = pl.DeviceIdType.MESH
ANY = pl.BlockSpec(memory_space=pl.ANY)


def _params(sem=None, vmem=VMEM_LIMIT):
    return pltpu.CompilerParams(dimension_semantics=sem, vmem_limit_bytes=vmem)


def _split3(a):
    hi = a.astype(BF16)
    r = a - hi.astype(F32)
    mid = r.astype(BF16)
    lo = (r - mid.astype(F32)).astype(BF16)
    return hi, mid, lo


def _dot(a, b):
    return jnp.dot(a, b, preferred_element_type=F32)


def _dot_nt(a, b):
    return lax.dot_general(a, b, (((1,), (1,)), ((), ())), preferred_element_type=F32)


def _dot_tn(a, b):
    return lax.dot_general(a, b, (((0,), (0,)), ((), ())), preferred_element_type=F32)


def _dot3(m01, a):
    hi, mid, lo = _split3(a)
    return _dot(m01, hi) + _dot(m01, mid) + _dot(m01, lo)


def _sigmoid(z):
    return 1.0 / (1.0 + jnp.exp(-z))


def _lanes(shape):
    return lax.broadcasted_iota(jnp.int32, shape, len(shape) - 1)


def _place3(lane, base, parts, other):
    out = other
    for j in range(3):
        out = jnp.where(lane == base + j, parts[j], out)
    return out


def _mesh_pos():
    return lax.axis_index("x"), lax.axis_index("y"), lax.axis_index("c")


def _dev_index(px, py, pc):
    return 4 * px + 2 * py + pc


def _all_gather(blocks, name):
    n = len(blocks)

    def body(*refs):
        ins, outs = refs[:n], refs[n:2 * n]
        send_sems, recv_sems, local_sems = refs[2 * n:]
        x, y, c = _mesh_pos()
        me, sibling = (x, y, c), (x, y, 1 - c)
        chips = [(1 - x, y), (x, 1 - y), (1 - x, 1 - y)]

        def copy(a, k, block, to, src=None):
            slot = outs[a].at[_dev_index(*block)]
            return pltpu.make_async_remote_copy(
                src_ref=slot if src is None else src, dst_ref=slot,
                send_sem=send_sems.at[a, k], recv_sem=recv_sems.at[a, k],
                device_id=to, device_id_type=MESH)

        mine = [pltpu.make_async_copy(ins[a], outs[a].at[_dev_index(*me)], local_sems.at[a]) for a in range(n)]
        for cp in mine:
            cp.start()
        first = []
        for a in range(n):
            first.append(copy(a, 0, me, sibling, src=ins[a]))
            first += [copy(a, 1 + j, me, (*chip, c), src=ins[a]) for j, chip in enumerate(chips)]
        for cp in first:
            cp.start()
        passed = []
        for j, chip in enumerate(chips):
            for a in range(n):
                copy(a, 1 + j, (*chip, c), me).wait_recv()
                fwd = copy(a, 4 + j, (*chip, c), sibling)
                fwd.start()
                passed.append(fwd)
        for a in range(n):
            copy(a, 0, sibling, me).wait_recv()
            for j, chip in enumerate(chips):
                copy(a, 4 + j, (*chip, 1 - c), me).wait_recv()
        for cp in first + passed:
            cp.wait_send()
        for cp in mine:
            cp.wait()

    return pl.pallas_call(
        body, name=name,
        out_shape=[jax.ShapeDtypeStruct((N_DEV,) + b.shape, b.dtype) for b in blocks],
        in_specs=[ANY] * n, out_specs=[ANY] * n,
        scratch_shapes=[pltpu.SemaphoreType.DMA((n, 7)), pltpu.SemaphoreType.DMA((n, 7)),
                        pltpu.SemaphoreType.DMA((n,))],
    )(*blocks)


def _exchange(scatter, gather, name):
    ns, n = len(scatter), len(scatter) + len(gather)
    arrays = list(scatter) + list(gather)

    def body(*refs):
        ins, outs = refs[:n], refs[n:2 * n]
        send_sems, recv_sems, local_sems = refs[2 * n:]
        x, y, c = _mesh_pos()
        me = _dev_index(x, y, c)
        peers = []
        for p in range(1, N_DEV):
            px, py, pc = (p >> 2) & 1, (p >> 1) & 1, p & 1
            peers.append((1 - x if px else x, 1 - y if py else y, 1 - c if pc else c))

        def src_for(a, slot):
            return ins[a].at[slot] if a < ns else ins[a]

        def copy(a, k, to):
            return pltpu.make_async_remote_copy(
                src_ref=src_for(a, _dev_index(*to)), dst_ref=outs[a].at[me],
                send_sem=send_sems.at[a, k], recv_sem=recv_sems.at[a, k],
                device_id=to, device_id_type=MESH)

        def arrival(a, k, frm):
            slot = _dev_index(*frm)
            return pltpu.make_async_remote_copy(
                src_ref=src_for(a, slot), dst_ref=outs[a].at[slot],
                send_sem=send_sems.at[a, k], recv_sem=recv_sems.at[a, k],
                device_id=frm, device_id_type=MESH)

        mine = [pltpu.make_async_copy(src_for(a, me), outs[a].at[me], local_sems.at[a]) for a in range(n)]
        for cp in mine:
            cp.start()
        sends = [copy(a, k, to) for k, to in enumerate(peers) for a in range(n)]
        for cp in sends:
            cp.start()
        for k, frm in enumerate(peers):
            for a in range(n):
                arrival(a, k, frm).wait_recv()
        for cp in sends:
            cp.wait_send()
        for cp in mine:
            cp.wait()

    out_shape = [jax.ShapeDtypeStruct(a.shape, a.dtype) for a in scatter]
    out_shape += [jax.ShapeDtypeStruct((N_DEV,) + a.shape, a.dtype) for a in gather]
    return pl.pallas_call(
        body, name=name, out_shape=out_shape,
        in_specs=[ANY] * n, out_specs=[ANY] * n,
        scratch_shapes=[pltpu.SemaphoreType.DMA((n, 7)), pltpu.SemaphoreType.DMA((n, 7)),
                        pltpu.SemaphoreType.DMA((n,))],
    )(*arrays)


def _gather_stages(src_ref, out_ref, send_sems, recv_sems, local_sem):
    x, y, c = _mesh_pos()
    me, sibling = (x, y, c), (x, y, 1 - c)
    chips = [(1 - x, y), (x, 1 - y), (1 - x, 1 - y)]

    def copy(k, block, to, src=None):
        slot = out_ref.at[_dev_index(*block)]
        return pltpu.make_async_remote_copy(
            src_ref=slot if src is None else src, dst_ref=slot, send_sem=send_sems.at[k], recv_sem=recv_sems.at[k],
            device_id=to, device_id_type=MESH)

    mine = pltpu.make_async_copy(src_ref, out_ref.at[_dev_index(*me)], local_sem)
    first = [copy(0, me, sibling, src=src_ref)] + [copy(1 + j, me, (*chip, c), src=src_ref) for j, chip in enumerate(chips)]
    passed = [copy(4 + j, (*chip, c), sibling) for j, chip in enumerate(chips)]

    def start():
        mine.start()
        for cp in first:
            cp.start()

    def forward():
        for j, chip in enumerate(chips):
            copy(1 + j, (*chip, c), me).wait_recv()
            passed[j].start()

    def finish():
        copy(0, sibling, me).wait_recv()
        for j, chip in enumerate(chips):
            copy(4 + j, (*chip, 1 - c), me).wait_recv()
        for cp in first + passed:
            cp.wait_send()
        mine.wait()

    return start, forward, finish


N_REDUCE_SEMS = 7
N_SMALL_SEMS = 4 + 7


def _reduce_stages(ins, gs, r1, s2, r2, small, send_sems, recv_sems, rows=None):
    n = len(ins)
    x, y, c = _mesh_pos()
    me = _dev_index(x, y, c)
    sibling = (x, y, 1 - c)
    chips = [(x, y), (1 - x, y), (x, 1 - y), (1 - x, 1 - y)]
    peers = []
    for p in range(1, N_DEV):
        px, py, pc = (p >> 2) & 1, (p >> 1) & 1, p & 1
        peers.append((1 - x if px else x, 1 - y if py else y, 1 - c if pc else c))
    base_small = N_REDUCE_SEMS * n

    def remote(src, dst, k, to):
        return pltpu.make_async_remote_copy(src_ref=src, dst_ref=dst, send_sem=send_sems.at[k],
                                            recv_sem=recv_sems.at[k], device_id=to, device_id_type=MESH)

    def level1(a, q):
        return remote(ins[a].at[_dev_index(*chips[q], 1 - c)], r1[a].at[q], N_REDUCE_SEMS * a + q, sibling)

    def level2(a, j):
        return remote(s2[a].at[j], r2[a].at[j], N_REDUCE_SEMS * a + 4 + j, (*chips[j + 1], c))

    if small is not None:
        small_ref, total_ref, sm_sib, sm_chip, sm_recv = small
        to_sibling = remote(small_ref, sm_sib, base_small, sibling)
        to_chips = [remote(sm_chip, sm_recv.at[j], base_small + 1 + j, (*chips[j + 1], c)) for j in range(3)]
    if rows is not None:
        rows_ref, land_ref, all_ref = rows
        row_sends = [remote(rows_ref, land_ref.at[me], base_small + 4 + k, to) for k, to in enumerate(peers)]

    def start():
        for a in range(n):
            for q in range(4):
                level1(a, q).start()
        if small is not None:
            to_sibling.start()
        if rows is not None:
            for cp in row_sends:
                cp.start()
            land_ref[me] = rows_ref[...]

    def middle():
        for a in range(n):
            for q in (1, 2, 3, 0):
                level1(a, q).wait_recv()
                pair = ins[a][_dev_index(*chips[q], c)].astype(F32) + r1[a][q].astype(F32)
                if q == 0:
                    gs[a][...] = pair
                else:
                    s2[a][q - 1] = pair.astype(BF16)
                    level2(a, q - 1).start()
        if small is not None:
            to_sibling.wait_recv()
            sm_chip[...] = small_ref[...] + sm_sib[...]
            for cp in to_chips:
                cp.start()

    def finish():
        for a in range(n):
            own = gs[a][...]
            for j in range(3):
                level2(a, j).wait_recv()
                own = own + r2[a][j].astype(F32)
            gs[a][...] = own
            for q in range(4):
                level1(a, q).wait_send()
            for j in range(3):
                level2(a, j).wait_send()
        if small is not None:
            for cp in to_chips:
                cp.wait_recv()
            total = None
            for ax in range(2):
                for ay in range(2):
                    dx, dy = x != ax, y != ay
                    term = jnp.where(dx, jnp.where(dy, sm_recv[2], sm_recv[0]), jnp.where(dy, sm_recv[1], sm_chip[...]))
                    total = term if total is None else total + term
            total_ref[...] = total
            for cp in [to_sibling] + to_chips:
                cp.wait_send()
        if rows is not None:
            for k, frm in enumerate(peers):
                remote(rows_ref, land_ref.at[_dev_index(*frm)], base_small + 4 + k, frm).wait_recv()
            all_ref[...] = land_ref[...]
            for cp in row_sends:
                cp.wait_send()

    return start, middle, finish


def _reduce_scratch(shard, small, rows=None):
    out = [pltpu.VMEM((lead,) + shard.shape[1:], BF16) for lead in (4, 3, 3)]
    out += [pltpu.VMEM(small.shape, F32), pltpu.VMEM(small.shape, F32), pltpu.VMEM((3,) + small.shape, F32)]
    if rows is not None:
        out.append(pltpu.VMEM((N_DEV,) + rows.shape, F32))
    return out


def _reduce_grads(gw_in, small, rows):
    def body(in_ref, small_ref, rows_ref, g_ref, total_ref, rows_all_ref,
             r1, s2, r2, sm_sib, sm_chip, sm_recv, rows_land, send_sems, recv_sems):
        start, middle, finish = _reduce_stages(
            [in_ref], [g_ref], [r1], [s2], [r2], (small_ref, total_ref, sm_sib, sm_chip, sm_recv),
            send_sems, recv_sems, rows=(rows_ref, rows_land, rows_all_ref))
        start()
        middle()
        finish()

    return pl.pallas_call(
        body, name="reduce_grads",
        out_shape=[jax.ShapeDtypeStruct(gw_in.shape[1:], F32), jax.ShapeDtypeStruct(small.shape, F32),
                   jax.ShapeDtypeStruct((N_DEV,) + rows.shape, F32)],
        scratch_shapes=_reduce_scratch(gw_in, small, rows)
        + [pltpu.SemaphoreType.DMA((N_REDUCE_SEMS + N_SMALL_SEMS,))] * 2,
        compiler_params=_params(),
    )(gw_in, small, rows)


def _ada_forward(c_all, w_ada):
    def body(c_ref, w_ref, sc_ref, part_ref):
        cc = c_ref[...]
        sc = cc * _sigmoid(cc)
        sc_ref[...] = sc
        part_ref[...] = _dot3_rhs(sc, w_ref[...])

    return pl.pallas_call(
        body, name="ada_forward",
        out_shape=[jax.ShapeDtypeStruct(c_all.shape, F32),
                   jax.ShapeDtypeStruct((N_DEV, w_ada.shape[1]), F32)],
        compiler_params=_params(),
    )(c_all, w_ada)


def _dot3_rhs(a, b):
    a0, a1, a2 = _split3(a)
    b0, b1, b2 = _split3(b)
    return (_dot(a0, b0) + (_dot(a0, b1) + _dot(a1, b0))
            + (_dot(a0, b2) + _dot(a1, b1) + _dot(a2, b0)))


def _inproj_forward(x, mod, w_main, w_f, b_main, b_f, tile):
    seq = x.shape[0]
    nt = seq // tile

    def body(x_ref, mod_ref, w_ref, wf_ref, b_ref, bf_ref,
             qp_ref, kp_ref, vp_ref, f_ref, p_ref, ga_ref, gp_ref, u_ref, carry_ref):
        i = pl.program_id(0)

        @pl.when(i == 0)
        def _():
            carry_ref[...] = jnp.zeros_like(carry_ref)

        u = x_ref[...] * mod_ref[0:1, :] + mod_ref[1:2, :]
        ub = u.astype(BF16)
        u_ref[...] = ub

        f = _dot_nt(ub, wf_ref[...]) + bf_ref[...]
        f_ref[...] = f
        lane = _lanes((tile, LANE))
        log_f = jnp.where(lane < N_HEADS, jnp.minimum(f, 0.0) - jnp.log(1.0 + jnp.exp(-jnp.abs(f))), 0.0)
        row = lax.broadcasted_iota(jnp.int32, (tile, tile), 0)
        col = lax.broadcasted_iota(jnp.int32, (tile, tile), 1)
        tri = (row >= col).astype(BF16)
        cum = _dot3(tri, log_f) + carry_ref[0:1, :]
        carry_ref[0:1, :] = cum[tile - 1:tile, :]
        cq = [part.astype(F32) for part in _split3(cum)]
        ck = [part.astype(F32) for part in _split3(-cum)]

        def proj(chunk):
            cols = pl.ds(chunk * COL_CHUNK, COL_CHUNK)
            return _dot_nt(ub, w_ref[cols, :]) + b_ref[:, cols]

        def head_tiles(r):
            for pair in range(N_HEADS // 2):
                both = r[:, pair * LANE:(pair + 1) * LANE]
                yield 2 * pair, both
                yield 2 * pair + 1, pltpu.roll(both, HEAD_DIM, 1)

        for h, val in head_tiles(proj(0)):
            extra = jnp.where((lane >= L_CK) & (lane < L_CK + 3), 1.0, 0.0)
            extra = _place3(lane, L_CQ, [part[:, h:h + 1] for part in cq], extra)
            qp_ref[h] = jnp.where(lane < HEAD_DIM, val * Q_SCALE, extra).astype(BF16)
        for h, val in head_tiles(proj(1)):
            ones = ((lane >= L_CQ) & (lane < L_CQ + 3)) | ((lane >= L_LSE) & (lane < L_LSE + 3))
            extra = _place3(lane, L_CK, [part[:, h:h + 1] for part in ck], jnp.where(ones, 1.0, 0.0))
            kp_ref[h] = jnp.where(lane < HEAD_DIM, val, extra).astype(BF16)
        for h, val in head_tiles(proj(2)):
            extra = jnp.where((lane >= HEAD_DIM) & (lane < HEAD_DIM + 3), -1.0, 0.0)
            vp_ref[h] = jnp.where(lane < HEAD_DIM, val, extra).astype(BF16)
        p_ref[...] = proj(3)
        ga_ref[...] = proj(4)
        gp_ref[...] = proj(5)

    head_block = pl.BlockSpec((N_HEADS, tile, LANE), lambda i: (0, i, 0))
    tok = lambda width: pl.BlockSpec((tile, width), lambda i: (i, 0))
    whole = lambda a: pl.BlockSpec(a.shape, lambda i: (0,) * a.ndim)
    padded = jax.ShapeDtypeStruct((N_HEADS, seq, LANE), BF16)
    half = jax.ShapeDtypeStruct((seq, D_ATT), F32)
    return pl.pallas_call(
        body, name="inproj_forward", grid=(nt,),
        in_specs=[tok(D), whole(mod), whole(w_main), whole(w_f), whole(b_main), whole(b_f)],
        out_specs=[head_block, head_block, head_block, tok(LANE), tok(D_POOL), tok(D_ATT), tok(D_POOL),
                   tok(D)],
        out_shape=[padded, padded, padded, jax.ShapeDtypeStruct((seq, LANE), F32), half, half, half,
                   jax.ShapeDtypeStruct((seq, D), BF16)],
        scratch_shapes=[pltpu.VMEM((8, LANE), F32)],
        compiler_params=_params(("arbitrary",)),
    )(x, mod, w_main, w_f, b_main, b_f)


def _attention_forward(qp, kp, vp, w_out, tile):
    seq = qp.shape[1]
    nb = seq // tile
    steps = (N_HEADS // 2) * nb

    def body(q_ref, k_ref, v_ref, wo_ref, att_ref, q2t_ref, wo_all_ref, s_a, s_b, m_ref, acc_ref,
             send_sems, recv_sems, local_sem):
        step = pl.program_id(0) * nb + pl.program_id(1)
        start, forward, finish = _gather_stages(wo_ref, wo_all_ref, send_sems, recv_sems, local_sem.at[0])
        pl.when(step == 0)(start)
        pl.when(step == steps // 2)(forward)

        i = pl.program_id(1)
        sub = lax.broadcasted_iota(jnp.int32, (LANE, tile), 0)
        row = lax.broadcasted_iota(jnp.int32, (tile, tile), 0)
        col = lax.broadcasted_iota(jnp.int32, (tile, tile), 1)
        q = [q_ref[0], q_ref[1]]

        def scores(buf, kb):
            rows = pl.ds(pl.multiple_of(kb * tile, tile), tile)
            for hh in range(2):
                buf[hh] = _dot_nt(k_ref[hh, rows, :], q[hh])

        def absorb(buf, kb, masked):
            rows = pl.ds(pl.multiple_of(kb * tile, tile), tile)
            for hh in range(2):
                m = m_ref[hh, 0:1, :]
                s = buf[hh]
                if masked:
                    s = jnp.where(row <= col, s, -1e30)
                m_new = jnp.maximum(m, jnp.max(s, axis=0, keepdims=True))
                p = jnp.exp(s - m_new).astype(BF16)
                acc_ref[hh] = jnp.exp(m - m_new) * acc_ref[hh] + _dot_tn(v_ref[hh, rows, :], p)
                m_ref[hh, 0:1, :] = m_new

        def two_blocks(j, _):
            scores(s_b, 2 * j + 1)
            absorb(s_a, 2 * j, False)
            scores(s_a, 2 * j + 2)
            absorb(s_b, 2 * j + 1, False)
            return 0

        def last_block():
            absorb(s_a, i, True)

        def last_two_blocks():
            scores(s_b, i)
            absorb(s_a, i - 1, False)
            absorb(s_b, i, True)

        scores(s_a, 0)
        m_ref[...] = jnp.full(m_ref.shape, -1e30, F32)
        acc_ref[...] = jnp.zeros_like(acc_ref)
        lax.fori_loop(0, i // 2, two_blocks, 0)
        lax.cond(i % 2 == 0, last_block, last_two_blocks)
        outs = []
        for hh in range(2):
            m, acc = m_ref[hh, 0:1, :], acc_ref[hh]
            l = -acc[HEAD_DIM:HEAD_DIM + 1, :]
            outs.append((acc / l)[:HEAD_DIM, :])
            neg_lse = [part.astype(F32) for part in _split3(-(m + jnp.log(l)))]
            q2t_ref[hh] = _place3(sub, L_LSE, neg_lse, q[hh].astype(F32).T).astype(BF16)
        att_ref[...] = jnp.concatenate(outs, axis=0).T
        pl.when(step == steps - 1)(finish)

    pair = pl.BlockSpec((2, tile, LANE), lambda hp, i: (hp, i, 0))
    full = pl.BlockSpec((2, seq, LANE), lambda hp, i: (hp, 0, 0))
    return pl.pallas_call(
        body, name="attention_forward", grid=(N_HEADS // 2, nb),
        in_specs=[pair, full, full, ANY],
        out_specs=[pl.BlockSpec((tile, LANE), lambda hp, i: (i, hp)),
                   pl.BlockSpec((2, LANE, tile), lambda hp, i: (hp, 0, i)), ANY],
        out_shape=[jax.ShapeDtypeStruct((seq, D_ATT), F32),
                   jax.ShapeDtypeStruct((N_HEADS, LANE, seq), BF16),
                   jax.ShapeDtypeStruct((N_DEV,) + w_out.shape, w_out.dtype)],
        scratch_shapes=[pltpu.VMEM((2, tile, tile), F32), pltpu.VMEM((2, tile, tile), F32),
                        pltpu.VMEM((2, 8, tile), F32), pltpu.VMEM((2, LANE, tile), F32),
                        pltpu.SemaphoreType.DMA((7,)), pltpu.SemaphoreType.DMA((7,)), pltpu.SemaphoreType.DMA((1,))],
        compiler_params=_params(("arbitrary", "arbitrary")),
    )(qp, kp, vp, w_out)


def _window_sum(x, halo, window, transposed):
    tile = x.shape[0]

    def split_cat(a):
        hi = a.astype(BF16)
        return jnp.concatenate([hi, (a - hi.astype(F32)).astype(BF16)], axis=1)

    def fold(r):
        return r[:, :LANE] + r[:, LANE:]

    r = lax.broadcasted_iota(jnp.int32, (tile, tile), 0)
    c = lax.broadcasted_iota(jnp.int32, (tile, tile), 1)
    rh = lax.broadcasted_iota(jnp.int32, (HALO, HALO), 0)
    ch = lax.broadcasted_iota(jnp.int32, (HALO, HALO), 1)
    if not transposed:
        band = (c <= r) & (r - c < window)
        edge = (rh + HALO - ch) < window
    else:
        band = (r <= c) & (c - r < window)
        edge = (HALO + ch - rh) < window
    out = fold(_dot(band.astype(BF16), split_cat(x)))
    reach = fold(_dot(edge.astype(BF16), split_cat(halo)))
    if not transposed:
        return jnp.concatenate([out[:HALO] + reach, out[HALO:]], axis=0)
    return jnp.concatenate([out[:tile - HALO], out[tile - HALO:] + reach], axis=0)


def _silu_parts(g):
    sig = _sigmoid(g)
    return g * sig, sig * (1.0 + g * (1.0 - sig))


def _middle(x, tgt, att, g_att, g_pool, p, vecs, pool_vecs, w_out, w_pool, tile):
    seq = x.shape[0]
    nt = seq // tile
    halo_blocks = tile // HALO

    def body(x_ref, tgt_ref, att_ref, ga_ref, gp_ref, p_ref, ph_ref, vec_ref, pvec_ref, wo_ref, wp_ref,
             dxa_ref, do2_ref, dga_ref, dgp_ref, dpooled_ref, dwo_ref, dwp_ref, dvec_ref, dpvec_ref):
        i = pl.program_id(0)

        @pl.when(i == 0)
        def _():
            dwo_ref[...] = jnp.zeros_like(dwo_ref)
            dwp_ref[...] = jnp.zeros_like(dwp_ref)
            dvec_ref[...] = jnp.zeros_like(dvec_ref)
            dpvec_ref[...] = jnp.zeros_like(dpvec_ref)

        gate, b_out, ln_g, ln_b = (vec_ref[k:k + 1, :] for k in range(4))
        b_pool, pool_scale = pvec_ref[0:1, :], pvec_ref[1:2, :]
        x = x_ref[...]
        p = p_ref[...]
        p_halo = ph_ref[...] * jnp.where(i > 0, 1.0, 0.0)
        pos = i * tile + lax.broadcasted_iota(jnp.int32, (tile, 1), 0) + 1

        pooled, mixed = [], []
        for g, window in enumerate(POOL_WINDOWS):
            cols = slice(g * GROUP_DIM, (g + 1) * GROUP_DIM)
            wsum = _window_sum(p[:, cols], p_halo[:, cols], window, False)
            count = jnp.minimum(pos, window).astype(F32)
            pooled.append(wsum / count - p[:, cols])
            mixed.append(_dot(pooled[g].astype(BF16), wp_ref[g]) + b_pool[:, cols])
        mixed = jnp.concatenate(mixed, axis=1)
        pool = mixed * pool_scale

        att = att_ref[...]
        g_att, g_pool = ga_ref[...], gp_ref[...]
        silu_a, dsilu_a = _silu_parts(g_att)
        silu_p, dsilu_p = _silu_parts(g_pool)
        y_in = jnp.concatenate([att * silu_a, pool * silu_p], axis=1)
        y = _dot(y_in.astype(BF16), wo_ref[...]) + b_out
        h = ALPHA * x + gate * y
        mu = jnp.mean(h, axis=1, keepdims=True)
        hc = h - mu
        var = jnp.mean(hc * hc, axis=1, keepdims=True)
        rstd = lax.rsqrt(var + LN_EPS)
        yhat = hc * rstd
        diff = yhat * ln_g + ln_b - tgt_ref[...]
        loss_rows = jnp.sum(diff * diff, axis=1, keepdims=True)
        d_out = diff * (1.0 / D)

        d_yhat = d_out * ln_g
        dh = rstd * (d_yhat - jnp.mean(d_yhat, axis=1, keepdims=True)
                     - yhat * jnp.mean(d_yhat * yhat, axis=1, keepdims=True))
        dxa_ref[...] = ALPHA * dh
        dy = dh * gate
        dyb = dy.astype(BF16)
        lane = _lanes((1, D))
        loss_row = jnp.where(lane == 0, (0.5 / D) * jnp.sum(loss_rows, axis=0, keepdims=True), 0.0)
        dvec_ref[0:1, :] += jnp.sum(dh * y, axis=0, keepdims=True)
        dvec_ref[1:2, :] += jnp.sum(dy, axis=0, keepdims=True)
        dvec_ref[2:3, :] += jnp.sum(d_out * yhat, axis=0, keepdims=True)
        dvec_ref[3:4, :] += jnp.sum(d_out, axis=0, keepdims=True)
        dvec_ref[4:5, :] += loss_row

        dwo_ref[...] += _dot(y_in.T.astype(BF16), dyb)
        d_yin = _dot_nt(dyb, wo_ref[...])
        d_a, d_pl = d_yin[:, :D_ATT], d_yin[:, D_ATT:]
        d_att = d_a * silu_a
        d_att_t = d_att.T
        prod_t = (d_att * att).T
        sub = lax.broadcasted_iota(jnp.int32, (HEAD_DIM, tile), 0)
        for h in range(N_HEADS):
            rows = slice(h * HEAD_DIM, (h + 1) * HEAD_DIM)
            delta = jnp.sum(prod_t[rows], axis=0, keepdims=True)
            extra = _place3(sub, 0, [part.astype(F32) for part in _split3(delta)], 0.0)
            do2_ref[h] = jnp.concatenate([d_att_t[rows], extra], axis=0).astype(BF16)
        dga_ref[...] = d_a * att * dsilu_a
        dgp_ref[...] = d_pl * pool * dsilu_p
        d_pool = d_pl * silu_p
        d_mixed = d_pool * pool_scale
        dpvec_ref[0:1, :] += jnp.sum(d_mixed, axis=0, keepdims=True)
        dpvec_ref[1:2, :] += jnp.sum(d_pool * mixed, axis=0, keepdims=True)
        d_pooled = []
        for g in range(len(POOL_WINDOWS)):
            cols = slice(g * GROUP_DIM, (g + 1) * GROUP_DIM)
            dmb = d_mixed[:, cols].astype(BF16)
            dwp_ref[g] += _dot(pooled[g].T.astype(BF16), dmb)
            d_pooled.append(_dot_nt(dmb, wp_ref[g]))
        dpooled_ref[...] = jnp.concatenate(d_pooled, axis=1)

    tok = lambda width: pl.BlockSpec((tile, width), lambda i: (i, 0))
    whole = lambda a: pl.BlockSpec(a.shape, lambda i: (0,) * a.ndim)
    halo = pl.BlockSpec((HALO, D_POOL), lambda i: (jnp.maximum(i * halo_blocks - 1, 0), 0))
    half = jax.ShapeDtypeStruct((seq, D_ATT), F32)
    outs = [jax.ShapeDtypeStruct((seq, D), F32), jax.ShapeDtypeStruct((N_HEADS, LANE, seq), BF16), half, half, half,
            jax.ShapeDtypeStruct(w_out.shape, F32), jax.ShapeDtypeStruct(w_pool.shape, F32),
            jax.ShapeDtypeStruct(vecs.shape, F32), jax.ShapeDtypeStruct(pool_vecs.shape, F32)]
    return pl.pallas_call(
        body, name="middle", grid=(nt,),
        in_specs=[tok(D), tok(D), tok(D_ATT), tok(D_ATT), tok(D_POOL), tok(D_POOL), halo,
                  whole(vecs), whole(pool_vecs), whole(w_out), whole(w_pool)],
        out_specs=[tok(D), pl.BlockSpec((N_HEADS, LANE, tile), lambda i: (0, 0, i)),
                   tok(D_ATT), tok(D_POOL), tok(D_POOL),
                   whole(w_out), whole(w_pool), whole(vecs), whole(pool_vecs)],
        out_shape=outs,
        compiler_params=_params(("arbitrary",)),
    )(x, tgt, att, g_att, g_pool, p, p, vecs, pool_vecs, w_out, w_pool)


def _attention_backward(q2t, kp, vp, do2t, gw_out, small, tile):
    seq = kp.shape[1]
    nb = seq // tile
    last = N_HEADS // 2 - 1

    def body(qt_ref, k_ref, v_ref, dot_ref, gwo_hbm, small_hbm,
             dq_ref, dk_ref, dv_ref, dcum_ref, g_out_ref, total_ref,
             dq_acc, dk_acc, dv_acc, gwo_ref, r1, s2, r2, sm_sib, sm_chip, sm_recv, small_ref, send_sems, recv_sems):
        hp = pl.program_id(0)
        start, middle, finish = _reduce_stages(
            [gwo_ref], [g_out_ref], [r1], [s2], [r2], (small_ref, total_ref, sm_sib, sm_chip, sm_recv),
            send_sems, recv_sems)

        @pl.when(hp == 0)
        def _():
            pltpu.sync_copy(gwo_hbm, gwo_ref)
            pltpu.sync_copy(small_hbm, small_ref)
            start()

        pl.when(hp == 1)(middle)

        row = lax.broadcasted_iota(jnp.int32, (tile, tile), 0)
        col = lax.broadcasted_iota(jnp.int32, (tile, tile), 1)
        dq_acc[...] = jnp.zeros_like(dq_acc)

        def kv_block(kb, _):
            krows = pl.ds(pl.multiple_of(kb * tile, tile), tile)
            k = [k_ref[hh, krows, :] for hh in range(2)]
            v = [v_ref[hh, krows, :] for hh in range(2)]
            k_t = [k[hh].T for hh in range(2)]

            def q_block(qb, masked):
                qcols = pl.ds(pl.multiple_of(qb * tile, tile), tile)
                for hh in range(2):
                    q_t = qt_ref[hh, :, qcols]
                    do_t = dot_ref[hh, :, qcols]
                    s_t = _dot(k[hh], q_t)
                    if masked:
                        s_t = jnp.where(row <= col, s_t, -1e30)
                    p_t = jnp.exp(s_t)
                    ds_t = (p_t * _dot(v[hh], do_t)).astype(BF16)
                    dv_new = _dot_nt(do_t, p_t.astype(BF16))
                    dk_new = _dot_nt(q_t, ds_t)
                    if masked:
                        dv_acc[hh], dk_acc[hh] = dv_new, dk_new
                    else:
                        dv_acc[hh] += dv_new
                        dk_acc[hh] += dk_new
                    dq_acc[hh, :, qcols] += _dot(k_t[hh], ds_t)

            q_block(kb, True)

            def two_later_blocks(j, _):
                q_block(kb + 1 + 2 * j, False)
                q_block(kb + 2 + 2 * j, False)
                return 0

            later = nb - 1 - kb
            lax.fori_loop(0, later // 2, two_later_blocks, 0)
            pl.when(later % 2 == 1)(lambda: q_block(nb - 1, False))
            for hh in range(2):
                dk = dk_acc[hh]
                dk_ref[hh, :, krows] = dk.astype(BF16)
                dv_ref[hh, :, krows] = dv_acc[hh].astype(BF16)
                dcum_ref[hh, :, krows] = -dk[L_CK:L_CK + 1, :]
            return 0

        lax.fori_loop(0, nb, kv_block, 0)
        for hh in range(2):
            dq = dq_acc[hh]
            dcum_ref[hh] += dq[L_CQ:L_CQ + 1, :]
            dq_ref[hh] = (dq * Q_SCALE).astype(BF16)
        pl.when(hp == last)(finish)

    pair = pl.BlockSpec((2, seq, LANE), lambda hp: (hp, 0, 0))
    pair_t = pl.BlockSpec((2, LANE, seq), lambda hp: (hp, 0, 0))
    whole = lambda shape: pl.BlockSpec(shape, lambda hp: (0,) * len(shape))
    grad = jax.ShapeDtypeStruct((N_HEADS, LANE, seq), BF16)
    return pl.pallas_call(
        body, name="attention_backward", grid=(N_HEADS // 2,),
        in_specs=[pair_t, pair, pair, pair_t, ANY, ANY],
        out_specs=[pair_t, pair_t, pair_t, pl.BlockSpec((2, 1, seq), lambda hp: (hp, 0, 0)),
                   whole(gw_out.shape[1:]), whole(small.shape)],
        out_shape=[grad, grad, grad, jax.ShapeDtypeStruct((N_HEADS, 1, seq), F32),
                   jax.ShapeDtypeStruct(gw_out.shape[1:], F32), jax.ShapeDtypeStruct(small.shape, F32)],
        scratch_shapes=[pltpu.VMEM((2, LANE, seq), F32), pltpu.VMEM((2, LANE, tile), F32),
                        pltpu.VMEM((2, LANE, tile), F32), pltpu.VMEM(gw_out.shape, BF16)]
        + _reduce_scratch(gw_out, small)
        + [pltpu.VMEM(small.shape, F32),
           pltpu.SemaphoreType.DMA((N_REDUCE_SEMS + N_SMALL_SEMS,)), pltpu.SemaphoreType.DMA((N_REDUCE_SEMS + N_SMALL_SEMS,))],
        compiler_params=_params(("arbitrary",)),
    )(q2t, kp, vp, do2t, gw_out, small)


def _inproj_backward(dqp, dkp, dvp, d_cum, f, d_pooled, d_ga, d_gp, x, dxa, u, mod, w_main, w_f, tile):
    seq = x.shape[0]
    nt = seq // tile
    halo_blocks = tile // HALO

    def body(dq_ref, dk_ref, dv_ref, dcum_ref, f_ref, dpo_ref, dph_ref, dga_ref, dgp_ref, x_ref, dxa_ref, u_ref,
             mod_ref, w_ref, wf_ref,
             dx_ref, dproj_ref, dwf_ref, db_ref, dbf_ref, dmod_ref, carry_ref):
        step = pl.program_id(0)
        i = nt - 1 - step

        @pl.when(step == 0)
        def _():
            carry_ref[...] = jnp.zeros_like(carry_ref)
            dwf_ref[...] = jnp.zeros_like(dwf_ref)
            db_ref[...] = jnp.zeros_like(db_ref)
            dbf_ref[...] = jnp.zeros_like(dbf_ref)
            dmod_ref[...] = jnp.zeros_like(dmod_ref)

        ones = jnp.ones((8, tile), BF16)

        def emit(chunk, val):
            cols = pl.ds(chunk * COL_CHUNK, COL_CHUNK)
            db_ref[0:1, cols] += jnp.sum(val, axis=0, keepdims=True)
            vb = val.astype(BF16)
            dproj_ref[:, pl.ds((chunk - 3) * COL_CHUNK, COL_CHUNK)] = vb
            return _dot(vb, w_ref[cols, :])

        d_u = jnp.zeros((tile, D), F32)
        for chunk, ref in enumerate((dq_ref, dk_ref, dv_ref)):
            cols = pl.ds(chunk * COL_CHUNK, COL_CHUNK)
            val_t = ref[:, 0:HEAD_DIM, :].reshape(COL_CHUNK, tile)
            db_ref[:, cols] += _dot_nt(ones, val_t)
            d_u += _dot_tn(val_t, w_ref[cols, :])

        d_pooled = dpo_ref[...]
        d_halo = dph_ref[...] * jnp.where(i < nt - 1, 1.0, 0.0)
        pos = i * tile + lax.broadcasted_iota(jnp.int32, (tile, 1), 0) + 1
        d_p = []
        for g, window in enumerate(POOL_WINDOWS):
            cols = slice(g * GROUP_DIM, (g + 1) * GROUP_DIM)
            scaled = d_pooled[:, cols] / jnp.minimum(pos, window).astype(F32)
            d_p.append(_window_sum(scaled, d_halo[:, cols] * (1.0 / window), window, True) - d_pooled[:, cols])
        d_u += emit(3, jnp.concatenate(d_p, axis=1))
        d_u += emit(4, dga_ref[...])
        d_u += emit(5, dgp_ref[...])

        row = lax.broadcasted_iota(jnp.int32, (tile, tile), 0)
        col = lax.broadcasted_iota(jnp.int32, (tile, tile), 1)
        later = (row >= col).astype(BF16)
        d_logf = sum(_dot(part, later) for part in _split3(dcum_ref[:, 0, :])) + carry_ref[:, 0:1]
        carry_ref[:, 0:1] = d_logf[:, 0:1]
        d_f = d_logf * _sigmoid(-f_ref[...].T[0:N_HEADS, :])
        d_f = jnp.concatenate([d_f, jnp.zeros((LANE - N_HEADS, tile), F32)], axis=0)
        dbf_ref[...] += sum(_dot_nt(ones, part) for part in _split3(d_f))
        d_fb = d_f.astype(BF16)
        d_u += _dot_tn(d_fb, wf_ref[...])
        dwf_ref[...] += _dot(d_fb, u_ref[...])

        x = x_ref[...]
        dx_ref[...] = dxa_ref[...] + d_u * mod_ref[0:1, :]
        dmod_ref[0:1, :] += jnp.sum(d_u * x, axis=0, keepdims=True)
        dmod_ref[1:2, :] += jnp.sum(d_u, axis=0, keepdims=True)

    rev = lambda step: nt - 1 - step
    tok = lambda width: pl.BlockSpec((tile, width), lambda s: (rev(s), 0))
    head_block = pl.BlockSpec((N_HEADS, LANE, tile), lambda s: (0, 0, rev(s)))
    whole = lambda a: pl.BlockSpec(a.shape, lambda s: (0,) * a.ndim)
    halo = pl.BlockSpec((HALO, D_POOL), lambda s: (jnp.minimum((rev(s) + 1) * halo_blocks, seq // HALO - 1), 0))
    small = lambda width: jax.ShapeDtypeStruct((8, width), F32)
    n_rest = N_MAIN - OFF_P
    return pl.pallas_call(
        body, name="inproj_backward", grid=(nt,),
        in_specs=[head_block, head_block, head_block, pl.BlockSpec((N_HEADS, 1, tile), lambda s: (0, 0, rev(s))),
                  tok(LANE), tok(D_POOL), halo, tok(D_ATT), tok(D_POOL),
                  tok(D), tok(D), tok(D),
                  whole(mod), whole(w_main), whole(w_f)],
        out_specs=[tok(D), tok(n_rest), pl.BlockSpec((LANE, D), lambda s: (0, 0)),
                   pl.BlockSpec((8, N_MAIN), lambda s: (0, 0)), pl.BlockSpec((8, LANE), lambda s: (0, 0)),
                   pl.BlockSpec((8, D), lambda s: (0, 0))],
        out_shape=[jax.ShapeDtypeStruct((seq, D), F32), jax.ShapeDtypeStruct((seq, n_rest), BF16),
                   jax.ShapeDtypeStruct((LANE, D), F32), small(N_MAIN), small(LANE), small(D)],
        scratch_shapes=[pltpu.VMEM((8, LANE), F32)],
        compiler_params=_params(("arbitrary",)),
    )(dqp, dkp, dvp, d_cum, f, d_pooled, d_pooled, d_ga, d_gp, x, dxa, u, mod, w_main, w_f)


def _weight_grad(dproj, u, k_tile):
    seq, n_cols = dproj.shape
    nk = seq // k_tile

    def body(dp_ref, u_ref, out_ref):
        @pl.when(pl.program_id(1) == 0)
        def _():
            out_ref[...] = jnp.zeros_like(out_ref)

        out_ref[...] += _dot_tn(dp_ref[...], u_ref[...])

    return pl.pallas_call(
        body, name="weight_grad", grid=(n_cols // COL_CHUNK, nk),
        in_specs=[pl.BlockSpec((k_tile, COL_CHUNK), lambda n, k: (k, n)),
                  pl.BlockSpec((k_tile, D), lambda n, k: (k, 0))],
        out_specs=pl.BlockSpec((COL_CHUNK, D), lambda n, k: (n, 0)),
        out_shape=jax.ShapeDtypeStruct((n_cols, D), F32),
        compiler_params=_params(("arbitrary", "arbitrary")),
    )(dproj, u)


def _weight_grad_heads(grad_t, u, k_tile, name):
    seq = u.shape[0]
    nk = seq // k_tile

    def body(g_ref, u_ref, out_ref):
        @pl.when(pl.program_id(0) == 0)
        def _():
            out_ref[...] = jnp.zeros_like(out_ref)

        out_ref[...] += _dot(g_ref[...].reshape(N_HEADS * HEAD_DIM, k_tile), u_ref[...])

    return pl.pallas_call(
        body, name=name, grid=(nk,),
        in_specs=[pl.BlockSpec((N_HEADS, HEAD_DIM, k_tile), lambda k: (0, 0, k)),
                  pl.BlockSpec((k_tile, D), lambda k: (k, 0))],
        out_specs=pl.BlockSpec((N_HEADS * HEAD_DIM, D), lambda k: (0, 0)),
        out_shape=jax.ShapeDtypeStruct((N_HEADS * HEAD_DIM, D), F32),
        compiler_params=_params(("arbitrary",)),
    )(grad_t, u)


def _adamw(w, g, m, v):
    m = ADAM_B1 * m + (1.0 - ADAM_B1) * g
    v = ADAM_B2 * v + (1.0 - ADAM_B2) * (g * g)
    m_hat = m / (1.0 - ADAM_B1 ** ADAM_STEP)
    v_hat = v / (1.0 - ADAM_B2 ** ADAM_STEP)
    delta = -ADAM_LR * (m_hat / (jnp.sqrt(v_hat) + ADAM_EPS) + ADAM_WD * w)
    return delta, m, v


def _adamw_call(g, w, m, v, lead_tile, name):
    nr = w.shape[0] // lead_tile

    def body(gi_ref, w_ref, m_ref, v_ref, g_ref, d_ref, nm_ref, nv_ref):
        g = gi_ref[...]
        g_ref[...] = g
        d_ref[...], nm_ref[...], nv_ref[...] = _adamw(w_ref[...], g, m_ref[...], v_ref[...])

    blk = pl.BlockSpec((lead_tile,) + w.shape[1:], lambda r: (r,) + (0,) * (w.ndim - 1))
    shape = jax.ShapeDtypeStruct(w.shape, F32)
    return pl.pallas_call(
        body, name=name, grid=(nr,),
        in_specs=[blk, blk, blk, blk], out_specs=[blk, blk, blk, blk],
        out_shape=[shape, shape, shape, shape],
        compiler_params=_params(("arbitrary",)),
    )(g, w, m, v)


def _sum_adamw(parts, w, m, v, row_tile, name):
    rows, cols = w.shape
    nr = rows // row_tile

    def body(parts_ref, w_ref, m_ref, v_ref, g_ref, d_ref, nm_ref, nv_ref):
        g = parts_ref[0]
        for k in range(1, N_DEV):
            g = g + parts_ref[k]
        g_ref[...] = g
        d_ref[...], nm_ref[...], nv_ref[...] = _adamw(w_ref[...], g, m_ref[...], v_ref[...])

    blk = pl.BlockSpec((row_tile, cols), lambda r: (r, 0))
    shape = jax.ShapeDtypeStruct(w.shape, F32)
    return pl.pallas_call(
        body, name=name, grid=(nr,),
        in_specs=[pl.BlockSpec((N_DEV, row_tile, cols), lambda r: (0, r, 0)), blk, blk, blk],
        out_specs=[blk, blk, blk, blk],
        out_shape=[shape, shape, shape, shape],
        compiler_params=_params(("arbitrary",)),
    )(parts, w, m, v)


def _ada_adamw(sc_t, d_ada, w, m, v):
    def body(sc_ref, d_ref, w_ref, m_ref, v_ref, g_ref, dl_ref, nm_ref, nv_ref):
        g = sc_ref[:, 0:1] * d_ref[0:1, :]
        for b in range(1, N_DEV):
            g = g + sc_ref[:, b:b + 1] * d_ref[b:b + 1, :]
        g_ref[...] = g
        dl_ref[...], nm_ref[...], nv_ref[...] = _adamw(w_ref[...], g, m_ref[...], v_ref[...])

    shape = jax.ShapeDtypeStruct(w.shape, F32)
    return pl.pallas_call(
        body, name="ada_adamw", out_shape=[shape, shape, shape, shape], compiler_params=_params(),
    )(sc_t, d_ada, w, m, v)


F_LO, F_HI = 3 * D_ATT, 3 * D_ATT + N_HEADS


def _split_forget(a, axis):
    idx = lambda lo, hi: tuple(slice(lo, hi) if d == axis else slice(None) for d in range(a.ndim))
    pad = [(0, LANE - N_HEADS) if d == axis else (0, 0) for d in range(a.ndim)]
    return jnp.concatenate([a[idx(0, F_LO)], a[idx(F_HI, D_IN)]], axis=axis), jnp.pad(a[idx(F_LO, F_HI)], pad)


def _join_forget(main, f, axis):
    idx = lambda lo, hi: tuple(slice(lo, hi) if d == axis else slice(None) for d in range(main.ndim))
    return jnp.concatenate([main[idx(0, F_LO)], f[idx(0, N_HEADS)], main[idx(F_LO, N_MAIN)]], axis=axis)


_EARLY = ((("w_pool", 65536), ("b_pool", 512), ("pool_scale", 512), ("b_out", 1024), ("ln_g", 1024), ("ln_b", 1024),
           ("loss", 128)), 552)
_LATE = ((("b_in", 3200),), 32)


def _pack_small(parts, layout):
    spec, rows = layout
    flat = []
    for name, size in spec:
        a = parts[name].reshape(-1)
        flat.append(jnp.pad(a, (0, size - a.shape[0])))
    flat = jnp.concatenate(flat)
    flat = jnp.pad(flat, (0, rows * LANE - flat.shape[0]))
    return flat.reshape(rows, LANE)


def _unpack_small(packed, shapes, layout):
    flat = packed.reshape(-1)
    out, off = {}, 0
    for name, size in layout[0]:
        n = 1
        for s in shapes[name]:
            n *= s
        out[name] = flat[off:off + n].reshape(shapes[name])
        off += size
    return out


def kernel(x, c, w_ada, b_ada, w_in, b_in, w_pool_mix, b_pool_mix, pool_scale, w_out, b_out, ln_g, ln_b, loss_target, m_w_ada, m_b_ada, m_w_in, m_b_in, m_w_pool_mix, m_b_pool_mix, m_pool_scale, m_w_out, m_b_out, m_ln_g, m_ln_b, v_w_ada, v_b_ada, v_w_in, v_b_in, v_w_pool_mix, v_b_pool_mix, v_pool_scale, v_w_out, v_b_out, v_ln_g, v_ln_b):
    seq = x.shape[1]
    tile = min(256, seq)
    attn_tile = min(512, max(128, seq // 4))
    me = _dev_index(*_mesh_pos())
    x2, tgt = x[0], loss_target[0]

    rows_in = D_IN // N_DEV
    c_all, w_in_g = _all_gather([jnp.pad(c, ((0, 7), (0, 0))), w_in[0].T.astype(BF16)], "gather_weights")
    sc_all, ada_part = _ada_forward(c_all[:, 0, :], w_ada[0])
    (ada_mine,) = _exchange([ada_part.reshape(N_DEV, 1, -1)], [], "exchange_ada")
    ada = ada_mine.reshape(1, D_ADA) + b_ada
    shift, scale, gate = ada[:, 0:D], ada[:, D:2 * D], ada[:, 2 * D:]
    mod = jnp.concatenate([1.0 + scale, shift, jnp.zeros((6, D), F32)], axis=0)

    w_main, w_f = _split_forget(w_in_g.reshape(D_IN, D), 0)
    b_main, b_f = _split_forget(b_in, 1)

    qp, kp, vp, f, p, g_att, g_pool, u = _inproj_forward(x2, mod, w_main, w_f, b_main, b_f, tile)
    att, q2t, w_out_g = _attention_forward(qp, kp, vp, w_out[0].astype(BF16), attn_tile)

    vecs = jnp.concatenate([gate, b_out, ln_g, ln_b, jnp.zeros((4, D), F32)], axis=0)
    pool_vecs = jnp.concatenate([b_pool_mix.reshape(1, D_POOL), pool_scale, jnp.zeros((6, D_POOL), F32)], axis=0)
    dxa, do2, d_ga, d_gp, d_pooled, dw_out, dw_pool, dvec, dpvec = _middle(
        x2, tgt, att, g_att, g_pool, p, vecs, pool_vecs, w_out_g.reshape(D, D), w_pool_mix[0].astype(BF16), tile)

    gw_out = dw_out.reshape(N_DEV, D // N_DEV, D).astype(BF16)
    early = _pack_small({"w_pool": dw_pool, "b_pool": dpvec[0:1], "pool_scale": dpvec[1:2], "b_out": dvec[1:2],
                         "ln_g": dvec[2:3], "ln_b": dvec[3:4], "loss": dvec[4:5, 0:LANE]}, _EARLY)
    dqp, dkp, dvp, d_cum, g_out, early_sum = _attention_backward(q2t, kp, vp, do2, gw_out, early, attn_tile)
    dx, dproj, dw_f, db_main, db_f, dmod = _inproj_backward(
        dqp, dkp, dvp, d_cum, f, d_pooled, d_ga, d_gp, x2, dxa, u, mod, w_main, w_f, tile)
    k_tile = min(1024, seq)
    dw_q, dw_k, dw_v = (_weight_grad_heads(g, u, k_tile, "weight_grad_" + n)
                        for g, n in ((dqp, "q"), (dkp, "k"), (dvp, "v")))
    dw_rest = _weight_grad(dproj, u, k_tile)

    dw_main = jnp.concatenate([dw_q, dw_k, dw_v, dw_rest], axis=0)
    gw_in = _join_forget(dw_main, dw_f, 0).reshape(N_DEV, rows_in, D).astype(BF16)
    d_ada = jnp.concatenate([dmod[1:2], dmod[0:1], dvec[0:1]], axis=1)
    late = _pack_small({"b_in": _join_forget(db_main[0:1], db_f[0:1], 1)}, _LATE)
    g_in_rows, late_sum, d_ada_all = _reduce_grads(gw_in, late, d_ada.reshape(D_ADA // LANE, LANE))

    def rows3(a):
        return a[0].T.reshape(rows_in, D // LANE, LANE)

    outs_in = _adamw_call(g_in_rows.reshape(rows_in, D // LANE, LANE), rows3(w_in), rows3(m_w_in), rows3(v_w_in),
                          rows_in // 5, "adamw_w_in")
    g_w_in, d_w_in, nm_w_in, nv_w_in = (a.reshape(rows_in, D).T for a in outs_in)
    g_w_out, d_w_out, nm_w_out, nv_w_out = _adamw_call(g_out, w_out[0], m_w_out[0], v_w_out[0], D // N_DEV, "adamw_w_out")

    def small_adamw(grad_sum, weights, first, second, layout, name):
        packed = _adamw_call(grad_sum, _pack_small(weights, layout), _pack_small(first, layout),
                             _pack_small(second, layout), layout[1], name)
        shapes = {k: a.shape for k, a in weights.items()}
        return [_unpack_small(a, shapes, layout) for a in packed]

    zero = jnp.zeros((1,), F32)
    smalls = small_adamw(
        early_sum,
        {"w_pool": w_pool_mix, "b_pool": b_pool_mix, "pool_scale": pool_scale, "b_out": b_out, "ln_g": ln_g,
         "ln_b": ln_b, "loss": zero},
        {"w_pool": m_w_pool_mix, "b_pool": m_b_pool_mix, "pool_scale": m_pool_scale, "b_out": m_b_out,
         "ln_g": m_ln_g, "ln_b": m_ln_b, "loss": zero},
        {"w_pool": v_w_pool_mix, "b_pool": v_b_pool_mix, "pool_scale": v_pool_scale, "b_out": v_b_out,
         "ln_g": v_ln_g, "ln_b": v_ln_b, "loss": zero}, _EARLY, "adamw_small")
    b_ins = small_adamw(late_sum, {"b_in": b_in}, {"b_in": m_b_in}, {"b_in": v_b_in}, _LATE, "adamw_b_in")
    g_s, d_s, nm_s, nv_s = ({**a, **b} for a, b in zip(smalls, b_ins))
    loss = g_s["loss"][0]

    ada_rows = D_ADA // LANE
    b_ada_outs = _sum_adamw(d_ada_all, b_ada.reshape(ada_rows, LANE), m_b_ada.reshape(ada_rows, LANE),
                            v_b_ada.reshape(ada_rows, LANE), ada_rows, "adamw_b_ada")
    g_b_ada, d_b_ada, nm_b_ada, nv_b_ada = (a.reshape(1, D_ADA) for a in b_ada_outs)
    d_ada_local = lax.dynamic_slice_in_dim(d_ada_all.reshape(N_DEV, D_ADA), me * (D_ADA // N_DEV), D_ADA // N_DEV, axis=1)
    g_w_ada, d_w_ada, nm_w_ada, nv_w_ada = _ada_adamw(sc_all.T, d_ada_local, w_ada[0], m_w_ada[0], v_w_ada[0])

    def ordered(w_ada_, b_ada_, w_in_, w_out_, s):
        return (w_ada_[None], b_ada_, w_in_[None], s["b_in"], s["w_pool"], s["b_pool"], s["pool_scale"],
                w_out_[None], s["b_out"], s["ln_g"], s["ln_b"])

    return (loss, dx[None],
            *ordered(g_w_ada, g_b_ada, g_w_in, g_w_out, g_s),
            *ordered(d_w_ada, d_b_ada, d_w_in, d_w_out, d_s),
            *ordered(nm_w_ada, nm_b_ada, nm_w_in, nm_w_out, nm_s),
            *ordered(nv_w_ada, nv_b_ada, nv_w_in, nv_w_out, nv_s))
```

```python
import jax
import jax.numpy as jnp
from jax import lax
from jax.experimental import pallas as pl
from jax.experimental.pallas import tpu as pltpu

F32 = jnp.float32
BF16 = jnp.bfloat16

N_DEV = 8
D = 1024
N_HEADS = 8
HEAD_DIM = 64
D_ATT = 512
D_POOL = 512
POOL_WINDOWS = (2, 4, 8, 16)
GROUP_DIM = 128
HALO = 16
LANE = 128
D_IN = 3080
D_ADA = 3072
N_MAIN = 3072
OFF_P = 1536
COL_CHUNK = 512
Q_SCALE = 0.125
LN_EPS = 1e-5
ALPHA = 2.0 ** 0.25
L_CQ, L_CK, L_LSE = 64, 67, 70

ADAM_LR, ADAM_B1, ADAM_B2, ADAM_EPS, ADAM_WD, ADAM_STEP = 0.001, 0.9, 0.999, 1e-08, 0.01, 10
VMEM_LIMIT = 56 * 1024 * 1024

MESH = pl.DeviceIdType.MESH
ANY = pl.BlockSpec(memory_space=pl.ANY)


def _params(sem=None, vmem=VMEM_LIMIT):
    return pltpu.CompilerParams(dimension_semantics=sem, vmem_limit_bytes=vmem)


def _split3(a):
    hi = a.astype(BF16)
    r = a - hi.astype(F32)
    mid = r.astype(BF16)
    lo = (r - mid.astype(F32)).astype(BF16)
    return hi, mid, lo


def _dot(a, b):
    return jnp.dot(a, b, preferred_element_type=F32)


def _dot_nt(a, b):
    return lax.dot_general(a, b, (((1,), (1,)), ((), ())), preferred_element_type=F32)


def _dot_tn(a, b):
    return lax.dot_general(a, b, (((0,), (0,)), ((), ())), preferred_element_type=F32)


def _dot3(m01, a):
    hi, mid, lo = _split3(a)
    return _dot(m01, hi) + _dot(m01, mid) + _dot(m01, lo)


def _sigmoid(z):
    return 1.0 / (1.0 + jnp.exp(-z))


def _lanes(shape):
    return lax.broadcasted_iota(jnp.int32, shape, len(shape) - 1)


def _place3(lane, base, parts, other):
    out = other
    for j in range(3):
        out = jnp.where(lane == base + j, parts[j], out)
    return out


def _mesh_pos():
    return lax.axis_index("x"), lax.axis_index("y"), lax.axis_index("c")


def _dev_index(px, py, pc):
    return 4 * px + 2 * py + pc


def _gather_stages(src_ref, out_ref, send_sems, recv_sems, local_sem):
    x, y, c = _mesh_pos()
    me, sibling = (x, y, c), (x, y, 1 - c)
    chips = [(1 - x, y), (x, 1 - y), (1 - x, 1 - y)]

    def copy(k, block, to, src=None):
        slot = out_ref.at[_dev_index(*block)]
        return pltpu.make_async_remote_copy(
            src_ref=slot if src is None else src, dst_ref=slot, send_sem=send_sems.at[k], recv_sem=recv_sems.at[k],
            device_id=to, device_id_type=MESH)

    mine = pltpu.make_async_copy(src_ref, out_ref.at[_dev_index(*me)], local_sem)
    first = [copy(0, me, sibling, src=src_ref)] + [copy(1 + j, me, (*chip, c), src=src_ref) for j, chip in enumerate(chips)]
    passed = [copy(4 + j, (*chip, c), sibling) for j, chip in enumerate(chips)]

    def start():
        mine.start()
        for cp in first:
            cp.start()

    def forward():
        for j, chip in enumerate(chips):
            copy(1 + j, (*chip, c), me).wait_recv()
            passed[j].start()

    def finish():
        copy(0, sibling, me).wait_recv()
        for j, chip in enumerate(chips):
            copy(4 + j, (*chip, 1 - c), me).wait_recv()
        for cp in first + passed:
            cp.wait_send()
        mine.wait()

    return start, forward, finish


N_REDUCE_SEMS = 7
N_SMALL_SEMS = 4 + 7


def _reduce_stages(ins, gs, r1, s2, r2, small, send_sems, recv_sems, rows=None):
    n = len(ins)
    x, y, c = _mesh_pos()
    me = _dev_index(x, y, c)
    sibling = (x, y, 1 - c)
    chips = [(x, y), (1 - x, y), (x, 1 - y), (1 - x, 1 - y)]
    peers = []
    for p in range(1, N_DEV):
        px, py, pc = (p >> 2) & 1, (p >> 1) & 1, p & 1
        peers.append((1 - x if px else x, 1 - y if py else y, 1 - c if pc else c))
    base_small = N_REDUCE_SEMS * n

    def remote(src, dst, k, to):
        return pltpu.make_async_remote_copy(src_ref=src, dst_ref=dst, send_sem=send_sems.at[k],
                                            recv_sem=recv_sems.at[k], device_id=to, device_id_type=MESH)

    def level1(a, q):
        return remote(ins[a].at[_dev_index(*chips[q], 1 - c)], r1[a].at[q], N_REDUCE_SEMS * a + q, sibling)

    def level2(a, j):
        return remote(s2[a].at[j], r2[a].at[j], N_REDUCE_SEMS * a + 4 + j, (*chips[j + 1], c))

    if small is not None:
        small_ref, total_ref, sm_sib, sm_chip, sm_recv = small
        to_sibling = remote(small_ref, sm_sib, base_small, sibling)
        to_chips = [remote(sm_chip, sm_recv.at[j], base_small + 1 + j, (*chips[j + 1], c)) for j in range(3)]
    if rows is not None:
        rows_ref, land_ref, all_ref = rows
        row_sends = [remote(rows_ref, land_ref.at[me], base_small + 4 + k, to) for k, to in enumerate(peers)]

    def start():
        for a in range(n):
            for q in range(4):
                level1(a, q).start()
        if small is not None:
            to_sibling.start()
        if rows is not None:
            for cp in row_sends:
                cp.start()
            land_ref[me] = rows_ref[...]

    def middle():
        for a in range(n):
            for q in (1, 2, 3, 0):
                level1(a, q).wait_recv()
                pair = ins[a][_dev_index(*chips[q], c)].astype(F32) + r1[a][q].astype(F32)
                if q == 0:
                    gs[a][...] = pair
                else:
                    s2[a][q - 1] = pair.astype(BF16)
                    level2(a, q - 1).start()
        if small is not None:
            to_sibling.wait_recv()
            sm_chip[...] = small_ref[...] + sm_sib[...]
            for cp in to_chips:
                cp.start()

    def finish():
        for a in range(n):
            own = gs[a][...]
            for j in range(3):
                level2(a, j).wait_recv()
                own = own + r2[a][j].astype(F32)
            gs[a][...] = own
            for q in range(4):
                level1(a, q).wait_send()
            for j in range(3):
                level2(a, j).wait_send()
        if small is not None:
            for cp in to_chips:
                cp.wait_recv()
            total = None
            for ax in range(2):
                for ay in range(2):
                    dx, dy = x != ax, y != ay
                    term = jnp.where(dx, jnp.where(dy, sm_recv[2], sm_recv[0]), jnp.where(dy, sm_recv[1], sm_chip[...]))
                    total = term if total is None else total + term
            total_ref[...] = total
            for cp in [to_sibling] + to_chips:
                cp.wait_send()
        if rows is not None:
            for k, frm in enumerate(peers):
                remote(rows_ref, land_ref.at[_dev_index(*frm)], base_small + 4 + k, frm).wait_recv()
            all_ref[...] = land_ref[...]
            for cp in row_sends:
                cp.wait_send()

    return start, middle, finish


def _reduce_scratch(shard, small, rows=None):
    out = [pltpu.VMEM((lead,) + shard.shape[1:], BF16) for lead in (4, 3, 3)]
    out += [pltpu.VMEM(small.shape, F32), pltpu.VMEM(small.shape, F32), pltpu.VMEM((3,) + small.shape, F32)]
    if rows is not None:
        out.append(pltpu.VMEM((N_DEV,) + rows.shape, F32))
    return out


def _reduce_grads(gw_in, small, rows):
    def body(in_ref, small_ref, rows_ref, g_ref, total_ref, rows_all_ref,
             r1, s2, r2, sm_sib, sm_chip, sm_recv, rows_land, send_sems, recv_sems):
        start, middle, finish = _reduce_stages(
            [in_ref], [g_ref], [r1], [s2], [r2], (small_ref, total_ref, sm_sib, sm_chip, sm_recv),
            send_sems, recv_sems, rows=(rows_ref, rows_land, rows_all_ref))
        start()
        middle()
        finish()

    return pl.pallas_call(
        body, name="reduce_grads",
        out_shape=[jax.ShapeDtypeStruct(gw_in.shape[1:], F32), jax.ShapeDtypeStruct(small.shape, F32),
                   jax.ShapeDtypeStruct((N_DEV,) + rows.shape, F32)],
        scratch_shapes=_reduce_scratch(gw_in, small, rows)
        + [pltpu.SemaphoreType.DMA((N_REDUCE_SEMS + N_SMALL_SEMS,))] * 2,
        compiler_params=_params(),
    )(gw_in, small, rows)


def _dot3_rhs(a, b):
    a0, a1, a2 = _split3(a)
    b0, b1, b2 = _split3(b)
    return (_dot(a0, b0) + (_dot(a0, b1) + _dot(a1, b0))
            + (_dot(a0, b2) + _dot(a1, b1) + _dot(a2, b0)))


def _gather_and_ada(c, w_in_rows, w_ada):
    cols = w_ada.shape[1]

    def body(c_ref, w_ref, wa_ref, w_all_ref, sc_ref, ada_ref,
             c_land, part, ada_land, send_sems, recv_sems, local_sem, x_send, x_recv):
        x, y, cc = _mesh_pos()
        me = _dev_index(x, y, cc)
        peers = []
        for p in range(1, N_DEV):
            px, py, pc = (p >> 2) & 1, (p >> 1) & 1, p & 1
            peers.append((1 - x if px else x, 1 - y if py else y, 1 - cc if pc else cc))

        def remote(src, dst, k, to):
            return pltpu.make_async_remote_copy(src_ref=src, dst_ref=dst, send_sem=x_send.at[k], recv_sem=x_recv.at[k],
                                                device_id=to, device_id_type=MESH)

        start, forward, finish = _gather_stages(w_ref, w_all_ref, send_sems, recv_sems, local_sem.at[0])
        start()

        c_sends = [remote(c_ref, c_land.at[me], k, to) for k, to in enumerate(peers)]
        for cp in c_sends:
            cp.start()
        c_land[me] = c_ref[...]
        for k, frm in enumerate(peers):
            remote(c_ref, c_land.at[_dev_index(*frm)], k, frm).wait_recv()
        c_all = jnp.concatenate([c_land[b] for b in range(N_DEV)], axis=0)
        sc = c_all * _sigmoid(c_all)
        sc_ref[...] = sc
        rows = _dot3_rhs(sc, wa_ref[...])
        for b in range(N_DEV):
            part[b] = rows[b:b + 1, :]
        a_sends = [remote(part.at[_dev_index(*to)], ada_land.at[me], 7 + k, to) for k, to in enumerate(peers)]
        for cp in a_sends:
            cp.start()
        ada_land[me] = part[me]
        for k, frm in enumerate(peers):
            remote(part.at[0], ada_land.at[_dev_index(*frm)], 7 + k, frm).wait_recv()
        ada_ref[...] = ada_land[...]

        forward()
        finish()
        for cp in c_sends + a_sends:
            cp.wait_send()

    vmem = pl.BlockSpec(memory_space=pltpu.VMEM)
    return pl.pallas_call(
        body, name="gather_weights",
        in_specs=[vmem, ANY, vmem], out_specs=[ANY, vmem, vmem],
        out_shape=[jax.ShapeDtypeStruct((N_DEV,) + w_in_rows.shape, w_in_rows.dtype),
                   jax.ShapeDtypeStruct((N_DEV, D), F32), jax.ShapeDtypeStruct((N_DEV, 1, cols), F32)],
        scratch_shapes=[pltpu.VMEM((N_DEV, 1, D), F32), pltpu.VMEM((N_DEV, 1, cols), F32), pltpu.VMEM((N_DEV, 1, cols), F32),
                        pltpu.SemaphoreType.DMA((7,)), pltpu.SemaphoreType.DMA((7,)), pltpu.SemaphoreType.DMA((1,)),
                        pltpu.SemaphoreType.DMA((14,)), pltpu.SemaphoreType.DMA((14,))],
        compiler_params=_params(),
    )(c, w_in_rows, w_ada)


def _inproj_forward(x, mod, w_main, w_f, b_main, b_f, tile):
    seq = x.shape[0]
    nt = seq // tile

    def body(x_ref, mod_ref, w_ref, wf_ref, b_ref, bf_ref,
             qp_ref, kp_ref, vp_ref, f_ref, p_ref, ga_ref, gp_ref, u_ref, carry_ref):
        i = pl.program_id(0)

        @pl.when(i == 0)
        def _():
            carry_ref[...] = jnp.zeros_like(carry_ref)

        u = x_ref[...] * mod_ref[0:1, :] + mod_ref[1:2, :]
        ub = u.astype(BF16)
        u_ref[...] = ub

        f = _dot_nt(ub, wf_ref[...]) + bf_ref[...]
        f_ref[...] = f
        lane = _lanes((tile, LANE))
        log_f = jnp.where(lane < N_HEADS, jnp.minimum(f, 0.0) - jnp.log(1.0 + jnp.exp(-jnp.abs(f))), 0.0)
        row = lax.broadcasted_iota(jnp.int32, (tile, tile), 0)
        col = lax.broadcasted_iota(jnp.int32, (tile, tile), 1)
        tri = (row >= col).astype(BF16)
        cum = _dot3(tri, log_f) + carry_ref[0:1, :]
        carry_ref[0:1, :] = cum[tile - 1:tile, :]
        cq = [part.astype(F32) for part in _split3(cum)]
        ck = [part.astype(F32) for part in _split3(-cum)]

        def proj(chunk):
            cols = pl.ds(chunk * COL_CHUNK, COL_CHUNK)
            return _dot_nt(ub, w_ref[cols, :]) + b_ref[:, cols]

        def head_tiles(r):
            for pair in range(N_HEADS // 2):
                both = r[:, pair * LANE:(pair + 1) * LANE]
                yield 2 * pair, both
                yield 2 * pair + 1, pltpu.roll(both, HEAD_DIM, 1)

        for h, val in head_tiles(proj(0)):
            extra = jnp.where((lane >= L_CK) & (lane < L_CK + 3), 1.0, 0.0)
            extra = _place3(lane, L_CQ, [part[:, h:h + 1] for part in cq], extra)
            qp_ref[h] = jnp.where(lane < HEAD_DIM, val * Q_SCALE, extra).astype(BF16)
        for h, val in head_tiles(proj(1)):
            ones = ((lane >= L_CQ) & (lane < L_CQ + 3)) | ((lane >= L_LSE) & (lane < L_LSE + 3))
            extra = _place3(lane, L_CK, [part[:, h:h + 1] for part in ck], jnp.where(ones, 1.0, 0.0))
            kp_ref[h] = jnp.where(lane < HEAD_DIM, val, extra).astype(BF16)
        for h, val in head_tiles(proj(2)):
            extra = jnp.where((lane >= HEAD_DIM) & (lane < HEAD_DIM + 3), -1.0, 0.0)
            vp_ref[h] = jnp.where(lane < HEAD_DIM, val, extra).astype(BF16)
        p_ref[...] = proj(3)
        ga_ref[...] = proj(4)
        gp_ref[...] = proj(5)

    head_block = pl.BlockSpec((N_HEADS, tile, LANE), lambda i: (0, i, 0))
    tok = lambda width: pl.BlockSpec((tile, width), lambda i: (i, 0))
    whole = lambda a: pl.BlockSpec(a.shape, lambda i: (0,) * a.ndim)
    padded = jax.ShapeDtypeStruct((N_HEADS, seq, LANE), BF16)
    half = jax.ShapeDtypeStruct((seq, D_ATT), F32)
    return pl.pallas_call(
        body, name="inproj_forward", grid=(nt,),
        in_specs=[tok(D), whole(mod), whole(w_main), whole(w_f), whole(b_main), whole(b_f)],
        out_specs=[head_block, head_block, head_block, tok(LANE), tok(D_POOL), tok(D_ATT), tok(D_POOL),
                   tok(D)],
        out_shape=[padded, padded, padded, jax.ShapeDtypeStruct((seq, LANE), F32), half, half, half,
                   jax.ShapeDtypeStruct((seq, D), BF16)],
        scratch_shapes=[pltpu.VMEM((8, LANE), F32)],
        compiler_params=_params(("arbitrary",)),
    )(x, mod, w_main, w_f, b_main, b_f)


def _attention_forward(qp, kp, vp, w_out, tile):
    seq = qp.shape[1]
    nb = seq // tile
    steps = (N_HEADS // 2) * nb

    def body(q_ref, k_ref, v_ref, wo_ref, att_ref, q2t_ref, wo_all_ref, s_a, s_b, m_ref, acc_ref,
             send_sems, recv_sems, local_sem):
        step = pl.program_id(0) * nb + pl.program_id(1)
        start, forward, finish = _gather_stages(wo_ref, wo_all_ref, send_sems, recv_sems, local_sem.at[0])
        pl.when(step == 0)(start)
        pl.when(step == steps // 2)(forward)

        i = pl.program_id(1)
        sub = lax.broadcasted_iota(jnp.int32, (LANE, tile), 0)
        row = lax.broadcasted_iota(jnp.int32, (tile, tile), 0)
        col = lax.broadcasted_iota(jnp.int32, (tile, tile), 1)
        q = [q_ref[0], q_ref[1]]

        def scores(buf, kb):
            rows = pl.ds(pl.multiple_of(kb * tile, tile), tile)
            for hh in range(2):
                buf[hh] = _dot_nt(k_ref[hh, rows, :], q[hh])

        def absorb(buf, kb, masked):
            rows = pl.ds(pl.multiple_of(kb * tile, tile), tile)
            for hh in range(2):
                m = m_ref[hh, 0:1, :]
                s = buf[hh]
                if masked:
                    s = jnp.where(row <= col, s, -1e30)
                m_new = jnp.maximum(m, jnp.max(s, axis=0, keepdims=True))
                p = jnp.exp(s - m_new).astype(BF16)
                acc_ref[hh] = jnp.exp(m - m_new) * acc_ref[hh] + _dot_tn(v_ref[hh, rows, :], p)
                m_ref[hh, 0:1, :] = m_new

        def two_blocks(j, _):
            scores(s_b, 2 * j + 1)
            absorb(s_a, 2 * j, False)
            scores(s_a, 2 * j + 2)
            absorb(s_b, 2 * j + 1, False)
            return 0

        def last_block():
            absorb(s_a, i, True)

        def last_two_blocks():
            scores(s_b, i)
            absorb(s_a, i - 1, False)
            absorb(s_b, i, True)

        scores(s_a, 0)
        m_ref[...] = jnp.full(m_ref.shape, -1e30, F32)
        acc_ref[...] = jnp.zeros_like(acc_ref)
        lax.fori_loop(0, i // 2, two_blocks, 0)
        lax.cond(i % 2 == 0, last_block, last_two_blocks)
        outs = []
        for hh in range(2):
            m, acc = m_ref[hh, 0:1, :], acc_ref[hh]
            l = -acc[HEAD_DIM:HEAD_DIM + 1, :]
            outs.append((acc / l)[:HEAD_DIM, :])
            neg_lse = [part.astype(F32) for part in _split3(-(m + jnp.log(l)))]
            q2t_ref[hh] = _place3(sub, L_LSE, neg_lse, q[hh].astype(F32).T).astype(BF16)
        att_ref[...] = jnp.concatenate(outs, axis=0).T
        pl.when(step == steps - 1)(finish)

    pair = pl.BlockSpec((2, tile, LANE), lambda hp, i: (hp, i, 0))
    full = pl.BlockSpec((2, seq, LANE), lambda hp, i: (hp, 0, 0))
    return pl.pallas_call(
        body, name="attention_forward", grid=(N_HEADS // 2, nb),
        in_specs=[pair, full, full, ANY],
        out_specs=[pl.BlockSpec((tile, LANE), lambda hp, i: (i, hp)),
                   pl.BlockSpec((2, LANE, tile), lambda hp, i: (hp, 0, i)), ANY],
        out_shape=[jax.ShapeDtypeStruct((seq, D_ATT), F32),
                   jax.ShapeDtypeStruct((N_HEADS, LANE, seq), BF16),
                   jax.ShapeDtypeStruct((N_DEV,) + w_out.shape, w_out.dtype)],
        scratch_shapes=[pltpu.VMEM((2, tile, tile), F32), pltpu.VMEM((2, tile, tile), F32),
                        pltpu.VMEM((2, 8, tile), F32), pltpu.VMEM((2, LANE, tile), F32),
                        pltpu.SemaphoreType.DMA((7,)), pltpu.SemaphoreType.DMA((7,)), pltpu.SemaphoreType.DMA((1,))],
        compiler_params=_params(("arbitrary", "arbitrary")),
    )(qp, kp, vp, w_out)


def _window_sum(x, halo, window, transposed):
    tile = x.shape[0]

    def split_cat(a):
        hi = a.astype(BF16)
        return jnp.concatenate([hi, (a - hi.astype(F32)).astype(BF16)], axis=1)

    def fold(r):
        return r[:, :LANE] + r[:, LANE:]

    r = lax.broadcasted_iota(jnp.int32, (tile, tile), 0)
    c = lax.broadcasted_iota(jnp.int32, (tile, tile), 1)
    rh = lax.broadcasted_iota(jnp.int32, (HALO, HALO), 0)
    ch = lax.broadcasted_iota(jnp.int32, (HALO, HALO), 1)
    if not transposed:
        band = (c <= r) & (r - c < window)
        edge = (rh + HALO - ch) < window
    else:
        band = (r <= c) & (c - r < window)
        edge = (HALO + ch - rh) < window
    out = fold(_dot(band.astype(BF16), split_cat(x)))
    reach = fold(_dot(edge.astype(BF16), split_cat(halo)))
    if not transposed:
        return jnp.concatenate([out[:HALO] + reach, out[HALO:]], axis=0)
    return jnp.concatenate([out[:tile - HALO], out[tile - HALO:] + reach], axis=0)


def _silu_parts(g):
    sig = _sigmoid(g)
    return g * sig, sig * (1.0 + g * (1.0 - sig))


def _middle(x, tgt, att, g_att, g_pool, p, vecs, pool_vecs, w_out, w_pool, tile):
    seq = x.shape[0]
    nt = seq // tile
    halo_blocks = tile // HALO

    def body(x_ref, tgt_ref, att_ref, ga_ref, gp_ref, p_ref, ph_ref, vec_ref, pvec_ref, wo_ref, wp_ref,
             dxa_ref, do2_ref, dga_ref, dgp_ref, dpooled_ref, dwo_ref, dwp_ref, dvec_ref, dpvec_ref):
        i = pl.program_id(0)

        @pl.when(i == 0)
        def _():
            dwo_ref[...] = jnp.zeros_like(dwo_ref)
            dwp_ref[...] = jnp.zeros_like(dwp_ref)
            dvec_ref[...] = jnp.zeros_like(dvec_ref)
            dpvec_ref[...] = jnp.zeros_like(dpvec_ref)

        gate, b_out, ln_g, ln_b = (vec_ref[k:k + 1, :] for k in range(4))
        b_pool, pool_scale = pvec_ref[0:1, :], pvec_ref[1:2, :]
        x = x_ref[...]
        p = p_ref[...]
        p_halo = ph_ref[...] * jnp.where(i > 0, 1.0, 0.0)
        pos = i * tile + lax.broadcasted_iota(jnp.int32, (tile, 1), 0) + 1

        pooled, mixed = [], []
        for g, window in enumerate(POOL_WINDOWS):
            cols = slice(g * GROUP_DIM, (g + 1) * GROUP_DIM)
            wsum = _window_sum(p[:, cols], p_halo[:, cols], window, False)
            count = jnp.minimum(pos, window).astype(F32)
            pooled.append(wsum / count - p[:, cols])
            mixed.append(_dot(pooled[g].astype(BF16), wp_ref[g]) + b_pool[:, cols])
        mixed = jnp.concatenate(mixed, axis=1)
        pool = mixed * pool_scale

        att = att_ref[...]
        g_att, g_pool = ga_ref[...], gp_ref[...]
        silu_a, dsilu_a = _silu_parts(g_att)
        silu_p, dsilu_p = _silu_parts(g_pool)
        y_in = jnp.concatenate([att * silu_a, pool * silu_p], axis=1)
        y = _dot(y_in.astype(BF16), wo_ref[...]) + b_out
        h = ALPHA * x + gate * y
        mu = jnp.mean(h, axis=1, keepdims=True)
        hc = h - mu
        var = jnp.mean(hc * hc, axis=1, keepdims=True)
        rstd = lax.rsqrt(var + LN_EPS)
        yhat = hc * rstd
        diff = yhat * ln_g + ln_b - tgt_ref[...]
        loss_rows = jnp.sum(diff * diff, axis=1, keepdims=True)
        d_out = diff * (1.0 / D)

        d_yhat = d_out * ln_g
        dh = rstd * (d_yhat - jnp.mean(d_yhat, axis=1, keepdims=True)
                     - yhat * jnp.mean(d_yhat * yhat, axis=1, keepdims=True))
        dxa_ref[...] = ALPHA * dh
        dy = dh * gate
        dyb = dy.astype(BF16)
        lane = _lanes((1, D))
        loss_row = jnp.where(lane == 0, (0.5 / D) * jnp.sum(loss_rows, axis=0, keepdims=True), 0.0)
        dvec_ref[0:1, :] += jnp.sum(dh * y, axis=0, keepdims=True)
        dvec_ref[1:2, :] += jnp.sum(dy, axis=0, keepdims=True)
        dvec_ref[2:3, :] += jnp.sum(d_out * yhat, axis=0, keepdims=True)
        dvec_ref[3:4, :] += jnp.sum(d_out, axis=0, keepdims=True)
        dvec_ref[4:5, :] += loss_row

        dwo_ref[...] += _dot(y_in.T.astype(BF16), dyb)
        d_yin = _dot_nt(dyb, wo_ref[...])
        d_a, d_pl = d_yin[:, :D_ATT], d_yin[:, D_ATT:]
        d_att = d_a * silu_a
        d_att_t = d_att.T
        prod_t = (d_att * att).T
        sub = lax.broadcasted_iota(jnp.int32, (HEAD_DIM, tile), 0)
        for h in range(N_HEADS):
            rows = slice(h * HEAD_DIM, (h + 1) * HEAD_DIM)
            delta = jnp.sum(prod_t[rows], axis=0, keepdims=True)
            extra = _place3(sub, 0, [part.astype(F32) for part in _split3(delta)], 0.0)
            do2_ref[h] = jnp.concatenate([d_att_t[rows], extra], axis=0).astype(BF16)
        dga_ref[...] = d_a * att * dsilu_a
        dgp_ref[...] = d_pl * pool * dsilu_p
        d_pool = d_pl * silu_p
        d_mixed = d_pool * pool_scale
        dpvec_ref[0:1, :] += jnp.sum(d_mixed, axis=0, keepdims=True)
        dpvec_ref[1:2, :] += jnp.sum(d_pool * mixed, axis=0, keepdims=True)
        d_pooled = []
        for g in range(len(POOL_WINDOWS)):
            cols = slice(g * GROUP_DIM, (g + 1) * GROUP_DIM)
            dmb = d_mixed[:, cols].astype(BF16)
            dwp_ref[g] += _dot(pooled[g].T.astype(BF16), dmb)
            d_pooled.append(_dot_nt(dmb, wp_ref[g]))
        dpooled_ref[...] = jnp.concatenate(d_pooled, axis=1)

    tok = lambda width: pl.BlockSpec((tile, width), lambda i: (i, 0))
    whole = lambda a: pl.BlockSpec(a.shape, lambda i: (0,) * a.ndim)
    halo = pl.BlockSpec((HALO, D_POOL), lambda i: (jnp.maximum(i * halo_blocks - 1, 0), 0))
    half = jax.ShapeDtypeStruct((seq, D_ATT), F32)
    outs = [jax.ShapeDtypeStruct((seq, D), F32), jax.ShapeDtypeStruct((N_HEADS, LANE, seq), BF16), half, half, half,
            jax.ShapeDtypeStruct(w_out.shape, F32), jax.ShapeDtypeStruct(w_pool.shape, F32),
            jax.ShapeDtypeStruct(vecs.shape, F32), jax.ShapeDtypeStruct(pool_vecs.shape, F32)]
    return pl.pallas_call(
        body, name="middle", grid=(nt,),
        in_specs=[tok(D), tok(D), tok(D_ATT), tok(D_ATT), tok(D_POOL), tok(D_POOL), halo,
                  whole(vecs), whole(pool_vecs), whole(w_out), whole(w_pool)],
        out_specs=[tok(D), pl.BlockSpec((N_HEADS, LANE, tile), lambda i: (0, 0, i)),
                   tok(D_ATT), tok(D_POOL), tok(D_POOL),
                   whole(w_out), whole(w_pool), whole(vecs), whole(pool_vecs)],
        out_shape=outs,
        compiler_params=_params(("arbitrary",)),
    )(x, tgt, att, g_att, g_pool, p, p, vecs, pool_vecs, w_out, w_pool)


def _attention_backward(q2t, kp, vp, do2t, gw_out, small, tile):
    seq = kp.shape[1]
    nb = seq // tile
    last = N_HEADS // 2 - 1

    def body(qt_ref, k_ref, v_ref, dot_ref, gwo_hbm, small_hbm,
             dq_ref, dk_ref, dv_ref, dcum_ref, g_out_ref, total_ref,
             dq_acc, dk_acc, dv_acc, gwo_ref, r1, s2, r2, sm_sib, sm_chip, sm_recv, small_ref, send_sems, recv_sems):
        hp = pl.program_id(0)
        start, middle, finish = _reduce_stages(
            [gwo_ref], [g_out_ref], [r1], [s2], [r2], (small_ref, total_ref, sm_sib, sm_chip, sm_recv),
            send_sems, recv_sems)

        @pl.when(hp == 0)
        def _():
            pltpu.sync_copy(gwo_hbm, gwo_ref)
            pltpu.sync_copy(small_hbm, small_ref)
            start()

        pl.when(hp == 1)(middle)

        row = lax.broadcasted_iota(jnp.int32, (tile, tile), 0)
        col = lax.broadcasted_iota(jnp.int32, (tile, tile), 1)
        dq_acc[...] = jnp.zeros_like(dq_acc)

        def kv_block(kb, _):
            krows = pl.ds(pl.multiple_of(kb * tile, tile), tile)
            k = [k_ref[hh, krows, :] for hh in range(2)]
            v = [v_ref[hh, krows, :] for hh in range(2)]
            k_t = [k[hh].T for hh in range(2)]

            def q_block(qb, masked):
                qcols = pl.ds(pl.multiple_of(qb * tile, tile), tile)
                for hh in range(2):
                    q_t = qt_ref[hh, :, qcols]
                    do_t = dot_ref[hh, :, qcols]
                    s_t = _dot(k[hh], q_t)
                    if masked:
                        s_t = jnp.where(row <= col, s_t, -1e30)
                    p_t = jnp.exp(s_t)
                    ds_t = (p_t * _dot(v[hh], do_t)).astype(BF16)
                    dv_new = _dot_nt(do_t, p_t.astype(BF16))
                    dk_new = _dot_nt(q_t, ds_t)
                    if masked:
                        dv_acc[hh], dk_acc[hh] = dv_new, dk_new
                    else:
                        dv_acc[hh] += dv_new
                        dk_acc[hh] += dk_new
                    dq_acc[hh, :, qcols] += _dot(k_t[hh], ds_t)

            q_block(kb, True)

            def two_later_blocks(j, _):
                q_block(kb + 1 + 2 * j, False)
                q_block(kb + 2 + 2 * j, False)
                return 0

            later = nb - 1 - kb
            lax.fori_loop(0, later // 2, two_later_blocks, 0)
            pl.when(later % 2 == 1)(lambda: q_block(nb - 1, False))
            for hh in range(2):
                dk = dk_acc[hh]
                dk_ref[hh, :, krows] = dk.astype(BF16)
                dv_ref[hh, :, krows] = dv_acc[hh].astype(BF16)
                dcum_ref[hh, :, krows] = -dk[L_CK:L_CK + 1, :]
            return 0

        lax.fori_loop(0, nb, kv_block, 0)
        for hh in range(2):
            dq = dq_acc[hh]
            dcum_ref[hh] += dq[L_CQ:L_CQ + 1, :]
            dq_ref[hh] = (dq * Q_SCALE).astype(BF16)
        pl.when(hp == last)(finish)

    pair = pl.BlockSpec((2, seq, LANE), lambda hp: (hp, 0, 0))
    pair_t = pl.BlockSpec((2, LANE, seq), lambda hp: (hp, 0, 0))
    whole = lambda shape: pl.BlockSpec(shape, lambda hp: (0,) * len(shape))
    grad = jax.ShapeDtypeStruct((N_HEADS, LANE, seq), BF16)
    return pl.pallas_call(
        body, name="attention_backward", grid=(N_HEADS // 2,),
        in_specs=[pair_t, pair, pair, pair_t, ANY, ANY],
        out_specs=[pair_t, pair_t, pair_t, pl.BlockSpec((2, 1, seq), lambda hp: (hp, 0, 0)),
                   whole(gw_out.shape[1:]), whole(small.shape)],
        out_shape=[grad, grad, grad, jax.ShapeDtypeStruct((N_HEADS, 1, seq), F32),
                   jax.ShapeDtypeStruct(gw_out.shape[1:], F32), jax.ShapeDtypeStruct(small.shape, F32)],
        scratch_shapes=[pltpu.VMEM((2, LANE, seq), F32), pltpu.VMEM((2, LANE, tile), F32),
                        pltpu.VMEM((2, LANE, tile), F32), pltpu.VMEM(gw_out.shape, BF16)]
        + _reduce_scratch(gw_out, small)
        + [pltpu.VMEM(small.shape, F32),
           pltpu.SemaphoreType.DMA((N_REDUCE_SEMS + N_SMALL_SEMS,)), pltpu.SemaphoreType.DMA((N_REDUCE_SEMS + N_SMALL_SEMS,))],
        compiler_params=_params(("arbitrary",)),
    )(q2t, kp, vp, do2t, gw_out, small)


def _inproj_backward(dqp, dkp, dvp, d_cum, f, d_pooled, d_ga, d_gp, x, dxa, u, mod, w_main, w_f, tile):
    seq = x.shape[0]
    nt = seq // tile
    halo_blocks = tile // HALO

    def body(dq_ref, dk_ref, dv_ref, dcum_ref, f_ref, dpo_ref, dph_ref, dga_ref, dgp_ref, x_ref, dxa_ref, u_ref,
             mod_ref, w_ref, wf_ref,
             dx_ref, dproj_ref, dwf_ref, db_ref, dbf_ref, dmod_ref, carry_ref):
        step = pl.program_id(0)
        i = nt - 1 - step

        @pl.when(step == 0)
        def _():
            carry_ref[...] = jnp.zeros_like(carry_ref)
            dwf_ref[...] = jnp.zeros_like(dwf_ref)
            db_ref[...] = jnp.zeros_like(db_ref)
            dbf_ref[...] = jnp.zeros_like(dbf_ref)
            dmod_ref[...] = jnp.zeros_like(dmod_ref)

        ones = jnp.ones((8, tile), BF16)

        def emit(chunk, val):
            cols = pl.ds(chunk * COL_CHUNK, COL_CHUNK)
            db_ref[0:1, cols] += jnp.sum(val, axis=0, keepdims=True)
            vb = val.astype(BF16)
            dproj_ref[:, pl.ds((chunk - 3) * COL_CHUNK, COL_CHUNK)] = vb
            return _dot(vb, w_ref[cols, :])

        d_u = jnp.zeros((tile, D), F32)
        for chunk, ref in enumerate((dq_ref, dk_ref, dv_ref)):
            cols = pl.ds(chunk * COL_CHUNK, COL_CHUNK)
            val_t = ref[:, 0:HEAD_DIM, :].reshape(COL_CHUNK, tile)
            db_ref[:, cols] += _dot_nt(ones, val_t)
            d_u += _dot_tn(val_t, w_ref[cols, :])

        d_pooled = dpo_ref[...]
        d_halo = dph_ref[...] * jnp.where(i < nt - 1, 1.0, 0.0)
        pos = i * tile + lax.broadcasted_iota(jnp.int32, (tile, 1), 0) + 1
        d_p = []
        for g, window in enumerate(POOL_WINDOWS):
            cols = slice(g * GROUP_DIM, (g + 1) * GROUP_DIM)
            scaled = d_pooled[:, cols] / jnp.minimum(pos, window).astype(F32)
            d_p.append(_window_sum(scaled, d_halo[:, cols] * (1.0 / window), window, True) - d_pooled[:, cols])
        d_u += emit(3, jnp.concatenate(d_p, axis=1))
        d_u += emit(4, dga_ref[...])
        d_u += emit(5, dgp_ref[...])

        row = lax.broadcasted_iota(jnp.int32, (tile, tile), 0)
        col = lax.broadcasted_iota(jnp.int32, (tile, tile), 1)
        later = (row >= col).astype(BF16)
        d_logf = sum(_dot(part, later) for part in _split3(dcum_ref[:, 0, :])) + carry_ref[:, 0:1]
        carry_ref[:, 0:1] = d_logf[:, 0:1]
        d_f = d_logf * _sigmoid(-f_ref[...].T[0:N_HEADS, :])
        d_f = jnp.concatenate([d_f, jnp.zeros((LANE - N_HEADS, tile), F32)], axis=0)
        dbf_ref[...] += sum(_dot_nt(ones, part) for part in _split3(d_f))
        d_fb = d_f.astype(BF16)
        d_u += _dot_tn(d_fb, wf_ref[...])
        dwf_ref[...] += _dot(d_fb, u_ref[...])

        x = x_ref[...]
        dx_ref[...] = dxa_ref[...] + d_u * mod_ref[0:1, :]
        dmod_ref[0:1, :] += jnp.sum(d_u * x, axis=0, keepdims=True)
        dmod_ref[1:2, :] += jnp.sum(d_u, axis=0, keepdims=True)

    rev = lambda step: nt - 1 - step
    tok = lambda width: pl.BlockSpec((tile, width), lambda s: (rev(s), 0))
    head_block = pl.BlockSpec((N_HEADS, LANE, tile), lambda s: (0, 0, rev(s)))
    whole = lambda a: pl.BlockSpec(a.shape, lambda s: (0,) * a.ndim)
    halo = pl.BlockSpec((HALO, D_POOL), lambda s: (jnp.minimum((rev(s) + 1) * halo_blocks, seq // HALO - 1), 0))
    small = lambda width: jax.ShapeDtypeStruct((8, width), F32)
    n_rest = N_MAIN - OFF_P
    return pl.pallas_call(
        body, name="inproj_backward", grid=(nt,),
        in_specs=[head_block, head_block, head_block, pl.BlockSpec((N_HEADS, 1, tile), lambda s: (0, 0, rev(s))),
                  tok(LANE), tok(D_POOL), halo, tok(D_ATT), tok(D_POOL),
                  tok(D), tok(D), tok(D),
                  whole(mod), whole(w_main), whole(w_f)],
        out_specs=[tok(D), tok(n_rest), pl.BlockSpec((LANE, D), lambda s: (0, 0)),
                   pl.BlockSpec((8, N_MAIN), lambda s: (0, 0)), pl.BlockSpec((8, LANE), lambda s: (0, 0)),
                   pl.BlockSpec((8, D), lambda s: (0, 0))],
        out_shape=[jax.ShapeDtypeStruct((seq, D), F32), jax.ShapeDtypeStruct((seq, n_rest), BF16),
                   jax.ShapeDtypeStruct((LANE, D), F32), small(N_MAIN), small(LANE), small(D)],
        scratch_shapes=[pltpu.VMEM((8, LANE), F32)],
        compiler_params=_params(("arbitrary",)),
    )(dqp, dkp, dvp, d_cum, f, d_pooled, d_pooled, d_ga, d_gp, x, dxa, u, mod, w_main, w_f)


def _weight_grad(dproj, u, k_tile):
    seq, n_cols = dproj.shape
    nk = seq // k_tile

    def body(dp_ref, u_ref, out_ref):
        @pl.when(pl.program_id(1) == 0)
        def _():
            out_ref[...] = jnp.zeros_like(out_ref)

        out_ref[...] += _dot_tn(dp_ref[...], u_ref[...])

    return pl.pallas_call(
        body, name="weight_grad", grid=(n_cols // COL_CHUNK, nk),
        in_specs=[pl.BlockSpec((k_tile, COL_CHUNK), lambda n, k: (k, n)),
                  pl.BlockSpec((k_tile, D), lambda n, k: (k, 0))],
        out_specs=pl.BlockSpec((COL_CHUNK, D), lambda n, k: (n, 0)),
        out_shape=jax.ShapeDtypeStruct((n_cols, D), F32),
        compiler_params=_params(("arbitrary", "arbitrary")),
    )(dproj, u)


def _weight_grad_heads(grad_t, u, k_tile, name):
    seq = u.shape[0]
    nk = seq // k_tile

    def body(g_ref, u_ref, out_ref):
        @pl.when(pl.program_id(0) == 0)
        def _():
            out_ref[...] = jnp.zeros_like(out_ref)

        out_ref[...] += _dot(g_ref[...].reshape(N_HEADS * HEAD_DIM, k_tile), u_ref[...])

    return pl.pallas_call(
        body, name=name, grid=(nk,),
        in_specs=[pl.BlockSpec((N_HEADS, HEAD_DIM, k_tile), lambda k: (0, 0, k)),
                  pl.BlockSpec((k_tile, D), lambda k: (k, 0))],
        out_specs=pl.BlockSpec((N_HEADS * HEAD_DIM, D), lambda k: (0, 0)),
        out_shape=jax.ShapeDtypeStruct((N_HEADS * HEAD_DIM, D), F32),
        compiler_params=_params(("arbitrary",)),
    )(grad_t, u)


def _adamw(w, g, m, v):
    m = ADAM_B1 * m + (1.0 - ADAM_B1) * g
    v = ADAM_B2 * v + (1.0 - ADAM_B2) * (g * g)
    m_hat = m / (1.0 - ADAM_B1 ** ADAM_STEP)
    v_hat = v / (1.0 - ADAM_B2 ** ADAM_STEP)
    delta = -ADAM_LR * (m_hat / (jnp.sqrt(v_hat) + ADAM_EPS) + ADAM_WD * w)
    return delta, m, v


def _adamw_call(g, w, m, v, lead_tile, name):
    nr = w.shape[0] // lead_tile

    def body(gi_ref, w_ref, m_ref, v_ref, g_ref, d_ref, nm_ref, nv_ref):
        g = gi_ref[...]
        g_ref[...] = g
        d_ref[...], nm_ref[...], nv_ref[...] = _adamw(w_ref[...], g, m_ref[...], v_ref[...])

    blk = pl.BlockSpec((lead_tile,) + w.shape[1:], lambda r: (r,) + (0,) * (w.ndim - 1))
    shape = jax.ShapeDtypeStruct(w.shape, F32)
    return pl.pallas_call(
        body, name=name, grid=(nr,),
        in_specs=[blk, blk, blk, blk], out_specs=[blk, blk, blk, blk],
        out_shape=[shape, shape, shape, shape],
        compiler_params=_params(("arbitrary",)),
    )(g, w, m, v)


def _sum_adamw(parts, w, m, v, row_tile, name):
    rows, cols = w.shape
    nr = rows // row_tile

    def body(parts_ref, w_ref, m_ref, v_ref, g_ref, d_ref, nm_ref, nv_ref):
        g = parts_ref[0]
        for k in range(1, N_DEV):
            g = g + parts_ref[k]
        g_ref[...] = g
        d_ref[...], nm_ref[...], nv_ref[...] = _adamw(w_ref[...], g, m_ref[...], v_ref[...])

    blk = pl.BlockSpec((row_tile, cols), lambda r: (r, 0))
    shape = jax.ShapeDtypeStruct(w.shape, F32)
    return pl.pallas_call(
        body, name=name, grid=(nr,),
        in_specs=[pl.BlockSpec((N_DEV, row_tile, cols), lambda r: (0, r, 0)), blk, blk, blk],
        out_specs=[blk, blk, blk, blk],
        out_shape=[shape, shape, shape, shape],
        compiler_params=_params(("arbitrary",)),
    )(parts, w, m, v)


def _ada_adamw(sc_t, d_ada, w, m, v):
    def body(sc_ref, d_ref, w_ref, m_ref, v_ref, g_ref, dl_ref, nm_ref, nv_ref):
        g = sc_ref[:, 0:1] * d_ref[0:1, :]
        for b in range(1, N_DEV):
            g = g + sc_ref[:, b:b + 1] * d_ref[b:b + 1, :]
        g_ref[...] = g
        dl_ref[...], nm_ref[...], nv_ref[...] = _adamw(w_ref[...], g, m_ref[...], v_ref[...])

    shape = jax.ShapeDtypeStruct(w.shape, F32)
    return pl.pallas_call(
        body, name="ada_adamw", out_shape=[shape, shape, shape, shape], compiler_params=_params(),
    )(sc_t, d_ada, w, m, v)


F_LO, F_HI = 3 * D_ATT, 3 * D_ATT + N_HEADS


def _split_forget(a, axis):
    idx = lambda lo, hi: tuple(slice(lo, hi) if d == axis else slice(None) for d in range(a.ndim))
    pad = [(0, LANE - N_HEADS) if d == axis else (0, 0) for d in range(a.ndim)]
    return jnp.concatenate([a[idx(0, F_LO)], a[idx(F_HI, D_IN)]], axis=axis), jnp.pad(a[idx(F_LO, F_HI)], pad)


def _join_forget(main, f, axis):
    idx = lambda lo, hi: tuple(slice(lo, hi) if d == axis else slice(None) for d in range(main.ndim))
    return jnp.concatenate([main[idx(0, F_LO)], f[idx(0, N_HEADS)], main[idx(F_LO, N_MAIN)]], axis=axis)


_EARLY = ((("w_pool", 65536), ("b_pool", 512), ("pool_scale", 512), ("b_out", 1024), ("ln_g", 1024), ("ln_b", 1024),
           ("loss", 128)), 552)
_LATE = ((("b_in", 3200),), 32)


def _pack_small(parts, layout):
    spec, rows = layout
    flat = []
    for name, size in spec:
        a = parts[name].reshape(-1)
        flat.append(jnp.pad(a, (0, size - a.shape[0])))
    flat = jnp.concatenate(flat)
    flat = jnp.pad(flat, (0, rows * LANE - flat.shape[0]))
    return flat.reshape(rows, LANE)


def _unpack_small(packed, shapes, layout):
    flat = packed.reshape(-1)
    out, off = {}, 0
    for name, size in layout[0]:
        n = 1
        for s in shapes[name]:
            n *= s
        out[name] = flat[off:off + n].reshape(shapes[name])
        off += size
    return out


def kernel(x, c, w_ada, b_ada, w_in, b_in, w_pool_mix, b_pool_mix, pool_scale, w_out, b_out, ln_g, ln_b, loss_target, m_w_ada, m_b_ada, m_w_in, m_b_in, m_w_pool_mix, m_b_pool_mix, m_pool_scale, m_w_out, m_b_out, m_ln_g, m_ln_b, v_w_ada, v_b_ada, v_w_in, v_b_in, v_w_pool_mix, v_b_pool_mix, v_pool_scale, v_w_out, v_b_out, v_ln_g, v_ln_b):
    seq = x.shape[1]
    tile = min(256, seq)
    attn_tile = min(512, max(128, seq // 4))
    me = _dev_index(*_mesh_pos())
    x2, tgt = x[0], loss_target[0]

    rows_in = D_IN // N_DEV
    w_in_g, sc_all, ada_mine = _gather_and_ada(c, w_in[0].T.astype(BF16), w_ada[0])
    ada = ada_mine.reshape(1, D_ADA) + b_ada
    shift, scale, gate = ada[:, 0:D], ada[:, D:2 * D], ada[:, 2 * D:]
    mod = jnp.concatenate([1.0 + scale, shift, jnp.zeros((6, D), F32)], axis=0)

    w_main, w_f = _split_forget(w_in_g.reshape(D_IN, D), 0)
    b_main, b_f = _split_forget(b_in, 1)

    qp, kp, vp, f, p, g_att, g_pool, u = _inproj_forward(x2, mod, w_main, w_f, b_main, b_f, tile)
    att, q2t, w_out_g = _attention_forward(qp, kp, vp, w_out[0].astype(BF16), attn_tile)

    vecs = jnp.concatenate([gate, b_out, ln_g, ln_b, jnp.zeros((4, D), F32)], axis=0)
    pool_vecs = jnp.concatenate([b_pool_mix.reshape(1, D_POOL), pool_scale, jnp.zeros((6, D_POOL), F32)], axis=0)
    dxa, do2, d_ga, d_gp, d_pooled, dw_out, dw_pool, dvec, dpvec = _middle(
        x2, tgt, att, g_att, g_pool, p, vecs, pool_vecs, w_out_g.reshape(D, D), w_pool_mix[0].astype(BF16), tile)

    gw_out = dw_out.reshape(N_DEV, D // N_DEV, D).astype(BF16)
    early = _pack_small({"w_pool": dw_pool, "b_pool": dpvec[0:1], "pool_scale": dpvec[1:2], "b_out": dvec[1:2],
                         "ln_g": dvec[2:3], "ln_b": dvec[3:4], "loss": dvec[4:5, 0:LANE]}, _EARLY)
    dqp, dkp, dvp, d_cum, g_out, early_sum = _attention_backward(q2t, kp, vp, do2, gw_out, early, attn_tile)
    dx, dproj, dw_f, db_main, db_f, dmod = _inproj_backward(
        dqp, dkp, dvp, d_cum, f, d_pooled, d_ga, d_gp, x2, dxa, u, mod, w_main, w_f, tile)
    k_tile = min(1024, seq)
    dw_q, dw_k, dw_v = (_weight_grad_heads(g, u, k_tile, "weight_grad_" + n)
                        for g, n in ((dqp, "q"), (dkp, "k"), (dvp, "v")))
    dw_rest = _weight_grad(dproj, u, k_tile)

    dw_main = jnp.concatenate([dw_q, dw_k, dw_v, dw_rest], axis=0)
    gw_in = _join_forget(dw_main, dw_f, 0).reshape(N_DEV, rows_in, D).astype(BF16)
    d_ada = jnp.concatenate([dmod[1:2], dmod[0:1], dvec[0:1]], axis=1)
    late = _pack_small({"b_in": _join_forget(db_main[0:1], db_f[0:1], 1)}, _LATE)
    g_in_rows, late_sum, d_ada_all = _reduce_grads(gw_in, late, d_ada.reshape(D_ADA // LANE, LANE))

    def rows3(a):
        return a[0].T.reshape(rows_in, D // LANE, LANE)

    outs_in = _adamw_call(g_in_rows.reshape(rows_in, D // LANE, LANE), rows3(w_in), rows3(m_w_in), rows3(v_w_in),
                          rows_in // 5, "adamw_w_in")
    g_w_in, d_w_in, nm_w_in, nv_w_in = (a.reshape(rows_in, D).T for a in outs_in)
    g_w_out, d_w_out, nm_w_out, nv_w_out = _adamw_call(g_out, w_out[0], m_w_out[0], v_w_out[0], D // N_DEV, "adamw_w_out")

    def small_adamw(grad_sum, weights, first, second, layout, name):
        packed = _adamw_call(grad_sum, _pack_small(weights, layout), _pack_small(first, layout),
                             _pack_small(second, layout), layout[1], name)
        shapes = {k: a.shape for k, a in weights.items()}
        return [_unpack_small(a, shapes, layout) for a in packed]

    zero = jnp.zeros((1,), F32)
    smalls = small_adamw(
        early_sum,
        {"w_pool": w_pool_mix, "b_pool": b_pool_mix, "pool_scale": pool_scale, "b_out": b_out, "ln_g": ln_g,
         "ln_b": ln_b, "loss": zero},
        {"w_pool": m_w_pool_mix, "b_pool": m_b_pool_mix, "pool_scale": m_pool_scale, "b_out": m_b_out,
         "ln_g": m_ln_g, "ln_b": m_ln_b, "loss": zero},
        {"w_pool": v_w_pool_mix, "b_pool": v_b_pool_mix, "pool_scale": v_pool_scale, "b_out": v_b_out,
         "ln_g": v_ln_g, "ln_b": v_ln_b, "loss": zero}, _EARLY, "adamw_small")
    b_ins = small_adamw(late_sum, {"b_in": b_in}, {"b_in": m_b_in}, {"b_in": v_b_in}, _LATE, "adamw_b_in")
    g_s, d_s, nm_s, nv_s = ({**a, **b} for a, b in zip(smalls, b_ins))
    loss = g_s["loss"][0]

    ada_rows = D_ADA // LANE
    b_ada_outs = _sum_adamw(d_ada_all, b_ada.reshape(ada_rows, LANE), m_b_ada.reshape(ada_rows, LANE),
                            v_b_ada.reshape(ada_rows, LANE), ada_rows, "adamw_b_ada")
    g_b_ada, d_b_ada, nm_b_ada, nv_b_ada = (a.reshape(1, D_ADA) for a in b_ada_outs)
    d_ada_local = lax.dynamic_slice_in_dim(d_ada_all.reshape(N_DEV, D_ADA), me * (D_ADA // N_DEV), D_ADA // N_DEV, axis=1)
    g_w_ada, d_w_ada, nm_w_ada, nv_w_ada = _ada_adamw(sc_all.T, d_ada_local, w_ada[0], m_w_ada[0], v_w_ada[0])

    def ordered(w_ada_, b_ada_, w_in_, w_out_, s):
        return (w_ada_[None], b_ada_, w_in_[None], s["b_in"], s["w_pool"], s["b_pool"], s["pool_scale"],
                w_out_[None], s["b_out"], s["ln_g"], s["ln_b"])

    return (loss, dx[None],
            *ordered(g_w_ada, g_b_ada, g_w_in, g_w_out, g_s),
            *ordered(d_w_ada, d_b_ada, d_w_in, d_w_out, d_s),
            *ordered(nm_w_ada, nm_b_ada, nm_w_in, nm_w_out, nm_s),
            *ordered(nv_w_ada, nv_b_ada, nv_w_in, nv_w_out, nv_s))
```

```python
import jax
import jax.numpy as jnp
from jax import lax
from jax.experimental import pallas as pl
from jax.experimental.pallas import tpu as pltpu

F32 = jnp.float32
BF16 = jnp.bfloat16

N_DEV = 8
D = 1024
N_HEADS = 8
HEAD_DIM = 64
D_ATT = 512
D_POOL = 512
POOL_WINDOWS = (2, 4, 8, 16)
GROUP_DIM = 128
HALO = 16
LANE = 128
D_IN = 3080
D_ADA = 3072
N_MAIN = 3072
OFF_P = 1536
COL_CHUNK = 512
Q_SCALE = 0.125
LN_EPS = 1e-5
ALPHA = 2.0 ** 0.25
L_CQ, L_CK, L_LSE = 64, 67, 70

ADAM_LR, ADAM_B1, ADAM_B2, ADAM_EPS, ADAM_WD, ADAM_STEP = 0.001, 0.9, 0.999, 1e-08, 0.01, 10
VMEM_LIMIT = 56 * 1024 * 1024

MESH = pl.DeviceIdType.MESH
ANY = pl.BlockSpec(memory_space=pl.ANY)


def _params(sem=None, vmem=VMEM_LIMIT):
    return pltpu.CompilerParams(dimension_semantics=sem, vmem_limit_bytes=vmem)


def _split3(a):
    hi = a.astype(BF16)
    r = a - hi.astype(F32)
    mid = r.astype(BF16)
    lo = (r - mid.astype(F32)).astype(BF16)
    return hi, mid, lo


def _dot(a, b):
    return jnp.dot(a, b, preferred_element_type=F32)


def _dot_nt(a, b):
    return lax.dot_general(a, b, (((1,), (1,)), ((), ())), preferred_element_type=F32)


def _dot_tn(a, b):
    return lax.dot_general(a, b, (((0,), (0,)), ((), ())), preferred_element_type=F32)


def _dot3(m01, a):
    hi, mid, lo = _split3(a)
    return _dot(m01, hi) + _dot(m01, mid) + _dot(m01, lo)


def _sigmoid(z):
    return 1.0 / (1.0 + jnp.exp(-z))


def _lanes(shape):
    return lax.broadcasted_iota(jnp.int32, shape, len(shape) - 1)


def _place3(lane, base, parts, other):
    out = other
    for j in range(3):
        out = jnp.where(lane == base + j, parts[j], out)
    return out


def _mesh_pos():
    return lax.axis_index("x"), lax.axis_index("y"), lax.axis_index("c")


def _dev_index(px, py, pc):
    return 4 * px + 2 * py + pc


def _gather_stages(src_ref, out_ref, send_sems, recv_sems, local_sem):
    x, y, c = _mesh_pos()
    me, sibling = (x, y, c), (x, y, 1 - c)
    chips = [(1 - x, y), (x, 1 - y), (1 - x, 1 - y)]

    def copy(k, block, to, src=None):
        slot = out_ref.at[_dev_index(*block)]
        return pltpu.make_async_remote_copy(
            src_ref=slot if src is None else src, dst_ref=slot, send_sem=send_sems.at[k], recv_sem=recv_sems.at[k],
            device_id=to, device_id_type=MESH)

    mine = pltpu.make_async_copy(src_ref, out_ref.at[_dev_index(*me)], local_sem)
    first = [copy(0, me, sibling, src=src_ref)] + [copy(1 + j, me, (*chip, c), src=src_ref) for j, chip in enumerate(chips)]
    passed = [copy(4 + j, (*chip, c), sibling) for j, chip in enumerate(chips)]

    def start():
        mine.start()
        for cp in first:
            cp.start()

    def forward():
        for j, chip in enumerate(chips):
            copy(1 + j, (*chip, c), me).wait_recv()
            passed[j].start()

    def finish():
        copy(0, sibling, me).wait_recv()
        for j, chip in enumerate(chips):
            copy(4 + j, (*chip, 1 - c), me).wait_recv()
        for cp in first + passed:
            cp.wait_send()
        mine.wait()

    return start, forward, finish


N_REDUCE_SEMS = 7
N_SMALL_SEMS = 4 + 7


def _reduce_stages(ins, gs, r1, s2, r2, small, send_sems, recv_sems, rows=None):
    n = len(ins)
    x, y, c = _mesh_pos()
    me = _dev_index(x, y, c)
    sibling = (x, y, 1 - c)
    chips = [(x, y), (1 - x, y), (x, 1 - y), (1 - x, 1 - y)]
    peers = []
    for p in range(1, N_DEV):
        px, py, pc = (p >> 2) & 1, (p >> 1) & 1, p & 1
        peers.append((1 - x if px else x, 1 - y if py else y, 1 - c if pc else c))
    base_small = N_REDUCE_SEMS * n

    def remote(src, dst, k, to):
        return pltpu.make_async_remote_copy(src_ref=src, dst_ref=dst, send_sem=send_sems.at[k],
                                            recv_sem=recv_sems.at[k], device_id=to, device_id_type=MESH)

    def level1(a, q):
        return remote(ins[a].at[_dev_index(*chips[q], 1 - c)], r1[a].at[q], N_REDUCE_SEMS * a + q, sibling)

    def level2(a, j):
        return remote(s2[a].at[j], r2[a].at[j], N_REDUCE_SEMS * a + 4 + j, (*chips[j + 1], c))

    if small is not None:
        small_ref, total_ref, sm_sib, sm_chip, sm_recv = small
        to_sibling = remote(small_ref, sm_sib, base_small, sibling)
        to_chips = [remote(sm_chip, sm_recv.at[j], base_small + 1 + j, (*chips[j + 1], c)) for j in range(3)]
    if rows is not None:
        rows_ref, land_ref, all_ref = rows
        row_sends = [remote(rows_ref, land_ref.at[me], base_small + 4 + k, to) for k, to in enumerate(peers)]

    def start():
        for a in range(n):
            for q in range(4):
                level1(a, q).start()
        if small is not None:
            to_sibling.start()
        if rows is not None:
            for cp in row_sends:
                cp.start()
            land_ref[me] = rows_ref[...]

    def middle():
        for a in range(n):
            for q in (1, 2, 3, 0):
                level1(a, q).wait_recv()
                pair = ins[a][_dev_index(*chips[q], c)].astype(F32) + r1[a][q].astype(F32)
                if q == 0:
                    gs[a][...] = pair
                else:
                    s2[a][q - 1] = pair.astype(BF16)
                    level2(a, q - 1).start()
        if small is not None:
            to_sibling.wait_recv()
            sm_chip[...] = small_ref[...] + sm_sib[...]
            for cp in to_chips:
                cp.start()

    def finish():
        for a in range(n):
            own = gs[a][...]
            for j in range(3):
                level2(a, j).wait_recv()
                own = own + r2[a][j].astype(F32)
            gs[a][...] = own
            for q in range(4):
                level1(a, q).wait_send()
            for j in range(3):
                level2(a, j).wait_send()
        if small is not None:
            for cp in to_chips:
                cp.wait_recv()
            total = None
            for ax in range(2):
                for ay in range(2):
                    dx, dy = x != ax, y != ay
                    term = jnp.where(dx, jnp.where(dy, sm_recv[2], sm_recv[0]), jnp.where(dy, sm_recv[1], sm_chip[...]))
                    total = term if total is None else total + term
            total_ref[...] = total
            for cp in [to_sibling] + to_chips:
                cp.wait_send()
        if rows is not None:
            for k, frm in enumerate(peers):
                remote(rows_ref, land_ref.at[_dev_index(*frm)], base_small + 4 + k, frm).wait_recv()
            all_ref[...] = land_ref[...]
            for cp in row_sends:
                cp.wait_send()

    return start, middle, finish


def _reduce_scratch(shard, small, rows=None):
    out = [pltpu.VMEM((lead,) + shard.shape[1:], BF16) for lead in (4, 3, 3)]
    out += [pltpu.VMEM(small.shape, F32), pltpu.VMEM(small.shape, F32), pltpu.VMEM((3,) + small.shape, F32)]
    if rows is not None:
        out.append(pltpu.VMEM((N_DEV,) + rows.shape, F32))
    return out


def _reduce_grads(gw_in, small, rows):
    def body(in_ref, small_ref, rows_ref, g_ref, total_ref, rows_all_ref,
             r1, s2, r2, sm_sib, sm_chip, sm_recv, rows_land, send_sems, recv_sems):
        start, middle, finish = _reduce_stages(
            [in_ref], [g_ref], [r1], [s2], [r2], (small_ref, total_ref, sm_sib, sm_chip, sm_recv),
            send_sems, recv_sems, rows=(rows_ref, rows_land, rows_all_ref))
        start()
        middle()
        finish()

    return pl.pallas_call(
        body, name="reduce_grads",
        out_shape=[jax.ShapeDtypeStruct(gw_in.shape[1:], F32), jax.ShapeDtypeStruct(small.shape, F32),
                   jax.ShapeDtypeStruct((N_DEV,) + rows.shape, F32)],
        scratch_shapes=_reduce_scratch(gw_in, small, rows)
        + [pltpu.SemaphoreType.DMA((N_REDUCE_SEMS + N_SMALL_SEMS,))] * 2,
        compiler_params=_params(),
    )(gw_in, small, rows)


def _dot3_rhs(a, b):
    a0, a1, a2 = _split3(a)
    b0, b1, b2 = _split3(b)
    return (_dot(a0, b0) + (_dot(a0, b1) + _dot(a1, b0))
            + (_dot(a0, b2) + _dot(a1, b1) + _dot(a2, b0)))


def _gather_and_ada(c, w_in_rows, w_ada):
    cols = w_ada.shape[1]

    def body(c_ref, w_ref, wa_ref, w_all_ref, sc_ref, ada_ref,
             c_land, part, ada_land, send_sems, recv_sems, local_sem, x_send, x_recv):
        x, y, cc = _mesh_pos()
        me = _dev_index(x, y, cc)
        peers = []
        for p in range(1, N_DEV):
            px, py, pc = (p >> 2) & 1, (p >> 1) & 1, p & 1
            peers.append((1 - x if px else x, 1 - y if py else y, 1 - cc if pc else cc))

        def remote(src, dst, k, to):
            return pltpu.make_async_remote_copy(src_ref=src, dst_ref=dst, send_sem=x_send.at[k], recv_sem=x_recv.at[k],
                                                device_id=to, device_id_type=MESH)

        c_sends = [remote(c_ref, c_land.at[me], k, to) for k, to in enumerate(peers)]
        for cp in c_sends:
            cp.start()
        start, forward, finish = _gather_stages(w_ref, w_all_ref, send_sems, recv_sems, local_sem.at[0])
        start()
        c_land[me] = c_ref[...]
        for k, frm in enumerate(peers):
            remote(c_ref, c_land.at[_dev_index(*frm)], k, frm).wait_recv()
        c_all = jnp.concatenate([c_land[b] for b in range(N_DEV)], axis=0)
        sc = c_all * _sigmoid(c_all)
        sc_ref[...] = sc
        rows = _dot3_rhs(sc, wa_ref[...])
        for b in range(N_DEV):
            part[b] = rows[b:b + 1, :]
        a_sends = [remote(part.at[_dev_index(*to)], ada_land.at[me], 7 + k, to) for k, to in enumerate(peers)]
        for cp in a_sends:
            cp.start()
        ada_land[me] = part[me]
        for k, frm in enumerate(peers):
            remote(part.at[0], ada_land.at[_dev_index(*frm)], 7 + k, frm).wait_recv()
        ada_ref[...] = ada_land[...]

        forward()
        finish()
        for cp in c_sends + a_sends:
            cp.wait_send()

    vmem = pl.BlockSpec(memory_space=pltpu.VMEM)
    return pl.pallas_call(
        body, name="gather_weights",
        in_specs=[vmem, ANY, vmem], out_specs=[ANY, vmem, vmem],
        out_shape=[jax.ShapeDtypeStruct((N_DEV,) + w_in_rows.shape, w_in_rows.dtype),
                   jax.ShapeDtypeStruct((N_DEV, D), F32), jax.ShapeDtypeStruct((N_DEV, 1, cols), F32)],
        scratch_shapes=[pltpu.VMEM((N_DEV, 1, D), F32), pltpu.VMEM((N_DEV, 1, cols), F32), pltpu.VMEM((N_DEV, 1, cols), F32),
                        pltpu.SemaphoreType.DMA((7,)), pltpu.SemaphoreType.DMA((7,)), pltpu.SemaphoreType.DMA((1,)),
                        pltpu.SemaphoreType.DMA((14,)), pltpu.SemaphoreType.DMA((14,))],
        compiler_params=_params(),
    )(c, w_in_rows, w_ada)


def _inproj_forward(x, mod, w_main, w_f, b_main, b_f, tile):
    seq = x.shape[0]
    nt = seq // tile

    def body(x_ref, mod_ref, w_ref, wf_ref, b_ref, bf_ref,
             qp_ref, kp_ref, vp_ref, f_ref, p_ref, ga_ref, gp_ref, u_ref, carry_ref):
        i = pl.program_id(0)

        @pl.when(i == 0)
        def _():
            carry_ref[...] = jnp.zeros_like(carry_ref)

        u = x_ref[...] * mod_ref[0:1, :] + mod_ref[1:2, :]
        ub = u.astype(BF16)
        u_ref[...] = ub

        f = _dot_nt(ub, wf_ref[...]) + bf_ref[...]
        f_ref[...] = f
        lane = _lanes((tile, LANE))
        log_f = jnp.where(lane < N_HEADS, jnp.minimum(f, 0.0) - jnp.log(1.0 + jnp.exp(-jnp.abs(f))), 0.0)
        row = lax.broadcasted_iota(jnp.int32, (tile, tile), 0)
        col = lax.broadcasted_iota(jnp.int32, (tile, tile), 1)
        tri = (row >= col).astype(BF16)
        cum = _dot3(tri, log_f) + carry_ref[0:1, :]
        carry_ref[0:1, :] = cum[tile - 1:tile, :]
        cq = [part.astype(F32) for part in _split3(cum)]
        ck = [part.astype(F32) for part in _split3(-cum)]

        def proj(chunk):
            cols = pl.ds(chunk * COL_CHUNK, COL_CHUNK)
            return _dot_nt(ub, w_ref[cols, :]) + b_ref[:, cols]

        def head_tiles(r):
            for pair in range(N_HEADS // 2):
                both = r[:, pair * LANE:(pair + 1) * LANE]
                yield 2 * pair, both
                yield 2 * pair + 1, pltpu.roll(both, HEAD_DIM, 1)

        for h, val in head_tiles(proj(0)):
            extra = jnp.where((lane >= L_CK) & (lane < L_CK + 3), 1.0, 0.0)
            extra = _place3(lane, L_CQ, [part[:, h:h + 1] for part in cq], extra)
            qp_ref[h] = jnp.where(lane < HEAD_DIM, val * Q_SCALE, extra).astype(BF16)
        for h, val in head_tiles(proj(1)):
            ones = ((lane >= L_CQ) & (lane < L_CQ + 3)) | ((lane >= L_LSE) & (lane < L_LSE + 3))
            extra = _place3(lane, L_CK, [part[:, h:h + 1] for part in ck], jnp.where(ones, 1.0, 0.0))
            kp_ref[h] = jnp.where(lane < HEAD_DIM, val, extra).astype(BF16)
        for h, val in head_tiles(proj(2)):
            extra = jnp.where((lane >= HEAD_DIM) & (lane < HEAD_DIM + 3), -1.0, 0.0)
            vp_ref[h] = jnp.where(lane < HEAD_DIM, val, extra).astype(BF16)
        p_ref[...] = proj(3)
        ga_ref[...] = proj(4)
        gp_ref[...] = proj(5)

    head_block = pl.BlockSpec((N_HEADS, tile, LANE), lambda i: (0, i, 0))
    tok = lambda width: pl.BlockSpec((tile, width), lambda i: (i, 0))
    whole = lambda a: pl.BlockSpec(a.shape, lambda i: (0,) * a.ndim)
    padded = jax.ShapeDtypeStruct((N_HEADS, seq, LANE), BF16)
    half = jax.ShapeDtypeStruct((seq, D_ATT), F32)
    return pl.pallas_call(
        body, name="inproj_forward", grid=(nt,),
        in_specs=[tok(D), whole(mod), whole(w_main), whole(w_f), whole(b_main), whole(b_f)],
        out_specs=[head_block, head_block, head_block, tok(LANE), tok(D_POOL), tok(D_ATT), tok(D_POOL),
                   tok(D)],
        out_shape=[padded, padded, padded, jax.ShapeDtypeStruct((seq, LANE), F32), half, half, half,
                   jax.ShapeDtypeStruct((seq, D), BF16)],
        scratch_shapes=[pltpu.VMEM((8, LANE), F32)],
        compiler_params=_params(("arbitrary",)),
    )(x, mod, w_main, w_f, b_main, b_f)


def _attention_forward(qp, kp, vp, w_out, tile):
    seq = qp.shape[1]
    nb = seq // tile
    steps = (N_HEADS // 2) * nb

    def body(q_ref, k_ref, v_ref, wo_ref, att_ref, q2t_ref, wo_all_ref, s_a, s_b, m_ref, acc_ref,
             send_sems, recv_sems, local_sem):
        step = pl.program_id(0) * nb + pl.program_id(1)
        start, forward, finish = _gather_stages(wo_ref, wo_all_ref, send_sems, recv_sems, local_sem.at[0])
        pl.when(step == 0)(start)
        pl.when(step == steps // 2)(forward)

        i = pl.program_id(1)
        sub = lax.broadcasted_iota(jnp.int32, (LANE, tile), 0)
        row = lax.broadcasted_iota(jnp.int32, (tile, tile), 0)
        col = lax.broadcasted_iota(jnp.int32, (tile, tile), 1)
        q = [q_ref[0], q_ref[1]]

        def scores(buf, kb):
            rows = pl.ds(pl.multiple_of(kb * tile, tile), tile)
            for hh in range(2):
                buf[hh] = _dot_nt(k_ref[hh, rows, :], q[hh])

        def absorb(buf, kb, masked):
            rows = pl.ds(pl.multiple_of(kb * tile, tile), tile)
            for hh in range(2):
                m = m_ref[hh, 0:1, :]
                s = buf[hh]
                if masked:
                    s = jnp.where(row <= col, s, -1e30)
                m_new = jnp.maximum(m, jnp.max(s, axis=0, keepdims=True))
                p = jnp.exp(s - m_new).astype(BF16)
                acc_ref[hh] = jnp.exp(m - m_new) * acc_ref[hh] + _dot_tn(v_ref[hh, rows, :], p)
                m_ref[hh, 0:1, :] = m_new

        def two_blocks(j, _):
            scores(s_b, 2 * j + 1)
            absorb(s_a, 2 * j, False)
            scores(s_a, 2 * j + 2)
            absorb(s_b, 2 * j + 1, False)
            return 0

        def last_block():
            absorb(s_a, i, True)

        def last_two_blocks():
            scores(s_b, i)
            absorb(s_a, i - 1, False)
            absorb(s_b, i, True)

        scores(s_a, 0)
        m_ref[...] = jnp.full(m_ref.shape, -1e30, F32)
        acc_ref[...] = jnp.zeros_like(acc_ref)
        lax.fori_loop(0, i // 2, two_blocks, 0)
        lax.cond(i % 2 == 0, last_block, last_two_blocks)
        outs = []
        for hh in range(2):
            m, acc = m_ref[hh, 0:1, :], acc_ref[hh]
            l = -acc[HEAD_DIM:HEAD_DIM + 1, :]
            outs.append((acc / l)[:HEAD_DIM, :])
            neg_lse = [part.astype(F32) for part in _split3(-(m + jnp.log(l)))]
            q2t_ref[hh] = _place3(sub, L_LSE, neg_lse, q[hh].astype(F32).T).astype(BF16)
        att_ref[...] = jnp.concatenate(outs, axis=0).T
        pl.when(step == steps - 1)(finish)

    pair = pl.BlockSpec((2, tile, LANE), lambda hp, i: (hp, i, 0))
    full = pl.BlockSpec((2, seq, LANE), lambda hp, i: (hp, 0, 0))
    return pl.pallas_call(
        body, name="attention_forward", grid=(N_HEADS // 2, nb),
        in_specs=[pair, full, full, ANY],
        out_specs=[pl.BlockSpec((tile, LANE), lambda hp, i: (i, hp)),
                   pl.BlockSpec((2, LANE, tile), lambda hp, i: (hp, 0, i)), ANY],
        out_shape=[jax.ShapeDtypeStruct((seq, D_ATT), F32),
                   jax.ShapeDtypeStruct((N_HEADS, LANE, seq), BF16),
                   jax.ShapeDtypeStruct((N_DEV,) + w_out.shape, w_out.dtype)],
        scratch_shapes=[pltpu.VMEM((2, tile, tile), F32), pltpu.VMEM((2, tile, tile), F32),
                        pltpu.VMEM((2, 8, tile), F32), pltpu.VMEM((2, LANE, tile), F32),
                        pltpu.SemaphoreType.DMA((7,)), pltpu.SemaphoreType.DMA((7,)), pltpu.SemaphoreType.DMA((1,))],
        compiler_params=_params(("arbitrary", "arbitrary")),
    )(qp, kp, vp, w_out)


def _window_sum(x, halo, window, transposed):
    tile = x.shape[0]

    def split_cat(a):
        hi = a.astype(BF16)
        return jnp.concatenate([hi, (a - hi.astype(F32)).astype(BF16)], axis=1)

    def fold(r):
        return r[:, :LANE] + r[:, LANE:]

    r = lax.broadcasted_iota(jnp.int32, (tile, tile), 0)
    c = lax.broadcasted_iota(jnp.int32, (tile, tile), 1)
    rh = lax.broadcasted_iota(jnp.int32, (HALO, HALO), 0)
    ch = lax.broadcasted_iota(jnp.int32, (HALO, HALO), 1)
    if not transposed:
        band = (c <= r) & (r - c < window)
        edge = (rh + HALO - ch) < window
    else:
        band = (r <= c) & (c - r < window)
        edge = (HALO + ch - rh) < window
    out = fold(_dot(band.astype(BF16), split_cat(x)))
    reach = fold(_dot(edge.astype(BF16), split_cat(halo)))
    if not transposed:
        return jnp.concatenate([out[:HALO] + reach, out[HALO:]], axis=0)
    return jnp.concatenate([out[:tile - HALO], out[tile - HALO:] + reach], axis=0)


def _silu_parts(g):
    sig = _sigmoid(g)
    return g * sig, sig * (1.0 + g * (1.0 - sig))


def _middle(x, tgt, att, g_att, g_pool, p, vecs, pool_vecs, w_out, w_pool, tile):
    seq = x.shape[0]
    nt = seq // tile
    halo_blocks = tile // HALO

    def body(x_ref, tgt_ref, att_ref, ga_ref, gp_ref, p_ref, ph_ref, vec_ref, pvec_ref, wo_ref, wp_ref,
             dxa_ref, do2_ref, dga_ref, dgp_ref, dpooled_ref, dwo_ref, dwp_ref, dvec_ref, dpvec_ref):
        i = pl.program_id(0)

        @pl.when(i == 0)
        def _():
            dwo_ref[...] = jnp.zeros_like(dwo_ref)
            dwp_ref[...] = jnp.zeros_like(dwp_ref)
            dvec_ref[...] = jnp.zeros_like(dvec_ref)
            dpvec_ref[...] = jnp.zeros_like(dpvec_ref)

        gate, b_out, ln_g, ln_b = (vec_ref[k:k + 1, :] for k in range(4))
        b_pool, pool_scale = pvec_ref[0:1, :], pvec_ref[1:2, :]
        x = x_ref[...]
        p = p_ref[...]
        p_halo = ph_ref[...] * jnp.where(i > 0, 1.0, 0.0)
        pos = i * tile + lax.broadcasted_iota(jnp.int32, (tile, 1), 0) + 1

        pooled, mixed = [], []
        for g, window in enumerate(POOL_WINDOWS):
            cols = slice(g * GROUP_DIM, (g + 1) * GROUP_DIM)
            wsum = _window_sum(p[:, cols], p_halo[:, cols], window, False)
            count = jnp.minimum(pos, window).astype(F32)
            pooled.append(wsum / count - p[:, cols])
            mixed.append(_dot(pooled[g].astype(BF16), wp_ref[g]) + b_pool[:, cols])
        mixed = jnp.concatenate(mixed, axis=1)
        pool = mixed * pool_scale

        att = att_ref[...]
        g_att, g_pool = ga_ref[...], gp_ref[...]
        silu_a, dsilu_a = _silu_parts(g_att)
        silu_p, dsilu_p = _silu_parts(g_pool)
        y_in = jnp.concatenate([att * silu_a, pool * silu_p], axis=1)
        y = _dot(y_in.astype(BF16), wo_ref[...]) + b_out
        h = ALPHA * x + gate * y
        mu = jnp.mean(h, axis=1, keepdims=True)
        hc = h - mu
        var = jnp.mean(hc * hc, axis=1, keepdims=True)
        rstd = lax.rsqrt(var + LN_EPS)
        yhat = hc * rstd
        diff = yhat * ln_g + ln_b - tgt_ref[...]
        loss_rows = jnp.sum(diff * diff, axis=1, keepdims=True)
        d_out = diff * (1.0 / D)

        d_yhat = d_out * ln_g
        dh = rstd * (d_yhat - jnp.mean(d_yhat, axis=1, keepdims=True)
                     - yhat * jnp.mean(d_yhat * yhat, axis=1, keepdims=True))
        dxa_ref[...] = ALPHA * dh
        dy = dh * gate
        dyb = dy.astype(BF16)
        lane = _lanes((1, D))
        loss_row = jnp.where(lane == 0, (0.5 / D) * jnp.sum(loss_rows, axis=0, keepdims=True), 0.0)
        dvec_ref[0:1, :] += jnp.sum(dh * y, axis=0, keepdims=True)
        dvec_ref[1:2, :] += jnp.sum(dy, axis=0, keepdims=True)
        dvec_ref[2:3, :] += jnp.sum(d_out * yhat, axis=0, keepdims=True)
        dvec_ref[3:4, :] += jnp.sum(d_out, axis=0, keepdims=True)
        dvec_ref[4:5, :] += loss_row

        dwo_ref[...] += _dot(y_in.T.astype(BF16), dyb)
        d_yin = _dot_nt(dyb, wo_ref[...])
        d_a, d_pl = d_yin[:, :D_ATT], d_yin[:, D_ATT:]
        d_att = d_a * silu_a
        d_att_t = d_att.T
        prod_t = (d_att * att).T
        sub = lax.broadcasted_iota(jnp.int32, (HEAD_DIM, tile), 0)
        for h in range(N_HEADS):
            rows = slice(h * HEAD_DIM, (h + 1) * HEAD_DIM)
            delta = jnp.sum(prod_t[rows], axis=0, keepdims=True)
            extra = _place3(sub, 0, [part.astype(F32) for part in _split3(delta)], 0.0)
            do2_ref[h] = jnp.concatenate([d_att_t[rows], extra], axis=0).astype(BF16)
        dga_ref[...] = d_a * att * dsilu_a
        dgp_ref[...] = d_pl * pool * dsilu_p
        d_pool = d_pl * silu_p
        d_mixed = d_pool * pool_scale
        dpvec_ref[0:1, :] += jnp.sum(d_mixed, axis=0, keepdims=True)
        dpvec_ref[1:2, :] += jnp.sum(d_pool * mixed, axis=0, keepdims=True)
        d_pooled = []
        for g in range(len(POOL_WINDOWS)):
            cols = slice(g * GROUP_DIM, (g + 1) * GROUP_DIM)
            dmb = d_mixed[:, cols].astype(BF16)
            dwp_ref[g] += _dot(pooled[g].T.astype(BF16), dmb)
            d_pooled.append(_dot_nt(dmb, wp_ref[g]))
        dpooled_ref[...] = jnp.concatenate(d_pooled, axis=1)

    tok = lambda width: pl.BlockSpec((tile, width), lambda i: (i, 0))
    whole = lambda a: pl.BlockSpec(a.shape, lambda i: (0,) * a.ndim)
    halo = pl.BlockSpec((HALO, D_POOL), lambda i: (jnp.maximum(i * halo_blocks - 1, 0), 0))
    half = jax.ShapeDtypeStruct((seq, D_ATT), F32)
    outs = [jax.ShapeDtypeStruct((seq, D), F32), jax.ShapeDtypeStruct((N_HEADS, LANE, seq), BF16), half, half, half,
            jax.ShapeDtypeStruct(w_out.shape, F32), jax.ShapeDtypeStruct(w_pool.shape, F32),
            jax.ShapeDtypeStruct(vecs.shape, F32), jax.ShapeDtypeStruct(pool_vecs.shape, F32)]
    return pl.pallas_call(
        body, name="middle", grid=(nt,),
        in_specs=[tok(D), tok(D), tok(D_ATT), tok(D_ATT), tok(D_POOL), tok(D_POOL), halo,
                  whole(vecs), whole(pool_vecs), whole(w_out), whole(w_pool)],
        out_specs=[tok(D), pl.BlockSpec((N_HEADS, LANE, tile), lambda i: (0, 0, i)),
                   tok(D_ATT), tok(D_POOL), tok(D_POOL),
                   whole(w_out), whole(w_pool), whole(vecs), whole(pool_vecs)],
        out_shape=outs,
        compiler_params=_params(("arbitrary",)),
    )(x, tgt, att, g_att, g_pool, p, p, vecs, pool_vecs, w_out, w_pool)


def _attention_backward(q2t, kp, vp, do2t, gw_out, small, tile):
    seq = kp.shape[1]
    nb = seq // tile
    last = N_HEADS // 2 - 1

    def body(qt_ref, k_ref, v_ref, dot_ref, gwo_hbm, small_hbm,
             dq_ref, dk_ref, dv_ref, dcum_ref, g_out_ref, total_ref,
             dq_acc, dk_acc, dv_acc, gwo_ref, r1, s2, r2, sm_sib, sm_chip, sm_recv, small_ref, send_sems, recv_sems):
        hp = pl.program_id(0)
        start, middle, finish = _reduce_stages(
            [gwo_ref], [g_out_ref], [r1], [s2], [r2], (small_ref, total_ref, sm_sib, sm_chip, sm_recv),
            send_sems, recv_sems)

        @pl.when(hp == 0)
        def _():
            pltpu.sync_copy(gwo_hbm, gwo_ref)
            pltpu.sync_copy(small_hbm, small_ref)
            start()

        pl.when(hp == 1)(middle)

        row = lax.broadcasted_iota(jnp.int32, (tile, tile), 0)
        col = lax.broadcasted_iota(jnp.int32, (tile, tile), 1)
        dq_acc[...] = jnp.zeros_like(dq_acc)

        def kv_block(kb, _):
            krows = pl.ds(pl.multiple_of(kb * tile, tile), tile)
            k = [k_ref[hh, krows, :] for hh in range(2)]
            v = [v_ref[hh, krows, :] for hh in range(2)]
            k_t = [k[hh].T for hh in range(2)]

            def q_block(qb, masked):
                qcols = pl.ds(pl.multiple_of(qb * tile, tile), tile)
                for hh in range(2):
                    q_t = qt_ref[hh, :, qcols]
                    do_t = dot_ref[hh, :, qcols]
                    s_t = _dot(k[hh], q_t)
                    if masked:
                        s_t = jnp.where(row <= col, s_t, -1e30)
                    p_t = jnp.exp(s_t)
                    ds_t = (p_t * _dot(v[hh], do_t)).astype(BF16)
                    dv_new = _dot_nt(do_t, p_t.astype(BF16))
                    dk_new = _dot_nt(q_t, ds_t)
                    if masked:
                        dv_acc[hh], dk_acc[hh] = dv_new, dk_new
                    else:
                        dv_acc[hh] += dv_new
                        dk_acc[hh] += dk_new
                    dq_acc[hh, :, qcols] += _dot(k_t[hh], ds_t)

            q_block(kb, True)

            def two_later_blocks(j, _):
                q_block(kb + 1 + 2 * j, False)
                q_block(kb + 2 + 2 * j, False)
                return 0

            later = nb - 1 - kb
            lax.fori_loop(0, later // 2, two_later_blocks, 0)
            pl.when(later % 2 == 1)(lambda: q_block(nb - 1, False))
            for hh in range(2):
                dk = dk_acc[hh]
                dk_ref[hh, :, krows] = dk.astype(BF16)
                dv_ref[hh, :, krows] = dv_acc[hh].astype(BF16)
                dcum_ref[hh, :, krows] = -dk[L_CK:L_CK + 1, :]
            return 0

        lax.fori_loop(0, nb, kv_block, 0)
        for hh in range(2):
            dq = dq_acc[hh]
            dcum_ref[hh] += dq[L_CQ:L_CQ + 1, :]
            dq_ref[hh] = (dq * Q_SCALE).astype(BF16)
        pl.when(hp == last)(finish)

    pair = pl.BlockSpec((2, seq, LANE), lambda hp: (hp, 0, 0))
    pair_t = pl.BlockSpec((2, LANE, seq), lambda hp: (hp, 0, 0))
    whole = lambda shape: pl.BlockSpec(shape, lambda hp: (0,) * len(shape))
    grad = jax.ShapeDtypeStruct((N_HEADS, LANE, seq), BF16)
    return pl.pallas_call(
        body, name="attention_backward", grid=(N_HEADS // 2,),
        in_specs=[pair_t, pair, pair, pair_t, ANY, ANY],
        out_specs=[pair_t, pair_t, pair_t, pl.BlockSpec((2, 1, seq), lambda hp: (hp, 0, 0)),
                   whole(gw_out.shape[1:]), whole(small.shape)],
        out_shape=[grad, grad, grad, jax.ShapeDtypeStruct((N_HEADS, 1, seq), F32),
                   jax.ShapeDtypeStruct(gw_out.shape[1:], F32), jax.ShapeDtypeStruct(small.shape, F32)],
        scratch_shapes=[pltpu.VMEM((2, LANE, seq), F32), pltpu.VMEM((2, LANE, tile), F32),
                        pltpu.VMEM((2, LANE, tile), F32), pltpu.VMEM(gw_out.shape, BF16)]
        + _reduce_scratch(gw_out, small)
        + [pltpu.VMEM(small.shape, F32),
           pltpu.SemaphoreType.DMA((N_REDUCE_SEMS + N_SMALL_SEMS,)), pltpu.SemaphoreType.DMA((N_REDUCE_SEMS + N_SMALL_SEMS,))],
        compiler_params=_params(("arbitrary",)),
    )(q2t, kp, vp, do2t, gw_out, small)


def _inproj_backward(dqp, dkp, dvp, d_cum, f, d_pooled, d_ga, d_gp, x, dxa, u, mod, w_main, w_f, tile):
    seq = x.shape[0]
    nt = seq // tile
    halo_blocks = tile // HALO

    def body(dq_ref, dk_ref, dv_ref, dcum_ref, f_ref, dpo_ref, dph_ref, dga_ref, dgp_ref, x_ref, dxa_ref, u_ref,
             mod_ref, w_ref, wf_ref,
             dx_ref, dproj_ref, dwf_ref, db_ref, dbf_ref, dmod_ref, carry_ref):
        step = pl.program_id(0)
        i = nt - 1 - step

        @pl.when(step == 0)
        def _():
            carry_ref[...] = jnp.zeros_like(carry_ref)
            dwf_ref[...] = jnp.zeros_like(dwf_ref)
            db_ref[...] = jnp.zeros_like(db_ref)
            dbf_ref[...] = jnp.zeros_like(dbf_ref)
            dmod_ref[...] = jnp.zeros_like(dmod_ref)

        ones = jnp.ones((8, tile), BF16)

        def emit(chunk, val):
            cols = pl.ds(chunk * COL_CHUNK, COL_CHUNK)
            db_ref[0:1, cols] += jnp.sum(val, axis=0, keepdims=True)
            vb = val.astype(BF16)
            dproj_ref[:, pl.ds((chunk - 3) * COL_CHUNK, COL_CHUNK)] = vb
            return _dot(vb, w_ref[cols, :])

        d_u = jnp.zeros((tile, D), F32)
        for chunk, ref in enumerate((dq_ref, dk_ref, dv_ref)):
            cols = pl.ds(chunk * COL_CHUNK, COL_CHUNK)
            val_t = ref[:, 0:HEAD_DIM, :].reshape(COL_CHUNK, tile)
            db_ref[:, cols] += _dot_nt(ones, val_t)
            d_u += _dot_tn(val_t, w_ref[cols, :])

        d_pooled = dpo_ref[...]
        d_halo = dph_ref[...] * jnp.where(i < nt - 1, 1.0, 0.0)
        pos = i * tile + lax.broadcasted_iota(jnp.int32, (tile, 1), 0) + 1
        d_p = []
        for g, window in enumerate(POOL_WINDOWS):
            cols = slice(g * GROUP_DIM, (g + 1) * GROUP_DIM)
            scaled = d_pooled[:, cols] / jnp.minimum(pos, window).astype(F32)
            d_p.append(_window_sum(scaled, d_halo[:, cols] * (1.0 / window), window, True) - d_pooled[:, cols])
        d_u += emit(3, jnp.concatenate(d_p, axis=1))
        d_u += emit(4, dga_ref[...])
        d_u += emit(5, dgp_ref[...])

        row = lax.broadcasted_iota(jnp.int32, (tile, tile), 0)
        col = lax.broadcasted_iota(jnp.int32, (tile, tile), 1)
        later = (row >= col).astype(BF16)
        d_logf = sum(_dot(part, later) for part in _split3(dcum_ref[:, 0, :])) + carry_ref[:, 0:1]
        carry_ref[:, 0:1] = d_logf[:, 0:1]
        d_f = d_logf * _sigmoid(-f_ref[...].T[0:N_HEADS, :])
        d_f = jnp.concatenate([d_f, jnp.zeros((LANE - N_HEADS, tile), F32)], axis=0)
        dbf_ref[...] += sum(_dot_nt(ones, part) for part in _split3(d_f))
        d_fb = d_f.astype(BF16)
        d_u += _dot_tn(d_fb, wf_ref[...])
        dwf_ref[...] += _dot(d_fb, u_ref[...])

        x = x_ref[...]
        dx_ref[...] = dxa_ref[...] + d_u * mod_ref[0:1, :]
        dmod_ref[0:1, :] += jnp.sum(d_u * x, axis=0, keepdims=True)
        dmod_ref[1:2, :] += jnp.sum(d_u, axis=0, keepdims=True)

    rev = lambda step: nt - 1 - step
    tok = lambda width: pl.BlockSpec((tile, width), lambda s: (rev(s), 0))
    head_block = pl.BlockSpec((N_HEADS, LANE, tile), lambda s: (0, 0, rev(s)))
    whole = lambda a: pl.BlockSpec(a.shape, lambda s: (0,) * a.ndim)
    halo = pl.BlockSpec((HALO, D_POOL), lambda s: (jnp.minimum((rev(s) + 1) * halo_blocks, seq // HALO - 1), 0))
    small = lambda width: jax.ShapeDtypeStruct((8, width), F32)
    n_rest = N_MAIN - OFF_P
    return pl.pallas_call(
        body, name="inproj_backward", grid=(nt,),
        in_specs=[head_block, head_block, head_block, pl.BlockSpec((N_HEADS, 1, tile), lambda s: (0, 0, rev(s))),
                  tok(LANE), tok(D_POOL), halo, tok(D_ATT), tok(D_POOL),
                  tok(D), tok(D), tok(D),
                  whole(mod), whole(w_main), whole(w_f)],
        out_specs=[tok(D), tok(n_rest), pl.BlockSpec((LANE, D), lambda s: (0, 0)),
                   pl.BlockSpec((8, N_MAIN), lambda s: (0, 0)), pl.BlockSpec((8, LANE), lambda s: (0, 0)),
                   pl.BlockSpec((8, D), lambda s: (0, 0))],
        out_shape=[jax.ShapeDtypeStruct((seq, D), F32), jax.ShapeDtypeStruct((seq, n_rest), BF16),
                   jax.ShapeDtypeStruct((LANE, D), F32), small(N_MAIN), small(LANE), small(D)],
        scratch_shapes=[pltpu.VMEM((8, LANE), F32)],
        compiler_params=_params(("arbitrary",)),
    )(dqp, dkp, dvp, d_cum, f, d_pooled, d_pooled, d_ga, d_gp, x, dxa, u, mod, w_main, w_f)


def _weight_grad(dproj, u, k_tile):
    seq, n_cols = dproj.shape
    nk = seq // k_tile

    def body(dp_ref, u_ref, out_ref):
        @pl.when(pl.program_id(1) == 0)
        def _():
            out_ref[...] = jnp.zeros_like(out_ref)

        out_ref[...] += _dot_tn(dp_ref[...], u_ref[...])

    return pl.pallas_call(
        body, name="weight_grad", grid=(n_cols // COL_CHUNK, nk),
        in_specs=[pl.BlockSpec((k_tile, COL_CHUNK), lambda n, k: (k, n)),
                  pl.BlockSpec((k_tile, D), lambda n, k: (k, 0))],
        out_specs=pl.BlockSpec((COL_CHUNK, D), lambda n, k: (n, 0)),
        out_shape=jax.ShapeDtypeStruct((n_cols, D), F32),
        compiler_params=_params(("arbitrary", "arbitrary")),
    )(dproj, u)


def _weight_grad_heads(grad_t, u, k_tile, name):
    seq = u.shape[0]
    nk = seq // k_tile

    def body(g_ref, u_ref, out_ref):
        @pl.when(pl.program_id(0) == 0)
        def _():
            out_ref[...] = jnp.zeros_like(out_ref)

        out_ref[...] += _dot(g_ref[...].reshape(N_HEADS * HEAD_DIM, k_tile), u_ref[...])

    return pl.pallas_call(
        body, name=name, grid=(nk,),
        in_specs=[pl.BlockSpec((N_HEADS, HEAD_DIM, k_tile), lambda k: (0, 0, k)),
                  pl.BlockSpec((k_tile, D), lambda k: (k, 0))],
        out_specs=pl.BlockSpec((N_HEADS * HEAD_DIM, D), lambda k: (0, 0)),
        out_shape=jax.ShapeDtypeStruct((N_HEADS * HEAD_DIM, D), F32),
        compiler_params=_params(("arbitrary",)),
    )(grad_t, u)


def _adamw(w, g, m, v):
    m = ADAM_B1 * m + (1.0 - ADAM_B1) * g
    v = ADAM_B2 * v + (1.0 - ADAM_B2) * (g * g)
    m_hat = m / (1.0 - ADAM_B1 ** ADAM_STEP)
    v_hat = v / (1.0 - ADAM_B2 ** ADAM_STEP)
    delta = -ADAM_LR * (m_hat / (jnp.sqrt(v_hat) + ADAM_EPS) + ADAM_WD * w)
    return delta, m, v


def _adamw_call(g, w, m, v, lead_tile, name):
    nr = w.shape[0] // lead_tile

    def body(gi_ref, w_ref, m_ref, v_ref, g_ref, d_ref, nm_ref, nv_ref):
        g = gi_ref[...]
        g_ref[...] = g
        d_ref[...], nm_ref[...], nv_ref[...] = _adamw(w_ref[...], g, m_ref[...], v_ref[...])

    blk = pl.BlockSpec((lead_tile,) + w.shape[1:], lambda r: (r,) + (0,) * (w.ndim - 1))
    shape = jax.ShapeDtypeStruct(w.shape, F32)
    return pl.pallas_call(
        body, name=name, grid=(nr,),
        in_specs=[blk, blk, blk, blk], out_specs=[blk, blk, blk, blk],
        out_shape=[shape, shape, shape, shape],
        compiler_params=_params(("arbitrary",)),
    )(g, w, m, v)


def _sum_adamw(parts, w, m, v, row_tile, name):
    rows, cols = w.shape
    nr = rows // row_tile

    def body(parts_ref, w_ref, m_ref, v_ref, g_ref, d_ref, nm_ref, nv_ref):
        g = parts_ref[0]
        for k in range(1, N_DEV):
            g = g + parts_ref[k]
        g_ref[...] = g
        d_ref[...], nm_ref[...], nv_ref[...] = _adamw(w_ref[...], g, m_ref[...], v_ref[...])

    blk = pl.BlockSpec((row_tile, cols), lambda r: (r, 0))
    shape = jax.ShapeDtypeStruct(w.shape, F32)
    return pl.pallas_call(
        body, name=name, grid=(nr,),
        in_specs=[pl.BlockSpec((N_DEV, row_tile, cols), lambda r: (0, r, 0)), blk, blk, blk],
        out_specs=[blk, blk, blk, blk],
        out_shape=[shape, shape, shape, shape],
        compiler_params=_params(("arbitrary",)),
    )(parts, w, m, v)


def _ada_adamw(sc_t, d_ada, w, m, v):
    def body(sc_ref, d_ref, w_ref, m_ref, v_ref, g_ref, dl_ref, nm_ref, nv_ref):
        g = sc_ref[:, 0:1] * d_ref[0:1, :]
        for b in range(1, N_DEV):
            g = g + sc_ref[:, b:b + 1] * d_ref[b:b + 1, :]
        g_ref[...] = g
        dl_ref[...], nm_ref[...], nv_ref[...] = _adamw(w_ref[...], g, m_ref[...], v_ref[...])

    shape = jax.ShapeDtypeStruct(w.shape, F32)
    return pl.pallas_call(
        body, name="ada_adamw", out_shape=[shape, shape, shape, shape], compiler_params=_params(),
    )(sc_t, d_ada, w, m, v)


F_LO, F_HI = 3 * D_ATT, 3 * D_ATT + N_HEADS


def _split_forget(a, axis):
    idx = lambda lo, hi: tuple(slice(lo, hi) if d == axis else slice(None) for d in range(a.ndim))
    pad = [(0, LANE - N_HEADS) if d == axis else (0, 0) for d in range(a.ndim)]
    return jnp.concatenate([a[idx(0, F_LO)], a[idx(F_HI, D_IN)]], axis=axis), jnp.pad(a[idx(F_LO, F_HI)], pad)


def _join_forget(main, f, axis):
    idx = lambda lo, hi: tuple(slice(lo, hi) if d == axis else slice(None) for d in range(main.ndim))
    return jnp.concatenate([main[idx(0, F_LO)], f[idx(0, N_HEADS)], main[idx(F_LO, N_MAIN)]], axis=axis)


_EARLY = ((("w_pool", 65536), ("b_pool", 512), ("pool_scale", 512), ("b_out", 1024), ("ln_g", 1024), ("ln_b", 1024),
           ("loss", 128)), 552)
_LATE = ((("b_in", 3200),), 32)


def _pack_small(parts, layout):
    spec, rows = layout
    flat = []
    for name, size in spec:
        a = parts[name].reshape(-1)
        flat.append(jnp.pad(a, (0, size - a.shape[0])))
    flat = jnp.concatenate(flat)
    flat = jnp.pad(flat, (0, rows * LANE - flat.shape[0]))
    return flat.reshape(rows, LANE)


def _unpack_small(packed, shapes, layout):
    flat = packed.reshape(-1)
    out, off = {}, 0
    for name, size in layout[0]:
        n = 1
        for s in shapes[name]:
            n *= s
        out[name] = flat[off:off + n].reshape(shapes[name])
        off += size
    return out


def kernel(x, c, w_ada, b_ada, w_in, b_in, w_pool_mix, b_pool_mix, pool_scale, w_out, b_out, ln_g, ln_b, loss_target, m_w_ada, m_b_ada, m_w_in, m_b_in, m_w_pool_mix, m_b_pool_mix, m_pool_scale, m_w_out, m_b_out, m_ln_g, m_ln_b, v_w_ada, v_b_ada, v_w_in, v_b_in, v_w_pool_mix, v_b_pool_mix, v_pool_scale, v_w_out, v_b_out, v_ln_g, v_ln_b):
    seq = x.shape[1]
    tile = min(256, seq)
    attn_tile = min(512, max(128, seq // 4))
    me = _dev_index(*_mesh_pos())
    x2, tgt = x[0], loss_target[0]

    rows_in = D_IN // N_DEV
    w_in_g, sc_all, ada_mine = _gather_and_ada(c, w_in[0].T.astype(BF16), w_ada[0])
    ada = ada_mine.reshape(1, D_ADA) + b_ada
    shift, scale, gate = ada[:, 0:D], ada[:, D:2 * D], ada[:, 2 * D:]
    mod = jnp.concatenate([1.0 + scale, shift, jnp.zeros((6, D), F32)], axis=0)

    w_main, w_f = _split_forget(w_in_g.reshape(D_IN, D), 0)
    b_main, b_f = _split_forget(b_in, 1)

    qp, kp, vp, f, p, g_att, g_pool, u = _inproj_forward(x2, mod, w_main, w_f, b_main, b_f, tile)
    att, q2t, w_out_g = _attention_forward(qp, kp, vp, w_out[0].astype(BF16), attn_tile)

    vecs = jnp.concatenate([gate, b_out, ln_g, ln_b, jnp.zeros((4, D), F32)], axis=0)
    pool_vecs = jnp.concatenate([b_pool_mix.reshape(1, D_POOL), pool_scale, jnp.zeros((6, D_POOL), F32)], axis=0)
    dxa, do2, d_ga, d_gp, d_pooled, dw_out, dw_pool, dvec, dpvec = _middle(
        x2, tgt, att, g_att, g_pool, p, vecs, pool_vecs, w_out_g.reshape(D, D), w_pool_mix[0].astype(BF16), tile)

    gw_out = dw_out.reshape(N_DEV, D // N_DEV, D).astype(BF16)
    early = _pack_small({"w_pool": dw_pool, "b_pool": dpvec[0:1], "pool_scale": dpvec[1:2], "b_out": dvec[1:2],
                         "ln_g": dvec[2:3], "ln_b": dvec[3:4], "loss": dvec[4:5, 0:LANE]}, _EARLY)
    dqp, dkp, dvp, d_cum, g_out, early_sum = _attention_backward(q2t, kp, vp, do2, gw_out, early, attn_tile)
    dx, dproj, dw_f, db_main, db_f, dmod = _inproj_backward(
        dqp, dkp, dvp, d_cum, f, d_pooled, d_ga, d_gp, x2, dxa, u, mod, w_main, w_f, tile)
    k_tile = min(1024, seq)
    dw_q, dw_k, dw_v = (_weight_grad_heads(g, u, k_tile, "weight_grad_" + n)
                        for g, n in ((dqp, "q"), (dkp, "k"), (dvp, "v")))
    dw_rest = _weight_grad(dproj, u, k_tile)

    dw_main = jnp.concatenate([dw_q, dw_k, dw_v, dw_rest], axis=0)
    gw_in = _join_forget(dw_main, dw_f, 0).reshape(N_DEV, rows_in, D).astype(BF16)
    d_ada = jnp.concatenate([dmod[1:2], dmod[0:1], dvec[0:1]], axis=1)
    late = _pack_small({"b_in": _join_forget(db_main[0:1], db_f[0:1], 1)}, _LATE)
    g_in_rows, late_sum, d_ada_all = _reduce_grads(gw_in, late, d_ada.reshape(D_ADA // LANE, LANE))

    def rows3(a):
        return a[0].T.reshape(rows_in, D // LANE, LANE)

    outs_in = _adamw_call(g_in_rows.reshape(rows_in, D // LANE, LANE), rows3(w_in), rows3(m_w_in), rows3(v_w_in),
                          rows_in // 5, "adamw_w_in")
    g_w_in, d_w_in, nm_w_in, nv_w_in = (a.reshape(rows_in, D).T for a in outs_in)
    g_w_out, d_w_out, nm_w_out, nv_w_out = _adamw_call(g_out, w_out[0], m_w_out[0], v_w_out[0], D // N_DEV, "adamw_w_out")

    def small_adamw(grad_sum, weights, first, second, layout, name):
        packed = _adamw_call(grad_sum, _pack_small(weights, layout), _pack_small(first, layout),
                             _pack_small(second, layout), layout[1], name)
        shapes = {k: a.shape for k, a in weights.items()}
        return [_unpack_small(a, shapes, layout) for a in packed]

    zero = jnp.zeros((1,), F32)
    smalls = small_adamw(
        early_sum,
        {"w_pool": w_pool_mix, "b_pool": b_pool_mix, "pool_scale": pool_scale, "b_out": b_out, "ln_g": ln_g,
         "ln_b": ln_b, "loss": zero},
        {"w_pool": m_w_pool_mix, "b_pool": m_b_pool_mix, "pool_scale": m_pool_scale, "b_out": m_b_out,
         "ln_g": m_ln_g, "ln_b": m_ln_b, "loss": zero},
        {"w_pool": v_w_pool_mix, "b_pool": v_b_pool_mix, "pool_scale": v_pool_scale, "b_out": v_b_out,
         "ln_g": v_ln_g, "ln_b": v_ln_b, "loss": zero}, _EARLY, "adamw_small")
    b_ins = small_adamw(late_sum, {"b_in": b_in}, {"b_in": m_b_in}, {"b_in": v_b_in}, _LATE, "adamw_b_in")
    g_s, d_s, nm_s, nv_s = ({**a, **b} for a, b in zip(smalls, b_ins))
    loss = g_s["loss"][0]

    ada_rows = D_ADA // LANE
    b_ada_outs = _sum_adamw(d_ada_all, b_ada.reshape(ada_rows, LANE), m_b_ada.reshape(ada_rows, LANE),
                            v_b_ada.reshape(ada_rows, LANE), ada_rows, "adamw_b_ada")
    g_b_ada, d_b_ada, nm_b_ada, nv_b_ada = (a.reshape(1, D_ADA) for a in b_ada_outs)
    d_ada_local = lax.dynamic_slice_in_dim(d_ada_all.reshape(N_DEV, D_ADA), me * (D_ADA // N_DEV), D_ADA // N_DEV, axis=1)
    g_w_ada, d_w_ada, nm_w_ada, nv_w_ada = _ada_adamw(sc_all.T, d_ada_local, w_ada[0], m_w_ada[0], v_w_ada[0])

    def ordered(w_ada_, b_ada_, w_in_, w_out_, s):
        return (w_ada_[None], b_ada_, w_in_[None], s["b_in"], s["w_pool"], s["b_pool"], s["pool_scale"],
                w_out_[None], s["b_out"], s["ln_g"], s["ln_b"])

    return (loss, dx[None],
            *ordered(g_w_ada, g_b_ada, g_w_in, g_w_out, g_s),
            *ordered(d_w_ada, d_b_ada, d_w_in, d_w_out, d_s),
            *ordered(nm_w_ada, nm_b_ada, nm_w_in, nm_w_out, nm_s),
            *ordered(nv_w_ada, nv_b_ada, nv_w_in, nv_w_out, nv_s))
```

```python
import jax
import jax.numpy as jnp
from jax import lax
from jax.experimental import pallas as pl
from jax.experimental.pallas import tpu as pltpu

F32 = jnp.float32
BF16 = jnp.bfloat16

N_DEV = 8
D = 1024
N_HEADS = 8
HEAD_DIM = 64
D_ATT = 512
D_POOL = 512
POOL_WINDOWS = (2, 4, 8, 16)
GROUP_DIM = 128
HALO = 16
LANE = 128
D_IN = 3080
D_ADA = 3072
N_MAIN = 3072
OFF_P = 1536
COL_CHUNK = 512
Q_SCALE = 0.125
LN_EPS = 1e-5
ALPHA = 2.0 ** 0.25
L_CQ, L_CK, L_LSE = 64, 67, 70

ADAM_LR, ADAM_B1, ADAM_B2, ADAM_EPS, ADAM_WD, ADAM_STEP = 0.001, 0.9, 0.999, 1e-08, 0.01, 10
VMEM_LIMIT = 56 * 1024 * 1024

MESH = pl.DeviceIdType.MESH
ANY = pl.BlockSpec(memory_space=pl.ANY)


def _params(sem=None, vmem=VMEM_LIMIT):
    return pltpu.CompilerParams(dimension_semantics=sem, vmem_limit_bytes=vmem)


def _split3(a):
    hi = a.astype(BF16)
    r = a - hi.astype(F32)
    mid = r.astype(BF16)
    lo = (r - mid.astype(F32)).astype(BF16)
    return hi, mid, lo


def _dot(a, b):
    return jnp.dot(a, b, preferred_element_type=F32)


def _dot_nt(a, b):
    return lax.dot_general(a, b, (((1,), (1,)), ((), ())), preferred_element_type=F32)


def _dot_tn(a, b):
    return lax.dot_general(a, b, (((0,), (0,)), ((), ())), preferred_element_type=F32)


def _dot3(m01, a):
    hi, mid, lo = _split3(a)
    return _dot(m01, hi) + _dot(m01, mid) + _dot(m01, lo)


def _sigmoid(z):
    return 1.0 / (1.0 + jnp.exp(-z))


def _lanes(shape):
    return lax.broadcasted_iota(jnp.int32, shape, len(shape) - 1)


def _place3(lane, base, parts, other):
    out = other
    for j in range(3):
        out = jnp.where(lane == base + j, parts[j], out)
    return out


def _mesh_pos():
    return lax.axis_index("x"), lax.axis_index("y"), lax.axis_index("c")


def _dev_index(px, py, pc):
    return 4 * px + 2 * py + pc


N_GATHER_SEMS = 9


def _gather_stages(src_ref, out_ref, send_sems, recv_sems, local_sem):
    x, y, c = _mesh_pos()
    me, sibling = (x, y, c), (x, y, 1 - c)
    nbr_x, nbr_y, diag = (1 - x, y), (x, 1 - y), (1 - x, 1 - y)
    half = out_ref.shape[-1] // 2
    left, right = pl.ds(0, half), pl.ds(half, half)

    def copy(k, block, to, cols=None, src=None):
        slot = out_ref.at[_dev_index(*block)]
        if cols is not None:
            slot = slot.at[:, cols]
        return pltpu.make_async_remote_copy(
            src_ref=slot if src is None else src, dst_ref=slot, send_sem=send_sems.at[k], recv_sem=recv_sems.at[k],
            device_id=to, device_id_type=MESH)

    mine = pltpu.make_async_copy(src_ref, out_ref.at[_dev_index(*me)], local_sem)
    first = [copy(0, me, sibling, src=src_ref), copy(1, me, (*nbr_x, c), src=src_ref), copy(2, me, (*nbr_y, c), src=src_ref)]
    relay = [(1, nbr_x, None, nbr_x), (2, nbr_y, None, nbr_y), (3, diag, left, nbr_y), (4, diag, right, nbr_x)]
    onward = [copy(3, (*nbr_x, c), (*nbr_y, c), cols=left), copy(4, (*nbr_y, c), (*nbr_x, c), cols=right)]
    passed = [copy(4 + k, (*block, c), sibling, cols=cols) for k, block, cols, _ in relay]

    def start():
        mine.start()
        for cp in first:
            cp.start()

    def forward():
        for j, (k, block, cols, frm) in enumerate(relay):
            copy(k, (*block, c), (*frm, c), cols=cols).wait_recv()
            if j < 2:
                onward[j].start()
            passed[j].start()

    def finish():
        copy(0, sibling, me).wait_recv()
        for k, block, cols, _ in relay:
            copy(4 + k, (*block, 1 - c), me, cols=cols).wait_recv()
        for cp in first + onward + passed:
            cp.wait_send()
        mine.wait()

    return start, forward, finish


N_REDUCE_SEMS = 7
N_SMALL_SEMS = 4 + 7


def _reduce_stages(ins, gs, r1, s2, r2, small, send_sems, recv_sems, rows=None):
    n = len(ins)
    x, y, c = _mesh_pos()
    me = _dev_index(x, y, c)
    sibling = (x, y, 1 - c)
    chips = [(x, y), (1 - x, y), (x, 1 - y), (1 - x, 1 - y)]
    peers = []
    for p in range(1, N_DEV):
        px, py, pc = (p >> 2) & 1, (p >> 1) & 1, p & 1
        peers.append((1 - x if px else x, 1 - y if py else y, 1 - c if pc else c))
    base_small = N_REDUCE_SEMS * n

    def remote(src, dst, k, to):
        return pltpu.make_async_remote_copy(src_ref=src, dst_ref=dst, send_sem=send_sems.at[k],
                                            recv_sem=recv_sems.at[k], device_id=to, device_id_type=MESH)

    def level1(a, q):
        return remote(ins[a].at[_dev_index(*chips[q], 1 - c)], r1[a].at[q], N_REDUCE_SEMS * a + q, sibling)

    def level2(a, j):
        return remote(s2[a].at[j], r2[a].at[j], N_REDUCE_SEMS * a + 4 + j, (*chips[j + 1], c))

    if small is not None:
        small_ref, total_ref, sm_sib, sm_chip, sm_recv = small
        to_sibling = remote(small_ref, sm_sib, base_small, sibling)
        to_chips = [remote(sm_chip, sm_recv.at[j], base_small + 1 + j, (*chips[j + 1], c)) for j in range(3)]
    if rows is not None:
        rows_ref, land_ref, all_ref = rows
        row_sends = [remote(rows_ref, land_ref.at[me], base_small + 4 + k, to) for k, to in enumerate(peers)]

    def start():
        for a in range(n):
            for q in range(4):
                level1(a, q).start()
        if small is not None:
            to_sibling.start()
        if rows is not None:
            for cp in row_sends:
                cp.start()
            land_ref[me] = rows_ref[...]

    def middle():
        for a in range(n):
            for q in (1, 2, 3, 0):
                level1(a, q).wait_recv()
                pair = ins[a][_dev_index(*chips[q], c)].astype(F32) + r1[a][q].astype(F32)
                if q == 0:
                    gs[a][...] = pair
                else:
                    s2[a][q - 1] = pair.astype(BF16)
                    level2(a, q - 1).start()
        if small is not None:
            to_sibling.wait_recv()
            sm_chip[...] = small_ref[...] + sm_sib[...]
            for cp in to_chips:
                cp.start()

    def finish():
        for a in range(n):
            own = gs[a][...]
            for j in range(3):
                level2(a, j).wait_recv()
                own = own + r2[a][j].astype(F32)
            gs[a][...] = own
            for q in range(4):
                level1(a, q).wait_send()
            for j in range(3):
                level2(a, j).wait_send()
        if small is not None:
            for cp in to_chips:
                cp.wait_recv()
            total = None
            for ax in range(2):
                for ay in range(2):
                    dx, dy = x != ax, y != ay
                    term = jnp.where(dx, jnp.where(dy, sm_recv[2], sm_recv[0]), jnp.where(dy, sm_recv[1], sm_chip[...]))
                    total = term if total is None else total + term
            total_ref[...] = total
            for cp in [to_sibling] + to_chips:
                cp.wait_send()
        if rows is not None:
            for k, frm in enumerate(peers):
                remote(rows_ref, land_ref.at[_dev_index(*frm)], base_small + 4 + k, frm).wait_recv()
            all_ref[...] = land_ref[...]
            for cp in row_sends:
                cp.wait_send()

    return start, middle, finish


def _reduce_scratch(shard, small, rows=None):
    out = [pltpu.VMEM((lead,) + shard.shape[1:], BF16) for lead in (4, 3, 3)]
    out += [pltpu.VMEM(small.shape, F32), pltpu.VMEM(small.shape, F32), pltpu.VMEM((3,) + small.shape, F32)]
    if rows is not None:
        out.append(pltpu.VMEM((N_DEV,) + rows.shape, F32))
    return out


def _reduce_grads(gw_in, small, rows):
    def body(in_ref, small_ref, rows_ref, g_ref, total_ref, rows_all_ref,
             r1, s2, r2, sm_sib, sm_chip, sm_recv, rows_land, send_sems, recv_sems):
        start, middle, finish = _reduce_stages(
            [in_ref], [g_ref], [r1], [s2], [r2], (small_ref, total_ref, sm_sib, sm_chip, sm_recv),
            send_sems, recv_sems, rows=(rows_ref, rows_land, rows_all_ref))
        start()
        middle()
        finish()

    return pl.pallas_call(
        body, name="reduce_grads",
        out_shape=[jax.ShapeDtypeStruct(gw_in.shape[1:], F32), jax.ShapeDtypeStruct(small.shape, F32),
                   jax.ShapeDtypeStruct((N_DEV,) + rows.shape, F32)],
        scratch_shapes=_reduce_scratch(gw_in, small, rows)
        + [pltpu.SemaphoreType.DMA((N_REDUCE_SEMS + N_SMALL_SEMS,))] * 2,
        compiler_params=_params(),
    )(gw_in, small, rows)


def _dot3_rhs(a, b):
    a0, a1, a2 = _split3(a)
    b0, b1, b2 = _split3(b)
    return (_dot(a0, b0) + (_dot(a0, b1) + _dot(a1, b0))
            + (_dot(a0, b2) + _dot(a1, b1) + _dot(a2, b0)))


def _gather_and_ada(c, w_in_rows, w_ada):
    cols = w_ada.shape[1]

    def body(c_ref, w_ref, wa_ref, w_all_ref, sc_ref, ada_ref,
             c_land, part, ada_land, send_sems, recv_sems, local_sem, x_send, x_recv):
        x, y, cc = _mesh_pos()
        me = _dev_index(x, y, cc)
        peers = []
        for p in range(1, N_DEV):
            px, py, pc = (p >> 2) & 1, (p >> 1) & 1, p & 1
            peers.append((1 - x if px else x, 1 - y if py else y, 1 - cc if pc else cc))

        def remote(src, dst, k, to):
            return pltpu.make_async_remote_copy(src_ref=src, dst_ref=dst, send_sem=x_send.at[k], recv_sem=x_recv.at[k],
                                                device_id=to, device_id_type=MESH)

        c_sends = [remote(c_ref, c_land.at[me], k, to) for k, to in enumerate(peers)]
        for cp in c_sends:
            cp.start()
        start, forward, finish = _gather_stages(w_ref, w_all_ref, send_sems, recv_sems, local_sem.at[0])
        start()
        c_land[me] = c_ref[...]
        for k, frm in enumerate(peers):
            remote(c_ref, c_land.at[_dev_index(*frm)], k, frm).wait_recv()
        c_all = jnp.concatenate([c_land[b] for b in range(N_DEV)], axis=0)
        sc = c_all * _sigmoid(c_all)
        sc_ref[...] = sc
        rows = _dot3_rhs(sc, wa_ref[...])
        for b in range(N_DEV):
            part[b] = rows[b:b + 1, :]
        a_sends = [remote(part.at[_dev_index(*to)], ada_land.at[me], 7 + k, to) for k, to in enumerate(peers)]
        for cp in a_sends:
            cp.start()
        ada_land[me] = part[me]
        for k, frm in enumerate(peers):
            remote(part.at[0], ada_land.at[_dev_index(*frm)], 7 + k, frm).wait_recv()
        ada_ref[...] = ada_land[...]

        forward()
        finish()
        for cp in c_sends + a_sends:
            cp.wait_send()

    vmem = pl.BlockSpec(memory_space=pltpu.VMEM)
    return pl.pallas_call(
        body, name="gather_weights",
        in_specs=[vmem, ANY, vmem], out_specs=[ANY, vmem, vmem],
        out_shape=[jax.ShapeDtypeStruct((N_DEV,) + w_in_rows.shape, w_in_rows.dtype),
                   jax.ShapeDtypeStruct((N_DEV, D), F32), jax.ShapeDtypeStruct((N_DEV, 1, cols), F32)],
        scratch_shapes=[pltpu.VMEM((N_DEV, 1, D), F32), pltpu.VMEM((N_DEV, 1, cols), F32), pltpu.VMEM((N_DEV, 1, cols), F32),
                        pltpu.SemaphoreType.DMA((N_GATHER_SEMS,)), pltpu.SemaphoreType.DMA((N_GATHER_SEMS,)),
                        pltpu.SemaphoreType.DMA((1,)),
                        pltpu.SemaphoreType.DMA((14,)), pltpu.SemaphoreType.DMA((14,))],
        compiler_params=_params(),
    )(c, w_in_rows, w_ada)


def _inproj_forward(x, mod, w_main, w_f, b_main, b_f, tile):
    seq = x.shape[0]
    nt = seq // tile

    def body(x_ref, mod_ref, w_ref, wf_ref, b_ref, bf_ref,
             qp_ref, kp_ref, vp_ref, f_ref, p_ref, ga_ref, gp_ref, u_ref, carry_ref):
        i = pl.program_id(0)

        @pl.when(i == 0)
        def _():
            carry_ref[...] = jnp.zeros_like(carry_ref)

        u = x_ref[...] * mod_ref[0:1, :] + mod_ref[1:2, :]
        ub = u.astype(BF16)
        u_ref[...] = ub

        f = _dot_nt(ub, wf_ref[...]) + bf_ref[...]
        f_ref[...] = f
        lane = _lanes((tile, LANE))
        log_f = jnp.where(lane < N_HEADS, jnp.minimum(f, 0.0) - jnp.log(1.0 + jnp.exp(-jnp.abs(f))), 0.0)
        row = lax.broadcasted_iota(jnp.int32, (tile, tile), 0)
        col = lax.broadcasted_iota(jnp.int32, (tile, tile), 1)
        tri = (row >= col).astype(BF16)
        cum = _dot3(tri, log_f) + carry_ref[0:1, :]
        carry_ref[0:1, :] = cum[tile - 1:tile, :]
        cq = [part.astype(F32) for part in _split3(cum)]
        ck = [part.astype(F32) for part in _split3(-cum)]

        def proj(chunk):
            cols = pl.ds(chunk * COL_CHUNK, COL_CHUNK)
            return _dot_nt(ub, w_ref[cols, :]) + b_ref[:, cols]

        def head_tiles(r):
            for pair in range(N_HEADS // 2):
                both = r[:, pair * LANE:(pair + 1) * LANE]
                yield 2 * pair, both
                yield 2 * pair + 1, pltpu.roll(both, HEAD_DIM, 1)

        for h, val in head_tiles(proj(0)):
            extra = jnp.where((lane >= L_CK) & (lane < L_CK + 3), 1.0, 0.0)
            extra = _place3(lane, L_CQ, [part[:, h:h + 1] for part in cq], extra)
            qp_ref[h] = jnp.where(lane < HEAD_DIM, val * Q_SCALE, extra).astype(BF16)
        for h, val in head_tiles(proj(1)):
            ones = ((lane >= L_CQ) & (lane < L_CQ + 3)) | ((lane >= L_LSE) & (lane < L_LSE + 3))
            extra = _place3(lane, L_CK, [part[:, h:h + 1] for part in ck], jnp.where(ones, 1.0, 0.0))
            kp_ref[h] = jnp.where(lane < HEAD_DIM, val, extra).astype(BF16)
        for h, val in head_tiles(proj(2)):
            extra = jnp.where((lane >= HEAD_DIM) & (lane < HEAD_DIM + 3), -1.0, 0.0)
            vp_ref[h] = jnp.where(lane < HEAD_DIM, val, extra).astype(BF16)
        p_ref[...] = proj(3)
        ga_ref[...] = proj(4)
        gp_ref[...] = proj(5)

    head_block = pl.BlockSpec((N_HEADS, tile, LANE), lambda i: (0, i, 0))
    tok = lambda width: pl.BlockSpec((tile, width), lambda i: (i, 0))
    whole = lambda a: pl.BlockSpec(a.shape, lambda i: (0,) * a.ndim)
    padded = jax.ShapeDtypeStruct((N_HEADS, seq, LANE), BF16)
    half = jax.ShapeDtypeStruct((seq, D_ATT), F32)
    return pl.pallas_call(
        body, name="inproj_forward", grid=(nt,),
        in_specs=[tok(D), whole(mod), whole(w_main), whole(w_f), whole(b_main), whole(b_f)],
        out_specs=[head_block, head_block, head_block, tok(LANE), tok(D_POOL), tok(D_ATT), tok(D_POOL),
                   tok(D)],
        out_shape=[padded, padded, padded, jax.ShapeDtypeStruct((seq, LANE), F32), half, half, half,
                   jax.ShapeDtypeStruct((seq, D), BF16)],
        scratch_shapes=[pltpu.VMEM((8, LANE), F32)],
        compiler_params=_params(("arbitrary",)),
    )(x, mod, w_main, w_f, b_main, b_f)


def _attention_forward(qp, kp, vp, w_out, tile):
    seq = qp.shape[1]
    nb = seq // tile
    steps = (N_HEADS // 2) * nb

    def body(q_ref, k_ref, v_ref, wo_ref, att_ref, q2t_ref, wo_all_ref, s_a, s_b, m_ref, acc_ref,
             send_sems, recv_sems, local_sem):
        step = pl.program_id(0) * nb + pl.program_id(1)
        start, forward, finish = _gather_stages(wo_ref, wo_all_ref, send_sems, recv_sems, local_sem.at[0])
        pl.when(step == 0)(start)
        pl.when(step == steps // 2)(forward)

        i = pl.program_id(1)
        sub = lax.broadcasted_iota(jnp.int32, (LANE, tile), 0)
        row = lax.broadcasted_iota(jnp.int32, (tile, tile), 0)
        col = lax.broadcasted_iota(jnp.int32, (tile, tile), 1)
        q = [q_ref[0], q_ref[1]]

        def scores(buf, kb):
            rows = pl.ds(pl.multiple_of(kb * tile, tile), tile)
            for hh in range(2):
                buf[hh] = _dot_nt(k_ref[hh, rows, :], q[hh])

        def absorb(buf, kb, masked):
            rows = pl.ds(pl.multiple_of(kb * tile, tile), tile)
            for hh in range(2):
                m = m_ref[hh, 0:1, :]
                s = buf[hh]
                if masked:
                    s = jnp.where(row <= col, s, -1e30)
                m_new = jnp.maximum(m, jnp.max(s, axis=0, keepdims=True))
                p = jnp.exp(s - m_new).astype(BF16)
                acc_ref[hh] = jnp.exp(m - m_new) * acc_ref[hh] + _dot_tn(v_ref[hh, rows, :], p)
                m_ref[hh, 0:1, :] = m_new

        def two_blocks(j, _):
            scores(s_b, 2 * j + 1)
            absorb(s_a, 2 * j, False)
            scores(s_a, 2 * j + 2)
            absorb(s_b, 2 * j + 1, False)
            return 0

        def last_block():
            absorb(s_a, i, True)

        def last_two_blocks():
            scores(s_b, i)
            absorb(s_a, i - 1, False)
            absorb(s_b, i, True)

        scores(s_a, 0)
        m_ref[...] = jnp.full(m_ref.shape, -1e30, F32)
        acc_ref[...] = jnp.zeros_like(acc_ref)
        lax.fori_loop(0, i // 2, two_blocks, 0)
        lax.cond(i % 2 == 0, last_block, last_two_blocks)
        outs = []
        for hh in range(2):
            m, acc = m_ref[hh, 0:1, :], acc_ref[hh]
            l = -acc[HEAD_DIM:HEAD_DIM + 1, :]
            outs.append((acc / l)[:HEAD_DIM, :])
            neg_lse = [part.astype(F32) for part in _split3(-(m + jnp.log(l)))]
            q2t_ref[hh] = _place3(sub, L_LSE, neg_lse, q[hh].astype(F32).T).astype(BF16)
        att_ref[...] = jnp.concatenate(outs, axis=0).T
        pl.when(step == steps - 1)(finish)

    pair = pl.BlockSpec((2, tile, LANE), lambda hp, i: (hp, i, 0))
    full = pl.BlockSpec((2, seq, LANE), lambda hp, i: (hp, 0, 0))
    return pl.pallas_call(
        body, name="attention_forward", grid=(N_HEADS // 2, nb),
        in_specs=[pair, full, full, ANY],
        out_specs=[pl.BlockSpec((tile, LANE), lambda hp, i: (i, hp)),
                   pl.BlockSpec((2, LANE, tile), lambda hp, i: (hp, 0, i)), ANY],
        out_shape=[jax.ShapeDtypeStruct((seq, D_ATT), F32),
                   jax.ShapeDtypeStruct((N_HEADS, LANE, seq), BF16),
                   jax.ShapeDtypeStruct((N_DEV,) + w_out.shape, w_out.dtype)],
        scratch_shapes=[pltpu.VMEM((2, tile, tile), F32), pltpu.VMEM((2, tile, tile), F32),
                        pltpu.VMEM((2, 8, tile), F32), pltpu.VMEM((2, LANE, tile), F32),
                        pltpu.SemaphoreType.DMA((N_GATHER_SEMS,)), pltpu.SemaphoreType.DMA((N_GATHER_SEMS,)),
                        pltpu.SemaphoreType.DMA((1,))],
        compiler_params=_params(("arbitrary", "arbitrary")),
    )(qp, kp, vp, w_out)


def _window_sum(x, halo, window, transposed):
    tile = x.shape[0]

    def split_cat(a):
        hi = a.astype(BF16)
        return jnp.concatenate([hi, (a - hi.astype(F32)).astype(BF16)], axis=1)

    def fold(r):
        return r[:, :LANE] + r[:, LANE:]

    r = lax.broadcasted_iota(jnp.int32, (tile, tile), 0)
    c = lax.broadcasted_iota(jnp.int32, (tile, tile), 1)
    rh = lax.broadcasted_iota(jnp.int32, (HALO, HALO), 0)
    ch = lax.broadcasted_iota(jnp.int32, (HALO, HALO), 1)
    if not transposed:
        band = (c <= r) & (r - c < window)
        edge = (rh + HALO - ch) < window
    else:
        band = (r <= c) & (c - r < window)
        edge = (HALO + ch - rh) < window
    out = fold(_dot(band.astype(BF16), split_cat(x)))
    reach = fold(_dot(edge.astype(BF16), split_cat(halo)))
    if not transposed:
        return jnp.concatenate([out[:HALO] + reach, out[HALO:]], axis=0)
    return jnp.concatenate([out[:tile - HALO], out[tile - HALO:] + reach], axis=0)


def _silu_parts(g):
    sig = _sigmoid(g)
    return g * sig, sig * (1.0 + g * (1.0 - sig))


def _middle(x, tgt, att, g_att, g_pool, p, vecs, pool_vecs, w_out, w_pool, tile):
    seq = x.shape[0]
    nt = seq // tile
    halo_blocks = tile // HALO

    def body(x_ref, tgt_ref, att_ref, ga_ref, gp_ref, p_ref, ph_ref, vec_ref, pvec_ref, wo_ref, wp_ref,
             dxa_ref, do2_ref, dga_ref, dgp_ref, dpooled_ref, dwo_ref, dwp_ref, dvec_ref, dpvec_ref):
        i = pl.program_id(0)

        @pl.when(i == 0)
        def _():
            dwo_ref[...] = jnp.zeros_like(dwo_ref)
            dwp_ref[...] = jnp.zeros_like(dwp_ref)
            dvec_ref[...] = jnp.zeros_like(dvec_ref)
            dpvec_ref[...] = jnp.zeros_like(dpvec_ref)

        gate, b_out, ln_g, ln_b = (vec_ref[k:k + 1, :] for k in range(4))
        b_pool, pool_scale = pvec_ref[0:1, :], pvec_ref[1:2, :]
        x = x_ref[...]
        p = p_ref[...]
        p_halo = ph_ref[...] * jnp.where(i > 0, 1.0, 0.0)
        pos = i * tile + lax.broadcasted_iota(jnp.int32, (tile, 1), 0) + 1

        pooled, mixed = [], []
        for g, window in enumerate(POOL_WINDOWS):
            cols = slice(g * GROUP_DIM, (g + 1) * GROUP_DIM)
            wsum = _window_sum(p[:, cols], p_halo[:, cols], window, False)
            count = jnp.minimum(pos, window).astype(F32)
            pooled.append(wsum / count - p[:, cols])
            mixed.append(_dot(pooled[g].astype(BF16), wp_ref[g]) + b_pool[:, cols])
        mixed = jnp.concatenate(mixed, axis=1)
        pool = mixed * pool_scale

        att = att_ref[...]
        g_att, g_pool = ga_ref[...], gp_ref[...]
        silu_a, dsilu_a = _silu_parts(g_att)
        silu_p, dsilu_p = _silu_parts(g_pool)
        y_in = jnp.concatenate([att * silu_a, pool * silu_p], axis=1)
        y = _dot(y_in.astype(BF16), wo_ref[...]) + b_out
        h = ALPHA * x + gate * y
        mu = jnp.mean(h, axis=1, keepdims=True)
        hc = h - mu
        var = jnp.mean(hc * hc, axis=1, keepdims=True)
        rstd = lax.rsqrt(var + LN_EPS)
        yhat = hc * rstd
        diff = yhat * ln_g + ln_b - tgt_ref[...]
        loss_rows = jnp.sum(diff * diff, axis=1, keepdims=True)
        d_out = diff * (1.0 / D)

        d_yhat = d_out * ln_g
        dh = rstd * (d_yhat - jnp.mean(d_yhat, axis=1, keepdims=True)
                     - yhat * jnp.mean(d_yhat * yhat, axis=1, keepdims=True))
        dxa_ref[...] = ALPHA * dh
        dy = dh * gate
        dyb = dy.astype(BF16)
        lane = _lanes((1, D))
        loss_row = jnp.where(lane == 0, (0.5 / D) * jnp.sum(loss_rows, axis=0, keepdims=True), 0.0)
        dvec_ref[0:1, :] += jnp.sum(dh * y, axis=0, keepdims=True)
        dvec_ref[1:2, :] += jnp.sum(dy, axis=0, keepdims=True)
        dvec_ref[2:3, :] += jnp.sum(d_out * yhat, axis=0, keepdims=True)
        dvec_ref[3:4, :] += jnp.sum(d_out, axis=0, keepdims=True)
        dvec_ref[4:5, :] += loss_row

        dwo_ref[...] += _dot(y_in.T.astype(BF16), dyb)
        d_yin = _dot_nt(dyb, wo_ref[...])
        d_a, d_pl = d_yin[:, :D_ATT], d_yin[:, D_ATT:]
        d_att = d_a * silu_a
        d_att_t = d_att.T
        prod_t = (d_att * att).T
        sub = lax.broadcasted_iota(jnp.int32, (HEAD_DIM, tile), 0)
        for h in range(N_HEADS):
            rows = slice(h * HEAD_DIM, (h + 1) * HEAD_DIM)
            delta = jnp.sum(prod_t[rows], axis=0, keepdims=True)
            extra = _place3(sub, 0, [part.astype(F32) for part in _split3(delta)], 0.0)
            do2_ref[h] = jnp.concatenate([d_att_t[rows], extra], axis=0).astype(BF16)
        dga_ref[...] = d_a * att * dsilu_a
        dgp_ref[...] = d_pl * pool * dsilu_p
        d_pool = d_pl * silu_p
        d_mixed = d_pool * pool_scale
        dpvec_ref[0:1, :] += jnp.sum(d_mixed, axis=0, keepdims=True)
        dpvec_ref[1:2, :] += jnp.sum(d_pool * mixed, axis=0, keepdims=True)
        d_pooled = []
        for g in range(len(POOL_WINDOWS)):
            cols = slice(g * GROUP_DIM, (g + 1) * GROUP_DIM)
            dmb = d_mixed[:, cols].astype(BF16)
            dwp_ref[g] += _dot(pooled[g].T.astype(BF16), dmb)
            d_pooled.append(_dot_nt(dmb, wp_ref[g]))
        dpooled_ref[...] = jnp.concatenate(d_pooled, axis=1)

    tok = lambda width: pl.BlockSpec((tile, width), lambda i: (i, 0))
    whole = lambda a: pl.BlockSpec(a.shape, lambda i: (0,) * a.ndim)
    halo = pl.BlockSpec((HALO, D_POOL), lambda i: (jnp.maximum(i * halo_blocks - 1, 0), 0))
    half = jax.ShapeDtypeStruct((seq, D_ATT), F32)
    outs = [jax.ShapeDtypeStruct((seq, D), F32), jax.ShapeDtypeStruct((N_HEADS, LANE, seq), BF16), half, half, half,
            jax.ShapeDtypeStruct(w_out.shape, F32), jax.ShapeDtypeStruct(w_pool.shape, F32),
            jax.ShapeDtypeStruct(vecs.shape, F32), jax.ShapeDtypeStruct(pool_vecs.shape, F32)]
    return pl.pallas_call(
        body, name="middle", grid=(nt,),
        in_specs=[tok(D), tok(D), tok(D_ATT), tok(D_ATT), tok(D_POOL), tok(D_POOL), halo,
                  whole(vecs), whole(pool_vecs), whole(w_out), whole(w_pool)],
        out_specs=[tok(D), pl.BlockSpec((N_HEADS, LANE, tile), lambda i: (0, 0, i)),
                   tok(D_ATT), tok(D_POOL), tok(D_POOL),
                   whole(w_out), whole(w_pool), whole(vecs), whole(pool_vecs)],
        out_shape=outs,
        compiler_params=_params(("arbitrary",)),
    )(x, tgt, att, g_att, g_pool, p, p, vecs, pool_vecs, w_out, w_pool)


def _attention_backward(q2t, kp, vp, do2t, gw_out, small, tile):
    seq = kp.shape[1]
    nb = seq // tile
    last = N_HEADS // 2 - 1

    def body(qt_ref, k_ref, v_ref, dot_ref, gwo_hbm, small_hbm,
             dq_ref, dk_ref, dv_ref, dcum_ref, g_out_ref, total_ref,
             dq_acc, dk_acc, dv_acc, gwo_ref, r1, s2, r2, sm_sib, sm_chip, sm_recv, small_ref, send_sems, recv_sems):
        hp = pl.program_id(0)
        start, middle, finish = _reduce_stages(
            [gwo_ref], [g_out_ref], [r1], [s2], [r2], (small_ref, total_ref, sm_sib, sm_chip, sm_recv),
            send_sems, recv_sems)

        @pl.when(hp == 0)
        def _():
            pltpu.sync_copy(gwo_hbm, gwo_ref)
            pltpu.sync_copy(small_hbm, small_ref)
            start()

        pl.when(hp == 1)(middle)

        row = lax.broadcasted_iota(jnp.int32, (tile, tile), 0)
        col = lax.broadcasted_iota(jnp.int32, (tile, tile), 1)
        dq_acc[...] = jnp.zeros_like(dq_acc)

        def kv_block(kb, _):
            krows = pl.ds(pl.multiple_of(kb * tile, tile), tile)
            k = [k_ref[hh, krows, :] for hh in range(2)]
            v = [v_ref[hh, krows, :] for hh in range(2)]
            k_t = [k[hh].T for hh in range(2)]

            def q_block(qb, masked):
                qcols = pl.ds(pl.multiple_of(qb * tile, tile), tile)
                for hh in range(2):
                    q_t = qt_ref[hh, :, qcols]
                    do_t = dot_ref[hh, :, qcols]
                    s_t = _dot(k[hh], q_t)
                    if masked:
                        s_t = jnp.where(row <= col, s_t, -1e30)
                    p_t = jnp.exp(s_t)
                    ds_t = (p_t * _dot(v[hh], do_t)).astype(BF16)
                    dv_new = _dot_nt(do_t, p_t.astype(BF16))
                    dk_new = _dot_nt(q_t, ds_t)
                    if masked:
                        dv_acc[hh], dk_acc[hh] = dv_new, dk_new
                    else:
                        dv_acc[hh] += dv_new
                        dk_acc[hh] += dk_new
                    dq_acc[hh, :, qcols] += _dot(k_t[hh], ds_t)

            q_block(kb, True)

            def two_later_blocks(j, _):
                q_block(kb + 1 + 2 * j, False)
                q_block(kb + 2 + 2 * j, False)
                return 0

            later = nb - 1 - kb
            lax.fori_loop(0, later // 2, two_later_blocks, 0)
            pl.when(later % 2 == 1)(lambda: q_block(nb - 1, False))
            for hh in range(2):
                dk = dk_acc[hh]
                dk_ref[hh, :, krows] = dk.astype(BF16)
                dv_ref[hh, :, krows] = dv_acc[hh].astype(BF16)
                dcum_ref[hh, :, krows] = -dk[L_CK:L_CK + 1, :]
            return 0

        lax.fori_loop(0, nb, kv_block, 0)
        for hh in range(2):
            dq = dq_acc[hh]
            dcum_ref[hh] += dq[L_CQ:L_CQ + 1, :]
            dq_ref[hh] = (dq * Q_SCALE).astype(BF16)
        pl.when(hp == last)(finish)

    pair = pl.BlockSpec((2, seq, LANE), lambda hp: (hp, 0, 0))
    pair_t = pl.BlockSpec((2, LANE, seq), lambda hp: (hp, 0, 0))
    whole = lambda shape: pl.BlockSpec(shape, lambda hp: (0,) * len(shape))
    grad = jax.ShapeDtypeStruct((N_HEADS, LANE, seq), BF16)
    return pl.pallas_call(
        body, name="attention_backward", grid=(N_HEADS // 2,),
        in_specs=[pair_t, pair, pair, pair_t, ANY, ANY],
        out_specs=[pair_t, pair_t, pair_t, pl.BlockSpec((2, 1, seq), lambda hp: (hp, 0, 0)),
                   whole(gw_out.shape[1:]), whole(small.shape)],
        out_shape=[grad, grad, grad, jax.ShapeDtypeStruct((N_HEADS, 1, seq), F32),
                   jax.ShapeDtypeStruct(gw_out.shape[1:], F32), jax.ShapeDtypeStruct(small.shape, F32)],
        scratch_shapes=[pltpu.VMEM((2, LANE, seq), F32), pltpu.VMEM((2, LANE, tile), F32),
                        pltpu.VMEM((2, LANE, tile), F32), pltpu.VMEM(gw_out.shape, BF16)]
        + _reduce_scratch(gw_out, small)
        + [pltpu.VMEM(small.shape, F32),
           pltpu.SemaphoreType.DMA((N_REDUCE_SEMS + N_SMALL_SEMS,)), pltpu.SemaphoreType.DMA((N_REDUCE_SEMS + N_SMALL_SEMS,))],
        compiler_params=_params(("arbitrary",)),
    )(q2t, kp, vp, do2t, gw_out, small)


def _inproj_backward(dqp, dkp, dvp, d_cum, f, d_pooled, d_ga, d_gp, x, dxa, u, mod, w_main, w_f, tile):
    seq = x.shape[0]
    nt = seq // tile
    halo_blocks = tile // HALO

    def body(dq_ref, dk_ref, dv_ref, dcum_ref, f_ref, dpo_ref, dph_ref, dga_ref, dgp_ref, x_ref, dxa_ref, u_ref,
             mod_ref, w_ref, wf_ref,
             dx_ref, dproj_ref, dwf_ref, db_ref, dbf_ref, dmod_ref, carry_ref):
        step = pl.program_id(0)
        i = nt - 1 - step

        @pl.when(step == 0)
        def _():
            carry_ref[...] = jnp.zeros_like(carry_ref)
            dwf_ref[...] = jnp.zeros_like(dwf_ref)
            db_ref[...] = jnp.zeros_like(db_ref)
            dbf_ref[...] = jnp.zeros_like(dbf_ref)
            dmod_ref[...] = jnp.zeros_like(dmod_ref)

        ones = jnp.ones((8, tile), BF16)

        def emit(chunk, val):
            cols = pl.ds(chunk * COL_CHUNK, COL_CHUNK)
            db_ref[0:1, cols] += jnp.sum(val, axis=0, keepdims=True)
            vb = val.astype(BF16)
            dproj_ref[:, pl.ds((chunk - 3) * COL_CHUNK, COL_CHUNK)] = vb
            return _dot(vb, w_ref[cols, :])

        d_u = jnp.zeros((tile, D), F32)
        for chunk, ref in enumerate((dq_ref, dk_ref, dv_ref)):
            cols = pl.ds(chunk * COL_CHUNK, COL_CHUNK)
            val_t = ref[:, 0:HEAD_DIM, :].reshape(COL_CHUNK, tile)
            db_ref[:, cols] += _dot_nt(ones, val_t)
            d_u += _dot_tn(val_t, w_ref[cols, :])

        d_pooled = dpo_ref[...]
        d_halo = dph_ref[...] * jnp.where(i < nt - 1, 1.0, 0.0)
        pos = i * tile + lax.broadcasted_iota(jnp.int32, (tile, 1), 0) + 1
        d_p = []
        for g, window in enumerate(POOL_WINDOWS):
            cols = slice(g * GROUP_DIM, (g + 1) * GROUP_DIM)
            scaled = d_pooled[:, cols] / jnp.minimum(pos, window).astype(F32)
            d_p.append(_window_sum(scaled, d_halo[:, cols] * (1.0 / window), window, True) - d_pooled[:, cols])
        d_u += emit(3, jnp.concatenate(d_p, axis=1))
        d_u += emit(4, dga_ref[...])
        d_u += emit(5, dgp_ref[...])

        row = lax.broadcasted_iota(jnp.int32, (tile, tile), 0)
        col = lax.broadcasted_iota(jnp.int32, (tile, tile), 1)
        later = (row >= col).astype(BF16)
        d_logf = sum(_dot(part, later) for part in _split3(dcum_ref[:, 0, :])) + carry_ref[:, 0:1]
        carry_ref[:, 0:1] = d_logf[:, 0:1]
        d_f = d_logf * _sigmoid(-f_ref[...].T[0:N_HEADS, :])
        d_f = jnp.concatenate([d_f, jnp.zeros((LANE - N_HEADS, tile), F32)], axis=0)
        dbf_ref[...] += sum(_dot_nt(ones, part) for part in _split3(d_f))
        d_fb = d_f.astype(BF16)
        d_u += _dot_tn(d_fb, wf_ref[...])
        dwf_ref[...] += _dot(d_fb, u_ref[...])

        x = x_ref[...]
        dx_ref[...] = dxa_ref[...] + d_u * mod_ref[0:1, :]
        dmod_ref[0:1, :] += jnp.sum(d_u * x, axis=0, keepdims=True)
        dmod_ref[1:2, :] += jnp.sum(d_u, axis=0, keepdims=True)

    rev = lambda step: nt - 1 - step
    tok = lambda width: pl.BlockSpec((tile, width), lambda s: (rev(s), 0))
    head_block = pl.BlockSpec((N_HEADS, LANE, tile), lambda s: (0, 0, rev(s)))
    whole = lambda a: pl.BlockSpec(a.shape, lambda s: (0,) * a.ndim)
    halo = pl.BlockSpec((HALO, D_POOL), lambda s: (jnp.minimum((rev(s) + 1) * halo_blocks, seq // HALO - 1), 0))
    small = lambda width: jax.ShapeDtypeStruct((8, width), F32)
    n_rest = N_MAIN - OFF_P
    return pl.pallas_call(
        body, name="inproj_backward", grid=(nt,),
        in_specs=[head_block, head_block, head_block, pl.BlockSpec((N_HEADS, 1, tile), lambda s: (0, 0, rev(s))),
                  tok(LANE), tok(D_POOL), halo, tok(D_ATT), tok(D_POOL),
                  tok(D), tok(D), tok(D),
                  whole(mod), whole(w_main), whole(w_f)],
        out_specs=[tok(D), tok(n_rest), pl.BlockSpec((LANE, D), lambda s: (0, 0)),
                   pl.BlockSpec((8, N_MAIN), lambda s: (0, 0)), pl.BlockSpec((8, LANE), lambda s: (0, 0)),
                   pl.BlockSpec((8, D), lambda s: (0, 0))],
        out_shape=[jax.ShapeDtypeStruct((seq, D), F32), jax.ShapeDtypeStruct((seq, n_rest), BF16),
                   jax.ShapeDtypeStruct((LANE, D), F32), small(N_MAIN), small(LANE), small(D)],
        scratch_shapes=[pltpu.VMEM((8, LANE), F32)],
        compiler_params=_params(("arbitrary",)),
    )(dqp, dkp, dvp, d_cum, f, d_pooled, d_pooled, d_ga, d_gp, x, dxa, u, mod, w_main, w_f)


def _weight_grad(dproj, u, k_tile):
    seq, n_cols = dproj.shape
    nk = seq // k_tile

    def body(dp_ref, u_ref, out_ref):
        @pl.when(pl.program_id(1) == 0)
        def _():
            out_ref[...] = jnp.zeros_like(out_ref)

        out_ref[...] += _dot_tn(dp_ref[...], u_ref[...])

    return pl.pallas_call(
        body, name="weight_grad", grid=(n_cols // COL_CHUNK, nk),
        in_specs=[pl.BlockSpec((k_tile, COL_CHUNK), lambda n, k: (k, n)),
                  pl.BlockSpec((k_tile, D), lambda n, k: (k, 0))],
        out_specs=pl.BlockSpec((COL_CHUNK, D), lambda n, k: (n, 0)),
        out_shape=jax.ShapeDtypeStruct((n_cols, D), F32),
        compiler_params=_params(("arbitrary", "arbitrary")),
    )(dproj, u)


def _weight_grad_heads(grad_t, u, k_tile, name):
    seq = u.shape[0]
    nk = seq // k_tile

    def body(g_ref, u_ref, out_ref):
        @pl.when(pl.program_id(0) == 0)
        def _():
            out_ref[...] = jnp.zeros_like(out_ref)

        out_ref[...] += _dot(g_ref[...].reshape(N_HEADS * HEAD_DIM, k_tile), u_ref[...])

    return pl.pallas_call(
        body, name=name, grid=(nk,),
        in_specs=[pl.BlockSpec((N_HEADS, HEAD_DIM, k_tile), lambda k: (0, 0, k)),
                  pl.BlockSpec((k_tile, D), lambda k: (k, 0))],
        out_specs=pl.BlockSpec((N_HEADS * HEAD_DIM, D), lambda k: (0, 0)),
        out_shape=jax.ShapeDtypeStruct((N_HEADS * HEAD_DIM, D), F32),
        compiler_params=_params(("arbitrary",)),
    )(grad_t, u)


def _adamw(w, g, m, v):
    m = ADAM_B1 * m + (1.0 - ADAM_B1) * g
    v = ADAM_B2 * v + (1.0 - ADAM_B2) * (g * g)
    m_hat = m / (1.0 - ADAM_B1 ** ADAM_STEP)
    v_hat = v / (1.0 - ADAM_B2 ** ADAM_STEP)
    delta = -ADAM_LR * (m_hat / (jnp.sqrt(v_hat) + ADAM_EPS) + ADAM_WD * w)
    return delta, m, v


def _adamw_call(g, w, m, v, lead_tile, name):
    nr = w.shape[0] // lead_tile

    def body(gi_ref, w_ref, m_ref, v_ref, g_ref, d_ref, nm_ref, nv_ref):
        g = gi_ref[...]
        g_ref[...] = g
        d_ref[...], nm_ref[...], nv_ref[...] = _adamw(w_ref[...], g, m_ref[...], v_ref[...])

    blk = pl.BlockSpec((lead_tile,) + w.shape[1:], lambda r: (r,) + (0,) * (w.ndim - 1))
    shape = jax.ShapeDtypeStruct(w.shape, F32)
    return pl.pallas_call(
        body, name=name, grid=(nr,),
        in_specs=[blk, blk, blk, blk], out_specs=[blk, blk, blk, blk],
        out_shape=[shape, shape, shape, shape],
        compiler_params=_params(("arbitrary",)),
    )(g, w, m, v)


def _sum_adamw(parts, w, m, v, row_tile, name):
    rows, cols = w.shape
    nr = rows // row_tile

    def body(parts_ref, w_ref, m_ref, v_ref, g_ref, d_ref, nm_ref, nv_ref):
        g = parts_ref[0]
        for k in range(1, N_DEV):
            g = g + parts_ref[k]
        g_ref[...] = g
        d_ref[...], nm_ref[...], nv_ref[...] = _adamw(w_ref[...], g, m_ref[...], v_ref[...])

    blk = pl.BlockSpec((row_tile, cols), lambda r: (r, 0))
    shape = jax.ShapeDtypeStruct(w.shape, F32)
    return pl.pallas_call(
        body, name=name, grid=(nr,),
        in_specs=[pl.BlockSpec((N_DEV, row_tile, cols), lambda r: (0, r, 0)), blk, blk, blk],
        out_specs=[blk, blk, blk, blk],
        out_shape=[shape, shape, shape, shape],
        compiler_params=_params(("arbitrary",)),
    )(parts, w, m, v)


def _ada_adamw(sc_t, d_ada, w, m, v):
    def body(sc_ref, d_ref, w_ref, m_ref, v_ref, g_ref, dl_ref, nm_ref, nv_ref):
        g = sc_ref[:, 0:1] * d_ref[0:1, :]
        for b in range(1, N_DEV):
            g = g + sc_ref[:, b:b + 1] * d_ref[b:b + 1, :]
        g_ref[...] = g
        dl_ref[...], nm_ref[...], nv_ref[...] = _adamw(w_ref[...], g, m_ref[...], v_ref[...])

    shape = jax.ShapeDtypeStruct(w.shape, F32)
    return pl.pallas_call(
        body, name="ada_adamw", out_shape=[shape, shape, shape, shape], compiler_params=_params(),
    )(sc_t, d_ada, w, m, v)


F_LO, F_HI = 3 * D_ATT, 3 * D_ATT + N_HEADS


def _split_forget(a, axis):
    idx = lambda lo, hi: tuple(slice(lo, hi) if d == axis else slice(None) for d in range(a.ndim))
    pad = [(0, LANE - N_HEADS) if d == axis else (0, 0) for d in range(a.ndim)]
    return jnp.concatenate([a[idx(0, F_LO)], a[idx(F_HI, D_IN)]], axis=axis), jnp.pad(a[idx(F_LO, F_HI)], pad)


def _join_forget(main, f, axis):
    idx = lambda lo, hi: tuple(slice(lo, hi) if d == axis else slice(None) for d in range(main.ndim))
    return jnp.concatenate([main[idx(0, F_LO)], f[idx(0, N_HEADS)], main[idx(F_LO, N_MAIN)]], axis=axis)


_EARLY = ((("w_pool", 65536), ("b_pool", 512), ("pool_scale", 512), ("b_out", 1024), ("ln_g", 1024), ("ln_b", 1024),
           ("loss", 128)), 552)
_LATE = ((("b_in", 3200),), 32)


def _pack_small(parts, layout):
    spec, rows = layout
    flat = []
    for name, size in spec:
        a = parts[name].reshape(-1)
        flat.append(jnp.pad(a, (0, size - a.shape[0])))
    flat = jnp.concatenate(flat)
    flat = jnp.pad(flat, (0, rows * LANE - flat.shape[0]))
    return flat.reshape(rows, LANE)


def _unpack_small(packed, shapes, layout):
    flat = packed.reshape(-1)
    out, off = {}, 0
    for name, size in layout[0]:
        n = 1
        for s in shapes[name]:
            n *= s
        out[name] = flat[off:off + n].reshape(shapes[name])
        off += size
    return out


def kernel(x, c, w_ada, b_ada, w_in, b_in, w_pool_mix, b_pool_mix, pool_scale, w_out, b_out, ln_g, ln_b, loss_target, m_w_ada, m_b_ada, m_w_in, m_b_in, m_w_pool_mix, m_b_pool_mix, m_pool_scale, m_w_out, m_b_out, m_ln_g, m_ln_b, v_w_ada, v_b_ada, v_w_in, v_b_in, v_w_pool_mix, v_b_pool_mix, v_pool_scale, v_w_out, v_b_out, v_ln_g, v_ln_b):
    seq = x.shape[1]
    tile = min(256, seq)
    attn_tile = min(512, max(128, seq // 4))
    me = _dev_index(*_mesh_pos())
    x2, tgt = x[0], loss_target[0]

    rows_in = D_IN // N_DEV
    w_in_g, sc_all, ada_mine = _gather_and_ada(c, w_in[0].T.astype(BF16), w_ada[0])
    ada = ada_mine.reshape(1, D_ADA) + b_ada
    shift, scale, gate = ada[:, 0:D], ada[:, D:2 * D], ada[:, 2 * D:]
    mod = jnp.concatenate([1.0 + scale, shift, jnp.zeros((6, D), F32)], axis=0)

    w_main, w_f = _split_forget(w_in_g.reshape(D_IN, D), 0)
    b_main, b_f = _split_forget(b_in, 1)

    qp, kp, vp, f, p, g_att, g_pool, u = _inproj_forward(x2, mod, w_main, w_f, b_main, b_f, tile)
    att, q2t, w_out_g = _attention_forward(qp, kp, vp, w_out[0].astype(BF16), attn_tile)

    vecs = jnp.concatenate([gate, b_out, ln_g, ln_b, jnp.zeros((4, D), F32)], axis=0)
    pool_vecs = jnp.concatenate([b_pool_mix.reshape(1, D_POOL), pool_scale, jnp.zeros((6, D_POOL), F32)], axis=0)
    dxa, do2, d_ga, d_gp, d_pooled, dw_out, dw_pool, dvec, dpvec = _middle(
        x2, tgt, att, g_att, g_pool, p, vecs, pool_vecs, w_out_g.reshape(D, D), w_pool_mix[0].astype(BF16), tile)

    gw_out = dw_out.reshape(N_DEV, D // N_DEV, D).astype(BF16)
    early = _pack_small({"w_pool": dw_pool, "b_pool": dpvec[0:1], "pool_scale": dpvec[1:2], "b_out": dvec[1:2],
                         "ln_g": dvec[2:3], "ln_b": dvec[3:4], "loss": dvec[4:5, 0:LANE]}, _EARLY)
    dqp, dkp, dvp, d_cum, g_out, early_sum = _attention_backward(q2t, kp, vp, do2, gw_out, early, attn_tile)
    dx, dproj, dw_f, db_main, db_f, dmod = _inproj_backward(
        dqp, dkp, dvp, d_cum, f, d_pooled, d_ga, d_gp, x2, dxa, u, mod, w_main, w_f, tile)
    k_tile = min(1024, seq)
    dw_q, dw_k, dw_v = (_weight_grad_heads(g, u, k_tile, "weight_grad_" + n)
                        for g, n in ((dqp, "q"), (dkp, "k"), (dvp, "v")))
    dw_rest = _weight_grad(dproj, u, k_tile)

    dw_main = jnp.concatenate([dw_q, dw_k, dw_v, dw_rest], axis=0)
    gw_in = _join_forget(dw_main, dw_f, 0).reshape(N_DEV, rows_in, D).astype(BF16)
    d_ada = jnp.concatenate([dmod[1:2], dmod[0:1], dvec[0:1]], axis=1)
    late = _pack_small({"b_in": _join_forget(db_main[0:1], db_f[0:1], 1)}, _LATE)
    g_in_rows, late_sum, d_ada_all = _reduce_grads(gw_in, late, d_ada.reshape(D_ADA // LANE, LANE))

    def rows3(a):
        return a[0].T.reshape(rows_in, D // LANE, LANE)

    outs_in = _adamw_call(g_in_rows.reshape(rows_in, D // LANE, LANE), rows3(w_in), rows3(m_w_in), rows3(v_w_in),
                          rows_in // 5, "adamw_w_in")
    g_w_in, d_w_in, nm_w_in, nv_w_in = (a.reshape(rows_in, D).T for a in outs_in)
    g_w_out, d_w_out, nm_w_out, nv_w_out = _adamw_call(g_out, w_out[0], m_w_out[0], v_w_out[0], D // N_DEV, "adamw_w_out")

    def small_adamw(grad_sum, weights, first, second, layout, name):
        packed = _adamw_call(grad_sum, _pack_small(weights, layout), _pack_small(first, layout),
                             _pack_small(second, layout), layout[1], name)
        shapes = {k: a.shape for k, a in weights.items()}
        return [_unpack_small(a, shapes, layout) for a in packed]

    zero = jnp.zeros((1,), F32)
    smalls = small_adamw(
        early_sum,
        {"w_pool": w_pool_mix, "b_pool": b_pool_mix, "pool_scale": pool_scale, "b_out": b_out, "ln_g": ln_g,
         "ln_b": ln_b, "loss": zero},
        {"w_pool": m_w_pool_mix, "b_pool": m_b_pool_mix, "pool_scale": m_pool_scale, "b_out": m_b_out,
         "ln_g": m_ln_g, "ln_b": m_ln_b, "loss": zero},
        {"w_pool": v_w_pool_mix, "b_pool": v_b_pool_mix, "pool_scale": v_pool_scale, "b_out": v_b_out,
         "ln_g": v_ln_g, "ln_b": v_ln_b, "loss": zero}, _EARLY, "adamw_small")
    b_ins = small_adamw(late_sum, {"b_in": b_in}, {"b_in": m_b_in}, {"b_in": v_b_in}, _LATE, "adamw_b_in")
    g_s, d_s, nm_s, nv_s = ({**a, **b} for a, b in zip(smalls, b_ins))
    loss = g_s["loss"][0]

    ada_rows = D_ADA // LANE
    b_ada_outs = _sum_adamw(d_ada_all, b_ada.reshape(ada_rows, LANE), m_b_ada.reshape(ada_rows, LANE),
                            v_b_ada.reshape(ada_rows, LANE), ada_rows, "adamw_b_ada")
    g_b_ada, d_b_ada, nm_b_ada, nv_b_ada = (a.reshape(1, D_ADA) for a in b_ada_outs)
    d_ada_local = lax.dynamic_slice_in_dim(d_ada_all.reshape(N_DEV, D_ADA), me * (D_ADA // N_DEV), D_ADA // N_DEV, axis=1)
    g_w_ada, d_w_ada, nm_w_ada, nv_w_ada = _ada_adamw(sc_all.T, d_ada_local, w_ada[0], m_w_ada[0], v_w_ada[0])

    def ordered(w_ada_, b_ada_, w_in_, w_out_, s):
        return (w_ada_[None], b_ada_, w_in_[None], s["b_in"], s["w_pool"], s["b_pool"], s["pool_scale"],
                w_out_[None], s["b_out"], s["ln_g"], s["ln_b"])

    return (loss, dx[None],
            *ordered(g_w_ada, g_b_ada, g_w_in, g_w_out, g_s),
            *ordered(d_w_ada, d_b_ada, d_w_in, d_w_out, d_s),
            *ordered(nm_w_ada, nm_b_ada, nm_w_in, nm_w_out, nm_s),
            *ordered(nv_w_ada, nv_b_ada, nv_w_in, nv_w_out, nv_s))
```

```python
import jax
import jax.numpy as jnp
from jax import lax
from jax.experimental import pallas as pl
from jax.experimental.pallas import tpu as pltpu

F32 = jnp.float32
BF16 = jnp.bfloat16

N_DEV = 8
D = 1024
N_HEADS = 8
HEAD_DIM = 64
D_ATT = 512
D_POOL = 512
POOL_WINDOWS = (2, 4, 8, 16)
GROUP_DIM = 128
HALO = 16
LANE = 128
D_IN = 3080
D_ADA = 3072
N_MAIN = 3072
OFF_P = 1536
COL_CHUNK = 512
Q_SCALE = 0.125
LN_EPS = 1e-5
ALPHA = 2.0 ** 0.25
L_CQ, L_CK, L_LSE = 64, 67, 70

ADAM_LR, ADAM_B1, ADAM_B2, ADAM_EPS, ADAM_WD, ADAM_STEP = 0.001, 0.9, 0.999, 1e-08, 0.01, 10
VMEM_LIMIT = 56 * 1024 * 1024

MESH = pl.DeviceIdType.MESH
ANY = pl.BlockSpec(memory_space=pl.ANY)


def _params(sem=None, vmem=VMEM_LIMIT):
    return pltpu.CompilerParams(dimension_semantics=sem, vmem_limit_bytes=vmem)


def _split3(a):
    hi = a.astype(BF16)
    r = a - hi.astype(F32)
    mid = r.astype(BF16)
    lo = (r - mid.astype(F32)).astype(BF16)
    return hi, mid, lo


def _dot(a, b):
    return jnp.dot(a, b, preferred_element_type=F32)


def _dot_nt(a, b):
    return lax.dot_general(a, b, (((1,), (1,)), ((), ())), preferred_element_type=F32)


def _dot_tn(a, b):
    return lax.dot_general(a, b, (((0,), (0,)), ((), ())), preferred_element_type=F32)


def _dot3(m01, a):
    hi, mid, lo = _split3(a)
    return _dot(m01, hi) + _dot(m01, mid) + _dot(m01, lo)


def _sigmoid(z):
    return 1.0 / (1.0 + jnp.exp(-z))


def _lanes(shape):
    return lax.broadcasted_iota(jnp.int32, shape, len(shape) - 1)


def _place3(lane, base, parts, other):
    out = other
    for j in range(3):
        out = jnp.where(lane == base + j, parts[j], out)
    return out


def _mesh_pos():
    return lax.axis_index("x"), lax.axis_index("y"), lax.axis_index("c")


def _dev_index(px, py, pc):
    return 4 * px + 2 * py + pc


N_GATHER_SEMS = 9


def _gather_stages(src_ref, out_ref, send_sems, recv_sems, local_sem):
    x, y, c = _mesh_pos()
    me, sibling = (x, y, c), (x, y, 1 - c)
    nbr_x, nbr_y, diag = (1 - x, y), (x, 1 - y), (1 - x, 1 - y)
    half = out_ref.shape[-1] // 2
    left, right = pl.ds(0, half), pl.ds(half, half)

    def copy(k, block, to, cols=None, src=None):
        slot = out_ref.at[_dev_index(*block)]
        if cols is not None:
            slot = slot.at[:, cols]
        return pltpu.make_async_remote_copy(
            src_ref=slot if src is None else src, dst_ref=slot, send_sem=send_sems.at[k], recv_sem=recv_sems.at[k],
            device_id=to, device_id_type=MESH)

    mine = pltpu.make_async_copy(src_ref, out_ref.at[_dev_index(*me)], local_sem)
    first = [copy(0, me, sibling, src=src_ref), copy(1, me, (*nbr_x, c), src=src_ref), copy(2, me, (*nbr_y, c), src=src_ref)]
    relay = [(1, nbr_x, None, nbr_x), (2, nbr_y, None, nbr_y), (3, diag, left, nbr_y), (4, diag, right, nbr_x)]
    onward = [copy(3, (*nbr_x, c), (*nbr_y, c), cols=left), copy(4, (*nbr_y, c), (*nbr_x, c), cols=right)]
    passed = [copy(4 + k, (*block, c), sibling, cols=cols) for k, block, cols, _ in relay]

    def start():
        mine.start()
        for cp in first:
            cp.start()

    def relay_stage(first_item):
        def run():
            for j in (first_item, first_item + 1):
                k, block, cols, frm = relay[j]
                copy(k, (*block, c), (*frm, c), cols=cols).wait_recv()
                if j < 2:
                    onward[j].start()
                passed[j].start()
        return run

    def finish():
        copy(0, sibling, me).wait_recv()
        for k, block, cols, _ in relay:
            copy(4 + k, (*block, 1 - c), me, cols=cols).wait_recv()
        for cp in first + onward + passed:
            cp.wait_send()
        mine.wait()

    return start, relay_stage(0), relay_stage(2), finish


N_REDUCE_SEMS = 7
N_SMALL_SEMS = 4 + 7


def _reduce_stages(ins, gs, r1, s2, r2, small, send_sems, recv_sems, rows=None):
    n = len(ins)
    x, y, c = _mesh_pos()
    me = _dev_index(x, y, c)
    sibling = (x, y, 1 - c)
    chips = [(x, y), (1 - x, y), (x, 1 - y), (1 - x, 1 - y)]
    peers = []
    for p in range(1, N_DEV):
        px, py, pc = (p >> 2) & 1, (p >> 1) & 1, p & 1
        peers.append((1 - x if px else x, 1 - y if py else y, 1 - c if pc else c))
    base_small = N_REDUCE_SEMS * n

    def remote(src, dst, k, to):
        return pltpu.make_async_remote_copy(src_ref=src, dst_ref=dst, send_sem=send_sems.at[k],
                                            recv_sem=recv_sems.at[k], device_id=to, device_id_type=MESH)

    def level1(a, q):
        return remote(ins[a].at[_dev_index(*chips[q], 1 - c)], r1[a].at[q], N_REDUCE_SEMS * a + q, sibling)

    def level2(a, j):
        return remote(s2[a].at[j], r2[a].at[j], N_REDUCE_SEMS * a + 4 + j, (*chips[j + 1], c))

    if small is not None:
        small_ref, total_ref, sm_sib, sm_chip, sm_recv = small
        to_sibling = remote(small_ref, sm_sib, base_small, sibling)
        to_chips = [remote(sm_chip, sm_recv.at[j], base_small + 1 + j, (*chips[j + 1], c)) for j in range(3)]
    if rows is not None:
        rows_ref, land_ref, all_ref = rows
        row_sends = [remote(rows_ref, land_ref.at[me], base_small + 4 + k, to) for k, to in enumerate(peers)]

    def start():
        for a in range(n):
            for q in range(4):
                level1(a, q).start()
        if small is not None:
            to_sibling.start()
        if rows is not None:
            for cp in row_sends:
                cp.start()
            land_ref[me] = rows_ref[...]

    def middle():
        for a in range(n):
            for q in (1, 2, 3, 0):
                level1(a, q).wait_recv()
                pair = ins[a][_dev_index(*chips[q], c)].astype(F32) + r1[a][q].astype(F32)
                if q == 0:
                    gs[a][...] = pair
                else:
                    s2[a][q - 1] = pair.astype(BF16)
                    level2(a, q - 1).start()
        if small is not None:
            to_sibling.wait_recv()
            sm_chip[...] = small_ref[...] + sm_sib[...]
            for cp in to_chips:
                cp.start()

    def finish():
        for a in range(n):
            own = gs[a][...]
            for j in range(3):
                level2(a, j).wait_recv()
                own = own + r2[a][j].astype(F32)
            gs[a][...] = own
            for q in range(4):
                level1(a, q).wait_send()
            for j in range(3):
                level2(a, j).wait_send()
        if small is not None:
            for cp in to_chips:
                cp.wait_recv()
            total = None
            for ax in range(2):
                for ay in range(2):
                    dx, dy = x != ax, y != ay
                    term = jnp.where(dx, jnp.where(dy, sm_recv[2], sm_recv[0]), jnp.where(dy, sm_recv[1], sm_chip[...]))
                    total = term if total is None else total + term
            total_ref[...] = total
            for cp in [to_sibling] + to_chips:
                cp.wait_send()
        if rows is not None:
            for k, frm in enumerate(peers):
                remote(rows_ref, land_ref.at[_dev_index(*frm)], base_small + 4 + k, frm).wait_recv()
            all_ref[...] = land_ref[...]
            for cp in row_sends:
                cp.wait_send()

    return start, middle, finish


def _reduce_scratch(shard, small, rows=None):
    out = [pltpu.VMEM((lead,) + shard.shape[1:], BF16) for lead in (4, 3, 3)]
    out += [pltpu.VMEM(small.shape, F32), pltpu.VMEM(small.shape, F32), pltpu.VMEM((3,) + small.shape, F32)]
    if rows is not None:
        out.append(pltpu.VMEM((N_DEV,) + rows.shape, F32))
    return out


def _reduce_grads(gw_in, small, rows):
    def body(in_ref, small_ref, rows_ref, g_ref, total_ref, rows_all_ref,
             r1, s2, r2, sm_sib, sm_chip, sm_recv, rows_land, send_sems, recv_sems):
        start, middle, finish = _reduce_stages(
            [in_ref], [g_ref], [r1], [s2], [r2], (small_ref, total_ref, sm_sib, sm_chip, sm_recv),
            send_sems, recv_sems, rows=(rows_ref, rows_land, rows_all_ref))
        start()
        middle()
        finish()

    return pl.pallas_call(
        body, name="reduce_grads",
        out_shape=[jax.ShapeDtypeStruct(gw_in.shape[1:], F32), jax.ShapeDtypeStruct(small.shape, F32),
                   jax.ShapeDtypeStruct((N_DEV,) + rows.shape, F32)],
        scratch_shapes=_reduce_scratch(gw_in, small, rows)
        + [pltpu.SemaphoreType.DMA((N_REDUCE_SEMS + N_SMALL_SEMS,))] * 2,
        compiler_params=_params(),
    )(gw_in, small, rows)


def _dot3_rhs(a, b):
    a0, a1, a2 = _split3(a)
    b0, b1, b2 = _split3(b)
    return (_dot(a0, b0) + (_dot(a0, b1) + _dot(a1, b0))
            + (_dot(a0, b2) + _dot(a1, b1) + _dot(a2, b0)))


def _gather_and_ada(c, w_in_rows, w_ada):
    cols = w_ada.shape[1]

    def body(c_ref, w_ref, wa_ref, w_all_ref, sc_ref, ada_ref,
             c_land, part, ada_land, send_sems, recv_sems, local_sem, x_send, x_recv):
        x, y, cc = _mesh_pos()
        me = _dev_index(x, y, cc)
        peers = []
        for p in range(1, N_DEV):
            px, py, pc = (p >> 2) & 1, (p >> 1) & 1, p & 1
            peers.append((1 - x if px else x, 1 - y if py else y, 1 - cc if pc else cc))

        def remote(src, dst, k, to):
            return pltpu.make_async_remote_copy(src_ref=src, dst_ref=dst, send_sem=x_send.at[k], recv_sem=x_recv.at[k],
                                                device_id=to, device_id_type=MESH)

        c_sends = [remote(c_ref, c_land.at[me], k, to) for k, to in enumerate(peers)]
        for cp in c_sends:
            cp.start()
        start, relay_near, relay_far, finish = _gather_stages(w_ref, w_all_ref, send_sems, recv_sems, local_sem.at[0])
        start()
        c_land[me] = c_ref[...]
        for k, frm in enumerate(peers):
            remote(c_ref, c_land.at[_dev_index(*frm)], k, frm).wait_recv()
        c_all = jnp.concatenate([c_land[b] for b in range(N_DEV)], axis=0)
        sc = c_all * _sigmoid(c_all)
        sc_ref[...] = sc
        rows = _dot3_rhs(sc, wa_ref[...])
        for b in range(N_DEV):
            part[b] = rows[b:b + 1, :]
        a_sends = [remote(part.at[_dev_index(*to)], ada_land.at[me], 7 + k, to) for k, to in enumerate(peers)]
        for cp in a_sends:
            cp.start()
        ada_land[me] = part[me]
        for k, frm in enumerate(peers):
            remote(part.at[0], ada_land.at[_dev_index(*frm)], 7 + k, frm).wait_recv()
        ada_ref[...] = ada_land[...]

        relay_near()
        relay_far()
        finish()
        for cp in c_sends + a_sends:
            cp.wait_send()

    vmem = pl.BlockSpec(memory_space=pltpu.VMEM)
    return pl.pallas_call(
        body, name="gather_weights",
        in_specs=[vmem, ANY, vmem], out_specs=[ANY, vmem, vmem],
        out_shape=[jax.ShapeDtypeStruct((N_DEV,) + w_in_rows.shape, w_in_rows.dtype),
                   jax.ShapeDtypeStruct((N_DEV, D), F32), jax.ShapeDtypeStruct((N_DEV, 1, cols), F32)],
        scratch_shapes=[pltpu.VMEM((N_DEV, 1, D), F32), pltpu.VMEM((N_DEV, 1, cols), F32), pltpu.VMEM((N_DEV, 1, cols), F32),
                        pltpu.SemaphoreType.DMA((N_GATHER_SEMS,)), pltpu.SemaphoreType.DMA((N_GATHER_SEMS,)),
                        pltpu.SemaphoreType.DMA((1,)),
                        pltpu.SemaphoreType.DMA((14,)), pltpu.SemaphoreType.DMA((14,))],
        compiler_params=_params(),
    )(c, w_in_rows, w_ada)


def _inproj_forward(x, mod, w_main, w_f, b_main, b_f, tile):
    seq = x.shape[0]
    nt = seq // tile

    def body(x_ref, mod_ref, w_ref, wf_ref, b_ref, bf_ref,
             qp_ref, kp_ref, vp_ref, f_ref, p_ref, ga_ref, gp_ref, u_ref, carry_ref):
        i = pl.program_id(0)

        @pl.when(i == 0)
        def _():
            carry_ref[...] = jnp.zeros_like(carry_ref)

        u = x_ref[...] * mod_ref[0:1, :] + mod_ref[1:2, :]
        ub = u.astype(BF16)
        u_ref[...] = ub

        f = _dot_nt(ub, wf_ref[...]) + bf_ref[...]
        f_ref[...] = f
        lane = _lanes((tile, LANE))
        log_f = jnp.where(lane < N_HEADS, jnp.minimum(f, 0.0) - jnp.log(1.0 + jnp.exp(-jnp.abs(f))), 0.0)
        row = lax.broadcasted_iota(jnp.int32, (tile, tile), 0)
        col = lax.broadcasted_iota(jnp.int32, (tile, tile), 1)
        tri = (row >= col).astype(BF16)
        cum = _dot3(tri, log_f) + carry_ref[0:1, :]
        carry_ref[0:1, :] = cum[tile - 1:tile, :]
        cq = [part.astype(F32) for part in _split3(cum)]
        ck = [part.astype(F32) for part in _split3(-cum)]

        def proj(chunk):
            cols = pl.ds(chunk * COL_CHUNK, COL_CHUNK)
            return _dot_nt(ub, w_ref[cols, :]) + b_ref[:, cols]

        def head_tiles(r):
            for pair in range(N_HEADS // 2):
                both = r[:, pair * LANE:(pair + 1) * LANE]
                yield 2 * pair, both
                yield 2 * pair + 1, pltpu.roll(both, HEAD_DIM, 1)

        for h, val in head_tiles(proj(0)):
            extra = jnp.where((lane >= L_CK) & (lane < L_CK + 3), 1.0, 0.0)
            extra = _place3(lane, L_CQ, [part[:, h:h + 1] for part in cq], extra)
            qp_ref[h] = jnp.where(lane < HEAD_DIM, val * Q_SCALE, extra).astype(BF16)
        for h, val in head_tiles(proj(1)):
            ones = ((lane >= L_CQ) & (lane < L_CQ + 3)) | ((lane >= L_LSE) & (lane < L_LSE + 3))
            extra = _place3(lane, L_CK, [part[:, h:h + 1] for part in ck], jnp.where(ones, 1.0, 0.0))
            kp_ref[h] = jnp.where(lane < HEAD_DIM, val, extra).astype(BF16)
        for h, val in head_tiles(proj(2)):
            extra = jnp.where((lane >= HEAD_DIM) & (lane < HEAD_DIM + 3), -1.0, 0.0)
            vp_ref[h] = jnp.where(lane < HEAD_DIM, val, extra).astype(BF16)
        p_ref[...] = proj(3)
        ga_ref[...] = proj(4)
        gp_ref[...] = proj(5)

    head_block = pl.BlockSpec((N_HEADS, tile, LANE), lambda i: (0, i, 0))
    tok = lambda width: pl.BlockSpec((tile, width), lambda i: (i, 0))
    whole = lambda a: pl.BlockSpec(a.shape, lambda i: (0,) * a.ndim)
    padded = jax.ShapeDtypeStruct((N_HEADS, seq, LANE), BF16)
    half = jax.ShapeDtypeStruct((seq, D_ATT), F32)
    return pl.pallas_call(
        body, name="inproj_forward", grid=(nt,),
        in_specs=[tok(D), whole(mod), whole(w_main), whole(w_f), whole(b_main), whole(b_f)],
        out_specs=[head_block, head_block, head_block, tok(LANE), tok(D_POOL), tok(D_ATT), tok(D_POOL),
                   tok(D)],
        out_shape=[padded, padded, padded, jax.ShapeDtypeStruct((seq, LANE), F32), half, half, half,
                   jax.ShapeDtypeStruct((seq, D), BF16)],
        scratch_shapes=[pltpu.VMEM((8, LANE), F32)],
        compiler_params=_params(("arbitrary",)),
    )(x, mod, w_main, w_f, b_main, b_f)


def _attention_forward(qp, kp, vp, w_out, tile):
    seq = qp.shape[1]
    nb = seq // tile
    steps = (N_HEADS // 2) * nb

    def body(q_ref, k_ref, v_ref, wo_ref, att_ref, q2t_ref, wo_all_ref, s_a, s_b, m_ref, acc_ref,
             send_sems, recv_sems, local_sem):
        step = pl.program_id(0) * nb + pl.program_id(1)
        start, relay_near, relay_far, finish = _gather_stages(wo_ref, wo_all_ref, send_sems, recv_sems, local_sem.at[0])
        pl.when(step == 0)(start)
        pl.when(step == steps // 4)(relay_near)
        pl.when(step == (3 * steps) // 4)(relay_far)

        i = pl.program_id(1)
        sub = lax.broadcasted_iota(jnp.int32, (LANE, tile), 0)
        row = lax.broadcasted_iota(jnp.int32, (tile, tile), 0)
        col = lax.broadcasted_iota(jnp.int32, (tile, tile), 1)
        q = [q_ref[0], q_ref[1]]

        def scores(buf, kb):
            rows = pl.ds(pl.multiple_of(kb * tile, tile), tile)
            for hh in range(2):
                buf[hh] = _dot_nt(k_ref[hh, rows, :], q[hh])

        def absorb(buf, kb, masked):
            rows = pl.ds(pl.multiple_of(kb * tile, tile), tile)
            for hh in range(2):
                m = m_ref[hh, 0:1, :]
                s = buf[hh]
                if masked:
                    s = jnp.where(row <= col, s, -1e30)
                m_new = jnp.maximum(m, jnp.max(s, axis=0, keepdims=True))
                p = jnp.exp(s - m_new).astype(BF16)
                acc_ref[hh] = jnp.exp(m - m_new) * acc_ref[hh] + _dot_tn(v_ref[hh, rows, :], p)
                m_ref[hh, 0:1, :] = m_new

        def two_blocks(j, _):
            scores(s_b, 2 * j + 1)
            absorb(s_a, 2 * j, False)
            scores(s_a, 2 * j + 2)
            absorb(s_b, 2 * j + 1, False)
            return 0

        def last_block():
            absorb(s_a, i, True)

        def last_two_blocks():
            scores(s_b, i)
            absorb(s_a, i - 1, False)
            absorb(s_b, i, True)

        scores(s_a, 0)
        m_ref[...] = jnp.full(m_ref.shape, -1e30, F32)
        acc_ref[...] = jnp.zeros_like(acc_ref)
        lax.fori_loop(0, i // 2, two_blocks, 0)
        lax.cond(i % 2 == 0, last_block, last_two_blocks)
        outs = []
        for hh in range(2):
            m, acc = m_ref[hh, 0:1, :], acc_ref[hh]
            l = -acc[HEAD_DIM:HEAD_DIM + 1, :]
            outs.append((acc / l)[:HEAD_DIM, :])
            neg_lse = [part.astype(F32) for part in _split3(-(m + jnp.log(l)))]
            q2t_ref[hh] = _place3(sub, L_LSE, neg_lse, q[hh].astype(F32).T).astype(BF16)
        att_ref[...] = jnp.concatenate(outs, axis=0).T
        pl.when(step == steps - 1)(finish)

    pair = pl.BlockSpec((2, tile, LANE), lambda hp, i: (hp, i, 0))
    full = pl.BlockSpec((2, seq, LANE), lambda hp, i: (hp, 0, 0))
    return pl.pallas_call(
        body, name="attention_forward", grid=(N_HEADS // 2, nb),
        in_specs=[pair, full, full, ANY],
        out_specs=[pl.BlockSpec((tile, LANE), lambda hp, i: (i, hp)),
                   pl.BlockSpec((2, LANE, tile), lambda hp, i: (hp, 0, i)), ANY],
        out_shape=[jax.ShapeDtypeStruct((seq, D_ATT), F32),
                   jax.ShapeDtypeStruct((N_HEADS, LANE, seq), BF16),
                   jax.ShapeDtypeStruct((N_DEV,) + w_out.shape, w_out.dtype)],
        scratch_shapes=[pltpu.VMEM((2, tile, tile), F32), pltpu.VMEM((2, tile, tile), F32),
                        pltpu.VMEM((2, 8, tile), F32), pltpu.VMEM((2, LANE, tile), F32),
                        pltpu.SemaphoreType.DMA((N_GATHER_SEMS,)), pltpu.SemaphoreType.DMA((N_GATHER_SEMS,)),
                        pltpu.SemaphoreType.DMA((1,))],
        compiler_params=_params(("arbitrary", "arbitrary")),
    )(qp, kp, vp, w_out)


def _window_sum(x, halo, window, transposed):
    tile = x.shape[0]

    def split_cat(a):
        hi = a.astype(BF16)
        return jnp.concatenate([hi, (a - hi.astype(F32)).astype(BF16)], axis=1)

    def fold(r):
        return r[:, :LANE] + r[:, LANE:]

    r = lax.broadcasted_iota(jnp.int32, (tile, tile), 0)
    c = lax.broadcasted_iota(jnp.int32, (tile, tile), 1)
    rh = lax.broadcasted_iota(jnp.int32, (HALO, HALO), 0)
    ch = lax.broadcasted_iota(jnp.int32, (HALO, HALO), 1)
    if not transposed:
        band = (c <= r) & (r - c < window)
        edge = (rh + HALO - ch) < window
    else:
        band = (r <= c) & (c - r < window)
        edge = (HALO + ch - rh) < window
    out = fold(_dot(band.astype(BF16), split_cat(x)))
    reach = fold(_dot(edge.astype(BF16), split_cat(halo)))
    if not transposed:
        return jnp.concatenate([out[:HALO] + reach, out[HALO:]], axis=0)
    return jnp.concatenate([out[:tile - HALO], out[tile - HALO:] + reach], axis=0)


def _silu_parts(g):
    sig = _sigmoid(g)
    return g * sig, sig * (1.0 + g * (1.0 - sig))


def _middle(x, tgt, att, g_att, g_pool, p, vecs, pool_vecs, w_out, w_pool, tile):
    seq = x.shape[0]
    nt = seq // tile
    halo_blocks = tile // HALO

    def body(x_ref, tgt_ref, att_ref, ga_ref, gp_ref, p_ref, ph_ref, vec_ref, pvec_ref, wo_ref, wp_ref,
             dxa_ref, do2_ref, dga_ref, dgp_ref, dpooled_ref, dwo_ref, dwp_ref, dvec_ref, dpvec_ref):
        i = pl.program_id(0)

        @pl.when(i == 0)
        def _():
            dwo_ref[...] = jnp.zeros_like(dwo_ref)
            dwp_ref[...] = jnp.zeros_like(dwp_ref)
            dvec_ref[...] = jnp.zeros_like(dvec_ref)
            dpvec_ref[...] = jnp.zeros_like(dpvec_ref)

        gate, b_out, ln_g, ln_b = (vec_ref[k:k + 1, :] for k in range(4))
        b_pool, pool_scale = pvec_ref[0:1, :], pvec_ref[1:2, :]
        x = x_ref[...]
        p = p_ref[...]
        p_halo = ph_ref[...] * jnp.where(i > 0, 1.0, 0.0)
        pos = i * tile + lax.broadcasted_iota(jnp.int32, (tile, 1), 0) + 1

        pooled, mixed = [], []
        for g, window in enumerate(POOL_WINDOWS):
            cols = slice(g * GROUP_DIM, (g + 1) * GROUP_DIM)
            wsum = _window_sum(p[:, cols], p_halo[:, cols], window, False)
            count = jnp.minimum(pos, window).astype(F32)
            pooled.append(wsum / count - p[:, cols])
            mixed.append(_dot(pooled[g].astype(BF16), wp_ref[g]) + b_pool[:, cols])
        mixed = jnp.concatenate(mixed, axis=1)
        pool = mixed * pool_scale

        att = att_ref[...]
        g_att, g_pool = ga_ref[...], gp_ref[...]
        silu_a, dsilu_a = _silu_parts(g_att)
        silu_p, dsilu_p = _silu_parts(g_pool)
        y_in = jnp.concatenate([att * silu_a, pool * silu_p], axis=1)
        y = _dot(y_in.astype(BF16), wo_ref[...]) + b_out
        h = ALPHA * x + gate * y
        mu = jnp.mean(h, axis=1, keepdims=True)
        hc = h - mu
        var = jnp.mean(hc * hc, axis=1, keepdims=True)
        rstd = lax.rsqrt(var + LN_EPS)
        yhat = hc * rstd
        diff = yhat * ln_g + ln_b - tgt_ref[...]
        loss_rows = jnp.sum(diff * diff, axis=1, keepdims=True)
        d_out = diff * (1.0 / D)

        d_yhat = d_out * ln_g
        dh = rstd * (d_yhat - jnp.mean(d_yhat, axis=1, keepdims=True)
                     - yhat * jnp.mean(d_yhat * yhat, axis=1, keepdims=True))
        dxa_ref[...] = ALPHA * dh
        dy = dh * gate
        dyb = dy.astype(BF16)
        lane = _lanes((1, D))
        loss_row = jnp.where(lane == 0, (0.5 / D) * jnp.sum(loss_rows, axis=0, keepdims=True), 0.0)
        dvec_ref[0:1, :] += jnp.sum(dh * y, axis=0, keepdims=True)
        dvec_ref[1:2, :] += jnp.sum(dy, axis=0, keepdims=True)
        dvec_ref[2:3, :] += jnp.sum(d_out * yhat, axis=0, keepdims=True)
        dvec_ref[3:4, :] += jnp.sum(d_out, axis=0, keepdims=True)
        dvec_ref[4:5, :] += loss_row

        dwo_ref[...] += _dot(y_in.T.astype(BF16), dyb)
        d_yin = _dot_nt(dyb, wo_ref[...])
        d_a, d_pl = d_yin[:, :D_ATT], d_yin[:, D_ATT:]
        d_att = d_a * silu_a
        d_att_t = d_att.T
        prod_t = (d_att * att).T
        sub = lax.broadcasted_iota(jnp.int32, (HEAD_DIM, tile), 0)
        for h in range(N_HEADS):
            rows = slice(h * HEAD_DIM, (h + 1) * HEAD_DIM)
            delta = jnp.sum(prod_t[rows], axis=0, keepdims=True)
            extra = _place3(sub, 0, [part.astype(F32) for part in _split3(delta)], 0.0)
            do2_ref[h] = jnp.concatenate([d_att_t[rows], extra], axis=0).astype(BF16)
        dga_ref[...] = d_a * att * dsilu_a
        dgp_ref[...] = d_pl * pool * dsilu_p
        d_pool = d_pl * silu_p
        d_mixed = d_pool * pool_scale
        dpvec_ref[0:1, :] += jnp.sum(d_mixed, axis=0, keepdims=True)
        dpvec_ref[1:2, :] += jnp.sum(d_pool * mixed, axis=0, keepdims=True)
        d_pooled = []
        for g in range(len(POOL_WINDOWS)):
            cols = slice(g * GROUP_DIM, (g + 1) * GROUP_DIM)
            dmb = d_mixed[:, cols].astype(BF16)
            dwp_ref[g] += _dot(pooled[g].T.astype(BF16), dmb)
            d_pooled.append(_dot_nt(dmb, wp_ref[g]))
        dpooled_ref[...] = jnp.concatenate(d_pooled, axis=1)

    tok = lambda width: pl.BlockSpec((tile, width), lambda i: (i, 0))
    whole = lambda a: pl.BlockSpec(a.shape, lambda i: (0,) * a.ndim)
    halo = pl.BlockSpec((HALO, D_POOL), lambda i: (jnp.maximum(i * halo_blocks - 1, 0), 0))
    half = jax.ShapeDtypeStruct((seq, D_ATT), F32)
    outs = [jax.ShapeDtypeStruct((seq, D), F32), jax.ShapeDtypeStruct((N_HEADS, LANE, seq), BF16), half, half, half,
            jax.ShapeDtypeStruct(w_out.shape, F32), jax.ShapeDtypeStruct(w_pool.shape, F32),
            jax.ShapeDtypeStruct(vecs.shape, F32), jax.ShapeDtypeStruct(pool_vecs.shape, F32)]
    return pl.pallas_call(
        body, name="middle", grid=(nt,),
        in_specs=[tok(D), tok(D), tok(D_ATT), tok(D_ATT), tok(D_POOL), tok(D_POOL), halo,
                  whole(vecs), whole(pool_vecs), whole(w_out), whole(w_pool)],
        out_specs=[tok(D), pl.BlockSpec((N_HEADS, LANE, tile), lambda i: (0, 0, i)),
                   tok(D_ATT), tok(D_POOL), tok(D_POOL),
                   whole(w_out), whole(w_pool), whole(vecs), whole(pool_vecs)],
        out_shape=outs,
        compiler_params=_params(("arbitrary",)),
    )(x, tgt, att, g_att, g_pool, p, p, vecs, pool_vecs, w_out, w_pool)


def _attention_backward(q2t, kp, vp, do2t, gw_out, small, tile):
    seq = kp.shape[1]
    nb = seq // tile
    last = N_HEADS // 2 - 1

    def body(qt_ref, k_ref, v_ref, dot_ref, gwo_hbm, small_hbm,
             dq_ref, dk_ref, dv_ref, dcum_ref, g_out_ref, total_ref,
             dq_acc, dk_acc, dv_acc, gwo_ref, r1, s2, r2, sm_sib, sm_chip, sm_recv, small_ref, send_sems, recv_sems):
        hp = pl.program_id(0)
        start, middle, finish = _reduce_stages(
            [gwo_ref], [g_out_ref], [r1], [s2], [r2], (small_ref, total_ref, sm_sib, sm_chip, sm_recv),
            send_sems, recv_sems)

        @pl.when(hp == 0)
        def _():
            pltpu.sync_copy(gwo_hbm, gwo_ref)
            pltpu.sync_copy(small_hbm, small_ref)
            start()

        pl.when(hp == 1)(middle)

        row = lax.broadcasted_iota(jnp.int32, (tile, tile), 0)
        col = lax.broadcasted_iota(jnp.int32, (tile, tile), 1)
        dq_acc[...] = jnp.zeros_like(dq_acc)

        def kv_block(kb, _):
            krows = pl.ds(pl.multiple_of(kb * tile, tile), tile)
            k = [k_ref[hh, krows, :] for hh in range(2)]
            v = [v_ref[hh, krows, :] for hh in range(2)]
            k_t = [k[hh].T for hh in range(2)]

            def q_block(qb, masked):
                qcols = pl.ds(pl.multiple_of(qb * tile, tile), tile)
                for hh in range(2):
                    q_t = qt_ref[hh, :, qcols]
                    do_t = dot_ref[hh, :, qcols]
                    s_t = _dot(k[hh], q_t)
                    if masked:
                        s_t = jnp.where(row <= col, s_t, -1e30)
                    p_t = jnp.exp(s_t)
                    ds_t = (p_t * _dot(v[hh], do_t)).astype(BF16)
                    dv_new = _dot_nt(do_t, p_t.astype(BF16))
                    dk_new = _dot_nt(q_t, ds_t)
                    if masked:
                        dv_acc[hh], dk_acc[hh] = dv_new, dk_new
                    else:
                        dv_acc[hh] += dv_new
                        dk_acc[hh] += dk_new
                    dq_acc[hh, :, qcols] += _dot(k_t[hh], ds_t)

            q_block(kb, True)

            def two_later_blocks(j, _):
                q_block(kb + 1 + 2 * j, False)
                q_block(kb + 2 + 2 * j, False)
                return 0

            later = nb - 1 - kb
            lax.fori_loop(0, later // 2, two_later_blocks, 0)
            pl.when(later % 2 == 1)(lambda: q_block(nb - 1, False))
            for hh in range(2):
                dk = dk_acc[hh]
                dk_ref[hh, :, krows] = dk.astype(BF16)
                dv_ref[hh, :, krows] = dv_acc[hh].astype(BF16)
                dcum_ref[hh, :, krows] = -dk[L_CK:L_CK + 1, :]
            return 0

        lax.fori_loop(0, nb, kv_block, 0)
        for hh in range(2):
            dq = dq_acc[hh]
            dcum_ref[hh] += dq[L_CQ:L_CQ + 1, :]
            dq_ref[hh] = (dq * Q_SCALE).astype(BF16)
        pl.when(hp == last)(finish)

    pair = pl.BlockSpec((2, seq, LANE), lambda hp: (hp, 0, 0))
    pair_t = pl.BlockSpec((2, LANE, seq), lambda hp: (hp, 0, 0))
    whole = lambda shape: pl.BlockSpec(shape, lambda hp: (0,) * len(shape))
    grad = jax.ShapeDtypeStruct((N_HEADS, LANE, seq), BF16)
    return pl.pallas_call(
        body, name="attention_backward", grid=(N_HEADS // 2,),
        in_specs=[pair_t, pair, pair, pair_t, ANY, ANY],
        out_specs=[pair_t, pair_t, pair_t, pl.BlockSpec((2, 1, seq), lambda hp: (hp, 0, 0)),
                   whole(gw_out.shape[1:]), whole(small.shape)],
        out_shape=[grad, grad, grad, jax.ShapeDtypeStruct((N_HEADS, 1, seq), F32),
                   jax.ShapeDtypeStruct(gw_out.shape[1:], F32), jax.ShapeDtypeStruct(small.shape, F32)],
        scratch_shapes=[pltpu.VMEM((2, LANE, seq), F32), pltpu.VMEM((2, LANE, tile), F32),
                        pltpu.VMEM((2, LANE, tile), F32), pltpu.VMEM(gw_out.shape, BF16)]
        + _reduce_scratch(gw_out, small)
        + [pltpu.VMEM(small.shape, F32),
           pltpu.SemaphoreType.DMA((N_REDUCE_SEMS + N_SMALL_SEMS,)), pltpu.SemaphoreType.DMA((N_REDUCE_SEMS + N_SMALL_SEMS,))],
        compiler_params=_params(("arbitrary",)),
    )(q2t, kp, vp, do2t, gw_out, small)


def _inproj_backward(dqp, dkp, dvp, d_cum, f, d_pooled, d_ga, d_gp, x, dxa, u, mod, w_main, w_f, tile):
    seq = x.shape[0]
    nt = seq // tile
    halo_blocks = tile // HALO

    def body(dq_ref, dk_ref, dv_ref, dcum_ref, f_ref, dpo_ref, dph_ref, dga_ref, dgp_ref, x_ref, dxa_ref, u_ref,
             mod_ref, w_ref, wf_ref,
             dx_ref, dproj_ref, dwf_ref, db_ref, dbf_ref, dmod_ref, carry_ref):
        step = pl.program_id(0)
        i = nt - 1 - step

        @pl.when(step == 0)
        def _():
            carry_ref[...] = jnp.zeros_like(carry_ref)
            dwf_ref[...] = jnp.zeros_like(dwf_ref)
            db_ref[...] = jnp.zeros_like(db_ref)
            dbf_ref[...] = jnp.zeros_like(dbf_ref)
            dmod_ref[...] = jnp.zeros_like(dmod_ref)

        ones = jnp.ones((8, tile), BF16)

        def emit(chunk, val):
            cols = pl.ds(chunk * COL_CHUNK, COL_CHUNK)
            db_ref[0:1, cols] += jnp.sum(val, axis=0, keepdims=True)
            vb = val.astype(BF16)
            dproj_ref[:, pl.ds((chunk - 3) * COL_CHUNK, COL_CHUNK)] = vb
            return _dot(vb, w_ref[cols, :])

        d_u = jnp.zeros((tile, D), F32)
        for chunk, ref in enumerate((dq_ref, dk_ref, dv_ref)):
            cols = pl.ds(chunk * COL_CHUNK, COL_CHUNK)
            val_t = ref[:, 0:HEAD_DIM, :].reshape(COL_CHUNK, tile)
            db_ref[:, cols] += _dot_nt(ones, val_t)
            d_u += _dot_tn(val_t, w_ref[cols, :])

        d_pooled = dpo_ref[...]
        d_halo = dph_ref[...] * jnp.where(i < nt - 1, 1.0, 0.0)
        pos = i * tile + lax.broadcasted_iota(jnp.int32, (tile, 1), 0) + 1
        d_p = []
        for g, window in enumerate(POOL_WINDOWS):
            cols = slice(g * GROUP_DIM, (g + 1) * GROUP_DIM)
            scaled = d_pooled[:, cols] / jnp.minimum(pos, window).astype(F32)
            d_p.append(_window_sum(scaled, d_halo[:, cols] * (1.0 / window), window, True) - d_pooled[:, cols])
        d_u += emit(3, jnp.concatenate(d_p, axis=1))
        d_u += emit(4, dga_ref[...])
        d_u += emit(5, dgp_ref[...])

        row = lax.broadcasted_iota(jnp.int32, (tile, tile), 0)
        col = lax.broadcasted_iota(jnp.int32, (tile, tile), 1)
        later = (row >= col).astype(BF16)
        d_logf = sum(_dot(part, later) for part in _split3(dcum_ref[:, 0, :])) + carry_ref[:, 0:1]
        carry_ref[:, 0:1] = d_logf[:, 0:1]
        d_f = d_logf * _sigmoid(-f_ref[...].T[0:N_HEADS, :])
        d_f = jnp.concatenate([d_f, jnp.zeros((LANE - N_HEADS, tile), F32)], axis=0)
        dbf_ref[...] += sum(_dot_nt(ones, part) for part in _split3(d_f))
        d_fb = d_f.astype(BF16)
        d_u += _dot_tn(d_fb, wf_ref[...])
        dwf_ref[...] += _dot(d_fb, u_ref[...])

        x = x_ref[...]
        dx_ref[...] = dxa_ref[...] + d_u * mod_ref[0:1, :]
        dmod_ref[0:1, :] += jnp.sum(d_u * x, axis=0, keepdims=True)
        dmod_ref[1:2, :] += jnp.sum(d_u, axis=0, keepdims=True)

    rev = lambda step: nt - 1 - step
    tok = lambda width: pl.BlockSpec((tile, width), lambda s: (rev(s), 0))
    head_block = pl.BlockSpec((N_HEADS, LANE, tile), lambda s: (0, 0, rev(s)))
    whole = lambda a: pl.BlockSpec(a.shape, lambda s: (0,) * a.ndim)
    halo = pl.BlockSpec((HALO, D_POOL), lambda s: (jnp.minimum((rev(s) + 1) * halo_blocks, seq // HALO - 1), 0))
    small = lambda width: jax.ShapeDtypeStruct((8, width), F32)
    n_rest = N_MAIN - OFF_P
    return pl.pallas_call(
        body, name="inproj_backward", grid=(nt,),
        in_specs=[head_block, head_block, head_block, pl.BlockSpec((N_HEADS, 1, tile), lambda s: (0, 0, rev(s))),
                  tok(LANE), tok(D_POOL), halo, tok(D_ATT), tok(D_POOL),
                  tok(D), tok(D), tok(D),
                  whole(mod), whole(w_main), whole(w_f)],
        out_specs=[tok(D), tok(n_rest), pl.BlockSpec((LANE, D), lambda s: (0, 0)),
                   pl.BlockSpec((8, N_MAIN), lambda s: (0, 0)), pl.BlockSpec((8, LANE), lambda s: (0, 0)),
                   pl.BlockSpec((8, D), lambda s: (0, 0))],
        out_shape=[jax.ShapeDtypeStruct((seq, D), F32), jax.ShapeDtypeStruct((seq, n_rest), BF16),
                   jax.ShapeDtypeStruct((LANE, D), F32), small(N_MAIN), small(LANE), small(D)],
        scratch_shapes=[pltpu.VMEM((8, LANE), F32)],
        compiler_params=_params(("arbitrary",)),
    )(dqp, dkp, dvp, d_cum, f, d_pooled, d_pooled, d_ga, d_gp, x, dxa, u, mod, w_main, w_f)


def _weight_grad(dproj, u, k_tile):
    seq, n_cols = dproj.shape
    nk = seq // k_tile

    def body(dp_ref, u_ref, out_ref):
        @pl.when(pl.program_id(1) == 0)
        def _():
            out_ref[...] = jnp.zeros_like(out_ref)

        out_ref[...] += _dot_tn(dp_ref[...], u_ref[...])

    return pl.pallas_call(
        body, name="weight_grad", grid=(n_cols // COL_CHUNK, nk),
        in_specs=[pl.BlockSpec((k_tile, COL_CHUNK), lambda n, k: (k, n)),
                  pl.BlockSpec((k_tile, D), lambda n, k: (k, 0))],
        out_specs=pl.BlockSpec((COL_CHUNK, D), lambda n, k: (n, 0)),
        out_shape=jax.ShapeDtypeStruct((n_cols, D), F32),
        compiler_params=_params(("arbitrary", "arbitrary")),
    )(dproj, u)


def _weight_grad_heads(grad_t, u, k_tile, name):
    seq = u.shape[0]
    nk = seq // k_tile

    def body(g_ref, u_ref, out_ref):
        @pl.when(pl.program_id(0) == 0)
        def _():
            out_ref[...] = jnp.zeros_like(out_ref)

        out_ref[...] += _dot(g_ref[...].reshape(N_HEADS * HEAD_DIM, k_tile), u_ref[...])

    return pl.pallas_call(
        body, name=name, grid=(nk,),
        in_specs=[pl.BlockSpec((N_HEADS, HEAD_DIM, k_tile), lambda k: (0, 0, k)),
                  pl.BlockSpec((k_tile, D), lambda k: (k, 0))],
        out_specs=pl.BlockSpec((N_HEADS * HEAD_DIM, D), lambda k: (0, 0)),
        out_shape=jax.ShapeDtypeStruct((N_HEADS * HEAD_DIM, D), F32),
        compiler_params=_params(("arbitrary",)),
    )(grad_t, u)


def _adamw(w, g, m, v):
    m = ADAM_B1 * m + (1.0 - ADAM_B1) * g
    v = ADAM_B2 * v + (1.0 - ADAM_B2) * (g * g)
    m_hat = m / (1.0 - ADAM_B1 ** ADAM_STEP)
    v_hat = v / (1.0 - ADAM_B2 ** ADAM_STEP)
    delta = -ADAM_LR * (m_hat / (jnp.sqrt(v_hat) + ADAM_EPS) + ADAM_WD * w)
    return delta, m, v


def _adamw_call(g, w, m, v, lead_tile, name):
    nr = w.shape[0] // lead_tile

    def body(gi_ref, w_ref, m_ref, v_ref, g_ref, d_ref, nm_ref, nv_ref):
        g = gi_ref[...]
        g_ref[...] = g
        d_ref[...], nm_ref[...], nv_ref[...] = _adamw(w_ref[...], g, m_ref[...], v_ref[...])

    blk = pl.BlockSpec((lead_tile,) + w.shape[1:], lambda r: (r,) + (0,) * (w.ndim - 1))
    shape = jax.ShapeDtypeStruct(w.shape, F32)
    return pl.pallas_call(
        body, name=name, grid=(nr,),
        in_specs=[blk, blk, blk, blk], out_specs=[blk, blk, blk, blk],
        out_shape=[shape, shape, shape, shape],
        compiler_params=_params(("arbitrary",)),
    )(g, w, m, v)


def _sum_adamw(parts, w, m, v, row_tile, name):
    rows, cols = w.shape
    nr = rows // row_tile

    def body(parts_ref, w_ref, m_ref, v_ref, g_ref, d_ref, nm_ref, nv_ref):
        g = parts_ref[0]
        for k in range(1, N_DEV):
            g = g + parts_ref[k]
        g_ref[...] = g
        d_ref[...], nm_ref[...], nv_ref[...] = _adamw(w_ref[...], g, m_ref[...], v_ref[...])

    blk = pl.BlockSpec((row_tile, cols), lambda r: (r, 0))
    shape = jax.ShapeDtypeStruct(w.shape, F32)
    return pl.pallas_call(
        body, name=name, grid=(nr,),
        in_specs=[pl.BlockSpec((N_DEV, row_tile, cols), lambda r: (0, r, 0)), blk, blk, blk],
        out_specs=[blk, blk, blk, blk],
        out_shape=[shape, shape, shape, shape],
        compiler_params=_params(("arbitrary",)),
    )(parts, w, m, v)


def _ada_adamw(sc_t, d_ada, w, m, v):
    def body(sc_ref, d_ref, w_ref, m_ref, v_ref, g_ref, dl_ref, nm_ref, nv_ref):
        g = sc_ref[:, 0:1] * d_ref[0:1, :]
        for b in range(1, N_DEV):
            g = g + sc_ref[:, b:b + 1] * d_ref[b:b + 1, :]
        g_ref[...] = g
        dl_ref[...], nm_ref[...], nv_ref[...] = _adamw(w_ref[...], g, m_ref[...], v_ref[...])

    shape = jax.ShapeDtypeStruct(w.shape, F32)
    return pl.pallas_call(
        body, name="ada_adamw", out_shape=[shape, shape, shape, shape], compiler_params=_params(),
    )(sc_t, d_ada, w, m, v)


F_LO, F_HI = 3 * D_ATT, 3 * D_ATT + N_HEADS


def _split_forget(a, axis):
    idx = lambda lo, hi: tuple(slice(lo, hi) if d == axis else slice(None) for d in range(a.ndim))
    pad = [(0, LANE - N_HEADS) if d == axis else (0, 0) for d in range(a.ndim)]
    return jnp.concatenate([a[idx(0, F_LO)], a[idx(F_HI, D_IN)]], axis=axis), jnp.pad(a[idx(F_LO, F_HI)], pad)


def _join_forget(main, f, axis):
    idx = lambda lo, hi: tuple(slice(lo, hi) if d == axis else slice(None) for d in range(main.ndim))
    return jnp.concatenate([main[idx(0, F_LO)], f[idx(0, N_HEADS)], main[idx(F_LO, N_MAIN)]], axis=axis)


_EARLY = ((("w_pool", 65536), ("b_pool", 512), ("pool_scale", 512), ("b_out", 1024), ("ln_g", 1024), ("ln_b", 1024),
           ("loss", 128)), 552)
_LATE = ((("b_in", 3200),), 32)


def _pack_small(parts, layout):
    spec, rows = layout
    flat = []
    for name, size in spec:
        a = parts[name].reshape(-1)
        flat.append(jnp.pad(a, (0, size - a.shape[0])))
    flat = jnp.concatenate(flat)
    flat = jnp.pad(flat, (0, rows * LANE - flat.shape[0]))
    return flat.reshape(rows, LANE)


def _unpack_small(packed, shapes, layout):
    flat = packed.reshape(-1)
    out, off = {}, 0
    for name, size in layout[0]:
        n = 1
        for s in shapes[name]:
            n *= s
        out[name] = flat[off:off + n].reshape(shapes[name])
        off += size
    return out


def kernel(x, c, w_ada, b_ada, w_in, b_in, w_pool_mix, b_pool_mix, pool_scale, w_out, b_out, ln_g, ln_b, loss_target, m_w_ada, m_b_ada, m_w_in, m_b_in, m_w_pool_mix, m_b_pool_mix, m_pool_scale, m_w_out, m_b_out, m_ln_g, m_ln_b, v_w_ada, v_b_ada, v_w_in, v_b_in, v_w_pool_mix, v_b_pool_mix, v_pool_scale, v_w_out, v_b_out, v_ln_g, v_ln_b):
    seq = x.shape[1]
    tile = min(256, seq)
    attn_tile = min(512, max(128, seq // 4))
    me = _dev_index(*_mesh_pos())
    x2, tgt = x[0], loss_target[0]

    rows_in = D_IN // N_DEV
    w_in_g, sc_all, ada_mine = _gather_and_ada(c, w_in[0].T.astype(BF16), w_ada[0])
    ada = ada_mine.reshape(1, D_ADA) + b_ada
    shift, scale, gate = ada[:, 0:D], ada[:, D:2 * D], ada[:, 2 * D:]
    mod = jnp.concatenate([1.0 + scale, shift, jnp.zeros((6, D), F32)], axis=0)

    w_main, w_f = _split_forget(w_in_g.reshape(D_IN, D), 0)
    b_main, b_f = _split_forget(b_in, 1)

    qp, kp, vp, f, p, g_att, g_pool, u = _inproj_forward(x2, mod, w_main, w_f, b_main, b_f, tile)
    att, q2t, w_out_g = _attention_forward(qp, kp, vp, w_out[0].astype(BF16), attn_tile)

    vecs = jnp.concatenate([gate, b_out, ln_g, ln_b, jnp.zeros((4, D), F32)], axis=0)
    pool_vecs = jnp.concatenate([b_pool_mix.reshape(1, D_POOL), pool_scale, jnp.zeros((6, D_POOL), F32)], axis=0)
    dxa, do2, d_ga, d_gp, d_pooled, dw_out, dw_pool, dvec, dpvec = _middle(
        x2, tgt, att, g_att, g_pool, p, vecs, pool_vecs, w_out_g.reshape(D, D), w_pool_mix[0].astype(BF16), tile)

    gw_out = dw_out.reshape(N_DEV, D // N_DEV, D).astype(BF16)
    early = _pack_small({"w_pool": dw_pool, "b_pool": dpvec[0:1], "pool_scale": dpvec[1:2], "b_out": dvec[1:2],
                         "ln_g": dvec[2:3], "ln_b": dvec[3:4], "loss": dvec[4:5, 0:LANE]}, _EARLY)
    dqp, dkp, dvp, d_cum, g_out, early_sum = _attention_backward(q2t, kp, vp, do2, gw_out, early, attn_tile)
    dx, dproj, dw_f, db_main, db_f, dmod = _inproj_backward(
        dqp, dkp, dvp, d_cum, f, d_pooled, d_ga, d_gp, x2, dxa, u, mod, w_main, w_f, tile)
    k_tile = min(1024, seq)
    dw_q, dw_k, dw_v = (_weight_grad_heads(g, u, k_tile, "weight_grad_" + n)
                        for g, n in ((dqp, "q"), (dkp, "k"), (dvp, "v")))
    dw_rest = _weight_grad(dproj, u, k_tile)

    dw_main = jnp.concatenate([dw_q, dw_k, dw_v, dw_rest], axis=0)
    gw_in = _join_forget(dw_main, dw_f, 0).reshape(N_DEV, rows_in, D).astype(BF16)
    d_ada = jnp.concatenate([dmod[1:2], dmod[0:1], dvec[0:1]], axis=1)
    late = _pack_small({"b_in": _join_forget(db_main[0:1], db_f[0:1], 1)}, _LATE)
    g_in_rows, late_sum, d_ada_all = _reduce_grads(gw_in, late, d_ada.reshape(D_ADA // LANE, LANE))

    def rows3(a):
        return a[0].T.reshape(rows_in, D // LANE, LANE)

    outs_in = _adamw_call(g_in_rows.reshape(rows_in, D // LANE, LANE), rows3(w_in), rows3(m_w_in), rows3(v_w_in),
                          rows_in // 5, "adamw_w_in")
    g_w_in, d_w_in, nm_w_in, nv_w_in = (a.reshape(rows_in, D).T for a in outs_in)
    g_w_out, d_w_out, nm_w_out, nv_w_out = _adamw_call(g_out, w_out[0], m_w_out[0], v_w_out[0], D // N_DEV, "adamw_w_out")

    def small_adamw(grad_sum, weights, first, second, layout, name):
        packed = _adamw_call(grad_sum, _pack_small(weights, layout), _pack_small(first, layout),
                             _pack_small(second, layout), layout[1], name)
        shapes = {k: a.shape for k, a in weights.items()}
        return [_unpack_small(a, shapes, layout) for a in packed]

    zero = jnp.zeros((1,), F32)
    smalls = small_adamw(
        early_sum,
        {"w_pool": w_pool_mix, "b_pool": b_pool_mix, "pool_scale": pool_scale, "b_out": b_out, "ln_g": ln_g,
         "ln_b": ln_b, "loss": zero},
        {"w_pool": m_w_pool_mix, "b_pool": m_b_pool_mix, "pool_scale": m_pool_scale, "b_out": m_b_out,
         "ln_g": m_ln_g, "ln_b": m_ln_b, "loss": zero},
        {"w_pool": v_w_pool_mix, "b_pool": v_b_pool_mix, "pool_scale": v_pool_scale, "b_out": v_b_out,
         "ln_g": v_ln_g, "ln_b": v_ln_b, "loss": zero}, _EARLY, "adamw_small")
    b_ins = small_adamw(late_sum, {"b_in": b_in}, {"b_in": m_b_in}, {"b_in": v_b_in}, _LATE, "adamw_b_in")
    g_s, d_s, nm_s, nv_s = ({**a, **b} for a, b in zip(smalls, b_ins))
    loss = g_s["loss"][0]

    ada_rows = D_ADA // LANE
    b_ada_outs = _sum_adamw(d_ada_all, b_ada.reshape(ada_rows, LANE), m_b_ada.reshape(ada_rows, LANE),
                            v_b_ada.reshape(ada_rows, LANE), ada_rows, "adamw_b_ada")
    g_b_ada, d_b_ada, nm_b_ada, nv_b_ada = (a.reshape(1, D_ADA) for a in b_ada_outs)
    d_ada_local = lax.dynamic_slice_in_dim(d_ada_all.reshape(N_DEV, D_ADA), me * (D_ADA // N_DEV), D_ADA // N_DEV, axis=1)
    g_w_ada, d_w_ada, nm_w_ada, nv_w_ada = _ada_adamw(sc_all.T, d_ada_local, w_ada[0], m_w_ada[0], v_w_ada[0])

    def ordered(w_ada_, b_ada_, w_in_, w_out_, s):
        return (w_ada_[None], b_ada_, w_in_[None], s["b_in"], s["w_pool"], s["b_pool"], s["pool_scale"],
                w_out_[None], s["b_out"], s["ln_g"], s["ln_b"])

    return (loss, dx[None],
            *ordered(g_w_ada, g_b_ada, g_w_in, g_w_out, g_s),
            *ordered(d_w_ada, d_b_ada, d_w_in, d_w_out, d_s),
            *ordered(nm_w_ada, nm_b_ada, nm_w_in, nm_w_out, nm_s),
            *ordered(nv_w_ada, nv_b_ada, nv_w_in, nv_w_out, nv_s))
```

```python
import jax
import jax.numpy as jnp
from jax import lax
from jax.experimental import pallas as pl
from jax.experimental.pallas import tpu as pltpu

F32 = jnp.float32
BF16 = jnp.bfloat16

N_DEV = 8
D = 1024
N_HEADS = 8
HEAD_DIM = 64
D_ATT = 512
D_POOL = 512
POOL_WINDOWS = (2, 4, 8, 16)
GROUP_DIM = 128
HALO = 16
LANE = 128
D_IN = 3080
D_ADA = 3072
N_MAIN = 3072
OFF_P = 1536
COL_CHUNK = 512
Q_SCALE = 0.125
LN_EPS = 1e-5
ALPHA = 2.0 ** 0.25
L_CQ, L_CK, L_LSE = 64, 67, 70

ADAM_LR, ADAM_B1, ADAM_B2, ADAM_EPS, ADAM_WD, ADAM_STEP = 0.001, 0.9, 0.999, 1e-08, 0.01, 10
VMEM_LIMIT = 56 * 1024 * 1024

MESH = pl.DeviceIdType.MESH
ANY = pl.BlockSpec(memory_space=pl.ANY)


def _params(sem=None, vmem=VMEM_LIMIT):
    return pltpu.CompilerParams(dimension_semantics=sem, vmem_limit_bytes=vmem)


def _split3(a):
    hi = a.astype(BF16)
    r = a - hi.astype(F32)
    mid = r.astype(BF16)
    lo = (r - mid.astype(F32)).astype(BF16)
    return hi, mid, lo


def _dot(a, b):
    return jnp.dot(a, b, preferred_element_type=F32)


def _dot_nt(a, b):
    return lax.dot_general(a, b, (((1,), (1,)), ((), ())), preferred_element_type=F32)


def _dot_tn(a, b):
    return lax.dot_general(a, b, (((0,), (0,)), ((), ())), preferred_element_type=F32)


def _dot3(m01, a):
    hi, mid, lo = _split3(a)
    return _dot(m01, hi) + _dot(m01, mid) + _dot(m01, lo)


def _sigmoid(z):
    return 1.0 / (1.0 + jnp.exp(-z))


def _lanes(shape):
    return lax.broadcasted_iota(jnp.int32, shape, len(shape) - 1)


def _place3(lane, base, parts, other):
    out = other
    for j in range(3):
        out = jnp.where(lane == base + j, parts[j], out)
    return out


def _mesh_pos():
    return lax.axis_index("x"), lax.axis_index("y"), lax.axis_index("c")


def _dev_index(px, py, pc):
    return 4 * px + 2 * py + pc


N_GATHER_SEMS = 9


def _gather_stages(src_ref, out_ref, send_sems, recv_sems, local_sem):
    x, y, c = _mesh_pos()
    me, sibling = (x, y, c), (x, y, 1 - c)
    nbr_x, nbr_y, diag = (1 - x, y), (x, 1 - y), (1 - x, 1 - y)
    half = out_ref.shape[-1] // 2
    left, right = pl.ds(0, half), pl.ds(half, half)

    def copy(k, block, to, cols=None, src=None):
        slot = out_ref.at[_dev_index(*block)]
        if cols is not None:
            slot = slot.at[:, cols]
        return pltpu.make_async_remote_copy(
            src_ref=slot if src is None else src, dst_ref=slot, send_sem=send_sems.at[k], recv_sem=recv_sems.at[k],
            device_id=to, device_id_type=MESH)

    mine = pltpu.make_async_copy(src_ref, out_ref.at[_dev_index(*me)], local_sem)
    first = [copy(0, me, sibling, src=src_ref), copy(1, me, (*nbr_x, c), src=src_ref), copy(2, me, (*nbr_y, c), src=src_ref)]
    relay = [(1, nbr_x, None, nbr_x), (2, nbr_y, None, nbr_y), (3, diag, left, nbr_y), (4, diag, right, nbr_x)]
    onward = [copy(3, (*nbr_x, c), (*nbr_y, c), cols=left), copy(4, (*nbr_y, c), (*nbr_x, c), cols=right)]
    passed = [copy(4 + k, (*block, c), sibling, cols=cols) for k, block, cols, _ in relay]

    def start():
        mine.start()
        for cp in first:
            cp.start()

    def relay_stage(first_item):
        def run():
            for j in (first_item, first_item + 1):
                k, block, cols, frm = relay[j]
                copy(k, (*block, c), (*frm, c), cols=cols).wait_recv()
                if j < 2:
                    onward[j].start()
                passed[j].start()
        return run

    def finish():
        copy(0, sibling, me).wait_recv()
        for k, block, cols, _ in relay:
            copy(4 + k, (*block, 1 - c), me, cols=cols).wait_recv()
        for cp in first + onward + passed:
            cp.wait_send()
        mine.wait()

    return start, relay_stage(0), relay_stage(2), finish


N_REDUCE_SEMS = 10
N_SMALL_SEMS = 4 + 7


def _reduce_stages(ins, gs, r1, s2, r2, small, send_sems, recv_sems, rows=None):
    n = len(ins)
    x, y, c = _mesh_pos()
    me = _dev_index(x, y, c)
    sibling = (x, y, 1 - c)
    chips = [(x, y), (1 - x, y), (x, 1 - y), (1 - x, 1 - y)]
    peers = []
    for p in range(1, N_DEV):
        px, py, pc = (p >> 2) & 1, (p >> 1) & 1, p & 1
        peers.append((1 - x if px else x, 1 - y if py else y, 1 - c if pc else c))
    base_small = N_REDUCE_SEMS * n

    def remote(src, dst, k, to):
        return pltpu.make_async_remote_copy(src_ref=src, dst_ref=dst, send_sem=send_sems.at[k],
                                            recv_sem=recv_sems.at[k], device_id=to, device_id_type=MESH)

    def level1(a, q):
        return remote(ins[a].at[_dev_index(*chips[q], 1 - c)], r1[a].at[q], N_REDUCE_SEMS * a + q, sibling)

    def level2(a, k):
        half = ins[a].shape[-1] // 2
        left, right = pl.ds(0, half), pl.ds(half, half)
        nbr_x, nbr_y = (*chips[1], c), (*chips[2], c)
        src_slot, dst_slot, cols, to = [(0, 0, left, nbr_x), (1, 1, right, nbr_y), (2, 2, left, nbr_x),
                                        (2, 2, right, nbr_y), (0, 0, right, nbr_x), (1, 1, left, nbr_y)][k]
        return remote(s2[a].at[src_slot, :, cols], r2[a].at[dst_slot, :, cols], N_REDUCE_SEMS * a + 4 + k, to)

    if small is not None:
        small_ref, total_ref, sm_sib, sm_chip, sm_recv = small
        to_sibling = remote(small_ref, sm_sib, base_small, sibling)
        to_chips = [remote(sm_chip, sm_recv.at[j], base_small + 1 + j, (*chips[j + 1], c)) for j in range(3)]
    if rows is not None:
        rows_ref, land_ref, all_ref = rows
        row_sends = [remote(rows_ref, land_ref.at[me], base_small + 4 + k, to) for k, to in enumerate(peers)]

    def start():
        for a in range(n):
            for q in range(4):
                level1(a, q).start()
        if small is not None:
            to_sibling.start()
        if rows is not None:
            for cp in row_sends:
                cp.start()
            land_ref[me] = rows_ref[...]

    def middle():
        for a in range(n):
            for q in (1, 2, 3, 0):
                level1(a, q).wait_recv()
                pair = ins[a][_dev_index(*chips[q], c)].astype(F32) + r1[a][q].astype(F32)
                if q == 0:
                    gs[a][...] = pair
                else:
                    s2[a][q - 1] = pair.astype(BF16)
                    for k in ((0,), (1,), (2, 3))[q - 1]:
                        level2(a, k).start()
        if small is not None:
            to_sibling.wait_recv()
            sm_chip[...] = small_ref[...] + sm_sib[...]
            for cp in to_chips:
                cp.start()

    def fold():
        for a in range(n):
            half = ins[a].shape[-1] // 2
            level2(a, 3).wait_recv()
            s2[a][0, :, half:] = (s2[a][0, :, half:].astype(F32) + r2[a][2, :, half:].astype(F32)).astype(BF16)
            level2(a, 4).start()
            level2(a, 2).wait_recv()
            s2[a][1, :, :half] = (s2[a][1, :, :half].astype(F32) + r2[a][2, :, :half].astype(F32)).astype(BF16)
            level2(a, 5).start()

    def finish():
        for a in range(n):
            for k in (0, 1, 4, 5):
                level2(a, k).wait_recv()
            gs[a][...] = gs[a][...] + r2[a][0].astype(F32) + r2[a][1].astype(F32)
            for q in range(4):
                level1(a, q).wait_send()
            for k in range(6):
                level2(a, k).wait_send()
        if small is not None:
            for cp in to_chips:
                cp.wait_recv()
            total = None
            for ax in range(2):
                for ay in range(2):
                    dx, dy = x != ax, y != ay
                    term = jnp.where(dx, jnp.where(dy, sm_recv[2], sm_recv[0]), jnp.where(dy, sm_recv[1], sm_chip[...]))
                    total = term if total is None else total + term
            total_ref[...] = total
            for cp in [to_sibling] + to_chips:
                cp.wait_send()
        if rows is not None:
            for k, frm in enumerate(peers):
                remote(rows_ref, land_ref.at[_dev_index(*frm)], base_small + 4 + k, frm).wait_recv()
            all_ref[...] = land_ref[...]
            for cp in row_sends:
                cp.wait_send()

    return start, middle, fold, finish


def _reduce_scratch(shard, small, rows=None):
    out = [pltpu.VMEM((lead,) + shard.shape[1:], BF16) for lead in (4, 3, 3)]
    out += [pltpu.VMEM(small.shape, F32), pltpu.VMEM(small.shape, F32), pltpu.VMEM((3,) + small.shape, F32)]
    if rows is not None:
        out.append(pltpu.VMEM((N_DEV,) + rows.shape, F32))
    return out


def _reduce_grads(gw_in, small, rows):
    def body(in_ref, small_ref, rows_ref, g_ref, total_ref, rows_all_ref,
             r1, s2, r2, sm_sib, sm_chip, sm_recv, rows_land, send_sems, recv_sems):
        stages = _reduce_stages(
            [in_ref], [g_ref], [r1], [s2], [r2], (small_ref, total_ref, sm_sib, sm_chip, sm_recv),
            send_sems, recv_sems, rows=(rows_ref, rows_land, rows_all_ref))
        for stage in stages:
            stage()

    return pl.pallas_call(
        body, name="reduce_grads",
        out_shape=[jax.ShapeDtypeStruct(gw_in.shape[1:], F32), jax.ShapeDtypeStruct(small.shape, F32),
                   jax.ShapeDtypeStruct((N_DEV,) + rows.shape, F32)],
        scratch_shapes=_reduce_scratch(gw_in, small, rows)
        + [pltpu.SemaphoreType.DMA((N_REDUCE_SEMS + N_SMALL_SEMS,))] * 2,
        compiler_params=_params(),
    )(gw_in, small, rows)


def _dot3_rhs(a, b):
    a0, a1, a2 = _split3(a)
    b0, b1, b2 = _split3(b)
    return (_dot(a0, b0) + (_dot(a0, b1) + _dot(a1, b0))
            + (_dot(a0, b2) + _dot(a1, b1) + _dot(a2, b0)))


def _gather_and_ada(c, w_in_rows, w_ada):
    cols = w_ada.shape[1]

    def body(c_ref, w_ref, wa_ref, w_all_ref, sc_ref, ada_ref,
             c_land, part, ada_land, send_sems, recv_sems, local_sem, x_send, x_recv):
        x, y, cc = _mesh_pos()
        me = _dev_index(x, y, cc)
        peers = []
        for p in range(1, N_DEV):
            px, py, pc = (p >> 2) & 1, (p >> 1) & 1, p & 1
            peers.append((1 - x if px else x, 1 - y if py else y, 1 - cc if pc else cc))

        def remote(src, dst, k, to):
            return pltpu.make_async_remote_copy(src_ref=src, dst_ref=dst, send_sem=x_send.at[k], recv_sem=x_recv.at[k],
                                                device_id=to, device_id_type=MESH)

        c_sends = [remote(c_ref, c_land.at[me], k, to) for k, to in enumerate(peers)]
        for cp in c_sends:
            cp.start()
        start, relay_near, relay_far, finish = _gather_stages(w_ref, w_all_ref, send_sems, recv_sems, local_sem.at[0])
        start()
        c_land[me] = c_ref[...]
        for k, frm in enumerate(peers):
            remote(c_ref, c_land.at[_dev_index(*frm)], k, frm).wait_recv()
        c_all = jnp.concatenate([c_land[b] for b in range(N_DEV)], axis=0)
        sc = c_all * _sigmoid(c_all)
        sc_ref[...] = sc
        rows = _dot3_rhs(sc, wa_ref[...])
        for b in range(N_DEV):
            part[b] = rows[b:b + 1, :]
        a_sends = [remote(part.at[_dev_index(*to)], ada_land.at[me], 7 + k, to) for k, to in enumerate(peers)]
        for cp in a_sends:
            cp.start()
        ada_land[me] = part[me]
        for k, frm in enumerate(peers):
            remote(part.at[0], ada_land.at[_dev_index(*frm)], 7 + k, frm).wait_recv()
        ada_ref[...] = ada_land[...]

        relay_near()
        relay_far()
        finish()
        for cp in c_sends + a_sends:
            cp.wait_send()

    vmem = pl.BlockSpec(memory_space=pltpu.VMEM)
    return pl.pallas_call(
        body, name="gather_weights",
        in_specs=[vmem, ANY, vmem], out_specs=[ANY, vmem, vmem],
        out_shape=[jax.ShapeDtypeStruct((N_DEV,) + w_in_rows.shape, w_in_rows.dtype),
                   jax.ShapeDtypeStruct((N_DEV, D), F32), jax.ShapeDtypeStruct((N_DEV, 1, cols), F32)],
        scratch_shapes=[pltpu.VMEM((N_DEV, 1, D), F32), pltpu.VMEM((N_DEV, 1, cols), F32), pltpu.VMEM((N_DEV, 1, cols), F32),
                        pltpu.SemaphoreType.DMA((N_GATHER_SEMS,)), pltpu.SemaphoreType.DMA((N_GATHER_SEMS,)),
                        pltpu.SemaphoreType.DMA((1,)),
                        pltpu.SemaphoreType.DMA((14,)), pltpu.SemaphoreType.DMA((14,))],
        compiler_params=_params(),
    )(c, w_in_rows, w_ada)


def _inproj_forward(x, mod, w_main, w_f, b_main, b_f, tile):
    seq = x.shape[0]
    nt = seq // tile

    def body(x_ref, mod_ref, w_ref, wf_ref, b_ref, bf_ref,
             qp_ref, kp_ref, vp_ref, f_ref, p_ref, ga_ref, gp_ref, u_ref, carry_ref):
        i = pl.program_id(0)

        @pl.when(i == 0)
        def _():
            carry_ref[...] = jnp.zeros_like(carry_ref)

        u = x_ref[...] * mod_ref[0:1, :] + mod_ref[1:2, :]
        ub = u.astype(BF16)
        u_ref[...] = ub

        f = _dot_nt(ub, wf_ref[...]) + bf_ref[...]
        f_ref[...] = f
        lane = _lanes((tile, LANE))
        log_f = jnp.where(lane < N_HEADS, jnp.minimum(f, 0.0) - jnp.log(1.0 + jnp.exp(-jnp.abs(f))), 0.0)
        row = lax.broadcasted_iota(jnp.int32, (tile, tile), 0)
        col = lax.broadcasted_iota(jnp.int32, (tile, tile), 1)
        tri = (row >= col).astype(BF16)
        cum = _dot3(tri, log_f) + carry_ref[0:1, :]
        carry_ref[0:1, :] = cum[tile - 1:tile, :]
        cq = [part.astype(F32) for part in _split3(cum)]
        ck = [part.astype(F32) for part in _split3(-cum)]

        def proj(chunk):
            cols = pl.ds(chunk * COL_CHUNK, COL_CHUNK)
            return _dot_nt(ub, w_ref[cols, :]) + b_ref[:, cols]

        def head_tiles(r):
            for pair in range(N_HEADS // 2):
                both = r[:, pair * LANE:(pair + 1) * LANE]
                yield 2 * pair, both
                yield 2 * pair + 1, pltpu.roll(both, HEAD_DIM, 1)

        for h, val in head_tiles(proj(0)):
            extra = jnp.where((lane >= L_CK) & (lane < L_CK + 3), 1.0, 0.0)
            extra = _place3(lane, L_CQ, [part[:, h:h + 1] for part in cq], extra)
            qp_ref[h] = jnp.where(lane < HEAD_DIM, val * Q_SCALE, extra).astype(BF16)
        for h, val in head_tiles(proj(1)):
            ones = ((lane >= L_CQ) & (lane < L_CQ + 3)) | ((lane >= L_LSE) & (lane < L_LSE + 3))
            extra = _place3(lane, L_CK, [part[:, h:h + 1] for part in ck], jnp.where(ones, 1.0, 0.0))
            kp_ref[h] = jnp.where(lane < HEAD_DIM, val, extra).astype(BF16)
        for h, val in head_tiles(proj(2)):
            extra = jnp.where((lane >= HEAD_DIM) & (lane < HEAD_DIM + 3), -1.0, 0.0)
            vp_ref[h] = jnp.where(lane < HEAD_DIM, val, extra).astype(BF16)
        p_ref[...] = proj(3)
        ga_ref[...] = proj(4)
        gp_ref[...] = proj(5)

    head_block = pl.BlockSpec((N_HEADS, tile, LANE), lambda i: (0, i, 0))
    tok = lambda width: pl.BlockSpec((tile, width), lambda i: (i, 0))
    whole = lambda a: pl.BlockSpec(a.shape, lambda i: (0,) * a.ndim)
    padded = jax.ShapeDtypeStruct((N_HEADS, seq, LANE), BF16)
    half = jax.ShapeDtypeStruct((seq, D_ATT), F32)
    return pl.pallas_call(
        body, name="inproj_forward", grid=(nt,),
        in_specs=[tok(D), whole(mod), whole(w_main), whole(w_f), whole(b_main), whole(b_f)],
        out_specs=[head_block, head_block, head_block, tok(LANE), tok(D_POOL), tok(D_ATT), tok(D_POOL),
                   tok(D)],
        out_shape=[padded, padded, padded, jax.ShapeDtypeStruct((seq, LANE), F32), half, half, half,
                   jax.ShapeDtypeStruct((seq, D), BF16)],
        scratch_shapes=[pltpu.VMEM((8, LANE), F32)],
        compiler_params=_params(("arbitrary",)),
    )(x, mod, w_main, w_f, b_main, b_f)


def _attention_forward(qp, kp, vp, w_out, tile):
    seq = qp.shape[1]
    nb = seq // tile
    steps = (N_HEADS // 2) * nb

    def body(q_ref, k_ref, v_ref, wo_ref, att_ref, q2t_ref, wo_all_ref, s_a, s_b, m_ref, acc_ref,
             send_sems, recv_sems, local_sem):
        step = pl.program_id(0) * nb + pl.program_id(1)
        start, relay_near, relay_far, finish = _gather_stages(wo_ref, wo_all_ref, send_sems, recv_sems, local_sem.at[0])
        pl.when(step == 0)(start)
        pl.when(step == steps // 4)(relay_near)
        pl.when(step == (3 * steps) // 4)(relay_far)

        i = pl.program_id(1)
        sub = lax.broadcasted_iota(jnp.int32, (LANE, tile), 0)
        row = lax.broadcasted_iota(jnp.int32, (tile, tile), 0)
        col = lax.broadcasted_iota(jnp.int32, (tile, tile), 1)
        q = [q_ref[0], q_ref[1]]

        def scores(buf, kb):
            rows = pl.ds(pl.multiple_of(kb * tile, tile), tile)
            for hh in range(2):
                buf[hh] = _dot_nt(k_ref[hh, rows, :], q[hh])

        def absorb(buf, kb, masked):
            rows = pl.ds(pl.multiple_of(kb * tile, tile), tile)
            for hh in range(2):
                m = m_ref[hh, 0:1, :]
                s = buf[hh]
                if masked:
                    s = jnp.where(row <= col, s, -1e30)
                m_new = jnp.maximum(m, jnp.max(s, axis=0, keepdims=True))
                p = jnp.exp(s - m_new).astype(BF16)
                acc_ref[hh] = jnp.exp(m - m_new) * acc_ref[hh] + _dot_tn(v_ref[hh, rows, :], p)
                m_ref[hh, 0:1, :] = m_new

        def two_blocks(j, _):
            scores(s_b, 2 * j + 1)
            absorb(s_a, 2 * j, False)
            scores(s_a, 2 * j + 2)
            absorb(s_b, 2 * j + 1, False)
            return 0

        def last_block():
            absorb(s_a, i, True)

        def last_two_blocks():
            scores(s_b, i)
            absorb(s_a, i - 1, False)
            absorb(s_b, i, True)

        scores(s_a, 0)
        m_ref[...] = jnp.full(m_ref.shape, -1e30, F32)
        acc_ref[...] = jnp.zeros_like(acc_ref)
        lax.fori_loop(0, i // 2, two_blocks, 0)
        lax.cond(i % 2 == 0, last_block, last_two_blocks)
        outs = []
        for hh in range(2):
            m, acc = m_ref[hh, 0:1, :], acc_ref[hh]
            l = -acc[HEAD_DIM:HEAD_DIM + 1, :]
            outs.append((acc / l)[:HEAD_DIM, :])
            neg_lse = [part.astype(F32) for part in _split3(-(m + jnp.log(l)))]
            q2t_ref[hh] = _place3(sub, L_LSE, neg_lse, q[hh].astype(F32).T).astype(BF16)
        att_ref[...] = jnp.concatenate(outs, axis=0).T
        pl.when(step == steps - 1)(finish)

    pair = pl.BlockSpec((2, tile, LANE), lambda hp, i: (hp, i, 0))
    full = pl.BlockSpec((2, seq, LANE), lambda hp, i: (hp, 0, 0))
    return pl.pallas_call(
        body, name="attention_forward", grid=(N_HEADS // 2, nb),
        in_specs=[pair, full, full, ANY],
        out_specs=[pl.BlockSpec((tile, LANE), lambda hp, i: (i, hp)),
                   pl.BlockSpec((2, LANE, tile), lambda hp, i: (hp, 0, i)), ANY],
        out_shape=[jax.ShapeDtypeStruct((seq, D_ATT), F32),
                   jax.ShapeDtypeStruct((N_HEADS, LANE, seq), BF16),
                   jax.ShapeDtypeStruct((N_DEV,) + w_out.shape, w_out.dtype)],
        scratch_shapes=[pltpu.VMEM((2, tile, tile), F32), pltpu.VMEM((2, tile, tile), F32),
                        pltpu.VMEM((2, 8, tile), F32), pltpu.VMEM((2, LANE, tile), F32),
                        pltpu.SemaphoreType.DMA((N_GATHER_SEMS,)), pltpu.SemaphoreType.DMA((N_GATHER_SEMS,)),
                        pltpu.SemaphoreType.DMA((1,))],
        compiler_params=_params(("arbitrary", "arbitrary")),
    )(qp, kp, vp, w_out)


def _window_sum(x, halo, window, transposed):
    tile = x.shape[0]

    def split_cat(a):
        hi = a.astype(BF16)
        return jnp.concatenate([hi, (a - hi.astype(F32)).astype(BF16)], axis=1)

    def fold(r):
        return r[:, :LANE] + r[:, LANE:]

    r = lax.broadcasted_iota(jnp.int32, (tile, tile), 0)
    c = lax.broadcasted_iota(jnp.int32, (tile, tile), 1)
    rh = lax.broadcasted_iota(jnp.int32, (HALO, HALO), 0)
    ch = lax.broadcasted_iota(jnp.int32, (HALO, HALO), 1)
    if not transposed:
        band = (c <= r) & (r - c < window)
        edge = (rh + HALO - ch) < window
    else:
        band = (r <= c) & (c - r < window)
        edge = (HALO + ch - rh) < window
    out = fold(_dot(band.astype(BF16), split_cat(x)))
    reach = fold(_dot(edge.astype(BF16), split_cat(halo)))
    if not transposed:
        return jnp.concatenate([out[:HALO] + reach, out[HALO:]], axis=0)
    return jnp.concatenate([out[:tile - HALO], out[tile - HALO:] + reach], axis=0)


def _silu_parts(g):
    sig = _sigmoid(g)
    return g * sig, sig * (1.0 + g * (1.0 - sig))


def _middle(x, tgt, att, g_att, g_pool, p, vecs, pool_vecs, w_out, w_pool, tile):
    seq = x.shape[0]
    nt = seq // tile
    halo_blocks = tile // HALO

    def body(x_ref, tgt_ref, att_ref, ga_ref, gp_ref, p_ref, ph_ref, vec_ref, pvec_ref, wo_ref, wp_ref,
             dxa_ref, do2_ref, dga_ref, dgp_ref, dpooled_ref, dwo_ref, dwp_ref, dvec_ref, dpvec_ref):
        i = pl.program_id(0)

        @pl.when(i == 0)
        def _():
            dwo_ref[...] = jnp.zeros_like(dwo_ref)
            dwp_ref[...] = jnp.zeros_like(dwp_ref)
            dvec_ref[...] = jnp.zeros_like(dvec_ref)
            dpvec_ref[...] = jnp.zeros_like(dpvec_ref)

        gate, b_out, ln_g, ln_b = (vec_ref[k:k + 1, :] for k in range(4))
        b_pool, pool_scale = pvec_ref[0:1, :], pvec_ref[1:2, :]
        x = x_ref[...]
        p = p_ref[...]
        p_halo = ph_ref[...] * jnp.where(i > 0, 1.0, 0.0)
        pos = i * tile + lax.broadcasted_iota(jnp.int32, (tile, 1), 0) + 1

        pooled, mixed = [], []
        for g, window in enumerate(POOL_WINDOWS):
            cols = slice(g * GROUP_DIM, (g + 1) * GROUP_DIM)
            wsum = _window_sum(p[:, cols], p_halo[:, cols], window, False)
            count = jnp.minimum(pos, window).astype(F32)
            pooled.append(wsum / count - p[:, cols])
            mixed.append(_dot(pooled[g].astype(BF16), wp_ref[g]) + b_pool[:, cols])
        mixed = jnp.concatenate(mixed, axis=1)
        pool = mixed * pool_scale

        att = att_ref[...]
        g_att, g_pool = ga_ref[...], gp_ref[...]
        silu_a, dsilu_a = _silu_parts(g_att)
        silu_p, dsilu_p = _silu_parts(g_pool)
        y_in = jnp.concatenate([att * silu_a, pool * silu_p], axis=1)
        y = _dot(y_in.astype(BF16), wo_ref[...]) + b_out
        h = ALPHA * x + gate * y
        mu = jnp.mean(h, axis=1, keepdims=True)
        hc = h - mu
        var = jnp.mean(hc * hc, axis=1, keepdims=True)
        rstd = lax.rsqrt(var + LN_EPS)
        yhat = hc * rstd
        diff = yhat * ln_g + ln_b - tgt_ref[...]
        loss_rows = jnp.sum(diff * diff, axis=1, keepdims=True)
        d_out = diff * (1.0 / D)

        d_yhat = d_out * ln_g
        dh = rstd * (d_yhat - jnp.mean(d_yhat, axis=1, keepdims=True)
                     - yhat * jnp.mean(d_yhat * yhat, axis=1, keepdims=True))
        dxa_ref[...] = ALPHA * dh
        dy = dh * gate
        dyb = dy.astype(BF16)
        lane = _lanes((1, D))
        loss_row = jnp.where(lane == 0, (0.5 / D) * jnp.sum(loss_rows, axis=0, keepdims=True), 0.0)
        dvec_ref[0:1, :] += jnp.sum(dh * y, axis=0, keepdims=True)
        dvec_ref[1:2, :] += jnp.sum(dy, axis=0, keepdims=True)
        dvec_ref[2:3, :] += jnp.sum(d_out * yhat, axis=0, keepdims=True)
        dvec_ref[3:4, :] += jnp.sum(d_out, axis=0, keepdims=True)
        dvec_ref[4:5, :] += loss_row

        dwo_ref[...] += _dot(y_in.T.astype(BF16), dyb)
        d_yin = _dot_nt(dyb, wo_ref[...])
        d_a, d_pl = d_yin[:, :D_ATT], d_yin[:, D_ATT:]
        d_att = d_a * silu_a
        d_att_t = d_att.T
        prod_t = (d_att * att).T
        sub = lax.broadcasted_iota(jnp.int32, (HEAD_DIM, tile), 0)
        for h in range(N_HEADS):
            rows = slice(h * HEAD_DIM, (h + 1) * HEAD_DIM)
            delta = jnp.sum(prod_t[rows], axis=0, keepdims=True)
            extra = _place3(sub, 0, [part.astype(F32) for part in _split3(delta)], 0.0)
            do2_ref[h] = jnp.concatenate([d_att_t[rows], extra], axis=0).astype(BF16)
        dga_ref[...] = d_a * att * dsilu_a
        dgp_ref[...] = d_pl * pool * dsilu_p
        d_pool = d_pl * silu_p
        d_mixed = d_pool * pool_scale
        dpvec_ref[0:1, :] += jnp.sum(d_mixed, axis=0, keepdims=True)
        dpvec_ref[1:2, :] += jnp.sum(d_pool * mixed, axis=0, keepdims=True)
        d_pooled = []
        for g in range(len(POOL_WINDOWS)):
            cols = slice(g * GROUP_DIM, (g + 1) * GROUP_DIM)
            dmb = d_mixed[:, cols].astype(BF16)
            dwp_ref[g] += _dot(pooled[g].T.astype(BF16), dmb)
            d_pooled.append(_dot_nt(dmb, wp_ref[g]))
        dpooled_ref[...] = jnp.concatenate(d_pooled, axis=1)

    tok = lambda width: pl.BlockSpec((tile, width), lambda i: (i, 0))
    whole = lambda a: pl.BlockSpec(a.shape, lambda i: (0,) * a.ndim)
    halo = pl.BlockSpec((HALO, D_POOL), lambda i: (jnp.maximum(i * halo_blocks - 1, 0), 0))
    half = jax.ShapeDtypeStruct((seq, D_ATT), F32)
    outs = [jax.ShapeDtypeStruct((seq, D), F32), jax.ShapeDtypeStruct((N_HEADS, LANE, seq), BF16), half, half, half,
            jax.ShapeDtypeStruct(w_out.shape, F32), jax.ShapeDtypeStruct(w_pool.shape, F32),
            jax.ShapeDtypeStruct(vecs.shape, F32), jax.ShapeDtypeStruct(pool_vecs.shape, F32)]
    return pl.pallas_call(
        body, name="middle", grid=(nt,),
        in_specs=[tok(D), tok(D), tok(D_ATT), tok(D_ATT), tok(D_POOL), tok(D_POOL), halo,
                  whole(vecs), whole(pool_vecs), whole(w_out), whole(w_pool)],
        out_specs=[tok(D), pl.BlockSpec((N_HEADS, LANE, tile), lambda i: (0, 0, i)),
                   tok(D_ATT), tok(D_POOL), tok(D_POOL),
                   whole(w_out), whole(w_pool), whole(vecs), whole(pool_vecs)],
        out_shape=outs,
        compiler_params=_params(("arbitrary",)),
    )(x, tgt, att, g_att, g_pool, p, p, vecs, pool_vecs, w_out, w_pool)


def _attention_backward(q2t, kp, vp, do2t, gw_out, small, tile):
    seq = kp.shape[1]
    nb = seq // tile
    last = N_HEADS // 2 - 1

    def body(qt_ref, k_ref, v_ref, dot_ref, gwo_hbm, small_hbm,
             dq_ref, dk_ref, dv_ref, dcum_ref, g_out_ref, total_ref,
             dq_acc, dk_acc, dv_acc, gwo_ref, r1, s2, r2, sm_sib, sm_chip, sm_recv, small_ref, send_sems, recv_sems):
        hp = pl.program_id(0)
        start, middle, fold, finish = _reduce_stages(
            [gwo_ref], [g_out_ref], [r1], [s2], [r2], (small_ref, total_ref, sm_sib, sm_chip, sm_recv),
            send_sems, recv_sems)

        @pl.when(hp == 0)
        def _():
            pltpu.sync_copy(gwo_hbm, gwo_ref)
            pltpu.sync_copy(small_hbm, small_ref)
            start()

        pl.when(hp == 1)(middle)
        pl.when(hp == 2)(fold)

        row = lax.broadcasted_iota(jnp.int32, (tile, tile), 0)
        col = lax.broadcasted_iota(jnp.int32, (tile, tile), 1)
        dq_acc[...] = jnp.zeros_like(dq_acc)

        def kv_block(kb, _):
            krows = pl.ds(pl.multiple_of(kb * tile, tile), tile)
            k = [k_ref[hh, krows, :] for hh in range(2)]
            v = [v_ref[hh, krows, :] for hh in range(2)]
            k_t = [k[hh].T for hh in range(2)]

            def q_block(qb, masked):
                qcols = pl.ds(pl.multiple_of(qb * tile, tile), tile)
                for hh in range(2):
                    q_t = qt_ref[hh, :, qcols]
                    do_t = dot_ref[hh, :, qcols]
                    s_t = _dot(k[hh], q_t)
                    if masked:
                        s_t = jnp.where(row <= col, s_t, -1e30)
                    p_t = jnp.exp(s_t)
                    ds_t = (p_t * _dot(v[hh], do_t)).astype(BF16)
                    dv_new = _dot_nt(do_t, p_t.astype(BF16))
                    dk_new = _dot_nt(q_t, ds_t)
                    if masked:
                        dv_acc[hh], dk_acc[hh] = dv_new, dk_new
                    else:
                        dv_acc[hh] += dv_new
                        dk_acc[hh] += dk_new
                    dq_acc[hh, :, qcols] += _dot(k_t[hh], ds_t)

            q_block(kb, True)

            def two_later_blocks(j, _):
                q_block(kb + 1 + 2 * j, False)
                q_block(kb + 2 + 2 * j, False)
                return 0

            later = nb - 1 - kb
            lax.fori_loop(0, later // 2, two_later_blocks, 0)
            pl.when(later % 2 == 1)(lambda: q_block(nb - 1, False))
            for hh in range(2):
                dk = dk_acc[hh]
                dk_ref[hh, :, krows] = dk.astype(BF16)
                dv_ref[hh, :, krows] = dv_acc[hh].astype(BF16)
                dcum_ref[hh, :, krows] = -dk[L_CK:L_CK + 1, :]
            return 0

        lax.fori_loop(0, nb, kv_block, 0)
        for hh in range(2):
            dq = dq_acc[hh]
            dcum_ref[hh] += dq[L_CQ:L_CQ + 1, :]
            dq_ref[hh] = (dq * Q_SCALE).astype(BF16)
        pl.when(hp == last)(finish)

    pair = pl.BlockSpec((2, seq, LANE), lambda hp: (hp, 0, 0))
    pair_t = pl.BlockSpec((2, LANE, seq), lambda hp: (hp, 0, 0))
    whole = lambda shape: pl.BlockSpec(shape, lambda hp: (0,) * len(shape))
    grad = jax.ShapeDtypeStruct((N_HEADS, LANE, seq), BF16)
    return pl.pallas_call(
        body, name="attention_backward", grid=(N_HEADS // 2,),
        in_specs=[pair_t, pair, pair, pair_t, ANY, ANY],
        out_specs=[pair_t, pair_t, pair_t, pl.BlockSpec((2, 1, seq), lambda hp: (hp, 0, 0)),
                   whole(gw_out.shape[1:]), whole(small.shape)],
        out_shape=[grad, grad, grad, jax.ShapeDtypeStruct((N_HEADS, 1, seq), F32),
                   jax.ShapeDtypeStruct(gw_out.shape[1:], F32), jax.ShapeDtypeStruct(small.shape, F32)],
        scratch_shapes=[pltpu.VMEM((2, LANE, seq), F32), pltpu.VMEM((2, LANE, tile), F32),
                        pltpu.VMEM((2, LANE, tile), F32), pltpu.VMEM(gw_out.shape, BF16)]
        + _reduce_scratch(gw_out, small)
        + [pltpu.VMEM(small.shape, F32),
           pltpu.SemaphoreType.DMA((N_REDUCE_SEMS + N_SMALL_SEMS,)), pltpu.SemaphoreType.DMA((N_REDUCE_SEMS + N_SMALL_SEMS,))],
        compiler_params=_params(("arbitrary",)),
    )(q2t, kp, vp, do2t, gw_out, small)


def _inproj_backward(dqp, dkp, dvp, d_cum, f, d_pooled, d_ga, d_gp, x, dxa, u, mod, w_main, w_f, tile):
    seq = x.shape[0]
    nt = seq // tile
    halo_blocks = tile // HALO

    def body(dq_ref, dk_ref, dv_ref, dcum_ref, f_ref, dpo_ref, dph_ref, dga_ref, dgp_ref, x_ref, dxa_ref, u_ref,
             mod_ref, w_ref, wf_ref,
             dx_ref, dproj_ref, dwf_ref, db_ref, dbf_ref, dmod_ref, carry_ref):
        step = pl.program_id(0)
        i = nt - 1 - step

        @pl.when(step == 0)
        def _():
            carry_ref[...] = jnp.zeros_like(carry_ref)
            dwf_ref[...] = jnp.zeros_like(dwf_ref)
            db_ref[...] = jnp.zeros_like(db_ref)
            dbf_ref[...] = jnp.zeros_like(dbf_ref)
            dmod_ref[...] = jnp.zeros_like(dmod_ref)

        ones = jnp.ones((8, tile), BF16)

        def emit(chunk, val):
            cols = pl.ds(chunk * COL_CHUNK, COL_CHUNK)
            db_ref[0:1, cols] += jnp.sum(val, axis=0, keepdims=True)
            vb = val.astype(BF16)
            dproj_ref[:, pl.ds((chunk - 3) * COL_CHUNK, COL_CHUNK)] = vb
            return _dot(vb, w_ref[cols, :])

        d_u = jnp.zeros((tile, D), F32)
        for chunk, ref in enumerate((dq_ref, dk_ref, dv_ref)):
            cols = pl.ds(chunk * COL_CHUNK, COL_CHUNK)
            val_t = ref[:, 0:HEAD_DIM, :].reshape(COL_CHUNK, tile)
            db_ref[:, cols] += _dot_nt(ones, val_t)
            d_u += _dot_tn(val_t, w_ref[cols, :])

        d_pooled = dpo_ref[...]
        d_halo = dph_ref[...] * jnp.where(i < nt - 1, 1.0, 0.0)
        pos = i * tile + lax.broadcasted_iota(jnp.int32, (tile, 1), 0) + 1
        d_p = []
        for g, window in enumerate(POOL_WINDOWS):
            cols = slice(g * GROUP_DIM, (g + 1) * GROUP_DIM)
            scaled = d_pooled[:, cols] / jnp.minimum(pos, window).astype(F32)
            d_p.append(_window_sum(scaled, d_halo[:, cols] * (1.0 / window), window, True) - d_pooled[:, cols])
        d_u += emit(3, jnp.concatenate(d_p, axis=1))
        d_u += emit(4, dga_ref[...])
        d_u += emit(5, dgp_ref[...])

        row = lax.broadcasted_iota(jnp.int32, (tile, tile), 0)
        col = lax.broadcasted_iota(jnp.int32, (tile, tile), 1)
        later = (row >= col).astype(BF16)
        d_logf = sum(_dot(part, later) for part in _split3(dcum_ref[:, 0, :])) + carry_ref[:, 0:1]
        carry_ref[:, 0:1] = d_logf[:, 0:1]
        d_f = d_logf * _sigmoid(-f_ref[...].T[0:N_HEADS, :])
        d_f = jnp.concatenate([d_f, jnp.zeros((LANE - N_HEADS, tile), F32)], axis=0)
        dbf_ref[...] += sum(_dot_nt(ones, part) for part in _split3(d_f))
        d_fb = d_f.astype(BF16)
        d_u += _dot_tn(d_fb, wf_ref[...])
        dwf_ref[...] += _dot(d_fb, u_ref[...])

        x = x_ref[...]
        dx_ref[...] = dxa_ref[...] + d_u * mod_ref[0:1, :]
        dmod_ref[0:1, :] += jnp.sum(d_u * x, axis=0, keepdims=True)
        dmod_ref[1:2, :] += jnp.sum(d_u, axis=0, keepdims=True)

    rev = lambda step: nt - 1 - step
    tok = lambda width: pl.BlockSpec((tile, width), lambda s: (rev(s), 0))
    head_block = pl.BlockSpec((N_HEADS, LANE, tile), lambda s: (0, 0, rev(s)))
    whole = lambda a: pl.BlockSpec(a.shape, lambda s: (0,) * a.ndim)
    halo = pl.BlockSpec((HALO, D_POOL), lambda s: (jnp.minimum((rev(s) + 1) * halo_blocks, seq // HALO - 1), 0))
    small = lambda width: jax.ShapeDtypeStruct((8, width), F32)
    n_rest = N_MAIN - OFF_P
    return pl.pallas_call(
        body, name="inproj_backward", grid=(nt,),
        in_specs=[head_block, head_block, head_block, pl.BlockSpec((N_HEADS, 1, tile), lambda s: (0, 0, rev(s))),
                  tok(LANE), tok(D_POOL), halo, tok(D_ATT), tok(D_POOL),
                  tok(D), tok(D), tok(D),
                  whole(mod), whole(w_main), whole(w_f)],
        out_specs=[tok(D), tok(n_rest), pl.BlockSpec((LANE, D), lambda s: (0, 0)),
                   pl.BlockSpec((8, N_MAIN), lambda s: (0, 0)), pl.BlockSpec((8, LANE), lambda s: (0, 0)),
                   pl.BlockSpec((8, D), lambda s: (0, 0))],
        out_shape=[jax.ShapeDtypeStruct((seq, D), F32), jax.ShapeDtypeStruct((seq, n_rest), BF16),
                   jax.ShapeDtypeStruct((LANE, D), F32), small(N_MAIN), small(LANE), small(D)],
        scratch_shapes=[pltpu.VMEM((8, LANE), F32)],
        compiler_params=_params(("arbitrary",)),
    )(dqp, dkp, dvp, d_cum, f, d_pooled, d_pooled, d_ga, d_gp, x, dxa, u, mod, w_main, w_f)


def _weight_grad(dproj, u, k_tile):
    seq, n_cols = dproj.shape
    nk = seq // k_tile

    def body(dp_ref, u_ref, out_ref):
        @pl.when(pl.program_id(1) == 0)
        def _():
            out_ref[...] = jnp.zeros_like(out_ref)

        out_ref[...] += _dot_tn(dp_ref[...], u_ref[...])

    return pl.pallas_call(
        body, name="weight_grad", grid=(n_cols // COL_CHUNK, nk),
        in_specs=[pl.BlockSpec((k_tile, COL_CHUNK), lambda n, k: (k, n)),
                  pl.BlockSpec((k_tile, D), lambda n, k: (k, 0))],
        out_specs=pl.BlockSpec((COL_CHUNK, D), lambda n, k: (n, 0)),
        out_shape=jax.ShapeDtypeStruct((n_cols, D), F32),
        compiler_params=_params(("arbitrary", "arbitrary")),
    )(dproj, u)


def _weight_grad_heads(grad_t, u, k_tile, name):
    seq = u.shape[0]
    nk = seq // k_tile

    def body(g_ref, u_ref, out_ref):
        @pl.when(pl.program_id(0) == 0)
        def _():
            out_ref[...] = jnp.zeros_like(out_ref)

        out_ref[...] += _dot(g_ref[...].reshape(N_HEADS * HEAD_DIM, k_tile), u_ref[...])

    return pl.pallas_call(
        body, name=name, grid=(nk,),
        in_specs=[pl.BlockSpec((N_HEADS, HEAD_DIM, k_tile), lambda k: (0, 0, k)),
                  pl.BlockSpec((k_tile, D), lambda k: (k, 0))],
        out_specs=pl.BlockSpec((N_HEADS * HEAD_DIM, D), lambda k: (0, 0)),
        out_shape=jax.ShapeDtypeStruct((N_HEADS * HEAD_DIM, D), F32),
        compiler_params=_params(("arbitrary",)),
    )(grad_t, u)


def _adamw(w, g, m, v):
    m = ADAM_B1 * m + (1.0 - ADAM_B1) * g
    v = ADAM_B2 * v + (1.0 - ADAM_B2) * (g * g)
    m_hat = m / (1.0 - ADAM_B1 ** ADAM_STEP)
    v_hat = v / (1.0 - ADAM_B2 ** ADAM_STEP)
    delta = -ADAM_LR * (m_hat / (jnp.sqrt(v_hat) + ADAM_EPS) + ADAM_WD * w)
    return delta, m, v


def _adamw_call(g, w, m, v, lead_tile, name):
    nr = w.shape[0] // lead_tile

    def body(gi_ref, w_ref, m_ref, v_ref, g_ref, d_ref, nm_ref, nv_ref):
        g = gi_ref[...]
        g_ref[...] = g
        d_ref[...], nm_ref[...], nv_ref[...] = _adamw(w_ref[...], g, m_ref[...], v_ref[...])

    blk = pl.BlockSpec((lead_tile,) + w.shape[1:], lambda r: (r,) + (0,) * (w.ndim - 1))
    shape = jax.ShapeDtypeStruct(w.shape, F32)
    return pl.pallas_call(
        body, name=name, grid=(nr,),
        in_specs=[blk, blk, blk, blk], out_specs=[blk, blk, blk, blk],
        out_shape=[shape, shape, shape, shape],
        compiler_params=_params(("arbitrary",)),
    )(g, w, m, v)


def _sum_adamw(parts, w, m, v, row_tile, name):
    rows, cols = w.shape
    nr = rows // row_tile

    def body(parts_ref, w_ref, m_ref, v_ref, g_ref, d_ref, nm_ref, nv_ref):
        g = parts_ref[0]
        for k in range(1, N_DEV):
            g = g + parts_ref[k]
        g_ref[...] = g
        d_ref[...], nm_ref[...], nv_ref[...] = _adamw(w_ref[...], g, m_ref[...], v_ref[...])

    blk = pl.BlockSpec((row_tile, cols), lambda r: (r, 0))
    shape = jax.ShapeDtypeStruct(w.shape, F32)
    return pl.pallas_call(
        body, name=name, grid=(nr,),
        in_specs=[pl.BlockSpec((N_DEV, row_tile, cols), lambda r: (0, r, 0)), blk, blk, blk],
        out_specs=[blk, blk, blk, blk],
        out_shape=[shape, shape, shape, shape],
        compiler_params=_params(("arbitrary",)),
    )(parts, w, m, v)


def _ada_adamw(sc_t, d_ada, w, m, v):
    def body(sc_ref, d_ref, w_ref, m_ref, v_ref, g_ref, dl_ref, nm_ref, nv_ref):
        g = sc_ref[:, 0:1] * d_ref[0:1, :]
        for b in range(1, N_DEV):
            g = g + sc_ref[:, b:b + 1] * d_ref[b:b + 1, :]
        g_ref[...] = g
        dl_ref[...], nm_ref[...], nv_ref[...] = _adamw(w_ref[...], g, m_ref[...], v_ref[...])

    shape = jax.ShapeDtypeStruct(w.shape, F32)
    return pl.pallas_call(
        body, name="ada_adamw", out_shape=[shape, shape, shape, shape], compiler_params=_params(),
    )(sc_t, d_ada, w, m, v)


F_LO, F_HI = 3 * D_ATT, 3 * D_ATT + N_HEADS


def _split_forget(a, axis):
    idx = lambda lo, hi: tuple(slice(lo, hi) if d == axis else slice(None) for d in range(a.ndim))
    pad = [(0, LANE - N_HEADS) if d == axis else (0, 0) for d in range(a.ndim)]
    return jnp.concatenate([a[idx(0, F_LO)], a[idx(F_HI, D_IN)]], axis=axis), jnp.pad(a[idx(F_LO, F_HI)], pad)


def _join_forget(main, f, axis):
    idx = lambda lo, hi: tuple(slice(lo, hi) if d == axis else slice(None) for d in range(main.ndim))
    return jnp.concatenate([main[idx(0, F_LO)], f[idx(0, N_HEADS)], main[idx(F_LO, N_MAIN)]], axis=axis)


_EARLY = ((("w_pool", 65536), ("b_pool", 512), ("pool_scale", 512), ("b_out", 1024), ("ln_g", 1024), ("ln_b", 1024),
           ("loss", 128)), 552)
_LATE = ((("b_in", 3200),), 32)


def _pack_small(parts, layout):
    spec, rows = layout
    flat = []
    for name, size in spec:
        a = parts[name].reshape(-1)
        flat.append(jnp.pad(a, (0, size - a.shape[0])))
    flat = jnp.concatenate(flat)
    flat = jnp.pad(flat, (0, rows * LANE - flat.shape[0]))
    return flat.reshape(rows, LANE)


def _unpack_small(packed, shapes, layout):
    flat = packed.reshape(-1)
    out, off = {}, 0
    for name, size in layout[0]:
        n = 1
        for s in shapes[name]:
            n *= s
        out[name] = flat[off:off + n].reshape(shapes[name])
        off += size
    return out


def kernel(x, c, w_ada, b_ada, w_in, b_in, w_pool_mix, b_pool_mix, pool_scale, w_out, b_out, ln_g, ln_b, loss_target, m_w_ada, m_b_ada, m_w_in, m_b_in, m_w_pool_mix, m_b_pool_mix, m_pool_scale, m_w_out, m_b_out, m_ln_g, m_ln_b, v_w_ada, v_b_ada, v_w_in, v_b_in, v_w_pool_mix, v_b_pool_mix, v_pool_scale, v_w_out, v_b_out, v_ln_g, v_ln_b):
    seq = x.shape[1]
    tile = min(256, seq)
    attn_tile = min(512, max(128, seq // 4))
    me = _dev_index(*_mesh_pos())
    x2, tgt = x[0], loss_target[0]

    rows_in = D_IN // N_DEV
    w_in_g, sc_all, ada_mine = _gather_and_ada(c, w_in[0].T.astype(BF16), w_ada[0])
    ada = ada_mine.reshape(1, D_ADA) + b_ada
    shift, scale, gate = ada[:, 0:D], ada[:, D:2 * D], ada[:, 2 * D:]
    mod = jnp.concatenate([1.0 + scale, shift, jnp.zeros((6, D), F32)], axis=0)

    w_main, w_f = _split_forget(w_in_g.reshape(D_IN, D), 0)
    b_main, b_f = _split_forget(b_in, 1)

    qp, kp, vp, f, p, g_att, g_pool, u = _inproj_forward(x2, mod, w_main, w_f, b_main, b_f, tile)
    att, q2t, w_out_g = _attention_forward(qp, kp, vp, w_out[0].astype(BF16), attn_tile)

    vecs = jnp.concatenate([gate, b_out, ln_g, ln_b, jnp.zeros((4, D), F32)], axis=0)
    pool_vecs = jnp.concatenate([b_pool_mix.reshape(1, D_POOL), pool_scale, jnp.zeros((6, D_POOL), F32)], axis=0)
    dxa, do2, d_ga, d_gp, d_pooled, dw_out, dw_pool, dvec, dpvec = _middle(
        x2, tgt, att, g_att, g_pool, p, vecs, pool_vecs, w_out_g.reshape(D, D), w_pool_mix[0].astype(BF16), tile)

    gw_out = dw_out.reshape(N_DEV, D // N_DEV, D).astype(BF16)
    early = _pack_small({"w_pool": dw_pool, "b_pool": dpvec[0:1], "pool_scale": dpvec[1:2], "b_out": dvec[1:2],
                         "ln_g": dvec[2:3], "ln_b": dvec[3:4], "loss": dvec[4:5, 0:LANE]}, _EARLY)
    dqp, dkp, dvp, d_cum, g_out, early_sum = _attention_backward(q2t, kp, vp, do2, gw_out, early, attn_tile)
    dx, dproj, dw_f, db_main, db_f, dmod = _inproj_backward(
        dqp, dkp, dvp, d_cum, f, d_pooled, d_ga, d_gp, x2, dxa, u, mod, w_main, w_f, tile)
    k_tile = min(1024, seq)
    dw_q, dw_k, dw_v = (_weight_grad_heads(g, u, k_tile, "weight_grad_" + n)
                        for g, n in ((dqp, "q"), (dkp, "k"), (dvp, "v")))
    dw_rest = _weight_grad(dproj, u, k_tile)

    dw_main = jnp.concatenate([dw_q, dw_k, dw_v, dw_rest], axis=0)
    gw_in = _join_forget(dw_main, dw_f, 0).reshape(N_DEV, rows_in, D).astype(BF16)
    d_ada = jnp.concatenate([dmod[1:2], dmod[0:1], dvec[0:1]], axis=1)
    late = _pack_small({"b_in": _join_forget(db_main[0:1], db_f[0:1], 1)}, _LATE)
    g_in_rows, late_sum, d_ada_all = _reduce_grads(gw_in, late, d_ada.reshape(D_ADA // LANE, LANE))

    def rows3(a):
        return a[0].T.reshape(rows_in, D // LANE, LANE)

    outs_in = _adamw_call(g_in_rows.reshape(rows_in, D // LANE, LANE), rows3(w_in), rows3(m_w_in), rows3(v_w_in),
                          rows_in // 5, "adamw_w_in")
    g_w_in, d_w_in, nm_w_in, nv_w_in = (a.reshape(rows_in, D).T for a in outs_in)
    g_w_out, d_w_out, nm_w_out, nv_w_out = _adamw_call(g_out, w_out[0], m_w_out[0], v_w_out[0], D // N_DEV, "adamw_w_out")

    def small_adamw(grad_sum, weights, first, second, layout, name):
        packed = _adamw_call(grad_sum, _pack_small(weights, layout), _pack_small(first, layout),
                             _pack_small(second, layout), layout[1], name)
        shapes = {k: a.shape for k, a in weights.items()}
        return [_unpack_small(a, shapes, layout) for a in packed]

    zero = jnp.zeros((1,), F32)
    smalls = small_adamw(
        early_sum,
        {"w_pool": w_pool_mix, "b_pool": b_pool_mix, "pool_scale": pool_scale, "b_out": b_out, "ln_g": ln_g,
         "ln_b": ln_b, "loss": zero},
        {"w_pool": m_w_pool_mix, "b_pool": m_b_pool_mix, "pool_scale": m_pool_scale, "b_out": m_b_out,
         "ln_g": m_ln_g, "ln_b": m_ln_b, "loss": zero},
        {"w_pool": v_w_pool_mix, "b_pool": v_b_pool_mix, "pool_scale": v_pool_scale, "b_out": v_b_out,
         "ln_g": v_ln_g, "ln_b": v_ln_b, "loss": zero}, _EARLY, "adamw_small")
    b_ins = small_adamw(late_sum, {"b_in": b_in}, {"b_in": m_b_in}, {"b_in": v_b_in}, _LATE, "adamw_b_in")
    g_s, d_s, nm_s, nv_s = ({**a, **b} for a, b in zip(smalls, b_ins))
    loss = g_s["loss"][0]

    ada_rows = D_ADA // LANE
    b_ada_outs = _sum_adamw(d_ada_all, b_ada.reshape(ada_rows, LANE), m_b_ada.reshape(ada_rows, LANE),
                            v_b_ada.reshape(ada_rows, LANE), ada_rows, "adamw_b_ada")
    g_b_ada, d_b_ada, nm_b_ada, nv_b_ada = (a.reshape(1, D_ADA) for a in b_ada_outs)
    d_ada_local = lax.dynamic_slice_in_dim(d_ada_all.reshape(N_DEV, D_ADA), me * (D_ADA // N_DEV), D_ADA // N_DEV, axis=1)
    g_w_ada, d_w_ada, nm_w_ada, nv_w_ada = _ada_adamw(sc_all.T, d_ada_local, w_ada[0], m_w_ada[0], v_w_ada[0])

    def ordered(w_ada_, b_ada_, w_in_, w_out_, s):
        return (w_ada_[None], b_ada_, w_in_[None], s["b_in"], s["w_pool"], s["b_pool"], s["pool_scale"],
                w_out_[None], s["b_out"], s["ln_g"], s["ln_b"])

    return (loss, dx[None],
            *ordered(g_w_ada, g_b_ada, g_w_in, g_w_out, g_s),
            *ordered(d_w_ada, d_b_ada, d_w_in, d_w_out, d_s),
            *ordered(nm_w_ada, nm_b_ada, nm_w_in, nm_w_out, nm_s),
            *ordered(nv_w_ada, nv_b_ada, nv_w_in, nv_w_out, nv_s))
```

```python
import jax
import jax.numpy as jnp
from jax import lax
from jax.experimental import pallas as pl
from jax.experimental.pallas import tpu as pltpu

F32 = jnp.float32
BF16 = jnp.bfloat16

N_DEV = 8
D = 1024
N_HEADS = 8
HEAD_DIM = 64
D_ATT = 512
D_POOL = 512
POOL_WINDOWS = (2, 4, 8, 16)
GROUP_DIM = 128
HALO = 16
LANE = 128
D_IN = 3080
D_ADA = 3072
N_MAIN = 3072
OFF_P = 1536
COL_CHUNK = 512
Q_SCALE = 0.125
LN_EPS = 1e-5
ALPHA = 2.0 ** 0.25
L_CQ, L_CK, L_LSE = 64, 67, 70

ADAM_LR, ADAM_B1, ADAM_B2, ADAM_EPS, ADAM_WD, ADAM_STEP = 0.001, 0.9, 0.999, 1e-08, 0.01, 10
VMEM_LIMIT = 56 * 1024 * 1024

MESH = pl.DeviceIdType.MESH
ANY = pl.BlockSpec(memory_space=pl.ANY)


def _params(sem=None, vmem=VMEM_LIMIT):
    return pltpu.CompilerParams(dimension_semantics=sem, vmem_limit_bytes=vmem)


def _split3(a):
    hi = a.astype(BF16)
    r = a - hi.astype(F32)
    mid = r.astype(BF16)
    lo = (r - mid.astype(F32)).astype(BF16)
    return hi, mid, lo


def _dot(a, b):
    return jnp.dot(a, b, preferred_element_type=F32)


def _dot_nt(a, b):
    return lax.dot_general(a, b, (((1,), (1,)), ((), ())), preferred_element_type=F32)


def _dot_tn(a, b):
    return lax.dot_general(a, b, (((0,), (0,)), ((), ())), preferred_element_type=F32)


def _dot3(m01, a):
    hi, mid, lo = _split3(a)
    return _dot(m01, hi) + _dot(m01, mid) + _dot(m01, lo)


def _sigmoid(z):
    return 1.0 / (1.0 + jnp.exp(-z))


def _lanes(shape):
    return lax.broadcasted_iota(jnp.int32, shape, len(shape) - 1)


def _place3(lane, base, parts, other):
    out = other
    for j in range(3):
        out = jnp.where(lane == base + j, parts[j], out)
    return out


def _mesh_pos():
    return lax.axis_index("x"), lax.axis_index("y"), lax.axis_index("c")


def _dev_index(px, py, pc):
    return 4 * px + 2 * py + pc


N_GATHER_SEMS = 9


def _gather_stages(src_ref, out_ref, send_sems, recv_sems, local_sem):
    x, y, c = _mesh_pos()
    me, sibling = (x, y, c), (x, y, 1 - c)
    nbr_x, nbr_y, diag = (1 - x, y), (x, 1 - y), (1 - x, 1 - y)
    half = out_ref.shape[-1] // 2
    left, right = pl.ds(0, half), pl.ds(half, half)

    def copy(k, block, to, cols=None, src=None):
        slot = out_ref.at[_dev_index(*block)]
        if cols is not None:
            slot = slot.at[:, cols]
        return pltpu.make_async_remote_copy(
            src_ref=slot if src is None else src, dst_ref=slot, send_sem=send_sems.at[k], recv_sem=recv_sems.at[k],
            device_id=to, device_id_type=MESH)

    mine = pltpu.make_async_copy(src_ref, out_ref.at[_dev_index(*me)], local_sem)
    first = [copy(0, me, sibling, src=src_ref), copy(1, me, (*nbr_x, c), src=src_ref), copy(2, me, (*nbr_y, c), src=src_ref)]
    relay = [(1, nbr_x, None, nbr_x), (2, nbr_y, None, nbr_y), (3, diag, left, nbr_y), (4, diag, right, nbr_x)]
    onward = [copy(3, (*nbr_x, c), (*nbr_y, c), cols=left), copy(4, (*nbr_y, c), (*nbr_x, c), cols=right)]
    passed = [copy(4 + k, (*block, c), sibling, cols=cols) for k, block, cols, _ in relay]

    def start():
        mine.start()
        for cp in first:
            cp.start()

    def relay_stage(first_item):
        def run():
            for j in (first_item, first_item + 1):
                k, block, cols, frm = relay[j]
                copy(k, (*block, c), (*frm, c), cols=cols).wait_recv()
                if j < 2:
                    onward[j].start()
                passed[j].start()
        return run

    def finish():
        copy(0, sibling, me).wait_recv()
        for k, block, cols, _ in relay:
            copy(4 + k, (*block, 1 - c), me, cols=cols).wait_recv()
        for cp in first + onward + passed:
            cp.wait_send()
        mine.wait()

    return start, relay_stage(0), relay_stage(2), finish


N_REDUCE_SEMS = 10
N_SMALL_SEMS = 4
N_ROWS_SEMS = 7


def _reduce_stages(ins, gs, r1, s2, r2, smalls, send_sems, recv_sems, rows=None):
    n = len(ins)
    x, y, c = _mesh_pos()
    me = _dev_index(x, y, c)
    sibling = (x, y, 1 - c)
    chips = [(x, y), (1 - x, y), (x, 1 - y), (1 - x, 1 - y)]
    peers = []
    for p in range(1, N_DEV):
        px, py, pc = (p >> 2) & 1, (p >> 1) & 1, p & 1
        peers.append((1 - x if px else x, 1 - y if py else y, 1 - c if pc else c))
    base_small = N_REDUCE_SEMS * n

    def remote(src, dst, k, to):
        return pltpu.make_async_remote_copy(src_ref=src, dst_ref=dst, send_sem=send_sems.at[k],
                                            recv_sem=recv_sems.at[k], device_id=to, device_id_type=MESH)

    def level1(a, q):
        return remote(ins[a].at[_dev_index(*chips[q], 1 - c)], r1[a].at[q], N_REDUCE_SEMS * a + q, sibling)

    def level2(a, k):
        half = ins[a].shape[-1] // 2
        left, right = pl.ds(0, half), pl.ds(half, half)
        nbr_x, nbr_y = (*chips[1], c), (*chips[2], c)
        src_slot, dst_slot, cols, to = [(0, 0, left, nbr_x), (1, 1, right, nbr_y), (2, 2, left, nbr_x),
                                        (2, 2, right, nbr_y), (0, 0, right, nbr_x), (1, 1, left, nbr_y)][k]
        return remote(s2[a].at[src_slot, :, cols], r2[a].at[dst_slot, :, cols], N_REDUCE_SEMS * a + 4 + k, to)

    to_sibling = [remote(sm[0], sm[2], base_small + 4 * i, sibling) for i, sm in enumerate(smalls)]
    to_chips = [[remote(sm[3], sm[4].at[j], base_small + 4 * i + 1 + j, (*chips[j + 1], c)) for j in range(3)]
                for i, sm in enumerate(smalls)]
    if rows is not None:
        rows_ref, land_ref, all_ref = rows
        base_rows = base_small + 4 * len(smalls)
        row_sends = [remote(rows_ref, land_ref.at[me], base_rows + k, to) for k, to in enumerate(peers)]

    def start():
        for a in range(n):
            for q in range(4):
                level1(a, q).start()
        for cp in to_sibling:
            cp.start()
        if rows is not None:
            for cp in row_sends:
                cp.start()
            land_ref[me] = rows_ref[...]

    def middle():
        for a in range(n):
            for q in (1, 2, 3, 0):
                level1(a, q).wait_recv()
                pair = ins[a][_dev_index(*chips[q], c)].astype(F32) + r1[a][q].astype(F32)
                if q == 0:
                    gs[a][...] = pair
                else:
                    s2[a][q - 1] = pair.astype(BF16)
                    for k in ((0,), (1,), (2, 3))[q - 1]:
                        level2(a, k).start()
        for i, (small_ref, _, sm_sib, sm_chip, _) in enumerate(smalls):
            to_sibling[i].wait_recv()
            sm_chip[...] = small_ref[...] + sm_sib[...]
            for cp in to_chips[i]:
                cp.start()

    def fold():
        for a in range(n):
            half = ins[a].shape[-1] // 2
            level2(a, 3).wait_recv()
            s2[a][0, :, half:] = (s2[a][0, :, half:].astype(F32) + r2[a][2, :, half:].astype(F32)).astype(BF16)
            level2(a, 4).start()
            level2(a, 2).wait_recv()
            s2[a][1, :, :half] = (s2[a][1, :, :half].astype(F32) + r2[a][2, :, :half].astype(F32)).astype(BF16)
            level2(a, 5).start()

    def finish():
        for a in range(n):
            for k in (0, 1, 4, 5):
                level2(a, k).wait_recv()
            gs[a][...] = gs[a][...] + r2[a][0].astype(F32) + r2[a][1].astype(F32)
            for q in range(4):
                level1(a, q).wait_send()
            for k in range(6):
                level2(a, k).wait_send()
        for i, (_, total_ref, _, sm_chip, sm_recv) in enumerate(smalls):
            for cp in to_chips[i]:
                cp.wait_recv()
            total = None
            for ax in range(2):
                for ay in range(2):
                    dx, dy = x != ax, y != ay
                    term = jnp.where(dx, jnp.where(dy, sm_recv[2], sm_recv[0]), jnp.where(dy, sm_recv[1], sm_chip[...]))
                    total = term if total is None else total + term
            total_ref[...] = total
            for cp in [to_sibling[i]] + to_chips[i]:
                cp.wait_send()
        if rows is not None:
            for k, frm in enumerate(peers):
                remote(rows_ref, land_ref.at[_dev_index(*frm)], base_rows + k, frm).wait_recv()
            all_ref[...] = land_ref[...]
            for cp in row_sends:
                cp.wait_send()

    return start, middle, fold, finish


def _reduce_scratch(shard, smalls, rows=None):
    out = [pltpu.VMEM((lead,) + shard.shape[1:], BF16) for lead in (4, 3, 3)]
    for small in smalls:
        out += [pltpu.VMEM(small.shape, F32), pltpu.VMEM(small.shape, F32), pltpu.VMEM((3,) + small.shape, F32)]
    n_sems = N_REDUCE_SEMS + N_SMALL_SEMS * len(smalls)
    if rows is not None:
        out.append(pltpu.VMEM((N_DEV,) + rows.shape, F32))
        n_sems += N_ROWS_SEMS
    return out + [pltpu.SemaphoreType.DMA((n_sems,))] * 2


def _reduce_grads(gw_in, small, rows):
    def body(in_ref, small_ref, rows_ref, g_ref, total_ref, rows_all_ref,
             r1, s2, r2, sm_sib, sm_chip, sm_recv, rows_land, send_sems, recv_sems):
        stages = _reduce_stages(
            [in_ref], [g_ref], [r1], [s2], [r2], [(small_ref, total_ref, sm_sib, sm_chip, sm_recv)],
            send_sems, recv_sems, rows=(rows_ref, rows_land, rows_all_ref))
        for stage in stages:
            stage()

    return pl.pallas_call(
        body, name="reduce_grads",
        out_shape=[jax.ShapeDtypeStruct(gw_in.shape[1:], F32), jax.ShapeDtypeStruct(small.shape, F32),
                   jax.ShapeDtypeStruct((N_DEV,) + rows.shape, F32)],
        scratch_shapes=_reduce_scratch(gw_in, [small], rows),
        compiler_params=_params(),
    )(gw_in, small, rows)


def _dot3_rhs(a, b):
    a0, a1, a2 = _split3(a)
    b0, b1, b2 = _split3(b)
    return (_dot(a0, b0) + (_dot(a0, b1) + _dot(a1, b0))
            + (_dot(a0, b2) + _dot(a1, b1) + _dot(a2, b0)))


def _gather_and_ada(c, w_in_rows, w_ada):
    cols = w_ada.shape[1]

    def body(c_ref, w_ref, wa_ref, w_all_ref, sc_ref, ada_ref,
             c_land, part, ada_land, send_sems, recv_sems, local_sem, x_send, x_recv):
        x, y, cc = _mesh_pos()
        me = _dev_index(x, y, cc)
        peers = []
        for p in range(1, N_DEV):
            px, py, pc = (p >> 2) & 1, (p >> 1) & 1, p & 1
            peers.append((1 - x if px else x, 1 - y if py else y, 1 - cc if pc else cc))

        def remote(src, dst, k, to):
            return pltpu.make_async_remote_copy(src_ref=src, dst_ref=dst, send_sem=x_send.at[k], recv_sem=x_recv.at[k],
                                                device_id=to, device_id_type=MESH)

        c_sends = [remote(c_ref, c_land.at[me], k, to) for k, to in enumerate(peers)]
        for cp in c_sends:
            cp.start()
        start, relay_near, relay_far, finish = _gather_stages(w_ref, w_all_ref, send_sems, recv_sems, local_sem.at[0])
        start()
        c_land[me] = c_ref[...]
        for k, frm in enumerate(peers):
            remote(c_ref, c_land.at[_dev_index(*frm)], k, frm).wait_recv()
        c_all = jnp.concatenate([c_land[b] for b in range(N_DEV)], axis=0)
        sc = c_all * _sigmoid(c_all)
        sc_ref[...] = sc
        rows = _dot3_rhs(sc, wa_ref[...])
        for b in range(N_DEV):
            part[b] = rows[b:b + 1, :]
        a_sends = [remote(part.at[_dev_index(*to)], ada_land.at[me], 7 + k, to) for k, to in enumerate(peers)]
        for cp in a_sends:
            cp.start()
        ada_land[me] = part[me]
        for k, frm in enumerate(peers):
            remote(part.at[0], ada_land.at[_dev_index(*frm)], 7 + k, frm).wait_recv()
        ada_ref[...] = ada_land[...]

        relay_near()
        relay_far()
        finish()
        for cp in c_sends + a_sends:
            cp.wait_send()

    vmem = pl.BlockSpec(memory_space=pltpu.VMEM)
    return pl.pallas_call(
        body, name="gather_weights",
        in_specs=[vmem, ANY, vmem], out_specs=[ANY, vmem, vmem],
        out_shape=[jax.ShapeDtypeStruct((N_DEV,) + w_in_rows.shape, w_in_rows.dtype),
                   jax.ShapeDtypeStruct((N_DEV, D), F32), jax.ShapeDtypeStruct((N_DEV, 1, cols), F32)],
        scratch_shapes=[pltpu.VMEM((N_DEV, 1, D), F32), pltpu.VMEM((N_DEV, 1, cols), F32), pltpu.VMEM((N_DEV, 1, cols), F32),
                        pltpu.SemaphoreType.DMA((N_GATHER_SEMS,)), pltpu.SemaphoreType.DMA((N_GATHER_SEMS,)),
                        pltpu.SemaphoreType.DMA((1,)),
                        pltpu.SemaphoreType.DMA((14,)), pltpu.SemaphoreType.DMA((14,))],
        compiler_params=_params(),
    )(c, w_in_rows, w_ada)


def _inproj_forward(x, mod, w_main, w_f, b_main, b_f, tile):
    seq = x.shape[0]
    nt = seq // tile

    def body(x_ref, mod_ref, w_ref, wf_ref, b_ref, bf_ref,
             qp_ref, kp_ref, vp_ref, f_ref, p_ref, ga_ref, gp_ref, u_ref, carry_ref):
        i = pl.program_id(0)

        @pl.when(i == 0)
        def _():
            carry_ref[...] = jnp.zeros_like(carry_ref)

        u = x_ref[...] * mod_ref[0:1, :] + mod_ref[1:2, :]
        ub = u.astype(BF16)
        u_ref[...] = ub

        f = _dot_nt(ub, wf_ref[...]) + bf_ref[...]
        f_ref[...] = f
        lane = _lanes((tile, LANE))
        log_f = jnp.where(lane < N_HEADS, jnp.minimum(f, 0.0) - jnp.log(1.0 + jnp.exp(-jnp.abs(f))), 0.0)
        row = lax.broadcasted_iota(jnp.int32, (tile, tile), 0)
        col = lax.broadcasted_iota(jnp.int32, (tile, tile), 1)
        tri = (row >= col).astype(BF16)
        cum = _dot3(tri, log_f) + carry_ref[0:1, :]
        carry_ref[0:1, :] = cum[tile - 1:tile, :]
        cq = [part.astype(F32) for part in _split3(cum)]
        ck = [part.astype(F32) for part in _split3(-cum)]

        def proj(chunk):
            cols = pl.ds(chunk * COL_CHUNK, COL_CHUNK)
            return _dot_nt(ub, w_ref[cols, :]) + b_ref[:, cols]

        def head_tiles(r):
            for pair in range(N_HEADS // 2):
                both = r[:, pair * LANE:(pair + 1) * LANE]
                yield 2 * pair, both
                yield 2 * pair + 1, pltpu.roll(both, HEAD_DIM, 1)

        for h, val in head_tiles(proj(0)):
            extra = jnp.where((lane >= L_CK) & (lane < L_CK + 3), 1.0, 0.0)
            extra = _place3(lane, L_CQ, [part[:, h:h + 1] for part in cq], extra)
            qp_ref[h] = jnp.where(lane < HEAD_DIM, val * Q_SCALE, extra).astype(BF16)
        for h, val in head_tiles(proj(1)):
            ones = ((lane >= L_CQ) & (lane < L_CQ + 3)) | ((lane >= L_LSE) & (lane < L_LSE + 3))
            extra = _place3(lane, L_CK, [part[:, h:h + 1] for part in ck], jnp.where(ones, 1.0, 0.0))
            kp_ref[h] = jnp.where(lane < HEAD_DIM, val, extra).astype(BF16)
        for h, val in head_tiles(proj(2)):
            extra = jnp.where((lane >= HEAD_DIM) & (lane < HEAD_DIM + 3), -1.0, 0.0)
            vp_ref[h] = jnp.where(lane < HEAD_DIM, val, extra).astype(BF16)
        p_ref[...] = proj(3)
        ga_ref[...] = proj(4)
        gp_ref[...] = proj(5)

    head_block = pl.BlockSpec((N_HEADS, tile, LANE), lambda i: (0, i, 0))
    tok = lambda width: pl.BlockSpec((tile, width), lambda i: (i, 0))
    whole = lambda a: pl.BlockSpec(a.shape, lambda i: (0,) * a.ndim)
    padded = jax.ShapeDtypeStruct((N_HEADS, seq, LANE), BF16)
    half = jax.ShapeDtypeStruct((seq, D_ATT), F32)
    return pl.pallas_call(
        body, name="inproj_forward", grid=(nt,),
        in_specs=[tok(D), whole(mod), whole(w_main), whole(w_f), whole(b_main), whole(b_f)],
        out_specs=[head_block, head_block, head_block, tok(LANE), tok(D_POOL), tok(D_ATT), tok(D_POOL),
                   tok(D)],
        out_shape=[padded, padded, padded, jax.ShapeDtypeStruct((seq, LANE), F32), half, half, half,
                   jax.ShapeDtypeStruct((seq, D), BF16)],
        scratch_shapes=[pltpu.VMEM((8, LANE), F32)],
        compiler_params=_params(("arbitrary",)),
    )(x, mod, w_main, w_f, b_main, b_f)


def _attention_forward(qp, kp, vp, w_out, tile):
    seq = qp.shape[1]
    nb = seq // tile
    steps = (N_HEADS // 2) * nb

    def body(q_ref, k_ref, v_ref, wo_ref, att_ref, q2t_ref, wo_all_ref, s_a, s_b, m_ref, acc_ref,
             send_sems, recv_sems, local_sem):
        step = pl.program_id(0) * nb + pl.program_id(1)
        start, relay_near, relay_far, finish = _gather_stages(wo_ref, wo_all_ref, send_sems, recv_sems, local_sem.at[0])
        pl.when(step == 0)(start)
        pl.when(step == steps // 4)(relay_near)
        pl.when(step == (3 * steps) // 4)(relay_far)

        i = pl.program_id(1)
        sub = lax.broadcasted_iota(jnp.int32, (LANE, tile), 0)
        row = lax.broadcasted_iota(jnp.int32, (tile, tile), 0)
        col = lax.broadcasted_iota(jnp.int32, (tile, tile), 1)
        q = [q_ref[0], q_ref[1]]

        def scores(buf, kb):
            rows = pl.ds(pl.multiple_of(kb * tile, tile), tile)
            for hh in range(2):
                buf[hh] = _dot_nt(k_ref[hh, rows, :], q[hh])

        def absorb(buf, kb, masked):
            rows = pl.ds(pl.multiple_of(kb * tile, tile), tile)
            for hh in range(2):
                m = m_ref[hh, 0:1, :]
                s = buf[hh]
                if masked:
                    s = jnp.where(row <= col, s, -1e30)
                m_new = jnp.maximum(m, jnp.max(s, axis=0, keepdims=True))
                p = jnp.exp(s - m_new).astype(BF16)
                acc_ref[hh] = jnp.exp(m - m_new) * acc_ref[hh] + _dot_tn(v_ref[hh, rows, :], p)
                m_ref[hh, 0:1, :] = m_new

        def two_blocks(j, _):
            scores(s_b, 2 * j + 1)
            absorb(s_a, 2 * j, False)
            scores(s_a, 2 * j + 2)
            absorb(s_b, 2 * j + 1, False)
            return 0

        def last_block():
            absorb(s_a, i, True)

        def last_two_blocks():
            scores(s_b, i)
            absorb(s_a, i - 1, False)
            absorb(s_b, i, True)

        scores(s_a, 0)
        m_ref[...] = jnp.full(m_ref.shape, -1e30, F32)
        acc_ref[...] = jnp.zeros_like(acc_ref)
        lax.fori_loop(0, i // 2, two_blocks, 0)
        lax.cond(i % 2 == 0, last_block, last_two_blocks)
        outs = []
        for hh in range(2):
            m, acc = m_ref[hh, 0:1, :], acc_ref[hh]
            l = -acc[HEAD_DIM:HEAD_DIM + 1, :]
            outs.append((acc / l)[:HEAD_DIM, :])
            neg_lse = [part.astype(F32) for part in _split3(-(m + jnp.log(l)))]
            q2t_ref[hh] = _place3(sub, L_LSE, neg_lse, q[hh].astype(F32).T).astype(BF16)
        att_ref[...] = jnp.concatenate(outs, axis=0).T
        pl.when(step == steps - 1)(finish)

    pair = pl.BlockSpec((2, tile, LANE), lambda hp, i: (hp, i, 0))
    full = pl.BlockSpec((2, seq, LANE), lambda hp, i: (hp, 0, 0))
    return pl.pallas_call(
        body, name="attention_forward", grid=(N_HEADS // 2, nb),
        in_specs=[pair, full, full, ANY],
        out_specs=[pl.BlockSpec((tile, LANE), lambda hp, i: (i, hp)),
                   pl.BlockSpec((2, LANE, tile), lambda hp, i: (hp, 0, i)), ANY],
        out_shape=[jax.ShapeDtypeStruct((seq, D_ATT), F32),
                   jax.ShapeDtypeStruct((N_HEADS, LANE, seq), BF16),
                   jax.ShapeDtypeStruct((N_DEV,) + w_out.shape, w_out.dtype)],
        scratch_shapes=[pltpu.VMEM((2, tile, tile), F32), pltpu.VMEM((2, tile, tile), F32),
                        pltpu.VMEM((2, 8, tile), F32), pltpu.VMEM((2, LANE, tile), F32),
                        pltpu.SemaphoreType.DMA((N_GATHER_SEMS,)), pltpu.SemaphoreType.DMA((N_GATHER_SEMS,)),
                        pltpu.SemaphoreType.DMA((1,))],
        compiler_params=_params(("arbitrary", "arbitrary")),
    )(qp, kp, vp, w_out)


def _window_sum(x, halo, window, transposed):
    tile = x.shape[0]

    def split_cat(a):
        hi = a.astype(BF16)
        return jnp.concatenate([hi, (a - hi.astype(F32)).astype(BF16)], axis=1)

    def fold(r):
        return r[:, :LANE] + r[:, LANE:]

    r = lax.broadcasted_iota(jnp.int32, (tile, tile), 0)
    c = lax.broadcasted_iota(jnp.int32, (tile, tile), 1)
    rh = lax.broadcasted_iota(jnp.int32, (HALO, HALO), 0)
    ch = lax.broadcasted_iota(jnp.int32, (HALO, HALO), 1)
    if not transposed:
        band = (c <= r) & (r - c < window)
        edge = (rh + HALO - ch) < window
    else:
        band = (r <= c) & (c - r < window)
        edge = (HALO + ch - rh) < window
    out = fold(_dot(band.astype(BF16), split_cat(x)))
    reach = fold(_dot(edge.astype(BF16), split_cat(halo)))
    if not transposed:
        return jnp.concatenate([out[:HALO] + reach, out[HALO:]], axis=0)
    return jnp.concatenate([out[:tile - HALO], out[tile - HALO:] + reach], axis=0)


def _silu_parts(g):
    sig = _sigmoid(g)
    return g * sig, sig * (1.0 + g * (1.0 - sig))


def _middle(x, tgt, att, g_att, g_pool, p, vecs, pool_vecs, w_out, w_pool, tile):
    seq = x.shape[0]
    nt = seq // tile
    halo_blocks = tile // HALO

    def body(x_ref, tgt_ref, att_ref, ga_ref, gp_ref, p_ref, ph_ref, vec_ref, pvec_ref, wo_ref, wp_ref,
             dxa_ref, do2_ref, dga_ref, dgp_ref, dpooled_ref, gwo_ref, dwp_ref, dvec_ref, dwo_ref, dpvec_ref):
        i = pl.program_id(0)

        @pl.when(i == 0)
        def _():
            dwo_ref[...] = jnp.zeros_like(dwo_ref)
            dwp_ref[...] = jnp.zeros_like(dwp_ref)
            dvec_ref[...] = jnp.zeros_like(dvec_ref)
            dpvec_ref[...] = jnp.zeros_like(dpvec_ref)

        gate, b_out, ln_g, ln_b = (vec_ref[k:k + 1, :] for k in range(4))
        b_pool, pool_scale = pvec_ref[0:1, :], pvec_ref[1:2, :]
        x = x_ref[...]
        p = p_ref[...]
        p_halo = ph_ref[...] * jnp.where(i > 0, 1.0, 0.0)
        pos = i * tile + lax.broadcasted_iota(jnp.int32, (tile, 1), 0) + 1

        pooled, mixed = [], []
        for g, window in enumerate(POOL_WINDOWS):
            cols = slice(g * GROUP_DIM, (g + 1) * GROUP_DIM)
            wsum = _window_sum(p[:, cols], p_halo[:, cols], window, False)
            count = jnp.minimum(pos, window).astype(F32)
            pooled.append(wsum / count - p[:, cols])
            mixed.append(_dot(pooled[g].astype(BF16), wp_ref[g]) + b_pool[:, cols])
        mixed = jnp.concatenate(mixed, axis=1)
        pool = mixed * pool_scale

        att = att_ref[...]
        g_att, g_pool = ga_ref[...], gp_ref[...]
        silu_a, dsilu_a = _silu_parts(g_att)
        silu_p, dsilu_p = _silu_parts(g_pool)
        y_in = jnp.concatenate([att * silu_a, pool * silu_p], axis=1)
        y = _dot(y_in.astype(BF16), wo_ref[...]) + b_out
        h = ALPHA * x + gate * y
        mu = jnp.mean(h, axis=1, keepdims=True)
        hc = h - mu
        var = jnp.mean(hc * hc, axis=1, keepdims=True)
        rstd = lax.rsqrt(var + LN_EPS)
        yhat = hc * rstd
        diff = yhat * ln_g + ln_b - tgt_ref[...]
        loss_rows = jnp.sum(diff * diff, axis=1, keepdims=True)
        d_out = diff * (1.0 / D)

        d_yhat = d_out * ln_g
        dh = rstd * (d_yhat - jnp.mean(d_yhat, axis=1, keepdims=True)
                     - yhat * jnp.mean(d_yhat * yhat, axis=1, keepdims=True))
        dxa_ref[...] = ALPHA * dh
        dy = dh * gate
        dyb = dy.astype(BF16)
        lane = _lanes((1, D))
        loss_row = jnp.where(lane == 0, (0.5 / D) * jnp.sum(loss_rows, axis=0, keepdims=True), 0.0)
        dvec_ref[5:6, :] += jnp.sum(dh * y, axis=0, keepdims=True)
        dvec_ref[0:1, :] += jnp.sum(dy, axis=0, keepdims=True)
        dvec_ref[1:2, :] += jnp.sum(d_out * yhat, axis=0, keepdims=True)
        dvec_ref[2:3, :] += jnp.sum(d_out, axis=0, keepdims=True)
        dvec_ref[4:5, :] += loss_row

        dwo_ref[...] += _dot(y_in.T.astype(BF16), dyb)
        d_yin = _dot_nt(dyb, wo_ref[...])
        d_a, d_pl = d_yin[:, :D_ATT], d_yin[:, D_ATT:]
        d_att = d_a * silu_a
        d_att_t = d_att.T
        prod_t = (d_att * att).T
        sub = lax.broadcasted_iota(jnp.int32, (HEAD_DIM, tile), 0)
        for h in range(N_HEADS):
            rows = slice(h * HEAD_DIM, (h + 1) * HEAD_DIM)
            delta = jnp.sum(prod_t[rows], axis=0, keepdims=True)
            extra = _place3(sub, 0, [part.astype(F32) for part in _split3(delta)], 0.0)
            do2_ref[h] = jnp.concatenate([d_att_t[rows], extra], axis=0).astype(BF16)
        dga_ref[...] = d_a * att * dsilu_a
        dgp_ref[...] = d_pl * pool * dsilu_p
        d_pool = d_pl * silu_p
        d_mixed = d_pool * pool_scale
        dpvec_ref[0:1, :] += jnp.sum(d_mixed, axis=0, keepdims=True)
        dpvec_ref[1:2, :] += jnp.sum(d_pool * mixed, axis=0, keepdims=True)
        d_pooled = []
        for g in range(len(POOL_WINDOWS)):
            cols = slice(g * GROUP_DIM, (g + 1) * GROUP_DIM)
            dmb = d_mixed[:, cols].astype(BF16)
            dwp_ref[g] += _dot(pooled[g].T.astype(BF16), dmb)
            d_pooled.append(_dot_nt(dmb, wp_ref[g]))
        dpooled_ref[...] = jnp.concatenate(d_pooled, axis=1)

        @pl.when(i == nt - 1)
        def _():
            gwo_ref[...] = dwo_ref[...].astype(BF16)
            dvec_ref[3:4, :] = jnp.concatenate([dpvec_ref[0:1, :], dpvec_ref[1:2, :]], axis=1)

    tok = lambda width: pl.BlockSpec((tile, width), lambda i: (i, 0))
    whole = lambda a: pl.BlockSpec(a.shape, lambda i: (0,) * a.ndim)
    halo = pl.BlockSpec((HALO, D_POOL), lambda i: (jnp.maximum(i * halo_blocks - 1, 0), 0))
    half = jax.ShapeDtypeStruct((seq, D_ATT), F32)
    outs = [jax.ShapeDtypeStruct((seq, D), F32), jax.ShapeDtypeStruct((N_HEADS, LANE, seq), BF16), half, half, half,
            jax.ShapeDtypeStruct(w_out.shape, BF16), jax.ShapeDtypeStruct(w_pool.shape, F32),
            jax.ShapeDtypeStruct(vecs.shape, F32)]
    return pl.pallas_call(
        body, name="middle", grid=(nt,),
        in_specs=[tok(D), tok(D), tok(D_ATT), tok(D_ATT), tok(D_POOL), tok(D_POOL), halo,
                  whole(vecs), whole(pool_vecs), whole(w_out), whole(w_pool)],
        out_specs=[tok(D), pl.BlockSpec((N_HEADS, LANE, tile), lambda i: (0, 0, i)),
                   tok(D_ATT), tok(D_POOL), tok(D_POOL),
                   whole(w_out), whole(w_pool), whole(vecs)],
        out_shape=outs,
        scratch_shapes=[pltpu.VMEM(w_out.shape, F32), pltpu.VMEM(pool_vecs.shape, F32)],
        compiler_params=_params(("arbitrary",)),
    )(x, tgt, att, g_att, g_pool, p, p, vecs, pool_vecs, w_out, w_pool)


def _attention_backward(q2t, kp, vp, do2t, gw_out, vecs, pool, tile):
    seq = kp.shape[1]
    nb = seq // tile
    last = N_HEADS // 2 - 1

    def body(qt_ref, k_ref, v_ref, dot_ref, gwo_hbm, vecs_hbm, pool_hbm,
             dq_ref, dk_ref, dv_ref, dcum_ref, g_out_ref, vecs_sum_ref, pool_sum_ref,
             dq_acc, dk_acc, dv_acc, gwo_ref, vecs_ref, pool_ref,
             r1, s2, r2, v_sib, v_chip, v_recv, p_sib, p_chip, p_recv, send_sems, recv_sems):
        hp = pl.program_id(0)
        start, middle, fold, finish = _reduce_stages(
            [gwo_ref], [g_out_ref], [r1], [s2], [r2],
            [(vecs_ref, vecs_sum_ref, v_sib, v_chip, v_recv), (pool_ref, pool_sum_ref, p_sib, p_chip, p_recv)],
            send_sems, recv_sems)

        @pl.when(hp == 0)
        def _():
            pltpu.sync_copy(gwo_hbm, gwo_ref)
            pltpu.sync_copy(vecs_hbm, vecs_ref)
            pltpu.sync_copy(pool_hbm, pool_ref)
            start()

        pl.when(hp == 1)(middle)
        pl.when(hp == 2)(fold)

        row = lax.broadcasted_iota(jnp.int32, (tile, tile), 0)
        col = lax.broadcasted_iota(jnp.int32, (tile, tile), 1)
        dq_acc[...] = jnp.zeros_like(dq_acc)

        def kv_block(kb, _):
            krows = pl.ds(pl.multiple_of(kb * tile, tile), tile)
            k = [k_ref[hh, krows, :] for hh in range(2)]
            v = [v_ref[hh, krows, :] for hh in range(2)]
            k_t = [k[hh].T for hh in range(2)]

            def q_block(qb, masked):
                qcols = pl.ds(pl.multiple_of(qb * tile, tile), tile)
                for hh in range(2):
                    q_t = qt_ref[hh, :, qcols]
                    do_t = dot_ref[hh, :, qcols]
                    s_t = _dot(k[hh], q_t)
                    if masked:
                        s_t = jnp.where(row <= col, s_t, -1e30)
                    p_t = jnp.exp(s_t)
                    ds_t = (p_t * _dot(v[hh], do_t)).astype(BF16)
                    dv_new = _dot_nt(do_t, p_t.astype(BF16))
                    dk_new = _dot_nt(q_t, ds_t)
                    if masked:
                        dv_acc[hh], dk_acc[hh] = dv_new, dk_new
                    else:
                        dv_acc[hh] += dv_new
                        dk_acc[hh] += dk_new
                    dq_acc[hh, :, qcols] += _dot(k_t[hh], ds_t)

            q_block(kb, True)

            def two_later_blocks(j, _):
                q_block(kb + 1 + 2 * j, False)
                q_block(kb + 2 + 2 * j, False)
                return 0

            later = nb - 1 - kb
            lax.fori_loop(0, later // 2, two_later_blocks, 0)
            pl.when(later % 2 == 1)(lambda: q_block(nb - 1, False))
            for hh in range(2):
                dk = dk_acc[hh]
                dk_ref[hh, :, krows] = dk.astype(BF16)
                dv_ref[hh, :, krows] = dv_acc[hh].astype(BF16)
                dcum_ref[hh, :, krows] = -dk[L_CK:L_CK + 1, :]
            return 0

        lax.fori_loop(0, nb, kv_block, 0)
        for hh in range(2):
            dq = dq_acc[hh]
            dcum_ref[hh] += dq[L_CQ:L_CQ + 1, :]
            dq_ref[hh] = (dq * Q_SCALE).astype(BF16)
        pl.when(hp == last)(finish)

    pair = pl.BlockSpec((2, seq, LANE), lambda hp: (hp, 0, 0))
    pair_t = pl.BlockSpec((2, LANE, seq), lambda hp: (hp, 0, 0))
    whole = lambda shape: pl.BlockSpec(shape, lambda hp: (0,) * len(shape))
    grad = jax.ShapeDtypeStruct((N_HEADS, LANE, seq), BF16)
    return pl.pallas_call(
        body, name="attention_backward", grid=(N_HEADS // 2,),
        in_specs=[pair_t, pair, pair, pair_t, ANY, ANY, ANY],
        out_specs=[pair_t, pair_t, pair_t, pl.BlockSpec((2, 1, seq), lambda hp: (hp, 0, 0)),
                   whole(gw_out.shape[1:]), whole(vecs.shape), whole(pool.shape)],
        out_shape=[grad, grad, grad, jax.ShapeDtypeStruct((N_HEADS, 1, seq), F32),
                   jax.ShapeDtypeStruct(gw_out.shape[1:], F32), jax.ShapeDtypeStruct(vecs.shape, F32),
                   jax.ShapeDtypeStruct(pool.shape, F32)],
        scratch_shapes=[pltpu.VMEM((2, LANE, seq), F32), pltpu.VMEM((2, LANE, tile), F32),
                        pltpu.VMEM((2, LANE, tile), F32), pltpu.VMEM(gw_out.shape, BF16),
                        pltpu.VMEM(vecs.shape, F32), pltpu.VMEM(pool.shape, F32)]
        + _reduce_scratch(gw_out, [vecs, pool]),
        compiler_params=_params(("arbitrary",)),
    )(q2t, kp, vp, do2t, gw_out, vecs, pool)


def _inproj_backward(dqp, dkp, dvp, d_cum, f, d_pooled, d_ga, d_gp, x, dxa, u, mod, w_main, w_f, tile):
    seq = x.shape[0]
    nt = seq // tile
    halo_blocks = tile // HALO

    def body(dq_ref, dk_ref, dv_ref, dcum_ref, f_ref, dpo_ref, dph_ref, dga_ref, dgp_ref, x_ref, dxa_ref, u_ref,
             mod_ref, w_ref, wf_ref,
             dx_ref, dproj_ref, dwf_ref, db_ref, dbf_ref, dmod_ref, carry_ref):
        step = pl.program_id(0)
        i = nt - 1 - step

        @pl.when(step == 0)
        def _():
            carry_ref[...] = jnp.zeros_like(carry_ref)
            dwf_ref[...] = jnp.zeros_like(dwf_ref)
            db_ref[...] = jnp.zeros_like(db_ref)
            dbf_ref[...] = jnp.zeros_like(dbf_ref)
            dmod_ref[...] = jnp.zeros_like(dmod_ref)

        ones = jnp.ones((8, tile), BF16)

        def emit(chunk, val):
            cols = pl.ds(chunk * COL_CHUNK, COL_CHUNK)
            db_ref[0:1, cols] += jnp.sum(val, axis=0, keepdims=True)
            vb = val.astype(BF16)
            dproj_ref[:, pl.ds((chunk - 3) * COL_CHUNK, COL_CHUNK)] = vb
            return _dot(vb, w_ref[cols, :])

        d_u = jnp.zeros((tile, D), F32)
        for chunk, ref in enumerate((dq_ref, dk_ref, dv_ref)):
            cols = pl.ds(chunk * COL_CHUNK, COL_CHUNK)
            val_t = ref[:, 0:HEAD_DIM, :].reshape(COL_CHUNK, tile)
            db_ref[:, cols] += _dot_nt(ones, val_t)
            d_u += _dot_tn(val_t, w_ref[cols, :])

        d_pooled = dpo_ref[...]
        d_halo = dph_ref[...] * jnp.where(i < nt - 1, 1.0, 0.0)
        pos = i * tile + lax.broadcasted_iota(jnp.int32, (tile, 1), 0) + 1
        d_p = []
        for g, window in enumerate(POOL_WINDOWS):
            cols = slice(g * GROUP_DIM, (g + 1) * GROUP_DIM)
            scaled = d_pooled[:, cols] / jnp.minimum(pos, window).astype(F32)
            d_p.append(_window_sum(scaled, d_halo[:, cols] * (1.0 / window), window, True) - d_pooled[:, cols])
        d_u += emit(3, jnp.concatenate(d_p, axis=1))
        d_u += emit(4, dga_ref[...])
        d_u += emit(5, dgp_ref[...])

        row = lax.broadcasted_iota(jnp.int32, (tile, tile), 0)
        col = lax.broadcasted_iota(jnp.int32, (tile, tile), 1)
        later = (row >= col).astype(BF16)
        d_logf = sum(_dot(part, later) for part in _split3(dcum_ref[:, 0, :])) + carry_ref[:, 0:1]
        carry_ref[:, 0:1] = d_logf[:, 0:1]
        d_f = d_logf * _sigmoid(-f_ref[...].T[0:N_HEADS, :])
        d_f = jnp.concatenate([d_f, jnp.zeros((LANE - N_HEADS, tile), F32)], axis=0)
        dbf_ref[...] += sum(_dot_nt(ones, part) for part in _split3(d_f))
        d_fb = d_f.astype(BF16)
        d_u += _dot_tn(d_fb, wf_ref[...])
        dwf_ref[...] += _dot(d_fb, u_ref[...])

        x = x_ref[...]
        dx_ref[...] = dxa_ref[...] + d_u * mod_ref[0:1, :]
        dmod_ref[0:1, :] += jnp.sum(d_u * x, axis=0, keepdims=True)
        dmod_ref[1:2, :] += jnp.sum(d_u, axis=0, keepdims=True)

    rev = lambda step: nt - 1 - step
    tok = lambda width: pl.BlockSpec((tile, width), lambda s: (rev(s), 0))
    head_block = pl.BlockSpec((N_HEADS, LANE, tile), lambda s: (0, 0, rev(s)))
    whole = lambda a: pl.BlockSpec(a.shape, lambda s: (0,) * a.ndim)
    halo = pl.BlockSpec((HALO, D_POOL), lambda s: (jnp.minimum((rev(s) + 1) * halo_blocks, seq // HALO - 1), 0))
    small = lambda width: jax.ShapeDtypeStruct((8, width), F32)
    n_rest = N_MAIN - OFF_P
    return pl.pallas_call(
        body, name="inproj_backward", grid=(nt,),
        in_specs=[head_block, head_block, head_block, pl.BlockSpec((N_HEADS, 1, tile), lambda s: (0, 0, rev(s))),
                  tok(LANE), tok(D_POOL), halo, tok(D_ATT), tok(D_POOL),
                  tok(D), tok(D), tok(D),
                  whole(mod), whole(w_main), whole(w_f)],
        out_specs=[tok(D), tok(n_rest), pl.BlockSpec((LANE, D), lambda s: (0, 0)),
                   pl.BlockSpec((8, N_MAIN), lambda s: (0, 0)), pl.BlockSpec((8, LANE), lambda s: (0, 0)),
                   pl.BlockSpec((8, D), lambda s: (0, 0))],
        out_shape=[jax.ShapeDtypeStruct((seq, D), F32), jax.ShapeDtypeStruct((seq, n_rest), BF16),
                   jax.ShapeDtypeStruct((LANE, D), F32), small(N_MAIN), small(LANE), small(D)],
        scratch_shapes=[pltpu.VMEM((8, LANE), F32)],
        compiler_params=_params(("arbitrary",)),
    )(dqp, dkp, dvp, d_cum, f, d_pooled, d_pooled, d_ga, d_gp, x, dxa, u, mod, w_main, w_f)


def _weight_grad(dproj, u, k_tile):
    seq, n_cols = dproj.shape
    nk = seq // k_tile

    def body(dp_ref, u_ref, out_ref):
        @pl.when(pl.program_id(1) == 0)
        def _():
            out_ref[...] = jnp.zeros_like(out_ref)

        out_ref[...] += _dot_tn(dp_ref[...], u_ref[...])

    return pl.pallas_call(
        body, name="weight_grad", grid=(n_cols // COL_CHUNK, nk),
        in_specs=[pl.BlockSpec((k_tile, COL_CHUNK), lambda n, k: (k, n)),
                  pl.BlockSpec((k_tile, D), lambda n, k: (k, 0))],
        out_specs=pl.BlockSpec((COL_CHUNK, D), lambda n, k: (n, 0)),
        out_shape=jax.ShapeDtypeStruct((n_cols, D), F32),
        compiler_params=_params(("arbitrary", "arbitrary")),
    )(dproj, u)


def _weight_grad_heads(grad_t, u, k_tile, name):
    seq = u.shape[0]
    nk = seq // k_tile

    def body(g_ref, u_ref, out_ref):
        @pl.when(pl.program_id(0) == 0)
        def _():
            out_ref[...] = jnp.zeros_like(out_ref)

        out_ref[...] += _dot(g_ref[...].reshape(N_HEADS * HEAD_DIM, k_tile), u_ref[...])

    return pl.pallas_call(
        body, name=name, grid=(nk,),
        in_specs=[pl.BlockSpec((N_HEADS, HEAD_DIM, k_tile), lambda k: (0, 0, k)),
                  pl.BlockSpec((k_tile, D), lambda k: (k, 0))],
        out_specs=pl.BlockSpec((N_HEADS * HEAD_DIM, D), lambda k: (0, 0)),
        out_shape=jax.ShapeDtypeStruct((N_HEADS * HEAD_DIM, D), F32),
        compiler_params=_params(("arbitrary",)),
    )(grad_t, u)


def _adamw(w, g, m, v):
    m = ADAM_B1 * m + (1.0 - ADAM_B1) * g
    v = ADAM_B2 * v + (1.0 - ADAM_B2) * (g * g)
    m_hat = m / (1.0 - ADAM_B1 ** ADAM_STEP)
    v_hat = v / (1.0 - ADAM_B2 ** ADAM_STEP)
    delta = -ADAM_LR * (m_hat / (jnp.sqrt(v_hat) + ADAM_EPS) + ADAM_WD * w)
    return delta, m, v


def _adamw_call(g, w, m, v, lead_tile, name):
    nr = w.shape[0] // lead_tile

    def body(gi_ref, w_ref, m_ref, v_ref, g_ref, d_ref, nm_ref, nv_ref):
        g = gi_ref[...]
        g_ref[...] = g
        d_ref[...], nm_ref[...], nv_ref[...] = _adamw(w_ref[...], g, m_ref[...], v_ref[...])

    blk = pl.BlockSpec((lead_tile,) + w.shape[1:], lambda r: (r,) + (0,) * (w.ndim - 1))
    shape = jax.ShapeDtypeStruct(w.shape, F32)
    return pl.pallas_call(
        body, name=name, grid=(nr,),
        in_specs=[blk, blk, blk, blk], out_specs=[blk, blk, blk, blk],
        out_shape=[shape, shape, shape, shape],
        compiler_params=_params(("arbitrary",)),
    )(g, w, m, v)


def _sum_adamw(parts, w, m, v, row_tile, name):
    rows, cols = w.shape
    nr = rows // row_tile

    def body(parts_ref, w_ref, m_ref, v_ref, g_ref, d_ref, nm_ref, nv_ref):
        g = parts_ref[0]
        for k in range(1, N_DEV):
            g = g + parts_ref[k]
        g_ref[...] = g
        d_ref[...], nm_ref[...], nv_ref[...] = _adamw(w_ref[...], g, m_ref[...], v_ref[...])

    blk = pl.BlockSpec((row_tile, cols), lambda r: (r, 0))
    shape = jax.ShapeDtypeStruct(w.shape, F32)
    return pl.pallas_call(
        body, name=name, grid=(nr,),
        in_specs=[pl.BlockSpec((N_DEV, row_tile, cols), lambda r: (0, r, 0)), blk, blk, blk],
        out_specs=[blk, blk, blk, blk],
        out_shape=[shape, shape, shape, shape],
        compiler_params=_params(("arbitrary",)),
    )(parts, w, m, v)


def _ada_adamw(sc_t, d_ada, w, m, v):
    def body(sc_ref, d_ref, w_ref, m_ref, v_ref, g_ref, dl_ref, nm_ref, nv_ref):
        g = sc_ref[:, 0:1] * d_ref[0:1, :]
        for b in range(1, N_DEV):
            g = g + sc_ref[:, b:b + 1] * d_ref[b:b + 1, :]
        g_ref[...] = g
        dl_ref[...], nm_ref[...], nv_ref[...] = _adamw(w_ref[...], g, m_ref[...], v_ref[...])

    shape = jax.ShapeDtypeStruct(w.shape, F32)
    return pl.pallas_call(
        body, name="ada_adamw", out_shape=[shape, shape, shape, shape], compiler_params=_params(),
    )(sc_t, d_ada, w, m, v)


F_LO, F_HI = 3 * D_ATT, 3 * D_ATT + N_HEADS


def _split_forget(a, axis):
    idx = lambda lo, hi: tuple(slice(lo, hi) if d == axis else slice(None) for d in range(a.ndim))
    pad = [(0, LANE - N_HEADS) if d == axis else (0, 0) for d in range(a.ndim)]
    return jnp.concatenate([a[idx(0, F_LO)], a[idx(F_HI, D_IN)]], axis=axis), jnp.pad(a[idx(F_LO, F_HI)], pad)


def _join_forget(main, f, axis):
    idx = lambda lo, hi: tuple(slice(lo, hi) if d == axis else slice(None) for d in range(main.ndim))
    return jnp.concatenate([main[idx(0, F_LO)], f[idx(0, N_HEADS)], main[idx(F_LO, N_MAIN)]], axis=axis)


def _adamw_rows(grad_rows, params):
    n = len(params)

    def body(g_ref, *refs):
        ins, outs = refs[:3 * n], refs[3 * n:]
        for i, (row, lo, hi, _, _, _) in enumerate(params):
            g = g_ref[row:row + 1, lo:hi]
            outs[4 * i][...] = g
            outs[4 * i + 1][...], outs[4 * i + 2][...], outs[4 * i + 3][...] = _adamw(
                ins[3 * i][...], g, ins[3 * i + 1][...], ins[3 * i + 2][...])

    flat = pl.pallas_call(
        body, name="adamw_rows",
        out_shape=[jax.ShapeDtypeStruct(w.shape, F32) for _, _, _, w, _, _ in params for _ in range(4)],
        compiler_params=_params(),
    )(grad_rows, *[a for _, _, _, w, m, v in params for a in (w, m, v)])
    return [flat[4 * i:4 * i + 4] for i in range(n)]


def kernel(x, c, w_ada, b_ada, w_in, b_in, w_pool_mix, b_pool_mix, pool_scale, w_out, b_out, ln_g, ln_b, loss_target, m_w_ada, m_b_ada, m_w_in, m_b_in, m_w_pool_mix, m_b_pool_mix, m_pool_scale, m_w_out, m_b_out, m_ln_g, m_ln_b, v_w_ada, v_b_ada, v_w_in, v_b_in, v_w_pool_mix, v_b_pool_mix, v_pool_scale, v_w_out, v_b_out, v_ln_g, v_ln_b):
    seq = x.shape[1]
    tile = min(256, seq)
    attn_tile = min(512, max(128, seq // 4))
    me = _dev_index(*_mesh_pos())
    x2, tgt = x[0], loss_target[0]

    rows_in = D_IN // N_DEV
    w_in_g, sc_all, ada_mine = _gather_and_ada(c, w_in[0].T.astype(BF16), w_ada[0])
    ada = ada_mine.reshape(1, D_ADA) + b_ada
    shift, scale, gate = ada[:, 0:D], ada[:, D:2 * D], ada[:, 2 * D:]
    mod = jnp.concatenate([1.0 + scale, shift, jnp.zeros((6, D), F32)], axis=0)

    w_main, w_f = _split_forget(w_in_g.reshape(D_IN, D), 0)
    b_main, b_f = _split_forget(b_in, 1)

    qp, kp, vp, f, p, g_att, g_pool, u = _inproj_forward(x2, mod, w_main, w_f, b_main, b_f, tile)
    att, q2t, w_out_g = _attention_forward(qp, kp, vp, w_out[0].astype(BF16), attn_tile)

    vecs = jnp.concatenate([gate, b_out, ln_g, ln_b, jnp.zeros((4, D), F32)], axis=0)
    pool_vecs = jnp.concatenate([b_pool_mix.reshape(1, D_POOL), pool_scale, jnp.zeros((6, D_POOL), F32)], axis=0)
    dxa, do2, d_ga, d_gp, d_pooled, gw_out, dw_pool, dvec = _middle(
        x2, tgt, att, g_att, g_pool, p, vecs, pool_vecs, w_out_g.reshape(D, D), w_pool_mix[0].astype(BF16), tile)

    pool_rows = w_pool_mix.shape[1] * GROUP_DIM
    dqp, dkp, dvp, d_cum, g_out, dvec_sum, dw_pool_sum = _attention_backward(
        q2t, kp, vp, do2, gw_out.reshape(N_DEV, D // N_DEV, D), dvec, dw_pool.reshape(pool_rows, GROUP_DIM), attn_tile)
    dx, dproj, dw_f, db_main, db_f, dmod = _inproj_backward(
        dqp, dkp, dvp, d_cum, f, d_pooled, d_ga, d_gp, x2, dxa, u, mod, w_main, w_f, tile)
    k_tile = min(1024, seq)
    dw_q, dw_k, dw_v = (_weight_grad_heads(g, u, k_tile, "weight_grad_" + n)
                        for g, n in ((dqp, "q"), (dkp, "k"), (dvp, "v")))
    dw_rest = _weight_grad(dproj, u, k_tile)

    dw_main = jnp.concatenate([dw_q, dw_k, dw_v, dw_rest], axis=0)
    gw_in = _join_forget(dw_main, dw_f, 0).reshape(N_DEV, rows_in, D).astype(BF16)
    d_ada = jnp.concatenate([dmod[1:2], dmod[0:1], dvec[5:6]], axis=1)
    g_in_rows, g_b_in, d_ada_all = _reduce_grads(gw_in, _join_forget(db_main[0:1], db_f[0:1], 1), d_ada)

    def rows3(a):
        return a[0].T.reshape(rows_in, D // LANE, LANE)

    outs_in = _adamw_call(g_in_rows.reshape(rows_in, D // LANE, LANE), rows3(w_in), rows3(m_w_in), rows3(v_w_in),
                          rows_in // 5, "adamw_w_in")
    g_w_in, d_w_in, nm_w_in, nv_w_in = (a.reshape(rows_in, D).T for a in outs_in)
    g_w_out, d_w_out, nm_w_out, nv_w_out = _adamw_call(g_out, w_out[0], m_w_out[0], v_w_out[0], D // N_DEV, "adamw_w_out")

    flat_pool = lambda a: a.reshape(1, D_POOL)
    rows = _adamw_rows(dvec_sum, [
        (0, 0, D, b_out, m_b_out, v_b_out), (1, 0, D, ln_g, m_ln_g, v_ln_g), (2, 0, D, ln_b, m_ln_b, v_ln_b),
        (3, 0, D_POOL, flat_pool(b_pool_mix), flat_pool(m_b_pool_mix), flat_pool(v_b_pool_mix)),
        (3, D_POOL, 2 * D_POOL, pool_scale, m_pool_scale, v_pool_scale)])
    pools = _adamw_call(dw_pool_sum, *(a.reshape(pool_rows, GROUP_DIM) for a in (w_pool_mix, m_w_pool_mix, v_w_pool_mix)),
                        pool_rows, "adamw_w_pool")
    b_ins = _adamw_call(g_b_in, b_in, m_b_in, v_b_in, 1, "adamw_b_in")
    small = {"b_out": rows[0], "ln_g": rows[1], "ln_b": rows[2],
             "b_pool": [a.reshape(b_pool_mix.shape) for a in rows[3]], "pool_scale": rows[4],
             "w_pool": [a.reshape(w_pool_mix.shape) for a in pools], "b_in": b_ins}
    g_s, d_s, nm_s, nv_s = ({k: r[j] for k, r in small.items()} for j in range(4))
    loss = dvec_sum[4, 0]

    g_b_ada, d_b_ada, nm_b_ada, nv_b_ada = _sum_adamw(d_ada_all, b_ada, m_b_ada, v_b_ada, 1, "adamw_b_ada")
    d_ada_local = lax.dynamic_slice_in_dim(d_ada_all.reshape(N_DEV, D_ADA), me * (D_ADA // N_DEV), D_ADA // N_DEV, axis=1)
    g_w_ada, d_w_ada, nm_w_ada, nv_w_ada = _ada_adamw(sc_all.T, d_ada_local, w_ada[0], m_w_ada[0], v_w_ada[0])

    def ordered(w_ada_, b_ada_, w_in_, w_out_, s):
        return (w_ada_[None], b_ada_, w_in_[None], s["b_in"], s["w_pool"], s["b_pool"], s["pool_scale"],
                w_out_[None], s["b_out"], s["ln_g"], s["ln_b"])

    return (loss, dx[None],
            *ordered(g_w_ada, g_b_ada, g_w_in, g_w_out, g_s),
            *ordered(d_w_ada, d_b_ada, d_w_in, d_w_out, d_s),
            *ordered(nm_w_ada, nm_b_ada, nm_w_in, nm_w_out, nm_s),
            *ordered(nv_w_ada, nv_b_ada, nv_w_in, nv_w_out, nv_s))
```

```python
import jax
import jax.numpy as jnp
from jax import lax
from jax.experimental import pallas as pl
from jax.experimental.pallas import tpu as pltpu

F32 = jnp.float32
BF16 = jnp.bfloat16

N_DEV = 8
D = 1024
N_HEADS = 8
HEAD_DIM = 64
D_ATT = 512
D_POOL = 512
POOL_WINDOWS = (2, 4, 8, 16)
GROUP_DIM = 128
HALO = 16
LANE = 128
D_IN = 3080
D_ADA = 3072
N_MAIN = 3072
OFF_P = 1536
COL_CHUNK = 512
Q_SCALE = 0.125
LN_EPS = 1e-5
ALPHA = 2.0 ** 0.25
L_CQ, L_CK, L_LSE = 64, 67, 70

ADAM_LR, ADAM_B1, ADAM_B2, ADAM_EPS, ADAM_WD, ADAM_STEP = 0.001, 0.9, 0.999, 1e-08, 0.01, 10
VMEM_LIMIT = 56 * 1024 * 1024

MESH = pl.DeviceIdType.MESH
ANY = pl.BlockSpec(memory_space=pl.ANY)


def _params(sem=None, vmem=VMEM_LIMIT):
    return pltpu.CompilerParams(dimension_semantics=sem, vmem_limit_bytes=vmem)


def _split3(a):
    hi = a.astype(BF16)
    r = a - hi.astype(F32)
    mid = r.astype(BF16)
    lo = (r - mid.astype(F32)).astype(BF16)
    return hi, mid, lo


def _dot(a, b):
    return jnp.dot(a, b, preferred_element_type=F32)


def _dot_nt(a, b):
    return lax.dot_general(a, b, (((1,), (1,)), ((), ())), preferred_element_type=F32)


def _dot_tn(a, b):
    return lax.dot_general(a, b, (((0,), (0,)), ((), ())), preferred_element_type=F32)


def _dot3(m01, a):
    hi, mid, lo = _split3(a)
    return _dot(m01, hi) + _dot(m01, mid) + _dot(m01, lo)


def _sigmoid(z):
    return 1.0 / (1.0 + jnp.exp(-z))


def _lanes(shape):
    return lax.broadcasted_iota(jnp.int32, shape, len(shape) - 1)


def _place3(lane, base, parts, other):
    out = other
    for j in range(3):
        out = jnp.where(lane == base + j, parts[j], out)
    return out


def _mesh_pos():
    return lax.axis_index("x"), lax.axis_index("y"), lax.axis_index("c")


def _dev_index(px, py, pc):
    return 4 * px + 2 * py + pc


N_GATHER_SEMS = 9


def _gather_stages(src_ref, out_ref, send_sems, recv_sems, local_sem):
    x, y, c = _mesh_pos()
    me, sibling = (x, y, c), (x, y, 1 - c)
    nbr_x, nbr_y, diag = (1 - x, y), (x, 1 - y), (1 - x, 1 - y)
    half = out_ref.shape[-1] // 2
    left, right = pl.ds(0, half), pl.ds(half, half)

    def copy(k, block, to, cols=None, src=None):
        slot = out_ref.at[_dev_index(*block)]
        if cols is not None:
            slot = slot.at[:, cols]
        return pltpu.make_async_remote_copy(
            src_ref=slot if src is None else src, dst_ref=slot, send_sem=send_sems.at[k], recv_sem=recv_sems.at[k],
            device_id=to, device_id_type=MESH)

    mine = pltpu.make_async_copy(src_ref, out_ref.at[_dev_index(*me)], local_sem)
    first = [copy(0, me, sibling, src=src_ref), copy(1, me, (*nbr_x, c), src=src_ref), copy(2, me, (*nbr_y, c), src=src_ref)]
    relay = [(1, nbr_x, None, nbr_x), (2, nbr_y, None, nbr_y), (3, diag, left, nbr_y), (4, diag, right, nbr_x)]
    onward = [copy(3, (*nbr_x, c), (*nbr_y, c), cols=left), copy(4, (*nbr_y, c), (*nbr_x, c), cols=right)]
    passed = [copy(4 + k, (*block, c), sibling, cols=cols) for k, block, cols, _ in relay]

    def start():
        mine.start()
        for cp in first:
            cp.start()

    def relay_stage(first_item):
        def run():
            for j in (first_item, first_item + 1):
                k, block, cols, frm = relay[j]
                copy(k, (*block, c), (*frm, c), cols=cols).wait_recv()
                if j < 2:
                    onward[j].start()
                passed[j].start()
        return run

    def finish():
        copy(0, sibling, me).wait_recv()
        for k, block, cols, _ in relay:
            copy(4 + k, (*block, 1 - c), me, cols=cols).wait_recv()
        for cp in first + onward + passed:
            cp.wait_send()
        mine.wait()

    return start, relay_stage(0), relay_stage(2), finish


N_REDUCE_SEMS = 10
N_SMALL_SEMS = 4
N_ROWS_SEMS = 7


def _reduce_stages(ins, gs, r1, s2, r2, smalls, send_sems, recv_sems, rows=None):
    n = len(ins)
    x, y, c = _mesh_pos()
    me = _dev_index(x, y, c)
    sibling = (x, y, 1 - c)
    chips = [(x, y), (1 - x, y), (x, 1 - y), (1 - x, 1 - y)]
    peers = []
    for p in range(1, N_DEV):
        px, py, pc = (p >> 2) & 1, (p >> 1) & 1, p & 1
        peers.append((1 - x if px else x, 1 - y if py else y, 1 - c if pc else c))
    base_small = N_REDUCE_SEMS * n

    def remote(src, dst, k, to):
        return pltpu.make_async_remote_copy(src_ref=src, dst_ref=dst, send_sem=send_sems.at[k],
                                            recv_sem=recv_sems.at[k], device_id=to, device_id_type=MESH)

    def level1(a, q):
        return remote(ins[a].at[_dev_index(*chips[q], 1 - c)], r1[a].at[q], N_REDUCE_SEMS * a + q, sibling)

    def level2(a, k):
        half = ins[a].shape[-1] // 2
        left, right = pl.ds(0, half), pl.ds(half, half)
        nbr_x, nbr_y = (*chips[1], c), (*chips[2], c)
        src_slot, dst_slot, cols, to = [(0, 0, left, nbr_x), (1, 1, right, nbr_y), (2, 2, left, nbr_x),
                                        (2, 2, right, nbr_y), (0, 0, right, nbr_x), (1, 1, left, nbr_y)][k]
        return remote(s2[a].at[src_slot, :, cols], r2[a].at[dst_slot, :, cols], N_REDUCE_SEMS * a + 4 + k, to)

    to_sibling = [remote(sm[0], sm[2], base_small + 4 * i, sibling) for i, sm in enumerate(smalls)]
    to_chips = [[remote(sm[3], sm[4].at[j], base_small + 4 * i + 1 + j, (*chips[j + 1], c)) for j in range(3)]
                for i, sm in enumerate(smalls)]
    if rows is not None:
        rows_ref, land_ref, all_ref = rows
        base_rows = base_small + 4 * len(smalls)
        row_sends = [remote(rows_ref, land_ref.at[me], base_rows + k, to) for k, to in enumerate(peers)]

    def start():
        for a in range(n):
            for q in range(4):
                level1(a, q).start()
        for cp in to_sibling:
            cp.start()
        if rows is not None:
            for cp in row_sends:
                cp.start()
            land_ref[me] = rows_ref[...]

    def middle():
        for a in range(n):
            for q in (1, 2, 3, 0):
                level1(a, q).wait_recv()
                pair = ins[a][_dev_index(*chips[q], c)].astype(F32) + r1[a][q].astype(F32)
                if q == 0:
                    gs[a][...] = pair
                else:
                    s2[a][q - 1] = pair.astype(BF16)
                    for k in ((0,), (1,), (2, 3))[q - 1]:
                        level2(a, k).start()
        for i, (small_ref, _, sm_sib, sm_chip, _) in enumerate(smalls):
            to_sibling[i].wait_recv()
            sm_chip[...] = small_ref[...] + sm_sib[...]
            for cp in to_chips[i]:
                cp.start()

    def fold():
        for a in range(n):
            half = ins[a].shape[-1] // 2
            level2(a, 3).wait_recv()
            s2[a][0, :, half:] = (s2[a][0, :, half:].astype(F32) + r2[a][2, :, half:].astype(F32)).astype(BF16)
            level2(a, 4).start()
            level2(a, 2).wait_recv()
            s2[a][1, :, :half] = (s2[a][1, :, :half].astype(F32) + r2[a][2, :, :half].astype(F32)).astype(BF16)
            level2(a, 5).start()

    def finish():
        for a in range(n):
            for k in (0, 1, 4, 5):
                level2(a, k).wait_recv()
            gs[a][...] = gs[a][...] + r2[a][0].astype(F32) + r2[a][1].astype(F32)
            for q in range(4):
                level1(a, q).wait_send()
            for k in range(6):
                level2(a, k).wait_send()
        for i, (_, total_ref, _, sm_chip, sm_recv) in enumerate(smalls):
            for cp in to_chips[i]:
                cp.wait_recv()
            total = None
            for ax in range(2):
                for ay in range(2):
                    dx, dy = x != ax, y != ay
                    term = jnp.where(dx, jnp.where(dy, sm_recv[2], sm_recv[0]), jnp.where(dy, sm_recv[1], sm_chip[...]))
                    total = term if total is None else total + term
            total_ref[...] = total
            for cp in [to_sibling[i]] + to_chips[i]:
                cp.wait_send()
        if rows is not None:
            for k, frm in enumerate(peers):
                remote(rows_ref, land_ref.at[_dev_index(*frm)], base_rows + k, frm).wait_recv()
            all_ref[...] = land_ref[...]
            for cp in row_sends:
                cp.wait_send()

    return start, middle, fold, finish


def _reduce_scratch(shard, smalls, rows=None):
    out = [pltpu.VMEM((lead,) + shard.shape[1:], BF16) for lead in (4, 3, 3)]
    for small in smalls:
        out += [pltpu.VMEM(small.shape, F32), pltpu.VMEM(small.shape, F32), pltpu.VMEM((3,) + small.shape, F32)]
    n_sems = N_REDUCE_SEMS + N_SMALL_SEMS * len(smalls)
    if rows is not None:
        out.append(pltpu.VMEM((N_DEV,) + rows.shape, F32))
        n_sems += N_ROWS_SEMS
    return out + [pltpu.SemaphoreType.DMA((n_sems,))] * 2


def _reduce_grads(gw_in, small, rows):
    def body(in_ref, small_ref, rows_ref, g_ref, total_ref, rows_all_ref,
             r1, s2, r2, sm_sib, sm_chip, sm_recv, rows_land, send_sems, recv_sems):
        stages = _reduce_stages(
            [in_ref], [g_ref], [r1], [s2], [r2], [(small_ref, total_ref, sm_sib, sm_chip, sm_recv)],
            send_sems, recv_sems, rows=(rows_ref, rows_land, rows_all_ref))
        for stage in stages:
            stage()

    return pl.pallas_call(
        body, name="reduce_grads",
        out_shape=[jax.ShapeDtypeStruct(gw_in.shape[1:], F32), jax.ShapeDtypeStruct(small.shape, F32),
                   jax.ShapeDtypeStruct((N_DEV,) + rows.shape, F32)],
        scratch_shapes=_reduce_scratch(gw_in, [small], rows),
        compiler_params=_params(),
    )(gw_in, small, rows)


def _dot3_rhs(a, b):
    a0, a1, a2 = _split3(a)
    b0, b1, b2 = _split3(b)
    return (_dot(a0, b0) + (_dot(a0, b1) + _dot(a1, b0))
            + (_dot(a0, b2) + _dot(a1, b1) + _dot(a2, b0)))


def _gather_and_ada(c, w_in_rows, w_ada):
    cols = w_ada.shape[1]

    def body(c_ref, w_ref, wa_ref, w_all_ref, sc_ref, ada_ref,
             c_land, part, ada_land, send_sems, recv_sems, local_sem, x_send, x_recv):
        x, y, cc = _mesh_pos()
        me = _dev_index(x, y, cc)
        peers = []
        for p in range(1, N_DEV):
            px, py, pc = (p >> 2) & 1, (p >> 1) & 1, p & 1
            peers.append((1 - x if px else x, 1 - y if py else y, 1 - cc if pc else cc))

        def remote(src, dst, k, to):
            return pltpu.make_async_remote_copy(src_ref=src, dst_ref=dst, send_sem=x_send.at[k], recv_sem=x_recv.at[k],
                                                device_id=to, device_id_type=MESH)

        c_sends = [remote(c_ref, c_land.at[me], k, to) for k, to in enumerate(peers)]
        for cp in c_sends:
            cp.start()
        start, relay_near, relay_far, finish = _gather_stages(w_ref, w_all_ref, send_sems, recv_sems, local_sem.at[0])
        start()
        c_land[me] = c_ref[...]
        for k, frm in enumerate(peers):
            remote(c_ref, c_land.at[_dev_index(*frm)], k, frm).wait_recv()
        c_all = jnp.concatenate([c_land[b] for b in range(N_DEV)], axis=0)
        sc = c_all * _sigmoid(c_all)
        sc_ref[...] = sc
        rows = _dot3_rhs(sc, wa_ref[...])
        for b in range(N_DEV):
            part[b] = rows[b:b + 1, :]
        a_sends = [remote(part.at[_dev_index(*to)], ada_land.at[me], 7 + k, to) for k, to in enumerate(peers)]
        for cp in a_sends:
            cp.start()
        ada_land[me] = part[me]
        for k, frm in enumerate(peers):
            remote(part.at[0], ada_land.at[_dev_index(*frm)], 7 + k, frm).wait_recv()
        ada_ref[...] = ada_land[...]

        relay_near()
        relay_far()
        finish()
        for cp in c_sends + a_sends:
            cp.wait_send()

    vmem = pl.BlockSpec(memory_space=pltpu.VMEM)
    return pl.pallas_call(
        body, name="gather_weights",
        in_specs=[vmem, ANY, vmem], out_specs=[ANY, vmem, vmem],
        out_shape=[jax.ShapeDtypeStruct((N_DEV,) + w_in_rows.shape, w_in_rows.dtype),
                   jax.ShapeDtypeStruct((N_DEV, D), F32), jax.ShapeDtypeStruct((N_DEV, 1, cols), F32)],
        scratch_shapes=[pltpu.VMEM((N_DEV, 1, D), F32), pltpu.VMEM((N_DEV, 1, cols), F32), pltpu.VMEM((N_DEV, 1, cols), F32),
                        pltpu.SemaphoreType.DMA((N_GATHER_SEMS,)), pltpu.SemaphoreType.DMA((N_GATHER_SEMS,)),
                        pltpu.SemaphoreType.DMA((1,)),
                        pltpu.SemaphoreType.DMA((14,)), pltpu.SemaphoreType.DMA((14,))],
        compiler_params=_params(),
    )(c, w_in_rows, w_ada)


def _inproj_forward(x, mod, w_main, w_f, b_main, b_f, tile):
    seq = x.shape[0]
    nt = seq // tile

    def body(x_ref, mod_ref, w_ref, wf_ref, b_ref, bf_ref,
             qp_ref, kp_ref, vp_ref, f_ref, p_ref, ga_ref, gp_ref, u_ref, carry_ref):
        i = pl.program_id(0)

        @pl.when(i == 0)
        def _():
            carry_ref[...] = jnp.zeros_like(carry_ref)

        u = x_ref[...] * mod_ref[0:1, :] + mod_ref[1:2, :]
        ub = u.astype(BF16)
        u_ref[...] = ub

        f = _dot_nt(ub, wf_ref[...]) + bf_ref[...]
        f_ref[...] = f
        lane = _lanes((tile, LANE))
        log_f = jnp.where(lane < N_HEADS, jnp.minimum(f, 0.0) - jnp.log(1.0 + jnp.exp(-jnp.abs(f))), 0.0)
        row = lax.broadcasted_iota(jnp.int32, (tile, tile), 0)
        col = lax.broadcasted_iota(jnp.int32, (tile, tile), 1)
        tri = (row >= col).astype(BF16)
        cum = _dot3(tri, log_f) + carry_ref[0:1, :]
        carry_ref[0:1, :] = cum[tile - 1:tile, :]
        cq = [part.astype(F32) for part in _split3(cum)]
        ck = [part.astype(F32) for part in _split3(-cum)]

        def proj(chunk):
            cols = pl.ds(chunk * COL_CHUNK, COL_CHUNK)
            return _dot_nt(ub, w_ref[cols, :]) + b_ref[:, cols]

        def head_tiles(r):
            for pair in range(N_HEADS // 2):
                both = r[:, pair * LANE:(pair + 1) * LANE]
                yield 2 * pair, both
                yield 2 * pair + 1, pltpu.roll(both, HEAD_DIM, 1)

        for h, val in head_tiles(proj(0)):
            extra = jnp.where((lane >= L_CK) & (lane < L_CK + 3), 1.0, 0.0)
            extra = _place3(lane, L_CQ, [part[:, h:h + 1] for part in cq], extra)
            qp_ref[h] = jnp.where(lane < HEAD_DIM, val * Q_SCALE, extra).astype(BF16)
        for h, val in head_tiles(proj(1)):
            ones = ((lane >= L_CQ) & (lane < L_CQ + 3)) | ((lane >= L_LSE) & (lane < L_LSE + 3))
            extra = _place3(lane, L_CK, [part[:, h:h + 1] for part in ck], jnp.where(ones, 1.0, 0.0))
            kp_ref[h] = jnp.where(lane < HEAD_DIM, val, extra).astype(BF16)
        for h, val in head_tiles(proj(2)):
            extra = jnp.where((lane >= HEAD_DIM) & (lane < HEAD_DIM + 3), -1.0, 0.0)
            vp_ref[h] = jnp.where(lane < HEAD_DIM, val, extra).astype(BF16)
        p_ref[...] = proj(3)
        ga_ref[...] = proj(4)
        gp_ref[...] = proj(5)

    head_block = pl.BlockSpec((N_HEADS, tile, LANE), lambda i: (0, i, 0))
    tok = lambda width: pl.BlockSpec((tile, width), lambda i: (i, 0))
    whole = lambda a: pl.BlockSpec(a.shape, lambda i: (0,) * a.ndim)
    padded = jax.ShapeDtypeStruct((N_HEADS, seq, LANE), BF16)
    half = jax.ShapeDtypeStruct((seq, D_ATT), F32)
    return pl.pallas_call(
        body, name="inproj_forward", grid=(nt,),
        in_specs=[tok(D), whole(mod), whole(w_main), whole(w_f), whole(b_main), whole(b_f)],
        out_specs=[head_block, head_block, head_block, tok(LANE), tok(D_POOL), tok(D_ATT), tok(D_POOL),
                   tok(D)],
        out_shape=[padded, padded, padded, jax.ShapeDtypeStruct((seq, LANE), F32), half, half, half,
                   jax.ShapeDtypeStruct((seq, D), BF16)],
        scratch_shapes=[pltpu.VMEM((8, LANE), F32)],
        compiler_params=_params(("arbitrary",)),
    )(x, mod, w_main, w_f, b_main, b_f)


def _attention_forward(qp, kp, vp, w_out, tile):
    seq = qp.shape[1]
    nb = seq // tile
    steps = (N_HEADS // 2) * nb

    def body(q_ref, k_ref, v_ref, wo_ref, att_ref, q2t_ref, wo_all_ref, s_a, s_b, m_ref, acc_ref,
             send_sems, recv_sems, local_sem):
        step = pl.program_id(0) * nb + pl.program_id(1)
        start, relay_near, relay_far, finish = _gather_stages(wo_ref, wo_all_ref, send_sems, recv_sems, local_sem.at[0])
        pl.when(step == 0)(start)
        pl.when(step == steps // 4)(relay_near)
        pl.when(step == (3 * steps) // 4)(relay_far)

        i = pl.program_id(1)
        sub = lax.broadcasted_iota(jnp.int32, (LANE, tile), 0)
        row = lax.broadcasted_iota(jnp.int32, (tile, tile), 0)
        col = lax.broadcasted_iota(jnp.int32, (tile, tile), 1)
        q = [q_ref[0], q_ref[1]]

        def scores(buf, kb):
            rows = pl.ds(pl.multiple_of(kb * tile, tile), tile)
            for hh in range(2):
                buf[hh] = _dot_nt(k_ref[hh, rows, :], q[hh])

        def absorb(buf, kb, masked):
            rows = pl.ds(pl.multiple_of(kb * tile, tile), tile)
            for hh in range(2):
                m = m_ref[hh, 0:1, :]
                s = buf[hh]
                if masked:
                    s = jnp.where(row <= col, s, -1e30)
                m_new = jnp.maximum(m, jnp.max(s, axis=0, keepdims=True))
                p = jnp.exp(s - m_new).astype(BF16)
                acc_ref[hh] = jnp.exp(m - m_new) * acc_ref[hh] + _dot_tn(v_ref[hh, rows, :], p)
                m_ref[hh, 0:1, :] = m_new

        def two_blocks(j, _):
            scores(s_b, 2 * j + 1)
            absorb(s_a, 2 * j, False)
            scores(s_a, 2 * j + 2)
            absorb(s_b, 2 * j + 1, False)
            return 0

        def last_block():
            absorb(s_a, i, True)

        def last_two_blocks():
            scores(s_b, i)
            absorb(s_a, i - 1, False)
            absorb(s_b, i, True)

        scores(s_a, 0)
        m_ref[...] = jnp.full(m_ref.shape, -1e30, F32)
        acc_ref[...] = jnp.zeros_like(acc_ref)
        lax.fori_loop(0, i // 2, two_blocks, 0)
        lax.cond(i % 2 == 0, last_block, last_two_blocks)
        outs = []
        for hh in range(2):
            m, acc = m_ref[hh, 0:1, :], acc_ref[hh]
            l = -acc[HEAD_DIM:HEAD_DIM + 1, :]
            outs.append((acc / l)[:HEAD_DIM, :])
            neg_lse = [part.astype(F32) for part in _split3(-(m + jnp.log(l)))]
            q2t_ref[hh] = _place3(sub, L_LSE, neg_lse, q[hh].astype(F32).T).astype(BF16)
        att_ref[...] = jnp.concatenate(outs, axis=0).T
        pl.when(step == steps - 1)(finish)

    pair = pl.BlockSpec((2, tile, LANE), lambda hp, i: (hp, i, 0))
    full = pl.BlockSpec((2, seq, LANE), lambda hp, i: (hp, 0, 0))
    return pl.pallas_call(
        body, name="attention_forward", grid=(N_HEADS // 2, nb),
        in_specs=[pair, full, full, ANY],
        out_specs=[pl.BlockSpec((tile, LANE), lambda hp, i: (i, hp)),
                   pl.BlockSpec((2, LANE, tile), lambda hp, i: (hp, 0, i)), ANY],
        out_shape=[jax.ShapeDtypeStruct((seq, D_ATT), F32),
                   jax.ShapeDtypeStruct((N_HEADS, LANE, seq), BF16),
                   jax.ShapeDtypeStruct((N_DEV,) + w_out.shape, w_out.dtype)],
        scratch_shapes=[pltpu.VMEM((2, tile, tile), F32), pltpu.VMEM((2, tile, tile), F32),
                        pltpu.VMEM((2, 8, tile), F32), pltpu.VMEM((2, LANE, tile), F32),
                        pltpu.SemaphoreType.DMA((N_GATHER_SEMS,)), pltpu.SemaphoreType.DMA((N_GATHER_SEMS,)),
                        pltpu.SemaphoreType.DMA((1,))],
        compiler_params=_params(("arbitrary", "arbitrary")),
    )(qp, kp, vp, w_out)


def _window_sum(x, halo, window, transposed):
    tile = x.shape[0]

    def split_cat(a):
        hi = a.astype(BF16)
        return jnp.concatenate([hi, (a - hi.astype(F32)).astype(BF16)], axis=1)

    def fold(r):
        return r[:, :LANE] + r[:, LANE:]

    r = lax.broadcasted_iota(jnp.int32, (tile, tile), 0)
    c = lax.broadcasted_iota(jnp.int32, (tile, tile), 1)
    rh = lax.broadcasted_iota(jnp.int32, (HALO, HALO), 0)
    ch = lax.broadcasted_iota(jnp.int32, (HALO, HALO), 1)
    if not transposed:
        band = (c <= r) & (r - c < window)
        edge = (rh + HALO - ch) < window
    else:
        band = (r <= c) & (c - r < window)
        edge = (HALO + ch - rh) < window
    out = fold(_dot(band.astype(BF16), split_cat(x)))
    reach = fold(_dot(edge.astype(BF16), split_cat(halo)))
    if not transposed:
        return jnp.concatenate([out[:HALO] + reach, out[HALO:]], axis=0)
    return jnp.concatenate([out[:tile - HALO], out[tile - HALO:] + reach], axis=0)


def _silu_parts(g):
    sig = _sigmoid(g)
    return g * sig, sig * (1.0 + g * (1.0 - sig))


def _middle(x, tgt, att, g_att, g_pool, p, vecs, pool_vecs, w_out, w_pool, tile):
    seq = x.shape[0]
    nt = seq // tile
    halo_blocks = tile // HALO

    def body(x_ref, tgt_ref, att_ref, ga_ref, gp_ref, p_ref, ph_ref, vec_ref, pvec_ref, wo_ref, wp_ref,
             dxa_ref, do2_ref, dga_ref, dgp_ref, dpooled_ref, gwo_ref, dwp_ref, dvec_ref, dwo_ref, dpvec_ref):
        i = pl.program_id(0)

        @pl.when(i == 0)
        def _():
            dwo_ref[...] = jnp.zeros_like(dwo_ref)
            dwp_ref[...] = jnp.zeros_like(dwp_ref)
            dvec_ref[...] = jnp.zeros_like(dvec_ref)
            dpvec_ref[...] = jnp.zeros_like(dpvec_ref)

        gate, b_out, ln_g, ln_b = (vec_ref[k:k + 1, :] for k in range(4))
        b_pool, pool_scale = pvec_ref[0:1, :], pvec_ref[1:2, :]
        x = x_ref[...]
        p = p_ref[...]
        p_halo = ph_ref[...] * jnp.where(i > 0, 1.0, 0.0)
        pos = i * tile + lax.broadcasted_iota(jnp.int32, (tile, 1), 0) + 1

        pooled, mixed = [], []
        for g, window in enumerate(POOL_WINDOWS):
            cols = slice(g * GROUP_DIM, (g + 1) * GROUP_DIM)
            wsum = _window_sum(p[:, cols], p_halo[:, cols], window, False)
            count = jnp.minimum(pos, window).astype(F32)
            pooled.append(wsum / count - p[:, cols])
            mixed.append(_dot(pooled[g].astype(BF16), wp_ref[g]) + b_pool[:, cols])
        mixed = jnp.concatenate(mixed, axis=1)
        pool = mixed * pool_scale

        att = att_ref[...]
        g_att, g_pool = ga_ref[...], gp_ref[...]
        silu_a, dsilu_a = _silu_parts(g_att)
        silu_p, dsilu_p = _silu_parts(g_pool)
        y_in = jnp.concatenate([att * silu_a, pool * silu_p], axis=1)
        y = _dot(y_in.astype(BF16), wo_ref[...]) + b_out
        h = ALPHA * x + gate * y
        mu = jnp.mean(h, axis=1, keepdims=True)
        hc = h - mu
        var = jnp.mean(hc * hc, axis=1, keepdims=True)
        rstd = lax.rsqrt(var + LN_EPS)
        yhat = hc * rstd
        diff = yhat * ln_g + ln_b - tgt_ref[...]
        loss_rows = jnp.sum(diff * diff, axis=1, keepdims=True)
        d_out = diff * (1.0 / D)

        d_yhat = d_out * ln_g
        dh = rstd * (d_yhat - jnp.mean(d_yhat, axis=1, keepdims=True)
                     - yhat * jnp.mean(d_yhat * yhat, axis=1, keepdims=True))
        dxa_ref[...] = ALPHA * dh
        dy = dh * gate
        dyb = dy.astype(BF16)
        lane = _lanes((1, D))
        loss_row = jnp.where(lane == 0, (0.5 / D) * jnp.sum(loss_rows, axis=0, keepdims=True), 0.0)
        dvec_ref[5:6, :] += jnp.sum(dh * y, axis=0, keepdims=True)
        dvec_ref[0:1, :] += jnp.sum(dy, axis=0, keepdims=True)
        dvec_ref[1:2, :] += jnp.sum(d_out * yhat, axis=0, keepdims=True)
        dvec_ref[2:3, :] += jnp.sum(d_out, axis=0, keepdims=True)
        dvec_ref[4:5, :] += loss_row

        dwo_ref[...] += _dot(y_in.T.astype(BF16), dyb)
        d_yin = _dot_nt(dyb, wo_ref[...])
        d_a, d_pl = d_yin[:, :D_ATT], d_yin[:, D_ATT:]
        d_att = d_a * silu_a
        d_att_t = d_att.T
        prod_t = (d_att * att).T
        sub = lax.broadcasted_iota(jnp.int32, (HEAD_DIM, tile), 0)
        for h in range(N_HEADS):
            rows = slice(h * HEAD_DIM, (h + 1) * HEAD_DIM)
            delta = jnp.sum(prod_t[rows], axis=0, keepdims=True)
            extra = _place3(sub, 0, [part.astype(F32) for part in _split3(delta)], 0.0)
            do2_ref[h] = jnp.concatenate([d_att_t[rows], extra], axis=0).astype(BF16)
        dga_ref[...] = d_a * att * dsilu_a
        dgp_ref[...] = d_pl * pool * dsilu_p
        d_pool = d_pl * silu_p
        d_mixed = d_pool * pool_scale
        dpvec_ref[0:1, :] += jnp.sum(d_mixed, axis=0, keepdims=True)
        dpvec_ref[1:2, :] += jnp.sum(d_pool * mixed, axis=0, keepdims=True)
        d_pooled = []
        for g in range(len(POOL_WINDOWS)):
            cols = slice(g * GROUP_DIM, (g + 1) * GROUP_DIM)
            dmb = d_mixed[:, cols].astype(BF16)
            dwp_ref[g] += _dot(pooled[g].T.astype(BF16), dmb)
            d_pooled.append(_dot_nt(dmb, wp_ref[g]))
        dpooled_ref[...] = jnp.concatenate(d_pooled, axis=1)

        @pl.when(i == nt - 1)
        def _():
            gwo_ref[...] = dwo_ref[...].astype(BF16)
            dvec_ref[3:4, :] = jnp.concatenate([dpvec_ref[0:1, :], dpvec_ref[1:2, :]], axis=1)

    tok = lambda width: pl.BlockSpec((tile, width), lambda i: (i, 0))
    whole = lambda a: pl.BlockSpec(a.shape, lambda i: (0,) * a.ndim)
    halo = pl.BlockSpec((HALO, D_POOL), lambda i: (jnp.maximum(i * halo_blocks - 1, 0), 0))
    half = jax.ShapeDtypeStruct((seq, D_ATT), F32)
    outs = [jax.ShapeDtypeStruct((seq, D), F32), jax.ShapeDtypeStruct((N_HEADS, LANE, seq), BF16), half, half, half,
            jax.ShapeDtypeStruct(w_out.shape, BF16), jax.ShapeDtypeStruct(w_pool.shape, F32),
            jax.ShapeDtypeStruct(vecs.shape, F32)]
    return pl.pallas_call(
        body, name="middle", grid=(nt,),
        in_specs=[tok(D), tok(D), tok(D_ATT), tok(D_ATT), tok(D_POOL), tok(D_POOL), halo,
                  whole(vecs), whole(pool_vecs), whole(w_out), whole(w_pool)],
        out_specs=[tok(D), pl.BlockSpec((N_HEADS, LANE, tile), lambda i: (0, 0, i)),
                   tok(D_ATT), tok(D_POOL), tok(D_POOL),
                   whole(w_out), whole(w_pool), whole(vecs)],
        out_shape=outs,
        scratch_shapes=[pltpu.VMEM(w_out.shape, F32), pltpu.VMEM(pool_vecs.shape, F32)],
        compiler_params=_params(("arbitrary",)),
    )(x, tgt, att, g_att, g_pool, p, p, vecs, pool_vecs, w_out, w_pool)


def _attention_backward(q2t, kp, vp, do2t, gw_out, vecs, pool, tile):
    seq = kp.shape[1]
    nb = seq // tile
    last = N_HEADS // 2 - 1

    def body(qt_ref, k_ref, v_ref, dot_ref, gwo_hbm, vecs_hbm, pool_hbm,
             dq_ref, dk_ref, dv_ref, dcum_ref, g_out_ref, vecs_sum_ref, pool_sum_ref,
             dq_acc, dk_acc, dv_acc, gwo_ref, vecs_ref, pool_ref,
             r1, s2, r2, v_sib, v_chip, v_recv, p_sib, p_chip, p_recv, send_sems, recv_sems):
        hp = pl.program_id(0)
        start, middle, fold, finish = _reduce_stages(
            [gwo_ref], [g_out_ref], [r1], [s2], [r2],
            [(vecs_ref, vecs_sum_ref, v_sib, v_chip, v_recv), (pool_ref, pool_sum_ref, p_sib, p_chip, p_recv)],
            send_sems, recv_sems)

        @pl.when(hp == 0)
        def _():
            pltpu.sync_copy(gwo_hbm, gwo_ref)
            pltpu.sync_copy(vecs_hbm, vecs_ref)
            pltpu.sync_copy(pool_hbm, pool_ref)
            start()

        pl.when(hp == 1)(middle)
        pl.when(hp == 2)(fold)

        row = lax.broadcasted_iota(jnp.int32, (tile, tile), 0)
        col = lax.broadcasted_iota(jnp.int32, (tile, tile), 1)
        dq_acc[...] = jnp.zeros_like(dq_acc)

        def kv_block(kb, _):
            krows = pl.ds(pl.multiple_of(kb * tile, tile), tile)
            k = [k_ref[hh, krows, :] for hh in range(2)]
            v = [v_ref[hh, krows, :] for hh in range(2)]
            k_t = [k[hh].T for hh in range(2)]

            def q_block(qb, masked):
                qcols = pl.ds(pl.multiple_of(qb * tile, tile), tile)
                for hh in range(2):
                    q_t = qt_ref[hh, :, qcols]
                    do_t = dot_ref[hh, :, qcols]
                    s_t = _dot(k[hh], q_t)
                    if masked:
                        s_t = jnp.where(row <= col, s_t, -1e30)
                    p_t = jnp.exp(s_t)
                    ds_t = (p_t * _dot(v[hh], do_t)).astype(BF16)
                    dv_new = _dot_nt(do_t, p_t.astype(BF16))
                    dk_new = _dot_nt(q_t, ds_t)
                    if masked:
                        dv_acc[hh], dk_acc[hh] = dv_new, dk_new
                    else:
                        dv_acc[hh] += dv_new
                        dk_acc[hh] += dk_new
                    dq_acc[hh, :, qcols] += _dot(k_t[hh], ds_t)

            q_block(kb, True)

            def two_later_blocks(j, _):
                q_block(kb + 1 + 2 * j, False)
                q_block(kb + 2 + 2 * j, False)
                return 0

            later = nb - 1 - kb
            lax.fori_loop(0, later // 2, two_later_blocks, 0)
            pl.when(later % 2 == 1)(lambda: q_block(nb - 1, False))
            for hh in range(2):
                dk = dk_acc[hh]
                dk_ref[hh, :, krows] = dk.astype(BF16)
                dv_ref[hh, :, krows] = dv_acc[hh].astype(BF16)
                dcum_ref[hh, :, krows] = -dk[L_CK:L_CK + 1, :]
            return 0

        lax.fori_loop(0, nb, kv_block, 0)
        for hh in range(2):
            dq = dq_acc[hh]
            dcum_ref[hh] += dq[L_CQ:L_CQ + 1, :]
            dq_ref[hh] = (dq * Q_SCALE).astype(BF16)
        pl.when(hp == last)(finish)

    pair = pl.BlockSpec((2, seq, LANE), lambda hp: (hp, 0, 0))
    pair_t = pl.BlockSpec((2, LANE, seq), lambda hp: (hp, 0, 0))
    whole = lambda shape: pl.BlockSpec(shape, lambda hp: (0,) * len(shape))
    grad = jax.ShapeDtypeStruct((N_HEADS, LANE, seq), BF16)
    return pl.pallas_call(
        body, name="attention_backward", grid=(N_HEADS // 2,),
        in_specs=[pair_t, pair, pair, pair_t, ANY, ANY, ANY],
        out_specs=[pair_t, pair_t, pair_t, pl.BlockSpec((2, 1, seq), lambda hp: (hp, 0, 0)),
                   whole(gw_out.shape[1:]), whole(vecs.shape), whole(pool.shape)],
        out_shape=[grad, grad, grad, jax.ShapeDtypeStruct((N_HEADS, 1, seq), F32),
                   jax.ShapeDtypeStruct(gw_out.shape[1:], F32), jax.ShapeDtypeStruct(vecs.shape, F32),
                   jax.ShapeDtypeStruct(pool.shape, F32)],
        scratch_shapes=[pltpu.VMEM((2, LANE, seq), F32), pltpu.VMEM((2, LANE, tile), F32),
                        pltpu.VMEM((2, LANE, tile), F32), pltpu.VMEM(gw_out.shape, BF16),
                        pltpu.VMEM(vecs.shape, F32), pltpu.VMEM(pool.shape, F32)]
        + _reduce_scratch(gw_out, [vecs, pool]),
        compiler_params=_params(("arbitrary",)),
    )(q2t, kp, vp, do2t, gw_out, vecs, pool)


def _inproj_backward(dqp, dkp, dvp, d_cum, f, d_pooled, d_ga, d_gp, x, dxa, u, mod, w_main, w_f, tile):
    seq = x.shape[0]
    nt = seq // tile
    halo_blocks = tile // HALO

    def body(dq_ref, dk_ref, dv_ref, dcum_ref, f_ref, dpo_ref, dph_ref, dga_ref, dgp_ref, x_ref, dxa_ref, u_ref,
             mod_ref, w_ref, wf_ref,
             dx_ref, dproj_ref, dwf_ref, db_ref, dbf_ref, dmod_ref, carry_ref):
        step = pl.program_id(0)
        i = nt - 1 - step

        @pl.when(step == 0)
        def _():
            carry_ref[...] = jnp.zeros_like(carry_ref)
            dwf_ref[...] = jnp.zeros_like(dwf_ref)
            db_ref[...] = jnp.zeros_like(db_ref)
            dbf_ref[...] = jnp.zeros_like(dbf_ref)
            dmod_ref[...] = jnp.zeros_like(dmod_ref)

        ones = jnp.ones((8, tile), BF16)

        def emit(chunk, val):
            cols = pl.ds(chunk * COL_CHUNK, COL_CHUNK)
            db_ref[0:1, cols] += jnp.sum(val, axis=0, keepdims=True)
            vb = val.astype(BF16)
            dproj_ref[:, pl.ds((chunk - 3) * COL_CHUNK, COL_CHUNK)] = vb
            return _dot(vb, w_ref[cols, :])

        d_u = jnp.zeros((tile, D), F32)
        for chunk, ref in enumerate((dq_ref, dk_ref, dv_ref)):
            cols = pl.ds(chunk * COL_CHUNK, COL_CHUNK)
            val_t = ref[:, 0:HEAD_DIM, :].reshape(COL_CHUNK, tile)
            db_ref[:, cols] += _dot_nt(ones, val_t)
            d_u += _dot_tn(val_t, w_ref[cols, :])

        d_pooled = dpo_ref[...]
        d_halo = dph_ref[...] * jnp.where(i < nt - 1, 1.0, 0.0)
        pos = i * tile + lax.broadcasted_iota(jnp.int32, (tile, 1), 0) + 1
        d_p = []
        for g, window in enumerate(POOL_WINDOWS):
            cols = slice(g * GROUP_DIM, (g + 1) * GROUP_DIM)
            scaled = d_pooled[:, cols] / jnp.minimum(pos, window).astype(F32)
            d_p.append(_window_sum(scaled, d_halo[:, cols] * (1.0 / window), window, True) - d_pooled[:, cols])
        d_u += emit(3, jnp.concatenate(d_p, axis=1))
        d_u += emit(4, dga_ref[...])
        d_u += emit(5, dgp_ref[...])

        row = lax.broadcasted_iota(jnp.int32, (tile, tile), 0)
        col = lax.broadcasted_iota(jnp.int32, (tile, tile), 1)
        later = (row >= col).astype(BF16)
        d_logf = sum(_dot(part, later) for part in _split3(dcum_ref[:, 0, :])) + carry_ref[:, 0:1]
        carry_ref[:, 0:1] = d_logf[:, 0:1]
        d_f = d_logf * _sigmoid(-f_ref[...].T[0:N_HEADS, :])
        d_f = jnp.concatenate([d_f, jnp.zeros((LANE - N_HEADS, tile), F32)], axis=0)
        dbf_ref[...] += sum(_dot_nt(ones, part) for part in _split3(d_f))
        d_fb = d_f.astype(BF16)
        d_u += _dot_tn(d_fb, wf_ref[...])
        dwf_ref[...] += _dot(d_fb, u_ref[...])

        x = x_ref[...]
        dx_ref[...] = dxa_ref[...] + d_u * mod_ref[0:1, :]
        dmod_ref[0:1, :] += jnp.sum(d_u * x, axis=0, keepdims=True)
        dmod_ref[1:2, :] += jnp.sum(d_u, axis=0, keepdims=True)

    rev = lambda step: nt - 1 - step
    tok = lambda width: pl.BlockSpec((tile, width), lambda s: (rev(s), 0))
    head_block = pl.BlockSpec((N_HEADS, LANE, tile), lambda s: (0, 0, rev(s)))
    whole = lambda a: pl.BlockSpec(a.shape, lambda s: (0,) * a.ndim)
    halo = pl.BlockSpec((HALO, D_POOL), lambda s: (jnp.minimum((rev(s) + 1) * halo_blocks, seq // HALO - 1), 0))
    small = lambda width: jax.ShapeDtypeStruct((8, width), F32)
    n_rest = N_MAIN - OFF_P
    return pl.pallas_call(
        body, name="inproj_backward", grid=(nt,),
        in_specs=[head_block, head_block, head_block, pl.BlockSpec((N_HEADS, 1, tile), lambda s: (0, 0, rev(s))),
                  tok(LANE), tok(D_POOL), halo, tok(D_ATT), tok(D_POOL),
                  tok(D), tok(D), tok(D),
                  whole(mod), whole(w_main), whole(w_f)],
        out_specs=[tok(D), tok(n_rest), pl.BlockSpec((LANE, D), lambda s: (0, 0)),
                   pl.BlockSpec((8, N_MAIN), lambda s: (0, 0)), pl.BlockSpec((8, LANE), lambda s: (0, 0)),
                   pl.BlockSpec((8, D), lambda s: (0, 0))],
        out_shape=[jax.ShapeDtypeStruct((seq, D), F32), jax.ShapeDtypeStruct((seq, n_rest), BF16),
                   jax.ShapeDtypeStruct((LANE, D), F32), small(N_MAIN), small(LANE), small(D)],
        scratch_shapes=[pltpu.VMEM((8, LANE), F32)],
        compiler_params=_params(("arbitrary",)),
    )(dqp, dkp, dvp, d_cum, f, d_pooled, d_pooled, d_ga, d_gp, x, dxa, u, mod, w_main, w_f)


def _weight_grad(dproj, u, k_tile):
    seq, n_cols = dproj.shape
    nk = seq // k_tile

    def body(dp_ref, u_ref, out_ref, acc_ref):
        k = pl.program_id(1)

        @pl.when(k == 0)
        def _():
            acc_ref[...] = jnp.zeros_like(acc_ref)

        acc_ref[...] += _dot_tn(dp_ref[...], u_ref[...])

        @pl.when(k == nk - 1)
        def _():
            out_ref[...] = acc_ref[...].astype(BF16)

    return pl.pallas_call(
        body, name="weight_grad", grid=(n_cols // COL_CHUNK, nk),
        in_specs=[pl.BlockSpec((k_tile, COL_CHUNK), lambda n, k: (k, n)),
                  pl.BlockSpec((k_tile, D), lambda n, k: (k, 0))],
        out_specs=pl.BlockSpec((COL_CHUNK, D), lambda n, k: (n, 0)),
        out_shape=jax.ShapeDtypeStruct((n_cols, D), BF16),
        scratch_shapes=[pltpu.VMEM((COL_CHUNK, D), F32)],
        compiler_params=_params(("arbitrary", "arbitrary")),
    )(dproj, u)


def _weight_grad_heads(grad_t, u, k_tile, name):
    seq = u.shape[0]
    nk = seq // k_tile
    rows = N_HEADS * HEAD_DIM

    def body(g_ref, u_ref, out_ref, acc_ref):
        k = pl.program_id(0)

        @pl.when(k == 0)
        def _():
            acc_ref[...] = jnp.zeros_like(acc_ref)

        acc_ref[...] += _dot(g_ref[...].reshape(rows, k_tile), u_ref[...])

        @pl.when(k == nk - 1)
        def _():
            out_ref[...] = acc_ref[...].astype(BF16)

    return pl.pallas_call(
        body, name=name, grid=(nk,),
        in_specs=[pl.BlockSpec((N_HEADS, HEAD_DIM, k_tile), lambda k: (0, 0, k)),
                  pl.BlockSpec((k_tile, D), lambda k: (k, 0))],
        out_specs=pl.BlockSpec((rows, D), lambda k: (0, 0)),
        out_shape=jax.ShapeDtypeStruct((rows, D), BF16),
        scratch_shapes=[pltpu.VMEM((rows, D), F32)],
        compiler_params=_params(("arbitrary",)),
    )(grad_t, u)


def _adamw(w, g, m, v):
    m = ADAM_B1 * m + (1.0 - ADAM_B1) * g
    v = ADAM_B2 * v + (1.0 - ADAM_B2) * (g * g)
    m_hat = m / (1.0 - ADAM_B1 ** ADAM_STEP)
    v_hat = v / (1.0 - ADAM_B2 ** ADAM_STEP)
    delta = -ADAM_LR * (m_hat / (jnp.sqrt(v_hat) + ADAM_EPS) + ADAM_WD * w)
    return delta, m, v


def _adamw_call(g, w, m, v, lead_tile, name):
    nr = w.shape[0] // lead_tile

    def body(gi_ref, w_ref, m_ref, v_ref, g_ref, d_ref, nm_ref, nv_ref):
        g = gi_ref[...]
        g_ref[...] = g
        d_ref[...], nm_ref[...], nv_ref[...] = _adamw(w_ref[...], g, m_ref[...], v_ref[...])

    blk = pl.BlockSpec((lead_tile,) + w.shape[1:], lambda r: (r,) + (0,) * (w.ndim - 1))
    shape = jax.ShapeDtypeStruct(w.shape, F32)
    return pl.pallas_call(
        body, name=name, grid=(nr,),
        in_specs=[blk, blk, blk, blk], out_specs=[blk, blk, blk, blk],
        out_shape=[shape, shape, shape, shape],
        compiler_params=_params(("arbitrary",)),
    )(g, w, m, v)


def _sum_adamw(parts, w, m, v, row_tile, name):
    rows, cols = w.shape
    nr = rows // row_tile

    def body(parts_ref, w_ref, m_ref, v_ref, g_ref, d_ref, nm_ref, nv_ref):
        g = parts_ref[0]
        for k in range(1, N_DEV):
            g = g + parts_ref[k]
        g_ref[...] = g
        d_ref[...], nm_ref[...], nv_ref[...] = _adamw(w_ref[...], g, m_ref[...], v_ref[...])

    blk = pl.BlockSpec((row_tile, cols), lambda r: (r, 0))
    shape = jax.ShapeDtypeStruct(w.shape, F32)
    return pl.pallas_call(
        body, name=name, grid=(nr,),
        in_specs=[pl.BlockSpec((N_DEV, row_tile, cols), lambda r: (0, r, 0)), blk, blk, blk],
        out_specs=[blk, blk, blk, blk],
        out_shape=[shape, shape, shape, shape],
        compiler_params=_params(("arbitrary",)),
    )(parts, w, m, v)


def _ada_adamw(sc_t, d_ada, w, m, v):
    def body(sc_ref, d_ref, w_ref, m_ref, v_ref, g_ref, dl_ref, nm_ref, nv_ref):
        g = sc_ref[:, 0:1] * d_ref[0:1, :]
        for b in range(1, N_DEV):
            g = g + sc_ref[:, b:b + 1] * d_ref[b:b + 1, :]
        g_ref[...] = g
        dl_ref[...], nm_ref[...], nv_ref[...] = _adamw(w_ref[...], g, m_ref[...], v_ref[...])

    row_tile = 256
    blk = pl.BlockSpec((row_tile, w.shape[1]), lambda r: (r, 0))
    shape = jax.ShapeDtypeStruct(w.shape, F32)
    return pl.pallas_call(
        body, name="ada_adamw", grid=(w.shape[0] // row_tile,),
        in_specs=[pl.BlockSpec((row_tile, N_DEV), lambda r: (r, 0)), pl.BlockSpec(d_ada.shape, lambda r: (0, 0)),
                  blk, blk, blk],
        out_specs=[blk, blk, blk, blk], out_shape=[shape, shape, shape, shape],
        compiler_params=_params(("arbitrary",)),
    )(sc_t, d_ada, w, m, v)


F_LO, F_HI = 3 * D_ATT, 3 * D_ATT + N_HEADS


def _split_forget(a, axis):
    idx = lambda lo, hi: tuple(slice(lo, hi) if d == axis else slice(None) for d in range(a.ndim))
    pad = [(0, LANE - N_HEADS) if d == axis else (0, 0) for d in range(a.ndim)]
    return jnp.concatenate([a[idx(0, F_LO)], a[idx(F_HI, D_IN)]], axis=axis), jnp.pad(a[idx(F_LO, F_HI)], pad)


def _join_forget(main, f, axis):
    idx = lambda lo, hi: tuple(slice(lo, hi) if d == axis else slice(None) for d in range(main.ndim))
    return jnp.concatenate([main[idx(0, F_LO)], f[idx(0, N_HEADS)], main[idx(F_LO, N_MAIN)]], axis=axis)


def _adamw_rows(grad_rows, params):
    n = len(params)

    def body(g_ref, *refs):
        ins, outs = refs[:3 * n], refs[3 * n:]
        for i, (row, lo, hi, _, _, _) in enumerate(params):
            g = g_ref[row:row + 1, lo:hi]
            outs[4 * i][...] = g
            outs[4 * i + 1][...], outs[4 * i + 2][...], outs[4 * i + 3][...] = _adamw(
                ins[3 * i][...], g, ins[3 * i + 1][...], ins[3 * i + 2][...])

    flat = pl.pallas_call(
        body, name="adamw_rows",
        out_shape=[jax.ShapeDtypeStruct(w.shape, F32) for _, _, _, w, _, _ in params for _ in range(4)],
        compiler_params=_params(),
    )(grad_rows, *[a for _, _, _, w, m, v in params for a in (w, m, v)])
    return [flat[4 * i:4 * i + 4] for i in range(n)]


def kernel(x, c, w_ada, b_ada, w_in, b_in, w_pool_mix, b_pool_mix, pool_scale, w_out, b_out, ln_g, ln_b, loss_target, m_w_ada, m_b_ada, m_w_in, m_b_in, m_w_pool_mix, m_b_pool_mix, m_pool_scale, m_w_out, m_b_out, m_ln_g, m_ln_b, v_w_ada, v_b_ada, v_w_in, v_b_in, v_w_pool_mix, v_b_pool_mix, v_pool_scale, v_w_out, v_b_out, v_ln_g, v_ln_b):
    seq = x.shape[1]
    tile = min(256, seq)
    attn_tile = min(512, max(128, seq // 4))
    me = _dev_index(*_mesh_pos())
    x2, tgt = x[0], loss_target[0]

    rows_in = D_IN // N_DEV
    w_in_g, sc_all, ada_mine = _gather_and_ada(c, w_in[0].T.astype(BF16), w_ada[0])
    ada = ada_mine.reshape(1, D_ADA) + b_ada
    shift, scale, gate = ada[:, 0:D], ada[:, D:2 * D], ada[:, 2 * D:]
    mod = jnp.concatenate([1.0 + scale, shift, jnp.zeros((6, D), F32)], axis=0)

    w_main, w_f = _split_forget(w_in_g.reshape(D_IN, D), 0)
    b_main, b_f = _split_forget(b_in, 1)

    qp, kp, vp, f, p, g_att, g_pool, u = _inproj_forward(x2, mod, w_main, w_f, b_main, b_f, tile)
    att, q2t, w_out_g = _attention_forward(qp, kp, vp, w_out[0].astype(BF16), attn_tile)

    vecs = jnp.concatenate([gate, b_out, ln_g, ln_b, jnp.zeros((4, D), F32)], axis=0)
    pool_vecs = jnp.concatenate([b_pool_mix.reshape(1, D_POOL), pool_scale, jnp.zeros((6, D_POOL), F32)], axis=0)
    dxa, do2, d_ga, d_gp, d_pooled, gw_out, dw_pool, dvec = _middle(
        x2, tgt, att, g_att, g_pool, p, vecs, pool_vecs, w_out_g.reshape(D, D), w_pool_mix[0].astype(BF16), tile)

    pool_rows = w_pool_mix.shape[1] * GROUP_DIM
    dqp, dkp, dvp, d_cum, g_out, dvec_sum, dw_pool_sum = _attention_backward(
        q2t, kp, vp, do2, gw_out.reshape(N_DEV, D // N_DEV, D), dvec, dw_pool.reshape(pool_rows, GROUP_DIM), attn_tile)
    dx, dproj, dw_f, db_main, db_f, dmod = _inproj_backward(
        dqp, dkp, dvp, d_cum, f, d_pooled, d_ga, d_gp, x2, dxa, u, mod, w_main, w_f, tile)
    k_tile = min(1024, seq)
    dw_q, dw_k, dw_v = (_weight_grad_heads(g, u, k_tile, "weight_grad_" + n)
                        for g, n in ((dqp, "q"), (dkp, "k"), (dvp, "v")))
    dw_rest = _weight_grad(dproj, u, k_tile)

    dw_main = jnp.concatenate([dw_q, dw_k, dw_v, dw_rest], axis=0)
    gw_in = _join_forget(dw_main, dw_f.astype(BF16), 0).reshape(N_DEV, rows_in, D)
    d_ada = jnp.concatenate([dmod[1:2], dmod[0:1], dvec[5:6]], axis=1)
    g_in_rows, g_b_in, d_ada_all = _reduce_grads(gw_in, _join_forget(db_main[0:1], db_f[0:1], 1), d_ada)

    def rows3(a):
        return a[0].T.reshape(rows_in, D // LANE, LANE)

    outs_in = _adamw_call(g_in_rows.reshape(rows_in, D // LANE, LANE), rows3(w_in), rows3(m_w_in), rows3(v_w_in),
                          rows_in // 5, "adamw_w_in")
    g_w_in, d_w_in, nm_w_in, nv_w_in = (a.reshape(rows_in, D).T for a in outs_in)
    g_w_out, d_w_out, nm_w_out, nv_w_out = _adamw_call(g_out, w_out[0], m_w_out[0], v_w_out[0], D // N_DEV, "adamw_w_out")

    flat_pool = lambda a: a.reshape(1, D_POOL)
    rows = _adamw_rows(dvec_sum, [
        (0, 0, D, b_out, m_b_out, v_b_out), (1, 0, D, ln_g, m_ln_g, v_ln_g), (2, 0, D, ln_b, m_ln_b, v_ln_b),
        (3, 0, D_POOL, flat_pool(b_pool_mix), flat_pool(m_b_pool_mix), flat_pool(v_b_pool_mix)),
        (3, D_POOL, 2 * D_POOL, pool_scale, m_pool_scale, v_pool_scale)])
    pools = _adamw_call(dw_pool_sum, *(a.reshape(pool_rows, GROUP_DIM) for a in (w_pool_mix, m_w_pool_mix, v_w_pool_mix)),
                        pool_rows, "adamw_w_pool")
    b_ins = _adamw_call(g_b_in, b_in, m_b_in, v_b_in, 1, "adamw_b_in")
    small = {"b_out": rows[0], "ln_g": rows[1], "ln_b": rows[2],
             "b_pool": [a.reshape(b_pool_mix.shape) for a in rows[3]], "pool_scale": rows[4],
             "w_pool": [a.reshape(w_pool_mix.shape) for a in pools], "b_in": b_ins}
    g_s, d_s, nm_s, nv_s = ({k: r[j] for k, r in small.items()} for j in range(4))
    loss = dvec_sum[4, 0]

    g_b_ada, d_b_ada, nm_b_ada, nv_b_ada = _sum_adamw(d_ada_all, b_ada, m_b_ada, v_b_ada, 1, "adamw_b_ada")
    d_ada_local = lax.dynamic_slice_in_dim(d_ada_all.reshape(N_DEV, D_ADA), me * (D_ADA // N_DEV), D_ADA // N_DEV, axis=1)
    g_w_ada, d_w_ada, nm_w_ada, nv_w_ada = _ada_adamw(sc_all.T, d_ada_local, w_ada[0], m_w_ada[0], v_w_ada[0])

    def ordered(w_ada_, b_ada_, w_in_, w_out_, s):
        return (w_ada_[None], b_ada_, w_in_[None], s["b_in"], s["w_pool"], s["b_pool"], s["pool_scale"],
                w_out_[None], s["b_out"], s["ln_g"], s["ln_b"])

    return (loss, dx[None],
            *ordered(g_w_ada, g_b_ada, g_w_in, g_w_out, g_s),
            *ordered(d_w_ada, d_b_ada, d_w_in, d_w_out, d_s),
            *ordered(nm_w_ada, nm_b_ada, nm_w_in, nm_w_out, nm_s),
            *ordered(nv_w_ada, nv_b_ada, nv_w_in, nv_w_out, nv_s))
```

```python
import jax
import jax.numpy as jnp
from jax import lax
from jax.experimental import pallas as pl
from jax.experimental.pallas import tpu as pltpu

F32 = jnp.float32
BF16 = jnp.bfloat16

N_DEV = 8
D = 1024
N_HEADS = 8
HEAD_DIM = 64
D_ATT = 512
D_POOL = 512
POOL_WINDOWS = (2, 4, 8, 16)
GROUP_DIM = 128
HALO = 16
LANE = 128
D_IN = 3080
D_ADA = 3072
N_MAIN = 3072
OFF_P = 1536
COL_CHUNK = 512
Q_SCALE = 0.125
LOG2E = 1.4426950408889634
LN2 = 0.6931471805599453
LN_EPS = 1e-5
ALPHA = 2.0 ** 0.25
L_CQ, L_CK, L_LSE = 64, 67, 70

ADAM_LR, ADAM_B1, ADAM_B2, ADAM_EPS, ADAM_WD, ADAM_STEP = 0.001, 0.9, 0.999, 1e-08, 0.01, 10
VMEM_LIMIT = 56 * 1024 * 1024

MESH = pl.DeviceIdType.MESH
ANY = pl.BlockSpec(memory_space=pl.ANY)


def _params(sem=None, vmem=VMEM_LIMIT):
    return pltpu.CompilerParams(dimension_semantics=sem, vmem_limit_bytes=vmem)


def _split3(a):
    hi = a.astype(BF16)
    r = a - hi.astype(F32)
    mid = r.astype(BF16)
    lo = (r - mid.astype(F32)).astype(BF16)
    return hi, mid, lo


def _dot(a, b):
    return jnp.dot(a, b, preferred_element_type=F32)


def _dot_nt(a, b):
    return lax.dot_general(a, b, (((1,), (1,)), ((), ())), preferred_element_type=F32)


def _dot_tn(a, b):
    return lax.dot_general(a, b, (((0,), (0,)), ((), ())), preferred_element_type=F32)


def _dot3(m01, a):
    hi, mid, lo = _split3(a)
    return _dot(m01, hi) + _dot(m01, mid) + _dot(m01, lo)


def _sigmoid(z):
    return 1.0 / (1.0 + jnp.exp(-z))


def _lanes(shape):
    return lax.broadcasted_iota(jnp.int32, shape, len(shape) - 1)


def _place3(lane, base, parts, other):
    out = other
    for j in range(3):
        out = jnp.where(lane == base + j, parts[j], out)
    return out


def _mesh_pos():
    return lax.axis_index("x"), lax.axis_index("y"), lax.axis_index("c")


def _dev_index(px, py, pc):
    return 4 * px + 2 * py + pc


N_GATHER_SEMS = 9


def _gather_stages(src_ref, out_ref, send_sems, recv_sems, local_sem):
    x, y, c = _mesh_pos()
    me, sibling = (x, y, c), (x, y, 1 - c)
    nbr_x, nbr_y, diag = (1 - x, y), (x, 1 - y), (1 - x, 1 - y)
    half = out_ref.shape[-1] // 2
    left, right = pl.ds(0, half), pl.ds(half, half)

    def copy(k, block, to, cols=None, src=None):
        slot = out_ref.at[_dev_index(*block)]
        if cols is not None:
            slot = slot.at[:, cols]
        return pltpu.make_async_remote_copy(
            src_ref=slot if src is None else src, dst_ref=slot, send_sem=send_sems.at[k], recv_sem=recv_sems.at[k],
            device_id=to, device_id_type=MESH)

    mine = pltpu.make_async_copy(src_ref, out_ref.at[_dev_index(*me)], local_sem)
    first = [copy(0, me, sibling, src=src_ref), copy(1, me, (*nbr_x, c), src=src_ref), copy(2, me, (*nbr_y, c), src=src_ref)]
    relay = [(1, nbr_x, None, nbr_x), (2, nbr_y, None, nbr_y), (3, diag, left, nbr_y), (4, diag, right, nbr_x)]
    onward = [copy(3, (*nbr_x, c), (*nbr_y, c), cols=left), copy(4, (*nbr_y, c), (*nbr_x, c), cols=right)]
    passed = [copy(4 + k, (*block, c), sibling, cols=cols) for k, block, cols, _ in relay]

    def start():
        mine.start()
        for cp in first:
            cp.start()

    def relay_stage(first_item):
        def run():
            for j in (first_item, first_item + 1):
                k, block, cols, frm = relay[j]
                copy(k, (*block, c), (*frm, c), cols=cols).wait_recv()
                if j < 2:
                    onward[j].start()
                passed[j].start()
        return run

    def finish():
        copy(0, sibling, me).wait_recv()
        for k, block, cols, _ in relay:
            copy(4 + k, (*block, 1 - c), me, cols=cols).wait_recv()
        for cp in first + onward + passed:
            cp.wait_send()
        mine.wait()

    return start, relay_stage(0), relay_stage(2), finish


N_REDUCE_SEMS = 10
N_SMALL_SEMS = 4
N_ROWS_SEMS = 7


def _reduce_stages(ins, gs, r1, s2, r2, smalls, send_sems, recv_sems, rows=None):
    n = len(ins)
    x, y, c = _mesh_pos()
    me = _dev_index(x, y, c)
    sibling = (x, y, 1 - c)
    chips = [(x, y), (1 - x, y), (x, 1 - y), (1 - x, 1 - y)]
    peers = []
    for p in range(1, N_DEV):
        px, py, pc = (p >> 2) & 1, (p >> 1) & 1, p & 1
        peers.append((1 - x if px else x, 1 - y if py else y, 1 - c if pc else c))
    base_small = N_REDUCE_SEMS * n

    def remote(src, dst, k, to):
        return pltpu.make_async_remote_copy(src_ref=src, dst_ref=dst, send_sem=send_sems.at[k],
                                            recv_sem=recv_sems.at[k], device_id=to, device_id_type=MESH)

    def level1(a, q):
        return remote(ins[a].at[_dev_index(*chips[q], 1 - c)], r1[a].at[q], N_REDUCE_SEMS * a + q, sibling)

    def level2(a, k):
        half = ins[a].shape[-1] // 2
        left, right = pl.ds(0, half), pl.ds(half, half)
        nbr_x, nbr_y = (*chips[1], c), (*chips[2], c)
        src_slot, dst_slot, cols, to = [(0, 0, left, nbr_x), (1, 1, right, nbr_y), (2, 2, left, nbr_x),
                                        (2, 2, right, nbr_y), (0, 0, right, nbr_x), (1, 1, left, nbr_y)][k]
        return remote(s2[a].at[src_slot, :, cols], r2[a].at[dst_slot, :, cols], N_REDUCE_SEMS * a + 4 + k, to)

    to_sibling = [remote(sm[0], sm[2], base_small + 4 * i, sibling) for i, sm in enumerate(smalls)]
    to_chips = [[remote(sm[3], sm[4].at[j], base_small + 4 * i + 1 + j, (*chips[j + 1], c)) for j in range(3)]
                for i, sm in enumerate(smalls)]
    if rows is not None:
        rows_ref, land_ref, all_ref = rows
        base_rows = base_small + 4 * len(smalls)
        row_sends = [remote(rows_ref, land_ref.at[me], base_rows + k, to) for k, to in enumerate(peers)]

    def start():
        for a in range(n):
            for q in range(4):
                level1(a, q).start()
        for cp in to_sibling:
            cp.start()
        if rows is not None:
            for cp in row_sends:
                cp.start()
            land_ref[me] = rows_ref[...]

    def middle():
        for a in range(n):
            for q in (1, 2, 3, 0):
                level1(a, q).wait_recv()
                pair = ins[a][_dev_index(*chips[q], c)].astype(F32) + r1[a][q].astype(F32)
                if q == 0:
                    gs[a][...] = pair
                else:
                    s2[a][q - 1] = pair.astype(BF16)
                    for k in ((0,), (1,), (2, 3))[q - 1]:
                        level2(a, k).start()
        for i, (small_ref, _, sm_sib, sm_chip, _) in enumerate(smalls):
            to_sibling[i].wait_recv()
            sm_chip[...] = small_ref[...] + sm_sib[...]
            for cp in to_chips[i]:
                cp.start()

    def fold():
        for a in range(n):
            half = ins[a].shape[-1] // 2
            level2(a, 3).wait_recv()
            s2[a][0, :, half:] = (s2[a][0, :, half:].astype(F32) + r2[a][2, :, half:].astype(F32)).astype(BF16)
            level2(a, 4).start()
            level2(a, 2).wait_recv()
            s2[a][1, :, :half] = (s2[a][1, :, :half].astype(F32) + r2[a][2, :, :half].astype(F32)).astype(BF16)
            level2(a, 5).start()

    def finish():
        for a in range(n):
            for k in (0, 1, 4, 5):
                level2(a, k).wait_recv()
            gs[a][...] = gs[a][...] + r2[a][0].astype(F32) + r2[a][1].astype(F32)
            for q in range(4):
                level1(a, q).wait_send()
            for k in range(6):
                level2(a, k).wait_send()
        for i, (_, total_ref, _, sm_chip, sm_recv) in enumerate(smalls):
            for cp in to_chips[i]:
                cp.wait_recv()
            total = None
            for ax in range(2):
                for ay in range(2):
                    dx, dy = x != ax, y != ay
                    term = jnp.where(dx, jnp.where(dy, sm_recv[2], sm_recv[0]), jnp.where(dy, sm_recv[1], sm_chip[...]))
                    total = term if total is None else total + term
            total_ref[...] = total
            for cp in [to_sibling[i]] + to_chips[i]:
                cp.wait_send()
        if rows is not None:
            for k, frm in enumerate(peers):
                remote(rows_ref, land_ref.at[_dev_index(*frm)], base_rows + k, frm).wait_recv()
            all_ref[...] = land_ref[...]
            for cp in row_sends:
                cp.wait_send()

    return start, middle, fold, finish


def _reduce_scratch(shard, smalls, rows=None):
    out = [pltpu.VMEM((lead,) + shard.shape[1:], BF16) for lead in (4, 3, 3)]
    for small in smalls:
        out += [pltpu.VMEM(small.shape, F32), pltpu.VMEM(small.shape, F32), pltpu.VMEM((3,) + small.shape, F32)]
    n_sems = N_REDUCE_SEMS + N_SMALL_SEMS * len(smalls)
    if rows is not None:
        out.append(pltpu.VMEM((N_DEV,) + rows.shape, F32))
        n_sems += N_ROWS_SEMS
    return out + [pltpu.SemaphoreType.DMA((n_sems,))] * 2


def _reduce_grads(gw_in, small, rows):
    def body(in_ref, small_ref, rows_ref, g_ref, total_ref, rows_all_ref,
             r1, s2, r2, sm_sib, sm_chip, sm_recv, rows_land, send_sems, recv_sems):
        stages = _reduce_stages(
            [in_ref], [g_ref], [r1], [s2], [r2], [(small_ref, total_ref, sm_sib, sm_chip, sm_recv)],
            send_sems, recv_sems, rows=(rows_ref, rows_land, rows_all_ref))
        for stage in stages:
            stage()

    return pl.pallas_call(
        body, name="reduce_grads",
        out_shape=[jax.ShapeDtypeStruct(gw_in.shape[1:], F32), jax.ShapeDtypeStruct(small.shape, F32),
                   jax.ShapeDtypeStruct((N_DEV,) + rows.shape, F32)],
        scratch_shapes=_reduce_scratch(gw_in, [small], rows),
        compiler_params=_params(),
    )(gw_in, small, rows)


def _dot3_rhs(a, b):
    a0, a1, a2 = _split3(a)
    b0, b1, b2 = _split3(b)
    return (_dot(a0, b0) + (_dot(a0, b1) + _dot(a1, b0))
            + (_dot(a0, b2) + _dot(a1, b1) + _dot(a2, b0)))


def _gather_and_ada(c, w_in_rows, w_ada):
    cols = w_ada.shape[1]

    def body(c_ref, w_ref, wa_ref, w_all_ref, sc_ref, ada_ref,
             c_land, part, ada_land, send_sems, recv_sems, local_sem, x_send, x_recv):
        x, y, cc = _mesh_pos()
        me = _dev_index(x, y, cc)
        peers = []
        for p in range(1, N_DEV):
            px, py, pc = (p >> 2) & 1, (p >> 1) & 1, p & 1
            peers.append((1 - x if px else x, 1 - y if py else y, 1 - cc if pc else cc))

        def remote(src, dst, k, to):
            return pltpu.make_async_remote_copy(src_ref=src, dst_ref=dst, send_sem=x_send.at[k], recv_sem=x_recv.at[k],
                                                device_id=to, device_id_type=MESH)

        c_sends = [remote(c_ref, c_land.at[me], k, to) for k, to in enumerate(peers)]
        for cp in c_sends:
            cp.start()
        start, relay_near, relay_far, finish = _gather_stages(w_ref, w_all_ref, send_sems, recv_sems, local_sem.at[0])
        start()
        c_land[me] = c_ref[...]
        for k, frm in enumerate(peers):
            remote(c_ref, c_land.at[_dev_index(*frm)], k, frm).wait_recv()
        c_all = jnp.concatenate([c_land[b] for b in range(N_DEV)], axis=0)
        sc = c_all * _sigmoid(c_all)
        sc_ref[...] = sc
        rows = _dot3_rhs(sc, wa_ref[...])
        for b in range(N_DEV):
            part[b] = rows[b:b + 1, :]
        a_sends = [remote(part.at[_dev_index(*to)], ada_land.at[me], 7 + k, to) for k, to in enumerate(peers)]
        for cp in a_sends:
            cp.start()
        ada_land[me] = part[me]
        for k, frm in enumerate(peers):
            remote(part.at[0], ada_land.at[_dev_index(*frm)], 7 + k, frm).wait_recv()
        ada_ref[...] = ada_land[...]

        relay_near()
        relay_far()
        finish()
        for cp in c_sends + a_sends:
            cp.wait_send()

    vmem = pl.BlockSpec(memory_space=pltpu.VMEM)
    return pl.pallas_call(
        body, name="gather_weights",
        in_specs=[vmem, ANY, vmem], out_specs=[ANY, vmem, vmem],
        out_shape=[jax.ShapeDtypeStruct((N_DEV,) + w_in_rows.shape, w_in_rows.dtype),
                   jax.ShapeDtypeStruct((N_DEV, D), F32), jax.ShapeDtypeStruct((N_DEV, 1, cols), F32)],
        scratch_shapes=[pltpu.VMEM((N_DEV, 1, D), F32), pltpu.VMEM((N_DEV, 1, cols), F32), pltpu.VMEM((N_DEV, 1, cols), F32),
                        pltpu.SemaphoreType.DMA((N_GATHER_SEMS,)), pltpu.SemaphoreType.DMA((N_GATHER_SEMS,)),
                        pltpu.SemaphoreType.DMA((1,)),
                        pltpu.SemaphoreType.DMA((14,)), pltpu.SemaphoreType.DMA((14,))],
        compiler_params=_params(),
    )(c, w_in_rows, w_ada)


def _inproj_forward(x, mod, w_main, w_f, b_main, b_f, tile):
    seq = x.shape[0]
    nt = seq // tile

    def body(x_ref, mod_ref, w_ref, wf_ref, b_ref, bf_ref,
             qp_ref, kp_ref, vp_ref, f_ref, p_ref, ga_ref, gp_ref, u_ref, carry_ref):
        i = pl.program_id(0)

        @pl.when(i == 0)
        def _():
            carry_ref[...] = jnp.zeros_like(carry_ref)

        u = x_ref[...] * mod_ref[0:1, :] + mod_ref[1:2, :]
        ub = u.astype(BF16)
        u_ref[...] = ub

        f = _dot_nt(ub, wf_ref[...]) + bf_ref[...]
        f_ref[...] = f
        lane = _lanes((tile, LANE))
        log_f = jnp.where(lane < N_HEADS, jnp.minimum(f, 0.0) - jnp.log(1.0 + jnp.exp(-jnp.abs(f))), 0.0)
        row = lax.broadcasted_iota(jnp.int32, (tile, tile), 0)
        col = lax.broadcasted_iota(jnp.int32, (tile, tile), 1)
        tri = (row >= col).astype(BF16)
        cum = _dot3(tri, log_f) + carry_ref[0:1, :]
        carry_ref[0:1, :] = cum[tile - 1:tile, :]
        cq = [part.astype(F32) for part in _split3(cum * LOG2E)]
        ck = [part.astype(F32) for part in _split3(cum * -LOG2E)]

        def proj(chunk):
            cols = pl.ds(chunk * COL_CHUNK, COL_CHUNK)
            return _dot_nt(ub, w_ref[cols, :]) + b_ref[:, cols]

        def head_tiles(r):
            for pair in range(N_HEADS // 2):
                both = r[:, pair * LANE:(pair + 1) * LANE]
                yield 2 * pair, both
                yield 2 * pair + 1, pltpu.roll(both, HEAD_DIM, 1)

        for h, val in head_tiles(proj(0)):
            extra = jnp.where((lane >= L_CK) & (lane < L_CK + 3), 1.0, 0.0)
            extra = _place3(lane, L_CQ, [part[:, h:h + 1] for part in cq], extra)
            qp_ref[h] = jnp.where(lane < HEAD_DIM, val * (Q_SCALE * LOG2E), extra).astype(BF16)
        for h, val in head_tiles(proj(1)):
            ones = ((lane >= L_CQ) & (lane < L_CQ + 3)) | ((lane >= L_LSE) & (lane < L_LSE + 3))
            extra = _place3(lane, L_CK, [part[:, h:h + 1] for part in ck], jnp.where(ones, 1.0, 0.0))
            kp_ref[h] = jnp.where(lane < HEAD_DIM, val, extra).astype(BF16)
        for h, val in head_tiles(proj(2)):
            extra = jnp.where((lane >= HEAD_DIM) & (lane < HEAD_DIM + 3), -1.0, 0.0)
            vp_ref[h] = jnp.where(lane < HEAD_DIM, val, extra).astype(BF16)
        p_ref[...] = proj(3)
        ga_ref[...] = proj(4)
        gp_ref[...] = proj(5)

    head_block = pl.BlockSpec((N_HEADS, tile, LANE), lambda i: (0, i, 0))
    tok = lambda width: pl.BlockSpec((tile, width), lambda i: (i, 0))
    whole = lambda a: pl.BlockSpec(a.shape, lambda i: (0,) * a.ndim)
    padded = jax.ShapeDtypeStruct((N_HEADS, seq, LANE), BF16)
    half = jax.ShapeDtypeStruct((seq, D_ATT), F32)
    return pl.pallas_call(
        body, name="inproj_forward", grid=(nt,),
        in_specs=[tok(D), whole(mod), whole(w_main), whole(w_f), whole(b_main), whole(b_f)],
        out_specs=[head_block, head_block, head_block, tok(LANE), tok(D_POOL), tok(D_ATT), tok(D_POOL),
                   tok(D)],
        out_shape=[padded, padded, padded, jax.ShapeDtypeStruct((seq, LANE), F32), half, half, half,
                   jax.ShapeDtypeStruct((seq, D), BF16)],
        scratch_shapes=[pltpu.VMEM((8, LANE), F32)],
        compiler_params=_params(("arbitrary",)),
    )(x, mod, w_main, w_f, b_main, b_f)


def _attention_forward(qp, kp, vp, w_out, tile):
    seq = qp.shape[1]
    nb = seq // tile
    steps = (N_HEADS // 2) * nb

    def body(q_ref, k_ref, v_ref, wo_ref, att_ref, q2t_ref, wo_all_ref, s_a, s_b, m_ref, acc_ref,
             send_sems, recv_sems, local_sem):
        step = pl.program_id(0) * nb + pl.program_id(1)
        start, relay_near, relay_far, finish = _gather_stages(wo_ref, wo_all_ref, send_sems, recv_sems, local_sem.at[0])
        pl.when(step == 0)(start)
        pl.when(step == steps // 4)(relay_near)
        pl.when(step == (3 * steps) // 4)(relay_far)

        i = pl.program_id(1)
        sub = lax.broadcasted_iota(jnp.int32, (LANE, tile), 0)
        row = lax.broadcasted_iota(jnp.int32, (tile, tile), 0)
        col = lax.broadcasted_iota(jnp.int32, (tile, tile), 1)
        q = [q_ref[0], q_ref[1]]

        def scores(buf, kb):
            rows = pl.ds(pl.multiple_of(kb * tile, tile), tile)
            for hh in range(2):
                buf[hh] = _dot_nt(k_ref[hh, rows, :], q[hh])

        def absorb(buf, kb, masked):
            rows = pl.ds(pl.multiple_of(kb * tile, tile), tile)
            for hh in range(2):
                m = m_ref[hh, 0:1, :]
                s = buf[hh]
                if masked:
                    s = jnp.where(row <= col, s, -1e30)
                m_new = jnp.maximum(m, jnp.max(s, axis=0, keepdims=True))
                p = jnp.exp2(s - m_new).astype(BF16)
                acc_ref[hh] = jnp.exp2(m - m_new) * acc_ref[hh] + _dot_tn(v_ref[hh, rows, :], p)
                m_ref[hh, 0:1, :] = m_new

        def two_blocks(j, _):
            scores(s_b, 2 * j + 1)
            absorb(s_a, 2 * j, False)
            scores(s_a, 2 * j + 2)
            absorb(s_b, 2 * j + 1, False)
            return 0

        def last_block():
            absorb(s_a, i, True)

        def last_two_blocks():
            scores(s_b, i)
            absorb(s_a, i - 1, False)
            absorb(s_b, i, True)

        scores(s_a, 0)
        m_ref[...] = jnp.full(m_ref.shape, -1e30, F32)
        acc_ref[...] = jnp.zeros_like(acc_ref)
        lax.fori_loop(0, i // 2, two_blocks, 0)
        lax.cond(i % 2 == 0, last_block, last_two_blocks)
        outs = []
        for hh in range(2):
            m, acc = m_ref[hh, 0:1, :], acc_ref[hh]
            l = -acc[HEAD_DIM:HEAD_DIM + 1, :]
            outs.append((acc / l)[:HEAD_DIM, :])
            neg_lse = [part.astype(F32) for part in _split3(-(m + jnp.log2(l)))]
            q2t_ref[hh] = _place3(sub, L_LSE, neg_lse, q[hh].astype(F32).T).astype(BF16)
        att_ref[...] = jnp.concatenate(outs, axis=0).T
        pl.when(step == steps - 1)(finish)

    pair = pl.BlockSpec((2, tile, LANE), lambda hp, i: (hp, i, 0))
    full = pl.BlockSpec((2, seq, LANE), lambda hp, i: (hp, 0, 0))
    return pl.pallas_call(
        body, name="attention_forward", grid=(N_HEADS // 2, nb),
        in_specs=[pair, full, full, ANY],
        out_specs=[pl.BlockSpec((tile, LANE), lambda hp, i: (i, hp)),
                   pl.BlockSpec((2, LANE, tile), lambda hp, i: (hp, 0, i)), ANY],
        out_shape=[jax.ShapeDtypeStruct((seq, D_ATT), F32),
                   jax.ShapeDtypeStruct((N_HEADS, LANE, seq), BF16),
                   jax.ShapeDtypeStruct((N_DEV,) + w_out.shape, w_out.dtype)],
        scratch_shapes=[pltpu.VMEM((2, tile, tile), F32), pltpu.VMEM((2, tile, tile), F32),
                        pltpu.VMEM((2, 8, tile), F32), pltpu.VMEM((2, LANE, tile), F32),
                        pltpu.SemaphoreType.DMA((N_GATHER_SEMS,)), pltpu.SemaphoreType.DMA((N_GATHER_SEMS,)),
                        pltpu.SemaphoreType.DMA((1,))],
        compiler_params=_params(("arbitrary", "arbitrary")),
    )(qp, kp, vp, w_out)


def _window_sum(x, halo, window, transposed):
    tile = x.shape[0]

    def split_cat(a):
        hi = a.astype(BF16)
        return jnp.concatenate([hi, (a - hi.astype(F32)).astype(BF16)], axis=1)

    def fold(r):
        return r[:, :LANE] + r[:, LANE:]

    r = lax.broadcasted_iota(jnp.int32, (tile, tile), 0)
    c = lax.broadcasted_iota(jnp.int32, (tile, tile), 1)
    rh = lax.broadcasted_iota(jnp.int32, (HALO, HALO), 0)
    ch = lax.broadcasted_iota(jnp.int32, (HALO, HALO), 1)
    if not transposed:
        band = (c <= r) & (r - c < window)
        edge = (rh + HALO - ch) < window
    else:
        band = (r <= c) & (c - r < window)
        edge = (HALO + ch - rh) < window
    out = fold(_dot(band.astype(BF16), split_cat(x)))
    reach = fold(_dot(edge.astype(BF16), split_cat(halo)))
    if not transposed:
        return jnp.concatenate([out[:HALO] + reach, out[HALO:]], axis=0)
    return jnp.concatenate([out[:tile - HALO], out[tile - HALO:] + reach], axis=0)


def _silu_parts(g):
    sig = _sigmoid(g)
    return g * sig, sig * (1.0 + g * (1.0 - sig))


def _middle(x, tgt, att, g_att, g_pool, p, vecs, pool_vecs, w_out, w_pool, tile):
    seq = x.shape[0]
    nt = seq // tile
    halo_blocks = tile // HALO

    def body(x_ref, tgt_ref, att_ref, ga_ref, gp_ref, p_ref, ph_ref, vec_ref, pvec_ref, wo_ref, wp_ref,
             dxa_ref, do2_ref, dga_ref, dgp_ref, dpooled_ref, gwo_ref, dwp_ref, dvec_ref, dwo_ref, dpvec_ref):
        i = pl.program_id(0)

        @pl.when(i == 0)
        def _():
            dwo_ref[...] = jnp.zeros_like(dwo_ref)
            dwp_ref[...] = jnp.zeros_like(dwp_ref)
            dvec_ref[...] = jnp.zeros_like(dvec_ref)
            dpvec_ref[...] = jnp.zeros_like(dpvec_ref)

        gate, b_out, ln_g, ln_b = (vec_ref[k:k + 1, :] for k in range(4))
        b_pool, pool_scale = pvec_ref[0:1, :], pvec_ref[1:2, :]
        x = x_ref[...]
        p = p_ref[...]
        p_halo = ph_ref[...] * jnp.where(i > 0, 1.0, 0.0)
        pos = i * tile + lax.broadcasted_iota(jnp.int32, (tile, 1), 0) + 1

        pooled, mixed = [], []
        for g, window in enumerate(POOL_WINDOWS):
            cols = slice(g * GROUP_DIM, (g + 1) * GROUP_DIM)
            wsum = _window_sum(p[:, cols], p_halo[:, cols], window, False)
            count = jnp.minimum(pos, window).astype(F32)
            pooled.append(wsum / count - p[:, cols])
            mixed.append(_dot(pooled[g].astype(BF16), wp_ref[g]) + b_pool[:, cols])
        mixed = jnp.concatenate(mixed, axis=1)
        pool = mixed * pool_scale

        att = att_ref[...]
        g_att, g_pool = ga_ref[...], gp_ref[...]
        silu_a, dsilu_a = _silu_parts(g_att)
        silu_p, dsilu_p = _silu_parts(g_pool)
        y_in = jnp.concatenate([att * silu_a, pool * silu_p], axis=1)
        y = _dot(y_in.astype(BF16), wo_ref[...]) + b_out
        h = ALPHA * x + gate * y
        mu = jnp.mean(h, axis=1, keepdims=True)
        hc = h - mu
        var = jnp.mean(hc * hc, axis=1, keepdims=True)
        rstd = lax.rsqrt(var + LN_EPS)
        yhat = hc * rstd
        diff = yhat * ln_g + ln_b - tgt_ref[...]
        loss_rows = jnp.sum(diff * diff, axis=1, keepdims=True)
        d_out = diff * (1.0 / D)

        d_yhat = d_out * ln_g
        dh = rstd * (d_yhat - jnp.mean(d_yhat, axis=1, keepdims=True)
                     - yhat * jnp.mean(d_yhat * yhat, axis=1, keepdims=True))
        dxa_ref[...] = ALPHA * dh
        dy = dh * gate
        dyb = dy.astype(BF16)
        lane = _lanes((1, D))
        loss_row = jnp.where(lane == 0, (0.5 / D) * jnp.sum(loss_rows, axis=0, keepdims=True), 0.0)
        dvec_ref[5:6, :] += jnp.sum(dh * y, axis=0, keepdims=True)
        dvec_ref[0:1, :] += jnp.sum(dy, axis=0, keepdims=True)
        dvec_ref[1:2, :] += jnp.sum(d_out * yhat, axis=0, keepdims=True)
        dvec_ref[2:3, :] += jnp.sum(d_out, axis=0, keepdims=True)
        dvec_ref[4:5, :] += loss_row

        dwo_ref[...] += _dot(y_in.T.astype(BF16), dyb)
        d_yin = _dot_nt(dyb, wo_ref[...])
        d_a, d_pl = d_yin[:, :D_ATT], d_yin[:, D_ATT:]
        d_att = d_a * silu_a
        d_att_t = d_att.T
        prod_t = (d_att * att).T
        sub = lax.broadcasted_iota(jnp.int32, (HEAD_DIM, tile), 0)
        for h in range(N_HEADS):
            rows = slice(h * HEAD_DIM, (h + 1) * HEAD_DIM)
            delta = jnp.sum(prod_t[rows], axis=0, keepdims=True)
            extra = _place3(sub, 0, [part.astype(F32) for part in _split3(delta)], 0.0)
            do2_ref[h] = jnp.concatenate([d_att_t[rows], extra], axis=0).astype(BF16)
        dga_ref[...] = d_a * att * dsilu_a
        dgp_ref[...] = d_pl * pool * dsilu_p
        d_pool = d_pl * silu_p
        d_mixed = d_pool * pool_scale
        dpvec_ref[0:1, :] += jnp.sum(d_mixed, axis=0, keepdims=True)
        dpvec_ref[1:2, :] += jnp.sum(d_pool * mixed, axis=0, keepdims=True)
        d_pooled = []
        for g in range(len(POOL_WINDOWS)):
            cols = slice(g * GROUP_DIM, (g + 1) * GROUP_DIM)
            dmb = d_mixed[:, cols].astype(BF16)
            dwp_ref[g] += _dot(pooled[g].T.astype(BF16), dmb)
            d_pooled.append(_dot_nt(dmb, wp_ref[g]))
        dpooled_ref[...] = jnp.concatenate(d_pooled, axis=1)

        @pl.when(i == nt - 1)
        def _():
            gwo_ref[...] = dwo_ref[...].astype(BF16)
            dvec_ref[3:4, :] = jnp.concatenate([dpvec_ref[0:1, :], dpvec_ref[1:2, :]], axis=1)

    tok = lambda width: pl.BlockSpec((tile, width), lambda i: (i, 0))
    whole = lambda a: pl.BlockSpec(a.shape, lambda i: (0,) * a.ndim)
    halo = pl.BlockSpec((HALO, D_POOL), lambda i: (jnp.maximum(i * halo_blocks - 1, 0), 0))
    half = jax.ShapeDtypeStruct((seq, D_ATT), F32)
    outs = [jax.ShapeDtypeStruct((seq, D), F32), jax.ShapeDtypeStruct((N_HEADS, LANE, seq), BF16), half, half, half,
            jax.ShapeDtypeStruct(w_out.shape, BF16), jax.ShapeDtypeStruct(w_pool.shape, F32),
            jax.ShapeDtypeStruct(vecs.shape, F32)]
    return pl.pallas_call(
        body, name="middle", grid=(nt,),
        in_specs=[tok(D), tok(D), tok(D_ATT), tok(D_ATT), tok(D_POOL), tok(D_POOL), halo,
                  whole(vecs), whole(pool_vecs), whole(w_out), whole(w_pool)],
        out_specs=[tok(D), pl.BlockSpec((N_HEADS, LANE, tile), lambda i: (0, 0, i)),
                   tok(D_ATT), tok(D_POOL), tok(D_POOL),
                   whole(w_out), whole(w_pool), whole(vecs)],
        out_shape=outs,
        scratch_shapes=[pltpu.VMEM(w_out.shape, F32), pltpu.VMEM(pool_vecs.shape, F32)],
        compiler_params=_params(("arbitrary",)),
    )(x, tgt, att, g_att, g_pool, p, p, vecs, pool_vecs, w_out, w_pool)


def _attention_backward(q2t, kp, vp, do2t, gw_out, vecs, pool, tile):
    seq = kp.shape[1]
    nb = seq // tile
    last = N_HEADS // 2 - 1

    def body(qt_ref, k_ref, v_ref, dot_ref, gwo_hbm, vecs_hbm, pool_hbm,
             dq_ref, dk_ref, dv_ref, dcum_ref, g_out_ref, vecs_sum_ref, pool_sum_ref,
             dq_acc, dk_acc, dv_acc, gwo_ref, vecs_ref, pool_ref,
             r1, s2, r2, v_sib, v_chip, v_recv, p_sib, p_chip, p_recv, send_sems, recv_sems):
        hp = pl.program_id(0)
        start, middle, fold, finish = _reduce_stages(
            [gwo_ref], [g_out_ref], [r1], [s2], [r2],
            [(vecs_ref, vecs_sum_ref, v_sib, v_chip, v_recv), (pool_ref, pool_sum_ref, p_sib, p_chip, p_recv)],
            send_sems, recv_sems)

        @pl.when(hp == 0)
        def _():
            pltpu.sync_copy(gwo_hbm, gwo_ref)
            pltpu.sync_copy(vecs_hbm, vecs_ref)
            pltpu.sync_copy(pool_hbm, pool_ref)
            start()

        pl.when(hp == 1)(middle)
        pl.when(hp == 2)(fold)

        row = lax.broadcasted_iota(jnp.int32, (tile, tile), 0)
        col = lax.broadcasted_iota(jnp.int32, (tile, tile), 1)
        dq_acc[...] = jnp.zeros_like(dq_acc)

        def kv_block(kb, _):
            krows = pl.ds(pl.multiple_of(kb * tile, tile), tile)
            k = [k_ref[hh, krows, :] for hh in range(2)]
            v = [v_ref[hh, krows, :] for hh in range(2)]
            k_t = [k[hh].T for hh in range(2)]

            def q_block(qb, masked):
                qcols = pl.ds(pl.multiple_of(qb * tile, tile), tile)
                for hh in range(2):
                    q_t = qt_ref[hh, :, qcols]
                    do_t = dot_ref[hh, :, qcols]
                    s_t = _dot(k[hh], q_t)
                    if masked:
                        s_t = jnp.where(row <= col, s_t, -1e30)
                    p_t = jnp.exp2(s_t)
                    ds_t = (p_t * _dot(v[hh], do_t)).astype(BF16)
                    dv_new = _dot_nt(do_t, p_t.astype(BF16))
                    dk_new = _dot_nt(q_t, ds_t)
                    if masked:
                        dv_acc[hh], dk_acc[hh] = dv_new, dk_new
                    else:
                        dv_acc[hh] += dv_new
                        dk_acc[hh] += dk_new
                    dq_acc[hh, :, qcols] += _dot(k_t[hh], ds_t)

            q_block(kb, True)

            def two_later_blocks(j, _):
                q_block(kb + 1 + 2 * j, False)
                q_block(kb + 2 + 2 * j, False)
                return 0

            later = nb - 1 - kb
            lax.fori_loop(0, later // 2, two_later_blocks, 0)
            pl.when(later % 2 == 1)(lambda: q_block(nb - 1, False))
            for hh in range(2):
                dk = dk_acc[hh]
                dk_ref[hh, :, krows] = (dk * LN2).astype(BF16)
                dv_ref[hh, :, krows] = dv_acc[hh].astype(BF16)
                dcum_ref[hh, :, krows] = -dk[L_CK:L_CK + 1, :]
            return 0

        lax.fori_loop(0, nb, kv_block, 0)
        for hh in range(2):
            dq = dq_acc[hh]
            dcum_ref[hh] += dq[L_CQ:L_CQ + 1, :]
            dq_ref[hh] = (dq * Q_SCALE).astype(BF16)
        pl.when(hp == last)(finish)

    pair = pl.BlockSpec((2, seq, LANE), lambda hp: (hp, 0, 0))
    pair_t = pl.BlockSpec((2, LANE, seq), lambda hp: (hp, 0, 0))
    whole = lambda shape: pl.BlockSpec(shape, lambda hp: (0,) * len(shape))
    grad = jax.ShapeDtypeStruct((N_HEADS, LANE, seq), BF16)
    return pl.pallas_call(
        body, name="attention_backward", grid=(N_HEADS // 2,),
        in_specs=[pair_t, pair, pair, pair_t, ANY, ANY, ANY],
        out_specs=[pair_t, pair_t, pair_t, pl.BlockSpec((2, 1, seq), lambda hp: (hp, 0, 0)),
                   whole(gw_out.shape[1:]), whole(vecs.shape), whole(pool.shape)],
        out_shape=[grad, grad, grad, jax.ShapeDtypeStruct((N_HEADS, 1, seq), F32),
                   jax.ShapeDtypeStruct(gw_out.shape[1:], F32), jax.ShapeDtypeStruct(vecs.shape, F32),
                   jax.ShapeDtypeStruct(pool.shape, F32)],
        scratch_shapes=[pltpu.VMEM((2, LANE, seq), F32), pltpu.VMEM((2, LANE, tile), F32),
                        pltpu.VMEM((2, LANE, tile), F32), pltpu.VMEM(gw_out.shape, BF16),
                        pltpu.VMEM(vecs.shape, F32), pltpu.VMEM(pool.shape, F32)]
        + _reduce_scratch(gw_out, [vecs, pool]),
        compiler_params=_params(("arbitrary",)),
    )(q2t, kp, vp, do2t, gw_out, vecs, pool)


def _inproj_backward(dqp, dkp, dvp, d_cum, f, d_pooled, d_ga, d_gp, x, dxa, u, mod, w_main, w_f, tile):
    seq = x.shape[0]
    nt = seq // tile
    halo_blocks = tile // HALO

    def body(dq_ref, dk_ref, dv_ref, dcum_ref, f_ref, dpo_ref, dph_ref, dga_ref, dgp_ref, x_ref, dxa_ref, u_ref,
             mod_ref, w_ref, wf_ref,
             dx_ref, dproj_ref, dwf_ref, db_ref, dbf_ref, dmod_ref, carry_ref):
        step = pl.program_id(0)
        i = nt - 1 - step

        @pl.when(step == 0)
        def _():
            carry_ref[...] = jnp.zeros_like(carry_ref)
            dwf_ref[...] = jnp.zeros_like(dwf_ref)
            db_ref[...] = jnp.zeros_like(db_ref)
            dbf_ref[...] = jnp.zeros_like(dbf_ref)
            dmod_ref[...] = jnp.zeros_like(dmod_ref)

        ones = jnp.ones((8, tile), BF16)

        def emit(chunk, val):
            cols = pl.ds(chunk * COL_CHUNK, COL_CHUNK)
            db_ref[0:1, cols] += jnp.sum(val, axis=0, keepdims=True)
            vb = val.astype(BF16)
            dproj_ref[:, pl.ds((chunk - 3) * COL_CHUNK, COL_CHUNK)] = vb
            return _dot(vb, w_ref[cols, :])

        d_u = jnp.zeros((tile, D), F32)
        for chunk, ref in enumerate((dq_ref, dk_ref, dv_ref)):
            cols = pl.ds(chunk * COL_CHUNK, COL_CHUNK)
            val_t = ref[:, 0:HEAD_DIM, :].reshape(COL_CHUNK, tile)
            db_ref[:, cols] += _dot_nt(ones, val_t)
            d_u += _dot_tn(val_t, w_ref[cols, :])

        d_pooled = dpo_ref[...]
        d_halo = dph_ref[...] * jnp.where(i < nt - 1, 1.0, 0.0)
        pos = i * tile + lax.broadcasted_iota(jnp.int32, (tile, 1), 0) + 1
        d_p = []
        for g, window in enumerate(POOL_WINDOWS):
            cols = slice(g * GROUP_DIM, (g + 1) * GROUP_DIM)
            scaled = d_pooled[:, cols] / jnp.minimum(pos, window).astype(F32)
            d_p.append(_window_sum(scaled, d_halo[:, cols] * (1.0 / window), window, True) - d_pooled[:, cols])
        d_u += emit(3, jnp.concatenate(d_p, axis=1))
        d_u += emit(4, dga_ref[...])
        d_u += emit(5, dgp_ref[...])

        row = lax.broadcasted_iota(jnp.int32, (tile, tile), 0)
        col = lax.broadcasted_iota(jnp.int32, (tile, tile), 1)
        later = (row >= col).astype(BF16)
        d_logf = sum(_dot(part, later) for part in _split3(dcum_ref[:, 0, :])) + carry_ref[:, 0:1]
        carry_ref[:, 0:1] = d_logf[:, 0:1]
        d_f = d_logf * _sigmoid(-f_ref[...].T[0:N_HEADS, :])
        d_f = jnp.concatenate([d_f, jnp.zeros((LANE - N_HEADS, tile), F32)], axis=0)
        dbf_ref[...] += sum(_dot_nt(ones, part) for part in _split3(d_f))
        d_fb = d_f.astype(BF16)
        d_u += _dot_tn(d_fb, wf_ref[...])
        dwf_ref[...] += _dot(d_fb, u_ref[...])

        x = x_ref[...]
        dx_ref[...] = dxa_ref[...] + d_u * mod_ref[0:1, :]
        dmod_ref[0:1, :] += jnp.sum(d_u * x, axis=0, keepdims=True)
        dmod_ref[1:2, :] += jnp.sum(d_u, axis=0, keepdims=True)

    rev = lambda step: nt - 1 - step
    tok = lambda width: pl.BlockSpec((tile, width), lambda s: (rev(s), 0))
    head_block = pl.BlockSpec((N_HEADS, LANE, tile), lambda s: (0, 0, rev(s)))
    whole = lambda a: pl.BlockSpec(a.shape, lambda s: (0,) * a.ndim)
    halo = pl.BlockSpec((HALO, D_POOL), lambda s: (jnp.minimum((rev(s) + 1) * halo_blocks, seq // HALO - 1), 0))
    small = lambda width: jax.ShapeDtypeStruct((8, width), F32)
    n_rest = N_MAIN - OFF_P
    return pl.pallas_call(
        body, name="inproj_backward", grid=(nt,),
        in_specs=[head_block, head_block, head_block, pl.BlockSpec((N_HEADS, 1, tile), lambda s: (0, 0, rev(s))),
                  tok(LANE), tok(D_POOL), halo, tok(D_ATT), tok(D_POOL),
                  tok(D), tok(D), tok(D),
                  whole(mod), whole(w_main), whole(w_f)],
        out_specs=[tok(D), tok(n_rest), pl.BlockSpec((LANE, D), lambda s: (0, 0)),
                   pl.BlockSpec((8, N_MAIN), lambda s: (0, 0)), pl.BlockSpec((8, LANE), lambda s: (0, 0)),
                   pl.BlockSpec((8, D), lambda s: (0, 0))],
        out_shape=[jax.ShapeDtypeStruct((seq, D), F32), jax.ShapeDtypeStruct((seq, n_rest), BF16),
                   jax.ShapeDtypeStruct((LANE, D), F32), small(N_MAIN), small(LANE), small(D)],
        scratch_shapes=[pltpu.VMEM((8, LANE), F32)],
        compiler_params=_params(("arbitrary",)),
    )(dqp, dkp, dvp, d_cum, f, d_pooled, d_pooled, d_ga, d_gp, x, dxa, u, mod, w_main, w_f)


def _weight_grad(dproj, u, k_tile):
    seq, n_cols = dproj.shape
    nk = seq // k_tile

    def body(dp_ref, u_ref, out_ref, acc_ref):
        k = pl.program_id(1)

        @pl.when(k == 0)
        def _():
            acc_ref[...] = jnp.zeros_like(acc_ref)

        acc_ref[...] += _dot_tn(dp_ref[...], u_ref[...])

        @pl.when(k == nk - 1)
        def _():
            out_ref[...] = acc_ref[...].astype(BF16)

    return pl.pallas_call(
        body, name="weight_grad", grid=(n_cols // COL_CHUNK, nk),
        in_specs=[pl.BlockSpec((k_tile, COL_CHUNK), lambda n, k: (k, n)),
                  pl.BlockSpec((k_tile, D), lambda n, k: (k, 0))],
        out_specs=pl.BlockSpec((COL_CHUNK, D), lambda n, k: (n, 0)),
        out_shape=jax.ShapeDtypeStruct((n_cols, D), BF16),
        scratch_shapes=[pltpu.VMEM((COL_CHUNK, D), F32)],
        compiler_params=_params(("arbitrary", "arbitrary")),
    )(dproj, u)


def _weight_grad_heads(grad_t, u, k_tile, name):
    seq = u.shape[0]
    nk = seq // k_tile
    rows = N_HEADS * HEAD_DIM

    def body(g_ref, u_ref, out_ref, acc_ref):
        k = pl.program_id(0)

        @pl.when(k == 0)
        def _():
            acc_ref[...] = jnp.zeros_like(acc_ref)

        acc_ref[...] += _dot(g_ref[...].reshape(rows, k_tile), u_ref[...])

        @pl.when(k == nk - 1)
        def _():
            out_ref[...] = acc_ref[...].astype(BF16)

    return pl.pallas_call(
        body, name=name, grid=(nk,),
        in_specs=[pl.BlockSpec((N_HEADS, HEAD_DIM, k_tile), lambda k: (0, 0, k)),
                  pl.BlockSpec((k_tile, D), lambda k: (k, 0))],
        out_specs=pl.BlockSpec((rows, D), lambda k: (0, 0)),
        out_shape=jax.ShapeDtypeStruct((rows, D), BF16),
        scratch_shapes=[pltpu.VMEM((rows, D), F32)],
        compiler_params=_params(("arbitrary",)),
    )(grad_t, u)


def _adamw(w, g, m, v):
    m = ADAM_B1 * m + (1.0 - ADAM_B1) * g
    v = ADAM_B2 * v + (1.0 - ADAM_B2) * (g * g)
    m_hat = m / (1.0 - ADAM_B1 ** ADAM_STEP)
    v_hat = v / (1.0 - ADAM_B2 ** ADAM_STEP)
    delta = -ADAM_LR * (m_hat / (jnp.sqrt(v_hat) + ADAM_EPS) + ADAM_WD * w)
    return delta, m, v


def _adamw_call(g, w, m, v, lead_tile, name):
    nr = w.shape[0] // lead_tile

    def body(gi_ref, w_ref, m_ref, v_ref, g_ref, d_ref, nm_ref, nv_ref):
        g = gi_ref[...]
        g_ref[...] = g
        d_ref[...], nm_ref[...], nv_ref[...] = _adamw(w_ref[...], g, m_ref[...], v_ref[...])

    blk = pl.BlockSpec((lead_tile,) + w.shape[1:], lambda r: (r,) + (0,) * (w.ndim - 1))
    shape = jax.ShapeDtypeStruct(w.shape, F32)
    return pl.pallas_call(
        body, name=name, grid=(nr,),
        in_specs=[blk, blk, blk, blk], out_specs=[blk, blk, blk, blk],
        out_shape=[shape, shape, shape, shape],
        compiler_params=_params(("arbitrary",)),
    )(g, w, m, v)


def _sum_adamw(parts, w, m, v, row_tile, name):
    rows, cols = w.shape
    nr = rows // row_tile

    def body(parts_ref, w_ref, m_ref, v_ref, g_ref, d_ref, nm_ref, nv_ref):
        g = parts_ref[0]
        for k in range(1, N_DEV):
            g = g + parts_ref[k]
        g_ref[...] = g
        d_ref[...], nm_ref[...], nv_ref[...] = _adamw(w_ref[...], g, m_ref[...], v_ref[...])

    blk = pl.BlockSpec((row_tile, cols), lambda r: (r, 0))
    shape = jax.ShapeDtypeStruct(w.shape, F32)
    return pl.pallas_call(
        body, name=name, grid=(nr,),
        in_specs=[pl.BlockSpec((N_DEV, row_tile, cols), lambda r: (0, r, 0)), blk, blk, blk],
        out_specs=[blk, blk, blk, blk],
        out_shape=[shape, shape, shape, shape],
        compiler_params=_params(("arbitrary",)),
    )(parts, w, m, v)


def _ada_adamw(sc_t, d_ada, w, m, v):
    def body(sc_ref, d_ref, w_ref, m_ref, v_ref, g_ref, dl_ref, nm_ref, nv_ref):
        g = sc_ref[:, 0:1] * d_ref[0:1, :]
        for b in range(1, N_DEV):
            g = g + sc_ref[:, b:b + 1] * d_ref[b:b + 1, :]
        g_ref[...] = g
        dl_ref[...], nm_ref[...], nv_ref[...] = _adamw(w_ref[...], g, m_ref[...], v_ref[...])

    row_tile = 256
    blk = pl.BlockSpec((row_tile, w.shape[1]), lambda r: (r, 0))
    shape = jax.ShapeDtypeStruct(w.shape, F32)
    return pl.pallas_call(
        body, name="ada_adamw", grid=(w.shape[0] // row_tile,),
        in_specs=[pl.BlockSpec((row_tile, N_DEV), lambda r: (r, 0)), pl.BlockSpec(d_ada.shape, lambda r: (0, 0)),
                  blk, blk, blk],
        out_specs=[blk, blk, blk, blk], out_shape=[shape, shape, shape, shape],
        compiler_params=_params(("arbitrary",)),
    )(sc_t, d_ada, w, m, v)


F_LO, F_HI = 3 * D_ATT, 3 * D_ATT + N_HEADS


def _split_forget(a, axis):
    idx = lambda lo, hi: tuple(slice(lo, hi) if d == axis else slice(None) for d in range(a.ndim))
    pad = [(0, LANE - N_HEADS) if d == axis else (0, 0) for d in range(a.ndim)]
    return jnp.concatenate([a[idx(0, F_LO)], a[idx(F_HI, D_IN)]], axis=axis), jnp.pad(a[idx(F_LO, F_HI)], pad)


def _join_forget(main, f, axis):
    idx = lambda lo, hi: tuple(slice(lo, hi) if d == axis else slice(None) for d in range(main.ndim))
    return jnp.concatenate([main[idx(0, F_LO)], f[idx(0, N_HEADS)], main[idx(F_LO, N_MAIN)]], axis=axis)


def _adamw_rows(grad_rows, params):
    n = len(params)

    def body(g_ref, *refs):
        ins, outs = refs[:3 * n], refs[3 * n:]
        for i, (row, lo, hi, _, _, _) in enumerate(params):
            g = g_ref[row:row + 1, lo:hi]
            outs[4 * i][...] = g
            outs[4 * i + 1][...], outs[4 * i + 2][...], outs[4 * i + 3][...] = _adamw(
                ins[3 * i][...], g, ins[3 * i + 1][...], ins[3 * i + 2][...])

    flat = pl.pallas_call(
        body, name="adamw_rows",
        out_shape=[jax.ShapeDtypeStruct(w.shape, F32) for _, _, _, w, _, _ in params for _ in range(4)],
        compiler_params=_params(),
    )(grad_rows, *[a for _, _, _, w, m, v in params for a in (w, m, v)])
    return [flat[4 * i:4 * i + 4] for i in range(n)]


def kernel(x, c, w_ada, b_ada, w_in, b_in, w_pool_mix, b_pool_mix, pool_scale, w_out, b_out, ln_g, ln_b, loss_target, m_w_ada, m_b_ada, m_w_in, m_b_in, m_w_pool_mix, m_b_pool_mix, m_pool_scale, m_w_out, m_b_out, m_ln_g, m_ln_b, v_w_ada, v_b_ada, v_w_in, v_b_in, v_w_pool_mix, v_b_pool_mix, v_pool_scale, v_w_out, v_b_out, v_ln_g, v_ln_b):
    seq = x.shape[1]
    tile = min(256, seq)
    attn_tile = min(512, max(128, seq // 4))
    me = _dev_index(*_mesh_pos())
    x2, tgt = x[0], loss_target[0]

    rows_in = D_IN // N_DEV
    w_in_g, sc_all, ada_mine = _gather_and_ada(c, w_in[0].T.astype(BF16), w_ada[0])
    ada = ada_mine.reshape(1, D_ADA) + b_ada
    shift, scale, gate = ada[:, 0:D], ada[:, D:2 * D], ada[:, 2 * D:]
    mod = jnp.concatenate([1.0 + scale, shift, jnp.zeros((6, D), F32)], axis=0)

    w_main, w_f = _split_forget(w_in_g.reshape(D_IN, D), 0)
    b_main, b_f = _split_forget(b_in, 1)

    qp, kp, vp, f, p, g_att, g_pool, u = _inproj_forward(x2, mod, w_main, w_f, b_main, b_f, tile)
    att, q2t, w_out_g = _attention_forward(qp, kp, vp, w_out[0].astype(BF16), attn_tile)

    vecs = jnp.concatenate([gate, b_out, ln_g, ln_b, jnp.zeros((4, D), F32)], axis=0)
    pool_vecs = jnp.concatenate([b_pool_mix.reshape(1, D_POOL), pool_scale, jnp.zeros((6, D_POOL), F32)], axis=0)
    dxa, do2, d_ga, d_gp, d_pooled, gw_out, dw_pool, dvec = _middle(
        x2, tgt, att, g_att, g_pool, p, vecs, pool_vecs, w_out_g.reshape(D, D), w_pool_mix[0].astype(BF16), tile)

    pool_rows = w_pool_mix.shape[1] * GROUP_DIM
    dqp, dkp, dvp, d_cum, g_out, dvec_sum, dw_pool_sum = _attention_backward(
        q2t, kp, vp, do2, gw_out.reshape(N_DEV, D // N_DEV, D), dvec, dw_pool.reshape(pool_rows, GROUP_DIM), attn_tile)
    dx, dproj, dw_f, db_main, db_f, dmod = _inproj_backward(
        dqp, dkp, dvp, d_cum, f, d_pooled, d_ga, d_gp, x2, dxa, u, mod, w_main, w_f, tile)
    k_tile = min(1024, seq)
    dw_q, dw_k, dw_v = (_weight_grad_heads(g, u, k_tile, "weight_grad_" + n)
                        for g, n in ((dqp, "q"), (dkp, "k"), (dvp, "v")))
    dw_rest = _weight_grad(dproj, u, k_tile)

    dw_main = jnp.concatenate([dw_q, dw_k, dw_v, dw_rest], axis=0)
    gw_in = _join_forget(dw_main, dw_f.astype(BF16), 0).reshape(N_DEV, rows_in, D)
    d_ada = jnp.concatenate([dmod[1:2], dmod[0:1], dvec[5:6]], axis=1)
    g_in_rows, g_b_in, d_ada_all = _reduce_grads(gw_in, _join_forget(db_main[0:1], db_f[0:1], 1), d_ada)

    def rows3(a):
        return a[0].T.reshape(rows_in, D // LANE, LANE)

    outs_in = _adamw_call(g_in_rows.reshape(rows_in, D // LANE, LANE), rows3(w_in), rows3(m_w_in), rows3(v_w_in),
                          rows_in // 5, "adamw_w_in")
    g_w_in, d_w_in, nm_w_in, nv_w_in = (a.reshape(rows_in, D).T for a in outs_in)
    g_w_out, d_w_out, nm_w_out, nv_w_out = _adamw_call(g_out, w_out[0], m_w_out[0], v_w_out[0], D // N_DEV, "adamw_w_out")

    flat_pool = lambda a: a.reshape(1, D_POOL)
    rows = _adamw_rows(dvec_sum, [
        (0, 0, D, b_out, m_b_out, v_b_out), (1, 0, D, ln_g, m_ln_g, v_ln_g), (2, 0, D, ln_b, m_ln_b, v_ln_b),
        (3, 0, D_POOL, flat_pool(b_pool_mix), flat_pool(m_b_pool_mix), flat_pool(v_b_pool_mix)),
        (3, D_POOL, 2 * D_POOL, pool_scale, m_pool_scale, v_pool_scale)])
    pools = _adamw_call(dw_pool_sum, *(a.reshape(pool_rows, GROUP_DIM) for a in (w_pool_mix, m_w_pool_mix, v_w_pool_mix)),
                        pool_rows, "adamw_w_pool")
    b_ins = _adamw_call(g_b_in, b_in, m_b_in, v_b_in, 1, "adamw_b_in")
    small = {"b_out": rows[0], "ln_g": rows[1], "ln_b": rows[2],
             "b_pool": [a.reshape(b_pool_mix.shape) for a in rows[3]], "pool_scale": rows[4],
             "w_pool": [a.reshape(w_pool_mix.shape) for a in pools], "b_in": b_ins}
    g_s, d_s, nm_s, nv_s = ({k: r[j] for k, r in small.items()} for j in range(4))
    loss = dvec_sum[4, 0]

    g_b_ada, d_b_ada, nm_b_ada, nv_b_ada = _sum_adamw(d_ada_all, b_ada, m_b_ada, v_b_ada, 1, "adamw_b_ada")
    d_ada_local = lax.dynamic_slice_in_dim(d_ada_all.reshape(N_DEV, D_ADA), me * (D_ADA // N_DEV), D_ADA // N_DEV, axis=1)
    g_w_ada, d_w_ada, nm_w_ada, nv_w_ada = _ada_adamw(sc_all.T, d_ada_local, w_ada[0], m_w_ada[0], v_w_ada[0])

    def ordered(w_ada_, b_ada_, w_in_, w_out_, s):
        return (w_ada_[None], b_ada_, w_in_[None], s["b_in"], s["w_pool"], s["b_pool"], s["pool_scale"],
                w_out_[None], s["b_out"], s["ln_g"], s["ln_b"])

    return (loss, dx[None],
            *ordered(g_w_ada, g_b_ada, g_w_in, g_w_out, g_s),
            *ordered(d_w_ada, d_b_ada, d_w_in, d_w_out, d_s),
            *ordered(nm_w_ada, nm_b_ada, nm_w_in, nm_w_out, nm_s),
            *ordered(nv_w_ada, nv_b_ada, nv_w_in, nv_w_out, nv_s))
```

```python
import jax
import jax.numpy as jnp
from jax import lax
from jax.experimental import pallas as pl
from jax.experimental.pallas import tpu as pltpu

F32 = jnp.float32
BF16 = jnp.bfloat16

N_DEV = 8
D = 1024
N_HEADS = 8
HEAD_DIM = 64
D_ATT = 512
D_POOL = 512
POOL_WINDOWS = (2, 4, 8, 16)
GROUP_DIM = 128
HALO = 16
LANE = 128
D_IN = 3080
D_ADA = 3072
N_MAIN = 3072
OFF_P = 1536
COL_CHUNK = 512
Q_SCALE = 0.125
LN_EPS = 1e-5
ALPHA = 2.0 ** 0.25
L_CQ, L_CK, L_LSE = 64, 67, 70

ADAM_LR, ADAM_B1, ADAM_B2, ADAM_EPS, ADAM_WD, ADAM_STEP = 0.001, 0.9, 0.999, 1e-08, 0.01, 10
VMEM_LIMIT = 56 * 1024 * 1024

MESH = pl.DeviceIdType.MESH
ANY = pl.BlockSpec(memory_space=pl.ANY)


def _params(sem=None, vmem=VMEM_LIMIT):
    return pltpu.CompilerParams(dimension_semantics=sem, vmem_limit_bytes=vmem)


def _split3(a):
    hi = a.astype(BF16)
    r = a - hi.astype(F32)
    mid = r.astype(BF16)
    lo = (r - mid.astype(F32)).astype(BF16)
    return hi, mid, lo


def _dot(a, b):
    return jnp.dot(a, b, preferred_element_type=F32)


def _dot_nt(a, b):
    return lax.dot_general(a, b, (((1,), (1,)), ((), ())), preferred_element_type=F32)


def _dot_tn(a, b):
    return lax.dot_general(a, b, (((0,), (0,)), ((), ())), preferred_element_type=F32)


def _dot3(m01, a):
    hi, mid, lo = _split3(a)
    return _dot(m01, hi) + _dot(m01, mid) + _dot(m01, lo)


def _sigmoid(z):
    return 1.0 / (1.0 + jnp.exp(-z))


def _lanes(shape):
    return lax.broadcasted_iota(jnp.int32, shape, len(shape) - 1)


def _place3(lane, base, parts, other):
    out = other
    for j in range(3):
        out = jnp.where(lane == base + j, parts[j], out)
    return out


def _mesh_pos():
    return lax.axis_index("x"), lax.axis_index("y"), lax.axis_index("c")


def _dev_index(px, py, pc):
    return 4 * px + 2 * py + pc


N_GATHER_SEMS = 9


def _gather_stages(src_ref, out_ref, send_sems, recv_sems, local_sem):
    x, y, c = _mesh_pos()
    me, sibling = (x, y, c), (x, y, 1 - c)
    nbr_x, nbr_y, diag = (1 - x, y), (x, 1 - y), (1 - x, 1 - y)
    half = out_ref.shape[-1] // 2
    left, right = pl.ds(0, half), pl.ds(half, half)

    def copy(k, block, to, cols=None, src=None):
        slot = out_ref.at[_dev_index(*block)]
        if cols is not None:
            slot = slot.at[:, cols]
        return pltpu.make_async_remote_copy(
            src_ref=slot if src is None else src, dst_ref=slot, send_sem=send_sems.at[k], recv_sem=recv_sems.at[k],
            device_id=to, device_id_type=MESH)

    mine = pltpu.make_async_copy(src_ref, out_ref.at[_dev_index(*me)], local_sem)
    first = [copy(0, me, sibling, src=src_ref), copy(1, me, (*nbr_x, c), src=src_ref), copy(2, me, (*nbr_y, c), src=src_ref)]
    relay = [(1, nbr_x, None, nbr_x), (2, nbr_y, None, nbr_y), (3, diag, left, nbr_y), (4, diag, right, nbr_x)]
    onward = [copy(3, (*nbr_x, c), (*nbr_y, c), cols=left), copy(4, (*nbr_y, c), (*nbr_x, c), cols=right)]
    passed = [copy(4 + k, (*block, c), sibling, cols=cols) for k, block, cols, _ in relay]

    def start():
        mine.start()
        for cp in first:
            cp.start()

    def relay_stage(first_item):
        def run():
            for j in (first_item, first_item + 1):
                k, block, cols, frm = relay[j]
                copy(k, (*block, c), (*frm, c), cols=cols).wait_recv()
                if j < 2:
                    onward[j].start()
                passed[j].start()
        return run

    def finish():
        copy(0, sibling, me).wait_recv()
        for k, block, cols, _ in relay:
            copy(4 + k, (*block, 1 - c), me, cols=cols).wait_recv()
        for cp in first + onward + passed:
            cp.wait_send()
        mine.wait()

    return start, relay_stage(0), relay_stage(2), finish


N_REDUCE_SEMS = 10
N_SMALL_SEMS = 4
N_ROWS_SEMS = 7


def _reduce_stages(ins, gs, r1, s2, r2, smalls, send_sems, recv_sems, rows=None, own=None):
    n = len(ins)
    x, y, c = _mesh_pos()
    me = _dev_index(x, y, c)
    sibling = (x, y, 1 - c)
    chips = [(x, y), (1 - x, y), (x, 1 - y), (1 - x, 1 - y)]
    peers = []
    for p in range(1, N_DEV):
        px, py, pc = (p >> 2) & 1, (p >> 1) & 1, p & 1
        peers.append((1 - x if px else x, 1 - y if py else y, 1 - c if pc else c))
    base_small = N_REDUCE_SEMS * n

    def remote(src, dst, k, to):
        return pltpu.make_async_remote_copy(src_ref=src, dst_ref=dst, send_sem=send_sems.at[k],
                                            recv_sem=recv_sems.at[k], device_id=to, device_id_type=MESH)

    def level1(a, q):
        return remote(ins[a].at[_dev_index(*chips[q], 1 - c)], r1[a].at[q], N_REDUCE_SEMS * a + q, sibling)

    def level2(a, k):
        half = ins[a].shape[-1] // 2
        left, right = pl.ds(0, half), pl.ds(half, half)
        nbr_x, nbr_y = (*chips[1], c), (*chips[2], c)
        src_slot, dst_slot, cols, to = [(0, 0, left, nbr_x), (1, 1, right, nbr_y), (2, 2, left, nbr_x),
                                        (2, 2, right, nbr_y), (0, 0, right, nbr_x), (1, 1, left, nbr_y)][k]
        return remote(s2[a].at[src_slot, :, cols], r2[a].at[dst_slot, :, cols], N_REDUCE_SEMS * a + 4 + k, to)

    to_sibling = [remote(sm[0], sm[2], base_small + 4 * i, sibling) for i, sm in enumerate(smalls)]
    to_chips = [[remote(sm[3], sm[4].at[j], base_small + 4 * i + 1 + j, (*chips[j + 1], c)) for j in range(3)]
                for i, sm in enumerate(smalls)]
    if rows is not None:
        rows_ref, land_ref, all_ref = rows
        base_rows = base_small + 4 * len(smalls)
        row_sends = [remote(rows_ref, land_ref.at[me], base_rows + k, to) for k, to in enumerate(peers)]

    def mine(a, q):
        buf, sems = own[a]
        return pltpu.make_async_copy(ins[a].at[_dev_index(*chips[q], c)], buf.at[q], sems.at[q])

    def start():
        for a in range(n):
            for q in range(4):
                level1(a, q).start()
            if own is not None:
                for q in (1, 2, 3, 0):
                    mine(a, q).start()
        for cp in to_sibling:
            cp.start()
        if rows is not None:
            for cp in row_sends:
                cp.start()
            land_ref[me] = rows_ref[...]

    def middle():
        for a in range(n):
            for q in (1, 2, 3, 0):
                level1(a, q).wait_recv()
                if own is None:
                    kept = ins[a][_dev_index(*chips[q], c)]
                else:
                    mine(a, q).wait()
                    kept = own[a][0][q]
                pair = kept.astype(F32) + r1[a][q].astype(F32)
                if q == 0:
                    gs[a][...] = pair
                else:
                    s2[a][q - 1] = pair.astype(BF16)
                    for k in ((0,), (1,), (2, 3))[q - 1]:
                        level2(a, k).start()
        for i, (small_ref, _, sm_sib, sm_chip, _) in enumerate(smalls):
            to_sibling[i].wait_recv()
            sm_chip[...] = small_ref[...] + sm_sib[...]
            for cp in to_chips[i]:
                cp.start()

    def fold():
        for a in range(n):
            half = ins[a].shape[-1] // 2
            level2(a, 3).wait_recv()
            s2[a][0, :, half:] = (s2[a][0, :, half:].astype(F32) + r2[a][2, :, half:].astype(F32)).astype(BF16)
            level2(a, 4).start()
            level2(a, 2).wait_recv()
            s2[a][1, :, :half] = (s2[a][1, :, :half].astype(F32) + r2[a][2, :, :half].astype(F32)).astype(BF16)
            level2(a, 5).start()

    def finish():
        for a in range(n):
            for k in (0, 1, 4, 5):
                level2(a, k).wait_recv()
            gs[a][...] = gs[a][...] + r2[a][0].astype(F32) + r2[a][1].astype(F32)
            for q in range(4):
                level1(a, q).wait_send()
            for k in range(6):
                level2(a, k).wait_send()
        for i, (_, total_ref, _, sm_chip, sm_recv) in enumerate(smalls):
            for cp in to_chips[i]:
                cp.wait_recv()
            total = None
            for ax in range(2):
                for ay in range(2):
                    dx, dy = x != ax, y != ay
                    term = jnp.where(dx, jnp.where(dy, sm_recv[2], sm_recv[0]), jnp.where(dy, sm_recv[1], sm_chip[...]))
                    total = term if total is None else total + term
            total_ref[...] = total
            for cp in [to_sibling[i]] + to_chips[i]:
                cp.wait_send()
        if rows is not None:
            for k, frm in enumerate(peers):
                remote(rows_ref, land_ref.at[_dev_index(*frm)], base_rows + k, frm).wait_recv()
            all_ref[...] = land_ref[...]
            for cp in row_sends:
                cp.wait_send()

    return start, middle, fold, finish


def _reduce_scratch(shard, smalls, rows=None):
    out = [pltpu.VMEM((lead,) + shard.shape[1:], BF16) for lead in (4, 3, 3)]
    for small in smalls:
        out += [pltpu.VMEM(small.shape, F32), pltpu.VMEM(small.shape, F32), pltpu.VMEM((3,) + small.shape, F32)]
    n_sems = N_REDUCE_SEMS + N_SMALL_SEMS * len(smalls)
    if rows is not None:
        out.append(pltpu.VMEM((N_DEV,) + rows.shape, F32))
        n_sems += N_ROWS_SEMS
    return out + [pltpu.SemaphoreType.DMA((n_sems,))] * 2


def _reduce_grads(gw_in, small, rows):
    def body(in_ref, small_ref, rows_ref, g_ref, total_ref, rows_all_ref,
             r1, s2, r2, sm_sib, sm_chip, sm_recv, rows_land, send_sems, recv_sems, kept, kept_sems):
        stages = _reduce_stages(
            [in_ref], [g_ref], [r1], [s2], [r2], [(small_ref, total_ref, sm_sib, sm_chip, sm_recv)],
            send_sems, recv_sems, rows=(rows_ref, rows_land, rows_all_ref), own=[(kept, kept_sems)])
        for stage in stages:
            stage()

    vmem = pl.BlockSpec(memory_space=pltpu.VMEM)
    return pl.pallas_call(
        body, name="reduce_grads",
        in_specs=[ANY, vmem, vmem], out_specs=[vmem, vmem, vmem],
        out_shape=[jax.ShapeDtypeStruct(gw_in.shape[1:], F32), jax.ShapeDtypeStruct(small.shape, F32),
                   jax.ShapeDtypeStruct((N_DEV,) + rows.shape, F32)],
        scratch_shapes=_reduce_scratch(gw_in, [small], rows)
        + [pltpu.VMEM((4,) + gw_in.shape[1:], BF16), pltpu.SemaphoreType.DMA((4,))],
        compiler_params=_params(),
    )(gw_in, small, rows)


def _dot3_rhs(a, b):
    a0, a1, a2 = _split3(a)
    b0, b1, b2 = _split3(b)
    return (_dot(a0, b0) + (_dot(a0, b1) + _dot(a1, b0))
            + (_dot(a0, b2) + _dot(a1, b1) + _dot(a2, b0)))


def _gather_and_ada(c, w_in_rows, w_ada):
    cols = w_ada.shape[1]

    def body(c_ref, w_ref, wa_ref, w_all_ref, sc_ref, ada_ref,
             c_land, part, ada_land, send_sems, recv_sems, local_sem, x_send, x_recv):
        x, y, cc = _mesh_pos()
        me = _dev_index(x, y, cc)
        peers = []
        for p in range(1, N_DEV):
            px, py, pc = (p >> 2) & 1, (p >> 1) & 1, p & 1
            peers.append((1 - x if px else x, 1 - y if py else y, 1 - cc if pc else cc))

        def remote(src, dst, k, to):
            return pltpu.make_async_remote_copy(src_ref=src, dst_ref=dst, send_sem=x_send.at[k], recv_sem=x_recv.at[k],
                                                device_id=to, device_id_type=MESH)

        c_sends = [remote(c_ref, c_land.at[me], k, to) for k, to in enumerate(peers)]
        for cp in c_sends:
            cp.start()
        start, relay_near, relay_far, finish = _gather_stages(w_ref, w_all_ref, send_sems, recv_sems, local_sem.at[0])
        start()
        c_land[me] = c_ref[...]
        for k, frm in enumerate(peers):
            remote(c_ref, c_land.at[_dev_index(*frm)], k, frm).wait_recv()
        c_all = jnp.concatenate([c_land[b] for b in range(N_DEV)], axis=0)
        sc = c_all * _sigmoid(c_all)
        sc_ref[...] = sc
        rows = _dot3_rhs(sc, wa_ref[...])
        for b in range(N_DEV):
            part[b] = rows[b:b + 1, :]
        a_sends = [remote(part.at[_dev_index(*to)], ada_land.at[me], 7 + k, to) for k, to in enumerate(peers)]
        for cp in a_sends:
            cp.start()
        ada_land[me] = part[me]
        for k, frm in enumerate(peers):
            remote(part.at[0], ada_land.at[_dev_index(*frm)], 7 + k, frm).wait_recv()
        ada_ref[...] = ada_land[...]

        relay_near()
        relay_far()
        finish()
        for cp in c_sends + a_sends:
            cp.wait_send()

    vmem = pl.BlockSpec(memory_space=pltpu.VMEM)
    return pl.pallas_call(
        body, name="gather_weights",
        in_specs=[vmem, ANY, vmem], out_specs=[ANY, vmem, vmem],
        out_shape=[jax.ShapeDtypeStruct((N_DEV,) + w_in_rows.shape, w_in_rows.dtype),
                   jax.ShapeDtypeStruct((N_DEV, D), F32), jax.ShapeDtypeStruct((N_DEV, 1, cols), F32)],
        scratch_shapes=[pltpu.VMEM((N_DEV, 1, D), F32), pltpu.VMEM((N_DEV, 1, cols), F32), pltpu.VMEM((N_DEV, 1, cols), F32),
                        pltpu.SemaphoreType.DMA((N_GATHER_SEMS,)), pltpu.SemaphoreType.DMA((N_GATHER_SEMS,)),
                        pltpu.SemaphoreType.DMA((1,)),
                        pltpu.SemaphoreType.DMA((14,)), pltpu.SemaphoreType.DMA((14,))],
        compiler_params=_params(),
    )(c, w_in_rows, w_ada)


def _inproj_forward(x, mod, w_main, w_f, b_main, b_f, tile):
    seq = x.shape[0]
    nt = seq // tile

    def body(x_ref, mod_ref, w_ref, wf_ref, b_ref, bf_ref,
             qp_ref, kp_ref, vp_ref, f_ref, p_ref, ga_ref, gp_ref, u_ref, carry_ref):
        i = pl.program_id(0)

        @pl.when(i == 0)
        def _():
            carry_ref[...] = jnp.zeros_like(carry_ref)

        u = x_ref[...] * mod_ref[0:1, :] + mod_ref[1:2, :]
        ub = u.astype(BF16)
        u_ref[...] = ub

        f = _dot_nt(ub, wf_ref[...]) + bf_ref[...]
        f_ref[...] = f
        lane = _lanes((tile, LANE))
        log_f = jnp.where(lane < N_HEADS, jnp.minimum(f, 0.0) - jnp.log(1.0 + jnp.exp(-jnp.abs(f))), 0.0)
        row = lax.broadcasted_iota(jnp.int32, (tile, tile), 0)
        col = lax.broadcasted_iota(jnp.int32, (tile, tile), 1)
        tri = (row >= col).astype(BF16)
        cum = _dot3(tri, log_f) + carry_ref[0:1, :]
        carry_ref[0:1, :] = cum[tile - 1:tile, :]
        cq = [part.astype(F32) for part in _split3(cum)]
        ck = [part.astype(F32) for part in _split3(-cum)]

        def proj(chunk):
            cols = pl.ds(chunk * COL_CHUNK, COL_CHUNK)
            return _dot_nt(ub, w_ref[cols, :]) + b_ref[:, cols]

        def head_tiles(r):
            for pair in range(N_HEADS // 2):
                both = r[:, pair * LANE:(pair + 1) * LANE]
                yield 2 * pair, both
                yield 2 * pair + 1, pltpu.roll(both, HEAD_DIM, 1)

        for h, val in head_tiles(proj(0)):
            extra = jnp.where((lane >= L_CK) & (lane < L_CK + 3), 1.0, 0.0)
            extra = _place3(lane, L_CQ, [part[:, h:h + 1] for part in cq], extra)
            qp_ref[h] = jnp.where(lane < HEAD_DIM, val * Q_SCALE, extra).astype(BF16)
        for h, val in head_tiles(proj(1)):
            ones = ((lane >= L_CQ) & (lane < L_CQ + 3)) | ((lane >= L_LSE) & (lane < L_LSE + 3))
            extra = _place3(lane, L_CK, [part[:, h:h + 1] for part in ck], jnp.where(ones, 1.0, 0.0))
            kp_ref[h] = jnp.where(lane < HEAD_DIM, val, extra).astype(BF16)
        for h, val in head_tiles(proj(2)):
            extra = jnp.where((lane >= HEAD_DIM) & (lane < HEAD_DIM + 3), -1.0, 0.0)
            vp_ref[h] = jnp.where(lane < HEAD_DIM, val, extra).astype(BF16)
        p_ref[...] = proj(3)
        ga_ref[...] = proj(4)
        gp_ref[...] = proj(5)

    head_block = pl.BlockSpec((N_HEADS, tile, LANE), lambda i: (0, i, 0))
    tok = lambda width: pl.BlockSpec((tile, width), lambda i: (i, 0))
    whole = lambda a: pl.BlockSpec(a.shape, lambda i: (0,) * a.ndim)
    padded = jax.ShapeDtypeStruct((N_HEADS, seq, LANE), BF16)
    half = jax.ShapeDtypeStruct((seq, D_ATT), F32)
    return pl.pallas_call(
        body, name="inproj_forward", grid=(nt,),
        in_specs=[tok(D), whole(mod), whole(w_main), whole(w_f), whole(b_main), whole(b_f)],
        out_specs=[head_block, head_block, head_block, tok(LANE), tok(D_POOL), tok(D_ATT), tok(D_POOL),
                   tok(D)],
        out_shape=[padded, padded, padded, jax.ShapeDtypeStruct((seq, LANE), F32), half, half, half,
                   jax.ShapeDtypeStruct((seq, D), BF16)],
        scratch_shapes=[pltpu.VMEM((8, LANE), F32)],
        compiler_params=_params(("arbitrary",)),
    )(x, mod, w_main, w_f, b_main, b_f)


def _attention_forward(qp, kp, vp, w_out, tile):
    seq = qp.shape[1]
    nb = seq // tile
    steps = (N_HEADS // 2) * nb

    def body(q_ref, k_ref, v_ref, wo_ref, att_ref, q2t_ref, wo_all_ref, s_a, s_b, m_ref, acc_ref,
             send_sems, recv_sems, local_sem):
        step = pl.program_id(0) * nb + pl.program_id(1)
        start, relay_near, relay_far, finish = _gather_stages(wo_ref, wo_all_ref, send_sems, recv_sems, local_sem.at[0])
        pl.when(step == 0)(start)
        pl.when(step == steps // 4)(relay_near)
        pl.when(step == (3 * steps) // 4)(relay_far)

        i = pl.program_id(1)
        sub = lax.broadcasted_iota(jnp.int32, (LANE, tile), 0)
        row = lax.broadcasted_iota(jnp.int32, (tile, tile), 0)
        col = lax.broadcasted_iota(jnp.int32, (tile, tile), 1)
        q = [q_ref[0], q_ref[1]]

        def scores(buf, kb):
            rows = pl.ds(pl.multiple_of(kb * tile, tile), tile)
            for hh in range(2):
                buf[hh] = _dot_nt(k_ref[hh, rows, :], q[hh])

        def absorb(buf, kb, masked):
            rows = pl.ds(pl.multiple_of(kb * tile, tile), tile)
            for hh in range(2):
                m = m_ref[hh, 0:1, :]
                s = buf[hh]
                if masked:
                    s = jnp.where(row <= col, s, -1e30)
                m_new = jnp.maximum(m, jnp.max(s, axis=0, keepdims=True))
                p = jnp.exp(s - m_new).astype(BF16)
                acc_ref[hh] = jnp.exp(m - m_new) * acc_ref[hh] + _dot_tn(v_ref[hh, rows, :], p)
                m_ref[hh, 0:1, :] = m_new

        def two_blocks(j, _):
            scores(s_b, 2 * j + 1)
            absorb(s_a, 2 * j, False)
            scores(s_a, 2 * j + 2)
            absorb(s_b, 2 * j + 1, False)
            return 0

        def last_block():
            absorb(s_a, i, True)

        def last_two_blocks():
            scores(s_b, i)
            absorb(s_a, i - 1, False)
            absorb(s_b, i, True)

        scores(s_a, 0)
        m_ref[...] = jnp.full(m_ref.shape, -1e30, F32)
        acc_ref[...] = jnp.zeros_like(acc_ref)
        lax.fori_loop(0, i // 2, two_blocks, 0)
        lax.cond(i % 2 == 0, last_block, last_two_blocks)
        outs = []
        for hh in range(2):
            m, acc = m_ref[hh, 0:1, :], acc_ref[hh]
            l = -acc[HEAD_DIM:HEAD_DIM + 1, :]
            outs.append((acc / l)[:HEAD_DIM, :])
            neg_lse = [part.astype(F32) for part in _split3(-(m + jnp.log(l)))]
            q2t_ref[hh] = _place3(sub, L_LSE, neg_lse, q[hh].astype(F32).T).astype(BF16)
        att_ref[...] = jnp.concatenate(outs, axis=0).T
        pl.when(step == steps - 1)(finish)

    pair = pl.BlockSpec((2, tile, LANE), lambda hp, i: (hp, i, 0))
    full = pl.BlockSpec((2, seq, LANE), lambda hp, i: (hp, 0, 0))
    return pl.pallas_call(
        body, name="attention_forward", grid=(N_HEADS // 2, nb),
        in_specs=[pair, full, full, ANY],
        out_specs=[pl.BlockSpec((tile, LANE), lambda hp, i: (i, hp)),
                   pl.BlockSpec((2, LANE, tile), lambda hp, i: (hp, 0, i)), ANY],
        out_shape=[jax.ShapeDtypeStruct((seq, D_ATT), F32),
                   jax.ShapeDtypeStruct((N_HEADS, LANE, seq), BF16),
                   jax.ShapeDtypeStruct((N_DEV,) + w_out.shape, w_out.dtype)],
        scratch_shapes=[pltpu.VMEM((2, tile, tile), F32), pltpu.VMEM((2, tile, tile), F32),
                        pltpu.VMEM((2, 8, tile), F32), pltpu.VMEM((2, LANE, tile), F32),
                        pltpu.SemaphoreType.DMA((N_GATHER_SEMS,)), pltpu.SemaphoreType.DMA((N_GATHER_SEMS,)),
                        pltpu.SemaphoreType.DMA((1,))],
        compiler_params=_params(("arbitrary", "arbitrary")),
    )(qp, kp, vp, w_out)


def _window_sum(x, halo, window, transposed):
    tile = x.shape[0]

    def split_cat(a):
        hi = a.astype(BF16)
        return jnp.concatenate([hi, (a - hi.astype(F32)).astype(BF16)], axis=1)

    def fold(r):
        return r[:, :LANE] + r[:, LANE:]

    r = lax.broadcasted_iota(jnp.int32, (tile, tile), 0)
    c = lax.broadcasted_iota(jnp.int32, (tile, tile), 1)
    rh = lax.broadcasted_iota(jnp.int32, (HALO, HALO), 0)
    ch = lax.broadcasted_iota(jnp.int32, (HALO, HALO), 1)
    if not transposed:
        band = (c <= r) & (r - c < window)
        edge = (rh + HALO - ch) < window
    else:
        band = (r <= c) & (c - r < window)
        edge = (HALO + ch - rh) < window
    out = fold(_dot(band.astype(BF16), split_cat(x)))
    reach = fold(_dot(edge.astype(BF16), split_cat(halo)))
    if not transposed:
        return jnp.concatenate([out[:HALO] + reach, out[HALO:]], axis=0)
    return jnp.concatenate([out[:tile - HALO], out[tile - HALO:] + reach], axis=0)


def _silu_parts(g):
    sig = _sigmoid(g)
    return g * sig, sig * (1.0 + g * (1.0 - sig))


def _middle(x, tgt, att, g_att, g_pool, p, vecs, pool_vecs, w_out, w_pool, tile):
    seq = x.shape[0]
    nt = seq // tile
    halo_blocks = tile // HALO

    def body(x_ref, tgt_ref, att_ref, ga_ref, gp_ref, p_ref, ph_ref, vec_ref, pvec_ref, wo_ref, wp_ref,
             dxa_ref, do2_ref, dga_ref, dgp_ref, dpooled_ref, gwo_ref, dwp_ref, dvec_ref, dwo_ref, dpvec_ref):
        i = pl.program_id(0)

        @pl.when(i == 0)
        def _():
            dwo_ref[...] = jnp.zeros_like(dwo_ref)
            dwp_ref[...] = jnp.zeros_like(dwp_ref)
            dvec_ref[...] = jnp.zeros_like(dvec_ref)
            dpvec_ref[...] = jnp.zeros_like(dpvec_ref)

        gate, b_out, ln_g, ln_b = (vec_ref[k:k + 1, :] for k in range(4))
        b_pool, pool_scale = pvec_ref[0:1, :], pvec_ref[1:2, :]
        x = x_ref[...]
        p = p_ref[...]
        p_halo = ph_ref[...] * jnp.where(i > 0, 1.0, 0.0)
        pos = i * tile + lax.broadcasted_iota(jnp.int32, (tile, 1), 0) + 1

        pooled, mixed = [], []
        for g, window in enumerate(POOL_WINDOWS):
            cols = slice(g * GROUP_DIM, (g + 1) * GROUP_DIM)
            wsum = _window_sum(p[:, cols], p_halo[:, cols], window, False)
            count = jnp.minimum(pos, window).astype(F32)
            pooled.append(wsum / count - p[:, cols])
            mixed.append(_dot(pooled[g].astype(BF16), wp_ref[g]) + b_pool[:, cols])
        mixed = jnp.concatenate(mixed, axis=1)
        pool = mixed * pool_scale

        att = att_ref[...]
        g_att, g_pool = ga_ref[...], gp_ref[...]
        silu_a, dsilu_a = _silu_parts(g_att)
        silu_p, dsilu_p = _silu_parts(g_pool)
        y_in = jnp.concatenate([att * silu_a, pool * silu_p], axis=1)
        y = _dot(y_in.astype(BF16), wo_ref[...]) + b_out
        h = ALPHA * x + gate * y
        mu = jnp.mean(h, axis=1, keepdims=True)
        hc = h - mu
        var = jnp.mean(hc * hc, axis=1, keepdims=True)
        rstd = lax.rsqrt(var + LN_EPS)
        yhat = hc * rstd
        diff = yhat * ln_g + ln_b - tgt_ref[...]
        loss_rows = jnp.sum(diff * diff, axis=1, keepdims=True)
        d_out = diff * (1.0 / D)

        d_yhat = d_out * ln_g
        dh = rstd * (d_yhat - jnp.mean(d_yhat, axis=1, keepdims=True)
                     - yhat * jnp.mean(d_yhat * yhat, axis=1, keepdims=True))
        dxa_ref[...] = ALPHA * dh
        dy = dh * gate
        dyb = dy.astype(BF16)
        lane = _lanes((1, D))
        loss_row = jnp.where(lane == 0, (0.5 / D) * jnp.sum(loss_rows, axis=0, keepdims=True), 0.0)
        dvec_ref[5:6, :] += jnp.sum(dh * y, axis=0, keepdims=True)
        dvec_ref[0:1, :] += jnp.sum(dy, axis=0, keepdims=True)
        dvec_ref[1:2, :] += jnp.sum(d_out * yhat, axis=0, keepdims=True)
        dvec_ref[2:3, :] += jnp.sum(d_out, axis=0, keepdims=True)
        dvec_ref[4:5, :] += loss_row

        dwo_ref[...] += _dot(y_in.T.astype(BF16), dyb)
        d_yin = _dot_nt(dyb, wo_ref[...])
        d_a, d_pl = d_yin[:, :D_ATT], d_yin[:, D_ATT:]
        d_att = d_a * silu_a
        d_att_t = d_att.T
        prod_t = (d_att * att).T
        sub = lax.broadcasted_iota(jnp.int32, (HEAD_DIM, tile), 0)
        for h in range(N_HEADS):
            rows = slice(h * HEAD_DIM, (h + 1) * HEAD_DIM)
            delta = jnp.sum(prod_t[rows], axis=0, keepdims=True)
            extra = _place3(sub, 0, [part.astype(F32) for part in _split3(delta)], 0.0)
            do2_ref[h] = jnp.concatenate([d_att_t[rows], extra], axis=0).astype(BF16)
        dga_ref[...] = d_a * att * dsilu_a
        dgp_ref[...] = d_pl * pool * dsilu_p
        d_pool = d_pl * silu_p
        d_mixed = d_pool * pool_scale
        dpvec_ref[0:1, :] += jnp.sum(d_mixed, axis=0, keepdims=True)
        dpvec_ref[1:2, :] += jnp.sum(d_pool * mixed, axis=0, keepdims=True)
        d_pooled = []
        for g in range(len(POOL_WINDOWS)):
            cols = slice(g * GROUP_DIM, (g + 1) * GROUP_DIM)
            dmb = d_mixed[:, cols].astype(BF16)
            dwp_ref[g] += _dot(pooled[g].T.astype(BF16), dmb)
            d_pooled.append(_dot_nt(dmb, wp_ref[g]))
        dpooled_ref[...] = jnp.concatenate(d_pooled, axis=1)

        @pl.when(i == nt - 1)
        def _():
            gwo_ref[...] = dwo_ref[...].astype(BF16)
            dvec_ref[3:4, :] = jnp.concatenate([dpvec_ref[0:1, :], dpvec_ref[1:2, :]], axis=1)

    tok = lambda width: pl.BlockSpec((tile, width), lambda i: (i, 0))
    whole = lambda a: pl.BlockSpec(a.shape, lambda i: (0,) * a.ndim)
    halo = pl.BlockSpec((HALO, D_POOL), lambda i: (jnp.maximum(i * halo_blocks - 1, 0), 0))
    half = jax.ShapeDtypeStruct((seq, D_ATT), F32)
    outs = [jax.ShapeDtypeStruct((seq, D), F32), jax.ShapeDtypeStruct((N_HEADS, LANE, seq), BF16), half, half, half,
            jax.ShapeDtypeStruct(w_out.shape, BF16), jax.ShapeDtypeStruct(w_pool.shape, F32),
            jax.ShapeDtypeStruct(vecs.shape, F32)]
    return pl.pallas_call(
        body, name="middle", grid=(nt,),
        in_specs=[tok(D), tok(D), tok(D_ATT), tok(D_ATT), tok(D_POOL), tok(D_POOL), halo,
                  whole(vecs), whole(pool_vecs), whole(w_out), whole(w_pool)],
        out_specs=[tok(D), pl.BlockSpec((N_HEADS, LANE, tile), lambda i: (0, 0, i)),
                   tok(D_ATT), tok(D_POOL), tok(D_POOL),
                   whole(w_out), whole(w_pool), whole(vecs)],
        out_shape=outs,
        scratch_shapes=[pltpu.VMEM(w_out.shape, F32), pltpu.VMEM(pool_vecs.shape, F32)],
        compiler_params=_params(("arbitrary",)),
    )(x, tgt, att, g_att, g_pool, p, p, vecs, pool_vecs, w_out, w_pool)


def _attention_backward(q2t, kp, vp, do2t, gw_out, vecs, pool, tile):
    seq = kp.shape[1]
    nb = seq // tile
    last = N_HEADS // 2 - 1

    def body(qt_ref, k_ref, v_ref, dot_ref, gwo_hbm, vecs_hbm, pool_hbm,
             dq_ref, dk_ref, dv_ref, dcum_ref, g_out_ref, vecs_sum_ref, pool_sum_ref,
             dq_acc, dk_acc, dv_acc, gwo_ref, vecs_ref, pool_ref,
             r1, s2, r2, v_sib, v_chip, v_recv, p_sib, p_chip, p_recv, send_sems, recv_sems):
        hp = pl.program_id(0)
        start, middle, fold, finish = _reduce_stages(
            [gwo_ref], [g_out_ref], [r1], [s2], [r2],
            [(vecs_ref, vecs_sum_ref, v_sib, v_chip, v_recv), (pool_ref, pool_sum_ref, p_sib, p_chip, p_recv)],
            send_sems, recv_sems)

        @pl.when(hp == 0)
        def _():
            pltpu.sync_copy(gwo_hbm, gwo_ref)
            pltpu.sync_copy(vecs_hbm, vecs_ref)
            pltpu.sync_copy(pool_hbm, pool_ref)
            start()

        pl.when(hp == 1)(middle)
        pl.when(hp == 2)(fold)

        row = lax.broadcasted_iota(jnp.int32, (tile, tile), 0)
        col = lax.broadcasted_iota(jnp.int32, (tile, tile), 1)
        dq_acc[...] = jnp.zeros_like(dq_acc)

        def kv_block(kb, _):
            krows = pl.ds(pl.multiple_of(kb * tile, tile), tile)
            k = [k_ref[hh, krows, :] for hh in range(2)]
            v = [v_ref[hh, krows, :] for hh in range(2)]
            k_t = [k[hh].T for hh in range(2)]

            def q_block(qb, masked):
                qcols = pl.ds(pl.multiple_of(qb * tile, tile), tile)
                for hh in range(2):
                    q_t = qt_ref[hh, :, qcols]
                    do_t = dot_ref[hh, :, qcols]
                    s_t = _dot(k[hh], q_t)
                    if masked:
                        s_t = jnp.where(row <= col, s_t, -1e30)
                    p_t = jnp.exp(s_t)
                    ds_t = (p_t * _dot(v[hh], do_t)).astype(BF16)
                    dv_new = _dot_nt(do_t, p_t.astype(BF16))
                    dk_new = _dot_nt(q_t, ds_t)
                    if masked:
                        dv_acc[hh], dk_acc[hh] = dv_new, dk_new
                    else:
                        dv_acc[hh] += dv_new
                        dk_acc[hh] += dk_new
                    dq_acc[hh, :, qcols] += _dot(k_t[hh], ds_t)

            q_block(kb, True)

            def two_later_blocks(j, _):
                q_block(kb + 1 + 2 * j, False)
                q_block(kb + 2 + 2 * j, False)
                return 0

            later = nb - 1 - kb
            lax.fori_loop(0, later // 2, two_later_blocks, 0)
            pl.when(later % 2 == 1)(lambda: q_block(nb - 1, False))
            for hh in range(2):
                dk = dk_acc[hh]
                dk_ref[hh, :, krows] = dk.astype(BF16)
                dv_ref[hh, :, krows] = dv_acc[hh].astype(BF16)
                dcum_ref[hh, :, krows] = -dk[L_CK:L_CK + 1, :]
            return 0

        lax.fori_loop(0, nb, kv_block, 0)
        for hh in range(2):
            dq = dq_acc[hh]
            dcum_ref[hh] += dq[L_CQ:L_CQ + 1, :]
            dq_ref[hh] = (dq * Q_SCALE).astype(BF16)
        pl.when(hp == last)(finish)

    pair = pl.BlockSpec((2, seq, LANE), lambda hp: (hp, 0, 0))
    pair_t = pl.BlockSpec((2, LANE, seq), lambda hp: (hp, 0, 0))
    whole = lambda shape: pl.BlockSpec(shape, lambda hp: (0,) * len(shape))
    grad = jax.ShapeDtypeStruct((N_HEADS, LANE, seq), BF16)
    return pl.pallas_call(
        body, name="attention_backward", grid=(N_HEADS // 2,),
        in_specs=[pair_t, pair, pair, pair_t, ANY, ANY, ANY],
        out_specs=[pair_t, pair_t, pair_t, pl.BlockSpec((2, 1, seq), lambda hp: (hp, 0, 0)),
                   whole(gw_out.shape[1:]), whole(vecs.shape), whole(pool.shape)],
        out_shape=[grad, grad, grad, jax.ShapeDtypeStruct((N_HEADS, 1, seq), F32),
                   jax.ShapeDtypeStruct(gw_out.shape[1:], F32), jax.ShapeDtypeStruct(vecs.shape, F32),
                   jax.ShapeDtypeStruct(pool.shape, F32)],
        scratch_shapes=[pltpu.VMEM((2, LANE, seq), F32), pltpu.VMEM((2, LANE, tile), F32),
                        pltpu.VMEM((2, LANE, tile), F32), pltpu.VMEM(gw_out.shape, BF16),
                        pltpu.VMEM(vecs.shape, F32), pltpu.VMEM(pool.shape, F32)]
        + _reduce_scratch(gw_out, [vecs, pool]),
        compiler_params=_params(("arbitrary",)),
    )(q2t, kp, vp, do2t, gw_out, vecs, pool)


def _inproj_backward(dqp, dkp, dvp, d_cum, f, d_pooled, d_ga, d_gp, x, dxa, u, mod, w_main, w_f, tile):
    seq = x.shape[0]
    nt = seq // tile
    halo_blocks = tile // HALO

    def body(dq_ref, dk_ref, dv_ref, dcum_ref, f_ref, dpo_ref, dph_ref, dga_ref, dgp_ref, x_ref, dxa_ref, u_ref,
             mod_ref, w_ref, wf_ref,
             dx_ref, dproj_ref, dwf_ref, db_ref, dbf_ref, dmod_ref, carry_ref):
        step = pl.program_id(0)
        i = nt - 1 - step

        @pl.when(step == 0)
        def _():
            carry_ref[...] = jnp.zeros_like(carry_ref)
            dwf_ref[...] = jnp.zeros_like(dwf_ref)
            db_ref[...] = jnp.zeros_like(db_ref)
            dbf_ref[...] = jnp.zeros_like(dbf_ref)
            dmod_ref[...] = jnp.zeros_like(dmod_ref)

        ones = jnp.ones((8, tile), BF16)

        def emit(chunk, val):
            cols = pl.ds(chunk * COL_CHUNK, COL_CHUNK)
            db_ref[0:1, cols] += jnp.sum(val, axis=0, keepdims=True)
            vb = val.astype(BF16)
            dproj_ref[:, pl.ds((chunk - 3) * COL_CHUNK, COL_CHUNK)] = vb
            return _dot(vb, w_ref[cols, :])

        d_u = jnp.zeros((tile, D), F32)
        for chunk, ref in enumerate((dq_ref, dk_ref, dv_ref)):
            cols = pl.ds(chunk * COL_CHUNK, COL_CHUNK)
            val_t = ref[:, 0:HEAD_DIM, :].reshape(COL_CHUNK, tile)
            db_ref[:, cols] += _dot_nt(ones, val_t)
            d_u += _dot_tn(val_t, w_ref[cols, :])

        d_pooled = dpo_ref[...]
        d_halo = dph_ref[...] * jnp.where(i < nt - 1, 1.0, 0.0)
        pos = i * tile + lax.broadcasted_iota(jnp.int32, (tile, 1), 0) + 1
        d_p = []
        for g, window in enumerate(POOL_WINDOWS):
            cols = slice(g * GROUP_DIM, (g + 1) * GROUP_DIM)
            scaled = d_pooled[:, cols] / jnp.minimum(pos, window).astype(F32)
            d_p.append(_window_sum(scaled, d_halo[:, cols] * (1.0 / window), window, True) - d_pooled[:, cols])
        d_u += emit(3, jnp.concatenate(d_p, axis=1))
        d_u += emit(4, dga_ref[...])
        d_u += emit(5, dgp_ref[...])

        row = lax.broadcasted_iota(jnp.int32, (tile, tile), 0)
        col = lax.broadcasted_iota(jnp.int32, (tile, tile), 1)
        later = (row >= col).astype(BF16)
        d_logf = sum(_dot(part, later) for part in _split3(dcum_ref[:, 0, :])) + carry_ref[:, 0:1]
        carry_ref[:, 0:1] = d_logf[:, 0:1]
        d_f = d_logf * _sigmoid(-f_ref[...].T[0:N_HEADS, :])
        d_f = jnp.concatenate([d_f, jnp.zeros((LANE - N_HEADS, tile), F32)], axis=0)
        dbf_ref[...] += sum(_dot_nt(ones, part) for part in _split3(d_f))
        d_fb = d_f.astype(BF16)
        d_u += _dot_tn(d_fb, wf_ref[...])
        dwf_ref[...] += _dot(d_fb, u_ref[...])

        x = x_ref[...]
        dx_ref[...] = dxa_ref[...] + d_u * mod_ref[0:1, :]
        dmod_ref[0:1, :] += jnp.sum(d_u * x, axis=0, keepdims=True)
        dmod_ref[1:2, :] += jnp.sum(d_u, axis=0, keepdims=True)

    rev = lambda step: nt - 1 - step
    tok = lambda width: pl.BlockSpec((tile, width), lambda s: (rev(s), 0))
    head_block = pl.BlockSpec((N_HEADS, LANE, tile), lambda s: (0, 0, rev(s)))
    whole = lambda a: pl.BlockSpec(a.shape, lambda s: (0,) * a.ndim)
    halo = pl.BlockSpec((HALO, D_POOL), lambda s: (jnp.minimum((rev(s) + 1) * halo_blocks, seq // HALO - 1), 0))
    small = lambda width: jax.ShapeDtypeStruct((8, width), F32)
    n_rest = N_MAIN - OFF_P
    return pl.pallas_call(
        body, name="inproj_backward", grid=(nt,),
        in_specs=[head_block, head_block, head_block, pl.BlockSpec((N_HEADS, 1, tile), lambda s: (0, 0, rev(s))),
                  tok(LANE), tok(D_POOL), halo, tok(D_ATT), tok(D_POOL),
                  tok(D), tok(D), tok(D),
                  whole(mod), whole(w_main), whole(w_f)],
        out_specs=[tok(D), tok(n_rest), pl.BlockSpec((LANE, D), lambda s: (0, 0)),
                   pl.BlockSpec((8, N_MAIN), lambda s: (0, 0)), pl.BlockSpec((8, LANE), lambda s: (0, 0)),
                   pl.BlockSpec((8, D), lambda s: (0, 0))],
        out_shape=[jax.ShapeDtypeStruct((seq, D), F32), jax.ShapeDtypeStruct((seq, n_rest), BF16),
                   jax.ShapeDtypeStruct((LANE, D), F32), small(N_MAIN), small(LANE), small(D)],
        scratch_shapes=[pltpu.VMEM((8, LANE), F32)],
        compiler_params=_params(("arbitrary",)),
    )(dqp, dkp, dvp, d_cum, f, d_pooled, d_pooled, d_ga, d_gp, x, dxa, u, mod, w_main, w_f)


def _weight_grad(dproj, u, k_tile):
    seq, n_cols = dproj.shape
    nk = seq // k_tile

    def body(dp_ref, u_ref, out_ref, acc_ref):
        k = pl.program_id(1)

        @pl.when(k == 0)
        def _():
            acc_ref[...] = jnp.zeros_like(acc_ref)

        acc_ref[...] += _dot_tn(dp_ref[...], u_ref[...])

        @pl.when(k == nk - 1)
        def _():
            out_ref[...] = acc_ref[...].astype(BF16)

    return pl.pallas_call(
        body, name="weight_grad", grid=(n_cols // COL_CHUNK, nk),
        in_specs=[pl.BlockSpec((k_tile, COL_CHUNK), lambda n, k: (k, n)),
                  pl.BlockSpec((k_tile, D), lambda n, k: (k, 0))],
        out_specs=pl.BlockSpec((COL_CHUNK, D), lambda n, k: (n, 0)),
        out_shape=jax.ShapeDtypeStruct((n_cols, D), BF16),
        scratch_shapes=[pltpu.VMEM((COL_CHUNK, D), F32)],
        compiler_params=_params(("arbitrary", "arbitrary")),
    )(dproj, u)


def _weight_grad_heads(grad_t, u, k_tile, name):
    seq = u.shape[0]
    nk = seq // k_tile
    rows = N_HEADS * HEAD_DIM

    def body(g_ref, u_ref, out_ref, acc_ref):
        k = pl.program_id(0)

        @pl.when(k == 0)
        def _():
            acc_ref[...] = jnp.zeros_like(acc_ref)

        acc_ref[...] += _dot(g_ref[...].reshape(rows, k_tile), u_ref[...])

        @pl.when(k == nk - 1)
        def _():
            out_ref[...] = acc_ref[...].astype(BF16)

    return pl.pallas_call(
        body, name=name, grid=(nk,),
        in_specs=[pl.BlockSpec((N_HEADS, HEAD_DIM, k_tile), lambda k: (0, 0, k)),
                  pl.BlockSpec((k_tile, D), lambda k: (k, 0))],
        out_specs=pl.BlockSpec((rows, D), lambda k: (0, 0)),
        out_shape=jax.ShapeDtypeStruct((rows, D), BF16),
        scratch_shapes=[pltpu.VMEM((rows, D), F32)],
        compiler_params=_params(("arbitrary",)),
    )(grad_t, u)


def _adamw(w, g, m, v):
    m = ADAM_B1 * m + (1.0 - ADAM_B1) * g
    v = ADAM_B2 * v + (1.0 - ADAM_B2) * (g * g)
    m_hat = m / (1.0 - ADAM_B1 ** ADAM_STEP)
    v_hat = v / (1.0 - ADAM_B2 ** ADAM_STEP)
    delta = -ADAM_LR * (m_hat / (jnp.sqrt(v_hat) + ADAM_EPS) + ADAM_WD * w)
    return delta, m, v


def _adamw_call(g, w, m, v, lead_tile, name):
    nr = w.shape[0] // lead_tile

    def body(gi_ref, w_ref, m_ref, v_ref, g_ref, d_ref, nm_ref, nv_ref):
        g = gi_ref[...]
        g_ref[...] = g
        d_ref[...], nm_ref[...], nv_ref[...] = _adamw(w_ref[...], g, m_ref[...], v_ref[...])

    blk = pl.BlockSpec((lead_tile,) + w.shape[1:], lambda r: (r,) + (0,) * (w.ndim - 1))
    shape = jax.ShapeDtypeStruct(w.shape, F32)
    return pl.pallas_call(
        body, name=name, grid=(nr,),
        in_specs=[blk, blk, blk, blk], out_specs=[blk, blk, blk, blk],
        out_shape=[shape, shape, shape, shape],
        compiler_params=_params(("arbitrary",)),
    )(g, w, m, v)


def _ada_adamw(sc_t, d_ada, w, m, v):
    def body(sc_ref, d_ref, w_ref, m_ref, v_ref, g_ref, dl_ref, nm_ref, nv_ref):
        g = sc_ref[:, 0:1] * d_ref[0:1, :]
        for b in range(1, N_DEV):
            g = g + sc_ref[:, b:b + 1] * d_ref[b:b + 1, :]
        g_ref[...] = g
        dl_ref[...], nm_ref[...], nv_ref[...] = _adamw(w_ref[...], g, m_ref[...], v_ref[...])

    row_tile = 256
    blk = pl.BlockSpec((row_tile, w.shape[1]), lambda r: (r, 0))
    shape = jax.ShapeDtypeStruct(w.shape, F32)
    return pl.pallas_call(
        body, name="ada_adamw", grid=(w.shape[0] // row_tile,),
        in_specs=[pl.BlockSpec((row_tile, N_DEV), lambda r: (r, 0)), pl.BlockSpec(d_ada.shape, lambda r: (0, 0)),
                  blk, blk, blk],
        out_specs=[blk, blk, blk, blk], out_shape=[shape, shape, shape, shape],
        compiler_params=_params(("arbitrary",)),
    )(sc_t, d_ada, w, m, v)


F_LO, F_HI = 3 * D_ATT, 3 * D_ATT + N_HEADS


def _split_forget(a, axis):
    idx = lambda lo, hi: tuple(slice(lo, hi) if d == axis else slice(None) for d in range(a.ndim))
    pad = [(0, LANE - N_HEADS) if d == axis else (0, 0) for d in range(a.ndim)]
    return jnp.concatenate([a[idx(0, F_LO)], a[idx(F_HI, D_IN)]], axis=axis), jnp.pad(a[idx(F_LO, F_HI)], pad)


def _join_forget(main, f, axis):
    idx = lambda lo, hi: tuple(slice(lo, hi) if d == axis else slice(None) for d in range(main.ndim))
    return jnp.concatenate([main[idx(0, F_LO)], f[idx(0, N_HEADS)], main[idx(F_LO, N_MAIN)]], axis=axis)


def _adamw_small(grad_rows, row_params, whole_params, summed_params):
    n_row, n_whole, n_sum = len(row_params), len(whole_params), len(summed_params)
    n = n_row + n_whole + n_sum

    def body(g_ref, *refs):
        n_in = 3 * n_row + 4 * (n_whole + n_sum)
        ins, outs = list(refs[:n_in]), refs[n_in:]
        for i in range(n):
            if i < n_row:
                row, lo, hi = row_params[i][:3]
                g = g_ref[row:row + 1, lo:hi]
            elif i < n_row + n_whole:
                g = ins.pop(0)[...]
            else:
                parts = ins.pop(0)
                g = parts[0]
                for k in range(1, N_DEV):
                    g = g + parts[k]
            w, m, v = (ins.pop(0)[...] for _ in range(3))
            outs[4 * i][...] = g
            outs[4 * i + 1][...], outs[4 * i + 2][...], outs[4 * i + 3][...] = _adamw(w, g, m, v)

    shapes = [p[3] for p in row_params] + [p[1] for p in whole_params] + [p[1] for p in summed_params]
    operands = [a for p in row_params for a in p[3:]] + [a for p in whole_params + summed_params for a in p]
    flat = pl.pallas_call(
        body, name="adamw_small",
        out_shape=[jax.ShapeDtypeStruct(w.shape, F32) for w in shapes for _ in range(4)],
        compiler_params=_params(),
    )(grad_rows, *operands)
    return [flat[4 * i:4 * i + 4] for i in range(n)]


def kernel(x, c, w_ada, b_ada, w_in, b_in, w_pool_mix, b_pool_mix, pool_scale, w_out, b_out, ln_g, ln_b, loss_target, m_w_ada, m_b_ada, m_w_in, m_b_in, m_w_pool_mix, m_b_pool_mix, m_pool_scale, m_w_out, m_b_out, m_ln_g, m_ln_b, v_w_ada, v_b_ada, v_w_in, v_b_in, v_w_pool_mix, v_b_pool_mix, v_pool_scale, v_w_out, v_b_out, v_ln_g, v_ln_b):
    seq = x.shape[1]
    tile = min(256, seq)
    attn_tile = min(512, max(128, seq // 4))
    me = _dev_index(*_mesh_pos())
    x2, tgt = x[0], loss_target[0]

    rows_in = D_IN // N_DEV
    w_in_g, sc_all, ada_mine = _gather_and_ada(c, w_in[0].T.astype(BF16), w_ada[0])
    ada = ada_mine.reshape(1, D_ADA) + b_ada
    shift, scale, gate = ada[:, 0:D], ada[:, D:2 * D], ada[:, 2 * D:]
    mod = jnp.concatenate([1.0 + scale, shift, jnp.zeros((6, D), F32)], axis=0)

    w_main, w_f = _split_forget(w_in_g.reshape(D_IN, D), 0)
    b_main, b_f = _split_forget(b_in, 1)

    qp, kp, vp, f, p, g_att, g_pool, u = _inproj_forward(x2, mod, w_main, w_f, b_main, b_f, tile)
    att, q2t, w_out_g = _attention_forward(qp, kp, vp, w_out[0].astype(BF16), attn_tile)

    vecs = jnp.concatenate([gate, b_out, ln_g, ln_b, jnp.zeros((4, D), F32)], axis=0)
    pool_vecs = jnp.concatenate([b_pool_mix.reshape(1, D_POOL), pool_scale, jnp.zeros((6, D_POOL), F32)], axis=0)
    dxa, do2, d_ga, d_gp, d_pooled, gw_out, dw_pool, dvec = _middle(
        x2, tgt, att, g_att, g_pool, p, vecs, pool_vecs, w_out_g.reshape(D, D), w_pool_mix[0].astype(BF16), tile)

    pool_rows = w_pool_mix.shape[1] * GROUP_DIM
    dqp, dkp, dvp, d_cum, g_out, dvec_sum, dw_pool_sum = _attention_backward(
        q2t, kp, vp, do2, gw_out.reshape(N_DEV, D // N_DEV, D), dvec, dw_pool.reshape(pool_rows, GROUP_DIM), attn_tile)
    dx, dproj, dw_f, db_main, db_f, dmod = _inproj_backward(
        dqp, dkp, dvp, d_cum, f, d_pooled, d_ga, d_gp, x2, dxa, u, mod, w_main, w_f, tile)
    k_tile = min(1024, seq)
    dw_q, dw_k, dw_v = (_weight_grad_heads(g, u, k_tile, "weight_grad_" + n)
                        for g, n in ((dqp, "q"), (dkp, "k"), (dvp, "v")))
    dw_rest = _weight_grad(dproj, u, k_tile)

    dw_main = jnp.concatenate([dw_q, dw_k, dw_v, dw_rest], axis=0)
    gw_in = _join_forget(dw_main, dw_f.astype(BF16), 0).reshape(N_DEV, rows_in, D)
    d_ada = jnp.concatenate([dmod[1:2], dmod[0:1], dvec[5:6]], axis=1)
    g_in_rows, g_b_in, d_ada_all = _reduce_grads(gw_in, _join_forget(db_main[0:1], db_f[0:1], 1), d_ada)

    def rows3(a):
        return a[0].T.reshape(rows_in, D // LANE, LANE)

    outs_in = _adamw_call(g_in_rows.reshape(rows_in, D // LANE, LANE), rows3(w_in), rows3(m_w_in), rows3(v_w_in),
                          rows_in // 5, "adamw_w_in")
    g_w_in, d_w_in, nm_w_in, nv_w_in = (a.reshape(rows_in, D).T for a in outs_in)
    flat_pool = lambda a: a.reshape(1, D_POOL)
    pool_2d = lambda a: a.reshape(pool_rows, GROUP_DIM)
    rows = _adamw_small(
        dvec_sum,
        [(0, 0, D, b_out, m_b_out, v_b_out), (1, 0, D, ln_g, m_ln_g, v_ln_g), (2, 0, D, ln_b, m_ln_b, v_ln_b),
         (3, 0, D_POOL, flat_pool(b_pool_mix), flat_pool(m_b_pool_mix), flat_pool(v_b_pool_mix)),
         (3, D_POOL, 2 * D_POOL, pool_scale, m_pool_scale, v_pool_scale)],
        [(g_out, w_out[0], m_w_out[0], v_w_out[0]),
         (dw_pool_sum, pool_2d(w_pool_mix), pool_2d(m_w_pool_mix), pool_2d(v_w_pool_mix)),
         (g_b_in, b_in, m_b_in, v_b_in)],
        [(d_ada_all, b_ada, m_b_ada, v_b_ada)])
    small = {"b_out": rows[0], "ln_g": rows[1], "ln_b": rows[2],
             "b_pool": [a.reshape(b_pool_mix.shape) for a in rows[3]], "pool_scale": rows[4],
             "w_pool": [a.reshape(w_pool_mix.shape) for a in rows[6]], "b_in": rows[7]}
    g_s, d_s, nm_s, nv_s = ({k: r[j] for k, r in small.items()} for j in range(4))
    g_w_out, d_w_out, nm_w_out, nv_w_out = rows[5]
    g_b_ada, d_b_ada, nm_b_ada, nv_b_ada = rows[8]
    loss = dvec_sum[4, 0]

    d_ada_local = lax.dynamic_slice_in_dim(d_ada_all.reshape(N_DEV, D_ADA), me * (D_ADA // N_DEV), D_ADA // N_DEV, axis=1)
    g_w_ada, d_w_ada, nm_w_ada, nv_w_ada = _ada_adamw(sc_all.T, d_ada_local, w_ada[0], m_w_ada[0], v_w_ada[0])

    def ordered(w_ada_, b_ada_, w_in_, w_out_, s):
        return (w_ada_[None], b_ada_, w_in_[None], s["b_in"], s["w_pool"], s["b_pool"], s["pool_scale"],
                w_out_[None], s["b_out"], s["ln_g"], s["ln_b"])

    return (loss, dx[None],
            *ordered(g_w_ada, g_b_ada, g_w_in, g_w_out, g_s),
            *ordered(d_w_ada, d_b_ada, d_w_in, d_w_out, d_s),
            *ordered(nm_w_ada, nm_b_ada, nm_w_in, nm_w_out, nm_s),
            *ordered(nv_w_ada, nv_b_ada, nv_w_in, nv_w_out, nv_s))
```

```python
import jax
import jax.numpy as jnp
from jax import lax
from jax.experimental import pallas as pl
from jax.experimental.pallas import tpu as pltpu

F32 = jnp.float32
BF16 = jnp.bfloat16

N_DEV = 8
D = 1024
N_HEADS = 8
HEAD_DIM = 64
D_ATT = 512
D_POOL = 512
POOL_WINDOWS = (2, 4, 8, 16)
GROUP_DIM = 128
HALO = 16
LANE = 128
D_IN = 3080
D_ADA = 3072
N_MAIN = 3072
OFF_P = 1536
COL_CHUNK = 512
Q_SCALE = 0.125
LN_EPS = 1e-5
ALPHA = 2.0 ** 0.25
L_CQ, L_CK, L_LSE = 64, 67, 70

ADAM_LR, ADAM_B1, ADAM_B2, ADAM_EPS, ADAM_WD, ADAM_STEP = 0.001, 0.9, 0.999, 1e-08, 0.01, 10
VMEM_LIMIT = 56 * 1024 * 1024

MESH = pl.DeviceIdType.MESH
ANY = pl.BlockSpec(memory_space=pl.ANY)


def _params(sem=None, vmem=VMEM_LIMIT):
    return pltpu.CompilerParams(dimension_semantics=sem, vmem_limit_bytes=vmem)


def _split3(a):
    hi = a.astype(BF16)
    r = a - hi.astype(F32)
    mid = r.astype(BF16)
    lo = (r - mid.astype(F32)).astype(BF16)
    return hi, mid, lo


def _dot(a, b):
    return jnp.dot(a, b, preferred_element_type=F32)


def _dot_nt(a, b):
    return lax.dot_general(a, b, (((1,), (1,)), ((), ())), preferred_element_type=F32)


def _dot_tn(a, b):
    return lax.dot_general(a, b, (((0,), (0,)), ((), ())), preferred_element_type=F32)


def _dot3(m01, a):
    hi, mid, lo = _split3(a)
    return _dot(m01, hi) + _dot(m01, mid) + _dot(m01, lo)


def _sigmoid(z):
    return 1.0 / (1.0 + jnp.exp(-z))


def _lanes(shape):
    return lax.broadcasted_iota(jnp.int32, shape, len(shape) - 1)


def _place3(lane, base, parts, other):
    out = other
    for j in range(3):
        out = jnp.where(lane == base + j, parts[j], out)
    return out


def _mesh_pos():
    return lax.axis_index("x"), lax.axis_index("y"), lax.axis_index("c")


def _dev_index(px, py, pc):
    return 4 * px + 2 * py + pc


N_GATHER_SEMS = 9


def _gather_stages(src_ref, out_ref, send_sems, recv_sems, local_sem):
    x, y, c = _mesh_pos()
    me, sibling = (x, y, c), (x, y, 1 - c)
    nbr_x, nbr_y, diag = (1 - x, y), (x, 1 - y), (1 - x, 1 - y)
    half = out_ref.shape[-1] // 2
    left, right = pl.ds(0, half), pl.ds(half, half)

    def copy(k, block, to, cols=None, src=None):
        slot = out_ref.at[_dev_index(*block)]
        if cols is not None:
            slot = slot.at[:, cols]
        return pltpu.make_async_remote_copy(
            src_ref=slot if src is None else src, dst_ref=slot, send_sem=send_sems.at[k], recv_sem=recv_sems.at[k],
            device_id=to, device_id_type=MESH)

    mine = pltpu.make_async_copy(src_ref, out_ref.at[_dev_index(*me)], local_sem)
    first = [copy(0, me, sibling, src=src_ref), copy(1, me, (*nbr_x, c), src=src_ref), copy(2, me, (*nbr_y, c), src=src_ref)]
    relay = [(1, nbr_x, None, nbr_x), (2, nbr_y, None, nbr_y), (3, diag, left, nbr_y), (4, diag, right, nbr_x)]
    onward = [copy(3, (*nbr_x, c), (*nbr_y, c), cols=left), copy(4, (*nbr_y, c), (*nbr_x, c), cols=right)]
    passed = [copy(4 + k, (*block, c), sibling, cols=cols) for k, block, cols, _ in relay]

    def start():
        mine.start()
        for cp in first:
            cp.start()

    def relay_stage(first_item):
        def run():
            for j in (first_item, first_item + 1):
                k, block, cols, frm = relay[j]
                copy(k, (*block, c), (*frm, c), cols=cols).wait_recv()
                if j < 2:
                    onward[j].start()
                passed[j].start()
        return run

    def finish():
        copy(0, sibling, me).wait_recv()
        for k, block, cols, _ in relay:
            copy(4 + k, (*block, 1 - c), me, cols=cols).wait_recv()
        for cp in first + onward + passed:
            cp.wait_send()
        mine.wait()

    return start, relay_stage(0), relay_stage(2), finish


N_REDUCE_SEMS = 10
N_SMALL_SEMS = 4
N_ROWS_SEMS = 7


def _reduce_stages(ins, gs, r1, s2, r2, smalls, send_sems, recv_sems, rows=None, own=None):
    n = len(ins)
    x, y, c = _mesh_pos()
    me = _dev_index(x, y, c)
    sibling = (x, y, 1 - c)
    chips = [(x, y), (1 - x, y), (x, 1 - y), (1 - x, 1 - y)]
    peers = []
    for p in range(1, N_DEV):
        px, py, pc = (p >> 2) & 1, (p >> 1) & 1, p & 1
        peers.append((1 - x if px else x, 1 - y if py else y, 1 - c if pc else c))
    base_small = N_REDUCE_SEMS * n

    def remote(src, dst, k, to):
        return pltpu.make_async_remote_copy(src_ref=src, dst_ref=dst, send_sem=send_sems.at[k],
                                            recv_sem=recv_sems.at[k], device_id=to, device_id_type=MESH)

    def level1(a, q):
        return remote(ins[a].at[_dev_index(*chips[q], 1 - c)], r1[a].at[q], N_REDUCE_SEMS * a + q, sibling)

    def level2(a, k):
        half = ins[a].shape[-1] // 2
        left, right = pl.ds(0, half), pl.ds(half, half)
        nbr_x, nbr_y = (*chips[1], c), (*chips[2], c)
        src_slot, dst_slot, cols, to = [(0, 0, left, nbr_x), (1, 1, right, nbr_y), (2, 2, left, nbr_x),
                                        (2, 2, right, nbr_y), (0, 0, right, nbr_x), (1, 1, left, nbr_y)][k]
        return remote(s2[a].at[src_slot, :, cols], r2[a].at[dst_slot, :, cols], N_REDUCE_SEMS * a + 4 + k, to)

    to_sibling = [remote(sm[0], sm[2], base_small + 4 * i, sibling) for i, sm in enumerate(smalls)]
    to_chips = [[remote(sm[3], sm[4].at[j], base_small + 4 * i + 1 + j, (*chips[j + 1], c)) for j in range(3)]
                for i, sm in enumerate(smalls)]
    if rows is not None:
        rows_ref, land_ref, all_ref = rows
        base_rows = base_small + 4 * len(smalls)
        row_sends = [remote(rows_ref, land_ref.at[me], base_rows + k, to) for k, to in enumerate(peers)]

    def mine(a, q):
        buf, sems = own[a]
        return pltpu.make_async_copy(ins[a].at[_dev_index(*chips[q], c)], buf.at[q], sems.at[q])

    def start():
        for a in range(n):
            for q in range(4):
                level1(a, q).start()
            if own is not None:
                for q in (1, 2, 3, 0):
                    mine(a, q).start()
        for cp in to_sibling:
            cp.start()
        if rows is not None:
            for cp in row_sends:
                cp.start()
            land_ref[me] = rows_ref[...]

    def middle():
        for a in range(n):
            for q in (1, 2, 3, 0):
                level1(a, q).wait_recv()
                if own is None:
                    kept = ins[a][_dev_index(*chips[q], c)]
                else:
                    mine(a, q).wait()
                    kept = own[a][0][q]
                pair = kept.astype(F32) + r1[a][q].astype(F32)
                if q == 0:
                    gs[a][...] = pair
                else:
                    s2[a][q - 1] = pair.astype(BF16)
                    for k in ((0,), (1,), (2, 3))[q - 1]:
                        level2(a, k).start()
        for i, (small_ref, _, sm_sib, sm_chip, _) in enumerate(smalls):
            to_sibling[i].wait_recv()
            sm_chip[...] = small_ref[...] + sm_sib[...]
            for cp in to_chips[i]:
                cp.start()

    def fold():
        for a in range(n):
            half = ins[a].shape[-1] // 2
            level2(a, 3).wait_recv()
            s2[a][0, :, half:] = (s2[a][0, :, half:].astype(F32) + r2[a][2, :, half:].astype(F32)).astype(BF16)
            level2(a, 4).start()
            level2(a, 2).wait_recv()
            s2[a][1, :, :half] = (s2[a][1, :, :half].astype(F32) + r2[a][2, :, :half].astype(F32)).astype(BF16)
            level2(a, 5).start()

    def finish():
        for a in range(n):
            for k in (0, 1, 4, 5):
                level2(a, k).wait_recv()
            gs[a][...] = gs[a][...] + r2[a][0].astype(F32) + r2[a][1].astype(F32)
            for q in range(4):
                level1(a, q).wait_send()
            for k in range(6):
                level2(a, k).wait_send()
        for i, (_, total_ref, _, sm_chip, sm_recv) in enumerate(smalls):
            for cp in to_chips[i]:
                cp.wait_recv()
            total = None
            for ax in range(2):
                for ay in range(2):
                    dx, dy = x != ax, y != ay
                    term = jnp.where(dx, jnp.where(dy, sm_recv[2], sm_recv[0]), jnp.where(dy, sm_recv[1], sm_chip[...]))
                    total = term if total is None else total + term
            total_ref[...] = total
            for cp in [to_sibling[i]] + to_chips[i]:
                cp.wait_send()
        if rows is not None:
            for k, frm in enumerate(peers):
                remote(rows_ref, land_ref.at[_dev_index(*frm)], base_rows + k, frm).wait_recv()
            all_ref[...] = land_ref[...]
            for cp in row_sends:
                cp.wait_send()

    return start, middle, fold, finish


def _reduce_scratch(shard, smalls, rows=None):
    out = [pltpu.VMEM((lead,) + shard.shape[1:], BF16) for lead in (4, 3, 3)]
    for small in smalls:
        out += [pltpu.VMEM(small.shape, F32), pltpu.VMEM(small.shape, F32), pltpu.VMEM((3,) + small.shape, F32)]
    n_sems = N_REDUCE_SEMS + N_SMALL_SEMS * len(smalls)
    if rows is not None:
        out.append(pltpu.VMEM((N_DEV,) + rows.shape, F32))
        n_sems += N_ROWS_SEMS
    return out + [pltpu.SemaphoreType.DMA((n_sems,))] * 2


def _reduce_grads(gw_in, small, rows):
    def body(in_ref, small_ref, rows_ref, g_ref, total_ref, rows_all_ref,
             r1, s2, r2, sm_sib, sm_chip, sm_recv, rows_land, send_sems, recv_sems, kept, kept_sems):
        stages = _reduce_stages(
            [in_ref], [g_ref], [r1], [s2], [r2], [(small_ref, total_ref, sm_sib, sm_chip, sm_recv)],
            send_sems, recv_sems, rows=(rows_ref, rows_land, rows_all_ref), own=[(kept, kept_sems)])
        for stage in stages:
            stage()

    vmem = pl.BlockSpec(memory_space=pltpu.VMEM)
    return pl.pallas_call(
        body, name="reduce_grads",
        in_specs=[ANY, vmem, vmem], out_specs=[vmem, vmem, vmem],
        out_shape=[jax.ShapeDtypeStruct(gw_in.shape[1:], F32), jax.ShapeDtypeStruct(small.shape, F32),
                   jax.ShapeDtypeStruct((N_DEV,) + rows.shape, F32)],
        scratch_shapes=_reduce_scratch(gw_in, [small], rows)
        + [pltpu.VMEM((4,) + gw_in.shape[1:], BF16), pltpu.SemaphoreType.DMA((4,))],
        compiler_params=_params(),
    )(gw_in, small, rows)


def _dot3_rhs(a, b):
    a0, a1, a2 = _split3(a)
    b0, b1, b2 = _split3(b)
    return (_dot(a0, b0) + (_dot(a0, b1) + _dot(a1, b0))
            + (_dot(a0, b2) + _dot(a1, b1) + _dot(a2, b0)))


def _gather_and_ada(c, w_in_rows, w_ada):
    cols = w_ada.shape[1]

    def body(c_ref, w_ref, wa_ref, w_all_ref, sc_ref, ada_ref,
             c_land, part, ada_land, send_sems, recv_sems, local_sem, x_send, x_recv):
        x, y, cc = _mesh_pos()
        me = _dev_index(x, y, cc)
        peers = []
        for p in range(1, N_DEV):
            px, py, pc = (p >> 2) & 1, (p >> 1) & 1, p & 1
            peers.append((1 - x if px else x, 1 - y if py else y, 1 - cc if pc else cc))

        def remote(src, dst, k, to):
            return pltpu.make_async_remote_copy(src_ref=src, dst_ref=dst, send_sem=x_send.at[k], recv_sem=x_recv.at[k],
                                                device_id=to, device_id_type=MESH)

        c_sends = [remote(c_ref, c_land.at[me], k, to) for k, to in enumerate(peers)]
        for cp in c_sends:
            cp.start()
        start, relay_near, relay_far, finish = _gather_stages(w_ref, w_all_ref, send_sems, recv_sems, local_sem.at[0])
        start()
        c_land[me] = c_ref[...]
        for k, frm in enumerate(peers):
            remote(c_ref, c_land.at[_dev_index(*frm)], k, frm).wait_recv()
        c_all = jnp.concatenate([c_land[b] for b in range(N_DEV)], axis=0)
        sc = c_all * _sigmoid(c_all)
        sc_ref[...] = sc
        rows = _dot3_rhs(sc, wa_ref[...])
        for b in range(N_DEV):
            part[b] = rows[b:b + 1, :]
        a_sends = [remote(part.at[_dev_index(*to)], ada_land.at[me], 7 + k, to) for k, to in enumerate(peers)]
        for cp in a_sends:
            cp.start()
        ada_land[me] = part[me]
        for k, frm in enumerate(peers):
            remote(part.at[0], ada_land.at[_dev_index(*frm)], 7 + k, frm).wait_recv()
        ada_ref[...] = ada_land[...]

        relay_near()
        relay_far()
        finish()
        for cp in c_sends + a_sends:
            cp.wait_send()

    vmem = pl.BlockSpec(memory_space=pltpu.VMEM)
    return pl.pallas_call(
        body, name="gather_weights",
        in_specs=[vmem, ANY, vmem], out_specs=[ANY, vmem, vmem],
        out_shape=[jax.ShapeDtypeStruct((N_DEV,) + w_in_rows.shape, w_in_rows.dtype),
                   jax.ShapeDtypeStruct((N_DEV, D), F32), jax.ShapeDtypeStruct((N_DEV, 1, cols), F32)],
        scratch_shapes=[pltpu.VMEM((N_DEV, 1, D), F32), pltpu.VMEM((N_DEV, 1, cols), F32), pltpu.VMEM((N_DEV, 1, cols), F32),
                        pltpu.SemaphoreType.DMA((N_GATHER_SEMS,)), pltpu.SemaphoreType.DMA((N_GATHER_SEMS,)),
                        pltpu.SemaphoreType.DMA((1,)),
                        pltpu.SemaphoreType.DMA((14,)), pltpu.SemaphoreType.DMA((14,))],
        compiler_params=_params(),
    )(c, w_in_rows, w_ada)


def _inproj_forward(x, mod, w_main, w_f, b_main, b_f, tile):
    seq = x.shape[0]
    nt = seq // tile

    def body(x_ref, mod_ref, w_ref, wf_ref, b_ref, bf_ref,
             qp_ref, kp_ref, vp_ref, f_ref, p_ref, ga_ref, gp_ref, u_ref, carry_ref):
        i = pl.program_id(0)

        @pl.when(i == 0)
        def _():
            carry_ref[...] = jnp.zeros_like(carry_ref)

        u = x_ref[...] * mod_ref[0:1, :] + mod_ref[1:2, :]
        ub = u.astype(BF16)
        u_ref[...] = ub

        f = _dot_nt(ub, wf_ref[...]) + bf_ref[...]
        f_ref[...] = f
        lane = _lanes((tile, LANE))
        log_f = jnp.where(lane < N_HEADS, jnp.minimum(f, 0.0) - jnp.log(1.0 + jnp.exp(-jnp.abs(f))), 0.0)
        row = lax.broadcasted_iota(jnp.int32, (tile, tile), 0)
        col = lax.broadcasted_iota(jnp.int32, (tile, tile), 1)
        tri = (row >= col).astype(BF16)
        cum = _dot3(tri, log_f) + carry_ref[0:1, :]
        carry_ref[0:1, :] = cum[tile - 1:tile, :]
        cq = [part.astype(F32) for part in _split3(cum)]
        ck = [part.astype(F32) for part in _split3(-cum)]

        def proj(chunk):
            cols = pl.ds(chunk * COL_CHUNK, COL_CHUNK)
            return _dot_nt(ub, w_ref[cols, :]) + b_ref[:, cols]

        def head_tiles(r):
            for pair in range(N_HEADS // 2):
                both = r[:, pair * LANE:(pair + 1) * LANE]
                yield 2 * pair, both
                yield 2 * pair + 1, pltpu.roll(both, HEAD_DIM, 1)

        for h, val in head_tiles(proj(0)):
            extra = jnp.where((lane >= L_CK) & (lane < L_CK + 3), 1.0, 0.0)
            extra = _place3(lane, L_CQ, [part[:, h:h + 1] for part in cq], extra)
            qp_ref[h] = jnp.where(lane < HEAD_DIM, val * Q_SCALE, extra).astype(BF16)
        for h, val in head_tiles(proj(1)):
            ones = ((lane >= L_CQ) & (lane < L_CQ + 3)) | ((lane >= L_LSE) & (lane < L_LSE + 3))
            extra = _place3(lane, L_CK, [part[:, h:h + 1] for part in ck], jnp.where(ones, 1.0, 0.0))
            kp_ref[h] = jnp.where(lane < HEAD_DIM, val, extra).astype(BF16)
        for h, val in head_tiles(proj(2)):
            extra = jnp.where((lane >= HEAD_DIM) & (lane < HEAD_DIM + 3), -1.0, 0.0)
            vp_ref[h] = jnp.where(lane < HEAD_DIM, val, extra).astype(BF16)
        p_ref[...] = proj(3)
        ga_ref[...] = proj(4)
        gp_ref[...] = proj(5)

    head_block = pl.BlockSpec((N_HEADS, tile, LANE), lambda i: (0, i, 0))
    tok = lambda width: pl.BlockSpec((tile, width), lambda i: (i, 0))
    whole = lambda a: pl.BlockSpec(a.shape, lambda i: (0,) * a.ndim)
    padded = jax.ShapeDtypeStruct((N_HEADS, seq, LANE), BF16)
    half = jax.ShapeDtypeStruct((seq, D_ATT), F32)
    return pl.pallas_call(
        body, name="inproj_forward", grid=(nt,),
        in_specs=[tok(D), whole(mod), whole(w_main), whole(w_f), whole(b_main), whole(b_f)],
        out_specs=[head_block, head_block, head_block, tok(LANE), tok(D_POOL), tok(D_ATT), tok(D_POOL),
                   tok(D)],
        out_shape=[padded, padded, padded, jax.ShapeDtypeStruct((seq, LANE), F32), half, half, half,
                   jax.ShapeDtypeStruct((seq, D), BF16)],
        scratch_shapes=[pltpu.VMEM((8, LANE), F32)],
        compiler_params=_params(("arbitrary",)),
    )(x, mod, w_main, w_f, b_main, b_f)


def _attention_forward(qp, kp, vp, w_out, tile):
    seq = qp.shape[1]
    nb = seq // tile
    steps = (N_HEADS // 2) * nb

    def body(q_ref, k_ref, v_ref, wo_ref, att_ref, q2t_ref, wo_all_ref, s_a, s_b, m_ref, acc_ref,
             send_sems, recv_sems, local_sem):
        step = pl.program_id(0) * nb + pl.program_id(1)
        start, relay_near, relay_far, finish = _gather_stages(wo_ref, wo_all_ref, send_sems, recv_sems, local_sem.at[0])
        pl.when(step == 0)(start)
        pl.when(step == steps // 4)(relay_near)
        pl.when(step == (3 * steps) // 4)(relay_far)

        i = pl.program_id(1)
        sub = lax.broadcasted_iota(jnp.int32, (LANE, tile), 0)
        row = lax.broadcasted_iota(jnp.int32, (tile, tile), 0)
        col = lax.broadcasted_iota(jnp.int32, (tile, tile), 1)
        q = [q_ref[0], q_ref[1]]

        def scores(buf, kb):
            rows = pl.ds(pl.multiple_of(kb * tile, tile), tile)
            for hh in range(2):
                buf[hh] = _dot_nt(k_ref[hh, rows, :], q[hh])

        def absorb(buf, kb, masked):
            rows = pl.ds(pl.multiple_of(kb * tile, tile), tile)
            for hh in range(2):
                m = m_ref[hh, 0:1, :]
                s = buf[hh]
                if masked:
                    s = jnp.where(row <= col, s, -1e30)
                m_new = jnp.maximum(m, jnp.max(s, axis=0, keepdims=True))
                p = jnp.exp(s - m_new).astype(BF16)
                acc_ref[hh] = jnp.exp(m - m_new) * acc_ref[hh] + _dot_tn(v_ref[hh, rows, :], p)
                m_ref[hh, 0:1, :] = m_new

        def two_blocks(j, _):
            scores(s_b, 2 * j + 1)
            absorb(s_a, 2 * j, False)
            scores(s_a, 2 * j + 2)
            absorb(s_b, 2 * j + 1, False)
            return 0

        def last_block():
            absorb(s_a, i, True)

        def last_two_blocks():
            scores(s_b, i)
            absorb(s_a, i - 1, False)
            absorb(s_b, i, True)

        scores(s_a, 0)
        m_ref[...] = jnp.full(m_ref.shape, -1e30, F32)
        acc_ref[...] = jnp.zeros_like(acc_ref)
        lax.fori_loop(0, i // 2, two_blocks, 0)
        lax.cond(i % 2 == 0, last_block, last_two_blocks)
        outs = []
        for hh in range(2):
            m, acc = m_ref[hh, 0:1, :], acc_ref[hh]
            l = -acc[HEAD_DIM:HEAD_DIM + 1, :]
            outs.append((acc / l)[:HEAD_DIM, :])
            neg_lse = [part.astype(F32) for part in _split3(-(m + jnp.log(l)))]
            q2t_ref[hh] = _place3(sub, L_LSE, neg_lse, q[hh].astype(F32).T).astype(BF16)
        att_ref[...] = jnp.concatenate(outs, axis=0).T
        pl.when(step == steps - 1)(finish)

    pair = pl.BlockSpec((2, tile, LANE), lambda hp, i: (hp, i, 0))
    full = pl.BlockSpec((2, seq, LANE), lambda hp, i: (hp, 0, 0))
    return pl.pallas_call(
        body, name="attention_forward", grid=(N_HEADS // 2, nb),
        in_specs=[pair, full, full, ANY],
        out_specs=[pl.BlockSpec((tile, LANE), lambda hp, i: (i, hp)),
                   pl.BlockSpec((2, LANE, tile), lambda hp, i: (hp, 0, i)), ANY],
        out_shape=[jax.ShapeDtypeStruct((seq, D_ATT), F32),
                   jax.ShapeDtypeStruct((N_HEADS, LANE, seq), BF16),
                   jax.ShapeDtypeStruct((N_DEV,) + w_out.shape, w_out.dtype)],
        scratch_shapes=[pltpu.VMEM((2, tile, tile), F32), pltpu.VMEM((2, tile, tile), F32),
                        pltpu.VMEM((2, 8, tile), F32), pltpu.VMEM((2, LANE, tile), F32),
                        pltpu.SemaphoreType.DMA((N_GATHER_SEMS,)), pltpu.SemaphoreType.DMA((N_GATHER_SEMS,)),
                        pltpu.SemaphoreType.DMA((1,))],
        compiler_params=_params(("arbitrary", "arbitrary")),
    )(qp, kp, vp, w_out)


def _window_sum(x, halo, window, transposed):
    tile = x.shape[0]

    def split_cat(a):
        hi = a.astype(BF16)
        return jnp.concatenate([hi, (a - hi.astype(F32)).astype(BF16)], axis=1)

    def fold(r):
        return r[:, :LANE] + r[:, LANE:]

    r = lax.broadcasted_iota(jnp.int32, (tile, tile), 0)
    c = lax.broadcasted_iota(jnp.int32, (tile, tile), 1)
    rh = lax.broadcasted_iota(jnp.int32, (HALO, HALO), 0)
    ch = lax.broadcasted_iota(jnp.int32, (HALO, HALO), 1)
    if not transposed:
        band = (c <= r) & (r - c < window)
        edge = (rh + HALO - ch) < window
    else:
        band = (r <= c) & (c - r < window)
        edge = (HALO + ch - rh) < window
    out = fold(_dot(band.astype(BF16), split_cat(x)))
    reach = fold(_dot(edge.astype(BF16), split_cat(halo)))
    if not transposed:
        return jnp.concatenate([out[:HALO] + reach, out[HALO:]], axis=0)
    return jnp.concatenate([out[:tile - HALO], out[tile - HALO:] + reach], axis=0)


def _silu_parts(g):
    sig = _sigmoid(g)
    return g * sig, sig * (1.0 + g * (1.0 - sig))


def _middle(x, tgt, att, g_att, g_pool, p, vecs, pool_vecs, w_out, w_pool, tile):
    seq = x.shape[0]
    nt = seq // tile
    halo_blocks = tile // HALO

    def body(x_ref, tgt_ref, att_ref, ga_ref, gp_ref, p_ref, ph_ref, vec_ref, pvec_ref, wo_ref, wp_ref,
             dxa_ref, do2_ref, dga_ref, dgp_ref, dpooled_ref, gwo_ref, dwp_ref, dvec_ref, dwo_ref, dpvec_ref):
        i = pl.program_id(0)

        @pl.when(i == 0)
        def _():
            dwo_ref[...] = jnp.zeros_like(dwo_ref)
            dwp_ref[...] = jnp.zeros_like(dwp_ref)
            dvec_ref[...] = jnp.zeros_like(dvec_ref)
            dpvec_ref[...] = jnp.zeros_like(dpvec_ref)

        gate, b_out, ln_g, ln_b = (vec_ref[k:k + 1, :] for k in range(4))
        b_pool, pool_scale = pvec_ref[0:1, :], pvec_ref[1:2, :]
        x = x_ref[...]
        p = p_ref[...]
        p_halo = ph_ref[...] * jnp.where(i > 0, 1.0, 0.0)
        pos = i * tile + lax.broadcasted_iota(jnp.int32, (tile, 1), 0) + 1

        pooled, mixed = [], []
        for g, window in enumerate(POOL_WINDOWS):
            cols = slice(g * GROUP_DIM, (g + 1) * GROUP_DIM)
            wsum = _window_sum(p[:, cols], p_halo[:, cols], window, False)
            count = jnp.minimum(pos, window).astype(F32)
            pooled.append(wsum / count - p[:, cols])
            mixed.append(_dot(pooled[g].astype(BF16), wp_ref[g]) + b_pool[:, cols])
        mixed = jnp.concatenate(mixed, axis=1)
        pool = mixed * pool_scale

        att = att_ref[...]
        g_att, g_pool = ga_ref[...], gp_ref[...]
        silu_a, dsilu_a = _silu_parts(g_att)
        silu_p, dsilu_p = _silu_parts(g_pool)
        y_in = jnp.concatenate([att * silu_a, pool * silu_p], axis=1)
        y = _dot(y_in.astype(BF16), wo_ref[...]) + b_out
        h = ALPHA * x + gate * y
        mu = jnp.mean(h, axis=1, keepdims=True)
        hc = h - mu
        var = jnp.mean(hc * hc, axis=1, keepdims=True)
        rstd = lax.rsqrt(var + LN_EPS)
        yhat = hc * rstd
        diff = yhat * ln_g + ln_b - tgt_ref[...]
        loss_rows = jnp.sum(diff * diff, axis=1, keepdims=True)
        d_out = diff * (1.0 / D)

        d_yhat = d_out * ln_g
        dh = rstd * (d_yhat - jnp.mean(d_yhat, axis=1, keepdims=True)
                     - yhat * jnp.mean(d_yhat * yhat, axis=1, keepdims=True))
        dxa_ref[...] = ALPHA * dh
        dy = dh * gate
        dyb = dy.astype(BF16)
        lane = _lanes((1, D))
        loss_row = jnp.where(lane == 0, (0.5 / D) * jnp.sum(loss_rows, axis=0, keepdims=True), 0.0)
        dvec_ref[5:6, :] += jnp.sum(dh * y, axis=0, keepdims=True)
        dvec_ref[0:1, :] += jnp.sum(dy, axis=0, keepdims=True)
        dvec_ref[1:2, :] += jnp.sum(d_out * yhat, axis=0, keepdims=True)
        dvec_ref[2:3, :] += jnp.sum(d_out, axis=0, keepdims=True)
        dvec_ref[4:5, :] += loss_row

        dwo_ref[...] += _dot(y_in.T.astype(BF16), dyb)
        d_yin = _dot_nt(dyb, wo_ref[...])
        d_a, d_pl = d_yin[:, :D_ATT], d_yin[:, D_ATT:]
        d_att = d_a * silu_a
        d_att_t = d_att.T
        prod_t = (d_att * att).T
        sub = lax.broadcasted_iota(jnp.int32, (HEAD_DIM, tile), 0)
        for h in range(N_HEADS):
            rows = slice(h * HEAD_DIM, (h + 1) * HEAD_DIM)
            delta = jnp.sum(prod_t[rows], axis=0, keepdims=True)
            extra = _place3(sub, 0, [part.astype(F32) for part in _split3(delta)], 0.0)
            do2_ref[h] = jnp.concatenate([d_att_t[rows], extra], axis=0).astype(BF16)
        dga_ref[...] = d_a * att * dsilu_a
        dgp_ref[...] = d_pl * pool * dsilu_p
        d_pool = d_pl * silu_p
        d_mixed = d_pool * pool_scale
        dpvec_ref[0:1, :] += jnp.sum(d_mixed, axis=0, keepdims=True)
        dpvec_ref[1:2, :] += jnp.sum(d_pool * mixed, axis=0, keepdims=True)
        d_pooled = []
        for g in range(len(POOL_WINDOWS)):
            cols = slice(g * GROUP_DIM, (g + 1) * GROUP_DIM)
            dmb = d_mixed[:, cols].astype(BF16)
            dwp_ref[g] += _dot(pooled[g].T.astype(BF16), dmb)
            d_pooled.append(_dot_nt(dmb, wp_ref[g]))
        dpooled_ref[...] = jnp.concatenate(d_pooled, axis=1)

        @pl.when(i == nt - 1)
        def _():
            gwo_ref[...] = dwo_ref[...].astype(BF16)
            dvec_ref[3:4, :] = jnp.concatenate([dpvec_ref[0:1, :], dpvec_ref[1:2, :]], axis=1)

    tok = lambda width: pl.BlockSpec((tile, width), lambda i: (i, 0))
    whole = lambda a: pl.BlockSpec(a.shape, lambda i: (0,) * a.ndim)
    halo = pl.BlockSpec((HALO, D_POOL), lambda i: (jnp.maximum(i * halo_blocks - 1, 0), 0))
    half = jax.ShapeDtypeStruct((seq, D_ATT), F32)
    outs = [jax.ShapeDtypeStruct((seq, D), F32), jax.ShapeDtypeStruct((N_HEADS, LANE, seq), BF16), half, half, half,
            jax.ShapeDtypeStruct(w_out.shape, BF16), jax.ShapeDtypeStruct(w_pool.shape, F32),
            jax.ShapeDtypeStruct(vecs.shape, F32)]
    return pl.pallas_call(
        body, name="middle", grid=(nt,),
        in_specs=[tok(D), tok(D), tok(D_ATT), tok(D_ATT), tok(D_POOL), tok(D_POOL), halo,
                  whole(vecs), whole(pool_vecs), whole(w_out), whole(w_pool)],
        out_specs=[tok(D), pl.BlockSpec((N_HEADS, LANE, tile), lambda i: (0, 0, i)),
                   tok(D_ATT), tok(D_POOL), tok(D_POOL),
                   whole(w_out), whole(w_pool), whole(vecs)],
        out_shape=outs,
        scratch_shapes=[pltpu.VMEM(w_out.shape, F32), pltpu.VMEM(pool_vecs.shape, F32)],
        compiler_params=_params(("arbitrary",)),
    )(x, tgt, att, g_att, g_pool, p, p, vecs, pool_vecs, w_out, w_pool)


def _attention_backward(q2t, kp, vp, do2t, gw_out, vecs, pool, tile):
    seq = kp.shape[1]
    nb = seq // tile
    last = N_HEADS // 2 - 1

    def body(qt_ref, k_ref, v_ref, dot_ref, gwo_hbm, vecs_hbm, pool_hbm,
             dq_ref, dk_ref, dv_ref, dcum_ref, g_out_ref, vecs_sum_ref, pool_sum_ref,
             dq_acc, dk_acc, dv_acc, gwo_ref, vecs_ref, pool_ref,
             r1, s2, r2, v_sib, v_chip, v_recv, p_sib, p_chip, p_recv, send_sems, recv_sems):
        hp = pl.program_id(0)
        start, middle, fold, finish = _reduce_stages(
            [gwo_ref], [g_out_ref], [r1], [s2], [r2],
            [(vecs_ref, vecs_sum_ref, v_sib, v_chip, v_recv), (pool_ref, pool_sum_ref, p_sib, p_chip, p_recv)],
            send_sems, recv_sems)

        @pl.when(hp == 0)
        def _():
            pltpu.sync_copy(gwo_hbm, gwo_ref)
            pltpu.sync_copy(vecs_hbm, vecs_ref)
            pltpu.sync_copy(pool_hbm, pool_ref)
            start()

        pl.when(hp == 1)(middle)
        pl.when(hp == 2)(fold)

        row = lax.broadcasted_iota(jnp.int32, (tile, tile), 0)
        col = lax.broadcasted_iota(jnp.int32, (tile, tile), 1)
        dq_acc[...] = jnp.zeros_like(dq_acc)

        def kv_block(kb, _):
            krows = pl.ds(pl.multiple_of(kb * tile, tile), tile)
            k = [k_ref[hh, krows, :] for hh in range(2)]
            v = [v_ref[hh, krows, :] for hh in range(2)]
            k_t = [k[hh].T for hh in range(2)]

            def q_block(qb, masked):
                qcols = pl.ds(pl.multiple_of(qb * tile, tile), tile)
                for hh in range(2):
                    q_t = qt_ref[hh, :, qcols]
                    do_t = dot_ref[hh, :, qcols]
                    s_t = _dot(k[hh], q_t)
                    if masked:
                        s_t = jnp.where(row <= col, s_t, -1e30)
                    p_t = jnp.exp(s_t)
                    ds_t = (p_t * _dot(v[hh], do_t)).astype(BF16)
                    dv_new = _dot_nt(do_t, p_t.astype(BF16))
                    dk_new = _dot_nt(q_t, ds_t)
                    if masked:
                        dv_acc[hh], dk_acc[hh] = dv_new, dk_new
                    else:
                        dv_acc[hh] += dv_new
                        dk_acc[hh] += dk_new
                    dq_acc[hh, :, qcols] += _dot(k_t[hh], ds_t)

            q_block(kb, True)

            def two_later_blocks(j, _):
                q_block(kb + 1 + 2 * j, False)
                q_block(kb + 2 + 2 * j, False)
                return 0

            later = nb - 1 - kb
            lax.fori_loop(0, later // 2, two_later_blocks, 0)
            pl.when(later % 2 == 1)(lambda: q_block(nb - 1, False))
            for hh in range(2):
                dk = dk_acc[hh]
                dk_ref[hh, :, krows] = dk.astype(BF16)
                dv_ref[hh, :, krows] = dv_acc[hh].astype(BF16)
                dcum_ref[hh, :, krows] = -dk[L_CK:L_CK + 1, :]
            return 0

        lax.fori_loop(0, nb, kv_block, 0)
        for hh in range(2):
            dq = dq_acc[hh]
            dcum_ref[hh] += dq[L_CQ:L_CQ + 1, :]
            dq_ref[hh] = (dq * Q_SCALE).astype(BF16)
        pl.when(hp == last)(finish)

    pair = pl.BlockSpec((2, seq, LANE), lambda hp: (hp, 0, 0))
    pair_t = pl.BlockSpec((2, LANE, seq), lambda hp: (hp, 0, 0))
    whole = lambda shape: pl.BlockSpec(shape, lambda hp: (0,) * len(shape))
    grad = jax.ShapeDtypeStruct((N_HEADS, LANE, seq), BF16)
    return pl.pallas_call(
        body, name="attention_backward", grid=(N_HEADS // 2,),
        in_specs=[pair_t, pair, pair, pair_t, ANY, ANY, ANY],
        out_specs=[pair_t, pair_t, pair_t, pl.BlockSpec((2, 1, seq), lambda hp: (hp, 0, 0)),
                   whole(gw_out.shape[1:]), whole(vecs.shape), whole(pool.shape)],
        out_shape=[grad, grad, grad, jax.ShapeDtypeStruct((N_HEADS, 1, seq), F32),
                   jax.ShapeDtypeStruct(gw_out.shape[1:], F32), jax.ShapeDtypeStruct(vecs.shape, F32),
                   jax.ShapeDtypeStruct(pool.shape, F32)],
        scratch_shapes=[pltpu.VMEM((2, LANE, seq), F32), pltpu.VMEM((2, LANE, tile), F32),
                        pltpu.VMEM((2, LANE, tile), F32), pltpu.VMEM(gw_out.shape, BF16),
                        pltpu.VMEM(vecs.shape, F32), pltpu.VMEM(pool.shape, F32)]
        + _reduce_scratch(gw_out, [vecs, pool]),
        compiler_params=_params(("arbitrary",)),
    )(q2t, kp, vp, do2t, gw_out, vecs, pool)


def _inproj_backward(dqp, dkp, dvp, d_cum, f, d_pooled, d_ga, d_gp, x, dxa, u, mod, w_main, w_f, tile):
    seq = x.shape[0]
    nt = seq // tile
    halo_blocks = tile // HALO

    def body(dq_ref, dk_ref, dv_ref, dcum_ref, f_ref, dpo_ref, dph_ref, dga_ref, dgp_ref, x_ref, dxa_ref, u_ref,
             mod_ref, w_ref, wf_ref,
             dx_ref, dproj_ref, dwf_ref, db_ref, dbf_ref, dmod_ref, carry_ref):
        step = pl.program_id(0)
        i = nt - 1 - step

        @pl.when(step == 0)
        def _():
            carry_ref[...] = jnp.zeros_like(carry_ref)
            dwf_ref[...] = jnp.zeros_like(dwf_ref)
            db_ref[...] = jnp.zeros_like(db_ref)
            dbf_ref[...] = jnp.zeros_like(dbf_ref)
            dmod_ref[...] = jnp.zeros_like(dmod_ref)

        ones = jnp.ones((8, tile), BF16)

        def emit(chunk, val):
            cols = pl.ds(chunk * COL_CHUNK, COL_CHUNK)
            db_ref[0:1, cols] += jnp.sum(val, axis=0, keepdims=True)
            vb = val.astype(BF16)
            dproj_ref[:, pl.ds((chunk - 3) * COL_CHUNK, COL_CHUNK)] = vb
            return _dot(vb, w_ref[cols, :])

        d_u = jnp.zeros((tile, D), F32)
        for chunk, ref in enumerate((dq_ref, dk_ref, dv_ref)):
            cols = pl.ds(chunk * COL_CHUNK, COL_CHUNK)
            val_t = ref[:, 0:HEAD_DIM, :].reshape(COL_CHUNK, tile)
            db_ref[:, cols] += _dot_nt(ones, val_t)
            d_u += _dot_tn(val_t, w_ref[cols, :])

        d_pooled = dpo_ref[...]
        d_halo = dph_ref[...] * jnp.where(i < nt - 1, 1.0, 0.0)
        pos = i * tile + lax.broadcasted_iota(jnp.int32, (tile, 1), 0) + 1
        d_p = []
        for g, window in enumerate(POOL_WINDOWS):
            cols = slice(g * GROUP_DIM, (g + 1) * GROUP_DIM)
            scaled = d_pooled[:, cols] / jnp.minimum(pos, window).astype(F32)
            d_p.append(_window_sum(scaled, d_halo[:, cols] * (1.0 / window), window, True) - d_pooled[:, cols])
        d_u += emit(3, jnp.concatenate(d_p, axis=1))
        d_u += emit(4, dga_ref[...])
        d_u += emit(5, dgp_ref[...])

        row = lax.broadcasted_iota(jnp.int32, (tile, tile), 0)
        col = lax.broadcasted_iota(jnp.int32, (tile, tile), 1)
        later = (row >= col).astype(BF16)
        d_logf = sum(_dot(part, later) for part in _split3(dcum_ref[:, 0, :])) + carry_ref[:, 0:1]
        carry_ref[:, 0:1] = d_logf[:, 0:1]
        d_f = d_logf * _sigmoid(-f_ref[...].T[0:N_HEADS, :])
        d_f = jnp.concatenate([d_f, jnp.zeros((LANE - N_HEADS, tile), F32)], axis=0)
        dbf_ref[...] += sum(_dot_nt(ones, part) for part in _split3(d_f))
        d_fb = d_f.astype(BF16)
        d_u += _dot_tn(d_fb, wf_ref[...])
        dwf_ref[...] += _dot(d_fb, u_ref[...])

        x = x_ref[...]
        dx_ref[...] = dxa_ref[...] + d_u * mod_ref[0:1, :]
        dmod_ref[0:1, :] += jnp.sum(d_u * x, axis=0, keepdims=True)
        dmod_ref[1:2, :] += jnp.sum(d_u, axis=0, keepdims=True)

    rev = lambda step: nt - 1 - step
    tok = lambda width: pl.BlockSpec((tile, width), lambda s: (rev(s), 0))
    head_block = pl.BlockSpec((N_HEADS, LANE, tile), lambda s: (0, 0, rev(s)))
    whole = lambda a: pl.BlockSpec(a.shape, lambda s: (0,) * a.ndim)
    halo = pl.BlockSpec((HALO, D_POOL), lambda s: (jnp.minimum((rev(s) + 1) * halo_blocks, seq // HALO - 1), 0))
    small = lambda width: jax.ShapeDtypeStruct((8, width), F32)
    n_rest = N_MAIN - OFF_P
    return pl.pallas_call(
        body, name="inproj_backward", grid=(nt,),
        in_specs=[head_block, head_block, head_block, pl.BlockSpec((N_HEADS, 1, tile), lambda s: (0, 0, rev(s))),
                  tok(LANE), tok(D_POOL), halo, tok(D_ATT), tok(D_POOL),
                  tok(D), tok(D), tok(D),
                  whole(mod), whole(w_main), whole(w_f)],
        out_specs=[tok(D), tok(n_rest), pl.BlockSpec((LANE, D), lambda s: (0, 0)),
                   pl.BlockSpec((8, N_MAIN), lambda s: (0, 0)), pl.BlockSpec((8, LANE), lambda s: (0, 0)),
                   pl.BlockSpec((8, D), lambda s: (0, 0))],
        out_shape=[jax.ShapeDtypeStruct((seq, D), F32), jax.ShapeDtypeStruct((seq, n_rest), BF16),
                   jax.ShapeDtypeStruct((LANE, D), F32), small(N_MAIN), small(LANE), small(D)],
        scratch_shapes=[pltpu.VMEM((8, LANE), F32)],
        compiler_params=_params(("arbitrary",)),
    )(dqp, dkp, dvp, d_cum, f, d_pooled, d_pooled, d_ga, d_gp, x, dxa, u, mod, w_main, w_f)


def _weight_grads(dq_t, dk_t, dv_t, dproj, u, k_tile):
    seq = u.shape[0]
    nk = seq // k_tile
    rows = N_HEADS * HEAD_DIM

    def body(dq_ref, dk_ref, dv_ref, dp_ref, u_ref, out_ref, acc_ref):
        k = pl.program_id(0)

        @pl.when(k == 0)
        def _():
            acc_ref[...] = jnp.zeros_like(acc_ref)

        tokens = u_ref[...]
        for j, ref in enumerate((dq_ref, dk_ref, dv_ref)):
            acc_ref[pl.ds(j * rows, rows), :] += _dot(ref[...].reshape(rows, k_tile), tokens)
        for j in range(dproj.shape[1] // COL_CHUNK):
            cols = pl.ds(j * COL_CHUNK, COL_CHUNK)
            acc_ref[pl.ds(OFF_P + j * COL_CHUNK, COL_CHUNK), :] += _dot_tn(dp_ref[:, cols], tokens)

        @pl.when(k == nk - 1)
        def _():
            out_ref[...] = acc_ref[...].astype(BF16)

    heads = pl.BlockSpec((N_HEADS, HEAD_DIM, k_tile), lambda k: (0, 0, k))
    return pl.pallas_call(
        body, name="weight_grads", grid=(nk,),
        in_specs=[heads, heads, heads, pl.BlockSpec((k_tile, dproj.shape[1]), lambda k: (k, 0)),
                  pl.BlockSpec((k_tile, D), lambda k: (k, 0))],
        out_specs=pl.BlockSpec((N_MAIN, D), lambda k: (0, 0)),
        out_shape=jax.ShapeDtypeStruct((N_MAIN, D), BF16),
        scratch_shapes=[pltpu.VMEM((N_MAIN, D), F32)],
        compiler_params=_params(("arbitrary",)),
    )(dq_t, dk_t, dv_t, dproj, u)


def _adamw(w, g, m, v):
    m = ADAM_B1 * m + (1.0 - ADAM_B1) * g
    v = ADAM_B2 * v + (1.0 - ADAM_B2) * (g * g)
    m_hat = m / (1.0 - ADAM_B1 ** ADAM_STEP)
    v_hat = v / (1.0 - ADAM_B2 ** ADAM_STEP)
    delta = -ADAM_LR * (m_hat / (jnp.sqrt(v_hat) + ADAM_EPS) + ADAM_WD * w)
    return delta, m, v


def _adamw_call(g, w, m, v, lead_tile, name):
    nr = w.shape[0] // lead_tile

    def body(gi_ref, w_ref, m_ref, v_ref, g_ref, d_ref, nm_ref, nv_ref):
        g = gi_ref[...]
        g_ref[...] = g
        d_ref[...], nm_ref[...], nv_ref[...] = _adamw(w_ref[...], g, m_ref[...], v_ref[...])

    blk = pl.BlockSpec((lead_tile,) + w.shape[1:], lambda r: (r,) + (0,) * (w.ndim - 1))
    shape = jax.ShapeDtypeStruct(w.shape, F32)
    return pl.pallas_call(
        body, name=name, grid=(nr,),
        in_specs=[blk, blk, blk, blk], out_specs=[blk, blk, blk, blk],
        out_shape=[shape, shape, shape, shape],
        compiler_params=_params(("arbitrary",)),
    )(g, w, m, v)


def _ada_adamw(sc_t, d_ada, w, m, v):
    def body(sc_ref, d_ref, w_ref, m_ref, v_ref, g_ref, dl_ref, nm_ref, nv_ref):
        g = sc_ref[:, 0:1] * d_ref[0:1, :]
        for b in range(1, N_DEV):
            g = g + sc_ref[:, b:b + 1] * d_ref[b:b + 1, :]
        g_ref[...] = g
        dl_ref[...], nm_ref[...], nv_ref[...] = _adamw(w_ref[...], g, m_ref[...], v_ref[...])

    row_tile = 256
    blk = pl.BlockSpec((row_tile, w.shape[1]), lambda r: (r, 0))
    shape = jax.ShapeDtypeStruct(w.shape, F32)
    return pl.pallas_call(
        body, name="ada_adamw", grid=(w.shape[0] // row_tile,),
        in_specs=[pl.BlockSpec((row_tile, N_DEV), lambda r: (r, 0)), pl.BlockSpec(d_ada.shape, lambda r: (0, 0)),
                  blk, blk, blk],
        out_specs=[blk, blk, blk, blk], out_shape=[shape, shape, shape, shape],
        compiler_params=_params(("arbitrary",)),
    )(sc_t, d_ada, w, m, v)


F_LO, F_HI = 3 * D_ATT, 3 * D_ATT + N_HEADS


def _split_forget(a, axis):
    idx = lambda lo, hi: tuple(slice(lo, hi) if d == axis else slice(None) for d in range(a.ndim))
    pad = [(0, LANE - N_HEADS) if d == axis else (0, 0) for d in range(a.ndim)]
    return jnp.concatenate([a[idx(0, F_LO)], a[idx(F_HI, D_IN)]], axis=axis), jnp.pad(a[idx(F_LO, F_HI)], pad)


def _join_forget(main, f, axis):
    idx = lambda lo, hi: tuple(slice(lo, hi) if d == axis else slice(None) for d in range(main.ndim))
    return jnp.concatenate([main[idx(0, F_LO)], f[idx(0, N_HEADS)], main[idx(F_LO, N_MAIN)]], axis=axis)


def _adamw_small(grad_rows, row_params, whole_params, summed_params):
    n_row, n_whole, n_sum = len(row_params), len(whole_params), len(summed_params)
    n = n_row + n_whole + n_sum

    def body(g_ref, *refs):
        n_in = 3 * n_row + 4 * (n_whole + n_sum)
        ins, outs = list(refs[:n_in]), refs[n_in:]
        for i in range(n):
            if i < n_row:
                row, lo, hi = row_params[i][:3]
                g = g_ref[row:row + 1, lo:hi]
            elif i < n_row + n_whole:
                g = ins.pop(0)[...]
            else:
                parts = ins.pop(0)
                g = parts[0]
                for k in range(1, N_DEV):
                    g = g + parts[k]
            w, m, v = (ins.pop(0)[...] for _ in range(3))
            outs[4 * i][...] = g
            outs[4 * i + 1][...], outs[4 * i + 2][...], outs[4 * i + 3][...] = _adamw(w, g, m, v)

    shapes = [p[3] for p in row_params] + [p[1] for p in whole_params] + [p[1] for p in summed_params]
    operands = [a for p in row_params for a in p[3:]] + [a for p in whole_params + summed_params for a in p]
    flat = pl.pallas_call(
        body, name="adamw_small",
        out_shape=[jax.ShapeDtypeStruct(w.shape, F32) for w in shapes for _ in range(4)],
        compiler_params=_params(),
    )(grad_rows, *operands)
    return [flat[4 * i:4 * i + 4] for i in range(n)]


def kernel(x, c, w_ada, b_ada, w_in, b_in, w_pool_mix, b_pool_mix, pool_scale, w_out, b_out, ln_g, ln_b, loss_target, m_w_ada, m_b_ada, m_w_in, m_b_in, m_w_pool_mix, m_b_pool_mix, m_pool_scale, m_w_out, m_b_out, m_ln_g, m_ln_b, v_w_ada, v_b_ada, v_w_in, v_b_in, v_w_pool_mix, v_b_pool_mix, v_pool_scale, v_w_out, v_b_out, v_ln_g, v_ln_b):
    seq = x.shape[1]
    tile = min(256, seq)
    attn_tile = min(512, max(128, seq // 4))
    me = _dev_index(*_mesh_pos())
    x2, tgt = x[0], loss_target[0]

    rows_in = D_IN // N_DEV
    w_in_g, sc_all, ada_mine = _gather_and_ada(c, w_in[0].T.astype(BF16), w_ada[0])
    ada = ada_mine.reshape(1, D_ADA) + b_ada
    shift, scale, gate = ada[:, 0:D], ada[:, D:2 * D], ada[:, 2 * D:]
    mod = jnp.concatenate([1.0 + scale, shift, jnp.zeros((6, D), F32)], axis=0)

    w_main, w_f = _split_forget(w_in_g.reshape(D_IN, D), 0)
    b_main, b_f = _split_forget(b_in, 1)

    qp, kp, vp, f, p, g_att, g_pool, u = _inproj_forward(x2, mod, w_main, w_f, b_main, b_f, tile)
    att, q2t, w_out_g = _attention_forward(qp, kp, vp, w_out[0].astype(BF16), attn_tile)

    vecs = jnp.concatenate([gate, b_out, ln_g, ln_b, jnp.zeros((4, D), F32)], axis=0)
    pool_vecs = jnp.concatenate([b_pool_mix.reshape(1, D_POOL), pool_scale, jnp.zeros((6, D_POOL), F32)], axis=0)
    dxa, do2, d_ga, d_gp, d_pooled, gw_out, dw_pool, dvec = _middle(
        x2, tgt, att, g_att, g_pool, p, vecs, pool_vecs, w_out_g.reshape(D, D), w_pool_mix[0].astype(BF16), tile)

    pool_rows = w_pool_mix.shape[1] * GROUP_DIM
    dqp, dkp, dvp, d_cum, g_out, dvec_sum, dw_pool_sum = _attention_backward(
        q2t, kp, vp, do2, gw_out.reshape(N_DEV, D // N_DEV, D), dvec, dw_pool.reshape(pool_rows, GROUP_DIM), attn_tile)
    dx, dproj, dw_f, db_main, db_f, dmod = _inproj_backward(
        dqp, dkp, dvp, d_cum, f, d_pooled, d_ga, d_gp, x2, dxa, u, mod, w_main, w_f, tile)
    dw_main = _weight_grads(dqp, dkp, dvp, dproj, u, min(512, seq))
    gw_in = _join_forget(dw_main, dw_f.astype(BF16), 0).reshape(N_DEV, rows_in, D)
    d_ada = jnp.concatenate([dmod[1:2], dmod[0:1], dvec[5:6]], axis=1)
    g_in_rows, g_b_in, d_ada_all = _reduce_grads(gw_in, _join_forget(db_main[0:1], db_f[0:1], 1), d_ada)

    def rows3(a):
        return a[0].T.reshape(rows_in, D // LANE, LANE)

    outs_in = _adamw_call(g_in_rows.reshape(rows_in, D // LANE, LANE), rows3(w_in), rows3(m_w_in), rows3(v_w_in),
                          rows_in // 5, "adamw_w_in")
    g_w_in, d_w_in, nm_w_in, nv_w_in = (a.reshape(rows_in, D).T for a in outs_in)
    flat_pool = lambda a: a.reshape(1, D_POOL)
    pool_2d = lambda a: a.reshape(pool_rows, GROUP_DIM)
    rows = _adamw_small(
        dvec_sum,
        [(0, 0, D, b_out, m_b_out, v_b_out), (1, 0, D, ln_g, m_ln_g, v_ln_g), (2, 0, D, ln_b, m_ln_b, v_ln_b),
         (3, 0, D_POOL, flat_pool(b_pool_mix), flat_pool(m_b_pool_mix), flat_pool(v_b_pool_mix)),
         (3, D_POOL, 2 * D_POOL, pool_scale, m_pool_scale, v_pool_scale)],
        [(g_out, w_out[0], m_w_out[0], v_w_out[0]),
         (dw_pool_sum, pool_2d(w_pool_mix), pool_2d(m_w_pool_mix), pool_2d(v_w_pool_mix)),
         (g_b_in, b_in, m_b_in, v_b_in)],
        [(d_ada_all, b_ada, m_b_ada, v_b_ada)])
    small = {"b_out": rows[0], "ln_g": rows[1], "ln_b": rows[2],
             "b_pool": [a.reshape(b_pool_mix.shape) for a in rows[3]], "pool_scale": rows[4],
             "w_pool": [a.reshape(w_pool_mix.shape) for a in rows[6]], "b_in": rows[7]}
    g_s, d_s, nm_s, nv_s = ({k: r[j] for k, r in small.items()} for j in range(4))
    g_w_out, d_w_out, nm_w_out, nv_w_out = rows[5]
    g_b_ada, d_b_ada, nm_b_ada, nv_b_ada = rows[8]
    loss = dvec_sum[4, 0]

    d_ada_local = lax.dynamic_slice_in_dim(d_ada_all.reshape(N_DEV, D_ADA), me * (D_ADA // N_DEV), D_ADA // N_DEV, axis=1)
    g_w_ada, d_w_ada, nm_w_ada, nv_w_ada = _ada_adamw(sc_all.T, d_ada_local, w_ada[0], m_w_ada[0], v_w_ada[0])

    def ordered(w_ada_, b_ada_, w_in_, w_out_, s):
        return (w_ada_[None], b_ada_, w_in_[None], s["b_in"], s["w_pool"], s["b_pool"], s["pool_scale"],
                w_out_[None], s["b_out"], s["ln_g"], s["ln_b"])

    return (loss, dx[None],
            *ordered(g_w_ada, g_b_ada, g_w_in, g_w_out, g_s),
            *ordered(d_w_ada, d_b_ada, d_w_in, d_w_out, d_s),
            *ordered(nm_w_ada, nm_b_ada, nm_w_in, nm_w_out, nm_s),
            *ordered(nv_w_ada, nv_b_ada, nv_w_in, nv_w_out, nv_s))
```

```python
import jax
import jax.numpy as jnp
from jax import lax
from jax.experimental import pallas as pl
from jax.experimental.pallas import tpu as pltpu

F32 = jnp.float32
BF16 = jnp.bfloat16

N_DEV = 8
D = 1024
N_HEADS = 8
HEAD_DIM = 64
D_ATT = 512
D_POOL = 512
POOL_WINDOWS = (2, 4, 8, 16)
GROUP_DIM = 128
HALO = 16
LANE = 128
D_IN = 3080
D_ADA = 3072
N_MAIN = 3072
OFF_P = 1536
COL_CHUNK = 512
Q_SCALE = 0.125
LN_EPS = 1e-5
ALPHA = 2.0 ** 0.25
L_CQ, L_CK, L_LSE = 64, 67, 70

ADAM_LR, ADAM_B1, ADAM_B2, ADAM_EPS, ADAM_WD, ADAM_STEP = 0.001, 0.9, 0.999, 1e-08, 0.01, 10
VMEM_LIMIT = 56 * 1024 * 1024

MESH = pl.DeviceIdType.MESH
ANY = pl.BlockSpec(memory_space=pl.ANY)


def _params(sem=None, vmem=VMEM_LIMIT):
    return pltpu.CompilerParams(dimension_semantics=sem, vmem_limit_bytes=vmem)


def _split3(a):
    hi = a.astype(BF16)
    r = a - hi.astype(F32)
    mid = r.astype(BF16)
    lo = (r - mid.astype(F32)).astype(BF16)
    return hi, mid, lo


def _dot(a, b):
    return jnp.dot(a, b, preferred_element_type=F32)


def _dot_nt(a, b):
    return lax.dot_general(a, b, (((1,), (1,)), ((), ())), preferred_element_type=F32)


def _dot_tn(a, b):
    return lax.dot_general(a, b, (((0,), (0,)), ((), ())), preferred_element_type=F32)


def _dot3(m01, a):
    hi, mid, lo = _split3(a)
    return _dot(m01, hi) + _dot(m01, mid) + _dot(m01, lo)


def _sigmoid(z):
    return 1.0 / (1.0 + jnp.exp(-z))


def _lanes(shape):
    return lax.broadcasted_iota(jnp.int32, shape, len(shape) - 1)


def _place3(lane, base, parts, other):
    out = other
    for j in range(3):
        out = jnp.where(lane == base + j, parts[j], out)
    return out


def _mesh_pos():
    return lax.axis_index("x"), lax.axis_index("y"), lax.axis_index("c")


def _dev_index(px, py, pc):
    return 4 * px + 2 * py + pc


N_GATHER_SEMS = 9


def _gather_stages(src_ref, out_ref, send_sems, recv_sems, local_sem):
    x, y, c = _mesh_pos()
    me, sibling = (x, y, c), (x, y, 1 - c)
    nbr_x, nbr_y, diag = (1 - x, y), (x, 1 - y), (1 - x, 1 - y)
    half = out_ref.shape[-1] // 2
    left, right = pl.ds(0, half), pl.ds(half, half)

    def copy(k, block, to, cols=None, src=None):
        slot = out_ref.at[_dev_index(*block)]
        if cols is not None:
            slot = slot.at[:, cols]
        return pltpu.make_async_remote_copy(
            src_ref=slot if src is None else src, dst_ref=slot, send_sem=send_sems.at[k], recv_sem=recv_sems.at[k],
            device_id=to, device_id_type=MESH)

    mine = pltpu.make_async_copy(src_ref, out_ref.at[_dev_index(*me)], local_sem)
    first = [copy(0, me, sibling, src=src_ref), copy(1, me, (*nbr_x, c), src=src_ref), copy(2, me, (*nbr_y, c), src=src_ref)]
    relay = [(1, nbr_x, None, nbr_x), (2, nbr_y, None, nbr_y), (3, diag, left, nbr_y), (4, diag, right, nbr_x)]
    onward = [copy(3, (*nbr_x, c), (*nbr_y, c), cols=left), copy(4, (*nbr_y, c), (*nbr_x, c), cols=right)]
    passed = [copy(4 + k, (*block, c), sibling, cols=cols) for k, block, cols, _ in relay]

    def start():
        mine.start()
        for cp in first:
            cp.start()

    def relay_stage(first_item):
        def run():
            for j in (first_item, first_item + 1):
                k, block, cols, frm = relay[j]
                copy(k, (*block, c), (*frm, c), cols=cols).wait_recv()
                if j < 2:
                    onward[j].start()
                passed[j].start()
        return run

    def finish():
        copy(0, sibling, me).wait_recv()
        for k, block, cols, _ in relay:
            copy(4 + k, (*block, 1 - c), me, cols=cols).wait_recv()
        for cp in first + onward + passed:
            cp.wait_send()
        mine.wait()

    return start, relay_stage(0), relay_stage(2), finish


N_REDUCE_SEMS = 10
N_SMALL_SEMS = 4
N_ROWS_SEMS = 7


def _reduce_stages(ins, gs, r1, s2, r2, smalls, send_sems, recv_sems, rows=None, own=None):
    n = len(ins)
    x, y, c = _mesh_pos()
    me = _dev_index(x, y, c)
    sibling = (x, y, 1 - c)
    chips = [(x, y), (1 - x, y), (x, 1 - y), (1 - x, 1 - y)]
    peers = []
    for p in range(1, N_DEV):
        px, py, pc = (p >> 2) & 1, (p >> 1) & 1, p & 1
        peers.append((1 - x if px else x, 1 - y if py else y, 1 - c if pc else c))
    base_small = N_REDUCE_SEMS * n

    def remote(src, dst, k, to):
        return pltpu.make_async_remote_copy(src_ref=src, dst_ref=dst, send_sem=send_sems.at[k],
                                            recv_sem=recv_sems.at[k], device_id=to, device_id_type=MESH)

    def level1(a, q):
        return remote(ins[a].at[_dev_index(*chips[q], 1 - c)], r1[a].at[q], N_REDUCE_SEMS * a + q, sibling)

    def level2(a, k):
        half = ins[a].shape[-1] // 2
        left, right = pl.ds(0, half), pl.ds(half, half)
        nbr_x, nbr_y = (*chips[1], c), (*chips[2], c)
        src_slot, dst_slot, cols, to = [(0, 0, left, nbr_x), (1, 1, right, nbr_y), (2, 2, left, nbr_x),
                                        (2, 2, right, nbr_y), (0, 0, right, nbr_x), (1, 1, left, nbr_y)][k]
        return remote(s2[a].at[src_slot, :, cols], r2[a].at[dst_slot, :, cols], N_REDUCE_SEMS * a + 4 + k, to)

    to_sibling = [remote(sm[0], sm[2], base_small + 4 * i, sibling) for i, sm in enumerate(smalls)]
    to_chips = [[remote(sm[3], sm[4].at[j], base_small + 4 * i + 1 + j, (*chips[j + 1], c)) for j in range(3)]
                for i, sm in enumerate(smalls)]
    if rows is not None:
        rows_ref, land_ref, all_ref = rows
        base_rows = base_small + 4 * len(smalls)
        row_sends = [remote(rows_ref, land_ref.at[me], base_rows + k, to) for k, to in enumerate(peers)]

    def mine(a, q):
        buf, sems = own[a]
        return pltpu.make_async_copy(ins[a].at[_dev_index(*chips[q], c)], buf.at[q], sems.at[q])

    def start():
        for a in range(n):
            for q in range(4):
                level1(a, q).start()
            if own is not None:
                for q in (1, 2, 3, 0):
                    mine(a, q).start()
        for cp in to_sibling:
            cp.start()
        if rows is not None:
            for cp in row_sends:
                cp.start()
            land_ref[me] = rows_ref[...]

    def middle():
        for a in range(n):
            for q in (1, 2, 3, 0):
                level1(a, q).wait_recv()
                if own is None:
                    kept = ins[a][_dev_index(*chips[q], c)]
                else:
                    mine(a, q).wait()
                    kept = own[a][0][q]
                pair = kept.astype(F32) + r1[a][q].astype(F32)
                if q == 0:
                    gs[a][...] = pair
                else:
                    s2[a][q - 1] = pair.astype(BF16)
                    for k in ((0,), (1,), (2, 3))[q - 1]:
                        level2(a, k).start()
        for i, (small_ref, _, sm_sib, sm_chip, _) in enumerate(smalls):
            to_sibling[i].wait_recv()
            sm_chip[...] = small_ref[...] + sm_sib[...]
            for cp in to_chips[i]:
                cp.start()

    def fold():
        for a in range(n):
            half = ins[a].shape[-1] // 2
            level2(a, 3).wait_recv()
            s2[a][0, :, half:] = (s2[a][0, :, half:].astype(F32) + r2[a][2, :, half:].astype(F32)).astype(BF16)
            level2(a, 4).start()
            level2(a, 2).wait_recv()
            s2[a][1, :, :half] = (s2[a][1, :, :half].astype(F32) + r2[a][2, :, :half].astype(F32)).astype(BF16)
            level2(a, 5).start()

    def finish():
        for a in range(n):
            for k in (0, 1, 4, 5):
                level2(a, k).wait_recv()
            gs[a][...] = gs[a][...] + r2[a][0].astype(F32) + r2[a][1].astype(F32)
            for q in range(4):
                level1(a, q).wait_send()
            for k in range(6):
                level2(a, k).wait_send()
        for i, (_, total_ref, _, sm_chip, sm_recv) in enumerate(smalls):
            for cp in to_chips[i]:
                cp.wait_recv()
            total = None
            for ax in range(2):
                for ay in range(2):
                    dx, dy = x != ax, y != ay
                    term = jnp.where(dx, jnp.where(dy, sm_recv[2], sm_recv[0]), jnp.where(dy, sm_recv[1], sm_chip[...]))
                    total = term if total is None else total + term
            total_ref[...] = total
            for cp in [to_sibling[i]] + to_chips[i]:
                cp.wait_send()
        if rows is not None:
            for k, frm in enumerate(peers):
                remote(rows_ref, land_ref.at[_dev_index(*frm)], base_rows + k, frm).wait_recv()
            all_ref[...] = land_ref[...]
            for cp in row_sends:
                cp.wait_send()

    return start, middle, fold, finish


def _reduce_scratch(shard, smalls, rows=None):
    out = [pltpu.VMEM((lead,) + shard.shape[1:], BF16) for lead in (4, 3, 3)]
    for small in smalls:
        out += [pltpu.VMEM(small.shape, F32), pltpu.VMEM(small.shape, F32), pltpu.VMEM((3,) + small.shape, F32)]
    n_sems = N_REDUCE_SEMS + N_SMALL_SEMS * len(smalls)
    if rows is not None:
        out.append(pltpu.VMEM((N_DEV,) + rows.shape, F32))
        n_sems += N_ROWS_SEMS
    return out + [pltpu.SemaphoreType.DMA((n_sems,))] * 2


def _reduce_grads(gw_in, small, rows):
    def body(in_ref, small_ref, rows_ref, g_ref, total_ref, rows_all_ref,
             r1, s2, r2, sm_sib, sm_chip, sm_recv, rows_land, send_sems, recv_sems, kept, kept_sems):
        stages = _reduce_stages(
            [in_ref], [g_ref], [r1], [s2], [r2], [(small_ref, total_ref, sm_sib, sm_chip, sm_recv)],
            send_sems, recv_sems, rows=(rows_ref, rows_land, rows_all_ref), own=[(kept, kept_sems)])
        for stage in stages:
            stage()

    vmem = pl.BlockSpec(memory_space=pltpu.VMEM)
    return pl.pallas_call(
        body, name="reduce_grads",
        in_specs=[ANY, vmem, vmem], out_specs=[vmem, vmem, vmem],
        out_shape=[jax.ShapeDtypeStruct(gw_in.shape[1:], F32), jax.ShapeDtypeStruct(small.shape, F32),
                   jax.ShapeDtypeStruct((N_DEV,) + rows.shape, F32)],
        scratch_shapes=_reduce_scratch(gw_in, [small], rows)
        + [pltpu.VMEM((4,) + gw_in.shape[1:], BF16), pltpu.SemaphoreType.DMA((4,))],
        compiler_params=_params(),
    )(gw_in, small, rows)


def _dot3_rhs(a, b):
    a0, a1, a2 = _split3(a)
    b0, b1, b2 = _split3(b)
    return (_dot(a0, b0) + (_dot(a0, b1) + _dot(a1, b0))
            + (_dot(a0, b2) + _dot(a1, b1) + _dot(a2, b0)))


def _gather_and_ada(c, w_in_rows, w_ada):
    cols = w_ada.shape[1]

    def body(c_ref, w_ref, wa_ref, w_all_ref, sc_ref, ada_ref,
             c_land, part, ada_land, send_sems, recv_sems, local_sem, x_send, x_recv):
        x, y, cc = _mesh_pos()
        me = _dev_index(x, y, cc)
        peers = []
        for p in range(1, N_DEV):
            px, py, pc = (p >> 2) & 1, (p >> 1) & 1, p & 1
            peers.append((1 - x if px else x, 1 - y if py else y, 1 - cc if pc else cc))

        def remote(src, dst, k, to):
            return pltpu.make_async_remote_copy(src_ref=src, dst_ref=dst, send_sem=x_send.at[k], recv_sem=x_recv.at[k],
                                                device_id=to, device_id_type=MESH)

        c_sends = [remote(c_ref, c_land.at[me], k, to) for k, to in enumerate(peers)]
        for cp in c_sends:
            cp.start()
        start, relay_near, relay_far, finish = _gather_stages(w_ref, w_all_ref, send_sems, recv_sems, local_sem.at[0])
        start()
        c_land[me] = c_ref[...]
        for k, frm in enumerate(peers):
            remote(c_ref, c_land.at[_dev_index(*frm)], k, frm).wait_recv()
        c_all = jnp.concatenate([c_land[b] for b in range(N_DEV)], axis=0)
        sc = c_all * _sigmoid(c_all)
        sc_ref[...] = sc
        rows = _dot3_rhs(sc, wa_ref[...])
        for b in range(N_DEV):
            part[b] = rows[b:b + 1, :]
        a_sends = [remote(part.at[_dev_index(*to)], ada_land.at[me], 7 + k, to) for k, to in enumerate(peers)]
        for cp in a_sends:
            cp.start()
        ada_land[me] = part[me]
        for k, frm in enumerate(peers):
            remote(part.at[0], ada_land.at[_dev_index(*frm)], 7 + k, frm).wait_recv()
        ada_ref[...] = ada_land[...]

        relay_near()
        relay_far()
        finish()
        for cp in c_sends + a_sends:
            cp.wait_send()

    vmem = pl.BlockSpec(memory_space=pltpu.VMEM)
    return pl.pallas_call(
        body, name="gather_weights",
        in_specs=[vmem, ANY, vmem], out_specs=[ANY, vmem, vmem],
        out_shape=[jax.ShapeDtypeStruct((N_DEV,) + w_in_rows.shape, w_in_rows.dtype),
                   jax.ShapeDtypeStruct((N_DEV, D), F32), jax.ShapeDtypeStruct((N_DEV, 1, cols), F32)],
        scratch_shapes=[pltpu.VMEM((N_DEV, 1, D), F32), pltpu.VMEM((N_DEV, 1, cols), F32), pltpu.VMEM((N_DEV, 1, cols), F32),
                        pltpu.SemaphoreType.DMA((N_GATHER_SEMS,)), pltpu.SemaphoreType.DMA((N_GATHER_SEMS,)),
                        pltpu.SemaphoreType.DMA((1,)),
                        pltpu.SemaphoreType.DMA((14,)), pltpu.SemaphoreType.DMA((14,))],
        compiler_params=_params(),
    )(c, w_in_rows, w_ada)


def _inproj_forward(x, mod, w_main, w_f, b_main, b_f, tile):
    seq = x.shape[0]
    nt = seq // tile

    def body(x_ref, mod_ref, w_ref, wf_ref, b_ref, bf_ref,
             qp_ref, kp_ref, vp_ref, f_ref, p_ref, ga_ref, gp_ref, u_ref, carry_ref):
        i = pl.program_id(0)

        @pl.when(i == 0)
        def _():
            carry_ref[...] = jnp.zeros_like(carry_ref)

        u = x_ref[...] * mod_ref[0:1, :] + mod_ref[1:2, :]
        ub = u.astype(BF16)
        u_ref[...] = ub

        f = _dot_nt(ub, wf_ref[...]) + bf_ref[...]
        f_ref[...] = f
        lane = _lanes((tile, LANE))
        log_f = jnp.where(lane < N_HEADS, jnp.minimum(f, 0.0) - jnp.log(1.0 + jnp.exp(-jnp.abs(f))), 0.0)
        row = lax.broadcasted_iota(jnp.int32, (tile, tile), 0)
        col = lax.broadcasted_iota(jnp.int32, (tile, tile), 1)
        tri = (row >= col).astype(BF16)
        cum = _dot3(tri, log_f) + carry_ref[0:1, :]
        carry_ref[0:1, :] = cum[tile - 1:tile, :]
        cq = [part.astype(F32) for part in _split3(cum)]
        ck = [part.astype(F32) for part in _split3(-cum)]

        def proj(chunk):
            cols = pl.ds(chunk * COL_CHUNK, COL_CHUNK)
            return _dot_nt(ub, w_ref[cols, :]) + b_ref[:, cols]

        def head_tiles(r):
            for pair in range(N_HEADS // 2):
                both = r[:, pair * LANE:(pair + 1) * LANE]
                yield 2 * pair, both
                yield 2 * pair + 1, pltpu.roll(both, HEAD_DIM, 1)

        for h, val in head_tiles(proj(0)):
            extra = jnp.where((lane >= L_CK) & (lane < L_CK + 3), 1.0, 0.0)
            extra = _place3(lane, L_CQ, [part[:, h:h + 1] for part in cq], extra)
            qp_ref[h] = jnp.where(lane < HEAD_DIM, val * Q_SCALE, extra).astype(BF16)
        for h, val in head_tiles(proj(1)):
            ones = ((lane >= L_CQ) & (lane < L_CQ + 3)) | ((lane >= L_LSE) & (lane < L_LSE + 3))
            extra = _place3(lane, L_CK, [part[:, h:h + 1] for part in ck], jnp.where(ones, 1.0, 0.0))
            kp_ref[h] = jnp.where(lane < HEAD_DIM, val, extra).astype(BF16)
        for h, val in head_tiles(proj(2)):
            extra = jnp.where((lane >= HEAD_DIM) & (lane < HEAD_DIM + 3), -1.0, 0.0)
            vp_ref[h] = jnp.where(lane < HEAD_DIM, val, extra).astype(BF16)
        p_ref[...] = proj(3)
        ga_ref[...] = proj(4)
        gp_ref[...] = proj(5)

    head_block = pl.BlockSpec((N_HEADS, tile, LANE), lambda i: (0, i, 0))
    tok = lambda width: pl.BlockSpec((tile, width), lambda i: (i, 0))
    whole = lambda a: pl.BlockSpec(a.shape, lambda i: (0,) * a.ndim)
    padded = jax.ShapeDtypeStruct((N_HEADS, seq, LANE), BF16)
    half = jax.ShapeDtypeStruct((seq, D_ATT), F32)
    return pl.pallas_call(
        body, name="inproj_forward", grid=(nt,),
        in_specs=[tok(D), whole(mod), whole(w_main), whole(w_f), whole(b_main), whole(b_f)],
        out_specs=[head_block, head_block, head_block, tok(LANE), tok(D_POOL), tok(D_ATT), tok(D_POOL),
                   tok(D)],
        out_shape=[padded, padded, padded, jax.ShapeDtypeStruct((seq, LANE), F32), half, half, half,
                   jax.ShapeDtypeStruct((seq, D), BF16)],
        scratch_shapes=[pltpu.VMEM((8, LANE), F32)],
        compiler_params=_params(("arbitrary",)),
    )(x, mod, w_main, w_f, b_main, b_f)


def _attention_forward(qp, kp, vp, w_out, tile):
    seq = qp.shape[1]
    nb = seq // tile
    steps = (N_HEADS // 2) * nb

    def body(q_ref, k_ref, v_ref, wo_ref, att_ref, q2t_ref, wo_all_ref, s_a, s_b, m_ref, acc_ref,
             send_sems, recv_sems, local_sem):
        step = pl.program_id(0) * nb + pl.program_id(1)
        start, relay_near, relay_far, finish = _gather_stages(wo_ref, wo_all_ref, send_sems, recv_sems, local_sem.at[0])
        pl.when(step == 0)(start)
        pl.when(step == steps // 4)(relay_near)
        pl.when(step == (3 * steps) // 4)(relay_far)

        i = pl.program_id(1)
        sub = lax.broadcasted_iota(jnp.int32, (LANE, tile), 0)
        row = lax.broadcasted_iota(jnp.int32, (tile, tile), 0)
        col = lax.broadcasted_iota(jnp.int32, (tile, tile), 1)
        q = [q_ref[0], q_ref[1]]

        def scores(buf, kb):
            rows = pl.ds(pl.multiple_of(kb * tile, tile), tile)
            for hh in range(2):
                buf[hh] = _dot_nt(k_ref[hh, rows, :], q[hh])

        def absorb(buf, kb, masked):
            rows = pl.ds(pl.multiple_of(kb * tile, tile), tile)
            for hh in range(2):
                m = m_ref[hh, 0:1, :]
                s = buf[hh]
                if masked:
                    s = jnp.where(row <= col, s, -1e30)
                m_new = jnp.maximum(m, jnp.max(s, axis=0, keepdims=True))
                p = jnp.exp(s - m_new).astype(BF16)
                acc_ref[hh] = jnp.exp(m - m_new) * acc_ref[hh] + _dot_tn(v_ref[hh, rows, :], p)
                m_ref[hh, 0:1, :] = m_new

        def two_blocks(j, _):
            scores(s_b, 2 * j + 1)
            absorb(s_a, 2 * j, False)
            scores(s_a, 2 * j + 2)
            absorb(s_b, 2 * j + 1, False)
            return 0

        def last_block():
            absorb(s_a, i, True)

        def last_two_blocks():
            scores(s_b, i)
            absorb(s_a, i - 1, False)
            absorb(s_b, i, True)

        scores(s_a, 0)
        m_ref[...] = jnp.full(m_ref.shape, -1e30, F32)
        acc_ref[...] = jnp.zeros_like(acc_ref)
        lax.fori_loop(0, i // 2, two_blocks, 0)
        lax.cond(i % 2 == 0, last_block, last_two_blocks)
        outs = []
        for hh in range(2):
            m, acc = m_ref[hh, 0:1, :], acc_ref[hh]
            l = -acc[HEAD_DIM:HEAD_DIM + 1, :]
            outs.append((acc / l)[:HEAD_DIM, :])
            neg_lse = [part.astype(F32) for part in _split3(-(m + jnp.log(l)))]
            q2t_ref[hh] = _place3(sub, L_LSE, neg_lse, q[hh].astype(F32).T).astype(BF16)
        att_ref[...] = jnp.concatenate(outs, axis=0).T
        pl.when(step == steps - 1)(finish)

    pair = pl.BlockSpec((2, tile, LANE), lambda hp, i: (hp, i, 0))
    full = pl.BlockSpec((2, seq, LANE), lambda hp, i: (hp, 0, 0))
    return pl.pallas_call(
        body, name="attention_forward", grid=(N_HEADS // 2, nb),
        in_specs=[pair, full, full, ANY],
        out_specs=[pl.BlockSpec((tile, LANE), lambda hp, i: (i, hp)),
                   pl.BlockSpec((2, LANE, tile), lambda hp, i: (hp, 0, i)), ANY],
        out_shape=[jax.ShapeDtypeStruct((seq, D_ATT), F32),
                   jax.ShapeDtypeStruct((N_HEADS, LANE, seq), BF16),
                   jax.ShapeDtypeStruct((N_DEV,) + w_out.shape, w_out.dtype)],
        scratch_shapes=[pltpu.VMEM((2, tile, tile), F32), pltpu.VMEM((2, tile, tile), F32),
                        pltpu.VMEM((2, 8, tile), F32), pltpu.VMEM((2, LANE, tile), F32),
                        pltpu.SemaphoreType.DMA((N_GATHER_SEMS,)), pltpu.SemaphoreType.DMA((N_GATHER_SEMS,)),
                        pltpu.SemaphoreType.DMA((1,))],
        compiler_params=_params(("arbitrary", "arbitrary")),
    )(qp, kp, vp, w_out)


def _window_sum(x, halo, window, transposed):
    tile = x.shape[0]

    def split_cat(a):
        hi = a.astype(BF16)
        return jnp.concatenate([hi, (a - hi.astype(F32)).astype(BF16)], axis=1)

    def fold(r):
        return r[:, :LANE] + r[:, LANE:]

    r = lax.broadcasted_iota(jnp.int32, (tile, tile), 0)
    c = lax.broadcasted_iota(jnp.int32, (tile, tile), 1)
    rh = lax.broadcasted_iota(jnp.int32, (HALO, HALO), 0)
    ch = lax.broadcasted_iota(jnp.int32, (HALO, HALO), 1)
    if not transposed:
        band = (c <= r) & (r - c < window)
        edge = (rh + HALO - ch) < window
    else:
        band = (r <= c) & (c - r < window)
        edge = (HALO + ch - rh) < window
    out = fold(_dot(band.astype(BF16), split_cat(x)))
    reach = fold(_dot(edge.astype(BF16), split_cat(halo)))
    if not transposed:
        return jnp.concatenate([out[:HALO] + reach, out[HALO:]], axis=0)
    return jnp.concatenate([out[:tile - HALO], out[tile - HALO:] + reach], axis=0)


def _silu_parts(g):
    sig = _sigmoid(g)
    return g * sig, sig * (1.0 + g * (1.0 - sig))


def _middle(x, tgt, att, g_att, g_pool, p, vecs, pool_vecs, w_out, w_pool, tile):
    seq = x.shape[0]
    nt = seq // tile
    halo_blocks = tile // HALO

    def body(x_ref, tgt_ref, att_ref, ga_ref, gp_ref, p_ref, ph_ref, vec_ref, pvec_ref, wo_ref, wp_ref,
             dxa_ref, do2_ref, dga_ref, dgp_ref, dpooled_ref, gwo_ref, dwp_ref, dvec_ref, dwo_ref, dpvec_ref):
        i = pl.program_id(0)

        @pl.when(i == 0)
        def _():
            dwo_ref[...] = jnp.zeros_like(dwo_ref)
            dwp_ref[...] = jnp.zeros_like(dwp_ref)
            dvec_ref[...] = jnp.zeros_like(dvec_ref)
            dpvec_ref[...] = jnp.zeros_like(dpvec_ref)

        gate, b_out, ln_g, ln_b = (vec_ref[k:k + 1, :] for k in range(4))
        b_pool, pool_scale = pvec_ref[0:1, :], pvec_ref[1:2, :]
        x = x_ref[...]
        p = p_ref[...]
        p_halo = ph_ref[...] * jnp.where(i > 0, 1.0, 0.0)
        pos = i * tile + lax.broadcasted_iota(jnp.int32, (tile, 1), 0) + 1

        pooled, mixed = [], []
        for g, window in enumerate(POOL_WINDOWS):
            cols = slice(g * GROUP_DIM, (g + 1) * GROUP_DIM)
            wsum = _window_sum(p[:, cols], p_halo[:, cols], window, False)
            count = jnp.minimum(pos, window).astype(F32)
            pooled.append(wsum / count - p[:, cols])
            mixed.append(_dot(pooled[g].astype(BF16), wp_ref[g]) + b_pool[:, cols])
        mixed = jnp.concatenate(mixed, axis=1)
        pool = mixed * pool_scale

        att = att_ref[...]
        g_att, g_pool = ga_ref[...], gp_ref[...]
        silu_a, dsilu_a = _silu_parts(g_att)
        silu_p, dsilu_p = _silu_parts(g_pool)
        y_in = jnp.concatenate([att * silu_a, pool * silu_p], axis=1)
        y = _dot(y_in.astype(BF16), wo_ref[...]) + b_out
        h = ALPHA * x + gate * y
        mu = jnp.mean(h, axis=1, keepdims=True)
        hc = h - mu
        var = jnp.mean(hc * hc, axis=1, keepdims=True)
        rstd = lax.rsqrt(var + LN_EPS)
        yhat = hc * rstd
        diff = yhat * ln_g + ln_b - tgt_ref[...]
        loss_rows = jnp.sum(diff * diff, axis=1, keepdims=True)
        d_out = diff * (1.0 / D)

        d_yhat = d_out * ln_g
        dh = rstd * (d_yhat - jnp.mean(d_yhat, axis=1, keepdims=True)
                     - yhat * jnp.mean(d_yhat * yhat, axis=1, keepdims=True))
        dxa_ref[...] = ALPHA * dh
        dy = dh * gate
        dyb = dy.astype(BF16)
        lane = _lanes((1, D))
        loss_row = jnp.where(lane == 0, (0.5 / D) * jnp.sum(loss_rows, axis=0, keepdims=True), 0.0)
        dvec_ref[5:6, :] += jnp.sum(dh * y, axis=0, keepdims=True)
        dvec_ref[0:1, :] += jnp.sum(dy, axis=0, keepdims=True)
        dvec_ref[1:2, :] += jnp.sum(d_out * yhat, axis=0, keepdims=True)
        dvec_ref[2:3, :] += jnp.sum(d_out, axis=0, keepdims=True)
        dvec_ref[4:5, :] += loss_row

        dwo_ref[...] += _dot(y_in.T.astype(BF16), dyb)
        d_yin = _dot_nt(dyb, wo_ref[...])
        d_a, d_pl = d_yin[:, :D_ATT], d_yin[:, D_ATT:]
        d_att = d_a * silu_a
        d_att_t = d_att.T
        prod_t = (d_att * att).T
        sub = lax.broadcasted_iota(jnp.int32, (HEAD_DIM, tile), 0)
        for h in range(N_HEADS):
            rows = slice(h * HEAD_DIM, (h + 1) * HEAD_DIM)
            delta = jnp.sum(prod_t[rows], axis=0, keepdims=True)
            extra = _place3(sub, 0, [part.astype(F32) for part in _split3(delta)], 0.0)
            do2_ref[h] = jnp.concatenate([d_att_t[rows], extra], axis=0).astype(BF16)
        dga_ref[...] = d_a * att * dsilu_a
        dgp_ref[...] = d_pl * pool * dsilu_p
        d_pool = d_pl * silu_p
        d_mixed = d_pool * pool_scale
        dpvec_ref[0:1, :] += jnp.sum(d_mixed, axis=0, keepdims=True)
        dpvec_ref[1:2, :] += jnp.sum(d_pool * mixed, axis=0, keepdims=True)
        d_pooled = []
        for g in range(len(POOL_WINDOWS)):
            cols = slice(g * GROUP_DIM, (g + 1) * GROUP_DIM)
            dmb = d_mixed[:, cols].astype(BF16)
            dwp_ref[g] += _dot(pooled[g].T.astype(BF16), dmb)
            d_pooled.append(_dot_nt(dmb, wp_ref[g]))
        dpooled_ref[...] = jnp.concatenate(d_pooled, axis=1)

        @pl.when(i == nt - 1)
        def _():
            gwo_ref[...] = dwo_ref[...].astype(BF16)
            dvec_ref[3:4, :] = jnp.concatenate([dpvec_ref[0:1, :], dpvec_ref[1:2, :]], axis=1)

    tok = lambda width: pl.BlockSpec((tile, width), lambda i: (i, 0))
    whole = lambda a: pl.BlockSpec(a.shape, lambda i: (0,) * a.ndim)
    halo = pl.BlockSpec((HALO, D_POOL), lambda i: (jnp.maximum(i * halo_blocks - 1, 0), 0))
    half = jax.ShapeDtypeStruct((seq, D_ATT), F32)
    outs = [jax.ShapeDtypeStruct((seq, D), F32), jax.ShapeDtypeStruct((N_HEADS, LANE, seq), BF16), half, half, half,
            jax.ShapeDtypeStruct(w_out.shape, BF16), jax.ShapeDtypeStruct(w_pool.shape, F32),
            jax.ShapeDtypeStruct(vecs.shape, F32)]
    return pl.pallas_call(
        body, name="middle", grid=(nt,),
        in_specs=[tok(D), tok(D), tok(D_ATT), tok(D_ATT), tok(D_POOL), tok(D_POOL), halo,
                  whole(vecs), whole(pool_vecs), whole(w_out), whole(w_pool)],
        out_specs=[tok(D), pl.BlockSpec((N_HEADS, LANE, tile), lambda i: (0, 0, i)),
                   tok(D_ATT), tok(D_POOL), tok(D_POOL),
                   whole(w_out), whole(w_pool), whole(vecs)],
        out_shape=outs,
        scratch_shapes=[pltpu.VMEM(w_out.shape, F32), pltpu.VMEM(pool_vecs.shape, F32)],
        compiler_params=_params(("arbitrary",)),
    )(x, tgt, att, g_att, g_pool, p, p, vecs, pool_vecs, w_out, w_pool)


def _attention_backward(q2t, kp, vp, do2t, gw_out, vecs, pool, tile):
    seq = kp.shape[1]
    nb = seq // tile
    last = N_HEADS // 2 - 1

    def body(qt_ref, k_ref, v_ref, dot_ref, gwo_hbm, vecs_hbm, pool_hbm,
             dq_ref, dk_ref, dv_ref, dcum_ref, g_out_ref, vecs_sum_ref, pool_sum_ref,
             dq_acc, dk_acc, dv_acc, gwo_ref, vecs_ref, pool_ref,
             r1, s2, r2, v_sib, v_chip, v_recv, p_sib, p_chip, p_recv, send_sems, recv_sems):
        hp = pl.program_id(0)
        start, middle, fold, finish = _reduce_stages(
            [gwo_ref], [g_out_ref], [r1], [s2], [r2],
            [(vecs_ref, vecs_sum_ref, v_sib, v_chip, v_recv), (pool_ref, pool_sum_ref, p_sib, p_chip, p_recv)],
            send_sems, recv_sems)

        @pl.when(hp == 0)
        def _():
            pltpu.sync_copy(gwo_hbm, gwo_ref)
            pltpu.sync_copy(vecs_hbm, vecs_ref)
            pltpu.sync_copy(pool_hbm, pool_ref)
            start()

        pl.when(hp == 1)(middle)
        pl.when(hp == 2)(fold)

        row = lax.broadcasted_iota(jnp.int32, (tile, tile), 0)
        col = lax.broadcasted_iota(jnp.int32, (tile, tile), 1)
        dq_acc[...] = jnp.zeros_like(dq_acc)

        def kv_block(kb, _):
            krows = pl.ds(pl.multiple_of(kb * tile, tile), tile)
            k = [k_ref[hh, krows, :] for hh in range(2)]
            v = [v_ref[hh, krows, :] for hh in range(2)]
            k_t = [k[hh].T for hh in range(2)]

            def q_block(qb, masked):
                qcols = pl.ds(pl.multiple_of(qb * tile, tile), tile)
                for hh in range(2):
                    q_t = qt_ref[hh, :, qcols]
                    do_t = dot_ref[hh, :, qcols]
                    s_t = _dot(k[hh], q_t)
                    if masked:
                        s_t = jnp.where(row <= col, s_t, -1e30)
                    p_t = jnp.exp(s_t)
                    ds_t = (p_t * _dot(v[hh], do_t)).astype(BF16)
                    dv_new = _dot_nt(do_t, p_t.astype(BF16))
                    dk_new = _dot_nt(q_t, ds_t)
                    if masked:
                        dv_acc[hh], dk_acc[hh] = dv_new, dk_new
                    else:
                        dv_acc[hh] += dv_new
                        dk_acc[hh] += dk_new
                    dq_acc[hh, :, qcols] += _dot(k_t[hh], ds_t)

            q_block(kb, True)

            def two_later_blocks(j, _):
                q_block(kb + 1 + 2 * j, False)
                q_block(kb + 2 + 2 * j, False)
                return 0

            later = nb - 1 - kb
            lax.fori_loop(0, later // 2, two_later_blocks, 0)
            pl.when(later % 2 == 1)(lambda: q_block(nb - 1, False))
            for hh in range(2):
                dk = dk_acc[hh]
                dk_ref[hh, :, krows] = dk.astype(BF16)
                dv_ref[hh, :, krows] = dv_acc[hh].astype(BF16)
                dcum_ref[hh, :, krows] = -dk[L_CK:L_CK + 1, :]
            return 0

        lax.fori_loop(0, nb, kv_block, 0)
        for hh in range(2):
            dq = dq_acc[hh]
            dcum_ref[hh] += dq[L_CQ:L_CQ + 1, :]
            dq_ref[hh] = (dq * Q_SCALE).astype(BF16)
        pl.when(hp == last)(finish)

    pair = pl.BlockSpec((2, seq, LANE), lambda hp: (hp, 0, 0))
    pair_t = pl.BlockSpec((2, LANE, seq), lambda hp: (hp, 0, 0))
    whole = lambda shape: pl.BlockSpec(shape, lambda hp: (0,) * len(shape))
    grad = jax.ShapeDtypeStruct((N_HEADS, LANE, seq), BF16)
    return pl.pallas_call(
        body, name="attention_backward", grid=(N_HEADS // 2,),
        in_specs=[pair_t, pair, pair, pair_t, ANY, ANY, ANY],
        out_specs=[pair_t, pair_t, pair_t, pl.BlockSpec((2, 1, seq), lambda hp: (hp, 0, 0)),
                   whole(gw_out.shape[1:]), whole(vecs.shape), whole(pool.shape)],
        out_shape=[grad, grad, grad, jax.ShapeDtypeStruct((N_HEADS, 1, seq), F32),
                   jax.ShapeDtypeStruct(gw_out.shape[1:], F32), jax.ShapeDtypeStruct(vecs.shape, F32),
                   jax.ShapeDtypeStruct(pool.shape, F32)],
        scratch_shapes=[pltpu.VMEM((2, LANE, seq), F32), pltpu.VMEM((2, LANE, tile), F32),
                        pltpu.VMEM((2, LANE, tile), F32), pltpu.VMEM(gw_out.shape, BF16),
                        pltpu.VMEM(vecs.shape, F32), pltpu.VMEM(pool.shape, F32)]
        + _reduce_scratch(gw_out, [vecs, pool]),
        compiler_params=_params(("arbitrary",)),
    )(q2t, kp, vp, do2t, gw_out, vecs, pool)


def _inproj_backward(dqp, dkp, dvp, d_cum, f, d_pooled, d_ga, d_gp, x, dxa, u, mod, w_main, w_f, tile):
    seq = x.shape[0]
    nt = seq // tile
    halo_blocks = tile // HALO

    def body(dq_ref, dk_ref, dv_ref, dcum_ref, f_ref, dpo_ref, dph_ref, dga_ref, dgp_ref, x_ref, dxa_ref, u_ref,
             mod_ref, w_ref, wf_ref,
             dx_ref, dproj_ref, dwf_ref, db_ref, dbf_ref, dmod_ref, carry_ref):
        step = pl.program_id(0)
        i = nt - 1 - step

        @pl.when(step == 0)
        def _():
            carry_ref[...] = jnp.zeros_like(carry_ref)
            dwf_ref[...] = jnp.zeros_like(dwf_ref)
            db_ref[...] = jnp.zeros_like(db_ref)
            dbf_ref[...] = jnp.zeros_like(dbf_ref)
            dmod_ref[...] = jnp.zeros_like(dmod_ref)

        ones = jnp.ones((8, tile), BF16)

        def emit(chunk, val):
            cols = pl.ds(chunk * COL_CHUNK, COL_CHUNK)
            db_ref[0:1, cols] += jnp.sum(val, axis=0, keepdims=True)
            vb = val.astype(BF16)
            dproj_ref[:, pl.ds((chunk - 3) * COL_CHUNK, COL_CHUNK)] = vb
            return _dot(vb, w_ref[cols, :])

        d_u = jnp.zeros((tile, D), F32)
        for chunk, ref in enumerate((dq_ref, dk_ref, dv_ref)):
            cols = pl.ds(chunk * COL_CHUNK, COL_CHUNK)
            val_t = ref[:, 0:HEAD_DIM, :].reshape(COL_CHUNK, tile)
            db_ref[:, cols] += _dot_nt(ones, val_t)
            d_u += _dot_tn(val_t, w_ref[cols, :])

        d_pooled = dpo_ref[...]
        d_halo = dph_ref[...] * jnp.where(i < nt - 1, 1.0, 0.0)
        pos = i * tile + lax.broadcasted_iota(jnp.int32, (tile, 1), 0) + 1
        d_p = []
        for g, window in enumerate(POOL_WINDOWS):
            cols = slice(g * GROUP_DIM, (g + 1) * GROUP_DIM)
            scaled = d_pooled[:, cols] / jnp.minimum(pos, window).astype(F32)
            d_p.append(_window_sum(scaled, d_halo[:, cols] * (1.0 / window), window, True) - d_pooled[:, cols])
        d_u += emit(3, jnp.concatenate(d_p, axis=1))
        d_u += emit(4, dga_ref[...])
        d_u += emit(5, dgp_ref[...])

        row = lax.broadcasted_iota(jnp.int32, (tile, tile), 0)
        col = lax.broadcasted_iota(jnp.int32, (tile, tile), 1)
        later = (row >= col).astype(BF16)
        d_logf = sum(_dot(part, later) for part in _split3(dcum_ref[:, 0, :])) + carry_ref[:, 0:1]
        carry_ref[:, 0:1] = d_logf[:, 0:1]
        d_f = d_logf * _sigmoid(-f_ref[...].T[0:N_HEADS, :])
        d_f = jnp.concatenate([d_f, jnp.zeros((LANE - N_HEADS, tile), F32)], axis=0)
        dbf_ref[...] += sum(_dot_nt(ones, part) for part in _split3(d_f))
        d_fb = d_f.astype(BF16)
        d_u += _dot_tn(d_fb, wf_ref[...])
        dwf_ref[...] += _dot(d_fb, u_ref[...])

        x = x_ref[...]
        dx_ref[...] = dxa_ref[...] + d_u * mod_ref[0:1, :]
        dmod_ref[0:1, :] += jnp.sum(d_u * x, axis=0, keepdims=True)
        dmod_ref[1:2, :] += jnp.sum(d_u, axis=0, keepdims=True)

    rev = lambda step: nt - 1 - step
    tok = lambda width: pl.BlockSpec((tile, width), lambda s: (rev(s), 0))
    head_block = pl.BlockSpec((N_HEADS, LANE, tile), lambda s: (0, 0, rev(s)))
    whole = lambda a: pl.BlockSpec(a.shape, lambda s: (0,) * a.ndim)
    halo = pl.BlockSpec((HALO, D_POOL), lambda s: (jnp.minimum((rev(s) + 1) * halo_blocks, seq // HALO - 1), 0))
    small = lambda width: jax.ShapeDtypeStruct((8, width), F32)
    n_rest = N_MAIN - OFF_P
    return pl.pallas_call(
        body, name="inproj_backward", grid=(nt,),
        in_specs=[head_block, head_block, head_block, pl.BlockSpec((N_HEADS, 1, tile), lambda s: (0, 0, rev(s))),
                  tok(LANE), tok(D_POOL), halo, tok(D_ATT), tok(D_POOL),
                  tok(D), tok(D), tok(D),
                  whole(mod), whole(w_main), whole(w_f)],
        out_specs=[tok(D), tok(n_rest), pl.BlockSpec((LANE, D), lambda s: (0, 0)),
                   pl.BlockSpec((8, N_MAIN), lambda s: (0, 0)), pl.BlockSpec((8, LANE), lambda s: (0, 0)),
                   pl.BlockSpec((8, D), lambda s: (0, 0))],
        out_shape=[jax.ShapeDtypeStruct((seq, D), F32), jax.ShapeDtypeStruct((seq, n_rest), BF16),
                   jax.ShapeDtypeStruct((LANE, D), F32), small(N_MAIN), small(LANE), small(D)],
        scratch_shapes=[pltpu.VMEM((8, LANE), F32)],
        compiler_params=_params(("arbitrary",)),
    )(dqp, dkp, dvp, d_cum, f, d_pooled, d_pooled, d_ga, d_gp, x, dxa, u, mod, w_main, w_f)


def _weight_grads(dq_t, dk_t, dv_t, dw_f, dproj, u, k_tile):
    seq = u.shape[0]
    nk = seq // k_tile
    rows = N_HEADS * HEAD_DIM

    def body(dq_ref, dk_ref, dv_ref, dwf_ref, dp_ref, u_ref, out_ref, acc_ref):
        k = pl.program_id(0)

        @pl.when(k == 0)
        def _():
            acc_ref[...] = jnp.zeros_like(acc_ref)

        tokens = u_ref[...]
        for j, ref in enumerate((dq_ref, dk_ref, dv_ref)):
            acc_ref[pl.ds(j * rows, rows), :] += _dot(ref[...].reshape(rows, k_tile), tokens)
        for j in range(dproj.shape[1] // COL_CHUNK):
            cols = pl.ds(j * COL_CHUNK, COL_CHUNK)
            acc_ref[pl.ds(F_HI + j * COL_CHUNK, COL_CHUNK), :] += _dot_tn(dp_ref[:, cols], tokens)

        @pl.when(k == nk - 1)
        def _():
            acc_ref[F_LO:F_HI, :] = dwf_ref[0:N_HEADS, :]
            out_ref[...] = acc_ref[...].astype(BF16)

    heads = pl.BlockSpec((N_HEADS, HEAD_DIM, k_tile), lambda k: (0, 0, k))
    return pl.pallas_call(
        body, name="weight_grads", grid=(nk,),
        in_specs=[heads, heads, heads, pl.BlockSpec(dw_f.shape, lambda k: (0, 0)),
                  pl.BlockSpec((k_tile, dproj.shape[1]), lambda k: (k, 0)), pl.BlockSpec((k_tile, D), lambda k: (k, 0))],
        out_specs=pl.BlockSpec((D_IN, D), lambda k: (0, 0)),
        out_shape=jax.ShapeDtypeStruct((D_IN, D), BF16),
        scratch_shapes=[pltpu.VMEM((D_IN, D), F32)],
        compiler_params=_params(("arbitrary",)),
    )(dq_t, dk_t, dv_t, dw_f, dproj, u)


def _adamw(w, g, m, v):
    m = ADAM_B1 * m + (1.0 - ADAM_B1) * g
    v = ADAM_B2 * v + (1.0 - ADAM_B2) * (g * g)
    m_hat = m / (1.0 - ADAM_B1 ** ADAM_STEP)
    v_hat = v / (1.0 - ADAM_B2 ** ADAM_STEP)
    delta = -ADAM_LR * (m_hat / (jnp.sqrt(v_hat) + ADAM_EPS) + ADAM_WD * w)
    return delta, m, v


def _adamw_call(g, w, m, v, lead_tile, name):
    nr = w.shape[0] // lead_tile

    def body(gi_ref, w_ref, m_ref, v_ref, g_ref, d_ref, nm_ref, nv_ref):
        g = gi_ref[...]
        g_ref[...] = g
        d_ref[...], nm_ref[...], nv_ref[...] = _adamw(w_ref[...], g, m_ref[...], v_ref[...])

    blk = pl.BlockSpec((lead_tile,) + w.shape[1:], lambda r: (r,) + (0,) * (w.ndim - 1))
    shape = jax.ShapeDtypeStruct(w.shape, F32)
    return pl.pallas_call(
        body, name=name, grid=(nr,),
        in_specs=[blk, blk, blk, blk], out_specs=[blk, blk, blk, blk],
        out_shape=[shape, shape, shape, shape],
        compiler_params=_params(("arbitrary",)),
    )(g, w, m, v)


def _ada_adamw(sc_t, d_ada, w, m, v):
    def body(sc_ref, d_ref, w_ref, m_ref, v_ref, g_ref, dl_ref, nm_ref, nv_ref):
        g = sc_ref[:, 0:1] * d_ref[0:1, :]
        for b in range(1, N_DEV):
            g = g + sc_ref[:, b:b + 1] * d_ref[b:b + 1, :]
        g_ref[...] = g
        dl_ref[...], nm_ref[...], nv_ref[...] = _adamw(w_ref[...], g, m_ref[...], v_ref[...])

    row_tile = 256
    blk = pl.BlockSpec((row_tile, w.shape[1]), lambda r: (r, 0))
    shape = jax.ShapeDtypeStruct(w.shape, F32)
    return pl.pallas_call(
        body, name="ada_adamw", grid=(w.shape[0] // row_tile,),
        in_specs=[pl.BlockSpec((row_tile, N_DEV), lambda r: (r, 0)), pl.BlockSpec(d_ada.shape, lambda r: (0, 0)),
                  blk, blk, blk],
        out_specs=[blk, blk, blk, blk], out_shape=[shape, shape, shape, shape],
        compiler_params=_params(("arbitrary",)),
    )(sc_t, d_ada, w, m, v)


F_LO, F_HI = 3 * D_ATT, 3 * D_ATT + N_HEADS


def _split_forget(a, axis):
    idx = lambda lo, hi: tuple(slice(lo, hi) if d == axis else slice(None) for d in range(a.ndim))
    pad = [(0, LANE - N_HEADS) if d == axis else (0, 0) for d in range(a.ndim)]
    return jnp.concatenate([a[idx(0, F_LO)], a[idx(F_HI, D_IN)]], axis=axis), jnp.pad(a[idx(F_LO, F_HI)], pad)


def _join_forget(main, f, axis):
    idx = lambda lo, hi: tuple(slice(lo, hi) if d == axis else slice(None) for d in range(main.ndim))
    return jnp.concatenate([main[idx(0, F_LO)], f[idx(0, N_HEADS)], main[idx(F_LO, N_MAIN)]], axis=axis)


def _adamw_small(grad_rows, row_params, whole_params, summed_params):
    n_row, n_whole, n_sum = len(row_params), len(whole_params), len(summed_params)
    n = n_row + n_whole + n_sum

    def body(g_ref, *refs):
        n_in = 3 * n_row + 4 * (n_whole + n_sum)
        ins, outs = list(refs[:n_in]), refs[n_in:]
        for i in range(n):
            if i < n_row:
                row, lo, hi = row_params[i][:3]
                g = g_ref[row:row + 1, lo:hi]
            elif i < n_row + n_whole:
                g = ins.pop(0)[...]
            else:
                parts = ins.pop(0)
                g = parts[0]
                for k in range(1, N_DEV):
                    g = g + parts[k]
            w, m, v = (ins.pop(0)[...] for _ in range(3))
            outs[4 * i][...] = g
            outs[4 * i + 1][...], outs[4 * i + 2][...], outs[4 * i + 3][...] = _adamw(w, g, m, v)

    shapes = [p[3] for p in row_params] + [p[1] for p in whole_params] + [p[1] for p in summed_params]
    operands = [a for p in row_params for a in p[3:]] + [a for p in whole_params + summed_params for a in p]
    flat = pl.pallas_call(
        body, name="adamw_small",
        out_shape=[jax.ShapeDtypeStruct(w.shape, F32) for w in shapes for _ in range(4)],
        compiler_params=_params(),
    )(grad_rows, *operands)
    return [flat[4 * i:4 * i + 4] for i in range(n)]


def kernel(x, c, w_ada, b_ada, w_in, b_in, w_pool_mix, b_pool_mix, pool_scale, w_out, b_out, ln_g, ln_b, loss_target, m_w_ada, m_b_ada, m_w_in, m_b_in, m_w_pool_mix, m_b_pool_mix, m_pool_scale, m_w_out, m_b_out, m_ln_g, m_ln_b, v_w_ada, v_b_ada, v_w_in, v_b_in, v_w_pool_mix, v_b_pool_mix, v_pool_scale, v_w_out, v_b_out, v_ln_g, v_ln_b):
    seq = x.shape[1]
    tile = min(256, seq)
    attn_tile = min(512, max(128, seq // 4))
    me = _dev_index(*_mesh_pos())
    x2, tgt = x[0], loss_target[0]

    rows_in = D_IN // N_DEV
    w_in_g, sc_all, ada_mine = _gather_and_ada(c, w_in[0].T.astype(BF16), w_ada[0])
    ada = ada_mine.reshape(1, D_ADA) + b_ada
    shift, scale, gate = ada[:, 0:D], ada[:, D:2 * D], ada[:, 2 * D:]
    mod = jnp.concatenate([1.0 + scale, shift, jnp.zeros((6, D), F32)], axis=0)

    w_main, w_f = _split_forget(w_in_g.reshape(D_IN, D), 0)
    b_main, b_f = _split_forget(b_in, 1)

    qp, kp, vp, f, p, g_att, g_pool, u = _inproj_forward(x2, mod, w_main, w_f, b_main, b_f, tile)
    att, q2t, w_out_g = _attention_forward(qp, kp, vp, w_out[0].astype(BF16), attn_tile)

    vecs = jnp.concatenate([gate, b_out, ln_g, ln_b, jnp.zeros((4, D), F32)], axis=0)
    pool_vecs = jnp.concatenate([b_pool_mix.reshape(1, D_POOL), pool_scale, jnp.zeros((6, D_POOL), F32)], axis=0)
    dxa, do2, d_ga, d_gp, d_pooled, gw_out, dw_pool, dvec = _middle(
        x2, tgt, att, g_att, g_pool, p, vecs, pool_vecs, w_out_g.reshape(D, D), w_pool_mix[0].astype(BF16), tile)

    pool_rows = w_pool_mix.shape[1] * GROUP_DIM
    dqp, dkp, dvp, d_cum, g_out, dvec_sum, dw_pool_sum = _attention_backward(
        q2t, kp, vp, do2, gw_out.reshape(N_DEV, D // N_DEV, D), dvec, dw_pool.reshape(pool_rows, GROUP_DIM), attn_tile)
    dx, dproj, dw_f, db_main, db_f, dmod = _inproj_backward(
        dqp, dkp, dvp, d_cum, f, d_pooled, d_ga, d_gp, x2, dxa, u, mod, w_main, w_f, tile)
    gw_in = _weight_grads(dqp, dkp, dvp, dw_f, dproj, u, min(512, seq)).reshape(N_DEV, rows_in, D)
    d_ada = jnp.concatenate([dmod[1:2], dmod[0:1], dvec[5:6]], axis=1)
    g_in_rows, g_b_in, d_ada_all = _reduce_grads(gw_in, _join_forget(db_main[0:1], db_f[0:1], 1), d_ada)

    def rows3(a):
        return a[0].T.reshape(rows_in, D // LANE, LANE)

    outs_in = _adamw_call(g_in_rows.reshape(rows_in, D // LANE, LANE), rows3(w_in), rows3(m_w_in), rows3(v_w_in),
                          rows_in // 5, "adamw_w_in")
    g_w_in, d_w_in, nm_w_in, nv_w_in = (a.reshape(rows_in, D).T for a in outs_in)
    flat_pool = lambda a: a.reshape(1, D_POOL)
    pool_2d = lambda a: a.reshape(pool_rows, GROUP_DIM)
    rows = _adamw_small(
        dvec_sum,
        [(0, 0, D, b_out, m_b_out, v_b_out), (1, 0, D, ln_g, m_ln_g, v_ln_g), (2, 0, D, ln_b, m_ln_b, v_ln_b),
         (3, 0, D_POOL, flat_pool(b_pool_mix), flat_pool(m_b_pool_mix), flat_pool(v_b_pool_mix)),
         (3, D_POOL, 2 * D_POOL, pool_scale, m_pool_scale, v_pool_scale)],
        [(g_out, w_out[0], m_w_out[0], v_w_out[0]),
         (dw_pool_sum, pool_2d(w_pool_mix), pool_2d(m_w_pool_mix), pool_2d(v_w_pool_mix)),
         (g_b_in, b_in, m_b_in, v_b_in)],
        [(d_ada_all, b_ada, m_b_ada, v_b_ada)])
    small = {"b_out": rows[0], "ln_g": rows[1], "ln_b": rows[2],
             "b_pool": [a.reshape(b_pool_mix.shape) for a in rows[3]], "pool_scale": rows[4],
             "w_pool": [a.reshape(w_pool_mix.shape) for a in rows[6]], "b_in": rows[7]}
    g_s, d_s, nm_s, nv_s = ({k: r[j] for k, r in small.items()} for j in range(4))
    g_w_out, d_w_out, nm_w_out, nv_w_out = rows[5]
    g_b_ada, d_b_ada, nm_b_ada, nv_b_ada = rows[8]
    loss = dvec_sum[4, 0]

    d_ada_local = lax.dynamic_slice_in_dim(d_ada_all.reshape(N_DEV, D_ADA), me * (D_ADA // N_DEV), D_ADA // N_DEV, axis=1)
    g_w_ada, d_w_ada, nm_w_ada, nv_w_ada = _ada_adamw(sc_all.T, d_ada_local, w_ada[0], m_w_ada[0], v_w_ada[0])

    def ordered(w_ada_, b_ada_, w_in_, w_out_, s):
        return (w_ada_[None], b_ada_, w_in_[None], s["b_in"], s["w_pool"], s["b_pool"], s["pool_scale"],
                w_out_[None], s["b_out"], s["ln_g"], s["ln_b"])

    return (loss, dx[None],
            *ordered(g_w_ada, g_b_ada, g_w_in, g_w_out, g_s),
            *ordered(d_w_ada, d_b_ada, d_w_in, d_w_out, d_s),
            *ordered(nm_w_ada, nm_b_ada, nm_w_in, nm_w_out, nm_s),
            *ordered(nv_w_ada, nv_b_ada, nv_w_in, nv_w_out, nv_s))
```

```python
import jax
import jax.numpy as jnp
from jax import lax
from jax.experimental import pallas as pl
from jax.experimental.pallas import tpu as pltpu

F32 = jnp.float32
BF16 = jnp.bfloat16

N_DEV = 8
D = 1024
N_HEADS = 8
HEAD_DIM = 64
D_ATT = 512
D_POOL = 512
POOL_WINDOWS = (2, 4, 8, 16)
GROUP_DIM = 128
HALO = 16
LANE = 128
D_IN = 3080
D_ADA = 3072
N_MAIN = 3072
OFF_P = 1536
COL_CHUNK = 512
Q_SCALE = 0.125
LN_EPS = 1e-5
ALPHA = 2.0 ** 0.25
L_CQ, L_CK, L_LSE = 64, 67, 70

ADAM_LR, ADAM_B1, ADAM_B2, ADAM_EPS, ADAM_WD, ADAM_STEP = 0.001, 0.9, 0.999, 1e-08, 0.01, 10
VMEM_LIMIT = 56 * 1024 * 1024

MESH = pl.DeviceIdType.MESH
ANY = pl.BlockSpec(memory_space=pl.ANY)


def _params(sem=None, vmem=VMEM_LIMIT):
    return pltpu.CompilerParams(dimension_semantics=sem, vmem_limit_bytes=vmem)


def _split3(a):
    hi = a.astype(BF16)
    r = a - hi.astype(F32)
    mid = r.astype(BF16)
    lo = (r - mid.astype(F32)).astype(BF16)
    return hi, mid, lo


def _dot(a, b):
    return jnp.dot(a, b, preferred_element_type=F32)


def _dot_nt(a, b):
    return lax.dot_general(a, b, (((1,), (1,)), ((), ())), preferred_element_type=F32)


def _dot_tn(a, b):
    return lax.dot_general(a, b, (((0,), (0,)), ((), ())), preferred_element_type=F32)


def _dot3(m01, a):
    hi, mid, lo = _split3(a)
    return _dot(m01, hi) + _dot(m01, mid) + _dot(m01, lo)


def _sigmoid(z):
    return 1.0 / (1.0 + jnp.exp(-z))


def _lanes(shape):
    return lax.broadcasted_iota(jnp.int32, shape, len(shape) - 1)


def _place3(lane, base, parts, other):
    out = other
    for j in range(3):
        out = jnp.where(lane == base + j, parts[j], out)
    return out


def _mesh_pos():
    return lax.axis_index("x"), lax.axis_index("y"), lax.axis_index("c")


def _dev_index(px, py, pc):
    return 4 * px + 2 * py + pc


N_GATHER_SEMS = 9


def _gather_stages(src_ref, out_ref, send_sems, recv_sems, local_sem):
    x, y, c = _mesh_pos()
    me, sibling = (x, y, c), (x, y, 1 - c)
    nbr_x, nbr_y, diag = (1 - x, y), (x, 1 - y), (1 - x, 1 - y)
    half = out_ref.shape[-1] // 2
    left, right = pl.ds(0, half), pl.ds(half, half)

    def copy(k, block, to, cols=None, src=None):
        slot = out_ref.at[_dev_index(*block)]
        if cols is not None:
            slot = slot.at[:, cols]
        return pltpu.make_async_remote_copy(
            src_ref=slot if src is None else src, dst_ref=slot, send_sem=send_sems.at[k], recv_sem=recv_sems.at[k],
            device_id=to, device_id_type=MESH)

    mine = pltpu.make_async_copy(src_ref, out_ref.at[_dev_index(*me)], local_sem)
    first = [copy(0, me, sibling, src=src_ref), copy(1, me, (*nbr_x, c), src=src_ref), copy(2, me, (*nbr_y, c), src=src_ref)]
    relay = [(1, nbr_x, None, nbr_x), (2, nbr_y, None, nbr_y), (3, diag, left, nbr_y), (4, diag, right, nbr_x)]
    onward = [copy(3, (*nbr_x, c), (*nbr_y, c), cols=left), copy(4, (*nbr_y, c), (*nbr_x, c), cols=right)]
    passed = [copy(4 + k, (*block, c), sibling, cols=cols) for k, block, cols, _ in relay]

    def start():
        mine.start()
        for cp in first:
            cp.start()

    def relay_stage(first_item):
        def run():
            for j in (first_item, first_item + 1):
                k, block, cols, frm = relay[j]
                copy(k, (*block, c), (*frm, c), cols=cols).wait_recv()
                if j < 2:
                    onward[j].start()
                passed[j].start()
        return run

    def finish():
        copy(0, sibling, me).wait_recv()
        for k, block, cols, _ in relay:
            copy(4 + k, (*block, 1 - c), me, cols=cols).wait_recv()
        for cp in first + onward + passed:
            cp.wait_send()
        mine.wait()

    return start, relay_stage(0), relay_stage(2), finish


N_REDUCE_SEMS = 10
N_SMALL_SEMS = 4
N_ROWS_SEMS = 7


def _reduce_stages(ins, gs, r1, s2, r2, smalls, send_sems, recv_sems, rows=None, own=None):
    n = len(ins)
    x, y, c = _mesh_pos()
    me = _dev_index(x, y, c)
    sibling = (x, y, 1 - c)
    chips = [(x, y), (1 - x, y), (x, 1 - y), (1 - x, 1 - y)]
    peers = []
    for p in range(1, N_DEV):
        px, py, pc = (p >> 2) & 1, (p >> 1) & 1, p & 1
        peers.append((1 - x if px else x, 1 - y if py else y, 1 - c if pc else c))
    base_small = N_REDUCE_SEMS * n

    def remote(src, dst, k, to):
        return pltpu.make_async_remote_copy(src_ref=src, dst_ref=dst, send_sem=send_sems.at[k],
                                            recv_sem=recv_sems.at[k], device_id=to, device_id_type=MESH)

    def level1(a, q):
        return remote(ins[a].at[_dev_index(*chips[q], 1 - c)], r1[a].at[q], N_REDUCE_SEMS * a + q, sibling)

    def level2(a, k):
        half = ins[a].shape[-1] // 2
        left, right = pl.ds(0, half), pl.ds(half, half)
        nbr_x, nbr_y = (*chips[1], c), (*chips[2], c)
        src_slot, dst_slot, cols, to = [(0, 0, left, nbr_x), (1, 1, right, nbr_y), (2, 2, left, nbr_x),
                                        (2, 2, right, nbr_y), (0, 0, right, nbr_x), (1, 1, left, nbr_y)][k]
        return remote(s2[a].at[src_slot, :, cols], r2[a].at[dst_slot, :, cols], N_REDUCE_SEMS * a + 4 + k, to)

    to_sibling = [remote(sm[0], sm[2], base_small + 4 * i, sibling) for i, sm in enumerate(smalls)]
    to_chips = [[remote(sm[3], sm[4].at[j], base_small + 4 * i + 1 + j, (*chips[j + 1], c)) for j in range(3)]
                for i, sm in enumerate(smalls)]
    if rows is not None:
        rows_ref, land_ref, all_ref = rows
        base_rows = base_small + 4 * len(smalls)
        row_sends = [remote(rows_ref, land_ref.at[me], base_rows + k, to) for k, to in enumerate(peers)]

    def mine(a, q):
        buf, sems = own[a]
        return pltpu.make_async_copy(ins[a].at[_dev_index(*chips[q], c)], buf.at[q], sems.at[q])

    def start():
        for a in range(n):
            for q in range(4):
                level1(a, q).start()
            if own is not None:
                for q in (1, 2, 3, 0):
                    mine(a, q).start()
        for cp in to_sibling:
            cp.start()
        if rows is not None:
            for cp in row_sends:
                cp.start()
            land_ref[me] = rows_ref[...]

    def middle():
        for a in range(n):
            for q in (1, 2, 3, 0):
                level1(a, q).wait_recv()
                if own is None:
                    kept = ins[a][_dev_index(*chips[q], c)]
                else:
                    mine(a, q).wait()
                    kept = own[a][0][q]
                pair = kept.astype(F32) + r1[a][q].astype(F32)
                if q == 0:
                    gs[a][...] = pair
                else:
                    s2[a][q - 1] = pair.astype(BF16)
                    for k in ((0,), (1,), (2, 3))[q - 1]:
                        level2(a, k).start()
        for i, (small_ref, _, sm_sib, sm_chip, _) in enumerate(smalls):
            to_sibling[i].wait_recv()
            sm_chip[...] = small_ref[...] + sm_sib[...]
            for cp in to_chips[i]:
                cp.start()

    def fold():
        for a in range(n):
            half = ins[a].shape[-1] // 2
            level2(a, 3).wait_recv()
            s2[a][0, :, half:] = (s2[a][0, :, half:].astype(F32) + r2[a][2, :, half:].astype(F32)).astype(BF16)
            level2(a, 4).start()
            level2(a, 2).wait_recv()
            s2[a][1, :, :half] = (s2[a][1, :, :half].astype(F32) + r2[a][2, :, :half].astype(F32)).astype(BF16)
            level2(a, 5).start()

    def finish():
        for a in range(n):
            for k in (0, 1, 4, 5):
                level2(a, k).wait_recv()
            gs[a][...] = gs[a][...] + r2[a][0].astype(F32) + r2[a][1].astype(F32)
            for q in range(4):
                level1(a, q).wait_send()
            for k in range(6):
                level2(a, k).wait_send()
        for i, (_, total_ref, _, sm_chip, sm_recv) in enumerate(smalls):
            for cp in to_chips[i]:
                cp.wait_recv()
            total = None
            for ax in range(2):
                for ay in range(2):
                    dx, dy = x != ax, y != ay
                    term = jnp.where(dx, jnp.where(dy, sm_recv[2], sm_recv[0]), jnp.where(dy, sm_recv[1], sm_chip[...]))
                    total = term if total is None else total + term
            total_ref[...] = total
            for cp in [to_sibling[i]] + to_chips[i]:
                cp.wait_send()
        if rows is not None:
            for k, frm in enumerate(peers):
                remote(rows_ref, land_ref.at[_dev_index(*frm)], base_rows + k, frm).wait_recv()
            all_ref[...] = land_ref[...]
            for cp in row_sends:
                cp.wait_send()

    return start, middle, fold, finish


def _reduce_scratch(shard, smalls, rows=None):
    out = [pltpu.VMEM((lead,) + shard.shape[1:], BF16) for lead in (4, 3, 3)]
    for small in smalls:
        out += [pltpu.VMEM(small.shape, F32), pltpu.VMEM(small.shape, F32), pltpu.VMEM((3,) + small.shape, F32)]
    n_sems = N_REDUCE_SEMS + N_SMALL_SEMS * len(smalls)
    if rows is not None:
        out.append(pltpu.VMEM((N_DEV,) + rows.shape, F32))
        n_sems += N_ROWS_SEMS
    return out + [pltpu.SemaphoreType.DMA((n_sems,))] * 2


def _reduce_grads(gw_in, small, rows):
    def body(in_ref, small_ref, rows_ref, g_ref, total_ref, rows_all_ref,
             r1, s2, r2, sm_sib, sm_chip, sm_recv, rows_land, send_sems, recv_sems, kept, kept_sems):
        stages = _reduce_stages(
            [in_ref], [g_ref], [r1], [s2], [r2], [(small_ref, total_ref, sm_sib, sm_chip, sm_recv)],
            send_sems, recv_sems, rows=(rows_ref, rows_land, rows_all_ref), own=[(kept, kept_sems)])
        for stage in stages:
            stage()

    vmem = pl.BlockSpec(memory_space=pltpu.VMEM)
    return pl.pallas_call(
        body, name="reduce_grads",
        in_specs=[ANY, vmem, vmem], out_specs=[vmem, vmem, vmem],
        out_shape=[jax.ShapeDtypeStruct(gw_in.shape[1:], F32), jax.ShapeDtypeStruct(small.shape, F32),
                   jax.ShapeDtypeStruct((N_DEV,) + rows.shape, F32)],
        scratch_shapes=_reduce_scratch(gw_in, [small], rows)
        + [pltpu.VMEM((4,) + gw_in.shape[1:], BF16), pltpu.SemaphoreType.DMA((4,))],
        compiler_params=_params(),
    )(gw_in, small, rows)


def _dot3_rhs(a, b):
    a0, a1, a2 = _split3(a)
    b0, b1, b2 = _split3(b)
    return (_dot(a0, b0) + (_dot(a0, b1) + _dot(a1, b0))
            + (_dot(a0, b2) + _dot(a1, b1) + _dot(a2, b0)))


def _gather_and_ada(c, w_in_rows, w_ada):
    cols = w_ada.shape[1]

    def body(c_ref, w_ref, wa_ref, w_all_ref, sc_ref, ada_ref,
             c_land, part, ada_land, send_sems, recv_sems, local_sem, x_send, x_recv):
        x, y, cc = _mesh_pos()
        me = _dev_index(x, y, cc)
        peers = []
        for p in range(1, N_DEV):
            px, py, pc = (p >> 2) & 1, (p >> 1) & 1, p & 1
            peers.append((1 - x if px else x, 1 - y if py else y, 1 - cc if pc else cc))

        def remote(src, dst, k, to):
            return pltpu.make_async_remote_copy(src_ref=src, dst_ref=dst, send_sem=x_send.at[k], recv_sem=x_recv.at[k],
                                                device_id=to, device_id_type=MESH)

        c_sends = [remote(c_ref, c_land.at[me], k, to) for k, to in enumerate(peers)]
        for cp in c_sends:
            cp.start()
        start, relay_near, relay_far, finish = _gather_stages(w_ref, w_all_ref, send_sems, recv_sems, local_sem.at[0])
        start()
        c_land[me] = c_ref[...]
        for k, frm in enumerate(peers):
            remote(c_ref, c_land.at[_dev_index(*frm)], k, frm).wait_recv()
        c_all = jnp.concatenate([c_land[b] for b in range(N_DEV)], axis=0)
        sc = c_all * _sigmoid(c_all)
        sc_ref[...] = sc
        rows = _dot3_rhs(sc, wa_ref[...])
        for b in range(N_DEV):
            part[b] = rows[b:b + 1, :]
        a_sends = [remote(part.at[_dev_index(*to)], ada_land.at[me], 7 + k, to) for k, to in enumerate(peers)]
        for cp in a_sends:
            cp.start()
        ada_land[me] = part[me]
        for k, frm in enumerate(peers):
            remote(part.at[0], ada_land.at[_dev_index(*frm)], 7 + k, frm).wait_recv()
        ada_ref[...] = ada_land[...]

        relay_near()
        relay_far()
        finish()
        for cp in c_sends + a_sends:
            cp.wait_send()

    vmem = pl.BlockSpec(memory_space=pltpu.VMEM)
    return pl.pallas_call(
        body, name="gather_weights",
        in_specs=[vmem, ANY, vmem], out_specs=[ANY, vmem, vmem],
        out_shape=[jax.ShapeDtypeStruct((N_DEV,) + w_in_rows.shape, w_in_rows.dtype),
                   jax.ShapeDtypeStruct((N_DEV, D), F32), jax.ShapeDtypeStruct((N_DEV, 1, cols), F32)],
        scratch_shapes=[pltpu.VMEM((N_DEV, 1, D), F32), pltpu.VMEM((N_DEV, 1, cols), F32), pltpu.VMEM((N_DEV, 1, cols), F32),
                        pltpu.SemaphoreType.DMA((N_GATHER_SEMS,)), pltpu.SemaphoreType.DMA((N_GATHER_SEMS,)),
                        pltpu.SemaphoreType.DMA((1,)),
                        pltpu.SemaphoreType.DMA((14,)), pltpu.SemaphoreType.DMA((14,))],
        compiler_params=_params(),
    )(c, w_in_rows, w_ada)


def _inproj_forward(x, mod, w_main, w_f, b_main, b_f, tile):
    seq = x.shape[0]
    nt = seq // tile

    def body(x_ref, mod_ref, w_ref, wf_ref, b_ref, bf_ref,
             qp_ref, kp_ref, vp_ref, f_ref, p_ref, ga_ref, gp_ref, u_ref, carry_ref):
        i = pl.program_id(0)

        @pl.when(i == 0)
        def _():
            carry_ref[...] = jnp.zeros_like(carry_ref)

        u = x_ref[...] * mod_ref[0:1, :] + mod_ref[1:2, :]
        ub = u.astype(BF16)
        u_ref[...] = ub

        f = _dot_nt(ub, wf_ref[...]) + bf_ref[...]
        f_ref[...] = f
        lane = _lanes((tile, LANE))
        log_f = jnp.where(lane < N_HEADS, jnp.minimum(f, 0.0) - jnp.log(1.0 + jnp.exp(-jnp.abs(f))), 0.0)
        row = lax.broadcasted_iota(jnp.int32, (tile, tile), 0)
        col = lax.broadcasted_iota(jnp.int32, (tile, tile), 1)
        tri = (row >= col).astype(BF16)
        cum = _dot3(tri, log_f) + carry_ref[0:1, :]
        carry_ref[0:1, :] = cum[tile - 1:tile, :]
        cq = [part.astype(F32) for part in _split3(cum)]
        ck = [part.astype(F32) for part in _split3(-cum)]

        def proj(chunk):
            cols = pl.ds(chunk * COL_CHUNK, COL_CHUNK)
            return _dot_nt(ub, w_ref[cols, :]) + b_ref[:, cols]

        def head_tiles(r):
            for pair in range(N_HEADS // 2):
                both = r[:, pair * LANE:(pair + 1) * LANE]
                yield 2 * pair, both
                yield 2 * pair + 1, pltpu.roll(both, HEAD_DIM, 1)

        for h, val in head_tiles(proj(0)):
            extra = jnp.where((lane >= L_CK) & (lane < L_CK + 3), 1.0, 0.0)
            extra = _place3(lane, L_CQ, [part[:, h:h + 1] for part in cq], extra)
            qp_ref[h] = jnp.where(lane < HEAD_DIM, val * Q_SCALE, extra).astype(BF16)
        for h, val in head_tiles(proj(1)):
            ones = ((lane >= L_CQ) & (lane < L_CQ + 3)) | ((lane >= L_LSE) & (lane < L_LSE + 3))
            extra = _place3(lane, L_CK, [part[:, h:h + 1] for part in ck], jnp.where(ones, 1.0, 0.0))
            kp_ref[h] = jnp.where(lane < HEAD_DIM, val, extra).astype(BF16)
        for h, val in head_tiles(proj(2)):
            extra = jnp.where((lane >= HEAD_DIM) & (lane < HEAD_DIM + 3), -1.0, 0.0)
            vp_ref[h] = jnp.where(lane < HEAD_DIM, val, extra).astype(BF16)
        p_ref[...] = proj(3)
        ga_ref[...] = proj(4)
        gp_ref[...] = proj(5)

    head_block = pl.BlockSpec((N_HEADS, tile, LANE), lambda i: (0, i, 0))
    tok = lambda width: pl.BlockSpec((tile, width), lambda i: (i, 0))
    whole = lambda a: pl.BlockSpec(a.shape, lambda i: (0,) * a.ndim)
    padded = jax.ShapeDtypeStruct((N_HEADS, seq, LANE), BF16)
    half = jax.ShapeDtypeStruct((seq, D_ATT), F32)
    return pl.pallas_call(
        body, name="inproj_forward", grid=(nt,),
        in_specs=[tok(D), whole(mod), whole(w_main), whole(w_f), whole(b_main), whole(b_f)],
        out_specs=[head_block, head_block, head_block, tok(LANE), tok(D_POOL), tok(D_ATT), tok(D_POOL),
                   tok(D)],
        out_shape=[padded, padded, padded, jax.ShapeDtypeStruct((seq, LANE), F32), half, half, half,
                   jax.ShapeDtypeStruct((seq, D), BF16)],
        scratch_shapes=[pltpu.VMEM((8, LANE), F32)],
        compiler_params=_params(("arbitrary",)),
    )(x, mod, w_main, w_f, b_main, b_f)


def _attention_forward(qp, kp, vp, w_out, tile):
    seq = qp.shape[1]
    nb = seq // tile
    steps = (N_HEADS // 2) * nb

    def body(q_ref, k_ref, v_ref, wo_ref, att_ref, q2t_ref, wo_all_ref, s_a, s_b, m_ref, acc_ref,
             send_sems, recv_sems, local_sem):
        step = pl.program_id(0) * nb + pl.program_id(1)
        start, relay_near, relay_far, finish = _gather_stages(wo_ref, wo_all_ref, send_sems, recv_sems, local_sem.at[0])
        pl.when(step == 0)(start)
        pl.when(step == steps // 4)(relay_near)
        pl.when(step == (3 * steps) // 4)(relay_far)

        i = pl.program_id(1)
        sub = lax.broadcasted_iota(jnp.int32, (LANE, tile), 0)
        row = lax.broadcasted_iota(jnp.int32, (tile, tile), 0)
        col = lax.broadcasted_iota(jnp.int32, (tile, tile), 1)
        q = [q_ref[0], q_ref[1]]

        def scores(buf, kb):
            rows = pl.ds(pl.multiple_of(kb * tile, tile), tile)
            for hh in range(2):
                buf[hh] = _dot_nt(k_ref[hh, rows, :], q[hh])

        def absorb(buf, kb, masked):
            rows = pl.ds(pl.multiple_of(kb * tile, tile), tile)
            for hh in range(2):
                m = m_ref[hh, 0:1, :]
                s = buf[hh]
                if masked:
                    s = jnp.where(row <= col, s, -1e30)
                m_new = jnp.maximum(m, jnp.max(s, axis=0, keepdims=True))
                p = jnp.exp(s - m_new).astype(BF16)
                acc_ref[hh] = jnp.exp(m - m_new) * acc_ref[hh] + _dot_tn(v_ref[hh, rows, :], p)
                m_ref[hh, 0:1, :] = m_new

        def two_blocks(j, _):
            scores(s_b, 2 * j + 1)
            absorb(s_a, 2 * j, False)
            scores(s_a, 2 * j + 2)
            absorb(s_b, 2 * j + 1, False)
            return 0

        def last_block():
            absorb(s_a, i, True)

        def last_two_blocks():
            scores(s_b, i)
            absorb(s_a, i - 1, False)
            absorb(s_b, i, True)

        scores(s_a, 0)
        m_ref[...] = jnp.full(m_ref.shape, -1e30, F32)
        acc_ref[...] = jnp.zeros_like(acc_ref)
        lax.fori_loop(0, i // 2, two_blocks, 0)
        lax.cond(i % 2 == 0, last_block, last_two_blocks)
        outs = []
        for hh in range(2):
            m, acc = m_ref[hh, 0:1, :], acc_ref[hh]
            l = -acc[HEAD_DIM:HEAD_DIM + 1, :]
            outs.append((acc / l)[:HEAD_DIM, :])
            neg_lse = [part.astype(F32) for part in _split3(-(m + jnp.log(l)))]
            q2t_ref[hh] = _place3(sub, L_LSE, neg_lse, q[hh].astype(F32).T).astype(BF16)
        att_ref[...] = jnp.concatenate(outs, axis=0).T
        pl.when(step == steps - 1)(finish)

    pair = pl.BlockSpec((2, tile, LANE), lambda hp, i: (hp, i, 0))
    full = pl.BlockSpec((2, seq, LANE), lambda hp, i: (hp, 0, 0))
    return pl.pallas_call(
        body, name="attention_forward", grid=(N_HEADS // 2, nb),
        in_specs=[pair, full, full, ANY],
        out_specs=[pl.BlockSpec((tile, LANE), lambda hp, i: (i, hp)),
                   pl.BlockSpec((2, LANE, tile), lambda hp, i: (hp, 0, i)), ANY],
        out_shape=[jax.ShapeDtypeStruct((seq, D_ATT), F32),
                   jax.ShapeDtypeStruct((N_HEADS, LANE, seq), BF16),
                   jax.ShapeDtypeStruct((N_DEV,) + w_out.shape, w_out.dtype)],
        scratch_shapes=[pltpu.VMEM((2, tile, tile), F32), pltpu.VMEM((2, tile, tile), F32),
                        pltpu.VMEM((2, 8, tile), F32), pltpu.VMEM((2, LANE, tile), F32),
                        pltpu.SemaphoreType.DMA((N_GATHER_SEMS,)), pltpu.SemaphoreType.DMA((N_GATHER_SEMS,)),
                        pltpu.SemaphoreType.DMA((1,))],
        compiler_params=_params(("arbitrary", "arbitrary")),
    )(qp, kp, vp, w_out)


def _window_sum(x, halo, window, transposed):
    tile = x.shape[0]

    def split_cat(a):
        hi = a.astype(BF16)
        return jnp.concatenate([hi, (a - hi.astype(F32)).astype(BF16)], axis=1)

    def fold(r):
        return r[:, :LANE] + r[:, LANE:]

    r = lax.broadcasted_iota(jnp.int32, (tile, tile), 0)
    c = lax.broadcasted_iota(jnp.int32, (tile, tile), 1)
    rh = lax.broadcasted_iota(jnp.int32, (HALO, HALO), 0)
    ch = lax.broadcasted_iota(jnp.int32, (HALO, HALO), 1)
    if not transposed:
        band = (c <= r) & (r - c < window)
        edge = (rh + HALO - ch) < window
    else:
        band = (r <= c) & (c - r < window)
        edge = (HALO + ch - rh) < window
    out = fold(_dot(band.astype(BF16), split_cat(x)))
    reach = fold(_dot(edge.astype(BF16), split_cat(halo)))
    if not transposed:
        return jnp.concatenate([out[:HALO] + reach, out[HALO:]], axis=0)
    return jnp.concatenate([out[:tile - HALO], out[tile - HALO:] + reach], axis=0)


def _silu_parts(g):
    sig = _sigmoid(g)
    return g * sig, sig * (1.0 + g * (1.0 - sig))


def _middle(x, tgt, att, g_att, g_pool, p, vecs, pool_vecs, w_out, w_pool, tile):
    seq = x.shape[0]
    nt = seq // tile
    halo_blocks = tile // HALO

    def body(x_ref, tgt_ref, att_ref, ga_ref, gp_ref, p_ref, ph_ref, vec_ref, pvec_ref, wo_ref, wp_ref,
             dxa_ref, do2_ref, dga_ref, dgp_ref, dpooled_ref, gwo_ref, dwp_ref, dvec_ref, dwo_ref, dpvec_ref):
        i = pl.program_id(0)

        @pl.when(i == 0)
        def _():
            dwo_ref[...] = jnp.zeros_like(dwo_ref)
            dwp_ref[...] = jnp.zeros_like(dwp_ref)
            dvec_ref[...] = jnp.zeros_like(dvec_ref)
            dpvec_ref[...] = jnp.zeros_like(dpvec_ref)

        gate, b_out, ln_g, ln_b = (vec_ref[k:k + 1, :] for k in range(4))
        b_pool, pool_scale = pvec_ref[0:1, :], pvec_ref[1:2, :]
        x = x_ref[...]
        p = p_ref[...]
        p_halo = ph_ref[...] * jnp.where(i > 0, 1.0, 0.0)
        pos = i * tile + lax.broadcasted_iota(jnp.int32, (tile, 1), 0) + 1

        pooled, mixed = [], []
        for g, window in enumerate(POOL_WINDOWS):
            cols = slice(g * GROUP_DIM, (g + 1) * GROUP_DIM)
            wsum = _window_sum(p[:, cols], p_halo[:, cols], window, False)
            count = jnp.minimum(pos, window).astype(F32)
            pooled.append(wsum / count - p[:, cols])
            mixed.append(_dot(pooled[g].astype(BF16), wp_ref[g]) + b_pool[:, cols])
        mixed = jnp.concatenate(mixed, axis=1)
        pool = mixed * pool_scale

        att = att_ref[...]
        g_att, g_pool = ga_ref[...], gp_ref[...]
        silu_a, dsilu_a = _silu_parts(g_att)
        silu_p, dsilu_p = _silu_parts(g_pool)
        y_in = jnp.concatenate([att * silu_a, pool * silu_p], axis=1)
        y = _dot(y_in.astype(BF16), wo_ref[...]) + b_out
        h = ALPHA * x + gate * y
        mu = jnp.mean(h, axis=1, keepdims=True)
        hc = h - mu
        var = jnp.mean(hc * hc, axis=1, keepdims=True)
        rstd = lax.rsqrt(var + LN_EPS)
        yhat = hc * rstd
        diff = yhat * ln_g + ln_b - tgt_ref[...]
        loss_rows = jnp.sum(diff * diff, axis=1, keepdims=True)
        d_out = diff * (1.0 / D)

        d_yhat = d_out * ln_g
        dh = rstd * (d_yhat - jnp.mean(d_yhat, axis=1, keepdims=True)
                     - yhat * jnp.mean(d_yhat * yhat, axis=1, keepdims=True))
        dxa_ref[...] = ALPHA * dh
        dy = dh * gate
        dyb = dy.astype(BF16)
        lane = _lanes((1, D))
        loss_row = jnp.where(lane == 0, (0.5 / D) * jnp.sum(loss_rows, axis=0, keepdims=True), 0.0)
        dvec_ref[5:6, :] += jnp.sum(dh * y, axis=0, keepdims=True)
        dvec_ref[0:1, :] += jnp.sum(dy, axis=0, keepdims=True)
        dvec_ref[1:2, :] += jnp.sum(d_out * yhat, axis=0, keepdims=True)
        dvec_ref[2:3, :] += jnp.sum(d_out, axis=0, keepdims=True)
        dvec_ref[4:5, :] += loss_row

        dwo_ref[...] += _dot(y_in.T.astype(BF16), dyb)
        d_yin = _dot_nt(dyb, wo_ref[...])
        d_a, d_pl = d_yin[:, :D_ATT], d_yin[:, D_ATT:]
        d_att = d_a * silu_a
        d_att_t = d_att.T
        prod_t = (d_att * att).T
        sub = lax.broadcasted_iota(jnp.int32, (HEAD_DIM, tile), 0)
        for h in range(N_HEADS):
            rows = slice(h * HEAD_DIM, (h + 1) * HEAD_DIM)
            delta = jnp.sum(prod_t[rows], axis=0, keepdims=True)
            extra = _place3(sub, 0, [part.astype(F32) for part in _split3(delta)], 0.0)
            do2_ref[h] = jnp.concatenate([d_att_t[rows], extra], axis=0).astype(BF16)
        dga_ref[...] = d_a * att * dsilu_a
        dgp_ref[...] = d_pl * pool * dsilu_p
        d_pool = d_pl * silu_p
        d_mixed = d_pool * pool_scale
        dpvec_ref[0:1, :] += jnp.sum(d_mixed, axis=0, keepdims=True)
        dpvec_ref[1:2, :] += jnp.sum(d_pool * mixed, axis=0, keepdims=True)
        d_pooled = []
        for g in range(len(POOL_WINDOWS)):
            cols = slice(g * GROUP_DIM, (g + 1) * GROUP_DIM)
            dmb = d_mixed[:, cols].astype(BF16)
            dwp_ref[g] += _dot(pooled[g].T.astype(BF16), dmb)
            d_pooled.append(_dot_nt(dmb, wp_ref[g]))
        dpooled_ref[...] = jnp.concatenate(d_pooled, axis=1)

        @pl.when(i == nt - 1)
        def _():
            gwo_ref[...] = dwo_ref[...].astype(BF16)
            dvec_ref[3:4, :] = jnp.concatenate([dpvec_ref[0:1, :], dpvec_ref[1:2, :]], axis=1)

    tok = lambda width: pl.BlockSpec((tile, width), lambda i: (i, 0))
    whole = lambda a: pl.BlockSpec(a.shape, lambda i: (0,) * a.ndim)
    halo = pl.BlockSpec((HALO, D_POOL), lambda i: (jnp.maximum(i * halo_blocks - 1, 0), 0))
    half = jax.ShapeDtypeStruct((seq, D_ATT), F32)
    outs = [jax.ShapeDtypeStruct((seq, D), F32), jax.ShapeDtypeStruct((N_HEADS, LANE, seq), BF16), half, half, half,
            jax.ShapeDtypeStruct(w_out.shape, BF16), jax.ShapeDtypeStruct(w_pool.shape, F32),
            jax.ShapeDtypeStruct(vecs.shape, F32)]
    return pl.pallas_call(
        body, name="middle", grid=(nt,),
        in_specs=[tok(D), tok(D), tok(D_ATT), tok(D_ATT), tok(D_POOL), tok(D_POOL), halo,
                  whole(vecs), whole(pool_vecs), whole(w_out), whole(w_pool)],
        out_specs=[tok(D), pl.BlockSpec((N_HEADS, LANE, tile), lambda i: (0, 0, i)),
                   tok(D_ATT), tok(D_POOL), tok(D_POOL),
                   whole(w_out), whole(w_pool), whole(vecs)],
        out_shape=outs,
        scratch_shapes=[pltpu.VMEM(w_out.shape, F32), pltpu.VMEM(pool_vecs.shape, F32)],
        compiler_params=_params(("arbitrary",)),
    )(x, tgt, att, g_att, g_pool, p, p, vecs, pool_vecs, w_out, w_pool)


def _attention_backward(q2t, kp, vp, do2t, gw_out, vecs, pool, tile):
    seq = kp.shape[1]
    nb = seq // tile
    last = N_HEADS // 2 - 1

    def body(qt_ref, k_ref, v_ref, dot_ref, gwo_hbm, vecs_hbm, pool_hbm,
             dq_ref, dk_ref, dv_ref, dcum_ref, g_out_ref, vecs_sum_ref, pool_sum_ref,
             dq_acc, dk_acc, dv_acc, gwo_ref, vecs_ref, pool_ref,
             r1, s2, r2, v_sib, v_chip, v_recv, p_sib, p_chip, p_recv, send_sems, recv_sems):
        hp = pl.program_id(0)
        start, middle, fold, finish = _reduce_stages(
            [gwo_ref], [g_out_ref], [r1], [s2], [r2],
            [(vecs_ref, vecs_sum_ref, v_sib, v_chip, v_recv), (pool_ref, pool_sum_ref, p_sib, p_chip, p_recv)],
            send_sems, recv_sems)

        @pl.when(hp == 0)
        def _():
            pltpu.sync_copy(gwo_hbm, gwo_ref)
            pltpu.sync_copy(vecs_hbm, vecs_ref)
            pltpu.sync_copy(pool_hbm, pool_ref)
            start()

        pl.when(hp == 1)(middle)
        pl.when(hp == 2)(fold)

        row = lax.broadcasted_iota(jnp.int32, (tile, tile), 0)
        col = lax.broadcasted_iota(jnp.int32, (tile, tile), 1)
        dq_acc[...] = jnp.zeros_like(dq_acc)

        def kv_block(kb, _):
            krows = pl.ds(pl.multiple_of(kb * tile, tile), tile)
            k = [k_ref[hh, krows, :] for hh in range(2)]
            v = [v_ref[hh, krows, :] for hh in range(2)]
            k_t = [k[hh].T for hh in range(2)]

            def q_block(qb, masked):
                qcols = pl.ds(pl.multiple_of(qb * tile, tile), tile)
                for hh in range(2):
                    q_t = qt_ref[hh, :, qcols]
                    do_t = dot_ref[hh, :, qcols]
                    s_t = _dot(k[hh], q_t)
                    if masked:
                        s_t = jnp.where(row <= col, s_t, -1e30)
                    p_t = jnp.exp(s_t)
                    ds_t = (p_t * _dot(v[hh], do_t)).astype(BF16)
                    dv_new = _dot_nt(do_t, p_t.astype(BF16))
                    dk_new = _dot_nt(q_t, ds_t)
                    if masked:
                        dv_acc[hh], dk_acc[hh] = dv_new, dk_new
                    else:
                        dv_acc[hh] += dv_new
                        dk_acc[hh] += dk_new
                    dq_acc[hh, :, qcols] += _dot(k_t[hh], ds_t)

            q_block(kb, True)

            def two_later_blocks(j, _):
                q_block(kb + 1 + 2 * j, False)
                q_block(kb + 2 + 2 * j, False)
                return 0

            later = nb - 1 - kb
            lax.fori_loop(0, later // 2, two_later_blocks, 0)
            pl.when(later % 2 == 1)(lambda: q_block(nb - 1, False))
            for hh in range(2):
                dk = dk_acc[hh]
                dk_ref[hh, :, krows] = dk.astype(BF16)
                dv_ref[hh, :, krows] = dv_acc[hh].astype(BF16)
                dcum_ref[hh, :, krows] = -dk[L_CK:L_CK + 1, :]
            return 0

        lax.fori_loop(0, nb, kv_block, 0)
        for hh in range(2):
            dq = dq_acc[hh]
            dcum_ref[hh] += dq[L_CQ:L_CQ + 1, :]
            dq_ref[hh] = (dq * Q_SCALE).astype(BF16)
        pl.when(hp == last)(finish)

    pair = pl.BlockSpec((2, seq, LANE), lambda hp: (hp, 0, 0))
    pair_t = pl.BlockSpec((2, LANE, seq), lambda hp: (hp, 0, 0))
    whole = lambda shape: pl.BlockSpec(shape, lambda hp: (0,) * len(shape))
    grad = jax.ShapeDtypeStruct((N_HEADS, LANE, seq), BF16)
    return pl.pallas_call(
        body, name="attention_backward", grid=(N_HEADS // 2,),
        in_specs=[pair_t, pair, pair, pair_t, ANY, ANY, ANY],
        out_specs=[pair_t, pair_t, pair_t, pl.BlockSpec((2, 1, seq), lambda hp: (hp, 0, 0)),
                   whole(gw_out.shape[1:]), whole(vecs.shape), whole(pool.shape)],
        out_shape=[grad, grad, grad, jax.ShapeDtypeStruct((N_HEADS, 1, seq), F32),
                   jax.ShapeDtypeStruct(gw_out.shape[1:], F32), jax.ShapeDtypeStruct(vecs.shape, F32),
                   jax.ShapeDtypeStruct(pool.shape, F32)],
        scratch_shapes=[pltpu.VMEM((2, LANE, seq), F32), pltpu.VMEM((2, LANE, tile), F32),
                        pltpu.VMEM((2, LANE, tile), F32), pltpu.VMEM(gw_out.shape, BF16),
                        pltpu.VMEM(vecs.shape, F32), pltpu.VMEM(pool.shape, F32)]
        + _reduce_scratch(gw_out, [vecs, pool]),
        compiler_params=_params(("arbitrary",)),
    )(q2t, kp, vp, do2t, gw_out, vecs, pool)


def _inproj_backward(dqp, dkp, dvp, d_cum, f, d_pooled, d_ga, d_gp, x, dxa, u, mod, w_main, w_f, tile):
    seq = x.shape[0]
    nt = seq // tile
    halo_blocks = tile // HALO

    def body(dq_ref, dk_ref, dv_ref, dcum_ref, f_ref, dpo_ref, dph_ref, dga_ref, dgp_ref, x_ref, dxa_ref, u_ref,
             mod_ref, w_ref, wf_ref,
             dx_ref, dproj_ref, dwf_ref, db_ref, dbf_ref, dmod_ref, carry_ref):
        step = pl.program_id(0)
        i = nt - 1 - step

        @pl.when(step == 0)
        def _():
            carry_ref[...] = jnp.zeros_like(carry_ref)
            dwf_ref[...] = jnp.zeros_like(dwf_ref)
            db_ref[...] = jnp.zeros_like(db_ref)
            dbf_ref[...] = jnp.zeros_like(dbf_ref)
            dmod_ref[...] = jnp.zeros_like(dmod_ref)

        ones = jnp.ones((8, tile), BF16)

        def emit(chunk, val):
            cols = pl.ds(chunk * COL_CHUNK, COL_CHUNK)
            db_ref[0:1, cols] += jnp.sum(val, axis=0, keepdims=True)
            vb = val.astype(BF16)
            dproj_ref[:, pl.ds((chunk - 3) * COL_CHUNK, COL_CHUNK)] = vb
            return _dot(vb, w_ref[cols, :])

        d_u = jnp.zeros((tile, D), F32)
        for chunk, ref in enumerate((dq_ref, dk_ref, dv_ref)):
            cols = pl.ds(chunk * COL_CHUNK, COL_CHUNK)
            val_t = ref[:, 0:HEAD_DIM, :].reshape(COL_CHUNK, tile)
            db_ref[:, cols] += _dot_nt(ones, val_t)
            d_u += _dot_tn(val_t, w_ref[cols, :])

        d_pooled = dpo_ref[...]
        d_halo = dph_ref[...] * jnp.where(i < nt - 1, 1.0, 0.0)
        pos = i * tile + lax.broadcasted_iota(jnp.int32, (tile, 1), 0) + 1
        d_p = []
        for g, window in enumerate(POOL_WINDOWS):
            cols = slice(g * GROUP_DIM, (g + 1) * GROUP_DIM)
            scaled = d_pooled[:, cols] / jnp.minimum(pos, window).astype(F32)
            d_p.append(_window_sum(scaled, d_halo[:, cols] * (1.0 / window), window, True) - d_pooled[:, cols])
        d_u += emit(3, jnp.concatenate(d_p, axis=1))
        d_u += emit(4, dga_ref[...])
        d_u += emit(5, dgp_ref[...])

        row = lax.broadcasted_iota(jnp.int32, (tile, tile), 0)
        col = lax.broadcasted_iota(jnp.int32, (tile, tile), 1)
        later = (row >= col).astype(BF16)
        d_logf = sum(_dot(part, later) for part in _split3(dcum_ref[:, 0, :])) + carry_ref[:, 0:1]
        carry_ref[:, 0:1] = d_logf[:, 0:1]
        d_f = d_logf * _sigmoid(-f_ref[...].T[0:N_HEADS, :])
        d_f = jnp.concatenate([d_f, jnp.zeros((LANE - N_HEADS, tile), F32)], axis=0)
        dbf_ref[...] += sum(_dot_nt(ones, part) for part in _split3(d_f))
        d_fb = d_f.astype(BF16)
        d_u += _dot_tn(d_fb, wf_ref[...])
        dwf_ref[...] += _dot(d_fb, u_ref[...])

        x = x_ref[...]
        dx_ref[...] = dxa_ref[...] + d_u * mod_ref[0:1, :]
        dmod_ref[0:1, :] += jnp.sum(d_u * x, axis=0, keepdims=True)
        dmod_ref[1:2, :] += jnp.sum(d_u, axis=0, keepdims=True)

    rev = lambda step: nt - 1 - step
    tok = lambda width: pl.BlockSpec((tile, width), lambda s: (rev(s), 0))
    head_block = pl.BlockSpec((N_HEADS, LANE, tile), lambda s: (0, 0, rev(s)))
    whole = lambda a: pl.BlockSpec(a.shape, lambda s: (0,) * a.ndim)
    halo = pl.BlockSpec((HALO, D_POOL), lambda s: (jnp.minimum((rev(s) + 1) * halo_blocks, seq // HALO - 1), 0))
    small = lambda width: jax.ShapeDtypeStruct((8, width), F32)
    n_rest = N_MAIN - OFF_P
    return pl.pallas_call(
        body, name="inproj_backward", grid=(nt,),
        in_specs=[head_block, head_block, head_block, pl.BlockSpec((N_HEADS, 1, tile), lambda s: (0, 0, rev(s))),
                  tok(LANE), tok(D_POOL), halo, tok(D_ATT), tok(D_POOL),
                  tok(D), tok(D), tok(D),
                  whole(mod), whole(w_main), whole(w_f)],
        out_specs=[tok(D), tok(n_rest), pl.BlockSpec((LANE, D), lambda s: (0, 0)),
                   pl.BlockSpec((8, N_MAIN), lambda s: (0, 0)), pl.BlockSpec((8, LANE), lambda s: (0, 0)),
                   pl.BlockSpec((8, D), lambda s: (0, 0))],
        out_shape=[jax.ShapeDtypeStruct((seq, D), F32), jax.ShapeDtypeStruct((seq, n_rest), BF16),
                   jax.ShapeDtypeStruct((LANE, D), F32), small(N_MAIN), small(LANE), small(D)],
        scratch_shapes=[pltpu.VMEM((8, LANE), F32)],
        compiler_params=_params(("arbitrary",)),
    )(dqp, dkp, dvp, d_cum, f, d_pooled, d_pooled, d_ga, d_gp, x, dxa, u, mod, w_main, w_f)


def _weight_grads(dq_t, dk_t, dv_t, dw_f, dproj, u, k_tile):
    seq = u.shape[0]
    nk = seq // k_tile
    rows = N_HEADS * HEAD_DIM

    def body(dq_ref, dk_ref, dv_ref, dwf_ref, dp_ref, u_ref, out_ref, acc_ref):
        k = pl.program_id(0)

        @pl.when(k == 0)
        def _():
            acc_ref[...] = jnp.zeros_like(acc_ref)

        tokens = u_ref[...]
        for j, ref in enumerate((dq_ref, dk_ref, dv_ref)):
            acc_ref[pl.ds(j * rows, rows), :] += _dot(ref[...].reshape(rows, k_tile), tokens)
        for j in range(dproj.shape[1] // COL_CHUNK):
            cols = pl.ds(j * COL_CHUNK, COL_CHUNK)
            acc_ref[pl.ds(F_HI + j * COL_CHUNK, COL_CHUNK), :] += _dot_tn(dp_ref[:, cols], tokens)

        @pl.when(k == nk - 1)
        def _():
            acc_ref[F_LO:F_HI, :] = dwf_ref[0:N_HEADS, :]
            out_ref[...] = acc_ref[...].astype(BF16)

    heads = pl.BlockSpec((N_HEADS, HEAD_DIM, k_tile), lambda k: (0, 0, k))
    return pl.pallas_call(
        body, name="weight_grads", grid=(nk,),
        in_specs=[heads, heads, heads, pl.BlockSpec(dw_f.shape, lambda k: (0, 0)),
                  pl.BlockSpec((k_tile, dproj.shape[1]), lambda k: (k, 0)), pl.BlockSpec((k_tile, D), lambda k: (k, 0))],
        out_specs=pl.BlockSpec((D_IN, D), lambda k: (0, 0)),
        out_shape=jax.ShapeDtypeStruct((D_IN, D), BF16),
        scratch_shapes=[pltpu.VMEM((D_IN, D), F32)],
        compiler_params=_params(("arbitrary",)),
    )(dq_t, dk_t, dv_t, dw_f, dproj, u)


def _adamw(w, g, m, v):
    m = ADAM_B1 * m + (1.0 - ADAM_B1) * g
    v = ADAM_B2 * v + (1.0 - ADAM_B2) * (g * g)
    m_hat = m / (1.0 - ADAM_B1 ** ADAM_STEP)
    v_hat = v / (1.0 - ADAM_B2 ** ADAM_STEP)
    delta = -ADAM_LR * (m_hat / (jnp.sqrt(v_hat) + ADAM_EPS) + ADAM_WD * w)
    return delta, m, v


def _adamw_call(g, w, m, v, col_tile, name):
    nr = w.shape[1] // col_tile

    def body(gi_ref, w_ref, m_ref, v_ref, g_ref, d_ref, nm_ref, nv_ref):
        g = gi_ref[...]
        g_ref[...] = g
        d_ref[...], nm_ref[...], nv_ref[...] = _adamw(w_ref[...], g, m_ref[...], v_ref[...])

    blk = pl.BlockSpec((w.shape[0], col_tile), lambda r: (0, r))
    shape = jax.ShapeDtypeStruct(w.shape, F32)
    return pl.pallas_call(
        body, name=name, grid=(nr,),
        in_specs=[blk, blk, blk, blk], out_specs=[blk, blk, blk, blk],
        out_shape=[shape, shape, shape, shape],
        compiler_params=_params(("arbitrary",)),
    )(g, w, m, v)


def _ada_adamw(sc_t, d_ada, w, m, v):
    def body(sc_ref, d_ref, w_ref, m_ref, v_ref, g_ref, dl_ref, nm_ref, nv_ref):
        g = sc_ref[:, 0:1] * d_ref[0:1, :]
        for b in range(1, N_DEV):
            g = g + sc_ref[:, b:b + 1] * d_ref[b:b + 1, :]
        g_ref[...] = g
        dl_ref[...], nm_ref[...], nv_ref[...] = _adamw(w_ref[...], g, m_ref[...], v_ref[...])

    row_tile = 256
    blk = pl.BlockSpec((row_tile, w.shape[1]), lambda r: (r, 0))
    shape = jax.ShapeDtypeStruct(w.shape, F32)
    return pl.pallas_call(
        body, name="ada_adamw", grid=(w.shape[0] // row_tile,),
        in_specs=[pl.BlockSpec((row_tile, N_DEV), lambda r: (r, 0)), pl.BlockSpec(d_ada.shape, lambda r: (0, 0)),
                  blk, blk, blk],
        out_specs=[blk, blk, blk, blk], out_shape=[shape, shape, shape, shape],
        compiler_params=_params(("arbitrary",)),
    )(sc_t, d_ada, w, m, v)


F_LO, F_HI = 3 * D_ATT, 3 * D_ATT + N_HEADS


def _split_forget(a, axis):
    idx = lambda lo, hi: tuple(slice(lo, hi) if d == axis else slice(None) for d in range(a.ndim))
    pad = [(0, LANE - N_HEADS) if d == axis else (0, 0) for d in range(a.ndim)]
    return jnp.concatenate([a[idx(0, F_LO)], a[idx(F_HI, D_IN)]], axis=axis), jnp.pad(a[idx(F_LO, F_HI)], pad)


def _join_forget(main, f, axis):
    idx = lambda lo, hi: tuple(slice(lo, hi) if d == axis else slice(None) for d in range(main.ndim))
    return jnp.concatenate([main[idx(0, F_LO)], f[idx(0, N_HEADS)], main[idx(F_LO, N_MAIN)]], axis=axis)


def _adamw_small(grad_rows, row_params, whole_params, summed_params):
    n_row, n_whole, n_sum = len(row_params), len(whole_params), len(summed_params)
    n = n_row + n_whole + n_sum

    def body(g_ref, *refs):
        n_in = 3 * n_row + 4 * (n_whole + n_sum)
        ins, outs = list(refs[:n_in]), refs[n_in:]
        for i in range(n):
            if i < n_row:
                row, lo, hi = row_params[i][:3]
                g = g_ref[row:row + 1, lo:hi]
            elif i < n_row + n_whole:
                g = ins.pop(0)[...]
            else:
                parts = ins.pop(0)
                g = parts[0]
                for k in range(1, N_DEV):
                    g = g + parts[k]
            w, m, v = (ins.pop(0)[...] for _ in range(3))
            outs[4 * i][...] = g
            outs[4 * i + 1][...], outs[4 * i + 2][...], outs[4 * i + 3][...] = _adamw(w, g, m, v)

    shapes = [p[3] for p in row_params] + [p[1] for p in whole_params] + [p[1] for p in summed_params]
    operands = [a for p in row_params for a in p[3:]] + [a for p in whole_params + summed_params for a in p]
    flat = pl.pallas_call(
        body, name="adamw_small",
        out_shape=[jax.ShapeDtypeStruct(w.shape, F32) for w in shapes for _ in range(4)],
        compiler_params=_params(),
    )(grad_rows, *operands)
    return [flat[4 * i:4 * i + 4] for i in range(n)]


def kernel(x, c, w_ada, b_ada, w_in, b_in, w_pool_mix, b_pool_mix, pool_scale, w_out, b_out, ln_g, ln_b, loss_target, m_w_ada, m_b_ada, m_w_in, m_b_in, m_w_pool_mix, m_b_pool_mix, m_pool_scale, m_w_out, m_b_out, m_ln_g, m_ln_b, v_w_ada, v_b_ada, v_w_in, v_b_in, v_w_pool_mix, v_b_pool_mix, v_pool_scale, v_w_out, v_b_out, v_ln_g, v_ln_b):
    seq = x.shape[1]
    tile = min(256, seq)
    attn_tile = min(512, max(128, seq // 4))
    me = _dev_index(*_mesh_pos())
    x2, tgt = x[0], loss_target[0]

    rows_in = D_IN // N_DEV
    w_in_g, sc_all, ada_mine = _gather_and_ada(c, w_in[0].T.astype(BF16), w_ada[0])
    ada = ada_mine.reshape(1, D_ADA) + b_ada
    shift, scale, gate = ada[:, 0:D], ada[:, D:2 * D], ada[:, 2 * D:]
    mod = jnp.concatenate([1.0 + scale, shift, jnp.zeros((6, D), F32)], axis=0)

    w_main, w_f = _split_forget(w_in_g.reshape(D_IN, D), 0)
    b_main, b_f = _split_forget(b_in, 1)

    qp, kp, vp, f, p, g_att, g_pool, u = _inproj_forward(x2, mod, w_main, w_f, b_main, b_f, tile)
    att, q2t, w_out_g = _attention_forward(qp, kp, vp, w_out[0].astype(BF16), attn_tile)

    vecs = jnp.concatenate([gate, b_out, ln_g, ln_b, jnp.zeros((4, D), F32)], axis=0)
    pool_vecs = jnp.concatenate([b_pool_mix.reshape(1, D_POOL), pool_scale, jnp.zeros((6, D_POOL), F32)], axis=0)
    dxa, do2, d_ga, d_gp, d_pooled, gw_out, dw_pool, dvec = _middle(
        x2, tgt, att, g_att, g_pool, p, vecs, pool_vecs, w_out_g.reshape(D, D), w_pool_mix[0].astype(BF16), tile)

    pool_rows = w_pool_mix.shape[1] * GROUP_DIM
    dqp, dkp, dvp, d_cum, g_out, dvec_sum, dw_pool_sum = _attention_backward(
        q2t, kp, vp, do2, gw_out.reshape(N_DEV, D // N_DEV, D), dvec, dw_pool.reshape(pool_rows, GROUP_DIM), attn_tile)
    dx, dproj, dw_f, db_main, db_f, dmod = _inproj_backward(
        dqp, dkp, dvp, d_cum, f, d_pooled, d_ga, d_gp, x2, dxa, u, mod, w_main, w_f, tile)
    gw_in = _weight_grads(dqp, dkp, dvp, dw_f, dproj, u, min(512, seq)).reshape(N_DEV, rows_in, D)
    d_ada = jnp.concatenate([dmod[1:2], dmod[0:1], dvec[5:6]], axis=1)
    g_in_rows, g_b_in, d_ada_all = _reduce_grads(gw_in, _join_forget(db_main[0:1], db_f[0:1], 1), d_ada)

    outs_in = _adamw_call(g_in_rows, w_in[0].T, m_w_in[0].T, v_w_in[0].T, D // 4, "adamw_w_in")
    g_w_in, d_w_in, nm_w_in, nv_w_in = (a.T for a in outs_in)
    flat_pool = lambda a: a.reshape(1, D_POOL)
    pool_2d = lambda a: a.reshape(pool_rows, GROUP_DIM)
    rows = _adamw_small(
        dvec_sum,
        [(0, 0, D, b_out, m_b_out, v_b_out), (1, 0, D, ln_g, m_ln_g, v_ln_g), (2, 0, D, ln_b, m_ln_b, v_ln_b),
         (3, 0, D_POOL, flat_pool(b_pool_mix), flat_pool(m_b_pool_mix), flat_pool(v_b_pool_mix)),
         (3, D_POOL, 2 * D_POOL, pool_scale, m_pool_scale, v_pool_scale)],
        [(g_out, w_out[0], m_w_out[0], v_w_out[0]),
         (dw_pool_sum, pool_2d(w_pool_mix), pool_2d(m_w_pool_mix), pool_2d(v_w_pool_mix)),
         (g_b_in, b_in, m_b_in, v_b_in)],
        [(d_ada_all, b_ada, m_b_ada, v_b_ada)])
    small = {"b_out": rows[0], "ln_g": rows[1], "ln_b": rows[2],
             "b_pool": [a.reshape(b_pool_mix.shape) for a in rows[3]], "pool_scale": rows[4],
             "w_pool": [a.reshape(w_pool_mix.shape) for a in rows[6]], "b_in": rows[7]}
    g_s, d_s, nm_s, nv_s = ({k: r[j] for k, r in small.items()} for j in range(4))
    g_w_out, d_w_out, nm_w_out, nv_w_out = rows[5]
    g_b_ada, d_b_ada, nm_b_ada, nv_b_ada = rows[8]
    loss = dvec_sum[4, 0]

    d_ada_local = lax.dynamic_slice_in_dim(d_ada_all.reshape(N_DEV, D_ADA), me * (D_ADA // N_DEV), D_ADA // N_DEV, axis=1)
    g_w_ada, d_w_ada, nm_w_ada, nv_w_ada = _ada_adamw(sc_all.T, d_ada_local, w_ada[0], m_w_ada[0], v_w_ada[0])

    def ordered(w_ada_, b_ada_, w_in_, w_out_, s):
        return (w_ada_[None], b_ada_, w_in_[None], s["b_in"], s["w_pool"], s["b_pool"], s["pool_scale"],
                w_out_[None], s["b_out"], s["ln_g"], s["ln_b"])

    return (loss, dx[None],
            *ordered(g_w_ada, g_b_ada, g_w_in, g_w_out, g_s),
            *ordered(d_w_ada, d_b_ada, d_w_in, d_w_out, d_s),
            *ordered(nm_w_ada, nm_b_ada, nm_w_in, nm_w_out, nm_s),
            *ordered(nv_w_ada, nv_b_ada, nv_w_in, nv_w_out, nv_s))
```

```python
import jax
import jax.numpy as jnp
from jax import lax
from jax.experimental import pallas as pl
from jax.experimental.pallas import tpu as pltpu

F32 = jnp.float32
BF16 = jnp.bfloat16

N_DEV = 8
D = 1024
N_HEADS = 8
HEAD_DIM = 64
D_ATT = 512
D_POOL = 512
POOL_WINDOWS = (2, 4, 8, 16)
GROUP_DIM = 128
HALO = 16
LANE = 128
D_IN = 3080
D_ADA = 3072
N_MAIN = 3072
OFF_P = 1536
COL_CHUNK = 512
Q_SCALE = 0.125
LN_EPS = 1e-5
ALPHA = 2.0 ** 0.25
L_CQ, L_CK, L_LSE = 64, 67, 70

ADAM_LR, ADAM_B1, ADAM_B2, ADAM_EPS, ADAM_WD, ADAM_STEP = 0.001, 0.9, 0.999, 1e-08, 0.01, 10
VMEM_LIMIT = 56 * 1024 * 1024

MESH = pl.DeviceIdType.MESH
ANY = pl.BlockSpec(memory_space=pl.ANY)


def _params(sem=None, vmem=VMEM_LIMIT):
    return pltpu.CompilerParams(dimension_semantics=sem, vmem_limit_bytes=vmem)


def _split3(a):
    hi = a.astype(BF16)
    r = a - hi.astype(F32)
    mid = r.astype(BF16)
    lo = (r - mid.astype(F32)).astype(BF16)
    return hi, mid, lo


def _dot(a, b):
    return jnp.dot(a, b, preferred_element_type=F32)


def _dot_nt(a, b):
    return lax.dot_general(a, b, (((1,), (1,)), ((), ())), preferred_element_type=F32)


def _dot_tn(a, b):
    return lax.dot_general(a, b, (((0,), (0,)), ((), ())), preferred_element_type=F32)


def _dot3(m01, a):
    hi, mid, lo = _split3(a)
    return _dot(m01, hi) + _dot(m01, mid) + _dot(m01, lo)


def _sigmoid(z):
    return 1.0 / (1.0 + jnp.exp(-z))


def _lanes(shape):
    return lax.broadcasted_iota(jnp.int32, shape, len(shape) - 1)


def _place3(lane, base, parts, other):
    out = other
    for j in range(3):
        out = jnp.where(lane == base + j, parts[j], out)
    return out


def _mesh_pos():
    return lax.axis_index("x"), lax.axis_index("y"), lax.axis_index("c")


def _dev_index(px, py, pc):
    return 4 * px + 2 * py + pc


N_GATHER_SEMS = 9


def _gather_stages(src_ref, out_ref, send_sems, recv_sems, local_sem):
    x, y, c = _mesh_pos()
    me, sibling = (x, y, c), (x, y, 1 - c)
    nbr_x, nbr_y, diag = (1 - x, y), (x, 1 - y), (1 - x, 1 - y)
    half = out_ref.shape[-1] // 2
    left, right = pl.ds(0, half), pl.ds(half, half)

    def copy(k, block, to, cols=None, src=None):
        slot = out_ref.at[_dev_index(*block)]
        if cols is not None:
            slot = slot.at[:, cols]
        return pltpu.make_async_remote_copy(
            src_ref=slot if src is None else src, dst_ref=slot, send_sem=send_sems.at[k], recv_sem=recv_sems.at[k],
            device_id=to, device_id_type=MESH)

    mine = pltpu.make_async_copy(src_ref, out_ref.at[_dev_index(*me)], local_sem)
    first = [copy(0, me, sibling, src=src_ref), copy(1, me, (*nbr_x, c), src=src_ref), copy(2, me, (*nbr_y, c), src=src_ref)]
    relay = [(1, nbr_x, None, nbr_x), (2, nbr_y, None, nbr_y), (3, diag, left, nbr_y), (4, diag, right, nbr_x)]
    onward = [copy(3, (*nbr_x, c), (*nbr_y, c), cols=left), copy(4, (*nbr_y, c), (*nbr_x, c), cols=right)]
    passed = [copy(4 + k, (*block, c), sibling, cols=cols) for k, block, cols, _ in relay]

    def start():
        mine.start()
        for cp in first:
            cp.start()

    def relay_stage(first_item):
        def run():
            for j in (first_item, first_item + 1):
                k, block, cols, frm = relay[j]
                copy(k, (*block, c), (*frm, c), cols=cols).wait_recv()
                if j < 2:
                    onward[j].start()
                passed[j].start()
        return run

    def finish():
        copy(0, sibling, me).wait_recv()
        for k, block, cols, _ in relay:
            copy(4 + k, (*block, 1 - c), me, cols=cols).wait_recv()
        for cp in first + onward + passed:
            cp.wait_send()
        mine.wait()

    return start, relay_stage(0), relay_stage(2), finish


N_REDUCE_SEMS = 10
N_SMALL_SEMS = 4
N_ROWS_SEMS = 7


def _reduce_stages(ins, gs, r1, s2, r2, smalls, send_sems, recv_sems, rows=None, own=None):
    n = len(ins)
    x, y, c = _mesh_pos()
    me = _dev_index(x, y, c)
    sibling = (x, y, 1 - c)
    chips = [(x, y), (1 - x, y), (x, 1 - y), (1 - x, 1 - y)]
    peers = []
    for p in range(1, N_DEV):
        px, py, pc = (p >> 2) & 1, (p >> 1) & 1, p & 1
        peers.append((1 - x if px else x, 1 - y if py else y, 1 - c if pc else c))
    base_small = N_REDUCE_SEMS * n

    def remote(src, dst, k, to):
        return pltpu.make_async_remote_copy(src_ref=src, dst_ref=dst, send_sem=send_sems.at[k],
                                            recv_sem=recv_sems.at[k], device_id=to, device_id_type=MESH)

    def level1(a, q):
        return remote(ins[a].at[_dev_index(*chips[q], 1 - c)], r1[a].at[q], N_REDUCE_SEMS * a + q, sibling)

    def level2(a, k):
        half = ins[a].shape[-1] // 2
        left, right = pl.ds(0, half), pl.ds(half, half)
        nbr_x, nbr_y = (*chips[1], c), (*chips[2], c)
        src_slot, dst_slot, cols, to = [(0, 0, left, nbr_x), (1, 1, right, nbr_y), (2, 2, left, nbr_x),
                                        (2, 2, right, nbr_y), (0, 0, right, nbr_x), (1, 1, left, nbr_y)][k]
        return remote(s2[a].at[src_slot, :, cols], r2[a].at[dst_slot, :, cols], N_REDUCE_SEMS * a + 4 + k, to)

    to_sibling = [remote(sm[0], sm[2], base_small + 4 * i, sibling) for i, sm in enumerate(smalls)]
    to_chips = [[remote(sm[3], sm[4].at[j], base_small + 4 * i + 1 + j, (*chips[j + 1], c)) for j in range(3)]
                for i, sm in enumerate(smalls)]
    if rows is not None:
        rows_ref, land_ref, all_ref = rows
        base_rows = base_small + 4 * len(smalls)
        row_sends = [remote(rows_ref, land_ref.at[me], base_rows + k, to) for k, to in enumerate(peers)]

    def mine(a, q):
        buf, sems = own[a]
        return pltpu.make_async_copy(ins[a].at[_dev_index(*chips[q], c)], buf.at[q], sems.at[q])

    def start():
        for a in range(n):
            for q in range(4):
                level1(a, q).start()
            if own is not None:
                for q in (1, 2, 3, 0):
                    mine(a, q).start()
        for cp in to_sibling:
            cp.start()
        if rows is not None:
            for cp in row_sends:
                cp.start()
            land_ref[me] = rows_ref[...]

    def middle():
        for a in range(n):
            for q in (1, 2, 3, 0):
                level1(a, q).wait_recv()
                if own is None:
                    kept = ins[a][_dev_index(*chips[q], c)]
                else:
                    mine(a, q).wait()
                    kept = own[a][0][q]
                pair = kept.astype(F32) + r1[a][q].astype(F32)
                if q == 0:
                    gs[a][...] = pair
                else:
                    s2[a][q - 1] = pair.astype(BF16)
                    for k in ((0,), (1,), (2, 3))[q - 1]:
                        level2(a, k).start()
        for i, (small_ref, _, sm_sib, sm_chip, _) in enumerate(smalls):
            to_sibling[i].wait_recv()
            sm_chip[...] = small_ref[...] + sm_sib[...]
            for cp in to_chips[i]:
                cp.start()

    def fold():
        for a in range(n):
            half = ins[a].shape[-1] // 2
            level2(a, 3).wait_recv()
            s2[a][0, :, half:] = (s2[a][0, :, half:].astype(F32) + r2[a][2, :, half:].astype(F32)).astype(BF16)
            level2(a, 4).start()
            level2(a, 2).wait_recv()
            s2[a][1, :, :half] = (s2[a][1, :, :half].astype(F32) + r2[a][2, :, :half].astype(F32)).astype(BF16)
            level2(a, 5).start()

    def finish():
        for a in range(n):
            for k in (0, 1, 4, 5):
                level2(a, k).wait_recv()
            gs[a][...] = gs[a][...] + r2[a][0].astype(F32) + r2[a][1].astype(F32)
            for q in range(4):
                level1(a, q).wait_send()
            for k in range(6):
                level2(a, k).wait_send()
        for i, (_, total_ref, _, sm_chip, sm_recv) in enumerate(smalls):
            for cp in to_chips[i]:
                cp.wait_recv()
            total = None
            for ax in range(2):
                for ay in range(2):
                    dx, dy = x != ax, y != ay
                    term = jnp.where(dx, jnp.where(dy, sm_recv[2], sm_recv[0]), jnp.where(dy, sm_recv[1], sm_chip[...]))
                    total = term if total is None else total + term
            total_ref[...] = total
            for cp in [to_sibling[i]] + to_chips[i]:
                cp.wait_send()
        if rows is not None:
            for k, frm in enumerate(peers):
                remote(rows_ref, land_ref.at[_dev_index(*frm)], base_rows + k, frm).wait_recv()
            all_ref[...] = land_ref[...]
            for cp in row_sends:
                cp.wait_send()

    return start, middle, fold, finish


def _reduce_scratch(shard, smalls, rows=None):
    out = [pltpu.VMEM((lead,) + shard.shape[1:], BF16) for lead in (4, 3, 3)]
    for small in smalls:
        out += [pltpu.VMEM(small.shape, F32), pltpu.VMEM(small.shape, F32), pltpu.VMEM((3,) + small.shape, F32)]
    n_sems = N_REDUCE_SEMS + N_SMALL_SEMS * len(smalls)
    if rows is not None:
        out.append(pltpu.VMEM((N_DEV,) + rows.shape, F32))
        n_sems += N_ROWS_SEMS
    return out + [pltpu.SemaphoreType.DMA((n_sems,))] * 2


def _reduce_grads(gw_in, small, rows):
    def body(in_ref, small_ref, rows_ref, g_ref, total_ref, rows_all_ref,
             r1, s2, r2, sm_sib, sm_chip, sm_recv, rows_land, send_sems, recv_sems, kept, kept_sems):
        stages = _reduce_stages(
            [in_ref], [g_ref], [r1], [s2], [r2], [(small_ref, total_ref, sm_sib, sm_chip, sm_recv)],
            send_sems, recv_sems, rows=(rows_ref, rows_land, rows_all_ref), own=[(kept, kept_sems)])
        for stage in stages:
            stage()

    vmem = pl.BlockSpec(memory_space=pltpu.VMEM)
    return pl.pallas_call(
        body, name="reduce_grads",
        in_specs=[ANY, vmem, vmem], out_specs=[vmem, vmem, vmem],
        out_shape=[jax.ShapeDtypeStruct(gw_in.shape[1:], F32), jax.ShapeDtypeStruct(small.shape, F32),
                   jax.ShapeDtypeStruct((N_DEV,) + rows.shape, F32)],
        scratch_shapes=_reduce_scratch(gw_in, [small], rows)
        + [pltpu.VMEM((4,) + gw_in.shape[1:], BF16), pltpu.SemaphoreType.DMA((4,))],
        compiler_params=_params(),
    )(gw_in, small, rows)


def _dot3_rhs(a, b):
    a0, a1, a2 = _split3(a)
    b0, b1, b2 = _split3(b)
    return (_dot(a0, b0) + (_dot(a0, b1) + _dot(a1, b0))
            + (_dot(a0, b2) + _dot(a1, b1) + _dot(a2, b0)))


def _gather_and_ada(c, w_in_rows, w_ada):
    cols = w_ada.shape[1]

    def body(c_ref, w_ref, wa_ref, w_all_ref, sc_ref, ada_ref,
             c_land, part, ada_land, send_sems, recv_sems, local_sem, x_send, x_recv):
        x, y, cc = _mesh_pos()
        me = _dev_index(x, y, cc)
        peers = []
        for p in range(1, N_DEV):
            px, py, pc = (p >> 2) & 1, (p >> 1) & 1, p & 1
            peers.append((1 - x if px else x, 1 - y if py else y, 1 - cc if pc else cc))

        def remote(src, dst, k, to):
            return pltpu.make_async_remote_copy(src_ref=src, dst_ref=dst, send_sem=x_send.at[k], recv_sem=x_recv.at[k],
                                                device_id=to, device_id_type=MESH)

        c_sends = [remote(c_ref, c_land.at[me], k, to) for k, to in enumerate(peers)]
        for cp in c_sends:
            cp.start()
        start, relay_near, relay_far, finish = _gather_stages(w_ref, w_all_ref, send_sems, recv_sems, local_sem.at[0])
        start()
        c_land[me] = c_ref[...]
        for k, frm in enumerate(peers):
            remote(c_ref, c_land.at[_dev_index(*frm)], k, frm).wait_recv()
        c_all = jnp.concatenate([c_land[b] for b in range(N_DEV)], axis=0)
        sc = c_all * _sigmoid(c_all)
        sc_ref[...] = sc
        rows = _dot3_rhs(sc, wa_ref[...])
        for b in range(N_DEV):
            part[b] = rows[b:b + 1, :]
        a_sends = [remote(part.at[_dev_index(*to)], ada_land.at[me], 7 + k, to) for k, to in enumerate(peers)]
        for cp in a_sends:
            cp.start()
        ada_land[me] = part[me]
        for k, frm in enumerate(peers):
            remote(part.at[0], ada_land.at[_dev_index(*frm)], 7 + k, frm).wait_recv()
        ada_ref[...] = ada_land[...]

        relay_near()
        relay_far()
        finish()
        for cp in c_sends + a_sends:
            cp.wait_send()

    vmem = pl.BlockSpec(memory_space=pltpu.VMEM)
    return pl.pallas_call(
        body, name="gather_weights",
        in_specs=[vmem, ANY, vmem], out_specs=[ANY, vmem, vmem],
        out_shape=[jax.ShapeDtypeStruct((N_DEV,) + w_in_rows.shape, w_in_rows.dtype),
                   jax.ShapeDtypeStruct((N_DEV, D), F32), jax.ShapeDtypeStruct((N_DEV, 1, cols), F32)],
        scratch_shapes=[pltpu.VMEM((N_DEV, 1, D), F32), pltpu.VMEM((N_DEV, 1, cols), F32), pltpu.VMEM((N_DEV, 1, cols), F32),
                        pltpu.SemaphoreType.DMA((N_GATHER_SEMS,)), pltpu.SemaphoreType.DMA((N_GATHER_SEMS,)),
                        pltpu.SemaphoreType.DMA((1,)),
                        pltpu.SemaphoreType.DMA((14,)), pltpu.SemaphoreType.DMA((14,))],
        compiler_params=_params(),
    )(c, w_in_rows, w_ada)


def _inproj_forward(x, mod, w_main, w_f, b_main, b_f, tile):
    seq = x.shape[0]
    nt = seq // tile

    def body(x_ref, mod_ref, w_ref, wf_ref, b_ref, bf_ref,
             qp_ref, kp_ref, vp_ref, f_ref, p_ref, ga_ref, gp_ref, u_ref, carry_ref):
        i = pl.program_id(0)

        @pl.when(i == 0)
        def _():
            carry_ref[...] = jnp.zeros_like(carry_ref)

        u = x_ref[...] * mod_ref[0:1, :] + mod_ref[1:2, :]
        ub = u.astype(BF16)
        u_ref[...] = ub

        f = _dot_nt(ub, wf_ref[...]) + bf_ref[...]
        f_ref[...] = f
        lane = _lanes((tile, LANE))
        log_f = jnp.where(lane < N_HEADS, jnp.minimum(f, 0.0) - jnp.log(1.0 + jnp.exp(-jnp.abs(f))), 0.0)
        row = lax.broadcasted_iota(jnp.int32, (tile, tile), 0)
        col = lax.broadcasted_iota(jnp.int32, (tile, tile), 1)
        tri = (row >= col).astype(BF16)
        cum = _dot3(tri, log_f) + carry_ref[0:1, :]
        carry_ref[0:1, :] = cum[tile - 1:tile, :]
        cq = [part.astype(F32) for part in _split3(cum)]
        ck = [part.astype(F32) for part in _split3(-cum)]

        def proj(chunk):
            cols = pl.ds(chunk * COL_CHUNK, COL_CHUNK)
            return _dot_nt(ub, w_ref[cols, :]) + b_ref[:, cols]

        def head_tiles(r):
            for pair in range(N_HEADS // 2):
                both = r[:, pair * LANE:(pair + 1) * LANE]
                yield 2 * pair, both
                yield 2 * pair + 1, pltpu.roll(both, HEAD_DIM, 1)

        for h, val in head_tiles(proj(0)):
            extra = jnp.where((lane >= L_CK) & (lane < L_CK + 3), 1.0, 0.0)
            extra = _place3(lane, L_CQ, [part[:, h:h + 1] for part in cq], extra)
            qp_ref[h] = jnp.where(lane < HEAD_DIM, val * Q_SCALE, extra).astype(BF16)
        for h, val in head_tiles(proj(1)):
            ones = ((lane >= L_CQ) & (lane < L_CQ + 3)) | ((lane >= L_LSE) & (lane < L_LSE + 3))
            extra = _place3(lane, L_CK, [part[:, h:h + 1] for part in ck], jnp.where(ones, 1.0, 0.0))
            kp_ref[h] = jnp.where(lane < HEAD_DIM, val, extra).astype(BF16)
        for h, val in head_tiles(proj(2)):
            extra = jnp.where((lane >= HEAD_DIM) & (lane < HEAD_DIM + 3), -1.0, 0.0)
            vp_ref[h] = jnp.where(lane < HEAD_DIM, val, extra).astype(BF16)
        p_ref[...] = proj(3)
        ga_ref[...] = proj(4)
        gp_ref[...] = proj(5)

    head_block = pl.BlockSpec((N_HEADS, tile, LANE), lambda i: (0, i, 0))
    tok = lambda width: pl.BlockSpec((tile, width), lambda i: (i, 0))
    whole = lambda a: pl.BlockSpec(a.shape, lambda i: (0,) * a.ndim)
    padded = jax.ShapeDtypeStruct((N_HEADS, seq, LANE), BF16)
    half = jax.ShapeDtypeStruct((seq, D_ATT), F32)
    return pl.pallas_call(
        body, name="inproj_forward", grid=(nt,),
        in_specs=[tok(D), whole(mod), whole(w_main), whole(w_f), whole(b_main), whole(b_f)],
        out_specs=[head_block, head_block, head_block, tok(LANE), tok(D_POOL), tok(D_ATT), tok(D_POOL),
                   tok(D)],
        out_shape=[padded, padded, padded, jax.ShapeDtypeStruct((seq, LANE), F32), half, half, half,
                   jax.ShapeDtypeStruct((seq, D), BF16)],
        scratch_shapes=[pltpu.VMEM((8, LANE), F32)],
        compiler_params=_params(("arbitrary",)),
    )(x, mod, w_main, w_f, b_main, b_f)


def _attention_forward(qp, kp, vp, w_out, tile):
    seq = qp.shape[1]
    nb = seq // tile
    steps = (N_HEADS // 2) * nb

    def body(q_ref, k_ref, v_ref, wo_ref, att_ref, q2t_ref, wo_all_ref, s_a, s_b, m_ref, acc_ref,
             send_sems, recv_sems, local_sem):
        step = pl.program_id(0) * nb + pl.program_id(1)
        start, relay_near, relay_far, finish = _gather_stages(wo_ref, wo_all_ref, send_sems, recv_sems, local_sem.at[0])
        pl.when(step == 0)(start)
        pl.when(step == steps // 4)(relay_near)
        pl.when(step == (3 * steps) // 4)(relay_far)

        i = pl.program_id(1)
        sub = lax.broadcasted_iota(jnp.int32, (LANE, tile), 0)
        row = lax.broadcasted_iota(jnp.int32, (tile, tile), 0)
        col = lax.broadcasted_iota(jnp.int32, (tile, tile), 1)
        q = [q_ref[0], q_ref[1]]

        def scores(buf, kb):
            rows = pl.ds(pl.multiple_of(kb * tile, tile), tile)
            for hh in range(2):
                buf[hh] = _dot_nt(k_ref[hh, rows, :], q[hh])

        def absorb(buf, kb, masked):
            rows = pl.ds(pl.multiple_of(kb * tile, tile), tile)
            for hh in range(2):
                m = m_ref[hh, 0:1, :]
                s = buf[hh]
                if masked:
                    s = jnp.where(row <= col, s, -1e30)
                m_new = jnp.maximum(m, jnp.max(s, axis=0, keepdims=True))
                p = jnp.exp(s - m_new).astype(BF16)
                acc_ref[hh] = jnp.exp(m - m_new) * acc_ref[hh] + _dot_tn(v_ref[hh, rows, :], p)
                m_ref[hh, 0:1, :] = m_new

        def two_blocks(j, _):
            scores(s_b, 2 * j + 1)
            absorb(s_a, 2 * j, False)
            scores(s_a, 2 * j + 2)
            absorb(s_b, 2 * j + 1, False)
            return 0

        def last_block():
            absorb(s_a, i, True)

        def last_two_blocks():
            scores(s_b, i)
            absorb(s_a, i - 1, False)
            absorb(s_b, i, True)

        scores(s_a, 0)
        m_ref[...] = jnp.full(m_ref.shape, -1e30, F32)
        acc_ref[...] = jnp.zeros_like(acc_ref)
        lax.fori_loop(0, i // 2, two_blocks, 0)
        lax.cond(i % 2 == 0, last_block, last_two_blocks)
        outs = []
        for hh in range(2):
            m, acc = m_ref[hh, 0:1, :], acc_ref[hh]
            l = -acc[HEAD_DIM:HEAD_DIM + 1, :]
            outs.append((acc / l)[:HEAD_DIM, :])
            neg_lse = [part.astype(F32) for part in _split3(-(m + jnp.log(l)))]
            q2t_ref[hh] = _place3(sub, L_LSE, neg_lse, q[hh].astype(F32).T).astype(BF16)
        att_ref[...] = jnp.concatenate(outs, axis=0).T
        pl.when(step == steps - 1)(finish)

    pair = pl.BlockSpec((2, tile, LANE), lambda hp, i: (hp, i, 0))
    full = pl.BlockSpec((2, seq, LANE), lambda hp, i: (hp, 0, 0))
    return pl.pallas_call(
        body, name="attention_forward", grid=(N_HEADS // 2, nb),
        in_specs=[pair, full, full, ANY],
        out_specs=[pl.BlockSpec((tile, LANE), lambda hp, i: (i, hp)),
                   pl.BlockSpec((2, LANE, tile), lambda hp, i: (hp, 0, i)), ANY],
        out_shape=[jax.ShapeDtypeStruct((seq, D_ATT), F32),
                   jax.ShapeDtypeStruct((N_HEADS, LANE, seq), BF16),
                   jax.ShapeDtypeStruct((N_DEV,) + w_out.shape, w_out.dtype)],
        scratch_shapes=[pltpu.VMEM((2, tile, tile), F32), pltpu.VMEM((2, tile, tile), F32),
                        pltpu.VMEM((2, 8, tile), F32), pltpu.VMEM((2, LANE, tile), F32),
                        pltpu.SemaphoreType.DMA((N_GATHER_SEMS,)), pltpu.SemaphoreType.DMA((N_GATHER_SEMS,)),
                        pltpu.SemaphoreType.DMA((1,))],
        compiler_params=_params(("arbitrary", "arbitrary")),
    )(qp, kp, vp, w_out)


def _window_sum(x, halo, window, transposed):
    tile = x.shape[0]

    def split_cat(a):
        hi = a.astype(BF16)
        return jnp.concatenate([hi, (a - hi.astype(F32)).astype(BF16)], axis=1)

    def fold(r):
        return r[:, :LANE] + r[:, LANE:]

    r = lax.broadcasted_iota(jnp.int32, (tile, tile), 0)
    c = lax.broadcasted_iota(jnp.int32, (tile, tile), 1)
    rh = lax.broadcasted_iota(jnp.int32, (HALO, HALO), 0)
    ch = lax.broadcasted_iota(jnp.int32, (HALO, HALO), 1)
    if not transposed:
        band = (c <= r) & (r - c < window)
        edge = (rh + HALO - ch) < window
    else:
        band = (r <= c) & (c - r < window)
        edge = (HALO + ch - rh) < window
    out = fold(_dot(band.astype(BF16), split_cat(x)))
    reach = fold(_dot(edge.astype(BF16), split_cat(halo)))
    if not transposed:
        return jnp.concatenate([out[:HALO] + reach, out[HALO:]], axis=0)
    return jnp.concatenate([out[:tile - HALO], out[tile - HALO:] + reach], axis=0)


def _silu_parts(g):
    sig = _sigmoid(g)
    return g * sig, sig * (1.0 + g * (1.0 - sig))


def _middle(x, tgt, att, g_att, g_pool, p, vecs, pool_vecs, w_out, w_pool, tile):
    seq = x.shape[0]
    nt = seq // tile
    halo_blocks = tile // HALO

    def body(x_ref, tgt_ref, att_ref, ga_ref, gp_ref, p_ref, ph_ref, vec_ref, pvec_ref, wo_ref, wp_ref,
             dxa_ref, do2_ref, dga_ref, dgp_ref, dpooled_ref, gwo_ref, dwp_ref, dvec_ref, dwo_ref, dpvec_ref):
        i = pl.program_id(0)

        @pl.when(i == 0)
        def _():
            dwo_ref[...] = jnp.zeros_like(dwo_ref)
            dwp_ref[...] = jnp.zeros_like(dwp_ref)
            dvec_ref[...] = jnp.zeros_like(dvec_ref)
            dpvec_ref[...] = jnp.zeros_like(dpvec_ref)

        gate, b_out, ln_g, ln_b = (vec_ref[k:k + 1, :] for k in range(4))
        b_pool, pool_scale = pvec_ref[0:1, :], pvec_ref[1:2, :]
        x = x_ref[...]
        p = p_ref[...]
        p_halo = ph_ref[...] * jnp.where(i > 0, 1.0, 0.0)
        pos = i * tile + lax.broadcasted_iota(jnp.int32, (tile, 1), 0) + 1

        pooled, mixed = [], []
        for g, window in enumerate(POOL_WINDOWS):
            cols = slice(g * GROUP_DIM, (g + 1) * GROUP_DIM)
            wsum = _window_sum(p[:, cols], p_halo[:, cols], window, False)
            count = jnp.minimum(pos, window).astype(F32)
            pooled.append(wsum / count - p[:, cols])
            mixed.append(_dot(pooled[g].astype(BF16), wp_ref[g]) + b_pool[:, cols])
        mixed = jnp.concatenate(mixed, axis=1)
        pool = mixed * pool_scale

        att = att_ref[...]
        g_att, g_pool = ga_ref[...], gp_ref[...]
        silu_a, dsilu_a = _silu_parts(g_att)
        silu_p, dsilu_p = _silu_parts(g_pool)
        y_in = jnp.concatenate([att * silu_a, pool * silu_p], axis=1)
        y = _dot(y_in.astype(BF16), wo_ref[...]) + b_out
        h = ALPHA * x + gate * y
        mu = jnp.mean(h, axis=1, keepdims=True)
        hc = h - mu
        var = jnp.mean(hc * hc, axis=1, keepdims=True)
        rstd = lax.rsqrt(var + LN_EPS)
        yhat = hc * rstd
        diff = yhat * ln_g + ln_b - tgt_ref[...]
        loss_rows = jnp.sum(diff * diff, axis=1, keepdims=True)
        d_out = diff * (1.0 / D)

        d_yhat = d_out * ln_g
        dh = rstd * (d_yhat - jnp.mean(d_yhat, axis=1, keepdims=True)
                     - yhat * jnp.mean(d_yhat * yhat, axis=1, keepdims=True))
        dxa_ref[...] = ALPHA * dh
        dy = dh * gate
        dyb = dy.astype(BF16)
        lane = _lanes((1, D))
        loss_row = jnp.where(lane == 0, (0.5 / D) * jnp.sum(loss_rows, axis=0, keepdims=True), 0.0)
        dvec_ref[5:6, :] += jnp.sum(dh * y, axis=0, keepdims=True)
        dvec_ref[0:1, :] += jnp.sum(dy, axis=0, keepdims=True)
        dvec_ref[1:2, :] += jnp.sum(d_out * yhat, axis=0, keepdims=True)
        dvec_ref[2:3, :] += jnp.sum(d_out, axis=0, keepdims=True)
        dvec_ref[4:5, :] += loss_row

        dwo_ref[...] += _dot(y_in.T.astype(BF16), dyb)
        d_yin = _dot_nt(dyb, wo_ref[...])
        d_a, d_pl = d_yin[:, :D_ATT], d_yin[:, D_ATT:]
        d_att = d_a * silu_a
        d_att_t = d_att.T
        prod_t = (d_att * att).T
        sub = lax.broadcasted_iota(jnp.int32, (HEAD_DIM, tile), 0)
        for h in range(N_HEADS):
            rows = slice(h * HEAD_DIM, (h + 1) * HEAD_DIM)
            delta = jnp.sum(prod_t[rows], axis=0, keepdims=True)
            extra = _place3(sub, 0, [part.astype(F32) for part in _split3(delta)], 0.0)
            do2_ref[h] = jnp.concatenate([d_att_t[rows], extra], axis=0).astype(BF16)
        dga_ref[...] = d_a * att * dsilu_a
        dgp_ref[...] = d_pl * pool * dsilu_p
        d_pool = d_pl * silu_p
        d_mixed = d_pool * pool_scale
        dpvec_ref[0:1, :] += jnp.sum(d_mixed, axis=0, keepdims=True)
        dpvec_ref[1:2, :] += jnp.sum(d_pool * mixed, axis=0, keepdims=True)
        d_pooled = []
        for g in range(len(POOL_WINDOWS)):
            cols = slice(g * GROUP_DIM, (g + 1) * GROUP_DIM)
            dmb = d_mixed[:, cols].astype(BF16)
            dwp_ref[g] += _dot(pooled[g].T.astype(BF16), dmb)
            d_pooled.append(_dot_nt(dmb, wp_ref[g]))
        dpooled_ref[...] = jnp.concatenate(d_pooled, axis=1)

        @pl.when(i == nt - 1)
        def _():
            gwo_ref[...] = dwo_ref[...].astype(BF16)
            dvec_ref[3:4, :] = jnp.concatenate([dpvec_ref[0:1, :], dpvec_ref[1:2, :]], axis=1)

    tok = lambda width: pl.BlockSpec((tile, width), lambda i: (i, 0))
    whole = lambda a: pl.BlockSpec(a.shape, lambda i: (0,) * a.ndim)
    halo = pl.BlockSpec((HALO, D_POOL), lambda i: (jnp.maximum(i * halo_blocks - 1, 0), 0))
    half = jax.ShapeDtypeStruct((seq, D_ATT), F32)
    outs = [jax.ShapeDtypeStruct((seq, D), F32), jax.ShapeDtypeStruct((N_HEADS, LANE, seq), BF16), half, half, half,
            jax.ShapeDtypeStruct(w_out.shape, BF16), jax.ShapeDtypeStruct(w_pool.shape, F32),
            jax.ShapeDtypeStruct(vecs.shape, F32)]
    return pl.pallas_call(
        body, name="middle", grid=(nt,),
        in_specs=[tok(D), tok(D), tok(D_ATT), tok(D_ATT), tok(D_POOL), tok(D_POOL), halo,
                  whole(vecs), whole(pool_vecs), whole(w_out), whole(w_pool)],
        out_specs=[tok(D), pl.BlockSpec((N_HEADS, LANE, tile), lambda i: (0, 0, i)),
                   tok(D_ATT), tok(D_POOL), tok(D_POOL),
                   whole(w_out), whole(w_pool), whole(vecs)],
        out_shape=outs,
        scratch_shapes=[pltpu.VMEM(w_out.shape, F32), pltpu.VMEM(pool_vecs.shape, F32)],
        compiler_params=_params(("arbitrary",)),
    )(x, tgt, att, g_att, g_pool, p, p, vecs, pool_vecs, w_out, w_pool)


def _attention_backward(q2t, kp, vp, do2t, gw_out, vecs, pool, tile):
    seq = kp.shape[1]
    nb = seq // tile
    last = N_HEADS // 2 - 1

    def body(qt_ref, k_ref, v_ref, dot_ref, gwo_hbm, vecs_hbm, pool_hbm,
             dq_ref, dk_ref, dv_ref, dcum_ref, g_out_ref, vecs_sum_ref, pool_sum_ref,
             dq_acc, dk_acc, dv_acc, gwo_ref, vecs_ref, pool_ref,
             r1, s2, r2, v_sib, v_chip, v_recv, p_sib, p_chip, p_recv, send_sems, recv_sems):
        hp = pl.program_id(0)
        start, middle, fold, finish = _reduce_stages(
            [gwo_ref], [g_out_ref], [r1], [s2], [r2],
            [(vecs_ref, vecs_sum_ref, v_sib, v_chip, v_recv), (pool_ref, pool_sum_ref, p_sib, p_chip, p_recv)],
            send_sems, recv_sems)

        @pl.when(hp == 0)
        def _():
            pltpu.sync_copy(gwo_hbm, gwo_ref)
            pltpu.sync_copy(vecs_hbm, vecs_ref)
            pltpu.sync_copy(pool_hbm, pool_ref)
            start()

        pl.when(hp == 1)(middle)
        pl.when(hp == 2)(fold)

        row = lax.broadcasted_iota(jnp.int32, (tile, tile), 0)
        col = lax.broadcasted_iota(jnp.int32, (tile, tile), 1)
        dq_acc[...] = jnp.zeros_like(dq_acc)

        def kv_block(kb, _):
            krows = pl.ds(pl.multiple_of(kb * tile, tile), tile)
            k = [k_ref[hh, krows, :] for hh in range(2)]
            v = [v_ref[hh, krows, :] for hh in range(2)]
            k_t = [k[hh].T for hh in range(2)]

            def q_block(qb, masked):
                qcols = pl.ds(pl.multiple_of(qb * tile, tile), tile)
                for hh in range(2):
                    q_t = qt_ref[hh, :, qcols]
                    do_t = dot_ref[hh, :, qcols]
                    s_t = _dot(k[hh], q_t)
                    if masked:
                        s_t = jnp.where(row <= col, s_t, -1e30)
                    p_t = jnp.exp(s_t)
                    ds_t = (p_t * _dot(v[hh], do_t)).astype(BF16)
                    dv_new = _dot_nt(do_t, p_t.astype(BF16))
                    dk_new = _dot_nt(q_t, ds_t)
                    if masked:
                        dv_acc[hh], dk_acc[hh] = dv_new, dk_new
                    else:
                        dv_acc[hh] += dv_new
                        dk_acc[hh] += dk_new
                    dq_acc[hh, :, qcols] += _dot(k_t[hh], ds_t)

            q_block(kb, True)

            def two_later_blocks(j, _):
                q_block(kb + 1 + 2 * j, False)
                q_block(kb + 2 + 2 * j, False)
                return 0

            later = nb - 1 - kb
            lax.fori_loop(0, later // 2, two_later_blocks, 0)
            pl.when(later % 2 == 1)(lambda: q_block(nb - 1, False))
            for hh in range(2):
                dk = dk_acc[hh]
                dk_ref[hh, :, krows] = dk.astype(BF16)
                dv_ref[hh, :, krows] = dv_acc[hh].astype(BF16)
                dcum_ref[hh, :, krows] = -dk[L_CK:L_CK + 1, :]
            return 0

        lax.fori_loop(0, nb, kv_block, 0)
        for hh in range(2):
            dq = dq_acc[hh]
            dcum_ref[hh] += dq[L_CQ:L_CQ + 1, :]
            dq_ref[hh] = (dq * Q_SCALE).astype(BF16)
        pl.when(hp == last)(finish)

    pair = pl.BlockSpec((2, seq, LANE), lambda hp: (hp, 0, 0))
    pair_t = pl.BlockSpec((2, LANE, seq), lambda hp: (hp, 0, 0))
    whole = lambda shape: pl.BlockSpec(shape, lambda hp: (0,) * len(shape))
    grad = jax.ShapeDtypeStruct((N_HEADS, LANE, seq), BF16)
    return pl.pallas_call(
        body, name="attention_backward", grid=(N_HEADS // 2,),
        in_specs=[pair_t, pair, pair, pair_t, ANY, ANY, ANY],
        out_specs=[pair_t, pair_t, pair_t, pl.BlockSpec((2, 1, seq), lambda hp: (hp, 0, 0)),
                   whole(gw_out.shape[1:]), whole(vecs.shape), whole(pool.shape)],
        out_shape=[grad, grad, grad, jax.ShapeDtypeStruct((N_HEADS, 1, seq), F32),
                   jax.ShapeDtypeStruct(gw_out.shape[1:], F32), jax.ShapeDtypeStruct(vecs.shape, F32),
                   jax.ShapeDtypeStruct(pool.shape, F32)],
        scratch_shapes=[pltpu.VMEM((2, LANE, seq), F32), pltpu.VMEM((2, LANE, tile), F32),
                        pltpu.VMEM((2, LANE, tile), F32), pltpu.VMEM(gw_out.shape, BF16),
                        pltpu.VMEM(vecs.shape, F32), pltpu.VMEM(pool.shape, F32)]
        + _reduce_scratch(gw_out, [vecs, pool]),
        compiler_params=_params(("arbitrary",)),
    )(q2t, kp, vp, do2t, gw_out, vecs, pool)


def _inproj_backward(dqp, dkp, dvp, d_cum, f, d_pooled, d_ga, d_gp, x, dxa, u, mod, w_main, w_f, tile):
    seq = x.shape[0]
    nt = seq // tile
    halo_blocks = tile // HALO

    def body(dq_ref, dk_ref, dv_ref, dcum_ref, f_ref, dpo_ref, dph_ref, dga_ref, dgp_ref, x_ref, dxa_ref, u_ref,
             mod_ref, w_ref, wf_ref,
             dx_ref, dproj_ref, dwf_ref, db_ref, dbf_ref, dmod_ref, carry_ref):
        step = pl.program_id(0)
        i = nt - 1 - step

        @pl.when(step == 0)
        def _():
            carry_ref[...] = jnp.zeros_like(carry_ref)
            dwf_ref[...] = jnp.zeros_like(dwf_ref)
            db_ref[...] = jnp.zeros_like(db_ref)
            dbf_ref[...] = jnp.zeros_like(dbf_ref)
            dmod_ref[...] = jnp.zeros_like(dmod_ref)

        ones = jnp.ones((8, tile), BF16)

        def emit(chunk, val):
            cols = pl.ds(chunk * COL_CHUNK, COL_CHUNK)
            db_ref[0:1, cols] += jnp.sum(val, axis=0, keepdims=True)
            vb = val.astype(BF16)
            dproj_ref[:, pl.ds((chunk - 3) * COL_CHUNK, COL_CHUNK)] = vb
            return _dot(vb, w_ref[cols, :])

        d_u = jnp.zeros((tile, D), F32)
        for chunk, ref in enumerate((dq_ref, dk_ref, dv_ref)):
            cols = pl.ds(chunk * COL_CHUNK, COL_CHUNK)
            val_t = ref[:, 0:HEAD_DIM, :].reshape(COL_CHUNK, tile)
            db_ref[:, cols] += _dot_nt(ones, val_t)
            d_u += _dot_tn(val_t, w_ref[cols, :])

        d_pooled = dpo_ref[...]
        d_halo = dph_ref[...] * jnp.where(i < nt - 1, 1.0, 0.0)
        pos = i * tile + lax.broadcasted_iota(jnp.int32, (tile, 1), 0) + 1
        d_p = []
        for g, window in enumerate(POOL_WINDOWS):
            cols = slice(g * GROUP_DIM, (g + 1) * GROUP_DIM)
            scaled = d_pooled[:, cols] / jnp.minimum(pos, window).astype(F32)
            d_p.append(_window_sum(scaled, d_halo[:, cols] * (1.0 / window), window, True) - d_pooled[:, cols])
        d_u += emit(3, jnp.concatenate(d_p, axis=1))
        d_u += emit(4, dga_ref[...])
        d_u += emit(5, dgp_ref[...])

        row = lax.broadcasted_iota(jnp.int32, (tile, tile), 0)
        col = lax.broadcasted_iota(jnp.int32, (tile, tile), 1)
        later = (row >= col).astype(BF16)
        d_logf = sum(_dot(part, later) for part in _split3(dcum_ref[:, 0, :])) + carry_ref[:, 0:1]
        carry_ref[:, 0:1] = d_logf[:, 0:1]
        d_f = d_logf * _sigmoid(-f_ref[...].T[0:N_HEADS, :])
        d_f = jnp.concatenate([d_f, jnp.zeros((LANE - N_HEADS, tile), F32)], axis=0)
        dbf_ref[...] += sum(_dot_nt(ones, part) for part in _split3(d_f))
        d_fb = d_f.astype(BF16)
        d_u += _dot_tn(d_fb, wf_ref[...])
        dwf_ref[...] += _dot(d_fb, u_ref[...])

        x = x_ref[...]
        dx_ref[...] = dxa_ref[...] + d_u * mod_ref[0:1, :]
        dmod_ref[0:1, :] += jnp.sum(d_u * x, axis=0, keepdims=True)
        dmod_ref[1:2, :] += jnp.sum(d_u, axis=0, keepdims=True)

    rev = lambda step: nt - 1 - step
    tok = lambda width: pl.BlockSpec((tile, width), lambda s: (rev(s), 0))
    head_block = pl.BlockSpec((N_HEADS, LANE, tile), lambda s: (0, 0, rev(s)))
    whole = lambda a: pl.BlockSpec(a.shape, lambda s: (0,) * a.ndim)
    halo = pl.BlockSpec((HALO, D_POOL), lambda s: (jnp.minimum((rev(s) + 1) * halo_blocks, seq // HALO - 1), 0))
    small = lambda width: jax.ShapeDtypeStruct((8, width), F32)
    n_rest = N_MAIN - OFF_P
    return pl.pallas_call(
        body, name="inproj_backward", grid=(nt,),
        in_specs=[head_block, head_block, head_block, pl.BlockSpec((N_HEADS, 1, tile), lambda s: (0, 0, rev(s))),
                  tok(LANE), tok(D_POOL), halo, tok(D_ATT), tok(D_POOL),
                  tok(D), tok(D), tok(D),
                  whole(mod), whole(w_main), whole(w_f)],
        out_specs=[tok(D), tok(n_rest), pl.BlockSpec((LANE, D), lambda s: (0, 0)),
                   pl.BlockSpec((8, N_MAIN), lambda s: (0, 0)), pl.BlockSpec((8, LANE), lambda s: (0, 0)),
                   pl.BlockSpec((8, D), lambda s: (0, 0))],
        out_shape=[jax.ShapeDtypeStruct((seq, D), F32), jax.ShapeDtypeStruct((seq, n_rest), BF16),
                   jax.ShapeDtypeStruct((LANE, D), F32), small(N_MAIN), small(LANE), small(D)],
        scratch_shapes=[pltpu.VMEM((8, LANE), F32)],
        compiler_params=_params(("arbitrary",)),
    )(dqp, dkp, dvp, d_cum, f, d_pooled, d_pooled, d_ga, d_gp, x, dxa, u, mod, w_main, w_f)


def _weight_grads(dq_t, dk_t, dv_t, dw_f, dproj, u, k_tile):
    seq = u.shape[0]
    nk = seq // k_tile
    rows = N_HEADS * HEAD_DIM

    def body(dq_ref, dk_ref, dv_ref, dwf_ref, dp_ref, u_ref, out_ref, acc_ref):
        k = pl.program_id(0)

        @pl.when(k == 0)
        def _():
            acc_ref[...] = jnp.zeros_like(acc_ref)

        tokens = u_ref[...]
        for j, ref in enumerate((dq_ref, dk_ref, dv_ref)):
            acc_ref[pl.ds(j * rows, rows), :] += _dot(ref[...].reshape(rows, k_tile), tokens)
        for j in range(dproj.shape[1] // COL_CHUNK):
            cols = pl.ds(j * COL_CHUNK, COL_CHUNK)
            acc_ref[pl.ds(F_HI + j * COL_CHUNK, COL_CHUNK), :] += _dot_tn(dp_ref[:, cols], tokens)

        @pl.when(k == nk - 1)
        def _():
            acc_ref[F_LO:F_HI, :] = dwf_ref[0:N_HEADS, :]
            for slot in range(N_DEV):
                out_ref[slot] = acc_ref[slot * shard:(slot + 1) * shard, :].astype(BF16)

    shard = D_IN // N_DEV
    heads = pl.BlockSpec((N_HEADS, HEAD_DIM, k_tile), lambda k: (0, 0, k))
    return pl.pallas_call(
        body, name="weight_grads", grid=(nk,),
        in_specs=[heads, heads, heads, pl.BlockSpec(dw_f.shape, lambda k: (0, 0)),
                  pl.BlockSpec((k_tile, dproj.shape[1]), lambda k: (k, 0)), pl.BlockSpec((k_tile, D), lambda k: (k, 0))],
        out_specs=pl.BlockSpec((N_DEV, shard, D), lambda k: (0, 0, 0)),
        out_shape=jax.ShapeDtypeStruct((N_DEV, shard, D), BF16),
        scratch_shapes=[pltpu.VMEM((D_IN, D), F32)],
        compiler_params=_params(("arbitrary",)),
    )(dq_t, dk_t, dv_t, dw_f, dproj, u)


def _adamw(w, g, m, v):
    m = ADAM_B1 * m + (1.0 - ADAM_B1) * g
    v = ADAM_B2 * v + (1.0 - ADAM_B2) * (g * g)
    m_hat = m / (1.0 - ADAM_B1 ** ADAM_STEP)
    v_hat = v / (1.0 - ADAM_B2 ** ADAM_STEP)
    delta = -ADAM_LR * (m_hat / (jnp.sqrt(v_hat) + ADAM_EPS) + ADAM_WD * w)
    return delta, m, v


def _adamw_call(g, w, m, v, col_tile, name):
    nr = w.shape[1] // col_tile

    def body(gi_ref, w_ref, m_ref, v_ref, g_ref, d_ref, nm_ref, nv_ref):
        g = gi_ref[...]
        g_ref[...] = g
        d_ref[...], nm_ref[...], nv_ref[...] = _adamw(w_ref[...], g, m_ref[...], v_ref[...])

    blk = pl.BlockSpec((w.shape[0], col_tile), lambda r: (0, r))
    shape = jax.ShapeDtypeStruct(w.shape, F32)
    return pl.pallas_call(
        body, name=name, grid=(nr,),
        in_specs=[blk, blk, blk, blk], out_specs=[blk, blk, blk, blk],
        out_shape=[shape, shape, shape, shape],
        compiler_params=_params(("arbitrary",)),
    )(g, w, m, v)


def _ada_adamw(sc_t, d_ada, w, m, v):
    def body(sc_ref, d_ref, w_ref, m_ref, v_ref, g_ref, dl_ref, nm_ref, nv_ref):
        g = sc_ref[:, 0:1] * d_ref[0:1, :]
        for b in range(1, N_DEV):
            g = g + sc_ref[:, b:b + 1] * d_ref[b:b + 1, :]
        g_ref[...] = g
        dl_ref[...], nm_ref[...], nv_ref[...] = _adamw(w_ref[...], g, m_ref[...], v_ref[...])

    row_tile = 256
    blk = pl.BlockSpec((row_tile, w.shape[1]), lambda r: (r, 0))
    shape = jax.ShapeDtypeStruct(w.shape, F32)
    return pl.pallas_call(
        body, name="ada_adamw", grid=(w.shape[0] // row_tile,),
        in_specs=[pl.BlockSpec((row_tile, N_DEV), lambda r: (r, 0)), pl.BlockSpec(d_ada.shape, lambda r: (0, 0)),
                  blk, blk, blk],
        out_specs=[blk, blk, blk, blk], out_shape=[shape, shape, shape, shape],
        compiler_params=_params(("arbitrary",)),
    )(sc_t, d_ada, w, m, v)


F_LO, F_HI = 3 * D_ATT, 3 * D_ATT + N_HEADS


def _split_forget(a, axis):
    idx = lambda lo, hi: tuple(slice(lo, hi) if d == axis else slice(None) for d in range(a.ndim))
    pad = [(0, LANE - N_HEADS) if d == axis else (0, 0) for d in range(a.ndim)]
    return jnp.concatenate([a[idx(0, F_LO)], a[idx(F_HI, D_IN)]], axis=axis), jnp.pad(a[idx(F_LO, F_HI)], pad)


def _join_forget(main, f, axis):
    idx = lambda lo, hi: tuple(slice(lo, hi) if d == axis else slice(None) for d in range(main.ndim))
    return jnp.concatenate([main[idx(0, F_LO)], f[idx(0, N_HEADS)], main[idx(F_LO, N_MAIN)]], axis=axis)


def _adamw_small(grad_rows, row_params, whole_params, summed_params):
    n_row, n_whole, n_sum = len(row_params), len(whole_params), len(summed_params)
    n = n_row + n_whole + n_sum

    def body(g_ref, *refs):
        n_in = 3 * n_row + 4 * (n_whole + n_sum)
        ins, outs = list(refs[:n_in]), refs[n_in:]
        for i in range(n):
            if i < n_row:
                row, lo, hi = row_params[i][:3]
                g = g_ref[row:row + 1, lo:hi]
            elif i < n_row + n_whole:
                g = ins.pop(0)[...]
            else:
                parts = ins.pop(0)
                g = parts[0]
                for k in range(1, N_DEV):
                    g = g + parts[k]
            w, m, v = (ins.pop(0)[...] for _ in range(3))
            outs[4 * i][...] = g
            outs[4 * i + 1][...], outs[4 * i + 2][...], outs[4 * i + 3][...] = _adamw(w, g, m, v)

    shapes = [p[3] for p in row_params] + [p[1] for p in whole_params] + [p[1] for p in summed_params]
    operands = [a for p in row_params for a in p[3:]] + [a for p in whole_params + summed_params for a in p]
    flat = pl.pallas_call(
        body, name="adamw_small",
        out_shape=[jax.ShapeDtypeStruct(w.shape, F32) for w in shapes for _ in range(4)],
        compiler_params=_params(),
    )(grad_rows, *operands)
    return [flat[4 * i:4 * i + 4] for i in range(n)]


def kernel(x, c, w_ada, b_ada, w_in, b_in, w_pool_mix, b_pool_mix, pool_scale, w_out, b_out, ln_g, ln_b, loss_target, m_w_ada, m_b_ada, m_w_in, m_b_in, m_w_pool_mix, m_b_pool_mix, m_pool_scale, m_w_out, m_b_out, m_ln_g, m_ln_b, v_w_ada, v_b_ada, v_w_in, v_b_in, v_w_pool_mix, v_b_pool_mix, v_pool_scale, v_w_out, v_b_out, v_ln_g, v_ln_b):
    seq = x.shape[1]
    tile = min(256, seq)
    attn_tile = min(512, max(128, seq // 4))
    me = _dev_index(*_mesh_pos())
    x2, tgt = x[0], loss_target[0]

    rows_in = D_IN // N_DEV
    w_in_g, sc_all, ada_mine = _gather_and_ada(c, w_in[0].T.astype(BF16), w_ada[0])
    ada = ada_mine.reshape(1, D_ADA) + b_ada
    shift, scale, gate = ada[:, 0:D], ada[:, D:2 * D], ada[:, 2 * D:]
    mod = jnp.concatenate([1.0 + scale, shift, jnp.zeros((6, D), F32)], axis=0)

    w_main, w_f = _split_forget(w_in_g.reshape(D_IN, D), 0)
    b_main, b_f = _split_forget(b_in, 1)

    qp, kp, vp, f, p, g_att, g_pool, u = _inproj_forward(x2, mod, w_main, w_f, b_main, b_f, tile)
    att, q2t, w_out_g = _attention_forward(qp, kp, vp, w_out[0].astype(BF16), attn_tile)

    vecs = jnp.concatenate([gate, b_out, ln_g, ln_b, jnp.zeros((4, D), F32)], axis=0)
    pool_vecs = jnp.concatenate([b_pool_mix.reshape(1, D_POOL), pool_scale, jnp.zeros((6, D_POOL), F32)], axis=0)
    dxa, do2, d_ga, d_gp, d_pooled, gw_out, dw_pool, dvec = _middle(
        x2, tgt, att, g_att, g_pool, p, vecs, pool_vecs, w_out_g.reshape(D, D), w_pool_mix[0].astype(BF16), tile)

    pool_rows = w_pool_mix.shape[1] * GROUP_DIM
    dqp, dkp, dvp, d_cum, g_out, dvec_sum, dw_pool_sum = _attention_backward(
        q2t, kp, vp, do2, gw_out.reshape(N_DEV, D // N_DEV, D), dvec, dw_pool.reshape(pool_rows, GROUP_DIM), attn_tile)
    dx, dproj, dw_f, db_main, db_f, dmod = _inproj_backward(
        dqp, dkp, dvp, d_cum, f, d_pooled, d_ga, d_gp, x2, dxa, u, mod, w_main, w_f, tile)
    gw_in = _weight_grads(dqp, dkp, dvp, dw_f, dproj, u, min(512, seq))
    d_ada = jnp.concatenate([dmod[1:2], dmod[0:1], dvec[5:6]], axis=1)
    g_in_rows, g_b_in, d_ada_all = _reduce_grads(gw_in, _join_forget(db_main[0:1], db_f[0:1], 1), d_ada)

    outs_in = _adamw_call(g_in_rows, w_in[0].T, m_w_in[0].T, v_w_in[0].T, D // 4, "adamw_w_in")
    g_w_in, d_w_in, nm_w_in, nv_w_in = (a.T for a in outs_in)
    flat_pool = lambda a: a.reshape(1, D_POOL)
    pool_2d = lambda a: a.reshape(pool_rows, GROUP_DIM)
    rows = _adamw_small(
        dvec_sum,
        [(0, 0, D, b_out, m_b_out, v_b_out), (1, 0, D, ln_g, m_ln_g, v_ln_g), (2, 0, D, ln_b, m_ln_b, v_ln_b),
         (3, 0, D_POOL, flat_pool(b_pool_mix), flat_pool(m_b_pool_mix), flat_pool(v_b_pool_mix)),
         (3, D_POOL, 2 * D_POOL, pool_scale, m_pool_scale, v_pool_scale)],
        [(g_out, w_out[0], m_w_out[0], v_w_out[0]),
         (dw_pool_sum, pool_2d(w_pool_mix), pool_2d(m_w_pool_mix), pool_2d(v_w_pool_mix)),
         (g_b_in, b_in, m_b_in, v_b_in)],
        [(d_ada_all, b_ada, m_b_ada, v_b_ada)])
    small = {"b_out": rows[0], "ln_g": rows[1], "ln_b": rows[2],
             "b_pool": [a.reshape(b_pool_mix.shape) for a in rows[3]], "pool_scale": rows[4],
             "w_pool": [a.reshape(w_pool_mix.shape) for a in rows[6]], "b_in": rows[7]}
    g_s, d_s, nm_s, nv_s = ({k: r[j] for k, r in small.items()} for j in range(4))
    g_w_out, d_w_out, nm_w_out, nv_w_out = rows[5]
    g_b_ada, d_b_ada, nm_b_ada, nv_b_ada = rows[8]
    loss = dvec_sum[4, 0]

    d_ada_local = lax.dynamic_slice_in_dim(d_ada_all.reshape(N_DEV, D_ADA), me * (D_ADA // N_DEV), D_ADA // N_DEV, axis=1)
    g_w_ada, d_w_ada, nm_w_ada, nv_w_ada = _ada_adamw(sc_all.T, d_ada_local, w_ada[0], m_w_ada[0], v_w_ada[0])

    def ordered(w_ada_, b_ada_, w_in_, w_out_, s):
        return (w_ada_[None], b_ada_, w_in_[None], s["b_in"], s["w_pool"], s["b_pool"], s["pool_scale"],
                w_out_[None], s["b_out"], s["ln_g"], s["ln_b"])

    return (loss, dx[None],
            *ordered(g_w_ada, g_b_ada, g_w_in, g_w_out, g_s),
            *ordered(d_w_ada, d_b_ada, d_w_in, d_w_out, d_s),
            *ordered(nm_w_ada, nm_b_ada, nm_w_in, nm_w_out, nm_s),
            *ordered(nv_w_ada, nv_b_ada, nv_w_in, nv_w_out, nv_s))
```

```python
import jax
import jax.numpy as jnp
from jax import lax
from jax.experimental import pallas as pl
from jax.experimental.pallas import tpu as pltpu

F32 = jnp.float32
BF16 = jnp.bfloat16

N_DEV = 8
D = 1024
N_HEADS = 8
HEAD_DIM = 64
D_ATT = 512
D_POOL = 512
POOL_WINDOWS = (2, 4, 8, 16)
GROUP_DIM = 128
HALO = 16
LANE = 128
D_IN = 3080
D_ADA = 3072
N_MAIN = 3072
OFF_P = 1536
COL_CHUNK = 512
Q_SCALE = 0.125
LN_EPS = 1e-5
ALPHA = 2.0 ** 0.25
L_CQ, L_CK, L_LSE = 64, 67, 70

ADAM_LR, ADAM_B1, ADAM_B2, ADAM_EPS, ADAM_WD, ADAM_STEP = 0.001, 0.9, 0.999, 1e-08, 0.01, 10
VMEM_LIMIT = 56 * 1024 * 1024

MESH = pl.DeviceIdType.MESH
ANY = pl.BlockSpec(memory_space=pl.ANY)


def _params(sem=None, vmem=VMEM_LIMIT):
    return pltpu.CompilerParams(dimension_semantics=sem, vmem_limit_bytes=vmem)


def _split3(a):
    hi = a.astype(BF16)
    r = a - hi.astype(F32)
    mid = r.astype(BF16)
    lo = (r - mid.astype(F32)).astype(BF16)
    return hi, mid, lo


def _dot(a, b):
    return jnp.dot(a, b, preferred_element_type=F32)


def _dot_nt(a, b):
    return lax.dot_general(a, b, (((1,), (1,)), ((), ())), preferred_element_type=F32)


def _dot_tn(a, b):
    return lax.dot_general(a, b, (((0,), (0,)), ((), ())), preferred_element_type=F32)


def _dot3(m01, a):
    hi, mid, lo = _split3(a)
    return _dot(m01, hi) + _dot(m01, mid) + _dot(m01, lo)


def _sigmoid(z):
    return 1.0 / (1.0 + jnp.exp(-z))


def _lanes(shape):
    return lax.broadcasted_iota(jnp.int32, shape, len(shape) - 1)


def _place3(lane, base, parts, other):
    out = other
    for j in range(3):
        out = jnp.where(lane == base + j, parts[j], out)
    return out


def _mesh_pos():
    return lax.axis_index("x"), lax.axis_index("y"), lax.axis_index("c")


def _dev_index(px, py, pc):
    return 4 * px + 2 * py + pc


N_GATHER_SEMS = 9


def _gather_stages(src_ref, out_ref, send_sems, recv_sems, local_sem):
    x, y, c = _mesh_pos()
    me, sibling = (x, y, c), (x, y, 1 - c)
    nbr_x, nbr_y, diag = (1 - x, y), (x, 1 - y), (1 - x, 1 - y)
    half = out_ref.shape[-1] // 2
    left, right = pl.ds(0, half), pl.ds(half, half)

    def copy(k, block, to, cols=None, src=None):
        slot = out_ref.at[_dev_index(*block)]
        if cols is not None:
            slot = slot.at[:, cols]
        return pltpu.make_async_remote_copy(
            src_ref=slot if src is None else src, dst_ref=slot, send_sem=send_sems.at[k], recv_sem=recv_sems.at[k],
            device_id=to, device_id_type=MESH)

    mine = pltpu.make_async_copy(src_ref, out_ref.at[_dev_index(*me)], local_sem)
    first = [copy(0, me, sibling, src=src_ref), copy(1, me, (*nbr_x, c), src=src_ref), copy(2, me, (*nbr_y, c), src=src_ref)]
    relay = [(1, nbr_x, None, nbr_x), (2, nbr_y, None, nbr_y), (3, diag, left, nbr_y), (4, diag, right, nbr_x)]
    onward = [copy(3, (*nbr_x, c), (*nbr_y, c), cols=left), copy(4, (*nbr_y, c), (*nbr_x, c), cols=right)]
    passed = [copy(4 + k, (*block, c), sibling, cols=cols) for k, block, cols, _ in relay]

    def start():
        mine.start()
        for cp in first:
            cp.start()

    def relay_stage(first_item):
        def run():
            for j in (first_item, first_item + 1):
                k, block, cols, frm = relay[j]
                copy(k, (*block, c), (*frm, c), cols=cols).wait_recv()
                if j < 2:
                    onward[j].start()
                passed[j].start()
        return run

    def finish():
        copy(0, sibling, me).wait_recv()
        for k, block, cols, _ in relay:
            copy(4 + k, (*block, 1 - c), me, cols=cols).wait_recv()
        for cp in first + onward + passed:
            cp.wait_send()
        mine.wait()

    return start, relay_stage(0), relay_stage(2), finish


N_REDUCE_SEMS = 10
N_SMALL_SEMS = 4
N_ROWS_SEMS = 7


def _reduce_stages(ins, gs, r1, s2, r2, smalls, send_sems, recv_sems, rows=None, own=None):
    n = len(ins)
    x, y, c = _mesh_pos()
    me = _dev_index(x, y, c)
    sibling = (x, y, 1 - c)
    chips = [(x, y), (1 - x, y), (x, 1 - y), (1 - x, 1 - y)]
    peers = []
    for p in range(1, N_DEV):
        px, py, pc = (p >> 2) & 1, (p >> 1) & 1, p & 1
        peers.append((1 - x if px else x, 1 - y if py else y, 1 - c if pc else c))
    base_small = N_REDUCE_SEMS * n

    def remote(src, dst, k, to):
        return pltpu.make_async_remote_copy(src_ref=src, dst_ref=dst, send_sem=send_sems.at[k],
                                            recv_sem=recv_sems.at[k], device_id=to, device_id_type=MESH)

    def level1(a, q):
        return remote(ins[a].at[_dev_index(*chips[q], 1 - c)], r1[a].at[q], N_REDUCE_SEMS * a + q, sibling)

    def level2(a, k):
        half = ins[a].shape[-1] // 2
        left, right = pl.ds(0, half), pl.ds(half, half)
        nbr_x, nbr_y = (*chips[1], c), (*chips[2], c)
        src_slot, dst_slot, cols, to = [(0, 0, left, nbr_x), (1, 1, right, nbr_y), (2, 2, left, nbr_x),
                                        (2, 2, right, nbr_y), (0, 0, right, nbr_x), (1, 1, left, nbr_y)][k]
        return remote(s2[a].at[src_slot, :, cols], r2[a].at[dst_slot, :, cols], N_REDUCE_SEMS * a + 4 + k, to)

    to_sibling = [remote(sm[0], sm[2], base_small + 4 * i, sibling) for i, sm in enumerate(smalls)]
    to_chips = [[remote(sm[3], sm[4].at[j], base_small + 4 * i + 1 + j, (*chips[j + 1], c)) for j in range(3)]
                for i, sm in enumerate(smalls)]
    if rows is not None:
        rows_ref, land_ref, all_ref = rows
        base_rows = base_small + 4 * len(smalls)
        row_sends = [remote(rows_ref, land_ref.at[me], base_rows + k, to) for k, to in enumerate(peers)]

    def mine(a, q):
        buf, sems = own[a]
        return pltpu.make_async_copy(ins[a].at[_dev_index(*chips[q], c)], buf.at[q], sems.at[q])

    def start():
        for a in range(n):
            for q in range(4):
                level1(a, q).start()
            if own is not None:
                for q in (1, 2, 3, 0):
                    mine(a, q).start()
        for cp in to_sibling:
            cp.start()
        if rows is not None:
            for cp in row_sends:
                cp.start()
            land_ref[me] = rows_ref[...]

    def middle():
        for a in range(n):
            for q in (1, 2, 3, 0):
                level1(a, q).wait_recv()
                if own is None:
                    kept = ins[a][_dev_index(*chips[q], c)]
                else:
                    mine(a, q).wait()
                    kept = own[a][0][q]
                pair = kept.astype(F32) + r1[a][q].astype(F32)
                if q == 0:
                    gs[a][...] = pair
                else:
                    s2[a][q - 1] = pair.astype(BF16)
                    for k in ((0,), (1,), (2, 3))[q - 1]:
                        level2(a, k).start()
        for i, (small_ref, _, sm_sib, sm_chip, _) in enumerate(smalls):
            to_sibling[i].wait_recv()
            sm_chip[...] = small_ref[...] + sm_sib[...]
            for cp in to_chips[i]:
                cp.start()

    def fold():
        for a in range(n):
            half = ins[a].shape[-1] // 2
            level2(a, 3).wait_recv()
            s2[a][0, :, half:] = (s2[a][0, :, half:].astype(F32) + r2[a][2, :, half:].astype(F32)).astype(BF16)
            level2(a, 4).start()
            level2(a, 2).wait_recv()
            s2[a][1, :, :half] = (s2[a][1, :, :half].astype(F32) + r2[a][2, :, :half].astype(F32)).astype(BF16)
            level2(a, 5).start()

    def finish():
        for a in range(n):
            for k in (0, 1, 4, 5):
                level2(a, k).wait_recv()
            gs[a][...] = gs[a][...] + r2[a][0].astype(F32) + r2[a][1].astype(F32)
            for q in range(4):
                level1(a, q).wait_send()
            for k in range(6):
                level2(a, k).wait_send()
        for i, (_, total_ref, _, sm_chip, sm_recv) in enumerate(smalls):
            for cp in to_chips[i]:
                cp.wait_recv()
            total = None
            for ax in range(2):
                for ay in range(2):
                    dx, dy = x != ax, y != ay
                    term = jnp.where(dx, jnp.where(dy, sm_recv[2], sm_recv[0]), jnp.where(dy, sm_recv[1], sm_chip[...]))
                    total = term if total is None else total + term
            total_ref[...] = total
            for cp in [to_sibling[i]] + to_chips[i]:
                cp.wait_send()
        if rows is not None:
            for k, frm in enumerate(peers):
                remote(rows_ref, land_ref.at[_dev_index(*frm)], base_rows + k, frm).wait_recv()
            all_ref[...] = land_ref[...]
            for cp in row_sends:
                cp.wait_send()

    return start, middle, fold, finish


def _reduce_scratch(shard, smalls, rows=None):
    out = [pltpu.VMEM((lead,) + shard.shape[1:], BF16) for lead in (4, 3, 3)]
    for small in smalls:
        out += [pltpu.VMEM(small.shape, F32), pltpu.VMEM(small.shape, F32), pltpu.VMEM((3,) + small.shape, F32)]
    n_sems = N_REDUCE_SEMS + N_SMALL_SEMS * len(smalls)
    if rows is not None:
        out.append(pltpu.VMEM((N_DEV,) + rows.shape, F32))
        n_sems += N_ROWS_SEMS
    return out + [pltpu.SemaphoreType.DMA((n_sems,))] * 2


def _reduce_grads(gw_in, small, rows):
    def body(in_ref, small_ref, rows_ref, g_ref, total_ref, rows_all_ref,
             r1, s2, r2, sm_sib, sm_chip, sm_recv, rows_land, send_sems, recv_sems, kept, kept_sems):
        stages = _reduce_stages(
            [in_ref], [g_ref], [r1], [s2], [r2], [(small_ref, total_ref, sm_sib, sm_chip, sm_recv)],
            send_sems, recv_sems, rows=(rows_ref, rows_land, rows_all_ref), own=[(kept, kept_sems)])
        for stage in stages:
            stage()

    vmem = pl.BlockSpec(memory_space=pltpu.VMEM)
    return pl.pallas_call(
        body, name="reduce_grads",
        in_specs=[ANY, vmem, vmem], out_specs=[vmem, vmem, vmem],
        out_shape=[jax.ShapeDtypeStruct(gw_in.shape[1:], F32), jax.ShapeDtypeStruct(small.shape, F32),
                   jax.ShapeDtypeStruct((N_DEV,) + rows.shape, F32)],
        scratch_shapes=_reduce_scratch(gw_in, [small], rows)
        + [pltpu.VMEM((4,) + gw_in.shape[1:], BF16), pltpu.SemaphoreType.DMA((4,))],
        compiler_params=_params(),
    )(gw_in, small, rows)


def _dot3_rhs(a, b):
    a0, a1, a2 = _split3(a)
    b0, b1, b2 = _split3(b)
    return (_dot(a0, b0) + (_dot(a0, b1) + _dot(a1, b0))
            + (_dot(a0, b2) + _dot(a1, b1) + _dot(a2, b0)))


def _gather_and_ada(c, w_in_rows, w_ada):
    cols = w_ada.shape[1]
    shard = w_in_rows.shape[0]

    def body(c_ref, w_ref, wa_ref, w_main_ref, w_f_ref, sc_ref, ada_ref,
             w_all_ref, w_f32, c_land, part, ada_land, send_sems, recv_sems, local_sem, x_send, x_recv):
        x, y, cc = _mesh_pos()
        me = _dev_index(x, y, cc)
        peers = []
        for p in range(1, N_DEV):
            px, py, pc = (p >> 2) & 1, (p >> 1) & 1, p & 1
            peers.append((1 - x if px else x, 1 - y if py else y, 1 - cc if pc else cc))

        def remote(src, dst, k, to):
            return pltpu.make_async_remote_copy(src_ref=src, dst_ref=dst, send_sem=x_send.at[k], recv_sem=x_recv.at[k],
                                                device_id=to, device_id_type=MESH)

        c_sends = [remote(c_ref, c_land.at[me], k, to) for k, to in enumerate(peers)]
        for cp in c_sends:
            cp.start()
        start, relay_near, relay_far, finish = _gather_stages(w_ref, w_all_ref, send_sems, recv_sems, local_sem.at[0])
        start()
        c_land[me] = c_ref[...]
        for k, frm in enumerate(peers):
            remote(c_ref, c_land.at[_dev_index(*frm)], k, frm).wait_recv()
        c_all = jnp.concatenate([c_land[b] for b in range(N_DEV)], axis=0)
        sc = c_all * _sigmoid(c_all)
        sc_ref[...] = sc
        rows = _dot3_rhs(sc, wa_ref[...])
        for b in range(N_DEV):
            part[b] = rows[b:b + 1, :]
        a_sends = [remote(part.at[_dev_index(*to)], ada_land.at[me], 7 + k, to) for k, to in enumerate(peers)]
        for cp in a_sends:
            cp.start()
        ada_land[me] = part[me]
        for k, frm in enumerate(peers):
            remote(part.at[0], ada_land.at[_dev_index(*frm)], 7 + k, frm).wait_recv()
        ada_ref[...] = ada_land[...]

        relay_near()
        relay_far()
        finish()
        for cp in c_sends + a_sends:
            cp.wait_send()

        for slot in range(N_DEV):
            w_f32[slot * shard:(slot + 1) * shard, :] = w_all_ref[slot].astype(F32)
        w_main_ref[0:F_LO, :] = w_f32[0:F_LO, :].astype(BF16)
        w_main_ref[F_LO:N_MAIN, :] = w_f32[F_HI:D_IN, :].astype(BF16)
        w_f_ref[...] = jnp.concatenate(
            [w_f32[F_LO:F_HI, :], jnp.zeros((LANE - N_HEADS, D), F32)], axis=0).astype(BF16)

    vmem = pl.BlockSpec(memory_space=pltpu.VMEM)
    return pl.pallas_call(
        body, name="gather_weights",
        in_specs=[vmem, ANY, vmem], out_specs=[vmem, vmem, vmem, vmem],
        out_shape=[jax.ShapeDtypeStruct((N_MAIN, D), BF16), jax.ShapeDtypeStruct((LANE, D), BF16),
                   jax.ShapeDtypeStruct((N_DEV, D), F32), jax.ShapeDtypeStruct((N_DEV, 1, cols), F32)],
        scratch_shapes=[pltpu.VMEM((N_DEV,) + w_in_rows.shape, BF16), pltpu.VMEM((D_IN, D), F32),
                        pltpu.VMEM((N_DEV, 1, D), F32), pltpu.VMEM((N_DEV, 1, cols), F32), pltpu.VMEM((N_DEV, 1, cols), F32),
                        pltpu.SemaphoreType.DMA((N_GATHER_SEMS,)), pltpu.SemaphoreType.DMA((N_GATHER_SEMS,)),
                        pltpu.SemaphoreType.DMA((1,)),
                        pltpu.SemaphoreType.DMA((14,)), pltpu.SemaphoreType.DMA((14,))],
        compiler_params=_params(),
    )(c, w_in_rows, w_ada)


def _inproj_forward(x, mod, w_main, w_f, b_main, b_f, tile):
    seq = x.shape[0]
    nt = seq // tile

    def body(x_ref, mod_ref, w_ref, wf_ref, b_ref, bf_ref,
             qp_ref, kp_ref, vp_ref, f_ref, p_ref, ga_ref, gp_ref, u_ref, carry_ref):
        i = pl.program_id(0)

        @pl.when(i == 0)
        def _():
            carry_ref[...] = jnp.zeros_like(carry_ref)

        u = x_ref[...] * mod_ref[0:1, :] + mod_ref[1:2, :]
        ub = u.astype(BF16)
        u_ref[...] = ub

        f = _dot_nt(ub, wf_ref[...]) + bf_ref[...]
        f_ref[...] = f
        lane = _lanes((tile, LANE))
        log_f = jnp.where(lane < N_HEADS, jnp.minimum(f, 0.0) - jnp.log(1.0 + jnp.exp(-jnp.abs(f))), 0.0)
        row = lax.broadcasted_iota(jnp.int32, (tile, tile), 0)
        col = lax.broadcasted_iota(jnp.int32, (tile, tile), 1)
        tri = (row >= col).astype(BF16)
        cum = _dot3(tri, log_f) + carry_ref[0:1, :]
        carry_ref[0:1, :] = cum[tile - 1:tile, :]
        cq = [part.astype(F32) for part in _split3(cum)]
        ck = [part.astype(F32) for part in _split3(-cum)]

        def proj(chunk):
            cols = pl.ds(chunk * COL_CHUNK, COL_CHUNK)
            return _dot_nt(ub, w_ref[cols, :]) + b_ref[:, cols]

        def head_tiles(r):
            for pair in range(N_HEADS // 2):
                both = r[:, pair * LANE:(pair + 1) * LANE]
                yield 2 * pair, both
                yield 2 * pair + 1, pltpu.roll(both, HEAD_DIM, 1)

        for h, val in head_tiles(proj(0)):
            extra = jnp.where((lane >= L_CK) & (lane < L_CK + 3), 1.0, 0.0)
            extra = _place3(lane, L_CQ, [part[:, h:h + 1] for part in cq], extra)
            qp_ref[h] = jnp.where(lane < HEAD_DIM, val * Q_SCALE, extra).astype(BF16)
        for h, val in head_tiles(proj(1)):
            ones = ((lane >= L_CQ) & (lane < L_CQ + 3)) | ((lane >= L_LSE) & (lane < L_LSE + 3))
            extra = _place3(lane, L_CK, [part[:, h:h + 1] for part in ck], jnp.where(ones, 1.0, 0.0))
            kp_ref[h] = jnp.where(lane < HEAD_DIM, val, extra).astype(BF16)
        for h, val in head_tiles(proj(2)):
            extra = jnp.where((lane >= HEAD_DIM) & (lane < HEAD_DIM + 3), -1.0, 0.0)
            vp_ref[h] = jnp.where(lane < HEAD_DIM, val, extra).astype(BF16)
        p_ref[...] = proj(3)
        ga_ref[...] = proj(4)
        gp_ref[...] = proj(5)

    head_block = pl.BlockSpec((N_HEADS, tile, LANE), lambda i: (0, i, 0))
    tok = lambda width: pl.BlockSpec((tile, width), lambda i: (i, 0))
    whole = lambda a: pl.BlockSpec(a.shape, lambda i: (0,) * a.ndim)
    padded = jax.ShapeDtypeStruct((N_HEADS, seq, LANE), BF16)
    half = jax.ShapeDtypeStruct((seq, D_ATT), F32)
    return pl.pallas_call(
        body, name="inproj_forward", grid=(nt,),
        in_specs=[tok(D), whole(mod), whole(w_main), whole(w_f), whole(b_main), whole(b_f)],
        out_specs=[head_block, head_block, head_block, tok(LANE), tok(D_POOL), tok(D_ATT), tok(D_POOL),
                   tok(D)],
        out_shape=[padded, padded, padded, jax.ShapeDtypeStruct((seq, LANE), F32), half, half, half,
                   jax.ShapeDtypeStruct((seq, D), BF16)],
        scratch_shapes=[pltpu.VMEM((8, LANE), F32)],
        compiler_params=_params(("arbitrary",)),
    )(x, mod, w_main, w_f, b_main, b_f)


def _attention_forward(qp, kp, vp, w_out, tile):
    seq = qp.shape[1]
    nb = seq // tile
    steps = (N_HEADS // 2) * nb

    def body(q_ref, k_ref, v_ref, wo_ref, att_ref, q2t_ref, wo_all_ref, s_a, s_b, m_ref, acc_ref,
             send_sems, recv_sems, local_sem):
        step = pl.program_id(0) * nb + pl.program_id(1)
        start, relay_near, relay_far, finish = _gather_stages(wo_ref, wo_all_ref, send_sems, recv_sems, local_sem.at[0])
        pl.when(step == 0)(start)
        pl.when(step == steps // 4)(relay_near)
        pl.when(step == (3 * steps) // 4)(relay_far)

        i = pl.program_id(1)
        sub = lax.broadcasted_iota(jnp.int32, (LANE, tile), 0)
        row = lax.broadcasted_iota(jnp.int32, (tile, tile), 0)
        col = lax.broadcasted_iota(jnp.int32, (tile, tile), 1)
        q = [q_ref[0], q_ref[1]]

        def scores(buf, kb):
            rows = pl.ds(pl.multiple_of(kb * tile, tile), tile)
            for hh in range(2):
                buf[hh] = _dot_nt(k_ref[hh, rows, :], q[hh])

        def absorb(buf, kb, masked):
            rows = pl.ds(pl.multiple_of(kb * tile, tile), tile)
            for hh in range(2):
                m = m_ref[hh, 0:1, :]
                s = buf[hh]
                if masked:
                    s = jnp.where(row <= col, s, -1e30)
                m_new = jnp.maximum(m, jnp.max(s, axis=0, keepdims=True))
                p = jnp.exp(s - m_new).astype(BF16)
                acc_ref[hh] = jnp.exp(m - m_new) * acc_ref[hh] + _dot_tn(v_ref[hh, rows, :], p)
                m_ref[hh, 0:1, :] = m_new

        def two_blocks(j, _):
            scores(s_b, 2 * j + 1)
            absorb(s_a, 2 * j, False)
            scores(s_a, 2 * j + 2)
            absorb(s_b, 2 * j + 1, False)
            return 0

        def last_block():
            absorb(s_a, i, True)

        def last_two_blocks():
            scores(s_b, i)
            absorb(s_a, i - 1, False)
            absorb(s_b, i, True)

        scores(s_a, 0)
        m_ref[...] = jnp.full(m_ref.shape, -1e30, F32)
        acc_ref[...] = jnp.zeros_like(acc_ref)
        lax.fori_loop(0, i // 2, two_blocks, 0)
        lax.cond(i % 2 == 0, last_block, last_two_blocks)
        outs = []
        for hh in range(2):
            m, acc = m_ref[hh, 0:1, :], acc_ref[hh]
            l = -acc[HEAD_DIM:HEAD_DIM + 1, :]
            outs.append((acc / l)[:HEAD_DIM, :])
            neg_lse = [part.astype(F32) for part in _split3(-(m + jnp.log(l)))]
            q2t_ref[hh] = _place3(sub, L_LSE, neg_lse, q[hh].astype(F32).T).astype(BF16)
        att_ref[...] = jnp.concatenate(outs, axis=0).T
        pl.when(step == steps - 1)(finish)

    pair = pl.BlockSpec((2, tile, LANE), lambda hp, i: (hp, i, 0))
    full = pl.BlockSpec((2, seq, LANE), lambda hp, i: (hp, 0, 0))
    return pl.pallas_call(
        body, name="attention_forward", grid=(N_HEADS // 2, nb),
        in_specs=[pair, full, full, ANY],
        out_specs=[pl.BlockSpec((tile, LANE), lambda hp, i: (i, hp)),
                   pl.BlockSpec((2, LANE, tile), lambda hp, i: (hp, 0, i)), ANY],
        out_shape=[jax.ShapeDtypeStruct((seq, D_ATT), F32),
                   jax.ShapeDtypeStruct((N_HEADS, LANE, seq), BF16),
                   jax.ShapeDtypeStruct((N_DEV,) + w_out.shape, w_out.dtype)],
        scratch_shapes=[pltpu.VMEM((2, tile, tile), F32), pltpu.VMEM((2, tile, tile), F32),
                        pltpu.VMEM((2, 8, tile), F32), pltpu.VMEM((2, LANE, tile), F32),
                        pltpu.SemaphoreType.DMA((N_GATHER_SEMS,)), pltpu.SemaphoreType.DMA((N_GATHER_SEMS,)),
                        pltpu.SemaphoreType.DMA((1,))],
        compiler_params=_params(("arbitrary", "arbitrary")),
    )(qp, kp, vp, w_out)


def _window_sum(x, halo, window, transposed):
    tile = x.shape[0]

    def split_cat(a):
        hi = a.astype(BF16)
        return jnp.concatenate([hi, (a - hi.astype(F32)).astype(BF16)], axis=1)

    def fold(r):
        return r[:, :LANE] + r[:, LANE:]

    r = lax.broadcasted_iota(jnp.int32, (tile, tile), 0)
    c = lax.broadcasted_iota(jnp.int32, (tile, tile), 1)
    rh = lax.broadcasted_iota(jnp.int32, (HALO, HALO), 0)
    ch = lax.broadcasted_iota(jnp.int32, (HALO, HALO), 1)
    if not transposed:
        band = (c <= r) & (r - c < window)
        edge = (rh + HALO - ch) < window
    else:
        band = (r <= c) & (c - r < window)
        edge = (HALO + ch - rh) < window
    out = fold(_dot(band.astype(BF16), split_cat(x)))
    reach = fold(_dot(edge.astype(BF16), split_cat(halo)))
    if not transposed:
        return jnp.concatenate([out[:HALO] + reach, out[HALO:]], axis=0)
    return jnp.concatenate([out[:tile - HALO], out[tile - HALO:] + reach], axis=0)


def _silu_parts(g):
    sig = _sigmoid(g)
    return g * sig, sig * (1.0 + g * (1.0 - sig))


def _middle(x, tgt, att, g_att, g_pool, p, vecs, pool_vecs, w_out, w_pool, tile):
    seq = x.shape[0]
    nt = seq // tile
    halo_blocks = tile // HALO

    def body(x_ref, tgt_ref, att_ref, ga_ref, gp_ref, p_ref, ph_ref, vec_ref, pvec_ref, wo_ref, wp_ref,
             dxa_ref, do2_ref, dga_ref, dgp_ref, dpooled_ref, gwo_ref, dwp_ref, dvec_ref, dwo_ref, dpvec_ref):
        i = pl.program_id(0)

        @pl.when(i == 0)
        def _():
            dwo_ref[...] = jnp.zeros_like(dwo_ref)
            dwp_ref[...] = jnp.zeros_like(dwp_ref)
            dvec_ref[...] = jnp.zeros_like(dvec_ref)
            dpvec_ref[...] = jnp.zeros_like(dpvec_ref)

        gate, b_out, ln_g, ln_b = (vec_ref[k:k + 1, :] for k in range(4))
        b_pool, pool_scale = pvec_ref[0:1, :], pvec_ref[1:2, :]
        x = x_ref[...]
        p = p_ref[...]
        p_halo = ph_ref[...] * jnp.where(i > 0, 1.0, 0.0)
        pos = i * tile + lax.broadcasted_iota(jnp.int32, (tile, 1), 0) + 1

        pooled, mixed = [], []
        for g, window in enumerate(POOL_WINDOWS):
            cols = slice(g * GROUP_DIM, (g + 1) * GROUP_DIM)
            wsum = _window_sum(p[:, cols], p_halo[:, cols], window, False)
            count = jnp.minimum(pos, window).astype(F32)
            pooled.append(wsum / count - p[:, cols])
            mixed.append(_dot(pooled[g].astype(BF16), wp_ref[g]) + b_pool[:, cols])
        mixed = jnp.concatenate(mixed, axis=1)
        pool = mixed * pool_scale

        att = att_ref[...]
        g_att, g_pool = ga_ref[...], gp_ref[...]
        silu_a, dsilu_a = _silu_parts(g_att)
        silu_p, dsilu_p = _silu_parts(g_pool)
        y_in = jnp.concatenate([att * silu_a, pool * silu_p], axis=1)
        y = _dot(y_in.astype(BF16), wo_ref[...]) + b_out
        h = ALPHA * x + gate * y
        mu = jnp.mean(h, axis=1, keepdims=True)
        hc = h - mu
        var = jnp.mean(hc * hc, axis=1, keepdims=True)
        rstd = lax.rsqrt(var + LN_EPS)
        yhat = hc * rstd
        diff = yhat * ln_g + ln_b - tgt_ref[...]
        loss_rows = jnp.sum(diff * diff, axis=1, keepdims=True)
        d_out = diff * (1.0 / D)

        d_yhat = d_out * ln_g
        dh = rstd * (d_yhat - jnp.mean(d_yhat, axis=1, keepdims=True)
                     - yhat * jnp.mean(d_yhat * yhat, axis=1, keepdims=True))
        dxa_ref[...] = ALPHA * dh
        dy = dh * gate
        dyb = dy.astype(BF16)
        lane = _lanes((1, D))
        loss_row = jnp.where(lane == 0, (0.5 / D) * jnp.sum(loss_rows, axis=0, keepdims=True), 0.0)
        dvec_ref[5:6, :] += jnp.sum(dh * y, axis=0, keepdims=True)
        dvec_ref[0:1, :] += jnp.sum(dy, axis=0, keepdims=True)
        dvec_ref[1:2, :] += jnp.sum(d_out * yhat, axis=0, keepdims=True)
        dvec_ref[2:3, :] += jnp.sum(d_out, axis=0, keepdims=True)
        dvec_ref[4:5, :] += loss_row

        dwo_ref[...] += _dot(y_in.T.astype(BF16), dyb)
        d_yin = _dot_nt(dyb, wo_ref[...])
        d_a, d_pl = d_yin[:, :D_ATT], d_yin[:, D_ATT:]
        d_att = d_a * silu_a
        d_att_t = d_att.T
        prod_t = (d_att * att).T
        sub = lax.broadcasted_iota(jnp.int32, (HEAD_DIM, tile), 0)
        for h in range(N_HEADS):
            rows = slice(h * HEAD_DIM, (h + 1) * HEAD_DIM)
            delta = jnp.sum(prod_t[rows], axis=0, keepdims=True)
            extra = _place3(sub, 0, [part.astype(F32) for part in _split3(delta)], 0.0)
            do2_ref[h] = jnp.concatenate([d_att_t[rows], extra], axis=0).astype(BF16)
        dga_ref[...] = d_a * att * dsilu_a
        dgp_ref[...] = d_pl * pool * dsilu_p
        d_pool = d_pl * silu_p
        d_mixed = d_pool * pool_scale
        dpvec_ref[0:1, :] += jnp.sum(d_mixed, axis=0, keepdims=True)
        dpvec_ref[1:2, :] += jnp.sum(d_pool * mixed, axis=0, keepdims=True)
        d_pooled = []
        for g in range(len(POOL_WINDOWS)):
            cols = slice(g * GROUP_DIM, (g + 1) * GROUP_DIM)
            dmb = d_mixed[:, cols].astype(BF16)
            dwp_ref[g] += _dot(pooled[g].T.astype(BF16), dmb)
            d_pooled.append(_dot_nt(dmb, wp_ref[g]))
        dpooled_ref[...] = jnp.concatenate(d_pooled, axis=1)

        @pl.when(i == nt - 1)
        def _():
            gwo_ref[...] = dwo_ref[...].astype(BF16)
            dvec_ref[3:4, :] = jnp.concatenate([dpvec_ref[0:1, :], dpvec_ref[1:2, :]], axis=1)

    tok = lambda width: pl.BlockSpec((tile, width), lambda i: (i, 0))
    whole = lambda a: pl.BlockSpec(a.shape, lambda i: (0,) * a.ndim)
    halo = pl.BlockSpec((HALO, D_POOL), lambda i: (jnp.maximum(i * halo_blocks - 1, 0), 0))
    half = jax.ShapeDtypeStruct((seq, D_ATT), F32)
    outs = [jax.ShapeDtypeStruct((seq, D), F32), jax.ShapeDtypeStruct((N_HEADS, LANE, seq), BF16), half, half, half,
            jax.ShapeDtypeStruct(w_out.shape, BF16), jax.ShapeDtypeStruct(w_pool.shape, F32),
            jax.ShapeDtypeStruct(vecs.shape, F32)]
    return pl.pallas_call(
        body, name="middle", grid=(nt,),
        in_specs=[tok(D), tok(D), tok(D_ATT), tok(D_ATT), tok(D_POOL), tok(D_POOL), halo,
                  whole(vecs), whole(pool_vecs), whole(w_out), whole(w_pool)],
        out_specs=[tok(D), pl.BlockSpec((N_HEADS, LANE, tile), lambda i: (0, 0, i)),
                   tok(D_ATT), tok(D_POOL), tok(D_POOL),
                   whole(w_out), whole(w_pool), whole(vecs)],
        out_shape=outs,
        scratch_shapes=[pltpu.VMEM(w_out.shape, F32), pltpu.VMEM(pool_vecs.shape, F32)],
        compiler_params=_params(("arbitrary",)),
    )(x, tgt, att, g_att, g_pool, p, p, vecs, pool_vecs, w_out, w_pool)


def _attention_backward(q2t, kp, vp, do2t, gw_out, vecs, pool, tile):
    seq = kp.shape[1]
    nb = seq // tile
    last = N_HEADS // 2 - 1

    def body(qt_ref, k_ref, v_ref, dot_ref, gwo_hbm, vecs_hbm, pool_hbm,
             dq_ref, dk_ref, dv_ref, dcum_ref, g_out_ref, vecs_sum_ref, pool_sum_ref,
             dq_acc, dk_acc, dv_acc, gwo_ref, vecs_ref, pool_ref,
             r1, s2, r2, v_sib, v_chip, v_recv, p_sib, p_chip, p_recv, send_sems, recv_sems):
        hp = pl.program_id(0)
        start, middle, fold, finish = _reduce_stages(
            [gwo_ref], [g_out_ref], [r1], [s2], [r2],
            [(vecs_ref, vecs_sum_ref, v_sib, v_chip, v_recv), (pool_ref, pool_sum_ref, p_sib, p_chip, p_recv)],
            send_sems, recv_sems)

        @pl.when(hp == 0)
        def _():
            pltpu.sync_copy(gwo_hbm, gwo_ref)
            pltpu.sync_copy(vecs_hbm, vecs_ref)
            pltpu.sync_copy(pool_hbm, pool_ref)
            start()

        pl.when(hp == 1)(middle)
        pl.when(hp == 2)(fold)

        row = lax.broadcasted_iota(jnp.int32, (tile, tile), 0)
        col = lax.broadcasted_iota(jnp.int32, (tile, tile), 1)
        dq_acc[...] = jnp.zeros_like(dq_acc)

        def kv_block(kb, _):
            krows = pl.ds(pl.multiple_of(kb * tile, tile), tile)
            k = [k_ref[hh, krows, :] for hh in range(2)]
            v = [v_ref[hh, krows, :] for hh in range(2)]
            k_t = [k[hh].T for hh in range(2)]

            def q_block(qb, masked):
                qcols = pl.ds(pl.multiple_of(qb * tile, tile), tile)
                for hh in range(2):
                    q_t = qt_ref[hh, :, qcols]
                    do_t = dot_ref[hh, :, qcols]
                    s_t = _dot(k[hh], q_t)
                    if masked:
                        s_t = jnp.where(row <= col, s_t, -1e30)
                    p_t = jnp.exp(s_t)
                    ds_t = (p_t * _dot(v[hh], do_t)).astype(BF16)
                    dv_new = _dot_nt(do_t, p_t.astype(BF16))
                    dk_new = _dot_nt(q_t, ds_t)
                    if masked:
                        dv_acc[hh], dk_acc[hh] = dv_new, dk_new
                    else:
                        dv_acc[hh] += dv_new
                        dk_acc[hh] += dk_new
                    dq_acc[hh, :, qcols] += _dot(k_t[hh], ds_t)

            q_block(kb, True)

            def two_later_blocks(j, _):
                q_block(kb + 1 + 2 * j, False)
                q_block(kb + 2 + 2 * j, False)
                return 0

            later = nb - 1 - kb
            lax.fori_loop(0, later // 2, two_later_blocks, 0)
            pl.when(later % 2 == 1)(lambda: q_block(nb - 1, False))
            for hh in range(2):
                dk = dk_acc[hh]
                dk_ref[hh, :, krows] = dk.astype(BF16)
                dv_ref[hh, :, krows] = dv_acc[hh].astype(BF16)
                dcum_ref[hh, :, krows] = -dk[L_CK:L_CK + 1, :]
            return 0

        lax.fori_loop(0, nb, kv_block, 0)
        for hh in range(2):
            dq = dq_acc[hh]
            dcum_ref[hh] += dq[L_CQ:L_CQ + 1, :]
            dq_ref[hh] = (dq * Q_SCALE).astype(BF16)
        pl.when(hp == last)(finish)

    pair = pl.BlockSpec((2, seq, LANE), lambda hp: (hp, 0, 0))
    pair_t = pl.BlockSpec((2, LANE, seq), lambda hp: (hp, 0, 0))
    whole = lambda shape: pl.BlockSpec(shape, lambda hp: (0,) * len(shape))
    grad = jax.ShapeDtypeStruct((N_HEADS, LANE, seq), BF16)
    return pl.pallas_call(
        body, name="attention_backward", grid=(N_HEADS // 2,),
        in_specs=[pair_t, pair, pair, pair_t, ANY, ANY, ANY],
        out_specs=[pair_t, pair_t, pair_t, pl.BlockSpec((2, 1, seq), lambda hp: (hp, 0, 0)),
                   whole(gw_out.shape[1:]), whole(vecs.shape), whole(pool.shape)],
        out_shape=[grad, grad, grad, jax.ShapeDtypeStruct((N_HEADS, 1, seq), F32),
                   jax.ShapeDtypeStruct(gw_out.shape[1:], F32), jax.ShapeDtypeStruct(vecs.shape, F32),
                   jax.ShapeDtypeStruct(pool.shape, F32)],
        scratch_shapes=[pltpu.VMEM((2, LANE, seq), F32), pltpu.VMEM((2, LANE, tile), F32),
                        pltpu.VMEM((2, LANE, tile), F32), pltpu.VMEM(gw_out.shape, BF16),
                        pltpu.VMEM(vecs.shape, F32), pltpu.VMEM(pool.shape, F32)]
        + _reduce_scratch(gw_out, [vecs, pool]),
        compiler_params=_params(("arbitrary",)),
    )(q2t, kp, vp, do2t, gw_out, vecs, pool)


def _inproj_backward(dqp, dkp, dvp, d_cum, f, d_pooled, d_ga, d_gp, x, dxa, u, mod, w_main, w_f, tile):
    seq = x.shape[0]
    nt = seq // tile
    halo_blocks = tile // HALO

    def body(dq_ref, dk_ref, dv_ref, dcum_ref, f_ref, dpo_ref, dph_ref, dga_ref, dgp_ref, x_ref, dxa_ref, u_ref,
             mod_ref, w_ref, wf_ref,
             dx_ref, dproj_ref, dwf_ref, db_ref, dbf_ref, dmod_ref, carry_ref):
        step = pl.program_id(0)
        i = nt - 1 - step

        @pl.when(step == 0)
        def _():
            carry_ref[...] = jnp.zeros_like(carry_ref)
            dwf_ref[...] = jnp.zeros_like(dwf_ref)
            db_ref[...] = jnp.zeros_like(db_ref)
            dbf_ref[...] = jnp.zeros_like(dbf_ref)
            dmod_ref[...] = jnp.zeros_like(dmod_ref)

        ones = jnp.ones((8, tile), BF16)

        def emit(chunk, val):
            cols = pl.ds(chunk * COL_CHUNK, COL_CHUNK)
            db_ref[0:1, cols] += jnp.sum(val, axis=0, keepdims=True)
            vb = val.astype(BF16)
            dproj_ref[:, pl.ds((chunk - 3) * COL_CHUNK, COL_CHUNK)] = vb
            return _dot(vb, w_ref[cols, :])

        d_u = jnp.zeros((tile, D), F32)
        for chunk, ref in enumerate((dq_ref, dk_ref, dv_ref)):
            cols = pl.ds(chunk * COL_CHUNK, COL_CHUNK)
            val_t = ref[:, 0:HEAD_DIM, :].reshape(COL_CHUNK, tile)
            db_ref[:, cols] += _dot_nt(ones, val_t)
            d_u += _dot_tn(val_t, w_ref[cols, :])

        d_pooled = dpo_ref[...]
        d_halo = dph_ref[...] * jnp.where(i < nt - 1, 1.0, 0.0)
        pos = i * tile + lax.broadcasted_iota(jnp.int32, (tile, 1), 0) + 1
        d_p = []
        for g, window in enumerate(POOL_WINDOWS):
            cols = slice(g * GROUP_DIM, (g + 1) * GROUP_DIM)
            scaled = d_pooled[:, cols] / jnp.minimum(pos, window).astype(F32)
            d_p.append(_window_sum(scaled, d_halo[:, cols] * (1.0 / window), window, True) - d_pooled[:, cols])
        d_u += emit(3, jnp.concatenate(d_p, axis=1))
        d_u += emit(4, dga_ref[...])
        d_u += emit(5, dgp_ref[...])

        row = lax.broadcasted_iota(jnp.int32, (tile, tile), 0)
        col = lax.broadcasted_iota(jnp.int32, (tile, tile), 1)
        later = (row >= col).astype(BF16)
        d_logf = sum(_dot(part, later) for part in _split3(dcum_ref[:, 0, :])) + carry_ref[:, 0:1]
        carry_ref[:, 0:1] = d_logf[:, 0:1]
        d_f = d_logf * _sigmoid(-f_ref[...].T[0:N_HEADS, :])
        d_f = jnp.concatenate([d_f, jnp.zeros((LANE - N_HEADS, tile), F32)], axis=0)
        dbf_ref[...] += sum(_dot_nt(ones, part) for part in _split3(d_f))
        d_fb = d_f.astype(BF16)
        d_u += _dot_tn(d_fb, wf_ref[...])
        dwf_ref[...] += _dot(d_fb, u_ref[...])

        x = x_ref[...]
        dx_ref[...] = dxa_ref[...] + d_u * mod_ref[0:1, :]
        dmod_ref[0:1, :] += jnp.sum(d_u * x, axis=0, keepdims=True)
        dmod_ref[1:2, :] += jnp.sum(d_u, axis=0, keepdims=True)

    rev = lambda step: nt - 1 - step
    tok = lambda width: pl.BlockSpec((tile, width), lambda s: (rev(s), 0))
    head_block = pl.BlockSpec((N_HEADS, LANE, tile), lambda s: (0, 0, rev(s)))
    whole = lambda a: pl.BlockSpec(a.shape, lambda s: (0,) * a.ndim)
    halo = pl.BlockSpec((HALO, D_POOL), lambda s: (jnp.minimum((rev(s) + 1) * halo_blocks, seq // HALO - 1), 0))
    small = lambda width: jax.ShapeDtypeStruct((8, width), F32)
    n_rest = N_MAIN - OFF_P
    return pl.pallas_call(
        body, name="inproj_backward", grid=(nt,),
        in_specs=[head_block, head_block, head_block, pl.BlockSpec((N_HEADS, 1, tile), lambda s: (0, 0, rev(s))),
                  tok(LANE), tok(D_POOL), halo, tok(D_ATT), tok(D_POOL),
                  tok(D), tok(D), tok(D),
                  whole(mod), whole(w_main), whole(w_f)],
        out_specs=[tok(D), tok(n_rest), pl.BlockSpec((LANE, D), lambda s: (0, 0)),
                   pl.BlockSpec((8, N_MAIN), lambda s: (0, 0)), pl.BlockSpec((8, LANE), lambda s: (0, 0)),
                   pl.BlockSpec((8, D), lambda s: (0, 0))],
        out_shape=[jax.ShapeDtypeStruct((seq, D), F32), jax.ShapeDtypeStruct((seq, n_rest), BF16),
                   jax.ShapeDtypeStruct((LANE, D), F32), small(N_MAIN), small(LANE), small(D)],
        scratch_shapes=[pltpu.VMEM((8, LANE), F32)],
        compiler_params=_params(("arbitrary",)),
    )(dqp, dkp, dvp, d_cum, f, d_pooled, d_pooled, d_ga, d_gp, x, dxa, u, mod, w_main, w_f)


def _weight_grads(dq_t, dk_t, dv_t, dw_f, dproj, u, k_tile):
    seq = u.shape[0]
    nk = seq // k_tile
    rows = N_HEADS * HEAD_DIM

    def body(dq_ref, dk_ref, dv_ref, dwf_ref, dp_ref, u_ref, out_ref, acc_ref):
        k = pl.program_id(0)

        @pl.when(k == 0)
        def _():
            acc_ref[...] = jnp.zeros_like(acc_ref)

        tokens = u_ref[...]
        for j, ref in enumerate((dq_ref, dk_ref, dv_ref)):
            acc_ref[pl.ds(j * rows, rows), :] += _dot(ref[...].reshape(rows, k_tile), tokens)
        for j in range(dproj.shape[1] // COL_CHUNK):
            cols = pl.ds(j * COL_CHUNK, COL_CHUNK)
            acc_ref[pl.ds(F_HI + j * COL_CHUNK, COL_CHUNK), :] += _dot_tn(dp_ref[:, cols], tokens)

        @pl.when(k == nk - 1)
        def _():
            acc_ref[F_LO:F_HI, :] = dwf_ref[0:N_HEADS, :]
            for slot in range(N_DEV):
                out_ref[slot] = acc_ref[slot * shard:(slot + 1) * shard, :].astype(BF16)

    shard = D_IN // N_DEV
    heads = pl.BlockSpec((N_HEADS, HEAD_DIM, k_tile), lambda k: (0, 0, k))
    return pl.pallas_call(
        body, name="weight_grads", grid=(nk,),
        in_specs=[heads, heads, heads, pl.BlockSpec(dw_f.shape, lambda k: (0, 0)),
                  pl.BlockSpec((k_tile, dproj.shape[1]), lambda k: (k, 0)), pl.BlockSpec((k_tile, D), lambda k: (k, 0))],
        out_specs=pl.BlockSpec((N_DEV, shard, D), lambda k: (0, 0, 0)),
        out_shape=jax.ShapeDtypeStruct((N_DEV, shard, D), BF16),
        scratch_shapes=[pltpu.VMEM((D_IN, D), F32)],
        compiler_params=_params(("arbitrary",)),
    )(dq_t, dk_t, dv_t, dw_f, dproj, u)


def _adamw(w, g, m, v):
    m = ADAM_B1 * m + (1.0 - ADAM_B1) * g
    v = ADAM_B2 * v + (1.0 - ADAM_B2) * (g * g)
    m_hat = m / (1.0 - ADAM_B1 ** ADAM_STEP)
    v_hat = v / (1.0 - ADAM_B2 ** ADAM_STEP)
    delta = -ADAM_LR * (m_hat / (jnp.sqrt(v_hat) + ADAM_EPS) + ADAM_WD * w)
    return delta, m, v


def _adamw_call(g, w, m, v, col_tile, name):
    nr = w.shape[1] // col_tile

    def body(gi_ref, w_ref, m_ref, v_ref, g_ref, d_ref, nm_ref, nv_ref):
        g = gi_ref[...]
        g_ref[...] = g
        d_ref[...], nm_ref[...], nv_ref[...] = _adamw(w_ref[...], g, m_ref[...], v_ref[...])

    blk = pl.BlockSpec((w.shape[0], col_tile), lambda r: (0, r))
    shape = jax.ShapeDtypeStruct(w.shape, F32)
    return pl.pallas_call(
        body, name=name, grid=(nr,),
        in_specs=[blk, blk, blk, blk], out_specs=[blk, blk, blk, blk],
        out_shape=[shape, shape, shape, shape],
        compiler_params=_params(("arbitrary",)),
    )(g, w, m, v)


def _ada_adamw(sc_t, d_ada, w, m, v):
    def body(sc_ref, d_ref, w_ref, m_ref, v_ref, g_ref, dl_ref, nm_ref, nv_ref):
        g = sc_ref[:, 0:1] * d_ref[0:1, :]
        for b in range(1, N_DEV):
            g = g + sc_ref[:, b:b + 1] * d_ref[b:b + 1, :]
        g_ref[...] = g
        dl_ref[...], nm_ref[...], nv_ref[...] = _adamw(w_ref[...], g, m_ref[...], v_ref[...])

    row_tile = 256
    blk = pl.BlockSpec((row_tile, w.shape[1]), lambda r: (r, 0))
    shape = jax.ShapeDtypeStruct(w.shape, F32)
    return pl.pallas_call(
        body, name="ada_adamw", grid=(w.shape[0] // row_tile,),
        in_specs=[pl.BlockSpec((row_tile, N_DEV), lambda r: (r, 0)), pl.BlockSpec(d_ada.shape, lambda r: (0, 0)),
                  blk, blk, blk],
        out_specs=[blk, blk, blk, blk], out_shape=[shape, shape, shape, shape],
        compiler_params=_params(("arbitrary",)),
    )(sc_t, d_ada, w, m, v)


F_LO, F_HI = 3 * D_ATT, 3 * D_ATT + N_HEADS


def _split_forget(a, axis):
    idx = lambda lo, hi: tuple(slice(lo, hi) if d == axis else slice(None) for d in range(a.ndim))
    pad = [(0, LANE - N_HEADS) if d == axis else (0, 0) for d in range(a.ndim)]
    return jnp.concatenate([a[idx(0, F_LO)], a[idx(F_HI, D_IN)]], axis=axis), jnp.pad(a[idx(F_LO, F_HI)], pad)


def _join_forget(main, f, axis):
    idx = lambda lo, hi: tuple(slice(lo, hi) if d == axis else slice(None) for d in range(main.ndim))
    return jnp.concatenate([main[idx(0, F_LO)], f[idx(0, N_HEADS)], main[idx(F_LO, N_MAIN)]], axis=axis)


def _adamw_small(grad_rows, row_params, whole_params, summed_params):
    n_row, n_whole, n_sum = len(row_params), len(whole_params), len(summed_params)
    n = n_row + n_whole + n_sum

    def body(g_ref, *refs):
        n_in = 3 * n_row + 4 * (n_whole + n_sum)
        ins, outs = list(refs[:n_in]), refs[n_in:]
        for i in range(n):
            if i < n_row:
                row, lo, hi = row_params[i][:3]
                g = g_ref[row:row + 1, lo:hi]
            elif i < n_row + n_whole:
                g = ins.pop(0)[...]
            else:
                parts = ins.pop(0)
                g = parts[0]
                for k in range(1, N_DEV):
                    g = g + parts[k]
            w, m, v = (ins.pop(0)[...] for _ in range(3))
            outs[4 * i][...] = g
            outs[4 * i + 1][...], outs[4 * i + 2][...], outs[4 * i + 3][...] = _adamw(w, g, m, v)

    shapes = [p[3] for p in row_params] + [p[1] for p in whole_params] + [p[1] for p in summed_params]
    operands = [a for p in row_params for a in p[3:]] + [a for p in whole_params + summed_params for a in p]
    flat = pl.pallas_call(
        body, name="adamw_small",
        out_shape=[jax.ShapeDtypeStruct(w.shape, F32) for w in shapes for _ in range(4)],
        compiler_params=_params(),
    )(grad_rows, *operands)
    return [flat[4 * i:4 * i + 4] for i in range(n)]


def kernel(x, c, w_ada, b_ada, w_in, b_in, w_pool_mix, b_pool_mix, pool_scale, w_out, b_out, ln_g, ln_b, loss_target, m_w_ada, m_b_ada, m_w_in, m_b_in, m_w_pool_mix, m_b_pool_mix, m_pool_scale, m_w_out, m_b_out, m_ln_g, m_ln_b, v_w_ada, v_b_ada, v_w_in, v_b_in, v_w_pool_mix, v_b_pool_mix, v_pool_scale, v_w_out, v_b_out, v_ln_g, v_ln_b):
    seq = x.shape[1]
    tile = min(256, seq)
    attn_tile = min(512, max(128, seq // 4))
    me = _dev_index(*_mesh_pos())
    x2, tgt = x[0], loss_target[0]

    w_main, w_f, sc_all, ada_mine = _gather_and_ada(c, w_in[0].T.astype(BF16), w_ada[0])
    ada = ada_mine.reshape(1, D_ADA) + b_ada
    shift, scale, gate = ada[:, 0:D], ada[:, D:2 * D], ada[:, 2 * D:]
    mod = jnp.concatenate([1.0 + scale, shift, jnp.zeros((6, D), F32)], axis=0)
    b_main, b_f = _split_forget(b_in, 1)

    qp, kp, vp, f, p, g_att, g_pool, u = _inproj_forward(x2, mod, w_main, w_f, b_main, b_f, tile)
    att, q2t, w_out_g = _attention_forward(qp, kp, vp, w_out[0].astype(BF16), attn_tile)

    vecs = jnp.concatenate([gate, b_out, ln_g, ln_b, jnp.zeros((4, D), F32)], axis=0)
    pool_vecs = jnp.concatenate([b_pool_mix.reshape(1, D_POOL), pool_scale, jnp.zeros((6, D_POOL), F32)], axis=0)
    dxa, do2, d_ga, d_gp, d_pooled, gw_out, dw_pool, dvec = _middle(
        x2, tgt, att, g_att, g_pool, p, vecs, pool_vecs, w_out_g.reshape(D, D), w_pool_mix[0].astype(BF16), tile)

    pool_rows = w_pool_mix.shape[1] * GROUP_DIM
    dqp, dkp, dvp, d_cum, g_out, dvec_sum, dw_pool_sum = _attention_backward(
        q2t, kp, vp, do2, gw_out.reshape(N_DEV, D // N_DEV, D), dvec, dw_pool.reshape(pool_rows, GROUP_DIM), attn_tile)
    dx, dproj, dw_f, db_main, db_f, dmod = _inproj_backward(
        dqp, dkp, dvp, d_cum, f, d_pooled, d_ga, d_gp, x2, dxa, u, mod, w_main, w_f, tile)
    gw_in = _weight_grads(dqp, dkp, dvp, dw_f, dproj, u, min(512, seq))
    d_ada = jnp.concatenate([dmod[1:2], dmod[0:1], dvec[5:6]], axis=1)
    g_in_rows, g_b_in, d_ada_all = _reduce_grads(gw_in, _join_forget(db_main[0:1], db_f[0:1], 1), d_ada)

    outs_in = _adamw_call(g_in_rows, w_in[0].T, m_w_in[0].T, v_w_in[0].T, D // 4, "adamw_w_in")
    g_w_in, d_w_in, nm_w_in, nv_w_in = (a.T for a in outs_in)
    flat_pool = lambda a: a.reshape(1, D_POOL)
    pool_2d = lambda a: a.reshape(pool_rows, GROUP_DIM)
    rows = _adamw_small(
        dvec_sum,
        [(0, 0, D, b_out, m_b_out, v_b_out), (1, 0, D, ln_g, m_ln_g, v_ln_g), (2, 0, D, ln_b, m_ln_b, v_ln_b),
         (3, 0, D_POOL, flat_pool(b_pool_mix), flat_pool(m_b_pool_mix), flat_pool(v_b_pool_mix)),
         (3, D_POOL, 2 * D_POOL, pool_scale, m_pool_scale, v_pool_scale)],
        [(g_out, w_out[0], m_w_out[0], v_w_out[0]),
         (dw_pool_sum, pool_2d(w_pool_mix), pool_2d(m_w_pool_mix), pool_2d(v_w_pool_mix)),
         (g_b_in, b_in, m_b_in, v_b_in)],
        [(d_ada_all, b_ada, m_b_ada, v_b_ada)])
    small = {"b_out": rows[0], "ln_g": rows[1], "ln_b": rows[2],
             "b_pool": [a.reshape(b_pool_mix.shape) for a in rows[3]], "pool_scale": rows[4],
             "w_pool": [a.reshape(w_pool_mix.shape) for a in rows[6]], "b_in": rows[7]}
    g_s, d_s, nm_s, nv_s = ({k: r[j] for k, r in small.items()} for j in range(4))
    g_w_out, d_w_out, nm_w_out, nv_w_out = rows[5]
    g_b_ada, d_b_ada, nm_b_ada, nv_b_ada = rows[8]
    loss = dvec_sum[4, 0]

    d_ada_local = lax.dynamic_slice_in_dim(d_ada_all.reshape(N_DEV, D_ADA), me * (D_ADA // N_DEV), D_ADA // N_DEV, axis=1)
    g_w_ada, d_w_ada, nm_w_ada, nv_w_ada = _ada_adamw(sc_all.T, d_ada_local, w_ada[0], m_w_ada[0], v_w_ada[0])

    def ordered(w_ada_, b_ada_, w_in_, w_out_, s):
        return (w_ada_[None], b_ada_, w_in_[None], s["b_in"], s["w_pool"], s["b_pool"], s["pool_scale"],
                w_out_[None], s["b_out"], s["ln_g"], s["ln_b"])

    return (loss, dx[None],
            *ordered(g_w_ada, g_b_ada, g_w_in, g_w_out, g_s),
            *ordered(d_w_ada, d_b_ada, d_w_in, d_w_out, d_s),
            *ordered(nm_w_ada, nm_b_ada, nm_w_in, nm_w_out, nm_s),
            *ordered(nv_w_ada, nv_b_ada, nv_w_in, nv_w_out, nv_s))
```

```python
import jax
import jax.numpy as jnp
from jax import lax
from jax.experimental import pallas as pl
from jax.experimental.pallas import tpu as pltpu

F32 = jnp.float32
BF16 = jnp.bfloat16

N_DEV = 8
D = 1024
N_HEADS = 8
HEAD_DIM = 64
D_ATT = 512
D_POOL = 512
POOL_WINDOWS = (2, 4, 8, 16)
GROUP_DIM = 128
HALO = 16
LANE = 128
D_IN = 3080
D_ADA = 3072
N_MAIN = 3072
OFF_P = 1536
COL_CHUNK = 512
Q_SCALE = 0.125
LN_EPS = 1e-5
ALPHA = 2.0 ** 0.25
L_CQ, L_CK, L_LSE = 64, 67, 70

ADAM_LR, ADAM_B1, ADAM_B2, ADAM_EPS, ADAM_WD, ADAM_STEP = 0.001, 0.9, 0.999, 1e-08, 0.01, 10
VMEM_LIMIT = 56 * 1024 * 1024

MESH = pl.DeviceIdType.MESH
ANY = pl.BlockSpec(memory_space=pl.ANY)


def _params(sem=None, vmem=VMEM_LIMIT):
    return pltpu.CompilerParams(dimension_semantics=sem, vmem_limit_bytes=vmem)


def _split3(a):
    hi = a.astype(BF16)
    r = a - hi.astype(F32)
    mid = r.astype(BF16)
    lo = (r - mid.astype(F32)).astype(BF16)
    return hi, mid, lo


def _dot(a, b):
    return jnp.dot(a, b, preferred_element_type=F32)


def _dot_nt(a, b):
    return lax.dot_general(a, b, (((1,), (1,)), ((), ())), preferred_element_type=F32)


def _dot_tn(a, b):
    return lax.dot_general(a, b, (((0,), (0,)), ((), ())), preferred_element_type=F32)


def _dot3(m01, a):
    hi, mid, lo = _split3(a)
    return _dot(m01, hi) + _dot(m01, mid) + _dot(m01, lo)


def _sigmoid(z):
    return 1.0 / (1.0 + jnp.exp(-z))


def _lanes(shape):
    return lax.broadcasted_iota(jnp.int32, shape, len(shape) - 1)


def _place3(lane, base, parts, other):
    out = other
    for j in range(3):
        out = jnp.where(lane == base + j, parts[j], out)
    return out


def _mesh_pos():
    return lax.axis_index("x"), lax.axis_index("y"), lax.axis_index("c")


def _dev_index(px, py, pc):
    return 4 * px + 2 * py + pc


N_GATHER_SEMS = 9


def _gather_stages(src_ref, out_ref, send_sems, recv_sems, local_sem):
    x, y, c = _mesh_pos()
    me, sibling = (x, y, c), (x, y, 1 - c)
    nbr_x, nbr_y, diag = (1 - x, y), (x, 1 - y), (1 - x, 1 - y)
    half = out_ref.shape[-1] // 2
    left, right = pl.ds(0, half), pl.ds(half, half)

    def copy(k, block, to, cols=None, src=None):
        slot = out_ref.at[_dev_index(*block)]
        if cols is not None:
            slot = slot.at[:, cols]
        return pltpu.make_async_remote_copy(
            src_ref=slot if src is None else src, dst_ref=slot, send_sem=send_sems.at[k], recv_sem=recv_sems.at[k],
            device_id=to, device_id_type=MESH)

    mine = pltpu.make_async_copy(src_ref, out_ref.at[_dev_index(*me)], local_sem)
    first = [copy(0, me, sibling, src=src_ref), copy(1, me, (*nbr_x, c), src=src_ref), copy(2, me, (*nbr_y, c), src=src_ref)]
    relay = [(1, nbr_x, None, nbr_x), (2, nbr_y, None, nbr_y), (3, diag, left, nbr_y), (4, diag, right, nbr_x)]
    onward = [copy(3, (*nbr_x, c), (*nbr_y, c), cols=left), copy(4, (*nbr_y, c), (*nbr_x, c), cols=right)]
    passed = [copy(4 + k, (*block, c), sibling, cols=cols) for k, block, cols, _ in relay]

    def start():
        mine.start()
        for cp in first:
            cp.start()

    def relay_stage(first_item):
        def run():
            for j in (first_item, first_item + 1):
                k, block, cols, frm = relay[j]
                copy(k, (*block, c), (*frm, c), cols=cols).wait_recv()
                if j < 2:
                    onward[j].start()
                passed[j].start()
        return run

    def finish():
        copy(0, sibling, me).wait_recv()
        for k, block, cols, _ in relay:
            copy(4 + k, (*block, 1 - c), me, cols=cols).wait_recv()
        for cp in first + onward + passed:
            cp.wait_send()
        mine.wait()

    return start, relay_stage(0), relay_stage(2), finish


N_REDUCE_SEMS = 10
N_SMALL_SEMS = 4
N_ROWS_SEMS = 7


def _reduce_stages(ins, gs, r1, s2, r2, smalls, send_sems, recv_sems, rows=None, own=None):
    n = len(ins)
    x, y, c = _mesh_pos()
    me = _dev_index(x, y, c)
    sibling = (x, y, 1 - c)
    chips = [(x, y), (1 - x, y), (x, 1 - y), (1 - x, 1 - y)]
    peers = []
    for p in range(1, N_DEV):
        px, py, pc = (p >> 2) & 1, (p >> 1) & 1, p & 1
        peers.append((1 - x if px else x, 1 - y if py else y, 1 - c if pc else c))
    base_small = N_REDUCE_SEMS * n

    def remote(src, dst, k, to):
        return pltpu.make_async_remote_copy(src_ref=src, dst_ref=dst, send_sem=send_sems.at[k],
                                            recv_sem=recv_sems.at[k], device_id=to, device_id_type=MESH)

    def level1(a, q):
        return remote(ins[a].at[_dev_index(*chips[q], 1 - c)], r1[a].at[q], N_REDUCE_SEMS * a + q, sibling)

    def level2(a, k):
        half = ins[a].shape[-1] // 2
        left, right = pl.ds(0, half), pl.ds(half, half)
        nbr_x, nbr_y = (*chips[1], c), (*chips[2], c)
        src_slot, dst_slot, cols, to = [(0, 0, left, nbr_x), (1, 1, right, nbr_y), (2, 2, left, nbr_x),
                                        (2, 2, right, nbr_y), (0, 0, right, nbr_x), (1, 1, left, nbr_y)][k]
        return remote(s2[a].at[src_slot, :, cols], r2[a].at[dst_slot, :, cols], N_REDUCE_SEMS * a + 4 + k, to)

    to_sibling = [remote(sm[0], sm[2], base_small + 4 * i, sibling) for i, sm in enumerate(smalls)]
    to_chips = [[remote(sm[3], sm[4].at[j], base_small + 4 * i + 1 + j, (*chips[j + 1], c)) for j in range(3)]
                for i, sm in enumerate(smalls)]
    if rows is not None:
        rows_ref, land_ref, all_ref = rows
        base_rows = base_small + 4 * len(smalls)
        row_sends = [remote(rows_ref, land_ref.at[me], base_rows + k, to) for k, to in enumerate(peers)]

    def mine(a, q):
        buf, sems = own[a]
        return pltpu.make_async_copy(ins[a].at[_dev_index(*chips[q], c)], buf.at[q], sems.at[q])

    def start():
        for a in range(n):
            for q in range(4):
                level1(a, q).start()
            if own is not None:
                for q in (1, 2, 3, 0):
                    mine(a, q).start()
        for cp in to_sibling:
            cp.start()
        if rows is not None:
            for cp in row_sends:
                cp.start()
            land_ref[me] = rows_ref[...]

    def middle():
        for a in range(n):
            for q in (1, 2, 3, 0):
                level1(a, q).wait_recv()
                if own is None:
                    kept = ins[a][_dev_index(*chips[q], c)]
                else:
                    mine(a, q).wait()
                    kept = own[a][0][q]
                pair = kept.astype(F32) + r1[a][q].astype(F32)
                if q == 0:
                    gs[a][...] = pair
                else:
                    s2[a][q - 1] = pair.astype(BF16)
                    for k in ((0,), (1,), (2, 3))[q - 1]:
                        level2(a, k).start()
        for i, (small_ref, _, sm_sib, sm_chip, _) in enumerate(smalls):
            to_sibling[i].wait_recv()
            sm_chip[...] = small_ref[...] + sm_sib[...]
            for cp in to_chips[i]:
                cp.start()

    def fold():
        for a in range(n):
            half = ins[a].shape[-1] // 2
            level2(a, 3).wait_recv()
            s2[a][0, :, half:] = (s2[a][0, :, half:].astype(F32) + r2[a][2, :, half:].astype(F32)).astype(BF16)
            level2(a, 4).start()
            level2(a, 2).wait_recv()
            s2[a][1, :, :half] = (s2[a][1, :, :half].astype(F32) + r2[a][2, :, :half].astype(F32)).astype(BF16)
            level2(a, 5).start()

    def finish():
        for a in range(n):
            for k in (0, 1, 4, 5):
                level2(a, k).wait_recv()
            gs[a][...] = gs[a][...] + r2[a][0].astype(F32) + r2[a][1].astype(F32)
            for q in range(4):
                level1(a, q).wait_send()
            for k in range(6):
                level2(a, k).wait_send()
        for i, (_, total_ref, _, sm_chip, sm_recv) in enumerate(smalls):
            for cp in to_chips[i]:
                cp.wait_recv()
            total = None
            for ax in range(2):
                for ay in range(2):
                    dx, dy = x != ax, y != ay
                    term = jnp.where(dx, jnp.where(dy, sm_recv[2], sm_recv[0]), jnp.where(dy, sm_recv[1], sm_chip[...]))
                    total = term if total is None else total + term
            total_ref[...] = total
            for cp in [to_sibling[i]] + to_chips[i]:
                cp.wait_send()
        if rows is not None:
            for k, frm in enumerate(peers):
                remote(rows_ref, land_ref.at[_dev_index(*frm)], base_rows + k, frm).wait_recv()
            all_ref[...] = land_ref[...]
            for cp in row_sends:
                cp.wait_send()

    return start, middle, fold, finish


def _reduce_scratch(shard, smalls, rows=None):
    out = [pltpu.VMEM((lead,) + shard.shape[1:], BF16) for lead in (4, 3, 3)]
    for small in smalls:
        out += [pltpu.VMEM(small.shape, F32), pltpu.VMEM(small.shape, F32), pltpu.VMEM((3,) + small.shape, F32)]
    n_sems = N_REDUCE_SEMS + N_SMALL_SEMS * len(smalls)
    if rows is not None:
        out.append(pltpu.VMEM((N_DEV,) + rows.shape, F32))
        n_sems += N_ROWS_SEMS
    return out + [pltpu.SemaphoreType.DMA((n_sems,))] * 2


def _reduce_grads(gw_in, small, rows):
    def body(in_ref, small_ref, rows_ref, g_ref, total_ref, rows_all_ref,
             r1, s2, r2, sm_sib, sm_chip, sm_recv, rows_land, send_sems, recv_sems, kept, kept_sems):
        stages = _reduce_stages(
            [in_ref], [g_ref], [r1], [s2], [r2], [(small_ref, total_ref, sm_sib, sm_chip, sm_recv)],
            send_sems, recv_sems, rows=(rows_ref, rows_land, rows_all_ref), own=[(kept, kept_sems)])
        for stage in stages:
            stage()

    vmem = pl.BlockSpec(memory_space=pltpu.VMEM)
    return pl.pallas_call(
        body, name="reduce_grads",
        in_specs=[ANY, vmem, vmem], out_specs=[vmem, vmem, vmem],
        out_shape=[jax.ShapeDtypeStruct(gw_in.shape[1:], F32), jax.ShapeDtypeStruct(small.shape, F32),
                   jax.ShapeDtypeStruct((N_DEV,) + rows.shape, F32)],
        scratch_shapes=_reduce_scratch(gw_in, [small], rows)
        + [pltpu.VMEM((4,) + gw_in.shape[1:], BF16), pltpu.SemaphoreType.DMA((4,))],
        compiler_params=_params(),
    )(gw_in, small, rows)


def _dot3_rhs(a, b):
    a0, a1, a2 = _split3(a)
    b0, b1, b2 = _split3(b)
    return (_dot(a0, b0) + (_dot(a0, b1) + _dot(a1, b0))
            + (_dot(a0, b2) + _dot(a1, b1) + _dot(a2, b0)))


def _gather_and_ada(c, w_in_rows, w_ada):
    cols = w_ada.shape[1]
    shard = w_in_rows.shape[0]

    def body(c_ref, w_ref, wa_ref, w_main_ref, w_f_ref, sc_ref, ada_ref,
             w_all_ref, w_f32, c_land, part, ada_land, send_sems, recv_sems, local_sem, x_send, x_recv):
        x, y, cc = _mesh_pos()
        me = _dev_index(x, y, cc)
        peers = []
        for p in range(1, N_DEV):
            px, py, pc = (p >> 2) & 1, (p >> 1) & 1, p & 1
            peers.append((1 - x if px else x, 1 - y if py else y, 1 - cc if pc else cc))

        def remote(src, dst, k, to):
            return pltpu.make_async_remote_copy(src_ref=src, dst_ref=dst, send_sem=x_send.at[k], recv_sem=x_recv.at[k],
                                                device_id=to, device_id_type=MESH)

        c_sends = [remote(c_ref, c_land.at[me], k, to) for k, to in enumerate(peers)]
        for cp in c_sends:
            cp.start()
        start, relay_near, relay_far, finish = _gather_stages(w_ref, w_all_ref, send_sems, recv_sems, local_sem.at[0])
        start()
        c_land[me] = c_ref[...]
        for k, frm in enumerate(peers):
            remote(c_ref, c_land.at[_dev_index(*frm)], k, frm).wait_recv()
        c_all = jnp.concatenate([c_land[b] for b in range(N_DEV)], axis=0)
        sc = c_all * _sigmoid(c_all)
        sc_ref[...] = sc
        rows = _dot3_rhs(sc, wa_ref[...])
        for b in range(N_DEV):
            part[b] = rows[b:b + 1, :]
        a_sends = [remote(part.at[_dev_index(*to)], ada_land.at[me], 7 + k, to) for k, to in enumerate(peers)]
        for cp in a_sends:
            cp.start()
        ada_land[me] = part[me]
        for k, frm in enumerate(peers):
            remote(part.at[0], ada_land.at[_dev_index(*frm)], 7 + k, frm).wait_recv()
        ada_ref[...] = ada_land[...]

        relay_near()
        relay_far()
        finish()
        for cp in c_sends + a_sends:
            cp.wait_send()

        for slot in range(N_DEV):
            w_f32[slot * shard:(slot + 1) * shard, :] = w_all_ref[slot].astype(F32)
        w_main_ref[0:F_LO, :] = w_f32[0:F_LO, :].astype(BF16)
        w_main_ref[F_LO:N_MAIN, :] = w_f32[F_HI:D_IN, :].astype(BF16)
        w_f_ref[...] = jnp.concatenate(
            [w_f32[F_LO:F_HI, :], jnp.zeros((LANE - N_HEADS, D), F32)], axis=0).astype(BF16)

    vmem = pl.BlockSpec(memory_space=pltpu.VMEM)
    return pl.pallas_call(
        body, name="gather_weights",
        in_specs=[vmem, ANY, vmem], out_specs=[vmem, vmem, vmem, vmem],
        out_shape=[jax.ShapeDtypeStruct((N_MAIN, D), BF16), jax.ShapeDtypeStruct((LANE, D), BF16),
                   jax.ShapeDtypeStruct((N_DEV, D), F32), jax.ShapeDtypeStruct((N_DEV, 1, cols), F32)],
        scratch_shapes=[pltpu.VMEM((N_DEV,) + w_in_rows.shape, BF16), pltpu.VMEM((D_IN, D), F32),
                        pltpu.VMEM((N_DEV, 1, D), F32), pltpu.VMEM((N_DEV, 1, cols), F32), pltpu.VMEM((N_DEV, 1, cols), F32),
                        pltpu.SemaphoreType.DMA((N_GATHER_SEMS,)), pltpu.SemaphoreType.DMA((N_GATHER_SEMS,)),
                        pltpu.SemaphoreType.DMA((1,)),
                        pltpu.SemaphoreType.DMA((14,)), pltpu.SemaphoreType.DMA((14,))],
        compiler_params=_params(),
    )(c, w_in_rows, w_ada)


def _inproj_forward(x, mod, w_main, w_f, b_main, b_f, tile):
    seq = x.shape[0]
    nt = seq // tile

    def body(x_ref, mod_ref, w_ref, wf_ref, b_ref, bf_ref,
             qp_ref, kp_ref, vp_ref, f_ref, p_ref, ga_ref, gp_ref, u_ref, carry_ref):
        i = pl.program_id(0)

        @pl.when(i == 0)
        def _():
            carry_ref[...] = jnp.zeros_like(carry_ref)

        u = x_ref[...] * mod_ref[0:1, :] + mod_ref[1:2, :]
        ub = u.astype(BF16)
        u_ref[...] = ub

        f = _dot_nt(ub, wf_ref[...]) + bf_ref[...]
        f_ref[...] = f
        lane = _lanes((tile, LANE))
        log_f = jnp.where(lane < N_HEADS, jnp.minimum(f, 0.0) - jnp.log(1.0 + jnp.exp(-jnp.abs(f))), 0.0)
        row = lax.broadcasted_iota(jnp.int32, (tile, tile), 0)
        col = lax.broadcasted_iota(jnp.int32, (tile, tile), 1)
        tri = (row >= col).astype(BF16)
        cum = _dot3(tri, log_f) + carry_ref[0:1, :]
        carry_ref[0:1, :] = cum[tile - 1:tile, :]
        cq = [part.astype(F32) for part in _split3(cum)]
        ck = [part.astype(F32) for part in _split3(-cum)]

        def proj(chunk):
            cols = pl.ds(chunk * COL_CHUNK, COL_CHUNK)
            return _dot_nt(ub, w_ref[cols, :]) + b_ref[:, cols]

        def head_tiles(r):
            for pair in range(N_HEADS // 2):
                both = r[:, pair * LANE:(pair + 1) * LANE]
                yield 2 * pair, both
                yield 2 * pair + 1, pltpu.roll(both, HEAD_DIM, 1)

        for h, val in head_tiles(proj(0)):
            extra = jnp.where((lane >= L_CK) & (lane < L_CK + 3), 1.0, 0.0)
            extra = _place3(lane, L_CQ, [part[:, h:h + 1] for part in cq], extra)
            qp_ref[h] = jnp.where(lane < HEAD_DIM, val * Q_SCALE, extra).astype(BF16)
        for h, val in head_tiles(proj(1)):
            ones = ((lane >= L_CQ) & (lane < L_CQ + 3)) | ((lane >= L_LSE) & (lane < L_LSE + 3))
            extra = _place3(lane, L_CK, [part[:, h:h + 1] for part in ck], jnp.where(ones, 1.0, 0.0))
            kp_ref[h] = jnp.where(lane < HEAD_DIM, val, extra).astype(BF16)
        for h, val in head_tiles(proj(2)):
            extra = jnp.where((lane >= HEAD_DIM) & (lane < HEAD_DIM + 3), -1.0, 0.0)
            vp_ref[h] = jnp.where(lane < HEAD_DIM, val, extra).astype(BF16)
        p_ref[...] = proj(3)
        ga_ref[...] = proj(4)
        gp_ref[...] = proj(5)

    head_block = pl.BlockSpec((N_HEADS, tile, LANE), lambda i: (0, i, 0))
    tok = lambda width: pl.BlockSpec((tile, width), lambda i: (i, 0))
    whole = lambda a: pl.BlockSpec(a.shape, lambda i: (0,) * a.ndim)
    padded = jax.ShapeDtypeStruct((N_HEADS, seq, LANE), BF16)
    half = jax.ShapeDtypeStruct((seq, D_ATT), F32)
    return pl.pallas_call(
        body, name="inproj_forward", grid=(nt,),
        in_specs=[tok(D), whole(mod), whole(w_main), whole(w_f), whole(b_main), whole(b_f)],
        out_specs=[head_block, head_block, head_block, tok(LANE), tok(D_POOL), tok(D_ATT), tok(D_POOL),
                   tok(D)],
        out_shape=[padded, padded, padded, jax.ShapeDtypeStruct((seq, LANE), F32), half, half, half,
                   jax.ShapeDtypeStruct((seq, D), BF16)],
        scratch_shapes=[pltpu.VMEM((8, LANE), F32)],
        compiler_params=_params(("arbitrary",)),
    )(x, mod, w_main, w_f, b_main, b_f)


def _attention_forward(qp, kp, vp, w_out, tile):
    seq = qp.shape[1]
    nb = seq // tile
    steps = (N_HEADS // 2) * nb

    def body(q_ref, k_ref, v_ref, wo_ref, att_ref, q2t_ref, wo_all_ref, s_a, s_b, m_ref, acc_ref,
             send_sems, recv_sems, local_sem):
        step = pl.program_id(0) * nb + pl.program_id(1)
        start, relay_near, relay_far, finish = _gather_stages(wo_ref, wo_all_ref, send_sems, recv_sems, local_sem.at[0])
        pl.when(step == 0)(start)
        pl.when(step == steps // 4)(relay_near)
        pl.when(step == (3 * steps) // 4)(relay_far)

        i = pl.program_id(1)
        sub = lax.broadcasted_iota(jnp.int32, (LANE, tile), 0)
        row = lax.broadcasted_iota(jnp.int32, (tile, tile), 0)
        col = lax.broadcasted_iota(jnp.int32, (tile, tile), 1)
        q = [q_ref[0], q_ref[1]]

        def scores(buf, kb):
            rows = pl.ds(pl.multiple_of(kb * tile, tile), tile)
            for hh in range(2):
                buf[hh] = _dot_nt(k_ref[hh, rows, :], q[hh])

        def absorb(buf, kb, masked):
            rows = pl.ds(pl.multiple_of(kb * tile, tile), tile)
            for hh in range(2):
                m = m_ref[hh, 0:1, :]
                s = buf[hh]
                if masked:
                    s = jnp.where(row <= col, s, -1e30)
                m_new = jnp.maximum(m, jnp.max(s, axis=0, keepdims=True))
                p = jnp.exp(s - m_new).astype(BF16)
                acc_ref[hh] = jnp.exp(m - m_new) * acc_ref[hh] + _dot_tn(v_ref[hh, rows, :], p)
                m_ref[hh, 0:1, :] = m_new

        def two_blocks(j, _):
            scores(s_b, 2 * j + 1)
            absorb(s_a, 2 * j, False)
            scores(s_a, 2 * j + 2)
            absorb(s_b, 2 * j + 1, False)
            return 0

        def last_block():
            absorb(s_a, i, True)

        def last_two_blocks():
            scores(s_b, i)
            absorb(s_a, i - 1, False)
            absorb(s_b, i, True)

        scores(s_a, 0)
        m_ref[...] = jnp.full(m_ref.shape, -1e30, F32)
        acc_ref[...] = jnp.zeros_like(acc_ref)
        lax.fori_loop(0, i // 2, two_blocks, 0)
        lax.cond(i % 2 == 0, last_block, last_two_blocks)
        outs = []
        for hh in range(2):
            m, acc = m_ref[hh, 0:1, :], acc_ref[hh]
            l = -acc[HEAD_DIM:HEAD_DIM + 1, :]
            outs.append((acc / l)[:HEAD_DIM, :])
            neg_lse = [part.astype(F32) for part in _split3(-(m + jnp.log(l)))]
            q2t_ref[hh] = _place3(sub, L_LSE, neg_lse, q[hh].astype(F32).T).astype(BF16)
        att_ref[...] = jnp.concatenate(outs, axis=0).T
        pl.when(step == steps - 1)(finish)

    pair = pl.BlockSpec((2, tile, LANE), lambda hp, i: (hp, i, 0))
    full = pl.BlockSpec((2, seq, LANE), lambda hp, i: (hp, 0, 0))
    return pl.pallas_call(
        body, name="attention_forward", grid=(N_HEADS // 2, nb),
        in_specs=[pair, full, full, ANY],
        out_specs=[pl.BlockSpec((tile, LANE), lambda hp, i: (i, hp)),
                   pl.BlockSpec((2, LANE, tile), lambda hp, i: (hp, 0, i)), ANY],
        out_shape=[jax.ShapeDtypeStruct((seq, D_ATT), F32),
                   jax.ShapeDtypeStruct((N_HEADS, LANE, seq), BF16),
                   jax.ShapeDtypeStruct((N_DEV,) + w_out.shape, w_out.dtype)],
        scratch_shapes=[pltpu.VMEM((2, tile, tile), F32), pltpu.VMEM((2, tile, tile), F32),
                        pltpu.VMEM((2, 8, tile), F32), pltpu.VMEM((2, LANE, tile), F32),
                        pltpu.SemaphoreType.DMA((N_GATHER_SEMS,)), pltpu.SemaphoreType.DMA((N_GATHER_SEMS,)),
                        pltpu.SemaphoreType.DMA((1,))],
        compiler_params=_params(("arbitrary", "arbitrary")),
    )(qp, kp, vp, w_out)


def _window_sum(x, halo, window, transposed):
    tile = x.shape[0]

    def split_cat(a):
        hi = a.astype(BF16)
        return jnp.concatenate([hi, (a - hi.astype(F32)).astype(BF16)], axis=1)

    def fold(r):
        return r[:, :LANE] + r[:, LANE:]

    r = lax.broadcasted_iota(jnp.int32, (tile, tile), 0)
    c = lax.broadcasted_iota(jnp.int32, (tile, tile), 1)
    rh = lax.broadcasted_iota(jnp.int32, (HALO, HALO), 0)
    ch = lax.broadcasted_iota(jnp.int32, (HALO, HALO), 1)
    if not transposed:
        band = (c <= r) & (r - c < window)
        edge = (rh + HALO - ch) < window
    else:
        band = (r <= c) & (c - r < window)
        edge = (HALO + ch - rh) < window
    out = fold(_dot(band.astype(BF16), split_cat(x)))
    reach = fold(_dot(edge.astype(BF16), split_cat(halo)))
    if not transposed:
        return jnp.concatenate([out[:HALO] + reach, out[HALO:]], axis=0)
    return jnp.concatenate([out[:tile - HALO], out[tile - HALO:] + reach], axis=0)


def _silu_parts(g):
    sig = _sigmoid(g)
    return g * sig, sig * (1.0 + g * (1.0 - sig))


def _middle(x, tgt, att, g_att, g_pool, p, vecs, pool_vecs, w_out, w_pool, tile):
    seq = x.shape[0]
    nt = seq // tile
    halo_blocks = tile // HALO

    def body(x_ref, tgt_ref, att_ref, ga_ref, gp_ref, p_ref, ph_ref, vec_ref, pvec_ref, wo_ref, wp_ref,
             dxa_ref, do2_ref, dga_ref, dgp_ref, dpooled_ref, gwo_ref, dwp_ref, dvec_ref, dwo_ref, dpvec_ref):
        i = pl.program_id(0)

        @pl.when(i == 0)
        def _():
            dwo_ref[...] = jnp.zeros_like(dwo_ref)
            dwp_ref[...] = jnp.zeros_like(dwp_ref)
            dvec_ref[...] = jnp.zeros_like(dvec_ref)
            dpvec_ref[...] = jnp.zeros_like(dpvec_ref)

        gate, b_out, ln_g, ln_b = (vec_ref[k:k + 1, :] for k in range(4))
        b_pool, pool_scale = pvec_ref[0:1, :], pvec_ref[1:2, :]
        x = x_ref[...]
        p = p_ref[...]
        p_halo = ph_ref[...] * jnp.where(i > 0, 1.0, 0.0)
        pos = i * tile + lax.broadcasted_iota(jnp.int32, (tile, 1), 0) + 1

        pooled, mixed = [], []
        for g, window in enumerate(POOL_WINDOWS):
            cols = slice(g * GROUP_DIM, (g + 1) * GROUP_DIM)
            wsum = _window_sum(p[:, cols], p_halo[:, cols], window, False)
            count = jnp.minimum(pos, window).astype(F32)
            pooled.append(wsum / count - p[:, cols])
            mixed.append(_dot(pooled[g].astype(BF16), wp_ref[g]) + b_pool[:, cols])
        mixed = jnp.concatenate(mixed, axis=1)
        pool = mixed * pool_scale

        att = att_ref[...]
        g_att, g_pool = ga_ref[...], gp_ref[...]
        silu_a, dsilu_a = _silu_parts(g_att)
        silu_p, dsilu_p = _silu_parts(g_pool)
        y_in = jnp.concatenate([att * silu_a, pool * silu_p], axis=1)
        y = _dot(y_in.astype(BF16), wo_ref[...]) + b_out
        h = ALPHA * x + gate * y
        mu = jnp.mean(h, axis=1, keepdims=True)
        hc = h - mu
        var = jnp.mean(hc * hc, axis=1, keepdims=True)
        rstd = lax.rsqrt(var + LN_EPS)
        yhat = hc * rstd
        diff = yhat * ln_g + ln_b - tgt_ref[...]
        loss_rows = jnp.sum(diff * diff, axis=1, keepdims=True)
        d_out = diff * (1.0 / D)

        d_yhat = d_out * ln_g
        dh = rstd * (d_yhat - jnp.mean(d_yhat, axis=1, keepdims=True)
                     - yhat * jnp.mean(d_yhat * yhat, axis=1, keepdims=True))
        dxa_ref[...] = ALPHA * dh
        dy = dh * gate
        dyb = dy.astype(BF16)
        lane = _lanes((1, D))
        loss_row = jnp.where(lane == 0, (0.5 / D) * jnp.sum(loss_rows, axis=0, keepdims=True), 0.0)
        dvec_ref[5:6, :] += jnp.sum(dh * y, axis=0, keepdims=True)
        dvec_ref[0:1, :] += jnp.sum(dy, axis=0, keepdims=True)
        dvec_ref[1:2, :] += jnp.sum(d_out * yhat, axis=0, keepdims=True)
        dvec_ref[2:3, :] += jnp.sum(d_out, axis=0, keepdims=True)
        dvec_ref[4:5, :] += loss_row

        dwo_ref[...] += _dot(y_in.T.astype(BF16), dyb)
        d_yin = _dot_nt(dyb, wo_ref[...])
        d_a, d_pl = d_yin[:, :D_ATT], d_yin[:, D_ATT:]
        d_att = d_a * silu_a
        d_att_t = d_att.T
        prod_t = (d_att * att).T
        sub = lax.broadcasted_iota(jnp.int32, (HEAD_DIM, tile), 0)
        for h in range(N_HEADS):
            rows = slice(h * HEAD_DIM, (h + 1) * HEAD_DIM)
            delta = jnp.sum(prod_t[rows], axis=0, keepdims=True)
            extra = _place3(sub, 0, [part.astype(F32) for part in _split3(delta)], 0.0)
            do2_ref[h] = jnp.concatenate([d_att_t[rows], extra], axis=0).astype(BF16)
        dga_ref[...] = d_a * att * dsilu_a
        dgp_ref[...] = d_pl * pool * dsilu_p
        d_pool = d_pl * silu_p
        d_mixed = d_pool * pool_scale
        dpvec_ref[0:1, :] += jnp.sum(d_mixed, axis=0, keepdims=True)
        dpvec_ref[1:2, :] += jnp.sum(d_pool * mixed, axis=0, keepdims=True)
        d_pooled = []
        for g in range(len(POOL_WINDOWS)):
            cols = slice(g * GROUP_DIM, (g + 1) * GROUP_DIM)
            dmb = d_mixed[:, cols].astype(BF16)
            dwp_ref[g] += _dot(pooled[g].T.astype(BF16), dmb)
            d_pooled.append(_dot_nt(dmb, wp_ref[g]))
        dpooled_ref[...] = jnp.concatenate(d_pooled, axis=1)

        @pl.when(i == nt - 1)
        def _():
            gwo_ref[...] = dwo_ref[...].astype(BF16)
            dvec_ref[3:4, :] = jnp.concatenate([dpvec_ref[0:1, :], dpvec_ref[1:2, :]], axis=1)

    tok = lambda width: pl.BlockSpec((tile, width), lambda i: (i, 0))
    whole = lambda a: pl.BlockSpec(a.shape, lambda i: (0,) * a.ndim)
    halo = pl.BlockSpec((HALO, D_POOL), lambda i: (jnp.maximum(i * halo_blocks - 1, 0), 0))
    half = jax.ShapeDtypeStruct((seq, D_ATT), F32)
    outs = [jax.ShapeDtypeStruct((seq, D), F32), jax.ShapeDtypeStruct((N_HEADS, LANE, seq), BF16), half, half, half,
            jax.ShapeDtypeStruct(w_out.shape, BF16), jax.ShapeDtypeStruct(w_pool.shape, F32),
            jax.ShapeDtypeStruct(vecs.shape, F32)]
    return pl.pallas_call(
        body, name="middle", grid=(nt,),
        in_specs=[tok(D), tok(D), tok(D_ATT), tok(D_ATT), tok(D_POOL), tok(D_POOL), halo,
                  whole(vecs), whole(pool_vecs), whole(w_out), whole(w_pool)],
        out_specs=[tok(D), pl.BlockSpec((N_HEADS, LANE, tile), lambda i: (0, 0, i)),
                   tok(D_ATT), tok(D_POOL), tok(D_POOL),
                   whole(w_out), whole(w_pool), whole(vecs)],
        out_shape=outs,
        scratch_shapes=[pltpu.VMEM(w_out.shape, F32), pltpu.VMEM(pool_vecs.shape, F32)],
        compiler_params=_params(("arbitrary",)),
    )(x, tgt, att, g_att, g_pool, p, p, vecs, pool_vecs, w_out, w_pool)


def _attention_backward(q2t, kp, vp, do2t, gw_out, vecs, pool, tile):
    seq = kp.shape[1]
    nb = seq // tile
    last = N_HEADS // 2 - 1

    def body(qt_ref, k_ref, v_ref, dot_ref, gwo_hbm, vecs_hbm, pool_hbm,
             dq_ref, dk_ref, dv_ref, dcum_ref, g_out_ref, vecs_sum_ref, pool_sum_ref,
             dq_acc, dk_acc, dv_acc, gwo_ref, vecs_ref, pool_ref,
             r1, s2, r2, v_sib, v_chip, v_recv, p_sib, p_chip, p_recv, send_sems, recv_sems):
        hp = pl.program_id(0)
        start, middle, fold, finish = _reduce_stages(
            [gwo_ref], [g_out_ref], [r1], [s2], [r2],
            [(vecs_ref, vecs_sum_ref, v_sib, v_chip, v_recv), (pool_ref, pool_sum_ref, p_sib, p_chip, p_recv)],
            send_sems, recv_sems)

        @pl.when(hp == 0)
        def _():
            pltpu.sync_copy(gwo_hbm, gwo_ref)
            pltpu.sync_copy(vecs_hbm, vecs_ref)
            pltpu.sync_copy(pool_hbm, pool_ref)
            start()

        pl.when(hp == 1)(middle)
        pl.when(hp == 2)(fold)

        row = lax.broadcasted_iota(jnp.int32, (tile, tile), 0)
        col = lax.broadcasted_iota(jnp.int32, (tile, tile), 1)
        dq_acc[...] = jnp.zeros_like(dq_acc)

        def kv_block(kb, _):
            krows = pl.ds(pl.multiple_of(kb * tile, tile), tile)
            k = [k_ref[hh, krows, :] for hh in range(2)]
            v = [v_ref[hh, krows, :] for hh in range(2)]
            k_t = [k[hh].T for hh in range(2)]

            def q_block(qb, masked):
                qcols = pl.ds(pl.multiple_of(qb * tile, tile), tile)
                for hh in range(2):
                    q_t = qt_ref[hh, :, qcols]
                    do_t = dot_ref[hh, :, qcols]
                    s_t = _dot(k[hh], q_t)
                    if masked:
                        s_t = jnp.where(row <= col, s_t, -1e30)
                    p_t = jnp.exp(s_t)
                    ds_t = (p_t * _dot(v[hh], do_t)).astype(BF16)
                    dv_new = _dot_nt(do_t, p_t.astype(BF16))
                    dk_new = _dot_nt(q_t, ds_t)
                    if masked:
                        dv_acc[hh], dk_acc[hh] = dv_new, dk_new
                    else:
                        dv_acc[hh] += dv_new
                        dk_acc[hh] += dk_new
                    dq_acc[hh, :, qcols] += _dot(k_t[hh], ds_t)

            q_block(kb, True)

            def two_later_blocks(j, _):
                q_block(kb + 1 + 2 * j, False)
                q_block(kb + 2 + 2 * j, False)
                return 0

            later = nb - 1 - kb
            lax.fori_loop(0, later // 2, two_later_blocks, 0)
            pl.when(later % 2 == 1)(lambda: q_block(nb - 1, False))
            for hh in range(2):
                dk = dk_acc[hh]
                dk_ref[hh, :, krows] = dk.astype(BF16)
                dv_ref[hh, :, krows] = dv_acc[hh].astype(BF16)
                dcum_ref[hh, :, krows] = -dk[L_CK:L_CK + 1, :]
            return 0

        lax.fori_loop(0, nb, kv_block, 0)
        for hh in range(2):
            dq = dq_acc[hh]
            dcum_ref[hh] += dq[L_CQ:L_CQ + 1, :]
            dq_ref[hh] = (dq * Q_SCALE).astype(BF16)
        pl.when(hp == last)(finish)

    pair = pl.BlockSpec((2, seq, LANE), lambda hp: (hp, 0, 0))
    pair_t = pl.BlockSpec((2, LANE, seq), lambda hp: (hp, 0, 0))
    whole = lambda shape: pl.BlockSpec(shape, lambda hp: (0,) * len(shape))
    grad = jax.ShapeDtypeStruct((N_HEADS, LANE, seq), BF16)
    return pl.pallas_call(
        body, name="attention_backward", grid=(N_HEADS // 2,),
        in_specs=[pair_t, pair, pair, pair_t, ANY, ANY, ANY],
        out_specs=[pair_t, pair_t, pair_t, pl.BlockSpec((2, 1, seq), lambda hp: (hp, 0, 0)),
                   whole(gw_out.shape[1:]), whole(vecs.shape), whole(pool.shape)],
        out_shape=[grad, grad, grad, jax.ShapeDtypeStruct((N_HEADS, 1, seq), F32),
                   jax.ShapeDtypeStruct(gw_out.shape[1:], F32), jax.ShapeDtypeStruct(vecs.shape, F32),
                   jax.ShapeDtypeStruct(pool.shape, F32)],
        scratch_shapes=[pltpu.VMEM((2, LANE, seq), F32), pltpu.VMEM((2, LANE, tile), F32),
                        pltpu.VMEM((2, LANE, tile), F32), pltpu.VMEM(gw_out.shape, BF16),
                        pltpu.VMEM(vecs.shape, F32), pltpu.VMEM(pool.shape, F32)]
        + _reduce_scratch(gw_out, [vecs, pool]),
        compiler_params=_params(("arbitrary",)),
    )(q2t, kp, vp, do2t, gw_out, vecs, pool)


def _inproj_backward(dqp, dkp, dvp, d_cum, f, d_pooled, d_ga, d_gp, x, dxa, u, mod, w_main, w_f, tile):
    seq = x.shape[0]
    nt = seq // tile
    halo_blocks = tile // HALO

    def body(dq_ref, dk_ref, dv_ref, dcum_ref, f_ref, dpo_ref, dph_ref, dga_ref, dgp_ref, x_ref, dxa_ref, u_ref,
             mod_ref, w_ref, wf_ref,
             dx_ref, dproj_ref, dwf_ref, db_ref, dbf_ref, dmod_ref, carry_ref):
        step = pl.program_id(0)
        i = nt - 1 - step

        @pl.when(step == 0)
        def _():
            carry_ref[...] = jnp.zeros_like(carry_ref)
            dwf_ref[...] = jnp.zeros_like(dwf_ref)
            db_ref[...] = jnp.zeros_like(db_ref)
            dbf_ref[...] = jnp.zeros_like(dbf_ref)
            dmod_ref[...] = jnp.zeros_like(dmod_ref)

        ones = jnp.ones((8, tile), BF16)

        def emit(chunk, val):
            cols = pl.ds(chunk * COL_CHUNK, COL_CHUNK)
            db_ref[0:1, cols] += jnp.sum(val, axis=0, keepdims=True)
            vb = val.astype(BF16)
            dproj_ref[:, pl.ds((chunk - 3) * COL_CHUNK, COL_CHUNK)] = vb
            return _dot(vb, w_ref[cols, :])

        d_u = jnp.zeros((tile, D), F32)
        for chunk, ref in enumerate((dq_ref, dk_ref, dv_ref)):
            cols = pl.ds(chunk * COL_CHUNK, COL_CHUNK)
            val_t = ref[:, 0:HEAD_DIM, :].reshape(COL_CHUNK, tile)
            db_ref[:, cols] += _dot_nt(ones, val_t)
            d_u += _dot_tn(val_t, w_ref[cols, :])

        d_pooled = dpo_ref[...]
        d_halo = dph_ref[...] * jnp.where(i < nt - 1, 1.0, 0.0)
        pos = i * tile + lax.broadcasted_iota(jnp.int32, (tile, 1), 0) + 1
        d_p = []
        for g, window in enumerate(POOL_WINDOWS):
            cols = slice(g * GROUP_DIM, (g + 1) * GROUP_DIM)
            scaled = d_pooled[:, cols] / jnp.minimum(pos, window).astype(F32)
            d_p.append(_window_sum(scaled, d_halo[:, cols] * (1.0 / window), window, True) - d_pooled[:, cols])
        d_u += emit(3, jnp.concatenate(d_p, axis=1))
        d_u += emit(4, dga_ref[...])
        d_u += emit(5, dgp_ref[...])

        row = lax.broadcasted_iota(jnp.int32, (tile, tile), 0)
        col = lax.broadcasted_iota(jnp.int32, (tile, tile), 1)
        later = (row >= col).astype(BF16)
        d_logf = sum(_dot(part, later) for part in _split3(dcum_ref[:, 0, :])) + carry_ref[:, 0:1]
        carry_ref[:, 0:1] = d_logf[:, 0:1]
        d_f = d_logf * _sigmoid(-f_ref[...].T[0:N_HEADS, :])
        d_f = jnp.concatenate([d_f, jnp.zeros((LANE - N_HEADS, tile), F32)], axis=0)
        dbf_ref[...] += sum(_dot_nt(ones, part) for part in _split3(d_f))
        d_fb = d_f.astype(BF16)
        d_u += _dot_tn(d_fb, wf_ref[...])
        dwf_ref[...] += _dot(d_fb, u_ref[...])

        x = x_ref[...]
        dx_ref[...] = dxa_ref[...] + d_u * mod_ref[0:1, :]
        dmod_ref[0:1, :] += jnp.sum(d_u * x, axis=0, keepdims=True)
        dmod_ref[1:2, :] += jnp.sum(d_u, axis=0, keepdims=True)

    rev = lambda step: nt - 1 - step
    tok = lambda width: pl.BlockSpec((tile, width), lambda s: (rev(s), 0))
    head_block = pl.BlockSpec((N_HEADS, LANE, tile), lambda s: (0, 0, rev(s)))
    whole = lambda a: pl.BlockSpec(a.shape, lambda s: (0,) * a.ndim)
    halo = pl.BlockSpec((HALO, D_POOL), lambda s: (jnp.minimum((rev(s) + 1) * halo_blocks, seq // HALO - 1), 0))
    small = lambda width: jax.ShapeDtypeStruct((8, width), F32)
    n_rest = N_MAIN - OFF_P
    return pl.pallas_call(
        body, name="inproj_backward", grid=(nt,),
        in_specs=[head_block, head_block, head_block, pl.BlockSpec((N_HEADS, 1, tile), lambda s: (0, 0, rev(s))),
                  tok(LANE), tok(D_POOL), halo, tok(D_ATT), tok(D_POOL),
                  tok(D), tok(D), tok(D),
                  whole(mod), whole(w_main), whole(w_f)],
        out_specs=[tok(D), tok(n_rest), pl.BlockSpec((LANE, D), lambda s: (0, 0)),
                   pl.BlockSpec((8, N_MAIN), lambda s: (0, 0)), pl.BlockSpec((8, LANE), lambda s: (0, 0)),
                   pl.BlockSpec((8, D), lambda s: (0, 0))],
        out_shape=[jax.ShapeDtypeStruct((seq, D), F32), jax.ShapeDtypeStruct((seq, n_rest), BF16),
                   jax.ShapeDtypeStruct((LANE, D), F32), small(N_MAIN), small(LANE), small(D)],
        scratch_shapes=[pltpu.VMEM((8, LANE), F32)],
        compiler_params=_params(("arbitrary",)),
    )(dqp, dkp, dvp, d_cum, f, d_pooled, d_pooled, d_ga, d_gp, x, dxa, u, mod, w_main, w_f)


def _weight_grads(dq_t, dk_t, dv_t, dw_f, dproj, u, k_tile):
    seq = u.shape[0]
    nk = seq // k_tile
    rows = N_HEADS * HEAD_DIM

    def body(dq_ref, dk_ref, dv_ref, dwf_ref, dp_ref, u_ref, out_ref, acc_ref):
        k = pl.program_id(0)

        @pl.when(k == 0)
        def _():
            acc_ref[...] = jnp.zeros_like(acc_ref)

        tokens = u_ref[...]
        for j, ref in enumerate((dq_ref, dk_ref, dv_ref)):
            acc_ref[pl.ds(j * rows, rows), :] += _dot(ref[...].reshape(rows, k_tile), tokens)
        for j in range(dproj.shape[1] // COL_CHUNK):
            cols = pl.ds(j * COL_CHUNK, COL_CHUNK)
            acc_ref[pl.ds(F_HI + j * COL_CHUNK, COL_CHUNK), :] += _dot_tn(dp_ref[:, cols], tokens)

        @pl.when(k == nk - 1)
        def _():
            acc_ref[F_LO:F_HI, :] = dwf_ref[0:N_HEADS, :]
            for slot in range(N_DEV):
                out_ref[slot] = acc_ref[slot * shard:(slot + 1) * shard, :].astype(BF16)

    shard = D_IN // N_DEV
    heads = pl.BlockSpec((N_HEADS, HEAD_DIM, k_tile), lambda k: (0, 0, k))
    return pl.pallas_call(
        body, name="weight_grads", grid=(nk,),
        in_specs=[heads, heads, heads, pl.BlockSpec(dw_f.shape, lambda k: (0, 0)),
                  pl.BlockSpec((k_tile, dproj.shape[1]), lambda k: (k, 0)), pl.BlockSpec((k_tile, D), lambda k: (k, 0))],
        out_specs=pl.BlockSpec((N_DEV, shard, D), lambda k: (0, 0, 0)),
        out_shape=jax.ShapeDtypeStruct((N_DEV, shard, D), BF16),
        scratch_shapes=[pltpu.VMEM((D_IN, D), F32)],
        compiler_params=_params(("arbitrary",)),
    )(dq_t, dk_t, dv_t, dw_f, dproj, u)


def _adamw(w, g, m, v):
    m = ADAM_B1 * m + (1.0 - ADAM_B1) * g
    v = ADAM_B2 * v + (1.0 - ADAM_B2) * (g * g)
    m_hat = m / (1.0 - ADAM_B1 ** ADAM_STEP)
    v_hat = v / (1.0 - ADAM_B2 ** ADAM_STEP)
    delta = -ADAM_LR * (m_hat / (jnp.sqrt(v_hat) + ADAM_EPS) + ADAM_WD * w)
    return delta, m, v


SUBLANES = 8


def _adamw_packed(g, w, m, v, name, chunks=4):
    rows, cols = g.shape
    per_row = cols // LANE
    assert cols % LANE == 0 and per_row == SUBLANES and w.shape == (rows * per_row, LANE)
    step = -(-rows // (chunks * SUBLANES)) * SUBLANES
    bounds = [(r0, min(r0 + step, rows)) for r0 in range(0, rows, step)]

    def body(g_hbm, w_hbm, m_hbm, v_hbm, og_hbm, od_hbm, om_hbm, ov_hbm, g_buf, in_buf, out_buf, in_sems, out_sems):
        def copies_in(c):
            r0, r1 = bounds[c]
            packed = slice(r0 * per_row, r1 * per_row)
            return [pltpu.make_async_copy(g_hbm.at[r0:r1], g_buf.at[r0:r1], in_sems.at[c, 0])] + [
                pltpu.make_async_copy(src.at[packed], in_buf.at[i, packed], in_sems.at[c, 1 + i])
                for i, src in enumerate((w_hbm, m_hbm, v_hbm))]

        def copies_out(c):
            r0, r1 = bounds[c]
            packed = slice(r0 * per_row, r1 * per_row)
            return [pltpu.make_async_copy(out_buf.at[i, packed], dst.at[packed], out_sems.at[c, i])
                    for i, dst in enumerate((og_hbm, od_hbm, om_hbm, ov_hbm))]

        for c in range(len(bounds)):
            for cp in copies_in(c):
                cp.start()
        for c, (r0, r1) in enumerate(bounds):
            for cp in copies_in(c):
                cp.wait()
            for j in range(per_row):
                lanes = pl.ds(r0 * per_row + j, r1 - r0, stride=per_row)
                g_part = g_buf[r0:r1, j * LANE:(j + 1) * LANE]
                results = _adamw(in_buf[0, lanes, :], g_part, in_buf[1, lanes, :], in_buf[2, lanes, :])
                for i, val in enumerate((g_part,) + results):
                    out_buf[i, lanes, :] = val
            for cp in copies_out(c):
                cp.start()
        for c in range(len(bounds)):
            for cp in copies_out(c):
                cp.wait()

    shape = jax.ShapeDtypeStruct(w.shape, F32)
    return pl.pallas_call(
        body, name=name,
        in_specs=[ANY] * 4, out_specs=[ANY] * 4, out_shape=[shape] * 4,
        scratch_shapes=[pltpu.VMEM(g.shape, F32), pltpu.VMEM((3,) + w.shape, F32), pltpu.VMEM((4,) + w.shape, F32),
                        pltpu.SemaphoreType.DMA((len(bounds), 4)), pltpu.SemaphoreType.DMA((len(bounds), 4))],
        compiler_params=_params(),
    )(g, w, m, v)


def _ada_adamw(sc_t, d_ada, w, m, v):
    def body(sc_ref, d_ref, w_ref, m_ref, v_ref, g_ref, dl_ref, nm_ref, nv_ref):
        g = sc_ref[:, 0:1] * d_ref[0:1, :]
        for b in range(1, N_DEV):
            g = g + sc_ref[:, b:b + 1] * d_ref[b:b + 1, :]
        g_ref[...] = g
        dl_ref[...], nm_ref[...], nv_ref[...] = _adamw(w_ref[...], g, m_ref[...], v_ref[...])

    row_tile = 256
    blk = pl.BlockSpec((row_tile, w.shape[1]), lambda r: (r, 0))
    shape = jax.ShapeDtypeStruct(w.shape, F32)
    return pl.pallas_call(
        body, name="ada_adamw", grid=(w.shape[0] // row_tile,),
        in_specs=[pl.BlockSpec((row_tile, N_DEV), lambda r: (r, 0)), pl.BlockSpec(d_ada.shape, lambda r: (0, 0)),
                  blk, blk, blk],
        out_specs=[blk, blk, blk, blk], out_shape=[shape, shape, shape, shape],
        compiler_params=_params(("arbitrary",)),
    )(sc_t, d_ada, w, m, v)


F_LO, F_HI = 3 * D_ATT, 3 * D_ATT + N_HEADS


def _split_forget(a, axis):
    idx = lambda lo, hi: tuple(slice(lo, hi) if d == axis else slice(None) for d in range(a.ndim))
    pad = [(0, LANE - N_HEADS) if d == axis else (0, 0) for d in range(a.ndim)]
    return jnp.concatenate([a[idx(0, F_LO)], a[idx(F_HI, D_IN)]], axis=axis), jnp.pad(a[idx(F_LO, F_HI)], pad)


def _join_forget(main, f, axis):
    idx = lambda lo, hi: tuple(slice(lo, hi) if d == axis else slice(None) for d in range(main.ndim))
    return jnp.concatenate([main[idx(0, F_LO)], f[idx(0, N_HEADS)], main[idx(F_LO, N_MAIN)]], axis=axis)


def _adamw_small(grad_rows, row_params, whole_params, summed_params):
    n_row, n_whole, n_sum = len(row_params), len(whole_params), len(summed_params)
    n = n_row + n_whole + n_sum

    def body(g_ref, *refs):
        n_in = 3 * n_row + 4 * (n_whole + n_sum)
        ins, outs = list(refs[:n_in]), refs[n_in:]
        for i in range(n):
            if i < n_row:
                row, lo, hi = row_params[i][:3]
                g = g_ref[row:row + 1, lo:hi]
            elif i < n_row + n_whole:
                g = ins.pop(0)[...]
            else:
                parts = ins.pop(0)
                g = parts[0]
                for k in range(1, N_DEV):
                    g = g + parts[k]
            w, m, v = (ins.pop(0)[...] for _ in range(3))
            outs[4 * i][...] = g
            outs[4 * i + 1][...], outs[4 * i + 2][...], outs[4 * i + 3][...] = _adamw(w, g, m, v)

    shapes = [p[3] for p in row_params] + [p[1] for p in whole_params] + [p[1] for p in summed_params]
    operands = [a for p in row_params for a in p[3:]] + [a for p in whole_params + summed_params for a in p]
    flat = pl.pallas_call(
        body, name="adamw_small",
        out_shape=[jax.ShapeDtypeStruct(w.shape, F32) for w in shapes for _ in range(4)],
        compiler_params=_params(),
    )(grad_rows, *operands)
    return [flat[4 * i:4 * i + 4] for i in range(n)]


def kernel(x, c, w_ada, b_ada, w_in, b_in, w_pool_mix, b_pool_mix, pool_scale, w_out, b_out, ln_g, ln_b, loss_target, m_w_ada, m_b_ada, m_w_in, m_b_in, m_w_pool_mix, m_b_pool_mix, m_pool_scale, m_w_out, m_b_out, m_ln_g, m_ln_b, v_w_ada, v_b_ada, v_w_in, v_b_in, v_w_pool_mix, v_b_pool_mix, v_pool_scale, v_w_out, v_b_out, v_ln_g, v_ln_b):
    seq = x.shape[1]
    tile = min(256, seq)
    attn_tile = min(512, max(128, seq // 4))
    me = _dev_index(*_mesh_pos())
    x2, tgt = x[0], loss_target[0]

    rows_of = lambda a: jnp.swapaxes(a, 1, 2)[0]
    w_main, w_f, sc_all, ada_mine = _gather_and_ada(c, rows_of(w_in).astype(BF16), w_ada[0])
    ada = ada_mine.reshape(1, D_ADA) + b_ada
    shift, scale, gate = ada[:, 0:D], ada[:, D:2 * D], ada[:, 2 * D:]
    mod = jnp.concatenate([1.0 + scale, shift, jnp.zeros((6, D), F32)], axis=0)
    b_main, b_f = _split_forget(b_in, 1)

    qp, kp, vp, f, p, g_att, g_pool, u = _inproj_forward(x2, mod, w_main, w_f, b_main, b_f, tile)
    att, q2t, w_out_g = _attention_forward(qp, kp, vp, w_out[0].astype(BF16), attn_tile)

    vecs = jnp.concatenate([gate, b_out, ln_g, ln_b, jnp.zeros((4, D), F32)], axis=0)
    pool_vecs = jnp.concatenate([b_pool_mix.reshape(1, D_POOL), pool_scale, jnp.zeros((6, D_POOL), F32)], axis=0)
    dxa, do2, d_ga, d_gp, d_pooled, gw_out, dw_pool, dvec = _middle(
        x2, tgt, att, g_att, g_pool, p, vecs, pool_vecs, w_out_g.reshape(D, D), w_pool_mix[0].astype(BF16), tile)

    pool_rows = w_pool_mix.shape[1] * GROUP_DIM
    dqp, dkp, dvp, d_cum, g_out, dvec_sum, dw_pool_sum = _attention_backward(
        q2t, kp, vp, do2, gw_out.reshape(N_DEV, D // N_DEV, D), dvec, dw_pool.reshape(pool_rows, GROUP_DIM), attn_tile)
    dx, dproj, dw_f, db_main, db_f, dmod = _inproj_backward(
        dqp, dkp, dvp, d_cum, f, d_pooled, d_ga, d_gp, x2, dxa, u, mod, w_main, w_f, tile)
    gw_in = _weight_grads(dqp, dkp, dvp, dw_f, dproj, u, min(512, seq))
    d_ada = jnp.concatenate([dmod[1:2], dmod[0:1], dvec[5:6]], axis=1)
    g_in_rows, g_b_in, d_ada_all = _reduce_grads(gw_in, _join_forget(db_main[0:1], db_f[0:1], 1), d_ada)

    packed = lambda a: jnp.transpose(a.reshape(SUBLANES, LANE, -1), (2, 0, 1)).reshape(-1, LANE)
    outs_in = _adamw_packed(g_in_rows, packed(w_in), packed(m_w_in), packed(v_w_in), "adamw_w_in")
    g_w_in, d_w_in, nm_w_in, nv_w_in = (
        jnp.transpose(a.reshape(-1, SUBLANES, LANE), (1, 2, 0)).reshape(D, -1) for a in outs_in)
    flat_pool = lambda a: a.reshape(1, D_POOL)
    pool_2d = lambda a: a.reshape(pool_rows, GROUP_DIM)
    rows = _adamw_small(
        dvec_sum,
        [(0, 0, D, b_out, m_b_out, v_b_out), (1, 0, D, ln_g, m_ln_g, v_ln_g), (2, 0, D, ln_b, m_ln_b, v_ln_b),
         (3, 0, D_POOL, flat_pool(b_pool_mix), flat_pool(m_b_pool_mix), flat_pool(v_b_pool_mix)),
         (3, D_POOL, 2 * D_POOL, pool_scale, m_pool_scale, v_pool_scale)],
        [(g_out, w_out[0], m_w_out[0], v_w_out[0]),
         (dw_pool_sum, pool_2d(w_pool_mix), pool_2d(m_w_pool_mix), pool_2d(v_w_pool_mix)),
         (g_b_in, b_in, m_b_in, v_b_in)],
        [(d_ada_all, b_ada, m_b_ada, v_b_ada)])
    small = {"b_out": rows[0], "ln_g": rows[1], "ln_b": rows[2],
             "b_pool": [a.reshape(b_pool_mix.shape) for a in rows[3]], "pool_scale": rows[4],
             "w_pool": [a.reshape(w_pool_mix.shape) for a in rows[6]], "b_in": rows[7]}
    g_s, d_s, nm_s, nv_s = ({k: r[j] for k, r in small.items()} for j in range(4))
    g_w_out, d_w_out, nm_w_out, nv_w_out = rows[5]
    g_b_ada, d_b_ada, nm_b_ada, nv_b_ada = rows[8]
    loss = dvec_sum[4, 0]

    d_ada_local = lax.dynamic_slice_in_dim(d_ada_all.reshape(N_DEV, D_ADA), me * (D_ADA // N_DEV), D_ADA // N_DEV, axis=1)
    g_w_ada, d_w_ada, nm_w_ada, nv_w_ada = _ada_adamw(sc_all.T, d_ada_local, w_ada[0], m_w_ada[0], v_w_ada[0])

    def ordered(w_ada_, b_ada_, w_in_, w_out_, s):
        return (w_ada_[None], b_ada_, w_in_[None], s["b_in"], s["w_pool"], s["b_pool"], s["pool_scale"],
                w_out_[None], s["b_out"], s["ln_g"], s["ln_b"])

    return (loss, dx[None],
            *ordered(g_w_ada, g_b_ada, g_w_in, g_w_out, g_s),
            *ordered(d_w_ada, d_b_ada, d_w_in, d_w_out, d_s),
            *ordered(nm_w_ada, nm_b_ada, nm_w_in, nm_w_out, nm_s),
            *ordered(nv_w_ada, nv_b_ada, nv_w_in, nv_w_out, nv_s))
```

```python
import jax
import jax.numpy as jnp
from jax import lax
from jax.experimental import pallas as pl
from jax.experimental.pallas import tpu as pltpu

F32 = jnp.float32
BF16 = jnp.bfloat16

N_DEV = 8
D = 1024
N_HEADS = 8
HEAD_DIM = 64
D_ATT = 512
D_POOL = 512
POOL_WINDOWS = (2, 4, 8, 16)
GROUP_DIM = 128
HALO = 16
LANE = 128
D_IN = 3080
D_ADA = 3072
N_MAIN = 3072
OFF_P = 1536
COL_CHUNK = 512
Q_SCALE = 0.125
LN_EPS = 1e-5
ALPHA = 2.0 ** 0.25
L_CQ, L_CK, L_LSE = 64, 67, 70

ADAM_LR, ADAM_B1, ADAM_B2, ADAM_EPS, ADAM_WD, ADAM_STEP = 0.001, 0.9, 0.999, 1e-08, 0.01, 10
VMEM_LIMIT = 56 * 1024 * 1024

MESH = pl.DeviceIdType.MESH
ANY = pl.BlockSpec(memory_space=pl.ANY)


def _params(sem=None, vmem=VMEM_LIMIT):
    return pltpu.CompilerParams(dimension_semantics=sem, vmem_limit_bytes=vmem)


def _split3(a):
    hi = a.astype(BF16)
    r = a - hi.astype(F32)
    mid = r.astype(BF16)
    lo = (r - mid.astype(F32)).astype(BF16)
    return hi, mid, lo


def _dot(a, b):
    return jnp.dot(a, b, preferred_element_type=F32)


def _dot_nt(a, b):
    return lax.dot_general(a, b, (((1,), (1,)), ((), ())), preferred_element_type=F32)


def _dot_tn(a, b):
    return lax.dot_general(a, b, (((0,), (0,)), ((), ())), preferred_element_type=F32)


def _dot3(m01, a):
    hi, mid, lo = _split3(a)
    return _dot(m01, hi) + _dot(m01, mid) + _dot(m01, lo)


def _sigmoid(z):
    return 1.0 / (1.0 + jnp.exp(-z))


def _lanes(shape):
    return lax.broadcasted_iota(jnp.int32, shape, len(shape) - 1)


def _place3(lane, base, parts, other):
    out = other
    for j in range(3):
        out = jnp.where(lane == base + j, parts[j], out)
    return out


def _mesh_pos():
    return lax.axis_index("x"), lax.axis_index("y"), lax.axis_index("c")


def _dev_index(px, py, pc):
    return 4 * px + 2 * py + pc


N_GATHER_SEMS = 9


def _gather_stages(src_ref, out_ref, send_sems, recv_sems, local_sem):
    x, y, c = _mesh_pos()
    me, sibling = (x, y, c), (x, y, 1 - c)
    nbr_x, nbr_y, diag = (1 - x, y), (x, 1 - y), (1 - x, 1 - y)
    half = out_ref.shape[-1] // 2
    left, right = pl.ds(0, half), pl.ds(half, half)

    def copy(k, block, to, cols=None, src=None):
        slot = out_ref.at[_dev_index(*block)]
        if cols is not None:
            slot = slot.at[:, cols]
        return pltpu.make_async_remote_copy(
            src_ref=slot if src is None else src, dst_ref=slot, send_sem=send_sems.at[k], recv_sem=recv_sems.at[k],
            device_id=to, device_id_type=MESH)

    mine = pltpu.make_async_copy(src_ref, out_ref.at[_dev_index(*me)], local_sem)
    first = [copy(0, me, sibling, src=src_ref), copy(1, me, (*nbr_x, c), src=src_ref), copy(2, me, (*nbr_y, c), src=src_ref)]
    relay = [(1, nbr_x, None, nbr_x), (2, nbr_y, None, nbr_y), (3, diag, left, nbr_y), (4, diag, right, nbr_x)]
    onward = [copy(3, (*nbr_x, c), (*nbr_y, c), cols=left), copy(4, (*nbr_y, c), (*nbr_x, c), cols=right)]
    passed = [copy(4 + k, (*block, c), sibling, cols=cols) for k, block, cols, _ in relay]

    def start():
        mine.start()
        for cp in first:
            cp.start()

    def relay_stage(first_item):
        def run():
            for j in (first_item, first_item + 1):
                k, block, cols, frm = relay[j]
                copy(k, (*block, c), (*frm, c), cols=cols).wait_recv()
                if j < 2:
                    onward[j].start()
                passed[j].start()
        return run

    def finish():
        copy(0, sibling, me).wait_recv()
        for k, block, cols, _ in relay:
            copy(4 + k, (*block, 1 - c), me, cols=cols).wait_recv()
        for cp in first + onward + passed:
            cp.wait_send()
        mine.wait()

    return start, relay_stage(0), relay_stage(2), finish


N_REDUCE_SEMS = 10
N_SMALL_SEMS = 4
N_ROWS_SEMS = 7


def _reduce_stages(ins, gs, r1, s2, r2, smalls, send_sems, recv_sems, rows=None, own=None):
    n = len(ins)
    x, y, c = _mesh_pos()
    me = _dev_index(x, y, c)
    sibling = (x, y, 1 - c)
    chips = [(x, y), (1 - x, y), (x, 1 - y), (1 - x, 1 - y)]
    peers = []
    for p in range(1, N_DEV):
        px, py, pc = (p >> 2) & 1, (p >> 1) & 1, p & 1
        peers.append((1 - x if px else x, 1 - y if py else y, 1 - c if pc else c))
    base_small = N_REDUCE_SEMS * n

    def remote(src, dst, k, to):
        return pltpu.make_async_remote_copy(src_ref=src, dst_ref=dst, send_sem=send_sems.at[k],
                                            recv_sem=recv_sems.at[k], device_id=to, device_id_type=MESH)

    def level1(a, q):
        return remote(ins[a].at[_dev_index(*chips[q], 1 - c)], r1[a].at[q], N_REDUCE_SEMS * a + q, sibling)

    def level2(a, k):
        half = ins[a].shape[-1] // 2
        left, right = pl.ds(0, half), pl.ds(half, half)
        nbr_x, nbr_y = (*chips[1], c), (*chips[2], c)
        src_slot, dst_slot, cols, to = [(0, 0, left, nbr_x), (1, 1, right, nbr_y), (2, 2, left, nbr_x),
                                        (2, 2, right, nbr_y), (0, 0, right, nbr_x), (1, 1, left, nbr_y)][k]
        return remote(s2[a].at[src_slot, :, cols], r2[a].at[dst_slot, :, cols], N_REDUCE_SEMS * a + 4 + k, to)

    to_sibling = [remote(sm[0], sm[2], base_small + 4 * i, sibling) for i, sm in enumerate(smalls)]
    to_chips = [[remote(sm[3], sm[4].at[j], base_small + 4 * i + 1 + j, (*chips[j + 1], c)) for j in range(3)]
                for i, sm in enumerate(smalls)]
    if rows is not None:
        rows_ref, land_ref, all_ref = rows
        base_rows = base_small + 4 * len(smalls)
        row_sends = [remote(rows_ref, land_ref.at[me], base_rows + k, to) for k, to in enumerate(peers)]

    def mine(a, q):
        buf, sems = own[a]
        return pltpu.make_async_copy(ins[a].at[_dev_index(*chips[q], c)], buf.at[q], sems.at[q])

    def start():
        for a in range(n):
            for q in range(4):
                level1(a, q).start()
            if own is not None:
                for q in (1, 2, 3, 0):
                    mine(a, q).start()
        for cp in to_sibling:
            cp.start()
        if rows is not None:
            for cp in row_sends:
                cp.start()
            land_ref[me] = rows_ref[...]

    def middle():
        for a in range(n):
            for q in (1, 2, 3, 0):
                level1(a, q).wait_recv()
                if own is None:
                    kept = ins[a][_dev_index(*chips[q], c)]
                else:
                    mine(a, q).wait()
                    kept = own[a][0][q]
                pair = kept.astype(F32) + r1[a][q].astype(F32)
                if q == 0:
                    gs[a][...] = pair
                else:
                    s2[a][q - 1] = pair.astype(BF16)
                    for k in ((0,), (1,), (2, 3))[q - 1]:
                        level2(a, k).start()
        for i, (small_ref, _, sm_sib, sm_chip, _) in enumerate(smalls):
            to_sibling[i].wait_recv()
            sm_chip[...] = small_ref[...] + sm_sib[...]
            for cp in to_chips[i]:
                cp.start()

    def fold():
        for a in range(n):
            half = ins[a].shape[-1] // 2
            level2(a, 3).wait_recv()
            s2[a][0, :, half:] = (s2[a][0, :, half:].astype(F32) + r2[a][2, :, half:].astype(F32)).astype(BF16)
            level2(a, 4).start()
            level2(a, 2).wait_recv()
            s2[a][1, :, :half] = (s2[a][1, :, :half].astype(F32) + r2[a][2, :, :half].astype(F32)).astype(BF16)
            level2(a, 5).start()

    def finish():
        for a in range(n):
            for k in (0, 1, 4, 5):
                level2(a, k).wait_recv()
            gs[a][...] = gs[a][...] + r2[a][0].astype(F32) + r2[a][1].astype(F32)
            for q in range(4):
                level1(a, q).wait_send()
            for k in range(6):
                level2(a, k).wait_send()
        for i, (_, total_ref, _, sm_chip, sm_recv) in enumerate(smalls):
            for cp in to_chips[i]:
                cp.wait_recv()
            total = None
            for ax in range(2):
                for ay in range(2):
                    dx, dy = x != ax, y != ay
                    term = jnp.where(dx, jnp.where(dy, sm_recv[2], sm_recv[0]), jnp.where(dy, sm_recv[1], sm_chip[...]))
                    total = term if total is None else total + term
            total_ref[...] = total
            for cp in [to_sibling[i]] + to_chips[i]:
                cp.wait_send()
        if rows is not None:
            for k, frm in enumerate(peers):
                remote(rows_ref, land_ref.at[_dev_index(*frm)], base_rows + k, frm).wait_recv()
            all_ref[...] = land_ref[...]
            for cp in row_sends:
                cp.wait_send()

    return start, middle, fold, finish


def _reduce_scratch(shard, smalls, rows=None):
    out = [pltpu.VMEM((lead,) + shard.shape[1:], BF16) for lead in (4, 3, 3)]
    for small in smalls:
        out += [pltpu.VMEM(small.shape, F32), pltpu.VMEM(small.shape, F32), pltpu.VMEM((3,) + small.shape, F32)]
    n_sems = N_REDUCE_SEMS + N_SMALL_SEMS * len(smalls)
    if rows is not None:
        out.append(pltpu.VMEM((N_DEV,) + rows.shape, F32))
        n_sems += N_ROWS_SEMS
    return out + [pltpu.SemaphoreType.DMA((n_sems,))] * 2


def _reduce_grads(gw_in, small, rows):
    def body(in_ref, small_ref, rows_ref, g_ref, total_ref, rows_all_ref,
             r1, s2, r2, sm_sib, sm_chip, sm_recv, rows_land, send_sems, recv_sems, kept, kept_sems):
        stages = _reduce_stages(
            [in_ref], [g_ref], [r1], [s2], [r2], [(small_ref, total_ref, sm_sib, sm_chip, sm_recv)],
            send_sems, recv_sems, rows=(rows_ref, rows_land, rows_all_ref), own=[(kept, kept_sems)])
        for stage in stages:
            stage()

    vmem = pl.BlockSpec(memory_space=pltpu.VMEM)
    return pl.pallas_call(
        body, name="reduce_grads",
        in_specs=[ANY, vmem, vmem], out_specs=[vmem, vmem, vmem],
        out_shape=[jax.ShapeDtypeStruct(gw_in.shape[1:], F32), jax.ShapeDtypeStruct(small.shape, F32),
                   jax.ShapeDtypeStruct((N_DEV,) + rows.shape, F32)],
        scratch_shapes=_reduce_scratch(gw_in, [small], rows)
        + [pltpu.VMEM((4,) + gw_in.shape[1:], BF16), pltpu.SemaphoreType.DMA((4,))],
        compiler_params=_params(),
    )(gw_in, small, rows)


def _dot3_rhs(a, b):
    a0, a1, a2 = _split3(a)
    b0, b1, b2 = _split3(b)
    return (_dot(a0, b0) + (_dot(a0, b1) + _dot(a1, b0))
            + (_dot(a0, b2) + _dot(a1, b1) + _dot(a2, b0)))


def _gather_and_ada(c, w_in_rows, w_ada):
    cols = w_ada.shape[1]
    shard = w_in_rows.shape[0]

    def body(c_ref, w_ref, wa_ref, w_main_ref, w_f_ref, sc_ref, ada_ref,
             w_all_ref, w_f32, c_land, part, ada_land, send_sems, recv_sems, local_sem, x_send, x_recv):
        x, y, cc = _mesh_pos()
        me = _dev_index(x, y, cc)
        peers = []
        for p in range(1, N_DEV):
            px, py, pc = (p >> 2) & 1, (p >> 1) & 1, p & 1
            peers.append((1 - x if px else x, 1 - y if py else y, 1 - cc if pc else cc))

        def remote(src, dst, k, to):
            return pltpu.make_async_remote_copy(src_ref=src, dst_ref=dst, send_sem=x_send.at[k], recv_sem=x_recv.at[k],
                                                device_id=to, device_id_type=MESH)

        c_sends = [remote(c_ref, c_land.at[me], k, to) for k, to in enumerate(peers)]
        for cp in c_sends:
            cp.start()
        start, relay_near, relay_far, finish = _gather_stages(w_ref, w_all_ref, send_sems, recv_sems, local_sem.at[0])
        start()
        c_land[me] = c_ref[...]
        for k, frm in enumerate(peers):
            remote(c_ref, c_land.at[_dev_index(*frm)], k, frm).wait_recv()
        c_all = jnp.concatenate([c_land[b] for b in range(N_DEV)], axis=0)
        sc = c_all * _sigmoid(c_all)
        sc_ref[...] = sc
        rows = _dot3_rhs(sc, wa_ref[...])
        for b in range(N_DEV):
            part[b] = rows[b:b + 1, :]
        a_sends = [remote(part.at[_dev_index(*to)], ada_land.at[me], 7 + k, to) for k, to in enumerate(peers)]
        for cp in a_sends:
            cp.start()
        ada_land[me] = part[me]
        for k, frm in enumerate(peers):
            remote(part.at[0], ada_land.at[_dev_index(*frm)], 7 + k, frm).wait_recv()
        ada_ref[...] = ada_land[...]

        relay_near()
        relay_far()
        finish()
        for cp in c_sends + a_sends:
            cp.wait_send()

        for slot in range(N_DEV):
            w_f32[slot * shard:(slot + 1) * shard, :] = w_all_ref[slot].astype(F32)
        w_main_ref[0:F_LO, :] = w_f32[0:F_LO, :].astype(BF16)
        w_main_ref[F_LO:N_MAIN, :] = w_f32[F_HI:D_IN, :].astype(BF16)
        w_f_ref[...] = jnp.concatenate(
            [w_f32[F_LO:F_HI, :], jnp.zeros((LANE - N_HEADS, D), F32)], axis=0).astype(BF16)

    vmem = pl.BlockSpec(memory_space=pltpu.VMEM)
    return pl.pallas_call(
        body, name="gather_weights",
        in_specs=[vmem, ANY, vmem], out_specs=[vmem, vmem, vmem, vmem],
        out_shape=[jax.ShapeDtypeStruct((N_MAIN, D), BF16), jax.ShapeDtypeStruct((LANE, D), BF16),
                   jax.ShapeDtypeStruct((N_DEV, D), F32), jax.ShapeDtypeStruct((N_DEV, 1, cols), F32)],
        scratch_shapes=[pltpu.VMEM((N_DEV,) + w_in_rows.shape, BF16), pltpu.VMEM((D_IN, D), F32),
                        pltpu.VMEM((N_DEV, 1, D), F32), pltpu.VMEM((N_DEV, 1, cols), F32), pltpu.VMEM((N_DEV, 1, cols), F32),
                        pltpu.SemaphoreType.DMA((N_GATHER_SEMS,)), pltpu.SemaphoreType.DMA((N_GATHER_SEMS,)),
                        pltpu.SemaphoreType.DMA((1,)),
                        pltpu.SemaphoreType.DMA((14,)), pltpu.SemaphoreType.DMA((14,))],
        compiler_params=_params(),
    )(c, w_in_rows, w_ada)


def _inproj_forward(x, mod, w_main, w_f, b_main, b_f, tile):
    seq = x.shape[0]
    nt = seq // tile

    def body(x_ref, mod_ref, w_ref, wf_ref, b_ref, bf_ref,
             qp_ref, kp_ref, vp_ref, f_ref, p_ref, ga_ref, gp_ref, u_ref, carry_ref):
        i = pl.program_id(0)

        @pl.when(i == 0)
        def _():
            carry_ref[...] = jnp.zeros_like(carry_ref)

        u = x_ref[...] * mod_ref[0:1, :] + mod_ref[1:2, :]
        ub = u.astype(BF16)
        u_ref[...] = ub

        f = _dot_nt(ub, wf_ref[...]) + bf_ref[...]
        f_ref[...] = f
        lane = _lanes((tile, LANE))
        log_f = jnp.where(lane < N_HEADS, jnp.minimum(f, 0.0) - jnp.log(1.0 + jnp.exp(-jnp.abs(f))), 0.0)
        row = lax.broadcasted_iota(jnp.int32, (tile, tile), 0)
        col = lax.broadcasted_iota(jnp.int32, (tile, tile), 1)
        tri = (row >= col).astype(BF16)
        cum = _dot3(tri, log_f) + carry_ref[0:1, :]
        carry_ref[0:1, :] = cum[tile - 1:tile, :]
        cq = [part.astype(F32) for part in _split3(cum)]
        ck = [part.astype(F32) for part in _split3(-cum)]

        def proj(chunk):
            cols = pl.ds(chunk * COL_CHUNK, COL_CHUNK)
            return _dot_nt(ub, w_ref[cols, :]) + b_ref[:, cols]

        def head_tiles(r):
            for pair in range(N_HEADS // 2):
                both = r[:, pair * LANE:(pair + 1) * LANE]
                yield 2 * pair, both
                yield 2 * pair + 1, pltpu.roll(both, HEAD_DIM, 1)

        for h, val in head_tiles(proj(0)):
            extra = jnp.where((lane >= L_CK) & (lane < L_CK + 3), 1.0, 0.0)
            extra = _place3(lane, L_CQ, [part[:, h:h + 1] for part in cq], extra)
            qp_ref[h] = jnp.where(lane < HEAD_DIM, val * Q_SCALE, extra).astype(BF16)
        for h, val in head_tiles(proj(1)):
            ones = ((lane >= L_CQ) & (lane < L_CQ + 3)) | ((lane >= L_LSE) & (lane < L_LSE + 3))
            extra = _place3(lane, L_CK, [part[:, h:h + 1] for part in ck], jnp.where(ones, 1.0, 0.0))
            kp_ref[h] = jnp.where(lane < HEAD_DIM, val, extra).astype(BF16)
        for h, val in head_tiles(proj(2)):
            extra = jnp.where((lane >= HEAD_DIM) & (lane < HEAD_DIM + 3), -1.0, 0.0)
            vp_ref[h] = jnp.where(lane < HEAD_DIM, val, extra).astype(BF16)
        p_ref[...] = proj(3)
        ga_ref[...] = proj(4)
        gp_ref[...] = proj(5)

    head_block = pl.BlockSpec((N_HEADS, tile, LANE), lambda i: (0, i, 0))
    tok = lambda width: pl.BlockSpec((tile, width), lambda i: (i, 0))
    whole = lambda a: pl.BlockSpec(a.shape, lambda i: (0,) * a.ndim)
    padded = jax.ShapeDtypeStruct((N_HEADS, seq, LANE), BF16)
    half = jax.ShapeDtypeStruct((seq, D_ATT), F32)
    return pl.pallas_call(
        body, name="inproj_forward", grid=(nt,),
        in_specs=[tok(D), whole(mod), whole(w_main), whole(w_f), whole(b_main), whole(b_f)],
        out_specs=[head_block, head_block, head_block, tok(LANE), tok(D_POOL), tok(D_ATT), tok(D_POOL),
                   tok(D)],
        out_shape=[padded, padded, padded, jax.ShapeDtypeStruct((seq, LANE), F32), half, half, half,
                   jax.ShapeDtypeStruct((seq, D), BF16)],
        scratch_shapes=[pltpu.VMEM((8, LANE), F32)],
        compiler_params=_params(("arbitrary",)),
    )(x, mod, w_main, w_f, b_main, b_f)


def _attention_forward(qp, kp, vp, w_out, tile):
    seq = qp.shape[1]
    nb = seq // tile
    steps = (N_HEADS // 2) * nb

    def body(q_ref, k_ref, v_ref, wo_ref, att_ref, q2t_ref, wo_all_ref, s_a, s_b, m_ref, acc_ref,
             send_sems, recv_sems, local_sem):
        step = pl.program_id(0) * nb + pl.program_id(1)
        start, relay_near, relay_far, finish = _gather_stages(wo_ref, wo_all_ref, send_sems, recv_sems, local_sem.at[0])
        pl.when(step == 0)(start)
        pl.when(step == steps // 4)(relay_near)
        pl.when(step == (3 * steps) // 4)(relay_far)

        i = pl.program_id(1)
        sub = lax.broadcasted_iota(jnp.int32, (LANE, tile), 0)
        row = lax.broadcasted_iota(jnp.int32, (tile, tile), 0)
        col = lax.broadcasted_iota(jnp.int32, (tile, tile), 1)
        q = [q_ref[0], q_ref[1]]

        def scores(buf, kb):
            rows = pl.ds(pl.multiple_of(kb * tile, tile), tile)
            for hh in range(2):
                buf[hh] = _dot_nt(k_ref[hh, rows, :], q[hh])

        def absorb(buf, kb, masked):
            rows = pl.ds(pl.multiple_of(kb * tile, tile), tile)
            for hh in range(2):
                m = m_ref[hh, 0:1, :]
                s = buf[hh]
                if masked:
                    s = jnp.where(row <= col, s, -1e30)
                m_new = jnp.maximum(m, jnp.max(s, axis=0, keepdims=True))
                p = jnp.exp(s - m_new).astype(BF16)
                acc_ref[hh] = jnp.exp(m - m_new) * acc_ref[hh] + _dot_tn(v_ref[hh, rows, :], p)
                m_ref[hh, 0:1, :] = m_new

        def two_blocks(j, _):
            scores(s_b, 2 * j + 1)
            absorb(s_a, 2 * j, False)
            scores(s_a, 2 * j + 2)
            absorb(s_b, 2 * j + 1, False)
            return 0

        def last_block():
            absorb(s_a, i, True)

        def last_two_blocks():
            scores(s_b, i)
            absorb(s_a, i - 1, False)
            absorb(s_b, i, True)

        scores(s_a, 0)
        m_ref[...] = jnp.full(m_ref.shape, -1e30, F32)
        acc_ref[...] = jnp.zeros_like(acc_ref)
        lax.fori_loop(0, i // 2, two_blocks, 0)
        lax.cond(i % 2 == 0, last_block, last_two_blocks)
        outs = []
        for hh in range(2):
            m, acc = m_ref[hh, 0:1, :], acc_ref[hh]
            l = -acc[HEAD_DIM:HEAD_DIM + 1, :]
            outs.append((acc / l)[:HEAD_DIM, :])
            neg_lse = [part.astype(F32) for part in _split3(-(m + jnp.log(l)))]
            q2t_ref[hh] = _place3(sub, L_LSE, neg_lse, q[hh].astype(F32).T).astype(BF16)
        att_ref[...] = jnp.concatenate(outs, axis=0).T
        pl.when(step == steps - 1)(finish)

    pair = pl.BlockSpec((2, tile, LANE), lambda hp, i: (hp, i, 0))
    full = pl.BlockSpec((2, seq, LANE), lambda hp, i: (hp, 0, 0))
    return pl.pallas_call(
        body, name="attention_forward", grid=(N_HEADS // 2, nb),
        in_specs=[pair, full, full, ANY],
        out_specs=[pl.BlockSpec((tile, LANE), lambda hp, i: (i, hp)),
                   pl.BlockSpec((2, LANE, tile), lambda hp, i: (hp, 0, i)), ANY],
        out_shape=[jax.ShapeDtypeStruct((seq, D_ATT), F32),
                   jax.ShapeDtypeStruct((N_HEADS, LANE, seq), BF16),
                   jax.ShapeDtypeStruct((N_DEV,) + w_out.shape, w_out.dtype)],
        scratch_shapes=[pltpu.VMEM((2, tile, tile), F32), pltpu.VMEM((2, tile, tile), F32),
                        pltpu.VMEM((2, 8, tile), F32), pltpu.VMEM((2, LANE, tile), F32),
                        pltpu.SemaphoreType.DMA((N_GATHER_SEMS,)), pltpu.SemaphoreType.DMA((N_GATHER_SEMS,)),
                        pltpu.SemaphoreType.DMA((1,))],
        compiler_params=_params(("arbitrary", "arbitrary")),
    )(qp, kp, vp, w_out)


def _window_sum(x, halo, window, transposed):
    tile = x.shape[0]

    def split_cat(a):
        hi = a.astype(BF16)
        return jnp.concatenate([hi, (a - hi.astype(F32)).astype(BF16)], axis=1)

    def fold(r):
        return r[:, :LANE] + r[:, LANE:]

    r = lax.broadcasted_iota(jnp.int32, (tile, tile), 0)
    c = lax.broadcasted_iota(jnp.int32, (tile, tile), 1)
    rh = lax.broadcasted_iota(jnp.int32, (HALO, HALO), 0)
    ch = lax.broadcasted_iota(jnp.int32, (HALO, HALO), 1)
    if not transposed:
        band = (c <= r) & (r - c < window)
        edge = (rh + HALO - ch) < window
    else:
        band = (r <= c) & (c - r < window)
        edge = (HALO + ch - rh) < window
    out = fold(_dot(band.astype(BF16), split_cat(x)))
    reach = fold(_dot(edge.astype(BF16), split_cat(halo)))
    if not transposed:
        return jnp.concatenate([out[:HALO] + reach, out[HALO:]], axis=0)
    return jnp.concatenate([out[:tile - HALO], out[tile - HALO:] + reach], axis=0)


def _silu_parts(g):
    sig = _sigmoid(g)
    return g * sig, sig * (1.0 + g * (1.0 - sig))


def _middle(x, tgt, att, g_att, g_pool, p, vecs, pool_vecs, w_out, w_pool, tile):
    seq = x.shape[0]
    nt = seq // tile
    halo_blocks = tile // HALO

    def body(x_ref, tgt_ref, att_ref, ga_ref, gp_ref, p_ref, ph_ref, vec_ref, pvec_ref, wo_ref, wp_ref,
             dxa_ref, do2_ref, dga_ref, dgp_ref, dpooled_ref, gwo_ref, dwp_ref, dvec_ref, dwo_ref, dpvec_ref):
        i = pl.program_id(0)

        @pl.when(i == 0)
        def _():
            dwo_ref[...] = jnp.zeros_like(dwo_ref)
            dwp_ref[...] = jnp.zeros_like(dwp_ref)
            dvec_ref[...] = jnp.zeros_like(dvec_ref)
            dpvec_ref[...] = jnp.zeros_like(dpvec_ref)

        gate, b_out, ln_g, ln_b = (vec_ref[k:k + 1, :] for k in range(4))
        b_pool, pool_scale = pvec_ref[0:1, :], pvec_ref[1:2, :]
        x = x_ref[...]
        p = p_ref[...]
        p_halo = ph_ref[...] * jnp.where(i > 0, 1.0, 0.0)
        pos = i * tile + lax.broadcasted_iota(jnp.int32, (tile, 1), 0) + 1

        pooled, mixed = [], []
        for g, window in enumerate(POOL_WINDOWS):
            cols = slice(g * GROUP_DIM, (g + 1) * GROUP_DIM)
            wsum = _window_sum(p[:, cols], p_halo[:, cols], window, False)
            count = jnp.minimum(pos, window).astype(F32)
            pooled.append(wsum / count - p[:, cols])
            mixed.append(_dot(pooled[g].astype(BF16), wp_ref[g]) + b_pool[:, cols])
        mixed = jnp.concatenate(mixed, axis=1)
        pool = mixed * pool_scale

        att = att_ref[...]
        g_att, g_pool = ga_ref[...], gp_ref[...]
        silu_a, dsilu_a = _silu_parts(g_att)
        silu_p, dsilu_p = _silu_parts(g_pool)
        y_in = jnp.concatenate([att * silu_a, pool * silu_p], axis=1)
        y = _dot(y_in.astype(BF16), wo_ref[...]) + b_out
        h = ALPHA * x + gate * y
        mu = jnp.mean(h, axis=1, keepdims=True)
        hc = h - mu
        var = jnp.mean(hc * hc, axis=1, keepdims=True)
        rstd = lax.rsqrt(var + LN_EPS)
        yhat = hc * rstd
        diff = yhat * ln_g + ln_b - tgt_ref[...]
        loss_rows = jnp.sum(diff * diff, axis=1, keepdims=True)
        d_out = diff * (1.0 / D)

        d_yhat = d_out * ln_g
        dh = rstd * (d_yhat - jnp.mean(d_yhat, axis=1, keepdims=True)
                     - yhat * jnp.mean(d_yhat * yhat, axis=1, keepdims=True))
        dxa_ref[...] = ALPHA * dh
        dy = dh * gate
        dyb = dy.astype(BF16)
        lane = _lanes((1, D))
        loss_row = jnp.where(lane == 0, (0.5 / D) * jnp.sum(loss_rows, axis=0, keepdims=True), 0.0)
        dvec_ref[5:6, :] += jnp.sum(dh * y, axis=0, keepdims=True)
        dvec_ref[0:1, :] += jnp.sum(dy, axis=0, keepdims=True)
        dvec_ref[1:2, :] += jnp.sum(d_out * yhat, axis=0, keepdims=True)
        dvec_ref[2:3, :] += jnp.sum(d_out, axis=0, keepdims=True)
        dvec_ref[4:5, :] += loss_row

        dwo_ref[...] += _dot(y_in.T.astype(BF16), dyb)
        d_yin = _dot_nt(dyb, wo_ref[...])
        d_a, d_pl = d_yin[:, :D_ATT], d_yin[:, D_ATT:]
        d_att = d_a * silu_a
        d_att_t = d_att.T
        prod_t = (d_att * att).T
        sub = lax.broadcasted_iota(jnp.int32, (HEAD_DIM, tile), 0)
        for h in range(N_HEADS):
            rows = slice(h * HEAD_DIM, (h + 1) * HEAD_DIM)
            delta = jnp.sum(prod_t[rows], axis=0, keepdims=True)
            extra = _place3(sub, 0, [part.astype(F32) for part in _split3(delta)], 0.0)
            do2_ref[h] = jnp.concatenate([d_att_t[rows], extra], axis=0).astype(BF16)
        dga_ref[...] = d_a * att * dsilu_a
        dgp_ref[...] = d_pl * pool * dsilu_p
        d_pool = d_pl * silu_p
        d_mixed = d_pool * pool_scale
        dpvec_ref[0:1, :] += jnp.sum(d_mixed, axis=0, keepdims=True)
        dpvec_ref[1:2, :] += jnp.sum(d_pool * mixed, axis=0, keepdims=True)
        d_pooled = []
        for g in range(len(POOL_WINDOWS)):
            cols = slice(g * GROUP_DIM, (g + 1) * GROUP_DIM)
            dmb = d_mixed[:, cols].astype(BF16)
            dwp_ref[g] += _dot(pooled[g].T.astype(BF16), dmb)
            d_pooled.append(_dot_nt(dmb, wp_ref[g]))
        dpooled_ref[...] = jnp.concatenate(d_pooled, axis=1)

        @pl.when(i == nt - 1)
        def _():
            gwo_ref[...] = dwo_ref[...].astype(BF16)
            dvec_ref[3:4, :] = jnp.concatenate([dpvec_ref[0:1, :], dpvec_ref[1:2, :]], axis=1)

    tok = lambda width: pl.BlockSpec((tile, width), lambda i: (i, 0))
    whole = lambda a: pl.BlockSpec(a.shape, lambda i: (0,) * a.ndim)
    halo = pl.BlockSpec((HALO, D_POOL), lambda i: (jnp.maximum(i * halo_blocks - 1, 0), 0))
    half = jax.ShapeDtypeStruct((seq, D_ATT), F32)
    outs = [jax.ShapeDtypeStruct((seq, D), F32), jax.ShapeDtypeStruct((N_HEADS, LANE, seq), BF16), half, half, half,
            jax.ShapeDtypeStruct(w_out.shape, BF16), jax.ShapeDtypeStruct(w_pool.shape, F32),
            jax.ShapeDtypeStruct(vecs.shape, F32)]
    return pl.pallas_call(
        body, name="middle", grid=(nt,),
        in_specs=[tok(D), tok(D), tok(D_ATT), tok(D_ATT), tok(D_POOL), tok(D_POOL), halo,
                  whole(vecs), whole(pool_vecs), whole(w_out), whole(w_pool)],
        out_specs=[tok(D), pl.BlockSpec((N_HEADS, LANE, tile), lambda i: (0, 0, i)),
                   tok(D_ATT), tok(D_POOL), tok(D_POOL),
                   whole(w_out), whole(w_pool), whole(vecs)],
        out_shape=outs,
        scratch_shapes=[pltpu.VMEM(w_out.shape, F32), pltpu.VMEM(pool_vecs.shape, F32)],
        compiler_params=_params(("arbitrary",)),
    )(x, tgt, att, g_att, g_pool, p, p, vecs, pool_vecs, w_out, w_pool)


def _attention_backward(q2t, kp, vp, do2t, gw_out, vecs, pool, tile):
    seq = kp.shape[1]
    nb = seq // tile
    last = N_HEADS // 2 - 1

    def body(qt_ref, k_ref, v_ref, dot_ref, gwo_hbm, vecs_hbm, pool_hbm,
             dq_ref, dk_ref, dv_ref, dcum_ref, g_out_ref, vecs_sum_ref, pool_sum_ref,
             dq_acc, dk_acc, dv_acc, gwo_ref, vecs_ref, pool_ref,
             r1, s2, r2, v_sib, v_chip, v_recv, p_sib, p_chip, p_recv, send_sems, recv_sems):
        hp = pl.program_id(0)
        start, middle, fold, finish = _reduce_stages(
            [gwo_ref], [g_out_ref], [r1], [s2], [r2],
            [(vecs_ref, vecs_sum_ref, v_sib, v_chip, v_recv), (pool_ref, pool_sum_ref, p_sib, p_chip, p_recv)],
            send_sems, recv_sems)

        @pl.when(hp == 0)
        def _():
            pltpu.sync_copy(gwo_hbm, gwo_ref)
            pltpu.sync_copy(vecs_hbm, vecs_ref)
            pltpu.sync_copy(pool_hbm, pool_ref)
            start()

        pl.when(hp == 1)(middle)
        pl.when(hp == 2)(fold)

        row = lax.broadcasted_iota(jnp.int32, (tile, tile), 0)
        col = lax.broadcasted_iota(jnp.int32, (tile, tile), 1)
        dq_acc[...] = jnp.zeros_like(dq_acc)

        def kv_block(kb, _):
            krows = pl.ds(pl.multiple_of(kb * tile, tile), tile)
            k = [k_ref[hh, krows, :] for hh in range(2)]
            v = [v_ref[hh, krows, :] for hh in range(2)]
            k_t = [k[hh].T for hh in range(2)]

            def q_block(qb, masked):
                qcols = pl.ds(pl.multiple_of(qb * tile, tile), tile)
                for hh in range(2):
                    q_t = qt_ref[hh, :, qcols]
                    do_t = dot_ref[hh, :, qcols]
                    s_t = _dot(k[hh], q_t)
                    if masked:
                        s_t = jnp.where(row <= col, s_t, -1e30)
                    p_t = jnp.exp(s_t)
                    ds_t = (p_t * _dot(v[hh], do_t)).astype(BF16)
                    dv_new = _dot_nt(do_t, p_t.astype(BF16))
                    dk_new = _dot_nt(q_t, ds_t)
                    if masked:
                        dv_acc[hh], dk_acc[hh] = dv_new, dk_new
                    else:
                        dv_acc[hh] += dv_new
                        dk_acc[hh] += dk_new
                    dq_acc[hh, :, qcols] += _dot(k_t[hh], ds_t)

            q_block(kb, True)

            def two_later_blocks(j, _):
                q_block(kb + 1 + 2 * j, False)
                q_block(kb + 2 + 2 * j, False)
                return 0

            later = nb - 1 - kb
            lax.fori_loop(0, later // 2, two_later_blocks, 0)
            pl.when(later % 2 == 1)(lambda: q_block(nb - 1, False))
            for hh in range(2):
                dk = dk_acc[hh]
                dk_ref[hh, :, krows] = dk.astype(BF16)
                dv_ref[hh, :, krows] = dv_acc[hh].astype(BF16)
                dcum_ref[hh, :, krows] = -dk[L_CK:L_CK + 1, :]
            return 0

        lax.fori_loop(0, nb, kv_block, 0)
        for hh in range(2):
            dq = dq_acc[hh]
            dcum_ref[hh] += dq[L_CQ:L_CQ + 1, :]
            dq_ref[hh] = (dq * Q_SCALE).astype(BF16)
        pl.when(hp == last)(finish)

    pair = pl.BlockSpec((2, seq, LANE), lambda hp: (hp, 0, 0))
    pair_t = pl.BlockSpec((2, LANE, seq), lambda hp: (hp, 0, 0))
    whole = lambda shape: pl.BlockSpec(shape, lambda hp: (0,) * len(shape))
    grad = jax.ShapeDtypeStruct((N_HEADS, LANE, seq), BF16)
    return pl.pallas_call(
        body, name="attention_backward", grid=(N_HEADS // 2,),
        in_specs=[pair_t, pair, pair, pair_t, ANY, ANY, ANY],
        out_specs=[pair_t, pair_t, pair_t, pl.BlockSpec((2, 1, seq), lambda hp: (hp, 0, 0)),
                   whole(gw_out.shape[1:]), whole(vecs.shape), whole(pool.shape)],
        out_shape=[grad, grad, grad, jax.ShapeDtypeStruct((N_HEADS, 1, seq), F32),
                   jax.ShapeDtypeStruct(gw_out.shape[1:], F32), jax.ShapeDtypeStruct(vecs.shape, F32),
                   jax.ShapeDtypeStruct(pool.shape, F32)],
        scratch_shapes=[pltpu.VMEM((2, LANE, seq), F32), pltpu.VMEM((2, LANE, tile), F32),
                        pltpu.VMEM((2, LANE, tile), F32), pltpu.VMEM(gw_out.shape, BF16),
                        pltpu.VMEM(vecs.shape, F32), pltpu.VMEM(pool.shape, F32)]
        + _reduce_scratch(gw_out, [vecs, pool]),
        compiler_params=_params(("arbitrary",)),
    )(q2t, kp, vp, do2t, gw_out, vecs, pool)


def _inproj_backward(dqp, dkp, dvp, d_cum, f, d_pooled, d_ga, d_gp, x, dxa, u, mod, w_main, w_f, tile):
    seq = x.shape[0]
    nt = seq // tile
    halo_blocks = tile // HALO

    def body(dq_ref, dk_ref, dv_ref, dcum_ref, f_ref, dpo_ref, dph_ref, dga_ref, dgp_ref, x_ref, dxa_ref, u_ref,
             mod_ref, w_ref, wf_ref,
             dx_ref, dproj_ref, dwf_ref, db_ref, dbf_ref, dmod_ref, carry_ref):
        step = pl.program_id(0)
        i = nt - 1 - step

        @pl.when(step == 0)
        def _():
            carry_ref[...] = jnp.zeros_like(carry_ref)
            dwf_ref[...] = jnp.zeros_like(dwf_ref)
            db_ref[...] = jnp.zeros_like(db_ref)
            dbf_ref[...] = jnp.zeros_like(dbf_ref)
            dmod_ref[...] = jnp.zeros_like(dmod_ref)

        ones = jnp.ones((8, tile), BF16)

        def emit(chunk, val):
            cols = pl.ds(chunk * COL_CHUNK, COL_CHUNK)
            db_ref[0:1, cols] += jnp.sum(val, axis=0, keepdims=True)
            vb = val.astype(BF16)
            dproj_ref[:, pl.ds((chunk - 3) * COL_CHUNK, COL_CHUNK)] = vb
            return _dot(vb, w_ref[cols, :])

        d_u = jnp.zeros((tile, D), F32)
        for chunk, ref in enumerate((dq_ref, dk_ref, dv_ref)):
            cols = pl.ds(chunk * COL_CHUNK, COL_CHUNK)
            val_t = ref[:, 0:HEAD_DIM, :].reshape(COL_CHUNK, tile)
            db_ref[:, cols] += _dot_nt(ones, val_t)
            d_u += _dot_tn(val_t, w_ref[cols, :])

        d_pooled = dpo_ref[...]
        d_halo = dph_ref[...] * jnp.where(i < nt - 1, 1.0, 0.0)
        pos = i * tile + lax.broadcasted_iota(jnp.int32, (tile, 1), 0) + 1
        d_p = []
        for g, window in enumerate(POOL_WINDOWS):
            cols = slice(g * GROUP_DIM, (g + 1) * GROUP_DIM)
            scaled = d_pooled[:, cols] / jnp.minimum(pos, window).astype(F32)
            d_p.append(_window_sum(scaled, d_halo[:, cols] * (1.0 / window), window, True) - d_pooled[:, cols])
        d_u += emit(3, jnp.concatenate(d_p, axis=1))
        d_u += emit(4, dga_ref[...])
        d_u += emit(5, dgp_ref[...])

        row = lax.broadcasted_iota(jnp.int32, (tile, tile), 0)
        col = lax.broadcasted_iota(jnp.int32, (tile, tile), 1)
        later = (row >= col).astype(BF16)
        d_logf = sum(_dot(part, later) for part in _split3(dcum_ref[:, 0, :])) + carry_ref[:, 0:1]
        carry_ref[:, 0:1] = d_logf[:, 0:1]
        d_f = d_logf * _sigmoid(-f_ref[...].T[0:N_HEADS, :])
        d_f = jnp.concatenate([d_f, jnp.zeros((LANE - N_HEADS, tile), F32)], axis=0)
        dbf_ref[...] += sum(_dot_nt(ones, part) for part in _split3(d_f))
        d_fb = d_f.astype(BF16)
        d_u += _dot_tn(d_fb, wf_ref[...])
        dwf_ref[...] += _dot(d_fb, u_ref[...])

        x = x_ref[...]
        dx_ref[...] = dxa_ref[...] + d_u * mod_ref[0:1, :]
        dmod_ref[0:1, :] += jnp.sum(d_u * x, axis=0, keepdims=True)
        dmod_ref[1:2, :] += jnp.sum(d_u, axis=0, keepdims=True)

    rev = lambda step: nt - 1 - step
    tok = lambda width: pl.BlockSpec((tile, width), lambda s: (rev(s), 0))
    head_block = pl.BlockSpec((N_HEADS, LANE, tile), lambda s: (0, 0, rev(s)))
    whole = lambda a: pl.BlockSpec(a.shape, lambda s: (0,) * a.ndim)
    halo = pl.BlockSpec((HALO, D_POOL), lambda s: (jnp.minimum((rev(s) + 1) * halo_blocks, seq // HALO - 1), 0))
    small = lambda width: jax.ShapeDtypeStruct((8, width), F32)
    n_rest = N_MAIN - OFF_P
    return pl.pallas_call(
        body, name="inproj_backward", grid=(nt,),
        in_specs=[head_block, head_block, head_block, pl.BlockSpec((N_HEADS, 1, tile), lambda s: (0, 0, rev(s))),
                  tok(LANE), tok(D_POOL), halo, tok(D_ATT), tok(D_POOL),
                  tok(D), tok(D), tok(D),
                  whole(mod), whole(w_main), whole(w_f)],
        out_specs=[tok(D), tok(n_rest), pl.BlockSpec((LANE, D), lambda s: (0, 0)),
                   pl.BlockSpec((8, N_MAIN), lambda s: (0, 0)), pl.BlockSpec((8, LANE), lambda s: (0, 0)),
                   pl.BlockSpec((8, D), lambda s: (0, 0))],
        out_shape=[jax.ShapeDtypeStruct((seq, D), F32), jax.ShapeDtypeStruct((seq, n_rest), BF16),
                   jax.ShapeDtypeStruct((LANE, D), F32), small(N_MAIN), small(LANE), small(D)],
        scratch_shapes=[pltpu.VMEM((8, LANE), F32)],
        compiler_params=_params(("arbitrary",)),
    )(dqp, dkp, dvp, d_cum, f, d_pooled, d_pooled, d_ga, d_gp, x, dxa, u, mod, w_main, w_f)


def _weight_grads(dq_t, dk_t, dv_t, dw_f, dproj, u, k_tile):
    seq = u.shape[0]
    nk = seq // k_tile
    rows = N_HEADS * HEAD_DIM

    def body(dq_ref, dk_ref, dv_ref, dwf_ref, dp_ref, u_ref, out_ref, acc_ref):
        k = pl.program_id(0)

        @pl.when(k == 0)
        def _():
            acc_ref[...] = jnp.zeros_like(acc_ref)

        tokens = u_ref[...]
        for j, ref in enumerate((dq_ref, dk_ref, dv_ref)):
            acc_ref[pl.ds(j * rows, rows), :] += _dot(ref[...].reshape(rows, k_tile), tokens)
        for j in range(dproj.shape[1] // COL_CHUNK):
            cols = pl.ds(j * COL_CHUNK, COL_CHUNK)
            acc_ref[pl.ds(F_HI + j * COL_CHUNK, COL_CHUNK), :] += _dot_tn(dp_ref[:, cols], tokens)

        @pl.when(k == nk - 1)
        def _():
            acc_ref[F_LO:F_HI, :] = dwf_ref[0:N_HEADS, :]
            for slot in range(N_DEV):
                out_ref[slot] = acc_ref[slot * shard:(slot + 1) * shard, :].astype(BF16)

    shard = D_IN // N_DEV
    heads = pl.BlockSpec((N_HEADS, HEAD_DIM, k_tile), lambda k: (0, 0, k))
    return pl.pallas_call(
        body, name="weight_grads", grid=(nk,),
        in_specs=[heads, heads, heads, pl.BlockSpec(dw_f.shape, lambda k: (0, 0)),
                  pl.BlockSpec((k_tile, dproj.shape[1]), lambda k: (k, 0)), pl.BlockSpec((k_tile, D), lambda k: (k, 0))],
        out_specs=pl.BlockSpec((N_DEV, shard, D), lambda k: (0, 0, 0)),
        out_shape=jax.ShapeDtypeStruct((N_DEV, shard, D), BF16),
        scratch_shapes=[pltpu.VMEM((D_IN, D), F32)],
        compiler_params=_params(("arbitrary",)),
    )(dq_t, dk_t, dv_t, dw_f, dproj, u)


def _adamw(w, g, m, v):
    m = ADAM_B1 * m + (1.0 - ADAM_B1) * g
    v = ADAM_B2 * v + (1.0 - ADAM_B2) * (g * g)
    m_hat = m / (1.0 - ADAM_B1 ** ADAM_STEP)
    v_hat = v / (1.0 - ADAM_B2 ** ADAM_STEP)
    delta = -ADAM_LR * (m_hat / (jnp.sqrt(v_hat) + ADAM_EPS) + ADAM_WD * w)
    return delta, m, v


SUBLANES = 8


def _adamw_packed(g, w, m, v, name, chunks=4):
    rows, cols = g.shape
    per_row = cols // LANE
    assert cols % LANE == 0 and per_row == SUBLANES and w.shape == (rows * per_row, LANE)
    step = -(-rows // (chunks * SUBLANES)) * SUBLANES
    bounds = [(r0, min(r0 + step, rows)) for r0 in range(0, rows, step)]

    def body(g_hbm, w_hbm, m_hbm, v_hbm, og_hbm, od_hbm, om_hbm, ov_hbm, g_buf, in_buf, out_buf, in_sems, out_sems):
        def copies_in(c):
            r0, r1 = bounds[c]
            packed = slice(r0 * per_row, r1 * per_row)
            return [pltpu.make_async_copy(g_hbm.at[r0:r1], g_buf.at[r0:r1], in_sems.at[c, 0])] + [
                pltpu.make_async_copy(src.at[packed], in_buf.at[i, packed], in_sems.at[c, 1 + i])
                for i, src in enumerate((w_hbm, m_hbm, v_hbm))]

        def copies_out(c):
            r0, r1 = bounds[c]
            packed = slice(r0 * per_row, r1 * per_row)
            return [pltpu.make_async_copy(out_buf.at[i, packed], dst.at[packed], out_sems.at[c, i])
                    for i, dst in enumerate((og_hbm, od_hbm, om_hbm, ov_hbm))]

        for c in range(len(bounds)):
            for cp in copies_in(c):
                cp.start()
        for c, (r0, r1) in enumerate(bounds):
            for cp in copies_in(c):
                cp.wait()
            for j in range(per_row):
                lanes = pl.ds(r0 * per_row + j, r1 - r0, stride=per_row)
                g_part = g_buf[r0:r1, j * LANE:(j + 1) * LANE]
                results = _adamw(in_buf[0, lanes, :], g_part, in_buf[1, lanes, :], in_buf[2, lanes, :])
                for i, val in enumerate((g_part,) + results):
                    out_buf[i, lanes, :] = val
            for cp in copies_out(c):
                cp.start()
        for c in range(len(bounds)):
            for cp in copies_out(c):
                cp.wait()

    shape = jax.ShapeDtypeStruct(w.shape, F32)
    return pl.pallas_call(
        body, name=name,
        in_specs=[ANY] * 4, out_specs=[ANY] * 4, out_shape=[shape] * 4,
        scratch_shapes=[pltpu.VMEM(g.shape, F32), pltpu.VMEM((3,) + w.shape, F32), pltpu.VMEM((4,) + w.shape, F32),
                        pltpu.SemaphoreType.DMA((len(bounds), 4)), pltpu.SemaphoreType.DMA((len(bounds), 4))],
        compiler_params=_params(),
    )(g, w, m, v)


def _ada_adamw(sc_all, d_ada, w, m, v, chunks=4):
    rows, cols = w.shape
    step = rows // chunks
    assert rows % chunks == 0 and step % LANE == 0

    def body(sc_ref, d_ref, w_hbm, m_hbm, v_hbm, og_hbm, od_hbm, om_hbm, ov_hbm, in_buf, out_buf, in_sems, out_sems):
        def copies_in(c):
            part = slice(c * step, (c + 1) * step)
            return [pltpu.make_async_copy(src.at[part], in_buf.at[i, part], in_sems.at[c, i])
                    for i, src in enumerate((w_hbm, m_hbm, v_hbm))]

        def copies_out(c):
            part = slice(c * step, (c + 1) * step)
            return [pltpu.make_async_copy(out_buf.at[i, part], dst.at[part], out_sems.at[c, i])
                    for i, dst in enumerate((og_hbm, od_hbm, om_hbm, ov_hbm))]

        for c in range(chunks):
            for cp in copies_in(c):
                cp.start()
        for c in range(chunks):
            part = slice(c * step, (c + 1) * step)
            sc_t = sc_ref[:, part].T
            g = sc_t[:, 0:1] * d_ref[0:1, :]
            for b in range(1, N_DEV):
                g = g + sc_t[:, b:b + 1] * d_ref[b:b + 1, :]
            for cp in copies_in(c):
                cp.wait()
            results = _adamw(in_buf[0, part, :], g, in_buf[1, part, :], in_buf[2, part, :])
            for i, val in enumerate((g,) + results):
                out_buf[i, part, :] = val
            for cp in copies_out(c):
                cp.start()
        for c in range(chunks):
            for cp in copies_out(c):
                cp.wait()

    in_vmem = pl.BlockSpec(memory_space=pltpu.VMEM)
    shape = jax.ShapeDtypeStruct(w.shape, F32)
    return pl.pallas_call(
        body, name="ada_adamw",
        in_specs=[in_vmem, in_vmem, ANY, ANY, ANY], out_specs=[ANY] * 4, out_shape=[shape] * 4,
        scratch_shapes=[pltpu.VMEM((3,) + w.shape, F32), pltpu.VMEM((4,) + w.shape, F32),
                        pltpu.SemaphoreType.DMA((chunks, 3)), pltpu.SemaphoreType.DMA((chunks, 4))],
        compiler_params=_params(),
    )(sc_all, d_ada, w, m, v)


F_LO, F_HI = 3 * D_ATT, 3 * D_ATT + N_HEADS


def _split_forget(a, axis):
    idx = lambda lo, hi: tuple(slice(lo, hi) if d == axis else slice(None) for d in range(a.ndim))
    pad = [(0, LANE - N_HEADS) if d == axis else (0, 0) for d in range(a.ndim)]
    return jnp.concatenate([a[idx(0, F_LO)], a[idx(F_HI, D_IN)]], axis=axis), jnp.pad(a[idx(F_LO, F_HI)], pad)


def _join_forget(main, f, axis):
    idx = lambda lo, hi: tuple(slice(lo, hi) if d == axis else slice(None) for d in range(main.ndim))
    return jnp.concatenate([main[idx(0, F_LO)], f[idx(0, N_HEADS)], main[idx(F_LO, N_MAIN)]], axis=axis)


def _adamw_small(grad_rows, row_params, whole_params, summed_params, scalar_at):
    n_row, n_whole, n_sum = len(row_params), len(whole_params), len(summed_params)
    n = n_row + n_whole + n_sum

    def body(g_ref, *refs):
        n_in = 3 * n_row + 4 * (n_whole + n_sum)
        ins, outs = list(refs[:n_in]), refs[n_in:]
        for i in range(n):
            if i < n_row:
                row, lo, hi = row_params[i][:3]
                g = g_ref[row:row + 1, lo:hi]
            elif i < n_row + n_whole:
                g = ins.pop(0)[...]
            else:
                parts = ins.pop(0)
                g = parts[0]
                for k in range(1, N_DEV):
                    g = g + parts[k]
            w, m, v = (ins.pop(0)[...] for _ in range(3))
            outs[4 * i][...] = g
            outs[4 * i + 1][...], outs[4 * i + 2][...], outs[4 * i + 3][...] = _adamw(w, g, m, v)
        row, lane = scalar_at
        outs[4 * n][...] = g_ref[row:row + 1, lane:lane + 1]

    shapes = [p[3] for p in row_params] + [p[1] for p in whole_params] + [p[1] for p in summed_params]
    operands = [a for p in row_params for a in p[3:]] + [a for p in whole_params + summed_params for a in p]
    flat = pl.pallas_call(
        body, name="adamw_small",
        out_shape=[jax.ShapeDtypeStruct(w.shape, F32) for w in shapes for _ in range(4)]
        + [jax.ShapeDtypeStruct((1, 1), F32)],
        compiler_params=_params(),
    )(grad_rows, *operands)
    return [flat[4 * i:4 * i + 4] for i in range(n)], flat[4 * n].reshape(())


def kernel(x, c, w_ada, b_ada, w_in, b_in, w_pool_mix, b_pool_mix, pool_scale, w_out, b_out, ln_g, ln_b, loss_target, m_w_ada, m_b_ada, m_w_in, m_b_in, m_w_pool_mix, m_b_pool_mix, m_pool_scale, m_w_out, m_b_out, m_ln_g, m_ln_b, v_w_ada, v_b_ada, v_w_in, v_b_in, v_w_pool_mix, v_b_pool_mix, v_pool_scale, v_w_out, v_b_out, v_ln_g, v_ln_b):
    seq = x.shape[1]
    tile = min(256, seq)
    attn_tile = min(512, max(128, seq // 4))
    me = _dev_index(*_mesh_pos())
    x2, tgt = x[0], loss_target[0]

    rows_of = lambda a: jnp.swapaxes(a, 1, 2)[0]
    w_main, w_f, sc_all, ada_mine = _gather_and_ada(c, rows_of(w_in).astype(BF16), w_ada[0])
    ada = ada_mine.reshape(1, D_ADA) + b_ada
    shift, scale, gate = ada[:, 0:D], ada[:, D:2 * D], ada[:, 2 * D:]
    mod = jnp.concatenate([1.0 + scale, shift, jnp.zeros((6, D), F32)], axis=0)
    b_main, b_f = _split_forget(b_in, 1)

    qp, kp, vp, f, p, g_att, g_pool, u = _inproj_forward(x2, mod, w_main, w_f, b_main, b_f, tile)
    att, q2t, w_out_g = _attention_forward(qp, kp, vp, w_out[0].astype(BF16), attn_tile)

    vecs = jnp.concatenate([gate, b_out, ln_g, ln_b, jnp.zeros((4, D), F32)], axis=0)
    pool_vecs = jnp.concatenate([b_pool_mix.reshape(1, D_POOL), pool_scale, jnp.zeros((6, D_POOL), F32)], axis=0)
    dxa, do2, d_ga, d_gp, d_pooled, gw_out, dw_pool, dvec = _middle(
        x2, tgt, att, g_att, g_pool, p, vecs, pool_vecs, w_out_g.reshape(D, D), w_pool_mix[0].astype(BF16), tile)

    pool_rows = w_pool_mix.shape[1] * GROUP_DIM
    dqp, dkp, dvp, d_cum, g_out, dvec_sum, dw_pool_sum = _attention_backward(
        q2t, kp, vp, do2, gw_out.reshape(N_DEV, D // N_DEV, D), dvec, dw_pool.reshape(pool_rows, GROUP_DIM), attn_tile)
    dx, dproj, dw_f, db_main, db_f, dmod = _inproj_backward(
        dqp, dkp, dvp, d_cum, f, d_pooled, d_ga, d_gp, x2, dxa, u, mod, w_main, w_f, tile)
    gw_in = _weight_grads(dqp, dkp, dvp, dw_f, dproj, u, min(512, seq))
    d_ada = jnp.concatenate([dmod[1:2], dmod[0:1], dvec[5:6]], axis=1)
    g_in_rows, g_b_in, d_ada_all = _reduce_grads(gw_in, _join_forget(db_main[0:1], db_f[0:1], 1), d_ada)

    packed = lambda a: jnp.transpose(a.reshape(SUBLANES, LANE, -1), (2, 0, 1)).reshape(-1, LANE)
    outs_in = _adamw_packed(g_in_rows, packed(w_in), packed(m_w_in), packed(v_w_in), "adamw_w_in")
    g_w_in, d_w_in, nm_w_in, nv_w_in = (
        jnp.transpose(a.reshape(-1, SUBLANES, LANE), (1, 2, 0)).reshape(D, -1) for a in outs_in)
    flat_pool = lambda a: a.reshape(1, D_POOL)
    pool_2d = lambda a: a.reshape(pool_rows, GROUP_DIM)
    rows, loss = _adamw_small(
        dvec_sum,
        [(0, 0, D, b_out, m_b_out, v_b_out), (1, 0, D, ln_g, m_ln_g, v_ln_g), (2, 0, D, ln_b, m_ln_b, v_ln_b),
         (3, 0, D_POOL, flat_pool(b_pool_mix), flat_pool(m_b_pool_mix), flat_pool(v_b_pool_mix)),
         (3, D_POOL, 2 * D_POOL, pool_scale, m_pool_scale, v_pool_scale)],
        [(g_out, w_out[0], m_w_out[0], v_w_out[0]),
         (dw_pool_sum, pool_2d(w_pool_mix), pool_2d(m_w_pool_mix), pool_2d(v_w_pool_mix)),
         (g_b_in, b_in, m_b_in, v_b_in)],
        [(d_ada_all, b_ada, m_b_ada, v_b_ada)],
        scalar_at=(4, 0))
    small = {"b_out": rows[0], "ln_g": rows[1], "ln_b": rows[2],
             "b_pool": [a.reshape(b_pool_mix.shape) for a in rows[3]], "pool_scale": rows[4],
             "w_pool": [a.reshape(w_pool_mix.shape) for a in rows[6]], "b_in": rows[7]}
    g_s, d_s, nm_s, nv_s = ({k: r[j] for k, r in small.items()} for j in range(4))
    g_w_out, d_w_out, nm_w_out, nv_w_out = rows[5]
    g_b_ada, d_b_ada, nm_b_ada, nv_b_ada = rows[8]

    d_ada_local = lax.dynamic_slice_in_dim(d_ada_all.reshape(N_DEV, D_ADA), me * (D_ADA // N_DEV), D_ADA // N_DEV, axis=1)
    g_w_ada, d_w_ada, nm_w_ada, nv_w_ada = _ada_adamw(sc_all, d_ada_local, w_ada[0], m_w_ada[0], v_w_ada[0])

    def ordered(w_ada_, b_ada_, w_in_, w_out_, s):
        return (w_ada_[None], b_ada_, w_in_[None], s["b_in"], s["w_pool"], s["b_pool"], s["pool_scale"],
                w_out_[None], s["b_out"], s["ln_g"], s["ln_b"])

    return (loss, dx[None],
            *ordered(g_w_ada, g_b_ada, g_w_in, g_w_out, g_s),
            *ordered(d_w_ada, d_b_ada, d_w_in, d_w_out, d_s),
            *ordered(nm_w_ada, nm_b_ada, nm_w_in, nm_w_out, nm_s),
            *ordered(nv_w_ada, nv_b_ada, nv_w_in, nv_w_out, nv_s))
```

```python
import jax
import jax.numpy as jnp
from jax import lax
from jax.experimental import pallas as pl
from jax.experimental.pallas import tpu as pltpu

F32 = jnp.float32
BF16 = jnp.bfloat16

N_DEV = 8
D = 1024
N_HEADS = 8
HEAD_DIM = 64
D_ATT = 512
D_POOL = 512
POOL_WINDOWS = (2, 4, 8, 16)
GROUP_DIM = 128
HALO = 16
LANE = 128
D_IN = 3080
D_ADA = 3072
N_MAIN = 3072
OFF_P = 1536
COL_CHUNK = 512
Q_SCALE = 0.125
LN_EPS = 1e-5
ALPHA = 2.0 ** 0.25
L_CQ, L_CK, L_LSE = 64, 67, 70

ADAM_LR, ADAM_B1, ADAM_B2, ADAM_EPS, ADAM_WD, ADAM_STEP = 0.001, 0.9, 0.999, 1e-08, 0.01, 10
VMEM_LIMIT = 56 * 1024 * 1024

MESH = pl.DeviceIdType.MESH
ANY = pl.BlockSpec(memory_space=pl.ANY)


def _params(sem=None, vmem=VMEM_LIMIT):
    return pltpu.CompilerParams(dimension_semantics=sem, vmem_limit_bytes=vmem)


def _split3(a):
    hi = a.astype(BF16)
    r = a - hi.astype(F32)
    mid = r.astype(BF16)
    lo = (r - mid.astype(F32)).astype(BF16)
    return hi, mid, lo


def _dot(a, b):
    return jnp.dot(a, b, preferred_element_type=F32)


def _dot_nt(a, b):
    return lax.dot_general(a, b, (((1,), (1,)), ((), ())), preferred_element_type=F32)


def _dot_tn(a, b):
    return lax.dot_general(a, b, (((0,), (0,)), ((), ())), preferred_element_type=F32)


def _dot3(m01, a):
    hi, mid, lo = _split3(a)
    return _dot(m01, hi) + _dot(m01, mid) + _dot(m01, lo)


def _sigmoid(z):
    return 1.0 / (1.0 + jnp.exp(-z))


def _lanes(shape):
    return lax.broadcasted_iota(jnp.int32, shape, len(shape) - 1)


def _place3(lane, base, parts, other):
    out = other
    for j in range(3):
        out = jnp.where(lane == base + j, parts[j], out)
    return out


def _mesh_pos():
    return lax.axis_index("x"), lax.axis_index("y"), lax.axis_index("c")


def _dev_index(px, py, pc):
    return 4 * px + 2 * py + pc


N_GATHER_SEMS = 9


def _gather_stages(src_ref, out_ref, send_sems, recv_sems, local_sem):
    x, y, c = _mesh_pos()
    me, sibling = (x, y, c), (x, y, 1 - c)
    nbr_x, nbr_y, diag = (1 - x, y), (x, 1 - y), (1 - x, 1 - y)
    half = out_ref.shape[-1] // 2
    left, right = pl.ds(0, half), pl.ds(half, half)

    def copy(k, block, to, cols=None, src=None):
        slot = out_ref.at[_dev_index(*block)]
        if cols is not None:
            slot = slot.at[:, cols]
        return pltpu.make_async_remote_copy(
            src_ref=slot if src is None else src, dst_ref=slot, send_sem=send_sems.at[k], recv_sem=recv_sems.at[k],
            device_id=to, device_id_type=MESH)

    mine = pltpu.make_async_copy(src_ref, out_ref.at[_dev_index(*me)], local_sem)
    first = [copy(0, me, sibling, src=src_ref), copy(1, me, (*nbr_x, c), src=src_ref), copy(2, me, (*nbr_y, c), src=src_ref)]
    relay = [(1, nbr_x, None, nbr_x), (2, nbr_y, None, nbr_y), (3, diag, left, nbr_y), (4, diag, right, nbr_x)]
    onward = [copy(3, (*nbr_x, c), (*nbr_y, c), cols=left), copy(4, (*nbr_y, c), (*nbr_x, c), cols=right)]
    passed = [copy(4 + k, (*block, c), sibling, cols=cols) for k, block, cols, _ in relay]

    def start():
        mine.start()
        for cp in first:
            cp.start()

    def relay_stage(first_item):
        def run():
            for j in (first_item, first_item + 1):
                k, block, cols, frm = relay[j]
                copy(k, (*block, c), (*frm, c), cols=cols).wait_recv()
                if j < 2:
                    onward[j].start()
                passed[j].start()
        return run

    def finish():
        copy(0, sibling, me).wait_recv()
        for k, block, cols, _ in relay:
            copy(4 + k, (*block, 1 - c), me, cols=cols).wait_recv()
        for cp in first + onward + passed:
            cp.wait_send()
        mine.wait()

    return start, relay_stage(0), relay_stage(2), finish


N_REDUCE_SEMS = 10
N_SMALL_SEMS = 4
N_ROWS_SEMS = 7


def _reduce_stages(ins, gs, r1, s2, r2, smalls, send_sems, recv_sems, rows=None, own=None):
    n = len(ins)
    x, y, c = _mesh_pos()
    me = _dev_index(x, y, c)
    sibling = (x, y, 1 - c)
    chips = [(x, y), (1 - x, y), (x, 1 - y), (1 - x, 1 - y)]
    peers = []
    for p in range(1, N_DEV):
        px, py, pc = (p >> 2) & 1, (p >> 1) & 1, p & 1
        peers.append((1 - x if px else x, 1 - y if py else y, 1 - c if pc else c))
    base_small = N_REDUCE_SEMS * n

    def remote(src, dst, k, to):
        return pltpu.make_async_remote_copy(src_ref=src, dst_ref=dst, send_sem=send_sems.at[k],
                                            recv_sem=recv_sems.at[k], device_id=to, device_id_type=MESH)

    def level1(a, q):
        return remote(ins[a].at[_dev_index(*chips[q], 1 - c)], r1[a].at[q], N_REDUCE_SEMS * a + q, sibling)

    def level2(a, k):
        half = ins[a].shape[-1] // 2
        left, right = pl.ds(0, half), pl.ds(half, half)
        nbr_x, nbr_y = (*chips[1], c), (*chips[2], c)
        src_slot, dst_slot, cols, to = [(0, 0, left, nbr_x), (1, 1, right, nbr_y), (2, 2, left, nbr_x),
                                        (2, 2, right, nbr_y), (0, 0, right, nbr_x), (1, 1, left, nbr_y)][k]
        return remote(s2[a].at[src_slot, :, cols], r2[a].at[dst_slot, :, cols], N_REDUCE_SEMS * a + 4 + k, to)

    to_sibling = [remote(sm[0], sm[2], base_small + 4 * i, sibling) for i, sm in enumerate(smalls)]
    to_chips = [[remote(sm[3], sm[4].at[j], base_small + 4 * i + 1 + j, (*chips[j + 1], c)) for j in range(3)]
                for i, sm in enumerate(smalls)]
    if rows is not None:
        rows_ref, land_ref, all_ref = rows
        base_rows = base_small + 4 * len(smalls)
        row_sends = [remote(rows_ref, land_ref.at[me], base_rows + k, to) for k, to in enumerate(peers)]

    def mine(a, q):
        buf, sems = own[a]
        return pltpu.make_async_copy(ins[a].at[_dev_index(*chips[q], c)], buf.at[q], sems.at[q])

    def start():
        for a in range(n):
            for q in range(4):
                level1(a, q).start()
            if own is not None:
                for q in (1, 2, 3, 0):
                    mine(a, q).start()
        for cp in to_sibling:
            cp.start()
        if rows is not None:
            for cp in row_sends:
                cp.start()
            land_ref[me] = rows_ref[...]

    def middle():
        for a in range(n):
            for q in (1, 2, 3, 0):
                level1(a, q).wait_recv()
                if own is None:
                    kept = ins[a][_dev_index(*chips[q], c)]
                else:
                    mine(a, q).wait()
                    kept = own[a][0][q]
                pair = kept.astype(F32) + r1[a][q].astype(F32)
                if q == 0:
                    gs[a][...] = pair
                else:
                    s2[a][q - 1] = pair.astype(BF16)
                    for k in ((0,), (1,), (2, 3))[q - 1]:
                        level2(a, k).start()
        for i, (small_ref, _, sm_sib, sm_chip, _) in enumerate(smalls):
            to_sibling[i].wait_recv()
            sm_chip[...] = small_ref[...] + sm_sib[...]
            for cp in to_chips[i]:
                cp.start()

    def fold():
        for a in range(n):
            half = ins[a].shape[-1] // 2
            level2(a, 3).wait_recv()
            s2[a][0, :, half:] = (s2[a][0, :, half:].astype(F32) + r2[a][2, :, half:].astype(F32)).astype(BF16)
            level2(a, 4).start()
            level2(a, 2).wait_recv()
            s2[a][1, :, :half] = (s2[a][1, :, :half].astype(F32) + r2[a][2, :, :half].astype(F32)).astype(BF16)
            level2(a, 5).start()

    def finish():
        for a in range(n):
            for k in (0, 1, 4, 5):
                level2(a, k).wait_recv()
            gs[a][...] = gs[a][...] + r2[a][0].astype(F32) + r2[a][1].astype(F32)
            for q in range(4):
                level1(a, q).wait_send()
            for k in range(6):
                level2(a, k).wait_send()
        for i, (_, total_ref, _, sm_chip, sm_recv) in enumerate(smalls):
            for cp in to_chips[i]:
                cp.wait_recv()
            total = None
            for ax in range(2):
                for ay in range(2):
                    dx, dy = x != ax, y != ay
                    term = jnp.where(dx, jnp.where(dy, sm_recv[2], sm_recv[0]), jnp.where(dy, sm_recv[1], sm_chip[...]))
                    total = term if total is None else total + term
            total_ref[...] = total
            for cp in [to_sibling[i]] + to_chips[i]:
                cp.wait_send()
        if rows is not None:
            for k, frm in enumerate(peers):
                remote(rows_ref, land_ref.at[_dev_index(*frm)], base_rows + k, frm).wait_recv()
            all_ref[...] = land_ref[...]
            for cp in row_sends:
                cp.wait_send()

    return start, middle, fold, finish


def _reduce_scratch(shard, smalls, rows=None):
    out = [pltpu.VMEM((lead,) + shard.shape[1:], BF16) for lead in (4, 3, 3)]
    for small in smalls:
        out += [pltpu.VMEM(small.shape, F32), pltpu.VMEM(small.shape, F32), pltpu.VMEM((3,) + small.shape, F32)]
    n_sems = N_REDUCE_SEMS + N_SMALL_SEMS * len(smalls)
    if rows is not None:
        out.append(pltpu.VMEM((N_DEV,) + rows.shape, F32))
        n_sems += N_ROWS_SEMS
    return out + [pltpu.SemaphoreType.DMA((n_sems,))] * 2


def _reduce_grads(gw_in, small, rows):
    def body(in_ref, small_ref, rows_ref, g_ref, total_ref, rows_all_ref,
             r1, s2, r2, sm_sib, sm_chip, sm_recv, rows_land, send_sems, recv_sems, kept, kept_sems):
        stages = _reduce_stages(
            [in_ref], [g_ref], [r1], [s2], [r2], [(small_ref, total_ref, sm_sib, sm_chip, sm_recv)],
            send_sems, recv_sems, rows=(rows_ref, rows_land, rows_all_ref), own=[(kept, kept_sems)])
        for stage in stages:
            stage()

    vmem = pl.BlockSpec(memory_space=pltpu.VMEM)
    return pl.pallas_call(
        body, name="reduce_grads",
        in_specs=[ANY, vmem, vmem], out_specs=[vmem, vmem, vmem],
        out_shape=[jax.ShapeDtypeStruct(gw_in.shape[1:], F32), jax.ShapeDtypeStruct(small.shape, F32),
                   jax.ShapeDtypeStruct((N_DEV,) + rows.shape, F32)],
        scratch_shapes=_reduce_scratch(gw_in, [small], rows)
        + [pltpu.VMEM((4,) + gw_in.shape[1:], BF16), pltpu.SemaphoreType.DMA((4,))],
        compiler_params=_params(),
    )(gw_in, small, rows)


def _dot3_rhs(a, b):
    a0, a1, a2 = _split3(a)
    b0, b1, b2 = _split3(b)
    return (_dot(a0, b0) + (_dot(a0, b1) + _dot(a1, b0))
            + (_dot(a0, b2) + _dot(a1, b1) + _dot(a2, b0)))


def _gather_and_ada(c, w_in_rows, w_ada):
    cols = w_ada.shape[1]
    shard = w_in_rows.shape[0]

    def body(c_ref, w_ref, wa_ref, w_main_ref, w_f_ref, sc_ref, ada_ref,
             w_all_ref, w_f32, c_land, part, ada_land, send_sems, recv_sems, local_sem, x_send, x_recv):
        x, y, cc = _mesh_pos()
        me = _dev_index(x, y, cc)
        peers = []
        for p in range(1, N_DEV):
            px, py, pc = (p >> 2) & 1, (p >> 1) & 1, p & 1
            peers.append((1 - x if px else x, 1 - y if py else y, 1 - cc if pc else cc))

        def remote(src, dst, k, to):
            return pltpu.make_async_remote_copy(src_ref=src, dst_ref=dst, send_sem=x_send.at[k], recv_sem=x_recv.at[k],
                                                device_id=to, device_id_type=MESH)

        c_sends = [remote(c_ref, c_land.at[me], k, to) for k, to in enumerate(peers)]
        for cp in c_sends:
            cp.start()
        start, relay_near, relay_far, finish = _gather_stages(w_ref, w_all_ref, send_sems, recv_sems, local_sem.at[0])
        start()
        c_land[me] = c_ref[...]
        for k, frm in enumerate(peers):
            remote(c_ref, c_land.at[_dev_index(*frm)], k, frm).wait_recv()
        c_all = jnp.concatenate([c_land[b] for b in range(N_DEV)], axis=0)
        sc = c_all * _sigmoid(c_all)
        sc_ref[...] = sc
        rows = _dot3_rhs(sc, wa_ref[...])
        for b in range(N_DEV):
            part[b] = rows[b:b + 1, :]
        a_sends = [remote(part.at[_dev_index(*to)], ada_land.at[me], 7 + k, to) for k, to in enumerate(peers)]
        for cp in a_sends:
            cp.start()
        ada_land[me] = part[me]
        for k, frm in enumerate(peers):
            remote(part.at[0], ada_land.at[_dev_index(*frm)], 7 + k, frm).wait_recv()
        ada_ref[...] = ada_land[...]

        relay_near()
        relay_far()
        finish()
        for cp in c_sends + a_sends:
            cp.wait_send()

        for slot in range(N_DEV):
            w_f32[slot * shard:(slot + 1) * shard, :] = w_all_ref[slot].astype(F32)
        w_main_ref[0:F_LO, :] = w_f32[0:F_LO, :].astype(BF16)
        w_main_ref[F_LO:N_MAIN, :] = w_f32[F_HI:D_IN, :].astype(BF16)
        w_f_ref[...] = jnp.concatenate(
            [w_f32[F_LO:F_HI, :], jnp.zeros((LANE - N_HEADS, D), F32)], axis=0).astype(BF16)

    vmem = pl.BlockSpec(memory_space=pltpu.VMEM)
    return pl.pallas_call(
        body, name="gather_weights",
        in_specs=[vmem, ANY, vmem], out_specs=[vmem, vmem, vmem, vmem],
        out_shape=[jax.ShapeDtypeStruct((N_MAIN, D), BF16), jax.ShapeDtypeStruct((LANE, D), BF16),
                   jax.ShapeDtypeStruct((N_DEV, D), F32), jax.ShapeDtypeStruct((N_DEV, 1, cols), F32)],
        scratch_shapes=[pltpu.VMEM((N_DEV,) + w_in_rows.shape, BF16), pltpu.VMEM((D_IN, D), F32),
                        pltpu.VMEM((N_DEV, 1, D), F32), pltpu.VMEM((N_DEV, 1, cols), F32), pltpu.VMEM((N_DEV, 1, cols), F32),
                        pltpu.SemaphoreType.DMA((N_GATHER_SEMS,)), pltpu.SemaphoreType.DMA((N_GATHER_SEMS,)),
                        pltpu.SemaphoreType.DMA((1,)),
                        pltpu.SemaphoreType.DMA((14,)), pltpu.SemaphoreType.DMA((14,))],
        compiler_params=_params(),
    )(c, w_in_rows, w_ada)


def _inproj_forward(x, mod, w_main, w_f, b_main, b_f, tile):
    seq = x.shape[0]
    nt = seq // tile

    def body(x_ref, mod_ref, w_ref, wf_ref, b_ref, bf_ref,
             qp_ref, kp_ref, vp_ref, f_ref, p_ref, ga_ref, gp_ref, u_ref, carry_ref):
        i = pl.program_id(0)

        @pl.when(i == 0)
        def _():
            carry_ref[...] = jnp.zeros_like(carry_ref)

        u = x_ref[...] * mod_ref[0:1, :] + mod_ref[1:2, :]
        ub = u.astype(BF16)
        u_ref[...] = ub

        f = _dot_nt(ub, wf_ref[...]) + bf_ref[...]
        f_ref[...] = f
        lane = _lanes((tile, LANE))
        log_f = jnp.where(lane < N_HEADS, jnp.minimum(f, 0.0) - jnp.log(1.0 + jnp.exp(-jnp.abs(f))), 0.0)
        row = lax.broadcasted_iota(jnp.int32, (tile, tile), 0)
        col = lax.broadcasted_iota(jnp.int32, (tile, tile), 1)
        tri = (row >= col).astype(BF16)
        cum = _dot3(tri, log_f) + carry_ref[0:1, :]
        carry_ref[0:1, :] = cum[tile - 1:tile, :]
        cq = [part.astype(F32) for part in _split3(cum)]
        ck = [part.astype(F32) for part in _split3(-cum)]

        def proj(chunk):
            cols = pl.ds(chunk * COL_CHUNK, COL_CHUNK)
            return _dot_nt(ub, w_ref[cols, :]) + b_ref[:, cols]

        def head_tiles(r):
            for pair in range(N_HEADS // 2):
                both = r[:, pair * LANE:(pair + 1) * LANE]
                yield 2 * pair, both
                yield 2 * pair + 1, pltpu.roll(both, HEAD_DIM, 1)

        for h, val in head_tiles(proj(0)):
            extra = jnp.where((lane >= L_CK) & (lane < L_CK + 3), 1.0, 0.0)
            extra = _place3(lane, L_CQ, [part[:, h:h + 1] for part in cq], extra)
            qp_ref[h] = jnp.where(lane < HEAD_DIM, val * Q_SCALE, extra).astype(BF16)
        for h, val in head_tiles(proj(1)):
            ones = ((lane >= L_CQ) & (lane < L_CQ + 3)) | ((lane >= L_LSE) & (lane < L_LSE + 3))
            extra = _place3(lane, L_CK, [part[:, h:h + 1] for part in ck], jnp.where(ones, 1.0, 0.0))
            kp_ref[h] = jnp.where(lane < HEAD_DIM, val, extra).astype(BF16)
        for h, val in head_tiles(proj(2)):
            extra = jnp.where((lane >= HEAD_DIM) & (lane < HEAD_DIM + 3), -1.0, 0.0)
            vp_ref[h] = jnp.where(lane < HEAD_DIM, val, extra).astype(BF16)
        p_ref[...] = proj(3)
        ga_ref[...] = proj(4)
        gp_ref[...] = proj(5)

    head_block = pl.BlockSpec((N_HEADS, tile, LANE), lambda i: (0, i, 0))
    tok = lambda width: pl.BlockSpec((tile, width), lambda i: (i, 0))
    whole = lambda a: pl.BlockSpec(a.shape, lambda i: (0,) * a.ndim)
    padded = jax.ShapeDtypeStruct((N_HEADS, seq, LANE), BF16)
    half = jax.ShapeDtypeStruct((seq, D_ATT), F32)
    return pl.pallas_call(
        body, name="inproj_forward", grid=(nt,),
        in_specs=[tok(D), whole(mod), whole(w_main), whole(w_f), whole(b_main), whole(b_f)],
        out_specs=[head_block, head_block, head_block, tok(LANE), tok(D_POOL), tok(D_ATT), tok(D_POOL),
                   tok(D)],
        out_shape=[padded, padded, padded, jax.ShapeDtypeStruct((seq, LANE), F32), half, half, half,
                   jax.ShapeDtypeStruct((seq, D), BF16)],
        scratch_shapes=[pltpu.VMEM((8, LANE), F32)],
        compiler_params=_params(("arbitrary",)),
    )(x, mod, w_main, w_f, b_main, b_f)


def _attention_forward(qp, kp, vp, w_out, tile):
    seq = qp.shape[1]
    nb = seq // tile
    steps = (N_HEADS // 2) * nb

    def body(q_ref, k_ref, v_ref, wo_ref, att_ref, q2t_ref, wo_all_ref, s_a, s_b, m_ref, acc_ref,
             send_sems, recv_sems, local_sem):
        step = pl.program_id(0) * nb + pl.program_id(1)
        start, relay_near, relay_far, finish = _gather_stages(wo_ref, wo_all_ref, send_sems, recv_sems, local_sem.at[0])
        pl.when(step == 0)(start)
        pl.when(step == steps // 4)(relay_near)
        pl.when(step == (3 * steps) // 4)(relay_far)

        i = pl.program_id(1)
        sub = lax.broadcasted_iota(jnp.int32, (LANE, tile), 0)
        row = lax.broadcasted_iota(jnp.int32, (tile, tile), 0)
        col = lax.broadcasted_iota(jnp.int32, (tile, tile), 1)
        q = [q_ref[0], q_ref[1]]

        def scores(buf, kb):
            rows = pl.ds(pl.multiple_of(kb * tile, tile), tile)
            for hh in range(2):
                buf[hh] = _dot_nt(k_ref[hh, rows, :], q[hh])

        def absorb(buf, kb, masked):
            rows = pl.ds(pl.multiple_of(kb * tile, tile), tile)
            for hh in range(2):
                m = m_ref[hh, 0:1, :]
                s = buf[hh]
                if masked:
                    s = jnp.where(row <= col, s, -1e30)
                m_new = jnp.maximum(m, jnp.max(s, axis=0, keepdims=True))
                p = jnp.exp(s - m_new).astype(BF16)
                acc_ref[hh] = jnp.exp(m - m_new) * acc_ref[hh] + _dot_tn(v_ref[hh, rows, :], p)
                m_ref[hh, 0:1, :] = m_new

        def two_blocks(j, _):
            scores(s_b, 2 * j + 1)
            absorb(s_a, 2 * j, False)
            scores(s_a, 2 * j + 2)
            absorb(s_b, 2 * j + 1, False)
            return 0

        def last_block():
            absorb(s_a, i, True)

        def last_two_blocks():
            scores(s_b, i)
            absorb(s_a, i - 1, False)
            absorb(s_b, i, True)

        scores(s_a, 0)
        m_ref[...] = jnp.full(m_ref.shape, -1e30, F32)
        acc_ref[...] = jnp.zeros_like(acc_ref)
        lax.fori_loop(0, i // 2, two_blocks, 0)
        lax.cond(i % 2 == 0, last_block, last_two_blocks)
        outs = []
        for hh in range(2):
            m, acc = m_ref[hh, 0:1, :], acc_ref[hh]
            l = -acc[HEAD_DIM:HEAD_DIM + 1, :]
            outs.append((acc / l)[:HEAD_DIM, :])
            neg_lse = [part.astype(F32) for part in _split3(-(m + jnp.log(l)))]
            q2t_ref[hh] = _place3(sub, L_LSE, neg_lse, q[hh].astype(F32).T).astype(BF16)
        att_ref[...] = jnp.concatenate(outs, axis=0).T
        pl.when(step == steps - 1)(finish)

    pair = pl.BlockSpec((2, tile, LANE), lambda hp, i: (hp, i, 0))
    full = pl.BlockSpec((2, seq, LANE), lambda hp, i: (hp, 0, 0))
    return pl.pallas_call(
        body, name="attention_forward", grid=(N_HEADS // 2, nb),
        in_specs=[pair, full, full, ANY],
        out_specs=[pl.BlockSpec((tile, LANE), lambda hp, i: (i, hp)),
                   pl.BlockSpec((2, LANE, tile), lambda hp, i: (hp, 0, i)), ANY],
        out_shape=[jax.ShapeDtypeStruct((seq, D_ATT), F32),
                   jax.ShapeDtypeStruct((N_HEADS, LANE, seq), BF16),
                   jax.ShapeDtypeStruct((N_DEV,) + w_out.shape, w_out.dtype)],
        scratch_shapes=[pltpu.VMEM((2, tile, tile), F32), pltpu.VMEM((2, tile, tile), F32),
                        pltpu.VMEM((2, 8, tile), F32), pltpu.VMEM((2, LANE, tile), F32),
                        pltpu.SemaphoreType.DMA((N_GATHER_SEMS,)), pltpu.SemaphoreType.DMA((N_GATHER_SEMS,)),
                        pltpu.SemaphoreType.DMA((1,))],
        compiler_params=_params(("arbitrary", "arbitrary")),
    )(qp, kp, vp, w_out)


def _window_sum(x, halo, window, transposed):
    tile = x.shape[0]

    def split_cat(a):
        hi = a.astype(BF16)
        return jnp.concatenate([hi, (a - hi.astype(F32)).astype(BF16)], axis=1)

    def fold(r):
        return r[:, :LANE] + r[:, LANE:]

    r = lax.broadcasted_iota(jnp.int32, (tile, tile), 0)
    c = lax.broadcasted_iota(jnp.int32, (tile, tile), 1)
    rh = lax.broadcasted_iota(jnp.int32, (HALO, HALO), 0)
    ch = lax.broadcasted_iota(jnp.int32, (HALO, HALO), 1)
    if not transposed:
        band = (c <= r) & (r - c < window)
        edge = (rh + HALO - ch) < window
    else:
        band = (r <= c) & (c - r < window)
        edge = (HALO + ch - rh) < window
    out = fold(_dot(band.astype(BF16), split_cat(x)))
    reach = fold(_dot(edge.astype(BF16), split_cat(halo)))
    if not transposed:
        return jnp.concatenate([out[:HALO] + reach, out[HALO:]], axis=0)
    return jnp.concatenate([out[:tile - HALO], out[tile - HALO:] + reach], axis=0)


def _silu_parts(g):
    sig = _sigmoid(g)
    return g * sig, sig * (1.0 + g * (1.0 - sig))


def _middle(x, tgt, att, g_att, g_pool, p, vecs, pool_vecs, w_out, w_pool, tile):
    seq = x.shape[0]
    nt = seq // tile
    halo_blocks = tile // HALO

    def body(x_ref, tgt_ref, att_ref, ga_ref, gp_ref, p_ref, ph_ref, vec_ref, pvec_ref, wo_ref, wp_ref,
             dxa_ref, do2_ref, dga_ref, dgp_ref, dpooled_ref, gwo_ref, dwp_ref, dvec_ref, dwo_ref, dpvec_ref):
        i = pl.program_id(0)

        @pl.when(i == 0)
        def _():
            dwo_ref[...] = jnp.zeros_like(dwo_ref)
            dwp_ref[...] = jnp.zeros_like(dwp_ref)
            dvec_ref[...] = jnp.zeros_like(dvec_ref)
            dpvec_ref[...] = jnp.zeros_like(dpvec_ref)

        gate, b_out, ln_g, ln_b = (vec_ref[k:k + 1, :] for k in range(4))
        b_pool, pool_scale = pvec_ref[0:1, :], pvec_ref[1:2, :]
        x = x_ref[...]
        p = p_ref[...]
        p_halo = ph_ref[...] * jnp.where(i > 0, 1.0, 0.0)
        pos = i * tile + lax.broadcasted_iota(jnp.int32, (tile, 1), 0) + 1

        pooled, mixed = [], []
        for g, window in enumerate(POOL_WINDOWS):
            cols = slice(g * GROUP_DIM, (g + 1) * GROUP_DIM)
            wsum = _window_sum(p[:, cols], p_halo[:, cols], window, False)
            count = jnp.minimum(pos, window).astype(F32)
            pooled.append(wsum / count - p[:, cols])
            mixed.append(_dot(pooled[g].astype(BF16), wp_ref[g]) + b_pool[:, cols])
        mixed = jnp.concatenate(mixed, axis=1)
        pool = mixed * pool_scale

        att = att_ref[...]
        g_att, g_pool = ga_ref[...], gp_ref[...]
        silu_a, dsilu_a = _silu_parts(g_att)
        silu_p, dsilu_p = _silu_parts(g_pool)
        y_in = jnp.concatenate([att * silu_a, pool * silu_p], axis=1)
        y = _dot(y_in.astype(BF16), wo_ref[...]) + b_out
        h = ALPHA * x + gate * y
        mu = jnp.mean(h, axis=1, keepdims=True)
        hc = h - mu
        var = jnp.mean(hc * hc, axis=1, keepdims=True)
        rstd = lax.rsqrt(var + LN_EPS)
        yhat = hc * rstd
        diff = yhat * ln_g + ln_b - tgt_ref[...]
        loss_rows = jnp.sum(diff * diff, axis=1, keepdims=True)
        d_out = diff * (1.0 / D)

        d_yhat = d_out * ln_g
        dh = rstd * (d_yhat - jnp.mean(d_yhat, axis=1, keepdims=True)
                     - yhat * jnp.mean(d_yhat * yhat, axis=1, keepdims=True))
        dxa_ref[...] = ALPHA * dh
        dy = dh * gate
        dyb = dy.astype(BF16)
        lane = _lanes((1, D))
        loss_row = jnp.where(lane == 0, (0.5 / D) * jnp.sum(loss_rows, axis=0, keepdims=True), 0.0)
        dvec_ref[5:6, :] += jnp.sum(dh * y, axis=0, keepdims=True)
        dvec_ref[0:1, :] += jnp.sum(dy, axis=0, keepdims=True)
        dvec_ref[1:2, :] += jnp.sum(d_out * yhat, axis=0, keepdims=True)
        dvec_ref[2:3, :] += jnp.sum(d_out, axis=0, keepdims=True)
        dvec_ref[4:5, :] += loss_row

        dwo_ref[...] += _dot(y_in.T.astype(BF16), dyb)
        d_yin = _dot_nt(dyb, wo_ref[...])
        d_a, d_pl = d_yin[:, :D_ATT], d_yin[:, D_ATT:]
        d_att = d_a * silu_a
        d_att_t = d_att.T
        prod_t = (d_att * att).T
        sub = lax.broadcasted_iota(jnp.int32, (HEAD_DIM, tile), 0)
        for h in range(N_HEADS):
            rows = slice(h * HEAD_DIM, (h + 1) * HEAD_DIM)
            delta = jnp.sum(prod_t[rows], axis=0, keepdims=True)
            extra = _place3(sub, 0, [part.astype(F32) for part in _split3(delta)], 0.0)
            do2_ref[h] = jnp.concatenate([d_att_t[rows], extra], axis=0).astype(BF16)
        dga_ref[...] = d_a * att * dsilu_a
        dgp_ref[...] = d_pl * pool * dsilu_p
        d_pool = d_pl * silu_p
        d_mixed = d_pool * pool_scale
        dpvec_ref[0:1, :] += jnp.sum(d_mixed, axis=0, keepdims=True)
        dpvec_ref[1:2, :] += jnp.sum(d_pool * mixed, axis=0, keepdims=True)
        d_pooled = []
        for g in range(len(POOL_WINDOWS)):
            cols = slice(g * GROUP_DIM, (g + 1) * GROUP_DIM)
            dmb = d_mixed[:, cols].astype(BF16)
            dwp_ref[g] += _dot(pooled[g].T.astype(BF16), dmb)
            d_pooled.append(_dot_nt(dmb, wp_ref[g]))
        dpooled_ref[...] = jnp.concatenate(d_pooled, axis=1)

        @pl.when(i == nt - 1)
        def _():
            gwo_ref[...] = dwo_ref[...].astype(BF16)
            dvec_ref[3:4, :] = jnp.concatenate([dpvec_ref[0:1, :], dpvec_ref[1:2, :]], axis=1)

    tok = lambda width: pl.BlockSpec((tile, width), lambda i: (i, 0))
    whole = lambda a: pl.BlockSpec(a.shape, lambda i: (0,) * a.ndim)
    halo = pl.BlockSpec((HALO, D_POOL), lambda i: (jnp.maximum(i * halo_blocks - 1, 0), 0))
    half = jax.ShapeDtypeStruct((seq, D_ATT), F32)
    outs = [jax.ShapeDtypeStruct((seq, D), F32), jax.ShapeDtypeStruct((N_HEADS, LANE, seq), BF16), half, half, half,
            jax.ShapeDtypeStruct(w_out.shape, BF16), jax.ShapeDtypeStruct(w_pool.shape, F32),
            jax.ShapeDtypeStruct(vecs.shape, F32)]
    return pl.pallas_call(
        body, name="middle", grid=(nt,),
        in_specs=[tok(D), tok(D), tok(D_ATT), tok(D_ATT), tok(D_POOL), tok(D_POOL), halo,
                  whole(vecs), whole(pool_vecs), whole(w_out), whole(w_pool)],
        out_specs=[tok(D), pl.BlockSpec((N_HEADS, LANE, tile), lambda i: (0, 0, i)),
                   tok(D_ATT), tok(D_POOL), tok(D_POOL),
                   whole(w_out), whole(w_pool), whole(vecs)],
        out_shape=outs,
        scratch_shapes=[pltpu.VMEM(w_out.shape, F32), pltpu.VMEM(pool_vecs.shape, F32)],
        compiler_params=_params(("arbitrary",)),
    )(x, tgt, att, g_att, g_pool, p, p, vecs, pool_vecs, w_out, w_pool)


def _attention_backward(q2t, kp, vp, do2t, gw_out, vecs, pool, tile):
    seq = kp.shape[1]
    nb = seq // tile
    last = N_HEADS // 2 - 1

    def body(qt_ref, k_ref, v_ref, dot_ref, gwo_hbm, vecs_hbm, pool_hbm,
             dq_ref, dk_ref, dv_ref, dcum_ref, g_out_ref, vecs_sum_ref, pool_sum_ref,
             dq_acc, dk_acc, dv_acc, gwo_ref, vecs_ref, pool_ref,
             r1, s2, r2, v_sib, v_chip, v_recv, p_sib, p_chip, p_recv, send_sems, recv_sems):
        hp = pl.program_id(0)
        start, middle, fold, finish = _reduce_stages(
            [gwo_ref], [g_out_ref], [r1], [s2], [r2],
            [(vecs_ref, vecs_sum_ref, v_sib, v_chip, v_recv), (pool_ref, pool_sum_ref, p_sib, p_chip, p_recv)],
            send_sems, recv_sems)

        @pl.when(hp == 0)
        def _():
            pltpu.sync_copy(gwo_hbm, gwo_ref)
            pltpu.sync_copy(vecs_hbm, vecs_ref)
            pltpu.sync_copy(pool_hbm, pool_ref)
            start()

        pl.when(hp == 1)(middle)
        pl.when(hp == 2)(fold)

        row = lax.broadcasted_iota(jnp.int32, (tile, tile), 0)
        col = lax.broadcasted_iota(jnp.int32, (tile, tile), 1)
        dq_acc[...] = jnp.zeros_like(dq_acc)

        def kv_block(kb, _):
            krows = pl.ds(pl.multiple_of(kb * tile, tile), tile)
            k = [k_ref[hh, krows, :] for hh in range(2)]
            v = [v_ref[hh, krows, :] for hh in range(2)]
            k_t = [k[hh].T for hh in range(2)]

            def q_block(qb, masked):
                qcols = pl.ds(pl.multiple_of(qb * tile, tile), tile)
                for hh in range(2):
                    q_t = qt_ref[hh, :, qcols]
                    do_t = dot_ref[hh, :, qcols]
                    s_t = _dot(k[hh], q_t)
                    if masked:
                        s_t = jnp.where(row <= col, s_t, -1e30)
                    p_t = jnp.exp(s_t)
                    ds_t = (p_t * _dot(v[hh], do_t)).astype(BF16)
                    dv_new = _dot_nt(do_t, p_t.astype(BF16))
                    dk_new = _dot_nt(q_t, ds_t)
                    if masked:
                        dv_acc[hh], dk_acc[hh] = dv_new, dk_new
                    else:
                        dv_acc[hh] += dv_new
                        dk_acc[hh] += dk_new
                    dq_acc[hh, :, qcols] += _dot(k_t[hh], ds_t)

            q_block(kb, True)

            def two_later_blocks(j, _):
                q_block(kb + 1 + 2 * j, False)
                q_block(kb + 2 + 2 * j, False)
                return 0

            later = nb - 1 - kb
            lax.fori_loop(0, later // 2, two_later_blocks, 0)
            pl.when(later % 2 == 1)(lambda: q_block(nb - 1, False))
            for hh in range(2):
                dk = dk_acc[hh]
                dk_ref[hh, :, krows] = dk.astype(BF16)
                dv_ref[hh, :, krows] = dv_acc[hh].astype(BF16)
                dcum_ref[hh, :, krows] = -dk[L_CK:L_CK + 1, :]
            return 0

        lax.fori_loop(0, nb, kv_block, 0)
        for hh in range(2):
            dq = dq_acc[hh]
            dcum_ref[hh] += dq[L_CQ:L_CQ + 1, :]
            dq_ref[hh] = (dq * Q_SCALE).astype(BF16)
        pl.when(hp == last)(finish)

    pair = pl.BlockSpec((2, seq, LANE), lambda hp: (hp, 0, 0))
    pair_t = pl.BlockSpec((2, LANE, seq), lambda hp: (hp, 0, 0))
    whole = lambda shape: pl.BlockSpec(shape, lambda hp: (0,) * len(shape))
    grad = jax.ShapeDtypeStruct((N_HEADS, LANE, seq), BF16)
    return pl.pallas_call(
        body, name="attention_backward", grid=(N_HEADS // 2,),
        in_specs=[pair_t, pair, pair, pair_t, ANY, ANY, ANY],
        out_specs=[pair_t, pair_t, pair_t, pl.BlockSpec((2, 1, seq), lambda hp: (hp, 0, 0)),
                   whole(gw_out.shape[1:]), whole(vecs.shape), whole(pool.shape)],
        out_shape=[grad, grad, grad, jax.ShapeDtypeStruct((N_HEADS, 1, seq), F32),
                   jax.ShapeDtypeStruct(gw_out.shape[1:], F32), jax.ShapeDtypeStruct(vecs.shape, F32),
                   jax.ShapeDtypeStruct(pool.shape, F32)],
        scratch_shapes=[pltpu.VMEM((2, LANE, seq), F32), pltpu.VMEM((2, LANE, tile), F32),
                        pltpu.VMEM((2, LANE, tile), F32), pltpu.VMEM(gw_out.shape, BF16),
                        pltpu.VMEM(vecs.shape, F32), pltpu.VMEM(pool.shape, F32)]
        + _reduce_scratch(gw_out, [vecs, pool]),
        compiler_params=_params(("arbitrary",)),
    )(q2t, kp, vp, do2t, gw_out, vecs, pool)


def _inproj_backward(dqp, dkp, dvp, d_cum, f, d_pooled, d_ga, d_gp, x, dxa, u, mod, w_main, w_f, tile):
    seq = x.shape[0]
    nt = seq // tile
    halo_blocks = tile // HALO

    def body(dq_ref, dk_ref, dv_ref, dcum_ref, f_ref, dpo_ref, dph_ref, dga_ref, dgp_ref, x_ref, dxa_ref, u_ref,
             mod_ref, w_ref, wf_ref,
             dx_ref, dproj_ref, dwf_ref, db_ref, dbf_ref, dmod_ref, carry_ref):
        step = pl.program_id(0)
        i = nt - 1 - step

        @pl.when(step == 0)
        def _():
            carry_ref[...] = jnp.zeros_like(carry_ref)
            dwf_ref[...] = jnp.zeros_like(dwf_ref)
            db_ref[...] = jnp.zeros_like(db_ref)
            dbf_ref[...] = jnp.zeros_like(dbf_ref)
            dmod_ref[...] = jnp.zeros_like(dmod_ref)

        ones = jnp.ones((8, tile), BF16)

        def emit(chunk, val):
            cols = pl.ds(chunk * COL_CHUNK, COL_CHUNK)
            db_ref[0:1, cols] += jnp.sum(val, axis=0, keepdims=True)
            vb = val.astype(BF16)
            dproj_ref[:, pl.ds((chunk - 3) * COL_CHUNK, COL_CHUNK)] = vb
            return _dot(vb, w_ref[cols, :])

        d_u = jnp.zeros((tile, D), F32)
        for chunk, ref in enumerate((dq_ref, dk_ref, dv_ref)):
            cols = pl.ds(chunk * COL_CHUNK, COL_CHUNK)
            val_t = ref[:, 0:HEAD_DIM, :].reshape(COL_CHUNK, tile)
            db_ref[:, cols] += _dot_nt(ones, val_t)
            d_u += _dot_tn(val_t, w_ref[cols, :])

        d_pooled = dpo_ref[...]
        d_halo = dph_ref[...] * jnp.where(i < nt - 1, 1.0, 0.0)
        pos = i * tile + lax.broadcasted_iota(jnp.int32, (tile, 1), 0) + 1
        d_p = []
        for g, window in enumerate(POOL_WINDOWS):
            cols = slice(g * GROUP_DIM, (g + 1) * GROUP_DIM)
            scaled = d_pooled[:, cols] / jnp.minimum(pos, window).astype(F32)
            d_p.append(_window_sum(scaled, d_halo[:, cols] * (1.0 / window), window, True) - d_pooled[:, cols])
        d_u += emit(3, jnp.concatenate(d_p, axis=1))
        d_u += emit(4, dga_ref[...])
        d_u += emit(5, dgp_ref[...])

        row = lax.broadcasted_iota(jnp.int32, (tile, tile), 0)
        col = lax.broadcasted_iota(jnp.int32, (tile, tile), 1)
        later = (row >= col).astype(BF16)
        d_logf = sum(_dot(part, later) for part in _split3(dcum_ref[:, 0, :])) + carry_ref[:, 0:1]
        carry_ref[:, 0:1] = d_logf[:, 0:1]
        d_f = d_logf * _sigmoid(-f_ref[...].T[0:N_HEADS, :])
        d_f = jnp.concatenate([d_f, jnp.zeros((LANE - N_HEADS, tile), F32)], axis=0)
        dbf_ref[...] += sum(_dot_nt(ones, part) for part in _split3(d_f))
        d_fb = d_f.astype(BF16)
        d_u += _dot_tn(d_fb, wf_ref[...])
        dwf_ref[...] += _dot(d_fb, u_ref[...])

        x = x_ref[...]
        dx_ref[...] = dxa_ref[...] + d_u * mod_ref[0:1, :]
        dmod_ref[0:1, :] += jnp.sum(d_u * x, axis=0, keepdims=True)
        dmod_ref[1:2, :] += jnp.sum(d_u, axis=0, keepdims=True)

    rev = lambda step: nt - 1 - step
    tok = lambda width: pl.BlockSpec((tile, width), lambda s: (rev(s), 0))
    head_block = pl.BlockSpec((N_HEADS, LANE, tile), lambda s: (0, 0, rev(s)))
    whole = lambda a: pl.BlockSpec(a.shape, lambda s: (0,) * a.ndim)
    halo = pl.BlockSpec((HALO, D_POOL), lambda s: (jnp.minimum((rev(s) + 1) * halo_blocks, seq // HALO - 1), 0))
    small = lambda width: jax.ShapeDtypeStruct((8, width), F32)
    n_rest = N_MAIN - OFF_P
    return pl.pallas_call(
        body, name="inproj_backward", grid=(nt,),
        in_specs=[head_block, head_block, head_block, pl.BlockSpec((N_HEADS, 1, tile), lambda s: (0, 0, rev(s))),
                  tok(LANE), tok(D_POOL), halo, tok(D_ATT), tok(D_POOL),
                  tok(D), tok(D), tok(D),
                  whole(mod), whole(w_main), whole(w_f)],
        out_specs=[tok(D), tok(n_rest), pl.BlockSpec((LANE, D), lambda s: (0, 0)),
                   pl.BlockSpec((8, N_MAIN), lambda s: (0, 0)), pl.BlockSpec((8, LANE), lambda s: (0, 0)),
                   pl.BlockSpec((8, D), lambda s: (0, 0))],
        out_shape=[jax.ShapeDtypeStruct((seq, D), F32), jax.ShapeDtypeStruct((seq, n_rest), BF16),
                   jax.ShapeDtypeStruct((LANE, D), F32), small(N_MAIN), small(LANE), small(D)],
        scratch_shapes=[pltpu.VMEM((8, LANE), F32)],
        compiler_params=_params(("arbitrary",)),
    )(dqp, dkp, dvp, d_cum, f, d_pooled, d_pooled, d_ga, d_gp, x, dxa, u, mod, w_main, w_f)


def _weight_grads(dq_t, dk_t, dv_t, dw_f, dproj, u, k_tile):
    seq = u.shape[0]
    nk = seq // k_tile
    rows = N_HEADS * HEAD_DIM

    def body(dq_ref, dk_ref, dv_ref, dwf_ref, dp_ref, u_ref, out_ref, acc_ref):
        k = pl.program_id(0)

        @pl.when(k == 0)
        def _():
            acc_ref[...] = jnp.zeros_like(acc_ref)

        tokens = u_ref[...]
        for j, ref in enumerate((dq_ref, dk_ref, dv_ref)):
            acc_ref[pl.ds(j * rows, rows), :] += _dot(ref[...].reshape(rows, k_tile), tokens)
        for j in range(dproj.shape[1] // COL_CHUNK):
            cols = pl.ds(j * COL_CHUNK, COL_CHUNK)
            acc_ref[pl.ds(F_HI + j * COL_CHUNK, COL_CHUNK), :] += _dot_tn(dp_ref[:, cols], tokens)

        @pl.when(k == nk - 1)
        def _():
            acc_ref[F_LO:F_HI, :] = dwf_ref[0:N_HEADS, :]
            for slot in range(N_DEV):
                out_ref[slot] = acc_ref[slot * shard:(slot + 1) * shard, :].astype(BF16)

    shard = D_IN // N_DEV
    heads = pl.BlockSpec((N_HEADS, HEAD_DIM, k_tile), lambda k: (0, 0, k))
    return pl.pallas_call(
        body, name="weight_grads", grid=(nk,),
        in_specs=[heads, heads, heads, pl.BlockSpec(dw_f.shape, lambda k: (0, 0)),
                  pl.BlockSpec((k_tile, dproj.shape[1]), lambda k: (k, 0)), pl.BlockSpec((k_tile, D), lambda k: (k, 0))],
        out_specs=pl.BlockSpec((N_DEV, shard, D), lambda k: (0, 0, 0)),
        out_shape=jax.ShapeDtypeStruct((N_DEV, shard, D), BF16),
        scratch_shapes=[pltpu.VMEM((D_IN, D), F32)],
        compiler_params=_params(("arbitrary",)),
    )(dq_t, dk_t, dv_t, dw_f, dproj, u)


def _adamw(w, g, m, v):
    m = ADAM_B1 * m + (1.0 - ADAM_B1) * g
    v = ADAM_B2 * v + (1.0 - ADAM_B2) * (g * g)
    m_hat = m / (1.0 - ADAM_B1 ** ADAM_STEP)
    v_hat = v / (1.0 - ADAM_B2 ** ADAM_STEP)
    delta = -ADAM_LR * (m_hat / (jnp.sqrt(v_hat) + ADAM_EPS) + ADAM_WD * w)
    return delta, m, v


SUBLANES = 8


def _adamw_packed(g, w, m, v, name, chunks=4):
    rows, cols = g.shape
    per_row = cols // LANE
    assert cols % LANE == 0 and per_row == SUBLANES and w.shape == (rows * per_row, LANE)
    step = -(-rows // (chunks * SUBLANES)) * SUBLANES
    bounds = [(r0, min(r0 + step, rows)) for r0 in range(0, rows, step)]

    def body(g_hbm, w_hbm, m_hbm, v_hbm, og_hbm, od_hbm, om_hbm, ov_hbm, g_buf, in_buf, out_buf, in_sems, out_sems):
        def copies_in(c):
            r0, r1 = bounds[c]
            packed = slice(r0 * per_row, r1 * per_row)
            return [pltpu.make_async_copy(g_hbm.at[r0:r1], g_buf.at[r0:r1], in_sems.at[c, 0])] + [
                pltpu.make_async_copy(src.at[packed], in_buf.at[i, packed], in_sems.at[c, 1 + i])
                for i, src in enumerate((w_hbm, m_hbm, v_hbm))]

        def copies_out(c):
            r0, r1 = bounds[c]
            packed = slice(r0 * per_row, r1 * per_row)
            return [pltpu.make_async_copy(out_buf.at[i, packed], dst.at[packed], out_sems.at[c, i])
                    for i, dst in enumerate((og_hbm, od_hbm, om_hbm, ov_hbm))]

        for c in range(len(bounds)):
            for cp in copies_in(c):
                cp.start()
        for c, (r0, r1) in enumerate(bounds):
            for cp in copies_in(c):
                cp.wait()
            for j in range(per_row):
                lanes = pl.ds(r0 * per_row + j, r1 - r0, stride=per_row)
                g_part = g_buf[r0:r1, j * LANE:(j + 1) * LANE]
                results = _adamw(in_buf[0, lanes, :], g_part, in_buf[1, lanes, :], in_buf[2, lanes, :])
                for i, val in enumerate((g_part,) + results):
                    out_buf[i, lanes, :] = val
            for cp in copies_out(c):
                cp.start()
        for c in range(len(bounds)):
            for cp in copies_out(c):
                cp.wait()

    shape = jax.ShapeDtypeStruct(w.shape, F32)
    return pl.pallas_call(
        body, name=name,
        in_specs=[ANY] * 4, out_specs=[ANY] * 4, out_shape=[shape] * 4,
        scratch_shapes=[pltpu.VMEM(g.shape, F32), pltpu.VMEM((3,) + w.shape, F32), pltpu.VMEM((4,) + w.shape, F32),
                        pltpu.SemaphoreType.DMA((len(bounds), 4)), pltpu.SemaphoreType.DMA((len(bounds), 4))],
        compiler_params=_params(),
    )(g, w, m, v)


def _ada_adamw(sc_all, d_ada, w, m, v, chunks=4):
    rows, cols = w.shape
    step, sub = rows // chunks, 32
    assert rows % chunks == 0 and step % LANE == 0 and step % sub == 0

    def body(sc_ref, d_ref, w_hbm, m_hbm, v_hbm, og_hbm, od_hbm, om_hbm, ov_hbm, in_buf, out_buf, in_sems, out_sems):
        def copies_in(c):
            part = slice(c * step, (c + 1) * step)
            return [pltpu.make_async_copy(src.at[part], in_buf.at[i, part], in_sems.at[c, i])
                    for i, src in enumerate((w_hbm, m_hbm, v_hbm))]

        def copies_out(c):
            part = slice(c * step, (c + 1) * step)
            return [pltpu.make_async_copy(out_buf.at[i, part], dst.at[part], out_sems.at[c, i])
                    for i, dst in enumerate((og_hbm, od_hbm, om_hbm, ov_hbm))]

        for c in range(chunks):
            for cp in copies_in(c):
                cp.start()
        for c in range(chunks):
            sc_t = sc_ref[:, c * step:(c + 1) * step].T
            for cp in copies_in(c):
                cp.wait()
            for r0 in range(0, step, sub):
                part = slice(c * step + r0, c * step + r0 + sub)
                g = sc_t[r0:r0 + sub, 0:1] * d_ref[0:1, :]
                for b in range(1, N_DEV):
                    g = g + sc_t[r0:r0 + sub, b:b + 1] * d_ref[b:b + 1, :]
                results = _adamw(in_buf[0, part, :], g, in_buf[1, part, :], in_buf[2, part, :])
                for i, val in enumerate((g,) + results):
                    out_buf[i, part, :] = val
            for cp in copies_out(c):
                cp.start()
        for c in range(chunks):
            for cp in copies_out(c):
                cp.wait()

    in_vmem = pl.BlockSpec(memory_space=pltpu.VMEM)
    shape = jax.ShapeDtypeStruct(w.shape, F32)
    return pl.pallas_call(
        body, name="ada_adamw",
        in_specs=[in_vmem, in_vmem, ANY, ANY, ANY], out_specs=[ANY] * 4, out_shape=[shape] * 4,
        scratch_shapes=[pltpu.VMEM((3,) + w.shape, F32), pltpu.VMEM((4,) + w.shape, F32),
                        pltpu.SemaphoreType.DMA((chunks, 3)), pltpu.SemaphoreType.DMA((chunks, 4))],
        compiler_params=_params(),
    )(sc_all, d_ada, w, m, v)


F_LO, F_HI = 3 * D_ATT, 3 * D_ATT + N_HEADS


def _split_forget(a, axis):
    idx = lambda lo, hi: tuple(slice(lo, hi) if d == axis else slice(None) for d in range(a.ndim))
    pad = [(0, LANE - N_HEADS) if d == axis else (0, 0) for d in range(a.ndim)]
    return jnp.concatenate([a[idx(0, F_LO)], a[idx(F_HI, D_IN)]], axis=axis), jnp.pad(a[idx(F_LO, F_HI)], pad)


def _join_forget(main, f, axis):
    idx = lambda lo, hi: tuple(slice(lo, hi) if d == axis else slice(None) for d in range(main.ndim))
    return jnp.concatenate([main[idx(0, F_LO)], f[idx(0, N_HEADS)], main[idx(F_LO, N_MAIN)]], axis=axis)


def _adamw_small(grad_rows, row_params, whole_params, summed_params, scalar_at):
    n_row, n_whole, n_sum = len(row_params), len(whole_params), len(summed_params)
    n = n_row + n_whole + n_sum

    def body(g_ref, *refs):
        n_in = 3 * n_row + 4 * (n_whole + n_sum)
        ins, outs = list(refs[:n_in]), refs[n_in:]
        for i in range(n):
            if i < n_row:
                row, lo, hi = row_params[i][:3]
                g = g_ref[row:row + 1, lo:hi]
            elif i < n_row + n_whole:
                g = ins.pop(0)[...]
            else:
                parts = ins.pop(0)
                g = parts[0]
                for k in range(1, N_DEV):
                    g = g + parts[k]
            w, m, v = (ins.pop(0)[...] for _ in range(3))
            outs[4 * i][...] = g
            outs[4 * i + 1][...], outs[4 * i + 2][...], outs[4 * i + 3][...] = _adamw(w, g, m, v)
        row, lane = scalar_at
        outs[4 * n][...] = g_ref[row:row + 1, lane:lane + 1]

    shapes = [p[3] for p in row_params] + [p[1] for p in whole_params] + [p[1] for p in summed_params]
    operands = [a for p in row_params for a in p[3:]] + [a for p in whole_params + summed_params for a in p]
    flat = pl.pallas_call(
        body, name="adamw_small",
        out_shape=[jax.ShapeDtypeStruct(w.shape, F32) for w in shapes for _ in range(4)]
        + [jax.ShapeDtypeStruct((1, 1), F32)],
        compiler_params=_params(),
    )(grad_rows, *operands)
    return [flat[4 * i:4 * i + 4] for i in range(n)], flat[4 * n].reshape(())


def kernel(x, c, w_ada, b_ada, w_in, b_in, w_pool_mix, b_pool_mix, pool_scale, w_out, b_out, ln_g, ln_b, loss_target, m_w_ada, m_b_ada, m_w_in, m_b_in, m_w_pool_mix, m_b_pool_mix, m_pool_scale, m_w_out, m_b_out, m_ln_g, m_ln_b, v_w_ada, v_b_ada, v_w_in, v_b_in, v_w_pool_mix, v_b_pool_mix, v_pool_scale, v_w_out, v_b_out, v_ln_g, v_ln_b):
    seq = x.shape[1]
    tile = min(256, seq)
    attn_tile = min(512, max(128, seq // 4))
    me = _dev_index(*_mesh_pos())
    x2, tgt = x[0], loss_target[0]

    rows_of = lambda a: jnp.swapaxes(a, 1, 2)[0]
    w_main, w_f, sc_all, ada_mine = _gather_and_ada(c, rows_of(w_in).astype(BF16), w_ada[0])
    ada = ada_mine.reshape(1, D_ADA) + b_ada
    shift, scale, gate = ada[:, 0:D], ada[:, D:2 * D], ada[:, 2 * D:]
    mod = jnp.concatenate([1.0 + scale, shift, jnp.zeros((6, D), F32)], axis=0)
    b_main, b_f = _split_forget(b_in, 1)

    qp, kp, vp, f, p, g_att, g_pool, u = _inproj_forward(x2, mod, w_main, w_f, b_main, b_f, tile)
    att, q2t, w_out_g = _attention_forward(qp, kp, vp, w_out[0].astype(BF16), attn_tile)

    vecs = jnp.concatenate([gate, b_out, ln_g, ln_b, jnp.zeros((4, D), F32)], axis=0)
    pool_vecs = jnp.concatenate([b_pool_mix.reshape(1, D_POOL), pool_scale, jnp.zeros((6, D_POOL), F32)], axis=0)
    dxa, do2, d_ga, d_gp, d_pooled, gw_out, dw_pool, dvec = _middle(
        x2, tgt, att, g_att, g_pool, p, vecs, pool_vecs, w_out_g.reshape(D, D), w_pool_mix[0].astype(BF16), tile)

    pool_rows = w_pool_mix.shape[1] * GROUP_DIM
    dqp, dkp, dvp, d_cum, g_out, dvec_sum, dw_pool_sum = _attention_backward(
        q2t, kp, vp, do2, gw_out.reshape(N_DEV, D // N_DEV, D), dvec, dw_pool.reshape(pool_rows, GROUP_DIM), attn_tile)
    dx, dproj, dw_f, db_main, db_f, dmod = _inproj_backward(
        dqp, dkp, dvp, d_cum, f, d_pooled, d_ga, d_gp, x2, dxa, u, mod, w_main, w_f, tile)
    gw_in = _weight_grads(dqp, dkp, dvp, dw_f, dproj, u, min(512, seq))
    d_ada = jnp.concatenate([dmod[1:2], dmod[0:1], dvec[5:6]], axis=1)
    g_in_rows, g_b_in, d_ada_all = _reduce_grads(gw_in, _join_forget(db_main[0:1], db_f[0:1], 1), d_ada)

    packed = lambda a: jnp.transpose(a.reshape(SUBLANES, LANE, -1), (2, 0, 1)).reshape(-1, LANE)
    outs_in = _adamw_packed(g_in_rows, packed(w_in), packed(m_w_in), packed(v_w_in), "adamw_w_in")
    g_w_in, d_w_in, nm_w_in, nv_w_in = (
        jnp.transpose(a.reshape(-1, SUBLANES, LANE), (1, 2, 0)).reshape(D, -1) for a in outs_in)
    flat_pool = lambda a: a.reshape(1, D_POOL)
    pool_2d = lambda a: a.reshape(pool_rows, GROUP_DIM)
    rows, loss = _adamw_small(
        dvec_sum,
        [(0, 0, D, b_out, m_b_out, v_b_out), (1, 0, D, ln_g, m_ln_g, v_ln_g), (2, 0, D, ln_b, m_ln_b, v_ln_b),
         (3, 0, D_POOL, flat_pool(b_pool_mix), flat_pool(m_b_pool_mix), flat_pool(v_b_pool_mix)),
         (3, D_POOL, 2 * D_POOL, pool_scale, m_pool_scale, v_pool_scale)],
        [(g_out, w_out[0], m_w_out[0], v_w_out[0]),
         (dw_pool_sum, pool_2d(w_pool_mix), pool_2d(m_w_pool_mix), pool_2d(v_w_pool_mix)),
         (g_b_in, b_in, m_b_in, v_b_in)],
        [(d_ada_all, b_ada, m_b_ada, v_b_ada)],
        scalar_at=(4, 0))
    small = {"b_out": rows[0], "ln_g": rows[1], "ln_b": rows[2],
             "b_pool": [a.reshape(b_pool_mix.shape) for a in rows[3]], "pool_scale": rows[4],
             "w_pool": [a.reshape(w_pool_mix.shape) for a in rows[6]], "b_in": rows[7]}
    g_s, d_s, nm_s, nv_s = ({k: r[j] for k, r in small.items()} for j in range(4))
    g_w_out, d_w_out, nm_w_out, nv_w_out = rows[5]
    g_b_ada, d_b_ada, nm_b_ada, nv_b_ada = rows[8]

    d_ada_local = lax.dynamic_slice_in_dim(d_ada_all.reshape(N_DEV, D_ADA), me * (D_ADA // N_DEV), D_ADA // N_DEV, axis=1)
    g_w_ada, d_w_ada, nm_w_ada, nv_w_ada = _ada_adamw(sc_all, d_ada_local, w_ada[0], m_w_ada[0], v_w_ada[0])

    def ordered(w_ada_, b_ada_, w_in_, w_out_, s):
        return (w_ada_[None], b_ada_, w_in_[None], s["b_in"], s["w_pool"], s["b_pool"], s["pool_scale"],
                w_out_[None], s["b_out"], s["ln_g"], s["ln_b"])

    return (loss, dx[None],
            *ordered(g_w_ada, g_b_ada, g_w_in, g_w_out, g_s),
            *ordered(d_w_ada, d_b_ada, d_w_in, d_w_out, d_s),
            *ordered(nm_w_ada, nm_b_ada, nm_w_in, nm_w_out, nm_s),
            *ordered(nv_w_ada, nv_b_ada, nv_w_in, nv_w_out, nv_s))
```

```python
import jax
import jax.numpy as jnp
from jax import lax
from jax.experimental import pallas as pl
from jax.experimental.pallas import tpu as pltpu

F32 = jnp.float32
BF16 = jnp.bfloat16

N_DEV = 8
D = 1024
N_HEADS = 8
HEAD_DIM = 64
D_ATT = 512
D_POOL = 512
POOL_WINDOWS = (2, 4, 8, 16)
GROUP_DIM = 128
HALO = 16
LANE = 128
D_IN = 3080
D_ADA = 3072
N_MAIN = 3072
OFF_P = 1536
COL_CHUNK = 512
Q_SCALE = 0.125
LN_EPS = 1e-5
ALPHA = 2.0 ** 0.25
L_CQ, L_CK, L_LSE = 64, 67, 70

ADAM_LR, ADAM_B1, ADAM_B2, ADAM_EPS, ADAM_WD, ADAM_STEP = 0.001, 0.9, 0.999, 1e-08, 0.01, 10
VMEM_LIMIT = 56 * 1024 * 1024

MESH = pl.DeviceIdType.MESH
ANY = pl.BlockSpec(memory_space=pl.ANY)


def _params(sem=None, vmem=VMEM_LIMIT):
    return pltpu.CompilerParams(dimension_semantics=sem, vmem_limit_bytes=vmem)


def _split3(a):
    hi = a.astype(BF16)
    r = a - hi.astype(F32)
    mid = r.astype(BF16)
    lo = (r - mid.astype(F32)).astype(BF16)
    return hi, mid, lo


def _dot(a, b):
    return jnp.dot(a, b, preferred_element_type=F32)


def _dot_nt(a, b):
    return lax.dot_general(a, b, (((1,), (1,)), ((), ())), preferred_element_type=F32)


def _dot_tn(a, b):
    return lax.dot_general(a, b, (((0,), (0,)), ((), ())), preferred_element_type=F32)


def _dot3(m01, a):
    hi, mid, lo = _split3(a)
    return _dot(m01, hi) + _dot(m01, mid) + _dot(m01, lo)


def _sigmoid(z):
    return 1.0 / (1.0 + jnp.exp(-z))


def _lanes(shape):
    return lax.broadcasted_iota(jnp.int32, shape, len(shape) - 1)


def _place3(lane, base, parts, other):
    out = other
    for j in range(3):
        out = jnp.where(lane == base + j, parts[j], out)
    return out


def _mesh_pos():
    return lax.axis_index("x"), lax.axis_index("y"), lax.axis_index("c")


def _dev_index(px, py, pc):
    return 4 * px + 2 * py + pc


N_GATHER_SEMS = 9


def _gather_stages(src_ref, out_ref, send_sems, recv_sems, local_sem):
    x, y, c = _mesh_pos()
    me, sibling = (x, y, c), (x, y, 1 - c)
    nbr_x, nbr_y, diag = (1 - x, y), (x, 1 - y), (1 - x, 1 - y)
    half = out_ref.shape[-1] // 2
    left, right = pl.ds(0, half), pl.ds(half, half)

    def copy(k, block, to, cols=None, src=None):
        slot = out_ref.at[_dev_index(*block)]
        if cols is not None:
            slot = slot.at[:, cols]
        return pltpu.make_async_remote_copy(
            src_ref=slot if src is None else src, dst_ref=slot, send_sem=send_sems.at[k], recv_sem=recv_sems.at[k],
            device_id=to, device_id_type=MESH)

    mine = pltpu.make_async_copy(src_ref, out_ref.at[_dev_index(*me)], local_sem)
    first = [copy(0, me, sibling, src=src_ref), copy(1, me, (*nbr_x, c), src=src_ref), copy(2, me, (*nbr_y, c), src=src_ref)]
    relay = [(1, nbr_x, None, nbr_x), (2, nbr_y, None, nbr_y), (3, diag, left, nbr_y), (4, diag, right, nbr_x)]
    onward = [copy(3, (*nbr_x, c), (*nbr_y, c), cols=left), copy(4, (*nbr_y, c), (*nbr_x, c), cols=right)]
    passed = [copy(4 + k, (*block, c), sibling, cols=cols) for k, block, cols, _ in relay]

    def start():
        mine.start()
        for cp in first:
            cp.start()

    def relay_stage(first_item):
        def run():
            for j in (first_item, first_item + 1):
                k, block, cols, frm = relay[j]
                copy(k, (*block, c), (*frm, c), cols=cols).wait_recv()
                if j < 2:
                    onward[j].start()
                passed[j].start()
        return run

    def finish():
        copy(0, sibling, me).wait_recv()
        for k, block, cols, _ in relay:
            copy(4 + k, (*block, 1 - c), me, cols=cols).wait_recv()
        for cp in first + onward + passed:
            cp.wait_send()
        mine.wait()

    return start, relay_stage(0), relay_stage(2), finish


N_REDUCE_SEMS = 10
N_SMALL_SEMS = 4
N_ROWS_SEMS = 7


def _reduce_stages(ins, gs, r1, s2, r2, smalls, send_sems, recv_sems, rows=None, own=None):
    n = len(ins)
    x, y, c = _mesh_pos()
    me = _dev_index(x, y, c)
    sibling = (x, y, 1 - c)
    chips = [(x, y), (1 - x, y), (x, 1 - y), (1 - x, 1 - y)]
    peers = []
    for p in range(1, N_DEV):
        px, py, pc = (p >> 2) & 1, (p >> 1) & 1, p & 1
        peers.append((1 - x if px else x, 1 - y if py else y, 1 - c if pc else c))
    base_small = N_REDUCE_SEMS * n

    def remote(src, dst, k, to):
        return pltpu.make_async_remote_copy(src_ref=src, dst_ref=dst, send_sem=send_sems.at[k],
                                            recv_sem=recv_sems.at[k], device_id=to, device_id_type=MESH)

    def level1(a, q):
        return remote(ins[a].at[_dev_index(*chips[q], 1 - c)], r1[a].at[q], N_REDUCE_SEMS * a + q, sibling)

    def level2(a, k):
        half = ins[a].shape[-1] // 2
        left, right = pl.ds(0, half), pl.ds(half, half)
        nbr_x, nbr_y = (*chips[1], c), (*chips[2], c)
        src_slot, dst_slot, cols, to = [(0, 0, left, nbr_x), (1, 1, right, nbr_y), (2, 2, left, nbr_x),
                                        (2, 2, right, nbr_y), (0, 0, right, nbr_x), (1, 1, left, nbr_y)][k]
        return remote(s2[a].at[src_slot, :, cols], r2[a].at[dst_slot, :, cols], N_REDUCE_SEMS * a + 4 + k, to)

    to_sibling = [remote(sm[0], sm[2], base_small + 4 * i, sibling) for i, sm in enumerate(smalls)]
    to_chips = [[remote(sm[3], sm[4].at[j], base_small + 4 * i + 1 + j, (*chips[j + 1], c)) for j in range(3)]
                for i, sm in enumerate(smalls)]
    if rows is not None:
        rows_ref, land_ref, all_ref = rows
        base_rows = base_small + 4 * len(smalls)
        row_sends = [remote(rows_ref, land_ref.at[me], base_rows + k, to) for k, to in enumerate(peers)]

    def mine(a, q):
        buf, sems = own[a]
        return pltpu.make_async_copy(ins[a].at[_dev_index(*chips[q], c)], buf.at[q], sems.at[q])

    def start():
        for a in range(n):
            for q in (1, 2, 3, 0):
                level1(a, q).start()
            if own is not None:
                for q in (1, 2, 3, 0):
                    mine(a, q).start()
        for cp in to_sibling:
            cp.start()
        if rows is not None:
            for cp in row_sends:
                cp.start()
            land_ref[me] = rows_ref[...]

    def middle():
        for a in range(n):
            for q in (1, 2, 3, 0):
                level1(a, q).wait_recv()
                if own is None:
                    kept = ins[a][_dev_index(*chips[q], c)]
                else:
                    mine(a, q).wait()
                    kept = own[a][0][q]
                pair = kept.astype(F32) + r1[a][q].astype(F32)
                if q == 0:
                    gs[a][...] = pair
                else:
                    s2[a][q - 1] = pair.astype(BF16)
                    for k in ((0,), (1,), (2, 3))[q - 1]:
                        level2(a, k).start()
        for i, (small_ref, _, sm_sib, sm_chip, _) in enumerate(smalls):
            to_sibling[i].wait_recv()
            sm_chip[...] = small_ref[...] + sm_sib[...]
            for cp in to_chips[i]:
                cp.start()

    def fold():
        for a in range(n):
            half = ins[a].shape[-1] // 2
            level2(a, 3).wait_recv()
            s2[a][0, :, half:] = (s2[a][0, :, half:].astype(F32) + r2[a][2, :, half:].astype(F32)).astype(BF16)
            level2(a, 4).start()
            level2(a, 2).wait_recv()
            s2[a][1, :, :half] = (s2[a][1, :, :half].astype(F32) + r2[a][2, :, :half].astype(F32)).astype(BF16)
            level2(a, 5).start()

    def finish():
        for a in range(n):
            for k in (0, 1, 4, 5):
                level2(a, k).wait_recv()
            gs[a][...] = gs[a][...] + r2[a][0].astype(F32) + r2[a][1].astype(F32)
            for q in range(4):
                level1(a, q).wait_send()
            for k in range(6):
                level2(a, k).wait_send()
        for i, (_, total_ref, _, sm_chip, sm_recv) in enumerate(smalls):
            for cp in to_chips[i]:
                cp.wait_recv()
            total = None
            for ax in range(2):
                for ay in range(2):
                    dx, dy = x != ax, y != ay
                    term = jnp.where(dx, jnp.where(dy, sm_recv[2], sm_recv[0]), jnp.where(dy, sm_recv[1], sm_chip[...]))
                    total = term if total is None else total + term
            total_ref[...] = total
            for cp in [to_sibling[i]] + to_chips[i]:
                cp.wait_send()
        if rows is not None:
            for k, frm in enumerate(peers):
                remote(rows_ref, land_ref.at[_dev_index(*frm)], base_rows + k, frm).wait_recv()
            all_ref[...] = land_ref[...]
            for cp in row_sends:
                cp.wait_send()

    return start, middle, fold, finish


def _reduce_scratch(shard, smalls, rows=None):
    out = [pltpu.VMEM((lead,) + shard.shape[1:], BF16) for lead in (4, 3, 3)]
    for small in smalls:
        out += [pltpu.VMEM(small.shape, F32), pltpu.VMEM(small.shape, F32), pltpu.VMEM((3,) + small.shape, F32)]
    n_sems = N_REDUCE_SEMS + N_SMALL_SEMS * len(smalls)
    if rows is not None:
        out.append(pltpu.VMEM((N_DEV,) + rows.shape, F32))
        n_sems += N_ROWS_SEMS
    return out + [pltpu.SemaphoreType.DMA((n_sems,))] * 2


def _reduce_grads(gw_in, small, rows):
    def body(in_ref, small_ref, rows_ref, g_ref, total_ref, rows_all_ref,
             r1, s2, r2, sm_sib, sm_chip, sm_recv, rows_land, send_sems, recv_sems, kept, kept_sems):
        stages = _reduce_stages(
            [in_ref], [g_ref], [r1], [s2], [r2], [(small_ref, total_ref, sm_sib, sm_chip, sm_recv)],
            send_sems, recv_sems, rows=(rows_ref, rows_land, rows_all_ref), own=[(kept, kept_sems)])
        for stage in stages:
            stage()

    vmem = pl.BlockSpec(memory_space=pltpu.VMEM)
    return pl.pallas_call(
        body, name="reduce_grads",
        in_specs=[ANY, vmem, vmem], out_specs=[vmem, vmem, vmem],
        out_shape=[jax.ShapeDtypeStruct(gw_in.shape[1:], F32), jax.ShapeDtypeStruct(small.shape, F32),
                   jax.ShapeDtypeStruct((N_DEV,) + rows.shape, F32)],
        scratch_shapes=_reduce_scratch(gw_in, [small], rows)
        + [pltpu.VMEM((4,) + gw_in.shape[1:], BF16), pltpu.SemaphoreType.DMA((4,))],
        compiler_params=_params(),
    )(gw_in, small, rows)


def _dot3_rhs(a, b):
    a0, a1, a2 = _split3(a)
    b0, b1, b2 = _split3(b)
    return (_dot(a0, b0) + (_dot(a0, b1) + _dot(a1, b0))
            + (_dot(a0, b2) + _dot(a1, b1) + _dot(a2, b0)))


def _gather_and_ada(c, w_in_rows, w_ada):
    cols = w_ada.shape[1]
    shard = w_in_rows.shape[0]

    def body(c_ref, w_ref, wa_ref, w_main_ref, w_f_ref, sc_ref, ada_ref,
             w_all_ref, w_f32, c_land, part, ada_land, send_sems, recv_sems, local_sem, x_send, x_recv):
        x, y, cc = _mesh_pos()
        me = _dev_index(x, y, cc)
        peers = []
        for p in range(1, N_DEV):
            px, py, pc = (p >> 2) & 1, (p >> 1) & 1, p & 1
            peers.append((1 - x if px else x, 1 - y if py else y, 1 - cc if pc else cc))

        def remote(src, dst, k, to):
            return pltpu.make_async_remote_copy(src_ref=src, dst_ref=dst, send_sem=x_send.at[k], recv_sem=x_recv.at[k],
                                                device_id=to, device_id_type=MESH)

        c_sends = [remote(c_ref, c_land.at[me], k, to) for k, to in enumerate(peers)]
        for cp in c_sends:
            cp.start()
        start, relay_near, relay_far, finish = _gather_stages(w_ref, w_all_ref, send_sems, recv_sems, local_sem.at[0])
        start()
        c_land[me] = c_ref[...]
        for k, frm in enumerate(peers):
            remote(c_ref, c_land.at[_dev_index(*frm)], k, frm).wait_recv()
        c_all = jnp.concatenate([c_land[b] for b in range(N_DEV)], axis=0)
        sc = c_all * _sigmoid(c_all)
        sc_ref[...] = sc
        rows = _dot3_rhs(sc, wa_ref[...])
        for b in range(N_DEV):
            part[b] = rows[b:b + 1, :]
        a_sends = [remote(part.at[_dev_index(*to)], ada_land.at[me], 7 + k, to) for k, to in enumerate(peers)]
        for cp in a_sends:
            cp.start()
        ada_land[me] = part[me]
        for k, frm in enumerate(peers):
            remote(part.at[0], ada_land.at[_dev_index(*frm)], 7 + k, frm).wait_recv()
        ada_ref[...] = ada_land[...]

        relay_near()
        relay_far()
        finish()
        for cp in c_sends + a_sends:
            cp.wait_send()

        for slot in range(N_DEV):
            w_f32[slot * shard:(slot + 1) * shard, :] = w_all_ref[slot].astype(F32)
        w_main_ref[0:F_LO, :] = w_f32[0:F_LO, :].astype(BF16)
        w_main_ref[F_LO:N_MAIN, :] = w_f32[F_HI:D_IN, :].astype(BF16)
        w_f_ref[...] = jnp.concatenate(
            [w_f32[F_LO:F_HI, :], jnp.zeros((LANE - N_HEADS, D), F32)], axis=0).astype(BF16)

    vmem = pl.BlockSpec(memory_space=pltpu.VMEM)
    return pl.pallas_call(
        body, name="gather_weights",
        in_specs=[vmem, ANY, vmem], out_specs=[vmem, vmem, vmem, vmem],
        out_shape=[jax.ShapeDtypeStruct((N_MAIN, D), BF16), jax.ShapeDtypeStruct((LANE, D), BF16),
                   jax.ShapeDtypeStruct((N_DEV, D), F32), jax.ShapeDtypeStruct((N_DEV, 1, cols), F32)],
        scratch_shapes=[pltpu.VMEM((N_DEV,) + w_in_rows.shape, BF16), pltpu.VMEM((D_IN, D), F32),
                        pltpu.VMEM((N_DEV, 1, D), F32), pltpu.VMEM((N_DEV, 1, cols), F32), pltpu.VMEM((N_DEV, 1, cols), F32),
                        pltpu.SemaphoreType.DMA((N_GATHER_SEMS,)), pltpu.SemaphoreType.DMA((N_GATHER_SEMS,)),
                        pltpu.SemaphoreType.DMA((1,)),
                        pltpu.SemaphoreType.DMA((14,)), pltpu.SemaphoreType.DMA((14,))],
        compiler_params=_params(),
    )(c, w_in_rows, w_ada)


def _inproj_forward(x, mod, w_main, w_f, b_main, b_f, tile):
    seq = x.shape[0]
    nt = seq // tile

    def body(x_ref, mod_ref, w_ref, wf_ref, b_ref, bf_ref,
             qp_ref, kp_ref, vp_ref, f_ref, p_ref, ga_ref, gp_ref, u_ref, carry_ref):
        i = pl.program_id(0)

        @pl.when(i == 0)
        def _():
            carry_ref[...] = jnp.zeros_like(carry_ref)

        u = x_ref[...] * mod_ref[0:1, :] + mod_ref[1:2, :]
        ub = u.astype(BF16)
        u_ref[...] = ub

        f = _dot_nt(ub, wf_ref[...]) + bf_ref[...]
        f_ref[...] = f
        lane = _lanes((tile, LANE))
        log_f = jnp.where(lane < N_HEADS, jnp.minimum(f, 0.0) - jnp.log(1.0 + jnp.exp(-jnp.abs(f))), 0.0)
        row = lax.broadcasted_iota(jnp.int32, (tile, tile), 0)
        col = lax.broadcasted_iota(jnp.int32, (tile, tile), 1)
        tri = (row >= col).astype(BF16)
        cum = _dot3(tri, log_f) + carry_ref[0:1, :]
        carry_ref[0:1, :] = cum[tile - 1:tile, :]
        cq = [part.astype(F32) for part in _split3(cum)]
        ck = [part.astype(F32) for part in _split3(-cum)]

        def proj(chunk):
            cols = pl.ds(chunk * COL_CHUNK, COL_CHUNK)
            return _dot_nt(ub, w_ref[cols, :]) + b_ref[:, cols]

        def head_tiles(r):
            for pair in range(N_HEADS // 2):
                both = r[:, pair * LANE:(pair + 1) * LANE]
                yield 2 * pair, both
                yield 2 * pair + 1, pltpu.roll(both, HEAD_DIM, 1)

        for h, val in head_tiles(proj(0)):
            extra = jnp.where((lane >= L_CK) & (lane < L_CK + 3), 1.0, 0.0)
            extra = _place3(lane, L_CQ, [part[:, h:h + 1] for part in cq], extra)
            qp_ref[h] = jnp.where(lane < HEAD_DIM, val * Q_SCALE, extra).astype(BF16)
        for h, val in head_tiles(proj(1)):
            ones = ((lane >= L_CQ) & (lane < L_CQ + 3)) | ((lane >= L_LSE) & (lane < L_LSE + 3))
            extra = _place3(lane, L_CK, [part[:, h:h + 1] for part in ck], jnp.where(ones, 1.0, 0.0))
            kp_ref[h] = jnp.where(lane < HEAD_DIM, val, extra).astype(BF16)
        for h, val in head_tiles(proj(2)):
            extra = jnp.where((lane >= HEAD_DIM) & (lane < HEAD_DIM + 3), -1.0, 0.0)
            vp_ref[h] = jnp.where(lane < HEAD_DIM, val, extra).astype(BF16)
        p_ref[...] = proj(3)
        ga_ref[...] = proj(4)
        gp_ref[...] = proj(5)

    head_block = pl.BlockSpec((N_HEADS, tile, LANE), lambda i: (0, i, 0))
    tok = lambda width: pl.BlockSpec((tile, width), lambda i: (i, 0))
    whole = lambda a: pl.BlockSpec(a.shape, lambda i: (0,) * a.ndim)
    padded = jax.ShapeDtypeStruct((N_HEADS, seq, LANE), BF16)
    half = jax.ShapeDtypeStruct((seq, D_ATT), F32)
    return pl.pallas_call(
        body, name="inproj_forward", grid=(nt,),
        in_specs=[tok(D), whole(mod), whole(w_main), whole(w_f), whole(b_main), whole(b_f)],
        out_specs=[head_block, head_block, head_block, tok(LANE), tok(D_POOL), tok(D_ATT), tok(D_POOL),
                   tok(D)],
        out_shape=[padded, padded, padded, jax.ShapeDtypeStruct((seq, LANE), F32), half, half, half,
                   jax.ShapeDtypeStruct((seq, D), BF16)],
        scratch_shapes=[pltpu.VMEM((8, LANE), F32)],
        compiler_params=_params(("arbitrary",)),
    )(x, mod, w_main, w_f, b_main, b_f)


def _attention_forward(qp, kp, vp, w_out, tile):
    seq = qp.shape[1]
    nb = seq // tile
    steps = (N_HEADS // 2) * nb

    def body(q_ref, k_ref, v_ref, wo_ref, att_ref, q2t_ref, wo_all_ref, s_a, s_b, m_ref, acc_ref,
             send_sems, recv_sems, local_sem):
        step = pl.program_id(0) * nb + pl.program_id(1)
        start, relay_near, relay_far, finish = _gather_stages(wo_ref, wo_all_ref, send_sems, recv_sems, local_sem.at[0])
        pl.when(step == 0)(start)
        pl.when(step == steps // 4)(relay_near)
        pl.when(step == (3 * steps) // 4)(relay_far)

        i = pl.program_id(1)
        sub = lax.broadcasted_iota(jnp.int32, (LANE, tile), 0)
        row = lax.broadcasted_iota(jnp.int32, (tile, tile), 0)
        col = lax.broadcasted_iota(jnp.int32, (tile, tile), 1)
        q = [q_ref[0], q_ref[1]]

        def scores(buf, kb):
            rows = pl.ds(pl.multiple_of(kb * tile, tile), tile)
            for hh in range(2):
                buf[hh] = _dot_nt(k_ref[hh, rows, :], q[hh])

        def absorb(buf, kb, masked):
            rows = pl.ds(pl.multiple_of(kb * tile, tile), tile)
            for hh in range(2):
                m = m_ref[hh, 0:1, :]
                s = buf[hh]
                if masked:
                    s = jnp.where(row <= col, s, -1e30)
                m_new = jnp.maximum(m, jnp.max(s, axis=0, keepdims=True))
                p = jnp.exp(s - m_new).astype(BF16)
                acc_ref[hh] = jnp.exp(m - m_new) * acc_ref[hh] + _dot_tn(v_ref[hh, rows, :], p)
                m_ref[hh, 0:1, :] = m_new

        def two_blocks(j, _):
            scores(s_b, 2 * j + 1)
            absorb(s_a, 2 * j, False)
            scores(s_a, 2 * j + 2)
            absorb(s_b, 2 * j + 1, False)
            return 0

        def last_block():
            absorb(s_a, i, True)

        def last_two_blocks():
            scores(s_b, i)
            absorb(s_a, i - 1, False)
            absorb(s_b, i, True)

        scores(s_a, 0)
        m_ref[...] = jnp.full(m_ref.shape, -1e30, F32)
        acc_ref[...] = jnp.zeros_like(acc_ref)
        lax.fori_loop(0, i // 2, two_blocks, 0)
        lax.cond(i % 2 == 0, last_block, last_two_blocks)
        outs = []
        for hh in range(2):
            m, acc = m_ref[hh, 0:1, :], acc_ref[hh]
            l = -acc[HEAD_DIM:HEAD_DIM + 1, :]
            outs.append((acc / l)[:HEAD_DIM, :])
            neg_lse = [part.astype(F32) for part in _split3(-(m + jnp.log(l)))]
            q2t_ref[hh] = _place3(sub, L_LSE, neg_lse, q[hh].astype(F32).T).astype(BF16)
        att_ref[...] = jnp.concatenate(outs, axis=0).T
        pl.when(step == steps - 1)(finish)

    pair = pl.BlockSpec((2, tile, LANE), lambda hp, i: (hp, i, 0))
    full = pl.BlockSpec((2, seq, LANE), lambda hp, i: (hp, 0, 0))
    return pl.pallas_call(
        body, name="attention_forward", grid=(N_HEADS // 2, nb),
        in_specs=[pair, full, full, ANY],
        out_specs=[pl.BlockSpec((tile, LANE), lambda hp, i: (i, hp)),
                   pl.BlockSpec((2, LANE, tile), lambda hp, i: (hp, 0, i)), ANY],
        out_shape=[jax.ShapeDtypeStruct((seq, D_ATT), F32),
                   jax.ShapeDtypeStruct((N_HEADS, LANE, seq), BF16),
                   jax.ShapeDtypeStruct((N_DEV,) + w_out.shape, w_out.dtype)],
        scratch_shapes=[pltpu.VMEM((2, tile, tile), F32), pltpu.VMEM((2, tile, tile), F32),
                        pltpu.VMEM((2, 8, tile), F32), pltpu.VMEM((2, LANE, tile), F32),
                        pltpu.SemaphoreType.DMA((N_GATHER_SEMS,)), pltpu.SemaphoreType.DMA((N_GATHER_SEMS,)),
                        pltpu.SemaphoreType.DMA((1,))],
        compiler_params=_params(("arbitrary", "arbitrary")),
    )(qp, kp, vp, w_out)


def _window_sum(x, halo, window, transposed):
    tile = x.shape[0]

    def split_cat(a):
        hi = a.astype(BF16)
        return jnp.concatenate([hi, (a - hi.astype(F32)).astype(BF16)], axis=1)

    def fold(r):
        return r[:, :LANE] + r[:, LANE:]

    r = lax.broadcasted_iota(jnp.int32, (tile, tile), 0)
    c = lax.broadcasted_iota(jnp.int32, (tile, tile), 1)
    rh = lax.broadcasted_iota(jnp.int32, (HALO, HALO), 0)
    ch = lax.broadcasted_iota(jnp.int32, (HALO, HALO), 1)
    if not transposed:
        band = (c <= r) & (r - c < window)
        edge = (rh + HALO - ch) < window
    else:
        band = (r <= c) & (c - r < window)
        edge = (HALO + ch - rh) < window
    out = fold(_dot(band.astype(BF16), split_cat(x)))
    reach = fold(_dot(edge.astype(BF16), split_cat(halo)))
    if not transposed:
        return jnp.concatenate([out[:HALO] + reach, out[HALO:]], axis=0)
    return jnp.concatenate([out[:tile - HALO], out[tile - HALO:] + reach], axis=0)


def _silu_parts(g):
    sig = _sigmoid(g)
    return g * sig, sig * (1.0 + g * (1.0 - sig))


def _middle(x, tgt, att, g_att, g_pool, p, vecs, pool_vecs, w_out, w_pool, tile):
    seq = x.shape[0]
    nt = seq // tile
    halo_blocks = tile // HALO

    def body(x_ref, tgt_ref, att_ref, ga_ref, gp_ref, p_ref, ph_ref, vec_ref, pvec_ref, wo_ref, wp_ref,
             dxa_ref, do2_ref, dga_ref, dgp_ref, dpooled_ref, gwo_ref, dwp_ref, dvec_ref, dwo_ref, dpvec_ref):
        i = pl.program_id(0)

        @pl.when(i == 0)
        def _():
            dwo_ref[...] = jnp.zeros_like(dwo_ref)
            dwp_ref[...] = jnp.zeros_like(dwp_ref)
            dvec_ref[...] = jnp.zeros_like(dvec_ref)
            dpvec_ref[...] = jnp.zeros_like(dpvec_ref)

        gate, b_out, ln_g, ln_b = (vec_ref[k:k + 1, :] for k in range(4))
        b_pool, pool_scale = pvec_ref[0:1, :], pvec_ref[1:2, :]
        x = x_ref[...]
        p = p_ref[...]
        p_halo = ph_ref[...] * jnp.where(i > 0, 1.0, 0.0)
        pos = i * tile + lax.broadcasted_iota(jnp.int32, (tile, 1), 0) + 1

        pooled, mixed = [], []
        for g, window in enumerate(POOL_WINDOWS):
            cols = slice(g * GROUP_DIM, (g + 1) * GROUP_DIM)
            wsum = _window_sum(p[:, cols], p_halo[:, cols], window, False)
            count = jnp.minimum(pos, window).astype(F32)
            pooled.append(wsum / count - p[:, cols])
            mixed.append(_dot(pooled[g].astype(BF16), wp_ref[g]) + b_pool[:, cols])
        mixed = jnp.concatenate(mixed, axis=1)
        pool = mixed * pool_scale

        att = att_ref[...]
        g_att, g_pool = ga_ref[...], gp_ref[...]
        silu_a, dsilu_a = _silu_parts(g_att)
        silu_p, dsilu_p = _silu_parts(g_pool)
        y_in = jnp.concatenate([att * silu_a, pool * silu_p], axis=1)
        y = _dot(y_in.astype(BF16), wo_ref[...]) + b_out
        h = ALPHA * x + gate * y
        mu = jnp.mean(h, axis=1, keepdims=True)
        hc = h - mu
        var = jnp.mean(hc * hc, axis=1, keepdims=True)
        rstd = lax.rsqrt(var + LN_EPS)
        yhat = hc * rstd
        diff = yhat * ln_g + ln_b - tgt_ref[...]
        loss_rows = jnp.sum(diff * diff, axis=1, keepdims=True)
        d_out = diff * (1.0 / D)

        d_yhat = d_out * ln_g
        dh = rstd * (d_yhat - jnp.mean(d_yhat, axis=1, keepdims=True)
                     - yhat * jnp.mean(d_yhat * yhat, axis=1, keepdims=True))
        dxa_ref[...] = ALPHA * dh
        dy = dh * gate
        dyb = dy.astype(BF16)
        lane = _lanes((1, D))
        loss_row = jnp.where(lane == 0, (0.5 / D) * jnp.sum(loss_rows, axis=0, keepdims=True), 0.0)
        dvec_ref[5:6, :] += jnp.sum(dh * y, axis=0, keepdims=True)
        dvec_ref[0:1, :] += jnp.sum(dy, axis=0, keepdims=True)
        dvec_ref[1:2, :] += jnp.sum(d_out * yhat, axis=0, keepdims=True)
        dvec_ref[2:3, :] += jnp.sum(d_out, axis=0, keepdims=True)
        dvec_ref[4:5, :] += loss_row

        dwo_ref[...] += _dot(y_in.T.astype(BF16), dyb)
        d_yin = _dot_nt(dyb, wo_ref[...])
        d_a, d_pl = d_yin[:, :D_ATT], d_yin[:, D_ATT:]
        d_att = d_a * silu_a
        d_att_t = d_att.T
        prod_t = (d_att * att).T
        sub = lax.broadcasted_iota(jnp.int32, (HEAD_DIM, tile), 0)
        for h in range(N_HEADS):
            rows = slice(h * HEAD_DIM, (h + 1) * HEAD_DIM)
            delta = jnp.sum(prod_t[rows], axis=0, keepdims=True)
            extra = _place3(sub, 0, [part.astype(F32) for part in _split3(delta)], 0.0)
            do2_ref[h] = jnp.concatenate([d_att_t[rows], extra], axis=0).astype(BF16)
        dga_ref[...] = d_a * att * dsilu_a
        dgp_ref[...] = d_pl * pool * dsilu_p
        d_pool = d_pl * silu_p
        d_mixed = d_pool * pool_scale
        dpvec_ref[0:1, :] += jnp.sum(d_mixed, axis=0, keepdims=True)
        dpvec_ref[1:2, :] += jnp.sum(d_pool * mixed, axis=0, keepdims=True)
        d_pooled = []
        for g in range(len(POOL_WINDOWS)):
            cols = slice(g * GROUP_DIM, (g + 1) * GROUP_DIM)
            dmb = d_mixed[:, cols].astype(BF16)
            dwp_ref[g] += _dot(pooled[g].T.astype(BF16), dmb)
            d_pooled.append(_dot_nt(dmb, wp_ref[g]))
        dpooled_ref[...] = jnp.concatenate(d_pooled, axis=1)

        @pl.when(i == nt - 1)
        def _():
            gwo_ref[...] = dwo_ref[...].astype(BF16)
            dvec_ref[3:4, :] = jnp.concatenate([dpvec_ref[0:1, :], dpvec_ref[1:2, :]], axis=1)

    tok = lambda width: pl.BlockSpec((tile, width), lambda i: (i, 0))
    whole = lambda a: pl.BlockSpec(a.shape, lambda i: (0,) * a.ndim)
    halo = pl.BlockSpec((HALO, D_POOL), lambda i: (jnp.maximum(i * halo_blocks - 1, 0), 0))
    half = jax.ShapeDtypeStruct((seq, D_ATT), F32)
    outs = [jax.ShapeDtypeStruct((seq, D), F32), jax.ShapeDtypeStruct((N_HEADS, LANE, seq), BF16), half, half, half,
            jax.ShapeDtypeStruct(w_out.shape, BF16), jax.ShapeDtypeStruct(w_pool.shape, F32),
            jax.ShapeDtypeStruct(vecs.shape, F32)]
    return pl.pallas_call(
        body, name="middle", grid=(nt,),
        in_specs=[tok(D), tok(D), tok(D_ATT), tok(D_ATT), tok(D_POOL), tok(D_POOL), halo,
                  whole(vecs), whole(pool_vecs), whole(w_out), whole(w_pool)],
        out_specs=[tok(D), pl.BlockSpec((N_HEADS, LANE, tile), lambda i: (0, 0, i)),
                   tok(D_ATT), tok(D_POOL), tok(D_POOL),
                   whole(w_out), whole(w_pool), whole(vecs)],
        out_shape=outs,
        scratch_shapes=[pltpu.VMEM(w_out.shape, F32), pltpu.VMEM(pool_vecs.shape, F32)],
        compiler_params=_params(("arbitrary",)),
    )(x, tgt, att, g_att, g_pool, p, p, vecs, pool_vecs, w_out, w_pool)


def _attention_backward(q2t, kp, vp, do2t, gw_out, vecs, pool, tile):
    seq = kp.shape[1]
    nb = seq // tile
    last = N_HEADS // 2 - 1

    def body(qt_ref, k_ref, v_ref, dot_ref, gwo_hbm, vecs_hbm, pool_hbm,
             dq_ref, dk_ref, dv_ref, dcum_ref, g_out_ref, vecs_sum_ref, pool_sum_ref,
             dq_acc, dk_acc, dv_acc, gwo_ref, vecs_ref, pool_ref,
             r1, s2, r2, v_sib, v_chip, v_recv, p_sib, p_chip, p_recv, send_sems, recv_sems):
        hp = pl.program_id(0)
        start, middle, fold, finish = _reduce_stages(
            [gwo_ref], [g_out_ref], [r1], [s2], [r2],
            [(vecs_ref, vecs_sum_ref, v_sib, v_chip, v_recv), (pool_ref, pool_sum_ref, p_sib, p_chip, p_recv)],
            send_sems, recv_sems)

        @pl.when(hp == 0)
        def _():
            pltpu.sync_copy(gwo_hbm, gwo_ref)
            pltpu.sync_copy(vecs_hbm, vecs_ref)
            pltpu.sync_copy(pool_hbm, pool_ref)
            start()

        pl.when(hp == 1)(middle)
        pl.when(hp == 2)(fold)

        row = lax.broadcasted_iota(jnp.int32, (tile, tile), 0)
        col = lax.broadcasted_iota(jnp.int32, (tile, tile), 1)
        dq_acc[...] = jnp.zeros_like(dq_acc)

        def kv_block(kb, _):
            krows = pl.ds(pl.multiple_of(kb * tile, tile), tile)
            k = [k_ref[hh, krows, :] for hh in range(2)]
            v = [v_ref[hh, krows, :] for hh in range(2)]
            k_t = [k[hh].T for hh in range(2)]

            def q_block(qb, masked):
                qcols = pl.ds(pl.multiple_of(qb * tile, tile), tile)
                for hh in range(2):
                    q_t = qt_ref[hh, :, qcols]
                    do_t = dot_ref[hh, :, qcols]
                    s_t = _dot(k[hh], q_t)
                    if masked:
                        s_t = jnp.where(row <= col, s_t, -1e30)
                    p_t = jnp.exp(s_t)
                    ds_t = (p_t * _dot(v[hh], do_t)).astype(BF16)
                    dv_new = _dot_nt(do_t, p_t.astype(BF16))
                    dk_new = _dot_nt(q_t, ds_t)
                    if masked:
                        dv_acc[hh], dk_acc[hh] = dv_new, dk_new
                    else:
                        dv_acc[hh] += dv_new
                        dk_acc[hh] += dk_new
                    dq_acc[hh, :, qcols] += _dot(k_t[hh], ds_t)

            q_block(kb, True)

            def two_later_blocks(j, _):
                q_block(kb + 1 + 2 * j, False)
                q_block(kb + 2 + 2 * j, False)
                return 0

            later = nb - 1 - kb
            lax.fori_loop(0, later // 2, two_later_blocks, 0)
            pl.when(later % 2 == 1)(lambda: q_block(nb - 1, False))
            for hh in range(2):
                dk = dk_acc[hh]
                dk_ref[hh, :, krows] = dk.astype(BF16)
                dv_ref[hh, :, krows] = dv_acc[hh].astype(BF16)
                dcum_ref[hh, :, krows] = -dk[L_CK:L_CK + 1, :]
            return 0

        lax.fori_loop(0, nb, kv_block, 0)
        for hh in range(2):
            dq = dq_acc[hh]
            dcum_ref[hh] += dq[L_CQ:L_CQ + 1, :]
            dq_ref[hh] = (dq * Q_SCALE).astype(BF16)
        pl.when(hp == last)(finish)

    pair = pl.BlockSpec((2, seq, LANE), lambda hp: (hp, 0, 0))
    pair_t = pl.BlockSpec((2, LANE, seq), lambda hp: (hp, 0, 0))
    whole = lambda shape: pl.BlockSpec(shape, lambda hp: (0,) * len(shape))
    grad = jax.ShapeDtypeStruct((N_HEADS, LANE, seq), BF16)
    return pl.pallas_call(
        body, name="attention_backward", grid=(N_HEADS // 2,),
        in_specs=[pair_t, pair, pair, pair_t, ANY, ANY, ANY],
        out_specs=[pair_t, pair_t, pair_t, pl.BlockSpec((2, 1, seq), lambda hp: (hp, 0, 0)),
                   whole(gw_out.shape[1:]), whole(vecs.shape), whole(pool.shape)],
        out_shape=[grad, grad, grad, jax.ShapeDtypeStruct((N_HEADS, 1, seq), F32),
                   jax.ShapeDtypeStruct(gw_out.shape[1:], F32), jax.ShapeDtypeStruct(vecs.shape, F32),
                   jax.ShapeDtypeStruct(pool.shape, F32)],
        scratch_shapes=[pltpu.VMEM((2, LANE, seq), F32), pltpu.VMEM((2, LANE, tile), F32),
                        pltpu.VMEM((2, LANE, tile), F32), pltpu.VMEM(gw_out.shape, BF16),
                        pltpu.VMEM(vecs.shape, F32), pltpu.VMEM(pool.shape, F32)]
        + _reduce_scratch(gw_out, [vecs, pool]),
        compiler_params=_params(("arbitrary",)),
    )(q2t, kp, vp, do2t, gw_out, vecs, pool)


def _inproj_backward(dqp, dkp, dvp, d_cum, f, d_pooled, d_ga, d_gp, x, dxa, u, mod, w_main, w_f, tile):
    seq = x.shape[0]
    nt = seq // tile
    halo_blocks = tile // HALO

    def body(dq_ref, dk_ref, dv_ref, dcum_ref, f_ref, dpo_ref, dph_ref, dga_ref, dgp_ref, x_ref, dxa_ref, u_ref,
             mod_ref, w_ref, wf_ref,
             dx_ref, dproj_ref, dwf_ref, db_ref, dbf_ref, dmod_ref, carry_ref):
        step = pl.program_id(0)
        i = nt - 1 - step

        @pl.when(step == 0)
        def _():
            carry_ref[...] = jnp.zeros_like(carry_ref)
            dwf_ref[...] = jnp.zeros_like(dwf_ref)
            db_ref[...] = jnp.zeros_like(db_ref)
            dbf_ref[...] = jnp.zeros_like(dbf_ref)
            dmod_ref[...] = jnp.zeros_like(dmod_ref)

        ones = jnp.ones((8, tile), BF16)

        def emit(chunk, val):
            cols = pl.ds(chunk * COL_CHUNK, COL_CHUNK)
            db_ref[0:1, cols] += jnp.sum(val, axis=0, keepdims=True)
            vb = val.astype(BF16)
            dproj_ref[:, pl.ds((chunk - 3) * COL_CHUNK, COL_CHUNK)] = vb
            return _dot(vb, w_ref[cols, :])

        d_u = jnp.zeros((tile, D), F32)
        for chunk, ref in enumerate((dq_ref, dk_ref, dv_ref)):
            cols = pl.ds(chunk * COL_CHUNK, COL_CHUNK)
            val_t = ref[:, 0:HEAD_DIM, :].reshape(COL_CHUNK, tile)
            db_ref[:, cols] += _dot_nt(ones, val_t)
            d_u += _dot_tn(val_t, w_ref[cols, :])

        d_pooled = dpo_ref[...]
        d_halo = dph_ref[...] * jnp.where(i < nt - 1, 1.0, 0.0)
        pos = i * tile + lax.broadcasted_iota(jnp.int32, (tile, 1), 0) + 1
        d_p = []
        for g, window in enumerate(POOL_WINDOWS):
            cols = slice(g * GROUP_DIM, (g + 1) * GROUP_DIM)
            scaled = d_pooled[:, cols] / jnp.minimum(pos, window).astype(F32)
            d_p.append(_window_sum(scaled, d_halo[:, cols] * (1.0 / window), window, True) - d_pooled[:, cols])
        d_u += emit(3, jnp.concatenate(d_p, axis=1))
        d_u += emit(4, dga_ref[...])
        d_u += emit(5, dgp_ref[...])

        row = lax.broadcasted_iota(jnp.int32, (tile, tile), 0)
        col = lax.broadcasted_iota(jnp.int32, (tile, tile), 1)
        later = (row >= col).astype(BF16)
        d_logf = sum(_dot(part, later) for part in _split3(dcum_ref[:, 0, :])) + carry_ref[:, 0:1]
        carry_ref[:, 0:1] = d_logf[:, 0:1]
        d_f = d_logf * _sigmoid(-f_ref[...].T[0:N_HEADS, :])
        d_f = jnp.concatenate([d_f, jnp.zeros((LANE - N_HEADS, tile), F32)], axis=0)
        dbf_ref[...] += sum(_dot_nt(ones, part) for part in _split3(d_f))
        d_fb = d_f.astype(BF16)
        d_u += _dot_tn(d_fb, wf_ref[...])
        dwf_ref[...] += _dot(d_fb, u_ref[...])

        x = x_ref[...]
        dx_ref[...] = dxa_ref[...] + d_u * mod_ref[0:1, :]
        dmod_ref[0:1, :] += jnp.sum(d_u * x, axis=0, keepdims=True)
        dmod_ref[1:2, :] += jnp.sum(d_u, axis=0, keepdims=True)

    rev = lambda step: nt - 1 - step
    tok = lambda width: pl.BlockSpec((tile, width), lambda s: (rev(s), 0))
    head_block = pl.BlockSpec((N_HEADS, LANE, tile), lambda s: (0, 0, rev(s)))
    whole = lambda a: pl.BlockSpec(a.shape, lambda s: (0,) * a.ndim)
    halo = pl.BlockSpec((HALO, D_POOL), lambda s: (jnp.minimum((rev(s) + 1) * halo_blocks, seq // HALO - 1), 0))
    small = lambda width: jax.ShapeDtypeStruct((8, width), F32)
    n_rest = N_MAIN - OFF_P
    return pl.pallas_call(
        body, name="inproj_backward", grid=(nt,),
        in_specs=[head_block, head_block, head_block, pl.BlockSpec((N_HEADS, 1, tile), lambda s: (0, 0, rev(s))),
                  tok(LANE), tok(D_POOL), halo, tok(D_ATT), tok(D_POOL),
                  tok(D), tok(D), tok(D),
                  whole(mod), whole(w_main), whole(w_f)],
        out_specs=[tok(D), tok(n_rest), pl.BlockSpec((LANE, D), lambda s: (0, 0)),
                   pl.BlockSpec((8, N_MAIN), lambda s: (0, 0)), pl.BlockSpec((8, LANE), lambda s: (0, 0)),
                   pl.BlockSpec((8, D), lambda s: (0, 0))],
        out_shape=[jax.ShapeDtypeStruct((seq, D), F32), jax.ShapeDtypeStruct((seq, n_rest), BF16),
                   jax.ShapeDtypeStruct((LANE, D), F32), small(N_MAIN), small(LANE), small(D)],
        scratch_shapes=[pltpu.VMEM((8, LANE), F32)],
        compiler_params=_params(("arbitrary",)),
    )(dqp, dkp, dvp, d_cum, f, d_pooled, d_pooled, d_ga, d_gp, x, dxa, u, mod, w_main, w_f)


def _weight_grads(dq_t, dk_t, dv_t, dw_f, dproj, u, k_tile):
    seq = u.shape[0]
    nk = seq // k_tile
    rows = N_HEADS * HEAD_DIM

    def body(dq_ref, dk_ref, dv_ref, dwf_ref, dp_ref, u_ref, out_ref, acc_ref):
        k = pl.program_id(0)

        @pl.when(k == 0)
        def _():
            acc_ref[...] = jnp.zeros_like(acc_ref)

        tokens = u_ref[...]
        for j, ref in enumerate((dq_ref, dk_ref, dv_ref)):
            acc_ref[pl.ds(j * rows, rows), :] += _dot(ref[...].reshape(rows, k_tile), tokens)
        for j in range(dproj.shape[1] // COL_CHUNK):
            cols = pl.ds(j * COL_CHUNK, COL_CHUNK)
            acc_ref[pl.ds(F_HI + j * COL_CHUNK, COL_CHUNK), :] += _dot_tn(dp_ref[:, cols], tokens)

        @pl.when(k == nk - 1)
        def _():
            acc_ref[F_LO:F_HI, :] = dwf_ref[0:N_HEADS, :]
            for slot in range(N_DEV):
                out_ref[slot] = acc_ref[slot * shard:(slot + 1) * shard, :].astype(BF16)

    shard = D_IN // N_DEV
    heads = pl.BlockSpec((N_HEADS, HEAD_DIM, k_tile), lambda k: (0, 0, k))
    return pl.pallas_call(
        body, name="weight_grads", grid=(nk,),
        in_specs=[heads, heads, heads, pl.BlockSpec(dw_f.shape, lambda k: (0, 0)),
                  pl.BlockSpec((k_tile, dproj.shape[1]), lambda k: (k, 0)), pl.BlockSpec((k_tile, D), lambda k: (k, 0))],
        out_specs=pl.BlockSpec((N_DEV, shard, D), lambda k: (0, 0, 0)),
        out_shape=jax.ShapeDtypeStruct((N_DEV, shard, D), BF16),
        scratch_shapes=[pltpu.VMEM((D_IN, D), F32)],
        compiler_params=_params(("arbitrary",)),
    )(dq_t, dk_t, dv_t, dw_f, dproj, u)


def _adamw(w, g, m, v):
    m = ADAM_B1 * m + (1.0 - ADAM_B1) * g
    v = ADAM_B2 * v + (1.0 - ADAM_B2) * (g * g)
    m_hat = m / (1.0 - ADAM_B1 ** ADAM_STEP)
    v_hat = v / (1.0 - ADAM_B2 ** ADAM_STEP)
    delta = -ADAM_LR * (m_hat / (jnp.sqrt(v_hat) + ADAM_EPS) + ADAM_WD * w)
    return delta, m, v


SUBLANES = 8


def _adamw_packed(g, w, m, v, name, chunks=4):
    rows, cols = g.shape
    per_row = cols // LANE
    assert cols % LANE == 0 and per_row == SUBLANES and w.shape == (rows * per_row, LANE)
    step = -(-rows // (chunks * SUBLANES)) * SUBLANES
    bounds = [(r0, min(r0 + step, rows)) for r0 in range(0, rows, step)]

    def body(g_hbm, w_hbm, m_hbm, v_hbm, og_hbm, od_hbm, om_hbm, ov_hbm, g_buf, in_buf, out_buf, in_sems, out_sems):
        def copies_in(c):
            r0, r1 = bounds[c]
            packed = slice(r0 * per_row, r1 * per_row)
            return [pltpu.make_async_copy(g_hbm.at[r0:r1], g_buf.at[r0:r1], in_sems.at[c, 0])] + [
                pltpu.make_async_copy(src.at[packed], in_buf.at[i, packed], in_sems.at[c, 1 + i])
                for i, src in enumerate((w_hbm, m_hbm, v_hbm))]

        def copies_out(c):
            r0, r1 = bounds[c]
            packed = slice(r0 * per_row, r1 * per_row)
            return [pltpu.make_async_copy(out_buf.at[i, packed], dst.at[packed], out_sems.at[c, i])
                    for i, dst in enumerate((og_hbm, od_hbm, om_hbm, ov_hbm))]

        for c in range(len(bounds)):
            for cp in copies_in(c):
                cp.start()
        for c, (r0, r1) in enumerate(bounds):
            for cp in copies_in(c):
                cp.wait()
            for j in range(per_row):
                lanes = pl.ds(r0 * per_row + j, r1 - r0, stride=per_row)
                g_part = g_buf[r0:r1, j * LANE:(j + 1) * LANE]
                results = _adamw(in_buf[0, lanes, :], g_part, in_buf[1, lanes, :], in_buf[2, lanes, :])
                for i, val in enumerate((g_part,) + results):
                    out_buf[i, lanes, :] = val
            for cp in copies_out(c):
                cp.start()
        for c in range(len(bounds)):
            for cp in copies_out(c):
                cp.wait()

    shape = jax.ShapeDtypeStruct(w.shape, F32)
    return pl.pallas_call(
        body, name=name,
        in_specs=[ANY] * 4, out_specs=[ANY] * 4, out_shape=[shape] * 4,
        scratch_shapes=[pltpu.VMEM(g.shape, F32), pltpu.VMEM((3,) + w.shape, F32), pltpu.VMEM((4,) + w.shape, F32),
                        pltpu.SemaphoreType.DMA((len(bounds), 4)), pltpu.SemaphoreType.DMA((len(bounds), 4))],
        compiler_params=_params(),
    )(g, w, m, v)


def _ada_adamw(sc_all, d_ada, w, m, v, chunks=4):
    rows, cols = w.shape
    step, sub = rows // chunks, 32
    assert rows % chunks == 0 and step % LANE == 0 and step % sub == 0

    def body(sc_ref, d_ref, w_hbm, m_hbm, v_hbm, og_hbm, od_hbm, om_hbm, ov_hbm, in_buf, out_buf, in_sems, out_sems):
        def copies_in(c):
            part = slice(c * step, (c + 1) * step)
            return [pltpu.make_async_copy(src.at[part], in_buf.at[i, part], in_sems.at[c, i])
                    for i, src in enumerate((w_hbm, m_hbm, v_hbm))]

        def copies_out(c):
            part = slice(c * step, (c + 1) * step)
            return [pltpu.make_async_copy(out_buf.at[i, part], dst.at[part], out_sems.at[c, i])
                    for i, dst in enumerate((og_hbm, od_hbm, om_hbm, ov_hbm))]

        for c in range(chunks):
            for cp in copies_in(c):
                cp.start()
        for c in range(chunks):
            sc_t = sc_ref[:, c * step:(c + 1) * step].T
            for cp in copies_in(c):
                cp.wait()
            for r0 in range(0, step, sub):
                part = slice(c * step + r0, c * step + r0 + sub)
                g = sc_t[r0:r0 + sub, 0:1] * d_ref[0:1, :]
                for b in range(1, N_DEV):
                    g = g + sc_t[r0:r0 + sub, b:b + 1] * d_ref[b:b + 1, :]
                results = _adamw(in_buf[0, part, :], g, in_buf[1, part, :], in_buf[2, part, :])
                for i, val in enumerate((g,) + results):
                    out_buf[i, part, :] = val
            for cp in copies_out(c):
                cp.start()
        for c in range(chunks):
            for cp in copies_out(c):
                cp.wait()

    in_vmem = pl.BlockSpec(memory_space=pltpu.VMEM)
    shape = jax.ShapeDtypeStruct(w.shape, F32)
    return pl.pallas_call(
        body, name="ada_adamw",
        in_specs=[in_vmem, in_vmem, ANY, ANY, ANY], out_specs=[ANY] * 4, out_shape=[shape] * 4,
        scratch_shapes=[pltpu.VMEM((3,) + w.shape, F32), pltpu.VMEM((4,) + w.shape, F32),
                        pltpu.SemaphoreType.DMA((chunks, 3)), pltpu.SemaphoreType.DMA((chunks, 4))],
        compiler_params=_params(),
    )(sc_all, d_ada, w, m, v)


F_LO, F_HI = 3 * D_ATT, 3 * D_ATT + N_HEADS


def _split_forget(a, axis):
    idx = lambda lo, hi: tuple(slice(lo, hi) if d == axis else slice(None) for d in range(a.ndim))
    pad = [(0, LANE - N_HEADS) if d == axis else (0, 0) for d in range(a.ndim)]
    return jnp.concatenate([a[idx(0, F_LO)], a[idx(F_HI, D_IN)]], axis=axis), jnp.pad(a[idx(F_LO, F_HI)], pad)


def _join_forget(main, f, axis):
    idx = lambda lo, hi: tuple(slice(lo, hi) if d == axis else slice(None) for d in range(main.ndim))
    return jnp.concatenate([main[idx(0, F_LO)], f[idx(0, N_HEADS)], main[idx(F_LO, N_MAIN)]], axis=axis)


def _adamw_small(grad_rows, row_params, whole_params, summed_params, scalar_at):
    n_row, n_whole, n_sum = len(row_params), len(whole_params), len(summed_params)
    n = n_row + n_whole + n_sum

    def body(g_ref, *refs):
        n_in = 3 * n_row + 4 * (n_whole + n_sum)
        ins, outs = list(refs[:n_in]), refs[n_in:]
        for i in range(n):
            if i < n_row:
                row, lo, hi = row_params[i][:3]
                g = g_ref[row:row + 1, lo:hi]
            elif i < n_row + n_whole:
                g = ins.pop(0)[...]
            else:
                parts = ins.pop(0)
                g = parts[0]
                for k in range(1, N_DEV):
                    g = g + parts[k]
            w, m, v = (ins.pop(0)[...] for _ in range(3))
            outs[4 * i][...] = g
            outs[4 * i + 1][...], outs[4 * i + 2][...], outs[4 * i + 3][...] = _adamw(w, g, m, v)
        row, lane = scalar_at
        outs[4 * n][...] = g_ref[row:row + 1, lane:lane + 1]

    shapes = [p[3] for p in row_params] + [p[1] for p in whole_params] + [p[1] for p in summed_params]
    operands = [a for p in row_params for a in p[3:]] + [a for p in whole_params + summed_params for a in p]
    flat = pl.pallas_call(
        body, name="adamw_small",
        out_shape=[jax.ShapeDtypeStruct(w.shape, F32) for w in shapes for _ in range(4)]
        + [jax.ShapeDtypeStruct((1, 1), F32)],
        compiler_params=_params(),
    )(grad_rows, *operands)
    return [flat[4 * i:4 * i + 4] for i in range(n)], flat[4 * n].reshape(())


def kernel(x, c, w_ada, b_ada, w_in, b_in, w_pool_mix, b_pool_mix, pool_scale, w_out, b_out, ln_g, ln_b, loss_target, m_w_ada, m_b_ada, m_w_in, m_b_in, m_w_pool_mix, m_b_pool_mix, m_pool_scale, m_w_out, m_b_out, m_ln_g, m_ln_b, v_w_ada, v_b_ada, v_w_in, v_b_in, v_w_pool_mix, v_b_pool_mix, v_pool_scale, v_w_out, v_b_out, v_ln_g, v_ln_b):
    seq = x.shape[1]
    tile = min(256, seq)
    attn_tile = min(512, max(128, seq // 4))
    me = _dev_index(*_mesh_pos())
    x2, tgt = x[0], loss_target[0]

    rows_of = lambda a: jnp.swapaxes(a, 1, 2)[0]
    w_main, w_f, sc_all, ada_mine = _gather_and_ada(c, rows_of(w_in).astype(BF16), w_ada[0])
    ada = ada_mine.reshape(1, D_ADA) + b_ada
    shift, scale, gate = ada[:, 0:D], ada[:, D:2 * D], ada[:, 2 * D:]
    mod = jnp.concatenate([1.0 + scale, shift, jnp.zeros((6, D), F32)], axis=0)
    b_main, b_f = _split_forget(b_in, 1)

    qp, kp, vp, f, p, g_att, g_pool, u = _inproj_forward(x2, mod, w_main, w_f, b_main, b_f, tile)
    att, q2t, w_out_g = _attention_forward(qp, kp, vp, w_out[0].astype(BF16), attn_tile)

    vecs = jnp.concatenate([gate, b_out, ln_g, ln_b, jnp.zeros((4, D), F32)], axis=0)
    pool_vecs = jnp.concatenate([b_pool_mix.reshape(1, D_POOL), pool_scale, jnp.zeros((6, D_POOL), F32)], axis=0)
    dxa, do2, d_ga, d_gp, d_pooled, gw_out, dw_pool, dvec = _middle(
        x2, tgt, att, g_att, g_pool, p, vecs, pool_vecs, w_out_g.reshape(D, D), w_pool_mix[0].astype(BF16), tile)

    pool_rows = w_pool_mix.shape[1] * GROUP_DIM
    dqp, dkp, dvp, d_cum, g_out, dvec_sum, dw_pool_sum = _attention_backward(
        q2t, kp, vp, do2, gw_out.reshape(N_DEV, D // N_DEV, D), dvec, dw_pool.reshape(pool_rows, GROUP_DIM), attn_tile)
    dx, dproj, dw_f, db_main, db_f, dmod = _inproj_backward(
        dqp, dkp, dvp, d_cum, f, d_pooled, d_ga, d_gp, x2, dxa, u, mod, w_main, w_f, tile)
    gw_in = _weight_grads(dqp, dkp, dvp, dw_f, dproj, u, min(512, seq))
    d_ada = jnp.concatenate([dmod[1:2], dmod[0:1], dvec[5:6]], axis=1)
    g_in_rows, g_b_in, d_ada_all = _reduce_grads(gw_in, _join_forget(db_main[0:1], db_f[0:1], 1), d_ada)

    packed = lambda a: jnp.transpose(a.reshape(SUBLANES, LANE, -1), (2, 0, 1)).reshape(-1, LANE)
    outs_in = _adamw_packed(g_in_rows, packed(w_in), packed(m_w_in), packed(v_w_in), "adamw_w_in")
    g_w_in, d_w_in, nm_w_in, nv_w_in = (
        jnp.transpose(a.reshape(-1, SUBLANES, LANE), (1, 2, 0)).reshape(D, -1) for a in outs_in)
    flat_pool = lambda a: a.reshape(1, D_POOL)
    pool_2d = lambda a: a.reshape(pool_rows, GROUP_DIM)
    rows, loss = _adamw_small(
        dvec_sum,
        [(0, 0, D, b_out, m_b_out, v_b_out), (1, 0, D, ln_g, m_ln_g, v_ln_g), (2, 0, D, ln_b, m_ln_b, v_ln_b),
         (3, 0, D_POOL, flat_pool(b_pool_mix), flat_pool(m_b_pool_mix), flat_pool(v_b_pool_mix)),
         (3, D_POOL, 2 * D_POOL, pool_scale, m_pool_scale, v_pool_scale)],
        [(g_out, w_out[0], m_w_out[0], v_w_out[0]),
         (dw_pool_sum, pool_2d(w_pool_mix), pool_2d(m_w_pool_mix), pool_2d(v_w_pool_mix)),
         (g_b_in, b_in, m_b_in, v_b_in)],
        [(d_ada_all, b_ada, m_b_ada, v_b_ada)],
        scalar_at=(4, 0))
    small = {"b_out": rows[0], "ln_g": rows[1], "ln_b": rows[2],
             "b_pool": [a.reshape(b_pool_mix.shape) for a in rows[3]], "pool_scale": rows[4],
             "w_pool": [a.reshape(w_pool_mix.shape) for a in rows[6]], "b_in": rows[7]}
    g_s, d_s, nm_s, nv_s = ({k: r[j] for k, r in small.items()} for j in range(4))
    g_w_out, d_w_out, nm_w_out, nv_w_out = rows[5]
    g_b_ada, d_b_ada, nm_b_ada, nv_b_ada = rows[8]

    d_ada_local = lax.dynamic_slice_in_dim(d_ada_all.reshape(N_DEV, D_ADA), me * (D_ADA // N_DEV), D_ADA // N_DEV, axis=1)
    g_w_ada, d_w_ada, nm_w_ada, nv_w_ada = _ada_adamw(sc_all, d_ada_local, w_ada[0], m_w_ada[0], v_w_ada[0])

    def ordered(w_ada_, b_ada_, w_in_, w_out_, s):
        return (w_ada_[None], b_ada_, w_in_[None], s["b_in"], s["w_pool"], s["b_pool"], s["pool_scale"],
                w_out_[None], s["b_out"], s["ln_g"], s["ln_b"])

    return (loss, dx[None],
            *ordered(g_w_ada, g_b_ada, g_w_in, g_w_out, g_s),
            *ordered(d_w_ada, d_b_ada, d_w_in, d_w_out, d_s),
            *ordered(nm_w_ada, nm_b_ada, nm_w_in, nm_w_out, nm_s),
            *ordered(nv_w_ada, nv_b_ada, nv_w_in, nv_w_out, nv_s))
```

```python
import jax
import jax.numpy as jnp
from jax import lax
from jax.experimental import pallas as pl
from jax.experimental.pallas import tpu as pltpu

F32 = jnp.float32
BF16 = jnp.bfloat16

N_DEV = 8
D = 1024
N_HEADS = 8
HEAD_DIM = 64
D_ATT = 512
D_POOL = 512
POOL_WINDOWS = (2, 4, 8, 16)
GROUP_DIM = 128
HALO = 16
LANE = 128
D_IN = 3080
D_ADA = 3072
N_MAIN = 3072
OFF_P = 1536
COL_CHUNK = 512
Q_SCALE = 0.125
LN_EPS = 1e-5
ALPHA = 2.0 ** 0.25
L_CQ, L_CK, L_LSE = 64, 67, 70

ADAM_LR, ADAM_B1, ADAM_B2, ADAM_EPS, ADAM_WD, ADAM_STEP = 0.001, 0.9, 0.999, 1e-08, 0.01, 10
VMEM_LIMIT = 56 * 1024 * 1024

MESH = pl.DeviceIdType.MESH
ANY = pl.BlockSpec(memory_space=pl.ANY)


def _params(sem=None, vmem=VMEM_LIMIT):
    return pltpu.CompilerParams(dimension_semantics=sem, vmem_limit_bytes=vmem)


def _split3(a):
    hi = a.astype(BF16)
    r = a - hi.astype(F32)
    mid = r.astype(BF16)
    lo = (r - mid.astype(F32)).astype(BF16)
    return hi, mid, lo


def _dot(a, b):
    return jnp.dot(a, b, preferred_element_type=F32)


def _dot_nt(a, b):
    return lax.dot_general(a, b, (((1,), (1,)), ((), ())), preferred_element_type=F32)


def _dot_tn(a, b):
    return lax.dot_general(a, b, (((0,), (0,)), ((), ())), preferred_element_type=F32)


def _dot3(m01, a):
    hi, mid, lo = _split3(a)
    return _dot(m01, hi) + _dot(m01, mid) + _dot(m01, lo)


def _sigmoid(z):
    return 1.0 / (1.0 + jnp.exp(-z))


def _lanes(shape):
    return lax.broadcasted_iota(jnp.int32, shape, len(shape) - 1)


def _place3(lane, base, parts, other):
    out = other
    for j in range(3):
        out = jnp.where(lane == base + j, parts[j], out)
    return out


def _mesh_pos():
    return lax.axis_index("x"), lax.axis_index("y"), lax.axis_index("c")


def _dev_index(px, py, pc):
    return 4 * px + 2 * py + pc


N_GATHER_SEMS = 9


def _gather_stages(src_ref, out_ref, send_sems, recv_sems, local_sem):
    x, y, c = _mesh_pos()
    me, sibling = (x, y, c), (x, y, 1 - c)
    nbr_x, nbr_y, diag = (1 - x, y), (x, 1 - y), (1 - x, 1 - y)
    half = out_ref.shape[-1] // 2
    left, right = pl.ds(0, half), pl.ds(half, half)

    def copy(k, block, to, cols=None, src=None):
        slot = out_ref.at[_dev_index(*block)]
        if cols is not None:
            slot = slot.at[:, cols]
        return pltpu.make_async_remote_copy(
            src_ref=slot if src is None else src, dst_ref=slot, send_sem=send_sems.at[k], recv_sem=recv_sems.at[k],
            device_id=to, device_id_type=MESH)

    mine = pltpu.make_async_copy(src_ref, out_ref.at[_dev_index(*me)], local_sem)
    first = [copy(0, me, sibling, src=src_ref), copy(1, me, (*nbr_x, c), src=src_ref), copy(2, me, (*nbr_y, c), src=src_ref)]
    relay = [(1, nbr_x, None, nbr_x), (2, nbr_y, None, nbr_y), (3, diag, left, nbr_y), (4, diag, right, nbr_x)]
    onward = [copy(3, (*nbr_x, c), (*nbr_y, c), cols=left), copy(4, (*nbr_y, c), (*nbr_x, c), cols=right)]
    passed = [copy(4 + k, (*block, c), sibling, cols=cols) for k, block, cols, _ in relay]

    def start():
        mine.start()
        for cp in first:
            cp.start()

    def relay_stage(first_item):
        def run():
            for j in (first_item, first_item + 1):
                k, block, cols, frm = relay[j]
                copy(k, (*block, c), (*frm, c), cols=cols).wait_recv()
                if j < 2:
                    onward[j].start()
                passed[j].start()
        return run

    def finish():
        copy(0, sibling, me).wait_recv()
        for k, block, cols, _ in relay:
            copy(4 + k, (*block, 1 - c), me, cols=cols).wait_recv()
        for cp in first + onward + passed:
            cp.wait_send()
        mine.wait()

    return start, relay_stage(0), relay_stage(2), finish


N_REDUCE_SEMS = 10
N_SMALL_SEMS = 4
N_ROWS_SEMS = 7


def _reduce_stages(ins, gs, r1, s2, r2, smalls, send_sems, recv_sems, rows=None, own=None):
    n = len(ins)
    x, y, c = _mesh_pos()
    me = _dev_index(x, y, c)
    sibling = (x, y, 1 - c)
    chips = [(x, y), (1 - x, y), (x, 1 - y), (1 - x, 1 - y)]
    peers = []
    for p in range(1, N_DEV):
        px, py, pc = (p >> 2) & 1, (p >> 1) & 1, p & 1
        peers.append((1 - x if px else x, 1 - y if py else y, 1 - c if pc else c))
    base_small = N_REDUCE_SEMS * n

    def remote(src, dst, k, to):
        return pltpu.make_async_remote_copy(src_ref=src, dst_ref=dst, send_sem=send_sems.at[k],
                                            recv_sem=recv_sems.at[k], device_id=to, device_id_type=MESH)

    def level1(a, q):
        return remote(ins[a].at[_dev_index(*chips[q], 1 - c)], r1[a].at[q], N_REDUCE_SEMS * a + q, sibling)

    def level2(a, k):
        half = ins[a].shape[-1] // 2
        left, right = pl.ds(0, half), pl.ds(half, half)
        nbr_x, nbr_y = (*chips[1], c), (*chips[2], c)
        src_slot, dst_slot, cols, to = [(0, 0, left, nbr_x), (1, 1, right, nbr_y), (2, 2, left, nbr_x),
                                        (2, 2, right, nbr_y), (0, 0, right, nbr_x), (1, 1, left, nbr_y)][k]
        return remote(s2[a].at[src_slot, :, cols], r2[a].at[dst_slot, :, cols], N_REDUCE_SEMS * a + 4 + k, to)

    to_sibling = [remote(sm[0], sm[2], base_small + 4 * i, sibling) for i, sm in enumerate(smalls)]
    to_chips = [[remote(sm[3], sm[4].at[j], base_small + 4 * i + 1 + j, (*chips[j + 1], c)) for j in range(3)]
                for i, sm in enumerate(smalls)]
    if rows is not None:
        rows_ref, land_ref, all_ref = rows
        base_rows = base_small + 4 * len(smalls)
        row_sends = [remote(rows_ref, land_ref.at[me], base_rows + k, to) for k, to in enumerate(peers)]

    order = (3, 1, 2, 0)

    def mine(a, q):
        buf, sems = own[a]
        return pltpu.make_async_copy(ins[a].at[_dev_index(*chips[q], c)], buf.at[q], sems.at[q])

    def start():
        for a in range(n):
            for q in order:
                level1(a, q).start()
            if own is not None:
                for q in order:
                    mine(a, q).start()
        for cp in to_sibling:
            cp.start()
        if rows is not None:
            for cp in row_sends:
                cp.start()
            land_ref[me] = rows_ref[...]

    def middle():
        for a in range(n):
            for q in order:
                level1(a, q).wait_recv()
                if own is None:
                    kept = ins[a][_dev_index(*chips[q], c)]
                else:
                    mine(a, q).wait()
                    kept = own[a][0][q]
                pair = kept.astype(F32) + r1[a][q].astype(F32)
                if q == 0:
                    gs[a][...] = pair
                else:
                    s2[a][q - 1] = pair.astype(BF16)
                    for k in ((0,), (1,), (2, 3))[q - 1]:
                        level2(a, k).start()
        for i, (small_ref, _, sm_sib, sm_chip, _) in enumerate(smalls):
            to_sibling[i].wait_recv()
            sm_chip[...] = small_ref[...] + sm_sib[...]
            for cp in to_chips[i]:
                cp.start()

    def fold():
        for a in range(n):
            half = ins[a].shape[-1] // 2
            level2(a, 3).wait_recv()
            s2[a][0, :, half:] = (s2[a][0, :, half:].astype(F32) + r2[a][2, :, half:].astype(F32)).astype(BF16)
            level2(a, 4).start()
            level2(a, 2).wait_recv()
            s2[a][1, :, :half] = (s2[a][1, :, :half].astype(F32) + r2[a][2, :, :half].astype(F32)).astype(BF16)
            level2(a, 5).start()

    def finish():
        for a in range(n):
            for k in (0, 1, 4, 5):
                level2(a, k).wait_recv()
            gs[a][...] = gs[a][...] + r2[a][0].astype(F32) + r2[a][1].astype(F32)
            for q in range(4):
                level1(a, q).wait_send()
            for k in range(6):
                level2(a, k).wait_send()
        for i, (_, total_ref, _, sm_chip, sm_recv) in enumerate(smalls):
            for cp in to_chips[i]:
                cp.wait_recv()
            total = None
            for ax in range(2):
                for ay in range(2):
                    dx, dy = x != ax, y != ay
                    term = jnp.where(dx, jnp.where(dy, sm_recv[2], sm_recv[0]), jnp.where(dy, sm_recv[1], sm_chip[...]))
                    total = term if total is None else total + term
            total_ref[...] = total
            for cp in [to_sibling[i]] + to_chips[i]:
                cp.wait_send()
        if rows is not None:
            for k, frm in enumerate(peers):
                remote(rows_ref, land_ref.at[_dev_index(*frm)], base_rows + k, frm).wait_recv()
            all_ref[...] = land_ref[...]
            for cp in row_sends:
                cp.wait_send()

    return start, middle, fold, finish


def _reduce_scratch(shard, smalls, rows=None):
    out = [pltpu.VMEM((lead,) + shard.shape[1:], BF16) for lead in (4, 3, 3)]
    for small in smalls:
        out += [pltpu.VMEM(small.shape, F32), pltpu.VMEM(small.shape, F32), pltpu.VMEM((3,) + small.shape, F32)]
    n_sems = N_REDUCE_SEMS + N_SMALL_SEMS * len(smalls)
    if rows is not None:
        out.append(pltpu.VMEM((N_DEV,) + rows.shape, F32))
        n_sems += N_ROWS_SEMS
    return out + [pltpu.SemaphoreType.DMA((n_sems,))] * 2


def _reduce_grads(gw_in, small, rows):
    def body(in_ref, small_ref, rows_ref, g_ref, total_ref, rows_all_ref,
             r1, s2, r2, sm_sib, sm_chip, sm_recv, rows_land, send_sems, recv_sems, kept, kept_sems):
        stages = _reduce_stages(
            [in_ref], [g_ref], [r1], [s2], [r2], [(small_ref, total_ref, sm_sib, sm_chip, sm_recv)],
            send_sems, recv_sems, rows=(rows_ref, rows_land, rows_all_ref), own=[(kept, kept_sems)])
        for stage in stages:
            stage()

    vmem = pl.BlockSpec(memory_space=pltpu.VMEM)
    return pl.pallas_call(
        body, name="reduce_grads",
        in_specs=[ANY, vmem, vmem], out_specs=[vmem, vmem, vmem],
        out_shape=[jax.ShapeDtypeStruct(gw_in.shape[1:], F32), jax.ShapeDtypeStruct(small.shape, F32),
                   jax.ShapeDtypeStruct((N_DEV,) + rows.shape, F32)],
        scratch_shapes=_reduce_scratch(gw_in, [small], rows)
        + [pltpu.VMEM((4,) + gw_in.shape[1:], BF16), pltpu.SemaphoreType.DMA((4,))],
        compiler_params=_params(),
    )(gw_in, small, rows)


def _dot3_rhs(a, b):
    a0, a1, a2 = _split3(a)
    b0, b1, b2 = _split3(b)
    return (_dot(a0, b0) + (_dot(a0, b1) + _dot(a1, b0))
            + (_dot(a0, b2) + _dot(a1, b1) + _dot(a2, b0)))


def _gather_and_ada(c, w_in_rows, w_ada):
    cols = w_ada.shape[1]
    shard = w_in_rows.shape[0]

    def body(c_ref, w_ref, wa_ref, w_main_ref, w_f_ref, sc_ref, ada_ref,
             w_all_ref, w_f32, c_land, part, ada_land, send_sems, recv_sems, local_sem, x_send, x_recv):
        x, y, cc = _mesh_pos()
        me = _dev_index(x, y, cc)
        peers = []
        for p in range(1, N_DEV):
            px, py, pc = (p >> 2) & 1, (p >> 1) & 1, p & 1
            peers.append((1 - x if px else x, 1 - y if py else y, 1 - cc if pc else cc))

        def remote(src, dst, k, to):
            return pltpu.make_async_remote_copy(src_ref=src, dst_ref=dst, send_sem=x_send.at[k], recv_sem=x_recv.at[k],
                                                device_id=to, device_id_type=MESH)

        c_sends = [remote(c_ref, c_land.at[me], k, to) for k, to in enumerate(peers)]
        for cp in c_sends:
            cp.start()
        start, relay_near, relay_far, finish = _gather_stages(w_ref, w_all_ref, send_sems, recv_sems, local_sem.at[0])
        start()
        c_land[me] = c_ref[...]
        for k, frm in enumerate(peers):
            remote(c_ref, c_land.at[_dev_index(*frm)], k, frm).wait_recv()
        c_all = jnp.concatenate([c_land[b] for b in range(N_DEV)], axis=0)
        sc = c_all * _sigmoid(c_all)
        sc_ref[...] = sc
        rows = _dot3_rhs(sc, wa_ref[...])
        for b in range(N_DEV):
            part[b] = rows[b:b + 1, :]
        a_sends = [remote(part.at[_dev_index(*to)], ada_land.at[me], 7 + k, to) for k, to in enumerate(peers)]
        for cp in a_sends:
            cp.start()
        ada_land[me] = part[me]

        relay_near()
        relay_far()
        for k, frm in enumerate(peers):
            remote(part.at[0], ada_land.at[_dev_index(*frm)], 7 + k, frm).wait_recv()
        ada_ref[...] = ada_land[...]
        finish()
        for cp in c_sends + a_sends:
            cp.wait_send()

        for slot in range(N_DEV):
            w_f32[slot * shard:(slot + 1) * shard, :] = w_all_ref[slot].astype(F32)
        w_main_ref[0:F_LO, :] = w_f32[0:F_LO, :].astype(BF16)
        w_main_ref[F_LO:N_MAIN, :] = w_f32[F_HI:D_IN, :].astype(BF16)
        w_f_ref[...] = jnp.concatenate(
            [w_f32[F_LO:F_HI, :], jnp.zeros((LANE - N_HEADS, D), F32)], axis=0).astype(BF16)

    vmem = pl.BlockSpec(memory_space=pltpu.VMEM)
    return pl.pallas_call(
        body, name="gather_weights",
        in_specs=[vmem, ANY, vmem], out_specs=[vmem, vmem, vmem, vmem],
        out_shape=[jax.ShapeDtypeStruct((N_MAIN, D), BF16), jax.ShapeDtypeStruct((LANE, D), BF16),
                   jax.ShapeDtypeStruct((N_DEV, D), F32), jax.ShapeDtypeStruct((N_DEV, 1, cols), F32)],
        scratch_shapes=[pltpu.VMEM((N_DEV,) + w_in_rows.shape, BF16), pltpu.VMEM((D_IN, D), F32),
                        pltpu.VMEM((N_DEV, 1, D), F32), pltpu.VMEM((N_DEV, 1, cols), F32), pltpu.VMEM((N_DEV, 1, cols), F32),
                        pltpu.SemaphoreType.DMA((N_GATHER_SEMS,)), pltpu.SemaphoreType.DMA((N_GATHER_SEMS,)),
                        pltpu.SemaphoreType.DMA((1,)),
                        pltpu.SemaphoreType.DMA((14,)), pltpu.SemaphoreType.DMA((14,))],
        compiler_params=_params(),
    )(c, w_in_rows, w_ada)


def _inproj_forward(x, mod, w_main, w_f, b_main, b_f, tile):
    seq = x.shape[0]
    nt = seq // tile

    def body(x_ref, mod_ref, w_ref, wf_ref, b_ref, bf_ref,
             qp_ref, kp_ref, vp_ref, f_ref, p_ref, ga_ref, gp_ref, u_ref, carry_ref):
        i = pl.program_id(0)

        @pl.when(i == 0)
        def _():
            carry_ref[...] = jnp.zeros_like(carry_ref)

        u = x_ref[...] * mod_ref[0:1, :] + mod_ref[1:2, :]
        ub = u.astype(BF16)
        u_ref[...] = ub

        f = _dot_nt(ub, wf_ref[...]) + bf_ref[...]
        f_ref[...] = f
        lane = _lanes((tile, LANE))
        log_f = jnp.where(lane < N_HEADS, jnp.minimum(f, 0.0) - jnp.log(1.0 + jnp.exp(-jnp.abs(f))), 0.0)
        row = lax.broadcasted_iota(jnp.int32, (tile, tile), 0)
        col = lax.broadcasted_iota(jnp.int32, (tile, tile), 1)
        tri = (row >= col).astype(BF16)
        cum = _dot3(tri, log_f) + carry_ref[0:1, :]
        carry_ref[0:1, :] = cum[tile - 1:tile, :]
        cq = [part.astype(F32) for part in _split3(cum)]
        ck = [part.astype(F32) for part in _split3(-cum)]

        def proj(chunk):
            cols = pl.ds(chunk * COL_CHUNK, COL_CHUNK)
            return _dot_nt(ub, w_ref[cols, :]) + b_ref[:, cols]

        def head_tiles(r):
            for pair in range(N_HEADS // 2):
                both = r[:, pair * LANE:(pair + 1) * LANE]
                yield 2 * pair, both
                yield 2 * pair + 1, pltpu.roll(both, HEAD_DIM, 1)

        for h, val in head_tiles(proj(0)):
            extra = jnp.where((lane >= L_CK) & (lane < L_CK + 3), 1.0, 0.0)
            extra = _place3(lane, L_CQ, [part[:, h:h + 1] for part in cq], extra)
            qp_ref[h] = jnp.where(lane < HEAD_DIM, val * Q_SCALE, extra).astype(BF16)
        for h, val in head_tiles(proj(1)):
            ones = ((lane >= L_CQ) & (lane < L_CQ + 3)) | ((lane >= L_LSE) & (lane < L_LSE + 3))
            extra = _place3(lane, L_CK, [part[:, h:h + 1] for part in ck], jnp.where(ones, 1.0, 0.0))
            kp_ref[h] = jnp.where(lane < HEAD_DIM, val, extra).astype(BF16)
        for h, val in head_tiles(proj(2)):
            extra = jnp.where((lane >= HEAD_DIM) & (lane < HEAD_DIM + 3), -1.0, 0.0)
            vp_ref[h] = jnp.where(lane < HEAD_DIM, val, extra).astype(BF16)
        p_ref[...] = proj(3)
        ga_ref[...] = proj(4)
        gp_ref[...] = proj(5)

    head_block = pl.BlockSpec((N_HEADS, tile, LANE), lambda i: (0, i, 0))
    tok = lambda width: pl.BlockSpec((tile, width), lambda i: (i, 0))
    whole = lambda a: pl.BlockSpec(a.shape, lambda i: (0,) * a.ndim)
    padded = jax.ShapeDtypeStruct((N_HEADS, seq, LANE), BF16)
    half = jax.ShapeDtypeStruct((seq, D_ATT), F32)
    return pl.pallas_call(
        body, name="inproj_forward", grid=(nt,),
        in_specs=[tok(D), whole(mod), whole(w_main), whole(w_f), whole(b_main), whole(b_f)],
        out_specs=[head_block, head_block, head_block, tok(LANE), tok(D_POOL), tok(D_ATT), tok(D_POOL),
                   tok(D)],
        out_shape=[padded, padded, padded, jax.ShapeDtypeStruct((seq, LANE), F32), half, half, half,
                   jax.ShapeDtypeStruct((seq, D), BF16)],
        scratch_shapes=[pltpu.VMEM((8, LANE), F32)],
        compiler_params=_params(("arbitrary",)),
    )(x, mod, w_main, w_f, b_main, b_f)


def _attention_forward(qp, kp, vp, w_out, tile):
    seq = qp.shape[1]
    nb = seq // tile
    steps = (N_HEADS // 2) * nb

    def body(q_ref, k_ref, v_ref, wo_ref, att_ref, q2t_ref, wo_all_ref, s_a, s_b, m_ref, acc_ref,
             send_sems, recv_sems, local_sem):
        step = pl.program_id(0) * nb + pl.program_id(1)
        start, relay_near, relay_far, finish = _gather_stages(wo_ref, wo_all_ref, send_sems, recv_sems, local_sem.at[0])
        pl.when(step == 0)(start)
        pl.when(step == steps // 4)(relay_near)
        pl.when(step == (3 * steps) // 4)(relay_far)

        i = pl.program_id(1)
        sub = lax.broadcasted_iota(jnp.int32, (LANE, tile), 0)
        row = lax.broadcasted_iota(jnp.int32, (tile, tile), 0)
        col = lax.broadcasted_iota(jnp.int32, (tile, tile), 1)
        q = [q_ref[0], q_ref[1]]

        def scores(buf, kb):
            rows = pl.ds(pl.multiple_of(kb * tile, tile), tile)
            for hh in range(2):
                buf[hh] = _dot_nt(k_ref[hh, rows, :], q[hh])

        def absorb(buf, kb, masked):
            rows = pl.ds(pl.multiple_of(kb * tile, tile), tile)
            for hh in range(2):
                m = m_ref[hh, 0:1, :]
                s = buf[hh]
                if masked:
                    s = jnp.where(row <= col, s, -1e30)
                m_new = jnp.maximum(m, jnp.max(s, axis=0, keepdims=True))
                p = jnp.exp(s - m_new).astype(BF16)
                acc_ref[hh] = jnp.exp(m - m_new) * acc_ref[hh] + _dot_tn(v_ref[hh, rows, :], p)
                m_ref[hh, 0:1, :] = m_new

        def two_blocks(j, _):
            scores(s_b, 2 * j + 1)
            absorb(s_a, 2 * j, False)
            scores(s_a, 2 * j + 2)
            absorb(s_b, 2 * j + 1, False)
            return 0

        def last_block():
            absorb(s_a, i, True)

        def last_two_blocks():
            scores(s_b, i)
            absorb(s_a, i - 1, False)
            absorb(s_b, i, True)

        scores(s_a, 0)
        m_ref[...] = jnp.full(m_ref.shape, -1e30, F32)
        acc_ref[...] = jnp.zeros_like(acc_ref)
        lax.fori_loop(0, i // 2, two_blocks, 0)
        lax.cond(i % 2 == 0, last_block, last_two_blocks)
        outs = []
        for hh in range(2):
            m, acc = m_ref[hh, 0:1, :], acc_ref[hh]
            l = -acc[HEAD_DIM:HEAD_DIM + 1, :]
            outs.append((acc / l)[:HEAD_DIM, :])
            neg_lse = [part.astype(F32) for part in _split3(-(m + jnp.log(l)))]
            q2t_ref[hh] = _place3(sub, L_LSE, neg_lse, q[hh].astype(F32).T).astype(BF16)
        att_ref[...] = jnp.concatenate(outs, axis=0).T
        pl.when(step == steps - 1)(finish)

    pair = pl.BlockSpec((2, tile, LANE), lambda hp, i: (hp, i, 0))
    full = pl.BlockSpec((2, seq, LANE), lambda hp, i: (hp, 0, 0))
    return pl.pallas_call(
        body, name="attention_forward", grid=(N_HEADS // 2, nb),
        in_specs=[pair, full, full, ANY],
        out_specs=[pl.BlockSpec((tile, LANE), lambda hp, i: (i, hp)),
                   pl.BlockSpec((2, LANE, tile), lambda hp, i: (hp, 0, i)), ANY],
        out_shape=[jax.ShapeDtypeStruct((seq, D_ATT), F32),
                   jax.ShapeDtypeStruct((N_HEADS, LANE, seq), BF16),
                   jax.ShapeDtypeStruct((N_DEV,) + w_out.shape, w_out.dtype)],
        scratch_shapes=[pltpu.VMEM((2, tile, tile), F32), pltpu.VMEM((2, tile, tile), F32),
                        pltpu.VMEM((2, 8, tile), F32), pltpu.VMEM((2, LANE, tile), F32),
                        pltpu.SemaphoreType.DMA((N_GATHER_SEMS,)), pltpu.SemaphoreType.DMA((N_GATHER_SEMS,)),
                        pltpu.SemaphoreType.DMA((1,))],
        compiler_params=_params(("arbitrary", "arbitrary")),
    )(qp, kp, vp, w_out)


def _window_sum(x, halo, window, transposed):
    tile = x.shape[0]

    def split_cat(a):
        hi = a.astype(BF16)
        return jnp.concatenate([hi, (a - hi.astype(F32)).astype(BF16)], axis=1)

    def fold(r):
        return r[:, :LANE] + r[:, LANE:]

    r = lax.broadcasted_iota(jnp.int32, (tile, tile), 0)
    c = lax.broadcasted_iota(jnp.int32, (tile, tile), 1)
    rh = lax.broadcasted_iota(jnp.int32, (HALO, HALO), 0)
    ch = lax.broadcasted_iota(jnp.int32, (HALO, HALO), 1)
    if not transposed:
        band = (c <= r) & (r - c < window)
        edge = (rh + HALO - ch) < window
    else:
        band = (r <= c) & (c - r < window)
        edge = (HALO + ch - rh) < window
    out = fold(_dot(band.astype(BF16), split_cat(x)))
    reach = fold(_dot(edge.astype(BF16), split_cat(halo)))
    if not transposed:
        return jnp.concatenate([out[:HALO] + reach, out[HALO:]], axis=0)
    return jnp.concatenate([out[:tile - HALO], out[tile - HALO:] + reach], axis=0)


def _silu_parts(g):
    sig = _sigmoid(g)
    return g * sig, sig * (1.0 + g * (1.0 - sig))


def _middle(x, tgt, att, g_att, g_pool, p, vecs, pool_vecs, w_out, w_pool, tile):
    seq = x.shape[0]
    nt = seq // tile
    halo_blocks = tile // HALO

    def body(x_ref, tgt_ref, att_ref, ga_ref, gp_ref, p_ref, ph_ref, vec_ref, pvec_ref, wo_ref, wp_ref,
             dxa_ref, do2_ref, dga_ref, dgp_ref, dpooled_ref, gwo_ref, dwp_ref, dvec_ref, dwo_ref, dpvec_ref):
        i = pl.program_id(0)

        @pl.when(i == 0)
        def _():
            dwo_ref[...] = jnp.zeros_like(dwo_ref)
            dwp_ref[...] = jnp.zeros_like(dwp_ref)
            dvec_ref[...] = jnp.zeros_like(dvec_ref)
            dpvec_ref[...] = jnp.zeros_like(dpvec_ref)

        gate, b_out, ln_g, ln_b = (vec_ref[k:k + 1, :] for k in range(4))
        b_pool, pool_scale = pvec_ref[0:1, :], pvec_ref[1:2, :]
        x = x_ref[...]
        p = p_ref[...]
        p_halo = ph_ref[...] * jnp.where(i > 0, 1.0, 0.0)
        pos = i * tile + lax.broadcasted_iota(jnp.int32, (tile, 1), 0) + 1

        pooled, mixed = [], []
        for g, window in enumerate(POOL_WINDOWS):
            cols = slice(g * GROUP_DIM, (g + 1) * GROUP_DIM)
            wsum = _window_sum(p[:, cols], p_halo[:, cols], window, False)
            count = jnp.minimum(pos, window).astype(F32)
            pooled.append(wsum / count - p[:, cols])
            mixed.append(_dot(pooled[g].astype(BF16), wp_ref[g]) + b_pool[:, cols])
        mixed = jnp.concatenate(mixed, axis=1)
        pool = mixed * pool_scale

        att = att_ref[...]
        g_att, g_pool = ga_ref[...], gp_ref[...]
        silu_a, dsilu_a = _silu_parts(g_att)
        silu_p, dsilu_p = _silu_parts(g_pool)
        y_in = jnp.concatenate([att * silu_a, pool * silu_p], axis=1)
        y = _dot(y_in.astype(BF16), wo_ref[...]) + b_out
        h = ALPHA * x + gate * y
        mu = jnp.mean(h, axis=1, keepdims=True)
        hc = h - mu
        var = jnp.mean(hc * hc, axis=1, keepdims=True)
        rstd = lax.rsqrt(var + LN_EPS)
        yhat = hc * rstd
        diff = yhat * ln_g + ln_b - tgt_ref[...]
        loss_rows = jnp.sum(diff * diff, axis=1, keepdims=True)
        d_out = diff * (1.0 / D)

        d_yhat = d_out * ln_g
        dh = rstd * (d_yhat - jnp.mean(d_yhat, axis=1, keepdims=True)
                     - yhat * jnp.mean(d_yhat * yhat, axis=1, keepdims=True))
        dxa_ref[...] = ALPHA * dh
        dy = dh * gate
        dyb = dy.astype(BF16)
        lane = _lanes((1, D))
        loss_row = jnp.where(lane == 0, (0.5 / D) * jnp.sum(loss_rows, axis=0, keepdims=True), 0.0)
        dvec_ref[5:6, :] += jnp.sum(dh * y, axis=0, keepdims=True)
        dvec_ref[0:1, :] += jnp.sum(dy, axis=0, keepdims=True)
        dvec_ref[1:2, :] += jnp.sum(d_out * yhat, axis=0, keepdims=True)
        dvec_ref[2:3, :] += jnp.sum(d_out, axis=0, keepdims=True)
        dvec_ref[4:5, :] += loss_row

        dwo_ref[...] += _dot(y_in.T.astype(BF16), dyb)
        d_yin = _dot_nt(dyb, wo_ref[...])
        d_a, d_pl = d_yin[:, :D_ATT], d_yin[:, D_ATT:]
        d_att = d_a * silu_a
        d_att_t = d_att.T
        prod_t = (d_att * att).T
        sub = lax.broadcasted_iota(jnp.int32, (HEAD_DIM, tile), 0)
        for h in range(N_HEADS):
            rows = slice(h * HEAD_DIM, (h + 1) * HEAD_DIM)
            delta = jnp.sum(prod_t[rows], axis=0, keepdims=True)
            extra = _place3(sub, 0, [part.astype(F32) for part in _split3(delta)], 0.0)
            do2_ref[h] = jnp.concatenate([d_att_t[rows], extra], axis=0).astype(BF16)
        dga_ref[...] = d_a * att * dsilu_a
        dgp_ref[...] = d_pl * pool * dsilu_p
        d_pool = d_pl * silu_p
        d_mixed = d_pool * pool_scale
        dpvec_ref[0:1, :] += jnp.sum(d_mixed, axis=0, keepdims=True)
        dpvec_ref[1:2, :] += jnp.sum(d_pool * mixed, axis=0, keepdims=True)
        d_pooled = []
        for g in range(len(POOL_WINDOWS)):
            cols = slice(g * GROUP_DIM, (g + 1) * GROUP_DIM)
            dmb = d_mixed[:, cols].astype(BF16)
            dwp_ref[g] += _dot(pooled[g].T.astype(BF16), dmb)
            d_pooled.append(_dot_nt(dmb, wp_ref[g]))
        dpooled_ref[...] = jnp.concatenate(d_pooled, axis=1)

        @pl.when(i == nt - 1)
        def _():
            gwo_ref[...] = dwo_ref[...].astype(BF16)
            dvec_ref[3:4, :] = jnp.concatenate([dpvec_ref[0:1, :], dpvec_ref[1:2, :]], axis=1)

    tok = lambda width: pl.BlockSpec((tile, width), lambda i: (i, 0))
    whole = lambda a: pl.BlockSpec(a.shape, lambda i: (0,) * a.ndim)
    halo = pl.BlockSpec((HALO, D_POOL), lambda i: (jnp.maximum(i * halo_blocks - 1, 0), 0))
    half = jax.ShapeDtypeStruct((seq, D_ATT), F32)
    outs = [jax.ShapeDtypeStruct((seq, D), F32), jax.ShapeDtypeStruct((N_HEADS, LANE, seq), BF16), half, half, half,
            jax.ShapeDtypeStruct(w_out.shape, BF16), jax.ShapeDtypeStruct(w_pool.shape, F32),
            jax.ShapeDtypeStruct(vecs.shape, F32)]
    return pl.pallas_call(
        body, name="middle", grid=(nt,),
        in_specs=[tok(D), tok(D), tok(D_ATT), tok(D_ATT), tok(D_POOL), tok(D_POOL), halo,
                  whole(vecs), whole(pool_vecs), whole(w_out), whole(w_pool)],
        out_specs=[tok(D), pl.BlockSpec((N_HEADS, LANE, tile), lambda i: (0, 0, i)),
                   tok(D_ATT), tok(D_POOL), tok(D_POOL),
                   whole(w_out), whole(w_pool), whole(vecs)],
        out_shape=outs,
        scratch_shapes=[pltpu.VMEM(w_out.shape, F32), pltpu.VMEM(pool_vecs.shape, F32)],
        compiler_params=_params(("arbitrary",)),
    )(x, tgt, att, g_att, g_pool, p, p, vecs, pool_vecs, w_out, w_pool)


def _attention_backward(q2t, kp, vp, do2t, gw_out, vecs, pool, tile):
    seq = kp.shape[1]
    nb = seq // tile
    last = N_HEADS // 2 - 1

    def body(qt_ref, k_ref, v_ref, dot_ref, gwo_hbm, vecs_hbm, pool_hbm,
             dq_ref, dk_ref, dv_ref, dcum_ref, g_out_ref, vecs_sum_ref, pool_sum_ref,
             dq_acc, dk_acc, dv_acc, gwo_ref, vecs_ref, pool_ref,
             r1, s2, r2, v_sib, v_chip, v_recv, p_sib, p_chip, p_recv, send_sems, recv_sems):
        hp = pl.program_id(0)
        start, middle, fold, finish = _reduce_stages(
            [gwo_ref], [g_out_ref], [r1], [s2], [r2],
            [(vecs_ref, vecs_sum_ref, v_sib, v_chip, v_recv), (pool_ref, pool_sum_ref, p_sib, p_chip, p_recv)],
            send_sems, recv_sems)

        @pl.when(hp == 0)
        def _():
            pltpu.sync_copy(gwo_hbm, gwo_ref)
            pltpu.sync_copy(vecs_hbm, vecs_ref)
            pltpu.sync_copy(pool_hbm, pool_ref)
            start()

        pl.when(hp == 1)(middle)
        pl.when(hp == 2)(fold)

        row = lax.broadcasted_iota(jnp.int32, (tile, tile), 0)
        col = lax.broadcasted_iota(jnp.int32, (tile, tile), 1)
        dq_acc[...] = jnp.zeros_like(dq_acc)

        def kv_block(kb, _):
            krows = pl.ds(pl.multiple_of(kb * tile, tile), tile)
            k = [k_ref[hh, krows, :] for hh in range(2)]
            v = [v_ref[hh, krows, :] for hh in range(2)]
            k_t = [k[hh].T for hh in range(2)]

            def q_block(qb, masked):
                qcols = pl.ds(pl.multiple_of(qb * tile, tile), tile)
                for hh in range(2):
                    q_t = qt_ref[hh, :, qcols]
                    do_t = dot_ref[hh, :, qcols]
                    s_t = _dot(k[hh], q_t)
                    if masked:
                        s_t = jnp.where(row <= col, s_t, -1e30)
                    p_t = jnp.exp(s_t)
                    ds_t = (p_t * _dot(v[hh], do_t)).astype(BF16)
                    dv_new = _dot_nt(do_t, p_t.astype(BF16))
                    dk_new = _dot_nt(q_t, ds_t)
                    if masked:
                        dv_acc[hh], dk_acc[hh] = dv_new, dk_new
                    else:
                        dv_acc[hh] += dv_new
                        dk_acc[hh] += dk_new
                    dq_acc[hh, :, qcols] += _dot(k_t[hh], ds_t)

            q_block(kb, True)

            def two_later_blocks(j, _):
                q_block(kb + 1 + 2 * j, False)
                q_block(kb + 2 + 2 * j, False)
                return 0

            later = nb - 1 - kb
            lax.fori_loop(0, later // 2, two_later_blocks, 0)
            pl.when(later % 2 == 1)(lambda: q_block(nb - 1, False))
            for hh in range(2):
                dk = dk_acc[hh]
                dk_ref[hh, :, krows] = dk.astype(BF16)
                dv_ref[hh, :, krows] = dv_acc[hh].astype(BF16)
                dcum_ref[hh, :, krows] = -dk[L_CK:L_CK + 1, :]
            return 0

        lax.fori_loop(0, nb, kv_block, 0)
        for hh in range(2):
            dq = dq_acc[hh]
            dcum_ref[hh] += dq[L_CQ:L_CQ + 1, :]
            dq_ref[hh] = (dq * Q_SCALE).astype(BF16)
        pl.when(hp == last)(finish)

    pair = pl.BlockSpec((2, seq, LANE), lambda hp: (hp, 0, 0))
    pair_t = pl.BlockSpec((2, LANE, seq), lambda hp: (hp, 0, 0))
    whole = lambda shape: pl.BlockSpec(shape, lambda hp: (0,) * len(shape))
    grad = jax.ShapeDtypeStruct((N_HEADS, LANE, seq), BF16)
    return pl.pallas_call(
        body, name="attention_backward", grid=(N_HEADS // 2,),
        in_specs=[pair_t, pair, pair, pair_t, ANY, ANY, ANY],
        out_specs=[pair_t, pair_t, pair_t, pl.BlockSpec((2, 1, seq), lambda hp: (hp, 0, 0)),
                   whole(gw_out.shape[1:]), whole(vecs.shape), whole(pool.shape)],
        out_shape=[grad, grad, grad, jax.ShapeDtypeStruct((N_HEADS, 1, seq), F32),
                   jax.ShapeDtypeStruct(gw_out.shape[1:], F32), jax.ShapeDtypeStruct(vecs.shape, F32),
                   jax.ShapeDtypeStruct(pool.shape, F32)],
        scratch_shapes=[pltpu.VMEM((2, LANE, seq), F32), pltpu.VMEM((2, LANE, tile), F32),
                        pltpu.VMEM((2, LANE, tile), F32), pltpu.VMEM(gw_out.shape, BF16),
                        pltpu.VMEM(vecs.shape, F32), pltpu.VMEM(pool.shape, F32)]
        + _reduce_scratch(gw_out, [vecs, pool]),
        compiler_params=_params(("arbitrary",)),
    )(q2t, kp, vp, do2t, gw_out, vecs, pool)


def _inproj_backward(dqp, dkp, dvp, d_cum, f, d_pooled, d_ga, d_gp, x, dxa, u, mod, w_main, w_f, tile):
    seq = x.shape[0]
    nt = seq // tile
    halo_blocks = tile // HALO

    def body(dq_ref, dk_ref, dv_ref, dcum_ref, f_ref, dpo_ref, dph_ref, dga_ref, dgp_ref, x_ref, dxa_ref, u_ref,
             mod_ref, w_ref, wf_ref,
             dx_ref, dproj_ref, dwf_ref, db_ref, dbf_ref, dmod_ref, carry_ref):
        step = pl.program_id(0)
        i = nt - 1 - step

        @pl.when(step == 0)
        def _():
            carry_ref[...] = jnp.zeros_like(carry_ref)
            dwf_ref[...] = jnp.zeros_like(dwf_ref)
            db_ref[...] = jnp.zeros_like(db_ref)
            dbf_ref[...] = jnp.zeros_like(dbf_ref)
            dmod_ref[...] = jnp.zeros_like(dmod_ref)

        ones = jnp.ones((8, tile), BF16)

        def emit(chunk, val):
            cols = pl.ds(chunk * COL_CHUNK, COL_CHUNK)
            db_ref[0:1, cols] += jnp.sum(val, axis=0, keepdims=True)
            vb = val.astype(BF16)
            dproj_ref[:, pl.ds((chunk - 3) * COL_CHUNK, COL_CHUNK)] = vb
            return _dot(vb, w_ref[cols, :])

        d_u = jnp.zeros((tile, D), F32)
        for chunk, ref in enumerate((dq_ref, dk_ref, dv_ref)):
            cols = pl.ds(chunk * COL_CHUNK, COL_CHUNK)
            val_t = ref[:, 0:HEAD_DIM, :].reshape(COL_CHUNK, tile)
            db_ref[:, cols] += _dot_nt(ones, val_t)
            d_u += _dot_tn(val_t, w_ref[cols, :])

        d_pooled = dpo_ref[...]
        d_halo = dph_ref[...] * jnp.where(i < nt - 1, 1.0, 0.0)
        pos = i * tile + lax.broadcasted_iota(jnp.int32, (tile, 1), 0) + 1
        d_p = []
        for g, window in enumerate(POOL_WINDOWS):
            cols = slice(g * GROUP_DIM, (g + 1) * GROUP_DIM)
            scaled = d_pooled[:, cols] / jnp.minimum(pos, window).astype(F32)
            d_p.append(_window_sum(scaled, d_halo[:, cols] * (1.0 / window), window, True) - d_pooled[:, cols])
        d_u += emit(3, jnp.concatenate(d_p, axis=1))
        d_u += emit(4, dga_ref[...])
        d_u += emit(5, dgp_ref[...])

        row = lax.broadcasted_iota(jnp.int32, (tile, tile), 0)
        col = lax.broadcasted_iota(jnp.int32, (tile, tile), 1)
        later = (row >= col).astype(BF16)
        d_logf = sum(_dot(part, later) for part in _split3(dcum_ref[:, 0, :])) + carry_ref[:, 0:1]
        carry_ref[:, 0:1] = d_logf[:, 0:1]
        d_f = d_logf * _sigmoid(-f_ref[...].T[0:N_HEADS, :])
        d_f = jnp.concatenate([d_f, jnp.zeros((LANE - N_HEADS, tile), F32)], axis=0)
        dbf_ref[...] += sum(_dot_nt(ones, part) for part in _split3(d_f))
        d_fb = d_f.astype(BF16)
        d_u += _dot_tn(d_fb, wf_ref[...])
        dwf_ref[...] += _dot(d_fb, u_ref[...])

        x = x_ref[...]
        dx_ref[...] = dxa_ref[...] + d_u * mod_ref[0:1, :]
        dmod_ref[0:1, :] += jnp.sum(d_u * x, axis=0, keepdims=True)
        dmod_ref[1:2, :] += jnp.sum(d_u, axis=0, keepdims=True)

    rev = lambda step: nt - 1 - step
    tok = lambda width: pl.BlockSpec((tile, width), lambda s: (rev(s), 0))
    head_block = pl.BlockSpec((N_HEADS, LANE, tile), lambda s: (0, 0, rev(s)))
    whole = lambda a: pl.BlockSpec(a.shape, lambda s: (0,) * a.ndim)
    halo = pl.BlockSpec((HALO, D_POOL), lambda s: (jnp.minimum((rev(s) + 1) * halo_blocks, seq // HALO - 1), 0))
    small = lambda width: jax.ShapeDtypeStruct((8, width), F32)
    n_rest = N_MAIN - OFF_P
    return pl.pallas_call(
        body, name="inproj_backward", grid=(nt,),
        in_specs=[head_block, head_block, head_block, pl.BlockSpec((N_HEADS, 1, tile), lambda s: (0, 0, rev(s))),
                  tok(LANE), tok(D_POOL), halo, tok(D_ATT), tok(D_POOL),
                  tok(D), tok(D), tok(D),
                  whole(mod), whole(w_main), whole(w_f)],
        out_specs=[tok(D), tok(n_rest), pl.BlockSpec((LANE, D), lambda s: (0, 0)),
                   pl.BlockSpec((8, N_MAIN), lambda s: (0, 0)), pl.BlockSpec((8, LANE), lambda s: (0, 0)),
                   pl.BlockSpec((8, D), lambda s: (0, 0))],
        out_shape=[jax.ShapeDtypeStruct((seq, D), F32), jax.ShapeDtypeStruct((seq, n_rest), BF16),
                   jax.ShapeDtypeStruct((LANE, D), F32), small(N_MAIN), small(LANE), small(D)],
        scratch_shapes=[pltpu.VMEM((8, LANE), F32)],
        compiler_params=_params(("arbitrary",)),
    )(dqp, dkp, dvp, d_cum, f, d_pooled, d_pooled, d_ga, d_gp, x, dxa, u, mod, w_main, w_f)


def _weight_grads(dq_t, dk_t, dv_t, dw_f, dproj, u, k_tile):
    seq = u.shape[0]
    nk = seq // k_tile
    rows = N_HEADS * HEAD_DIM

    def body(dq_ref, dk_ref, dv_ref, dwf_ref, dp_ref, u_ref, out_ref, acc_ref):
        k = pl.program_id(0)

        @pl.when(k == 0)
        def _():
            acc_ref[...] = jnp.zeros_like(acc_ref)

        tokens = u_ref[...]
        for j, ref in enumerate((dq_ref, dk_ref, dv_ref)):
            acc_ref[pl.ds(j * rows, rows), :] += _dot(ref[...].reshape(rows, k_tile), tokens)
        for j in range(dproj.shape[1] // COL_CHUNK):
            cols = pl.ds(j * COL_CHUNK, COL_CHUNK)
            acc_ref[pl.ds(F_HI + j * COL_CHUNK, COL_CHUNK), :] += _dot_tn(dp_ref[:, cols], tokens)

        @pl.when(k == nk - 1)
        def _():
            acc_ref[F_LO:F_HI, :] = dwf_ref[0:N_HEADS, :]
            for slot in range(N_DEV):
                out_ref[slot] = acc_ref[slot * shard:(slot + 1) * shard, :].astype(BF16)

    shard = D_IN // N_DEV
    heads = pl.BlockSpec((N_HEADS, HEAD_DIM, k_tile), lambda k: (0, 0, k))
    return pl.pallas_call(
        body, name="weight_grads", grid=(nk,),
        in_specs=[heads, heads, heads, pl.BlockSpec(dw_f.shape, lambda k: (0, 0)),
                  pl.BlockSpec((k_tile, dproj.shape[1]), lambda k: (k, 0)), pl.BlockSpec((k_tile, D), lambda k: (k, 0))],
        out_specs=pl.BlockSpec((N_DEV, shard, D), lambda k: (0, 0, 0)),
        out_shape=jax.ShapeDtypeStruct((N_DEV, shard, D), BF16),
        scratch_shapes=[pltpu.VMEM((D_IN, D), F32)],
        compiler_params=_params(("arbitrary",)),
    )(dq_t, dk_t, dv_t, dw_f, dproj, u)


def _adamw(w, g, m, v):
    m = ADAM_B1 * m + (1.0 - ADAM_B1) * g
    v = ADAM_B2 * v + (1.0 - ADAM_B2) * (g * g)
    m_hat = m / (1.0 - ADAM_B1 ** ADAM_STEP)
    v_hat = v / (1.0 - ADAM_B2 ** ADAM_STEP)
    delta = -ADAM_LR * (m_hat / (jnp.sqrt(v_hat) + ADAM_EPS) + ADAM_WD * w)
    return delta, m, v


SUBLANES = 8


def _adamw_packed(g, w, m, v, name, chunks=4):
    rows, cols = g.shape
    per_row = cols // LANE
    assert cols % LANE == 0 and per_row == SUBLANES and w.shape == (rows * per_row, LANE)
    step = -(-rows // (chunks * SUBLANES)) * SUBLANES
    bounds = [(r0, min(r0 + step, rows)) for r0 in range(0, rows, step)]

    def body(g_hbm, w_hbm, m_hbm, v_hbm, og_hbm, od_hbm, om_hbm, ov_hbm, g_buf, in_buf, out_buf, in_sems, out_sems):
        def copies_in(c):
            r0, r1 = bounds[c]
            packed = slice(r0 * per_row, r1 * per_row)
            return [pltpu.make_async_copy(g_hbm.at[r0:r1], g_buf.at[r0:r1], in_sems.at[c, 0])] + [
                pltpu.make_async_copy(src.at[packed], in_buf.at[i, packed], in_sems.at[c, 1 + i])
                for i, src in enumerate((w_hbm, m_hbm, v_hbm))]

        def copies_out(c):
            r0, r1 = bounds[c]
            packed = slice(r0 * per_row, r1 * per_row)
            return [pltpu.make_async_copy(out_buf.at[i, packed], dst.at[packed], out_sems.at[c, i])
                    for i, dst in enumerate((og_hbm, od_hbm, om_hbm, ov_hbm))]

        for c in range(len(bounds)):
            for cp in copies_in(c):
                cp.start()
        for c, (r0, r1) in enumerate(bounds):
            for cp in copies_in(c):
                cp.wait()
            for j in range(per_row):
                lanes = pl.ds(r0 * per_row + j, r1 - r0, stride=per_row)
                g_part = g_buf[r0:r1, j * LANE:(j + 1) * LANE]
                results = _adamw(in_buf[0, lanes, :], g_part, in_buf[1, lanes, :], in_buf[2, lanes, :])
                for i, val in enumerate((g_part,) + results):
                    out_buf[i, lanes, :] = val
            for cp in copies_out(c):
                cp.start()
        for c in range(len(bounds)):
            for cp in copies_out(c):
                cp.wait()

    shape = jax.ShapeDtypeStruct(w.shape, F32)
    return pl.pallas_call(
        body, name=name,
        in_specs=[ANY] * 4, out_specs=[ANY] * 4, out_shape=[shape] * 4,
        scratch_shapes=[pltpu.VMEM(g.shape, F32), pltpu.VMEM((3,) + w.shape, F32), pltpu.VMEM((4,) + w.shape, F32),
                        pltpu.SemaphoreType.DMA((len(bounds), 4)), pltpu.SemaphoreType.DMA((len(bounds), 4))],
        compiler_params=_params(),
    )(g, w, m, v)


def _ada_adamw(sc_all, d_ada, w, m, v, chunks=4):
    rows, cols = w.shape
    step, sub = rows // chunks, 32
    assert rows % chunks == 0 and step % LANE == 0 and step % sub == 0

    def body(sc_ref, d_ref, w_hbm, m_hbm, v_hbm, og_hbm, od_hbm, om_hbm, ov_hbm, in_buf, out_buf, in_sems, out_sems):
        def copies_in(c):
            part = slice(c * step, (c + 1) * step)
            return [pltpu.make_async_copy(src.at[part], in_buf.at[i, part], in_sems.at[c, i])
                    for i, src in enumerate((w_hbm, m_hbm, v_hbm))]

        def copies_out(c):
            part = slice(c * step, (c + 1) * step)
            return [pltpu.make_async_copy(out_buf.at[i, part], dst.at[part], out_sems.at[c, i])
                    for i, dst in enumerate((og_hbm, od_hbm, om_hbm, ov_hbm))]

        for c in range(chunks):
            for cp in copies_in(c):
                cp.start()
        for c in range(chunks):
            sc_t = sc_ref[:, c * step:(c + 1) * step].T
            for cp in copies_in(c):
                cp.wait()
            for r0 in range(0, step, sub):
                part = slice(c * step + r0, c * step + r0 + sub)
                g = sc_t[r0:r0 + sub, 0:1] * d_ref[0:1, :]
                for b in range(1, N_DEV):
                    g = g + sc_t[r0:r0 + sub, b:b + 1] * d_ref[b:b + 1, :]
                results = _adamw(in_buf[0, part, :], g, in_buf[1, part, :], in_buf[2, part, :])
                for i, val in enumerate((g,) + results):
                    out_buf[i, part, :] = val
            for cp in copies_out(c):
                cp.start()
        for c in range(chunks):
            for cp in copies_out(c):
                cp.wait()

    in_vmem = pl.BlockSpec(memory_space=pltpu.VMEM)
    shape = jax.ShapeDtypeStruct(w.shape, F32)
    return pl.pallas_call(
        body, name="ada_adamw",
        in_specs=[in_vmem, in_vmem, ANY, ANY, ANY], out_specs=[ANY] * 4, out_shape=[shape] * 4,
        scratch_shapes=[pltpu.VMEM((3,) + w.shape, F32), pltpu.VMEM((4,) + w.shape, F32),
                        pltpu.SemaphoreType.DMA((chunks, 3)), pltpu.SemaphoreType.DMA((chunks, 4))],
        compiler_params=_params(),
    )(sc_all, d_ada, w, m, v)


F_LO, F_HI = 3 * D_ATT, 3 * D_ATT + N_HEADS


def _split_forget(a, axis):
    idx = lambda lo, hi: tuple(slice(lo, hi) if d == axis else slice(None) for d in range(a.ndim))
    pad = [(0, LANE - N_HEADS) if d == axis else (0, 0) for d in range(a.ndim)]
    return jnp.concatenate([a[idx(0, F_LO)], a[idx(F_HI, D_IN)]], axis=axis), jnp.pad(a[idx(F_LO, F_HI)], pad)


def _join_forget(main, f, axis):
    idx = lambda lo, hi: tuple(slice(lo, hi) if d == axis else slice(None) for d in range(main.ndim))
    return jnp.concatenate([main[idx(0, F_LO)], f[idx(0, N_HEADS)], main[idx(F_LO, N_MAIN)]], axis=axis)


def _adamw_small(grad_rows, row_params, whole_params, summed_params, scalar_at):
    n_row, n_whole, n_sum = len(row_params), len(whole_params), len(summed_params)
    n = n_row + n_whole + n_sum

    def body(g_ref, *refs):
        n_in = 3 * n_row + 4 * (n_whole + n_sum)
        ins, outs = list(refs[:n_in]), refs[n_in:]
        for i in range(n):
            if i < n_row:
                row, lo, hi = row_params[i][:3]
                g = g_ref[row:row + 1, lo:hi]
            elif i < n_row + n_whole:
                g = ins.pop(0)[...]
            else:
                parts = ins.pop(0)
                g = parts[0]
                for k in range(1, N_DEV):
                    g = g + parts[k]
            w, m, v = (ins.pop(0)[...] for _ in range(3))
            outs[4 * i][...] = g
            outs[4 * i + 1][...], outs[4 * i + 2][...], outs[4 * i + 3][...] = _adamw(w, g, m, v)
        row, lane = scalar_at
        outs[4 * n][...] = g_ref[row:row + 1, lane:lane + 1]

    shapes = [p[3] for p in row_params] + [p[1] for p in whole_params] + [p[1] for p in summed_params]
    operands = [a for p in row_params for a in p[3:]] + [a for p in whole_params + summed_params for a in p]
    flat = pl.pallas_call(
        body, name="adamw_small",
        out_shape=[jax.ShapeDtypeStruct(w.shape, F32) for w in shapes for _ in range(4)]
        + [jax.ShapeDtypeStruct((1, 1), F32)],
        compiler_params=_params(),
    )(grad_rows, *operands)
    return [flat[4 * i:4 * i + 4] for i in range(n)], flat[4 * n].reshape(())


def kernel(x, c, w_ada, b_ada, w_in, b_in, w_pool_mix, b_pool_mix, pool_scale, w_out, b_out, ln_g, ln_b, loss_target, m_w_ada, m_b_ada, m_w_in, m_b_in, m_w_pool_mix, m_b_pool_mix, m_pool_scale, m_w_out, m_b_out, m_ln_g, m_ln_b, v_w_ada, v_b_ada, v_w_in, v_b_in, v_w_pool_mix, v_b_pool_mix, v_pool_scale, v_w_out, v_b_out, v_ln_g, v_ln_b):
    seq = x.shape[1]
    tile = min(256, seq)
    attn_tile = min(512, max(128, seq // 4))
    me = _dev_index(*_mesh_pos())
    x2, tgt = x[0], loss_target[0]

    rows_of = lambda a: jnp.swapaxes(a, 1, 2)[0]
    w_main, w_f, sc_all, ada_mine = _gather_and_ada(c, rows_of(w_in).astype(BF16), w_ada[0])
    ada = ada_mine.reshape(1, D_ADA) + b_ada
    shift, scale, gate = ada[:, 0:D], ada[:, D:2 * D], ada[:, 2 * D:]
    mod = jnp.concatenate([1.0 + scale, shift, jnp.zeros((6, D), F32)], axis=0)
    b_main, b_f = _split_forget(b_in, 1)

    qp, kp, vp, f, p, g_att, g_pool, u = _inproj_forward(x2, mod, w_main, w_f, b_main, b_f, tile)
    att, q2t, w_out_g = _attention_forward(qp, kp, vp, w_out[0].astype(BF16), attn_tile)

    vecs = jnp.concatenate([gate, b_out, ln_g, ln_b, jnp.zeros((4, D), F32)], axis=0)
    pool_vecs = jnp.concatenate([b_pool_mix.reshape(1, D_POOL), pool_scale, jnp.zeros((6, D_POOL), F32)], axis=0)
    dxa, do2, d_ga, d_gp, d_pooled, gw_out, dw_pool, dvec = _middle(
        x2, tgt, att, g_att, g_pool, p, vecs, pool_vecs, w_out_g.reshape(D, D), w_pool_mix[0].astype(BF16), tile)

    pool_rows = w_pool_mix.shape[1] * GROUP_DIM
    dqp, dkp, dvp, d_cum, g_out, dvec_sum, dw_pool_sum = _attention_backward(
        q2t, kp, vp, do2, gw_out.reshape(N_DEV, D // N_DEV, D), dvec, dw_pool.reshape(pool_rows, GROUP_DIM), attn_tile)
    dx, dproj, dw_f, db_main, db_f, dmod = _inproj_backward(
        dqp, dkp, dvp, d_cum, f, d_pooled, d_ga, d_gp, x2, dxa, u, mod, w_main, w_f, tile)
    gw_in = _weight_grads(dqp, dkp, dvp, dw_f, dproj, u, min(512, seq))
    d_ada = jnp.concatenate([dmod[1:2], dmod[0:1], dvec[5:6]], axis=1)
    g_in_rows, g_b_in, d_ada_all = _reduce_grads(gw_in, _join_forget(db_main[0:1], db_f[0:1], 1), d_ada)

    packed = lambda a: jnp.transpose(a.reshape(SUBLANES, LANE, -1), (2, 0, 1)).reshape(-1, LANE)
    outs_in = _adamw_packed(g_in_rows, packed(w_in), packed(m_w_in), packed(v_w_in), "adamw_w_in")
    g_w_in, d_w_in, nm_w_in, nv_w_in = (
        jnp.transpose(a.reshape(-1, SUBLANES, LANE), (1, 2, 0)).reshape(D, -1) for a in outs_in)
    flat_pool = lambda a: a.reshape(1, D_POOL)
    pool_2d = lambda a: a.reshape(pool_rows, GROUP_DIM)
    rows, loss = _adamw_small(
        dvec_sum,
        [(0, 0, D, b_out, m_b_out, v_b_out), (1, 0, D, ln_g, m_ln_g, v_ln_g), (2, 0, D, ln_b, m_ln_b, v_ln_b),
         (3, 0, D_POOL, flat_pool(b_pool_mix), flat_pool(m_b_pool_mix), flat_pool(v_b_pool_mix)),
         (3, D_POOL, 2 * D_POOL, pool_scale, m_pool_scale, v_pool_scale)],
        [(g_out, w_out[0], m_w_out[0], v_w_out[0]),
         (dw_pool_sum, pool_2d(w_pool_mix), pool_2d(m_w_pool_mix), pool_2d(v_w_pool_mix)),
         (g_b_in, b_in, m_b_in, v_b_in)],
        [(d_ada_all, b_ada, m_b_ada, v_b_ada)],
        scalar_at=(4, 0))
    small = {"b_out": rows[0], "ln_g": rows[1], "ln_b": rows[2],
             "b_pool": [a.reshape(b_pool_mix.shape) for a in rows[3]], "pool_scale": rows[4],
             "w_pool": [a.reshape(w_pool_mix.shape) for a in rows[6]], "b_in": rows[7]}
    g_s, d_s, nm_s, nv_s = ({k: r[j] for k, r in small.items()} for j in range(4))
    g_w_out, d_w_out, nm_w_out, nv_w_out = rows[5]
    g_b_ada, d_b_ada, nm_b_ada, nv_b_ada = rows[8]

    d_ada_local = lax.dynamic_slice_in_dim(d_ada_all.reshape(N_DEV, D_ADA), me * (D_ADA // N_DEV), D_ADA // N_DEV, axis=1)
    g_w_ada, d_w_ada, nm_w_ada, nv_w_ada = _ada_adamw(sc_all, d_ada_local, w_ada[0], m_w_ada[0], v_w_ada[0])

    def ordered(w_ada_, b_ada_, w_in_, w_out_, s):
        return (w_ada_[None], b_ada_, w_in_[None], s["b_in"], s["w_pool"], s["b_pool"], s["pool_scale"],
                w_out_[None], s["b_out"], s["ln_g"], s["ln_b"])

    return (loss, dx[None],
            *ordered(g_w_ada, g_b_ada, g_w_in, g_w_out, g_s),
            *ordered(d_w_ada, d_b_ada, d_w_in, d_w_out, d_s),
            *ordered(nm_w_ada, nm_b_ada, nm_w_in, nm_w_out, nm_s),
            *ordered(nv_w_ada, nv_b_ada, nv_w_in, nv_w_out, nv_s))
```

```python
import jax
import jax.numpy as jnp
from jax import lax
from jax.experimental import pallas as pl
from jax.experimental.pallas import tpu as pltpu

F32 = jnp.float32
BF16 = jnp.bfloat16

N_DEV = 8
D = 1024
N_HEADS = 8
HEAD_DIM = 64
D_ATT = 512
D_POOL = 512
POOL_WINDOWS = (2, 4, 8, 16)
GROUP_DIM = 128
HALO = 16
LANE = 128
BF16_TILE_ROWS = 16
D_IN = 3080
D_ADA = 3072
N_MAIN = 3072
OFF_P = 1536
COL_CHUNK = 512
Q_SCALE = 0.125
LN_EPS = 1e-5
ALPHA = 2.0 ** 0.25
L_CQ, L_CK, L_LSE = 64, 67, 70

ADAM_LR, ADAM_B1, ADAM_B2, ADAM_EPS, ADAM_WD, ADAM_STEP = 0.001, 0.9, 0.999, 1e-08, 0.01, 10
VMEM_LIMIT = 56 * 1024 * 1024

MESH = pl.DeviceIdType.MESH
ANY = pl.BlockSpec(memory_space=pl.ANY)


def _params(sem=None, vmem=VMEM_LIMIT):
    return pltpu.CompilerParams(dimension_semantics=sem, vmem_limit_bytes=vmem)


def _split3(a):
    hi = a.astype(BF16)
    r = a - hi.astype(F32)
    mid = r.astype(BF16)
    lo = (r - mid.astype(F32)).astype(BF16)
    return hi, mid, lo


def _dot(a, b):
    return jnp.dot(a, b, preferred_element_type=F32)


def _dot_nt(a, b):
    return lax.dot_general(a, b, (((1,), (1,)), ((), ())), preferred_element_type=F32)


def _dot_tn(a, b):
    return lax.dot_general(a, b, (((0,), (0,)), ((), ())), preferred_element_type=F32)


def _dot3(m01, a):
    hi, mid, lo = _split3(a)
    return _dot(m01, hi) + _dot(m01, mid) + _dot(m01, lo)


def _sigmoid(z):
    return 1.0 / (1.0 + jnp.exp(-z))


def _lanes(shape):
    return lax.broadcasted_iota(jnp.int32, shape, len(shape) - 1)


def _place3(lane, base, parts, other):
    out = other
    for j in range(3):
        out = jnp.where(lane == base + j, parts[j], out)
    return out


def _mesh_pos():
    return lax.axis_index("x"), lax.axis_index("y"), lax.axis_index("c")


def _dev_index(px, py, pc):
    return 4 * px + 2 * py + pc


N_GATHER_SEMS = 9


def _gather_stages(src_ref, out_ref, send_sems, recv_sems, local_sem):
    x, y, c = _mesh_pos()
    me, sibling = (x, y, c), (x, y, 1 - c)
    nbr_x, nbr_y, diag = (1 - x, y), (x, 1 - y), (1 - x, 1 - y)
    half = out_ref.shape[-1] // 2
    left, right = pl.ds(0, half), pl.ds(half, half)

    def copy(k, block, to, cols=None, src=None):
        slot = out_ref.at[_dev_index(*block)]
        if cols is not None:
            slot = slot.at[:, cols]
        return pltpu.make_async_remote_copy(
            src_ref=slot if src is None else src, dst_ref=slot, send_sem=send_sems.at[k], recv_sem=recv_sems.at[k],
            device_id=to, device_id_type=MESH)

    mine = pltpu.make_async_copy(src_ref, out_ref.at[_dev_index(*me)], local_sem)
    first = [copy(0, me, sibling, src=src_ref), copy(1, me, (*nbr_x, c), src=src_ref), copy(2, me, (*nbr_y, c), src=src_ref)]
    relay = [(1, nbr_x, None, nbr_x), (2, nbr_y, None, nbr_y), (3, diag, left, nbr_y), (4, diag, right, nbr_x)]
    onward = [copy(3, (*nbr_x, c), (*nbr_y, c), cols=left), copy(4, (*nbr_y, c), (*nbr_x, c), cols=right)]
    passed = [copy(4 + k, (*block, c), sibling, cols=cols) for k, block, cols, _ in relay]

    def start():
        mine.start()
        for cp in first:
            cp.start()

    def relay_stage(first_item):
        def run():
            for j in (first_item, first_item + 1):
                k, block, cols, frm = relay[j]
                copy(k, (*block, c), (*frm, c), cols=cols).wait_recv()
                if j < 2:
                    onward[j].start()
                passed[j].start()
        return run

    def from_sibling(items):
        for k, block, cols, _ in items:
            copy(4 + k, (*block, 1 - c), me, cols=cols).wait_recv()

    def finish_near():
        copy(0, sibling, me).wait_recv()
        from_sibling(relay[:2])
        mine.wait()

    def finish_far():
        from_sibling(relay[2:])
        for cp in first + onward + passed:
            cp.wait_send()

    return start, relay_stage(0), relay_stage(2), finish_near, finish_far


N_REDUCE_SEMS = 10
N_SMALL_SEMS = 4
N_ROWS_SEMS = 7


def _reduce_stages(ins, gs, r1, s2, r2, smalls, send_sems, recv_sems, rows=None, own=None):
    n = len(ins)
    x, y, c = _mesh_pos()
    me = _dev_index(x, y, c)
    sibling = (x, y, 1 - c)
    chips = [(x, y), (1 - x, y), (x, 1 - y), (1 - x, 1 - y)]
    peers = []
    for p in range(1, N_DEV):
        px, py, pc = (p >> 2) & 1, (p >> 1) & 1, p & 1
        peers.append((1 - x if px else x, 1 - y if py else y, 1 - c if pc else c))
    base_small = N_REDUCE_SEMS * n

    def remote(src, dst, k, to):
        return pltpu.make_async_remote_copy(src_ref=src, dst_ref=dst, send_sem=send_sems.at[k],
                                            recv_sem=recv_sems.at[k], device_id=to, device_id_type=MESH)

    def level1(a, q):
        return remote(ins[a].at[_dev_index(*chips[q], 1 - c)], r1[a].at[q], N_REDUCE_SEMS * a + q, sibling)

    def level2(a, k):
        half = ins[a].shape[-1] // 2
        left, right = pl.ds(0, half), pl.ds(half, half)
        nbr_x, nbr_y = (*chips[1], c), (*chips[2], c)
        src_slot, dst_slot, cols, to = [(0, 0, left, nbr_x), (1, 1, right, nbr_y), (2, 2, left, nbr_x),
                                        (2, 2, right, nbr_y), (0, 0, right, nbr_x), (1, 1, left, nbr_y)][k]
        return remote(s2[a].at[src_slot, :, cols], r2[a].at[dst_slot, :, cols], N_REDUCE_SEMS * a + 4 + k, to)

    to_sibling = [remote(sm[0], sm[2], base_small + 4 * i, sibling) for i, sm in enumerate(smalls)]
    to_chips = [[remote(sm[3], sm[4].at[j], base_small + 4 * i + 1 + j, (*chips[j + 1], c)) for j in range(3)]
                for i, sm in enumerate(smalls)]
    if rows is not None:
        rows_ref, land_ref, all_ref = rows
        base_rows = base_small + 4 * len(smalls)
        row_sends = [remote(rows_ref, land_ref.at[me], base_rows + k, to) for k, to in enumerate(peers)]

    order = (3, 1, 2, 0)

    def mine(a, q):
        buf, sems = own[a]
        return pltpu.make_async_copy(ins[a].at[_dev_index(*chips[q], c)], buf.at[q], sems.at[q])

    def start():
        for a in range(n):
            for q in order:
                level1(a, q).start()
            if own is not None:
                for q in order:
                    mine(a, q).start()
        for cp in to_sibling:
            cp.start()
        if rows is not None:
            for cp in row_sends:
                cp.start()
            land_ref[me] = rows_ref[...]

    def middle():
        for a in range(n):
            for q in order:
                level1(a, q).wait_recv()
                if own is None:
                    kept = ins[a][_dev_index(*chips[q], c)]
                else:
                    mine(a, q).wait()
                    kept = own[a][0][q]
                pair = kept.astype(F32) + r1[a][q].astype(F32)
                if q == 0:
                    gs[a][...] = pair
                else:
                    s2[a][q - 1] = pair.astype(BF16)
                    for k in ((0,), (1,), (2, 3))[q - 1]:
                        level2(a, k).start()
        for i, (small_ref, _, sm_sib, sm_chip, _) in enumerate(smalls):
            to_sibling[i].wait_recv()
            sm_chip[...] = small_ref[...] + sm_sib[...]
            for cp in to_chips[i]:
                cp.start()

    def fold():
        for a in range(n):
            half = ins[a].shape[-1] // 2
            level2(a, 3).wait_recv()
            s2[a][0, :, half:] = (s2[a][0, :, half:].astype(F32) + r2[a][2, :, half:].astype(F32)).astype(BF16)
            level2(a, 4).start()
            level2(a, 2).wait_recv()
            s2[a][1, :, :half] = (s2[a][1, :, :half].astype(F32) + r2[a][2, :, :half].astype(F32)).astype(BF16)
            level2(a, 5).start()

    def finish():
        for a in range(n):
            for k in (0, 1, 4, 5):
                level2(a, k).wait_recv()
            gs[a][...] = gs[a][...] + r2[a][0].astype(F32) + r2[a][1].astype(F32)
            for q in range(4):
                level1(a, q).wait_send()
            for k in range(6):
                level2(a, k).wait_send()
        for i, (_, total_ref, _, sm_chip, sm_recv) in enumerate(smalls):
            for cp in to_chips[i]:
                cp.wait_recv()
            total = None
            for ax in range(2):
                for ay in range(2):
                    dx, dy = x != ax, y != ay
                    term = jnp.where(dx, jnp.where(dy, sm_recv[2], sm_recv[0]), jnp.where(dy, sm_recv[1], sm_chip[...]))
                    total = term if total is None else total + term
            total_ref[...] = total
            for cp in [to_sibling[i]] + to_chips[i]:
                cp.wait_send()
        if rows is not None:
            for k, frm in enumerate(peers):
                remote(rows_ref, land_ref.at[_dev_index(*frm)], base_rows + k, frm).wait_recv()
            all_ref[...] = land_ref[...]
            for cp in row_sends:
                cp.wait_send()

    return start, middle, fold, finish


def _reduce_scratch(shard, smalls, rows=None):
    out = [pltpu.VMEM((lead,) + shard.shape[1:], BF16) for lead in (4, 3, 3)]
    for small in smalls:
        out += [pltpu.VMEM(small.shape, F32), pltpu.VMEM(small.shape, F32), pltpu.VMEM((3,) + small.shape, F32)]
    n_sems = N_REDUCE_SEMS + N_SMALL_SEMS * len(smalls)
    if rows is not None:
        out.append(pltpu.VMEM((N_DEV,) + rows.shape, F32))
        n_sems += N_ROWS_SEMS
    return out + [pltpu.SemaphoreType.DMA((n_sems,))] * 2


def _reduce_grads(gw_in, small, rows):
    def body(in_ref, small_ref, rows_ref, g_ref, total_ref, rows_all_ref,
             r1, s2, r2, sm_sib, sm_chip, sm_recv, rows_land, send_sems, recv_sems, kept, kept_sems):
        stages = _reduce_stages(
            [in_ref], [g_ref], [r1], [s2], [r2], [(small_ref, total_ref, sm_sib, sm_chip, sm_recv)],
            send_sems, recv_sems, rows=(rows_ref, rows_land, rows_all_ref), own=[(kept, kept_sems)])
        for stage in stages:
            stage()

    vmem = pl.BlockSpec(memory_space=pltpu.VMEM)
    return pl.pallas_call(
        body, name="reduce_grads",
        in_specs=[ANY, vmem, vmem], out_specs=[vmem, vmem, vmem],
        out_shape=[jax.ShapeDtypeStruct(gw_in.shape[1:], F32), jax.ShapeDtypeStruct(small.shape, F32),
                   jax.ShapeDtypeStruct((N_DEV,) + rows.shape, F32)],
        scratch_shapes=_reduce_scratch(gw_in, [small], rows)
        + [pltpu.VMEM((4,) + gw_in.shape[1:], BF16), pltpu.SemaphoreType.DMA((4,))],
        compiler_params=_params(),
    )(gw_in, small, rows)


def _dot3_rhs(a, b):
    a0, a1, a2 = _split3(a)
    b0, b1, b2 = _split3(b)
    return (_dot(a0, b0) + (_dot(a0, b1) + _dot(a1, b0))
            + (_dot(a0, b2) + _dot(a1, b1) + _dot(a2, b0)))


def _gather_and_ada(c, w_in_rows, w_ada):
    cols = w_ada.shape[1]
    shard = w_in_rows.shape[0]

    def body(c_ref, w_ref, wa_ref, w_main_hbm, w_f_ref, sc_ref, ada_ref,
             w_all_ref, w_f32, wm_buf, c_land, part, ada_land, send_sems, recv_sems, local_sem, x_send, x_recv, out_sems):
        x, y, cc = _mesh_pos()
        me = _dev_index(x, y, cc)
        peers = []
        for p in range(1, N_DEV):
            px, py, pc = (p >> 2) & 1, (p >> 1) & 1, p & 1
            peers.append((1 - x if px else x, 1 - y if py else y, 1 - cc if pc else cc))

        def remote(src, dst, k, to):
            return pltpu.make_async_remote_copy(src_ref=src, dst_ref=dst, send_sem=x_send.at[k], recv_sem=x_recv.at[k],
                                                device_id=to, device_id_type=MESH)

        c_sends = [remote(c_ref, c_land.at[me], k, to) for k, to in enumerate(peers)]
        for cp in c_sends:
            cp.start()
        start, relay_near, relay_far, finish_near, finish_far = _gather_stages(
            w_ref, w_all_ref, send_sems, recv_sems, local_sem.at[0])
        start()
        c_land[me] = c_ref[...]
        for k, frm in enumerate(peers):
            remote(c_ref, c_land.at[_dev_index(*frm)], k, frm).wait_recv()
        c_all = jnp.concatenate([c_land[b] for b in range(N_DEV)], axis=0)
        sc = c_all * _sigmoid(c_all)
        sc_ref[...] = sc
        rows = _dot3_rhs(sc, wa_ref[...])
        for b in range(N_DEV):
            part[b] = rows[b:b + 1, :]
        a_sends = [remote(part.at[_dev_index(*to)], ada_land.at[me], 7 + k, to) for k, to in enumerate(peers)]
        for cp in a_sends:
            cp.start()
        ada_land[me] = part[me]

        relay_near()
        finish_near()

        far_chip = 2 * (1 - x) + (1 - y)

        def stage(slots):
            for slot in slots:
                w_f32[slot * shard:(slot + 1) * shard, :] = w_all_ref[slot].astype(F32)

        def far_rows(k):
            main_row = lambda r: r if r < F_LO else r - N_HEADS
            first, last = 2 * shard * k, 2 * shard * (k + 1) - 1
            first = F_HI if F_LO <= first < F_HI else first
            last = F_LO - 1 if F_LO <= last < F_HI else last
            return (main_row(first) // BF16_TILE_ROWS * BF16_TILE_ROWS,
                    -(-(main_row(last) + 1) // BF16_TILE_ROWS) * BF16_TILE_ROWS)

        def near_rows(k):
            lo, hi = far_rows(k)
            return [(i, a, b) for i, (a, b) in enumerate(((0, lo), (hi, N_MAIN))) if a < b]

        def forget_is_far(k):
            return 2 * shard * k < F_HI and 2 * shard * (k + 1) > F_LO

        def main_copy(i, lo, hi):
            return pltpu.make_async_copy(wm_buf.at[lo:hi], w_main_hbm.at[lo:hi], out_sems.at[i])

        def write_main(i, lo, hi):
            if lo < min(hi, F_LO):
                wm_buf[lo:min(hi, F_LO), :] = w_f32[lo:min(hi, F_LO), :].astype(BF16)
            if max(lo, F_LO) < hi:
                wm_buf[max(lo, F_LO):hi, :] = w_f32[max(lo, F_LO) + N_HEADS:hi + N_HEADS, :].astype(BF16)
            main_copy(i, lo, hi).start()

        def write_forget():
            w_f_ref[...] = jnp.concatenate(
                [w_f32[F_LO:F_HI, :], jnp.zeros((LANE - N_HEADS, D), F32)], axis=0).astype(BF16)

        for k in range(N_DEV // 2):
            @pl.when(far_chip == k)
            def _(k=k):
                stage([slot for slot in range(N_DEV) if slot // 2 != k])
                for i, lo, hi in near_rows(k):
                    write_main(i, lo, hi)
                if not forget_is_far(k):
                    write_forget()

        relay_far()
        for k, frm in enumerate(peers):
            remote(part.at[0], ada_land.at[_dev_index(*frm)], 7 + k, frm).wait_recv()
        ada_ref[...] = ada_land[...]
        finish_far()
        for cp in c_sends + a_sends:
            cp.wait_send()

        for k in range(N_DEV // 2):
            @pl.when(far_chip == k)
            def _(k=k):
                stage([2 * k, 2 * k + 1])
                write_main(2, *far_rows(k))
                if forget_is_far(k):
                    write_forget()
                for i, lo, hi in near_rows(k):
                    main_copy(i, lo, hi).wait()
                main_copy(2, *far_rows(k)).wait()

    vmem = pl.BlockSpec(memory_space=pltpu.VMEM)
    return pl.pallas_call(
        body, name="gather_weights",
        in_specs=[vmem, ANY, vmem], out_specs=[ANY, vmem, vmem, vmem],
        out_shape=[jax.ShapeDtypeStruct((N_MAIN, D), BF16), jax.ShapeDtypeStruct((LANE, D), BF16),
                   jax.ShapeDtypeStruct((N_DEV, D), F32), jax.ShapeDtypeStruct((N_DEV, 1, cols), F32)],
        scratch_shapes=[pltpu.VMEM((N_DEV,) + w_in_rows.shape, BF16), pltpu.VMEM((D_IN, D), F32), pltpu.VMEM((N_MAIN, D), BF16),
                        pltpu.VMEM((N_DEV, 1, D), F32), pltpu.VMEM((N_DEV, 1, cols), F32), pltpu.VMEM((N_DEV, 1, cols), F32),
                        pltpu.SemaphoreType.DMA((N_GATHER_SEMS,)), pltpu.SemaphoreType.DMA((N_GATHER_SEMS,)),
                        pltpu.SemaphoreType.DMA((1,)),
                        pltpu.SemaphoreType.DMA((14,)), pltpu.SemaphoreType.DMA((14,)), pltpu.SemaphoreType.DMA((3,))],
        compiler_params=_params(),
    )(c, w_in_rows, w_ada)


def _inproj_forward(x, mod, w_main, w_f, b_main, b_f, tile):
    seq = x.shape[0]
    nt = seq // tile

    def body(x_ref, mod_ref, w_ref, wf_ref, b_ref, bf_ref,
             qp_ref, kp_ref, vp_ref, f_ref, p_ref, ga_ref, gp_ref, u_ref, carry_ref):
        i = pl.program_id(0)

        @pl.when(i == 0)
        def _():
            carry_ref[...] = jnp.zeros_like(carry_ref)

        u = x_ref[...] * mod_ref[0:1, :] + mod_ref[1:2, :]
        ub = u.astype(BF16)
        u_ref[...] = ub

        f = _dot_nt(ub, wf_ref[...]) + bf_ref[...]
        f_ref[...] = f
        lane = _lanes((tile, LANE))
        log_f = jnp.where(lane < N_HEADS, jnp.minimum(f, 0.0) - jnp.log(1.0 + jnp.exp(-jnp.abs(f))), 0.0)
        row = lax.broadcasted_iota(jnp.int32, (tile, tile), 0)
        col = lax.broadcasted_iota(jnp.int32, (tile, tile), 1)
        tri = (row >= col).astype(BF16)
        cum = _dot3(tri, log_f) + carry_ref[0:1, :]
        carry_ref[0:1, :] = cum[tile - 1:tile, :]
        cq = [part.astype(F32) for part in _split3(cum)]
        ck = [part.astype(F32) for part in _split3(-cum)]

        def proj(chunk):
            cols = pl.ds(chunk * COL_CHUNK, COL_CHUNK)
            return _dot_nt(ub, w_ref[cols, :]) + b_ref[:, cols]

        def head_tiles(r):
            for pair in range(N_HEADS // 2):
                both = r[:, pair * LANE:(pair + 1) * LANE]
                yield 2 * pair, both
                yield 2 * pair + 1, pltpu.roll(both, HEAD_DIM, 1)

        for h, val in head_tiles(proj(0)):
            extra = jnp.where((lane >= L_CK) & (lane < L_CK + 3), 1.0, 0.0)
            extra = _place3(lane, L_CQ, [part[:, h:h + 1] for part in cq], extra)
            qp_ref[h] = jnp.where(lane < HEAD_DIM, val * Q_SCALE, extra).astype(BF16)
        for h, val in head_tiles(proj(1)):
            ones = ((lane >= L_CQ) & (lane < L_CQ + 3)) | ((lane >= L_LSE) & (lane < L_LSE + 3))
            extra = _place3(lane, L_CK, [part[:, h:h + 1] for part in ck], jnp.where(ones, 1.0, 0.0))
            kp_ref[h] = jnp.where(lane < HEAD_DIM, val, extra).astype(BF16)
        for h, val in head_tiles(proj(2)):
            extra = jnp.where((lane >= HEAD_DIM) & (lane < HEAD_DIM + 3), -1.0, 0.0)
            vp_ref[h] = jnp.where(lane < HEAD_DIM, val, extra).astype(BF16)
        p_ref[...] = proj(3)
        ga_ref[...] = proj(4)
        gp_ref[...] = proj(5)

    head_block = pl.BlockSpec((N_HEADS, tile, LANE), lambda i: (0, i, 0))
    tok = lambda width: pl.BlockSpec((tile, width), lambda i: (i, 0))
    whole = lambda a: pl.BlockSpec(a.shape, lambda i: (0,) * a.ndim)
    padded = jax.ShapeDtypeStruct((N_HEADS, seq, LANE), BF16)
    half = jax.ShapeDtypeStruct((seq, D_ATT), F32)
    return pl.pallas_call(
        body, name="inproj_forward", grid=(nt,),
        in_specs=[tok(D), whole(mod), whole(w_main), whole(w_f), whole(b_main), whole(b_f)],
        out_specs=[head_block, head_block, head_block, tok(LANE), tok(D_POOL), tok(D_ATT), tok(D_POOL),
                   tok(D)],
        out_shape=[padded, padded, padded, jax.ShapeDtypeStruct((seq, LANE), F32), half, half, half,
                   jax.ShapeDtypeStruct((seq, D), BF16)],
        scratch_shapes=[pltpu.VMEM((8, LANE), F32)],
        compiler_params=_params(("arbitrary",)),
    )(x, mod, w_main, w_f, b_main, b_f)


def _attention_forward(qp, kp, vp, w_out, tile):
    seq = qp.shape[1]
    nb = seq // tile
    steps = (N_HEADS // 2) * nb

    def body(q_ref, k_ref, v_ref, wo_ref, att_ref, q2t_ref, wo_all_ref, s_a, s_b, m_ref, acc_ref,
             send_sems, recv_sems, local_sem):
        step = pl.program_id(0) * nb + pl.program_id(1)
        start, relay_near, relay_far, finish_near, finish_far = _gather_stages(
            wo_ref, wo_all_ref, send_sems, recv_sems, local_sem.at[0])
        pl.when(step == 0)(start)
        pl.when(step == steps // 4)(relay_near)
        pl.when(step == (3 * steps) // 4)(relay_far)

        i = pl.program_id(1)
        sub = lax.broadcasted_iota(jnp.int32, (LANE, tile), 0)
        row = lax.broadcasted_iota(jnp.int32, (tile, tile), 0)
        col = lax.broadcasted_iota(jnp.int32, (tile, tile), 1)
        q = [q_ref[0], q_ref[1]]

        def scores(buf, kb):
            rows = pl.ds(pl.multiple_of(kb * tile, tile), tile)
            for hh in range(2):
                buf[hh] = _dot_nt(k_ref[hh, rows, :], q[hh])

        def absorb(buf, kb, masked):
            rows = pl.ds(pl.multiple_of(kb * tile, tile), tile)
            for hh in range(2):
                m = m_ref[hh, 0:1, :]
                s = buf[hh]
                if masked:
                    s = jnp.where(row <= col, s, -1e30)
                m_new = jnp.maximum(m, jnp.max(s, axis=0, keepdims=True))
                p = jnp.exp(s - m_new).astype(BF16)
                acc_ref[hh] = jnp.exp(m - m_new) * acc_ref[hh] + _dot_tn(v_ref[hh, rows, :], p)
                m_ref[hh, 0:1, :] = m_new

        def two_blocks(j, _):
            scores(s_b, 2 * j + 1)
            absorb(s_a, 2 * j, False)
            scores(s_a, 2 * j + 2)
            absorb(s_b, 2 * j + 1, False)
            return 0

        def last_block():
            absorb(s_a, i, True)

        def last_two_blocks():
            scores(s_b, i)
            absorb(s_a, i - 1, False)
            absorb(s_b, i, True)

        scores(s_a, 0)
        m_ref[...] = jnp.full(m_ref.shape, -1e30, F32)
        acc_ref[...] = jnp.zeros_like(acc_ref)
        lax.fori_loop(0, i // 2, two_blocks, 0)
        lax.cond(i % 2 == 0, last_block, last_two_blocks)
        outs = []
        for hh in range(2):
            m, acc = m_ref[hh, 0:1, :], acc_ref[hh]
            l = -acc[HEAD_DIM:HEAD_DIM + 1, :]
            outs.append((acc / l)[:HEAD_DIM, :])
            neg_lse = [part.astype(F32) for part in _split3(-(m + jnp.log(l)))]
            q2t_ref[hh] = _place3(sub, L_LSE, neg_lse, q[hh].astype(F32).T).astype(BF16)
        att_ref[...] = jnp.concatenate(outs, axis=0).T
        @pl.when(step == steps - 1)
        def _():
            finish_near()
            finish_far()

    pair = pl.BlockSpec((2, tile, LANE), lambda hp, i: (hp, i, 0))
    full = pl.BlockSpec((2, seq, LANE), lambda hp, i: (hp, 0, 0))
    return pl.pallas_call(
        body, name="attention_forward", grid=(N_HEADS // 2, nb),
        in_specs=[pair, full, full, ANY],
        out_specs=[pl.BlockSpec((tile, LANE), lambda hp, i: (i, hp)),
                   pl.BlockSpec((2, LANE, tile), lambda hp, i: (hp, 0, i)), ANY],
        out_shape=[jax.ShapeDtypeStruct((seq, D_ATT), F32),
                   jax.ShapeDtypeStruct((N_HEADS, LANE, seq), BF16),
                   jax.ShapeDtypeStruct((N_DEV,) + w_out.shape, w_out.dtype)],
        scratch_shapes=[pltpu.VMEM((2, tile, tile), F32), pltpu.VMEM((2, tile, tile), F32),
                        pltpu.VMEM((2, 8, tile), F32), pltpu.VMEM((2, LANE, tile), F32),
                        pltpu.SemaphoreType.DMA((N_GATHER_SEMS,)), pltpu.SemaphoreType.DMA((N_GATHER_SEMS,)),
                        pltpu.SemaphoreType.DMA((1,))],
        compiler_params=_params(("arbitrary", "arbitrary")),
    )(qp, kp, vp, w_out)


def _window_sum(x, halo, window, transposed):
    tile = x.shape[0]

    def split_cat(a):
        hi = a.astype(BF16)
        return jnp.concatenate([hi, (a - hi.astype(F32)).astype(BF16)], axis=1)

    def fold(r):
        return r[:, :LANE] + r[:, LANE:]

    r = lax.broadcasted_iota(jnp.int32, (tile, tile), 0)
    c = lax.broadcasted_iota(jnp.int32, (tile, tile), 1)
    rh = lax.broadcasted_iota(jnp.int32, (HALO, HALO), 0)
    ch = lax.broadcasted_iota(jnp.int32, (HALO, HALO), 1)
    if not transposed:
        band = (c <= r) & (r - c < window)
        edge = (rh + HALO - ch) < window
    else:
        band = (r <= c) & (c - r < window)
        edge = (HALO + ch - rh) < window
    out = fold(_dot(band.astype(BF16), split_cat(x)))
    reach = fold(_dot(edge.astype(BF16), split_cat(halo)))
    if not transposed:
        return jnp.concatenate([out[:HALO] + reach, out[HALO:]], axis=0)
    return jnp.concatenate([out[:tile - HALO], out[tile - HALO:] + reach], axis=0)


def _silu_parts(g):
    sig = _sigmoid(g)
    return g * sig, sig * (1.0 + g * (1.0 - sig))


def _middle(x, tgt, att, g_att, g_pool, p, vecs, pool_vecs, w_out, w_pool, tile):
    seq = x.shape[0]
    nt = seq // tile
    halo_blocks = tile // HALO

    def body(x_ref, tgt_ref, att_ref, ga_ref, gp_ref, p_ref, ph_ref, vec_ref, pvec_ref, wo_ref, wp_ref,
             dxa_ref, do2_ref, dga_ref, dgp_ref, dpooled_ref, gwo_ref, dwp_ref, dvec_ref, dwo_ref, dpvec_ref):
        i = pl.program_id(0)

        @pl.when(i == 0)
        def _():
            dwo_ref[...] = jnp.zeros_like(dwo_ref)
            dwp_ref[...] = jnp.zeros_like(dwp_ref)
            dvec_ref[...] = jnp.zeros_like(dvec_ref)
            dpvec_ref[...] = jnp.zeros_like(dpvec_ref)

        gate, b_out, ln_g, ln_b = (vec_ref[k:k + 1, :] for k in range(4))
        b_pool, pool_scale = pvec_ref[0:1, :], pvec_ref[1:2, :]
        x = x_ref[...]
        p = p_ref[...]
        p_halo = ph_ref[...] * jnp.where(i > 0, 1.0, 0.0)
        pos = i * tile + lax.broadcasted_iota(jnp.int32, (tile, 1), 0) + 1

        pooled, mixed = [], []
        for g, window in enumerate(POOL_WINDOWS):
            cols = slice(g * GROUP_DIM, (g + 1) * GROUP_DIM)
            wsum = _window_sum(p[:, cols], p_halo[:, cols], window, False)
            count = jnp.minimum(pos, window).astype(F32)
            pooled.append(wsum / count - p[:, cols])
            mixed.append(_dot(pooled[g].astype(BF16), wp_ref[g]) + b_pool[:, cols])
        mixed = jnp.concatenate(mixed, axis=1)
        pool = mixed * pool_scale

        att = att_ref[...]
        g_att, g_pool = ga_ref[...], gp_ref[...]
        silu_a, dsilu_a = _silu_parts(g_att)
        silu_p, dsilu_p = _silu_parts(g_pool)
        y_in = jnp.concatenate([att * silu_a, pool * silu_p], axis=1)
        y = _dot(y_in.astype(BF16), wo_ref[...]) + b_out
        h = ALPHA * x + gate * y
        mu = jnp.mean(h, axis=1, keepdims=True)
        hc = h - mu
        var = jnp.mean(hc * hc, axis=1, keepdims=True)
        rstd = lax.rsqrt(var + LN_EPS)
        yhat = hc * rstd
        diff = yhat * ln_g + ln_b - tgt_ref[...]
        loss_rows = jnp.sum(diff * diff, axis=1, keepdims=True)
        d_out = diff * (1.0 / D)

        d_yhat = d_out * ln_g
        dh = rstd * (d_yhat - jnp.mean(d_yhat, axis=1, keepdims=True)
                     - yhat * jnp.mean(d_yhat * yhat, axis=1, keepdims=True))
        dxa_ref[...] = ALPHA * dh
        dy = dh * gate
        dyb = dy.astype(BF16)
        lane = _lanes((1, D))
        loss_row = jnp.where(lane == 0, (0.5 / D) * jnp.sum(loss_rows, axis=0, keepdims=True), 0.0)
        dvec_ref[5:6, :] += jnp.sum(dh * y, axis=0, keepdims=True)
        dvec_ref[0:1, :] += jnp.sum(dy, axis=0, keepdims=True)
        dvec_ref[1:2, :] += jnp.sum(d_out * yhat, axis=0, keepdims=True)
        dvec_ref[2:3, :] += jnp.sum(d_out, axis=0, keepdims=True)
        dvec_ref[4:5, :] += loss_row

        dwo_ref[...] += _dot(y_in.T.astype(BF16), dyb)
        d_yin = _dot_nt(dyb, wo_ref[...])
        d_a, d_pl = d_yin[:, :D_ATT], d_yin[:, D_ATT:]
        d_att = d_a * silu_a
        d_att_t = d_att.T
        prod_t = (d_att * att).T
        sub = lax.broadcasted_iota(jnp.int32, (HEAD_DIM, tile), 0)
        for h in range(N_HEADS):
            rows = slice(h * HEAD_DIM, (h + 1) * HEAD_DIM)
            delta = jnp.sum(prod_t[rows], axis=0, keepdims=True)
            extra = _place3(sub, 0, [part.astype(F32) for part in _split3(delta)], 0.0)
            do2_ref[h] = jnp.concatenate([d_att_t[rows], extra], axis=0).astype(BF16)
        dga_ref[...] = d_a * att * dsilu_a
        dgp_ref[...] = d_pl * pool * dsilu_p
        d_pool = d_pl * silu_p
        d_mixed = d_pool * pool_scale
        dpvec_ref[0:1, :] += jnp.sum(d_mixed, axis=0, keepdims=True)
        dpvec_ref[1:2, :] += jnp.sum(d_pool * mixed, axis=0, keepdims=True)
        d_pooled = []
        for g in range(len(POOL_WINDOWS)):
            cols = slice(g * GROUP_DIM, (g + 1) * GROUP_DIM)
            dmb = d_mixed[:, cols].astype(BF16)
            dwp_ref[g] += _dot(pooled[g].T.astype(BF16), dmb)
            d_pooled.append(_dot_nt(dmb, wp_ref[g]))
        dpooled_ref[...] = jnp.concatenate(d_pooled, axis=1)

        @pl.when(i == nt - 1)
        def _():
            gwo_ref[...] = dwo_ref[...].astype(BF16)
            dvec_ref[3:4, :] = jnp.concatenate([dpvec_ref[0:1, :], dpvec_ref[1:2, :]], axis=1)

    tok = lambda width: pl.BlockSpec((tile, width), lambda i: (i, 0))
    whole = lambda a: pl.BlockSpec(a.shape, lambda i: (0,) * a.ndim)
    halo = pl.BlockSpec((HALO, D_POOL), lambda i: (jnp.maximum(i * halo_blocks - 1, 0), 0))
    half = jax.ShapeDtypeStruct((seq, D_ATT), F32)
    outs = [jax.ShapeDtypeStruct((seq, D), F32), jax.ShapeDtypeStruct((N_HEADS, LANE, seq), BF16), half, half, half,
            jax.ShapeDtypeStruct(w_out.shape, BF16), jax.ShapeDtypeStruct(w_pool.shape, F32),
            jax.ShapeDtypeStruct(vecs.shape, F32)]
    return pl.pallas_call(
        body, name="middle", grid=(nt,),
        in_specs=[tok(D), tok(D), tok(D_ATT), tok(D_ATT), tok(D_POOL), tok(D_POOL), halo,
                  whole(vecs), whole(pool_vecs), whole(w_out), whole(w_pool)],
        out_specs=[tok(D), pl.BlockSpec((N_HEADS, LANE, tile), lambda i: (0, 0, i)),
                   tok(D_ATT), tok(D_POOL), tok(D_POOL),
                   whole(w_out), whole(w_pool), whole(vecs)],
        out_shape=outs,
        scratch_shapes=[pltpu.VMEM(w_out.shape, F32), pltpu.VMEM(pool_vecs.shape, F32)],
        compiler_params=_params(("arbitrary",)),
    )(x, tgt, att, g_att, g_pool, p, p, vecs, pool_vecs, w_out, w_pool)


def _attention_backward(q2t, kp, vp, do2t, gw_out, vecs, pool, tile):
    seq = kp.shape[1]
    nb = seq // tile
    last = N_HEADS // 2 - 1

    def body(qt_ref, k_ref, v_ref, dot_ref, gwo_hbm, vecs_hbm, pool_hbm,
             dq_ref, dk_ref, dv_ref, dcum_ref, g_out_ref, vecs_sum_ref, pool_sum_ref,
             dq_acc, dk_acc, dv_acc, gwo_ref, vecs_ref, pool_ref,
             r1, s2, r2, v_sib, v_chip, v_recv, p_sib, p_chip, p_recv, send_sems, recv_sems):
        hp = pl.program_id(0)
        start, middle, fold, finish = _reduce_stages(
            [gwo_ref], [g_out_ref], [r1], [s2], [r2],
            [(vecs_ref, vecs_sum_ref, v_sib, v_chip, v_recv), (pool_ref, pool_sum_ref, p_sib, p_chip, p_recv)],
            send_sems, recv_sems)

        @pl.when(hp == 0)
        def _():
            pltpu.sync_copy(gwo_hbm, gwo_ref)
            pltpu.sync_copy(vecs_hbm, vecs_ref)
            pltpu.sync_copy(pool_hbm, pool_ref)
            start()

        pl.when(hp == 1)(middle)
        pl.when(hp == 2)(fold)

        row = lax.broadcasted_iota(jnp.int32, (tile, tile), 0)
        col = lax.broadcasted_iota(jnp.int32, (tile, tile), 1)
        dq_acc[...] = jnp.zeros_like(dq_acc)

        def kv_block(kb, _):
            krows = pl.ds(pl.multiple_of(kb * tile, tile), tile)
            k = [k_ref[hh, krows, :] for hh in range(2)]
            v = [v_ref[hh, krows, :] for hh in range(2)]
            k_t = [k[hh].T for hh in range(2)]

            def q_block(qb, masked):
                qcols = pl.ds(pl.multiple_of(qb * tile, tile), tile)
                for hh in range(2):
                    q_t = qt_ref[hh, :, qcols]
                    do_t = dot_ref[hh, :, qcols]
                    s_t = _dot(k[hh], q_t)
                    if masked:
                        s_t = jnp.where(row <= col, s_t, -1e30)
                    p_t = jnp.exp(s_t)
                    ds_t = (p_t * _dot(v[hh], do_t)).astype(BF16)
                    dv_new = _dot_nt(do_t, p_t.astype(BF16))
                    dk_new = _dot_nt(q_t, ds_t)
                    if masked:
                        dv_acc[hh], dk_acc[hh] = dv_new, dk_new
                    else:
                        dv_acc[hh] += dv_new
                        dk_acc[hh] += dk_new
                    dq_acc[hh, :, qcols] += _dot(k_t[hh], ds_t)

            q_block(kb, True)

            def two_later_blocks(j, _):
                q_block(kb + 1 + 2 * j, False)
                q_block(kb + 2 + 2 * j, False)
                return 0

            later = nb - 1 - kb
            lax.fori_loop(0, later // 2, two_later_blocks, 0)
            pl.when(later % 2 == 1)(lambda: q_block(nb - 1, False))
            for hh in range(2):
                dk = dk_acc[hh]
                dk_ref[hh, :, krows] = dk.astype(BF16)
                dv_ref[hh, :, krows] = dv_acc[hh].astype(BF16)
                dcum_ref[hh, :, krows] = -dk[L_CK:L_CK + 1, :]
            return 0

        lax.fori_loop(0, nb, kv_block, 0)
        for hh in range(2):
            dq = dq_acc[hh]
            dcum_ref[hh] += dq[L_CQ:L_CQ + 1, :]
            dq_ref[hh] = (dq * Q_SCALE).astype(BF16)
        pl.when(hp == last)(finish)

    pair = pl.BlockSpec((2, seq, LANE), lambda hp: (hp, 0, 0))
    pair_t = pl.BlockSpec((2, LANE, seq), lambda hp: (hp, 0, 0))
    whole = lambda shape: pl.BlockSpec(shape, lambda hp: (0,) * len(shape))
    grad = jax.ShapeDtypeStruct((N_HEADS, LANE, seq), BF16)
    return pl.pallas_call(
        body, name="attention_backward", grid=(N_HEADS // 2,),
        in_specs=[pair_t, pair, pair, pair_t, ANY, ANY, ANY],
        out_specs=[pair_t, pair_t, pair_t, pl.BlockSpec((2, 1, seq), lambda hp: (hp, 0, 0)),
                   whole(gw_out.shape[1:]), whole(vecs.shape), whole(pool.shape)],
        out_shape=[grad, grad, grad, jax.ShapeDtypeStruct((N_HEADS, 1, seq), F32),
                   jax.ShapeDtypeStruct(gw_out.shape[1:], F32), jax.ShapeDtypeStruct(vecs.shape, F32),
                   jax.ShapeDtypeStruct(pool.shape, F32)],
        scratch_shapes=[pltpu.VMEM((2, LANE, seq), F32), pltpu.VMEM((2, LANE, tile), F32),
                        pltpu.VMEM((2, LANE, tile), F32), pltpu.VMEM(gw_out.shape, BF16),
                        pltpu.VMEM(vecs.shape, F32), pltpu.VMEM(pool.shape, F32)]
        + _reduce_scratch(gw_out, [vecs, pool]),
        compiler_params=_params(("arbitrary",)),
    )(q2t, kp, vp, do2t, gw_out, vecs, pool)


def _inproj_backward(dqp, dkp, dvp, d_cum, f, d_pooled, d_ga, d_gp, x, dxa, u, mod, w_main, w_f, tile):
    seq = x.shape[0]
    nt = seq // tile
    halo_blocks = tile // HALO

    def body(dq_ref, dk_ref, dv_ref, dcum_ref, f_ref, dpo_ref, dph_ref, dga_ref, dgp_ref, x_ref, dxa_ref, u_ref,
             mod_ref, w_ref, wf_ref,
             dx_ref, dproj_ref, dwf_ref, db_ref, dbf_ref, dmod_ref, carry_ref):
        step = pl.program_id(0)
        i = nt - 1 - step

        @pl.when(step == 0)
        def _():
            carry_ref[...] = jnp.zeros_like(carry_ref)
            dwf_ref[...] = jnp.zeros_like(dwf_ref)
            db_ref[...] = jnp.zeros_like(db_ref)
            dbf_ref[...] = jnp.zeros_like(dbf_ref)
            dmod_ref[...] = jnp.zeros_like(dmod_ref)

        ones = jnp.ones((8, tile), BF16)

        def emit(chunk, val):
            cols = pl.ds(chunk * COL_CHUNK, COL_CHUNK)
            db_ref[0:1, cols] += jnp.sum(val, axis=0, keepdims=True)
            vb = val.astype(BF16)
            dproj_ref[:, pl.ds((chunk - 3) * COL_CHUNK, COL_CHUNK)] = vb
            return _dot(vb, w_ref[cols, :])

        d_u = jnp.zeros((tile, D), F32)
        for chunk, ref in enumerate((dq_ref, dk_ref, dv_ref)):
            cols = pl.ds(chunk * COL_CHUNK, COL_CHUNK)
            val_t = ref[:, 0:HEAD_DIM, :].reshape(COL_CHUNK, tile)
            db_ref[:, cols] += _dot_nt(ones, val_t)
            d_u += _dot_tn(val_t, w_ref[cols, :])

        d_pooled = dpo_ref[...]
        d_halo = dph_ref[...] * jnp.where(i < nt - 1, 1.0, 0.0)
        pos = i * tile + lax.broadcasted_iota(jnp.int32, (tile, 1), 0) + 1
        d_p = []
        for g, window in enumerate(POOL_WINDOWS):
            cols = slice(g * GROUP_DIM, (g + 1) * GROUP_DIM)
            scaled = d_pooled[:, cols] / jnp.minimum(pos, window).astype(F32)
            d_p.append(_window_sum(scaled, d_halo[:, cols] * (1.0 / window), window, True) - d_pooled[:, cols])
        d_u += emit(3, jnp.concatenate(d_p, axis=1))
        d_u += emit(4, dga_ref[...])
        d_u += emit(5, dgp_ref[...])

        row = lax.broadcasted_iota(jnp.int32, (tile, tile), 0)
        col = lax.broadcasted_iota(jnp.int32, (tile, tile), 1)
        later = (row >= col).astype(BF16)
        d_logf = sum(_dot(part, later) for part in _split3(dcum_ref[:, 0, :])) + carry_ref[:, 0:1]
        carry_ref[:, 0:1] = d_logf[:, 0:1]
        d_f = d_logf * _sigmoid(-f_ref[...].T[0:N_HEADS, :])
        d_f = jnp.concatenate([d_f, jnp.zeros((LANE - N_HEADS, tile), F32)], axis=0)
        dbf_ref[...] += sum(_dot_nt(ones, part) for part in _split3(d_f))
        d_fb = d_f.astype(BF16)
        d_u += _dot_tn(d_fb, wf_ref[...])
        dwf_ref[...] += _dot(d_fb, u_ref[...])

        x = x_ref[...]
        dx_ref[...] = dxa_ref[...] + d_u * mod_ref[0:1, :]
        dmod_ref[0:1, :] += jnp.sum(d_u * x, axis=0, keepdims=True)
        dmod_ref[1:2, :] += jnp.sum(d_u, axis=0, keepdims=True)

    rev = lambda step: nt - 1 - step
    tok = lambda width: pl.BlockSpec((tile, width), lambda s: (rev(s), 0))
    head_block = pl.BlockSpec((N_HEADS, LANE, tile), lambda s: (0, 0, rev(s)))
    whole = lambda a: pl.BlockSpec(a.shape, lambda s: (0,) * a.ndim)
    halo = pl.BlockSpec((HALO, D_POOL), lambda s: (jnp.minimum((rev(s) + 1) * halo_blocks, seq // HALO - 1), 0))
    small = lambda width: jax.ShapeDtypeStruct((8, width), F32)
    n_rest = N_MAIN - OFF_P
    return pl.pallas_call(
        body, name="inproj_backward", grid=(nt,),
        in_specs=[head_block, head_block, head_block, pl.BlockSpec((N_HEADS, 1, tile), lambda s: (0, 0, rev(s))),
                  tok(LANE), tok(D_POOL), halo, tok(D_ATT), tok(D_POOL),
                  tok(D), tok(D), tok(D),
                  whole(mod), whole(w_main), whole(w_f)],
        out_specs=[tok(D), tok(n_rest), pl.BlockSpec((LANE, D), lambda s: (0, 0)),
                   pl.BlockSpec((8, N_MAIN), lambda s: (0, 0)), pl.BlockSpec((8, LANE), lambda s: (0, 0)),
                   pl.BlockSpec((8, D), lambda s: (0, 0))],
        out_shape=[jax.ShapeDtypeStruct((seq, D), F32), jax.ShapeDtypeStruct((seq, n_rest), BF16),
                   jax.ShapeDtypeStruct((LANE, D), F32), small(N_MAIN), small(LANE), small(D)],
        scratch_shapes=[pltpu.VMEM((8, LANE), F32)],
        compiler_params=_params(("arbitrary",)),
    )(dqp, dkp, dvp, d_cum, f, d_pooled, d_pooled, d_ga, d_gp, x, dxa, u, mod, w_main, w_f)


def _weight_grads(dq_t, dk_t, dv_t, dw_f, dproj, u, k_tile):
    seq = u.shape[0]
    nk = seq // k_tile
    rows = N_HEADS * HEAD_DIM

    def body(dq_ref, dk_ref, dv_ref, dwf_ref, dp_ref, u_ref, out_ref, acc_ref):
        k = pl.program_id(0)

        @pl.when(k == 0)
        def _():
            acc_ref[...] = jnp.zeros_like(acc_ref)

        tokens = u_ref[...]
        for j, ref in enumerate((dq_ref, dk_ref, dv_ref)):
            acc_ref[pl.ds(j * rows, rows), :] += _dot(ref[...].reshape(rows, k_tile), tokens)
        for j in range(dproj.shape[1] // COL_CHUNK):
            cols = pl.ds(j * COL_CHUNK, COL_CHUNK)
            acc_ref[pl.ds(F_HI + j * COL_CHUNK, COL_CHUNK), :] += _dot_tn(dp_ref[:, cols], tokens)

        @pl.when(k == nk - 1)
        def _():
            acc_ref[F_LO:F_HI, :] = dwf_ref[0:N_HEADS, :]
            for slot in range(N_DEV):
                out_ref[slot] = acc_ref[slot * shard:(slot + 1) * shard, :].astype(BF16)

    shard = D_IN // N_DEV
    heads = pl.BlockSpec((N_HEADS, HEAD_DIM, k_tile), lambda k: (0, 0, k))
    return pl.pallas_call(
        body, name="weight_grads", grid=(nk,),
        in_specs=[heads, heads, heads, pl.BlockSpec(dw_f.shape, lambda k: (0, 0)),
                  pl.BlockSpec((k_tile, dproj.shape[1]), lambda k: (k, 0)), pl.BlockSpec((k_tile, D), lambda k: (k, 0))],
        out_specs=pl.BlockSpec((N_DEV, shard, D), lambda k: (0, 0, 0)),
        out_shape=jax.ShapeDtypeStruct((N_DEV, shard, D), BF16),
        scratch_shapes=[pltpu.VMEM((D_IN, D), F32)],
        compiler_params=_params(("arbitrary",)),
    )(dq_t, dk_t, dv_t, dw_f, dproj, u)


def _adamw(w, g, m, v):
    m = ADAM_B1 * m + (1.0 - ADAM_B1) * g
    v = ADAM_B2 * v + (1.0 - ADAM_B2) * (g * g)
    m_hat = m / (1.0 - ADAM_B1 ** ADAM_STEP)
    v_hat = v / (1.0 - ADAM_B2 ** ADAM_STEP)
    delta = -ADAM_LR * (m_hat / (jnp.sqrt(v_hat) + ADAM_EPS) + ADAM_WD * w)
    return delta, m, v


SUBLANES = 8


def _adamw_packed(g, w, m, v, name, chunks=4):
    rows, cols = g.shape
    per_row = cols // LANE
    assert cols % LANE == 0 and per_row == SUBLANES and w.shape == (rows * per_row, LANE)
    step = -(-rows // (chunks * SUBLANES)) * SUBLANES
    bounds = [(r0, min(r0 + step, rows)) for r0 in range(0, rows, step)]

    def body(g_hbm, w_hbm, m_hbm, v_hbm, og_hbm, od_hbm, om_hbm, ov_hbm, g_buf, in_buf, out_buf, in_sems, out_sems):
        def copies_in(c):
            r0, r1 = bounds[c]
            packed = slice(r0 * per_row, r1 * per_row)
            return [pltpu.make_async_copy(g_hbm.at[r0:r1], g_buf.at[r0:r1], in_sems.at[c, 0])] + [
                pltpu.make_async_copy(src.at[packed], in_buf.at[i, packed], in_sems.at[c, 1 + i])
                for i, src in enumerate((w_hbm, m_hbm, v_hbm))]

        def copies_out(c):
            r0, r1 = bounds[c]
            packed = slice(r0 * per_row, r1 * per_row)
            return [pltpu.make_async_copy(out_buf.at[i, packed], dst.at[packed], out_sems.at[c, i])
                    for i, dst in enumerate((og_hbm, od_hbm, om_hbm, ov_hbm))]

        for c in range(len(bounds)):
            for cp in copies_in(c):
                cp.start()
        for c, (r0, r1) in enumerate(bounds):
            for cp in copies_in(c):
                cp.wait()
            for j in range(per_row):
                lanes = pl.ds(r0 * per_row + j, r1 - r0, stride=per_row)
                g_part = g_buf[r0:r1, j * LANE:(j + 1) * LANE]
                results = _adamw(in_buf[0, lanes, :], g_part, in_buf[1, lanes, :], in_buf[2, lanes, :])
                for i, val in enumerate((g_part,) + results):
                    out_buf[i, lanes, :] = val
            for cp in copies_out(c):
                cp.start()
        for c in range(len(bounds)):
            for cp in copies_out(c):
                cp.wait()

    shape = jax.ShapeDtypeStruct(w.shape, F32)
    return pl.pallas_call(
        body, name=name,
        in_specs=[ANY] * 4, out_specs=[ANY] * 4, out_shape=[shape] * 4,
        scratch_shapes=[pltpu.VMEM(g.shape, F32), pltpu.VMEM((3,) + w.shape, F32), pltpu.VMEM((4,) + w.shape, F32),
                        pltpu.SemaphoreType.DMA((len(bounds), 4)), pltpu.SemaphoreType.DMA((len(bounds), 4))],
        compiler_params=_params(),
    )(g, w, m, v)


def _ada_adamw(sc_all, d_ada, w, m, v, chunks=4):
    rows, cols = w.shape
    step, sub = rows // chunks, 32
    assert rows % chunks == 0 and step % LANE == 0 and step % sub == 0

    def body(sc_ref, d_ref, w_hbm, m_hbm, v_hbm, og_hbm, od_hbm, om_hbm, ov_hbm, in_buf, out_buf, in_sems, out_sems):
        def copies_in(c):
            part = slice(c * step, (c + 1) * step)
            return [pltpu.make_async_copy(src.at[part], in_buf.at[i, part], in_sems.at[c, i])
                    for i, src in enumerate((w_hbm, m_hbm, v_hbm))]

        def copies_out(c):
            part = slice(c * step, (c + 1) * step)
            return [pltpu.make_async_copy(out_buf.at[i, part], dst.at[part], out_sems.at[c, i])
                    for i, dst in enumerate((og_hbm, od_hbm, om_hbm, ov_hbm))]

        for c in range(chunks):
            for cp in copies_in(c):
                cp.start()
        for c in range(chunks):
            sc_t = sc_ref[:, c * step:(c + 1) * step].T
            for cp in copies_in(c):
                cp.wait()
            for r0 in range(0, step, sub):
                part = slice(c * step + r0, c * step + r0 + sub)
                g = sc_t[r0:r0 + sub, 0:1] * d_ref[0:1, :]
                for b in range(1, N_DEV):
                    g = g + sc_t[r0:r0 + sub, b:b + 1] * d_ref[b:b + 1, :]
                results = _adamw(in_buf[0, part, :], g, in_buf[1, part, :], in_buf[2, part, :])
                for i, val in enumerate((g,) + results):
                    out_buf[i, part, :] = val
            for cp in copies_out(c):
                cp.start()
        for c in range(chunks):
            for cp in copies_out(c):
                cp.wait()

    in_vmem = pl.BlockSpec(memory_space=pltpu.VMEM)
    shape = jax.ShapeDtypeStruct(w.shape, F32)
    return pl.pallas_call(
        body, name="ada_adamw",
        in_specs=[in_vmem, in_vmem, ANY, ANY, ANY], out_specs=[ANY] * 4, out_shape=[shape] * 4,
        scratch_shapes=[pltpu.VMEM((3,) + w.shape, F32), pltpu.VMEM((4,) + w.shape, F32),
                        pltpu.SemaphoreType.DMA((chunks, 3)), pltpu.SemaphoreType.DMA((chunks, 4))],
        compiler_params=_params(),
    )(sc_all, d_ada, w, m, v)


F_LO, F_HI = 3 * D_ATT, 3 * D_ATT + N_HEADS


def _split_forget(a, axis):
    idx = lambda lo, hi: tuple(slice(lo, hi) if d == axis else slice(None) for d in range(a.ndim))
    pad = [(0, LANE - N_HEADS) if d == axis else (0, 0) for d in range(a.ndim)]
    return jnp.concatenate([a[idx(0, F_LO)], a[idx(F_HI, D_IN)]], axis=axis), jnp.pad(a[idx(F_LO, F_HI)], pad)


def _join_forget(main, f, axis):
    idx = lambda lo, hi: tuple(slice(lo, hi) if d == axis else slice(None) for d in range(main.ndim))
    return jnp.concatenate([main[idx(0, F_LO)], f[idx(0, N_HEADS)], main[idx(F_LO, N_MAIN)]], axis=axis)


def _adamw_small(grad_rows, row_params, whole_params, summed_params, scalar_at):
    n_row, n_whole, n_sum = len(row_params), len(whole_params), len(summed_params)
    n = n_row + n_whole + n_sum

    def body(g_ref, *refs):
        n_in = 3 * n_row + 4 * (n_whole + n_sum)
        ins, outs = list(refs[:n_in]), refs[n_in:]
        for i in range(n):
            if i < n_row:
                row, lo, hi = row_params[i][:3]
                g = g_ref[row:row + 1, lo:hi]
            elif i < n_row + n_whole:
                g = ins.pop(0)[...]
            else:
                parts = ins.pop(0)
                g = parts[0]
                for k in range(1, N_DEV):
                    g = g + parts[k]
            w, m, v = (ins.pop(0)[...] for _ in range(3))
            outs[4 * i][...] = g
            outs[4 * i + 1][...], outs[4 * i + 2][...], outs[4 * i + 3][...] = _adamw(w, g, m, v)
        row, lane = scalar_at
        outs[4 * n][...] = g_ref[row:row + 1, lane:lane + 1]

    shapes = [p[3] for p in row_params] + [p[1] for p in whole_params] + [p[1] for p in summed_params]
    operands = [a for p in row_params for a in p[3:]] + [a for p in whole_params + summed_params for a in p]
    flat = pl.pallas_call(
        body, name="adamw_small",
        out_shape=[jax.ShapeDtypeStruct(w.shape, F32) for w in shapes for _ in range(4)]
        + [jax.ShapeDtypeStruct((1, 1), F32)],
        compiler_params=_params(),
    )(grad_rows, *operands)
    return [flat[4 * i:4 * i + 4] for i in range(n)], flat[4 * n].reshape(())


def kernel(x, c, w_ada, b_ada, w_in, b_in, w_pool_mix, b_pool_mix, pool_scale, w_out, b_out, ln_g, ln_b, loss_target, m_w_ada, m_b_ada, m_w_in, m_b_in, m_w_pool_mix, m_b_pool_mix, m_pool_scale, m_w_out, m_b_out, m_ln_g, m_ln_b, v_w_ada, v_b_ada, v_w_in, v_b_in, v_w_pool_mix, v_b_pool_mix, v_pool_scale, v_w_out, v_b_out, v_ln_g, v_ln_b):
    seq = x.shape[1]
    tile = min(256, seq)
    attn_tile = min(512, max(128, seq // 4))
    me = _dev_index(*_mesh_pos())
    x2, tgt = x[0], loss_target[0]

    rows_of = lambda a: jnp.swapaxes(a, 1, 2)[0]
    w_main, w_f, sc_all, ada_mine = _gather_and_ada(c, rows_of(w_in).astype(BF16), w_ada[0])
    ada = ada_mine.reshape(1, D_ADA) + b_ada
    shift, scale, gate = ada[:, 0:D], ada[:, D:2 * D], ada[:, 2 * D:]
    mod = jnp.concatenate([1.0 + scale, shift, jnp.zeros((6, D), F32)], axis=0)
    b_main, b_f = _split_forget(b_in, 1)

    qp, kp, vp, f, p, g_att, g_pool, u = _inproj_forward(x2, mod, w_main, w_f, b_main, b_f, tile)
    att, q2t, w_out_g = _attention_forward(qp, kp, vp, w_out[0].astype(BF16), attn_tile)

    vecs = jnp.concatenate([gate, b_out, ln_g, ln_b, jnp.zeros((4, D), F32)], axis=0)
    pool_vecs = jnp.concatenate([b_pool_mix.reshape(1, D_POOL), pool_scale, jnp.zeros((6, D_POOL), F32)], axis=0)
    dxa, do2, d_ga, d_gp, d_pooled, gw_out, dw_pool, dvec = _middle(
        x2, tgt, att, g_att, g_pool, p, vecs, pool_vecs, w_out_g.reshape(D, D), w_pool_mix[0].astype(BF16), tile)

    pool_rows = w_pool_mix.shape[1] * GROUP_DIM
    dqp, dkp, dvp, d_cum, g_out, dvec_sum, dw_pool_sum = _attention_backward(
        q2t, kp, vp, do2, gw_out.reshape(N_DEV, D // N_DEV, D), dvec, dw_pool.reshape(pool_rows, GROUP_DIM), attn_tile)
    dx, dproj, dw_f, db_main, db_f, dmod = _inproj_backward(
        dqp, dkp, dvp, d_cum, f, d_pooled, d_ga, d_gp, x2, dxa, u, mod, w_main, w_f, tile)
    gw_in = _weight_grads(dqp, dkp, dvp, dw_f, dproj, u, min(512, seq))
    d_ada = jnp.concatenate([dmod[1:2], dmod[0:1], dvec[5:6]], axis=1)
    g_in_rows, g_b_in, d_ada_all = _reduce_grads(gw_in, _join_forget(db_main[0:1], db_f[0:1], 1), d_ada)

    packed = lambda a: jnp.transpose(a.reshape(SUBLANES, LANE, -1), (2, 0, 1)).reshape(-1, LANE)
    outs_in = _adamw_packed(g_in_rows, packed(w_in), packed(m_w_in), packed(v_w_in), "adamw_w_in")
    g_w_in, d_w_in, nm_w_in, nv_w_in = (
        jnp.transpose(a.reshape(-1, SUBLANES, LANE), (1, 2, 0)).reshape(D, -1) for a in outs_in)
    flat_pool = lambda a: a.reshape(1, D_POOL)
    pool_2d = lambda a: a.reshape(pool_rows, GROUP_DIM)
    rows, loss = _adamw_small(
        dvec_sum,
        [(0, 0, D, b_out, m_b_out, v_b_out), (1, 0, D, ln_g, m_ln_g, v_ln_g), (2, 0, D, ln_b, m_ln_b, v_ln_b),
         (3, 0, D_POOL, flat_pool(b_pool_mix), flat_pool(m_b_pool_mix), flat_pool(v_b_pool_mix)),
         (3, D_POOL, 2 * D_POOL, pool_scale, m_pool_scale, v_pool_scale)],
        [(g_out, w_out[0], m_w_out[0], v_w_out[0]),
         (dw_pool_sum, pool_2d(w_pool_mix), pool_2d(m_w_pool_mix), pool_2d(v_w_pool_mix)),
         (g_b_in, b_in, m_b_in, v_b_in)],
        [(d_ada_all, b_ada, m_b_ada, v_b_ada)],
        scalar_at=(4, 0))
    small = {"b_out": rows[0], "ln_g": rows[1], "ln_b": rows[2],
             "b_pool": [a.reshape(b_pool_mix.shape) for a in rows[3]], "pool_scale": rows[4],
             "w_pool": [a.reshape(w_pool_mix.shape) for a in rows[6]], "b_in": rows[7]}
    g_s, d_s, nm_s, nv_s = ({k: r[j] for k, r in small.items()} for j in range(4))
    g_w_out, d_w_out, nm_w_out, nv_w_out = rows[5]
    g_b_ada, d_b_ada, nm_b_ada, nv_b_ada = rows[8]

    d_ada_local = lax.dynamic_slice_in_dim(d_ada_all.reshape(N_DEV, D_ADA), me * (D_ADA // N_DEV), D_ADA // N_DEV, axis=1)
    g_w_ada, d_w_ada, nm_w_ada, nv_w_ada = _ada_adamw(sc_all, d_ada_local, w_ada[0], m_w_ada[0], v_w_ada[0])

    def ordered(w_ada_, b_ada_, w_in_, w_out_, s):
        return (w_ada_[None], b_ada_, w_in_[None], s["b_in"], s["w_pool"], s["b_pool"], s["pool_scale"],
                w_out_[None], s["b_out"], s["ln_g"], s["ln_b"])

    return (loss, dx[None],
            *ordered(g_w_ada, g_b_ada, g_w_in, g_w_out, g_s),
            *ordered(d_w_ada, d_b_ada, d_w_in, d_w_out, d_s),
            *ordered(nm_w_ada, nm_b_ada, nm_w_in, nm_w_out, nm_s),
            *ordered(nv_w_ada, nv_b_ada, nv_w_in, nv_w_out, nv_s))
```

```python
import jax
import jax.numpy as jnp
from jax import lax
from jax.experimental import pallas as pl
from jax.experimental.pallas import tpu as pltpu

F32 = jnp.float32
BF16 = jnp.bfloat16

N_DEV = 8
D = 1024
N_HEADS = 8
HEAD_DIM = 64
D_ATT = 512
D_POOL = 512
POOL_WINDOWS = (2, 4, 8, 16)
GROUP_DIM = 128
HALO = 16
LANE = 128
BF16_TILE_ROWS = 16
D_IN = 3080
D_ADA = 3072
N_MAIN = 3072
OFF_P = 1536
COL_CHUNK = 512
Q_SCALE = 0.125
LN_EPS = 1e-5
ALPHA = 2.0 ** 0.25
L_CQ, L_CK, L_LSE = 64, 67, 70

ADAM_LR, ADAM_B1, ADAM_B2, ADAM_EPS, ADAM_WD, ADAM_STEP = 0.001, 0.9, 0.999, 1e-08, 0.01, 10
VMEM_LIMIT = 56 * 1024 * 1024

MESH = pl.DeviceIdType.MESH
ANY = pl.BlockSpec(memory_space=pl.ANY)


def _params(sem=None, vmem=VMEM_LIMIT):
    return pltpu.CompilerParams(dimension_semantics=sem, vmem_limit_bytes=vmem)


def _split3(a):
    hi = a.astype(BF16)
    r = a - hi.astype(F32)
    mid = r.astype(BF16)
    lo = (r - mid.astype(F32)).astype(BF16)
    return hi, mid, lo


def _dot(a, b):
    return jnp.dot(a, b, preferred_element_type=F32)


def _dot_nt(a, b):
    return lax.dot_general(a, b, (((1,), (1,)), ((), ())), preferred_element_type=F32)


def _dot_tn(a, b):
    return lax.dot_general(a, b, (((0,), (0,)), ((), ())), preferred_element_type=F32)


def _dot3(m01, a):
    hi, mid, lo = _split3(a)
    return _dot(m01, hi) + _dot(m01, mid) + _dot(m01, lo)


def _sigmoid(z):
    return 1.0 / (1.0 + jnp.exp(-z))


def _lanes(shape):
    return lax.broadcasted_iota(jnp.int32, shape, len(shape) - 1)


def _place3(lane, base, parts, other):
    out = other
    for j in range(3):
        out = jnp.where(lane == base + j, parts[j], out)
    return out


def _mesh_pos():
    return lax.axis_index("x"), lax.axis_index("y"), lax.axis_index("c")


def _dev_index(px, py, pc):
    return 4 * px + 2 * py + pc


N_GATHER_SEMS = 9


def _gather_stages(src_ref, out_ref, send_sems, recv_sems, local_sem):
    x, y, c = _mesh_pos()
    me, sibling = (x, y, c), (x, y, 1 - c)
    nbr_x, nbr_y, diag = (1 - x, y), (x, 1 - y), (1 - x, 1 - y)
    half = out_ref.shape[-1] // 2
    left, right = pl.ds(0, half), pl.ds(half, half)

    def copy(k, block, to, cols=None, src=None):
        slot = out_ref.at[_dev_index(*block)]
        if cols is not None:
            slot = slot.at[:, cols]
        return pltpu.make_async_remote_copy(
            src_ref=slot if src is None else src, dst_ref=slot, send_sem=send_sems.at[k], recv_sem=recv_sems.at[k],
            device_id=to, device_id_type=MESH)

    mine = pltpu.make_async_copy(src_ref, out_ref.at[_dev_index(*me)], local_sem)
    first = [copy(0, me, sibling, src=src_ref), copy(1, me, (*nbr_x, c), src=src_ref), copy(2, me, (*nbr_y, c), src=src_ref)]
    relay = [(1, nbr_x, None, nbr_x), (2, nbr_y, None, nbr_y), (3, diag, left, nbr_y), (4, diag, right, nbr_x)]
    onward = [copy(3, (*nbr_x, c), (*nbr_y, c), cols=left), copy(4, (*nbr_y, c), (*nbr_x, c), cols=right)]
    passed = [copy(4 + k, (*block, c), sibling, cols=cols) for k, block, cols, _ in relay]

    def start():
        mine.start()
        for cp in first:
            cp.start()

    def relay_stage(first_item):
        def run():
            for j in (first_item, first_item + 1):
                k, block, cols, frm = relay[j]
                copy(k, (*block, c), (*frm, c), cols=cols).wait_recv()
                if j < 2:
                    onward[j].start()
                passed[j].start()
        return run

    def from_sibling(items):
        for k, block, cols, _ in items:
            copy(4 + k, (*block, 1 - c), me, cols=cols).wait_recv()

    def finish_near():
        copy(0, sibling, me).wait_recv()
        from_sibling(relay[:2])
        mine.wait()

    def finish_far():
        from_sibling(relay[2:])
        for cp in first + onward + passed:
            cp.wait_send()

    return start, relay_stage(0), relay_stage(2), finish_near, finish_far


N_REDUCE_SEMS = 10
N_SMALL_SEMS = 4
N_ROWS_SEMS = 7


def _reduce_stages(ins, gs, r1, s2, r2, smalls, send_sems, recv_sems, rows=None, own=None):
    n = len(ins)
    x, y, c = _mesh_pos()
    me = _dev_index(x, y, c)
    sibling = (x, y, 1 - c)
    chips = [(x, y), (1 - x, y), (x, 1 - y), (1 - x, 1 - y)]
    peers = []
    for p in range(1, N_DEV):
        px, py, pc = (p >> 2) & 1, (p >> 1) & 1, p & 1
        peers.append((1 - x if px else x, 1 - y if py else y, 1 - c if pc else c))
    base_small = N_REDUCE_SEMS * n

    def remote(src, dst, k, to):
        return pltpu.make_async_remote_copy(src_ref=src, dst_ref=dst, send_sem=send_sems.at[k],
                                            recv_sem=recv_sems.at[k], device_id=to, device_id_type=MESH)

    def level1(a, q):
        return remote(ins[a].at[_dev_index(*chips[q], 1 - c)], r1[a].at[q], N_REDUCE_SEMS * a + q, sibling)

    def level2(a, k):
        half = ins[a].shape[-1] // 2
        left, right = pl.ds(0, half), pl.ds(half, half)
        nbr_x, nbr_y = (*chips[1], c), (*chips[2], c)
        src_slot, dst_slot, cols, to = [(0, 0, left, nbr_x), (1, 1, right, nbr_y), (2, 2, left, nbr_x),
                                        (2, 2, right, nbr_y), (0, 0, right, nbr_x), (1, 1, left, nbr_y)][k]
        return remote(s2[a].at[src_slot, :, cols], r2[a].at[dst_slot, :, cols], N_REDUCE_SEMS * a + 4 + k, to)

    to_sibling = [remote(sm[0], sm[2], base_small + 4 * i, sibling) for i, sm in enumerate(smalls)]
    to_chips = [[remote(sm[3], sm[4].at[j], base_small + 4 * i + 1 + j, (*chips[j + 1], c)) for j in range(3)]
                for i, sm in enumerate(smalls)]
    if rows is not None:
        rows_ref, land_ref, all_ref = rows
        base_rows = base_small + 4 * len(smalls)
        row_sends = [remote(rows_ref, land_ref.at[me], base_rows + k, to) for k, to in enumerate(peers)]

    order = (3, 1, 2, 0)

    def mine(a, q):
        buf, sems = own[a]
        return pltpu.make_async_copy(ins[a].at[_dev_index(*chips[q], c)], buf.at[q], sems.at[q])

    def start():
        for a in range(n):
            for q in order:
                level1(a, q).start()
            if own is not None:
                for q in order:
                    mine(a, q).start()
        for cp in to_sibling:
            cp.start()
        if rows is not None:
            for cp in row_sends:
                cp.start()
            land_ref[me] = rows_ref[...]

    def middle():
        for a in range(n):
            for q in order:
                level1(a, q).wait_recv()
                if own is None:
                    kept = ins[a][_dev_index(*chips[q], c)]
                else:
                    mine(a, q).wait()
                    kept = own[a][0][q]
                pair = kept.astype(F32) + r1[a][q].astype(F32)
                if q == 0:
                    gs[a][...] = pair
                else:
                    s2[a][q - 1] = pair.astype(BF16)
                    for k in ((0,), (1,), (2, 3))[q - 1]:
                        level2(a, k).start()
        for i, (small_ref, _, sm_sib, sm_chip, _) in enumerate(smalls):
            to_sibling[i].wait_recv()
            sm_chip[...] = small_ref[...] + sm_sib[...]
            for cp in to_chips[i]:
                cp.start()

    def fold():
        for a in range(n):
            half = ins[a].shape[-1] // 2
            level2(a, 3).wait_recv()
            s2[a][0, :, half:] = (s2[a][0, :, half:].astype(F32) + r2[a][2, :, half:].astype(F32)).astype(BF16)
            level2(a, 4).start()
            level2(a, 2).wait_recv()
            s2[a][1, :, :half] = (s2[a][1, :, :half].astype(F32) + r2[a][2, :, :half].astype(F32)).astype(BF16)
            level2(a, 5).start()

    def finish():
        for a in range(n):
            for k in (0, 1, 4, 5):
                level2(a, k).wait_recv()
            gs[a][...] = gs[a][...] + r2[a][0].astype(F32) + r2[a][1].astype(F32)
            for q in range(4):
                level1(a, q).wait_send()
            for k in range(6):
                level2(a, k).wait_send()
        for i, (_, total_ref, _, sm_chip, sm_recv) in enumerate(smalls):
            for cp in to_chips[i]:
                cp.wait_recv()
            total = None
            for ax in range(2):
                for ay in range(2):
                    dx, dy = x != ax, y != ay
                    term = jnp.where(dx, jnp.where(dy, sm_recv[2], sm_recv[0]), jnp.where(dy, sm_recv[1], sm_chip[...]))
                    total = term if total is None else total + term
            total_ref[...] = total
            for cp in [to_sibling[i]] + to_chips[i]:
                cp.wait_send()
        if rows is not None:
            for cp in row_sends:
                cp.wait_send()

    def rows_ready():
        for k, frm in enumerate(peers):
            remote(rows_ref, land_ref.at[_dev_index(*frm)], base_rows + k, frm).wait_recv()
        all_ref[...] = land_ref[...]

    return start, middle, fold, finish, rows_ready


def _reduce_scratch(shard, smalls, rows=None):
    out = [pltpu.VMEM((lead,) + shard.shape[1:], BF16) for lead in (4, 3, 3)]
    for small in smalls:
        out += [pltpu.VMEM(small.shape, F32), pltpu.VMEM(small.shape, F32), pltpu.VMEM((3,) + small.shape, F32)]
    n_sems = N_REDUCE_SEMS + N_SMALL_SEMS * len(smalls)
    if rows is not None:
        out.append(pltpu.VMEM((N_DEV,) + rows.shape, F32))
        n_sems += N_ROWS_SEMS
    return out + [pltpu.SemaphoreType.DMA((n_sems,))] * 2


def _reduce_grads(gw_in, small, rows, sc_all, w_ada, m_ada, v_ada):
    cols = w_ada.shape[1]

    def body(in_ref, small_ref, rows_ref, sc_ref, w_hbm, m_hbm, v_hbm,
             g_ref, total_ref, rows_all_ref, og_hbm, od_hbm, om_hbm, ov_hbm,
             r1, s2, r2, sm_sib, sm_chip, sm_recv, rows_land, send_sems, recv_sems, kept, kept_sems,
             d_local, ada_in, ada_out, ada_in_sems, ada_out_sems):
        start, middle, fold, finish, rows_ready = _reduce_stages(
            [in_ref], [g_ref], [r1], [s2], [r2], [(small_ref, total_ref, sm_sib, sm_chip, sm_recv)],
            send_sems, recv_sems, rows=(rows_ref, rows_land, rows_all_ref), own=[(kept, kept_sems)])
        ada_start, ada_run, ada_finish = _ada_adamw_steps(
            sc_ref, d_local, (w_hbm, m_hbm, v_hbm), (og_hbm, od_hbm, om_hbm, ov_hbm),
            ada_in, ada_out, ada_in_sems, ada_out_sems)
        start()
        ada_start()
        middle()
        rows_ready()
        me = _dev_index(*_mesh_pos())
        for k in range(N_DEV):
            @pl.when(me == k)
            def _(k=k):
                for b in range(N_DEV):
                    d_local[b:b + 1, :] = rows_all_ref[b, :, k * cols:(k + 1) * cols]
        ada_run()
        fold()
        finish()
        ada_finish()

    vmem = pl.BlockSpec(memory_space=pltpu.VMEM)
    ada_shape = jax.ShapeDtypeStruct(w_ada.shape, F32)
    outs = pl.pallas_call(
        body, name="reduce_grads",
        in_specs=[ANY, vmem, vmem, vmem, ANY, ANY, ANY], out_specs=[vmem, vmem, vmem] + [ANY] * 4,
        out_shape=[jax.ShapeDtypeStruct(gw_in.shape[1:], F32), jax.ShapeDtypeStruct(small.shape, F32),
                   jax.ShapeDtypeStruct((N_DEV,) + rows.shape, F32)] + [ada_shape] * 4,
        scratch_shapes=_reduce_scratch(gw_in, [small], rows)
        + [pltpu.VMEM((4,) + gw_in.shape[1:], BF16), pltpu.SemaphoreType.DMA((4,)), pltpu.VMEM((N_DEV, cols), F32)]
        + _ada_adamw_scratch(w_ada.shape),
        compiler_params=_params(),
    )(gw_in, small, rows, sc_all, w_ada, m_ada, v_ada)
    return outs[0], outs[1], outs[2], outs[3:]


def _dot3_rhs(a, b):
    a0, a1, a2 = _split3(a)
    b0, b1, b2 = _split3(b)
    return (_dot(a0, b0) + (_dot(a0, b1) + _dot(a1, b0))
            + (_dot(a0, b2) + _dot(a1, b1) + _dot(a2, b0)))


def _gather_and_ada(c, w_in_rows, w_ada):
    cols = w_ada.shape[1]
    shard = w_in_rows.shape[0]

    def body(c_ref, w_ref, wa_ref, w_main_hbm, w_f_ref, sc_ref, ada_ref,
             w_all_ref, w_f32, wm_buf, c_land, part, ada_land, send_sems, recv_sems, local_sem, x_send, x_recv, out_sems):
        x, y, cc = _mesh_pos()
        me = _dev_index(x, y, cc)
        peers = []
        for p in range(1, N_DEV):
            px, py, pc = (p >> 2) & 1, (p >> 1) & 1, p & 1
            peers.append((1 - x if px else x, 1 - y if py else y, 1 - cc if pc else cc))

        def remote(src, dst, k, to):
            return pltpu.make_async_remote_copy(src_ref=src, dst_ref=dst, send_sem=x_send.at[k], recv_sem=x_recv.at[k],
                                                device_id=to, device_id_type=MESH)

        c_sends = [remote(c_ref, c_land.at[me], k, to) for k, to in enumerate(peers)]
        for cp in c_sends:
            cp.start()
        start, relay_near, relay_far, finish_near, finish_far = _gather_stages(
            w_ref, w_all_ref, send_sems, recv_sems, local_sem.at[0])
        start()
        c_land[me] = c_ref[...]
        for k, frm in enumerate(peers):
            remote(c_ref, c_land.at[_dev_index(*frm)], k, frm).wait_recv()
        c_all = jnp.concatenate([c_land[b] for b in range(N_DEV)], axis=0)
        sc = c_all * _sigmoid(c_all)
        sc_ref[...] = sc
        rows = _dot3_rhs(sc, wa_ref[...])
        for b in range(N_DEV):
            part[b] = rows[b:b + 1, :]
        a_sends = [remote(part.at[_dev_index(*to)], ada_land.at[me], 7 + k, to) for k, to in enumerate(peers)]
        for cp in a_sends:
            cp.start()
        ada_land[me] = part[me]

        relay_near()
        finish_near()

        far_chip = 2 * (1 - x) + (1 - y)

        def stage(slots):
            for slot in slots:
                w_f32[slot * shard:(slot + 1) * shard, :] = w_all_ref[slot].astype(F32)

        def far_rows(k):
            main_row = lambda r: r if r < F_LO else r - N_HEADS
            first, last = 2 * shard * k, 2 * shard * (k + 1) - 1
            first = F_HI if F_LO <= first < F_HI else first
            last = F_LO - 1 if F_LO <= last < F_HI else last
            return (main_row(first) // BF16_TILE_ROWS * BF16_TILE_ROWS,
                    -(-(main_row(last) + 1) // BF16_TILE_ROWS) * BF16_TILE_ROWS)

        def near_rows(k):
            lo, hi = far_rows(k)
            return [(i, a, b) for i, (a, b) in enumerate(((0, lo), (hi, N_MAIN))) if a < b]

        def forget_is_far(k):
            return 2 * shard * k < F_HI and 2 * shard * (k + 1) > F_LO

        def main_copy(i, lo, hi):
            return pltpu.make_async_copy(wm_buf.at[lo:hi], w_main_hbm.at[lo:hi], out_sems.at[i])

        def write_main(i, lo, hi):
            if lo < min(hi, F_LO):
                wm_buf[lo:min(hi, F_LO), :] = w_f32[lo:min(hi, F_LO), :].astype(BF16)
            if max(lo, F_LO) < hi:
                wm_buf[max(lo, F_LO):hi, :] = w_f32[max(lo, F_LO) + N_HEADS:hi + N_HEADS, :].astype(BF16)
            main_copy(i, lo, hi).start()

        def write_forget():
            w_f_ref[...] = jnp.concatenate(
                [w_f32[F_LO:F_HI, :], jnp.zeros((LANE - N_HEADS, D), F32)], axis=0).astype(BF16)

        for k in range(N_DEV // 2):
            @pl.when(far_chip == k)
            def _(k=k):
                stage([slot for slot in range(N_DEV) if slot // 2 != k])
                for i, lo, hi in near_rows(k):
                    write_main(i, lo, hi)
                if not forget_is_far(k):
                    write_forget()

        relay_far()
        for k, frm in enumerate(peers):
            remote(part.at[0], ada_land.at[_dev_index(*frm)], 7 + k, frm).wait_recv()
        ada_ref[...] = ada_land[...]
        finish_far()
        for cp in c_sends + a_sends:
            cp.wait_send()

        for k in range(N_DEV // 2):
            @pl.when(far_chip == k)
            def _(k=k):
                stage([2 * k, 2 * k + 1])
                write_main(2, *far_rows(k))
                if forget_is_far(k):
                    write_forget()
                for i, lo, hi in near_rows(k):
                    main_copy(i, lo, hi).wait()
                main_copy(2, *far_rows(k)).wait()

    vmem = pl.BlockSpec(memory_space=pltpu.VMEM)
    return pl.pallas_call(
        body, name="gather_weights",
        in_specs=[vmem, ANY, vmem], out_specs=[ANY, vmem, vmem, vmem],
        out_shape=[jax.ShapeDtypeStruct((N_MAIN, D), BF16), jax.ShapeDtypeStruct((LANE, D), BF16),
                   jax.ShapeDtypeStruct((N_DEV, D), F32), jax.ShapeDtypeStruct((N_DEV, 1, cols), F32)],
        scratch_shapes=[pltpu.VMEM((N_DEV,) + w_in_rows.shape, BF16), pltpu.VMEM((D_IN, D), F32), pltpu.VMEM((N_MAIN, D), BF16),
                        pltpu.VMEM((N_DEV, 1, D), F32), pltpu.VMEM((N_DEV, 1, cols), F32), pltpu.VMEM((N_DEV, 1, cols), F32),
                        pltpu.SemaphoreType.DMA((N_GATHER_SEMS,)), pltpu.SemaphoreType.DMA((N_GATHER_SEMS,)),
                        pltpu.SemaphoreType.DMA((1,)),
                        pltpu.SemaphoreType.DMA((14,)), pltpu.SemaphoreType.DMA((14,)), pltpu.SemaphoreType.DMA((3,))],
        compiler_params=_params(),
    )(c, w_in_rows, w_ada)


def _inproj_forward(x, mod, w_main, w_f, b_main, b_f, tile):
    seq = x.shape[0]
    nt = seq // tile

    def body(x_ref, mod_ref, w_ref, wf_ref, b_ref, bf_ref,
             qp_ref, kp_ref, vp_ref, f_ref, p_ref, ga_ref, gp_ref, u_ref, carry_ref):
        i = pl.program_id(0)

        @pl.when(i == 0)
        def _():
            carry_ref[...] = jnp.zeros_like(carry_ref)

        u = x_ref[...] * mod_ref[0:1, :] + mod_ref[1:2, :]
        ub = u.astype(BF16)
        u_ref[...] = ub

        f = _dot_nt(ub, wf_ref[...]) + bf_ref[...]
        f_ref[...] = f
        lane = _lanes((tile, LANE))
        log_f = jnp.where(lane < N_HEADS, jnp.minimum(f, 0.0) - jnp.log(1.0 + jnp.exp(-jnp.abs(f))), 0.0)
        row = lax.broadcasted_iota(jnp.int32, (tile, tile), 0)
        col = lax.broadcasted_iota(jnp.int32, (tile, tile), 1)
        tri = (row >= col).astype(BF16)
        cum = _dot3(tri, log_f) + carry_ref[0:1, :]
        carry_ref[0:1, :] = cum[tile - 1:tile, :]
        cq = [part.astype(F32) for part in _split3(cum)]
        ck = [part.astype(F32) for part in _split3(-cum)]

        def proj(chunk):
            cols = pl.ds(chunk * COL_CHUNK, COL_CHUNK)
            return _dot_nt(ub, w_ref[cols, :]) + b_ref[:, cols]

        def head_tiles(r):
            for pair in range(N_HEADS // 2):
                both = r[:, pair * LANE:(pair + 1) * LANE]
                yield 2 * pair, both
                yield 2 * pair + 1, pltpu.roll(both, HEAD_DIM, 1)

        for h, val in head_tiles(proj(0)):
            extra = jnp.where((lane >= L_CK) & (lane < L_CK + 3), 1.0, 0.0)
            extra = _place3(lane, L_CQ, [part[:, h:h + 1] for part in cq], extra)
            qp_ref[h] = jnp.where(lane < HEAD_DIM, val * Q_SCALE, extra).astype(BF16)
        for h, val in head_tiles(proj(1)):
            ones = ((lane >= L_CQ) & (lane < L_CQ + 3)) | ((lane >= L_LSE) & (lane < L_LSE + 3))
            extra = _place3(lane, L_CK, [part[:, h:h + 1] for part in ck], jnp.where(ones, 1.0, 0.0))
            kp_ref[h] = jnp.where(lane < HEAD_DIM, val, extra).astype(BF16)
        for h, val in head_tiles(proj(2)):
            extra = jnp.where((lane >= HEAD_DIM) & (lane < HEAD_DIM + 3), -1.0, 0.0)
            vp_ref[h] = jnp.where(lane < HEAD_DIM, val, extra).astype(BF16)
        p_ref[...] = proj(3)
        ga_ref[...] = proj(4)
        gp_ref[...] = proj(5)

    head_block = pl.BlockSpec((N_HEADS, tile, LANE), lambda i: (0, i, 0))
    tok = lambda width: pl.BlockSpec((tile, width), lambda i: (i, 0))
    whole = lambda a: pl.BlockSpec(a.shape, lambda i: (0,) * a.ndim)
    padded = jax.ShapeDtypeStruct((N_HEADS, seq, LANE), BF16)
    half = jax.ShapeDtypeStruct((seq, D_ATT), F32)
    return pl.pallas_call(
        body, name="inproj_forward", grid=(nt,),
        in_specs=[tok(D), whole(mod), whole(w_main), whole(w_f), whole(b_main), whole(b_f)],
        out_specs=[head_block, head_block, head_block, tok(LANE), tok(D_POOL), tok(D_ATT), tok(D_POOL),
                   tok(D)],
        out_shape=[padded, padded, padded, jax.ShapeDtypeStruct((seq, LANE), F32), half, half, half,
                   jax.ShapeDtypeStruct((seq, D), BF16)],
        scratch_shapes=[pltpu.VMEM((8, LANE), F32)],
        compiler_params=_params(("arbitrary",)),
    )(x, mod, w_main, w_f, b_main, b_f)


def _attention_forward(qp, kp, vp, w_out, tile):
    seq = qp.shape[1]
    nb = seq // tile
    steps = (N_HEADS // 2) * nb

    def body(q_ref, k_ref, v_ref, wo_ref, att_ref, q2t_ref, wo_all_ref, s_a, s_b, m_ref, acc_ref,
             send_sems, recv_sems, local_sem):
        step = pl.program_id(0) * nb + pl.program_id(1)
        start, relay_near, relay_far, finish_near, finish_far = _gather_stages(
            wo_ref, wo_all_ref, send_sems, recv_sems, local_sem.at[0])
        pl.when(step == 0)(start)
        pl.when(step == steps // 4)(relay_near)
        pl.when(step == (3 * steps) // 4)(relay_far)

        i = pl.program_id(1)
        sub = lax.broadcasted_iota(jnp.int32, (LANE, tile), 0)
        row = lax.broadcasted_iota(jnp.int32, (tile, tile), 0)
        col = lax.broadcasted_iota(jnp.int32, (tile, tile), 1)
        q = [q_ref[0], q_ref[1]]

        def scores(buf, kb):
            rows = pl.ds(pl.multiple_of(kb * tile, tile), tile)
            for hh in range(2):
                buf[hh] = _dot_nt(k_ref[hh, rows, :], q[hh])

        def absorb(buf, kb, masked):
            rows = pl.ds(pl.multiple_of(kb * tile, tile), tile)
            for hh in range(2):
                m = m_ref[hh, 0:1, :]
                s = buf[hh]
                if masked:
                    s = jnp.where(row <= col, s, -1e30)
                m_new = jnp.maximum(m, jnp.max(s, axis=0, keepdims=True))
                p = jnp.exp(s - m_new).astype(BF16)
                acc_ref[hh] = jnp.exp(m - m_new) * acc_ref[hh] + _dot_tn(v_ref[hh, rows, :], p)
                m_ref[hh, 0:1, :] = m_new

        def two_blocks(j, _):
            scores(s_b, 2 * j + 1)
            absorb(s_a, 2 * j, False)
            scores(s_a, 2 * j + 2)
            absorb(s_b, 2 * j + 1, False)
            return 0

        def last_block():
            absorb(s_a, i, True)

        def last_two_blocks():
            scores(s_b, i)
            absorb(s_a, i - 1, False)
            absorb(s_b, i, True)

        scores(s_a, 0)
        m_ref[...] = jnp.full(m_ref.shape, -1e30, F32)
        acc_ref[...] = jnp.zeros_like(acc_ref)
        lax.fori_loop(0, i // 2, two_blocks, 0)
        lax.cond(i % 2 == 0, last_block, last_two_blocks)
        outs = []
        for hh in range(2):
            m, acc = m_ref[hh, 0:1, :], acc_ref[hh]
            l = -acc[HEAD_DIM:HEAD_DIM + 1, :]
            outs.append((acc / l)[:HEAD_DIM, :])
            neg_lse = [part.astype(F32) for part in _split3(-(m + jnp.log(l)))]
            q2t_ref[hh] = _place3(sub, L_LSE, neg_lse, q[hh].astype(F32).T).astype(BF16)
        att_ref[...] = jnp.concatenate(outs, axis=0).T
        @pl.when(step == steps - 1)
        def _():
            finish_near()
            finish_far()

    pair = pl.BlockSpec((2, tile, LANE), lambda hp, i: (hp, i, 0))
    full = pl.BlockSpec((2, seq, LANE), lambda hp, i: (hp, 0, 0))
    return pl.pallas_call(
        body, name="attention_forward", grid=(N_HEADS // 2, nb),
        in_specs=[pair, full, full, ANY],
        out_specs=[pl.BlockSpec((tile, LANE), lambda hp, i: (i, hp)),
                   pl.BlockSpec((2, LANE, tile), lambda hp, i: (hp, 0, i)), ANY],
        out_shape=[jax.ShapeDtypeStruct((seq, D_ATT), F32),
                   jax.ShapeDtypeStruct((N_HEADS, LANE, seq), BF16),
                   jax.ShapeDtypeStruct((N_DEV,) + w_out.shape, w_out.dtype)],
        scratch_shapes=[pltpu.VMEM((2, tile, tile), F32), pltpu.VMEM((2, tile, tile), F32),
                        pltpu.VMEM((2, 8, tile), F32), pltpu.VMEM((2, LANE, tile), F32),
                        pltpu.SemaphoreType.DMA((N_GATHER_SEMS,)), pltpu.SemaphoreType.DMA((N_GATHER_SEMS,)),
                        pltpu.SemaphoreType.DMA((1,))],
        compiler_params=_params(("arbitrary", "arbitrary")),
    )(qp, kp, vp, w_out)


def _window_sum(x, halo, window, transposed):
    tile = x.shape[0]

    def split_cat(a):
        hi = a.astype(BF16)
        return jnp.concatenate([hi, (a - hi.astype(F32)).astype(BF16)], axis=1)

    def fold(r):
        return r[:, :LANE] + r[:, LANE:]

    r = lax.broadcasted_iota(jnp.int32, (tile, tile), 0)
    c = lax.broadcasted_iota(jnp.int32, (tile, tile), 1)
    rh = lax.broadcasted_iota(jnp.int32, (HALO, HALO), 0)
    ch = lax.broadcasted_iota(jnp.int32, (HALO, HALO), 1)
    if not transposed:
        band = (c <= r) & (r - c < window)
        edge = (rh + HALO - ch) < window
    else:
        band = (r <= c) & (c - r < window)
        edge = (HALO + ch - rh) < window
    out = fold(_dot(band.astype(BF16), split_cat(x)))
    reach = fold(_dot(edge.astype(BF16), split_cat(halo)))
    if not transposed:
        return jnp.concatenate([out[:HALO] + reach, out[HALO:]], axis=0)
    return jnp.concatenate([out[:tile - HALO], out[tile - HALO:] + reach], axis=0)


def _silu_parts(g):
    sig = _sigmoid(g)
    return g * sig, sig * (1.0 + g * (1.0 - sig))


def _middle(x, tgt, att, g_att, g_pool, p, vecs, pool_vecs, w_out, w_pool, tile):
    seq = x.shape[0]
    nt = seq // tile
    halo_blocks = tile // HALO

    def body(x_ref, tgt_ref, att_ref, ga_ref, gp_ref, p_ref, ph_ref, vec_ref, pvec_ref, wo_ref, wp_ref,
             dxa_ref, do2_ref, dga_ref, dgp_ref, dpooled_ref, gwo_ref, dwp_ref, dvec_ref, dwo_ref, dpvec_ref):
        i = pl.program_id(0)

        @pl.when(i == 0)
        def _():
            dwo_ref[...] = jnp.zeros_like(dwo_ref)
            dwp_ref[...] = jnp.zeros_like(dwp_ref)
            dvec_ref[...] = jnp.zeros_like(dvec_ref)
            dpvec_ref[...] = jnp.zeros_like(dpvec_ref)

        gate, b_out, ln_g, ln_b = (vec_ref[k:k + 1, :] for k in range(4))
        b_pool, pool_scale = pvec_ref[0:1, :], pvec_ref[1:2, :]
        x = x_ref[...]
        p = p_ref[...]
        p_halo = ph_ref[...] * jnp.where(i > 0, 1.0, 0.0)
        pos = i * tile + lax.broadcasted_iota(jnp.int32, (tile, 1), 0) + 1

        pooled, mixed = [], []
        for g, window in enumerate(POOL_WINDOWS):
            cols = slice(g * GROUP_DIM, (g + 1) * GROUP_DIM)
            wsum = _window_sum(p[:, cols], p_halo[:, cols], window, False)
            count = jnp.minimum(pos, window).astype(F32)
            pooled.append(wsum / count - p[:, cols])
            mixed.append(_dot(pooled[g].astype(BF16), wp_ref[g]) + b_pool[:, cols])
        mixed = jnp.concatenate(mixed, axis=1)
        pool = mixed * pool_scale

        att = att_ref[...]
        g_att, g_pool = ga_ref[...], gp_ref[...]
        silu_a, dsilu_a = _silu_parts(g_att)
        silu_p, dsilu_p = _silu_parts(g_pool)
        y_in = jnp.concatenate([att * silu_a, pool * silu_p], axis=1)
        y = _dot(y_in.astype(BF16), wo_ref[...]) + b_out
        h = ALPHA * x + gate * y
        mu = jnp.mean(h, axis=1, keepdims=True)
        hc = h - mu
        var = jnp.mean(hc * hc, axis=1, keepdims=True)
        rstd = lax.rsqrt(var + LN_EPS)
        yhat = hc * rstd
        diff = yhat * ln_g + ln_b - tgt_ref[...]
        loss_rows = jnp.sum(diff * diff, axis=1, keepdims=True)
        d_out = diff * (1.0 / D)

        d_yhat = d_out * ln_g
        dh = rstd * (d_yhat - jnp.mean(d_yhat, axis=1, keepdims=True)
                     - yhat * jnp.mean(d_yhat * yhat, axis=1, keepdims=True))
        dxa_ref[...] = ALPHA * dh
        dy = dh * gate
        dyb = dy.astype(BF16)
        lane = _lanes((1, D))
        loss_row = jnp.where(lane == 0, (0.5 / D) * jnp.sum(loss_rows, axis=0, keepdims=True), 0.0)
        dvec_ref[5:6, :] += jnp.sum(dh * y, axis=0, keepdims=True)
        dvec_ref[0:1, :] += jnp.sum(dy, axis=0, keepdims=True)
        dvec_ref[1:2, :] += jnp.sum(d_out * yhat, axis=0, keepdims=True)
        dvec_ref[2:3, :] += jnp.sum(d_out, axis=0, keepdims=True)
        dvec_ref[4:5, :] += loss_row

        dwo_ref[...] += _dot(y_in.T.astype(BF16), dyb)
        d_yin = _dot_nt(dyb, wo_ref[...])
        d_a, d_pl = d_yin[:, :D_ATT], d_yin[:, D_ATT:]
        d_att = d_a * silu_a
        d_att_t = d_att.T
        prod_t = (d_att * att).T
        sub = lax.broadcasted_iota(jnp.int32, (HEAD_DIM, tile), 0)
        for h in range(N_HEADS):
            rows = slice(h * HEAD_DIM, (h + 1) * HEAD_DIM)
            delta = jnp.sum(prod_t[rows], axis=0, keepdims=True)
            extra = _place3(sub, 0, [part.astype(F32) for part in _split3(delta)], 0.0)
            do2_ref[h] = jnp.concatenate([d_att_t[rows], extra], axis=0).astype(BF16)
        dga_ref[...] = d_a * att * dsilu_a
        dgp_ref[...] = d_pl * pool * dsilu_p
        d_pool = d_pl * silu_p
        d_mixed = d_pool * pool_scale
        dpvec_ref[0:1, :] += jnp.sum(d_mixed, axis=0, keepdims=True)
        dpvec_ref[1:2, :] += jnp.sum(d_pool * mixed, axis=0, keepdims=True)
        d_pooled = []
        for g in range(len(POOL_WINDOWS)):
            cols = slice(g * GROUP_DIM, (g + 1) * GROUP_DIM)
            dmb = d_mixed[:, cols].astype(BF16)
            dwp_ref[g] += _dot(pooled[g].T.astype(BF16), dmb)
            d_pooled.append(_dot_nt(dmb, wp_ref[g]))
        dpooled_ref[...] = jnp.concatenate(d_pooled, axis=1)

        @pl.when(i == nt - 1)
        def _():
            gwo_ref[...] = dwo_ref[...].astype(BF16)
            dvec_ref[3:4, :] = jnp.concatenate([dpvec_ref[0:1, :], dpvec_ref[1:2, :]], axis=1)

    tok = lambda width: pl.BlockSpec((tile, width), lambda i: (i, 0))
    whole = lambda a: pl.BlockSpec(a.shape, lambda i: (0,) * a.ndim)
    halo = pl.BlockSpec((HALO, D_POOL), lambda i: (jnp.maximum(i * halo_blocks - 1, 0), 0))
    half = jax.ShapeDtypeStruct((seq, D_ATT), F32)
    outs = [jax.ShapeDtypeStruct((seq, D), F32), jax.ShapeDtypeStruct((N_HEADS, LANE, seq), BF16), half, half, half,
            jax.ShapeDtypeStruct(w_out.shape, BF16), jax.ShapeDtypeStruct(w_pool.shape, F32),
            jax.ShapeDtypeStruct(vecs.shape, F32)]
    return pl.pallas_call(
        body, name="middle", grid=(nt,),
        in_specs=[tok(D), tok(D), tok(D_ATT), tok(D_ATT), tok(D_POOL), tok(D_POOL), halo,
                  whole(vecs), whole(pool_vecs), whole(w_out), whole(w_pool)],
        out_specs=[tok(D), pl.BlockSpec((N_HEADS, LANE, tile), lambda i: (0, 0, i)),
                   tok(D_ATT), tok(D_POOL), tok(D_POOL),
                   whole(w_out), whole(w_pool), whole(vecs)],
        out_shape=outs,
        scratch_shapes=[pltpu.VMEM(w_out.shape, F32), pltpu.VMEM(pool_vecs.shape, F32)],
        compiler_params=_params(("arbitrary",)),
    )(x, tgt, att, g_att, g_pool, p, p, vecs, pool_vecs, w_out, w_pool)


def _attention_backward(q2t, kp, vp, do2t, gw_out, vecs, pool, tile):
    seq = kp.shape[1]
    nb = seq // tile
    last = N_HEADS // 2 - 1

    def body(qt_ref, k_ref, v_ref, dot_ref, gwo_hbm, vecs_hbm, pool_hbm,
             dq_ref, dk_ref, dv_ref, dcum_ref, g_out_ref, vecs_sum_ref, pool_sum_ref,
             dq_acc, dk_acc, dv_acc, gwo_ref, vecs_ref, pool_ref,
             r1, s2, r2, v_sib, v_chip, v_recv, p_sib, p_chip, p_recv, send_sems, recv_sems):
        hp = pl.program_id(0)
        start, middle, fold, finish, _ = _reduce_stages(
            [gwo_ref], [g_out_ref], [r1], [s2], [r2],
            [(vecs_ref, vecs_sum_ref, v_sib, v_chip, v_recv), (pool_ref, pool_sum_ref, p_sib, p_chip, p_recv)],
            send_sems, recv_sems)

        @pl.when(hp == 0)
        def _():
            pltpu.sync_copy(gwo_hbm, gwo_ref)
            pltpu.sync_copy(vecs_hbm, vecs_ref)
            pltpu.sync_copy(pool_hbm, pool_ref)
            start()

        pl.when(hp == 1)(middle)
        pl.when(hp == 2)(fold)

        row = lax.broadcasted_iota(jnp.int32, (tile, tile), 0)
        col = lax.broadcasted_iota(jnp.int32, (tile, tile), 1)
        dq_acc[...] = jnp.zeros_like(dq_acc)

        def kv_block(kb, _):
            krows = pl.ds(pl.multiple_of(kb * tile, tile), tile)
            k = [k_ref[hh, krows, :] for hh in range(2)]
            v = [v_ref[hh, krows, :] for hh in range(2)]
            k_t = [k[hh].T for hh in range(2)]

            def q_block(qb, masked):
                qcols = pl.ds(pl.multiple_of(qb * tile, tile), tile)
                for hh in range(2):
                    q_t = qt_ref[hh, :, qcols]
                    do_t = dot_ref[hh, :, qcols]
                    s_t = _dot(k[hh], q_t)
                    if masked:
                        s_t = jnp.where(row <= col, s_t, -1e30)
                    p_t = jnp.exp(s_t)
                    ds_t = (p_t * _dot(v[hh], do_t)).astype(BF16)
                    dv_new = _dot_nt(do_t, p_t.astype(BF16))
                    dk_new = _dot_nt(q_t, ds_t)
                    if masked:
                        dv_acc[hh], dk_acc[hh] = dv_new, dk_new
                    else:
                        dv_acc[hh] += dv_new
                        dk_acc[hh] += dk_new
                    dq_acc[hh, :, qcols] += _dot(k_t[hh], ds_t)

            q_block(kb, True)

            def two_later_blocks(j, _):
                q_block(kb + 1 + 2 * j, False)
                q_block(kb + 2 + 2 * j, False)
                return 0

            later = nb - 1 - kb
            lax.fori_loop(0, later // 2, two_later_blocks, 0)
            pl.when(later % 2 == 1)(lambda: q_block(nb - 1, False))
            for hh in range(2):
                dk = dk_acc[hh]
                dk_ref[hh, :, krows] = dk.astype(BF16)
                dv_ref[hh, :, krows] = dv_acc[hh].astype(BF16)
                dcum_ref[hh, :, krows] = -dk[L_CK:L_CK + 1, :]
            return 0

        lax.fori_loop(0, nb, kv_block, 0)
        for hh in range(2):
            dq = dq_acc[hh]
            dcum_ref[hh] += dq[L_CQ:L_CQ + 1, :]
            dq_ref[hh] = (dq * Q_SCALE).astype(BF16)
        pl.when(hp == last)(finish)

    pair = pl.BlockSpec((2, seq, LANE), lambda hp: (hp, 0, 0))
    pair_t = pl.BlockSpec((2, LANE, seq), lambda hp: (hp, 0, 0))
    whole = lambda shape: pl.BlockSpec(shape, lambda hp: (0,) * len(shape))
    grad = jax.ShapeDtypeStruct((N_HEADS, LANE, seq), BF16)
    return pl.pallas_call(
        body, name="attention_backward", grid=(N_HEADS // 2,),
        in_specs=[pair_t, pair, pair, pair_t, ANY, ANY, ANY],
        out_specs=[pair_t, pair_t, pair_t, pl.BlockSpec((2, 1, seq), lambda hp: (hp, 0, 0)),
                   whole(gw_out.shape[1:]), whole(vecs.shape), whole(pool.shape)],
        out_shape=[grad, grad, grad, jax.ShapeDtypeStruct((N_HEADS, 1, seq), F32),
                   jax.ShapeDtypeStruct(gw_out.shape[1:], F32), jax.ShapeDtypeStruct(vecs.shape, F32),
                   jax.ShapeDtypeStruct(pool.shape, F32)],
        scratch_shapes=[pltpu.VMEM((2, LANE, seq), F32), pltpu.VMEM((2, LANE, tile), F32),
                        pltpu.VMEM((2, LANE, tile), F32), pltpu.VMEM(gw_out.shape, BF16),
                        pltpu.VMEM(vecs.shape, F32), pltpu.VMEM(pool.shape, F32)]
        + _reduce_scratch(gw_out, [vecs, pool]),
        compiler_params=_params(("arbitrary",)),
    )(q2t, kp, vp, do2t, gw_out, vecs, pool)


def _inproj_backward(dqp, dkp, dvp, d_cum, f, d_pooled, d_ga, d_gp, x, dxa, u, mod, w_main, w_f, tile):
    seq = x.shape[0]
    nt = seq // tile
    halo_blocks = tile // HALO

    def body(dq_ref, dk_ref, dv_ref, dcum_ref, f_ref, dpo_ref, dph_ref, dga_ref, dgp_ref, x_ref, dxa_ref, u_ref,
             mod_ref, w_ref, wf_ref,
             dx_ref, dproj_ref, dwf_ref, db_ref, dbf_ref, dmod_ref, carry_ref):
        step = pl.program_id(0)
        i = nt - 1 - step

        @pl.when(step == 0)
        def _():
            carry_ref[...] = jnp.zeros_like(carry_ref)
            dwf_ref[...] = jnp.zeros_like(dwf_ref)
            db_ref[...] = jnp.zeros_like(db_ref)
            dbf_ref[...] = jnp.zeros_like(dbf_ref)
            dmod_ref[...] = jnp.zeros_like(dmod_ref)

        ones = jnp.ones((8, tile), BF16)

        def emit(chunk, val):
            cols = pl.ds(chunk * COL_CHUNK, COL_CHUNK)
            db_ref[0:1, cols] += jnp.sum(val, axis=0, keepdims=True)
            vb = val.astype(BF16)
            dproj_ref[:, pl.ds((chunk - 3) * COL_CHUNK, COL_CHUNK)] = vb
            return _dot(vb, w_ref[cols, :])

        d_u = jnp.zeros((tile, D), F32)
        for chunk, ref in enumerate((dq_ref, dk_ref, dv_ref)):
            cols = pl.ds(chunk * COL_CHUNK, COL_CHUNK)
            val_t = ref[:, 0:HEAD_DIM, :].reshape(COL_CHUNK, tile)
            db_ref[:, cols] += _dot_nt(ones, val_t)
            d_u += _dot_tn(val_t, w_ref[cols, :])

        d_pooled = dpo_ref[...]
        d_halo = dph_ref[...] * jnp.where(i < nt - 1, 1.0, 0.0)
        pos = i * tile + lax.broadcasted_iota(jnp.int32, (tile, 1), 0) + 1
        d_p = []
        for g, window in enumerate(POOL_WINDOWS):
            cols = slice(g * GROUP_DIM, (g + 1) * GROUP_DIM)
            scaled = d_pooled[:, cols] / jnp.minimum(pos, window).astype(F32)
            d_p.append(_window_sum(scaled, d_halo[:, cols] * (1.0 / window), window, True) - d_pooled[:, cols])
        d_u += emit(3, jnp.concatenate(d_p, axis=1))
        d_u += emit(4, dga_ref[...])
        d_u += emit(5, dgp_ref[...])

        row = lax.broadcasted_iota(jnp.int32, (tile, tile), 0)
        col = lax.broadcasted_iota(jnp.int32, (tile, tile), 1)
        later = (row >= col).astype(BF16)
        d_logf = sum(_dot(part, later) for part in _split3(dcum_ref[:, 0, :])) + carry_ref[:, 0:1]
        carry_ref[:, 0:1] = d_logf[:, 0:1]
        d_f = d_logf * _sigmoid(-f_ref[...].T[0:N_HEADS, :])
        d_f = jnp.concatenate([d_f, jnp.zeros((LANE - N_HEADS, tile), F32)], axis=0)
        dbf_ref[...] += sum(_dot_nt(ones, part) for part in _split3(d_f))
        d_fb = d_f.astype(BF16)
        d_u += _dot_tn(d_fb, wf_ref[...])
        dwf_ref[...] += _dot(d_fb, u_ref[...])

        x = x_ref[...]
        dx_ref[...] = dxa_ref[...] + d_u * mod_ref[0:1, :]
        dmod_ref[0:1, :] += jnp.sum(d_u * x, axis=0, keepdims=True)
        dmod_ref[1:2, :] += jnp.sum(d_u, axis=0, keepdims=True)

    rev = lambda step: nt - 1 - step
    tok = lambda width: pl.BlockSpec((tile, width), lambda s: (rev(s), 0))
    head_block = pl.BlockSpec((N_HEADS, LANE, tile), lambda s: (0, 0, rev(s)))
    whole = lambda a: pl.BlockSpec(a.shape, lambda s: (0,) * a.ndim)
    halo = pl.BlockSpec((HALO, D_POOL), lambda s: (jnp.minimum((rev(s) + 1) * halo_blocks, seq // HALO - 1), 0))
    small = lambda width: jax.ShapeDtypeStruct((8, width), F32)
    n_rest = N_MAIN - OFF_P
    return pl.pallas_call(
        body, name="inproj_backward", grid=(nt,),
        in_specs=[head_block, head_block, head_block, pl.BlockSpec((N_HEADS, 1, tile), lambda s: (0, 0, rev(s))),
                  tok(LANE), tok(D_POOL), halo, tok(D_ATT), tok(D_POOL),
                  tok(D), tok(D), tok(D),
                  whole(mod), whole(w_main), whole(w_f)],
        out_specs=[tok(D), tok(n_rest), pl.BlockSpec((LANE, D), lambda s: (0, 0)),
                   pl.BlockSpec((8, N_MAIN), lambda s: (0, 0)), pl.BlockSpec((8, LANE), lambda s: (0, 0)),
                   pl.BlockSpec((8, D), lambda s: (0, 0))],
        out_shape=[jax.ShapeDtypeStruct((seq, D), F32), jax.ShapeDtypeStruct((seq, n_rest), BF16),
                   jax.ShapeDtypeStruct((LANE, D), F32), small(N_MAIN), small(LANE), small(D)],
        scratch_shapes=[pltpu.VMEM((8, LANE), F32)],
        compiler_params=_params(("arbitrary",)),
    )(dqp, dkp, dvp, d_cum, f, d_pooled, d_pooled, d_ga, d_gp, x, dxa, u, mod, w_main, w_f)


def _weight_grads(dq_t, dk_t, dv_t, dw_f, dproj, u, k_tile):
    seq = u.shape[0]
    nk = seq // k_tile
    rows = N_HEADS * HEAD_DIM

    def body(dq_ref, dk_ref, dv_ref, dwf_ref, dp_ref, u_ref, out_ref, acc_ref):
        k = pl.program_id(0)

        @pl.when(k == 0)
        def _():
            acc_ref[...] = jnp.zeros_like(acc_ref)

        tokens = u_ref[...]
        for j, ref in enumerate((dq_ref, dk_ref, dv_ref)):
            acc_ref[pl.ds(j * rows, rows), :] += _dot(ref[...].reshape(rows, k_tile), tokens)
        for j in range(dproj.shape[1] // COL_CHUNK):
            cols = pl.ds(j * COL_CHUNK, COL_CHUNK)
            acc_ref[pl.ds(F_HI + j * COL_CHUNK, COL_CHUNK), :] += _dot_tn(dp_ref[:, cols], tokens)

        @pl.when(k == nk - 1)
        def _():
            acc_ref[F_LO:F_HI, :] = dwf_ref[0:N_HEADS, :]
            for slot in range(N_DEV):
                out_ref[slot] = acc_ref[slot * shard:(slot + 1) * shard, :].astype(BF16)

    shard = D_IN // N_DEV
    heads = pl.BlockSpec((N_HEADS, HEAD_DIM, k_tile), lambda k: (0, 0, k))
    return pl.pallas_call(
        body, name="weight_grads", grid=(nk,),
        in_specs=[heads, heads, heads, pl.BlockSpec(dw_f.shape, lambda k: (0, 0)),
                  pl.BlockSpec((k_tile, dproj.shape[1]), lambda k: (k, 0)), pl.BlockSpec((k_tile, D), lambda k: (k, 0))],
        out_specs=pl.BlockSpec((N_DEV, shard, D), lambda k: (0, 0, 0)),
        out_shape=jax.ShapeDtypeStruct((N_DEV, shard, D), BF16),
        scratch_shapes=[pltpu.VMEM((D_IN, D), F32)],
        compiler_params=_params(("arbitrary",)),
    )(dq_t, dk_t, dv_t, dw_f, dproj, u)


def _adamw(w, g, m, v):
    m = ADAM_B1 * m + (1.0 - ADAM_B1) * g
    v = ADAM_B2 * v + (1.0 - ADAM_B2) * (g * g)
    m_hat = m / (1.0 - ADAM_B1 ** ADAM_STEP)
    v_hat = v / (1.0 - ADAM_B2 ** ADAM_STEP)
    delta = -ADAM_LR * (m_hat / (jnp.sqrt(v_hat) + ADAM_EPS) + ADAM_WD * w)
    return delta, m, v


SUBLANES = 8


def _adamw_packed(g, w, m, v, name, chunks=4):
    rows, cols = g.shape
    per_row = cols // LANE
    assert cols % LANE == 0 and per_row == SUBLANES and w.shape == (rows * per_row, LANE)
    step = -(-rows // (chunks * SUBLANES)) * SUBLANES
    bounds = [(r0, min(r0 + step, rows)) for r0 in range(0, rows, step)]

    def body(g_hbm, w_hbm, m_hbm, v_hbm, og_hbm, od_hbm, om_hbm, ov_hbm, g_buf, in_buf, out_buf, in_sems, out_sems):
        def copies_in(c):
            r0, r1 = bounds[c]
            packed = slice(r0 * per_row, r1 * per_row)
            return [pltpu.make_async_copy(g_hbm.at[r0:r1], g_buf.at[r0:r1], in_sems.at[c, 0])] + [
                pltpu.make_async_copy(src.at[packed], in_buf.at[i, packed], in_sems.at[c, 1 + i])
                for i, src in enumerate((w_hbm, m_hbm, v_hbm))]

        def copies_out(c):
            r0, r1 = bounds[c]
            packed = slice(r0 * per_row, r1 * per_row)
            return [pltpu.make_async_copy(out_buf.at[i, packed], dst.at[packed], out_sems.at[c, i])
                    for i, dst in enumerate((og_hbm, od_hbm, om_hbm, ov_hbm))]

        for c in range(len(bounds)):
            for cp in copies_in(c):
                cp.start()
        for c, (r0, r1) in enumerate(bounds):
            for cp in copies_in(c):
                cp.wait()
            for j in range(per_row):
                lanes = pl.ds(r0 * per_row + j, r1 - r0, stride=per_row)
                g_part = g_buf[r0:r1, j * LANE:(j + 1) * LANE]
                results = _adamw(in_buf[0, lanes, :], g_part, in_buf[1, lanes, :], in_buf[2, lanes, :])
                for i, val in enumerate((g_part,) + results):
                    out_buf[i, lanes, :] = val
            for cp in copies_out(c):
                cp.start()
        for c in range(len(bounds)):
            for cp in copies_out(c):
                cp.wait()

    shape = jax.ShapeDtypeStruct(w.shape, F32)
    return pl.pallas_call(
        body, name=name,
        in_specs=[ANY] * 4, out_specs=[ANY] * 4, out_shape=[shape] * 4,
        scratch_shapes=[pltpu.VMEM(g.shape, F32), pltpu.VMEM((3,) + w.shape, F32), pltpu.VMEM((4,) + w.shape, F32),
                        pltpu.SemaphoreType.DMA((len(bounds), 4)), pltpu.SemaphoreType.DMA((len(bounds), 4))],
        compiler_params=_params(),
    )(g, w, m, v)


ADA_CHUNKS = 4


def _ada_adamw_scratch(shape):
    return [pltpu.VMEM((3,) + shape, F32), pltpu.VMEM((4,) + shape, F32),
            pltpu.SemaphoreType.DMA((ADA_CHUNKS, 3)), pltpu.SemaphoreType.DMA((ADA_CHUNKS, 4))]


def _ada_adamw_steps(sc_ref, d_ref, ins, outs, in_buf, out_buf, in_sems, out_sems):
    rows = in_buf.shape[1]
    step, sub = rows // ADA_CHUNKS, 32
    assert rows % ADA_CHUNKS == 0 and step % LANE == 0 and step % sub == 0

    def copies_in(c):
        part = slice(c * step, (c + 1) * step)
        return [pltpu.make_async_copy(src.at[part], in_buf.at[i, part], in_sems.at[c, i]) for i, src in enumerate(ins)]

    def copies_out(c):
        part = slice(c * step, (c + 1) * step)
        return [pltpu.make_async_copy(out_buf.at[i, part], dst.at[part], out_sems.at[c, i]) for i, dst in enumerate(outs)]

    def start():
        for c in range(ADA_CHUNKS):
            for cp in copies_in(c):
                cp.start()

    def run():
        for c in range(ADA_CHUNKS):
            sc_t = sc_ref[:, c * step:(c + 1) * step].T
            for cp in copies_in(c):
                cp.wait()
            for r0 in range(0, step, sub):
                part = slice(c * step + r0, c * step + r0 + sub)
                g = sc_t[r0:r0 + sub, 0:1] * d_ref[0:1, :]
                for b in range(1, N_DEV):
                    g = g + sc_t[r0:r0 + sub, b:b + 1] * d_ref[b:b + 1, :]
                results = _adamw(in_buf[0, part, :], g, in_buf[1, part, :], in_buf[2, part, :])
                for i, val in enumerate((g,) + results):
                    out_buf[i, part, :] = val
            for cp in copies_out(c):
                cp.start()

    def finish():
        for c in range(ADA_CHUNKS):
            for cp in copies_out(c):
                cp.wait()

    return start, run, finish


F_LO, F_HI = 3 * D_ATT, 3 * D_ATT + N_HEADS


def _split_forget(a, axis):
    idx = lambda lo, hi: tuple(slice(lo, hi) if d == axis else slice(None) for d in range(a.ndim))
    pad = [(0, LANE - N_HEADS) if d == axis else (0, 0) for d in range(a.ndim)]
    return jnp.concatenate([a[idx(0, F_LO)], a[idx(F_HI, D_IN)]], axis=axis), jnp.pad(a[idx(F_LO, F_HI)], pad)


def _join_forget(main, f, axis):
    idx = lambda lo, hi: tuple(slice(lo, hi) if d == axis else slice(None) for d in range(main.ndim))
    return jnp.concatenate([main[idx(0, F_LO)], f[idx(0, N_HEADS)], main[idx(F_LO, N_MAIN)]], axis=axis)


def _adamw_small(grad_rows, row_params, whole_params, summed_params, scalar_at):
    n_row, n_whole, n_sum = len(row_params), len(whole_params), len(summed_params)
    n = n_row + n_whole + n_sum

    def body(g_ref, *refs):
        n_in = 3 * n_row + 4 * (n_whole + n_sum)
        ins, outs = list(refs[:n_in]), refs[n_in:]
        for i in range(n):
            if i < n_row:
                row, lo, hi = row_params[i][:3]
                g = g_ref[row:row + 1, lo:hi]
            elif i < n_row + n_whole:
                g = ins.pop(0)[...]
            else:
                parts = ins.pop(0)
                g = parts[0]
                for k in range(1, N_DEV):
                    g = g + parts[k]
            w, m, v = (ins.pop(0)[...] for _ in range(3))
            outs[4 * i][...] = g
            outs[4 * i + 1][...], outs[4 * i + 2][...], outs[4 * i + 3][...] = _adamw(w, g, m, v)
        row, lane = scalar_at
        outs[4 * n][...] = g_ref[row:row + 1, lane:lane + 1]

    shapes = [p[3] for p in row_params] + [p[1] for p in whole_params] + [p[1] for p in summed_params]
    operands = [a for p in row_params for a in p[3:]] + [a for p in whole_params + summed_params for a in p]
    flat = pl.pallas_call(
        body, name="adamw_small",
        out_shape=[jax.ShapeDtypeStruct(w.shape, F32) for w in shapes for _ in range(4)]
        + [jax.ShapeDtypeStruct((1, 1), F32)],
        compiler_params=_params(),
    )(grad_rows, *operands)
    return [flat[4 * i:4 * i + 4] for i in range(n)], flat[4 * n].reshape(())


def kernel(x, c, w_ada, b_ada, w_in, b_in, w_pool_mix, b_pool_mix, pool_scale, w_out, b_out, ln_g, ln_b, loss_target, m_w_ada, m_b_ada, m_w_in, m_b_in, m_w_pool_mix, m_b_pool_mix, m_pool_scale, m_w_out, m_b_out, m_ln_g, m_ln_b, v_w_ada, v_b_ada, v_w_in, v_b_in, v_w_pool_mix, v_b_pool_mix, v_pool_scale, v_w_out, v_b_out, v_ln_g, v_ln_b):
    seq = x.shape[1]
    tile = min(256, seq)
    attn_tile = min(512, max(128, seq // 4))
    x2, tgt = x[0], loss_target[0]

    rows_of = lambda a: jnp.swapaxes(a, 1, 2)[0]
    w_main, w_f, sc_all, ada_mine = _gather_and_ada(c, rows_of(w_in).astype(BF16), w_ada[0])
    ada = ada_mine.reshape(1, D_ADA) + b_ada
    shift, scale, gate = ada[:, 0:D], ada[:, D:2 * D], ada[:, 2 * D:]
    mod = jnp.concatenate([1.0 + scale, shift, jnp.zeros((6, D), F32)], axis=0)
    b_main, b_f = _split_forget(b_in, 1)

    qp, kp, vp, f, p, g_att, g_pool, u = _inproj_forward(x2, mod, w_main, w_f, b_main, b_f, tile)
    att, q2t, w_out_g = _attention_forward(qp, kp, vp, w_out[0].astype(BF16), attn_tile)

    vecs = jnp.concatenate([gate, b_out, ln_g, ln_b, jnp.zeros((4, D), F32)], axis=0)
    pool_vecs = jnp.concatenate([b_pool_mix.reshape(1, D_POOL), pool_scale, jnp.zeros((6, D_POOL), F32)], axis=0)
    dxa, do2, d_ga, d_gp, d_pooled, gw_out, dw_pool, dvec = _middle(
        x2, tgt, att, g_att, g_pool, p, vecs, pool_vecs, w_out_g.reshape(D, D), w_pool_mix[0].astype(BF16), tile)

    pool_rows = w_pool_mix.shape[1] * GROUP_DIM
    dqp, dkp, dvp, d_cum, g_out, dvec_sum, dw_pool_sum = _attention_backward(
        q2t, kp, vp, do2, gw_out.reshape(N_DEV, D // N_DEV, D), dvec, dw_pool.reshape(pool_rows, GROUP_DIM), attn_tile)
    dx, dproj, dw_f, db_main, db_f, dmod = _inproj_backward(
        dqp, dkp, dvp, d_cum, f, d_pooled, d_ga, d_gp, x2, dxa, u, mod, w_main, w_f, tile)
    gw_in = _weight_grads(dqp, dkp, dvp, dw_f, dproj, u, min(512, seq))
    d_ada = jnp.concatenate([dmod[1:2], dmod[0:1], dvec[5:6]], axis=1)
    g_in_rows, g_b_in, d_ada_all, (g_w_ada, d_w_ada, nm_w_ada, nv_w_ada) = _reduce_grads(
        gw_in, _join_forget(db_main[0:1], db_f[0:1], 1), d_ada, sc_all, w_ada[0], m_w_ada[0], v_w_ada[0])

    packed = lambda a: jnp.transpose(a.reshape(SUBLANES, LANE, -1), (2, 0, 1)).reshape(-1, LANE)
    outs_in = _adamw_packed(g_in_rows, packed(w_in), packed(m_w_in), packed(v_w_in), "adamw_w_in")
    g_w_in, d_w_in, nm_w_in, nv_w_in = (
        jnp.transpose(a.reshape(-1, SUBLANES, LANE), (1, 2, 0)).reshape(D, -1) for a in outs_in)
    flat_pool = lambda a: a.reshape(1, D_POOL)
    pool_2d = lambda a: a.reshape(pool_rows, GROUP_DIM)
    rows, loss = _adamw_small(
        dvec_sum,
        [(0, 0, D, b_out, m_b_out, v_b_out), (1, 0, D, ln_g, m_ln_g, v_ln_g), (2, 0, D, ln_b, m_ln_b, v_ln_b),
         (3, 0, D_POOL, flat_pool(b_pool_mix), flat_pool(m_b_pool_mix), flat_pool(v_b_pool_mix)),
         (3, D_POOL, 2 * D_POOL, pool_scale, m_pool_scale, v_pool_scale)],
        [(g_out, w_out[0], m_w_out[0], v_w_out[0]),
         (dw_pool_sum, pool_2d(w_pool_mix), pool_2d(m_w_pool_mix), pool_2d(v_w_pool_mix)),
         (g_b_in, b_in, m_b_in, v_b_in)],
        [(d_ada_all, b_ada, m_b_ada, v_b_ada)],
        scalar_at=(4, 0))
    small = {"b_out": rows[0], "ln_g": rows[1], "ln_b": rows[2],
             "b_pool": [a.reshape(b_pool_mix.shape) for a in rows[3]], "pool_scale": rows[4],
             "w_pool": [a.reshape(w_pool_mix.shape) for a in rows[6]], "b_in": rows[7]}
    g_s, d_s, nm_s, nv_s = ({k: r[j] for k, r in small.items()} for j in range(4))
    g_w_out, d_w_out, nm_w_out, nv_w_out = rows[5]
    g_b_ada, d_b_ada, nm_b_ada, nv_b_ada = rows[8]


    def ordered(w_ada_, b_ada_, w_in_, w_out_, s):
        return (w_ada_[None], b_ada_, w_in_[None], s["b_in"], s["w_pool"], s["b_pool"], s["pool_scale"],
                w_out_[None], s["b_out"], s["ln_g"], s["ln_b"])

    return (loss, dx[None],
            *ordered(g_w_ada, g_b_ada, g_w_in, g_w_out, g_s),
            *ordered(d_w_ada, d_b_ada, d_w_in, d_w_out, d_s),
            *ordered(nm_w_ada, nm_b_ada, nm_w_in, nm_w_out, nm_s),
            *ordered(nv_w_ada, nv_b_ada, nv_w_in, nv_w_out, nv_s))
```

```python
import jax
import jax.numpy as jnp
from jax import lax
from jax.experimental import pallas as pl
from jax.experimental.pallas import tpu as pltpu

F32 = jnp.float32
BF16 = jnp.bfloat16

N_DEV = 8
D = 1024
N_HEADS = 8
HEAD_DIM = 64
D_ATT = 512
D_POOL = 512
POOL_WINDOWS = (2, 4, 8, 16)
GROUP_DIM = 128
HALO = 16
LANE = 128
BF16_TILE_ROWS = 16
D_IN = 3080
D_ADA = 3072
N_MAIN = 3072
OFF_P = 1536
COL_CHUNK = 512
Q_SCALE = 0.125
LN_EPS = 1e-5
ALPHA = 2.0 ** 0.25
L_CQ, L_CK, L_LSE = 64, 67, 70

ADAM_LR, ADAM_B1, ADAM_B2, ADAM_EPS, ADAM_WD, ADAM_STEP = 0.001, 0.9, 0.999, 1e-08, 0.01, 10
VMEM_LIMIT = 56 * 1024 * 1024

MESH = pl.DeviceIdType.MESH
ANY = pl.BlockSpec(memory_space=pl.ANY)


def _params(sem=None, vmem=VMEM_LIMIT):
    return pltpu.CompilerParams(dimension_semantics=sem, vmem_limit_bytes=vmem)


def _split3(a):
    hi = a.astype(BF16)
    r = a - hi.astype(F32)
    mid = r.astype(BF16)
    lo = (r - mid.astype(F32)).astype(BF16)
    return hi, mid, lo


def _dot(a, b):
    return jnp.dot(a, b, preferred_element_type=F32)


def _dot_nt(a, b):
    return lax.dot_general(a, b, (((1,), (1,)), ((), ())), preferred_element_type=F32)


def _dot_tn(a, b):
    return lax.dot_general(a, b, (((0,), (0,)), ((), ())), preferred_element_type=F32)


def _dot3(m01, a):
    hi, mid, lo = _split3(a)
    return _dot(m01, hi) + _dot(m01, mid) + _dot(m01, lo)


def _sigmoid(z):
    return 1.0 / (1.0 + jnp.exp(-z))


def _lanes(shape):
    return lax.broadcasted_iota(jnp.int32, shape, len(shape) - 1)


def _place3(lane, base, parts, other):
    out = other
    for j in range(3):
        out = jnp.where(lane == base + j, parts[j], out)
    return out


def _mesh_pos():
    return lax.axis_index("x"), lax.axis_index("y"), lax.axis_index("c")


def _dev_index(px, py, pc):
    return 4 * px + 2 * py + pc


N_GATHER_SEMS = 11


def _gather_stages(src_ref, out_ref, send_sems, recv_sems, local_sem):
    x, y, c = _mesh_pos()
    me, sibling = (x, y, c), (x, y, 1 - c)
    nbr_x, nbr_y, diag = (1 - x, y), (x, 1 - y), (1 - x, 1 - y)
    half = out_ref.shape[-1] // 2
    left, right = pl.ds(0, half), pl.ds(half, half)

    def copy(k, block, to, cols=None, src=None):
        slot = out_ref.at[_dev_index(*block)]
        if cols is not None:
            slot = slot.at[:, cols]
            src = src if src is None else src.at[:, cols]
        return pltpu.make_async_remote_copy(
            src_ref=slot if src is None else src, dst_ref=slot, send_sem=send_sems.at[k], recv_sem=recv_sems.at[k],
            device_id=to, device_id_type=MESH)

    mine = pltpu.make_async_copy(src_ref, out_ref.at[_dev_index(*me)], local_sem)
    first = [copy(0, me, sibling, src=src_ref),
             copy(1, me, (*nbr_x, c), cols=left, src=src_ref), copy(2, me, (*nbr_y, c), cols=right, src=src_ref),
             copy(9, me, (*nbr_x, c), cols=right, src=src_ref), copy(10, me, (*nbr_y, c), cols=left, src=src_ref)]
    relay = [(1, nbr_x, left, nbr_x), (2, nbr_y, right, nbr_y), (3, diag, left, nbr_y), (4, diag, right, nbr_x)]
    other_half = [(9, nbr_x, right, nbr_x), (10, nbr_y, left, nbr_y)]
    onward = [copy(3, (*nbr_x, c), (*nbr_y, c), cols=left), copy(4, (*nbr_y, c), (*nbr_x, c), cols=right)]
    passed = [copy(4 + k, (*block, c), sibling, cols=None if k < 3 else cols) for k, block, cols, _ in relay]

    def start():
        mine.start()
        for cp in first:
            cp.start()

    def arrived(item):
        k, block, cols, frm = item
        copy(k, (*block, c), (*frm, c), cols=cols).wait_recv()

    def relay_near():
        for j in (0, 1):
            arrived(relay[j])
            onward[j].start()
        for j in (0, 1):
            arrived(other_half[j])
            passed[j].start()

    def relay_far():
        for j in (2, 3):
            arrived(relay[j])
            passed[j].start()

    def from_sibling(items):
        for k, block, cols, _ in items:
            copy(4 + k, (*block, 1 - c), me, cols=None if k < 3 else cols).wait_recv()

    def finish_near():
        copy(0, sibling, me).wait_recv()
        from_sibling(relay[:2])
        mine.wait()

    def finish_far():
        from_sibling(relay[2:])
        for cp in first + onward + passed:
            cp.wait_send()

    return start, relay_near, relay_far, finish_near, finish_far


N_REDUCE_SEMS = 10
N_SMALL_SEMS = 4
N_ROWS_SEMS = 7


def _reduce_stages(ins, gs, r1, s2, r2, smalls, send_sems, recv_sems, rows=None, own=None):
    n = len(ins)
    x, y, c = _mesh_pos()
    me = _dev_index(x, y, c)
    sibling = (x, y, 1 - c)
    chips = [(x, y), (1 - x, y), (x, 1 - y), (1 - x, 1 - y)]
    peers = []
    for p in range(1, N_DEV):
        px, py, pc = (p >> 2) & 1, (p >> 1) & 1, p & 1
        peers.append((1 - x if px else x, 1 - y if py else y, 1 - c if pc else c))
    base_small = N_REDUCE_SEMS * n

    def remote(src, dst, k, to):
        return pltpu.make_async_remote_copy(src_ref=src, dst_ref=dst, send_sem=send_sems.at[k],
                                            recv_sem=recv_sems.at[k], device_id=to, device_id_type=MESH)

    def level1(a, q):
        return remote(ins[a].at[_dev_index(*chips[q], 1 - c)], r1[a].at[q], N_REDUCE_SEMS * a + q, sibling)

    def level2(a, k):
        half = ins[a].shape[-1] // 2
        left, right = pl.ds(0, half), pl.ds(half, half)
        nbr_x, nbr_y = (*chips[1], c), (*chips[2], c)
        src_slot, dst_slot, cols, to = [(0, 0, left, nbr_x), (1, 1, right, nbr_y), (2, 2, left, nbr_x),
                                        (2, 2, right, nbr_y), (0, 0, right, nbr_x), (1, 1, left, nbr_y)][k]
        return remote(s2[a].at[src_slot, :, cols], r2[a].at[dst_slot, :, cols], N_REDUCE_SEMS * a + 4 + k, to)

    to_sibling = [remote(sm[0], sm[2], base_small + 4 * i, sibling) for i, sm in enumerate(smalls)]
    to_chips = [[remote(sm[3], sm[4].at[j], base_small + 4 * i + 1 + j, (*chips[j + 1], c)) for j in range(3)]
                for i, sm in enumerate(smalls)]
    if rows is not None:
        rows_ref, land_ref, all_ref = rows
        base_rows = base_small + 4 * len(smalls)
        row_sends = [remote(rows_ref, land_ref.at[me], base_rows + k, to) for k, to in enumerate(peers)]

    order = (3, 1, 2, 0)

    def mine(a, q):
        buf, sems = own[a]
        return pltpu.make_async_copy(ins[a].at[_dev_index(*chips[q], c)], buf.at[q], sems.at[q])

    def start():
        for a in range(n):
            for q in order:
                level1(a, q).start()
            if own is not None:
                for q in order:
                    mine(a, q).start()
        for cp in to_sibling:
            cp.start()
        if rows is not None:
            for cp in row_sends:
                cp.start()
            land_ref[me] = rows_ref[...]

    def middle():
        for a in range(n):
            for q in order:
                level1(a, q).wait_recv()
                if own is None:
                    kept = ins[a][_dev_index(*chips[q], c)]
                else:
                    mine(a, q).wait()
                    kept = own[a][0][q]
                pair = kept.astype(F32) + r1[a][q].astype(F32)
                if q == 0:
                    gs[a][...] = pair
                else:
                    s2[a][q - 1] = pair.astype(BF16)
                    for k in ((0,), (1,), (2, 3))[q - 1]:
                        level2(a, k).start()
        for i, (small_ref, _, sm_sib, sm_chip, _) in enumerate(smalls):
            to_sibling[i].wait_recv()
            sm_chip[...] = small_ref[...] + sm_sib[...]
            for cp in to_chips[i]:
                cp.start()

    def fold():
        for a in range(n):
            half = ins[a].shape[-1] // 2
            level2(a, 3).wait_recv()
            s2[a][0, :, half:] = (s2[a][0, :, half:].astype(F32) + r2[a][2, :, half:].astype(F32)).astype(BF16)
            level2(a, 4).start()
            level2(a, 2).wait_recv()
            s2[a][1, :, :half] = (s2[a][1, :, :half].astype(F32) + r2[a][2, :, :half].astype(F32)).astype(BF16)
            level2(a, 5).start()

    def finish():
        for a in range(n):
            for k in (0, 1, 4, 5):
                level2(a, k).wait_recv()
            gs[a][...] = gs[a][...] + r2[a][0].astype(F32) + r2[a][1].astype(F32)
            for q in range(4):
                level1(a, q).wait_send()
            for k in range(6):
                level2(a, k).wait_send()
        for i, (_, total_ref, _, sm_chip, sm_recv) in enumerate(smalls):
            for cp in to_chips[i]:
                cp.wait_recv()
            total = None
            for ax in range(2):
                for ay in range(2):
                    dx, dy = x != ax, y != ay
                    term = jnp.where(dx, jnp.where(dy, sm_recv[2], sm_recv[0]), jnp.where(dy, sm_recv[1], sm_chip[...]))
                    total = term if total is None else total + term
            total_ref[...] = total
            for cp in [to_sibling[i]] + to_chips[i]:
                cp.wait_send()
        if rows is not None:
            for k, frm in enumerate(peers):
                remote(rows_ref, land_ref.at[_dev_index(*frm)], base_rows + k, frm).wait_recv()
            all_ref[...] = land_ref[...]
            for cp in row_sends:
                cp.wait_send()

    return start, middle, fold, finish


def _reduce_scratch(shard, smalls, rows=None):
    out = [pltpu.VMEM((lead,) + shard.shape[1:], BF16) for lead in (4, 3, 3)]
    for small in smalls:
        out += [pltpu.VMEM(small.shape, F32), pltpu.VMEM(small.shape, F32), pltpu.VMEM((3,) + small.shape, F32)]
    n_sems = N_REDUCE_SEMS + N_SMALL_SEMS * len(smalls)
    if rows is not None:
        out.append(pltpu.VMEM((N_DEV,) + rows.shape, F32))
        n_sems += N_ROWS_SEMS
    return out + [pltpu.SemaphoreType.DMA((n_sems,))] * 2


def _reduce_grads(gw_in, small, rows):
    def body(in_ref, small_ref, rows_ref, g_ref, total_ref, rows_all_ref,
             r1, s2, r2, sm_sib, sm_chip, sm_recv, rows_land, send_sems, recv_sems, kept, kept_sems):
        stages = _reduce_stages(
            [in_ref], [g_ref], [r1], [s2], [r2], [(small_ref, total_ref, sm_sib, sm_chip, sm_recv)],
            send_sems, recv_sems, rows=(rows_ref, rows_land, rows_all_ref), own=[(kept, kept_sems)])
        for stage in stages:
            stage()

    vmem = pl.BlockSpec(memory_space=pltpu.VMEM)
    return pl.pallas_call(
        body, name="reduce_grads",
        in_specs=[ANY, vmem, vmem], out_specs=[vmem, vmem, vmem],
        out_shape=[jax.ShapeDtypeStruct(gw_in.shape[1:], F32), jax.ShapeDtypeStruct(small.shape, F32),
                   jax.ShapeDtypeStruct((N_DEV,) + rows.shape, F32)],
        scratch_shapes=_reduce_scratch(gw_in, [small], rows)
        + [pltpu.VMEM((4,) + gw_in.shape[1:], BF16), pltpu.SemaphoreType.DMA((4,))],
        compiler_params=_params(),
    )(gw_in, small, rows)


def _dot3_rhs(a, b):
    a0, a1, a2 = _split3(a)
    b0, b1, b2 = _split3(b)
    return (_dot(a0, b0) + (_dot(a0, b1) + _dot(a1, b0))
            + (_dot(a0, b2) + _dot(a1, b1) + _dot(a2, b0)))


def _gather_and_ada(c, w_in_rows, w_ada):
    cols = w_ada.shape[1]
    shard = w_in_rows.shape[0]

    def body(c_ref, w_ref, wa_ref, w_main_hbm, w_f_ref, sc_ref, ada_ref,
             w_all_ref, w_f32, wm_buf, c_land, part, ada_land, send_sems, recv_sems, local_sem, x_send, x_recv, out_sems):
        x, y, cc = _mesh_pos()
        me = _dev_index(x, y, cc)
        peers = []
        for p in range(1, N_DEV):
            px, py, pc = (p >> 2) & 1, (p >> 1) & 1, p & 1
            peers.append((1 - x if px else x, 1 - y if py else y, 1 - cc if pc else cc))

        def remote(src, dst, k, to):
            return pltpu.make_async_remote_copy(src_ref=src, dst_ref=dst, send_sem=x_send.at[k], recv_sem=x_recv.at[k],
                                                device_id=to, device_id_type=MESH)

        c_sends = [remote(c_ref, c_land.at[me], k, to) for k, to in enumerate(peers)]
        for cp in c_sends:
            cp.start()
        start, relay_near, relay_far, finish_near, finish_far = _gather_stages(
            w_ref, w_all_ref, send_sems, recv_sems, local_sem.at[0])
        start()
        c_land[me] = c_ref[...]
        for k, frm in enumerate(peers):
            remote(c_ref, c_land.at[_dev_index(*frm)], k, frm).wait_recv()
        c_all = jnp.concatenate([c_land[b] for b in range(N_DEV)], axis=0)
        sc = c_all * _sigmoid(c_all)
        sc_ref[...] = sc
        rows = _dot3_rhs(sc, wa_ref[...])
        for b in range(N_DEV):
            part[b] = rows[b:b + 1, :]
        a_sends = [remote(part.at[_dev_index(*to)], ada_land.at[me], 7 + k, to) for k, to in enumerate(peers)]
        for cp in a_sends:
            cp.start()
        ada_land[me] = part[me]

        relay_near()
        finish_near()

        far_chip = 2 * (1 - x) + (1 - y)

        def stage(slots):
            for slot in slots:
                w_f32[slot * shard:(slot + 1) * shard, :] = w_all_ref[slot].astype(F32)

        def far_rows(k):
            main_row = lambda r: r if r < F_LO else r - N_HEADS
            first, last = 2 * shard * k, 2 * shard * (k + 1) - 1
            first = F_HI if F_LO <= first < F_HI else first
            last = F_LO - 1 if F_LO <= last < F_HI else last
            return (main_row(first) // BF16_TILE_ROWS * BF16_TILE_ROWS,
                    -(-(main_row(last) + 1) // BF16_TILE_ROWS) * BF16_TILE_ROWS)

        def near_rows(k):
            lo, hi = far_rows(k)
            return [(i, a, b) for i, (a, b) in enumerate(((0, lo), (hi, N_MAIN))) if a < b]

        def forget_is_far(k):
            return 2 * shard * k < F_HI and 2 * shard * (k + 1) > F_LO

        def main_copy(i, lo, hi):
            return pltpu.make_async_copy(wm_buf.at[lo:hi], w_main_hbm.at[lo:hi], out_sems.at[i])

        def write_main(i, lo, hi):
            if lo < min(hi, F_LO):
                wm_buf[lo:min(hi, F_LO), :] = w_f32[lo:min(hi, F_LO), :].astype(BF16)
            if max(lo, F_LO) < hi:
                wm_buf[max(lo, F_LO):hi, :] = w_f32[max(lo, F_LO) + N_HEADS:hi + N_HEADS, :].astype(BF16)
            main_copy(i, lo, hi).start()

        def write_forget():
            w_f_ref[...] = jnp.concatenate(
                [w_f32[F_LO:F_HI, :], jnp.zeros((LANE - N_HEADS, D), F32)], axis=0).astype(BF16)

        for k in range(N_DEV // 2):
            @pl.when(far_chip == k)
            def _(k=k):
                stage([slot for slot in range(N_DEV) if slot // 2 != k])
                for i, lo, hi in near_rows(k):
                    write_main(i, lo, hi)
                if not forget_is_far(k):
                    write_forget()

        relay_far()
        for k, frm in enumerate(peers):
            remote(part.at[0], ada_land.at[_dev_index(*frm)], 7 + k, frm).wait_recv()
        ada_ref[...] = ada_land[...]
        finish_far()
        for cp in c_sends + a_sends:
            cp.wait_send()

        for k in range(N_DEV // 2):
            @pl.when(far_chip == k)
            def _(k=k):
                stage([2 * k, 2 * k + 1])
                write_main(2, *far_rows(k))
                if forget_is_far(k):
                    write_forget()
                for i, lo, hi in near_rows(k):
                    main_copy(i, lo, hi).wait()
                main_copy(2, *far_rows(k)).wait()

    vmem = pl.BlockSpec(memory_space=pltpu.VMEM)
    return pl.pallas_call(
        body, name="gather_weights",
        in_specs=[vmem, ANY, vmem], out_specs=[ANY, vmem, vmem, vmem],
        out_shape=[jax.ShapeDtypeStruct((N_MAIN, D), BF16), jax.ShapeDtypeStruct((LANE, D), BF16),
                   jax.ShapeDtypeStruct((N_DEV, D), F32), jax.ShapeDtypeStruct((N_DEV, 1, cols), F32)],
        scratch_shapes=[pltpu.VMEM((N_DEV,) + w_in_rows.shape, BF16), pltpu.VMEM((D_IN, D), F32), pltpu.VMEM((N_MAIN, D), BF16),
                        pltpu.VMEM((N_DEV, 1, D), F32), pltpu.VMEM((N_DEV, 1, cols), F32), pltpu.VMEM((N_DEV, 1, cols), F32),
                        pltpu.SemaphoreType.DMA((N_GATHER_SEMS,)), pltpu.SemaphoreType.DMA((N_GATHER_SEMS,)),
                        pltpu.SemaphoreType.DMA((1,)),
                        pltpu.SemaphoreType.DMA((14,)), pltpu.SemaphoreType.DMA((14,)), pltpu.SemaphoreType.DMA((3,))],
        compiler_params=_params(),
    )(c, w_in_rows, w_ada)


def _inproj_forward(x, mod, w_main, w_f, b_main, b_f, tile):
    seq = x.shape[0]
    nt = seq // tile

    def body(x_ref, mod_ref, w_ref, wf_ref, b_ref, bf_ref,
             qp_ref, kp_ref, vp_ref, f_ref, p_ref, ga_ref, gp_ref, u_ref, carry_ref):
        i = pl.program_id(0)

        @pl.when(i == 0)
        def _():
            carry_ref[...] = jnp.zeros_like(carry_ref)

        u = x_ref[...] * mod_ref[0:1, :] + mod_ref[1:2, :]
        ub = u.astype(BF16)
        u_ref[...] = ub

        f = _dot_nt(ub, wf_ref[...]) + bf_ref[...]
        f_ref[...] = f
        lane = _lanes((tile, LANE))
        log_f = jnp.where(lane < N_HEADS, jnp.minimum(f, 0.0) - jnp.log(1.0 + jnp.exp(-jnp.abs(f))), 0.0)
        row = lax.broadcasted_iota(jnp.int32, (tile, tile), 0)
        col = lax.broadcasted_iota(jnp.int32, (tile, tile), 1)
        tri = (row >= col).astype(BF16)
        cum = _dot3(tri, log_f) + carry_ref[0:1, :]
        carry_ref[0:1, :] = cum[tile - 1:tile, :]
        cq = [part.astype(F32) for part in _split3(cum)]
        ck = [part.astype(F32) for part in _split3(-cum)]

        def proj(chunk):
            cols = pl.ds(chunk * COL_CHUNK, COL_CHUNK)
            return _dot_nt(ub, w_ref[cols, :]) + b_ref[:, cols]

        def head_tiles(r):
            for pair in range(N_HEADS // 2):
                both = r[:, pair * LANE:(pair + 1) * LANE]
                yield 2 * pair, both
                yield 2 * pair + 1, pltpu.roll(both, HEAD_DIM, 1)

        for h, val in head_tiles(proj(0)):
            extra = jnp.where((lane >= L_CK) & (lane < L_CK + 3), 1.0, 0.0)
            extra = _place3(lane, L_CQ, [part[:, h:h + 1] for part in cq], extra)
            qp_ref[h] = jnp.where(lane < HEAD_DIM, val * Q_SCALE, extra).astype(BF16)
        for h, val in head_tiles(proj(1)):
            ones = ((lane >= L_CQ) & (lane < L_CQ + 3)) | ((lane >= L_LSE) & (lane < L_LSE + 3))
            extra = _place3(lane, L_CK, [part[:, h:h + 1] for part in ck], jnp.where(ones, 1.0, 0.0))
            kp_ref[h] = jnp.where(lane < HEAD_DIM, val, extra).astype(BF16)
        for h, val in head_tiles(proj(2)):
            extra = jnp.where((lane >= HEAD_DIM) & (lane < HEAD_DIM + 3), -1.0, 0.0)
            vp_ref[h] = jnp.where(lane < HEAD_DIM, val, extra).astype(BF16)
        p_ref[...] = proj(3)
        ga_ref[...] = proj(4)
        gp_ref[...] = proj(5)

    head_block = pl.BlockSpec((N_HEADS, tile, LANE), lambda i: (0, i, 0))
    tok = lambda width: pl.BlockSpec((tile, width), lambda i: (i, 0))
    whole = lambda a: pl.BlockSpec(a.shape, lambda i: (0,) * a.ndim)
    padded = jax.ShapeDtypeStruct((N_HEADS, seq, LANE), BF16)
    half = jax.ShapeDtypeStruct((seq, D_ATT), F32)
    return pl.pallas_call(
        body, name="inproj_forward", grid=(nt,),
        in_specs=[tok(D), whole(mod), whole(w_main), whole(w_f), whole(b_main), whole(b_f)],
        out_specs=[head_block, head_block, head_block, tok(LANE), tok(D_POOL), tok(D_ATT), tok(D_POOL),
                   tok(D)],
        out_shape=[padded, padded, padded, jax.ShapeDtypeStruct((seq, LANE), F32), half, half, half,
                   jax.ShapeDtypeStruct((seq, D), BF16)],
        scratch_shapes=[pltpu.VMEM((8, LANE), F32)],
        compiler_params=_params(("arbitrary",)),
    )(x, mod, w_main, w_f, b_main, b_f)


def _attention_forward(qp, kp, vp, w_out, tile):
    seq = qp.shape[1]
    nb = seq // tile
    steps = (N_HEADS // 2) * nb

    def body(q_ref, k_ref, v_ref, wo_ref, att_ref, q2t_ref, wo_all_ref, s_a, s_b, m_ref, acc_ref,
             send_sems, recv_sems, local_sem):
        step = pl.program_id(0) * nb + pl.program_id(1)
        start, relay_near, relay_far, finish_near, finish_far = _gather_stages(
            wo_ref, wo_all_ref, send_sems, recv_sems, local_sem.at[0])
        pl.when(step == 0)(start)
        pl.when(step == steps // 4)(relay_near)
        pl.when(step == (3 * steps) // 4)(relay_far)

        i = pl.program_id(1)
        sub = lax.broadcasted_iota(jnp.int32, (LANE, tile), 0)
        row = lax.broadcasted_iota(jnp.int32, (tile, tile), 0)
        col = lax.broadcasted_iota(jnp.int32, (tile, tile), 1)
        q = [q_ref[0], q_ref[1]]

        def scores(buf, kb):
            rows = pl.ds(pl.multiple_of(kb * tile, tile), tile)
            for hh in range(2):
                buf[hh] = _dot_nt(k_ref[hh, rows, :], q[hh])

        def absorb(buf, kb, masked):
            rows = pl.ds(pl.multiple_of(kb * tile, tile), tile)
            for hh in range(2):
                m = m_ref[hh, 0:1, :]
                s = buf[hh]
                if masked:
                    s = jnp.where(row <= col, s, -1e30)
                m_new = jnp.maximum(m, jnp.max(s, axis=0, keepdims=True))
                p = jnp.exp(s - m_new).astype(BF16)
                acc_ref[hh] = jnp.exp(m - m_new) * acc_ref[hh] + _dot_tn(v_ref[hh, rows, :], p)
                m_ref[hh, 0:1, :] = m_new

        def two_blocks(j, _):
            scores(s_b, 2 * j + 1)
            absorb(s_a, 2 * j, False)
            scores(s_a, 2 * j + 2)
            absorb(s_b, 2 * j + 1, False)
            return 0

        def last_block():
            absorb(s_a, i, True)

        def last_two_blocks():
            scores(s_b, i)
            absorb(s_a, i - 1, False)
            absorb(s_b, i, True)

        scores(s_a, 0)
        m_ref[...] = jnp.full(m_ref.shape, -1e30, F32)
        acc_ref[...] = jnp.zeros_like(acc_ref)
        lax.fori_loop(0, i // 2, two_blocks, 0)
        lax.cond(i % 2 == 0, last_block, last_two_blocks)
        outs = []
        for hh in range(2):
            m, acc = m_ref[hh, 0:1, :], acc_ref[hh]
            l = -acc[HEAD_DIM:HEAD_DIM + 1, :]
            outs.append((acc / l)[:HEAD_DIM, :])
            neg_lse = [part.astype(F32) for part in _split3(-(m + jnp.log(l)))]
            q2t_ref[hh] = _place3(sub, L_LSE, neg_lse, q[hh].astype(F32).T).astype(BF16)
        att_ref[...] = jnp.concatenate(outs, axis=0).T
        @pl.when(step == steps - 1)
        def _():
            finish_near()
            finish_far()

    pair = pl.BlockSpec((2, tile, LANE), lambda hp, i: (hp, i, 0))
    full = pl.BlockSpec((2, seq, LANE), lambda hp, i: (hp, 0, 0))
    return pl.pallas_call(
        body, name="attention_forward", grid=(N_HEADS // 2, nb),
        in_specs=[pair, full, full, ANY],
        out_specs=[pl.BlockSpec((tile, LANE), lambda hp, i: (i, hp)),
                   pl.BlockSpec((2, LANE, tile), lambda hp, i: (hp, 0, i)), ANY],
        out_shape=[jax.ShapeDtypeStruct((seq, D_ATT), F32),
                   jax.ShapeDtypeStruct((N_HEADS, LANE, seq), BF16),
                   jax.ShapeDtypeStruct((N_DEV,) + w_out.shape, w_out.dtype)],
        scratch_shapes=[pltpu.VMEM((2, tile, tile), F32), pltpu.VMEM((2, tile, tile), F32),
                        pltpu.VMEM((2, 8, tile), F32), pltpu.VMEM((2, LANE, tile), F32),
                        pltpu.SemaphoreType.DMA((N_GATHER_SEMS,)), pltpu.SemaphoreType.DMA((N_GATHER_SEMS,)),
                        pltpu.SemaphoreType.DMA((1,))],
        compiler_params=_params(("arbitrary", "arbitrary")),
    )(qp, kp, vp, w_out)


def _window_sum(x, halo, window, transposed):
    tile = x.shape[0]

    def split_cat(a):
        hi = a.astype(BF16)
        return jnp.concatenate([hi, (a - hi.astype(F32)).astype(BF16)], axis=1)

    def fold(r):
        return r[:, :LANE] + r[:, LANE:]

    r = lax.broadcasted_iota(jnp.int32, (tile, tile), 0)
    c = lax.broadcasted_iota(jnp.int32, (tile, tile), 1)
    rh = lax.broadcasted_iota(jnp.int32, (HALO, HALO), 0)
    ch = lax.broadcasted_iota(jnp.int32, (HALO, HALO), 1)
    if not transposed:
        band = (c <= r) & (r - c < window)
        edge = (rh + HALO - ch) < window
    else:
        band = (r <= c) & (c - r < window)
        edge = (HALO + ch - rh) < window
    out = fold(_dot(band.astype(BF16), split_cat(x)))
    reach = fold(_dot(edge.astype(BF16), split_cat(halo)))
    if not transposed:
        return jnp.concatenate([out[:HALO] + reach, out[HALO:]], axis=0)
    return jnp.concatenate([out[:tile - HALO], out[tile - HALO:] + reach], axis=0)


def _silu_parts(g):
    sig = _sigmoid(g)
    return g * sig, sig * (1.0 + g * (1.0 - sig))


def _middle(x, tgt, att, g_att, g_pool, p, vecs, pool_vecs, w_out, w_pool, tile):
    seq = x.shape[0]
    nt = seq // tile
    halo_blocks = tile // HALO

    def body(x_ref, tgt_ref, att_ref, ga_ref, gp_ref, p_ref, ph_ref, vec_ref, pvec_ref, wo_ref, wp_ref,
             dxa_ref, do2_ref, dga_ref, dgp_ref, dpooled_ref, gwo_ref, dwp_ref, dvec_ref, dwo_ref, dpvec_ref):
        i = pl.program_id(0)

        @pl.when(i == 0)
        def _():
            dwo_ref[...] = jnp.zeros_like(dwo_ref)
            dwp_ref[...] = jnp.zeros_like(dwp_ref)
            dvec_ref[...] = jnp.zeros_like(dvec_ref)
            dpvec_ref[...] = jnp.zeros_like(dpvec_ref)

        gate, b_out, ln_g, ln_b = (vec_ref[k:k + 1, :] for k in range(4))
        b_pool, pool_scale = pvec_ref[0:1, :], pvec_ref[1:2, :]
        x = x_ref[...]
        p = p_ref[...]
        p_halo = ph_ref[...] * jnp.where(i > 0, 1.0, 0.0)
        pos = i * tile + lax.broadcasted_iota(jnp.int32, (tile, 1), 0) + 1

        pooled, mixed = [], []
        for g, window in enumerate(POOL_WINDOWS):
            cols = slice(g * GROUP_DIM, (g + 1) * GROUP_DIM)
            wsum = _window_sum(p[:, cols], p_halo[:, cols], window, False)
            count = jnp.minimum(pos, window).astype(F32)
            pooled.append(wsum / count - p[:, cols])
            mixed.append(_dot(pooled[g].astype(BF16), wp_ref[g]) + b_pool[:, cols])
        mixed = jnp.concatenate(mixed, axis=1)
        pool = mixed * pool_scale

        att = att_ref[...]
        g_att, g_pool = ga_ref[...], gp_ref[...]
        silu_a, dsilu_a = _silu_parts(g_att)
        silu_p, dsilu_p = _silu_parts(g_pool)
        y_in = jnp.concatenate([att * silu_a, pool * silu_p], axis=1)
        y = _dot(y_in.astype(BF16), wo_ref[...]) + b_out
        h = ALPHA * x + gate * y
        mu = jnp.mean(h, axis=1, keepdims=True)
        hc = h - mu
        var = jnp.mean(hc * hc, axis=1, keepdims=True)
        rstd = lax.rsqrt(var + LN_EPS)
        yhat = hc * rstd
        diff = yhat * ln_g + ln_b - tgt_ref[...]
        loss_rows = jnp.sum(diff * diff, axis=1, keepdims=True)
        d_out = diff * (1.0 / D)

        d_yhat = d_out * ln_g
        dh = rstd * (d_yhat - jnp.mean(d_yhat, axis=1, keepdims=True)
                     - yhat * jnp.mean(d_yhat * yhat, axis=1, keepdims=True))
        dxa_ref[...] = ALPHA * dh
        dy = dh * gate
        dyb = dy.astype(BF16)
        lane = _lanes((1, D))
        loss_row = jnp.where(lane == 0, (0.5 / D) * jnp.sum(loss_rows, axis=0, keepdims=True), 0.0)
        dvec_ref[5:6, :] += jnp.sum(dh * y, axis=0, keepdims=True)
        dvec_ref[0:1, :] += jnp.sum(dy, axis=0, keepdims=True)
        dvec_ref[1:2, :] += jnp.sum(d_out * yhat, axis=0, keepdims=True)
        dvec_ref[2:3, :] += jnp.sum(d_out, axis=0, keepdims=True)
        dvec_ref[4:5, :] += loss_row

        dwo_ref[...] += _dot(y_in.T.astype(BF16), dyb)
        d_yin = _dot_nt(dyb, wo_ref[...])
        d_a, d_pl = d_yin[:, :D_ATT], d_yin[:, D_ATT:]
        d_att = d_a * silu_a
        d_att_t = d_att.T
        prod_t = (d_att * att).T
        sub = lax.broadcasted_iota(jnp.int32, (HEAD_DIM, tile), 0)
        for h in range(N_HEADS):
            rows = slice(h * HEAD_DIM, (h + 1) * HEAD_DIM)
            delta = jnp.sum(prod_t[rows], axis=0, keepdims=True)
            extra = _place3(sub, 0, [part.astype(F32) for part in _split3(delta)], 0.0)
            do2_ref[h] = jnp.concatenate([d_att_t[rows], extra], axis=0).astype(BF16)
        dga_ref[...] = d_a * att * dsilu_a
        dgp_ref[...] = d_pl * pool * dsilu_p
        d_pool = d_pl * silu_p
        d_mixed = d_pool * pool_scale
        dpvec_ref[0:1, :] += jnp.sum(d_mixed, axis=0, keepdims=True)
        dpvec_ref[1:2, :] += jnp.sum(d_pool * mixed, axis=0, keepdims=True)
        d_pooled = []
        for g in range(len(POOL_WINDOWS)):
            cols = slice(g * GROUP_DIM, (g + 1) * GROUP_DIM)
            dmb = d_mixed[:, cols].astype(BF16)
            dwp_ref[g] += _dot(pooled[g].T.astype(BF16), dmb)
            d_pooled.append(_dot_nt(dmb, wp_ref[g]))
        dpooled_ref[...] = jnp.concatenate(d_pooled, axis=1)

        @pl.when(i == nt - 1)
        def _():
            gwo_ref[...] = dwo_ref[...].astype(BF16)
            dvec_ref[3:4, :] = jnp.concatenate([dpvec_ref[0:1, :], dpvec_ref[1:2, :]], axis=1)

    tok = lambda width: pl.BlockSpec((tile, width), lambda i: (i, 0))
    whole = lambda a: pl.BlockSpec(a.shape, lambda i: (0,) * a.ndim)
    halo = pl.BlockSpec((HALO, D_POOL), lambda i: (jnp.maximum(i * halo_blocks - 1, 0), 0))
    half = jax.ShapeDtypeStruct((seq, D_ATT), F32)
    outs = [jax.ShapeDtypeStruct((seq, D), F32), jax.ShapeDtypeStruct((N_HEADS, LANE, seq), BF16), half, half, half,
            jax.ShapeDtypeStruct(w_out.shape, BF16), jax.ShapeDtypeStruct(w_pool.shape, F32),
            jax.ShapeDtypeStruct(vecs.shape, F32)]
    return pl.pallas_call(
        body, name="middle", grid=(nt,),
        in_specs=[tok(D), tok(D), tok(D_ATT), tok(D_ATT), tok(D_POOL), tok(D_POOL), halo,
                  whole(vecs), whole(pool_vecs), whole(w_out), whole(w_pool)],
        out_specs=[tok(D), pl.BlockSpec((N_HEADS, LANE, tile), lambda i: (0, 0, i)),
                   tok(D_ATT), tok(D_POOL), tok(D_POOL),
                   whole(w_out), whole(w_pool), whole(vecs)],
        out_shape=outs,
        scratch_shapes=[pltpu.VMEM(w_out.shape, F32), pltpu.VMEM(pool_vecs.shape, F32)],
        compiler_params=_params(("arbitrary",)),
    )(x, tgt, att, g_att, g_pool, p, p, vecs, pool_vecs, w_out, w_pool)


def _attention_backward(q2t, kp, vp, do2t, gw_out, vecs, pool, tile):
    seq = kp.shape[1]
    nb = seq // tile
    last = N_HEADS // 2 - 1

    def body(qt_ref, k_ref, v_ref, dot_ref, gwo_hbm, vecs_hbm, pool_hbm,
             dq_ref, dk_ref, dv_ref, dcum_ref, g_out_ref, vecs_sum_ref, pool_sum_ref,
             dq_acc, dk_acc, dv_acc, gwo_ref, vecs_ref, pool_ref,
             r1, s2, r2, v_sib, v_chip, v_recv, p_sib, p_chip, p_recv, send_sems, recv_sems):
        hp = pl.program_id(0)
        start, middle, fold, finish = _reduce_stages(
            [gwo_ref], [g_out_ref], [r1], [s2], [r2],
            [(vecs_ref, vecs_sum_ref, v_sib, v_chip, v_recv), (pool_ref, pool_sum_ref, p_sib, p_chip, p_recv)],
            send_sems, recv_sems)

        @pl.when(hp == 0)
        def _():
            pltpu.sync_copy(gwo_hbm, gwo_ref)
            pltpu.sync_copy(vecs_hbm, vecs_ref)
            pltpu.sync_copy(pool_hbm, pool_ref)
            start()

        pl.when(hp == 1)(middle)
        pl.when(hp == 2)(fold)

        row = lax.broadcasted_iota(jnp.int32, (tile, tile), 0)
        col = lax.broadcasted_iota(jnp.int32, (tile, tile), 1)
        dq_acc[...] = jnp.zeros_like(dq_acc)

        def kv_block(kb, _):
            krows = pl.ds(pl.multiple_of(kb * tile, tile), tile)
            k = [k_ref[hh, krows, :] for hh in range(2)]
            v = [v_ref[hh, krows, :] for hh in range(2)]
            k_t = [k[hh].T for hh in range(2)]

            def q_block(qb, masked):
                qcols = pl.ds(pl.multiple_of(qb * tile, tile), tile)
                for hh in range(2):
                    q_t = qt_ref[hh, :, qcols]
                    do_t = dot_ref[hh, :, qcols]
                    s_t = _dot(k[hh], q_t)
                    if masked:
                        s_t = jnp.where(row <= col, s_t, -1e30)
                    p_t = jnp.exp(s_t)
                    ds_t = (p_t * _dot(v[hh], do_t)).astype(BF16)
                    dv_new = _dot_nt(do_t, p_t.astype(BF16))
                    dk_new = _dot_nt(q_t, ds_t)
                    if masked:
                        dv_acc[hh], dk_acc[hh] = dv_new, dk_new
                    else:
                        dv_acc[hh] += dv_new
                        dk_acc[hh] += dk_new
                    dq_acc[hh, :, qcols] += _dot(k_t[hh], ds_t)

            q_block(kb, True)

            def two_later_blocks(j, _):
                q_block(kb + 1 + 2 * j, False)
                q_block(kb + 2 + 2 * j, False)
                return 0

            later = nb - 1 - kb
            lax.fori_loop(0, later // 2, two_later_blocks, 0)
            pl.when(later % 2 == 1)(lambda: q_block(nb - 1, False))
            for hh in range(2):
                dk = dk_acc[hh]
                dk_ref[hh, :, krows] = dk.astype(BF16)
                dv_ref[hh, :, krows] = dv_acc[hh].astype(BF16)
                dcum_ref[hh, :, krows] = -dk[L_CK:L_CK + 1, :]
            return 0

        lax.fori_loop(0, nb, kv_block, 0)
        for hh in range(2):
            dq = dq_acc[hh]
            dcum_ref[hh] += dq[L_CQ:L_CQ + 1, :]
            dq_ref[hh] = (dq * Q_SCALE).astype(BF16)
        pl.when(hp == last)(finish)

    pair = pl.BlockSpec((2, seq, LANE), lambda hp: (hp, 0, 0))
    pair_t = pl.BlockSpec((2, LANE, seq), lambda hp: (hp, 0, 0))
    whole = lambda shape: pl.BlockSpec(shape, lambda hp: (0,) * len(shape))
    grad = jax.ShapeDtypeStruct((N_HEADS, LANE, seq), BF16)
    return pl.pallas_call(
        body, name="attention_backward", grid=(N_HEADS // 2,),
        in_specs=[pair_t, pair, pair, pair_t, ANY, ANY, ANY],
        out_specs=[pair_t, pair_t, pair_t, pl.BlockSpec((2, 1, seq), lambda hp: (hp, 0, 0)),
                   whole(gw_out.shape[1:]), whole(vecs.shape), whole(pool.shape)],
        out_shape=[grad, grad, grad, jax.ShapeDtypeStruct((N_HEADS, 1, seq), F32),
                   jax.ShapeDtypeStruct(gw_out.shape[1:], F32), jax.ShapeDtypeStruct(vecs.shape, F32),
                   jax.ShapeDtypeStruct(pool.shape, F32)],
        scratch_shapes=[pltpu.VMEM((2, LANE, seq), F32), pltpu.VMEM((2, LANE, tile), F32),
                        pltpu.VMEM((2, LANE, tile), F32), pltpu.VMEM(gw_out.shape, BF16),
                        pltpu.VMEM(vecs.shape, F32), pltpu.VMEM(pool.shape, F32)]
        + _reduce_scratch(gw_out, [vecs, pool]),
        compiler_params=_params(("arbitrary",)),
    )(q2t, kp, vp, do2t, gw_out, vecs, pool)


def _inproj_backward(dqp, dkp, dvp, d_cum, f, d_pooled, d_ga, d_gp, x, dxa, u, mod, w_main, w_f, tile):
    seq = x.shape[0]
    nt = seq // tile
    halo_blocks = tile // HALO

    def body(dq_ref, dk_ref, dv_ref, dcum_ref, f_ref, dpo_ref, dph_ref, dga_ref, dgp_ref, x_ref, dxa_ref, u_ref,
             mod_ref, w_ref, wf_ref,
             dx_ref, dproj_ref, dwf_ref, db_ref, dbf_ref, dmod_ref, carry_ref):
        step = pl.program_id(0)
        i = nt - 1 - step

        @pl.when(step == 0)
        def _():
            carry_ref[...] = jnp.zeros_like(carry_ref)
            dwf_ref[...] = jnp.zeros_like(dwf_ref)
            db_ref[...] = jnp.zeros_like(db_ref)
            dbf_ref[...] = jnp.zeros_like(dbf_ref)
            dmod_ref[...] = jnp.zeros_like(dmod_ref)

        ones = jnp.ones((8, tile), BF16)

        def emit(chunk, val):
            cols = pl.ds(chunk * COL_CHUNK, COL_CHUNK)
            db_ref[0:1, cols] += jnp.sum(val, axis=0, keepdims=True)
            vb = val.astype(BF16)
            dproj_ref[:, pl.ds((chunk - 3) * COL_CHUNK, COL_CHUNK)] = vb
            return _dot(vb, w_ref[cols, :])

        d_u = jnp.zeros((tile, D), F32)
        for chunk, ref in enumerate((dq_ref, dk_ref, dv_ref)):
            cols = pl.ds(chunk * COL_CHUNK, COL_CHUNK)
            val_t = ref[:, 0:HEAD_DIM, :].reshape(COL_CHUNK, tile)
            db_ref[:, cols] += _dot_nt(ones, val_t)
            d_u += _dot_tn(val_t, w_ref[cols, :])

        d_pooled = dpo_ref[...]
        d_halo = dph_ref[...] * jnp.where(i < nt - 1, 1.0, 0.0)
        pos = i * tile + lax.broadcasted_iota(jnp.int32, (tile, 1), 0) + 1
        d_p = []
        for g, window in enumerate(POOL_WINDOWS):
            cols = slice(g * GROUP_DIM, (g + 1) * GROUP_DIM)
            scaled = d_pooled[:, cols] / jnp.minimum(pos, window).astype(F32)
            d_p.append(_window_sum(scaled, d_halo[:, cols] * (1.0 / window), window, True) - d_pooled[:, cols])
        d_u += emit(3, jnp.concatenate(d_p, axis=1))
        d_u += emit(4, dga_ref[...])
        d_u += emit(5, dgp_ref[...])

        row = lax.broadcasted_iota(jnp.int32, (tile, tile), 0)
        col = lax.broadcasted_iota(jnp.int32, (tile, tile), 1)
        later = (row >= col).astype(BF16)
        d_logf = sum(_dot(part, later) for part in _split3(dcum_ref[:, 0, :])) + carry_ref[:, 0:1]
        carry_ref[:, 0:1] = d_logf[:, 0:1]
        d_f = d_logf * _sigmoid(-f_ref[...].T[0:N_HEADS, :])
        d_f = jnp.concatenate([d_f, jnp.zeros((LANE - N_HEADS, tile), F32)], axis=0)
        dbf_ref[...] += sum(_dot_nt(ones, part) for part in _split3(d_f))
        d_fb = d_f.astype(BF16)
        d_u += _dot_tn(d_fb, wf_ref[...])
        dwf_ref[...] += _dot(d_fb, u_ref[...])

        x = x_ref[...]
        dx_ref[...] = dxa_ref[...] + d_u * mod_ref[0:1, :]
        dmod_ref[0:1, :] += jnp.sum(d_u * x, axis=0, keepdims=True)
        dmod_ref[1:2, :] += jnp.sum(d_u, axis=0, keepdims=True)

    rev = lambda step: nt - 1 - step
    tok = lambda width: pl.BlockSpec((tile, width), lambda s: (rev(s), 0))
    head_block = pl.BlockSpec((N_HEADS, LANE, tile), lambda s: (0, 0, rev(s)))
    whole = lambda a: pl.BlockSpec(a.shape, lambda s: (0,) * a.ndim)
    halo = pl.BlockSpec((HALO, D_POOL), lambda s: (jnp.minimum((rev(s) + 1) * halo_blocks, seq // HALO - 1), 0))
    small = lambda width: jax.ShapeDtypeStruct((8, width), F32)
    n_rest = N_MAIN - OFF_P
    return pl.pallas_call(
        body, name="inproj_backward", grid=(nt,),
        in_specs=[head_block, head_block, head_block, pl.BlockSpec((N_HEADS, 1, tile), lambda s: (0, 0, rev(s))),
                  tok(LANE), tok(D_POOL), halo, tok(D_ATT), tok(D_POOL),
                  tok(D), tok(D), tok(D),
                  whole(mod), whole(w_main), whole(w_f)],
        out_specs=[tok(D), tok(n_rest), pl.BlockSpec((LANE, D), lambda s: (0, 0)),
                   pl.BlockSpec((8, N_MAIN), lambda s: (0, 0)), pl.BlockSpec((8, LANE), lambda s: (0, 0)),
                   pl.BlockSpec((8, D), lambda s: (0, 0))],
        out_shape=[jax.ShapeDtypeStruct((seq, D), F32), jax.ShapeDtypeStruct((seq, n_rest), BF16),
                   jax.ShapeDtypeStruct((LANE, D), F32), small(N_MAIN), small(LANE), small(D)],
        scratch_shapes=[pltpu.VMEM((8, LANE), F32)],
        compiler_params=_params(("arbitrary",)),
    )(dqp, dkp, dvp, d_cum, f, d_pooled, d_pooled, d_ga, d_gp, x, dxa, u, mod, w_main, w_f)


def _weight_grads(dq_t, dk_t, dv_t, dw_f, dproj, u, k_tile):
    seq = u.shape[0]
    nk = seq // k_tile
    rows = N_HEADS * HEAD_DIM

    def body(dq_ref, dk_ref, dv_ref, dwf_ref, dp_ref, u_ref, out_ref, acc_ref):
        k = pl.program_id(0)

        @pl.when(k == 0)
        def _():
            acc_ref[...] = jnp.zeros_like(acc_ref)

        tokens = u_ref[...]
        for j, ref in enumerate((dq_ref, dk_ref, dv_ref)):
            acc_ref[pl.ds(j * rows, rows), :] += _dot(ref[...].reshape(rows, k_tile), tokens)
        for j in range(dproj.shape[1] // COL_CHUNK):
            cols = pl.ds(j * COL_CHUNK, COL_CHUNK)
            acc_ref[pl.ds(F_HI + j * COL_CHUNK, COL_CHUNK), :] += _dot_tn(dp_ref[:, cols], tokens)

        @pl.when(k == nk - 1)
        def _():
            acc_ref[F_LO:F_HI, :] = dwf_ref[0:N_HEADS, :]
            for slot in range(N_DEV):
                out_ref[slot] = acc_ref[slot * shard:(slot + 1) * shard, :].astype(BF16)

    shard = D_IN // N_DEV
    heads = pl.BlockSpec((N_HEADS, HEAD_DIM, k_tile), lambda k: (0, 0, k))
    return pl.pallas_call(
        body, name="weight_grads", grid=(nk,),
        in_specs=[heads, heads, heads, pl.BlockSpec(dw_f.shape, lambda k: (0, 0)),
                  pl.BlockSpec((k_tile, dproj.shape[1]), lambda k: (k, 0)), pl.BlockSpec((k_tile, D), lambda k: (k, 0))],
        out_specs=pl.BlockSpec((N_DEV, shard, D), lambda k: (0, 0, 0)),
        out_shape=jax.ShapeDtypeStruct((N_DEV, shard, D), BF16),
        scratch_shapes=[pltpu.VMEM((D_IN, D), F32)],
        compiler_params=_params(("arbitrary",)),
    )(dq_t, dk_t, dv_t, dw_f, dproj, u)


def _adamw(w, g, m, v):
    m = ADAM_B1 * m + (1.0 - ADAM_B1) * g
    v = ADAM_B2 * v + (1.0 - ADAM_B2) * (g * g)
    m_hat = m / (1.0 - ADAM_B1 ** ADAM_STEP)
    v_hat = v / (1.0 - ADAM_B2 ** ADAM_STEP)
    delta = -ADAM_LR * (m_hat / (jnp.sqrt(v_hat) + ADAM_EPS) + ADAM_WD * w)
    return delta, m, v


SUBLANES = 8


def _adamw_packed(g, w, m, v, name, chunks=4):
    rows, cols = g.shape
    per_row = cols // LANE
    assert cols % LANE == 0 and per_row == SUBLANES and w.shape == (rows * per_row, LANE)
    step = -(-rows // (chunks * SUBLANES)) * SUBLANES
    bounds = [(r0, min(r0 + step, rows)) for r0 in range(0, rows, step)]

    def body(g_hbm, w_hbm, m_hbm, v_hbm, og_hbm, od_hbm, om_hbm, ov_hbm, g_buf, in_buf, out_buf, in_sems, out_sems):
        def copies_in(c):
            r0, r1 = bounds[c]
            packed = slice(r0 * per_row, r1 * per_row)
            return [pltpu.make_async_copy(g_hbm.at[r0:r1], g_buf.at[r0:r1], in_sems.at[c, 0])] + [
                pltpu.make_async_copy(src.at[packed], in_buf.at[i, packed], in_sems.at[c, 1 + i])
                for i, src in enumerate((w_hbm, m_hbm, v_hbm))]

        def copies_out(c):
            r0, r1 = bounds[c]
            packed = slice(r0 * per_row, r1 * per_row)
            return [pltpu.make_async_copy(out_buf.at[i, packed], dst.at[packed], out_sems.at[c, i])
                    for i, dst in enumerate((og_hbm, od_hbm, om_hbm, ov_hbm))]

        for c in range(len(bounds)):
            for cp in copies_in(c):
                cp.start()
        for c, (r0, r1) in enumerate(bounds):
            for cp in copies_in(c):
                cp.wait()
            for j in range(per_row):
                lanes = pl.ds(r0 * per_row + j, r1 - r0, stride=per_row)
                g_part = g_buf[r0:r1, j * LANE:(j + 1) * LANE]
                results = _adamw(in_buf[0, lanes, :], g_part, in_buf[1, lanes, :], in_buf[2, lanes, :])
                for i, val in enumerate((g_part,) + results):
                    out_buf[i, lanes, :] = val
            for cp in copies_out(c):
                cp.start()
        for c in range(len(bounds)):
            for cp in copies_out(c):
                cp.wait()

    shape = jax.ShapeDtypeStruct(w.shape, F32)
    return pl.pallas_call(
        body, name=name,
        in_specs=[ANY] * 4, out_specs=[ANY] * 4, out_shape=[shape] * 4,
        scratch_shapes=[pltpu.VMEM(g.shape, F32), pltpu.VMEM((3,) + w.shape, F32), pltpu.VMEM((4,) + w.shape, F32),
                        pltpu.SemaphoreType.DMA((len(bounds), 4)), pltpu.SemaphoreType.DMA((len(bounds), 4))],
        compiler_params=_params(),
    )(g, w, m, v)


def _ada_adamw(sc_all, d_ada, w, m, v, chunks=4):
    rows, cols = w.shape
    step, sub = rows // chunks, 32
    assert rows % chunks == 0 and step % LANE == 0 and step % sub == 0

    def body(sc_ref, d_ref, w_hbm, m_hbm, v_hbm, og_hbm, od_hbm, om_hbm, ov_hbm, in_buf, out_buf, in_sems, out_sems):
        def copies_in(c):
            part = slice(c * step, (c + 1) * step)
            return [pltpu.make_async_copy(src.at[part], in_buf.at[i, part], in_sems.at[c, i])
                    for i, src in enumerate((w_hbm, m_hbm, v_hbm))]

        def copies_out(c):
            part = slice(c * step, (c + 1) * step)
            return [pltpu.make_async_copy(out_buf.at[i, part], dst.at[part], out_sems.at[c, i])
                    for i, dst in enumerate((og_hbm, od_hbm, om_hbm, ov_hbm))]

        for c in range(chunks):
            for cp in copies_in(c):
                cp.start()
        for c in range(chunks):
            sc_t = sc_ref[:, c * step:(c + 1) * step].T
            for cp in copies_in(c):
                cp.wait()
            for r0 in range(0, step, sub):
                part = slice(c * step + r0, c * step + r0 + sub)
                g = sc_t[r0:r0 + sub, 0:1] * d_ref[0:1, :]
                for b in range(1, N_DEV):
                    g = g + sc_t[r0:r0 + sub, b:b + 1] * d_ref[b:b + 1, :]
                results = _adamw(in_buf[0, part, :], g, in_buf[1, part, :], in_buf[2, part, :])
                for i, val in enumerate((g,) + results):
                    out_buf[i, part, :] = val
            for cp in copies_out(c):
                cp.start()
        for c in range(chunks):
            for cp in copies_out(c):
                cp.wait()

    in_vmem = pl.BlockSpec(memory_space=pltpu.VMEM)
    shape = jax.ShapeDtypeStruct(w.shape, F32)
    return pl.pallas_call(
        body, name="ada_adamw",
        in_specs=[in_vmem, in_vmem, ANY, ANY, ANY], out_specs=[ANY] * 4, out_shape=[shape] * 4,
        scratch_shapes=[pltpu.VMEM((3,) + w.shape, F32), pltpu.VMEM((4,) + w.shape, F32),
                        pltpu.SemaphoreType.DMA((chunks, 3)), pltpu.SemaphoreType.DMA((chunks, 4))],
        compiler_params=_params(),
    )(sc_all, d_ada, w, m, v)


F_LO, F_HI = 3 * D_ATT, 3 * D_ATT + N_HEADS


def _split_forget(a, axis):
    idx = lambda lo, hi: tuple(slice(lo, hi) if d == axis else slice(None) for d in range(a.ndim))
    pad = [(0, LANE - N_HEADS) if d == axis else (0, 0) for d in range(a.ndim)]
    return jnp.concatenate([a[idx(0, F_LO)], a[idx(F_HI, D_IN)]], axis=axis), jnp.pad(a[idx(F_LO, F_HI)], pad)


def _join_forget(main, f, axis):
    idx = lambda lo, hi: tuple(slice(lo, hi) if d == axis else slice(None) for d in range(main.ndim))
    return jnp.concatenate([main[idx(0, F_LO)], f[idx(0, N_HEADS)], main[idx(F_LO, N_MAIN)]], axis=axis)


def _adamw_small(grad_rows, row_params, whole_params, summed_params, scalar_at):
    n_row, n_whole, n_sum = len(row_params), len(whole_params), len(summed_params)
    n = n_row + n_whole + n_sum

    def body(g_ref, *refs):
        n_in = 3 * n_row + 4 * (n_whole + n_sum)
        ins, outs = list(refs[:n_in]), refs[n_in:]
        for i in range(n):
            if i < n_row:
                row, lo, hi = row_params[i][:3]
                g = g_ref[row:row + 1, lo:hi]
            elif i < n_row + n_whole:
                g = ins.pop(0)[...]
            else:
                parts = ins.pop(0)
                g = parts[0]
                for k in range(1, N_DEV):
                    g = g + parts[k]
            w, m, v = (ins.pop(0)[...] for _ in range(3))
            outs[4 * i][...] = g
            outs[4 * i + 1][...], outs[4 * i + 2][...], outs[4 * i + 3][...] = _adamw(w, g, m, v)
        row, lane = scalar_at
        outs[4 * n][...] = g_ref[row:row + 1, lane:lane + 1]

    shapes = [p[3] for p in row_params] + [p[1] for p in whole_params] + [p[1] for p in summed_params]
    operands = [a for p in row_params for a in p[3:]] + [a for p in whole_params + summed_params for a in p]
    flat = pl.pallas_call(
        body, name="adamw_small",
        out_shape=[jax.ShapeDtypeStruct(w.shape, F32) for w in shapes for _ in range(4)]
        + [jax.ShapeDtypeStruct((1, 1), F32)],
        compiler_params=_params(),
    )(grad_rows, *operands)
    return [flat[4 * i:4 * i + 4] for i in range(n)], flat[4 * n].reshape(())


def kernel(x, c, w_ada, b_ada, w_in, b_in, w_pool_mix, b_pool_mix, pool_scale, w_out, b_out, ln_g, ln_b, loss_target, m_w_ada, m_b_ada, m_w_in, m_b_in, m_w_pool_mix, m_b_pool_mix, m_pool_scale, m_w_out, m_b_out, m_ln_g, m_ln_b, v_w_ada, v_b_ada, v_w_in, v_b_in, v_w_pool_mix, v_b_pool_mix, v_pool_scale, v_w_out, v_b_out, v_ln_g, v_ln_b):
    seq = x.shape[1]
    tile = min(256, seq)
    attn_tile = min(512, max(128, seq // 4))
    me = _dev_index(*_mesh_pos())
    x2, tgt = x[0], loss_target[0]

    rows_of = lambda a: jnp.swapaxes(a, 1, 2)[0]
    w_main, w_f, sc_all, ada_mine = _gather_and_ada(c, rows_of(w_in).astype(BF16), w_ada[0])
    ada = ada_mine.reshape(1, D_ADA) + b_ada
    shift, scale, gate = ada[:, 0:D], ada[:, D:2 * D], ada[:, 2 * D:]
    mod = jnp.concatenate([1.0 + scale, shift, jnp.zeros((6, D), F32)], axis=0)
    b_main, b_f = _split_forget(b_in, 1)

    qp, kp, vp, f, p, g_att, g_pool, u = _inproj_forward(x2, mod, w_main, w_f, b_main, b_f, tile)
    att, q2t, w_out_g = _attention_forward(qp, kp, vp, w_out[0].astype(BF16), attn_tile)

    vecs = jnp.concatenate([gate, b_out, ln_g, ln_b, jnp.zeros((4, D), F32)], axis=0)
    pool_vecs = jnp.concatenate([b_pool_mix.reshape(1, D_POOL), pool_scale, jnp.zeros((6, D_POOL), F32)], axis=0)
    dxa, do2, d_ga, d_gp, d_pooled, gw_out, dw_pool, dvec = _middle(
        x2, tgt, att, g_att, g_pool, p, vecs, pool_vecs, w_out_g.reshape(D, D), w_pool_mix[0].astype(BF16), tile)

    pool_rows = w_pool_mix.shape[1] * GROUP_DIM
    dqp, dkp, dvp, d_cum, g_out, dvec_sum, dw_pool_sum = _attention_backward(
        q2t, kp, vp, do2, gw_out.reshape(N_DEV, D // N_DEV, D), dvec, dw_pool.reshape(pool_rows, GROUP_DIM), attn_tile)
    dx, dproj, dw_f, db_main, db_f, dmod = _inproj_backward(
        dqp, dkp, dvp, d_cum, f, d_pooled, d_ga, d_gp, x2, dxa, u, mod, w_main, w_f, tile)
    gw_in = _weight_grads(dqp, dkp, dvp, dw_f, dproj, u, min(512, seq))
    d_ada = jnp.concatenate([dmod[1:2], dmod[0:1], dvec[5:6]], axis=1)
    g_in_rows, g_b_in, d_ada_all = _reduce_grads(gw_in, _join_forget(db_main[0:1], db_f[0:1], 1), d_ada)

    packed = lambda a: jnp.transpose(a.reshape(SUBLANES, LANE, -1), (2, 0, 1)).reshape(-1, LANE)
    outs_in = _adamw_packed(g_in_rows, packed(w_in), packed(m_w_in), packed(v_w_in), "adamw_w_in")
    g_w_in, d_w_in, nm_w_in, nv_w_in = (
        jnp.transpose(a.reshape(-1, SUBLANES, LANE), (1, 2, 0)).reshape(D, -1) for a in outs_in)
    flat_pool = lambda a: a.reshape(1, D_POOL)
    pool_2d = lambda a: a.reshape(pool_rows, GROUP_DIM)
    rows, loss = _adamw_small(
        dvec_sum,
        [(0, 0, D, b_out, m_b_out, v_b_out), (1, 0, D, ln_g, m_ln_g, v_ln_g), (2, 0, D, ln_b, m_ln_b, v_ln_b),
         (3, 0, D_POOL, flat_pool(b_pool_mix), flat_pool(m_b_pool_mix), flat_pool(v_b_pool_mix)),
         (3, D_POOL, 2 * D_POOL, pool_scale, m_pool_scale, v_pool_scale)],
        [(g_out, w_out[0], m_w_out[0], v_w_out[0]),
         (dw_pool_sum, pool_2d(w_pool_mix), pool_2d(m_w_pool_mix), pool_2d(v_w_pool_mix)),
         (g_b_in, b_in, m_b_in, v_b_in)],
        [(d_ada_all, b_ada, m_b_ada, v_b_ada)],
        scalar_at=(4, 0))
    small = {"b_out": rows[0], "ln_g": rows[1], "ln_b": rows[2],
             "b_pool": [a.reshape(b_pool_mix.shape) for a in rows[3]], "pool_scale": rows[4],
             "w_pool": [a.reshape(w_pool_mix.shape) for a in rows[6]], "b_in": rows[7]}
    g_s, d_s, nm_s, nv_s = ({k: r[j] for k, r in small.items()} for j in range(4))
    g_w_out, d_w_out, nm_w_out, nv_w_out = rows[5]
    g_b_ada, d_b_ada, nm_b_ada, nv_b_ada = rows[8]

    d_ada_local = lax.dynamic_slice_in_dim(d_ada_all.reshape(N_DEV, D_ADA), me * (D_ADA // N_DEV), D_ADA // N_DEV, axis=1)
    g_w_ada, d_w_ada, nm_w_ada, nv_w_ada = _ada_adamw(sc_all, d_ada_local, w_ada[0], m_w_ada[0], v_w_ada[0])

    def ordered(w_ada_, b_ada_, w_in_, w_out_, s):
        return (w_ada_[None], b_ada_, w_in_[None], s["b_in"], s["w_pool"], s["b_pool"], s["pool_scale"],
                w_out_[None], s["b_out"], s["ln_g"], s["ln_b"])

    return (loss, dx[None],
            *ordered(g_w_ada, g_b_ada, g_w_in, g_w_out, g_s),
            *ordered(d_w_ada, d_b_ada, d_w_in, d_w_out, d_s),
            *ordered(nm_w_ada, nm_b_ada, nm_w_in, nm_w_out, nm_s),
            *ordered(nv_w_ada, nv_b_ada, nv_w_in, nv_w_out, nv_s))
```

```python
import jax
import jax.numpy as jnp
from jax import lax
from jax.experimental import pallas as pl
from jax.experimental.pallas import tpu as pltpu

F32 = jnp.float32
BF16 = jnp.bfloat16

N_DEV = 8
D = 1024
N_HEADS = 8
HEAD_DIM = 64
D_ATT = 512
D_POOL = 512
POOL_WINDOWS = (2, 4, 8, 16)
GROUP_DIM = 128
HALO = 16
LANE = 128
BF16_TILE_ROWS = 16
D_IN = 3080
D_ADA = 3072
N_MAIN = 3072
OFF_P = 1536
COL_CHUNK = 512
Q_SCALE = 0.125
LN_EPS = 1e-5
ALPHA = 2.0 ** 0.25
L_CQ, L_CK, L_LSE = 64, 67, 70

ADAM_LR, ADAM_B1, ADAM_B2, ADAM_EPS, ADAM_WD, ADAM_STEP = 0.001, 0.9, 0.999, 1e-08, 0.01, 10
VMEM_LIMIT = 56 * 1024 * 1024

MESH = pl.DeviceIdType.MESH
ANY = pl.BlockSpec(memory_space=pl.ANY)


def _params(sem=None, vmem=VMEM_LIMIT):
    return pltpu.CompilerParams(dimension_semantics=sem, vmem_limit_bytes=vmem)


def _split3(a):
    hi = a.astype(BF16)
    r = a - hi.astype(F32)
    mid = r.astype(BF16)
    lo = (r - mid.astype(F32)).astype(BF16)
    return hi, mid, lo


def _dot(a, b):
    return jnp.dot(a, b, preferred_element_type=F32)


def _dot_nt(a, b):
    return lax.dot_general(a, b, (((1,), (1,)), ((), ())), preferred_element_type=F32)


def _dot_tn(a, b):
    return lax.dot_general(a, b, (((0,), (0,)), ((), ())), preferred_element_type=F32)


def _dot3(m01, a):
    hi, mid, lo = _split3(a)
    return _dot(m01, hi) + _dot(m01, mid) + _dot(m01, lo)


def _sigmoid(z):
    return 1.0 / (1.0 + jnp.exp(-z))


def _lanes(shape):
    return lax.broadcasted_iota(jnp.int32, shape, len(shape) - 1)


def _place3(lane, base, parts, other):
    out = other
    for j in range(3):
        out = jnp.where(lane == base + j, parts[j], out)
    return out


def _mesh_pos():
    return lax.axis_index("x"), lax.axis_index("y"), lax.axis_index("c")


def _dev_index(px, py, pc):
    return 4 * px + 2 * py + pc


N_GATHER_SEMS = 11


def _gather_stages(src_ref, out_ref, send_sems, recv_sems, local_sem):
    x, y, c = _mesh_pos()
    me, sibling = (x, y, c), (x, y, 1 - c)
    nbr_x, nbr_y, diag = (1 - x, y), (x, 1 - y), (1 - x, 1 - y)
    half = out_ref.shape[-1] // 2
    left, right = pl.ds(0, half), pl.ds(half, half)

    def copy(k, block, to, cols=None, src=None):
        slot = out_ref.at[_dev_index(*block)]
        if cols is not None:
            slot = slot.at[:, cols]
            src = src if src is None else src.at[:, cols]
        return pltpu.make_async_remote_copy(
            src_ref=slot if src is None else src, dst_ref=slot, send_sem=send_sems.at[k], recv_sem=recv_sems.at[k],
            device_id=to, device_id_type=MESH)

    mine = pltpu.make_async_copy(src_ref, out_ref.at[_dev_index(*me)], local_sem)
    first = [copy(0, me, sibling, src=src_ref),
             copy(1, me, (*nbr_x, c), cols=left, src=src_ref), copy(2, me, (*nbr_y, c), cols=right, src=src_ref),
             copy(9, me, (*nbr_x, c), cols=right, src=src_ref), copy(10, me, (*nbr_y, c), cols=left, src=src_ref)]
    relay = [(1, nbr_x, left, nbr_x), (2, nbr_y, right, nbr_y), (3, diag, left, nbr_y), (4, diag, right, nbr_x)]
    other_half = [(9, nbr_x, right, nbr_x), (10, nbr_y, left, nbr_y)]
    onward = [copy(3, (*nbr_x, c), (*nbr_y, c), cols=left), copy(4, (*nbr_y, c), (*nbr_x, c), cols=right)]
    passed = [copy(4 + k, (*block, c), sibling, cols=None if k < 3 else cols) for k, block, cols, _ in relay]

    def start():
        mine.start()
        for cp in first:
            cp.start()

    def arrived(item):
        k, block, cols, frm = item
        copy(k, (*block, c), (*frm, c), cols=cols).wait_recv()

    def relay_near():
        for j in (0, 1):
            arrived(relay[j])
            onward[j].start()
        for j in (0, 1):
            arrived(other_half[j])
            passed[j].start()

    def relay_far():
        for j in (2, 3):
            arrived(relay[j])
            passed[j].start()

    def from_sibling(items):
        for k, block, cols, _ in items:
            copy(4 + k, (*block, 1 - c), me, cols=None if k < 3 else cols).wait_recv()

    def finish_near():
        copy(0, sibling, me).wait_recv()
        from_sibling(relay[:2])
        mine.wait()

    def finish_far():
        from_sibling(relay[2:])
        for cp in first + onward + passed:
            cp.wait_send()

    return start, relay_near, relay_far, finish_near, finish_far


N_REDUCE_SEMS = 10
N_SMALL_SEMS = 4
N_ROWS_SEMS = 7


def _reduce_stages(ins, gs, r1, s2, r2, smalls, send_sems, recv_sems, rows=None, own=None):
    n = len(ins)
    x, y, c = _mesh_pos()
    me = _dev_index(x, y, c)
    sibling = (x, y, 1 - c)
    chips = [(x, y), (1 - x, y), (x, 1 - y), (1 - x, 1 - y)]
    peers = []
    for p in range(1, N_DEV):
        px, py, pc = (p >> 2) & 1, (p >> 1) & 1, p & 1
        peers.append((1 - x if px else x, 1 - y if py else y, 1 - c if pc else c))
    base_small = N_REDUCE_SEMS * n

    def remote(src, dst, k, to):
        return pltpu.make_async_remote_copy(src_ref=src, dst_ref=dst, send_sem=send_sems.at[k],
                                            recv_sem=recv_sems.at[k], device_id=to, device_id_type=MESH)

    def level1(a, q):
        return remote(ins[a].at[_dev_index(*chips[q], 1 - c)], r1[a].at[q], N_REDUCE_SEMS * a + q, sibling)

    def level2(a, k):
        half = ins[a].shape[-1] // 2
        left, right = pl.ds(0, half), pl.ds(half, half)
        nbr_x, nbr_y = (*chips[1], c), (*chips[2], c)
        src_slot, dst_slot, cols, to = [(0, 0, left, nbr_x), (1, 1, right, nbr_y), (2, 2, left, nbr_x),
                                        (2, 2, right, nbr_y), (0, 0, right, nbr_x), (1, 1, left, nbr_y)][k]
        return remote(s2[a].at[src_slot, :, cols], r2[a].at[dst_slot, :, cols], N_REDUCE_SEMS * a + 4 + k, to)

    to_sibling = [remote(sm[0], sm[2], base_small + 4 * i, sibling) for i, sm in enumerate(smalls)]
    to_chips = [[remote(sm[3], sm[4].at[j], base_small + 4 * i + 1 + j, (*chips[j + 1], c)) for j in range(3)]
                for i, sm in enumerate(smalls)]
    if rows is not None:
        rows_ref, land_ref, all_ref = rows
        base_rows = base_small + 4 * len(smalls)
        row_sends = [remote(rows_ref, land_ref.at[me], base_rows + k, to) for k, to in enumerate(peers)]

    order = (3, 1, 2, 0)

    def mine(a, q):
        buf, sems = own[a]
        return pltpu.make_async_copy(ins[a].at[_dev_index(*chips[q], c)], buf.at[q], sems.at[q])

    def start():
        for a in range(n):
            for q in order:
                level1(a, q).start()
            if own is not None:
                for q in order:
                    mine(a, q).start()
        for cp in to_sibling:
            cp.start()
        if rows is not None:
            for cp in row_sends:
                cp.start()
            land_ref[me] = rows_ref[...]

    def middle():
        for a in range(n):
            for q in order:
                level1(a, q).wait_recv()
                if own is None:
                    kept = ins[a][_dev_index(*chips[q], c)]
                else:
                    mine(a, q).wait()
                    kept = own[a][0][q]
                pair = kept.astype(F32) + r1[a][q].astype(F32)
                if q == 0:
                    gs[a][...] = pair
                else:
                    s2[a][q - 1] = pair.astype(BF16)
                    for k in ((0,), (1,), (2, 3))[q - 1]:
                        level2(a, k).start()
        for i, (small_ref, _, sm_sib, sm_chip, _) in enumerate(smalls):
            to_sibling[i].wait_recv()
            sm_chip[...] = small_ref[...] + sm_sib[...]
            for cp in to_chips[i]:
                cp.start()

    def fold():
        for a in range(n):
            half = ins[a].shape[-1] // 2
            level2(a, 3).wait_recv()
            s2[a][0, :, half:] = (s2[a][0, :, half:].astype(F32) + r2[a][2, :, half:].astype(F32)).astype(BF16)
            level2(a, 4).start()
            level2(a, 2).wait_recv()
            s2[a][1, :, :half] = (s2[a][1, :, :half].astype(F32) + r2[a][2, :, :half].astype(F32)).astype(BF16)
            level2(a, 5).start()

    def finish():
        for a in range(n):
            for k in (0, 1, 4, 5):
                level2(a, k).wait_recv()
            gs[a][...] = gs[a][...] + r2[a][0].astype(F32) + r2[a][1].astype(F32)
            for q in range(4):
                level1(a, q).wait_send()
            for k in range(6):
                level2(a, k).wait_send()
        for i, (_, total_ref, _, sm_chip, sm_recv) in enumerate(smalls):
            for cp in to_chips[i]:
                cp.wait_recv()
            total = None
            for ax in range(2):
                for ay in range(2):
                    dx, dy = x != ax, y != ay
                    term = jnp.where(dx, jnp.where(dy, sm_recv[2], sm_recv[0]), jnp.where(dy, sm_recv[1], sm_chip[...]))
                    total = term if total is None else total + term
            total_ref[...] = total
            for cp in [to_sibling[i]] + to_chips[i]:
                cp.wait_send()
        if rows is not None:
            for k, frm in enumerate(peers):
                remote(rows_ref, land_ref.at[_dev_index(*frm)], base_rows + k, frm).wait_recv()
            all_ref[...] = land_ref[...]
            for cp in row_sends:
                cp.wait_send()

    return start, middle, fold, finish


def _reduce_scratch(shard, smalls, rows=None):
    out = [pltpu.VMEM((lead,) + shard.shape[1:], BF16) for lead in (4, 3, 3)]
    for small in smalls:
        out += [pltpu.VMEM(small.shape, F32), pltpu.VMEM(small.shape, F32), pltpu.VMEM((3,) + small.shape, F32)]
    n_sems = N_REDUCE_SEMS + N_SMALL_SEMS * len(smalls)
    if rows is not None:
        out.append(pltpu.VMEM((N_DEV,) + rows.shape, F32))
        n_sems += N_ROWS_SEMS
    return out + [pltpu.SemaphoreType.DMA((n_sems,))] * 2


def _reduce_grads(gw_in, small, rows):
    def body(in_ref, small_ref, rows_ref, g_ref, total_ref, rows_all_ref,
             r1, s2, r2, sm_sib, sm_chip, sm_recv, rows_land, send_sems, recv_sems, kept, kept_sems):
        stages = _reduce_stages(
            [in_ref], [g_ref], [r1], [s2], [r2], [(small_ref, total_ref, sm_sib, sm_chip, sm_recv)],
            send_sems, recv_sems, rows=(rows_ref, rows_land, rows_all_ref), own=[(kept, kept_sems)])
        for stage in stages:
            stage()

    vmem = pl.BlockSpec(memory_space=pltpu.VMEM)
    return pl.pallas_call(
        body, name="reduce_grads",
        in_specs=[ANY, vmem, vmem], out_specs=[vmem, vmem, vmem],
        out_shape=[jax.ShapeDtypeStruct(gw_in.shape[1:], F32), jax.ShapeDtypeStruct(small.shape, F32),
                   jax.ShapeDtypeStruct((N_DEV,) + rows.shape, F32)],
        scratch_shapes=_reduce_scratch(gw_in, [small], rows)
        + [pltpu.VMEM((4,) + gw_in.shape[1:], BF16), pltpu.SemaphoreType.DMA((4,))],
        compiler_params=_params(),
    )(gw_in, small, rows)


def _dot3_rhs(a, b):
    a0, a1, a2 = _split3(a)
    b0, b1, b2 = _split3(b)
    return (_dot(a0, b0) + (_dot(a0, b1) + _dot(a1, b0))
            + (_dot(a0, b2) + _dot(a1, b1) + _dot(a2, b0)))


def _gather_and_ada(c, w_in_rows, w_ada):
    cols = w_ada.shape[1]
    shard = w_in_rows.shape[0]

    def body(c_ref, w_ref, wa_ref, w_main_hbm, w_f_ref, sc_ref, ada_ref,
             w_all_ref, w_f32, wm_buf, c_land, part, ada_land, send_sems, recv_sems, local_sem, x_send, x_recv, out_sems):
        x, y, cc = _mesh_pos()
        me = _dev_index(x, y, cc)
        peers = []
        for p in range(1, N_DEV):
            px, py, pc = (p >> 2) & 1, (p >> 1) & 1, p & 1
            peers.append((1 - x if px else x, 1 - y if py else y, 1 - cc if pc else cc))

        def remote(src, dst, k, to):
            return pltpu.make_async_remote_copy(src_ref=src, dst_ref=dst, send_sem=x_send.at[k], recv_sem=x_recv.at[k],
                                                device_id=to, device_id_type=MESH)

        c_sends = [remote(c_ref, c_land.at[me], k, to) for k, to in enumerate(peers)]
        for cp in c_sends:
            cp.start()
        start, relay_near, relay_far, finish_near, finish_far = _gather_stages(
            w_ref, w_all_ref, send_sems, recv_sems, local_sem.at[0])
        start()
        c_land[me] = c_ref[...]
        for k, frm in enumerate(peers):
            remote(c_ref, c_land.at[_dev_index(*frm)], k, frm).wait_recv()
        c_all = jnp.concatenate([c_land[b] for b in range(N_DEV)], axis=0)
        sc = c_all * _sigmoid(c_all)
        sc_ref[...] = sc
        rows = _dot3_rhs(sc, wa_ref[...])
        for b in range(N_DEV):
            part[b] = rows[b:b + 1, :]
        a_sends = [remote(part.at[_dev_index(*to)], ada_land.at[me], 7 + k, to) for k, to in enumerate(peers)]
        for cp in a_sends:
            cp.start()
        ada_land[me] = part[me]

        relay_near()
        finish_near()

        far_chip = 2 * (1 - x) + (1 - y)

        def stage(slots):
            for slot in slots:
                w_f32[slot * shard:(slot + 1) * shard, :] = w_all_ref[slot].astype(F32)

        def far_rows(k):
            main_row = lambda r: r if r < F_LO else r - N_HEADS
            first, last = 2 * shard * k, 2 * shard * (k + 1) - 1
            first = F_HI if F_LO <= first < F_HI else first
            last = F_LO - 1 if F_LO <= last < F_HI else last
            return (main_row(first) // BF16_TILE_ROWS * BF16_TILE_ROWS,
                    -(-(main_row(last) + 1) // BF16_TILE_ROWS) * BF16_TILE_ROWS)

        def near_rows(k):
            lo, hi = far_rows(k)
            return [(i, a, b) for i, (a, b) in enumerate(((0, lo), (hi, N_MAIN))) if a < b]

        def forget_is_far(k):
            return 2 * shard * k < F_HI and 2 * shard * (k + 1) > F_LO

        def main_copy(i, lo, hi):
            return pltpu.make_async_copy(wm_buf.at[lo:hi], w_main_hbm.at[lo:hi], out_sems.at[i])

        def write_main(i, lo, hi):
            if lo < min(hi, F_LO):
                wm_buf[lo:min(hi, F_LO), :] = w_f32[lo:min(hi, F_LO), :].astype(BF16)
            if max(lo, F_LO) < hi:
                wm_buf[max(lo, F_LO):hi, :] = w_f32[max(lo, F_LO) + N_HEADS:hi + N_HEADS, :].astype(BF16)
            main_copy(i, lo, hi).start()

        def write_forget():
            w_f_ref[...] = jnp.concatenate(
                [w_f32[F_LO:F_HI, :], jnp.zeros((LANE - N_HEADS, D), F32)], axis=0).astype(BF16)

        for k in range(N_DEV // 2):
            @pl.when(far_chip == k)
            def _(k=k):
                stage([slot for slot in range(N_DEV) if slot // 2 != k])
                for i, lo, hi in near_rows(k):
                    write_main(i, lo, hi)
                if not forget_is_far(k):
                    write_forget()

        relay_far()
        for k, frm in enumerate(peers):
            remote(part.at[0], ada_land.at[_dev_index(*frm)], 7 + k, frm).wait_recv()
        ada_ref[...] = ada_land[...]
        finish_far()
        for cp in c_sends + a_sends:
            cp.wait_send()

        for k in range(N_DEV // 2):
            @pl.when(far_chip == k)
            def _(k=k):
                stage([2 * k, 2 * k + 1])
                write_main(2, *far_rows(k))
                if forget_is_far(k):
                    write_forget()
                for i, lo, hi in near_rows(k):
                    main_copy(i, lo, hi).wait()
                main_copy(2, *far_rows(k)).wait()

    vmem = pl.BlockSpec(memory_space=pltpu.VMEM)
    return pl.pallas_call(
        body, name="gather_weights",
        in_specs=[vmem, ANY, vmem], out_specs=[ANY, vmem, vmem, vmem],
        out_shape=[jax.ShapeDtypeStruct((N_MAIN, D), BF16), jax.ShapeDtypeStruct((LANE, D), BF16),
                   jax.ShapeDtypeStruct((N_DEV, D), F32), jax.ShapeDtypeStruct((N_DEV, 1, cols), F32)],
        scratch_shapes=[pltpu.VMEM((N_DEV,) + w_in_rows.shape, BF16), pltpu.VMEM((D_IN, D), F32), pltpu.VMEM((N_MAIN, D), BF16),
                        pltpu.VMEM((N_DEV, 1, D), F32), pltpu.VMEM((N_DEV, 1, cols), F32), pltpu.VMEM((N_DEV, 1, cols), F32),
                        pltpu.SemaphoreType.DMA((N_GATHER_SEMS,)), pltpu.SemaphoreType.DMA((N_GATHER_SEMS,)),
                        pltpu.SemaphoreType.DMA((1,)),
                        pltpu.SemaphoreType.DMA((14,)), pltpu.SemaphoreType.DMA((14,)), pltpu.SemaphoreType.DMA((3,))],
        compiler_params=_params(),
    )(c, w_in_rows, w_ada)


def _inproj_forward(x, mod, w_main, w_f, b_main, b_f, tile):
    seq = x.shape[0]
    nt = seq // tile

    def body(x_ref, mod_ref, w_ref, wf_ref, b_ref, bf_ref,
             qp_ref, kp_ref, vp_ref, f_ref, p_ref, ga_ref, gp_ref, u_ref, carry_ref):
        i = pl.program_id(0)

        @pl.when(i == 0)
        def _():
            carry_ref[...] = jnp.zeros_like(carry_ref)

        u = x_ref[...] * mod_ref[0:1, :] + mod_ref[1:2, :]
        ub = u.astype(BF16)
        u_ref[...] = ub

        f = _dot_nt(ub, wf_ref[...]) + bf_ref[...]
        f_ref[...] = f
        lane = _lanes((tile, LANE))
        log_f = jnp.where(lane < N_HEADS, jnp.minimum(f, 0.0) - jnp.log(1.0 + jnp.exp(-jnp.abs(f))), 0.0)
        row = lax.broadcasted_iota(jnp.int32, (tile, tile), 0)
        col = lax.broadcasted_iota(jnp.int32, (tile, tile), 1)
        tri = (row >= col).astype(BF16)
        cum = _dot3(tri, log_f) + carry_ref[0:1, :]
        carry_ref[0:1, :] = cum[tile - 1:tile, :]
        cq = [part.astype(F32) for part in _split3(cum)]
        ck = [part.astype(F32) for part in _split3(-cum)]

        def proj(chunk):
            cols = pl.ds(chunk * COL_CHUNK, COL_CHUNK)
            return _dot_nt(ub, w_ref[cols, :]) + b_ref[:, cols]

        def head_tiles(r):
            for pair in range(N_HEADS // 2):
                both = r[:, pair * LANE:(pair + 1) * LANE]
                yield 2 * pair, both
                yield 2 * pair + 1, pltpu.roll(both, HEAD_DIM, 1)

        for h, val in head_tiles(proj(0)):
            extra = jnp.where((lane >= L_CK) & (lane < L_CK + 3), 1.0, 0.0)
            extra = _place3(lane, L_CQ, [part[:, h:h + 1] for part in cq], extra)
            qp_ref[h] = jnp.where(lane < HEAD_DIM, val * Q_SCALE, extra).astype(BF16)
        for h, val in head_tiles(proj(1)):
            ones = ((lane >= L_CQ) & (lane < L_CQ + 3)) | ((lane >= L_LSE) & (lane < L_LSE + 3))
            extra = _place3(lane, L_CK, [part[:, h:h + 1] for part in ck], jnp.where(ones, 1.0, 0.0))
            kp_ref[h] = jnp.where(lane < HEAD_DIM, val, extra).astype(BF16)
        for h, val in head_tiles(proj(2)):
            extra = jnp.where((lane >= HEAD_DIM) & (lane < HEAD_DIM + 3), -1.0, 0.0)
            vp_ref[h] = jnp.where(lane < HEAD_DIM, val, extra).astype(BF16)
        p_ref[...] = proj(3)
        ga_ref[...] = proj(4)
        gp_ref[...] = proj(5)

    head_block = pl.BlockSpec((N_HEADS, tile, LANE), lambda i: (0, i, 0))
    tok = lambda width: pl.BlockSpec((tile, width), lambda i: (i, 0))
    whole = lambda a: pl.BlockSpec(a.shape, lambda i: (0,) * a.ndim)
    padded = jax.ShapeDtypeStruct((N_HEADS, seq, LANE), BF16)
    half = jax.ShapeDtypeStruct((seq, D_ATT), F32)
    return pl.pallas_call(
        body, name="inproj_forward", grid=(nt,),
        in_specs=[tok(D), whole(mod), whole(w_main), whole(w_f), whole(b_main), whole(b_f)],
        out_specs=[head_block, head_block, head_block, tok(LANE), tok(D_POOL), tok(D_ATT), tok(D_POOL),
                   tok(D)],
        out_shape=[padded, padded, padded, jax.ShapeDtypeStruct((seq, LANE), F32), half, half, half,
                   jax.ShapeDtypeStruct((seq, D), BF16)],
        scratch_shapes=[pltpu.VMEM((8, LANE), F32)],
        compiler_params=_params(("arbitrary",)),
    )(x, mod, w_main, w_f, b_main, b_f)


def _attention_forward(qp, kp, vp, w_out, tile):
    seq = qp.shape[1]
    nb = seq // tile
    steps = (N_HEADS // 2) * nb

    def body(q_ref, k_ref, v_ref, wo_ref, att_ref, q2t_ref, wo_all_ref, s_a, s_b, m_ref, acc_ref,
             send_sems, recv_sems, local_sem):
        step = pl.program_id(0) * nb + pl.program_id(1)
        start, relay_near, relay_far, finish_near, finish_far = _gather_stages(
            wo_ref, wo_all_ref, send_sems, recv_sems, local_sem.at[0])
        pl.when(step == 0)(start)
        pl.when(step == steps // 4)(relay_near)
        pl.when(step == (3 * steps) // 4)(relay_far)

        i = pl.program_id(1)
        sub = lax.broadcasted_iota(jnp.int32, (LANE, tile), 0)
        row = lax.broadcasted_iota(jnp.int32, (tile, tile), 0)
        col = lax.broadcasted_iota(jnp.int32, (tile, tile), 1)
        q = [q_ref[0], q_ref[1]]

        def scores(buf, kb):
            rows = pl.ds(pl.multiple_of(kb * tile, tile), tile)
            for hh in range(2):
                buf[hh] = _dot_nt(k_ref[hh, rows, :], q[hh])

        def absorb(buf, kb, masked):
            rows = pl.ds(pl.multiple_of(kb * tile, tile), tile)
            for hh in range(2):
                m = m_ref[hh, 0:1, :]
                s = buf[hh]
                if masked:
                    s = jnp.where(row <= col, s, -1e30)
                m_new = jnp.maximum(m, jnp.max(s, axis=0, keepdims=True))
                p = jnp.exp(s - m_new).astype(BF16)
                acc_ref[hh] = jnp.exp(m - m_new) * acc_ref[hh] + _dot_tn(v_ref[hh, rows, :], p)
                m_ref[hh, 0:1, :] = m_new

        def two_blocks(j, _):
            scores(s_b, 2 * j + 1)
            absorb(s_a, 2 * j, False)
            scores(s_a, 2 * j + 2)
            absorb(s_b, 2 * j + 1, False)
            return 0

        def last_block():
            absorb(s_a, i, True)

        def last_two_blocks():
            scores(s_b, i)
            absorb(s_a, i - 1, False)
            absorb(s_b, i, True)

        scores(s_a, 0)
        m_ref[...] = jnp.full(m_ref.shape, -1e30, F32)
        acc_ref[...] = jnp.zeros_like(acc_ref)
        lax.fori_loop(0, i // 2, two_blocks, 0)
        lax.cond(i % 2 == 0, last_block, last_two_blocks)
        outs = []
        for hh in range(2):
            m, acc = m_ref[hh, 0:1, :], acc_ref[hh]
            l = -acc[HEAD_DIM:HEAD_DIM + 1, :]
            outs.append((acc / l)[:HEAD_DIM, :])
            neg_lse = [part.astype(F32) for part in _split3(-(m + jnp.log(l)))]
            q2t_ref[hh] = _place3(sub, L_LSE, neg_lse, q[hh].astype(F32).T).astype(BF16)
        att_ref[...] = jnp.concatenate(outs, axis=0).T
        @pl.when(step == steps - 1)
        def _():
            finish_near()
            finish_far()

    pair = pl.BlockSpec((2, tile, LANE), lambda hp, i: (hp, i, 0))
    full = pl.BlockSpec((2, seq, LANE), lambda hp, i: (hp, 0, 0))
    return pl.pallas_call(
        body, name="attention_forward", grid=(N_HEADS // 2, nb),
        in_specs=[pair, full, full, ANY],
        out_specs=[pl.BlockSpec((tile, LANE), lambda hp, i: (i, hp)),
                   pl.BlockSpec((2, LANE, tile), lambda hp, i: (hp, 0, i)), ANY],
        out_shape=[jax.ShapeDtypeStruct((seq, D_ATT), F32),
                   jax.ShapeDtypeStruct((N_HEADS, LANE, seq), BF16),
                   jax.ShapeDtypeStruct((N_DEV,) + w_out.shape, w_out.dtype)],
        scratch_shapes=[pltpu.VMEM((2, tile, tile), F32), pltpu.VMEM((2, tile, tile), F32),
                        pltpu.VMEM((2, 8, tile), F32), pltpu.VMEM((2, LANE, tile), F32),
                        pltpu.SemaphoreType.DMA((N_GATHER_SEMS,)), pltpu.SemaphoreType.DMA((N_GATHER_SEMS,)),
                        pltpu.SemaphoreType.DMA((1,))],
        compiler_params=_params(("arbitrary", "arbitrary")),
    )(qp, kp, vp, w_out)


def _window_sum(x, halo, window, transposed):
    tile = x.shape[0]

    def split_cat(a):
        hi = a.astype(BF16)
        return jnp.concatenate([hi, (a - hi.astype(F32)).astype(BF16)], axis=1)

    def fold(r):
        return r[:, :LANE] + r[:, LANE:]

    r = lax.broadcasted_iota(jnp.int32, (tile, tile), 0)
    c = lax.broadcasted_iota(jnp.int32, (tile, tile), 1)
    rh = lax.broadcasted_iota(jnp.int32, (HALO, HALO), 0)
    ch = lax.broadcasted_iota(jnp.int32, (HALO, HALO), 1)
    if not transposed:
        band = (c <= r) & (r - c < window)
        edge = (rh + HALO - ch) < window
    else:
        band = (r <= c) & (c - r < window)
        edge = (HALO + ch - rh) < window
    out = fold(_dot(band.astype(BF16), split_cat(x)))
    reach = fold(_dot(edge.astype(BF16), split_cat(halo)))
    if not transposed:
        return jnp.concatenate([out[:HALO] + reach, out[HALO:]], axis=0)
    return jnp.concatenate([out[:tile - HALO], out[tile - HALO:] + reach], axis=0)


def _silu_parts(g):
    sig = _sigmoid(g)
    return g * sig, sig * (1.0 + g * (1.0 - sig))


def _middle(x, tgt, att, g_att, g_pool, p, vecs, pool_vecs, w_out, w_pool, tile):
    seq = x.shape[0]
    nt = seq // tile
    halo_blocks = tile // HALO

    def body(x_ref, tgt_ref, att_ref, ga_ref, gp_ref, p_ref, ph_ref, vec_ref, pvec_ref, wo_ref, wp_ref,
             dxa_ref, do2_ref, dga_ref, dgp_ref, dpooled_ref, gwo_ref, dwp_ref, dvec_ref, dwo_ref, dpvec_ref):
        i = pl.program_id(0)

        @pl.when(i == 0)
        def _():
            dwo_ref[...] = jnp.zeros_like(dwo_ref)
            dwp_ref[...] = jnp.zeros_like(dwp_ref)
            dvec_ref[...] = jnp.zeros_like(dvec_ref)
            dpvec_ref[...] = jnp.zeros_like(dpvec_ref)

        gate, b_out, ln_g, ln_b = (vec_ref[k:k + 1, :] for k in range(4))
        b_pool, pool_scale = pvec_ref[0:1, :], pvec_ref[1:2, :]
        x = x_ref[...]
        p = p_ref[...]
        p_halo = ph_ref[...] * jnp.where(i > 0, 1.0, 0.0)
        pos = i * tile + lax.broadcasted_iota(jnp.int32, (tile, 1), 0) + 1

        pooled, mixed = [], []
        for g, window in enumerate(POOL_WINDOWS):
            cols = slice(g * GROUP_DIM, (g + 1) * GROUP_DIM)
            wsum = _window_sum(p[:, cols], p_halo[:, cols], window, False)
            count = jnp.minimum(pos, window).astype(F32)
            pooled.append(wsum / count - p[:, cols])
            mixed.append(_dot(pooled[g].astype(BF16), wp_ref[g]) + b_pool[:, cols])
        mixed = jnp.concatenate(mixed, axis=1)
        pool = mixed * pool_scale

        att = att_ref[...]
        g_att, g_pool = ga_ref[...], gp_ref[...]
        silu_a, dsilu_a = _silu_parts(g_att)
        silu_p, dsilu_p = _silu_parts(g_pool)
        y_in = jnp.concatenate([att * silu_a, pool * silu_p], axis=1)
        y = _dot(y_in.astype(BF16), wo_ref[...]) + b_out
        h = ALPHA * x + gate * y
        mu = jnp.mean(h, axis=1, keepdims=True)
        hc = h - mu
        var = jnp.mean(hc * hc, axis=1, keepdims=True)
        rstd = lax.rsqrt(var + LN_EPS)
        yhat = hc * rstd
        diff = yhat * ln_g + ln_b - tgt_ref[...]
        loss_rows = jnp.sum(diff * diff, axis=1, keepdims=True)
        d_out = diff * (1.0 / D)

        d_yhat = d_out * ln_g
        dh = rstd * (d_yhat - jnp.mean(d_yhat, axis=1, keepdims=True)
                     - yhat * jnp.mean(d_yhat * yhat, axis=1, keepdims=True))
        dxa_ref[...] = ALPHA * dh
        dy = dh * gate
        dyb = dy.astype(BF16)
        lane = _lanes((1, D))
        loss_row = jnp.where(lane == 0, (0.5 / D) * jnp.sum(loss_rows, axis=0, keepdims=True), 0.0)
        dvec_ref[5:6, :] += jnp.sum(dh * y, axis=0, keepdims=True)
        dvec_ref[0:1, :] += jnp.sum(dy, axis=0, keepdims=True)
        dvec_ref[1:2, :] += jnp.sum(d_out * yhat, axis=0, keepdims=True)
        dvec_ref[2:3, :] += jnp.sum(d_out, axis=0, keepdims=True)
        dvec_ref[4:5, :] += loss_row

        dwo_ref[...] += _dot(y_in.T.astype(BF16), dyb)
        d_yin = _dot_nt(dyb, wo_ref[...])
        d_a, d_pl = d_yin[:, :D_ATT], d_yin[:, D_ATT:]
        d_att = d_a * silu_a
        d_att_t = d_att.T
        prod_t = (d_att * att).T
        sub = lax.broadcasted_iota(jnp.int32, (HEAD_DIM, tile), 0)
        for h in range(N_HEADS):
            rows = slice(h * HEAD_DIM, (h + 1) * HEAD_DIM)
            delta = jnp.sum(prod_t[rows], axis=0, keepdims=True)
            extra = _place3(sub, 0, [part.astype(F32) for part in _split3(delta)], 0.0)
            do2_ref[h] = jnp.concatenate([d_att_t[rows], extra], axis=0).astype(BF16)
        dga_ref[...] = d_a * att * dsilu_a
        dgp_ref[...] = d_pl * pool * dsilu_p
        d_pool = d_pl * silu_p
        d_mixed = d_pool * pool_scale
        dpvec_ref[0:1, :] += jnp.sum(d_mixed, axis=0, keepdims=True)
        dpvec_ref[1:2, :] += jnp.sum(d_pool * mixed, axis=0, keepdims=True)
        d_pooled = []
        for g in range(len(POOL_WINDOWS)):
            cols = slice(g * GROUP_DIM, (g + 1) * GROUP_DIM)
            dmb = d_mixed[:, cols].astype(BF16)
            dwp_ref[g] += _dot(pooled[g].T.astype(BF16), dmb)
            d_pooled.append(_dot_nt(dmb, wp_ref[g]))
        dpooled_ref[...] = jnp.concatenate(d_pooled, axis=1)

        @pl.when(i == nt - 1)
        def _():
            gwo_ref[...] = dwo_ref[...].astype(BF16)
            dvec_ref[3:4, :] = jnp.concatenate([dpvec_ref[0:1, :], dpvec_ref[1:2, :]], axis=1)

    tok = lambda width: pl.BlockSpec((tile, width), lambda i: (i, 0))
    whole = lambda a: pl.BlockSpec(a.shape, lambda i: (0,) * a.ndim)
    halo = pl.BlockSpec((HALO, D_POOL), lambda i: (jnp.maximum(i * halo_blocks - 1, 0), 0))
    half = jax.ShapeDtypeStruct((seq, D_ATT), F32)
    outs = [jax.ShapeDtypeStruct((seq, D), F32), jax.ShapeDtypeStruct((N_HEADS, LANE, seq), BF16), half, half, half,
            jax.ShapeDtypeStruct(w_out.shape, BF16), jax.ShapeDtypeStruct(w_pool.shape, F32),
            jax.ShapeDtypeStruct(vecs.shape, F32)]
    return pl.pallas_call(
        body, name="middle", grid=(nt,),
        in_specs=[tok(D), tok(D), tok(D_ATT), tok(D_ATT), tok(D_POOL), tok(D_POOL), halo,
                  whole(vecs), whole(pool_vecs), whole(w_out), whole(w_pool)],
        out_specs=[tok(D), pl.BlockSpec((N_HEADS, LANE, tile), lambda i: (0, 0, i)),
                   tok(D_ATT), tok(D_POOL), tok(D_POOL),
                   whole(w_out), whole(w_pool), whole(vecs)],
        out_shape=outs,
        scratch_shapes=[pltpu.VMEM(w_out.shape, F32), pltpu.VMEM(pool_vecs.shape, F32)],
        compiler_params=_params(("arbitrary",)),
    )(x, tgt, att, g_att, g_pool, p, p, vecs, pool_vecs, w_out, w_pool)


def _attention_backward(q2t, kp, vp, do2t, gw_out, vecs, pool, tile):
    seq = kp.shape[1]
    nb = seq // tile
    last = N_HEADS // 2 - 1

    def body(qt_ref, k_ref, v_ref, dot_ref, gwo_hbm, vecs_hbm, pool_hbm,
             dq_ref, dk_ref, dv_ref, dcum_ref, g_out_ref, vecs_sum_ref, pool_sum_ref,
             dq_acc, dk_acc, dv_acc, gwo_ref, vecs_ref, pool_ref,
             r1, s2, r2, v_sib, v_chip, v_recv, p_sib, p_chip, p_recv, send_sems, recv_sems):
        hp = pl.program_id(0)
        start, middle, fold, finish = _reduce_stages(
            [gwo_ref], [g_out_ref], [r1], [s2], [r2],
            [(vecs_ref, vecs_sum_ref, v_sib, v_chip, v_recv), (pool_ref, pool_sum_ref, p_sib, p_chip, p_recv)],
            send_sems, recv_sems)

        @pl.when(hp == 0)
        def _():
            pltpu.sync_copy(gwo_hbm, gwo_ref)
            pltpu.sync_copy(vecs_hbm, vecs_ref)
            pltpu.sync_copy(pool_hbm, pool_ref)
            start()

        pl.when(hp == 1)(middle)
        pl.when(hp == 2)(fold)

        row = lax.broadcasted_iota(jnp.int32, (tile, tile), 0)
        col = lax.broadcasted_iota(jnp.int32, (tile, tile), 1)
        dq_acc[...] = jnp.zeros_like(dq_acc)

        def kv_block(kb, _):
            krows = pl.ds(pl.multiple_of(kb * tile, tile), tile)
            k = [k_ref[hh, krows, :] for hh in range(2)]
            v = [v_ref[hh, krows, :] for hh in range(2)]
            k_t = [k[hh].T for hh in range(2)]

            def q_block(qb, masked):
                qcols = pl.ds(pl.multiple_of(qb * tile, tile), tile)
                for hh in range(2):
                    q_t = qt_ref[hh, :, qcols]
                    do_t = dot_ref[hh, :, qcols]
                    s_t = _dot(k[hh], q_t)
                    if masked:
                        s_t = jnp.where(row <= col, s_t, -1e30)
                    p_t = jnp.exp(s_t)
                    ds_t = (p_t * _dot(v[hh], do_t)).astype(BF16)
                    dv_new = _dot_nt(do_t, p_t.astype(BF16))
                    dk_new = _dot_nt(q_t, ds_t)
                    if masked:
                        dv_acc[hh], dk_acc[hh] = dv_new, dk_new
                    else:
                        dv_acc[hh] += dv_new
                        dk_acc[hh] += dk_new
                    dq_acc[hh, :, qcols] += _dot(k_t[hh], ds_t)

            q_block(kb, True)

            def two_later_blocks(j, _):
                q_block(kb + 1 + 2 * j, False)
                q_block(kb + 2 + 2 * j, False)
                return 0

            later = nb - 1 - kb
            lax.fori_loop(0, later // 2, two_later_blocks, 0)
            pl.when(later % 2 == 1)(lambda: q_block(nb - 1, False))
            for hh in range(2):
                dk = dk_acc[hh]
                dk_ref[hh, :, krows] = dk.astype(BF16)
                dv_ref[hh, :, krows] = dv_acc[hh].astype(BF16)
                dcum_ref[hh, :, krows] = -dk[L_CK:L_CK + 1, :]
            return 0

        lax.fori_loop(0, nb, kv_block, 0)
        for hh in range(2):
            dq = dq_acc[hh]
            dcum_ref[hh] += dq[L_CQ:L_CQ + 1, :]
            dq_ref[hh] = (dq * Q_SCALE).astype(BF16)
        pl.when(hp == last)(finish)

    pair = pl.BlockSpec((2, seq, LANE), lambda hp: (hp, 0, 0))
    pair_t = pl.BlockSpec((2, LANE, seq), lambda hp: (hp, 0, 0))
    whole = lambda shape: pl.BlockSpec(shape, lambda hp: (0,) * len(shape))
    grad = jax.ShapeDtypeStruct((N_HEADS, LANE, seq), BF16)
    return pl.pallas_call(
        body, name="attention_backward", grid=(N_HEADS // 2,),
        in_specs=[pair_t, pair, pair, pair_t, ANY, ANY, ANY],
        out_specs=[pair_t, pair_t, pair_t, pl.BlockSpec((2, 1, seq), lambda hp: (hp, 0, 0)),
                   whole(gw_out.shape[1:]), whole(vecs.shape), whole(pool.shape)],
        out_shape=[grad, grad, grad, jax.ShapeDtypeStruct((N_HEADS, 1, seq), F32),
                   jax.ShapeDtypeStruct(gw_out.shape[1:], F32), jax.ShapeDtypeStruct(vecs.shape, F32),
                   jax.ShapeDtypeStruct(pool.shape, F32)],
        scratch_shapes=[pltpu.VMEM((2, LANE, seq), F32), pltpu.VMEM((2, LANE, tile), F32),
                        pltpu.VMEM((2, LANE, tile), F32), pltpu.VMEM(gw_out.shape, BF16),
                        pltpu.VMEM(vecs.shape, F32), pltpu.VMEM(pool.shape, F32)]
        + _reduce_scratch(gw_out, [vecs, pool]),
        compiler_params=_params(("arbitrary",)),
    )(q2t, kp, vp, do2t, gw_out, vecs, pool)


def _inproj_backward(dqp, dkp, dvp, d_cum, f, d_pooled, d_ga, d_gp, x, dxa, u, mod, w_main, w_f, tile):
    seq = x.shape[0]
    nt = seq // tile
    halo_blocks = tile // HALO

    def body(dq_ref, dk_ref, dv_ref, dcum_ref, f_ref, dpo_ref, dph_ref, dga_ref, dgp_ref, x_ref, dxa_ref, u_ref,
             mod_ref, w_ref, wf_ref,
             dx_ref, dproj_ref, dwf_ref, db_ref, dbf_ref, dmod_ref, carry_ref):
        step = pl.program_id(0)
        i = nt - 1 - step

        @pl.when(step == 0)
        def _():
            carry_ref[...] = jnp.zeros_like(carry_ref)
            dwf_ref[...] = jnp.zeros_like(dwf_ref)
            db_ref[...] = jnp.zeros_like(db_ref)
            dbf_ref[...] = jnp.zeros_like(dbf_ref)
            dmod_ref[...] = jnp.zeros_like(dmod_ref)

        ones = jnp.ones((8, tile), BF16)

        def emit(chunk, val):
            cols = pl.ds(chunk * COL_CHUNK, COL_CHUNK)
            db_ref[0:1, cols] += jnp.sum(val, axis=0, keepdims=True)
            vb = val.astype(BF16)
            dproj_ref[:, pl.ds((chunk - 3) * COL_CHUNK, COL_CHUNK)] = vb
            return _dot(vb, w_ref[cols, :])

        d_u = jnp.zeros((tile, D), F32)
        for chunk, ref in enumerate((dq_ref, dk_ref, dv_ref)):
            cols = pl.ds(chunk * COL_CHUNK, COL_CHUNK)
            val_t = ref[:, 0:HEAD_DIM, :].reshape(COL_CHUNK, tile)
            db_ref[:, cols] += _dot_nt(ones, val_t)
            d_u += _dot_tn(val_t, w_ref[cols, :])

        d_pooled = dpo_ref[...]
        d_halo = dph_ref[...] * jnp.where(i < nt - 1, 1.0, 0.0)
        pos = i * tile + lax.broadcasted_iota(jnp.int32, (tile, 1), 0) + 1
        d_p = []
        for g, window in enumerate(POOL_WINDOWS):
            cols = slice(g * GROUP_DIM, (g + 1) * GROUP_DIM)
            scaled = d_pooled[:, cols] / jnp.minimum(pos, window).astype(F32)
            d_p.append(_window_sum(scaled, d_halo[:, cols] * (1.0 / window), window, True) - d_pooled[:, cols])
        d_u += emit(3, jnp.concatenate(d_p, axis=1))
        d_u += emit(4, dga_ref[...])
        d_u += emit(5, dgp_ref[...])

        row = lax.broadcasted_iota(jnp.int32, (tile, tile), 0)
        col = lax.broadcasted_iota(jnp.int32, (tile, tile), 1)
        later = (row >= col).astype(BF16)
        d_logf = sum(_dot(part, later) for part in _split3(dcum_ref[:, 0, :])) + carry_ref[:, 0:1]
        carry_ref[:, 0:1] = d_logf[:, 0:1]
        d_f = d_logf * _sigmoid(-f_ref[...].T[0:N_HEADS, :])
        d_f = jnp.concatenate([d_f, jnp.zeros((LANE - N_HEADS, tile), F32)], axis=0)
        dbf_ref[...] += sum(_dot_nt(ones, part) for part in _split3(d_f))
        d_fb = d_f.astype(BF16)
        d_u += _dot_tn(d_fb, wf_ref[...])
        dwf_ref[...] += _dot(d_fb, u_ref[...])

        x = x_ref[...]
        dx_ref[...] = dxa_ref[...] + d_u * mod_ref[0:1, :]
        dmod_ref[0:1, :] += jnp.sum(d_u * x, axis=0, keepdims=True)
        dmod_ref[1:2, :] += jnp.sum(d_u, axis=0, keepdims=True)

    rev = lambda step: nt - 1 - step
    tok = lambda width: pl.BlockSpec((tile, width), lambda s: (rev(s), 0))
    head_block = pl.BlockSpec((N_HEADS, LANE, tile), lambda s: (0, 0, rev(s)))
    whole = lambda a: pl.BlockSpec(a.shape, lambda s: (0,) * a.ndim)
    halo = pl.BlockSpec((HALO, D_POOL), lambda s: (jnp.minimum((rev(s) + 1) * halo_blocks, seq // HALO - 1), 0))
    small = lambda width: jax.ShapeDtypeStruct((8, width), F32)
    n_rest = N_MAIN - OFF_P
    return pl.pallas_call(
        body, name="inproj_backward", grid=(nt,),
        in_specs=[head_block, head_block, head_block, pl.BlockSpec((N_HEADS, 1, tile), lambda s: (0, 0, rev(s))),
                  tok(LANE), tok(D_POOL), halo, tok(D_ATT), tok(D_POOL),
                  tok(D), tok(D), tok(D),
                  whole(mod), whole(w_main), whole(w_f)],
        out_specs=[tok(D), tok(n_rest), pl.BlockSpec((LANE, D), lambda s: (0, 0)),
                   pl.BlockSpec((8, N_MAIN), lambda s: (0, 0)), pl.BlockSpec((8, LANE), lambda s: (0, 0)),
                   pl.BlockSpec((8, D), lambda s: (0, 0))],
        out_shape=[jax.ShapeDtypeStruct((seq, D), F32), jax.ShapeDtypeStruct((seq, n_rest), BF16),
                   jax.ShapeDtypeStruct((LANE, D), F32), small(N_MAIN), small(LANE), small(D)],
        scratch_shapes=[pltpu.VMEM((8, LANE), F32)],
        compiler_params=_params(("arbitrary",)),
    )(dqp, dkp, dvp, d_cum, f, d_pooled, d_pooled, d_ga, d_gp, x, dxa, u, mod, w_main, w_f)


def _weight_grads(dq_t, dk_t, dv_t, dw_f, dproj, u, k_tile, half, reduce_in=None):
    seq = u.shape[0]
    nk = seq // k_tile
    rows = N_HEADS * HEAD_DIM
    width = D // 2
    shard = D_IN // N_DEV

    def body(dq_ref, dk_ref, dv_ref, dwf_ref, dp_ref, u_ref, *rest):
        k = pl.program_id(0)
        if reduce_in is None:
            out_ref, acc_ref = rest
        else:
            other_hbm, out_ref, sum_ref, acc_ref, r1, s2, r2, send_sems, recv_sems, kept, kept_sems = rest
            start, middle, fold, finish = _reduce_stages(
                [other_hbm], [sum_ref], [r1], [s2], [r2], [], send_sems, recv_sems, own=[(kept, kept_sems)])
            pl.when(k == 0)(start)
            pl.when(k == 1)(middle)
            pl.when(k == nk // 2)(fold)

        @pl.when(k == 0)
        def _():
            acc_ref[...] = jnp.zeros_like(acc_ref)

        tokens = u_ref[...]
        for j, ref in enumerate((dq_ref, dk_ref, dv_ref)):
            acc_ref[pl.ds(j * rows, rows), :] += _dot(ref[...].reshape(rows, k_tile), tokens)
        for j in range(dproj.shape[1] // COL_CHUNK):
            cols = pl.ds(j * COL_CHUNK, COL_CHUNK)
            acc_ref[pl.ds(F_HI + j * COL_CHUNK, COL_CHUNK), :] += _dot_tn(dp_ref[:, cols], tokens)

        @pl.when(k == nk - 1)
        def _():
            acc_ref[F_LO:F_HI, :] = dwf_ref[0:N_HEADS, :]
            for slot in range(N_DEV):
                out_ref[slot] = acc_ref[slot * shard:(slot + 1) * shard, :].astype(BF16)
            if reduce_in is not None:
                finish()

    heads = pl.BlockSpec((N_HEADS, HEAD_DIM, k_tile), lambda k: (0, 0, k))
    slots = jax.ShapeDtypeStruct((N_DEV, shard, width), BF16)
    in_specs = [heads, heads, heads, pl.BlockSpec((dw_f.shape[0], width), lambda k: (0, half)),
                pl.BlockSpec((k_tile, dproj.shape[1]), lambda k: (k, 0)), pl.BlockSpec((k_tile, width), lambda k: (k, half))]
    out_specs = [pl.BlockSpec((N_DEV, shard, width), lambda k: (0, 0, 0))]
    out_shape, scratch, operands = [slots], [pltpu.VMEM((D_IN, width), F32)], [dq_t, dk_t, dv_t, dw_f, dproj, u]
    if reduce_in is not None:
        assert nk >= 4 and reduce_in.shape == slots.shape
        in_specs.append(ANY)
        operands.append(reduce_in)
        out_specs.append(pl.BlockSpec((shard, width), lambda k: (0, 0)))
        out_shape.append(jax.ShapeDtypeStruct((shard, width), F32))
        scratch += _reduce_scratch(slots, []) + [pltpu.VMEM((4, shard, width), BF16), pltpu.SemaphoreType.DMA((4,))]
    return pl.pallas_call(
        body, name="weight_grads_" + ("left", "right")[half], grid=(nk,),
        in_specs=in_specs, out_specs=out_specs, out_shape=out_shape, scratch_shapes=scratch,
        compiler_params=_params(("arbitrary",)),
    )(*operands)


def _adamw(w, g, m, v):
    m = ADAM_B1 * m + (1.0 - ADAM_B1) * g
    v = ADAM_B2 * v + (1.0 - ADAM_B2) * (g * g)
    m_hat = m / (1.0 - ADAM_B1 ** ADAM_STEP)
    v_hat = v / (1.0 - ADAM_B2 ** ADAM_STEP)
    delta = -ADAM_LR * (m_hat / (jnp.sqrt(v_hat) + ADAM_EPS) + ADAM_WD * w)
    return delta, m, v


SUBLANES = 8


def _adamw_packed(g_parts, w, m, v, name, chunks=4):
    n_parts, (rows, part_cols) = len(g_parts), g_parts[0].shape
    cols = n_parts * part_cols
    per_row, per_part = cols // LANE, part_cols // LANE
    assert part_cols % LANE == 0 and per_row == SUBLANES and w.shape == (rows * per_row, LANE)
    step = -(-rows // (chunks * SUBLANES)) * SUBLANES
    bounds = [(r0, min(r0 + step, rows)) for r0 in range(0, rows, step)]

    def body(*refs):
        g_hbm, (w_hbm, m_hbm, v_hbm, og_hbm, od_hbm, om_hbm, ov_hbm) = refs[:n_parts], refs[n_parts:n_parts + 7]
        g_buf, in_buf, out_buf, in_sems, out_sems = refs[n_parts + 7:]

        def copies_in(c):
            r0, r1 = bounds[c]
            packed = slice(r0 * per_row, r1 * per_row)
            return [pltpu.make_async_copy(g_hbm[p].at[r0:r1], g_buf.at[p, r0:r1], in_sems.at[c, 3 + p])
                    for p in range(n_parts)] + [
                pltpu.make_async_copy(src.at[packed], in_buf.at[i, packed], in_sems.at[c, i])
                for i, src in enumerate((w_hbm, m_hbm, v_hbm))]

        def copies_out(c):
            r0, r1 = bounds[c]
            packed = slice(r0 * per_row, r1 * per_row)
            return [pltpu.make_async_copy(out_buf.at[i, packed], dst.at[packed], out_sems.at[c, i])
                    for i, dst in enumerate((og_hbm, od_hbm, om_hbm, ov_hbm))]

        for c in range(len(bounds)):
            for cp in copies_in(c):
                cp.start()
        for c, (r0, r1) in enumerate(bounds):
            for cp in copies_in(c):
                cp.wait()
            for j in range(per_row):
                lanes = pl.ds(r0 * per_row + j, r1 - r0, stride=per_row)
                g_part = g_buf[j // per_part, r0:r1, (j % per_part) * LANE:(j % per_part + 1) * LANE]
                results = _adamw(in_buf[0, lanes, :], g_part, in_buf[1, lanes, :], in_buf[2, lanes, :])
                for i, val in enumerate((g_part,) + results):
                    out_buf[i, lanes, :] = val
            for cp in copies_out(c):
                cp.start()
        for c in range(len(bounds)):
            for cp in copies_out(c):
                cp.wait()

    shape = jax.ShapeDtypeStruct(w.shape, F32)
    return pl.pallas_call(
        body, name=name,
        in_specs=[ANY] * (n_parts + 3), out_specs=[ANY] * 4, out_shape=[shape] * 4,
        scratch_shapes=[pltpu.VMEM((n_parts, rows, part_cols), F32), pltpu.VMEM((3,) + w.shape, F32),
                        pltpu.VMEM((4,) + w.shape, F32),
                        pltpu.SemaphoreType.DMA((len(bounds), 3 + n_parts)), pltpu.SemaphoreType.DMA((len(bounds), 4))],
        compiler_params=_params(),
    )(*g_parts, w, m, v)


def _ada_adamw(sc_all, d_ada, w, m, v, chunks=4):
    rows, cols = w.shape
    step, sub = rows // chunks, 32
    assert rows % chunks == 0 and step % LANE == 0 and step % sub == 0

    def body(sc_ref, d_ref, w_hbm, m_hbm, v_hbm, og_hbm, od_hbm, om_hbm, ov_hbm, in_buf, out_buf, in_sems, out_sems):
        def copies_in(c):
            part = slice(c * step, (c + 1) * step)
            return [pltpu.make_async_copy(src.at[part], in_buf.at[i, part], in_sems.at[c, i])
                    for i, src in enumerate((w_hbm, m_hbm, v_hbm))]

        def copies_out(c):
            part = slice(c * step, (c + 1) * step)
            return [pltpu.make_async_copy(out_buf.at[i, part], dst.at[part], out_sems.at[c, i])
                    for i, dst in enumerate((og_hbm, od_hbm, om_hbm, ov_hbm))]

        for c in range(chunks):
            for cp in copies_in(c):
                cp.start()
        for c in range(chunks):
            sc_t = sc_ref[:, c * step:(c + 1) * step].T
            for cp in copies_in(c):
                cp.wait()
            for r0 in range(0, step, sub):
                part = slice(c * step + r0, c * step + r0 + sub)
                g = sc_t[r0:r0 + sub, 0:1] * d_ref[0:1, :]
                for b in range(1, N_DEV):
                    g = g + sc_t[r0:r0 + sub, b:b + 1] * d_ref[b:b + 1, :]
                results = _adamw(in_buf[0, part, :], g, in_buf[1, part, :], in_buf[2, part, :])
                for i, val in enumerate((g,) + results):
                    out_buf[i, part, :] = val
            for cp in copies_out(c):
                cp.start()
        for c in range(chunks):
            for cp in copies_out(c):
                cp.wait()

    in_vmem = pl.BlockSpec(memory_space=pltpu.VMEM)
    shape = jax.ShapeDtypeStruct(w.shape, F32)
    return pl.pallas_call(
        body, name="ada_adamw",
        in_specs=[in_vmem, in_vmem, ANY, ANY, ANY], out_specs=[ANY] * 4, out_shape=[shape] * 4,
        scratch_shapes=[pltpu.VMEM((3,) + w.shape, F32), pltpu.VMEM((4,) + w.shape, F32),
                        pltpu.SemaphoreType.DMA((chunks, 3)), pltpu.SemaphoreType.DMA((chunks, 4))],
        compiler_params=_params(),
    )(sc_all, d_ada, w, m, v)


F_LO, F_HI = 3 * D_ATT, 3 * D_ATT + N_HEADS


def _split_forget(a, axis):
    idx = lambda lo, hi: tuple(slice(lo, hi) if d == axis else slice(None) for d in range(a.ndim))
    pad = [(0, LANE - N_HEADS) if d == axis else (0, 0) for d in range(a.ndim)]
    return jnp.concatenate([a[idx(0, F_LO)], a[idx(F_HI, D_IN)]], axis=axis), jnp.pad(a[idx(F_LO, F_HI)], pad)


def _join_forget(main, f, axis):
    idx = lambda lo, hi: tuple(slice(lo, hi) if d == axis else slice(None) for d in range(main.ndim))
    return jnp.concatenate([main[idx(0, F_LO)], f[idx(0, N_HEADS)], main[idx(F_LO, N_MAIN)]], axis=axis)


def _adamw_small(grad_rows, row_params, whole_params, summed_params, scalar_at):
    n_row, n_whole, n_sum = len(row_params), len(whole_params), len(summed_params)
    n = n_row + n_whole + n_sum

    def body(g_ref, *refs):
        n_in = 3 * n_row + 4 * (n_whole + n_sum)
        ins, outs = list(refs[:n_in]), refs[n_in:]
        for i in range(n):
            if i < n_row:
                row, lo, hi = row_params[i][:3]
                g = g_ref[row:row + 1, lo:hi]
            elif i < n_row + n_whole:
                g = ins.pop(0)[...]
            else:
                parts = ins.pop(0)
                g = parts[0]
                for k in range(1, N_DEV):
                    g = g + parts[k]
            w, m, v = (ins.pop(0)[...] for _ in range(3))
            outs[4 * i][...] = g
            outs[4 * i + 1][...], outs[4 * i + 2][...], outs[4 * i + 3][...] = _adamw(w, g, m, v)
        row, lane = scalar_at
        outs[4 * n][...] = g_ref[row:row + 1, lane:lane + 1]

    shapes = [p[3] for p in row_params] + [p[1] for p in whole_params] + [p[1] for p in summed_params]
    operands = [a for p in row_params for a in p[3:]] + [a for p in whole_params + summed_params for a in p]
    flat = pl.pallas_call(
        body, name="adamw_small",
        out_shape=[jax.ShapeDtypeStruct(w.shape, F32) for w in shapes for _ in range(4)]
        + [jax.ShapeDtypeStruct((1, 1), F32)],
        compiler_params=_params(),
    )(grad_rows, *operands)
    return [flat[4 * i:4 * i + 4] for i in range(n)], flat[4 * n].reshape(())


def kernel(x, c, w_ada, b_ada, w_in, b_in, w_pool_mix, b_pool_mix, pool_scale, w_out, b_out, ln_g, ln_b, loss_target, m_w_ada, m_b_ada, m_w_in, m_b_in, m_w_pool_mix, m_b_pool_mix, m_pool_scale, m_w_out, m_b_out, m_ln_g, m_ln_b, v_w_ada, v_b_ada, v_w_in, v_b_in, v_w_pool_mix, v_b_pool_mix, v_pool_scale, v_w_out, v_b_out, v_ln_g, v_ln_b):
    seq = x.shape[1]
    tile = min(256, seq)
    attn_tile = min(512, max(128, seq // 4))
    me = _dev_index(*_mesh_pos())
    x2, tgt = x[0], loss_target[0]

    rows_of = lambda a: jnp.swapaxes(a, 1, 2)[0]
    w_main, w_f, sc_all, ada_mine = _gather_and_ada(c, rows_of(w_in).astype(BF16), w_ada[0])
    ada = ada_mine.reshape(1, D_ADA) + b_ada
    shift, scale, gate = ada[:, 0:D], ada[:, D:2 * D], ada[:, 2 * D:]
    mod = jnp.concatenate([1.0 + scale, shift, jnp.zeros((6, D), F32)], axis=0)
    b_main, b_f = _split_forget(b_in, 1)

    qp, kp, vp, f, p, g_att, g_pool, u = _inproj_forward(x2, mod, w_main, w_f, b_main, b_f, tile)
    att, q2t, w_out_g = _attention_forward(qp, kp, vp, w_out[0].astype(BF16), attn_tile)

    vecs = jnp.concatenate([gate, b_out, ln_g, ln_b, jnp.zeros((4, D), F32)], axis=0)
    pool_vecs = jnp.concatenate([b_pool_mix.reshape(1, D_POOL), pool_scale, jnp.zeros((6, D_POOL), F32)], axis=0)
    dxa, do2, d_ga, d_gp, d_pooled, gw_out, dw_pool, dvec = _middle(
        x2, tgt, att, g_att, g_pool, p, vecs, pool_vecs, w_out_g.reshape(D, D), w_pool_mix[0].astype(BF16), tile)

    pool_rows = w_pool_mix.shape[1] * GROUP_DIM
    dqp, dkp, dvp, d_cum, g_out, dvec_sum, dw_pool_sum = _attention_backward(
        q2t, kp, vp, do2, gw_out.reshape(N_DEV, D // N_DEV, D), dvec, dw_pool.reshape(pool_rows, GROUP_DIM), attn_tile)
    dx, dproj, dw_f, db_main, db_f, dmod = _inproj_backward(
        dqp, dkp, dvp, d_cum, f, d_pooled, d_ga, d_gp, x2, dxa, u, mod, w_main, w_f, tile)
    (gw_left,) = _weight_grads(dqp, dkp, dvp, dw_f, dproj, u, min(512, seq), 0)
    gw_right, g_in_left = _weight_grads(dqp, dkp, dvp, dw_f, dproj, u, min(512, seq), 1, reduce_in=gw_left)
    d_ada = jnp.concatenate([dmod[1:2], dmod[0:1], dvec[5:6]], axis=1)
    g_in_right, g_b_in, d_ada_all = _reduce_grads(gw_right, _join_forget(db_main[0:1], db_f[0:1], 1), d_ada)

    packed = lambda a: jnp.transpose(a.reshape(SUBLANES, LANE, -1), (2, 0, 1)).reshape(-1, LANE)
    outs_in = _adamw_packed((g_in_left, g_in_right), packed(w_in), packed(m_w_in), packed(v_w_in), "adamw_w_in")
    g_w_in, d_w_in, nm_w_in, nv_w_in = (
        jnp.transpose(a.reshape(-1, SUBLANES, LANE), (1, 2, 0)).reshape(D, -1) for a in outs_in)
    flat_pool = lambda a: a.reshape(1, D_POOL)
    pool_2d = lambda a: a.reshape(pool_rows, GROUP_DIM)
    rows, loss = _adamw_small(
        dvec_sum,
        [(0, 0, D, b_out, m_b_out, v_b_out), (1, 0, D, ln_g, m_ln_g, v_ln_g), (2, 0, D, ln_b, m_ln_b, v_ln_b),
         (3, 0, D_POOL, flat_pool(b_pool_mix), flat_pool(m_b_pool_mix), flat_pool(v_b_pool_mix)),
         (3, D_POOL, 2 * D_POOL, pool_scale, m_pool_scale, v_pool_scale)],
        [(g_out, w_out[0], m_w_out[0], v_w_out[0]),
         (dw_pool_sum, pool_2d(w_pool_mix), pool_2d(m_w_pool_mix), pool_2d(v_w_pool_mix)),
         (g_b_in, b_in, m_b_in, v_b_in)],
        [(d_ada_all, b_ada, m_b_ada, v_b_ada)],
        scalar_at=(4, 0))
    small = {"b_out": rows[0], "ln_g": rows[1], "ln_b": rows[2],
             "b_pool": [a.reshape(b_pool_mix.shape) for a in rows[3]], "pool_scale": rows[4],
             "w_pool": [a.reshape(w_pool_mix.shape) for a in rows[6]], "b_in": rows[7]}
    g_s, d_s, nm_s, nv_s = ({k: r[j] for k, r in small.items()} for j in range(4))
    g_w_out, d_w_out, nm_w_out, nv_w_out = rows[5]
    g_b_ada, d_b_ada, nm_b_ada, nv_b_ada = rows[8]

    d_ada_local = lax.dynamic_slice_in_dim(d_ada_all.reshape(N_DEV, D_ADA), me * (D_ADA // N_DEV), D_ADA // N_DEV, axis=1)
    g_w_ada, d_w_ada, nm_w_ada, nv_w_ada = _ada_adamw(sc_all, d_ada_local, w_ada[0], m_w_ada[0], v_w_ada[0])

    def ordered(w_ada_, b_ada_, w_in_, w_out_, s):
        return (w_ada_[None], b_ada_, w_in_[None], s["b_in"], s["w_pool"], s["b_pool"], s["pool_scale"],
                w_out_[None], s["b_out"], s["ln_g"], s["ln_b"])

    return (loss, dx[None],
            *ordered(g_w_ada, g_b_ada, g_w_in, g_w_out, g_s),
            *ordered(d_w_ada, d_b_ada, d_w_in, d_w_out, d_s),
            *ordered(nm_w_ada, nm_b_ada, nm_w_in, nm_w_out, nm_s),
            *ordered(nv_w_ada, nv_b_ada, nv_w_in, nv_w_out, nv_s))
```

```python
import jax
import jax.numpy as jnp
from jax import lax
from jax.experimental import pallas as pl
from jax.experimental.pallas import tpu as pltpu

F32 = jnp.float32
BF16 = jnp.bfloat16

N_DEV = 8
D = 1024
N_HEADS = 8
HEAD_DIM = 64
D_ATT = 512
D_POOL = 512
POOL_WINDOWS = (2, 4, 8, 16)
GROUP_DIM = 128
HALO = 16
LANE = 128
BF16_TILE_ROWS = 16
D_IN = 3080
D_ADA = 3072
N_MAIN = 3072
OFF_P = 1536
COL_CHUNK = 512
Q_SCALE = 0.125
LN_EPS = 1e-5
ALPHA = 2.0 ** 0.25
L_CQ, L_CK, L_LSE = 64, 67, 70

ADAM_LR, ADAM_B1, ADAM_B2, ADAM_EPS, ADAM_WD, ADAM_STEP = 0.001, 0.9, 0.999, 1e-08, 0.01, 10
VMEM_LIMIT = 56 * 1024 * 1024

MESH = pl.DeviceIdType.MESH
ANY = pl.BlockSpec(memory_space=pl.ANY)


def _params(sem=None, vmem=VMEM_LIMIT):
    return pltpu.CompilerParams(dimension_semantics=sem, vmem_limit_bytes=vmem)


def _split3(a):
    hi = a.astype(BF16)
    r = a - hi.astype(F32)
    mid = r.astype(BF16)
    lo = (r - mid.astype(F32)).astype(BF16)
    return hi, mid, lo


def _dot(a, b):
    return jnp.dot(a, b, preferred_element_type=F32)


def _dot_nt(a, b):
    return lax.dot_general(a, b, (((1,), (1,)), ((), ())), preferred_element_type=F32)


def _dot_tn(a, b):
    return lax.dot_general(a, b, (((0,), (0,)), ((), ())), preferred_element_type=F32)


def _dot3(m01, a):
    hi, mid, lo = _split3(a)
    return _dot(m01, hi) + _dot(m01, mid) + _dot(m01, lo)


def _sigmoid(z):
    return 1.0 / (1.0 + jnp.exp(-z))


def _lanes(shape):
    return lax.broadcasted_iota(jnp.int32, shape, len(shape) - 1)


def _place3(lane, base, parts, other):
    out = other
    for j in range(3):
        out = jnp.where(lane == base + j, parts[j], out)
    return out


def _mesh_pos():
    return lax.axis_index("x"), lax.axis_index("y"), lax.axis_index("c")


def _dev_index(px, py, pc):
    return 4 * px + 2 * py + pc


N_GATHER_SEMS = 11


def _gather_stages(src_ref, out_ref, send_sems, recv_sems, local_sem):
    x, y, c = _mesh_pos()
    me, sibling = (x, y, c), (x, y, 1 - c)
    nbr_x, nbr_y, diag = (1 - x, y), (x, 1 - y), (1 - x, 1 - y)
    half = out_ref.shape[-1] // 2
    left, right = pl.ds(0, half), pl.ds(half, half)

    def copy(k, block, to, cols=None, src=None):
        slot = out_ref.at[_dev_index(*block)]
        if cols is not None:
            slot = slot.at[:, cols]
            src = src if src is None else src.at[:, cols]
        return pltpu.make_async_remote_copy(
            src_ref=slot if src is None else src, dst_ref=slot, send_sem=send_sems.at[k], recv_sem=recv_sems.at[k],
            device_id=to, device_id_type=MESH)

    mine = pltpu.make_async_copy(src_ref, out_ref.at[_dev_index(*me)], local_sem)
    first = [copy(0, me, sibling, src=src_ref),
             copy(1, me, (*nbr_x, c), cols=left, src=src_ref), copy(2, me, (*nbr_y, c), cols=right, src=src_ref),
             copy(9, me, (*nbr_x, c), cols=right, src=src_ref), copy(10, me, (*nbr_y, c), cols=left, src=src_ref)]
    relay = [(1, nbr_x, left, nbr_x), (2, nbr_y, right, nbr_y), (3, diag, left, nbr_y), (4, diag, right, nbr_x)]
    other_half = [(9, nbr_x, right, nbr_x), (10, nbr_y, left, nbr_y)]
    onward = [copy(3, (*nbr_x, c), (*nbr_y, c), cols=left), copy(4, (*nbr_y, c), (*nbr_x, c), cols=right)]
    passed = [copy(4 + k, (*block, c), sibling, cols=None if k < 3 else cols) for k, block, cols, _ in relay]

    def start():
        mine.start()
        for cp in first:
            cp.start()

    def arrived(item):
        k, block, cols, frm = item
        copy(k, (*block, c), (*frm, c), cols=cols).wait_recv()

    def relay_near():
        for j in (0, 1):
            arrived(relay[j])
            onward[j].start()
        for j in (0, 1):
            arrived(other_half[j])
            passed[j].start()

    def relay_far():
        for j in (2, 3):
            arrived(relay[j])
            passed[j].start()

    def from_sibling(items):
        for k, block, cols, _ in items:
            copy(4 + k, (*block, 1 - c), me, cols=None if k < 3 else cols).wait_recv()

    def finish_near():
        copy(0, sibling, me).wait_recv()
        from_sibling(relay[:2])
        mine.wait()

    def finish_far():
        from_sibling(relay[2:])
        for cp in first + onward + passed:
            cp.wait_send()

    return start, relay_near, relay_far, finish_near, finish_far


N_REDUCE_SEMS = 10
N_SMALL_SEMS = 4
N_ROWS_SEMS = 7


def _reduce_stages(ins, gs, r1, s2, r2, smalls, send_sems, recv_sems, rows=None, own=None):
    n = len(ins)
    x, y, c = _mesh_pos()
    me = _dev_index(x, y, c)
    sibling = (x, y, 1 - c)
    chips = [(x, y), (1 - x, y), (x, 1 - y), (1 - x, 1 - y)]
    peers = []
    for p in range(1, N_DEV):
        px, py, pc = (p >> 2) & 1, (p >> 1) & 1, p & 1
        peers.append((1 - x if px else x, 1 - y if py else y, 1 - c if pc else c))
    base_small = N_REDUCE_SEMS * n

    def remote(src, dst, k, to):
        return pltpu.make_async_remote_copy(src_ref=src, dst_ref=dst, send_sem=send_sems.at[k],
                                            recv_sem=recv_sems.at[k], device_id=to, device_id_type=MESH)

    def level1(a, q):
        return remote(ins[a].at[_dev_index(*chips[q], 1 - c)], r1[a].at[q], N_REDUCE_SEMS * a + q, sibling)

    def level2(a, k):
        half = ins[a].shape[-1] // 2
        left, right = pl.ds(0, half), pl.ds(half, half)
        nbr_x, nbr_y = (*chips[1], c), (*chips[2], c)
        src_slot, dst_slot, cols, to = [(0, 0, left, nbr_x), (1, 1, right, nbr_y), (2, 2, left, nbr_x),
                                        (2, 2, right, nbr_y), (0, 0, right, nbr_x), (1, 1, left, nbr_y)][k]
        return remote(s2[a].at[src_slot, :, cols], r2[a].at[dst_slot, :, cols], N_REDUCE_SEMS * a + 4 + k, to)

    to_sibling = [remote(sm[0], sm[2], base_small + 4 * i, sibling) for i, sm in enumerate(smalls)]
    to_chips = [[remote(sm[3], sm[4].at[j], base_small + 4 * i + 1 + j, (*chips[j + 1], c)) for j in range(3)]
                for i, sm in enumerate(smalls)]
    if rows is not None:
        rows_ref, land_ref, all_ref = rows
        base_rows = base_small + 4 * len(smalls)
        row_sends = [remote(rows_ref, land_ref.at[me], base_rows + k, to) for k, to in enumerate(peers)]

    order = (3, 1, 2, 0)

    def mine(a, q):
        buf, sems = own[a]
        return pltpu.make_async_copy(ins[a].at[_dev_index(*chips[q], c)], buf.at[q], sems.at[q])

    def start():
        for a in range(n):
            for q in order:
                level1(a, q).start()
            if own is not None:
                for q in order:
                    mine(a, q).start()
        for cp in to_sibling:
            cp.start()
        if rows is not None:
            for cp in row_sends:
                cp.start()
            land_ref[me] = rows_ref[...]

    def middle():
        for a in range(n):
            for q in order:
                level1(a, q).wait_recv()
                if own is None:
                    kept = ins[a][_dev_index(*chips[q], c)]
                else:
                    mine(a, q).wait()
                    kept = own[a][0][q]
                pair = kept.astype(F32) + r1[a][q].astype(F32)
                if q == 0:
                    gs[a][...] = pair
                else:
                    s2[a][q - 1] = pair.astype(BF16)
                    for k in ((0,), (1,), (2, 3))[q - 1]:
                        level2(a, k).start()
        for i, (small_ref, _, sm_sib, sm_chip, _) in enumerate(smalls):
            to_sibling[i].wait_recv()
            sm_chip[...] = small_ref[...] + sm_sib[...]
            for cp in to_chips[i]:
                cp.start()

    def fold():
        for a in range(n):
            half = ins[a].shape[-1] // 2
            level2(a, 3).wait_recv()
            s2[a][0, :, half:] = (s2[a][0, :, half:].astype(F32) + r2[a][2, :, half:].astype(F32)).astype(BF16)
            level2(a, 4).start()
            level2(a, 2).wait_recv()
            s2[a][1, :, :half] = (s2[a][1, :, :half].astype(F32) + r2[a][2, :, :half].astype(F32)).astype(BF16)
            level2(a, 5).start()

    def finish():
        for a in range(n):
            for k in (0, 1, 4, 5):
                level2(a, k).wait_recv()
            gs[a][...] = gs[a][...] + r2[a][0].astype(F32) + r2[a][1].astype(F32)
            for q in range(4):
                level1(a, q).wait_send()
            for k in range(6):
                level2(a, k).wait_send()
        for i, (_, total_ref, _, sm_chip, sm_recv) in enumerate(smalls):
            for cp in to_chips[i]:
                cp.wait_recv()
            total = None
            for ax in range(2):
                for ay in range(2):
                    dx, dy = x != ax, y != ay
                    term = jnp.where(dx, jnp.where(dy, sm_recv[2], sm_recv[0]), jnp.where(dy, sm_recv[1], sm_chip[...]))
                    total = term if total is None else total + term
            total_ref[...] = total
            for cp in [to_sibling[i]] + to_chips[i]:
                cp.wait_send()
        if rows is not None:
            for k, frm in enumerate(peers):
                remote(rows_ref, land_ref.at[_dev_index(*frm)], base_rows + k, frm).wait_recv()
            all_ref[...] = land_ref[...]
            for cp in row_sends:
                cp.wait_send()

    return start, middle, fold, finish


def _reduce_scratch(shard, smalls, rows=None):
    out = [pltpu.VMEM((lead,) + shard.shape[1:], BF16) for lead in (4, 3, 3)]
    for small in smalls:
        out += [pltpu.VMEM(small.shape, F32), pltpu.VMEM(small.shape, F32), pltpu.VMEM((3,) + small.shape, F32)]
    n_sems = N_REDUCE_SEMS + N_SMALL_SEMS * len(smalls)
    if rows is not None:
        out.append(pltpu.VMEM((N_DEV,) + rows.shape, F32))
        n_sems += N_ROWS_SEMS
    return out + [pltpu.SemaphoreType.DMA((n_sems,))] * 2


def _reduce_grads(gw_in, small, rows):
    def body(in_ref, small_ref, rows_ref, g_ref, total_ref, rows_all_ref,
             r1, s2, r2, sm_sib, sm_chip, sm_recv, rows_land, send_sems, recv_sems, kept, kept_sems):
        stages = _reduce_stages(
            [in_ref], [g_ref], [r1], [s2], [r2], [(small_ref, total_ref, sm_sib, sm_chip, sm_recv)],
            send_sems, recv_sems, rows=(rows_ref, rows_land, rows_all_ref), own=[(kept, kept_sems)])
        for stage in stages:
            stage()

    vmem = pl.BlockSpec(memory_space=pltpu.VMEM)
    return pl.pallas_call(
        body, name="reduce_grads",
        in_specs=[ANY, vmem, vmem], out_specs=[vmem, vmem, vmem],
        out_shape=[jax.ShapeDtypeStruct(gw_in.shape[1:], F32), jax.ShapeDtypeStruct(small.shape, F32),
                   jax.ShapeDtypeStruct((N_DEV,) + rows.shape, F32)],
        scratch_shapes=_reduce_scratch(gw_in, [small], rows)
        + [pltpu.VMEM((4,) + gw_in.shape[1:], BF16), pltpu.SemaphoreType.DMA((4,))],
        compiler_params=_params(),
    )(gw_in, small, rows)


def _dot3_rhs(a, b):
    a0, a1, a2 = _split3(a)
    b0, b1, b2 = _split3(b)
    return (_dot(a0, b0) + (_dot(a0, b1) + _dot(a1, b0))
            + (_dot(a0, b2) + _dot(a1, b1) + _dot(a2, b0)))


def _gather_and_ada(c, w_in_rows, w_ada):
    cols = w_ada.shape[1]
    shard = w_in_rows.shape[0]

    def body(c_ref, w_ref, wa_ref, w_main_hbm, w_f_ref, sc_ref, ada_ref,
             w_all_ref, w_f32, wm_buf, c_land, part, ada_land, send_sems, recv_sems, local_sem, x_send, x_recv, out_sems):
        x, y, cc = _mesh_pos()
        me = _dev_index(x, y, cc)
        peers = []
        for p in range(1, N_DEV):
            px, py, pc = (p >> 2) & 1, (p >> 1) & 1, p & 1
            peers.append((1 - x if px else x, 1 - y if py else y, 1 - cc if pc else cc))

        def remote(src, dst, k, to):
            return pltpu.make_async_remote_copy(src_ref=src, dst_ref=dst, send_sem=x_send.at[k], recv_sem=x_recv.at[k],
                                                device_id=to, device_id_type=MESH)

        c_sends = [remote(c_ref, c_land.at[me], k, to) for k, to in enumerate(peers)]
        for cp in c_sends:
            cp.start()
        start, relay_near, relay_far, finish_near, finish_far = _gather_stages(
            w_ref, w_all_ref, send_sems, recv_sems, local_sem.at[0])
        start()
        c_land[me] = c_ref[...]
        for k, frm in enumerate(peers):
            remote(c_ref, c_land.at[_dev_index(*frm)], k, frm).wait_recv()
        c_all = jnp.concatenate([c_land[b] for b in range(N_DEV)], axis=0)
        sc = c_all * _sigmoid(c_all)
        sc_ref[...] = sc
        rows = _dot3_rhs(sc, wa_ref[...])
        for b in range(N_DEV):
            part[b] = rows[b:b + 1, :]
        a_sends = [remote(part.at[_dev_index(*to)], ada_land.at[me], 7 + k, to) for k, to in enumerate(peers)]
        for cp in a_sends:
            cp.start()
        ada_land[me] = part[me]

        relay_near()
        finish_near()

        far_chip = 2 * (1 - x) + (1 - y)

        def stage(slots):
            for slot in slots:
                w_f32[slot * shard:(slot + 1) * shard, :] = w_all_ref[slot].astype(F32)

        def far_rows(k):
            main_row = lambda r: r if r < F_LO else r - N_HEADS
            first, last = 2 * shard * k, 2 * shard * (k + 1) - 1
            first = F_HI if F_LO <= first < F_HI else first
            last = F_LO - 1 if F_LO <= last < F_HI else last
            return (main_row(first) // BF16_TILE_ROWS * BF16_TILE_ROWS,
                    -(-(main_row(last) + 1) // BF16_TILE_ROWS) * BF16_TILE_ROWS)

        def near_rows(k):
            lo, hi = far_rows(k)
            return [(i, a, b) for i, (a, b) in enumerate(((0, lo), (hi, N_MAIN))) if a < b]

        def forget_is_far(k):
            return 2 * shard * k < F_HI and 2 * shard * (k + 1) > F_LO

        def main_copy(i, lo, hi):
            return pltpu.make_async_copy(wm_buf.at[lo:hi], w_main_hbm.at[lo:hi], out_sems.at[i])

        def write_main(i, lo, hi):
            if lo < min(hi, F_LO):
                wm_buf[lo:min(hi, F_LO), :] = w_f32[lo:min(hi, F_LO), :].astype(BF16)
            if max(lo, F_LO) < hi:
                wm_buf[max(lo, F_LO):hi, :] = w_f32[max(lo, F_LO) + N_HEADS:hi + N_HEADS, :].astype(BF16)
            main_copy(i, lo, hi).start()

        def write_forget():
            w_f_ref[...] = jnp.concatenate(
                [w_f32[F_LO:F_HI, :], jnp.zeros((LANE - N_HEADS, D), F32)], axis=0).astype(BF16)

        for k in range(N_DEV // 2):
            @pl.when(far_chip == k)
            def _(k=k):
                stage([slot for slot in range(N_DEV) if slot // 2 != k])
                for i, lo, hi in near_rows(k):
                    write_main(i, lo, hi)
                if not forget_is_far(k):
                    write_forget()

        relay_far()
        for k, frm in enumerate(peers):
            remote(part.at[0], ada_land.at[_dev_index(*frm)], 7 + k, frm).wait_recv()
        ada_ref[...] = ada_land[...]
        finish_far()
        for cp in c_sends + a_sends:
            cp.wait_send()

        for k in range(N_DEV // 2):
            @pl.when(far_chip == k)
            def _(k=k):
                stage([2 * k, 2 * k + 1])
                write_main(2, *far_rows(k))
                if forget_is_far(k):
                    write_forget()
                for i, lo, hi in near_rows(k):
                    main_copy(i, lo, hi).wait()
                main_copy(2, *far_rows(k)).wait()

    vmem = pl.BlockSpec(memory_space=pltpu.VMEM)
    return pl.pallas_call(
        body, name="gather_weights",
        in_specs=[vmem, ANY, vmem], out_specs=[ANY, vmem, vmem, vmem],
        out_shape=[jax.ShapeDtypeStruct((N_MAIN, D), BF16), jax.ShapeDtypeStruct((LANE, D), BF16),
                   jax.ShapeDtypeStruct((N_DEV, D), F32), jax.ShapeDtypeStruct((N_DEV, 1, cols), F32)],
        scratch_shapes=[pltpu.VMEM((N_DEV,) + w_in_rows.shape, BF16), pltpu.VMEM((D_IN, D), F32), pltpu.VMEM((N_MAIN, D), BF16),
                        pltpu.VMEM((N_DEV, 1, D), F32), pltpu.VMEM((N_DEV, 1, cols), F32), pltpu.VMEM((N_DEV, 1, cols), F32),
                        pltpu.SemaphoreType.DMA((N_GATHER_SEMS,)), pltpu.SemaphoreType.DMA((N_GATHER_SEMS,)),
                        pltpu.SemaphoreType.DMA((1,)),
                        pltpu.SemaphoreType.DMA((14,)), pltpu.SemaphoreType.DMA((14,)), pltpu.SemaphoreType.DMA((3,))],
        compiler_params=_params(),
    )(c, w_in_rows, w_ada)


def _inproj_forward(x, mod, w_main, w_f, b_main, b_f, tile):
    seq = x.shape[0]
    nt = seq // tile

    def body(x_ref, mod_ref, w_ref, wf_ref, b_ref, bf_ref,
             qp_ref, kp_ref, vp_ref, f_ref, p_ref, ga_ref, gp_ref, u_ref, carry_ref):
        i = pl.program_id(0)

        @pl.when(i == 0)
        def _():
            carry_ref[...] = jnp.zeros_like(carry_ref)

        u = x_ref[...] * mod_ref[0:1, :] + mod_ref[1:2, :]
        ub = u.astype(BF16)
        u_ref[...] = ub

        f = _dot_nt(ub, wf_ref[...]) + bf_ref[...]
        f_ref[...] = f
        lane = _lanes((tile, LANE))
        log_f = jnp.where(lane < N_HEADS, jnp.minimum(f, 0.0) - jnp.log(1.0 + jnp.exp(-jnp.abs(f))), 0.0)
        row = lax.broadcasted_iota(jnp.int32, (tile, tile), 0)
        col = lax.broadcasted_iota(jnp.int32, (tile, tile), 1)
        tri = (row >= col).astype(BF16)
        cum = _dot3(tri, log_f) + carry_ref[0:1, :]
        carry_ref[0:1, :] = cum[tile - 1:tile, :]
        cq = [part.astype(F32) for part in _split3(cum)]
        ck = [part.astype(F32) for part in _split3(-cum)]

        def proj(chunk):
            cols = pl.ds(chunk * COL_CHUNK, COL_CHUNK)
            return _dot_nt(ub, w_ref[cols, :]) + b_ref[:, cols]

        def head_tiles(r):
            for pair in range(N_HEADS // 2):
                both = r[:, pair * LANE:(pair + 1) * LANE]
                yield 2 * pair, both
                yield 2 * pair + 1, pltpu.roll(both, HEAD_DIM, 1)

        for h, val in head_tiles(proj(0)):
            extra = jnp.where((lane >= L_CK) & (lane < L_CK + 3), 1.0, 0.0)
            extra = _place3(lane, L_CQ, [part[:, h:h + 1] for part in cq], extra)
            qp_ref[h] = jnp.where(lane < HEAD_DIM, val * Q_SCALE, extra).astype(BF16)
        for h, val in head_tiles(proj(1)):
            ones = ((lane >= L_CQ) & (lane < L_CQ + 3)) | ((lane >= L_LSE) & (lane < L_LSE + 3))
            extra = _place3(lane, L_CK, [part[:, h:h + 1] for part in ck], jnp.where(ones, 1.0, 0.0))
            kp_ref[h] = jnp.where(lane < HEAD_DIM, val, extra).astype(BF16)
        for h, val in head_tiles(proj(2)):
            extra = jnp.where((lane >= HEAD_DIM) & (lane < HEAD_DIM + 3), -1.0, 0.0)
            vp_ref[h] = jnp.where(lane < HEAD_DIM, val, extra).astype(BF16)
        p_ref[...] = proj(3)
        ga_ref[...] = proj(4)
        gp_ref[...] = proj(5)

    head_block = pl.BlockSpec((N_HEADS, tile, LANE), lambda i: (0, i, 0))
    tok = lambda width: pl.BlockSpec((tile, width), lambda i: (i, 0))
    whole = lambda a: pl.BlockSpec(a.shape, lambda i: (0,) * a.ndim)
    padded = jax.ShapeDtypeStruct((N_HEADS, seq, LANE), BF16)
    half = jax.ShapeDtypeStruct((seq, D_ATT), F32)
    return pl.pallas_call(
        body, name="inproj_forward", grid=(nt,),
        in_specs=[tok(D), whole(mod), whole(w_main), whole(w_f), whole(b_main), whole(b_f)],
        out_specs=[head_block, head_block, head_block, tok(LANE), tok(D_POOL), tok(D_ATT), tok(D_POOL),
                   tok(D)],
        out_shape=[padded, padded, padded, jax.ShapeDtypeStruct((seq, LANE), F32), half, half, half,
                   jax.ShapeDtypeStruct((seq, D), BF16)],
        scratch_shapes=[pltpu.VMEM((8, LANE), F32)],
        compiler_params=_params(("arbitrary",)),
    )(x, mod, w_main, w_f, b_main, b_f)


def _attention_forward(qp, kp, vp, w_out, tile):
    seq = qp.shape[1]
    nb = seq // tile
    steps = (N_HEADS // 2) * nb

    def body(q_ref, k_ref, v_ref, wo_ref, att_ref, q2t_ref, wo_all_ref, s_a, s_b, m_ref, acc_ref,
             send_sems, recv_sems, local_sem):
        step = pl.program_id(0) * nb + pl.program_id(1)
        start, relay_near, relay_far, finish_near, finish_far = _gather_stages(
            wo_ref, wo_all_ref, send_sems, recv_sems, local_sem.at[0])
        pl.when(step == 0)(start)
        pl.when(step == steps // 4)(relay_near)
        pl.when(step == (3 * steps) // 4)(relay_far)

        i = pl.program_id(1)
        sub = lax.broadcasted_iota(jnp.int32, (LANE, tile), 0)
        row = lax.broadcasted_iota(jnp.int32, (tile, tile), 0)
        col = lax.broadcasted_iota(jnp.int32, (tile, tile), 1)
        q = [q_ref[0], q_ref[1]]

        def scores(buf, kb):
            rows = pl.ds(pl.multiple_of(kb * tile, tile), tile)
            for hh in range(2):
                buf[hh] = _dot_nt(k_ref[hh, rows, :], q[hh])

        def absorb(buf, kb, masked):
            rows = pl.ds(pl.multiple_of(kb * tile, tile), tile)
            for hh in range(2):
                m = m_ref[hh, 0:1, :]
                s = buf[hh]
                if masked:
                    s = jnp.where(row <= col, s, -1e30)
                m_new = jnp.maximum(m, jnp.max(s, axis=0, keepdims=True))
                p = jnp.exp(s - m_new).astype(BF16)
                acc_ref[hh] = jnp.exp(m - m_new) * acc_ref[hh] + _dot_tn(v_ref[hh, rows, :], p)
                m_ref[hh, 0:1, :] = m_new

        def two_blocks(j, _):
            scores(s_b, 2 * j + 1)
            absorb(s_a, 2 * j, False)
            scores(s_a, 2 * j + 2)
            absorb(s_b, 2 * j + 1, False)
            return 0

        def last_block():
            absorb(s_a, i, True)

        def last_two_blocks():
            scores(s_b, i)
            absorb(s_a, i - 1, False)
            absorb(s_b, i, True)

        scores(s_a, 0)
        m_ref[...] = jnp.full(m_ref.shape, -1e30, F32)
        acc_ref[...] = jnp.zeros_like(acc_ref)
        lax.fori_loop(0, i // 2, two_blocks, 0)
        lax.cond(i % 2 == 0, last_block, last_two_blocks)
        outs = []
        for hh in range(2):
            m, acc = m_ref[hh, 0:1, :], acc_ref[hh]
            l = -acc[HEAD_DIM:HEAD_DIM + 1, :]
            outs.append((acc / l)[:HEAD_DIM, :])
            neg_lse = [part.astype(F32) for part in _split3(-(m + jnp.log(l)))]
            q2t_ref[hh] = _place3(sub, L_LSE, neg_lse, q[hh].astype(F32).T).astype(BF16)
        att_ref[...] = jnp.concatenate(outs, axis=0).T
        @pl.when(step == steps - 1)
        def _():
            finish_near()
            finish_far()

    pair = pl.BlockSpec((2, tile, LANE), lambda hp, i: (hp, i, 0))
    full = pl.BlockSpec((2, seq, LANE), lambda hp, i: (hp, 0, 0))
    return pl.pallas_call(
        body, name="attention_forward", grid=(N_HEADS // 2, nb),
        in_specs=[pair, full, full, ANY],
        out_specs=[pl.BlockSpec((tile, LANE), lambda hp, i: (i, hp)),
                   pl.BlockSpec((2, LANE, tile), lambda hp, i: (hp, 0, i)), ANY],
        out_shape=[jax.ShapeDtypeStruct((seq, D_ATT), F32),
                   jax.ShapeDtypeStruct((N_HEADS, LANE, seq), BF16),
                   jax.ShapeDtypeStruct((N_DEV,) + w_out.shape, w_out.dtype)],
        scratch_shapes=[pltpu.VMEM((2, tile, tile), F32), pltpu.VMEM((2, tile, tile), F32),
                        pltpu.VMEM((2, 8, tile), F32), pltpu.VMEM((2, LANE, tile), F32),
                        pltpu.SemaphoreType.DMA((N_GATHER_SEMS,)), pltpu.SemaphoreType.DMA((N_GATHER_SEMS,)),
                        pltpu.SemaphoreType.DMA((1,))],
        compiler_params=_params(("arbitrary", "arbitrary")),
    )(qp, kp, vp, w_out)


def _window_sum(x, halo, window, transposed):
    tile = x.shape[0]

    def split_cat(a):
        hi = a.astype(BF16)
        return jnp.concatenate([hi, (a - hi.astype(F32)).astype(BF16)], axis=1)

    def fold(r):
        return r[:, :LANE] + r[:, LANE:]

    r = lax.broadcasted_iota(jnp.int32, (tile, tile), 0)
    c = lax.broadcasted_iota(jnp.int32, (tile, tile), 1)
    rh = lax.broadcasted_iota(jnp.int32, (HALO, HALO), 0)
    ch = lax.broadcasted_iota(jnp.int32, (HALO, HALO), 1)
    if not transposed:
        band = (c <= r) & (r - c < window)
        edge = (rh + HALO - ch) < window
    else:
        band = (r <= c) & (c - r < window)
        edge = (HALO + ch - rh) < window
    out = fold(_dot(band.astype(BF16), split_cat(x)))
    reach = fold(_dot(edge.astype(BF16), split_cat(halo)))
    if not transposed:
        return jnp.concatenate([out[:HALO] + reach, out[HALO:]], axis=0)
    return jnp.concatenate([out[:tile - HALO], out[tile - HALO:] + reach], axis=0)


def _silu_parts(g):
    sig = _sigmoid(g)
    return g * sig, sig * (1.0 + g * (1.0 - sig))


def _middle(x, tgt, att, g_att, g_pool, p, vecs, pool_vecs, w_out, w_pool, tile):
    seq = x.shape[0]
    nt = seq // tile
    halo_blocks = tile // HALO

    def body(x_ref, tgt_ref, att_ref, ga_ref, gp_ref, p_ref, ph_ref, vec_ref, pvec_ref, wo_ref, wp_ref,
             dxa_ref, do2_ref, dga_ref, dgp_ref, dpooled_ref, gwo_ref, dwp_ref, dvec_ref, dwo_ref, dpvec_ref):
        i = pl.program_id(0)

        @pl.when(i == 0)
        def _():
            dwo_ref[...] = jnp.zeros_like(dwo_ref)
            dwp_ref[...] = jnp.zeros_like(dwp_ref)
            dvec_ref[...] = jnp.zeros_like(dvec_ref)
            dpvec_ref[...] = jnp.zeros_like(dpvec_ref)

        gate, b_out, ln_g, ln_b = (vec_ref[k:k + 1, :] for k in range(4))
        b_pool, pool_scale = pvec_ref[0:1, :], pvec_ref[1:2, :]
        x = x_ref[...]
        p = p_ref[...]
        p_halo = ph_ref[...] * jnp.where(i > 0, 1.0, 0.0)
        pos = i * tile + lax.broadcasted_iota(jnp.int32, (tile, 1), 0) + 1

        pooled, mixed = [], []
        for g, window in enumerate(POOL_WINDOWS):
            cols = slice(g * GROUP_DIM, (g + 1) * GROUP_DIM)
            wsum = _window_sum(p[:, cols], p_halo[:, cols], window, False)
            count = jnp.minimum(pos, window).astype(F32)
            pooled.append(wsum / count - p[:, cols])
            mixed.append(_dot(pooled[g].astype(BF16), wp_ref[g]) + b_pool[:, cols])
        mixed = jnp.concatenate(mixed, axis=1)
        pool = mixed * pool_scale

        att = att_ref[...]
        g_att, g_pool = ga_ref[...], gp_ref[...]
        silu_a, dsilu_a = _silu_parts(g_att)
        silu_p, dsilu_p = _silu_parts(g_pool)
        y_in = jnp.concatenate([att * silu_a, pool * silu_p], axis=1)
        y = _dot(y_in.astype(BF16), wo_ref[...]) + b_out
        h = ALPHA * x + gate * y
        mu = jnp.mean(h, axis=1, keepdims=True)
        hc = h - mu
        var = jnp.mean(hc * hc, axis=1, keepdims=True)
        rstd = lax.rsqrt(var + LN_EPS)
        yhat = hc * rstd
        diff = yhat * ln_g + ln_b - tgt_ref[...]
        loss_rows = jnp.sum(diff * diff, axis=1, keepdims=True)
        d_out = diff * (1.0 / D)

        d_yhat = d_out * ln_g
        dh = rstd * (d_yhat - jnp.mean(d_yhat, axis=1, keepdims=True)
                     - yhat * jnp.mean(d_yhat * yhat, axis=1, keepdims=True))
        dxa_ref[...] = ALPHA * dh
        dy = dh * gate
        dyb = dy.astype(BF16)
        lane = _lanes((1, D))
        loss_row = jnp.where(lane == 0, (0.5 / D) * jnp.sum(loss_rows, axis=0, keepdims=True), 0.0)
        dvec_ref[5:6, :] += jnp.sum(dh * y, axis=0, keepdims=True)
        dvec_ref[0:1, :] += jnp.sum(dy, axis=0, keepdims=True)
        dvec_ref[1:2, :] += jnp.sum(d_out * yhat, axis=0, keepdims=True)
        dvec_ref[2:3, :] += jnp.sum(d_out, axis=0, keepdims=True)
        dvec_ref[4:5, :] += loss_row

        dwo_ref[...] += _dot(y_in.T.astype(BF16), dyb)
        d_yin = _dot_nt(dyb, wo_ref[...])
        d_a, d_pl = d_yin[:, :D_ATT], d_yin[:, D_ATT:]
        d_att = d_a * silu_a
        d_att_t = d_att.T
        prod_t = (d_att * att).T
        sub = lax.broadcasted_iota(jnp.int32, (HEAD_DIM, tile), 0)
        for h in range(N_HEADS):
            rows = slice(h * HEAD_DIM, (h + 1) * HEAD_DIM)
            delta = jnp.sum(prod_t[rows], axis=0, keepdims=True)
            extra = _place3(sub, 0, [part.astype(F32) for part in _split3(delta)], 0.0)
            do2_ref[h] = jnp.concatenate([d_att_t[rows], extra], axis=0).astype(BF16)
        dga_ref[...] = d_a * att * dsilu_a
        dgp_ref[...] = d_pl * pool * dsilu_p
        d_pool = d_pl * silu_p
        d_mixed = d_pool * pool_scale
        dpvec_ref[0:1, :] += jnp.sum(d_mixed, axis=0, keepdims=True)
        dpvec_ref[1:2, :] += jnp.sum(d_pool * mixed, axis=0, keepdims=True)
        d_pooled = []
        for g in range(len(POOL_WINDOWS)):
            cols = slice(g * GROUP_DIM, (g + 1) * GROUP_DIM)
            dmb = d_mixed[:, cols].astype(BF16)
            dwp_ref[g] += _dot(pooled[g].T.astype(BF16), dmb)
            d_pooled.append(_dot_nt(dmb, wp_ref[g]))
        dpooled_ref[...] = jnp.concatenate(d_pooled, axis=1)

        @pl.when(i == nt - 1)
        def _():
            gwo_ref[...] = dwo_ref[...].astype(BF16)
            dvec_ref[3:4, :] = jnp.concatenate([dpvec_ref[0:1, :], dpvec_ref[1:2, :]], axis=1)

    tok = lambda width: pl.BlockSpec((tile, width), lambda i: (i, 0))
    whole = lambda a: pl.BlockSpec(a.shape, lambda i: (0,) * a.ndim)
    halo = pl.BlockSpec((HALO, D_POOL), lambda i: (jnp.maximum(i * halo_blocks - 1, 0), 0))
    half = jax.ShapeDtypeStruct((seq, D_ATT), F32)
    outs = [jax.ShapeDtypeStruct((seq, D), F32), jax.ShapeDtypeStruct((N_HEADS, LANE, seq), BF16), half, half, half,
            jax.ShapeDtypeStruct(w_out.shape, BF16), jax.ShapeDtypeStruct(w_pool.shape, F32),
            jax.ShapeDtypeStruct(vecs.shape, F32)]
    return pl.pallas_call(
        body, name="middle", grid=(nt,),
        in_specs=[tok(D), tok(D), tok(D_ATT), tok(D_ATT), tok(D_POOL), tok(D_POOL), halo,
                  whole(vecs), whole(pool_vecs), whole(w_out), whole(w_pool)],
        out_specs=[tok(D), pl.BlockSpec((N_HEADS, LANE, tile), lambda i: (0, 0, i)),
                   tok(D_ATT), tok(D_POOL), tok(D_POOL),
                   whole(w_out), whole(w_pool), whole(vecs)],
        out_shape=outs,
        scratch_shapes=[pltpu.VMEM(w_out.shape, F32), pltpu.VMEM(pool_vecs.shape, F32)],
        compiler_params=_params(("arbitrary",)),
    )(x, tgt, att, g_att, g_pool, p, p, vecs, pool_vecs, w_out, w_pool)


def _attention_backward(q2t, kp, vp, do2t, gw_out, vecs, pool, tile):
    seq = kp.shape[1]
    nb = seq // tile
    last = N_HEADS // 2 - 1

    def body(qt_ref, k_ref, v_ref, dot_ref, gwo_hbm, vecs_hbm, pool_hbm,
             dq_ref, dk_ref, dv_ref, dcum_ref, g_out_ref, vecs_sum_ref, pool_sum_ref,
             dq_acc, dk_acc, dv_acc, gwo_ref, vecs_ref, pool_ref,
             r1, s2, r2, v_sib, v_chip, v_recv, p_sib, p_chip, p_recv, send_sems, recv_sems):
        hp = pl.program_id(0)
        start, middle, fold, finish = _reduce_stages(
            [gwo_ref], [g_out_ref], [r1], [s2], [r2],
            [(vecs_ref, vecs_sum_ref, v_sib, v_chip, v_recv), (pool_ref, pool_sum_ref, p_sib, p_chip, p_recv)],
            send_sems, recv_sems)

        @pl.when(hp == 0)
        def _():
            pltpu.sync_copy(gwo_hbm, gwo_ref)
            pltpu.sync_copy(vecs_hbm, vecs_ref)
            pltpu.sync_copy(pool_hbm, pool_ref)
            start()

        pl.when(hp == 1)(middle)
        pl.when(hp == 2)(fold)

        row = lax.broadcasted_iota(jnp.int32, (tile, tile), 0)
        col = lax.broadcasted_iota(jnp.int32, (tile, tile), 1)
        dq_acc[...] = jnp.zeros_like(dq_acc)

        def kv_block(kb, _):
            krows = pl.ds(pl.multiple_of(kb * tile, tile), tile)
            k = [k_ref[hh, krows, :] for hh in range(2)]
            v = [v_ref[hh, krows, :] for hh in range(2)]
            k_t = [k[hh].T for hh in range(2)]

            def q_block(qb, masked):
                qcols = pl.ds(pl.multiple_of(qb * tile, tile), tile)
                for hh in range(2):
                    q_t = qt_ref[hh, :, qcols]
                    do_t = dot_ref[hh, :, qcols]
                    s_t = _dot(k[hh], q_t)
                    if masked:
                        s_t = jnp.where(row <= col, s_t, -1e30)
                    p_t = jnp.exp(s_t)
                    ds_t = (p_t * _dot(v[hh], do_t)).astype(BF16)
                    dv_new = _dot_nt(do_t, p_t.astype(BF16))
                    dk_new = _dot_nt(q_t, ds_t)
                    if masked:
                        dv_acc[hh], dk_acc[hh] = dv_new, dk_new
                    else:
                        dv_acc[hh] += dv_new
                        dk_acc[hh] += dk_new
                    dq_acc[hh, :, qcols] += _dot(k_t[hh], ds_t)

            q_block(kb, True)

            def two_later_blocks(j, _):
                q_block(kb + 1 + 2 * j, False)
                q_block(kb + 2 + 2 * j, False)
                return 0

            later = nb - 1 - kb
            lax.fori_loop(0, later // 2, two_later_blocks, 0)
            pl.when(later % 2 == 1)(lambda: q_block(nb - 1, False))
            for hh in range(2):
                dk = dk_acc[hh]
                dk_ref[hh, :, krows] = dk.astype(BF16)
                dv_ref[hh, :, krows] = dv_acc[hh].astype(BF16)
                dcum_ref[hh, :, krows] = -dk[L_CK:L_CK + 1, :]
            return 0

        lax.fori_loop(0, nb, kv_block, 0)
        for hh in range(2):
            dq = dq_acc[hh]
            dcum_ref[hh] += dq[L_CQ:L_CQ + 1, :]
            dq_ref[hh] = (dq * Q_SCALE).astype(BF16)
        pl.when(hp == last)(finish)

    pair = pl.BlockSpec((2, seq, LANE), lambda hp: (hp, 0, 0))
    pair_t = pl.BlockSpec((2, LANE, seq), lambda hp: (hp, 0, 0))
    whole = lambda shape: pl.BlockSpec(shape, lambda hp: (0,) * len(shape))
    grad = jax.ShapeDtypeStruct((N_HEADS, LANE, seq), BF16)
    return pl.pallas_call(
        body, name="attention_backward", grid=(N_HEADS // 2,),
        in_specs=[pair_t, pair, pair, pair_t, ANY, ANY, ANY],
        out_specs=[pair_t, pair_t, pair_t, pl.BlockSpec((2, 1, seq), lambda hp: (hp, 0, 0)),
                   whole(gw_out.shape[1:]), whole(vecs.shape), whole(pool.shape)],
        out_shape=[grad, grad, grad, jax.ShapeDtypeStruct((N_HEADS, 1, seq), F32),
                   jax.ShapeDtypeStruct(gw_out.shape[1:], F32), jax.ShapeDtypeStruct(vecs.shape, F32),
                   jax.ShapeDtypeStruct(pool.shape, F32)],
        scratch_shapes=[pltpu.VMEM((2, LANE, seq), F32), pltpu.VMEM((2, LANE, tile), F32),
                        pltpu.VMEM((2, LANE, tile), F32), pltpu.VMEM(gw_out.shape, BF16),
                        pltpu.VMEM(vecs.shape, F32), pltpu.VMEM(pool.shape, F32)]
        + _reduce_scratch(gw_out, [vecs, pool]),
        compiler_params=_params(("arbitrary",)),
    )(q2t, kp, vp, do2t, gw_out, vecs, pool)


def _inproj_backward(dqp, dkp, dvp, d_cum, f, d_pooled, d_ga, d_gp, x, dxa, u, mod, w_main, w_f, tile):
    seq = x.shape[0]
    nt = seq // tile
    halo_blocks = tile // HALO

    def body(dq_ref, dk_ref, dv_ref, dcum_ref, f_ref, dpo_ref, dph_ref, dga_ref, dgp_ref, x_ref, dxa_ref, u_ref,
             mod_ref, w_ref, wf_ref,
             dx_ref, dproj_ref, dwf_ref, db_ref, dbf_ref, dmod_ref, carry_ref):
        step = pl.program_id(0)
        i = nt - 1 - step

        @pl.when(step == 0)
        def _():
            carry_ref[...] = jnp.zeros_like(carry_ref)
            dwf_ref[...] = jnp.zeros_like(dwf_ref)
            db_ref[...] = jnp.zeros_like(db_ref)
            dbf_ref[...] = jnp.zeros_like(dbf_ref)
            dmod_ref[...] = jnp.zeros_like(dmod_ref)

        ones = jnp.ones((8, tile), BF16)

        def emit(chunk, val):
            cols = pl.ds(chunk * COL_CHUNK, COL_CHUNK)
            db_ref[0:1, cols] += jnp.sum(val, axis=0, keepdims=True)
            vb = val.astype(BF16)
            dproj_ref[:, pl.ds((chunk - 3) * COL_CHUNK, COL_CHUNK)] = vb
            return _dot(vb, w_ref[cols, :])

        d_u = jnp.zeros((tile, D), F32)
        for chunk, ref in enumerate((dq_ref, dk_ref, dv_ref)):
            cols = pl.ds(chunk * COL_CHUNK, COL_CHUNK)
            val_t = ref[:, 0:HEAD_DIM, :].reshape(COL_CHUNK, tile)
            db_ref[:, cols] += _dot_nt(ones, val_t)
            d_u += _dot_tn(val_t, w_ref[cols, :])

        d_pooled = dpo_ref[...]
        d_halo = dph_ref[...] * jnp.where(i < nt - 1, 1.0, 0.0)
        pos = i * tile + lax.broadcasted_iota(jnp.int32, (tile, 1), 0) + 1
        d_p = []
        for g, window in enumerate(POOL_WINDOWS):
            cols = slice(g * GROUP_DIM, (g + 1) * GROUP_DIM)
            scaled = d_pooled[:, cols] / jnp.minimum(pos, window).astype(F32)
            d_p.append(_window_sum(scaled, d_halo[:, cols] * (1.0 / window), window, True) - d_pooled[:, cols])
        d_u += emit(3, jnp.concatenate(d_p, axis=1))
        d_u += emit(4, dga_ref[...])
        d_u += emit(5, dgp_ref[...])

        row = lax.broadcasted_iota(jnp.int32, (tile, tile), 0)
        col = lax.broadcasted_iota(jnp.int32, (tile, tile), 1)
        later = (row >= col).astype(BF16)
        d_logf = sum(_dot(part, later) for part in _split3(dcum_ref[:, 0, :])) + carry_ref[:, 0:1]
        carry_ref[:, 0:1] = d_logf[:, 0:1]
        d_f = d_logf * _sigmoid(-f_ref[...].T[0:N_HEADS, :])
        d_f = jnp.concatenate([d_f, jnp.zeros((LANE - N_HEADS, tile), F32)], axis=0)
        dbf_ref[...] += sum(_dot_nt(ones, part) for part in _split3(d_f))
        d_fb = d_f.astype(BF16)
        d_u += _dot_tn(d_fb, wf_ref[...])
        dwf_ref[...] += _dot(d_fb, u_ref[...])

        x = x_ref[...]
        dx_ref[...] = dxa_ref[...] + d_u * mod_ref[0:1, :]
        dmod_ref[0:1, :] += jnp.sum(d_u * x, axis=0, keepdims=True)
        dmod_ref[1:2, :] += jnp.sum(d_u, axis=0, keepdims=True)

    rev = lambda step: nt - 1 - step
    tok = lambda width: pl.BlockSpec((tile, width), lambda s: (rev(s), 0))
    head_block = pl.BlockSpec((N_HEADS, LANE, tile), lambda s: (0, 0, rev(s)))
    whole = lambda a: pl.BlockSpec(a.shape, lambda s: (0,) * a.ndim)
    halo = pl.BlockSpec((HALO, D_POOL), lambda s: (jnp.minimum((rev(s) + 1) * halo_blocks, seq // HALO - 1), 0))
    small = lambda width: jax.ShapeDtypeStruct((8, width), F32)
    n_rest = N_MAIN - OFF_P
    return pl.pallas_call(
        body, name="inproj_backward", grid=(nt,),
        in_specs=[head_block, head_block, head_block, pl.BlockSpec((N_HEADS, 1, tile), lambda s: (0, 0, rev(s))),
                  tok(LANE), tok(D_POOL), halo, tok(D_ATT), tok(D_POOL),
                  tok(D), tok(D), tok(D),
                  whole(mod), whole(w_main), whole(w_f)],
        out_specs=[tok(D), tok(n_rest), pl.BlockSpec((LANE, D), lambda s: (0, 0)),
                   pl.BlockSpec((8, N_MAIN), lambda s: (0, 0)), pl.BlockSpec((8, LANE), lambda s: (0, 0)),
                   pl.BlockSpec((8, D), lambda s: (0, 0))],
        out_shape=[jax.ShapeDtypeStruct((seq, D), F32), jax.ShapeDtypeStruct((seq, n_rest), BF16),
                   jax.ShapeDtypeStruct((LANE, D), F32), small(N_MAIN), small(LANE), small(D)],
        scratch_shapes=[pltpu.VMEM((8, LANE), F32)],
        compiler_params=_params(("arbitrary",)),
    )(dqp, dkp, dvp, d_cum, f, d_pooled, d_pooled, d_ga, d_gp, x, dxa, u, mod, w_main, w_f)


def _weight_grads(dq_t, dk_t, dv_t, dw_f, dproj, u, k_tile, half, reduce_in=None):
    seq = u.shape[0]
    nk = seq // k_tile
    rows = N_HEADS * HEAD_DIM
    width = D // 2
    shard = D_IN // N_DEV

    def body(dq_ref, dk_ref, dv_ref, dwf_ref, dp_ref, u_ref, *rest):
        k = pl.program_id(0)
        if reduce_in is None:
            out_ref, acc_ref = rest
        else:
            other_hbm, out_ref, sum_ref, acc_ref, r1, s2, r2, send_sems, recv_sems, kept, kept_sems = rest
            start, middle, fold, finish = _reduce_stages(
                [other_hbm], [sum_ref], [r1], [s2], [r2], [], send_sems, recv_sems, own=[(kept, kept_sems)])
            pl.when(k == 0)(start)
            pl.when(k == 1)(middle)
            pl.when(k == nk // 2)(fold)

        @pl.when(k == 0)
        def _():
            acc_ref[...] = jnp.zeros_like(acc_ref)

        tokens = u_ref[...]
        for j, ref in enumerate((dq_ref, dk_ref, dv_ref)):
            acc_ref[pl.ds(j * rows, rows), :] += _dot(ref[...].reshape(rows, k_tile), tokens)
        for j in range(dproj.shape[1] // COL_CHUNK):
            cols = pl.ds(j * COL_CHUNK, COL_CHUNK)
            acc_ref[pl.ds(F_HI + j * COL_CHUNK, COL_CHUNK), :] += _dot_tn(dp_ref[:, cols], tokens)

        @pl.when(k == nk - 1)
        def _():
            acc_ref[F_LO:F_HI, :] = dwf_ref[0:N_HEADS, :]
            for slot in range(N_DEV):
                out_ref[slot] = acc_ref[slot * shard:(slot + 1) * shard, :].astype(BF16)
            if reduce_in is not None:
                finish()

    heads = pl.BlockSpec((N_HEADS, HEAD_DIM, k_tile), lambda k: (0, 0, k))
    slots = jax.ShapeDtypeStruct((N_DEV, shard, width), BF16)
    in_specs = [heads, heads, heads, pl.BlockSpec((dw_f.shape[0], width), lambda k: (0, half)),
                pl.BlockSpec((k_tile, dproj.shape[1]), lambda k: (k, 0)), pl.BlockSpec((k_tile, width), lambda k: (k, half))]
    out_specs = [pl.BlockSpec((N_DEV, shard, width), lambda k: (0, 0, 0))]
    out_shape, scratch, operands = [slots], [pltpu.VMEM((D_IN, width), F32)], [dq_t, dk_t, dv_t, dw_f, dproj, u]
    if reduce_in is not None:
        assert nk >= 4 and reduce_in.shape == slots.shape
        in_specs.append(ANY)
        operands.append(reduce_in)
        out_specs.append(pl.BlockSpec((shard, width), lambda k: (0, 0)))
        out_shape.append(jax.ShapeDtypeStruct((shard, width), F32))
        scratch += _reduce_scratch(slots, []) + [pltpu.VMEM((4, shard, width), BF16), pltpu.SemaphoreType.DMA((4,))]
    return pl.pallas_call(
        body, name="weight_grads_" + ("left", "right")[half], grid=(nk,),
        in_specs=in_specs, out_specs=out_specs, out_shape=out_shape, scratch_shapes=scratch,
        compiler_params=_params(("arbitrary",)),
    )(*operands)


def _adamw(w, g, m, v):
    m = ADAM_B1 * m + (1.0 - ADAM_B1) * g
    v = ADAM_B2 * v + (1.0 - ADAM_B2) * (g * g)
    m_hat = m / (1.0 - ADAM_B1 ** ADAM_STEP)
    v_hat = v / (1.0 - ADAM_B2 ** ADAM_STEP)
    delta = -ADAM_LR * (m_hat / (jnp.sqrt(v_hat) + ADAM_EPS) + ADAM_WD * w)
    return delta, m, v


SUBLANES = 8


def _adamw_packed(g_parts, w, m, v, name, chunks=4):
    n_parts, (rows, part_cols) = len(g_parts), g_parts[0].shape
    cols = n_parts * part_cols
    per_row, per_part = cols // LANE, part_cols // LANE
    assert part_cols % LANE == 0 and per_row == SUBLANES and w.shape == (rows * per_row, LANE)
    step = -(-rows // (chunks * SUBLANES)) * SUBLANES
    bounds = [(r0, min(r0 + step, rows)) for r0 in range(0, rows, step)]

    def body(*refs):
        g_hbm, (w_hbm, m_hbm, v_hbm, og_hbm, od_hbm, om_hbm, ov_hbm) = refs[:n_parts], refs[n_parts:n_parts + 7]
        g_buf, in_buf, out_buf, in_sems, out_sems = refs[n_parts + 7:]

        def copies_in(c):
            r0, r1 = bounds[c]
            packed = slice(r0 * per_row, r1 * per_row)
            return [pltpu.make_async_copy(g_hbm[p].at[r0:r1], g_buf.at[p, r0:r1], in_sems.at[c, 3 + p])
                    for p in range(n_parts)] + [
                pltpu.make_async_copy(src.at[packed], in_buf.at[i, packed], in_sems.at[c, i])
                for i, src in enumerate((w_hbm, m_hbm, v_hbm))]

        def copies_out(c):
            r0, r1 = bounds[c]
            packed = slice(r0 * per_row, r1 * per_row)
            return [pltpu.make_async_copy(out_buf.at[i, packed], dst.at[packed], out_sems.at[c, i])
                    for i, dst in enumerate((og_hbm, od_hbm, om_hbm, ov_hbm))]

        for c in range(len(bounds)):
            for cp in copies_in(c):
                cp.start()
        for c, (r0, r1) in enumerate(bounds):
            for cp in copies_in(c):
                cp.wait()
            for j in range(per_row):
                lanes = pl.ds(r0 * per_row + j, r1 - r0, stride=per_row)
                g_part = g_buf[j // per_part, r0:r1, (j % per_part) * LANE:(j % per_part + 1) * LANE]
                results = _adamw(in_buf[0, lanes, :], g_part, in_buf[1, lanes, :], in_buf[2, lanes, :])
                for i, val in enumerate((g_part,) + results):
                    out_buf[i, lanes, :] = val
            for cp in copies_out(c):
                cp.start()
        for c in range(len(bounds)):
            for cp in copies_out(c):
                cp.wait()

    shape = jax.ShapeDtypeStruct(w.shape, F32)
    return pl.pallas_call(
        body, name=name,
        in_specs=[ANY] * (n_parts + 3), out_specs=[ANY] * 4, out_shape=[shape] * 4,
        scratch_shapes=[pltpu.VMEM((n_parts, rows, part_cols), F32), pltpu.VMEM((3,) + w.shape, F32),
                        pltpu.VMEM((4,) + w.shape, F32),
                        pltpu.SemaphoreType.DMA((len(bounds), 3 + n_parts)), pltpu.SemaphoreType.DMA((len(bounds), 4))],
        compiler_params=_params(),
    )(*g_parts, w, m, v)


def _ada_adamw(sc_all, d_ada, w, m, v, chunks=4):
    rows, cols = w.shape
    step, sub = rows // chunks, 32
    assert rows % chunks == 0 and step % LANE == 0 and step % sub == 0

    def body(sc_ref, d_ref, w_hbm, m_hbm, v_hbm, og_hbm, od_hbm, om_hbm, ov_hbm, in_buf, out_buf, in_sems, out_sems):
        def copies_in(c):
            part = slice(c * step, (c + 1) * step)
            return [pltpu.make_async_copy(src.at[part], in_buf.at[i, part], in_sems.at[c, i])
                    for i, src in enumerate((w_hbm, m_hbm, v_hbm))]

        def copies_out(c):
            part = slice(c * step, (c + 1) * step)
            return [pltpu.make_async_copy(out_buf.at[i, part], dst.at[part], out_sems.at[c, i])
                    for i, dst in enumerate((og_hbm, od_hbm, om_hbm, ov_hbm))]

        for c in range(chunks):
            for cp in copies_in(c):
                cp.start()
        for c in range(chunks):
            sc_t = sc_ref[:, c * step:(c + 1) * step].T
            for cp in copies_in(c):
                cp.wait()
            for r0 in range(0, step, sub):
                part = slice(c * step + r0, c * step + r0 + sub)
                g = sc_t[r0:r0 + sub, 0:1] * d_ref[0:1, :]
                for b in range(1, N_DEV):
                    g = g + sc_t[r0:r0 + sub, b:b + 1] * d_ref[b:b + 1, :]
                results = _adamw(in_buf[0, part, :], g, in_buf[1, part, :], in_buf[2, part, :])
                for i, val in enumerate((g,) + results):
                    out_buf[i, part, :] = val
            for cp in copies_out(c):
                cp.start()
        for c in range(chunks):
            for cp in copies_out(c):
                cp.wait()

    in_vmem = pl.BlockSpec(memory_space=pltpu.VMEM)
    shape = jax.ShapeDtypeStruct(w.shape, F32)
    return pl.pallas_call(
        body, name="ada_adamw",
        in_specs=[in_vmem, in_vmem, ANY, ANY, ANY], out_specs=[ANY] * 4, out_shape=[shape] * 4,
        scratch_shapes=[pltpu.VMEM((3,) + w.shape, F32), pltpu.VMEM((4,) + w.shape, F32),
                        pltpu.SemaphoreType.DMA((chunks, 3)), pltpu.SemaphoreType.DMA((chunks, 4))],
        compiler_params=_params(),
    )(sc_all, d_ada, w, m, v)


F_LO, F_HI = 3 * D_ATT, 3 * D_ATT + N_HEADS


def _split_forget(a, axis):
    idx = lambda lo, hi: tuple(slice(lo, hi) if d == axis else slice(None) for d in range(a.ndim))
    pad = [(0, LANE - N_HEADS) if d == axis else (0, 0) for d in range(a.ndim)]
    return jnp.concatenate([a[idx(0, F_LO)], a[idx(F_HI, D_IN)]], axis=axis), jnp.pad(a[idx(F_LO, F_HI)], pad)


def _join_forget(main, f, axis):
    idx = lambda lo, hi: tuple(slice(lo, hi) if d == axis else slice(None) for d in range(main.ndim))
    return jnp.concatenate([main[idx(0, F_LO)], f[idx(0, N_HEADS)], main[idx(F_LO, N_MAIN)]], axis=axis)


def _adamw_small(grad_rows, row_params, whole_params, summed_params, scalar_at):
    n_row, n_whole, n_sum = len(row_params), len(whole_params), len(summed_params)
    n = n_row + n_whole + n_sum

    def body(g_ref, *refs):
        n_in = 3 * n_row + 4 * (n_whole + n_sum)
        ins, outs = list(refs[:n_in]), refs[n_in:]
        for i in range(n):
            if i < n_row:
                row, lo, hi = row_params[i][:3]
                g = g_ref[row:row + 1, lo:hi]
            elif i < n_row + n_whole:
                g = ins.pop(0)[...]
            else:
                parts = ins.pop(0)
                g = parts[0]
                for k in range(1, N_DEV):
                    g = g + parts[k]
            w, m, v = (ins.pop(0)[...] for _ in range(3))
            outs[4 * i][...] = g
            outs[4 * i + 1][...], outs[4 * i + 2][...], outs[4 * i + 3][...] = _adamw(w, g, m, v)
        row, lane = scalar_at
        outs[4 * n][...] = g_ref[row:row + 1, lane:lane + 1]

    shapes = [p[3] for p in row_params] + [p[1] for p in whole_params] + [p[1] for p in summed_params]
    operands = [a for p in row_params for a in p[3:]] + [a for p in whole_params + summed_params for a in p]
    flat = pl.pallas_call(
        body, name="adamw_small",
        out_shape=[jax.ShapeDtypeStruct(w.shape, F32) for w in shapes for _ in range(4)]
        + [jax.ShapeDtypeStruct((1, 1), F32)],
        compiler_params=_params(),
    )(grad_rows, *operands)
    return [flat[4 * i:4 * i + 4] for i in range(n)], flat[4 * n].reshape(())


def kernel(x, c, w_ada, b_ada, w_in, b_in, w_pool_mix, b_pool_mix, pool_scale, w_out, b_out, ln_g, ln_b, loss_target, m_w_ada, m_b_ada, m_w_in, m_b_in, m_w_pool_mix, m_b_pool_mix, m_pool_scale, m_w_out, m_b_out, m_ln_g, m_ln_b, v_w_ada, v_b_ada, v_w_in, v_b_in, v_w_pool_mix, v_b_pool_mix, v_pool_scale, v_w_out, v_b_out, v_ln_g, v_ln_b):
    seq = x.shape[1]
    tile = min(256, seq)
    attn_tile = min(512, max(128, seq // 4))
    me = _dev_index(*_mesh_pos())
    x2, tgt = x[0], loss_target[0]

    rows_of = lambda a: jnp.swapaxes(a, 1, 2)[0]
    w_main, w_f, sc_all, ada_mine = _gather_and_ada(c, rows_of(w_in).astype(BF16), w_ada[0])
    ada = ada_mine.reshape(1, D_ADA) + b_ada
    shift, scale, gate = ada[:, 0:D], ada[:, D:2 * D], ada[:, 2 * D:]
    mod = jnp.concatenate([1.0 + scale, shift, jnp.zeros((6, D), F32)], axis=0)
    b_main, b_f = _split_forget(b_in, 1)

    qp, kp, vp, f, p, g_att, g_pool, u = _inproj_forward(x2, mod, w_main, w_f, b_main, b_f, tile)
    att, q2t, w_out_g = _attention_forward(qp, kp, vp, w_out[0].astype(BF16), attn_tile)

    vecs = jnp.concatenate([gate, b_out, ln_g, ln_b, jnp.zeros((4, D), F32)], axis=0)
    pool_vecs = jnp.concatenate([b_pool_mix.reshape(1, D_POOL), pool_scale, jnp.zeros((6, D_POOL), F32)], axis=0)
    dxa, do2, d_ga, d_gp, d_pooled, gw_out, dw_pool, dvec = _middle(
        x2, tgt, att, g_att, g_pool, p, vecs, pool_vecs, w_out_g.reshape(D, D), w_pool_mix[0].astype(BF16), tile)

    pool_rows = w_pool_mix.shape[1] * GROUP_DIM
    dqp, dkp, dvp, d_cum, g_out, dvec_sum, dw_pool_sum = _attention_backward(
        q2t, kp, vp, do2, gw_out.reshape(N_DEV, D // N_DEV, D), dvec, dw_pool.reshape(pool_rows, GROUP_DIM), attn_tile)
    dx, dproj, dw_f, db_main, db_f, dmod = _inproj_backward(
        dqp, dkp, dvp, d_cum, f, d_pooled, d_ga, d_gp, x2, dxa, u, mod, w_main, w_f, tile)
    grad_tile = min(1024, seq)
    (gw_left,) = _weight_grads(dqp, dkp, dvp, dw_f, dproj, u, grad_tile, 0)
    gw_right, g_in_left = _weight_grads(dqp, dkp, dvp, dw_f, dproj, u, grad_tile, 1, reduce_in=gw_left)
    d_ada = jnp.concatenate([dmod[1:2], dmod[0:1], dvec[5:6]], axis=1)
    g_in_right, g_b_in, d_ada_all = _reduce_grads(gw_right, _join_forget(db_main[0:1], db_f[0:1], 1), d_ada)

    packed = lambda a: jnp.transpose(a.reshape(SUBLANES, LANE, -1), (2, 0, 1)).reshape(-1, LANE)
    outs_in = _adamw_packed((g_in_left, g_in_right), packed(w_in), packed(m_w_in), packed(v_w_in), "adamw_w_in")
    g_w_in, d_w_in, nm_w_in, nv_w_in = (
        jnp.transpose(a.reshape(-1, SUBLANES, LANE), (1, 2, 0)).reshape(D, -1) for a in outs_in)
    flat_pool = lambda a: a.reshape(1, D_POOL)
    pool_2d = lambda a: a.reshape(pool_rows, GROUP_DIM)
    rows, loss = _adamw_small(
        dvec_sum,
        [(0, 0, D, b_out, m_b_out, v_b_out), (1, 0, D, ln_g, m_ln_g, v_ln_g), (2, 0, D, ln_b, m_ln_b, v_ln_b),
         (3, 0, D_POOL, flat_pool(b_pool_mix), flat_pool(m_b_pool_mix), flat_pool(v_b_pool_mix)),
         (3, D_POOL, 2 * D_POOL, pool_scale, m_pool_scale, v_pool_scale)],
        [(g_out, w_out[0], m_w_out[0], v_w_out[0]),
         (dw_pool_sum, pool_2d(w_pool_mix), pool_2d(m_w_pool_mix), pool_2d(v_w_pool_mix)),
         (g_b_in, b_in, m_b_in, v_b_in)],
        [(d_ada_all, b_ada, m_b_ada, v_b_ada)],
        scalar_at=(4, 0))
    small = {"b_out": rows[0], "ln_g": rows[1], "ln_b": rows[2],
             "b_pool": [a.reshape(b_pool_mix.shape) for a in rows[3]], "pool_scale": rows[4],
             "w_pool": [a.reshape(w_pool_mix.shape) for a in rows[6]], "b_in": rows[7]}
    g_s, d_s, nm_s, nv_s = ({k: r[j] for k, r in small.items()} for j in range(4))
    g_w_out, d_w_out, nm_w_out, nv_w_out = rows[5]
    g_b_ada, d_b_ada, nm_b_ada, nv_b_ada = rows[8]

    d_ada_local = lax.dynamic_slice_in_dim(d_ada_all.reshape(N_DEV, D_ADA), me * (D_ADA // N_DEV), D_ADA // N_DEV, axis=1)
    g_w_ada, d_w_ada, nm_w_ada, nv_w_ada = _ada_adamw(sc_all, d_ada_local, w_ada[0], m_w_ada[0], v_w_ada[0])

    def ordered(w_ada_, b_ada_, w_in_, w_out_, s):
        return (w_ada_[None], b_ada_, w_in_[None], s["b_in"], s["w_pool"], s["b_pool"], s["pool_scale"],
                w_out_[None], s["b_out"], s["ln_g"], s["ln_b"])

    return (loss, dx[None],
            *ordered(g_w_ada, g_b_ada, g_w_in, g_w_out, g_s),
            *ordered(d_w_ada, d_b_ada, d_w_in, d_w_out, d_s),
            *ordered(nm_w_ada, nm_b_ada, nm_w_in, nm_w_out, nm_s),
            *ordered(nv_w_ada, nv_b_ada, nv_w_in, nv_w_out, nv_s))
```

```python
import jax
import jax.numpy as jnp
from jax import lax
from jax.experimental import pallas as pl
from jax.experimental.pallas import tpu as pltpu

F32 = jnp.float32
BF16 = jnp.bfloat16

N_DEV = 8
D = 1024
N_HEADS = 8
HEAD_DIM = 64
D_ATT = 512
D_POOL = 512
POOL_WINDOWS = (2, 4, 8, 16)
GROUP_DIM = 128
HALO = 16
LANE = 128
BF16_TILE_ROWS = 16
D_IN = 3080
D_ADA = 3072
N_MAIN = 3072
OFF_P = 1536
COL_CHUNK = 512
Q_SCALE = 0.125
LN_EPS = 1e-5
ALPHA = 2.0 ** 0.25
L_CQ, L_CK, L_LSE = 64, 67, 70

ADAM_LR, ADAM_B1, ADAM_B2, ADAM_EPS, ADAM_WD, ADAM_STEP = 0.001, 0.9, 0.999, 1e-08, 0.01, 10
VMEM_LIMIT = 56 * 1024 * 1024

MESH = pl.DeviceIdType.MESH
ANY = pl.BlockSpec(memory_space=pl.ANY)


def _params(sem=None, vmem=VMEM_LIMIT):
    return pltpu.CompilerParams(dimension_semantics=sem, vmem_limit_bytes=vmem)


def _split3(a):
    hi = a.astype(BF16)
    r = a - hi.astype(F32)
    mid = r.astype(BF16)
    lo = (r - mid.astype(F32)).astype(BF16)
    return hi, mid, lo


def _dot(a, b):
    return jnp.dot(a, b, preferred_element_type=F32)


def _dot_nt(a, b):
    return lax.dot_general(a, b, (((1,), (1,)), ((), ())), preferred_element_type=F32)


def _dot_tn(a, b):
    return lax.dot_general(a, b, (((0,), (0,)), ((), ())), preferred_element_type=F32)


def _dot3(m01, a):
    hi, mid, lo = _split3(a)
    return _dot(m01, hi) + _dot(m01, mid) + _dot(m01, lo)


def _sigmoid(z):
    return 1.0 / (1.0 + jnp.exp(-z))


def _lanes(shape):
    return lax.broadcasted_iota(jnp.int32, shape, len(shape) - 1)


def _place3(lane, base, parts, other):
    out = other
    for j in range(3):
        out = jnp.where(lane == base + j, parts[j], out)
    return out


def _mesh_pos():
    return lax.axis_index("x"), lax.axis_index("y"), lax.axis_index("c")


def _dev_index(px, py, pc):
    return 4 * px + 2 * py + pc


N_GATHER_SEMS = 11


def _gather_stages(src_ref, out_ref, send_sems, recv_sems, local_sem):
    x, y, c = _mesh_pos()
    me, sibling = (x, y, c), (x, y, 1 - c)
    nbr_x, nbr_y, diag = (1 - x, y), (x, 1 - y), (1 - x, 1 - y)
    half = out_ref.shape[-1] // 2
    left, right = pl.ds(0, half), pl.ds(half, half)

    def copy(k, block, to, cols=None, src=None):
        slot = out_ref.at[_dev_index(*block)]
        if cols is not None:
            slot = slot.at[:, cols]
            src = src if src is None else src.at[:, cols]
        return pltpu.make_async_remote_copy(
            src_ref=slot if src is None else src, dst_ref=slot, send_sem=send_sems.at[k], recv_sem=recv_sems.at[k],
            device_id=to, device_id_type=MESH)

    mine = pltpu.make_async_copy(src_ref, out_ref.at[_dev_index(*me)], local_sem)
    first = [copy(0, me, sibling, src=src_ref),
             copy(1, me, (*nbr_x, c), cols=left, src=src_ref), copy(2, me, (*nbr_y, c), cols=right, src=src_ref),
             copy(9, me, (*nbr_x, c), cols=right, src=src_ref), copy(10, me, (*nbr_y, c), cols=left, src=src_ref)]
    relay = [(1, nbr_x, left, nbr_x), (2, nbr_y, right, nbr_y), (3, diag, left, nbr_y), (4, diag, right, nbr_x)]
    other_half = [(9, nbr_x, right, nbr_x), (10, nbr_y, left, nbr_y)]
    onward = [copy(3, (*nbr_x, c), (*nbr_y, c), cols=left), copy(4, (*nbr_y, c), (*nbr_x, c), cols=right)]
    passed = [copy(4 + k, (*block, c), sibling, cols=None if k < 3 else cols) for k, block, cols, _ in relay]

    def start():
        mine.start()
        for cp in first:
            cp.start()

    def arrived(item):
        k, block, cols, frm = item
        copy(k, (*block, c), (*frm, c), cols=cols).wait_recv()

    def relay_near():
        for j in (0, 1):
            arrived(relay[j])
            onward[j].start()
        for j in (0, 1):
            arrived(other_half[j])
            passed[j].start()

    def relay_far():
        for j in (2, 3):
            arrived(relay[j])
            passed[j].start()

    def from_sibling(items):
        for k, block, cols, _ in items:
            copy(4 + k, (*block, 1 - c), me, cols=None if k < 3 else cols).wait_recv()

    def finish_near():
        copy(0, sibling, me).wait_recv()
        from_sibling(relay[:2])
        mine.wait()

    def finish_far():
        from_sibling(relay[2:])
        for cp in first + onward + passed:
            cp.wait_send()

    return start, relay_near, relay_far, finish_near, finish_far


N_REDUCE_SEMS = 10
N_SMALL_SEMS = 4
N_ROWS_SEMS = 7


def _reduce_stages(ins, gs, r1, s2, r2, smalls, send_sems, recv_sems, rows=None, own=None):
    n = len(ins)
    x, y, c = _mesh_pos()
    me = _dev_index(x, y, c)
    sibling = (x, y, 1 - c)
    chips = [(x, y), (1 - x, y), (x, 1 - y), (1 - x, 1 - y)]
    peers = []
    for p in range(1, N_DEV):
        px, py, pc = (p >> 2) & 1, (p >> 1) & 1, p & 1
        peers.append((1 - x if px else x, 1 - y if py else y, 1 - c if pc else c))
    base_small = N_REDUCE_SEMS * n

    def remote(src, dst, k, to):
        return pltpu.make_async_remote_copy(src_ref=src, dst_ref=dst, send_sem=send_sems.at[k],
                                            recv_sem=recv_sems.at[k], device_id=to, device_id_type=MESH)

    def level1(a, q):
        return remote(ins[a].at[_dev_index(*chips[q], 1 - c)], r1[a].at[q], N_REDUCE_SEMS * a + q, sibling)

    def level2(a, k):
        half = ins[a].shape[-1] // 2
        left, right = pl.ds(0, half), pl.ds(half, half)
        nbr_x, nbr_y = (*chips[1], c), (*chips[2], c)
        src_slot, dst_slot, cols, to = [(0, 0, left, nbr_x), (1, 1, right, nbr_y), (2, 2, left, nbr_x),
                                        (2, 2, right, nbr_y), (0, 0, right, nbr_x), (1, 1, left, nbr_y)][k]
        return remote(s2[a].at[src_slot, :, cols], r2[a].at[dst_slot, :, cols], N_REDUCE_SEMS * a + 4 + k, to)

    to_sibling = [remote(sm[0], sm[2], base_small + 4 * i, sibling) for i, sm in enumerate(smalls)]
    to_chips = [[remote(sm[3], sm[4].at[j], base_small + 4 * i + 1 + j, (*chips[j + 1], c)) for j in range(3)]
                for i, sm in enumerate(smalls)]
    if rows is not None:
        rows_ref, land_ref, all_ref = rows
        base_rows = base_small + 4 * len(smalls)
        row_sends = [remote(rows_ref, land_ref.at[me], base_rows + k, to) for k, to in enumerate(peers)]

    order = (3, 1, 2, 0)

    def mine(a, q):
        buf, sems = own[a]
        return pltpu.make_async_copy(ins[a].at[_dev_index(*chips[q], c)], buf.at[q], sems.at[q])

    def start():
        for a in range(n):
            for q in order:
                level1(a, q).start()
            if own is not None:
                for q in order:
                    mine(a, q).start()
        for cp in to_sibling:
            cp.start()
        if rows is not None:
            for cp in row_sends:
                cp.start()
            land_ref[me] = rows_ref[...]

    def middle():
        for a in range(n):
            for q in order:
                level1(a, q).wait_recv()
                if own is None:
                    kept = ins[a][_dev_index(*chips[q], c)]
                else:
                    mine(a, q).wait()
                    kept = own[a][0][q]
                pair = kept.astype(F32) + r1[a][q].astype(F32)
                if q == 0:
                    gs[a][...] = pair
                else:
                    s2[a][q - 1] = pair.astype(BF16)
                    for k in ((0,), (1,), (2, 3))[q - 1]:
                        level2(a, k).start()
        for i, (small_ref, _, sm_sib, sm_chip, _) in enumerate(smalls):
            to_sibling[i].wait_recv()
            sm_chip[...] = small_ref[...] + sm_sib[...]
            for cp in to_chips[i]:
                cp.start()

    def fold():
        for a in range(n):
            half = ins[a].shape[-1] // 2
            level2(a, 3).wait_recv()
            s2[a][0, :, half:] = (s2[a][0, :, half:].astype(F32) + r2[a][2, :, half:].astype(F32)).astype(BF16)
            level2(a, 4).start()
            level2(a, 2).wait_recv()
            s2[a][1, :, :half] = (s2[a][1, :, :half].astype(F32) + r2[a][2, :, :half].astype(F32)).astype(BF16)
            level2(a, 5).start()

    def finish():
        for a in range(n):
            for k in (0, 1, 4, 5):
                level2(a, k).wait_recv()
            gs[a][...] = gs[a][...] + r2[a][0].astype(F32) + r2[a][1].astype(F32)
            for q in range(4):
                level1(a, q).wait_send()
            for k in range(6):
                level2(a, k).wait_send()
        for i, (_, total_ref, _, sm_chip, sm_recv) in enumerate(smalls):
            for cp in to_chips[i]:
                cp.wait_recv()
            total = None
            for ax in range(2):
                for ay in range(2):
                    dx, dy = x != ax, y != ay
                    term = jnp.where(dx, jnp.where(dy, sm_recv[2], sm_recv[0]), jnp.where(dy, sm_recv[1], sm_chip[...]))
                    total = term if total is None else total + term
            total_ref[...] = total
            for cp in [to_sibling[i]] + to_chips[i]:
                cp.wait_send()
        if rows is not None:
            for k, frm in enumerate(peers):
                remote(rows_ref, land_ref.at[_dev_index(*frm)], base_rows + k, frm).wait_recv()
            all_ref[...] = land_ref[...]
            for cp in row_sends:
                cp.wait_send()

    return start, middle, fold, finish


def _reduce_scratch(shard, smalls, rows=None):
    out = [pltpu.VMEM((lead,) + shard.shape[1:], BF16) for lead in (4, 3, 3)]
    for small in smalls:
        out += [pltpu.VMEM(small.shape, F32), pltpu.VMEM(small.shape, F32), pltpu.VMEM((3,) + small.shape, F32)]
    n_sems = N_REDUCE_SEMS + N_SMALL_SEMS * len(smalls)
    if rows is not None:
        out.append(pltpu.VMEM((N_DEV,) + rows.shape, F32))
        n_sems += N_ROWS_SEMS
    return out + [pltpu.SemaphoreType.DMA((n_sems,))] * 2


def _reduce_grads(gw_in, small, rows):
    def body(in_ref, small_ref, rows_ref, g_ref, total_ref, rows_all_ref,
             r1, s2, r2, sm_sib, sm_chip, sm_recv, rows_land, send_sems, recv_sems, kept, kept_sems):
        stages = _reduce_stages(
            [in_ref], [g_ref], [r1], [s2], [r2], [(small_ref, total_ref, sm_sib, sm_chip, sm_recv)],
            send_sems, recv_sems, rows=(rows_ref, rows_land, rows_all_ref), own=[(kept, kept_sems)])
        for stage in stages:
            stage()

    vmem = pl.BlockSpec(memory_space=pltpu.VMEM)
    return pl.pallas_call(
        body, name="reduce_grads",
        in_specs=[ANY, vmem, vmem], out_specs=[vmem, vmem, vmem],
        out_shape=[jax.ShapeDtypeStruct(gw_in.shape[1:], F32), jax.ShapeDtypeStruct(small.shape, F32),
                   jax.ShapeDtypeStruct((N_DEV,) + rows.shape, F32)],
        scratch_shapes=_reduce_scratch(gw_in, [small], rows)
        + [pltpu.VMEM((4,) + gw_in.shape[1:], BF16), pltpu.SemaphoreType.DMA((4,))],
        compiler_params=_params(),
    )(gw_in, small, rows)


def _dot3_rhs(a, b):
    a0, a1, a2 = _split3(a)
    b0, b1, b2 = _split3(b)
    return (_dot(a0, b0) + (_dot(a0, b1) + _dot(a1, b0))
            + (_dot(a0, b2) + _dot(a1, b1) + _dot(a2, b0)))


def _gather_and_ada(c, w_in_rows, w_ada):
    cols = w_ada.shape[1]
    shard = w_in_rows.shape[0]

    def body(c_ref, w_ref, wa_ref, w_main_hbm, w_f_ref, sc_ref, ada_ref,
             w_all_ref, w_f32, wm_buf, c_land, part, ada_land, send_sems, recv_sems, local_sem, x_send, x_recv, out_sems):
        x, y, cc = _mesh_pos()
        me = _dev_index(x, y, cc)
        peers = []
        for p in range(1, N_DEV):
            px, py, pc = (p >> 2) & 1, (p >> 1) & 1, p & 1
            peers.append((1 - x if px else x, 1 - y if py else y, 1 - cc if pc else cc))

        def remote(src, dst, k, to):
            return pltpu.make_async_remote_copy(src_ref=src, dst_ref=dst, send_sem=x_send.at[k], recv_sem=x_recv.at[k],
                                                device_id=to, device_id_type=MESH)

        c_sends = [remote(c_ref, c_land.at[me], k, to) for k, to in enumerate(peers)]
        for cp in c_sends:
            cp.start()
        start, relay_near, relay_far, finish_near, finish_far = _gather_stages(
            w_ref, w_all_ref, send_sems, recv_sems, local_sem.at[0])
        start()
        c_land[me] = c_ref[...]
        for k, frm in enumerate(peers):
            remote(c_ref, c_land.at[_dev_index(*frm)], k, frm).wait_recv()
        c_all = jnp.concatenate([c_land[b] for b in range(N_DEV)], axis=0)
        sc = c_all * _sigmoid(c_all)
        sc_ref[...] = sc
        rows = _dot3_rhs(sc, wa_ref[...])
        for b in range(N_DEV):
            part[b] = rows[b:b + 1, :]
        a_sends = [remote(part.at[_dev_index(*to)], ada_land.at[me], 7 + k, to) for k, to in enumerate(peers)]
        for cp in a_sends:
            cp.start()
        ada_land[me] = part[me]

        relay_near()
        finish_near()

        far_chip = 2 * (1 - x) + (1 - y)

        def stage(slots):
            for slot in slots:
                w_f32[slot * shard:(slot + 1) * shard, :] = w_all_ref[slot].astype(F32)

        def far_rows(k):
            main_row = lambda r: r if r < F_LO else r - N_HEADS
            first, last = 2 * shard * k, 2 * shard * (k + 1) - 1
            first = F_HI if F_LO <= first < F_HI else first
            last = F_LO - 1 if F_LO <= last < F_HI else last
            return (main_row(first) // BF16_TILE_ROWS * BF16_TILE_ROWS,
                    -(-(main_row(last) + 1) // BF16_TILE_ROWS) * BF16_TILE_ROWS)

        def near_rows(k):
            lo, hi = far_rows(k)
            return [(i, a, b) for i, (a, b) in enumerate(((0, lo), (hi, N_MAIN))) if a < b]

        def forget_is_far(k):
            return 2 * shard * k < F_HI and 2 * shard * (k + 1) > F_LO

        def main_copy(i, lo, hi):
            return pltpu.make_async_copy(wm_buf.at[lo:hi], w_main_hbm.at[lo:hi], out_sems.at[i])

        def write_main(i, lo, hi):
            if lo < min(hi, F_LO):
                wm_buf[lo:min(hi, F_LO), :] = w_f32[lo:min(hi, F_LO), :].astype(BF16)
            if max(lo, F_LO) < hi:
                wm_buf[max(lo, F_LO):hi, :] = w_f32[max(lo, F_LO) + N_HEADS:hi + N_HEADS, :].astype(BF16)
            main_copy(i, lo, hi).start()

        def write_forget():
            w_f_ref[...] = jnp.concatenate(
                [w_f32[F_LO:F_HI, :], jnp.zeros((LANE - N_HEADS, D), F32)], axis=0).astype(BF16)

        for k in range(N_DEV // 2):
            @pl.when(far_chip == k)
            def _(k=k):
                stage([slot for slot in range(N_DEV) if slot // 2 != k])
                for i, lo, hi in near_rows(k):
                    write_main(i, lo, hi)
                if not forget_is_far(k):
                    write_forget()

        relay_far()
        for k, frm in enumerate(peers):
            remote(part.at[0], ada_land.at[_dev_index(*frm)], 7 + k, frm).wait_recv()
        ada_ref[...] = ada_land[...]
        finish_far()
        for cp in c_sends + a_sends:
            cp.wait_send()

        for k in range(N_DEV // 2):
            @pl.when(far_chip == k)
            def _(k=k):
                stage([2 * k, 2 * k + 1])
                write_main(2, *far_rows(k))
                if forget_is_far(k):
                    write_forget()
                for i, lo, hi in near_rows(k):
                    main_copy(i, lo, hi).wait()
                main_copy(2, *far_rows(k)).wait()

    vmem = pl.BlockSpec(memory_space=pltpu.VMEM)
    return pl.pallas_call(
        body, name="gather_weights",
        in_specs=[vmem, ANY, vmem], out_specs=[ANY, vmem, vmem, vmem],
        out_shape=[jax.ShapeDtypeStruct((N_MAIN, D), BF16), jax.ShapeDtypeStruct((LANE, D), BF16),
                   jax.ShapeDtypeStruct((N_DEV, D), F32), jax.ShapeDtypeStruct((N_DEV, 1, cols), F32)],
        scratch_shapes=[pltpu.VMEM((N_DEV,) + w_in_rows.shape, BF16), pltpu.VMEM((D_IN, D), F32), pltpu.VMEM((N_MAIN, D), BF16),
                        pltpu.VMEM((N_DEV, 1, D), F32), pltpu.VMEM((N_DEV, 1, cols), F32), pltpu.VMEM((N_DEV, 1, cols), F32),
                        pltpu.SemaphoreType.DMA((N_GATHER_SEMS,)), pltpu.SemaphoreType.DMA((N_GATHER_SEMS,)),
                        pltpu.SemaphoreType.DMA((1,)),
                        pltpu.SemaphoreType.DMA((14,)), pltpu.SemaphoreType.DMA((14,)), pltpu.SemaphoreType.DMA((3,))],
        compiler_params=_params(),
    )(c, w_in_rows, w_ada)


def _inproj_forward(x, mod, w_main, w_f, b_main, b_f, tile):
    seq = x.shape[0]
    nt = seq // tile

    def body(x_ref, mod_ref, w_ref, wf_ref, b_ref, bf_ref,
             qp_ref, kp_ref, vp_ref, f_ref, p_ref, ga_ref, gp_ref, u_ref, carry_ref):
        i = pl.program_id(0)

        @pl.when(i == 0)
        def _():
            carry_ref[...] = jnp.zeros_like(carry_ref)

        u = x_ref[...] * mod_ref[0:1, :] + mod_ref[1:2, :]
        ub = u.astype(BF16)
        u_ref[...] = ub

        f = _dot_nt(ub, wf_ref[...]) + bf_ref[...]
        f_ref[...] = f
        lane = _lanes((tile, LANE))
        log_f = jnp.where(lane < N_HEADS, jnp.minimum(f, 0.0) - jnp.log(1.0 + jnp.exp(-jnp.abs(f))), 0.0)
        row = lax.broadcasted_iota(jnp.int32, (tile, tile), 0)
        col = lax.broadcasted_iota(jnp.int32, (tile, tile), 1)
        tri = (row >= col).astype(BF16)
        cum = _dot3(tri, log_f) + carry_ref[0:1, :]
        carry_ref[0:1, :] = cum[tile - 1:tile, :]
        cq = [part.astype(F32) for part in _split3(cum)]
        ck = [part.astype(F32) for part in _split3(-cum)]

        def proj(chunk):
            cols = pl.ds(chunk * COL_CHUNK, COL_CHUNK)
            return _dot_nt(ub, w_ref[cols, :]) + b_ref[:, cols]

        def head_tiles(r):
            for pair in range(N_HEADS // 2):
                both = r[:, pair * LANE:(pair + 1) * LANE]
                yield 2 * pair, both
                yield 2 * pair + 1, pltpu.roll(both, HEAD_DIM, 1)

        for h, val in head_tiles(proj(0)):
            extra = jnp.where((lane >= L_CK) & (lane < L_CK + 3), 1.0, 0.0)
            extra = _place3(lane, L_CQ, [part[:, h:h + 1] for part in cq], extra)
            qp_ref[h] = jnp.where(lane < HEAD_DIM, val * Q_SCALE, extra).astype(BF16)
        for h, val in head_tiles(proj(1)):
            ones = ((lane >= L_CQ) & (lane < L_CQ + 3)) | ((lane >= L_LSE) & (lane < L_LSE + 3))
            extra = _place3(lane, L_CK, [part[:, h:h + 1] for part in ck], jnp.where(ones, 1.0, 0.0))
            kp_ref[h] = jnp.where(lane < HEAD_DIM, val, extra).astype(BF16)
        for h, val in head_tiles(proj(2)):
            extra = jnp.where((lane >= HEAD_DIM) & (lane < HEAD_DIM + 3), -1.0, 0.0)
            vp_ref[h] = jnp.where(lane < HEAD_DIM, val, extra).astype(BF16)
        p_ref[...] = proj(3)
        ga_ref[...] = proj(4)
        gp_ref[...] = proj(5)

    head_block = pl.BlockSpec((N_HEADS, tile, LANE), lambda i: (0, i, 0))
    tok = lambda width: pl.BlockSpec((tile, width), lambda i: (i, 0))
    whole = lambda a: pl.BlockSpec(a.shape, lambda i: (0,) * a.ndim)
    padded = jax.ShapeDtypeStruct((N_HEADS, seq, LANE), BF16)
    half = jax.ShapeDtypeStruct((seq, D_ATT), F32)
    return pl.pallas_call(
        body, name="inproj_forward", grid=(nt,),
        in_specs=[tok(D), whole(mod), whole(w_main), whole(w_f), whole(b_main), whole(b_f)],
        out_specs=[head_block, head_block, head_block, tok(LANE), tok(D_POOL), tok(D_ATT), tok(D_POOL),
                   tok(D)],
        out_shape=[padded, padded, padded, jax.ShapeDtypeStruct((seq, LANE), F32), half, half, half,
                   jax.ShapeDtypeStruct((seq, D), BF16)],
        scratch_shapes=[pltpu.VMEM((8, LANE), F32)],
        compiler_params=_params(("arbitrary",)),
    )(x, mod, w_main, w_f, b_main, b_f)


def _attention_forward(qp, kp, vp, w_out, tile):
    seq = qp.shape[1]
    nb = seq // tile
    steps = (N_HEADS // 2) * nb

    def body(q_ref, k_ref, v_ref, wo_ref, att_ref, q2t_ref, wo_all_ref, s_a, s_b, m_ref, acc_ref,
             send_sems, recv_sems, local_sem):
        step = pl.program_id(0) * nb + pl.program_id(1)
        start, relay_near, relay_far, finish_near, finish_far = _gather_stages(
            wo_ref, wo_all_ref, send_sems, recv_sems, local_sem.at[0])
        pl.when(step == 0)(start)
        pl.when(step == steps // 4)(relay_near)
        pl.when(step == (3 * steps) // 4)(relay_far)

        i = pl.program_id(1)
        sub = lax.broadcasted_iota(jnp.int32, (LANE, tile), 0)
        row = lax.broadcasted_iota(jnp.int32, (tile, tile), 0)
        col = lax.broadcasted_iota(jnp.int32, (tile, tile), 1)
        q = [q_ref[0], q_ref[1]]

        def scores(buf, kb):
            rows = pl.ds(pl.multiple_of(kb * tile, tile), tile)
            for hh in range(2):
                buf[hh] = _dot_nt(k_ref[hh, rows, :], q[hh])

        def absorb(buf, kb, masked):
            rows = pl.ds(pl.multiple_of(kb * tile, tile), tile)
            for hh in range(2):
                m = m_ref[hh, 0:1, :]
                s = buf[hh]
                if masked:
                    s = jnp.where(row <= col, s, -1e30)
                m_new = jnp.maximum(m, jnp.max(s, axis=0, keepdims=True))
                p = jnp.exp(s - m_new).astype(BF16)
                acc_ref[hh] = jnp.exp(m - m_new) * acc_ref[hh] + _dot_tn(v_ref[hh, rows, :], p)
                m_ref[hh, 0:1, :] = m_new

        def two_blocks(j, _):
            scores(s_b, 2 * j + 1)
            absorb(s_a, 2 * j, False)
            scores(s_a, 2 * j + 2)
            absorb(s_b, 2 * j + 1, False)
            return 0

        def last_block():
            absorb(s_a, i, True)

        def last_two_blocks():
            scores(s_b, i)
            absorb(s_a, i - 1, False)
            absorb(s_b, i, True)

        scores(s_a, 0)
        m_ref[...] = jnp.full(m_ref.shape, -1e30, F32)
        acc_ref[...] = jnp.zeros_like(acc_ref)
        lax.fori_loop(0, i // 2, two_blocks, 0)
        lax.cond(i % 2 == 0, last_block, last_two_blocks)
        outs = []
        for hh in range(2):
            m, acc = m_ref[hh, 0:1, :], acc_ref[hh]
            l = -acc[HEAD_DIM:HEAD_DIM + 1, :]
            outs.append((acc / l)[:HEAD_DIM, :])
            neg_lse = [part.astype(F32) for part in _split3(-(m + jnp.log(l)))]
            q2t_ref[hh] = _place3(sub, L_LSE, neg_lse, q[hh].astype(F32).T).astype(BF16)
        att_ref[...] = jnp.concatenate(outs, axis=0).T
        @pl.when(step == steps - 1)
        def _():
            finish_near()
            finish_far()

    pair = pl.BlockSpec((2, tile, LANE), lambda hp, i: (hp, i, 0))
    full = pl.BlockSpec((2, seq, LANE), lambda hp, i: (hp, 0, 0))
    return pl.pallas_call(
        body, name="attention_forward", grid=(N_HEADS // 2, nb),
        in_specs=[pair, full, full, ANY],
        out_specs=[pl.BlockSpec((tile, LANE), lambda hp, i: (i, hp)),
                   pl.BlockSpec((2, LANE, tile), lambda hp, i: (hp, 0, i)), ANY],
        out_shape=[jax.ShapeDtypeStruct((seq, D_ATT), F32),
                   jax.ShapeDtypeStruct((N_HEADS, LANE, seq), BF16),
                   jax.ShapeDtypeStruct((N_DEV,) + w_out.shape, w_out.dtype)],
        scratch_shapes=[pltpu.VMEM((2, tile, tile), F32), pltpu.VMEM((2, tile, tile), F32),
                        pltpu.VMEM((2, 8, tile), F32), pltpu.VMEM((2, LANE, tile), F32),
                        pltpu.SemaphoreType.DMA((N_GATHER_SEMS,)), pltpu.SemaphoreType.DMA((N_GATHER_SEMS,)),
                        pltpu.SemaphoreType.DMA((1,))],
        compiler_params=_params(("arbitrary", "arbitrary")),
    )(qp, kp, vp, w_out)


def _window_sum(x, halo, window, transposed):
    tile = x.shape[0]

    def split_cat(a):
        hi = a.astype(BF16)
        return jnp.concatenate([hi, (a - hi.astype(F32)).astype(BF16)], axis=1)

    def fold(r):
        return r[:, :LANE] + r[:, LANE:]

    r = lax.broadcasted_iota(jnp.int32, (tile, tile), 0)
    c = lax.broadcasted_iota(jnp.int32, (tile, tile), 1)
    rh = lax.broadcasted_iota(jnp.int32, (HALO, HALO), 0)
    ch = lax.broadcasted_iota(jnp.int32, (HALO, HALO), 1)
    if not transposed:
        band = (c <= r) & (r - c < window)
        edge = (rh + HALO - ch) < window
    else:
        band = (r <= c) & (c - r < window)
        edge = (HALO + ch - rh) < window
    out = fold(_dot(band.astype(BF16), split_cat(x)))
    reach = fold(_dot(edge.astype(BF16), split_cat(halo)))
    if not transposed:
        return jnp.concatenate([out[:HALO] + reach, out[HALO:]], axis=0)
    return jnp.concatenate([out[:tile - HALO], out[tile - HALO:] + reach], axis=0)


def _silu_parts(g):
    sig = _sigmoid(g)
    return g * sig, sig * (1.0 + g * (1.0 - sig))


def _middle(x, tgt, att, g_att, g_pool, p, vecs, pool_vecs, w_out, w_pool, tile):
    seq = x.shape[0]
    nt = seq // tile
    halo_blocks = tile // HALO

    def body(x_ref, tgt_ref, att_ref, ga_ref, gp_ref, p_ref, ph_ref, vec_ref, pvec_ref, wo_ref, wp_ref,
             dxa_ref, do2_ref, dga_ref, dgp_ref, dpooled_ref, gwo_ref, dwp_ref, dvec_ref, dwo_ref, dpvec_ref):
        i = pl.program_id(0)

        @pl.when(i == 0)
        def _():
            dwo_ref[...] = jnp.zeros_like(dwo_ref)
            dwp_ref[...] = jnp.zeros_like(dwp_ref)
            dvec_ref[...] = jnp.zeros_like(dvec_ref)
            dpvec_ref[...] = jnp.zeros_like(dpvec_ref)

        gate, b_out, ln_g, ln_b = (vec_ref[k:k + 1, :] for k in range(4))
        b_pool, pool_scale = pvec_ref[0:1, :], pvec_ref[1:2, :]
        x = x_ref[...]
        p = p_ref[...]
        p_halo = ph_ref[...] * jnp.where(i > 0, 1.0, 0.0)
        pos = i * tile + lax.broadcasted_iota(jnp.int32, (tile, 1), 0) + 1

        pooled, mixed = [], []
        for g, window in enumerate(POOL_WINDOWS):
            cols = slice(g * GROUP_DIM, (g + 1) * GROUP_DIM)
            wsum = _window_sum(p[:, cols], p_halo[:, cols], window, False)
            count = jnp.minimum(pos, window).astype(F32)
            pooled.append(wsum / count - p[:, cols])
            mixed.append(_dot(pooled[g].astype(BF16), wp_ref[g]) + b_pool[:, cols])
        mixed = jnp.concatenate(mixed, axis=1)
        pool = mixed * pool_scale

        att = att_ref[...]
        g_att, g_pool = ga_ref[...], gp_ref[...]
        silu_a, dsilu_a = _silu_parts(g_att)
        silu_p, dsilu_p = _silu_parts(g_pool)
        y_in = jnp.concatenate([att * silu_a, pool * silu_p], axis=1)
        y = _dot(y_in.astype(BF16), wo_ref[...]) + b_out
        h = ALPHA * x + gate * y
        mu = jnp.mean(h, axis=1, keepdims=True)
        hc = h - mu
        var = jnp.mean(hc * hc, axis=1, keepdims=True)
        rstd = lax.rsqrt(var + LN_EPS)
        yhat = hc * rstd
        diff = yhat * ln_g + ln_b - tgt_ref[...]
        loss_rows = jnp.sum(diff * diff, axis=1, keepdims=True)
        d_out = diff * (1.0 / D)

        d_yhat = d_out * ln_g
        dh = rstd * (d_yhat - jnp.mean(d_yhat, axis=1, keepdims=True)
                     - yhat * jnp.mean(d_yhat * yhat, axis=1, keepdims=True))
        dxa_ref[...] = ALPHA * dh
        dy = dh * gate
        dyb = dy.astype(BF16)
        lane = _lanes((1, D))
        loss_row = jnp.where(lane == 0, (0.5 / D) * jnp.sum(loss_rows, axis=0, keepdims=True), 0.0)
        dvec_ref[5:6, :] += jnp.sum(dh * y, axis=0, keepdims=True)
        dvec_ref[0:1, :] += jnp.sum(dy, axis=0, keepdims=True)
        dvec_ref[1:2, :] += jnp.sum(d_out * yhat, axis=0, keepdims=True)
        dvec_ref[2:3, :] += jnp.sum(d_out, axis=0, keepdims=True)
        dvec_ref[4:5, :] += loss_row

        dwo_ref[...] += _dot(y_in.T.astype(BF16), dyb)
        d_yin = _dot_nt(dyb, wo_ref[...])
        d_a, d_pl = d_yin[:, :D_ATT], d_yin[:, D_ATT:]
        d_att = d_a * silu_a
        d_att_t = d_att.T
        prod_t = (d_att * att).T
        sub = lax.broadcasted_iota(jnp.int32, (HEAD_DIM, tile), 0)
        for h in range(N_HEADS):
            rows = slice(h * HEAD_DIM, (h + 1) * HEAD_DIM)
            delta = jnp.sum(prod_t[rows], axis=0, keepdims=True)
            extra = _place3(sub, 0, [part.astype(F32) for part in _split3(delta)], 0.0)
            do2_ref[h] = jnp.concatenate([d_att_t[rows], extra], axis=0).astype(BF16)
        dga_ref[...] = d_a * att * dsilu_a
        dgp_ref[...] = d_pl * pool * dsilu_p
        d_pool = d_pl * silu_p
        d_mixed = d_pool * pool_scale
        dpvec_ref[0:1, :] += jnp.sum(d_mixed, axis=0, keepdims=True)
        dpvec_ref[1:2, :] += jnp.sum(d_pool * mixed, axis=0, keepdims=True)
        d_pooled = []
        for g in range(len(POOL_WINDOWS)):
            cols = slice(g * GROUP_DIM, (g + 1) * GROUP_DIM)
            dmb = d_mixed[:, cols].astype(BF16)
            dwp_ref[g] += _dot(pooled[g].T.astype(BF16), dmb)
            d_pooled.append(_dot_nt(dmb, wp_ref[g]))
        dpooled_ref[...] = jnp.concatenate(d_pooled, axis=1)

        @pl.when(i == nt - 1)
        def _():
            gwo_ref[...] = dwo_ref[...].astype(BF16)
            dvec_ref[3:4, :] = jnp.concatenate([dpvec_ref[0:1, :], dpvec_ref[1:2, :]], axis=1)

    tok = lambda width: pl.BlockSpec((tile, width), lambda i: (i, 0))
    whole = lambda a: pl.BlockSpec(a.shape, lambda i: (0,) * a.ndim)
    halo = pl.BlockSpec((HALO, D_POOL), lambda i: (jnp.maximum(i * halo_blocks - 1, 0), 0))
    half = jax.ShapeDtypeStruct((seq, D_ATT), F32)
    outs = [jax.ShapeDtypeStruct((seq, D), F32), jax.ShapeDtypeStruct((N_HEADS, LANE, seq), BF16), half, half, half,
            jax.ShapeDtypeStruct(w_out.shape, BF16), jax.ShapeDtypeStruct(w_pool.shape, F32),
            jax.ShapeDtypeStruct(vecs.shape, F32)]
    return pl.pallas_call(
        body, name="middle", grid=(nt,),
        in_specs=[tok(D), tok(D), tok(D_ATT), tok(D_ATT), tok(D_POOL), tok(D_POOL), halo,
                  whole(vecs), whole(pool_vecs), whole(w_out), whole(w_pool)],
        out_specs=[tok(D), pl.BlockSpec((N_HEADS, LANE, tile), lambda i: (0, 0, i)),
                   tok(D_ATT), tok(D_POOL), tok(D_POOL),
                   whole(w_out), whole(w_pool), whole(vecs)],
        out_shape=outs,
        scratch_shapes=[pltpu.VMEM(w_out.shape, F32), pltpu.VMEM(pool_vecs.shape, F32)],
        compiler_params=_params(("arbitrary",)),
    )(x, tgt, att, g_att, g_pool, p, p, vecs, pool_vecs, w_out, w_pool)


def _attention_backward(q2t, kp, vp, do2t, gw_out, vecs, pool, tile):
    seq = kp.shape[1]
    nb = seq // tile
    last = N_HEADS // 2 - 1

    def body(qt_ref, k_ref, v_ref, dot_ref, gwo_hbm, vecs_hbm, pool_hbm,
             dq_ref, dk_ref, dv_ref, dcum_ref, g_out_ref, vecs_sum_ref, pool_sum_ref,
             dq_acc, dk_acc, dv_acc, gwo_ref, vecs_ref, pool_ref,
             r1, s2, r2, v_sib, v_chip, v_recv, p_sib, p_chip, p_recv, send_sems, recv_sems):
        hp = pl.program_id(0)
        start, middle, fold, finish = _reduce_stages(
            [gwo_ref], [g_out_ref], [r1], [s2], [r2],
            [(vecs_ref, vecs_sum_ref, v_sib, v_chip, v_recv), (pool_ref, pool_sum_ref, p_sib, p_chip, p_recv)],
            send_sems, recv_sems)

        @pl.when(hp == 0)
        def _():
            pltpu.sync_copy(gwo_hbm, gwo_ref)
            pltpu.sync_copy(vecs_hbm, vecs_ref)
            pltpu.sync_copy(pool_hbm, pool_ref)
            start()

        pl.when(hp == 1)(middle)
        pl.when(hp == 2)(fold)

        row = lax.broadcasted_iota(jnp.int32, (tile, tile), 0)
        col = lax.broadcasted_iota(jnp.int32, (tile, tile), 1)
        dq_acc[...] = jnp.zeros_like(dq_acc)

        def kv_block(kb, _):
            krows = pl.ds(pl.multiple_of(kb * tile, tile), tile)
            k = [k_ref[hh, krows, :] for hh in range(2)]
            v = [v_ref[hh, krows, :] for hh in range(2)]
            k_t = [k[hh].T for hh in range(2)]

            def q_block(qb, masked):
                qcols = pl.ds(pl.multiple_of(qb * tile, tile), tile)
                for hh in range(2):
                    q_t = qt_ref[hh, :, qcols]
                    do_t = dot_ref[hh, :, qcols]
                    s_t = _dot(k[hh], q_t)
                    if masked:
                        s_t = jnp.where(row <= col, s_t, -1e30)
                    p_t = jnp.exp(s_t)
                    ds_t = (p_t * _dot(v[hh], do_t)).astype(BF16)
                    dv_new = _dot_nt(do_t, p_t.astype(BF16))
                    dk_new = _dot_nt(q_t, ds_t)
                    if masked:
                        dv_acc[hh], dk_acc[hh] = dv_new, dk_new
                    else:
                        dv_acc[hh] += dv_new
                        dk_acc[hh] += dk_new
                    dq_acc[hh, :, qcols] += _dot(k_t[hh], ds_t)

            q_block(kb, True)

            def two_later_blocks(j, _):
                q_block(kb + 1 + 2 * j, False)
                q_block(kb + 2 + 2 * j, False)
                return 0

            later = nb - 1 - kb
            lax.fori_loop(0, later // 2, two_later_blocks, 0)
            pl.when(later % 2 == 1)(lambda: q_block(nb - 1, False))
            for hh in range(2):
                dk = dk_acc[hh]
                dk_ref[hh, :, krows] = dk.astype(BF16)
                dv_ref[hh, :, krows] = dv_acc[hh].astype(BF16)
                dcum_ref[hh, :, krows] = -dk[L_CK:L_CK + 1, :]
            return 0

        lax.fori_loop(0, nb, kv_block, 0)
        for hh in range(2):
            dq = dq_acc[hh]
            dcum_ref[hh] += dq[L_CQ:L_CQ + 1, :]
            dq_ref[hh] = (dq * Q_SCALE).astype(BF16)
        pl.when(hp == last)(finish)

    pair = pl.BlockSpec((2, seq, LANE), lambda hp: (hp, 0, 0))
    pair_t = pl.BlockSpec((2, LANE, seq), lambda hp: (hp, 0, 0))
    whole = lambda shape: pl.BlockSpec(shape, lambda hp: (0,) * len(shape))
    grad = jax.ShapeDtypeStruct((N_HEADS, LANE, seq), BF16)
    return pl.pallas_call(
        body, name="attention_backward", grid=(N_HEADS // 2,),
        in_specs=[pair_t, pair, pair, pair_t, ANY, ANY, ANY],
        out_specs=[pair_t, pair_t, pair_t, pl.BlockSpec((2, 1, seq), lambda hp: (hp, 0, 0)),
                   whole(gw_out.shape[1:]), whole(vecs.shape), whole(pool.shape)],
        out_shape=[grad, grad, grad, jax.ShapeDtypeStruct((N_HEADS, 1, seq), F32),
                   jax.ShapeDtypeStruct(gw_out.shape[1:], F32), jax.ShapeDtypeStruct(vecs.shape, F32),
                   jax.ShapeDtypeStruct(pool.shape, F32)],
        scratch_shapes=[pltpu.VMEM((2, LANE, seq), F32), pltpu.VMEM((2, LANE, tile), F32),
                        pltpu.VMEM((2, LANE, tile), F32), pltpu.VMEM(gw_out.shape, BF16),
                        pltpu.VMEM(vecs.shape, F32), pltpu.VMEM(pool.shape, F32)]
        + _reduce_scratch(gw_out, [vecs, pool]),
        compiler_params=_params(("arbitrary",)),
    )(q2t, kp, vp, do2t, gw_out, vecs, pool)


def _inproj_backward(dqp, dkp, dvp, d_cum, f, d_pooled, d_ga, d_gp, x, dxa, u, mod, w_main, w_f, tile):
    seq = x.shape[0]
    nt = seq // tile
    halo_blocks = tile // HALO

    def body(dq_ref, dk_ref, dv_ref, dcum_ref, f_ref, dpo_ref, dph_ref, dga_ref, dgp_ref, x_ref, dxa_ref, u_ref,
             mod_ref, w_ref, wf_ref,
             dx_ref, dproj_ref, dwf_ref, db_ref, dbf_ref, dmod_ref, carry_ref):
        step = pl.program_id(0)
        i = nt - 1 - step

        @pl.when(step == 0)
        def _():
            carry_ref[...] = jnp.zeros_like(carry_ref)
            dwf_ref[...] = jnp.zeros_like(dwf_ref)
            db_ref[...] = jnp.zeros_like(db_ref)
            dbf_ref[...] = jnp.zeros_like(dbf_ref)
            dmod_ref[...] = jnp.zeros_like(dmod_ref)

        ones = jnp.ones((8, tile), BF16)

        def emit(chunk, val):
            cols = pl.ds(chunk * COL_CHUNK, COL_CHUNK)
            db_ref[0:1, cols] += jnp.sum(val, axis=0, keepdims=True)
            vb = val.astype(BF16)
            dproj_ref[:, pl.ds((chunk - 3) * COL_CHUNK, COL_CHUNK)] = vb
            return _dot(vb, w_ref[cols, :])

        d_u = jnp.zeros((tile, D), F32)
        for chunk, ref in enumerate((dq_ref, dk_ref, dv_ref)):
            cols = pl.ds(chunk * COL_CHUNK, COL_CHUNK)
            val_t = ref[:, 0:HEAD_DIM, :].reshape(COL_CHUNK, tile)
            db_ref[:, cols] += _dot_nt(ones, val_t)
            d_u += _dot_tn(val_t, w_ref[cols, :])

        d_pooled = dpo_ref[...]
        d_halo = dph_ref[...] * jnp.where(i < nt - 1, 1.0, 0.0)
        pos = i * tile + lax.broadcasted_iota(jnp.int32, (tile, 1), 0) + 1
        d_p = []
        for g, window in enumerate(POOL_WINDOWS):
            cols = slice(g * GROUP_DIM, (g + 1) * GROUP_DIM)
            scaled = d_pooled[:, cols] / jnp.minimum(pos, window).astype(F32)
            d_p.append(_window_sum(scaled, d_halo[:, cols] * (1.0 / window), window, True) - d_pooled[:, cols])
        d_u += emit(3, jnp.concatenate(d_p, axis=1))
        d_u += emit(4, dga_ref[...])
        d_u += emit(5, dgp_ref[...])

        row = lax.broadcasted_iota(jnp.int32, (tile, tile), 0)
        col = lax.broadcasted_iota(jnp.int32, (tile, tile), 1)
        later = (row >= col).astype(BF16)
        d_logf = sum(_dot(part, later) for part in _split3(dcum_ref[:, 0, :])) + carry_ref[:, 0:1]
        carry_ref[:, 0:1] = d_logf[:, 0:1]
        d_f = d_logf * _sigmoid(-f_ref[...].T[0:N_HEADS, :])
        d_f = jnp.concatenate([d_f, jnp.zeros((LANE - N_HEADS, tile), F32)], axis=0)
        dbf_ref[...] += sum(_dot_nt(ones, part) for part in _split3(d_f))
        d_fb = d_f.astype(BF16)
        d_u += _dot_tn(d_fb, wf_ref[...])
        dwf_ref[...] += _dot(d_fb, u_ref[...])

        x = x_ref[...]
        dx_ref[...] = dxa_ref[...] + d_u * mod_ref[0:1, :]
        dmod_ref[0:1, :] += jnp.sum(d_u * x, axis=0, keepdims=True)
        dmod_ref[1:2, :] += jnp.sum(d_u, axis=0, keepdims=True)

    rev = lambda step: nt - 1 - step
    tok = lambda width: pl.BlockSpec((tile, width), lambda s: (rev(s), 0))
    head_block = pl.BlockSpec((N_HEADS, LANE, tile), lambda s: (0, 0, rev(s)))
    whole = lambda a: pl.BlockSpec(a.shape, lambda s: (0,) * a.ndim)
    halo = pl.BlockSpec((HALO, D_POOL), lambda s: (jnp.minimum((rev(s) + 1) * halo_blocks, seq // HALO - 1), 0))
    small = lambda width: jax.ShapeDtypeStruct((8, width), F32)
    n_rest = N_MAIN - OFF_P
    return pl.pallas_call(
        body, name="inproj_backward", grid=(nt,),
        in_specs=[head_block, head_block, head_block, pl.BlockSpec((N_HEADS, 1, tile), lambda s: (0, 0, rev(s))),
                  tok(LANE), tok(D_POOL), halo, tok(D_ATT), tok(D_POOL),
                  tok(D), tok(D), tok(D),
                  whole(mod), whole(w_main), whole(w_f)],
        out_specs=[tok(D), tok(n_rest), pl.BlockSpec((LANE, D), lambda s: (0, 0)),
                   pl.BlockSpec((8, N_MAIN), lambda s: (0, 0)), pl.BlockSpec((8, LANE), lambda s: (0, 0)),
                   pl.BlockSpec((8, D), lambda s: (0, 0))],
        out_shape=[jax.ShapeDtypeStruct((seq, D), F32), jax.ShapeDtypeStruct((seq, n_rest), BF16),
                   jax.ShapeDtypeStruct((LANE, D), F32), small(N_MAIN), small(LANE), small(D)],
        scratch_shapes=[pltpu.VMEM((8, LANE), F32)],
        compiler_params=_params(("arbitrary",)),
    )(dqp, dkp, dvp, d_cum, f, d_pooled, d_pooled, d_ga, d_gp, x, dxa, u, mod, w_main, w_f)


def _weight_grads(dq_t, dk_t, dv_t, dw_f, dproj, u, k_tile, half, reduce_in=None, small=None, rows=None):
    seq = u.shape[0]
    nk = seq // k_tile
    head_rows = N_HEADS * HEAD_DIM
    width = D // 2
    shard = D_IN // N_DEV

    def body(dq_ref, dk_ref, dv_ref, dwf_ref, dp_ref, u_ref, *rest):
        k = pl.program_id(0)
        if reduce_in is None:
            out_ref, acc_ref = rest
        else:
            (other_hbm, small_ref, rows_ref, sum_ref, own_sum_ref, total_ref, rows_all_ref, acc_ref,
             r1, s2, r2, send_sems, recv_sems, kept, kept_sems,
             out_ref, own_r1, own_s2, own_r2, sm_sib, sm_chip, sm_recv, rows_land, own_send_sems, own_recv_sems) = rest
            start, middle, fold, finish = _reduce_stages(
                [other_hbm], [sum_ref], [r1], [s2], [r2], [], send_sems, recv_sems, own=[(kept, kept_sems)])
            own_stages = _reduce_stages(
                [out_ref], [own_sum_ref], [own_r1], [own_s2], [own_r2],
                [(small_ref, total_ref, sm_sib, sm_chip, sm_recv)], own_send_sems, own_recv_sems,
                rows=(rows_ref, rows_land, rows_all_ref))
            pl.when(k == 0)(start)
            pl.when(k == 1)(middle)
            pl.when(k == nk // 2)(fold)

        @pl.when(k == 0)
        def _():
            acc_ref[...] = jnp.zeros_like(acc_ref)

        tokens = u_ref[...]
        for j, ref in enumerate((dq_ref, dk_ref, dv_ref)):
            acc_ref[pl.ds(j * head_rows, head_rows), :] += _dot(ref[...].reshape(head_rows, k_tile), tokens)
        for j in range(dproj.shape[1] // COL_CHUNK):
            cols = pl.ds(j * COL_CHUNK, COL_CHUNK)
            acc_ref[pl.ds(F_HI + j * COL_CHUNK, COL_CHUNK), :] += _dot_tn(dp_ref[:, cols], tokens)

        @pl.when(k == nk - 1)
        def _():
            acc_ref[F_LO:F_HI, :] = dwf_ref[0:N_HEADS, :]
            for slot in range(N_DEV):
                out_ref[slot] = acc_ref[slot * shard:(slot + 1) * shard, :].astype(BF16)
            if reduce_in is not None:
                own_stages[0]()
                finish()
                for stage in own_stages[1:]:
                    stage()

    heads = pl.BlockSpec((N_HEADS, HEAD_DIM, k_tile), lambda k: (0, 0, k))
    slots = jax.ShapeDtypeStruct((N_DEV, shard, width), BF16)
    in_specs = [heads, heads, heads, pl.BlockSpec((dw_f.shape[0], width), lambda k: (0, half)),
                pl.BlockSpec((k_tile, dproj.shape[1]), lambda k: (k, 0)), pl.BlockSpec((k_tile, width), lambda k: (k, half))]
    out_specs = [pl.BlockSpec((N_DEV, shard, width), lambda k: (0, 0, 0))]
    out_shape, scratch, operands = [slots], [pltpu.VMEM((D_IN, width), F32)], [dq_t, dk_t, dv_t, dw_f, dproj, u]
    if reduce_in is not None:
        assert nk >= 4 and reduce_in.shape == slots.shape
        whole = lambda a: pl.BlockSpec(a.shape, lambda k: (0,) * a.ndim)
        shard_sum = jax.ShapeDtypeStruct((shard, width), F32)
        in_specs += [ANY, whole(small), whole(rows)]
        operands += [reduce_in, small, rows]
        out_shape = [shard_sum, shard_sum, jax.ShapeDtypeStruct(small.shape, F32),
                     jax.ShapeDtypeStruct((N_DEV,) + rows.shape, F32)]
        out_specs = [whole(a) for a in out_shape]
        scratch += (_reduce_scratch(slots, []) + [pltpu.VMEM((4, shard, width), BF16), pltpu.SemaphoreType.DMA((4,))]
                    + [pltpu.VMEM(slots.shape, BF16)] + _reduce_scratch(slots, [small], rows))
    return pl.pallas_call(
        body, name="weight_grads_" + ("left", "right")[half], grid=(nk,),
        in_specs=in_specs, out_specs=out_specs, out_shape=out_shape, scratch_shapes=scratch,
        compiler_params=_params(("arbitrary",)),
    )(*operands)


def _adamw(w, g, m, v):
    m = ADAM_B1 * m + (1.0 - ADAM_B1) * g
    v = ADAM_B2 * v + (1.0 - ADAM_B2) * (g * g)
    m_hat = m / (1.0 - ADAM_B1 ** ADAM_STEP)
    v_hat = v / (1.0 - ADAM_B2 ** ADAM_STEP)
    delta = -ADAM_LR * (m_hat / (jnp.sqrt(v_hat) + ADAM_EPS) + ADAM_WD * w)
    return delta, m, v


SUBLANES = 8


def _adamw_packed(g_parts, w, m, v, name, chunks=4):
    n_parts, (rows, part_cols) = len(g_parts), g_parts[0].shape
    cols = n_parts * part_cols
    per_row, per_part = cols // LANE, part_cols // LANE
    assert part_cols % LANE == 0 and per_row == SUBLANES and w.shape == (rows * per_row, LANE)
    step = -(-rows // (chunks * SUBLANES)) * SUBLANES
    bounds = [(r0, min(r0 + step, rows)) for r0 in range(0, rows, step)]

    def body(*refs):
        g_hbm, (w_hbm, m_hbm, v_hbm, og_hbm, od_hbm, om_hbm, ov_hbm) = refs[:n_parts], refs[n_parts:n_parts + 7]
        g_buf, in_buf, out_buf, in_sems, out_sems = refs[n_parts + 7:]

        def copies_in(c):
            r0, r1 = bounds[c]
            packed = slice(r0 * per_row, r1 * per_row)
            return [pltpu.make_async_copy(g_hbm[p].at[r0:r1], g_buf.at[p, r0:r1], in_sems.at[c, 3 + p])
                    for p in range(n_parts)] + [
                pltpu.make_async_copy(src.at[packed], in_buf.at[i, packed], in_sems.at[c, i])
                for i, src in enumerate((w_hbm, m_hbm, v_hbm))]

        def copies_out(c):
            r0, r1 = bounds[c]
            packed = slice(r0 * per_row, r1 * per_row)
            return [pltpu.make_async_copy(out_buf.at[i, packed], dst.at[packed], out_sems.at[c, i])
                    for i, dst in enumerate((og_hbm, od_hbm, om_hbm, ov_hbm))]

        for c in range(len(bounds)):
            for cp in copies_in(c):
                cp.start()
        for c, (r0, r1) in enumerate(bounds):
            for cp in copies_in(c):
                cp.wait()
            for j in range(per_row):
                lanes = pl.ds(r0 * per_row + j, r1 - r0, stride=per_row)
                g_part = g_buf[j // per_part, r0:r1, (j % per_part) * LANE:(j % per_part + 1) * LANE]
                results = _adamw(in_buf[0, lanes, :], g_part, in_buf[1, lanes, :], in_buf[2, lanes, :])
                for i, val in enumerate((g_part,) + results):
                    out_buf[i, lanes, :] = val
            for cp in copies_out(c):
                cp.start()
        for c in range(len(bounds)):
            for cp in copies_out(c):
                cp.wait()

    shape = jax.ShapeDtypeStruct(w.shape, F32)
    return pl.pallas_call(
        body, name=name,
        in_specs=[ANY] * (n_parts + 3), out_specs=[ANY] * 4, out_shape=[shape] * 4,
        scratch_shapes=[pltpu.VMEM((n_parts, rows, part_cols), F32), pltpu.VMEM((3,) + w.shape, F32),
                        pltpu.VMEM((4,) + w.shape, F32),
                        pltpu.SemaphoreType.DMA((len(bounds), 3 + n_parts)), pltpu.SemaphoreType.DMA((len(bounds), 4))],
        compiler_params=_params(),
    )(*g_parts, w, m, v)


def _ada_adamw(sc_all, d_ada, w, m, v, chunks=4):
    rows, cols = w.shape
    step, sub = rows // chunks, 32
    assert rows % chunks == 0 and step % LANE == 0 and step % sub == 0

    def body(sc_ref, d_ref, w_hbm, m_hbm, v_hbm, og_hbm, od_hbm, om_hbm, ov_hbm, in_buf, out_buf, in_sems, out_sems):
        def copies_in(c):
            part = slice(c * step, (c + 1) * step)
            return [pltpu.make_async_copy(src.at[part], in_buf.at[i, part], in_sems.at[c, i])
                    for i, src in enumerate((w_hbm, m_hbm, v_hbm))]

        def copies_out(c):
            part = slice(c * step, (c + 1) * step)
            return [pltpu.make_async_copy(out_buf.at[i, part], dst.at[part], out_sems.at[c, i])
                    for i, dst in enumerate((og_hbm, od_hbm, om_hbm, ov_hbm))]

        for c in range(chunks):
            for cp in copies_in(c):
                cp.start()
        for c in range(chunks):
            sc_t = sc_ref[:, c * step:(c + 1) * step].T
            for cp in copies_in(c):
                cp.wait()
            for r0 in range(0, step, sub):
                part = slice(c * step + r0, c * step + r0 + sub)
                g = sc_t[r0:r0 + sub, 0:1] * d_ref[0:1, :]
                for b in range(1, N_DEV):
                    g = g + sc_t[r0:r0 + sub, b:b + 1] * d_ref[b:b + 1, :]
                results = _adamw(in_buf[0, part, :], g, in_buf[1, part, :], in_buf[2, part, :])
                for i, val in enumerate((g,) + results):
                    out_buf[i, part, :] = val
            for cp in copies_out(c):
                cp.start()
        for c in range(chunks):
            for cp in copies_out(c):
                cp.wait()

    in_vmem = pl.BlockSpec(memory_space=pltpu.VMEM)
    shape = jax.ShapeDtypeStruct(w.shape, F32)
    return pl.pallas_call(
        body, name="ada_adamw",
        in_specs=[in_vmem, in_vmem, ANY, ANY, ANY], out_specs=[ANY] * 4, out_shape=[shape] * 4,
        scratch_shapes=[pltpu.VMEM((3,) + w.shape, F32), pltpu.VMEM((4,) + w.shape, F32),
                        pltpu.SemaphoreType.DMA((chunks, 3)), pltpu.SemaphoreType.DMA((chunks, 4))],
        compiler_params=_params(),
    )(sc_all, d_ada, w, m, v)


F_LO, F_HI = 3 * D_ATT, 3 * D_ATT + N_HEADS


def _split_forget(a, axis):
    idx = lambda lo, hi: tuple(slice(lo, hi) if d == axis else slice(None) for d in range(a.ndim))
    pad = [(0, LANE - N_HEADS) if d == axis else (0, 0) for d in range(a.ndim)]
    return jnp.concatenate([a[idx(0, F_LO)], a[idx(F_HI, D_IN)]], axis=axis), jnp.pad(a[idx(F_LO, F_HI)], pad)


def _join_forget(main, f, axis):
    idx = lambda lo, hi: tuple(slice(lo, hi) if d == axis else slice(None) for d in range(main.ndim))
    return jnp.concatenate([main[idx(0, F_LO)], f[idx(0, N_HEADS)], main[idx(F_LO, N_MAIN)]], axis=axis)


def _adamw_small(grad_rows, row_params, whole_params, summed_params, scalar_at):
    n_row, n_whole, n_sum = len(row_params), len(whole_params), len(summed_params)
    n = n_row + n_whole + n_sum

    def body(g_ref, *refs):
        n_in = 3 * n_row + 4 * (n_whole + n_sum)
        ins, outs = list(refs[:n_in]), refs[n_in:]
        for i in range(n):
            if i < n_row:
                row, lo, hi = row_params[i][:3]
                g = g_ref[row:row + 1, lo:hi]
            elif i < n_row + n_whole:
                g = ins.pop(0)[...]
            else:
                parts = ins.pop(0)
                g = parts[0]
                for k in range(1, N_DEV):
                    g = g + parts[k]
            w, m, v = (ins.pop(0)[...] for _ in range(3))
            outs[4 * i][...] = g
            outs[4 * i + 1][...], outs[4 * i + 2][...], outs[4 * i + 3][...] = _adamw(w, g, m, v)
        row, lane = scalar_at
        outs[4 * n][...] = g_ref[row:row + 1, lane:lane + 1]

    shapes = [p[3] for p in row_params] + [p[1] for p in whole_params] + [p[1] for p in summed_params]
    operands = [a for p in row_params for a in p[3:]] + [a for p in whole_params + summed_params for a in p]
    flat = pl.pallas_call(
        body, name="adamw_small",
        out_shape=[jax.ShapeDtypeStruct(w.shape, F32) for w in shapes for _ in range(4)]
        + [jax.ShapeDtypeStruct((1, 1), F32)],
        compiler_params=_params(),
    )(grad_rows, *operands)
    return [flat[4 * i:4 * i + 4] for i in range(n)], flat[4 * n].reshape(())


def kernel(x, c, w_ada, b_ada, w_in, b_in, w_pool_mix, b_pool_mix, pool_scale, w_out, b_out, ln_g, ln_b, loss_target, m_w_ada, m_b_ada, m_w_in, m_b_in, m_w_pool_mix, m_b_pool_mix, m_pool_scale, m_w_out, m_b_out, m_ln_g, m_ln_b, v_w_ada, v_b_ada, v_w_in, v_b_in, v_w_pool_mix, v_b_pool_mix, v_pool_scale, v_w_out, v_b_out, v_ln_g, v_ln_b):
    seq = x.shape[1]
    tile = min(256, seq)
    attn_tile = min(512, max(128, seq // 4))
    me = _dev_index(*_mesh_pos())
    x2, tgt = x[0], loss_target[0]

    rows_of = lambda a: jnp.swapaxes(a, 1, 2)[0]
    w_main, w_f, sc_all, ada_mine = _gather_and_ada(c, rows_of(w_in).astype(BF16), w_ada[0])
    ada = ada_mine.reshape(1, D_ADA) + b_ada
    shift, scale, gate = ada[:, 0:D], ada[:, D:2 * D], ada[:, 2 * D:]
    mod = jnp.concatenate([1.0 + scale, shift, jnp.zeros((6, D), F32)], axis=0)
    b_main, b_f = _split_forget(b_in, 1)

    qp, kp, vp, f, p, g_att, g_pool, u = _inproj_forward(x2, mod, w_main, w_f, b_main, b_f, tile)
    att, q2t, w_out_g = _attention_forward(qp, kp, vp, w_out[0].astype(BF16), attn_tile)

    vecs = jnp.concatenate([gate, b_out, ln_g, ln_b, jnp.zeros((4, D), F32)], axis=0)
    pool_vecs = jnp.concatenate([b_pool_mix.reshape(1, D_POOL), pool_scale, jnp.zeros((6, D_POOL), F32)], axis=0)
    dxa, do2, d_ga, d_gp, d_pooled, gw_out, dw_pool, dvec = _middle(
        x2, tgt, att, g_att, g_pool, p, vecs, pool_vecs, w_out_g.reshape(D, D), w_pool_mix[0].astype(BF16), tile)

    pool_rows = w_pool_mix.shape[1] * GROUP_DIM
    dqp, dkp, dvp, d_cum, g_out, dvec_sum, dw_pool_sum = _attention_backward(
        q2t, kp, vp, do2, gw_out.reshape(N_DEV, D // N_DEV, D), dvec, dw_pool.reshape(pool_rows, GROUP_DIM), attn_tile)
    dx, dproj, dw_f, db_main, db_f, dmod = _inproj_backward(
        dqp, dkp, dvp, d_cum, f, d_pooled, d_ga, d_gp, x2, dxa, u, mod, w_main, w_f, tile)
    (gw_left,) = _weight_grads(dqp, dkp, dvp, dw_f, dproj, u, min(512, seq), 0)
    d_ada = jnp.concatenate([dmod[1:2], dmod[0:1], dvec[5:6]], axis=1)
    g_in_left, g_in_right, g_b_in, d_ada_all = _weight_grads(
        dqp, dkp, dvp, dw_f, dproj, u, min(512, seq), 1, reduce_in=gw_left,
        small=_join_forget(db_main[0:1], db_f[0:1], 1), rows=d_ada)

    packed = lambda a: jnp.transpose(a.reshape(SUBLANES, LANE, -1), (2, 0, 1)).reshape(-1, LANE)
    outs_in = _adamw_packed((g_in_left, g_in_right), packed(w_in), packed(m_w_in), packed(v_w_in), "adamw_w_in")
    g_w_in, d_w_in, nm_w_in, nv_w_in = (
        jnp.transpose(a.reshape(-1, SUBLANES, LANE), (1, 2, 0)).reshape(D, -1) for a in outs_in)
    flat_pool = lambda a: a.reshape(1, D_POOL)
    pool_2d = lambda a: a.reshape(pool_rows, GROUP_DIM)
    rows, loss = _adamw_small(
        dvec_sum,
        [(0, 0, D, b_out, m_b_out, v_b_out), (1, 0, D, ln_g, m_ln_g, v_ln_g), (2, 0, D, ln_b, m_ln_b, v_ln_b),
         (3, 0, D_POOL, flat_pool(b_pool_mix), flat_pool(m_b_pool_mix), flat_pool(v_b_pool_mix)),
         (3, D_POOL, 2 * D_POOL, pool_scale, m_pool_scale, v_pool_scale)],
        [(g_out, w_out[0], m_w_out[0], v_w_out[0]),
         (dw_pool_sum, pool_2d(w_pool_mix), pool_2d(m_w_pool_mix), pool_2d(v_w_pool_mix)),
         (g_b_in, b_in, m_b_in, v_b_in)],
        [(d_ada_all, b_ada, m_b_ada, v_b_ada)],
        scalar_at=(4, 0))
    small = {"b_out": rows[0], "ln_g": rows[1], "ln_b": rows[2],
             "b_pool": [a.reshape(b_pool_mix.shape) for a in rows[3]], "pool_scale": rows[4],
             "w_pool": [a.reshape(w_pool_mix.shape) for a in rows[6]], "b_in": rows[7]}
    g_s, d_s, nm_s, nv_s = ({k: r[j] for k, r in small.items()} for j in range(4))
    g_w_out, d_w_out, nm_w_out, nv_w_out = rows[5]
    g_b_ada, d_b_ada, nm_b_ada, nv_b_ada = rows[8]

    d_ada_local = lax.dynamic_slice_in_dim(d_ada_all.reshape(N_DEV, D_ADA), me * (D_ADA // N_DEV), D_ADA // N_DEV, axis=1)
    g_w_ada, d_w_ada, nm_w_ada, nv_w_ada = _ada_adamw(sc_all, d_ada_local, w_ada[0], m_w_ada[0], v_w_ada[0])

    def ordered(w_ada_, b_ada_, w_in_, w_out_, s):
        return (w_ada_[None], b_ada_, w_in_[None], s["b_in"], s["w_pool"], s["b_pool"], s["pool_scale"],
                w_out_[None], s["b_out"], s["ln_g"], s["ln_b"])

    return (loss, dx[None],
            *ordered(g_w_ada, g_b_ada, g_w_in, g_w_out, g_s),
            *ordered(d_w_ada, d_b_ada, d_w_in, d_w_out, d_s),
            *ordered(nm_w_ada, nm_b_ada, nm_w_in, nm_w_out, nm_s),
            *ordered(nv_w_ada, nv_b_ada, nv_w_in, nv_w_out, nv_s))
```

```python
import jax
import jax.numpy as jnp
from jax import lax
from jax.experimental import pallas as pl
from jax.experimental.pallas import tpu as pltpu

F32 = jnp.float32
BF16 = jnp.bfloat16

N_DEV = 8
D = 1024
N_HEADS = 8
HEAD_DIM = 64
D_ATT = 512
D_POOL = 512
POOL_WINDOWS = (2, 4, 8, 16)
GROUP_DIM = 128
HALO = 16
LANE = 128
BF16_TILE_ROWS = 16
D_IN = 3080
D_ADA = 3072
N_MAIN = 3072
OFF_P = 1536
COL_CHUNK = 512
Q_SCALE = 0.125
LN_EPS = 1e-5
ALPHA = 2.0 ** 0.25
L_CQ, L_CK, L_LSE = 64, 67, 70

ADAM_LR, ADAM_B1, ADAM_B2, ADAM_EPS, ADAM_WD, ADAM_STEP = 0.001, 0.9, 0.999, 1e-08, 0.01, 10
VMEM_LIMIT = 56 * 1024 * 1024

MESH = pl.DeviceIdType.MESH
ANY = pl.BlockSpec(memory_space=pl.ANY)


def _params(sem=None, vmem=VMEM_LIMIT):
    return pltpu.CompilerParams(dimension_semantics=sem, vmem_limit_bytes=vmem)


def _split3(a):
    hi = a.astype(BF16)
    r = a - hi.astype(F32)
    mid = r.astype(BF16)
    lo = (r - mid.astype(F32)).astype(BF16)
    return hi, mid, lo


def _dot(a, b):
    return jnp.dot(a, b, preferred_element_type=F32)


def _dot_nt(a, b):
    return lax.dot_general(a, b, (((1,), (1,)), ((), ())), preferred_element_type=F32)


def _dot_tn(a, b):
    return lax.dot_general(a, b, (((0,), (0,)), ((), ())), preferred_element_type=F32)


def _dot3(m01, a):
    hi, mid, lo = _split3(a)
    return _dot(m01, hi) + _dot(m01, mid) + _dot(m01, lo)


def _sigmoid(z):
    return 1.0 / (1.0 + jnp.exp(-z))


def _lanes(shape):
    return lax.broadcasted_iota(jnp.int32, shape, len(shape) - 1)


def _place3(lane, base, parts, other):
    out = other
    for j in range(3):
        out = jnp.where(lane == base + j, parts[j], out)
    return out


def _mesh_pos():
    return lax.axis_index("x"), lax.axis_index("y"), lax.axis_index("c")


def _dev_index(px, py, pc):
    return 4 * px + 2 * py + pc


N_GATHER_SEMS = 11


def _gather_stages(src_ref, out_ref, send_sems, recv_sems, local_sem):
    x, y, c = _mesh_pos()
    me, sibling = (x, y, c), (x, y, 1 - c)
    nbr_x, nbr_y, diag = (1 - x, y), (x, 1 - y), (1 - x, 1 - y)
    half = out_ref.shape[-1] // 2
    left, right = pl.ds(0, half), pl.ds(half, half)

    def copy(k, block, to, cols=None, src=None):
        slot = out_ref.at[_dev_index(*block)]
        if cols is not None:
            slot = slot.at[:, cols]
            src = src if src is None else src.at[:, cols]
        return pltpu.make_async_remote_copy(
            src_ref=slot if src is None else src, dst_ref=slot, send_sem=send_sems.at[k], recv_sem=recv_sems.at[k],
            device_id=to, device_id_type=MESH)

    mine = pltpu.make_async_copy(src_ref, out_ref.at[_dev_index(*me)], local_sem)
    first = [copy(0, me, sibling, src=src_ref),
             copy(1, me, (*nbr_x, c), cols=left, src=src_ref), copy(2, me, (*nbr_y, c), cols=right, src=src_ref),
             copy(9, me, (*nbr_x, c), cols=right, src=src_ref), copy(10, me, (*nbr_y, c), cols=left, src=src_ref)]
    relay = [(1, nbr_x, left, nbr_x), (2, nbr_y, right, nbr_y), (3, diag, left, nbr_y), (4, diag, right, nbr_x)]
    other_half = [(9, nbr_x, right, nbr_x), (10, nbr_y, left, nbr_y)]
    onward = [copy(3, (*nbr_x, c), (*nbr_y, c), cols=left), copy(4, (*nbr_y, c), (*nbr_x, c), cols=right)]
    passed = [copy(4 + k, (*block, c), sibling, cols=None if k < 3 else cols) for k, block, cols, _ in relay]

    def start():
        mine.start()
        for cp in first:
            cp.start()

    def arrived(item):
        k, block, cols, frm = item
        copy(k, (*block, c), (*frm, c), cols=cols).wait_recv()

    def relay_near():
        for j in (0, 1):
            arrived(relay[j])
            onward[j].start()
        for j in (0, 1):
            arrived(other_half[j])
            passed[j].start()

    def relay_far():
        for j in (2, 3):
            arrived(relay[j])
            passed[j].start()

    def from_sibling(items):
        for k, block, cols, _ in items:
            copy(4 + k, (*block, 1 - c), me, cols=None if k < 3 else cols).wait_recv()

    def finish_near():
        copy(0, sibling, me).wait_recv()
        from_sibling(relay[:2])
        mine.wait()

    def finish_far():
        from_sibling(relay[2:])
        for cp in first + onward + passed:
            cp.wait_send()

    return start, relay_near, relay_far, finish_near, finish_far


N_REDUCE_SEMS = 10
N_SMALL_SEMS = 4
N_ROWS_SEMS = 7


def _reduce_stages(ins, gs, r1, s2, r2, smalls, send_sems, recv_sems, rows=None, own=None):
    n = len(ins)
    x, y, c = _mesh_pos()
    me = _dev_index(x, y, c)
    sibling = (x, y, 1 - c)
    chips = [(x, y), (1 - x, y), (x, 1 - y), (1 - x, 1 - y)]
    peers = []
    for p in range(1, N_DEV):
        px, py, pc = (p >> 2) & 1, (p >> 1) & 1, p & 1
        peers.append((1 - x if px else x, 1 - y if py else y, 1 - c if pc else c))
    base_small = N_REDUCE_SEMS * n

    def remote(src, dst, k, to):
        return pltpu.make_async_remote_copy(src_ref=src, dst_ref=dst, send_sem=send_sems.at[k],
                                            recv_sem=recv_sems.at[k], device_id=to, device_id_type=MESH)

    def level1(a, q):
        return remote(ins[a].at[_dev_index(*chips[q], 1 - c)], r1[a].at[q], N_REDUCE_SEMS * a + q, sibling)

    def level2(a, k):
        half = ins[a].shape[-1] // 2
        left, right = pl.ds(0, half), pl.ds(half, half)
        nbr_x, nbr_y = (*chips[1], c), (*chips[2], c)
        src_slot, dst_slot, cols, to = [(0, 0, left, nbr_x), (1, 1, right, nbr_y), (2, 2, left, nbr_x),
                                        (2, 2, right, nbr_y), (0, 0, right, nbr_x), (1, 1, left, nbr_y)][k]
        return remote(s2[a].at[src_slot, :, cols], r2[a].at[dst_slot, :, cols], N_REDUCE_SEMS * a + 4 + k, to)

    to_sibling = [remote(sm[0], sm[2], base_small + 4 * i, sibling) for i, sm in enumerate(smalls)]
    to_chips = [[remote(sm[3], sm[4].at[j], base_small + 4 * i + 1 + j, (*chips[j + 1], c)) for j in range(3)]
                for i, sm in enumerate(smalls)]
    if rows is not None:
        rows_ref, land_ref, all_ref = rows
        base_rows = base_small + 4 * len(smalls)
        row_sends = [remote(rows_ref, land_ref.at[me], base_rows + k, to) for k, to in enumerate(peers)]

    order = (3, 1, 2, 0)

    def mine(a, q):
        buf, sems = own[a]
        return pltpu.make_async_copy(ins[a].at[_dev_index(*chips[q], c)], buf.at[q], sems.at[q])

    def start():
        for a in range(n):
            for q in order:
                level1(a, q).start()
            if own is not None:
                for q in order:
                    mine(a, q).start()
        for cp in to_sibling:
            cp.start()
        if rows is not None:
            for cp in row_sends:
                cp.start()
            land_ref[me] = rows_ref[...]

    def middle():
        for a in range(n):
            for q in order:
                level1(a, q).wait_recv()
                if own is None:
                    kept = ins[a][_dev_index(*chips[q], c)]
                else:
                    mine(a, q).wait()
                    kept = own[a][0][q]
                pair = kept.astype(F32) + r1[a][q].astype(F32)
                if q == 0:
                    gs[a][...] = pair
                else:
                    s2[a][q - 1] = pair.astype(BF16)
                    for k in ((0,), (1,), (2, 3))[q - 1]:
                        level2(a, k).start()
        for i, (small_ref, _, sm_sib, sm_chip, _) in enumerate(smalls):
            to_sibling[i].wait_recv()
            sm_chip[...] = small_ref[...] + sm_sib[...]
            for cp in to_chips[i]:
                cp.start()

    def fold():
        for a in range(n):
            half = ins[a].shape[-1] // 2
            level2(a, 3).wait_recv()
            s2[a][0, :, half:] = (s2[a][0, :, half:].astype(F32) + r2[a][2, :, half:].astype(F32)).astype(BF16)
            level2(a, 4).start()
            level2(a, 2).wait_recv()
            s2[a][1, :, :half] = (s2[a][1, :, :half].astype(F32) + r2[a][2, :, :half].astype(F32)).astype(BF16)
            level2(a, 5).start()

    def finish():
        for a in range(n):
            for k in (0, 1, 4, 5):
                level2(a, k).wait_recv()
            gs[a][...] = gs[a][...] + r2[a][0].astype(F32) + r2[a][1].astype(F32)
            for q in range(4):
                level1(a, q).wait_send()
            for k in range(6):
                level2(a, k).wait_send()
        for i, (_, total_ref, _, sm_chip, sm_recv) in enumerate(smalls):
            for cp in to_chips[i]:
                cp.wait_recv()
            total = None
            for ax in range(2):
                for ay in range(2):
                    dx, dy = x != ax, y != ay
                    term = jnp.where(dx, jnp.where(dy, sm_recv[2], sm_recv[0]), jnp.where(dy, sm_recv[1], sm_chip[...]))
                    total = term if total is None else total + term
            total_ref[...] = total
            for cp in [to_sibling[i]] + to_chips[i]:
                cp.wait_send()
        if rows is not None:
            for k, frm in enumerate(peers):
                remote(rows_ref, land_ref.at[_dev_index(*frm)], base_rows + k, frm).wait_recv()
            all_ref[...] = land_ref[...]
            for cp in row_sends:
                cp.wait_send()

    return start, middle, fold, finish


def _reduce_scratch(shard, smalls, rows=None):
    out = [pltpu.VMEM((lead,) + shard.shape[1:], BF16) for lead in (4, 3, 3)]
    for small in smalls:
        out += [pltpu.VMEM(small.shape, F32), pltpu.VMEM(small.shape, F32), pltpu.VMEM((3,) + small.shape, F32)]
    n_sems = N_REDUCE_SEMS + N_SMALL_SEMS * len(smalls)
    if rows is not None:
        out.append(pltpu.VMEM((N_DEV,) + rows.shape, F32))
        n_sems += N_ROWS_SEMS
    return out + [pltpu.SemaphoreType.DMA((n_sems,))] * 2


def _dot3_rhs(a, b):
    a0, a1, a2 = _split3(a)
    b0, b1, b2 = _split3(b)
    return (_dot(a0, b0) + (_dot(a0, b1) + _dot(a1, b0))
            + (_dot(a0, b2) + _dot(a1, b1) + _dot(a2, b0)))


def _gather_and_ada(c, w_in_rows, w_ada):
    cols = w_ada.shape[1]
    shard = w_in_rows.shape[0]

    def body(c_ref, w_ref, wa_ref, w_main_hbm, w_f_ref, sc_ref, ada_ref,
             w_all_ref, w_f32, wm_buf, c_land, part, ada_land, send_sems, recv_sems, local_sem, x_send, x_recv, out_sems):
        x, y, cc = _mesh_pos()
        me = _dev_index(x, y, cc)
        peers = []
        for p in range(1, N_DEV):
            px, py, pc = (p >> 2) & 1, (p >> 1) & 1, p & 1
            peers.append((1 - x if px else x, 1 - y if py else y, 1 - cc if pc else cc))

        def remote(src, dst, k, to):
            return pltpu.make_async_remote_copy(src_ref=src, dst_ref=dst, send_sem=x_send.at[k], recv_sem=x_recv.at[k],
                                                device_id=to, device_id_type=MESH)

        c_sends = [remote(c_ref, c_land.at[me], k, to) for k, to in enumerate(peers)]
        for cp in c_sends:
            cp.start()
        start, relay_near, relay_far, finish_near, finish_far = _gather_stages(
            w_ref, w_all_ref, send_sems, recv_sems, local_sem.at[0])
        start()
        c_land[me] = c_ref[...]
        for k, frm in enumerate(peers):
            remote(c_ref, c_land.at[_dev_index(*frm)], k, frm).wait_recv()
        c_all = jnp.concatenate([c_land[b] for b in range(N_DEV)], axis=0)
        sc = c_all * _sigmoid(c_all)
        sc_ref[...] = sc
        rows = _dot3_rhs(sc, wa_ref[...])
        for b in range(N_DEV):
            part[b] = rows[b:b + 1, :]
        a_sends = [remote(part.at[_dev_index(*to)], ada_land.at[me], 7 + k, to) for k, to in enumerate(peers)]
        for cp in a_sends:
            cp.start()
        ada_land[me] = part[me]

        relay_near()
        finish_near()

        far_chip = 2 * (1 - x) + (1 - y)

        def stage(slots):
            for slot in slots:
                w_f32[slot * shard:(slot + 1) * shard, :] = w_all_ref[slot].astype(F32)

        def far_rows(k):
            main_row = lambda r: r if r < F_LO else r - N_HEADS
            first, last = 2 * shard * k, 2 * shard * (k + 1) - 1
            first = F_HI if F_LO <= first < F_HI else first
            last = F_LO - 1 if F_LO <= last < F_HI else last
            return (main_row(first) // BF16_TILE_ROWS * BF16_TILE_ROWS,
                    -(-(main_row(last) + 1) // BF16_TILE_ROWS) * BF16_TILE_ROWS)

        def near_rows(k):
            lo, hi = far_rows(k)
            return [(i, a, b) for i, (a, b) in enumerate(((0, lo), (hi, N_MAIN))) if a < b]

        def forget_is_far(k):
            return 2 * shard * k < F_HI and 2 * shard * (k + 1) > F_LO

        def main_copy(i, lo, hi):
            return pltpu.make_async_copy(wm_buf.at[lo:hi], w_main_hbm.at[lo:hi], out_sems.at[i])

        def write_main(i, lo, hi):
            if lo < min(hi, F_LO):
                wm_buf[lo:min(hi, F_LO), :] = w_f32[lo:min(hi, F_LO), :].astype(BF16)
            if max(lo, F_LO) < hi:
                wm_buf[max(lo, F_LO):hi, :] = w_f32[max(lo, F_LO) + N_HEADS:hi + N_HEADS, :].astype(BF16)
            main_copy(i, lo, hi).start()

        def write_forget():
            w_f_ref[...] = jnp.concatenate(
                [w_f32[F_LO:F_HI, :], jnp.zeros((LANE - N_HEADS, D), F32)], axis=0).astype(BF16)

        for k in range(N_DEV // 2):
            @pl.when(far_chip == k)
            def _(k=k):
                stage([slot for slot in range(N_DEV) if slot // 2 != k])
                for i, lo, hi in near_rows(k):
                    write_main(i, lo, hi)
                if not forget_is_far(k):
                    write_forget()

        relay_far()
        for k, frm in enumerate(peers):
            remote(part.at[0], ada_land.at[_dev_index(*frm)], 7 + k, frm).wait_recv()
        ada_ref[...] = ada_land[...]
        finish_far()
        for cp in c_sends + a_sends:
            cp.wait_send()

        for k in range(N_DEV // 2):
            @pl.when(far_chip == k)
            def _(k=k):
                stage([2 * k, 2 * k + 1])
                write_main(2, *far_rows(k))
                if forget_is_far(k):
                    write_forget()
                for i, lo, hi in near_rows(k):
                    main_copy(i, lo, hi).wait()
                main_copy(2, *far_rows(k)).wait()

    vmem = pl.BlockSpec(memory_space=pltpu.VMEM)
    return pl.pallas_call(
        body, name="gather_weights",
        in_specs=[vmem, ANY, vmem], out_specs=[ANY, vmem, vmem, vmem],
        out_shape=[jax.ShapeDtypeStruct((N_MAIN, D), BF16), jax.ShapeDtypeStruct((LANE, D), BF16),
                   jax.ShapeDtypeStruct((N_DEV, D), F32), jax.ShapeDtypeStruct((N_DEV, 1, cols), F32)],
        scratch_shapes=[pltpu.VMEM((N_DEV,) + w_in_rows.shape, BF16), pltpu.VMEM((D_IN, D), F32), pltpu.VMEM((N_MAIN, D), BF16),
                        pltpu.VMEM((N_DEV, 1, D), F32), pltpu.VMEM((N_DEV, 1, cols), F32), pltpu.VMEM((N_DEV, 1, cols), F32),
                        pltpu.SemaphoreType.DMA((N_GATHER_SEMS,)), pltpu.SemaphoreType.DMA((N_GATHER_SEMS,)),
                        pltpu.SemaphoreType.DMA((1,)),
                        pltpu.SemaphoreType.DMA((14,)), pltpu.SemaphoreType.DMA((14,)), pltpu.SemaphoreType.DMA((3,))],
        compiler_params=_params(),
    )(c, w_in_rows, w_ada)


def _inproj_forward(x, mod, w_main, w_f, b_main, b_f, tile):
    seq = x.shape[0]
    nt = seq // tile

    def body(x_ref, mod_ref, w_ref, wf_ref, b_ref, bf_ref,
             qp_ref, kp_ref, vp_ref, f_ref, p_ref, ga_ref, gp_ref, u_ref, carry_ref):
        i = pl.program_id(0)

        @pl.when(i == 0)
        def _():
            carry_ref[...] = jnp.zeros_like(carry_ref)

        u = x_ref[...] * mod_ref[0:1, :] + mod_ref[1:2, :]
        ub = u.astype(BF16)
        u_ref[...] = ub

        f = _dot_nt(ub, wf_ref[...]) + bf_ref[...]
        f_ref[...] = f
        lane = _lanes((tile, LANE))
        log_f = jnp.where(lane < N_HEADS, jnp.minimum(f, 0.0) - jnp.log(1.0 + jnp.exp(-jnp.abs(f))), 0.0)
        row = lax.broadcasted_iota(jnp.int32, (tile, tile), 0)
        col = lax.broadcasted_iota(jnp.int32, (tile, tile), 1)
        tri = (row >= col).astype(BF16)
        cum = _dot3(tri, log_f) + carry_ref[0:1, :]
        carry_ref[0:1, :] = cum[tile - 1:tile, :]
        cq = [part.astype(F32) for part in _split3(cum)]
        ck = [part.astype(F32) for part in _split3(-cum)]

        def proj(chunk):
            cols = pl.ds(chunk * COL_CHUNK, COL_CHUNK)
            return _dot_nt(ub, w_ref[cols, :]) + b_ref[:, cols]

        def head_tiles(r):
            for pair in range(N_HEADS // 2):
                both = r[:, pair * LANE:(pair + 1) * LANE]
                yield 2 * pair, both
                yield 2 * pair + 1, pltpu.roll(both, HEAD_DIM, 1)

        for h, val in head_tiles(proj(0)):
            extra = jnp.where((lane >= L_CK) & (lane < L_CK + 3), 1.0, 0.0)
            extra = _place3(lane, L_CQ, [part[:, h:h + 1] for part in cq], extra)
            qp_ref[h] = jnp.where(lane < HEAD_DIM, val * Q_SCALE, extra).astype(BF16)
        for h, val in head_tiles(proj(1)):
            ones = ((lane >= L_CQ) & (lane < L_CQ + 3)) | ((lane >= L_LSE) & (lane < L_LSE + 3))
            extra = _place3(lane, L_CK, [part[:, h:h + 1] for part in ck], jnp.where(ones, 1.0, 0.0))
            kp_ref[h] = jnp.where(lane < HEAD_DIM, val, extra).astype(BF16)
        for h, val in head_tiles(proj(2)):
            extra = jnp.where((lane >= HEAD_DIM) & (lane < HEAD_DIM + 3), -1.0, 0.0)
            vp_ref[h] = jnp.where(lane < HEAD_DIM, val, extra).astype(BF16)
        p_ref[...] = proj(3)
        ga_ref[...] = proj(4)
        gp_ref[...] = proj(5)

    head_block = pl.BlockSpec((N_HEADS, tile, LANE), lambda i: (0, i, 0))
    tok = lambda width: pl.BlockSpec((tile, width), lambda i: (i, 0))
    whole = lambda a: pl.BlockSpec(a.shape, lambda i: (0,) * a.ndim)
    padded = jax.ShapeDtypeStruct((N_HEADS, seq, LANE), BF16)
    half = jax.ShapeDtypeStruct((seq, D_ATT), F32)
    return pl.pallas_call(
        body, name="inproj_forward", grid=(nt,),
        in_specs=[tok(D), whole(mod), whole(w_main), whole(w_f), whole(b_main), whole(b_f)],
        out_specs=[head_block, head_block, head_block, tok(LANE), tok(D_POOL), tok(D_ATT), tok(D_POOL),
                   tok(D)],
        out_shape=[padded, padded, padded, jax.ShapeDtypeStruct((seq, LANE), F32), half, half, half,
                   jax.ShapeDtypeStruct((seq, D), BF16)],
        scratch_shapes=[pltpu.VMEM((8, LANE), F32)],
        compiler_params=_params(("arbitrary",)),
    )(x, mod, w_main, w_f, b_main, b_f)


def _attention_forward(qp, kp, vp, w_out, tile):
    seq = qp.shape[1]
    nb = seq // tile
    steps = (N_HEADS // 2) * nb

    def body(q_ref, k_ref, v_ref, wo_ref, att_ref, q2t_ref, wo_all_ref, s_a, s_b, m_ref, acc_ref,
             send_sems, recv_sems, local_sem):
        step = pl.program_id(0) * nb + pl.program_id(1)
        start, relay_near, relay_far, finish_near, finish_far = _gather_stages(
            wo_ref, wo_all_ref, send_sems, recv_sems, local_sem.at[0])
        pl.when(step == 0)(start)
        pl.when(step == steps // 4)(relay_near)
        pl.when(step == (3 * steps) // 4)(relay_far)

        i = pl.program_id(1)
        sub = lax.broadcasted_iota(jnp.int32, (LANE, tile), 0)
        row = lax.broadcasted_iota(jnp.int32, (tile, tile), 0)
        col = lax.broadcasted_iota(jnp.int32, (tile, tile), 1)
        q = [q_ref[0], q_ref[1]]

        def scores(buf, kb):
            rows = pl.ds(pl.multiple_of(kb * tile, tile), tile)
            for hh in range(2):
                buf[hh] = _dot_nt(k_ref[hh, rows, :], q[hh])

        def absorb(buf, kb, masked):
            rows = pl.ds(pl.multiple_of(kb * tile, tile), tile)
            for hh in range(2):
                m = m_ref[hh, 0:1, :]
                s = buf[hh]
                if masked:
                    s = jnp.where(row <= col, s, -1e30)
                m_new = jnp.maximum(m, jnp.max(s, axis=0, keepdims=True))
                p = jnp.exp(s - m_new).astype(BF16)
                acc_ref[hh] = jnp.exp(m - m_new) * acc_ref[hh] + _dot_tn(v_ref[hh, rows, :], p)
                m_ref[hh, 0:1, :] = m_new

        def two_blocks(j, _):
            scores(s_b, 2 * j + 1)
            absorb(s_a, 2 * j, False)
            scores(s_a, 2 * j + 2)
            absorb(s_b, 2 * j + 1, False)
            return 0

        def last_block():
            absorb(s_a, i, True)

        def last_two_blocks():
            scores(s_b, i)
            absorb(s_a, i - 1, False)
            absorb(s_b, i, True)

        scores(s_a, 0)
        m_ref[...] = jnp.full(m_ref.shape, -1e30, F32)
        acc_ref[...] = jnp.zeros_like(acc_ref)
        lax.fori_loop(0, i // 2, two_blocks, 0)
        lax.cond(i % 2 == 0, last_block, last_two_blocks)
        outs = []
        for hh in range(2):
            m, acc = m_ref[hh, 0:1, :], acc_ref[hh]
            l = -acc[HEAD_DIM:HEAD_DIM + 1, :]
            outs.append((acc / l)[:HEAD_DIM, :])
            neg_lse = [part.astype(F32) for part in _split3(-(m + jnp.log(l)))]
            q2t_ref[hh] = _place3(sub, L_LSE, neg_lse, q[hh].astype(F32).T).astype(BF16)
        att_ref[...] = jnp.concatenate(outs, axis=0).T
        @pl.when(step == steps - 1)
        def _():
            finish_near()
            finish_far()

    pair = pl.BlockSpec((2, tile, LANE), lambda hp, i: (hp, i, 0))
    full = pl.BlockSpec((2, seq, LANE), lambda hp, i: (hp, 0, 0))
    return pl.pallas_call(
        body, name="attention_forward", grid=(N_HEADS // 2, nb),
        in_specs=[pair, full, full, ANY],
        out_specs=[pl.BlockSpec((tile, LANE), lambda hp, i: (i, hp)),
                   pl.BlockSpec((2, LANE, tile), lambda hp, i: (hp, 0, i)), ANY],
        out_shape=[jax.ShapeDtypeStruct((seq, D_ATT), F32),
                   jax.ShapeDtypeStruct((N_HEADS, LANE, seq), BF16),
                   jax.ShapeDtypeStruct((N_DEV,) + w_out.shape, w_out.dtype)],
        scratch_shapes=[pltpu.VMEM((2, tile, tile), F32), pltpu.VMEM((2, tile, tile), F32),
                        pltpu.VMEM((2, 8, tile), F32), pltpu.VMEM((2, LANE, tile), F32),
                        pltpu.SemaphoreType.DMA((N_GATHER_SEMS,)), pltpu.SemaphoreType.DMA((N_GATHER_SEMS,)),
                        pltpu.SemaphoreType.DMA((1,))],
        compiler_params=_params(("arbitrary", "arbitrary")),
    )(qp, kp, vp, w_out)


def _window_sum(x, halo, window, transposed):
    tile = x.shape[0]

    def split_cat(a):
        hi = a.astype(BF16)
        return jnp.concatenate([hi, (a - hi.astype(F32)).astype(BF16)], axis=1)

    def fold(r):
        return r[:, :LANE] + r[:, LANE:]

    r = lax.broadcasted_iota(jnp.int32, (tile, tile), 0)
    c = lax.broadcasted_iota(jnp.int32, (tile, tile), 1)
    rh = lax.broadcasted_iota(jnp.int32, (HALO, HALO), 0)
    ch = lax.broadcasted_iota(jnp.int32, (HALO, HALO), 1)
    if not transposed:
        band = (c <= r) & (r - c < window)
        edge = (rh + HALO - ch) < window
    else:
        band = (r <= c) & (c - r < window)
        edge = (HALO + ch - rh) < window
    out = fold(_dot(band.astype(BF16), split_cat(x)))
    reach = fold(_dot(edge.astype(BF16), split_cat(halo)))
    if not transposed:
        return jnp.concatenate([out[:HALO] + reach, out[HALO:]], axis=0)
    return jnp.concatenate([out[:tile - HALO], out[tile - HALO:] + reach], axis=0)


def _silu_parts(g):
    sig = _sigmoid(g)
    return g * sig, sig * (1.0 + g * (1.0 - sig))


def _middle(x, tgt, att, g_att, g_pool, p, vecs, pool_vecs, w_out, w_pool, tile):
    seq = x.shape[0]
    nt = seq // tile
    halo_blocks = tile // HALO

    def body(x_ref, tgt_ref, att_ref, ga_ref, gp_ref, p_ref, ph_ref, vec_ref, pvec_ref, wo_ref, wp_ref,
             dxa_ref, do2_ref, dga_ref, dgp_ref, dpooled_ref, gwo_ref, dwp_ref, dvec_ref, dwo_ref, dpvec_ref):
        i = pl.program_id(0)

        @pl.when(i == 0)
        def _():
            dwo_ref[...] = jnp.zeros_like(dwo_ref)
            dwp_ref[...] = jnp.zeros_like(dwp_ref)
            dvec_ref[...] = jnp.zeros_like(dvec_ref)
            dpvec_ref[...] = jnp.zeros_like(dpvec_ref)

        gate, b_out, ln_g, ln_b = (vec_ref[k:k + 1, :] for k in range(4))
        b_pool, pool_scale = pvec_ref[0:1, :], pvec_ref[1:2, :]
        x = x_ref[...]
        p = p_ref[...]
        p_halo = ph_ref[...] * jnp.where(i > 0, 1.0, 0.0)
        pos = i * tile + lax.broadcasted_iota(jnp.int32, (tile, 1), 0) + 1

        pooled, mixed = [], []
        for g, window in enumerate(POOL_WINDOWS):
            cols = slice(g * GROUP_DIM, (g + 1) * GROUP_DIM)
            wsum = _window_sum(p[:, cols], p_halo[:, cols], window, False)
            count = jnp.minimum(pos, window).astype(F32)
            pooled.append(wsum / count - p[:, cols])
            mixed.append(_dot(pooled[g].astype(BF16), wp_ref[g]) + b_pool[:, cols])
        mixed = jnp.concatenate(mixed, axis=1)
        pool = mixed * pool_scale

        att = att_ref[...]
        g_att, g_pool = ga_ref[...], gp_ref[...]
        silu_a, dsilu_a = _silu_parts(g_att)
        silu_p, dsilu_p = _silu_parts(g_pool)
        y_in = jnp.concatenate([att * silu_a, pool * silu_p], axis=1)
        y = _dot(y_in.astype(BF16), wo_ref[...]) + b_out
        h = ALPHA * x + gate * y
        mu = jnp.mean(h, axis=1, keepdims=True)
        hc = h - mu
        var = jnp.mean(hc * hc, axis=1, keepdims=True)
        rstd = lax.rsqrt(var + LN_EPS)
        yhat = hc * rstd
        diff = yhat * ln_g + ln_b - tgt_ref[...]
        loss_rows = jnp.sum(diff * diff, axis=1, keepdims=True)
        d_out = diff * (1.0 / D)

        d_yhat = d_out * ln_g
        dh = rstd * (d_yhat - jnp.mean(d_yhat, axis=1, keepdims=True)
                     - yhat * jnp.mean(d_yhat * yhat, axis=1, keepdims=True))
        dxa_ref[...] = ALPHA * dh
        dy = dh * gate
        dyb = dy.astype(BF16)
        lane = _lanes((1, D))
        loss_row = jnp.where(lane == 0, (0.5 / D) * jnp.sum(loss_rows, axis=0, keepdims=True), 0.0)
        dvec_ref[5:6, :] += jnp.sum(dh * y, axis=0, keepdims=True)
        dvec_ref[0:1, :] += jnp.sum(dy, axis=0, keepdims=True)
        dvec_ref[1:2, :] += jnp.sum(d_out * yhat, axis=0, keepdims=True)
        dvec_ref[2:3, :] += jnp.sum(d_out, axis=0, keepdims=True)
        dvec_ref[4:5, :] += loss_row

        dwo_ref[...] += _dot(y_in.T.astype(BF16), dyb)
        d_yin = _dot_nt(dyb, wo_ref[...])
        d_a, d_pl = d_yin[:, :D_ATT], d_yin[:, D_ATT:]
        d_att = d_a * silu_a
        d_att_t = d_att.T
        prod_t = (d_att * att).T
        sub = lax.broadcasted_iota(jnp.int32, (HEAD_DIM, tile), 0)
        for h in range(N_HEADS):
            rows = slice(h * HEAD_DIM, (h + 1) * HEAD_DIM)
            delta = jnp.sum(prod_t[rows], axis=0, keepdims=True)
            extra = _place3(sub, 0, [part.astype(F32) for part in _split3(delta)], 0.0)
            do2_ref[h] = jnp.concatenate([d_att_t[rows], extra], axis=0).astype(BF16)
        dga_ref[...] = d_a * att * dsilu_a
        dgp_ref[...] = d_pl * pool * dsilu_p
        d_pool = d_pl * silu_p
        d_mixed = d_pool * pool_scale
        dpvec_ref[0:1, :] += jnp.sum(d_mixed, axis=0, keepdims=True)
        dpvec_ref[1:2, :] += jnp.sum(d_pool * mixed, axis=0, keepdims=True)
        d_pooled = []
        for g in range(len(POOL_WINDOWS)):
            cols = slice(g * GROUP_DIM, (g + 1) * GROUP_DIM)
            dmb = d_mixed[:, cols].astype(BF16)
            dwp_ref[g] += _dot(pooled[g].T.astype(BF16), dmb)
            d_pooled.append(_dot_nt(dmb, wp_ref[g]))
        dpooled_ref[...] = jnp.concatenate(d_pooled, axis=1)

        @pl.when(i == nt - 1)
        def _():
            gwo_ref[...] = dwo_ref[...].astype(BF16)
            dvec_ref[3:4, :] = jnp.concatenate([dpvec_ref[0:1, :], dpvec_ref[1:2, :]], axis=1)

    tok = lambda width: pl.BlockSpec((tile, width), lambda i: (i, 0))
    whole = lambda a: pl.BlockSpec(a.shape, lambda i: (0,) * a.ndim)
    halo = pl.BlockSpec((HALO, D_POOL), lambda i: (jnp.maximum(i * halo_blocks - 1, 0), 0))
    half = jax.ShapeDtypeStruct((seq, D_ATT), F32)
    outs = [jax.ShapeDtypeStruct((seq, D), F32), jax.ShapeDtypeStruct((N_HEADS, LANE, seq), BF16), half, half, half,
            jax.ShapeDtypeStruct(w_out.shape, BF16), jax.ShapeDtypeStruct(w_pool.shape, F32),
            jax.ShapeDtypeStruct(vecs.shape, F32)]
    return pl.pallas_call(
        body, name="middle", grid=(nt,),
        in_specs=[tok(D), tok(D), tok(D_ATT), tok(D_ATT), tok(D_POOL), tok(D_POOL), halo,
                  whole(vecs), whole(pool_vecs), whole(w_out), whole(w_pool)],
        out_specs=[tok(D), pl.BlockSpec((N_HEADS, LANE, tile), lambda i: (0, 0, i)),
                   tok(D_ATT), tok(D_POOL), tok(D_POOL),
                   whole(w_out), whole(w_pool), whole(vecs)],
        out_shape=outs,
        scratch_shapes=[pltpu.VMEM(w_out.shape, F32), pltpu.VMEM(pool_vecs.shape, F32)],
        compiler_params=_params(("arbitrary",)),
    )(x, tgt, att, g_att, g_pool, p, p, vecs, pool_vecs, w_out, w_pool)


def _attention_backward(q2t, kp, vp, do2t, gw_out, vecs, pool, tile):
    seq = kp.shape[1]
    nb = seq // tile
    last = N_HEADS // 2 - 1

    def body(qt_ref, k_ref, v_ref, dot_ref, gwo_hbm, vecs_hbm, pool_hbm,
             dq_ref, dk_ref, dv_ref, dcum_ref, g_out_ref, vecs_sum_ref, pool_sum_ref,
             dq_acc, dk_acc, dv_acc, gwo_ref, vecs_ref, pool_ref,
             r1, s2, r2, v_sib, v_chip, v_recv, p_sib, p_chip, p_recv, send_sems, recv_sems):
        hp = pl.program_id(0)
        start, middle, fold, finish = _reduce_stages(
            [gwo_ref], [g_out_ref], [r1], [s2], [r2],
            [(vecs_ref, vecs_sum_ref, v_sib, v_chip, v_recv), (pool_ref, pool_sum_ref, p_sib, p_chip, p_recv)],
            send_sems, recv_sems)

        @pl.when(hp == 0)
        def _():
            pltpu.sync_copy(gwo_hbm, gwo_ref)
            pltpu.sync_copy(vecs_hbm, vecs_ref)
            pltpu.sync_copy(pool_hbm, pool_ref)
            start()

        pl.when(hp == 1)(middle)
        pl.when(hp == 2)(fold)

        row = lax.broadcasted_iota(jnp.int32, (tile, tile), 0)
        col = lax.broadcasted_iota(jnp.int32, (tile, tile), 1)
        dq_acc[...] = jnp.zeros_like(dq_acc)

        def kv_block(kb, _):
            krows = pl.ds(pl.multiple_of(kb * tile, tile), tile)
            k = [k_ref[hh, krows, :] for hh in range(2)]
            v = [v_ref[hh, krows, :] for hh in range(2)]
            k_t = [k[hh].T for hh in range(2)]

            def q_block(qb, masked):
                qcols = pl.ds(pl.multiple_of(qb * tile, tile), tile)
                for hh in range(2):
                    q_t = qt_ref[hh, :, qcols]
                    do_t = dot_ref[hh, :, qcols]
                    s_t = _dot(k[hh], q_t)
                    if masked:
                        s_t = jnp.where(row <= col, s_t, -1e30)
                    p_t = jnp.exp(s_t)
                    ds_t = (p_t * _dot(v[hh], do_t)).astype(BF16)
                    dv_new = _dot_nt(do_t, p_t.astype(BF16))
                    dk_new = _dot_nt(q_t, ds_t)
                    if masked:
                        dv_acc[hh], dk_acc[hh] = dv_new, dk_new
                    else:
                        dv_acc[hh] += dv_new
                        dk_acc[hh] += dk_new
                    dq_acc[hh, :, qcols] += _dot(k_t[hh], ds_t)

            q_block(kb, True)

            def two_later_blocks(j, _):
                q_block(kb + 1 + 2 * j, False)
                q_block(kb + 2 + 2 * j, False)
                return 0

            later = nb - 1 - kb
            lax.fori_loop(0, later // 2, two_later_blocks, 0)
            pl.when(later % 2 == 1)(lambda: q_block(nb - 1, False))
            for hh in range(2):
                dk = dk_acc[hh]
                dk_ref[hh, :, krows] = dk.astype(BF16)
                dv_ref[hh, :, krows] = dv_acc[hh].astype(BF16)
                dcum_ref[hh, :, krows] = -dk[L_CK:L_CK + 1, :]
            return 0

        lax.fori_loop(0, nb, kv_block, 0)
        for hh in range(2):
            dq = dq_acc[hh]
            dcum_ref[hh] += dq[L_CQ:L_CQ + 1, :]
            dq_ref[hh] = (dq * Q_SCALE).astype(BF16)
        pl.when(hp == last)(finish)

    pair = pl.BlockSpec((2, seq, LANE), lambda hp: (hp, 0, 0))
    pair_t = pl.BlockSpec((2, LANE, seq), lambda hp: (hp, 0, 0))
    whole = lambda shape: pl.BlockSpec(shape, lambda hp: (0,) * len(shape))
    grad = jax.ShapeDtypeStruct((N_HEADS, LANE, seq), BF16)
    return pl.pallas_call(
        body, name="attention_backward", grid=(N_HEADS // 2,),
        in_specs=[pair_t, pair, pair, pair_t, ANY, ANY, ANY],
        out_specs=[pair_t, pair_t, pair_t, pl.BlockSpec((2, 1, seq), lambda hp: (hp, 0, 0)),
                   whole(gw_out.shape[1:]), whole(vecs.shape), whole(pool.shape)],
        out_shape=[grad, grad, grad, jax.ShapeDtypeStruct((N_HEADS, 1, seq), F32),
                   jax.ShapeDtypeStruct(gw_out.shape[1:], F32), jax.ShapeDtypeStruct(vecs.shape, F32),
                   jax.ShapeDtypeStruct(pool.shape, F32)],
        scratch_shapes=[pltpu.VMEM((2, LANE, seq), F32), pltpu.VMEM((2, LANE, tile), F32),
                        pltpu.VMEM((2, LANE, tile), F32), pltpu.VMEM(gw_out.shape, BF16),
                        pltpu.VMEM(vecs.shape, F32), pltpu.VMEM(pool.shape, F32)]
        + _reduce_scratch(gw_out, [vecs, pool]),
        compiler_params=_params(("arbitrary",)),
    )(q2t, kp, vp, do2t, gw_out, vecs, pool)


def _inproj_backward(dqp, dkp, dvp, d_cum, f, d_pooled, d_ga, d_gp, x, dxa, u, mod, w_main, w_f, tile):
    seq = x.shape[0]
    nt = seq // tile
    halo_blocks = tile // HALO

    def body(dq_ref, dk_ref, dv_ref, dcum_ref, f_ref, dpo_ref, dph_ref, dga_ref, dgp_ref, x_ref, dxa_ref, u_ref,
             mod_ref, w_ref, wf_ref,
             dx_ref, dproj_ref, dwf_ref, db_ref, dbf_ref, dmod_ref, carry_ref):
        step = pl.program_id(0)
        i = nt - 1 - step

        @pl.when(step == 0)
        def _():
            carry_ref[...] = jnp.zeros_like(carry_ref)
            dwf_ref[...] = jnp.zeros_like(dwf_ref)
            db_ref[...] = jnp.zeros_like(db_ref)
            dbf_ref[...] = jnp.zeros_like(dbf_ref)
            dmod_ref[...] = jnp.zeros_like(dmod_ref)

        ones = jnp.ones((8, tile), BF16)

        def emit(chunk, val):
            cols = pl.ds(chunk * COL_CHUNK, COL_CHUNK)
            db_ref[0:1, cols] += jnp.sum(val, axis=0, keepdims=True)
            vb = val.astype(BF16)
            dproj_ref[:, pl.ds((chunk - 3) * COL_CHUNK, COL_CHUNK)] = vb
            return _dot(vb, w_ref[cols, :])

        d_u = jnp.zeros((tile, D), F32)
        for chunk, ref in enumerate((dq_ref, dk_ref, dv_ref)):
            cols = pl.ds(chunk * COL_CHUNK, COL_CHUNK)
            val_t = ref[:, 0:HEAD_DIM, :].reshape(COL_CHUNK, tile)
            db_ref[:, cols] += _dot_nt(ones, val_t)
            d_u += _dot_tn(val_t, w_ref[cols, :])

        d_pooled = dpo_ref[...]
        d_halo = dph_ref[...] * jnp.where(i < nt - 1, 1.0, 0.0)
        pos = i * tile + lax.broadcasted_iota(jnp.int32, (tile, 1), 0) + 1
        d_p = []
        for g, window in enumerate(POOL_WINDOWS):
            cols = slice(g * GROUP_DIM, (g + 1) * GROUP_DIM)
            scaled = d_pooled[:, cols] / jnp.minimum(pos, window).astype(F32)
            d_p.append(_window_sum(scaled, d_halo[:, cols] * (1.0 / window), window, True) - d_pooled[:, cols])
        d_u += emit(3, jnp.concatenate(d_p, axis=1))
        d_u += emit(4, dga_ref[...])
        d_u += emit(5, dgp_ref[...])

        row = lax.broadcasted_iota(jnp.int32, (tile, tile), 0)
        col = lax.broadcasted_iota(jnp.int32, (tile, tile), 1)
        later = (row >= col).astype(BF16)
        d_logf = sum(_dot(part, later) for part in _split3(dcum_ref[:, 0, :])) + carry_ref[:, 0:1]
        carry_ref[:, 0:1] = d_logf[:, 0:1]
        d_f = d_logf * _sigmoid(-f_ref[...].T[0:N_HEADS, :])
        d_f = jnp.concatenate([d_f, jnp.zeros((LANE - N_HEADS, tile), F32)], axis=0)
        dbf_ref[...] += sum(_dot_nt(ones, part) for part in _split3(d_f))
        d_fb = d_f.astype(BF16)
        d_u += _dot_tn(d_fb, wf_ref[...])
        dwf_ref[...] += _dot(d_fb, u_ref[...])

        x = x_ref[...]
        dx_ref[...] = dxa_ref[...] + d_u * mod_ref[0:1, :]
        dmod_ref[0:1, :] += jnp.sum(d_u * x, axis=0, keepdims=True)
        dmod_ref[1:2, :] += jnp.sum(d_u, axis=0, keepdims=True)

    rev = lambda step: nt - 1 - step
    tok = lambda width: pl.BlockSpec((tile, width), lambda s: (rev(s), 0))
    head_block = pl.BlockSpec((N_HEADS, LANE, tile), lambda s: (0, 0, rev(s)))
    whole = lambda a: pl.BlockSpec(a.shape, lambda s: (0,) * a.ndim)
    halo = pl.BlockSpec((HALO, D_POOL), lambda s: (jnp.minimum((rev(s) + 1) * halo_blocks, seq // HALO - 1), 0))
    small = lambda width: jax.ShapeDtypeStruct((8, width), F32)
    n_rest = N_MAIN - OFF_P
    return pl.pallas_call(
        body, name="inproj_backward", grid=(nt,),
        in_specs=[head_block, head_block, head_block, pl.BlockSpec((N_HEADS, 1, tile), lambda s: (0, 0, rev(s))),
                  tok(LANE), tok(D_POOL), halo, tok(D_ATT), tok(D_POOL),
                  tok(D), tok(D), tok(D),
                  whole(mod), whole(w_main), whole(w_f)],
        out_specs=[tok(D), tok(n_rest), pl.BlockSpec((LANE, D), lambda s: (0, 0)),
                   pl.BlockSpec((8, N_MAIN), lambda s: (0, 0)), pl.BlockSpec((8, LANE), lambda s: (0, 0)),
                   pl.BlockSpec((8, D), lambda s: (0, 0))],
        out_shape=[jax.ShapeDtypeStruct((seq, D), F32), jax.ShapeDtypeStruct((seq, n_rest), BF16),
                   jax.ShapeDtypeStruct((LANE, D), F32), small(N_MAIN), small(LANE), small(D)],
        scratch_shapes=[pltpu.VMEM((8, LANE), F32)],
        compiler_params=_params(("arbitrary",)),
    )(dqp, dkp, dvp, d_cum, f, d_pooled, d_pooled, d_ga, d_gp, x, dxa, u, mod, w_main, w_f)


def _weight_grads(dq_t, dk_t, dv_t, dw_f, dproj, u, k_tile, small, rows):
    seq = u.shape[0]
    nk = seq // k_tile
    head_rows = N_HEADS * HEAD_DIM
    width = D // 2
    shard = D_IN // N_DEV
    assert nk >= 4

    def body(dq_ref, dk_ref, dv_ref, dwf_ref, dp_ref, u_ref, small_ref, rows_ref,
             left_sum_ref, right_sum_ref, total_ref, rows_all_ref, acc_ref, left_ref, right_ref,
             r1, s2, r2, send_sems, recv_sems,
             own_r1, own_s2, own_r2, sm_sib, sm_chip, sm_recv, rows_land, own_send_sems, own_recv_sems):
        half, k = pl.program_id(0), pl.program_id(1)
        start, middle, fold, finish = _reduce_stages(
            [left_ref], [left_sum_ref], [r1], [s2], [r2], [], send_sems, recv_sems)
        own_stages = _reduce_stages(
            [right_ref], [right_sum_ref], [own_r1], [own_s2], [own_r2],
            [(small_ref, total_ref, sm_sib, sm_chip, sm_recv)], own_send_sems, own_recv_sems,
            rows=(rows_ref, rows_land, rows_all_ref))
        on_right = lambda step: jnp.logical_and(half == 1, k == step)
        pl.when(on_right(0))(start)
        pl.when(on_right(1))(middle)
        pl.when(on_right(nk // 2))(fold)

        @pl.when(k == 0)
        def _():
            acc_ref[...] = jnp.zeros_like(acc_ref)

        tokens = u_ref[...]
        for j, ref in enumerate((dq_ref, dk_ref, dv_ref)):
            acc_ref[pl.ds(j * head_rows, head_rows), :] += _dot(ref[...].reshape(head_rows, k_tile), tokens)
        for j in range(dproj.shape[1] // COL_CHUNK):
            cols = pl.ds(j * COL_CHUNK, COL_CHUNK)
            acc_ref[pl.ds(F_HI + j * COL_CHUNK, COL_CHUNK), :] += _dot_tn(dp_ref[:, cols], tokens)

        def slots_to(out_ref):
            acc_ref[F_LO:F_HI, :] = dwf_ref[0:N_HEADS, :]
            for slot in range(N_DEV):
                out_ref[slot] = acc_ref[slot * shard:(slot + 1) * shard, :].astype(BF16)

        @pl.when(jnp.logical_and(half == 0, k == nk - 1))
        def _():
            slots_to(left_ref)

        @pl.when(on_right(nk - 1))
        def _():
            slots_to(right_ref)
            own_stages[0]()
            finish()
            for stage in own_stages[1:]:
                stage()

    heads = pl.BlockSpec((N_HEADS, HEAD_DIM, k_tile), lambda h, k: (0, 0, k))
    whole = lambda a: pl.BlockSpec(a.shape, lambda h, k: (0,) * a.ndim)
    slots = jax.ShapeDtypeStruct((N_DEV, shard, width), BF16)
    shard_sum = jax.ShapeDtypeStruct((shard, width), F32)
    out_shape = [shard_sum, shard_sum, jax.ShapeDtypeStruct(small.shape, F32),
                 jax.ShapeDtypeStruct((N_DEV,) + rows.shape, F32)]
    return pl.pallas_call(
        body, name="weight_grads", grid=(2, nk),
        in_specs=[heads, heads, heads, pl.BlockSpec((dw_f.shape[0], width), lambda h, k: (0, h)),
                  pl.BlockSpec((k_tile, dproj.shape[1]), lambda h, k: (k, 0)),
                  pl.BlockSpec((k_tile, width), lambda h, k: (k, h)), whole(small), whole(rows)],
        out_specs=[whole(a) for a in out_shape], out_shape=out_shape,
        scratch_shapes=[pltpu.VMEM((D_IN, width), F32), pltpu.VMEM(slots.shape, BF16), pltpu.VMEM(slots.shape, BF16)]
        + _reduce_scratch(slots, []) + _reduce_scratch(slots, [small], rows),
        compiler_params=_params(("arbitrary", "arbitrary")),
    )(dq_t, dk_t, dv_t, dw_f, dproj, u, small, rows)


def _adamw(w, g, m, v):
    m = ADAM_B1 * m + (1.0 - ADAM_B1) * g
    v = ADAM_B2 * v + (1.0 - ADAM_B2) * (g * g)
    m_hat = m / (1.0 - ADAM_B1 ** ADAM_STEP)
    v_hat = v / (1.0 - ADAM_B2 ** ADAM_STEP)
    delta = -ADAM_LR * (m_hat / (jnp.sqrt(v_hat) + ADAM_EPS) + ADAM_WD * w)
    return delta, m, v


SUBLANES = 8


def _adamw_packed(g_parts, w, m, v, name, chunks=4):
    n_parts, (rows, part_cols) = len(g_parts), g_parts[0].shape
    cols = n_parts * part_cols
    per_row, per_part = cols // LANE, part_cols // LANE
    assert part_cols % LANE == 0 and per_row == SUBLANES and w.shape == (rows * per_row, LANE)
    step = -(-rows // (chunks * SUBLANES)) * SUBLANES
    bounds = [(r0, min(r0 + step, rows)) for r0 in range(0, rows, step)]

    def body(*refs):
        g_hbm, (w_hbm, m_hbm, v_hbm, og_hbm, od_hbm, om_hbm, ov_hbm) = refs[:n_parts], refs[n_parts:n_parts + 7]
        g_buf, in_buf, out_buf, in_sems, out_sems = refs[n_parts + 7:]

        def copies_in(c):
            r0, r1 = bounds[c]
            packed = slice(r0 * per_row, r1 * per_row)
            return [pltpu.make_async_copy(g_hbm[p].at[r0:r1], g_buf.at[p, r0:r1], in_sems.at[c, 3 + p])
                    for p in range(n_parts)] + [
                pltpu.make_async_copy(src.at[packed], in_buf.at[i, packed], in_sems.at[c, i])
                for i, src in enumerate((w_hbm, m_hbm, v_hbm))]

        def copies_out(c):
            r0, r1 = bounds[c]
            packed = slice(r0 * per_row, r1 * per_row)
            return [pltpu.make_async_copy(out_buf.at[i, packed], dst.at[packed], out_sems.at[c, i])
                    for i, dst in enumerate((og_hbm, od_hbm, om_hbm, ov_hbm))]

        for c in range(len(bounds)):
            for cp in copies_in(c):
                cp.start()
        for c, (r0, r1) in enumerate(bounds):
            for cp in copies_in(c):
                cp.wait()
            for j in range(per_row):
                lanes = pl.ds(r0 * per_row + j, r1 - r0, stride=per_row)
                g_part = g_buf[j // per_part, r0:r1, (j % per_part) * LANE:(j % per_part + 1) * LANE]
                results = _adamw(in_buf[0, lanes, :], g_part, in_buf[1, lanes, :], in_buf[2, lanes, :])
                for i, val in enumerate((g_part,) + results):
                    out_buf[i, lanes, :] = val
            for cp in copies_out(c):
                cp.start()
        for c in range(len(bounds)):
            for cp in copies_out(c):
                cp.wait()

    shape = jax.ShapeDtypeStruct(w.shape, F32)
    return pl.pallas_call(
        body, name=name,
        in_specs=[ANY] * (n_parts + 3), out_specs=[ANY] * 4, out_shape=[shape] * 4,
        scratch_shapes=[pltpu.VMEM((n_parts, rows, part_cols), F32), pltpu.VMEM((3,) + w.shape, F32),
                        pltpu.VMEM((4,) + w.shape, F32),
                        pltpu.SemaphoreType.DMA((len(bounds), 3 + n_parts)), pltpu.SemaphoreType.DMA((len(bounds), 4))],
        compiler_params=_params(),
    )(*g_parts, w, m, v)


def _ada_adamw(sc_all, d_ada, w, m, v, chunks=4):
    rows, cols = w.shape
    step, sub = rows // chunks, 32
    assert rows % chunks == 0 and step % LANE == 0 and step % sub == 0

    def body(sc_ref, d_ref, w_hbm, m_hbm, v_hbm, og_hbm, od_hbm, om_hbm, ov_hbm, in_buf, out_buf, in_sems, out_sems):
        def copies_in(c):
            part = slice(c * step, (c + 1) * step)
            return [pltpu.make_async_copy(src.at[part], in_buf.at[i, part], in_sems.at[c, i])
                    for i, src in enumerate((w_hbm, m_hbm, v_hbm))]

        def copies_out(c):
            part = slice(c * step, (c + 1) * step)
            return [pltpu.make_async_copy(out_buf.at[i, part], dst.at[part], out_sems.at[c, i])
                    for i, dst in enumerate((og_hbm, od_hbm, om_hbm, ov_hbm))]

        for c in range(chunks):
            for cp in copies_in(c):
                cp.start()
        for c in range(chunks):
            sc_t = sc_ref[:, c * step:(c + 1) * step].T
            for cp in copies_in(c):
                cp.wait()
            for r0 in range(0, step, sub):
                part = slice(c * step + r0, c * step + r0 + sub)
                g = sc_t[r0:r0 + sub, 0:1] * d_ref[0:1, :]
                for b in range(1, N_DEV):
                    g = g + sc_t[r0:r0 + sub, b:b + 1] * d_ref[b:b + 1, :]
                results = _adamw(in_buf[0, part, :], g, in_buf[1, part, :], in_buf[2, part, :])
                for i, val in enumerate((g,) + results):
                    out_buf[i, part, :] = val
            for cp in copies_out(c):
                cp.start()
        for c in range(chunks):
            for cp in copies_out(c):
                cp.wait()

    in_vmem = pl.BlockSpec(memory_space=pltpu.VMEM)
    shape = jax.ShapeDtypeStruct(w.shape, F32)
    return pl.pallas_call(
        body, name="ada_adamw",
        in_specs=[in_vmem, in_vmem, ANY, ANY, ANY], out_specs=[ANY] * 4, out_shape=[shape] * 4,
        scratch_shapes=[pltpu.VMEM((3,) + w.shape, F32), pltpu.VMEM((4,) + w.shape, F32),
                        pltpu.SemaphoreType.DMA((chunks, 3)), pltpu.SemaphoreType.DMA((chunks, 4))],
        compiler_params=_params(),
    )(sc_all, d_ada, w, m, v)


F_LO, F_HI = 3 * D_ATT, 3 * D_ATT + N_HEADS


def _split_forget(a, axis):
    idx = lambda lo, hi: tuple(slice(lo, hi) if d == axis else slice(None) for d in range(a.ndim))
    pad = [(0, LANE - N_HEADS) if d == axis else (0, 0) for d in range(a.ndim)]
    return jnp.concatenate([a[idx(0, F_LO)], a[idx(F_HI, D_IN)]], axis=axis), jnp.pad(a[idx(F_LO, F_HI)], pad)


def _join_forget(main, f, axis):
    idx = lambda lo, hi: tuple(slice(lo, hi) if d == axis else slice(None) for d in range(main.ndim))
    return jnp.concatenate([main[idx(0, F_LO)], f[idx(0, N_HEADS)], main[idx(F_LO, N_MAIN)]], axis=axis)


def _adamw_small(grad_rows, row_params, whole_params, summed_params, scalar_at):
    n_row, n_whole, n_sum = len(row_params), len(whole_params), len(summed_params)
    n = n_row + n_whole + n_sum

    def body(g_ref, *refs):
        n_in = 3 * n_row + 4 * (n_whole + n_sum)
        ins, outs = list(refs[:n_in]), refs[n_in:]
        for i in range(n):
            if i < n_row:
                row, lo, hi = row_params[i][:3]
                g = g_ref[row:row + 1, lo:hi]
            elif i < n_row + n_whole:
                g = ins.pop(0)[...]
            else:
                parts = ins.pop(0)
                g = parts[0]
                for k in range(1, N_DEV):
                    g = g + parts[k]
            w, m, v = (ins.pop(0)[...] for _ in range(3))
            outs[4 * i][...] = g
            outs[4 * i + 1][...], outs[4 * i + 2][...], outs[4 * i + 3][...] = _adamw(w, g, m, v)
        row, lane = scalar_at
        outs[4 * n][...] = g_ref[row:row + 1, lane:lane + 1]

    shapes = [p[3] for p in row_params] + [p[1] for p in whole_params] + [p[1] for p in summed_params]
    operands = [a for p in row_params for a in p[3:]] + [a for p in whole_params + summed_params for a in p]
    flat = pl.pallas_call(
        body, name="adamw_small",
        out_shape=[jax.ShapeDtypeStruct(w.shape, F32) for w in shapes for _ in range(4)]
        + [jax.ShapeDtypeStruct((1, 1), F32)],
        compiler_params=_params(),
    )(grad_rows, *operands)
    return [flat[4 * i:4 * i + 4] for i in range(n)], flat[4 * n].reshape(())


def kernel(x, c, w_ada, b_ada, w_in, b_in, w_pool_mix, b_pool_mix, pool_scale, w_out, b_out, ln_g, ln_b, loss_target, m_w_ada, m_b_ada, m_w_in, m_b_in, m_w_pool_mix, m_b_pool_mix, m_pool_scale, m_w_out, m_b_out, m_ln_g, m_ln_b, v_w_ada, v_b_ada, v_w_in, v_b_in, v_w_pool_mix, v_b_pool_mix, v_pool_scale, v_w_out, v_b_out, v_ln_g, v_ln_b):
    seq = x.shape[1]
    tile = min(256, seq)
    attn_tile = min(512, max(128, seq // 4))
    me = _dev_index(*_mesh_pos())
    x2, tgt = x[0], loss_target[0]

    rows_of = lambda a: jnp.swapaxes(a, 1, 2)[0]
    w_main, w_f, sc_all, ada_mine = _gather_and_ada(c, rows_of(w_in).astype(BF16), w_ada[0])
    ada = ada_mine.reshape(1, D_ADA) + b_ada
    shift, scale, gate = ada[:, 0:D], ada[:, D:2 * D], ada[:, 2 * D:]
    mod = jnp.concatenate([1.0 + scale, shift, jnp.zeros((6, D), F32)], axis=0)
    b_main, b_f = _split_forget(b_in, 1)

    qp, kp, vp, f, p, g_att, g_pool, u = _inproj_forward(x2, mod, w_main, w_f, b_main, b_f, tile)
    att, q2t, w_out_g = _attention_forward(qp, kp, vp, w_out[0].astype(BF16), attn_tile)

    vecs = jnp.concatenate([gate, b_out, ln_g, ln_b, jnp.zeros((4, D), F32)], axis=0)
    pool_vecs = jnp.concatenate([b_pool_mix.reshape(1, D_POOL), pool_scale, jnp.zeros((6, D_POOL), F32)], axis=0)
    dxa, do2, d_ga, d_gp, d_pooled, gw_out, dw_pool, dvec = _middle(
        x2, tgt, att, g_att, g_pool, p, vecs, pool_vecs, w_out_g.reshape(D, D), w_pool_mix[0].astype(BF16), tile)

    pool_rows = w_pool_mix.shape[1] * GROUP_DIM
    dqp, dkp, dvp, d_cum, g_out, dvec_sum, dw_pool_sum = _attention_backward(
        q2t, kp, vp, do2, gw_out.reshape(N_DEV, D // N_DEV, D), dvec, dw_pool.reshape(pool_rows, GROUP_DIM), attn_tile)
    dx, dproj, dw_f, db_main, db_f, dmod = _inproj_backward(
        dqp, dkp, dvp, d_cum, f, d_pooled, d_ga, d_gp, x2, dxa, u, mod, w_main, w_f, tile)
    d_ada = jnp.concatenate([dmod[1:2], dmod[0:1], dvec[5:6]], axis=1)
    g_in_left, g_in_right, g_b_in, d_ada_all = _weight_grads(
        dqp, dkp, dvp, dw_f, dproj, u, min(512, seq), _join_forget(db_main[0:1], db_f[0:1], 1), d_ada)

    packed = lambda a: jnp.transpose(a.reshape(SUBLANES, LANE, -1), (2, 0, 1)).reshape(-1, LANE)
    outs_in = _adamw_packed((g_in_left, g_in_right), packed(w_in), packed(m_w_in), packed(v_w_in), "adamw_w_in")
    g_w_in, d_w_in, nm_w_in, nv_w_in = (
        jnp.transpose(a.reshape(-1, SUBLANES, LANE), (1, 2, 0)).reshape(D, -1) for a in outs_in)
    flat_pool = lambda a: a.reshape(1, D_POOL)
    pool_2d = lambda a: a.reshape(pool_rows, GROUP_DIM)
    rows, loss = _adamw_small(
        dvec_sum,
        [(0, 0, D, b_out, m_b_out, v_b_out), (1, 0, D, ln_g, m_ln_g, v_ln_g), (2, 0, D, ln_b, m_ln_b, v_ln_b),
         (3, 0, D_POOL, flat_pool(b_pool_mix), flat_pool(m_b_pool_mix), flat_pool(v_b_pool_mix)),
         (3, D_POOL, 2 * D_POOL, pool_scale, m_pool_scale, v_pool_scale)],
        [(g_out, w_out[0], m_w_out[0], v_w_out[0]),
         (dw_pool_sum, pool_2d(w_pool_mix), pool_2d(m_w_pool_mix), pool_2d(v_w_pool_mix)),
         (g_b_in, b_in, m_b_in, v_b_in)],
        [(d_ada_all, b_ada, m_b_ada, v_b_ada)],
        scalar_at=(4, 0))
    small = {"b_out": rows[0], "ln_g": rows[1], "ln_b": rows[2],
             "b_pool": [a.reshape(b_pool_mix.shape) for a in rows[3]], "pool_scale": rows[4],
             "w_pool": [a.reshape(w_pool_mix.shape) for a in rows[6]], "b_in": rows[7]}
    g_s, d_s, nm_s, nv_s = ({k: r[j] for k, r in small.items()} for j in range(4))
    g_w_out, d_w_out, nm_w_out, nv_w_out = rows[5]
    g_b_ada, d_b_ada, nm_b_ada, nv_b_ada = rows[8]

    d_ada_local = lax.dynamic_slice_in_dim(d_ada_all.reshape(N_DEV, D_ADA), me * (D_ADA // N_DEV), D_ADA // N_DEV, axis=1)
    g_w_ada, d_w_ada, nm_w_ada, nv_w_ada = _ada_adamw(sc_all, d_ada_local, w_ada[0], m_w_ada[0], v_w_ada[0])

    def ordered(w_ada_, b_ada_, w_in_, w_out_, s):
        return (w_ada_[None], b_ada_, w_in_[None], s["b_in"], s["w_pool"], s["b_pool"], s["pool_scale"],
                w_out_[None], s["b_out"], s["ln_g"], s["ln_b"])

    return (loss, dx[None],
            *ordered(g_w_ada, g_b_ada, g_w_in, g_w_out, g_s),
            *ordered(d_w_ada, d_b_ada, d_w_in, d_w_out, d_s),
            *ordered(nm_w_ada, nm_b_ada, nm_w_in, nm_w_out, nm_s),
            *ordered(nv_w_ada, nv_b_ada, nv_w_in, nv_w_out, nv_s))
```

```python
import jax
import jax.numpy as jnp
from jax import lax
from jax.experimental import pallas as pl
from jax.experimental.pallas import tpu as pltpu

F32 = jnp.float32
BF16 = jnp.bfloat16

N_DEV = 8
D = 1024
N_HEADS = 8
HEAD_DIM = 64
D_ATT = 512
D_POOL = 512
POOL_WINDOWS = (2, 4, 8, 16)
GROUP_DIM = 128
HALO = 16
LANE = 128
BF16_TILE_ROWS = 16
D_IN = 3080
D_ADA = 3072
N_MAIN = 3072
OFF_P = 1536
COL_CHUNK = 512
Q_SCALE = 0.125
LN_EPS = 1e-5
ALPHA = 2.0 ** 0.25
L_CQ, L_CK, L_LSE = 64, 67, 70

ADAM_LR, ADAM_B1, ADAM_B2, ADAM_EPS, ADAM_WD, ADAM_STEP = 0.001, 0.9, 0.999, 1e-08, 0.01, 10
VMEM_LIMIT = 56 * 1024 * 1024

MESH = pl.DeviceIdType.MESH
ANY = pl.BlockSpec(memory_space=pl.ANY)


def _params(sem=None, vmem=VMEM_LIMIT):
    return pltpu.CompilerParams(dimension_semantics=sem, vmem_limit_bytes=vmem)


def _split3(a):
    hi = a.astype(BF16)
    r = a - hi.astype(F32)
    mid = r.astype(BF16)
    lo = (r - mid.astype(F32)).astype(BF16)
    return hi, mid, lo


def _dot(a, b):
    return jnp.dot(a, b, preferred_element_type=F32)


def _dot_nt(a, b):
    return lax.dot_general(a, b, (((1,), (1,)), ((), ())), preferred_element_type=F32)


def _dot_tn(a, b):
    return lax.dot_general(a, b, (((0,), (0,)), ((), ())), preferred_element_type=F32)


def _dot3(m01, a):
    hi, mid, lo = _split3(a)
    return _dot(m01, hi) + _dot(m01, mid) + _dot(m01, lo)


def _sigmoid(z):
    return 1.0 / (1.0 + jnp.exp(-z))


def _lanes(shape):
    return lax.broadcasted_iota(jnp.int32, shape, len(shape) - 1)


def _place3(lane, base, parts, other):
    out = other
    for j in range(3):
        out = jnp.where(lane == base + j, parts[j], out)
    return out


def _mesh_pos():
    return lax.axis_index("x"), lax.axis_index("y"), lax.axis_index("c")


def _dev_index(px, py, pc):
    return 4 * px + 2 * py + pc


N_GATHER_SEMS = 11


def _gather_stages(src_ref, out_ref, send_sems, recv_sems, local_sem):
    x, y, c = _mesh_pos()
    me, sibling = (x, y, c), (x, y, 1 - c)
    nbr_x, nbr_y, diag = (1 - x, y), (x, 1 - y), (1 - x, 1 - y)
    half = out_ref.shape[-1] // 2
    left, right = pl.ds(0, half), pl.ds(half, half)

    def copy(k, block, to, cols=None, src=None):
        slot = out_ref.at[_dev_index(*block)]
        if cols is not None:
            slot = slot.at[:, cols]
            src = src if src is None else src.at[:, cols]
        return pltpu.make_async_remote_copy(
            src_ref=slot if src is None else src, dst_ref=slot, send_sem=send_sems.at[k], recv_sem=recv_sems.at[k],
            device_id=to, device_id_type=MESH)

    mine = pltpu.make_async_copy(src_ref, out_ref.at[_dev_index(*me)], local_sem)
    first = [copy(0, me, sibling, src=src_ref),
             copy(1, me, (*nbr_x, c), cols=left, src=src_ref), copy(2, me, (*nbr_y, c), cols=right, src=src_ref),
             copy(9, me, (*nbr_x, c), cols=right, src=src_ref), copy(10, me, (*nbr_y, c), cols=left, src=src_ref)]
    relay = [(1, nbr_x, left, nbr_x), (2, nbr_y, right, nbr_y), (3, diag, left, nbr_y), (4, diag, right, nbr_x)]
    other_half = [(9, nbr_x, right, nbr_x), (10, nbr_y, left, nbr_y)]
    onward = [copy(3, (*nbr_x, c), (*nbr_y, c), cols=left), copy(4, (*nbr_y, c), (*nbr_x, c), cols=right)]
    passed = [copy(4 + k, (*block, c), sibling, cols=None if k < 3 else cols) for k, block, cols, _ in relay]

    def start():
        mine.start()
        for cp in first:
            cp.start()

    def arrived(item):
        k, block, cols, frm = item
        copy(k, (*block, c), (*frm, c), cols=cols).wait_recv()

    def relay_near():
        for j in (0, 1):
            arrived(relay[j])
            onward[j].start()
        for j in (0, 1):
            arrived(other_half[j])
            passed[j].start()

    def relay_far():
        for j in (2, 3):
            arrived(relay[j])
            passed[j].start()

    def from_sibling(items):
        for k, block, cols, _ in items:
            copy(4 + k, (*block, 1 - c), me, cols=None if k < 3 else cols).wait_recv()

    def finish_near():
        copy(0, sibling, me).wait_recv()
        from_sibling(relay[:2])
        mine.wait()

    def finish_far():
        from_sibling(relay[2:])
        for cp in first + onward + passed:
            cp.wait_send()

    return start, relay_near, relay_far, finish_near, finish_far


N_REDUCE_SEMS = 10
N_SMALL_SEMS = 4
N_ROWS_SEMS = 7


def _reduce_stages(ins, gs, r1, s2, r2, smalls, send_sems, recv_sems, rows=None, own=None):
    n = len(ins)
    x, y, c = _mesh_pos()
    me = _dev_index(x, y, c)
    sibling = (x, y, 1 - c)
    chips = [(x, y), (1 - x, y), (x, 1 - y), (1 - x, 1 - y)]
    peers = []
    for p in range(1, N_DEV):
        px, py, pc = (p >> 2) & 1, (p >> 1) & 1, p & 1
        peers.append((1 - x if px else x, 1 - y if py else y, 1 - c if pc else c))
    base_small = N_REDUCE_SEMS * n

    def remote(src, dst, k, to):
        return pltpu.make_async_remote_copy(src_ref=src, dst_ref=dst, send_sem=send_sems.at[k],
                                            recv_sem=recv_sems.at[k], device_id=to, device_id_type=MESH)

    def level1(a, q):
        return remote(ins[a].at[_dev_index(*chips[q], 1 - c)], r1[a].at[q], N_REDUCE_SEMS * a + q, sibling)

    def level2(a, k):
        half = ins[a].shape[-1] // 2
        left, right = pl.ds(0, half), pl.ds(half, half)
        nbr_x, nbr_y = (*chips[1], c), (*chips[2], c)
        src_slot, dst_slot, cols, to = [(0, 0, left, nbr_x), (1, 1, right, nbr_y), (2, 2, left, nbr_x),
                                        (2, 2, right, nbr_y), (0, 0, right, nbr_x), (1, 1, left, nbr_y)][k]
        return remote(s2[a].at[src_slot, :, cols], r2[a].at[dst_slot, :, cols], N_REDUCE_SEMS * a + 4 + k, to)

    to_sibling = [remote(sm[0], sm[2], base_small + 4 * i, sibling) for i, sm in enumerate(smalls)]
    to_chips = [[remote(sm[3], sm[4].at[j], base_small + 4 * i + 1 + j, (*chips[j + 1], c)) for j in range(3)]
                for i, sm in enumerate(smalls)]
    if rows is not None:
        rows_ref, land_ref, all_ref = rows
        base_rows = base_small + 4 * len(smalls)
        row_sends = [remote(rows_ref, land_ref.at[me], base_rows + k, to) for k, to in enumerate(peers)]

    order = (3, 1, 2, 0)

    def mine(a, q):
        buf, sems = own[a]
        return pltpu.make_async_copy(ins[a].at[_dev_index(*chips[q], c)], buf.at[q], sems.at[q])

    def start():
        for a in range(n):
            for q in order:
                level1(a, q).start()
            if own is not None:
                for q in order:
                    mine(a, q).start()
        for cp in to_sibling:
            cp.start()
        if rows is not None:
            for cp in row_sends:
                cp.start()
            land_ref[me] = rows_ref[...]

    def middle():
        for a in range(n):
            for q in order:
                level1(a, q).wait_recv()
                if own is None:
                    kept = ins[a][_dev_index(*chips[q], c)]
                else:
                    mine(a, q).wait()
                    kept = own[a][0][q]
                pair = kept.astype(F32) + r1[a][q].astype(F32)
                if q == 0:
                    gs[a][...] = pair
                else:
                    s2[a][q - 1] = pair.astype(BF16)
                    for k in ((0,), (1,), (2, 3))[q - 1]:
                        level2(a, k).start()
        for i, (small_ref, _, sm_sib, sm_chip, _) in enumerate(smalls):
            to_sibling[i].wait_recv()
            sm_chip[...] = small_ref[...] + sm_sib[...]
            for cp in to_chips[i]:
                cp.start()

    def fold():
        for a in range(n):
            half = ins[a].shape[-1] // 2
            level2(a, 3).wait_recv()
            s2[a][0, :, half:] = (s2[a][0, :, half:].astype(F32) + r2[a][2, :, half:].astype(F32)).astype(BF16)
            level2(a, 4).start()
            level2(a, 2).wait_recv()
            s2[a][1, :, :half] = (s2[a][1, :, :half].astype(F32) + r2[a][2, :, :half].astype(F32)).astype(BF16)
            level2(a, 5).start()

    def finish():
        for a in range(n):
            for k in (0, 1, 4, 5):
                level2(a, k).wait_recv()
            gs[a][...] = gs[a][...] + r2[a][0].astype(F32) + r2[a][1].astype(F32)
            for q in range(4):
                level1(a, q).wait_send()
            for k in range(6):
                level2(a, k).wait_send()
        for i, (_, total_ref, _, sm_chip, sm_recv) in enumerate(smalls):
            for cp in to_chips[i]:
                cp.wait_recv()
            total = None
            for ax in range(2):
                for ay in range(2):
                    dx, dy = x != ax, y != ay
                    term = jnp.where(dx, jnp.where(dy, sm_recv[2], sm_recv[0]), jnp.where(dy, sm_recv[1], sm_chip[...]))
                    total = term if total is None else total + term
            total_ref[...] = total
            for cp in [to_sibling[i]] + to_chips[i]:
                cp.wait_send()
        if rows is not None:
            for k, frm in enumerate(peers):
                remote(rows_ref, land_ref.at[_dev_index(*frm)], base_rows + k, frm).wait_recv()
            all_ref[...] = land_ref[...]
            for cp in row_sends:
                cp.wait_send()

    return start, middle, fold, finish


def _reduce_scratch(shard, smalls, rows=None):
    out = [pltpu.VMEM((lead,) + shard.shape[1:], BF16) for lead in (4, 3, 3)]
    for small in smalls:
        out += [pltpu.VMEM(small.shape, F32), pltpu.VMEM(small.shape, F32), pltpu.VMEM((3,) + small.shape, F32)]
    n_sems = N_REDUCE_SEMS + N_SMALL_SEMS * len(smalls)
    if rows is not None:
        out.append(pltpu.VMEM((N_DEV,) + rows.shape, F32))
        n_sems += N_ROWS_SEMS
    return out + [pltpu.SemaphoreType.DMA((n_sems,))] * 2


def _dot3_rhs(a, b):
    a0, a1, a2 = _split3(a)
    b0, b1, b2 = _split3(b)
    return (_dot(a0, b0) + (_dot(a0, b1) + _dot(a1, b0))
            + (_dot(a0, b2) + _dot(a1, b1) + _dot(a2, b0)))


def _gather_and_ada(c, w_in_rows, w_ada):
    cols = w_ada.shape[1]
    shard = w_in_rows.shape[0]

    def body(c_ref, w_ref, wa_ref, w_main_hbm, w_f_ref, sc_ref, ada_ref,
             w_all_ref, w_f32, wm_buf, c_land, part, ada_land, send_sems, recv_sems, local_sem, x_send, x_recv, out_sems):
        x, y, cc = _mesh_pos()
        me = _dev_index(x, y, cc)
        peers = []
        for p in range(1, N_DEV):
            px, py, pc = (p >> 2) & 1, (p >> 1) & 1, p & 1
            peers.append((1 - x if px else x, 1 - y if py else y, 1 - cc if pc else cc))

        def remote(src, dst, k, to):
            return pltpu.make_async_remote_copy(src_ref=src, dst_ref=dst, send_sem=x_send.at[k], recv_sem=x_recv.at[k],
                                                device_id=to, device_id_type=MESH)

        c_sends = [remote(c_ref, c_land.at[me], k, to) for k, to in enumerate(peers)]
        for cp in c_sends:
            cp.start()
        start, relay_near, relay_far, finish_near, finish_far = _gather_stages(
            w_ref, w_all_ref, send_sems, recv_sems, local_sem.at[0])
        start()
        c_land[me] = c_ref[...]
        for k, frm in enumerate(peers):
            remote(c_ref, c_land.at[_dev_index(*frm)], k, frm).wait_recv()
        c_all = jnp.concatenate([c_land[b] for b in range(N_DEV)], axis=0)
        sc = c_all * _sigmoid(c_all)
        sc_ref[...] = sc
        rows = _dot3_rhs(sc, wa_ref[...])
        for b in range(N_DEV):
            part[b] = rows[b:b + 1, :]
        a_sends = [remote(part.at[_dev_index(*to)], ada_land.at[me], 7 + k, to) for k, to in enumerate(peers)]
        for cp in a_sends:
            cp.start()
        ada_land[me] = part[me]

        relay_near()
        finish_near()

        far_chip = 2 * (1 - x) + (1 - y)

        def stage(slots):
            for slot in slots:
                w_f32[slot * shard:(slot + 1) * shard, :] = w_all_ref[slot].astype(F32)

        def far_rows(k):
            main_row = lambda r: r if r < F_LO else r - N_HEADS
            first, last = 2 * shard * k, 2 * shard * (k + 1) - 1
            first = F_HI if F_LO <= first < F_HI else first
            last = F_LO - 1 if F_LO <= last < F_HI else last
            return (main_row(first) // BF16_TILE_ROWS * BF16_TILE_ROWS,
                    -(-(main_row(last) + 1) // BF16_TILE_ROWS) * BF16_TILE_ROWS)

        def near_rows(k):
            lo, hi = far_rows(k)
            return [(i, a, b) for i, (a, b) in enumerate(((0, lo), (hi, N_MAIN))) if a < b]

        def forget_is_far(k):
            return 2 * shard * k < F_HI and 2 * shard * (k + 1) > F_LO

        def main_copy(i, lo, hi):
            return pltpu.make_async_copy(wm_buf.at[lo:hi], w_main_hbm.at[lo:hi], out_sems.at[i])

        def write_main(i, lo, hi):
            if lo < min(hi, F_LO):
                wm_buf[lo:min(hi, F_LO), :] = w_f32[lo:min(hi, F_LO), :].astype(BF16)
            if max(lo, F_LO) < hi:
                wm_buf[max(lo, F_LO):hi, :] = w_f32[max(lo, F_LO) + N_HEADS:hi + N_HEADS, :].astype(BF16)
            main_copy(i, lo, hi).start()

        def write_forget():
            w_f_ref[...] = jnp.concatenate(
                [w_f32[F_LO:F_HI, :], jnp.zeros((LANE - N_HEADS, D), F32)], axis=0).astype(BF16)

        for k in range(N_DEV // 2):
            @pl.when(far_chip == k)
            def _(k=k):
                stage([slot for slot in range(N_DEV) if slot // 2 != k])
                for i, lo, hi in near_rows(k):
                    write_main(i, lo, hi)
                if not forget_is_far(k):
                    write_forget()

        relay_far()
        for k, frm in enumerate(peers):
            remote(part.at[0], ada_land.at[_dev_index(*frm)], 7 + k, frm).wait_recv()
        ada_ref[...] = ada_land[...]
        finish_far()
        for cp in c_sends + a_sends:
            cp.wait_send()

        for k in range(N_DEV // 2):
            @pl.when(far_chip == k)
            def _(k=k):
                stage([2 * k, 2 * k + 1])
                write_main(2, *far_rows(k))
                if forget_is_far(k):
                    write_forget()
                for i, lo, hi in near_rows(k):
                    main_copy(i, lo, hi).wait()
                main_copy(2, *far_rows(k)).wait()

    vmem = pl.BlockSpec(memory_space=pltpu.VMEM)
    return pl.pallas_call(
        body, name="gather_weights",
        in_specs=[vmem, ANY, vmem], out_specs=[ANY, vmem, vmem, vmem],
        out_shape=[jax.ShapeDtypeStruct((N_MAIN, D), BF16), jax.ShapeDtypeStruct((LANE, D), BF16),
                   jax.ShapeDtypeStruct((N_DEV, D), F32), jax.ShapeDtypeStruct((N_DEV, 1, cols), F32)],
        scratch_shapes=[pltpu.VMEM((N_DEV,) + w_in_rows.shape, BF16), pltpu.VMEM((D_IN, D), F32), pltpu.VMEM((N_MAIN, D), BF16),
                        pltpu.VMEM((N_DEV, 1, D), F32), pltpu.VMEM((N_DEV, 1, cols), F32), pltpu.VMEM((N_DEV, 1, cols), F32),
                        pltpu.SemaphoreType.DMA((N_GATHER_SEMS,)), pltpu.SemaphoreType.DMA((N_GATHER_SEMS,)),
                        pltpu.SemaphoreType.DMA((1,)),
                        pltpu.SemaphoreType.DMA((14,)), pltpu.SemaphoreType.DMA((14,)), pltpu.SemaphoreType.DMA((3,))],
        compiler_params=_params(),
    )(c, w_in_rows, w_ada)


def _inproj_forward(x, mod, w_main, w_f, b_main, b_f, tile):
    seq = x.shape[0]
    nt = seq // tile

    def body(x_ref, mod_ref, w_ref, wf_ref, b_ref, bf_ref,
             qp_ref, kp_ref, vp_ref, f_ref, p_ref, ga_ref, gp_ref, u_ref, carry_ref):
        i = pl.program_id(0)

        @pl.when(i == 0)
        def _():
            carry_ref[...] = jnp.zeros_like(carry_ref)

        u = x_ref[...] * mod_ref[0:1, :] + mod_ref[1:2, :]
        ub = u.astype(BF16)
        u_ref[...] = ub

        f = _dot_nt(ub, wf_ref[...]) + bf_ref[...]
        f_ref[...] = f
        lane = _lanes((tile, LANE))
        log_f = jnp.where(lane < N_HEADS, jnp.minimum(f, 0.0) - jnp.log(1.0 + jnp.exp(-jnp.abs(f))), 0.0)
        row = lax.broadcasted_iota(jnp.int32, (tile, tile), 0)
        col = lax.broadcasted_iota(jnp.int32, (tile, tile), 1)
        tri = (row >= col).astype(BF16)
        cum = _dot3(tri, log_f) + carry_ref[0:1, :]
        carry_ref[0:1, :] = cum[tile - 1:tile, :]
        cq = [part.astype(F32) for part in _split3(cum)]
        ck = [part.astype(F32) for part in _split3(-cum)]

        def proj(chunk):
            cols = pl.ds(chunk * COL_CHUNK, COL_CHUNK)
            return _dot_nt(ub, w_ref[cols, :]) + b_ref[:, cols]

        def head_tiles(r):
            for pair in range(N_HEADS // 2):
                both = r[:, pair * LANE:(pair + 1) * LANE]
                yield 2 * pair, both
                yield 2 * pair + 1, pltpu.roll(both, HEAD_DIM, 1)

        for h, val in head_tiles(proj(0)):
            extra = jnp.where((lane >= L_CK) & (lane < L_CK + 3), 1.0, 0.0)
            extra = _place3(lane, L_CQ, [part[:, h:h + 1] for part in cq], extra)
            qp_ref[h] = jnp.where(lane < HEAD_DIM, val * Q_SCALE, extra).astype(BF16)
        for h, val in head_tiles(proj(1)):
            ones = ((lane >= L_CQ) & (lane < L_CQ + 3)) | ((lane >= L_LSE) & (lane < L_LSE + 3))
            extra = _place3(lane, L_CK, [part[:, h:h + 1] for part in ck], jnp.where(ones, 1.0, 0.0))
            kp_ref[h] = jnp.where(lane < HEAD_DIM, val, extra).astype(BF16)
        for h, val in head_tiles(proj(2)):
            extra = jnp.where((lane >= HEAD_DIM) & (lane < HEAD_DIM + 3), -1.0, 0.0)
            vp_ref[h] = jnp.where(lane < HEAD_DIM, val, extra).astype(BF16)
        p_ref[...] = proj(3)
        ga_ref[...] = proj(4)
        gp_ref[...] = proj(5)

    head_block = pl.BlockSpec((N_HEADS, tile, LANE), lambda i: (0, i, 0))
    tok = lambda width: pl.BlockSpec((tile, width), lambda i: (i, 0))
    whole = lambda a: pl.BlockSpec(a.shape, lambda i: (0,) * a.ndim)
    padded = jax.ShapeDtypeStruct((N_HEADS, seq, LANE), BF16)
    half = jax.ShapeDtypeStruct((seq, D_ATT), F32)
    return pl.pallas_call(
        body, name="inproj_forward", grid=(nt,),
        in_specs=[tok(D), whole(mod), whole(w_main), whole(w_f), whole(b_main), whole(b_f)],
        out_specs=[head_block, head_block, head_block, tok(LANE), tok(D_POOL), tok(D_ATT), tok(D_POOL),
                   tok(D)],
        out_shape=[padded, padded, padded, jax.ShapeDtypeStruct((seq, LANE), F32), half, half, half,
                   jax.ShapeDtypeStruct((seq, D), BF16)],
        scratch_shapes=[pltpu.VMEM((8, LANE), F32)],
        compiler_params=_params(("arbitrary",)),
    )(x, mod, w_main, w_f, b_main, b_f)


def _attention_forward(qp, kp, vp, w_out, tile):
    seq = qp.shape[1]
    nb = seq // tile
    steps = (N_HEADS // 2) * nb

    def body(q_ref, k_ref, v_ref, wo_ref, att_ref, q2t_ref, wo_all_ref, s_a, s_b, m_ref, acc_ref,
             send_sems, recv_sems, local_sem):
        step = pl.program_id(0) * nb + pl.program_id(1)
        start, relay_near, relay_far, finish_near, finish_far = _gather_stages(
            wo_ref, wo_all_ref, send_sems, recv_sems, local_sem.at[0])
        pl.when(step == 0)(start)
        pl.when(step == steps // 4)(relay_near)
        pl.when(step == (3 * steps) // 4)(relay_far)

        i = pl.program_id(1)
        sub = lax.broadcasted_iota(jnp.int32, (LANE, tile), 0)
        row = lax.broadcasted_iota(jnp.int32, (tile, tile), 0)
        col = lax.broadcasted_iota(jnp.int32, (tile, tile), 1)
        q = [q_ref[0], q_ref[1]]

        def scores(buf, kb):
            rows = pl.ds(pl.multiple_of(kb * tile, tile), tile)
            for hh in range(2):
                buf[hh] = _dot_nt(k_ref[hh, rows, :], q[hh])

        def absorb(buf, kb, masked):
            rows = pl.ds(pl.multiple_of(kb * tile, tile), tile)
            for hh in range(2):
                m = m_ref[hh, 0:1, :]
                s = buf[hh]
                if masked:
                    s = jnp.where(row <= col, s, -1e30)
                m_new = jnp.maximum(m, jnp.max(s, axis=0, keepdims=True))
                p = jnp.exp(s - m_new).astype(BF16)
                acc_ref[hh] = jnp.exp(m - m_new) * acc_ref[hh] + _dot_tn(v_ref[hh, rows, :], p)
                m_ref[hh, 0:1, :] = m_new

        def two_blocks(j, _):
            scores(s_b, 2 * j + 1)
            absorb(s_a, 2 * j, False)
            scores(s_a, 2 * j + 2)
            absorb(s_b, 2 * j + 1, False)
            return 0

        def last_block():
            absorb(s_a, i, True)

        def last_two_blocks():
            scores(s_b, i)
            absorb(s_a, i - 1, False)
            absorb(s_b, i, True)

        scores(s_a, 0)
        m_ref[...] = jnp.full(m_ref.shape, -1e30, F32)
        acc_ref[...] = jnp.zeros_like(acc_ref)
        lax.fori_loop(0, i // 2, two_blocks, 0)
        lax.cond(i % 2 == 0, last_block, last_two_blocks)
        outs = []
        for hh in range(2):
            m, acc = m_ref[hh, 0:1, :], acc_ref[hh]
            l = -acc[HEAD_DIM:HEAD_DIM + 1, :]
            outs.append((acc / l)[:HEAD_DIM, :])
            neg_lse = [part.astype(F32) for part in _split3(-(m + jnp.log(l)))]
            q2t_ref[hh] = _place3(sub, L_LSE, neg_lse, q[hh].astype(F32).T).astype(BF16)
        att_ref[...] = jnp.concatenate(outs, axis=0).T
        @pl.when(step == steps - 1)
        def _():
            finish_near()
            finish_far()

    pair = pl.BlockSpec((2, tile, LANE), lambda hp, i: (hp, i, 0))
    full = pl.BlockSpec((2, seq, LANE), lambda hp, i: (hp, 0, 0))
    return pl.pallas_call(
        body, name="attention_forward", grid=(N_HEADS // 2, nb),
        in_specs=[pair, full, full, ANY],
        out_specs=[pl.BlockSpec((tile, LANE), lambda hp, i: (i, hp)),
                   pl.BlockSpec((2, LANE, tile), lambda hp, i: (hp, 0, i)), ANY],
        out_shape=[jax.ShapeDtypeStruct((seq, D_ATT), F32),
                   jax.ShapeDtypeStruct((N_HEADS, LANE, seq), BF16),
                   jax.ShapeDtypeStruct((N_DEV,) + w_out.shape, w_out.dtype)],
        scratch_shapes=[pltpu.VMEM((2, tile, tile), F32), pltpu.VMEM((2, tile, tile), F32),
                        pltpu.VMEM((2, 8, tile), F32), pltpu.VMEM((2, LANE, tile), F32),
                        pltpu.SemaphoreType.DMA((N_GATHER_SEMS,)), pltpu.SemaphoreType.DMA((N_GATHER_SEMS,)),
                        pltpu.SemaphoreType.DMA((1,))],
        compiler_params=_params(("arbitrary", "arbitrary")),
    )(qp, kp, vp, w_out)


def _window_sum(x, halo, window, transposed):
    tile = x.shape[0]

    def split_cat(a):
        hi = a.astype(BF16)
        return jnp.concatenate([hi, (a - hi.astype(F32)).astype(BF16)], axis=1)

    def fold(r):
        return r[:, :LANE] + r[:, LANE:]

    r = lax.broadcasted_iota(jnp.int32, (tile, tile), 0)
    c = lax.broadcasted_iota(jnp.int32, (tile, tile), 1)
    rh = lax.broadcasted_iota(jnp.int32, (HALO, HALO), 0)
    ch = lax.broadcasted_iota(jnp.int32, (HALO, HALO), 1)
    if not transposed:
        band = (c <= r) & (r - c < window)
        edge = (rh + HALO - ch) < window
    else:
        band = (r <= c) & (c - r < window)
        edge = (HALO + ch - rh) < window
    out = fold(_dot(band.astype(BF16), split_cat(x)))
    reach = fold(_dot(edge.astype(BF16), split_cat(halo)))
    if not transposed:
        return jnp.concatenate([out[:HALO] + reach, out[HALO:]], axis=0)
    return jnp.concatenate([out[:tile - HALO], out[tile - HALO:] + reach], axis=0)


def _silu_parts(g):
    sig = _sigmoid(g)
    return g * sig, sig * (1.0 + g * (1.0 - sig))


def _middle(x, tgt, att, g_att, g_pool, p, vecs, pool_vecs, w_out, w_pool, tile):
    seq = x.shape[0]
    nt = seq // tile
    halo_blocks = tile // HALO

    def body(x_ref, tgt_ref, att_ref, ga_ref, gp_ref, p_ref, ph_ref, vec_ref, pvec_ref, wo_ref, wp_ref,
             dxa_ref, do2_ref, dga_ref, dgp_ref, dpooled_ref, gwo_ref, dwp_ref, dvec_ref, dwo_ref, dpvec_ref):
        i = pl.program_id(0)

        @pl.when(i == 0)
        def _():
            dwo_ref[...] = jnp.zeros_like(dwo_ref)
            dwp_ref[...] = jnp.zeros_like(dwp_ref)
            dvec_ref[...] = jnp.zeros_like(dvec_ref)
            dpvec_ref[...] = jnp.zeros_like(dpvec_ref)

        gate, b_out, ln_g, ln_b = (vec_ref[k:k + 1, :] for k in range(4))
        b_pool, pool_scale = pvec_ref[0:1, :], pvec_ref[1:2, :]
        x = x_ref[...]
        p = p_ref[...]
        p_halo = ph_ref[...] * jnp.where(i > 0, 1.0, 0.0)
        pos = i * tile + lax.broadcasted_iota(jnp.int32, (tile, 1), 0) + 1

        pooled, mixed = [], []
        for g, window in enumerate(POOL_WINDOWS):
            cols = slice(g * GROUP_DIM, (g + 1) * GROUP_DIM)
            wsum = _window_sum(p[:, cols], p_halo[:, cols], window, False)
            count = jnp.minimum(pos, window).astype(F32)
            pooled.append(wsum / count - p[:, cols])
            mixed.append(_dot(pooled[g].astype(BF16), wp_ref[g]) + b_pool[:, cols])
        mixed = jnp.concatenate(mixed, axis=1)
        pool = mixed * pool_scale

        att = att_ref[...]
        g_att, g_pool = ga_ref[...], gp_ref[...]
        silu_a, dsilu_a = _silu_parts(g_att)
        silu_p, dsilu_p = _silu_parts(g_pool)
        y_in = jnp.concatenate([att * silu_a, pool * silu_p], axis=1)
        y = _dot(y_in.astype(BF16), wo_ref[...]) + b_out
        h = ALPHA * x + gate * y
        mu = jnp.mean(h, axis=1, keepdims=True)
        hc = h - mu
        var = jnp.mean(hc * hc, axis=1, keepdims=True)
        rstd = lax.rsqrt(var + LN_EPS)
        yhat = hc * rstd
        diff = yhat * ln_g + ln_b - tgt_ref[...]
        loss_rows = jnp.sum(diff * diff, axis=1, keepdims=True)
        d_out = diff * (1.0 / D)

        d_yhat = d_out * ln_g
        dh = rstd * (d_yhat - jnp.mean(d_yhat, axis=1, keepdims=True)
                     - yhat * jnp.mean(d_yhat * yhat, axis=1, keepdims=True))
        dxa_ref[...] = ALPHA * dh
        dy = dh * gate
        dyb = dy.astype(BF16)
        lane = _lanes((1, D))
        loss_row = jnp.where(lane == 0, (0.5 / D) * jnp.sum(loss_rows, axis=0, keepdims=True), 0.0)
        dvec_ref[5:6, :] += jnp.sum(dh * y, axis=0, keepdims=True)
        dvec_ref[0:1, :] += jnp.sum(dy, axis=0, keepdims=True)
        dvec_ref[1:2, :] += jnp.sum(d_out * yhat, axis=0, keepdims=True)
        dvec_ref[2:3, :] += jnp.sum(d_out, axis=0, keepdims=True)
        dvec_ref[4:5, :] += loss_row

        dwo_ref[...] += _dot(y_in.T.astype(BF16), dyb)
        d_yin = _dot_nt(dyb, wo_ref[...])
        d_a, d_pl = d_yin[:, :D_ATT], d_yin[:, D_ATT:]
        d_att = d_a * silu_a
        d_att_t = d_att.T
        prod_t = (d_att * att).T
        sub = lax.broadcasted_iota(jnp.int32, (HEAD_DIM, tile), 0)
        for h in range(N_HEADS):
            rows = slice(h * HEAD_DIM, (h + 1) * HEAD_DIM)
            delta = jnp.sum(prod_t[rows], axis=0, keepdims=True)
            extra = _place3(sub, 0, [part.astype(F32) for part in _split3(delta)], 0.0)
            do2_ref[h] = jnp.concatenate([d_att_t[rows], extra], axis=0).astype(BF16)
        dga_ref[...] = d_a * att * dsilu_a
        dgp_ref[...] = d_pl * pool * dsilu_p
        d_pool = d_pl * silu_p
        d_mixed = d_pool * pool_scale
        dpvec_ref[0:1, :] += jnp.sum(d_mixed, axis=0, keepdims=True)
        dpvec_ref[1:2, :] += jnp.sum(d_pool * mixed, axis=0, keepdims=True)
        d_pooled = []
        for g in range(len(POOL_WINDOWS)):
            cols = slice(g * GROUP_DIM, (g + 1) * GROUP_DIM)
            dmb = d_mixed[:, cols].astype(BF16)
            dwp_ref[g] += _dot(pooled[g].T.astype(BF16), dmb)
            d_pooled.append(_dot_nt(dmb, wp_ref[g]))
        dpooled_ref[...] = jnp.concatenate(d_pooled, axis=1)

        @pl.when(i == nt - 1)
        def _():
            gwo_ref[...] = dwo_ref[...].astype(BF16)
            dvec_ref[3:4, :] = jnp.concatenate([dpvec_ref[0:1, :], dpvec_ref[1:2, :]], axis=1)

    tok = lambda width: pl.BlockSpec((tile, width), lambda i: (i, 0))
    whole = lambda a: pl.BlockSpec(a.shape, lambda i: (0,) * a.ndim)
    halo = pl.BlockSpec((HALO, D_POOL), lambda i: (jnp.maximum(i * halo_blocks - 1, 0), 0))
    half = jax.ShapeDtypeStruct((seq, D_ATT), F32)
    outs = [jax.ShapeDtypeStruct((seq, D), F32), jax.ShapeDtypeStruct((N_HEADS, LANE, seq), BF16), half, half, half,
            jax.ShapeDtypeStruct(w_out.shape, BF16), jax.ShapeDtypeStruct(w_pool.shape, F32),
            jax.ShapeDtypeStruct(vecs.shape, F32)]
    return pl.pallas_call(
        body, name="middle", grid=(nt,),
        in_specs=[tok(D), tok(D), tok(D_ATT), tok(D_ATT), tok(D_POOL), tok(D_POOL), halo,
                  whole(vecs), whole(pool_vecs), whole(w_out), whole(w_pool)],
        out_specs=[tok(D), pl.BlockSpec((N_HEADS, LANE, tile), lambda i: (0, 0, i)),
                   tok(D_ATT), tok(D_POOL), tok(D_POOL),
                   whole(w_out), whole(w_pool), whole(vecs)],
        out_shape=outs,
        scratch_shapes=[pltpu.VMEM(w_out.shape, F32), pltpu.VMEM(pool_vecs.shape, F32)],
        compiler_params=_params(("arbitrary",)),
    )(x, tgt, att, g_att, g_pool, p, p, vecs, pool_vecs, w_out, w_pool)


def _attention_backward(q2t, kp, vp, do2t, gw_out, vecs, pool, tile):
    seq = kp.shape[1]
    nb = seq // tile
    last = N_HEADS // 2 - 1

    def body(qt_ref, k_ref, v_ref, dot_ref, gwo_hbm, vecs_hbm, pool_hbm,
             dq_ref, dk_ref, dv_ref, dcum_ref, g_out_ref, vecs_sum_ref, pool_sum_ref,
             dq_acc, dk_acc, dv_acc, gwo_ref, vecs_ref, pool_ref,
             r1, s2, r2, v_sib, v_chip, v_recv, p_sib, p_chip, p_recv, send_sems, recv_sems):
        hp = pl.program_id(0)
        start, middle, fold, finish = _reduce_stages(
            [gwo_ref], [g_out_ref], [r1], [s2], [r2],
            [(vecs_ref, vecs_sum_ref, v_sib, v_chip, v_recv), (pool_ref, pool_sum_ref, p_sib, p_chip, p_recv)],
            send_sems, recv_sems)

        @pl.when(hp == 0)
        def _():
            pltpu.sync_copy(gwo_hbm, gwo_ref)
            pltpu.sync_copy(vecs_hbm, vecs_ref)
            pltpu.sync_copy(pool_hbm, pool_ref)
            start()

        pl.when(hp == 1)(middle)
        pl.when(hp == 2)(fold)

        row = lax.broadcasted_iota(jnp.int32, (tile, tile), 0)
        col = lax.broadcasted_iota(jnp.int32, (tile, tile), 1)
        dq_acc[...] = jnp.zeros_like(dq_acc)

        def kv_block(kb, _):
            krows = pl.ds(pl.multiple_of(kb * tile, tile), tile)
            k = [k_ref[hh, krows, :] for hh in range(2)]
            v = [v_ref[hh, krows, :] for hh in range(2)]
            k_t = [k[hh].T for hh in range(2)]

            def q_block(qb, masked):
                qcols = pl.ds(pl.multiple_of(qb * tile, tile), tile)
                for hh in range(2):
                    q_t = qt_ref[hh, :, qcols]
                    do_t = dot_ref[hh, :, qcols]
                    s_t = _dot(k[hh], q_t)
                    if masked:
                        s_t = jnp.where(row <= col, s_t, -1e30)
                    p_t = jnp.exp(s_t)
                    ds_t = (p_t * _dot(v[hh], do_t)).astype(BF16)
                    dv_new = _dot_nt(do_t, p_t.astype(BF16))
                    dk_new = _dot_nt(q_t, ds_t)
                    if masked:
                        dv_acc[hh], dk_acc[hh] = dv_new, dk_new
                    else:
                        dv_acc[hh] += dv_new
                        dk_acc[hh] += dk_new
                    dq_acc[hh, :, qcols] += _dot(k_t[hh], ds_t)

            q_block(kb, True)

            def two_later_blocks(j, _):
                q_block(kb + 1 + 2 * j, False)
                q_block(kb + 2 + 2 * j, False)
                return 0

            later = nb - 1 - kb
            lax.fori_loop(0, later // 2, two_later_blocks, 0)
            pl.when(later % 2 == 1)(lambda: q_block(nb - 1, False))
            for hh in range(2):
                dk = dk_acc[hh]
                dk_ref[hh, :, krows] = dk.astype(BF16)
                dv_ref[hh, :, krows] = dv_acc[hh].astype(BF16)
                dcum_ref[hh, :, krows] = -dk[L_CK:L_CK + 1, :]
            return 0

        lax.fori_loop(0, nb, kv_block, 0)
        for hh in range(2):
            dq = dq_acc[hh]
            dcum_ref[hh] += dq[L_CQ:L_CQ + 1, :]
            dq_ref[hh] = (dq * Q_SCALE).astype(BF16)
        pl.when(hp == last)(finish)

    pair = pl.BlockSpec((2, seq, LANE), lambda hp: (hp, 0, 0))
    pair_t = pl.BlockSpec((2, LANE, seq), lambda hp: (hp, 0, 0))
    whole = lambda shape: pl.BlockSpec(shape, lambda hp: (0,) * len(shape))
    grad = jax.ShapeDtypeStruct((N_HEADS, LANE, seq), BF16)
    return pl.pallas_call(
        body, name="attention_backward", grid=(N_HEADS // 2,),
        in_specs=[pair_t, pair, pair, pair_t, ANY, ANY, ANY],
        out_specs=[pair_t, pair_t, pair_t, pl.BlockSpec((2, 1, seq), lambda hp: (hp, 0, 0)),
                   whole(gw_out.shape[1:]), whole(vecs.shape), whole(pool.shape)],
        out_shape=[grad, grad, grad, jax.ShapeDtypeStruct((N_HEADS, 1, seq), F32),
                   jax.ShapeDtypeStruct(gw_out.shape[1:], F32), jax.ShapeDtypeStruct(vecs.shape, F32),
                   jax.ShapeDtypeStruct(pool.shape, F32)],
        scratch_shapes=[pltpu.VMEM((2, LANE, seq), F32), pltpu.VMEM((2, LANE, tile), F32),
                        pltpu.VMEM((2, LANE, tile), F32), pltpu.VMEM(gw_out.shape, BF16),
                        pltpu.VMEM(vecs.shape, F32), pltpu.VMEM(pool.shape, F32)]
        + _reduce_scratch(gw_out, [vecs, pool]),
        compiler_params=_params(("arbitrary",)),
    )(q2t, kp, vp, do2t, gw_out, vecs, pool)


def _inproj_backward(dqp, dkp, dvp, d_cum, f, d_pooled, d_ga, d_gp, x, dxa, u, mod, w_main, w_f, tile):
    seq = x.shape[0]
    nt = seq // tile
    halo_blocks = tile // HALO

    def body(dq_ref, dk_ref, dv_ref, dcum_ref, f_ref, dpo_ref, dph_ref, dga_ref, dgp_ref, x_ref, dxa_ref, u_ref,
             mod_ref, w_ref, wf_ref,
             dx_ref, dproj_ref, dwf_ref, db_ref, dbf_ref, dmod_ref, carry_ref):
        step = pl.program_id(0)
        i = nt - 1 - step

        @pl.when(step == 0)
        def _():
            carry_ref[...] = jnp.zeros_like(carry_ref)
            dwf_ref[...] = jnp.zeros_like(dwf_ref)
            db_ref[...] = jnp.zeros_like(db_ref)
            dbf_ref[...] = jnp.zeros_like(dbf_ref)
            dmod_ref[...] = jnp.zeros_like(dmod_ref)

        ones = jnp.ones((8, tile), BF16)

        def emit(chunk, val):
            cols = pl.ds(chunk * COL_CHUNK, COL_CHUNK)
            db_ref[0:1, cols] += jnp.sum(val, axis=0, keepdims=True)
            vb = val.astype(BF16)
            dproj_ref[:, pl.ds((chunk - 3) * COL_CHUNK, COL_CHUNK)] = vb
            return _dot(vb, w_ref[cols, :])

        d_u = jnp.zeros((tile, D), F32)
        for chunk, ref in enumerate((dq_ref, dk_ref, dv_ref)):
            cols = pl.ds(chunk * COL_CHUNK, COL_CHUNK)
            val_t = ref[:, 0:HEAD_DIM, :].reshape(COL_CHUNK, tile)
            db_ref[:, cols] += _dot_nt(ones, val_t)
            d_u += _dot_tn(val_t, w_ref[cols, :])

        d_pooled = dpo_ref[...]
        d_halo = dph_ref[...] * jnp.where(i < nt - 1, 1.0, 0.0)
        pos = i * tile + lax.broadcasted_iota(jnp.int32, (tile, 1), 0) + 1
        d_p = []
        for g, window in enumerate(POOL_WINDOWS):
            cols = slice(g * GROUP_DIM, (g + 1) * GROUP_DIM)
            scaled = d_pooled[:, cols] / jnp.minimum(pos, window).astype(F32)
            d_p.append(_window_sum(scaled, d_halo[:, cols] * (1.0 / window), window, True) - d_pooled[:, cols])
        d_u += emit(3, jnp.concatenate(d_p, axis=1))
        d_u += emit(4, dga_ref[...])
        d_u += emit(5, dgp_ref[...])

        row = lax.broadcasted_iota(jnp.int32, (tile, tile), 0)
        col = lax.broadcasted_iota(jnp.int32, (tile, tile), 1)
        later = (row >= col).astype(BF16)
        d_logf = sum(_dot(part, later) for part in _split3(dcum_ref[:, 0, :])) + carry_ref[:, 0:1]
        carry_ref[:, 0:1] = d_logf[:, 0:1]
        d_f = d_logf * _sigmoid(-f_ref[...].T[0:N_HEADS, :])
        d_f = jnp.concatenate([d_f, jnp.zeros((LANE - N_HEADS, tile), F32)], axis=0)
        dbf_ref[...] += sum(_dot_nt(ones, part) for part in _split3(d_f))
        d_fb = d_f.astype(BF16)
        d_u += _dot_tn(d_fb, wf_ref[...])
        dwf_ref[...] += _dot(d_fb, u_ref[...])

        x = x_ref[...]
        dx_ref[...] = dxa_ref[...] + d_u * mod_ref[0:1, :]
        dmod_ref[0:1, :] += jnp.sum(d_u * x, axis=0, keepdims=True)
        dmod_ref[1:2, :] += jnp.sum(d_u, axis=0, keepdims=True)

    rev = lambda step: nt - 1 - step
    tok = lambda width: pl.BlockSpec((tile, width), lambda s: (rev(s), 0))
    head_block = pl.BlockSpec((N_HEADS, LANE, tile), lambda s: (0, 0, rev(s)))
    whole = lambda a: pl.BlockSpec(a.shape, lambda s: (0,) * a.ndim)
    halo = pl.BlockSpec((HALO, D_POOL), lambda s: (jnp.minimum((rev(s) + 1) * halo_blocks, seq // HALO - 1), 0))
    small = lambda width: jax.ShapeDtypeStruct((8, width), F32)
    n_rest = N_MAIN - OFF_P
    return pl.pallas_call(
        body, name="inproj_backward", grid=(nt,),
        in_specs=[head_block, head_block, head_block, pl.BlockSpec((N_HEADS, 1, tile), lambda s: (0, 0, rev(s))),
                  tok(LANE), tok(D_POOL), halo, tok(D_ATT), tok(D_POOL),
                  tok(D), tok(D), tok(D),
                  whole(mod), whole(w_main), whole(w_f)],
        out_specs=[tok(D), tok(n_rest), pl.BlockSpec((LANE, D), lambda s: (0, 0)),
                   pl.BlockSpec((8, N_MAIN), lambda s: (0, 0)), pl.BlockSpec((8, LANE), lambda s: (0, 0)),
                   pl.BlockSpec((8, D), lambda s: (0, 0))],
        out_shape=[jax.ShapeDtypeStruct((seq, D), F32), jax.ShapeDtypeStruct((seq, n_rest), BF16),
                   jax.ShapeDtypeStruct((LANE, D), F32), small(N_MAIN), small(LANE), small(D)],
        scratch_shapes=[pltpu.VMEM((8, LANE), F32)],
        compiler_params=_params(("arbitrary",)),
    )(dqp, dkp, dvp, d_cum, f, d_pooled, d_pooled, d_ga, d_gp, x, dxa, u, mod, w_main, w_f)


def _weight_grads(dq_t, dk_t, dv_t, dw_f, dproj, u, k_tile, small, rows):
    seq = u.shape[0]
    nk = seq // k_tile
    head_rows = N_HEADS * HEAD_DIM
    width = D // 2
    shard = D_IN // N_DEV
    assert nk >= 4

    def body(dq_ref, dk_ref, dv_ref, dwf_ref, dp_ref, u_ref, small_ref, rows_ref,
             left_sum_ref, right_sum_ref, total_ref, rows_all_ref, acc_ref, left_ref, right_ref,
             r1, s2, r2, sm_sib, sm_chip, sm_recv, rows_land, send_sems, recv_sems,
             own_r1, own_s2, own_r2, own_send_sems, own_recv_sems):
        half, k = pl.program_id(0), pl.program_id(1)
        start, middle, fold, finish = _reduce_stages(
            [left_ref], [left_sum_ref], [r1], [s2], [r2],
            [(small_ref, total_ref, sm_sib, sm_chip, sm_recv)], send_sems, recv_sems,
            rows=(rows_ref, rows_land, rows_all_ref))
        own_stages = _reduce_stages(
            [right_ref], [right_sum_ref], [own_r1], [own_s2], [own_r2], [], own_send_sems, own_recv_sems)
        on_right = lambda step: jnp.logical_and(half == 1, k == step)
        pl.when(on_right(0))(start)
        pl.when(on_right(1))(middle)
        pl.when(on_right(nk // 2))(fold)

        @pl.when(k == 0)
        def _():
            acc_ref[...] = jnp.zeros_like(acc_ref)

        tokens = u_ref[...]
        for j, ref in enumerate((dq_ref, dk_ref, dv_ref)):
            acc_ref[pl.ds(j * head_rows, head_rows), :] += _dot(ref[...].reshape(head_rows, k_tile), tokens)
        for j in range(dproj.shape[1] // COL_CHUNK):
            cols = pl.ds(j * COL_CHUNK, COL_CHUNK)
            acc_ref[pl.ds(F_HI + j * COL_CHUNK, COL_CHUNK), :] += _dot_tn(dp_ref[:, cols], tokens)

        def slots_to(out_ref):
            acc_ref[F_LO:F_HI, :] = dwf_ref[0:N_HEADS, :]
            for slot in range(N_DEV):
                out_ref[slot] = acc_ref[slot * shard:(slot + 1) * shard, :].astype(BF16)

        @pl.when(jnp.logical_and(half == 0, k == nk - 1))
        def _():
            slots_to(left_ref)

        @pl.when(on_right(nk - 1))
        def _():
            slots_to(right_ref)
            own_stages[0]()
            finish()
            for stage in own_stages[1:]:
                stage()

    heads = pl.BlockSpec((N_HEADS, HEAD_DIM, k_tile), lambda h, k: (0, 0, k))
    whole = lambda a: pl.BlockSpec(a.shape, lambda h, k: (0,) * a.ndim)
    slots = jax.ShapeDtypeStruct((N_DEV, shard, width), BF16)
    shard_sum = jax.ShapeDtypeStruct((shard, width), F32)
    out_shape = [shard_sum, shard_sum, jax.ShapeDtypeStruct(small.shape, F32),
                 jax.ShapeDtypeStruct((N_DEV,) + rows.shape, F32)]
    return pl.pallas_call(
        body, name="weight_grads", grid=(2, nk),
        in_specs=[heads, heads, heads, pl.BlockSpec((dw_f.shape[0], width), lambda h, k: (0, h)),
                  pl.BlockSpec((k_tile, dproj.shape[1]), lambda h, k: (k, 0)),
                  pl.BlockSpec((k_tile, width), lambda h, k: (k, h)), whole(small), whole(rows)],
        out_specs=[whole(a) for a in out_shape], out_shape=out_shape,
        scratch_shapes=[pltpu.VMEM((D_IN, width), F32), pltpu.VMEM(slots.shape, BF16), pltpu.VMEM(slots.shape, BF16)]
        + _reduce_scratch(slots, [small], rows) + _reduce_scratch(slots, []),
        compiler_params=_params(("arbitrary", "arbitrary")),
    )(dq_t, dk_t, dv_t, dw_f, dproj, u, small, rows)


def _adamw(w, g, m, v):
    m = ADAM_B1 * m + (1.0 - ADAM_B1) * g
    v = ADAM_B2 * v + (1.0 - ADAM_B2) * (g * g)
    m_hat = m / (1.0 - ADAM_B1 ** ADAM_STEP)
    v_hat = v / (1.0 - ADAM_B2 ** ADAM_STEP)
    delta = -ADAM_LR * (m_hat / (jnp.sqrt(v_hat) + ADAM_EPS) + ADAM_WD * w)
    return delta, m, v


SUBLANES = 8


def _adamw_packed(g_parts, w, m, v, name, chunks=4):
    n_parts, (rows, part_cols) = len(g_parts), g_parts[0].shape
    cols = n_parts * part_cols
    per_row, per_part = cols // LANE, part_cols // LANE
    assert part_cols % LANE == 0 and per_row == SUBLANES and w.shape == (rows * per_row, LANE)
    step = -(-rows // (chunks * SUBLANES)) * SUBLANES
    bounds = [(r0, min(r0 + step, rows)) for r0 in range(0, rows, step)]

    def body(*refs):
        g_hbm, (w_hbm, m_hbm, v_hbm, og_hbm, od_hbm, om_hbm, ov_hbm) = refs[:n_parts], refs[n_parts:n_parts + 7]
        g_buf, in_buf, out_buf, in_sems, out_sems = refs[n_parts + 7:]

        def copies_in(c):
            r0, r1 = bounds[c]
            packed = slice(r0 * per_row, r1 * per_row)
            return [pltpu.make_async_copy(g_hbm[p].at[r0:r1], g_buf.at[p, r0:r1], in_sems.at[c, 3 + p])
                    for p in range(n_parts)] + [
                pltpu.make_async_copy(src.at[packed], in_buf.at[i, packed], in_sems.at[c, i])
                for i, src in enumerate((w_hbm, m_hbm, v_hbm))]

        def copies_out(c):
            r0, r1 = bounds[c]
            packed = slice(r0 * per_row, r1 * per_row)
            return [pltpu.make_async_copy(out_buf.at[i, packed], dst.at[packed], out_sems.at[c, i])
                    for i, dst in enumerate((og_hbm, od_hbm, om_hbm, ov_hbm))]

        for c in range(len(bounds)):
            for cp in copies_in(c):
                cp.start()
        for c, (r0, r1) in enumerate(bounds):
            for cp in copies_in(c):
                cp.wait()
            for j in range(per_row):
                lanes = pl.ds(r0 * per_row + j, r1 - r0, stride=per_row)
                g_part = g_buf[j // per_part, r0:r1, (j % per_part) * LANE:(j % per_part + 1) * LANE]
                results = _adamw(in_buf[0, lanes, :], g_part, in_buf[1, lanes, :], in_buf[2, lanes, :])
                for i, val in enumerate((g_part,) + results):
                    out_buf[i, lanes, :] = val
            for cp in copies_out(c):
                cp.start()
        for c in range(len(bounds)):
            for cp in copies_out(c):
                cp.wait()

    shape = jax.ShapeDtypeStruct(w.shape, F32)
    return pl.pallas_call(
        body, name=name,
        in_specs=[ANY] * (n_parts + 3), out_specs=[ANY] * 4, out_shape=[shape] * 4,
        scratch_shapes=[pltpu.VMEM((n_parts, rows, part_cols), F32), pltpu.VMEM((3,) + w.shape, F32),
                        pltpu.VMEM((4,) + w.shape, F32),
                        pltpu.SemaphoreType.DMA((len(bounds), 3 + n_parts)), pltpu.SemaphoreType.DMA((len(bounds), 4))],
        compiler_params=_params(),
    )(*g_parts, w, m, v)


def _ada_adamw(sc_all, d_ada, w, m, v, chunks=4):
    rows, cols = w.shape
    step, sub = rows // chunks, 32
    assert rows % chunks == 0 and step % LANE == 0 and step % sub == 0

    def body(sc_ref, d_ref, w_hbm, m_hbm, v_hbm, og_hbm, od_hbm, om_hbm, ov_hbm, in_buf, out_buf, in_sems, out_sems):
        def copies_in(c):
            part = slice(c * step, (c + 1) * step)
            return [pltpu.make_async_copy(src.at[part], in_buf.at[i, part], in_sems.at[c, i])
                    for i, src in enumerate((w_hbm, m_hbm, v_hbm))]

        def copies_out(c):
            part = slice(c * step, (c + 1) * step)
            return [pltpu.make_async_copy(out_buf.at[i, part], dst.at[part], out_sems.at[c, i])
                    for i, dst in enumerate((og_hbm, od_hbm, om_hbm, ov_hbm))]

        for c in range(chunks):
            for cp in copies_in(c):
                cp.start()
        for c in range(chunks):
            sc_t = sc_ref[:, c * step:(c + 1) * step].T
            for cp in copies_in(c):
                cp.wait()
            for r0 in range(0, step, sub):
                part = slice(c * step + r0, c * step + r0 + sub)
                g = sc_t[r0:r0 + sub, 0:1] * d_ref[0:1, :]
                for b in range(1, N_DEV):
                    g = g + sc_t[r0:r0 + sub, b:b + 1] * d_ref[b:b + 1, :]
                results = _adamw(in_buf[0, part, :], g, in_buf[1, part, :], in_buf[2, part, :])
                for i, val in enumerate((g,) + results):
                    out_buf[i, part, :] = val
            for cp in copies_out(c):
                cp.start()
        for c in range(chunks):
            for cp in copies_out(c):
                cp.wait()

    in_vmem = pl.BlockSpec(memory_space=pltpu.VMEM)
    shape = jax.ShapeDtypeStruct(w.shape, F32)
    return pl.pallas_call(
        body, name="ada_adamw",
        in_specs=[in_vmem, in_vmem, ANY, ANY, ANY], out_specs=[ANY] * 4, out_shape=[shape] * 4,
        scratch_shapes=[pltpu.VMEM((3,) + w.shape, F32), pltpu.VMEM((4,) + w.shape, F32),
                        pltpu.SemaphoreType.DMA((chunks, 3)), pltpu.SemaphoreType.DMA((chunks, 4))],
        compiler_params=_params(),
    )(sc_all, d_ada, w, m, v)


F_LO, F_HI = 3 * D_ATT, 3 * D_ATT + N_HEADS


def _split_forget(a, axis):
    idx = lambda lo, hi: tuple(slice(lo, hi) if d == axis else slice(None) for d in range(a.ndim))
    pad = [(0, LANE - N_HEADS) if d == axis else (0, 0) for d in range(a.ndim)]
    return jnp.concatenate([a[idx(0, F_LO)], a[idx(F_HI, D_IN)]], axis=axis), jnp.pad(a[idx(F_LO, F_HI)], pad)


def _join_forget(main, f, axis):
    idx = lambda lo, hi: tuple(slice(lo, hi) if d == axis else slice(None) for d in range(main.ndim))
    return jnp.concatenate([main[idx(0, F_LO)], f[idx(0, N_HEADS)], main[idx(F_LO, N_MAIN)]], axis=axis)


def _adamw_small(grad_rows, row_params, whole_params, summed_params, scalar_at):
    n_row, n_whole, n_sum = len(row_params), len(whole_params), len(summed_params)
    n = n_row + n_whole + n_sum

    def body(g_ref, *refs):
        n_in = 3 * n_row + 4 * (n_whole + n_sum)
        ins, outs = list(refs[:n_in]), refs[n_in:]
        for i in range(n):
            if i < n_row:
                row, lo, hi = row_params[i][:3]
                g = g_ref[row:row + 1, lo:hi]
            elif i < n_row + n_whole:
                g = ins.pop(0)[...]
            else:
                parts = ins.pop(0)
                g = parts[0]
                for k in range(1, N_DEV):
                    g = g + parts[k]
            w, m, v = (ins.pop(0)[...] for _ in range(3))
            outs[4 * i][...] = g
            outs[4 * i + 1][...], outs[4 * i + 2][...], outs[4 * i + 3][...] = _adamw(w, g, m, v)
        row, lane = scalar_at
        outs[4 * n][...] = g_ref[row:row + 1, lane:lane + 1]

    shapes = [p[3] for p in row_params] + [p[1] for p in whole_params] + [p[1] for p in summed_params]
    operands = [a for p in row_params for a in p[3:]] + [a for p in whole_params + summed_params for a in p]
    flat = pl.pallas_call(
        body, name="adamw_small",
        out_shape=[jax.ShapeDtypeStruct(w.shape, F32) for w in shapes for _ in range(4)]
        + [jax.ShapeDtypeStruct((1, 1), F32)],
        compiler_params=_params(),
    )(grad_rows, *operands)
    return [flat[4 * i:4 * i + 4] for i in range(n)], flat[4 * n].reshape(())


def kernel(x, c, w_ada, b_ada, w_in, b_in, w_pool_mix, b_pool_mix, pool_scale, w_out, b_out, ln_g, ln_b, loss_target, m_w_ada, m_b_ada, m_w_in, m_b_in, m_w_pool_mix, m_b_pool_mix, m_pool_scale, m_w_out, m_b_out, m_ln_g, m_ln_b, v_w_ada, v_b_ada, v_w_in, v_b_in, v_w_pool_mix, v_b_pool_mix, v_pool_scale, v_w_out, v_b_out, v_ln_g, v_ln_b):
    seq = x.shape[1]
    tile = min(256, seq)
    attn_tile = min(512, max(128, seq // 4))
    me = _dev_index(*_mesh_pos())
    x2, tgt = x[0], loss_target[0]

    rows_of = lambda a: jnp.swapaxes(a, 1, 2)[0]
    w_main, w_f, sc_all, ada_mine = _gather_and_ada(c, rows_of(w_in).astype(BF16), w_ada[0])
    ada = ada_mine.reshape(1, D_ADA) + b_ada
    shift, scale, gate = ada[:, 0:D], ada[:, D:2 * D], ada[:, 2 * D:]
    mod = jnp.concatenate([1.0 + scale, shift, jnp.zeros((6, D), F32)], axis=0)
    b_main, b_f = _split_forget(b_in, 1)

    qp, kp, vp, f, p, g_att, g_pool, u = _inproj_forward(x2, mod, w_main, w_f, b_main, b_f, tile)
    att, q2t, w_out_g = _attention_forward(qp, kp, vp, w_out[0].astype(BF16), attn_tile)

    vecs = jnp.concatenate([gate, b_out, ln_g, ln_b, jnp.zeros((4, D), F32)], axis=0)
    pool_vecs = jnp.concatenate([b_pool_mix.reshape(1, D_POOL), pool_scale, jnp.zeros((6, D_POOL), F32)], axis=0)
    dxa, do2, d_ga, d_gp, d_pooled, gw_out, dw_pool, dvec = _middle(
        x2, tgt, att, g_att, g_pool, p, vecs, pool_vecs, w_out_g.reshape(D, D), w_pool_mix[0].astype(BF16), tile)

    pool_rows = w_pool_mix.shape[1] * GROUP_DIM
    dqp, dkp, dvp, d_cum, g_out, dvec_sum, dw_pool_sum = _attention_backward(
        q2t, kp, vp, do2, gw_out.reshape(N_DEV, D // N_DEV, D), dvec, dw_pool.reshape(pool_rows, GROUP_DIM), attn_tile)
    dx, dproj, dw_f, db_main, db_f, dmod = _inproj_backward(
        dqp, dkp, dvp, d_cum, f, d_pooled, d_ga, d_gp, x2, dxa, u, mod, w_main, w_f, tile)
    d_ada = jnp.concatenate([dmod[1:2], dmod[0:1], dvec[5:6]], axis=1)
    g_in_left, g_in_right, g_b_in, d_ada_all = _weight_grads(
        dqp, dkp, dvp, dw_f, dproj, u, min(512, seq), _join_forget(db_main[0:1], db_f[0:1], 1), d_ada)

    packed = lambda a: jnp.transpose(a.reshape(SUBLANES, LANE, -1), (2, 0, 1)).reshape(-1, LANE)
    outs_in = _adamw_packed((g_in_left, g_in_right), packed(w_in), packed(m_w_in), packed(v_w_in), "adamw_w_in")
    g_w_in, d_w_in, nm_w_in, nv_w_in = (
        jnp.transpose(a.reshape(-1, SUBLANES, LANE), (1, 2, 0)).reshape(D, -1) for a in outs_in)
    flat_pool = lambda a: a.reshape(1, D_POOL)
    pool_2d = lambda a: a.reshape(pool_rows, GROUP_DIM)
    rows, loss = _adamw_small(
        dvec_sum,
        [(0, 0, D, b_out, m_b_out, v_b_out), (1, 0, D, ln_g, m_ln_g, v_ln_g), (2, 0, D, ln_b, m_ln_b, v_ln_b),
         (3, 0, D_POOL, flat_pool(b_pool_mix), flat_pool(m_b_pool_mix), flat_pool(v_b_pool_mix)),
         (3, D_POOL, 2 * D_POOL, pool_scale, m_pool_scale, v_pool_scale)],
        [(g_out, w_out[0], m_w_out[0], v_w_out[0]),
         (dw_pool_sum, pool_2d(w_pool_mix), pool_2d(m_w_pool_mix), pool_2d(v_w_pool_mix)),
         (g_b_in, b_in, m_b_in, v_b_in)],
        [(d_ada_all, b_ada, m_b_ada, v_b_ada)],
        scalar_at=(4, 0))
    small = {"b_out": rows[0], "ln_g": rows[1], "ln_b": rows[2],
             "b_pool": [a.reshape(b_pool_mix.shape) for a in rows[3]], "pool_scale": rows[4],
             "w_pool": [a.reshape(w_pool_mix.shape) for a in rows[6]], "b_in": rows[7]}
    g_s, d_s, nm_s, nv_s = ({k: r[j] for k, r in small.items()} for j in range(4))
    g_w_out, d_w_out, nm_w_out, nv_w_out = rows[5]
    g_b_ada, d_b_ada, nm_b_ada, nv_b_ada = rows[8]

    d_ada_local = lax.dynamic_slice_in_dim(d_ada_all.reshape(N_DEV, D_ADA), me * (D_ADA // N_DEV), D_ADA // N_DEV, axis=1)
    g_w_ada, d_w_ada, nm_w_ada, nv_w_ada = _ada_adamw(sc_all, d_ada_local, w_ada[0], m_w_ada[0], v_w_ada[0])

    def ordered(w_ada_, b_ada_, w_in_, w_out_, s):
        return (w_ada_[None], b_ada_, w_in_[None], s["b_in"], s["w_pool"], s["b_pool"], s["pool_scale"],
                w_out_[None], s["b_out"], s["ln_g"], s["ln_b"])

    return (loss, dx[None],
            *ordered(g_w_ada, g_b_ada, g_w_in, g_w_out, g_s),
            *ordered(d_w_ada, d_b_ada, d_w_in, d_w_out, d_s),
            *ordered(nm_w_ada, nm_b_ada, nm_w_in, nm_w_out, nm_s),
            *ordered(nv_w_ada, nv_b_ada, nv_w_in, nv_w_out, nv_s))
```

```python
import jax
import jax.numpy as jnp
from jax import lax
from jax.experimental import pallas as pl
from jax.experimental.pallas import tpu as pltpu

F32 = jnp.float32
BF16 = jnp.bfloat16

N_DEV = 8
D = 1024
N_HEADS = 8
HEAD_DIM = 64
D_ATT = 512
D_POOL = 512
POOL_WINDOWS = (2, 4, 8, 16)
GROUP_DIM = 128
HALO = 16
LANE = 128
BF16_TILE_ROWS = 16
WRITE_PRIORITY = 1
D_IN = 3080
D_ADA = 3072
N_MAIN = 3072
OFF_P = 1536
COL_CHUNK = 512
Q_SCALE = 0.125
LN_EPS = 1e-5
ALPHA = 2.0 ** 0.25
L_CQ, L_CK, L_LSE = 64, 67, 70

ADAM_LR, ADAM_B1, ADAM_B2, ADAM_EPS, ADAM_WD, ADAM_STEP = 0.001, 0.9, 0.999, 1e-08, 0.01, 10
VMEM_LIMIT = 56 * 1024 * 1024

MESH = pl.DeviceIdType.MESH
ANY = pl.BlockSpec(memory_space=pl.ANY)


def _params(sem=None, vmem=VMEM_LIMIT):
    return pltpu.CompilerParams(dimension_semantics=sem, vmem_limit_bytes=vmem)


def _split3(a):
    hi = a.astype(BF16)
    r = a - hi.astype(F32)
    mid = r.astype(BF16)
    lo = (r - mid.astype(F32)).astype(BF16)
    return hi, mid, lo


def _dot(a, b):
    return jnp.dot(a, b, preferred_element_type=F32)


def _dot_nt(a, b):
    return lax.dot_general(a, b, (((1,), (1,)), ((), ())), preferred_element_type=F32)


def _dot_tn(a, b):
    return lax.dot_general(a, b, (((0,), (0,)), ((), ())), preferred_element_type=F32)


def _dot3(m01, a):
    hi, mid, lo = _split3(a)
    return _dot(m01, hi) + _dot(m01, mid) + _dot(m01, lo)


def _sigmoid(z):
    return 1.0 / (1.0 + jnp.exp(-z))


def _lanes(shape):
    return lax.broadcasted_iota(jnp.int32, shape, len(shape) - 1)


def _place3(lane, base, parts, other):
    out = other
    for j in range(3):
        out = jnp.where(lane == base + j, parts[j], out)
    return out


def _mesh_pos():
    return lax.axis_index("x"), lax.axis_index("y"), lax.axis_index("c")


def _dev_index(px, py, pc):
    return 4 * px + 2 * py + pc


N_GATHER_SEMS = 11


def _gather_stages(src_ref, out_ref, send_sems, recv_sems, local_sem):
    x, y, c = _mesh_pos()
    me, sibling = (x, y, c), (x, y, 1 - c)
    nbr_x, nbr_y, diag = (1 - x, y), (x, 1 - y), (1 - x, 1 - y)
    half = out_ref.shape[-1] // 2
    left, right = pl.ds(0, half), pl.ds(half, half)

    def copy(k, block, to, cols=None, src=None):
        slot = out_ref.at[_dev_index(*block)]
        if cols is not None:
            slot = slot.at[:, cols]
            src = src if src is None else src.at[:, cols]
        return pltpu.make_async_remote_copy(
            src_ref=slot if src is None else src, dst_ref=slot, send_sem=send_sems.at[k], recv_sem=recv_sems.at[k],
            device_id=to, device_id_type=MESH)

    mine = pltpu.make_async_copy(src_ref, out_ref.at[_dev_index(*me)], local_sem)
    first = [copy(0, me, sibling, src=src_ref),
             copy(1, me, (*nbr_x, c), cols=left, src=src_ref), copy(2, me, (*nbr_y, c), cols=right, src=src_ref),
             copy(9, me, (*nbr_x, c), cols=right, src=src_ref), copy(10, me, (*nbr_y, c), cols=left, src=src_ref)]
    relay = [(1, nbr_x, left, nbr_x), (2, nbr_y, right, nbr_y), (3, diag, left, nbr_y), (4, diag, right, nbr_x)]
    other_half = [(9, nbr_x, right, nbr_x), (10, nbr_y, left, nbr_y)]
    onward = [copy(3, (*nbr_x, c), (*nbr_y, c), cols=left), copy(4, (*nbr_y, c), (*nbr_x, c), cols=right)]
    passed = [copy(4 + k, (*block, c), sibling, cols=None if k < 3 else cols) for k, block, cols, _ in relay]

    def start():
        mine.start()
        for cp in first:
            cp.start()

    def arrived(item):
        k, block, cols, frm = item
        copy(k, (*block, c), (*frm, c), cols=cols).wait_recv()

    def relay_near():
        for j in (0, 1):
            arrived(relay[j])
            onward[j].start()
        for j in (0, 1):
            arrived(other_half[j])
            passed[j].start()

    def relay_far():
        for j in (2, 3):
            arrived(relay[j])
            passed[j].start()

    def from_sibling(items):
        for k, block, cols, _ in items:
            copy(4 + k, (*block, 1 - c), me, cols=None if k < 3 else cols).wait_recv()

    def finish_near():
        copy(0, sibling, me).wait_recv()
        from_sibling(relay[:2])
        mine.wait()

    def finish_far():
        from_sibling(relay[2:])
        for cp in first + onward + passed:
            cp.wait_send()

    return start, relay_near, relay_far, finish_near, finish_far


N_REDUCE_SEMS = 10
N_SMALL_SEMS = 4
N_ROWS_SEMS = 7


def _reduce_stages(ins, gs, r1, s2, r2, smalls, send_sems, recv_sems, rows=None, own=None):
    n = len(ins)
    x, y, c = _mesh_pos()
    me = _dev_index(x, y, c)
    sibling = (x, y, 1 - c)
    chips = [(x, y), (1 - x, y), (x, 1 - y), (1 - x, 1 - y)]
    peers = []
    for p in range(1, N_DEV):
        px, py, pc = (p >> 2) & 1, (p >> 1) & 1, p & 1
        peers.append((1 - x if px else x, 1 - y if py else y, 1 - c if pc else c))
    base_small = N_REDUCE_SEMS * n

    def remote(src, dst, k, to):
        return pltpu.make_async_remote_copy(src_ref=src, dst_ref=dst, send_sem=send_sems.at[k],
                                            recv_sem=recv_sems.at[k], device_id=to, device_id_type=MESH)

    def level1(a, q):
        return remote(ins[a].at[_dev_index(*chips[q], 1 - c)], r1[a].at[q], N_REDUCE_SEMS * a + q, sibling)

    def level2(a, k):
        half = ins[a].shape[-1] // 2
        left, right = pl.ds(0, half), pl.ds(half, half)
        nbr_x, nbr_y = (*chips[1], c), (*chips[2], c)
        src_slot, dst_slot, cols, to = [(0, 0, left, nbr_x), (1, 1, right, nbr_y), (2, 2, left, nbr_x),
                                        (2, 2, right, nbr_y), (0, 0, right, nbr_x), (1, 1, left, nbr_y)][k]
        return remote(s2[a].at[src_slot, :, cols], r2[a].at[dst_slot, :, cols], N_REDUCE_SEMS * a + 4 + k, to)

    to_sibling = [remote(sm[0], sm[2], base_small + 4 * i, sibling) for i, sm in enumerate(smalls)]
    to_chips = [[remote(sm[3], sm[4].at[j], base_small + 4 * i + 1 + j, (*chips[j + 1], c)) for j in range(3)]
                for i, sm in enumerate(smalls)]
    if rows is not None:
        rows_ref, land_ref, all_ref = rows
        base_rows = base_small + 4 * len(smalls)
        row_sends = [remote(rows_ref, land_ref.at[me], base_rows + k, to) for k, to in enumerate(peers)]

    order = (3, 1, 2, 0)

    def mine(a, q):
        buf, sems = own[a]
        return pltpu.make_async_copy(ins[a].at[_dev_index(*chips[q], c)], buf.at[q], sems.at[q])

    def start():
        for a in range(n):
            for q in order:
                level1(a, q).start()
            if own is not None:
                for q in order:
                    mine(a, q).start()
        for cp in to_sibling:
            cp.start()
        if rows is not None:
            for cp in row_sends:
                cp.start()
            land_ref[me] = rows_ref[...]

    def middle():
        for a in range(n):
            for q in order:
                level1(a, q).wait_recv()
                if own is None:
                    kept = ins[a][_dev_index(*chips[q], c)]
                else:
                    mine(a, q).wait()
                    kept = own[a][0][q]
                pair = kept.astype(F32) + r1[a][q].astype(F32)
                if q == 0:
                    gs[a][...] = pair
                else:
                    s2[a][q - 1] = pair.astype(BF16)
                    for k in ((0,), (1,), (2, 3))[q - 1]:
                        level2(a, k).start()
        for i, (small_ref, _, sm_sib, sm_chip, _) in enumerate(smalls):
            to_sibling[i].wait_recv()
            sm_chip[...] = small_ref[...] + sm_sib[...]
            for cp in to_chips[i]:
                cp.start()

    def fold():
        for a in range(n):
            half = ins[a].shape[-1] // 2
            level2(a, 3).wait_recv()
            s2[a][0, :, half:] = (s2[a][0, :, half:].astype(F32) + r2[a][2, :, half:].astype(F32)).astype(BF16)
            level2(a, 4).start()
            level2(a, 2).wait_recv()
            s2[a][1, :, :half] = (s2[a][1, :, :half].astype(F32) + r2[a][2, :, :half].astype(F32)).astype(BF16)
            level2(a, 5).start()

    def finish():
        for a in range(n):
            for k in (0, 1, 4, 5):
                level2(a, k).wait_recv()
            gs[a][...] = gs[a][...] + r2[a][0].astype(F32) + r2[a][1].astype(F32)
            for q in range(4):
                level1(a, q).wait_send()
            for k in range(6):
                level2(a, k).wait_send()
        for i, (_, total_ref, _, sm_chip, sm_recv) in enumerate(smalls):
            for cp in to_chips[i]:
                cp.wait_recv()
            total = None
            for ax in range(2):
                for ay in range(2):
                    dx, dy = x != ax, y != ay
                    term = jnp.where(dx, jnp.where(dy, sm_recv[2], sm_recv[0]), jnp.where(dy, sm_recv[1], sm_chip[...]))
                    total = term if total is None else total + term
            total_ref[...] = total
            for cp in [to_sibling[i]] + to_chips[i]:
                cp.wait_send()
        if rows is not None:
            for k, frm in enumerate(peers):
                remote(rows_ref, land_ref.at[_dev_index(*frm)], base_rows + k, frm).wait_recv()
            all_ref[...] = land_ref[...]
            for cp in row_sends:
                cp.wait_send()

    return start, middle, fold, finish


def _reduce_scratch(shard, smalls, rows=None):
    out = [pltpu.VMEM((lead,) + shard.shape[1:], BF16) for lead in (4, 3, 3)]
    for small in smalls:
        out += [pltpu.VMEM(small.shape, F32), pltpu.VMEM(small.shape, F32), pltpu.VMEM((3,) + small.shape, F32)]
    n_sems = N_REDUCE_SEMS + N_SMALL_SEMS * len(smalls)
    if rows is not None:
        out.append(pltpu.VMEM((N_DEV,) + rows.shape, F32))
        n_sems += N_ROWS_SEMS
    return out + [pltpu.SemaphoreType.DMA((n_sems,))] * 2


def _dot3_rhs(a, b):
    a0, a1, a2 = _split3(a)
    b0, b1, b2 = _split3(b)
    return (_dot(a0, b0) + (_dot(a0, b1) + _dot(a1, b0))
            + (_dot(a0, b2) + _dot(a1, b1) + _dot(a2, b0)))


def _gather_and_ada(c, w_in_rows, w_ada):
    cols = w_ada.shape[1]
    shard = w_in_rows.shape[0]

    def body(c_ref, w_ref, wa_ref, w_main_hbm, w_f_ref, sc_ref, ada_ref,
             w_all_ref, w_f32, wm_buf, c_land, part, ada_land, send_sems, recv_sems, local_sem, x_send, x_recv, out_sems):
        x, y, cc = _mesh_pos()
        me = _dev_index(x, y, cc)
        peers = []
        for p in range(1, N_DEV):
            px, py, pc = (p >> 2) & 1, (p >> 1) & 1, p & 1
            peers.append((1 - x if px else x, 1 - y if py else y, 1 - cc if pc else cc))

        def remote(src, dst, k, to):
            return pltpu.make_async_remote_copy(src_ref=src, dst_ref=dst, send_sem=x_send.at[k], recv_sem=x_recv.at[k],
                                                device_id=to, device_id_type=MESH)

        c_sends = [remote(c_ref, c_land.at[me], k, to) for k, to in enumerate(peers)]
        for cp in c_sends:
            cp.start()
        start, relay_near, relay_far, finish_near, finish_far = _gather_stages(
            w_ref, w_all_ref, send_sems, recv_sems, local_sem.at[0])
        start()
        c_land[me] = c_ref[...]
        for k, frm in enumerate(peers):
            remote(c_ref, c_land.at[_dev_index(*frm)], k, frm).wait_recv()
        c_all = jnp.concatenate([c_land[b] for b in range(N_DEV)], axis=0)
        sc = c_all * _sigmoid(c_all)
        sc_ref[...] = sc
        rows = _dot3_rhs(sc, wa_ref[...])
        for b in range(N_DEV):
            part[b] = rows[b:b + 1, :]
        a_sends = [remote(part.at[_dev_index(*to)], ada_land.at[me], 7 + k, to) for k, to in enumerate(peers)]
        for cp in a_sends:
            cp.start()
        ada_land[me] = part[me]

        relay_near()
        finish_near()

        far_chip = 2 * (1 - x) + (1 - y)

        def stage(slots):
            for slot in slots:
                w_f32[slot * shard:(slot + 1) * shard, :] = w_all_ref[slot].astype(F32)

        def far_rows(k):
            main_row = lambda r: r if r < F_LO else r - N_HEADS
            first, last = 2 * shard * k, 2 * shard * (k + 1) - 1
            first = F_HI if F_LO <= first < F_HI else first
            last = F_LO - 1 if F_LO <= last < F_HI else last
            return (main_row(first) // BF16_TILE_ROWS * BF16_TILE_ROWS,
                    -(-(main_row(last) + 1) // BF16_TILE_ROWS) * BF16_TILE_ROWS)

        def near_rows(k):
            lo, hi = far_rows(k)
            return [(i, a, b) for i, (a, b) in enumerate(((0, lo), (hi, N_MAIN))) if a < b]

        def forget_is_far(k):
            return 2 * shard * k < F_HI and 2 * shard * (k + 1) > F_LO

        def main_copy(i, lo, hi):
            return pltpu.make_async_copy(wm_buf.at[lo:hi], w_main_hbm.at[lo:hi], out_sems.at[i])

        def write_main(i, lo, hi):
            if lo < min(hi, F_LO):
                wm_buf[lo:min(hi, F_LO), :] = w_f32[lo:min(hi, F_LO), :].astype(BF16)
            if max(lo, F_LO) < hi:
                wm_buf[max(lo, F_LO):hi, :] = w_f32[max(lo, F_LO) + N_HEADS:hi + N_HEADS, :].astype(BF16)
            main_copy(i, lo, hi).start(priority=WRITE_PRIORITY)

        def write_forget():
            w_f_ref[...] = jnp.concatenate(
                [w_f32[F_LO:F_HI, :], jnp.zeros((LANE - N_HEADS, D), F32)], axis=0).astype(BF16)

        for k in range(N_DEV // 2):
            @pl.when(far_chip == k)
            def _(k=k):
                stage([slot for slot in range(N_DEV) if slot // 2 != k])
                for i, lo, hi in near_rows(k):
                    write_main(i, lo, hi)
                if not forget_is_far(k):
                    write_forget()

        relay_far()
        for k, frm in enumerate(peers):
            remote(part.at[0], ada_land.at[_dev_index(*frm)], 7 + k, frm).wait_recv()
        ada_ref[...] = ada_land[...]
        finish_far()
        for cp in c_sends + a_sends:
            cp.wait_send()

        for k in range(N_DEV // 2):
            @pl.when(far_chip == k)
            def _(k=k):
                stage([2 * k, 2 * k + 1])
                write_main(2, *far_rows(k))
                if forget_is_far(k):
                    write_forget()
                for i, lo, hi in near_rows(k):
                    main_copy(i, lo, hi).wait()
                main_copy(2, *far_rows(k)).wait()

    vmem = pl.BlockSpec(memory_space=pltpu.VMEM)
    return pl.pallas_call(
        body, name="gather_weights",
        in_specs=[vmem, ANY, vmem], out_specs=[ANY, vmem, vmem, vmem],
        out_shape=[jax.ShapeDtypeStruct((N_MAIN, D), BF16), jax.ShapeDtypeStruct((LANE, D), BF16),
                   jax.ShapeDtypeStruct((N_DEV, D), F32), jax.ShapeDtypeStruct((N_DEV, 1, cols), F32)],
        scratch_shapes=[pltpu.VMEM((N_DEV,) + w_in_rows.shape, BF16), pltpu.VMEM((D_IN, D), F32), pltpu.VMEM((N_MAIN, D), BF16),
                        pltpu.VMEM((N_DEV, 1, D), F32), pltpu.VMEM((N_DEV, 1, cols), F32), pltpu.VMEM((N_DEV, 1, cols), F32),
                        pltpu.SemaphoreType.DMA((N_GATHER_SEMS,)), pltpu.SemaphoreType.DMA((N_GATHER_SEMS,)),
                        pltpu.SemaphoreType.DMA((1,)),
                        pltpu.SemaphoreType.DMA((14,)), pltpu.SemaphoreType.DMA((14,)), pltpu.SemaphoreType.DMA((3,))],
        compiler_params=_params(),
    )(c, w_in_rows, w_ada)


def _inproj_forward(x, mod, w_main, w_f, b_main, b_f, tile):
    seq = x.shape[0]
    nt = seq // tile

    def body(x_ref, mod_ref, w_ref, wf_ref, b_ref, bf_ref,
             qp_ref, kp_ref, vp_ref, f_ref, p_ref, ga_ref, gp_ref, u_ref, carry_ref):
        i = pl.program_id(0)

        @pl.when(i == 0)
        def _():
            carry_ref[...] = jnp.zeros_like(carry_ref)

        u = x_ref[...] * mod_ref[0:1, :] + mod_ref[1:2, :]
        ub = u.astype(BF16)
        u_ref[...] = ub

        f = _dot_nt(ub, wf_ref[...]) + bf_ref[...]
        f_ref[...] = f
        lane = _lanes((tile, LANE))
        log_f = jnp.where(lane < N_HEADS, jnp.minimum(f, 0.0) - jnp.log(1.0 + jnp.exp(-jnp.abs(f))), 0.0)
        row = lax.broadcasted_iota(jnp.int32, (tile, tile), 0)
        col = lax.broadcasted_iota(jnp.int32, (tile, tile), 1)
        tri = (row >= col).astype(BF16)
        cum = _dot3(tri, log_f) + carry_ref[0:1, :]
        carry_ref[0:1, :] = cum[tile - 1:tile, :]
        cq = [part.astype(F32) for part in _split3(cum)]
        ck = [part.astype(F32) for part in _split3(-cum)]

        def proj(chunk):
            cols = pl.ds(chunk * COL_CHUNK, COL_CHUNK)
            return _dot_nt(ub, w_ref[cols, :]) + b_ref[:, cols]

        def head_tiles(r):
            for pair in range(N_HEADS // 2):
                both = r[:, pair * LANE:(pair + 1) * LANE]
                yield 2 * pair, both
                yield 2 * pair + 1, pltpu.roll(both, HEAD_DIM, 1)

        for h, val in head_tiles(proj(0)):
            extra = jnp.where((lane >= L_CK) & (lane < L_CK + 3), 1.0, 0.0)
            extra = _place3(lane, L_CQ, [part[:, h:h + 1] for part in cq], extra)
            qp_ref[h] = jnp.where(lane < HEAD_DIM, val * Q_SCALE, extra).astype(BF16)
        for h, val in head_tiles(proj(1)):
            ones = ((lane >= L_CQ) & (lane < L_CQ + 3)) | ((lane >= L_LSE) & (lane < L_LSE + 3))
            extra = _place3(lane, L_CK, [part[:, h:h + 1] for part in ck], jnp.where(ones, 1.0, 0.0))
            kp_ref[h] = jnp.where(lane < HEAD_DIM, val, extra).astype(BF16)
        for h, val in head_tiles(proj(2)):
            extra = jnp.where((lane >= HEAD_DIM) & (lane < HEAD_DIM + 3), -1.0, 0.0)
            vp_ref[h] = jnp.where(lane < HEAD_DIM, val, extra).astype(BF16)
        p_ref[...] = proj(3)
        ga_ref[...] = proj(4)
        gp_ref[...] = proj(5)

    head_block = pl.BlockSpec((N_HEADS, tile, LANE), lambda i: (0, i, 0))
    tok = lambda width: pl.BlockSpec((tile, width), lambda i: (i, 0))
    whole = lambda a: pl.BlockSpec(a.shape, lambda i: (0,) * a.ndim)
    padded = jax.ShapeDtypeStruct((N_HEADS, seq, LANE), BF16)
    half = jax.ShapeDtypeStruct((seq, D_ATT), F32)
    return pl.pallas_call(
        body, name="inproj_forward", grid=(nt,),
        in_specs=[tok(D), whole(mod), whole(w_main), whole(w_f), whole(b_main), whole(b_f)],
        out_specs=[head_block, head_block, head_block, tok(LANE), tok(D_POOL), tok(D_ATT), tok(D_POOL),
                   tok(D)],
        out_shape=[padded, padded, padded, jax.ShapeDtypeStruct((seq, LANE), F32), half, half, half,
                   jax.ShapeDtypeStruct((seq, D), BF16)],
        scratch_shapes=[pltpu.VMEM((8, LANE), F32)],
        compiler_params=_params(("arbitrary",)),
    )(x, mod, w_main, w_f, b_main, b_f)


def _attention_forward(qp, kp, vp, w_out, tile):
    seq = qp.shape[1]
    nb = seq // tile
    steps = (N_HEADS // 2) * nb

    def body(q_ref, k_ref, v_ref, wo_ref, att_ref, q2t_ref, wo_all_ref, s_a, s_b, m_ref, acc_ref,
             send_sems, recv_sems, local_sem):
        step = pl.program_id(0) * nb + pl.program_id(1)
        start, relay_near, relay_far, finish_near, finish_far = _gather_stages(
            wo_ref, wo_all_ref, send_sems, recv_sems, local_sem.at[0])
        pl.when(step == 0)(start)
        pl.when(step == steps // 4)(relay_near)
        pl.when(step == (3 * steps) // 4)(relay_far)

        i = pl.program_id(1)
        sub = lax.broadcasted_iota(jnp.int32, (LANE, tile), 0)
        row = lax.broadcasted_iota(jnp.int32, (tile, tile), 0)
        col = lax.broadcasted_iota(jnp.int32, (tile, tile), 1)
        q = [q_ref[0], q_ref[1]]

        def scores(buf, kb):
            rows = pl.ds(pl.multiple_of(kb * tile, tile), tile)
            for hh in range(2):
                buf[hh] = _dot_nt(k_ref[hh, rows, :], q[hh])

        def absorb(buf, kb, masked):
            rows = pl.ds(pl.multiple_of(kb * tile, tile), tile)
            for hh in range(2):
                m = m_ref[hh, 0:1, :]
                s = buf[hh]
                if masked:
                    s = jnp.where(row <= col, s, -1e30)
                m_new = jnp.maximum(m, jnp.max(s, axis=0, keepdims=True))
                p = jnp.exp(s - m_new).astype(BF16)
                acc_ref[hh] = jnp.exp(m - m_new) * acc_ref[hh] + _dot_tn(v_ref[hh, rows, :], p)
                m_ref[hh, 0:1, :] = m_new

        def two_blocks(j, _):
            scores(s_b, 2 * j + 1)
            absorb(s_a, 2 * j, False)
            scores(s_a, 2 * j + 2)
            absorb(s_b, 2 * j + 1, False)
            return 0

        def last_block():
            absorb(s_a, i, True)

        def last_two_blocks():
            scores(s_b, i)
            absorb(s_a, i - 1, False)
            absorb(s_b, i, True)

        scores(s_a, 0)
        m_ref[...] = jnp.full(m_ref.shape, -1e30, F32)
        acc_ref[...] = jnp.zeros_like(acc_ref)
        lax.fori_loop(0, i // 2, two_blocks, 0)
        lax.cond(i % 2 == 0, last_block, last_two_blocks)
        outs = []
        for hh in range(2):
            m, acc = m_ref[hh, 0:1, :], acc_ref[hh]
            l = -acc[HEAD_DIM:HEAD_DIM + 1, :]
            outs.append((acc / l)[:HEAD_DIM, :])
            neg_lse = [part.astype(F32) for part in _split3(-(m + jnp.log(l)))]
            q2t_ref[hh] = _place3(sub, L_LSE, neg_lse, q[hh].astype(F32).T).astype(BF16)
        att_ref[...] = jnp.concatenate(outs, axis=0).T
        @pl.when(step == steps - 1)
        def _():
            finish_near()
            finish_far()

    pair = pl.BlockSpec((2, tile, LANE), lambda hp, i: (hp, i, 0))
    full = pl.BlockSpec((2, seq, LANE), lambda hp, i: (hp, 0, 0))
    return pl.pallas_call(
        body, name="attention_forward", grid=(N_HEADS // 2, nb),
        in_specs=[pair, full, full, ANY],
        out_specs=[pl.BlockSpec((tile, LANE), lambda hp, i: (i, hp)),
                   pl.BlockSpec((2, LANE, tile), lambda hp, i: (hp, 0, i)), ANY],
        out_shape=[jax.ShapeDtypeStruct((seq, D_ATT), F32),
                   jax.ShapeDtypeStruct((N_HEADS, LANE, seq), BF16),
                   jax.ShapeDtypeStruct((N_DEV,) + w_out.shape, w_out.dtype)],
        scratch_shapes=[pltpu.VMEM((2, tile, tile), F32), pltpu.VMEM((2, tile, tile), F32),
                        pltpu.VMEM((2, 8, tile), F32), pltpu.VMEM((2, LANE, tile), F32),
                        pltpu.SemaphoreType.DMA((N_GATHER_SEMS,)), pltpu.SemaphoreType.DMA((N_GATHER_SEMS,)),
                        pltpu.SemaphoreType.DMA((1,))],
        compiler_params=_params(("arbitrary", "arbitrary")),
    )(qp, kp, vp, w_out)


def _window_sum(x, halo, window, transposed):
    tile = x.shape[0]

    def split_cat(a):
        hi = a.astype(BF16)
        return jnp.concatenate([hi, (a - hi.astype(F32)).astype(BF16)], axis=1)

    def fold(r):
        return r[:, :LANE] + r[:, LANE:]

    r = lax.broadcasted_iota(jnp.int32, (tile, tile), 0)
    c = lax.broadcasted_iota(jnp.int32, (tile, tile), 1)
    rh = lax.broadcasted_iota(jnp.int32, (HALO, HALO), 0)
    ch = lax.broadcasted_iota(jnp.int32, (HALO, HALO), 1)
    if not transposed:
        band = (c <= r) & (r - c < window)
        edge = (rh + HALO - ch) < window
    else:
        band = (r <= c) & (c - r < window)
        edge = (HALO + ch - rh) < window
    out = fold(_dot(band.astype(BF16), split_cat(x)))
    reach = fold(_dot(edge.astype(BF16), split_cat(halo)))
    if not transposed:
        return jnp.concatenate([out[:HALO] + reach, out[HALO:]], axis=0)
    return jnp.concatenate([out[:tile - HALO], out[tile - HALO:] + reach], axis=0)


def _silu_parts(g):
    sig = _sigmoid(g)
    return g * sig, sig * (1.0 + g * (1.0 - sig))


def _middle(x, tgt, att, g_att, g_pool, p, vecs, pool_vecs, w_out, w_pool, tile):
    seq = x.shape[0]
    nt = seq // tile
    halo_blocks = tile // HALO

    def body(x_ref, tgt_ref, att_ref, ga_ref, gp_ref, p_ref, ph_ref, vec_ref, pvec_ref, wo_ref, wp_ref,
             dxa_ref, do2_ref, dga_ref, dgp_ref, dpooled_ref, gwo_ref, dwp_ref, dvec_ref, dwo_ref, dpvec_ref):
        i = pl.program_id(0)

        @pl.when(i == 0)
        def _():
            dwo_ref[...] = jnp.zeros_like(dwo_ref)
            dwp_ref[...] = jnp.zeros_like(dwp_ref)
            dvec_ref[...] = jnp.zeros_like(dvec_ref)
            dpvec_ref[...] = jnp.zeros_like(dpvec_ref)

        gate, b_out, ln_g, ln_b = (vec_ref[k:k + 1, :] for k in range(4))
        b_pool, pool_scale = pvec_ref[0:1, :], pvec_ref[1:2, :]
        x = x_ref[...]
        p = p_ref[...]
        p_halo = ph_ref[...] * jnp.where(i > 0, 1.0, 0.0)
        pos = i * tile + lax.broadcasted_iota(jnp.int32, (tile, 1), 0) + 1

        pooled, mixed = [], []
        for g, window in enumerate(POOL_WINDOWS):
            cols = slice(g * GROUP_DIM, (g + 1) * GROUP_DIM)
            wsum = _window_sum(p[:, cols], p_halo[:, cols], window, False)
            count = jnp.minimum(pos, window).astype(F32)
            pooled.append(wsum / count - p[:, cols])
            mixed.append(_dot(pooled[g].astype(BF16), wp_ref[g]) + b_pool[:, cols])
        mixed = jnp.concatenate(mixed, axis=1)
        pool = mixed * pool_scale

        att = att_ref[...]
        g_att, g_pool = ga_ref[...], gp_ref[...]
        silu_a, dsilu_a = _silu_parts(g_att)
        silu_p, dsilu_p = _silu_parts(g_pool)
        y_in = jnp.concatenate([att * silu_a, pool * silu_p], axis=1)
        y = _dot(y_in.astype(BF16), wo_ref[...]) + b_out
        h = ALPHA * x + gate * y
        mu = jnp.mean(h, axis=1, keepdims=True)
        hc = h - mu
        var = jnp.mean(hc * hc, axis=1, keepdims=True)
        rstd = lax.rsqrt(var + LN_EPS)
        yhat = hc * rstd
        diff = yhat * ln_g + ln_b - tgt_ref[...]
        loss_rows = jnp.sum(diff * diff, axis=1, keepdims=True)
        d_out = diff * (1.0 / D)

        d_yhat = d_out * ln_g
        dh = rstd * (d_yhat - jnp.mean(d_yhat, axis=1, keepdims=True)
                     - yhat * jnp.mean(d_yhat * yhat, axis=1, keepdims=True))
        dxa_ref[...] = ALPHA * dh
        dy = dh * gate
        dyb = dy.astype(BF16)
        lane = _lanes((1, D))
        loss_row = jnp.where(lane == 0, (0.5 / D) * jnp.sum(loss_rows, axis=0, keepdims=True), 0.0)
        dvec_ref[5:6, :] += jnp.sum(dh * y, axis=0, keepdims=True)
        dvec_ref[0:1, :] += jnp.sum(dy, axis=0, keepdims=True)
        dvec_ref[1:2, :] += jnp.sum(d_out * yhat, axis=0, keepdims=True)
        dvec_ref[2:3, :] += jnp.sum(d_out, axis=0, keepdims=True)
        dvec_ref[4:5, :] += loss_row

        dwo_ref[...] += _dot(y_in.T.astype(BF16), dyb)
        d_yin = _dot_nt(dyb, wo_ref[...])
        d_a, d_pl = d_yin[:, :D_ATT], d_yin[:, D_ATT:]
        d_att = d_a * silu_a
        d_att_t = d_att.T
        prod_t = (d_att * att).T
        sub = lax.broadcasted_iota(jnp.int32, (HEAD_DIM, tile), 0)
        for h in range(N_HEADS):
            rows = slice(h * HEAD_DIM, (h + 1) * HEAD_DIM)
            delta = jnp.sum(prod_t[rows], axis=0, keepdims=True)
            extra = _place3(sub, 0, [part.astype(F32) for part in _split3(delta)], 0.0)
            do2_ref[h] = jnp.concatenate([d_att_t[rows], extra], axis=0).astype(BF16)
        dga_ref[...] = d_a * att * dsilu_a
        dgp_ref[...] = d_pl * pool * dsilu_p
        d_pool = d_pl * silu_p
        d_mixed = d_pool * pool_scale
        dpvec_ref[0:1, :] += jnp.sum(d_mixed, axis=0, keepdims=True)
        dpvec_ref[1:2, :] += jnp.sum(d_pool * mixed, axis=0, keepdims=True)
        d_pooled = []
        for g in range(len(POOL_WINDOWS)):
            cols = slice(g * GROUP_DIM, (g + 1) * GROUP_DIM)
            dmb = d_mixed[:, cols].astype(BF16)
            dwp_ref[g] += _dot(pooled[g].T.astype(BF16), dmb)
            d_pooled.append(_dot_nt(dmb, wp_ref[g]))
        dpooled_ref[...] = jnp.concatenate(d_pooled, axis=1)

        @pl.when(i == nt - 1)
        def _():
            gwo_ref[...] = dwo_ref[...].astype(BF16)
            dvec_ref[3:4, :] = jnp.concatenate([dpvec_ref[0:1, :], dpvec_ref[1:2, :]], axis=1)

    tok = lambda width: pl.BlockSpec((tile, width), lambda i: (i, 0))
    whole = lambda a: pl.BlockSpec(a.shape, lambda i: (0,) * a.ndim)
    halo = pl.BlockSpec((HALO, D_POOL), lambda i: (jnp.maximum(i * halo_blocks - 1, 0), 0))
    half = jax.ShapeDtypeStruct((seq, D_ATT), F32)
    outs = [jax.ShapeDtypeStruct((seq, D), F32), jax.ShapeDtypeStruct((N_HEADS, LANE, seq), BF16), half, half, half,
            jax.ShapeDtypeStruct(w_out.shape, BF16), jax.ShapeDtypeStruct(w_pool.shape, F32),
            jax.ShapeDtypeStruct(vecs.shape, F32)]
    return pl.pallas_call(
        body, name="middle", grid=(nt,),
        in_specs=[tok(D), tok(D), tok(D_ATT), tok(D_ATT), tok(D_POOL), tok(D_POOL), halo,
                  whole(vecs), whole(pool_vecs), whole(w_out), whole(w_pool)],
        out_specs=[tok(D), pl.BlockSpec((N_HEADS, LANE, tile), lambda i: (0, 0, i)),
                   tok(D_ATT), tok(D_POOL), tok(D_POOL),
                   whole(w_out), whole(w_pool), whole(vecs)],
        out_shape=outs,
        scratch_shapes=[pltpu.VMEM(w_out.shape, F32), pltpu.VMEM(pool_vecs.shape, F32)],
        compiler_params=_params(("arbitrary",)),
    )(x, tgt, att, g_att, g_pool, p, p, vecs, pool_vecs, w_out, w_pool)


def _attention_backward(q2t, kp, vp, do2t, gw_out, vecs, pool, tile):
    seq = kp.shape[1]
    nb = seq // tile
    last = N_HEADS // 2 - 1

    def body(qt_ref, k_ref, v_ref, dot_ref, gwo_hbm, vecs_hbm, pool_hbm,
             dq_ref, dk_ref, dv_ref, dcum_ref, g_out_ref, vecs_sum_ref, pool_sum_ref,
             dq_acc, dk_acc, dv_acc, gwo_ref, vecs_ref, pool_ref,
             r1, s2, r2, v_sib, v_chip, v_recv, p_sib, p_chip, p_recv, send_sems, recv_sems):
        hp = pl.program_id(0)
        start, middle, fold, finish = _reduce_stages(
            [gwo_ref], [g_out_ref], [r1], [s2], [r2],
            [(vecs_ref, vecs_sum_ref, v_sib, v_chip, v_recv), (pool_ref, pool_sum_ref, p_sib, p_chip, p_recv)],
            send_sems, recv_sems)

        @pl.when(hp == 0)
        def _():
            pltpu.sync_copy(gwo_hbm, gwo_ref)
            pltpu.sync_copy(vecs_hbm, vecs_ref)
            pltpu.sync_copy(pool_hbm, pool_ref)
            start()

        pl.when(hp == 1)(middle)
        pl.when(hp == 2)(fold)

        row = lax.broadcasted_iota(jnp.int32, (tile, tile), 0)
        col = lax.broadcasted_iota(jnp.int32, (tile, tile), 1)
        dq_acc[...] = jnp.zeros_like(dq_acc)

        def kv_block(kb, _):
            krows = pl.ds(pl.multiple_of(kb * tile, tile), tile)
            k = [k_ref[hh, krows, :] for hh in range(2)]
            v = [v_ref[hh, krows, :] for hh in range(2)]
            k_t = [k[hh].T for hh in range(2)]

            def q_block(qb, masked):
                qcols = pl.ds(pl.multiple_of(qb * tile, tile), tile)
                for hh in range(2):
                    q_t = qt_ref[hh, :, qcols]
                    do_t = dot_ref[hh, :, qcols]
                    s_t = _dot(k[hh], q_t)
                    if masked:
                        s_t = jnp.where(row <= col, s_t, -1e30)
                    p_t = jnp.exp(s_t)
                    ds_t = (p_t * _dot(v[hh], do_t)).astype(BF16)
                    dv_new = _dot_nt(do_t, p_t.astype(BF16))
                    dk_new = _dot_nt(q_t, ds_t)
                    if masked:
                        dv_acc[hh], dk_acc[hh] = dv_new, dk_new
                    else:
                        dv_acc[hh] += dv_new
                        dk_acc[hh] += dk_new
                    dq_acc[hh, :, qcols] += _dot(k_t[hh], ds_t)

            q_block(kb, True)

            def two_later_blocks(j, _):
                q_block(kb + 1 + 2 * j, False)
                q_block(kb + 2 + 2 * j, False)
                return 0

            later = nb - 1 - kb
            lax.fori_loop(0, later // 2, two_later_blocks, 0)
            pl.when(later % 2 == 1)(lambda: q_block(nb - 1, False))
            for hh in range(2):
                dk = dk_acc[hh]
                dk_ref[hh, :, krows] = dk.astype(BF16)
                dv_ref[hh, :, krows] = dv_acc[hh].astype(BF16)
                dcum_ref[hh, :, krows] = -dk[L_CK:L_CK + 1, :]
            return 0

        lax.fori_loop(0, nb, kv_block, 0)
        for hh in range(2):
            dq = dq_acc[hh]
            dcum_ref[hh] += dq[L_CQ:L_CQ + 1, :]
            dq_ref[hh] = (dq * Q_SCALE).astype(BF16)
        pl.when(hp == last)(finish)

    pair = pl.BlockSpec((2, seq, LANE), lambda hp: (hp, 0, 0))
    pair_t = pl.BlockSpec((2, LANE, seq), lambda hp: (hp, 0, 0))
    whole = lambda shape: pl.BlockSpec(shape, lambda hp: (0,) * len(shape))
    grad = jax.ShapeDtypeStruct((N_HEADS, LANE, seq), BF16)
    return pl.pallas_call(
        body, name="attention_backward", grid=(N_HEADS // 2,),
        in_specs=[pair_t, pair, pair, pair_t, ANY, ANY, ANY],
        out_specs=[pair_t, pair_t, pair_t, pl.BlockSpec((2, 1, seq), lambda hp: (hp, 0, 0)),
                   whole(gw_out.shape[1:]), whole(vecs.shape), whole(pool.shape)],
        out_shape=[grad, grad, grad, jax.ShapeDtypeStruct((N_HEADS, 1, seq), F32),
                   jax.ShapeDtypeStruct(gw_out.shape[1:], F32), jax.ShapeDtypeStruct(vecs.shape, F32),
                   jax.ShapeDtypeStruct(pool.shape, F32)],
        scratch_shapes=[pltpu.VMEM((2, LANE, seq), F32), pltpu.VMEM((2, LANE, tile), F32),
                        pltpu.VMEM((2, LANE, tile), F32), pltpu.VMEM(gw_out.shape, BF16),
                        pltpu.VMEM(vecs.shape, F32), pltpu.VMEM(pool.shape, F32)]
        + _reduce_scratch(gw_out, [vecs, pool]),
        compiler_params=_params(("arbitrary",)),
    )(q2t, kp, vp, do2t, gw_out, vecs, pool)


def _inproj_backward(dqp, dkp, dvp, d_cum, f, d_pooled, d_ga, d_gp, x, dxa, u, mod, w_main, w_f, tile):
    seq = x.shape[0]
    nt = seq // tile
    halo_blocks = tile // HALO

    def body(dq_ref, dk_ref, dv_ref, dcum_ref, f_ref, dpo_ref, dph_ref, dga_ref, dgp_ref, x_ref, dxa_ref, u_ref,
             mod_ref, w_ref, wf_ref,
             dx_ref, dproj_ref, dwf_ref, db_ref, dbf_ref, dmod_ref, carry_ref):
        step = pl.program_id(0)
        i = nt - 1 - step

        @pl.when(step == 0)
        def _():
            carry_ref[...] = jnp.zeros_like(carry_ref)
            dwf_ref[...] = jnp.zeros_like(dwf_ref)
            db_ref[...] = jnp.zeros_like(db_ref)
            dbf_ref[...] = jnp.zeros_like(dbf_ref)
            dmod_ref[...] = jnp.zeros_like(dmod_ref)

        ones = jnp.ones((8, tile), BF16)

        def emit(chunk, val):
            cols = pl.ds(chunk * COL_CHUNK, COL_CHUNK)
            db_ref[0:1, cols] += jnp.sum(val, axis=0, keepdims=True)
            vb = val.astype(BF16)
            dproj_ref[:, pl.ds((chunk - 3) * COL_CHUNK, COL_CHUNK)] = vb
            return _dot(vb, w_ref[cols, :])

        d_u = jnp.zeros((tile, D), F32)
        for chunk, ref in enumerate((dq_ref, dk_ref, dv_ref)):
            cols = pl.ds(chunk * COL_CHUNK, COL_CHUNK)
            val_t = ref[:, 0:HEAD_DIM, :].reshape(COL_CHUNK, tile)
            db_ref[:, cols] += _dot_nt(ones, val_t)
            d_u += _dot_tn(val_t, w_ref[cols, :])

        d_pooled = dpo_ref[...]
        d_halo = dph_ref[...] * jnp.where(i < nt - 1, 1.0, 0.0)
        pos = i * tile + lax.broadcasted_iota(jnp.int32, (tile, 1), 0) + 1
        d_p = []
        for g, window in enumerate(POOL_WINDOWS):
            cols = slice(g * GROUP_DIM, (g + 1) * GROUP_DIM)
            scaled = d_pooled[:, cols] / jnp.minimum(pos, window).astype(F32)
            d_p.append(_window_sum(scaled, d_halo[:, cols] * (1.0 / window), window, True) - d_pooled[:, cols])
        d_u += emit(3, jnp.concatenate(d_p, axis=1))
        d_u += emit(4, dga_ref[...])
        d_u += emit(5, dgp_ref[...])

        row = lax.broadcasted_iota(jnp.int32, (tile, tile), 0)
        col = lax.broadcasted_iota(jnp.int32, (tile, tile), 1)
        later = (row >= col).astype(BF16)
        d_logf = sum(_dot(part, later) for part in _split3(dcum_ref[:, 0, :])) + carry_ref[:, 0:1]
        carry_ref[:, 0:1] = d_logf[:, 0:1]
        d_f = d_logf * _sigmoid(-f_ref[...].T[0:N_HEADS, :])
        d_f = jnp.concatenate([d_f, jnp.zeros((LANE - N_HEADS, tile), F32)], axis=0)
        dbf_ref[...] += sum(_dot_nt(ones, part) for part in _split3(d_f))
        d_fb = d_f.astype(BF16)
        d_u += _dot_tn(d_fb, wf_ref[...])
        dwf_ref[...] += _dot(d_fb, u_ref[...])

        x = x_ref[...]
        dx_ref[...] = dxa_ref[...] + d_u * mod_ref[0:1, :]
        dmod_ref[0:1, :] += jnp.sum(d_u * x, axis=0, keepdims=True)
        dmod_ref[1:2, :] += jnp.sum(d_u, axis=0, keepdims=True)

    rev = lambda step: nt - 1 - step
    tok = lambda width: pl.BlockSpec((tile, width), lambda s: (rev(s), 0))
    head_block = pl.BlockSpec((N_HEADS, LANE, tile), lambda s: (0, 0, rev(s)))
    whole = lambda a: pl.BlockSpec(a.shape, lambda s: (0,) * a.ndim)
    halo = pl.BlockSpec((HALO, D_POOL), lambda s: (jnp.minimum((rev(s) + 1) * halo_blocks, seq // HALO - 1), 0))
    small = lambda width: jax.ShapeDtypeStruct((8, width), F32)
    n_rest = N_MAIN - OFF_P
    return pl.pallas_call(
        body, name="inproj_backward", grid=(nt,),
        in_specs=[head_block, head_block, head_block, pl.BlockSpec((N_HEADS, 1, tile), lambda s: (0, 0, rev(s))),
                  tok(LANE), tok(D_POOL), halo, tok(D_ATT), tok(D_POOL),
                  tok(D), tok(D), tok(D),
                  whole(mod), whole(w_main), whole(w_f)],
        out_specs=[tok(D), tok(n_rest), pl.BlockSpec((LANE, D), lambda s: (0, 0)),
                   pl.BlockSpec((8, N_MAIN), lambda s: (0, 0)), pl.BlockSpec((8, LANE), lambda s: (0, 0)),
                   pl.BlockSpec((8, D), lambda s: (0, 0))],
        out_shape=[jax.ShapeDtypeStruct((seq, D), F32), jax.ShapeDtypeStruct((seq, n_rest), BF16),
                   jax.ShapeDtypeStruct((LANE, D), F32), small(N_MAIN), small(LANE), small(D)],
        scratch_shapes=[pltpu.VMEM((8, LANE), F32)],
        compiler_params=_params(("arbitrary",)),
    )(dqp, dkp, dvp, d_cum, f, d_pooled, d_pooled, d_ga, d_gp, x, dxa, u, mod, w_main, w_f)


def _weight_grads(dq_t, dk_t, dv_t, dw_f, dproj, u, k_tile, small, rows):
    seq = u.shape[0]
    nk = seq // k_tile
    head_rows = N_HEADS * HEAD_DIM
    width = D // 2
    shard = D_IN // N_DEV
    assert nk >= 4

    def body(dq_ref, dk_ref, dv_ref, dwf_ref, dp_ref, u_ref, small_ref, rows_ref,
             left_sum_ref, right_sum_ref, total_ref, rows_all_ref, acc_ref, left_ref, right_ref,
             r1, s2, r2, send_sems, recv_sems,
             own_r1, own_s2, own_r2, sm_sib, sm_chip, sm_recv, rows_land, own_send_sems, own_recv_sems):
        half, k = pl.program_id(0), pl.program_id(1)
        start, middle, fold, finish = _reduce_stages(
            [left_ref], [left_sum_ref], [r1], [s2], [r2], [], send_sems, recv_sems)
        own_stages = _reduce_stages(
            [right_ref], [right_sum_ref], [own_r1], [own_s2], [own_r2],
            [(small_ref, total_ref, sm_sib, sm_chip, sm_recv)], own_send_sems, own_recv_sems,
            rows=(rows_ref, rows_land, rows_all_ref))
        on_right = lambda step: jnp.logical_and(half == 1, k == step)
        pl.when(on_right(0))(start)
        pl.when(on_right(1))(middle)
        pl.when(on_right(nk // 2))(fold)

        @pl.when(k == 0)
        def _():
            acc_ref[...] = jnp.zeros_like(acc_ref)

        tokens = u_ref[...]
        for j, ref in enumerate((dq_ref, dk_ref, dv_ref)):
            acc_ref[pl.ds(j * head_rows, head_rows), :] += _dot(ref[...].reshape(head_rows, k_tile), tokens)
        for j in range(dproj.shape[1] // COL_CHUNK):
            cols = pl.ds(j * COL_CHUNK, COL_CHUNK)
            acc_ref[pl.ds(F_HI + j * COL_CHUNK, COL_CHUNK), :] += _dot_tn(dp_ref[:, cols], tokens)

        def slots_to(out_ref):
            acc_ref[F_LO:F_HI, :] = dwf_ref[0:N_HEADS, :]
            for slot in range(N_DEV):
                out_ref[slot] = acc_ref[slot * shard:(slot + 1) * shard, :].astype(BF16)

        @pl.when(jnp.logical_and(half == 0, k == nk - 1))
        def _():
            slots_to(left_ref)

        @pl.when(on_right(nk - 1))
        def _():
            slots_to(right_ref)
            own_stages[0]()
            finish()
            for stage in own_stages[1:]:
                stage()

    heads = pl.BlockSpec((N_HEADS, HEAD_DIM, k_tile), lambda h, k: (0, 0, k))
    whole = lambda a: pl.BlockSpec(a.shape, lambda h, k: (0,) * a.ndim)
    slots = jax.ShapeDtypeStruct((N_DEV, shard, width), BF16)
    shard_sum = jax.ShapeDtypeStruct((shard, width), F32)
    out_shape = [shard_sum, shard_sum, jax.ShapeDtypeStruct(small.shape, F32),
                 jax.ShapeDtypeStruct((N_DEV,) + rows.shape, F32)]
    return pl.pallas_call(
        body, name="weight_grads", grid=(2, nk),
        in_specs=[heads, heads, heads, pl.BlockSpec((dw_f.shape[0], width), lambda h, k: (0, h)),
                  pl.BlockSpec((k_tile, dproj.shape[1]), lambda h, k: (k, 0)),
                  pl.BlockSpec((k_tile, width), lambda h, k: (k, h)), whole(small), whole(rows)],
        out_specs=[whole(a) for a in out_shape], out_shape=out_shape,
        scratch_shapes=[pltpu.VMEM((D_IN, width), F32), pltpu.VMEM(slots.shape, BF16), pltpu.VMEM(slots.shape, BF16)]
        + _reduce_scratch(slots, []) + _reduce_scratch(slots, [small], rows),
        compiler_params=_params(("arbitrary", "arbitrary")),
    )(dq_t, dk_t, dv_t, dw_f, dproj, u, small, rows)


def _adamw(w, g, m, v):
    m = ADAM_B1 * m + (1.0 - ADAM_B1) * g
    v = ADAM_B2 * v + (1.0 - ADAM_B2) * (g * g)
    m_hat = m / (1.0 - ADAM_B1 ** ADAM_STEP)
    v_hat = v / (1.0 - ADAM_B2 ** ADAM_STEP)
    delta = -ADAM_LR * (m_hat / (jnp.sqrt(v_hat) + ADAM_EPS) + ADAM_WD * w)
    return delta, m, v


SUBLANES = 8


def _adamw_packed(g_parts, w, m, v, name, chunks=4):
    n_parts, (rows, part_cols) = len(g_parts), g_parts[0].shape
    cols = n_parts * part_cols
    per_row, per_part = cols // LANE, part_cols // LANE
    assert part_cols % LANE == 0 and per_row == SUBLANES and w.shape == (rows * per_row, LANE)
    step = -(-rows // (chunks * SUBLANES)) * SUBLANES
    bounds = [(r0, min(r0 + step, rows)) for r0 in range(0, rows, step)]

    def body(*refs):
        g_hbm, (w_hbm, m_hbm, v_hbm, og_hbm, od_hbm, om_hbm, ov_hbm) = refs[:n_parts], refs[n_parts:n_parts + 7]
        g_buf, in_buf, out_buf, in_sems, out_sems = refs[n_parts + 7:]

        def copies_in(c):
            r0, r1 = bounds[c]
            packed = slice(r0 * per_row, r1 * per_row)
            return [pltpu.make_async_copy(g_hbm[p].at[r0:r1], g_buf.at[p, r0:r1], in_sems.at[c, 3 + p])
                    for p in range(n_parts)] + [
                pltpu.make_async_copy(src.at[packed], in_buf.at[i, packed], in_sems.at[c, i])
                for i, src in enumerate((w_hbm, m_hbm, v_hbm))]

        def copies_out(c):
            r0, r1 = bounds[c]
            packed = slice(r0 * per_row, r1 * per_row)
            return [pltpu.make_async_copy(out_buf.at[i, packed], dst.at[packed], out_sems.at[c, i])
                    for i, dst in enumerate((og_hbm, od_hbm, om_hbm, ov_hbm))]

        for c in range(len(bounds)):
            for cp in copies_in(c):
                cp.start()
        for c, (r0, r1) in enumerate(bounds):
            for cp in copies_in(c):
                cp.wait()
            for j in range(per_row):
                lanes = pl.ds(r0 * per_row + j, r1 - r0, stride=per_row)
                g_part = g_buf[j // per_part, r0:r1, (j % per_part) * LANE:(j % per_part + 1) * LANE]
                results = _adamw(in_buf[0, lanes, :], g_part, in_buf[1, lanes, :], in_buf[2, lanes, :])
                for i, val in enumerate((g_part,) + results):
                    out_buf[i, lanes, :] = val
            for cp in copies_out(c):
                cp.start(priority=WRITE_PRIORITY)
        for c in range(len(bounds)):
            for cp in copies_out(c):
                cp.wait()

    shape = jax.ShapeDtypeStruct(w.shape, F32)
    return pl.pallas_call(
        body, name=name,
        in_specs=[ANY] * (n_parts + 3), out_specs=[ANY] * 4, out_shape=[shape] * 4,
        scratch_shapes=[pltpu.VMEM((n_parts, rows, part_cols), F32), pltpu.VMEM((3,) + w.shape, F32),
                        pltpu.VMEM((4,) + w.shape, F32),
                        pltpu.SemaphoreType.DMA((len(bounds), 3 + n_parts)), pltpu.SemaphoreType.DMA((len(bounds), 4))],
        compiler_params=_params(),
    )(*g_parts, w, m, v)


def _ada_adamw(sc_all, d_ada, w, m, v, chunks=4):
    rows, cols = w.shape
    step, sub = rows // chunks, 32
    assert rows % chunks == 0 and step % LANE == 0 and step % sub == 0

    def body(sc_ref, d_ref, w_hbm, m_hbm, v_hbm, og_hbm, od_hbm, om_hbm, ov_hbm, in_buf, out_buf, in_sems, out_sems):
        def copies_in(c):
            part = slice(c * step, (c + 1) * step)
            return [pltpu.make_async_copy(src.at[part], in_buf.at[i, part], in_sems.at[c, i])
                    for i, src in enumerate((w_hbm, m_hbm, v_hbm))]

        def copies_out(c):
            part = slice(c * step, (c + 1) * step)
            return [pltpu.make_async_copy(out_buf.at[i, part], dst.at[part], out_sems.at[c, i])
                    for i, dst in enumerate((og_hbm, od_hbm, om_hbm, ov_hbm))]

        for c in range(chunks):
            for cp in copies_in(c):
                cp.start()
        for c in range(chunks):
            sc_t = sc_ref[:, c * step:(c + 1) * step].T
            for cp in copies_in(c):
                cp.wait()
            for r0 in range(0, step, sub):
                part = slice(c * step + r0, c * step + r0 + sub)
                g = sc_t[r0:r0 + sub, 0:1] * d_ref[0:1, :]
                for b in range(1, N_DEV):
                    g = g + sc_t[r0:r0 + sub, b:b + 1] * d_ref[b:b + 1, :]
                results = _adamw(in_buf[0, part, :], g, in_buf[1, part, :], in_buf[2, part, :])
                for i, val in enumerate((g,) + results):
                    out_buf[i, part, :] = val
            for cp in copies_out(c):
                cp.start(priority=WRITE_PRIORITY)
        for c in range(chunks):
            for cp in copies_out(c):
                cp.wait()

    in_vmem = pl.BlockSpec(memory_space=pltpu.VMEM)
    shape = jax.ShapeDtypeStruct(w.shape, F32)
    return pl.pallas_call(
        body, name="ada_adamw",
        in_specs=[in_vmem, in_vmem, ANY, ANY, ANY], out_specs=[ANY] * 4, out_shape=[shape] * 4,
        scratch_shapes=[pltpu.VMEM((3,) + w.shape, F32), pltpu.VMEM((4,) + w.shape, F32),
                        pltpu.SemaphoreType.DMA((chunks, 3)), pltpu.SemaphoreType.DMA((chunks, 4))],
        compiler_params=_params(),
    )(sc_all, d_ada, w, m, v)


F_LO, F_HI = 3 * D_ATT, 3 * D_ATT + N_HEADS


def _split_forget(a, axis):
    idx = lambda lo, hi: tuple(slice(lo, hi) if d == axis else slice(None) for d in range(a.ndim))
    pad = [(0, LANE - N_HEADS) if d == axis else (0, 0) for d in range(a.ndim)]
    return jnp.concatenate([a[idx(0, F_LO)], a[idx(F_HI, D_IN)]], axis=axis), jnp.pad(a[idx(F_LO, F_HI)], pad)


def _join_forget(main, f, axis):
    idx = lambda lo, hi: tuple(slice(lo, hi) if d == axis else slice(None) for d in range(main.ndim))
    return jnp.concatenate([main[idx(0, F_LO)], f[idx(0, N_HEADS)], main[idx(F_LO, N_MAIN)]], axis=axis)


def _adamw_small(grad_rows, row_params, whole_params, summed_params, scalar_at):
    n_row, n_whole, n_sum = len(row_params), len(whole_params), len(summed_params)
    n = n_row + n_whole + n_sum

    def body(g_ref, *refs):
        n_in = 3 * n_row + 4 * (n_whole + n_sum)
        ins, outs = list(refs[:n_in]), refs[n_in:]
        for i in range(n):
            if i < n_row:
                row, lo, hi = row_params[i][:3]
                g = g_ref[row:row + 1, lo:hi]
            elif i < n_row + n_whole:
                g = ins.pop(0)[...]
            else:
                parts = ins.pop(0)
                g = parts[0]
                for k in range(1, N_DEV):
                    g = g + parts[k]
            w, m, v = (ins.pop(0)[...] for _ in range(3))
            outs[4 * i][...] = g
            outs[4 * i + 1][...], outs[4 * i + 2][...], outs[4 * i + 3][...] = _adamw(w, g, m, v)
        row, lane = scalar_at
        outs[4 * n][...] = g_ref[row:row + 1, lane:lane + 1]

    shapes = [p[3] for p in row_params] + [p[1] for p in whole_params] + [p[1] for p in summed_params]
    operands = [a for p in row_params for a in p[3:]] + [a for p in whole_params + summed_params for a in p]
    flat = pl.pallas_call(
        body, name="adamw_small",
        out_shape=[jax.ShapeDtypeStruct(w.shape, F32) for w in shapes for _ in range(4)]
        + [jax.ShapeDtypeStruct((1, 1), F32)],
        compiler_params=_params(),
    )(grad_rows, *operands)
    return [flat[4 * i:4 * i + 4] for i in range(n)], flat[4 * n].reshape(())


def kernel(x, c, w_ada, b_ada, w_in, b_in, w_pool_mix, b_pool_mix, pool_scale, w_out, b_out, ln_g, ln_b, loss_target, m_w_ada, m_b_ada, m_w_in, m_b_in, m_w_pool_mix, m_b_pool_mix, m_pool_scale, m_w_out, m_b_out, m_ln_g, m_ln_b, v_w_ada, v_b_ada, v_w_in, v_b_in, v_w_pool_mix, v_b_pool_mix, v_pool_scale, v_w_out, v_b_out, v_ln_g, v_ln_b):
    seq = x.shape[1]
    tile = min(256, seq)
    attn_tile = min(512, max(128, seq // 4))
    me = _dev_index(*_mesh_pos())
    x2, tgt = x[0], loss_target[0]

    rows_of = lambda a: jnp.swapaxes(a, 1, 2)[0]
    w_main, w_f, sc_all, ada_mine = _gather_and_ada(c, rows_of(w_in).astype(BF16), w_ada[0])
    ada = ada_mine.reshape(1, D_ADA) + b_ada
    shift, scale, gate = ada[:, 0:D], ada[:, D:2 * D], ada[:, 2 * D:]
    mod = jnp.concatenate([1.0 + scale, shift, jnp.zeros((6, D), F32)], axis=0)
    b_main, b_f = _split_forget(b_in, 1)

    qp, kp, vp, f, p, g_att, g_pool, u = _inproj_forward(x2, mod, w_main, w_f, b_main, b_f, tile)
    att, q2t, w_out_g = _attention_forward(qp, kp, vp, w_out[0].astype(BF16), attn_tile)

    vecs = jnp.concatenate([gate, b_out, ln_g, ln_b, jnp.zeros((4, D), F32)], axis=0)
    pool_vecs = jnp.concatenate([b_pool_mix.reshape(1, D_POOL), pool_scale, jnp.zeros((6, D_POOL), F32)], axis=0)
    dxa, do2, d_ga, d_gp, d_pooled, gw_out, dw_pool, dvec = _middle(
        x2, tgt, att, g_att, g_pool, p, vecs, pool_vecs, w_out_g.reshape(D, D), w_pool_mix[0].astype(BF16), tile)

    pool_rows = w_pool_mix.shape[1] * GROUP_DIM
    dqp, dkp, dvp, d_cum, g_out, dvec_sum, dw_pool_sum = _attention_backward(
        q2t, kp, vp, do2, gw_out.reshape(N_DEV, D // N_DEV, D), dvec, dw_pool.reshape(pool_rows, GROUP_DIM), attn_tile)
    dx, dproj, dw_f, db_main, db_f, dmod = _inproj_backward(
        dqp, dkp, dvp, d_cum, f, d_pooled, d_ga, d_gp, x2, dxa, u, mod, w_main, w_f, tile)
    d_ada = jnp.concatenate([dmod[1:2], dmod[0:1], dvec[5:6]], axis=1)
    g_in_left, g_in_right, g_b_in, d_ada_all = _weight_grads(
        dqp, dkp, dvp, dw_f, dproj, u, min(512, seq), _join_forget(db_main[0:1], db_f[0:1], 1), d_ada)

    packed = lambda a: jnp.transpose(a.reshape(SUBLANES, LANE, -1), (2, 0, 1)).reshape(-1, LANE)
    outs_in = _adamw_packed((g_in_left, g_in_right), packed(w_in), packed(m_w_in), packed(v_w_in), "adamw_w_in")
    g_w_in, d_w_in, nm_w_in, nv_w_in = (
        jnp.transpose(a.reshape(-1, SUBLANES, LANE), (1, 2, 0)).reshape(D, -1) for a in outs_in)
    flat_pool = lambda a: a.reshape(1, D_POOL)
    pool_2d = lambda a: a.reshape(pool_rows, GROUP_DIM)
    rows, loss = _adamw_small(
        dvec_sum,
        [(0, 0, D, b_out, m_b_out, v_b_out), (1, 0, D, ln_g, m_ln_g, v_ln_g), (2, 0, D, ln_b, m_ln_b, v_ln_b),
         (3, 0, D_POOL, flat_pool(b_pool_mix), flat_pool(m_b_pool_mix), flat_pool(v_b_pool_mix)),
         (3, D_POOL, 2 * D_POOL, pool_scale, m_pool_scale, v_pool_scale)],
        [(g_out, w_out[0], m_w_out[0], v_w_out[0]),
         (dw_pool_sum, pool_2d(w_pool_mix), pool_2d(m_w_pool_mix), pool_2d(v_w_pool_mix)),
         (g_b_in, b_in, m_b_in, v_b_in)],
        [(d_ada_all, b_ada, m_b_ada, v_b_ada)],
        scalar_at=(4, 0))
    small = {"b_out": rows[0], "ln_g": rows[1], "ln_b": rows[2],
             "b_pool": [a.reshape(b_pool_mix.shape) for a in rows[3]], "pool_scale": rows[4],
             "w_pool": [a.reshape(w_pool_mix.shape) for a in rows[6]], "b_in": rows[7]}
    g_s, d_s, nm_s, nv_s = ({k: r[j] for k, r in small.items()} for j in range(4))
    g_w_out, d_w_out, nm_w_out, nv_w_out = rows[5]
    g_b_ada, d_b_ada, nm_b_ada, nv_b_ada = rows[8]

    d_ada_local = lax.dynamic_slice_in_dim(d_ada_all.reshape(N_DEV, D_ADA), me * (D_ADA // N_DEV), D_ADA // N_DEV, axis=1)
    g_w_ada, d_w_ada, nm_w_ada, nv_w_ada = _ada_adamw(sc_all, d_ada_local, w_ada[0], m_w_ada[0], v_w_ada[0])

    def ordered(w_ada_, b_ada_, w_in_, w_out_, s):
        return (w_ada_[None], b_ada_, w_in_[None], s["b_in"], s["w_pool"], s["b_pool"], s["pool_scale"],
                w_out_[None], s["b_out"], s["ln_g"], s["ln_b"])

    return (loss, dx[None],
            *ordered(g_w_ada, g_b_ada, g_w_in, g_w_out, g_s),
            *ordered(d_w_ada, d_b_ada, d_w_in, d_w_out, d_s),
            *ordered(nm_w_ada, nm_b_ada, nm_w_in, nm_w_out, nm_s),
            *ordered(nv_w_ada, nv_b_ada, nv_w_in, nv_w_out, nv_s))
```

```python
import jax
import jax.numpy as jnp
from jax import lax
from jax.experimental import pallas as pl
from jax.experimental.pallas import tpu as pltpu

F32 = jnp.float32
BF16 = jnp.bfloat16

N_DEV = 8
D = 1024
N_HEADS = 8
HEAD_DIM = 64
D_ATT = 512
D_POOL = 512
POOL_WINDOWS = (2, 4, 8, 16)
GROUP_DIM = 128
HALO = 16
LANE = 128
BF16_TILE_ROWS = 16
D_IN = 3080
D_ADA = 3072
N_MAIN = 3072
OFF_P = 1536
COL_CHUNK = 512
Q_SCALE = 0.125
LN_EPS = 1e-5
ALPHA = 2.0 ** 0.25
L_CQ, L_CK, L_LSE = 64, 67, 70

ADAM_LR, ADAM_B1, ADAM_B2, ADAM_EPS, ADAM_WD, ADAM_STEP = 0.001, 0.9, 0.999, 1e-08, 0.01, 10
VMEM_LIMIT = 56 * 1024 * 1024

MESH = pl.DeviceIdType.MESH
ANY = pl.BlockSpec(memory_space=pl.ANY)


def _params(sem=None, vmem=VMEM_LIMIT):
    return pltpu.CompilerParams(dimension_semantics=sem, vmem_limit_bytes=vmem)


def _split3(a):
    hi = a.astype(BF16)
    r = a - hi.astype(F32)
    mid = r.astype(BF16)
    lo = (r - mid.astype(F32)).astype(BF16)
    return hi, mid, lo


def _dot(a, b):
    return jnp.dot(a, b, preferred_element_type=F32)


def _dot_nt(a, b):
    return lax.dot_general(a, b, (((1,), (1,)), ((), ())), preferred_element_type=F32)


def _dot_tn(a, b):
    return lax.dot_general(a, b, (((0,), (0,)), ((), ())), preferred_element_type=F32)


def _dot3(m01, a):
    hi, mid, lo = _split3(a)
    return _dot(m01, hi) + _dot(m01, mid) + _dot(m01, lo)


def _sigmoid(z):
    return 1.0 / (1.0 + jnp.exp(-z))


def _lanes(shape):
    return lax.broadcasted_iota(jnp.int32, shape, len(shape) - 1)


def _place3(lane, base, parts, other):
    out = other
    for j in range(3):
        out = jnp.where(lane == base + j, parts[j], out)
    return out


def _mesh_pos():
    return lax.axis_index("x"), lax.axis_index("y"), lax.axis_index("c")


def _dev_index(px, py, pc):
    return 4 * px + 2 * py + pc


N_GATHER_SEMS = 11


def _gather_stages(src_ref, out_ref, send_sems, recv_sems, local_sem):
    x, y, c = _mesh_pos()
    me, sibling = (x, y, c), (x, y, 1 - c)
    nbr_x, nbr_y, diag = (1 - x, y), (x, 1 - y), (1 - x, 1 - y)
    half = out_ref.shape[-1] // 2
    left, right = pl.ds(0, half), pl.ds(half, half)

    def copy(k, block, to, cols=None, src=None):
        slot = out_ref.at[_dev_index(*block)]
        if cols is not None:
            slot = slot.at[:, cols]
            src = src if src is None else src.at[:, cols]
        return pltpu.make_async_remote_copy(
            src_ref=slot if src is None else src, dst_ref=slot, send_sem=send_sems.at[k], recv_sem=recv_sems.at[k],
            device_id=to, device_id_type=MESH)

    mine = None if src_ref is None else pltpu.make_async_copy(src_ref, out_ref.at[_dev_index(*me)], local_sem)
    first = [copy(0, me, sibling, src=src_ref),
             copy(1, me, (*nbr_x, c), cols=left, src=src_ref), copy(2, me, (*nbr_y, c), cols=right, src=src_ref),
             copy(9, me, (*nbr_x, c), cols=right, src=src_ref), copy(10, me, (*nbr_y, c), cols=left, src=src_ref)]
    relay = [(1, nbr_x, left, nbr_x), (2, nbr_y, right, nbr_y), (3, diag, left, nbr_y), (4, diag, right, nbr_x)]
    other_half = [(9, nbr_x, right, nbr_x), (10, nbr_y, left, nbr_y)]
    onward = [copy(3, (*nbr_x, c), (*nbr_y, c), cols=left), copy(4, (*nbr_y, c), (*nbr_x, c), cols=right)]
    passed = [copy(4 + k, (*block, c), sibling, cols=None if k < 3 else cols) for k, block, cols, _ in relay]

    def start():
        if mine is not None:
            mine.start()
        for cp in first:
            cp.start()

    def arrived(item):
        k, block, cols, frm = item
        copy(k, (*block, c), (*frm, c), cols=cols).wait_recv()

    def relay_near():
        for j in (0, 1):
            arrived(relay[j])
            onward[j].start()
        for j in (0, 1):
            arrived(other_half[j])
            passed[j].start()

    def relay_far():
        for j in (2, 3):
            arrived(relay[j])
            passed[j].start()

    def from_sibling(items):
        for k, block, cols, _ in items:
            copy(4 + k, (*block, 1 - c), me, cols=None if k < 3 else cols).wait_recv()

    def finish_near():
        copy(0, sibling, me).wait_recv()
        from_sibling(relay[:2])
        if mine is not None:
            mine.wait()

    def finish_far():
        from_sibling(relay[2:])
        for cp in first + onward + passed:
            cp.wait_send()

    return start, relay_near, relay_far, finish_near, finish_far


N_REDUCE_SEMS = 10
N_SMALL_SEMS = 4
N_ROWS_SEMS = 7


def _reduce_stages(ins, gs, r1, s2, r2, smalls, send_sems, recv_sems, rows=None, own=None):
    n = len(ins)
    x, y, c = _mesh_pos()
    me = _dev_index(x, y, c)
    sibling = (x, y, 1 - c)
    chips = [(x, y), (1 - x, y), (x, 1 - y), (1 - x, 1 - y)]
    peers = []
    for p in range(1, N_DEV):
        px, py, pc = (p >> 2) & 1, (p >> 1) & 1, p & 1
        peers.append((1 - x if px else x, 1 - y if py else y, 1 - c if pc else c))
    base_small = N_REDUCE_SEMS * n

    def remote(src, dst, k, to):
        return pltpu.make_async_remote_copy(src_ref=src, dst_ref=dst, send_sem=send_sems.at[k],
                                            recv_sem=recv_sems.at[k], device_id=to, device_id_type=MESH)

    def level1(a, q):
        return remote(ins[a].at[_dev_index(*chips[q], 1 - c)], r1[a].at[q], N_REDUCE_SEMS * a + q, sibling)

    def level2(a, k):
        half = ins[a].shape[-1] // 2
        left, right = pl.ds(0, half), pl.ds(half, half)
        nbr_x, nbr_y = (*chips[1], c), (*chips[2], c)
        src_slot, dst_slot, cols, to = [(0, 0, left, nbr_x), (1, 1, right, nbr_y), (2, 2, left, nbr_x),
                                        (2, 2, right, nbr_y), (0, 0, right, nbr_x), (1, 1, left, nbr_y)][k]
        return remote(s2[a].at[src_slot, :, cols], r2[a].at[dst_slot, :, cols], N_REDUCE_SEMS * a + 4 + k, to)

    to_sibling = [remote(sm[0], sm[2], base_small + 4 * i, sibling) for i, sm in enumerate(smalls)]
    to_chips = [[remote(sm[3], sm[4].at[j], base_small + 4 * i + 1 + j, (*chips[j + 1], c)) for j in range(3)]
                for i, sm in enumerate(smalls)]
    if rows is not None:
        rows_ref, land_ref, all_ref = rows
        base_rows = base_small + 4 * len(smalls)
        row_sends = [remote(rows_ref, land_ref.at[me], base_rows + k, to) for k, to in enumerate(peers)]

    order = (3, 1, 2, 0)

    def mine(a, q):
        buf, sems = own[a]
        return pltpu.make_async_copy(ins[a].at[_dev_index(*chips[q], c)], buf.at[q], sems.at[q])

    def start():
        for a in range(n):
            for q in order:
                level1(a, q).start()
            if own is not None:
                for q in order:
                    mine(a, q).start()
        for cp in to_sibling:
            cp.start()
        if rows is not None:
            for cp in row_sends:
                cp.start()
            land_ref[me] = rows_ref[...]

    def middle():
        for a in range(n):
            for q in order:
                level1(a, q).wait_recv()
                if own is None:
                    kept = ins[a][_dev_index(*chips[q], c)]
                else:
                    mine(a, q).wait()
                    kept = own[a][0][q]
                pair = kept.astype(F32) + r1[a][q].astype(F32)
                if q == 0:
                    gs[a][...] = pair
                else:
                    s2[a][q - 1] = pair.astype(BF16)
                    for k in ((0,), (1,), (2, 3))[q - 1]:
                        level2(a, k).start()
        for i, (small_ref, _, sm_sib, sm_chip, _) in enumerate(smalls):
            to_sibling[i].wait_recv()
            sm_chip[...] = small_ref[...] + sm_sib[...]
            for cp in to_chips[i]:
                cp.start()

    def fold():
        for a in range(n):
            half = ins[a].shape[-1] // 2
            level2(a, 3).wait_recv()
            s2[a][0, :, half:] = (s2[a][0, :, half:].astype(F32) + r2[a][2, :, half:].astype(F32)).astype(BF16)
            level2(a, 4).start()
            level2(a, 2).wait_recv()
            s2[a][1, :, :half] = (s2[a][1, :, :half].astype(F32) + r2[a][2, :, :half].astype(F32)).astype(BF16)
            level2(a, 5).start()

    def finish():
        for a in range(n):
            for k in (0, 1, 4, 5):
                level2(a, k).wait_recv()
            gs[a][...] = gs[a][...] + r2[a][0].astype(F32) + r2[a][1].astype(F32)
            for q in range(4):
                level1(a, q).wait_send()
            for k in range(6):
                level2(a, k).wait_send()
        for i, (_, total_ref, _, sm_chip, sm_recv) in enumerate(smalls):
            for cp in to_chips[i]:
                cp.wait_recv()
            total = None
            for ax in range(2):
                for ay in range(2):
                    dx, dy = x != ax, y != ay
                    term = jnp.where(dx, jnp.where(dy, sm_recv[2], sm_recv[0]), jnp.where(dy, sm_recv[1], sm_chip[...]))
                    total = term if total is None else total + term
            total_ref[...] = total
            for cp in [to_sibling[i]] + to_chips[i]:
                cp.wait_send()
        if rows is not None:
            for k, frm in enumerate(peers):
                remote(rows_ref, land_ref.at[_dev_index(*frm)], base_rows + k, frm).wait_recv()
            all_ref[...] = land_ref[...]
            for cp in row_sends:
                cp.wait_send()

    return start, middle, fold, finish


def _reduce_scratch(shard, smalls, rows=None):
    out = [pltpu.VMEM((lead,) + shard.shape[1:], BF16) for lead in (4, 3, 3)]
    for small in smalls:
        out += [pltpu.VMEM(small.shape, F32), pltpu.VMEM(small.shape, F32), pltpu.VMEM((3,) + small.shape, F32)]
    n_sems = N_REDUCE_SEMS + N_SMALL_SEMS * len(smalls)
    if rows is not None:
        out.append(pltpu.VMEM((N_DEV,) + rows.shape, F32))
        n_sems += N_ROWS_SEMS
    return out + [pltpu.SemaphoreType.DMA((n_sems,))] * 2


def _dot3_rhs(a, b):
    a0, a1, a2 = _split3(a)
    b0, b1, b2 = _split3(b)
    return (_dot(a0, b0) + (_dot(a0, b1) + _dot(a1, b0))
            + (_dot(a0, b2) + _dot(a1, b1) + _dot(a2, b0)))


def _gather_and_ada(c, w_in_rows, w_ada):
    cols = w_ada.shape[1]
    shard = w_in_rows.shape[0] // SUBLANES

    def body(c_ref, w_ref, wa_ref, w_main_hbm, w_f_ref, sc_ref, ada_ref,
             w_all_ref, w_f32, wm_buf, c_land, part, ada_land, send_sems, recv_sems, local_sem, x_send, x_recv, out_sems,
             w_packed):
        x, y, cc = _mesh_pos()
        me = _dev_index(x, y, cc)
        peers = []
        for p in range(1, N_DEV):
            px, py, pc = (p >> 2) & 1, (p >> 1) & 1, p & 1
            peers.append((1 - x if px else x, 1 - y if py else y, 1 - cc if pc else cc))

        def remote(src, dst, k, to):
            return pltpu.make_async_remote_copy(src_ref=src, dst_ref=dst, send_sem=x_send.at[k], recv_sem=x_recv.at[k],
                                                device_id=to, device_id_type=MESH)

        c_sends = [remote(c_ref, c_land.at[me], k, to) for k, to in enumerate(peers)]
        for cp in c_sends:
            cp.start()
        own = pltpu.make_async_copy(w_ref, w_packed, local_sem.at[0])
        own.start()
        own.wait()
        for j in range(SUBLANES):
            w_all_ref[me, :, j * LANE:(j + 1) * LANE] = w_packed[pl.ds(j, shard, stride=SUBLANES), :].astype(BF16)
        start, relay_near, relay_far, finish_near, finish_far = _gather_stages(
            None, w_all_ref, send_sems, recv_sems, local_sem.at[0])
        start()
        c_land[me] = c_ref[...]
        for k, frm in enumerate(peers):
            remote(c_ref, c_land.at[_dev_index(*frm)], k, frm).wait_recv()
        c_all = jnp.concatenate([c_land[b] for b in range(N_DEV)], axis=0)
        sc = c_all * _sigmoid(c_all)
        sc_ref[...] = sc
        rows = _dot3_rhs(sc, wa_ref[...])
        for b in range(N_DEV):
            part[b] = rows[b:b + 1, :]
        a_sends = [remote(part.at[_dev_index(*to)], ada_land.at[me], 7 + k, to) for k, to in enumerate(peers)]
        for cp in a_sends:
            cp.start()
        ada_land[me] = part[me]

        relay_near()
        finish_near()

        far_chip = 2 * (1 - x) + (1 - y)

        def stage(slots):
            for slot in slots:
                w_f32[slot * shard:(slot + 1) * shard, :] = w_all_ref[slot].astype(F32)

        def far_rows(k):
            main_row = lambda r: r if r < F_LO else r - N_HEADS
            first, last = 2 * shard * k, 2 * shard * (k + 1) - 1
            first = F_HI if F_LO <= first < F_HI else first
            last = F_LO - 1 if F_LO <= last < F_HI else last
            return (main_row(first) // BF16_TILE_ROWS * BF16_TILE_ROWS,
                    -(-(main_row(last) + 1) // BF16_TILE_ROWS) * BF16_TILE_ROWS)

        def near_rows(k):
            lo, hi = far_rows(k)
            return [(i, a, b) for i, (a, b) in enumerate(((0, lo), (hi, N_MAIN))) if a < b]

        def forget_is_far(k):
            return 2 * shard * k < F_HI and 2 * shard * (k + 1) > F_LO

        def main_copy(i, lo, hi):
            return pltpu.make_async_copy(wm_buf.at[lo:hi], w_main_hbm.at[lo:hi], out_sems.at[i])

        def write_main(i, lo, hi):
            if lo < min(hi, F_LO):
                wm_buf[lo:min(hi, F_LO), :] = w_f32[lo:min(hi, F_LO), :].astype(BF16)
            if max(lo, F_LO) < hi:
                wm_buf[max(lo, F_LO):hi, :] = w_f32[max(lo, F_LO) + N_HEADS:hi + N_HEADS, :].astype(BF16)
            main_copy(i, lo, hi).start()

        def write_forget():
            w_f_ref[...] = jnp.concatenate(
                [w_f32[F_LO:F_HI, :], jnp.zeros((LANE - N_HEADS, D), F32)], axis=0).astype(BF16)

        for k in range(N_DEV // 2):
            @pl.when(far_chip == k)
            def _(k=k):
                stage([slot for slot in range(N_DEV) if slot // 2 != k])
                for i, lo, hi in near_rows(k):
                    write_main(i, lo, hi)
                if not forget_is_far(k):
                    write_forget()

        relay_far()
        for k, frm in enumerate(peers):
            remote(part.at[0], ada_land.at[_dev_index(*frm)], 7 + k, frm).wait_recv()
        ada_ref[...] = ada_land[...]
        finish_far()
        for cp in c_sends + a_sends:
            cp.wait_send()

        for k in range(N_DEV // 2):
            @pl.when(far_chip == k)
            def _(k=k):
                stage([2 * k, 2 * k + 1])
                write_main(2, *far_rows(k))
                if forget_is_far(k):
                    write_forget()
                for i, lo, hi in near_rows(k):
                    main_copy(i, lo, hi).wait()
                main_copy(2, *far_rows(k)).wait()

    vmem = pl.BlockSpec(memory_space=pltpu.VMEM)
    return pl.pallas_call(
        body, name="gather_weights",
        in_specs=[vmem, ANY, vmem], out_specs=[ANY, vmem, vmem, vmem],
        out_shape=[jax.ShapeDtypeStruct((N_MAIN, D), BF16), jax.ShapeDtypeStruct((LANE, D), BF16),
                   jax.ShapeDtypeStruct((N_DEV, D), F32), jax.ShapeDtypeStruct((N_DEV, 1, cols), F32)],
        scratch_shapes=[pltpu.VMEM((N_DEV, shard, D), BF16), pltpu.VMEM((D_IN, D), F32), pltpu.VMEM((N_MAIN, D), BF16),
                        pltpu.VMEM((N_DEV, 1, D), F32), pltpu.VMEM((N_DEV, 1, cols), F32), pltpu.VMEM((N_DEV, 1, cols), F32),
                        pltpu.SemaphoreType.DMA((N_GATHER_SEMS,)), pltpu.SemaphoreType.DMA((N_GATHER_SEMS,)),
                        pltpu.SemaphoreType.DMA((1,)),
                        pltpu.SemaphoreType.DMA((14,)), pltpu.SemaphoreType.DMA((14,)), pltpu.SemaphoreType.DMA((3,)),
                        pltpu.VMEM(w_in_rows.shape, F32)],
        compiler_params=_params(),
    )(c, w_in_rows, w_ada)


def _inproj_forward(x, mod, w_main, w_f, b_main, b_f, tile):
    seq = x.shape[0]
    nt = seq // tile

    def body(x_ref, mod_ref, w_ref, wf_ref, b_ref, bf_ref,
             qp_ref, kp_ref, vp_ref, f_ref, p_ref, ga_ref, gp_ref, u_ref, carry_ref):
        i = pl.program_id(0)

        @pl.when(i == 0)
        def _():
            carry_ref[...] = jnp.zeros_like(carry_ref)

        u = x_ref[...] * mod_ref[0:1, :] + mod_ref[1:2, :]
        ub = u.astype(BF16)
        u_ref[...] = ub

        f = _dot_nt(ub, wf_ref[...]) + bf_ref[...]
        f_ref[...] = f
        lane = _lanes((tile, LANE))
        log_f = jnp.where(lane < N_HEADS, jnp.minimum(f, 0.0) - jnp.log(1.0 + jnp.exp(-jnp.abs(f))), 0.0)
        row = lax.broadcasted_iota(jnp.int32, (tile, tile), 0)
        col = lax.broadcasted_iota(jnp.int32, (tile, tile), 1)
        tri = (row >= col).astype(BF16)
        cum = _dot3(tri, log_f) + carry_ref[0:1, :]
        carry_ref[0:1, :] = cum[tile - 1:tile, :]
        cq = [part.astype(F32) for part in _split3(cum)]
        ck = [part.astype(F32) for part in _split3(-cum)]

        def proj(chunk):
            cols = pl.ds(chunk * COL_CHUNK, COL_CHUNK)
            return _dot_nt(ub, w_ref[cols, :]) + b_ref[:, cols]

        def head_tiles(r):
            for pair in range(N_HEADS // 2):
                both = r[:, pair * LANE:(pair + 1) * LANE]
                yield 2 * pair, both
                yield 2 * pair + 1, pltpu.roll(both, HEAD_DIM, 1)

        for h, val in head_tiles(proj(0)):
            extra = jnp.where((lane >= L_CK) & (lane < L_CK + 3), 1.0, 0.0)
            extra = _place3(lane, L_CQ, [part[:, h:h + 1] for part in cq], extra)
            qp_ref[h] = jnp.where(lane < HEAD_DIM, val * Q_SCALE, extra).astype(BF16)
        for h, val in head_tiles(proj(1)):
            ones = ((lane >= L_CQ) & (lane < L_CQ + 3)) | ((lane >= L_LSE) & (lane < L_LSE + 3))
            extra = _place3(lane, L_CK, [part[:, h:h + 1] for part in ck], jnp.where(ones, 1.0, 0.0))
            kp_ref[h] = jnp.where(lane < HEAD_DIM, val, extra).astype(BF16)
        for h, val in head_tiles(proj(2)):
            extra = jnp.where((lane >= HEAD_DIM) & (lane < HEAD_DIM + 3), -1.0, 0.0)
            vp_ref[h] = jnp.where(lane < HEAD_DIM, val, extra).astype(BF16)
        p_ref[...] = proj(3)
        ga_ref[...] = proj(4)
        gp_ref[...] = proj(5)

    head_block = pl.BlockSpec((N_HEADS, tile, LANE), lambda i: (0, i, 0))
    tok = lambda width: pl.BlockSpec((tile, width), lambda i: (i, 0))
    whole = lambda a: pl.BlockSpec(a.shape, lambda i: (0,) * a.ndim)
    padded = jax.ShapeDtypeStruct((N_HEADS, seq, LANE), BF16)
    half = jax.ShapeDtypeStruct((seq, D_ATT), F32)
    return pl.pallas_call(
        body, name="inproj_forward", grid=(nt,),
        in_specs=[tok(D), whole(mod), whole(w_main), whole(w_f), whole(b_main), whole(b_f)],
        out_specs=[head_block, head_block, head_block, tok(LANE), tok(D_POOL), tok(D_ATT), tok(D_POOL),
                   tok(D)],
        out_shape=[padded, padded, padded, jax.ShapeDtypeStruct((seq, LANE), F32), half, half, half,
                   jax.ShapeDtypeStruct((seq, D), BF16)],
        scratch_shapes=[pltpu.VMEM((8, LANE), F32)],
        compiler_params=_params(("arbitrary",)),
    )(x, mod, w_main, w_f, b_main, b_f)


def _attention_forward(qp, kp, vp, w_out, tile):
    seq = qp.shape[1]
    nb = seq // tile
    steps = (N_HEADS // 2) * nb

    def body(q_ref, k_ref, v_ref, wo_ref, att_ref, q2t_ref, wo_all_ref, s_a, s_b, m_ref, acc_ref,
             send_sems, recv_sems, local_sem):
        step = pl.program_id(0) * nb + pl.program_id(1)
        start, relay_near, relay_far, finish_near, finish_far = _gather_stages(
            wo_ref, wo_all_ref, send_sems, recv_sems, local_sem.at[0])
        pl.when(step == 0)(start)
        pl.when(step == steps // 4)(relay_near)
        pl.when(step == (3 * steps) // 4)(relay_far)

        i = pl.program_id(1)
        sub = lax.broadcasted_iota(jnp.int32, (LANE, tile), 0)
        row = lax.broadcasted_iota(jnp.int32, (tile, tile), 0)
        col = lax.broadcasted_iota(jnp.int32, (tile, tile), 1)
        q = [q_ref[0], q_ref[1]]

        def scores(buf, kb):
            rows = pl.ds(pl.multiple_of(kb * tile, tile), tile)
            for hh in range(2):
                buf[hh] = _dot_nt(k_ref[hh, rows, :], q[hh])

        def absorb(buf, kb, masked):
            rows = pl.ds(pl.multiple_of(kb * tile, tile), tile)
            for hh in range(2):
                m = m_ref[hh, 0:1, :]
                s = buf[hh]
                if masked:
                    s = jnp.where(row <= col, s, -1e30)
                m_new = jnp.maximum(m, jnp.max(s, axis=0, keepdims=True))
                p = jnp.exp(s - m_new).astype(BF16)
                acc_ref[hh] = jnp.exp(m - m_new) * acc_ref[hh] + _dot_tn(v_ref[hh, rows, :], p)
                m_ref[hh, 0:1, :] = m_new

        def two_blocks(j, _):
            scores(s_b, 2 * j + 1)
            absorb(s_a, 2 * j, False)
            scores(s_a, 2 * j + 2)
            absorb(s_b, 2 * j + 1, False)
            return 0

        def last_block():
            absorb(s_a, i, True)

        def last_two_blocks():
            scores(s_b, i)
            absorb(s_a, i - 1, False)
            absorb(s_b, i, True)

        scores(s_a, 0)
        m_ref[...] = jnp.full(m_ref.shape, -1e30, F32)
        acc_ref[...] = jnp.zeros_like(acc_ref)
        lax.fori_loop(0, i // 2, two_blocks, 0)
        lax.cond(i % 2 == 0, last_block, last_two_blocks)
        outs = []
        for hh in range(2):
            m, acc = m_ref[hh, 0:1, :], acc_ref[hh]
            l = -acc[HEAD_DIM:HEAD_DIM + 1, :]
            outs.append((acc / l)[:HEAD_DIM, :])
            neg_lse = [part.astype(F32) for part in _split3(-(m + jnp.log(l)))]
            q2t_ref[hh] = _place3(sub, L_LSE, neg_lse, q[hh].astype(F32).T).astype(BF16)
        att_ref[...] = jnp.concatenate(outs, axis=0).T
        @pl.when(step == steps - 1)
        def _():
            finish_near()
            finish_far()

    pair = pl.BlockSpec((2, tile, LANE), lambda hp, i: (hp, i, 0))
    full = pl.BlockSpec((2, seq, LANE), lambda hp, i: (hp, 0, 0))
    return pl.pallas_call(
        body, name="attention_forward", grid=(N_HEADS // 2, nb),
        in_specs=[pair, full, full, ANY],
        out_specs=[pl.BlockSpec((tile, LANE), lambda hp, i: (i, hp)),
                   pl.BlockSpec((2, LANE, tile), lambda hp, i: (hp, 0, i)), ANY],
        out_shape=[jax.ShapeDtypeStruct((seq, D_ATT), F32),
                   jax.ShapeDtypeStruct((N_HEADS, LANE, seq), BF16),
                   jax.ShapeDtypeStruct((N_DEV,) + w_out.shape, w_out.dtype)],
        scratch_shapes=[pltpu.VMEM((2, tile, tile), F32), pltpu.VMEM((2, tile, tile), F32),
                        pltpu.VMEM((2, 8, tile), F32), pltpu.VMEM((2, LANE, tile), F32),
                        pltpu.SemaphoreType.DMA((N_GATHER_SEMS,)), pltpu.SemaphoreType.DMA((N_GATHER_SEMS,)),
                        pltpu.SemaphoreType.DMA((1,))],
        compiler_params=_params(("arbitrary", "arbitrary")),
    )(qp, kp, vp, w_out)


def _window_sum(x, halo, window, transposed):
    tile = x.shape[0]

    def split_cat(a):
        hi = a.astype(BF16)
        return jnp.concatenate([hi, (a - hi.astype(F32)).astype(BF16)], axis=1)

    def fold(r):
        return r[:, :LANE] + r[:, LANE:]

    r = lax.broadcasted_iota(jnp.int32, (tile, tile), 0)
    c = lax.broadcasted_iota(jnp.int32, (tile, tile), 1)
    rh = lax.broadcasted_iota(jnp.int32, (HALO, HALO), 0)
    ch = lax.broadcasted_iota(jnp.int32, (HALO, HALO), 1)
    if not transposed:
        band = (c <= r) & (r - c < window)
        edge = (rh + HALO - ch) < window
    else:
        band = (r <= c) & (c - r < window)
        edge = (HALO + ch - rh) < window
    out = fold(_dot(band.astype(BF16), split_cat(x)))
    reach = fold(_dot(edge.astype(BF16), split_cat(halo)))
    if not transposed:
        return jnp.concatenate([out[:HALO] + reach, out[HALO:]], axis=0)
    return jnp.concatenate([out[:tile - HALO], out[tile - HALO:] + reach], axis=0)


def _silu_parts(g):
    sig = _sigmoid(g)
    return g * sig, sig * (1.0 + g * (1.0 - sig))


def _middle(x, tgt, att, g_att, g_pool, p, vecs, pool_vecs, w_out, w_pool, tile):
    seq = x.shape[0]
    nt = seq // tile
    halo_blocks = tile // HALO

    def body(x_ref, tgt_ref, att_ref, ga_ref, gp_ref, p_ref, ph_ref, vec_ref, pvec_ref, wo_ref, wp_ref,
             dxa_ref, do2_ref, dga_ref, dgp_ref, dpooled_ref, gwo_ref, dwp_ref, dvec_ref, dwo_ref, dpvec_ref):
        i = pl.program_id(0)

        @pl.when(i == 0)
        def _():
            dwo_ref[...] = jnp.zeros_like(dwo_ref)
            dwp_ref[...] = jnp.zeros_like(dwp_ref)
            dvec_ref[...] = jnp.zeros_like(dvec_ref)
            dpvec_ref[...] = jnp.zeros_like(dpvec_ref)

        gate, b_out, ln_g, ln_b = (vec_ref[k:k + 1, :] for k in range(4))
        b_pool, pool_scale = pvec_ref[0:1, :], pvec_ref[1:2, :]
        x = x_ref[...]
        p = p_ref[...]
        p_halo = ph_ref[...] * jnp.where(i > 0, 1.0, 0.0)
        pos = i * tile + lax.broadcasted_iota(jnp.int32, (tile, 1), 0) + 1

        pooled, mixed = [], []
        for g, window in enumerate(POOL_WINDOWS):
            cols = slice(g * GROUP_DIM, (g + 1) * GROUP_DIM)
            wsum = _window_sum(p[:, cols], p_halo[:, cols], window, False)
            count = jnp.minimum(pos, window).astype(F32)
            pooled.append(wsum / count - p[:, cols])
            mixed.append(_dot(pooled[g].astype(BF16), wp_ref[g]) + b_pool[:, cols])
        mixed = jnp.concatenate(mixed, axis=1)
        pool = mixed * pool_scale

        att = att_ref[...]
        g_att, g_pool = ga_ref[...], gp_ref[...]
        silu_a, dsilu_a = _silu_parts(g_att)
        silu_p, dsilu_p = _silu_parts(g_pool)
        y_in = jnp.concatenate([att * silu_a, pool * silu_p], axis=1)
        y = _dot(y_in.astype(BF16), wo_ref[...]) + b_out
        h = ALPHA * x + gate * y
        mu = jnp.mean(h, axis=1, keepdims=True)
        hc = h - mu
        var = jnp.mean(hc * hc, axis=1, keepdims=True)
        rstd = lax.rsqrt(var + LN_EPS)
        yhat = hc * rstd
        diff = yhat * ln_g + ln_b - tgt_ref[...]
        loss_rows = jnp.sum(diff * diff, axis=1, keepdims=True)
        d_out = diff * (1.0 / D)

        d_yhat = d_out * ln_g
        dh = rstd * (d_yhat - jnp.mean(d_yhat, axis=1, keepdims=True)
                     - yhat * jnp.mean(d_yhat * yhat, axis=1, keepdims=True))
        dxa_ref[...] = ALPHA * dh
        dy = dh * gate
        dyb = dy.astype(BF16)
        lane = _lanes((1, D))
        loss_row = jnp.where(lane == 0, (0.5 / D) * jnp.sum(loss_rows, axis=0, keepdims=True), 0.0)
        dvec_ref[5:6, :] += jnp.sum(dh * y, axis=0, keepdims=True)
        dvec_ref[0:1, :] += jnp.sum(dy, axis=0, keepdims=True)
        dvec_ref[1:2, :] += jnp.sum(d_out * yhat, axis=0, keepdims=True)
        dvec_ref[2:3, :] += jnp.sum(d_out, axis=0, keepdims=True)
        dvec_ref[4:5, :] += loss_row

        dwo_ref[...] += _dot(y_in.T.astype(BF16), dyb)
        d_yin = _dot_nt(dyb, wo_ref[...])
        d_a, d_pl = d_yin[:, :D_ATT], d_yin[:, D_ATT:]
        d_att = d_a * silu_a
        d_att_t = d_att.T
        prod_t = (d_att * att).T
        sub = lax.broadcasted_iota(jnp.int32, (HEAD_DIM, tile), 0)
        for h in range(N_HEADS):
            rows = slice(h * HEAD_DIM, (h + 1) * HEAD_DIM)
            delta = jnp.sum(prod_t[rows], axis=0, keepdims=True)
            extra = _place3(sub, 0, [part.astype(F32) for part in _split3(delta)], 0.0)
            do2_ref[h] = jnp.concatenate([d_att_t[rows], extra], axis=0).astype(BF16)
        dga_ref[...] = d_a * att * dsilu_a
        dgp_ref[...] = d_pl * pool * dsilu_p
        d_pool = d_pl * silu_p
        d_mixed = d_pool * pool_scale
        dpvec_ref[0:1, :] += jnp.sum(d_mixed, axis=0, keepdims=True)
        dpvec_ref[1:2, :] += jnp.sum(d_pool * mixed, axis=0, keepdims=True)
        d_pooled = []
        for g in range(len(POOL_WINDOWS)):
            cols = slice(g * GROUP_DIM, (g + 1) * GROUP_DIM)
            dmb = d_mixed[:, cols].astype(BF16)
            dwp_ref[g] += _dot(pooled[g].T.astype(BF16), dmb)
            d_pooled.append(_dot_nt(dmb, wp_ref[g]))
        dpooled_ref[...] = jnp.concatenate(d_pooled, axis=1)

        @pl.when(i == nt - 1)
        def _():
            gwo_ref[...] = dwo_ref[...].astype(BF16)
            dvec_ref[3:4, :] = jnp.concatenate([dpvec_ref[0:1, :], dpvec_ref[1:2, :]], axis=1)

    tok = lambda width: pl.BlockSpec((tile, width), lambda i: (i, 0))
    whole = lambda a: pl.BlockSpec(a.shape, lambda i: (0,) * a.ndim)
    halo = pl.BlockSpec((HALO, D_POOL), lambda i: (jnp.maximum(i * halo_blocks - 1, 0), 0))
    half = jax.ShapeDtypeStruct((seq, D_ATT), F32)
    outs = [jax.ShapeDtypeStruct((seq, D), F32), jax.ShapeDtypeStruct((N_HEADS, LANE, seq), BF16), half, half, half,
            jax.ShapeDtypeStruct(w_out.shape, BF16), jax.ShapeDtypeStruct(w_pool.shape, F32),
            jax.ShapeDtypeStruct(vecs.shape, F32)]
    return pl.pallas_call(
        body, name="middle", grid=(nt,),
        in_specs=[tok(D), tok(D), tok(D_ATT), tok(D_ATT), tok(D_POOL), tok(D_POOL), halo,
                  whole(vecs), whole(pool_vecs), whole(w_out), whole(w_pool)],
        out_specs=[tok(D), pl.BlockSpec((N_HEADS, LANE, tile), lambda i: (0, 0, i)),
                   tok(D_ATT), tok(D_POOL), tok(D_POOL),
                   whole(w_out), whole(w_pool), whole(vecs)],
        out_shape=outs,
        scratch_shapes=[pltpu.VMEM(w_out.shape, F32), pltpu.VMEM(pool_vecs.shape, F32)],
        compiler_params=_params(("arbitrary",)),
    )(x, tgt, att, g_att, g_pool, p, p, vecs, pool_vecs, w_out, w_pool)


def _attention_backward(q2t, kp, vp, do2t, gw_out, vecs, pool, tile):
    seq = kp.shape[1]
    nb = seq // tile
    last = N_HEADS // 2 - 1

    def body(qt_ref, k_ref, v_ref, dot_ref, gwo_hbm, vecs_hbm, pool_hbm,
             dq_ref, dk_ref, dv_ref, dcum_ref, g_out_ref, vecs_sum_ref, pool_sum_ref,
             dq_acc, dk_acc, dv_acc, gwo_ref, vecs_ref, pool_ref,
             r1, s2, r2, v_sib, v_chip, v_recv, p_sib, p_chip, p_recv, send_sems, recv_sems):
        hp = pl.program_id(0)
        start, middle, fold, finish = _reduce_stages(
            [gwo_ref], [g_out_ref], [r1], [s2], [r2],
            [(vecs_ref, vecs_sum_ref, v_sib, v_chip, v_recv), (pool_ref, pool_sum_ref, p_sib, p_chip, p_recv)],
            send_sems, recv_sems)

        @pl.when(hp == 0)
        def _():
            pltpu.sync_copy(gwo_hbm, gwo_ref)
            pltpu.sync_copy(vecs_hbm, vecs_ref)
            pltpu.sync_copy(pool_hbm, pool_ref)
            start()

        pl.when(hp == 1)(middle)
        pl.when(hp == 2)(fold)

        row = lax.broadcasted_iota(jnp.int32, (tile, tile), 0)
        col = lax.broadcasted_iota(jnp.int32, (tile, tile), 1)
        dq_acc[...] = jnp.zeros_like(dq_acc)

        def kv_block(kb, _):
            krows = pl.ds(pl.multiple_of(kb * tile, tile), tile)
            k = [k_ref[hh, krows, :] for hh in range(2)]
            v = [v_ref[hh, krows, :] for hh in range(2)]
            k_t = [k[hh].T for hh in range(2)]

            def q_block(qb, masked):
                qcols = pl.ds(pl.multiple_of(qb * tile, tile), tile)
                for hh in range(2):
                    q_t = qt_ref[hh, :, qcols]
                    do_t = dot_ref[hh, :, qcols]
                    s_t = _dot(k[hh], q_t)
                    if masked:
                        s_t = jnp.where(row <= col, s_t, -1e30)
                    p_t = jnp.exp(s_t)
                    ds_t = (p_t * _dot(v[hh], do_t)).astype(BF16)
                    dv_new = _dot_nt(do_t, p_t.astype(BF16))
                    dk_new = _dot_nt(q_t, ds_t)
                    if masked:
                        dv_acc[hh], dk_acc[hh] = dv_new, dk_new
                    else:
                        dv_acc[hh] += dv_new
                        dk_acc[hh] += dk_new
                    dq_acc[hh, :, qcols] += _dot(k_t[hh], ds_t)

            q_block(kb, True)

            def two_later_blocks(j, _):
                q_block(kb + 1 + 2 * j, False)
                q_block(kb + 2 + 2 * j, False)
                return 0

            later = nb - 1 - kb
            lax.fori_loop(0, later // 2, two_later_blocks, 0)
            pl.when(later % 2 == 1)(lambda: q_block(nb - 1, False))
            for hh in range(2):
                dk = dk_acc[hh]
                dk_ref[hh, :, krows] = dk.astype(BF16)
                dv_ref[hh, :, krows] = dv_acc[hh].astype(BF16)
                dcum_ref[hh, :, krows] = -dk[L_CK:L_CK + 1, :]
            return 0

        lax.fori_loop(0, nb, kv_block, 0)
        for hh in range(2):
            dq = dq_acc[hh]
            dcum_ref[hh] += dq[L_CQ:L_CQ + 1, :]
            dq_ref[hh] = (dq * Q_SCALE).astype(BF16)
        pl.when(hp == last)(finish)

    pair = pl.BlockSpec((2, seq, LANE), lambda hp: (hp, 0, 0))
    pair_t = pl.BlockSpec((2, LANE, seq), lambda hp: (hp, 0, 0))
    whole = lambda shape: pl.BlockSpec(shape, lambda hp: (0,) * len(shape))
    grad = jax.ShapeDtypeStruct((N_HEADS, LANE, seq), BF16)
    return pl.pallas_call(
        body, name="attention_backward", grid=(N_HEADS // 2,),
        in_specs=[pair_t, pair, pair, pair_t, ANY, ANY, ANY],
        out_specs=[pair_t, pair_t, pair_t, pl.BlockSpec((2, 1, seq), lambda hp: (hp, 0, 0)),
                   whole(gw_out.shape[1:]), whole(vecs.shape), whole(pool.shape)],
        out_shape=[grad, grad, grad, jax.ShapeDtypeStruct((N_HEADS, 1, seq), F32),
                   jax.ShapeDtypeStruct(gw_out.shape[1:], F32), jax.ShapeDtypeStruct(vecs.shape, F32),
                   jax.ShapeDtypeStruct(pool.shape, F32)],
        scratch_shapes=[pltpu.VMEM((2, LANE, seq), F32), pltpu.VMEM((2, LANE, tile), F32),
                        pltpu.VMEM((2, LANE, tile), F32), pltpu.VMEM(gw_out.shape, BF16),
                        pltpu.VMEM(vecs.shape, F32), pltpu.VMEM(pool.shape, F32)]
        + _reduce_scratch(gw_out, [vecs, pool]),
        compiler_params=_params(("arbitrary",)),
    )(q2t, kp, vp, do2t, gw_out, vecs, pool)


def _inproj_backward(dqp, dkp, dvp, d_cum, f, d_pooled, d_ga, d_gp, x, dxa, u, mod, w_main, w_f, tile):
    seq = x.shape[0]
    nt = seq // tile
    halo_blocks = tile // HALO

    def body(dq_ref, dk_ref, dv_ref, dcum_ref, f_ref, dpo_ref, dph_ref, dga_ref, dgp_ref, x_ref, dxa_ref, u_ref,
             mod_ref, w_ref, wf_ref,
             dx_ref, dproj_ref, dwf_ref, db_ref, dbf_ref, dmod_ref, carry_ref):
        step = pl.program_id(0)
        i = nt - 1 - step

        @pl.when(step == 0)
        def _():
            carry_ref[...] = jnp.zeros_like(carry_ref)
            dwf_ref[...] = jnp.zeros_like(dwf_ref)
            db_ref[...] = jnp.zeros_like(db_ref)
            dbf_ref[...] = jnp.zeros_like(dbf_ref)
            dmod_ref[...] = jnp.zeros_like(dmod_ref)

        ones = jnp.ones((8, tile), BF16)

        def emit(chunk, val):
            cols = pl.ds(chunk * COL_CHUNK, COL_CHUNK)
            db_ref[0:1, cols] += jnp.sum(val, axis=0, keepdims=True)
            vb = val.astype(BF16)
            dproj_ref[:, pl.ds((chunk - 3) * COL_CHUNK, COL_CHUNK)] = vb
            return _dot(vb, w_ref[cols, :])

        d_u = jnp.zeros((tile, D), F32)
        for chunk, ref in enumerate((dq_ref, dk_ref, dv_ref)):
            cols = pl.ds(chunk * COL_CHUNK, COL_CHUNK)
            val_t = ref[:, 0:HEAD_DIM, :].reshape(COL_CHUNK, tile)
            db_ref[:, cols] += _dot_nt(ones, val_t)
            d_u += _dot_tn(val_t, w_ref[cols, :])

        d_pooled = dpo_ref[...]
        d_halo = dph_ref[...] * jnp.where(i < nt - 1, 1.0, 0.0)
        pos = i * tile + lax.broadcasted_iota(jnp.int32, (tile, 1), 0) + 1
        d_p = []
        for g, window in enumerate(POOL_WINDOWS):
            cols = slice(g * GROUP_DIM, (g + 1) * GROUP_DIM)
            scaled = d_pooled[:, cols] / jnp.minimum(pos, window).astype(F32)
            d_p.append(_window_sum(scaled, d_halo[:, cols] * (1.0 / window), window, True) - d_pooled[:, cols])
        d_u += emit(3, jnp.concatenate(d_p, axis=1))
        d_u += emit(4, dga_ref[...])
        d_u += emit(5, dgp_ref[...])

        row = lax.broadcasted_iota(jnp.int32, (tile, tile), 0)
        col = lax.broadcasted_iota(jnp.int32, (tile, tile), 1)
        later = (row >= col).astype(BF16)
        d_logf = sum(_dot(part, later) for part in _split3(dcum_ref[:, 0, :])) + carry_ref[:, 0:1]
        carry_ref[:, 0:1] = d_logf[:, 0:1]
        d_f = d_logf * _sigmoid(-f_ref[...].T[0:N_HEADS, :])
        d_f = jnp.concatenate([d_f, jnp.zeros((LANE - N_HEADS, tile), F32)], axis=0)
        dbf_ref[...] += sum(_dot_nt(ones, part) for part in _split3(d_f))
        d_fb = d_f.astype(BF16)
        d_u += _dot_tn(d_fb, wf_ref[...])
        dwf_ref[...] += _dot(d_fb, u_ref[...])

        x = x_ref[...]
        dx_ref[...] = dxa_ref[...] + d_u * mod_ref[0:1, :]
        dmod_ref[0:1, :] += jnp.sum(d_u * x, axis=0, keepdims=True)
        dmod_ref[1:2, :] += jnp.sum(d_u, axis=0, keepdims=True)

    rev = lambda step: nt - 1 - step
    tok = lambda width: pl.BlockSpec((tile, width), lambda s: (rev(s), 0))
    head_block = pl.BlockSpec((N_HEADS, LANE, tile), lambda s: (0, 0, rev(s)))
    whole = lambda a: pl.BlockSpec(a.shape, lambda s: (0,) * a.ndim)
    halo = pl.BlockSpec((HALO, D_POOL), lambda s: (jnp.minimum((rev(s) + 1) * halo_blocks, seq // HALO - 1), 0))
    small = lambda width: jax.ShapeDtypeStruct((8, width), F32)
    n_rest = N_MAIN - OFF_P
    return pl.pallas_call(
        body, name="inproj_backward", grid=(nt,),
        in_specs=[head_block, head_block, head_block, pl.BlockSpec((N_HEADS, 1, tile), lambda s: (0, 0, rev(s))),
                  tok(LANE), tok(D_POOL), halo, tok(D_ATT), tok(D_POOL),
                  tok(D), tok(D), tok(D),
                  whole(mod), whole(w_main), whole(w_f)],
        out_specs=[tok(D), tok(n_rest), pl.BlockSpec((LANE, D), lambda s: (0, 0)),
                   pl.BlockSpec((8, N_MAIN), lambda s: (0, 0)), pl.BlockSpec((8, LANE), lambda s: (0, 0)),
                   pl.BlockSpec((8, D), lambda s: (0, 0))],
        out_shape=[jax.ShapeDtypeStruct((seq, D), F32), jax.ShapeDtypeStruct((seq, n_rest), BF16),
                   jax.ShapeDtypeStruct((LANE, D), F32), small(N_MAIN), small(LANE), small(D)],
        scratch_shapes=[pltpu.VMEM((8, LANE), F32)],
        compiler_params=_params(("arbitrary",)),
    )(dqp, dkp, dvp, d_cum, f, d_pooled, d_pooled, d_ga, d_gp, x, dxa, u, mod, w_main, w_f)


def _weight_grads(dq_t, dk_t, dv_t, dw_f, dproj, u, k_tile, small, rows):
    seq = u.shape[0]
    nk = seq // k_tile
    head_rows = N_HEADS * HEAD_DIM
    width = D // 2
    shard = D_IN // N_DEV
    assert nk >= 4

    def body(dq_ref, dk_ref, dv_ref, dwf_ref, dp_ref, u_ref, small_ref, rows_ref,
             left_sum_ref, right_sum_ref, total_ref, rows_all_ref, acc_ref, left_ref, right_ref,
             r1, s2, r2, send_sems, recv_sems,
             own_r1, own_s2, own_r2, sm_sib, sm_chip, sm_recv, rows_land, own_send_sems, own_recv_sems):
        half, k = pl.program_id(0), pl.program_id(1)
        start, middle, fold, finish = _reduce_stages(
            [left_ref], [left_sum_ref], [r1], [s2], [r2], [], send_sems, recv_sems)
        own_stages = _reduce_stages(
            [right_ref], [right_sum_ref], [own_r1], [own_s2], [own_r2],
            [(small_ref, total_ref, sm_sib, sm_chip, sm_recv)], own_send_sems, own_recv_sems,
            rows=(rows_ref, rows_land, rows_all_ref))
        on_right = lambda step: jnp.logical_and(half == 1, k == step)
        pl.when(on_right(0))(start)
        pl.when(on_right(1))(middle)
        pl.when(on_right(nk // 2))(fold)

        @pl.when(k == 0)
        def _():
            acc_ref[...] = jnp.zeros_like(acc_ref)

        tokens = u_ref[...]
        for j, ref in enumerate((dq_ref, dk_ref, dv_ref)):
            acc_ref[pl.ds(j * head_rows, head_rows), :] += _dot(ref[...].reshape(head_rows, k_tile), tokens)
        for j in range(dproj.shape[1] // COL_CHUNK):
            cols = pl.ds(j * COL_CHUNK, COL_CHUNK)
            acc_ref[pl.ds(F_HI + j * COL_CHUNK, COL_CHUNK), :] += _dot_tn(dp_ref[:, cols], tokens)

        def slots_to(out_ref):
            acc_ref[F_LO:F_HI, :] = dwf_ref[0:N_HEADS, :]
            for slot in range(N_DEV):
                out_ref[slot] = acc_ref[slot * shard:(slot + 1) * shard, :].astype(BF16)

        @pl.when(jnp.logical_and(half == 0, k == nk - 1))
        def _():
            slots_to(left_ref)

        @pl.when(on_right(nk - 1))
        def _():
            slots_to(right_ref)
            own_stages[0]()
            finish()
            for stage in own_stages[1:]:
                stage()

    heads = pl.BlockSpec((N_HEADS, HEAD_DIM, k_tile), lambda h, k: (0, 0, k))
    whole = lambda a: pl.BlockSpec(a.shape, lambda h, k: (0,) * a.ndim)
    slots = jax.ShapeDtypeStruct((N_DEV, shard, width), BF16)
    shard_sum = jax.ShapeDtypeStruct((shard, width), F32)
    out_shape = [shard_sum, shard_sum, jax.ShapeDtypeStruct(small.shape, F32),
                 jax.ShapeDtypeStruct((N_DEV,) + rows.shape, F32)]
    return pl.pallas_call(
        body, name="weight_grads", grid=(2, nk),
        in_specs=[heads, heads, heads, pl.BlockSpec((dw_f.shape[0], width), lambda h, k: (0, h)),
                  pl.BlockSpec((k_tile, dproj.shape[1]), lambda h, k: (k, 0)),
                  pl.BlockSpec((k_tile, width), lambda h, k: (k, h)), whole(small), whole(rows)],
        out_specs=[whole(a) for a in out_shape], out_shape=out_shape,
        scratch_shapes=[pltpu.VMEM((D_IN, width), F32), pltpu.VMEM(slots.shape, BF16), pltpu.VMEM(slots.shape, BF16)]
        + _reduce_scratch(slots, []) + _reduce_scratch(slots, [small], rows),
        compiler_params=_params(("arbitrary", "arbitrary")),
    )(dq_t, dk_t, dv_t, dw_f, dproj, u, small, rows)


def _adamw(w, g, m, v):
    m = ADAM_B1 * m + (1.0 - ADAM_B1) * g
    v = ADAM_B2 * v + (1.0 - ADAM_B2) * (g * g)
    m_hat = m / (1.0 - ADAM_B1 ** ADAM_STEP)
    v_hat = v / (1.0 - ADAM_B2 ** ADAM_STEP)
    delta = -ADAM_LR * (m_hat / (jnp.sqrt(v_hat) + ADAM_EPS) + ADAM_WD * w)
    return delta, m, v


SUBLANES = 8


def _adamw_packed(g_parts, w, m, v, name, chunks=4):
    n_parts, (rows, part_cols) = len(g_parts), g_parts[0].shape
    cols = n_parts * part_cols
    per_row, per_part = cols // LANE, part_cols // LANE
    assert part_cols % LANE == 0 and per_row == SUBLANES and w.shape == (rows * per_row, LANE)
    step = -(-rows // (chunks * SUBLANES)) * SUBLANES
    bounds = [(r0, min(r0 + step, rows)) for r0 in range(0, rows, step)]

    def body(*refs):
        g_hbm, (w_hbm, m_hbm, v_hbm, og_hbm, od_hbm, om_hbm, ov_hbm) = refs[:n_parts], refs[n_parts:n_parts + 7]
        g_buf, in_buf, out_buf, in_sems, out_sems = refs[n_parts + 7:]

        def copies_in(c):
            r0, r1 = bounds[c]
            packed = slice(r0 * per_row, r1 * per_row)
            return [pltpu.make_async_copy(g_hbm[p].at[r0:r1], g_buf.at[p, r0:r1], in_sems.at[c, 3 + p])
                    for p in range(n_parts)] + [
                pltpu.make_async_copy(src.at[packed], in_buf.at[i, packed], in_sems.at[c, i])
                for i, src in enumerate((w_hbm, m_hbm, v_hbm))]

        def copies_out(c):
            r0, r1 = bounds[c]
            packed = slice(r0 * per_row, r1 * per_row)
            return [pltpu.make_async_copy(out_buf.at[i, packed], dst.at[packed], out_sems.at[c, i])
                    for i, dst in enumerate((og_hbm, od_hbm, om_hbm, ov_hbm))]

        for c in range(len(bounds)):
            for cp in copies_in(c):
                cp.start()
        for c, (r0, r1) in enumerate(bounds):
            for cp in copies_in(c):
                cp.wait()
            for j in range(per_row):
                lanes = pl.ds(r0 * per_row + j, r1 - r0, stride=per_row)
                g_part = g_buf[j // per_part, r0:r1, (j % per_part) * LANE:(j % per_part + 1) * LANE]
                results = _adamw(in_buf[0, lanes, :], g_part, in_buf[1, lanes, :], in_buf[2, lanes, :])
                for i, val in enumerate((g_part,) + results):
                    out_buf[i, lanes, :] = val
            for cp in copies_out(c):
                cp.start()
        for c in range(len(bounds)):
            for cp in copies_out(c):
                cp.wait()

    shape = jax.ShapeDtypeStruct(w.shape, F32)
    return pl.pallas_call(
        body, name=name,
        in_specs=[ANY] * (n_parts + 3), out_specs=[ANY] * 4, out_shape=[shape] * 4,
        scratch_shapes=[pltpu.VMEM((n_parts, rows, part_cols), F32), pltpu.VMEM((3,) + w.shape, F32),
                        pltpu.VMEM((4,) + w.shape, F32),
                        pltpu.SemaphoreType.DMA((len(bounds), 3 + n_parts)), pltpu.SemaphoreType.DMA((len(bounds), 4))],
        compiler_params=_params(),
    )(*g_parts, w, m, v)


def _ada_adamw(sc_all, d_ada, w, m, v, chunks=4):
    rows, cols = w.shape
    step, sub = rows // chunks, 32
    assert rows % chunks == 0 and step % LANE == 0 and step % sub == 0

    def body(sc_ref, d_ref, w_hbm, m_hbm, v_hbm, og_hbm, od_hbm, om_hbm, ov_hbm, in_buf, out_buf, in_sems, out_sems):
        def copies_in(c):
            part = slice(c * step, (c + 1) * step)
            return [pltpu.make_async_copy(src.at[part], in_buf.at[i, part], in_sems.at[c, i])
                    for i, src in enumerate((w_hbm, m_hbm, v_hbm))]

        def copies_out(c):
            part = slice(c * step, (c + 1) * step)
            return [pltpu.make_async_copy(out_buf.at[i, part], dst.at[part], out_sems.at[c, i])
                    for i, dst in enumerate((og_hbm, od_hbm, om_hbm, ov_hbm))]

        for c in range(chunks):
            for cp in copies_in(c):
                cp.start()
        for c in range(chunks):
            sc_t = sc_ref[:, c * step:(c + 1) * step].T
            for cp in copies_in(c):
                cp.wait()
            for r0 in range(0, step, sub):
                part = slice(c * step + r0, c * step + r0 + sub)
                g = sc_t[r0:r0 + sub, 0:1] * d_ref[0:1, :]
                for b in range(1, N_DEV):
                    g = g + sc_t[r0:r0 + sub, b:b + 1] * d_ref[b:b + 1, :]
                results = _adamw(in_buf[0, part, :], g, in_buf[1, part, :], in_buf[2, part, :])
                for i, val in enumerate((g,) + results):
                    out_buf[i, part, :] = val
            for cp in copies_out(c):
                cp.start()
        for c in range(chunks):
            for cp in copies_out(c):
                cp.wait()

    in_vmem = pl.BlockSpec(memory_space=pltpu.VMEM)
    shape = jax.ShapeDtypeStruct(w.shape, F32)
    return pl.pallas_call(
        body, name="ada_adamw",
        in_specs=[in_vmem, in_vmem, ANY, ANY, ANY], out_specs=[ANY] * 4, out_shape=[shape] * 4,
        scratch_shapes=[pltpu.VMEM((3,) + w.shape, F32), pltpu.VMEM((4,) + w.shape, F32),
                        pltpu.SemaphoreType.DMA((chunks, 3)), pltpu.SemaphoreType.DMA((chunks, 4))],
        compiler_params=_params(),
    )(sc_all, d_ada, w, m, v)


F_LO, F_HI = 3 * D_ATT, 3 * D_ATT + N_HEADS


def _split_forget(a, axis):
    idx = lambda lo, hi: tuple(slice(lo, hi) if d == axis else slice(None) for d in range(a.ndim))
    pad = [(0, LANE - N_HEADS) if d == axis else (0, 0) for d in range(a.ndim)]
    return jnp.concatenate([a[idx(0, F_LO)], a[idx(F_HI, D_IN)]], axis=axis), jnp.pad(a[idx(F_LO, F_HI)], pad)


def _join_forget(main, f, axis):
    idx = lambda lo, hi: tuple(slice(lo, hi) if d == axis else slice(None) for d in range(main.ndim))
    return jnp.concatenate([main[idx(0, F_LO)], f[idx(0, N_HEADS)], main[idx(F_LO, N_MAIN)]], axis=axis)


def _adamw_small(grad_rows, row_params, whole_params, summed_params, scalar_at):
    n_row, n_whole, n_sum = len(row_params), len(whole_params), len(summed_params)
    n = n_row + n_whole + n_sum

    def body(g_ref, *refs):
        n_in = 3 * n_row + 4 * (n_whole + n_sum)
        ins, outs = list(refs[:n_in]), refs[n_in:]
        for i in range(n):
            if i < n_row:
                row, lo, hi = row_params[i][:3]
                g = g_ref[row:row + 1, lo:hi]
            elif i < n_row + n_whole:
                g = ins.pop(0)[...]
            else:
                parts = ins.pop(0)
                g = parts[0]
                for k in range(1, N_DEV):
                    g = g + parts[k]
            w, m, v = (ins.pop(0)[...] for _ in range(3))
            outs[4 * i][...] = g
            outs[4 * i + 1][...], outs[4 * i + 2][...], outs[4 * i + 3][...] = _adamw(w, g, m, v)
        row, lane = scalar_at
        outs[4 * n][...] = g_ref[row:row + 1, lane:lane + 1]

    shapes = [p[3] for p in row_params] + [p[1] for p in whole_params] + [p[1] for p in summed_params]
    operands = [a for p in row_params for a in p[3:]] + [a for p in whole_params + summed_params for a in p]
    flat = pl.pallas_call(
        body, name="adamw_small",
        out_shape=[jax.ShapeDtypeStruct(w.shape, F32) for w in shapes for _ in range(4)]
        + [jax.ShapeDtypeStruct((1, 1), F32)],
        compiler_params=_params(),
    )(grad_rows, *operands)
    return [flat[4 * i:4 * i + 4] for i in range(n)], flat[4 * n].reshape(())


def kernel(x, c, w_ada, b_ada, w_in, b_in, w_pool_mix, b_pool_mix, pool_scale, w_out, b_out, ln_g, ln_b, loss_target, m_w_ada, m_b_ada, m_w_in, m_b_in, m_w_pool_mix, m_b_pool_mix, m_pool_scale, m_w_out, m_b_out, m_ln_g, m_ln_b, v_w_ada, v_b_ada, v_w_in, v_b_in, v_w_pool_mix, v_b_pool_mix, v_pool_scale, v_w_out, v_b_out, v_ln_g, v_ln_b):
    seq = x.shape[1]
    tile = min(256, seq)
    attn_tile = min(512, max(128, seq // 4))
    me = _dev_index(*_mesh_pos())
    x2, tgt = x[0], loss_target[0]

    packed = lambda a: jnp.transpose(a.reshape(SUBLANES, LANE, -1), (2, 0, 1)).reshape(-1, LANE)
    w_main, w_f, sc_all, ada_mine = _gather_and_ada(c, packed(w_in), w_ada[0])
    ada = ada_mine.reshape(1, D_ADA) + b_ada
    shift, scale, gate = ada[:, 0:D], ada[:, D:2 * D], ada[:, 2 * D:]
    mod = jnp.concatenate([1.0 + scale, shift, jnp.zeros((6, D), F32)], axis=0)
    b_main, b_f = _split_forget(b_in, 1)

    qp, kp, vp, f, p, g_att, g_pool, u = _inproj_forward(x2, mod, w_main, w_f, b_main, b_f, tile)
    att, q2t, w_out_g = _attention_forward(qp, kp, vp, w_out[0].astype(BF16), attn_tile)

    vecs = jnp.concatenate([gate, b_out, ln_g, ln_b, jnp.zeros((4, D), F32)], axis=0)
    pool_vecs = jnp.concatenate([b_pool_mix.reshape(1, D_POOL), pool_scale, jnp.zeros((6, D_POOL), F32)], axis=0)
    dxa, do2, d_ga, d_gp, d_pooled, gw_out, dw_pool, dvec = _middle(
        x2, tgt, att, g_att, g_pool, p, vecs, pool_vecs, w_out_g.reshape(D, D), w_pool_mix[0].astype(BF16), tile)

    pool_rows = w_pool_mix.shape[1] * GROUP_DIM
    dqp, dkp, dvp, d_cum, g_out, dvec_sum, dw_pool_sum = _attention_backward(
        q2t, kp, vp, do2, gw_out.reshape(N_DEV, D // N_DEV, D), dvec, dw_pool.reshape(pool_rows, GROUP_DIM), attn_tile)
    dx, dproj, dw_f, db_main, db_f, dmod = _inproj_backward(
        dqp, dkp, dvp, d_cum, f, d_pooled, d_ga, d_gp, x2, dxa, u, mod, w_main, w_f, tile)
    d_ada = jnp.concatenate([dmod[1:2], dmod[0:1], dvec[5:6]], axis=1)
    g_in_left, g_in_right, g_b_in, d_ada_all = _weight_grads(
        dqp, dkp, dvp, dw_f, dproj, u, min(512, seq), _join_forget(db_main[0:1], db_f[0:1], 1), d_ada)

    outs_in = _adamw_packed((g_in_left, g_in_right), packed(w_in), packed(m_w_in), packed(v_w_in), "adamw_w_in")
    g_w_in, d_w_in, nm_w_in, nv_w_in = (
        jnp.transpose(a.reshape(-1, SUBLANES, LANE), (1, 2, 0)).reshape(D, -1) for a in outs_in)
    flat_pool = lambda a: a.reshape(1, D_POOL)
    pool_2d = lambda a: a.reshape(pool_rows, GROUP_DIM)
    rows, loss = _adamw_small(
        dvec_sum,
        [(0, 0, D, b_out, m_b_out, v_b_out), (1, 0, D, ln_g, m_ln_g, v_ln_g), (2, 0, D, ln_b, m_ln_b, v_ln_b),
         (3, 0, D_POOL, flat_pool(b_pool_mix), flat_pool(m_b_pool_mix), flat_pool(v_b_pool_mix)),
         (3, D_POOL, 2 * D_POOL, pool_scale, m_pool_scale, v_pool_scale)],
        [(g_out, w_out[0], m_w_out[0], v_w_out[0]),
         (dw_pool_sum, pool_2d(w_pool_mix), pool_2d(m_w_pool_mix), pool_2d(v_w_pool_mix)),
         (g_b_in, b_in, m_b_in, v_b_in)],
        [(d_ada_all, b_ada, m_b_ada, v_b_ada)],
        scalar_at=(4, 0))
    small = {"b_out": rows[0], "ln_g": rows[1], "ln_b": rows[2],
             "b_pool": [a.reshape(b_pool_mix.shape) for a in rows[3]], "pool_scale": rows[4],
             "w_pool": [a.reshape(w_pool_mix.shape) for a in rows[6]], "b_in": rows[7]}
    g_s, d_s, nm_s, nv_s = ({k: r[j] for k, r in small.items()} for j in range(4))
    g_w_out, d_w_out, nm_w_out, nv_w_out = rows[5]
    g_b_ada, d_b_ada, nm_b_ada, nv_b_ada = rows[8]

    d_ada_local = lax.dynamic_slice_in_dim(d_ada_all.reshape(N_DEV, D_ADA), me * (D_ADA // N_DEV), D_ADA // N_DEV, axis=1)
    g_w_ada, d_w_ada, nm_w_ada, nv_w_ada = _ada_adamw(sc_all, d_ada_local, w_ada[0], m_w_ada[0], v_w_ada[0])

    def ordered(w_ada_, b_ada_, w_in_, w_out_, s):
        return (w_ada_[None], b_ada_, w_in_[None], s["b_in"], s["w_pool"], s["b_pool"], s["pool_scale"],
                w_out_[None], s["b_out"], s["ln_g"], s["ln_b"])

    return (loss, dx[None],
            *ordered(g_w_ada, g_b_ada, g_w_in, g_w_out, g_s),
            *ordered(d_w_ada, d_b_ada, d_w_in, d_w_out, d_s),
            *ordered(nm_w_ada, nm_b_ada, nm_w_in, nm_w_out, nm_s),
            *ordered(nv_w_ada, nv_b_ada, nv_w_in, nv_w_out, nv_s))
```
